```python
import math
import jax, jax.numpy as jnp
from jax import lax
import numpy as np

D_MODEL = 1024
BATCH = 16
SEQ = 2048
DEPTH = 1

HEAD_DIM = 64
BLOCK = 128
A_Q_HEADS = 16
A_KV_HEADS = 2
A_GROUP = A_Q_HEADS // A_KV_HEADS
A_WINDOW = 128
B_PATTERNS = ((128, 1), (512, 4), (2048, 16))
B_HEADS_PER_GROUP = 8
B_HEADS = B_HEADS_PER_GROUP * len(B_PATTERNS)
A_Q_W = A_Q_HEADS * HEAD_DIM
A_KV_W = A_KV_HEADS * HEAD_DIM
B_W = B_HEADS * HEAD_DIM
B_OUT_W = B_HEADS_PER_GROUP * HEAD_DIM
IN_W = A_Q_W + 2 * A_KV_W + 3 * B_W + 2 * D_MODEL
D_FF = -(-8 * D_MODEL // (3 * 256)) * 256
ROPE_THETA = 10000.0
LN_EPS = 1e-5
DEEPNORM_ALPHA = (2 * DEPTH) ** 0.25
DEEPNORM_BETA = (8 * DEPTH) ** -0.25
NEG_INF = -1e30

kernel_name = "hybrid_swa_sink_dilated_gated_deepnorm_adaln"


def layer_norm(x, g, b):
    xf = x.astype(jnp.float32)
    mu = jnp.mean(xf, axis=-1, keepdims=True)
    xc = xf - mu
    var = jnp.mean(xc * xc, axis=-1, keepdims=True)
    y = xc * lax.rsqrt(var + LN_EPS)
    return (y * g.astype(jnp.float32) + b.astype(jnp.float32)).astype(x.dtype)


def rope(x, positions):
    half = HEAD_DIM // 2
    inv = ROPE_THETA ** (-jnp.arange(half, dtype=jnp.float32) / half)
    ang = positions.astype(jnp.float32)[..., None] * inv
    cos = jnp.cos(ang)[:, :, None, :]
    sin = jnp.sin(ang)[:, :, None, :]
    xf = x.astype(jnp.float32)
    x1, x2 = xf[..., :half], xf[..., half:]
    out = jnp.concatenate([x1 * cos - x2 * sin, x2 * cos + x1 * sin], axis=-1)
    return out.astype(x.dtype)


def banded_window_attention(q, k, v, n_back, sink=None):
    N, T, Hkv, G, dh = q.shape
    nblk = -(-T // BLOCK)
    pad = nblk * BLOCK - T
    qp = jnp.pad(q, ((0, 0), (0, pad), (0, 0), (0, 0), (0, 0)))
    kp = jnp.pad(k, ((0, 0), (BLOCK, pad), (0, 0), (0, 0)))
    vp = jnp.pad(v, ((0, 0), (BLOCK, pad), (0, 0), (0, 0)))
    qb = qp.reshape(N, nblk, BLOCK, Hkv, G, dh)
    kb = kp.reshape(N, nblk + 1, BLOCK, Hkv, dh)
    vb = vp.reshape(N, nblk + 1, BLOCK, Hkv, dh)
    kw = jnp.concatenate([kb[:, :-1], kb[:, 1:]], axis=2)
    vw = jnp.concatenate([vb[:, :-1], vb[:, 1:]], axis=2)
    s = jnp.einsum('nbqhgd,nbkhd->nbhgqk', qb, kw,
                   preferred_element_type=jnp.float32) * (dh ** -0.5)
    qi = jnp.arange(BLOCK)[:, None]
    ki = jnp.arange(2 * BLOCK)[None, :]
    dist = qi + BLOCK - ki
    kpos = jnp.arange(nblk)[:, None, None] * BLOCK + ki[None] - BLOCK
    mask = (dist >= 0) & (dist <= n_back) & (kpos >= 0)
    s = jnp.where(mask[None, :, None, None], s, NEG_INF)
    m = jnp.max(s, axis=-1)
    if sink is not None:
        sk = sink.astype(jnp.float32).reshape(Hkv, G)[None, None, :, :, None]
        m = jnp.maximum(m, sk)
    p = jnp.exp(s - m[..., None])
    denom = jnp.sum(p, axis=-1)
    if sink is not None:
        denom = denom + jnp.exp(sk - m)
    o = jnp.einsum('nbhgqk,nbkhd->nbqhgd', p, vw.astype(jnp.float32))
    denom_t = jnp.moveaxis(denom, -1, 2)
    o = o / denom_t[..., None]
    lse = jnp.moveaxis(m, -1, 2) + jnp.log(denom_t)
    o = o.reshape(N, nblk * BLOCK, Hkv, G, dh)[:, :T]
    lse = lse.reshape(N, nblk * BLOCK, Hkv, G)[:, :T]
    return o.astype(q.dtype), lse


def dilated_attention(q, k, v, window, dilation):
    Bn, T, H, dh = q.shape
    r = dilation
    tsub = -(-T // r)
    pad = tsub * r - T

    def to_strided(t):
        t = jnp.pad(t, ((0, 0), (0, pad), (0, 0), (0, 0)))
        return t.reshape(Bn, tsub, r, H, dh).transpose(0, 2, 1, 3, 4).reshape(Bn * r, tsub, H, dh)

    o, lse = banded_window_attention(to_strided(q)[:, :, :, None], to_strided(k), to_strided(v),
                                     window // r)
    o = o[:, :, :, 0].reshape(Bn, r, tsub, H, dh).transpose(0, 2, 1, 3, 4).reshape(Bn, tsub * r, H, dh)[:, :T]
    lse = lse[:, :, :, 0].reshape(Bn, r, tsub, H).transpose(0, 2, 1, 3).reshape(Bn, tsub * r, H)[:, :T]
    return o, lse


def mixer(u, positions, w_in, sinks, w_branch_a, w_branch_b, w_o):
    Bn, T, _ = u.shape
    proj = jnp.einsum('btd,de->bte', u, w_in)
    sizes = [A_Q_W, A_KV_W, A_KV_W, B_W, B_W, B_W, D_MODEL]
    offs, acc = [], 0
    for s_ in sizes:
        acc += s_
        offs.append(acc)
    qa, ka, va, qb, kb, vb, ga, gb = jnp.split(proj, offs, axis=-1)

    qa = rope(qa.reshape(Bn, T, A_Q_HEADS, HEAD_DIM), positions).reshape(Bn, T, A_KV_HEADS, A_GROUP, HEAD_DIM)
    ka = rope(ka.reshape(Bn, T, A_KV_HEADS, HEAD_DIM), positions)
    va = va.reshape(Bn, T, A_KV_HEADS, HEAD_DIM)
    oa, _ = banded_window_attention(qa, ka, va, A_WINDOW - 1, sinks)
    ya = jnp.einsum('bte,ed->btd', oa.reshape(Bn, T, A_Q_W), w_branch_a)

    qb = rope(qb.reshape(Bn, T, B_HEADS, HEAD_DIM), positions)
    kb = rope(kb.reshape(Bn, T, B_HEADS, HEAD_DIM), positions)
    vb = vb.reshape(Bn, T, B_HEADS, HEAD_DIM)
    outs, lses = [], []
    for g, (window, dil) in enumerate(B_PATTERNS):
        sl = slice(g * B_HEADS_PER_GROUP, (g + 1) * B_HEADS_PER_GROUP)
        o_g, l_g = dilated_attention(qb[:, :, sl], kb[:, :, sl], vb[:, :, sl], window, dil)
        outs.append(o_g)
        lses.append(l_g)
    o_all = jnp.stack(outs).astype(jnp.float32)
    wts = jax.nn.softmax(jnp.stack(lses), axis=0)
    ob = jnp.sum(wts[..., None] * o_all, axis=0).astype(u.dtype)
    yb = jnp.einsum('bte,ed->btd', ob.reshape(Bn, T, B_OUT_W), w_branch_b)

    merged = jax.nn.sigmoid(ga) * ya + jax.nn.sigmoid(gb) * yb
    return jnp.einsum('btd,de->bte', merged, w_o)


def swiglu(u, w_gate_up, w_down):
    h = jnp.einsum('btd,df->btf', u, w_gate_up)
    hg, hu = jnp.split(h, 2, axis=-1)
    return jnp.einsum('btf,fd->btd', jax.nn.silu(hg) * hu, w_down)


def _fwd_setup_inputs(seed: int = 0) -> dict:
    key = jax.random.key(seed)
    ks = jax.random.split(key, 20)
    f32 = jnp.float32
    nrm = lambda k, shape, scale: jax.random.normal(k, shape, f32) * scale
    x = jax.random.normal(ks[0], (BATCH, SEQ, D_MODEL), f32)
    c = jax.random.normal(ks[1], (BATCH, D_MODEL), f32)
    offset = jax.random.randint(ks[2], (BATCH, 1), 0, 1024, dtype=jnp.int32)
    positions = offset + jnp.arange(SEQ, dtype=jnp.int32)[None, :]
    return {
        "x": x,
        "c": c,
        "positions": positions,
        "w_ada": nrm(ks[3], (DEPTH, D_MODEL, 6 * D_MODEL), 0.1 * D_MODEL ** -0.5),
        "b_ada": nrm(ks[4], (DEPTH, 6 * D_MODEL), 0.01),
        "w_in": nrm(ks[5], (DEPTH, D_MODEL, IN_W), D_MODEL ** -0.5),
        "sinks": nrm(ks[6], (DEPTH, A_Q_HEADS), 1.0),
        "w_branch_a": nrm(ks[7], (DEPTH, A_Q_W, D_MODEL), A_Q_W ** -0.5),
        "w_branch_b": nrm(ks[8], (DEPTH, B_OUT_W, D_MODEL), B_OUT_W ** -0.5),
        "w_o": nrm(ks[9], (DEPTH, D_MODEL, D_MODEL), DEEPNORM_BETA * D_MODEL ** -0.5),
        "ln1_g": 1.0 + nrm(ks[10], (DEPTH, D_MODEL), 0.02),
        "ln1_b": nrm(ks[11], (DEPTH, D_MODEL), 0.02),
        "w_gate_up": nrm(ks[12], (DEPTH, D_MODEL, 2 * D_FF), D_MODEL ** -0.5),
        "w_down": nrm(ks[13], (DEPTH, D_FF, D_MODEL), DEEPNORM_BETA * D_FF ** -0.5),
        "ln2_g": 1.0 + nrm(ks[14], (DEPTH, D_MODEL), 0.02),
        "ln2_b": nrm(ks[15], (DEPTH, D_MODEL), 0.02),
    }


def _fwd_reference(x, c, positions, w_ada, b_ada, w_in, sinks, w_branch_a, w_branch_b, w_o,
              ln1_g, ln1_b, w_gate_up, w_down, ln2_g, ln2_b):
    c_act = jax.nn.silu(c)
    for l in range(DEPTH):
        mod = (jnp.einsum('bd,de->be', c_act, w_ada[l]) + b_ada[l])[:, None, :]
        shift_m, scale_m, gate_m, shift_f, scale_f, gate_f = jnp.split(mod, 6, axis=-1)
        u = x * (1.0 + scale_m) + shift_m
        y = mixer(u, positions, w_in[l], sinks[l], w_branch_a[l], w_branch_b[l], w_o[l])
        x = layer_norm(DEEPNORM_ALPHA * x + (1.0 + gate_m) * y, ln1_g[l], ln1_b[l])
        u = x * (1.0 + scale_f) + shift_f
        y = swiglu(u, w_gate_up[l], w_down[l])
        x = layer_norm(DEEPNORM_ALPHA * x + (1.0 + gate_f) * y, ln2_g[l], ln2_b[l])
    return x


import jax as _jax
import jax.numpy as _jnp

TWIN_FORMAT = 'train_step'
FWD_PARAMS = ['x', 'c', 'positions', 'w_ada', 'b_ada', 'w_in', 'sinks', 'w_branch_a', 'w_branch_b', 'w_o', 'ln1_g', 'ln1_b', 'w_gate_up', 'w_down', 'ln2_g', 'ln2_b']
TWIN_WEIGHTS = ['w_ada', 'b_ada', 'w_in', 'sinks', 'w_branch_a', 'w_branch_b', 'w_o', 'ln1_g', 'ln1_b', 'w_gate_up', 'w_down', 'ln2_g', 'ln2_b']
TWIN_DIFF_INPUT = 'x'
TWIN_INPUTS = ['x', 'c', 'positions', 'w_ada', 'b_ada', 'w_in', 'sinks', 'w_branch_a', 'w_branch_b', 'w_o', 'ln1_g', 'ln1_b', 'w_gate_up', 'w_down', 'ln2_g', 'ln2_b', 'loss_target', 'm_w_ada', 'm_b_ada', 'm_w_in', 'm_sinks', 'm_w_branch_a', 'm_w_branch_b', 'm_w_o', 'm_ln1_g', 'm_ln1_b', 'm_w_gate_up', 'm_w_down', 'm_ln2_g', 'm_ln2_b', 'v_w_ada', 'v_b_ada', 'v_w_in', 'v_sinks', 'v_w_branch_a', 'v_w_branch_b', 'v_w_o', 'v_ln1_g', 'v_ln1_b', 'v_w_gate_up', 'v_w_down', 'v_ln2_g', 'v_ln2_b']
TWIN_OUTPUTS = ['loss', 'grad_x', 'grad_w_ada', 'grad_b_ada', 'grad_w_in', 'grad_sinks', 'grad_w_branch_a', 'grad_w_branch_b', 'grad_w_o', 'grad_ln1_g', 'grad_ln1_b', 'grad_w_gate_up', 'grad_w_down', 'grad_ln2_g', 'grad_ln2_b', 'delta_w_ada', 'delta_b_ada', 'delta_w_in', 'delta_sinks', 'delta_w_branch_a', 'delta_w_branch_b', 'delta_w_o', 'delta_ln1_g', 'delta_ln1_b', 'delta_w_gate_up', 'delta_w_down', 'delta_ln2_g', 'delta_ln2_b', 'new_m_w_ada', 'new_m_b_ada', 'new_m_w_in', 'new_m_sinks', 'new_m_w_branch_a', 'new_m_w_branch_b', 'new_m_w_o', 'new_m_ln1_g', 'new_m_ln1_b', 'new_m_w_gate_up', 'new_m_w_down', 'new_m_ln2_g', 'new_m_ln2_b', 'new_v_w_ada', 'new_v_b_ada', 'new_v_w_in', 'new_v_sinks', 'new_v_w_branch_a', 'new_v_w_branch_b', 'new_v_w_o', 'new_v_ln1_g', 'new_v_ln1_b', 'new_v_w_gate_up', 'new_v_w_down', 'new_v_ln2_g', 'new_v_ln2_b']
TWIN_LEAF_KINDS = {'loss': 'loss', 'grad_x': 'grad_x', 'grad_w_ada': 'grad_w', 'grad_b_ada': 'grad_w', 'grad_w_in': 'grad_w', 'grad_sinks': 'grad_w', 'grad_w_branch_a': 'grad_w', 'grad_w_branch_b': 'grad_w', 'grad_w_o': 'grad_w', 'grad_ln1_g': 'grad_w', 'grad_ln1_b': 'grad_w', 'grad_w_gate_up': 'grad_w', 'grad_w_down': 'grad_w', 'grad_ln2_g': 'grad_w', 'grad_ln2_b': 'grad_w', 'delta_w_ada': 'delta_w', 'delta_b_ada': 'delta_w', 'delta_w_in': 'delta_w', 'delta_sinks': 'delta_w', 'delta_w_branch_a': 'delta_w', 'delta_w_branch_b': 'delta_w', 'delta_w_o': 'delta_w', 'delta_ln1_g': 'delta_w', 'delta_ln1_b': 'delta_w', 'delta_w_gate_up': 'delta_w', 'delta_w_down': 'delta_w', 'delta_ln2_g': 'delta_w', 'delta_ln2_b': 'delta_w', 'new_m_w_ada': 'new_m', 'new_m_b_ada': 'new_m', 'new_m_w_in': 'new_m', 'new_m_sinks': 'new_m', 'new_m_w_branch_a': 'new_m', 'new_m_w_branch_b': 'new_m', 'new_m_w_o': 'new_m', 'new_m_ln1_g': 'new_m', 'new_m_ln1_b': 'new_m', 'new_m_w_gate_up': 'new_m', 'new_m_w_down': 'new_m', 'new_m_ln2_g': 'new_m', 'new_m_ln2_b': 'new_m', 'new_v_w_ada': 'new_v', 'new_v_b_ada': 'new_v', 'new_v_w_in': 'new_v', 'new_v_sinks': 'new_v', 'new_v_w_branch_a': 'new_v', 'new_v_w_branch_b': 'new_v', 'new_v_w_o': 'new_v', 'new_v_ln1_g': 'new_v', 'new_v_ln1_b': 'new_v', 'new_v_w_gate_up': 'new_v', 'new_v_w_down': 'new_v', 'new_v_ln2_g': 'new_v', 'new_v_ln2_b': 'new_v'}


def _forward(args):
    return _fwd_reference(*[args[k] for k in FWD_PARAMS])


def _output_shape():
    out = _jax.eval_shape(lambda: _forward(_fwd_setup_inputs(0)))
    return out.shape, out.dtype

N_MICROBATCH = 1
ADAM_LR = 0.001
ADAM_B1 = 0.9
ADAM_B2 = 0.999
ADAM_EPS = 1e-08
ADAM_WD = 0.01
ADAM_STEP = 10
PER_EXAMPLE_BATCH_AXIS = {'x': 0, 'c': 0, 'positions': 0, 'loss_target': 0}
SHARED_INPUTS = []
_WEIGHT_DTYPES = {'w_ada': _jnp.float32, 'b_ada': _jnp.float32, 'w_in': _jnp.float32, 'sinks': _jnp.float32, 'w_branch_a': _jnp.float32, 'w_branch_b': _jnp.float32, 'w_o': _jnp.float32, 'ln1_g': _jnp.float32, 'ln1_b': _jnp.float32, 'w_gate_up': _jnp.float32, 'w_down': _jnp.float32, 'ln2_g': _jnp.float32, 'ln2_b': _jnp.float32}
MOMENT_SCALE = {'w_ada': 3.637290e-02, 'b_ada': 6.751681e-02, 'w_in': 8.395710e-03, 'sinks': 8.696896e-03, 'w_branch_a': 1.055092e-02, 'w_branch_b': 9.882756e-03, 'w_o': 2.431025e-02, 'ln1_g': 9.336940e-01, 'ln1_b': 4.405296e-01, 'w_gate_up': 3.148123e-02, 'w_down': 8.626503e-02, 'ln2_g': 3.199661e+01, 'ln2_b': 6.962133e-01}


def _to_microbatches(a, axis):
    t = _jnp.moveaxis(a, axis, 0)
    t = t.reshape((N_MICROBATCH, t.shape[0] // N_MICROBATCH) + t.shape[1:])
    return _jnp.moveaxis(t, 1, axis + 1)


def setup_inputs(seed: int = 0) -> dict:
    inp = _fwd_setup_inputs(seed)
    key = _jax.random.fold_in(_jax.random.key(seed), 7919)
    shape, _ = _output_shape()
    out = dict(inp)
    out["loss_target"] = _jax.random.normal(_jax.random.fold_in(key, 0), shape, _jnp.float32)
    for i, name in enumerate(TWIN_WEIGHTS):
        w = inp[name].astype(_jnp.float32)
        if MOMENT_SCALE is None:
            s = _jnp.sqrt(_jnp.mean(_jnp.square(w)) + 1e-30)
        else:
            s = MOMENT_SCALE[name]
        km, kv = _jax.random.split(_jax.random.fold_in(key, i + 1))
        out[name] = w
        out["m_" + name] = s * _jax.random.normal(km, w.shape, _jnp.float32)
        out["v_" + name] = (s * s) * _jax.random.uniform(kv, w.shape, _jnp.float32, 0.5, 1.5)
    if N_MICROBATCH > 1:
        for name, axis in PER_EXAMPLE_BATCH_AXIS.items():
            out[name] = _to_microbatches(out[name], axis)
    return {'x': out['x'], 'c': out['c'], 'positions': out['positions'], 'w_ada': out['w_ada'], 'b_ada': out['b_ada'], 'w_in': out['w_in'], 'sinks': out['sinks'], 'w_branch_a': out['w_branch_a'], 'w_branch_b': out['w_branch_b'], 'w_o': out['w_o'], 'ln1_g': out['ln1_g'], 'ln1_b': out['ln1_b'], 'w_gate_up': out['w_gate_up'], 'w_down': out['w_down'], 'ln2_g': out['ln2_g'], 'ln2_b': out['ln2_b'], 'loss_target': out['loss_target'], 'm_w_ada': out['m_w_ada'], 'm_b_ada': out['m_b_ada'], 'm_w_in': out['m_w_in'], 'm_sinks': out['m_sinks'], 'm_w_branch_a': out['m_w_branch_a'], 'm_w_branch_b': out['m_w_branch_b'], 'm_w_o': out['m_w_o'], 'm_ln1_g': out['m_ln1_g'], 'm_ln1_b': out['m_ln1_b'], 'm_w_gate_up': out['m_w_gate_up'], 'm_w_down': out['m_w_down'], 'm_ln2_g': out['m_ln2_g'], 'm_ln2_b': out['m_ln2_b'], 'v_w_ada': out['v_w_ada'], 'v_b_ada': out['v_b_ada'], 'v_w_in': out['v_w_in'], 'v_sinks': out['v_sinks'], 'v_w_branch_a': out['v_w_branch_a'], 'v_w_branch_b': out['v_w_branch_b'], 'v_w_o': out['v_w_o'], 'v_ln1_g': out['v_ln1_g'], 'v_ln1_b': out['v_ln1_b'], 'v_w_gate_up': out['v_w_gate_up'], 'v_w_down': out['v_w_down'], 'v_ln2_g': out['v_ln2_g'], 'v_ln2_b': out['v_ln2_b']}


def _loss(weights, diff, rest, loss_target):
    with _jax.named_scope("forward"):
        args = {**rest, TWIN_DIFF_INPUT: diff, **{k: w.astype(_WEIGHT_DTYPES[k]) for k, w in weights.items()}}
        y = _forward(args)
    with _jax.named_scope("loss_head"):
        err = _jnp.square(y.astype(_jnp.float32) - loss_target)
        return 0.5 * _jnp.sum(_jnp.mean(err, axis=-1)) if err.ndim else 0.5 * err


def _adamw(w, g, m, v):
    m = ADAM_B1 * m + (1.0 - ADAM_B1) * g
    v = ADAM_B2 * v + (1.0 - ADAM_B2) * _jnp.square(g)
    m_hat = m / (1.0 - ADAM_B1 ** ADAM_STEP)
    v_hat = v / (1.0 - ADAM_B2 ** ADAM_STEP)
    delta = -ADAM_LR * (m_hat / (_jnp.sqrt(v_hat) + ADAM_EPS) + ADAM_WD * w)
    return delta, m, v


def reference(x, c, positions, w_ada, b_ada, w_in, sinks, w_branch_a, w_branch_b, w_o, ln1_g, ln1_b, w_gate_up, w_down, ln2_g, ln2_b, loss_target, m_w_ada, m_b_ada, m_w_in, m_sinks, m_w_branch_a, m_w_branch_b, m_w_o, m_ln1_g, m_ln1_b, m_w_gate_up, m_w_down, m_ln2_g, m_ln2_b, v_w_ada, v_b_ada, v_w_in, v_sinks, v_w_branch_a, v_w_branch_b, v_w_o, v_ln1_g, v_ln1_b, v_w_gate_up, v_w_down, v_ln2_g, v_ln2_b):
    given = dict(x=x, c=c, positions=positions, w_ada=w_ada, b_ada=b_ada, w_in=w_in, sinks=sinks, w_branch_a=w_branch_a, w_branch_b=w_branch_b, w_o=w_o, ln1_g=ln1_g, ln1_b=ln1_b, w_gate_up=w_gate_up, w_down=w_down, ln2_g=ln2_g, ln2_b=ln2_b, loss_target=loss_target, m_w_ada=m_w_ada, m_b_ada=m_b_ada, m_w_in=m_w_in, m_sinks=m_sinks, m_w_branch_a=m_w_branch_a, m_w_branch_b=m_w_branch_b, m_w_o=m_w_o, m_ln1_g=m_ln1_g, m_ln1_b=m_ln1_b, m_w_gate_up=m_w_gate_up, m_w_down=m_w_down, m_ln2_g=m_ln2_g, m_ln2_b=m_ln2_b, v_w_ada=v_w_ada, v_b_ada=v_b_ada, v_w_in=v_w_in, v_sinks=v_sinks, v_w_branch_a=v_w_branch_a, v_w_branch_b=v_w_branch_b, v_w_o=v_w_o, v_ln1_g=v_ln1_g, v_ln1_b=v_ln1_b, v_w_gate_up=v_w_gate_up, v_w_down=v_w_down, v_ln2_g=v_ln2_g, v_ln2_b=v_ln2_b)
    weights = {n: given[n] for n in TWIN_WEIGHTS}
    shared = {n: given[n] for n in SHARED_INPUTS}
    per_example = {n: given[n] for n in ['x', 'c', 'positions']}
    grad_fn = _jax.value_and_grad(_loss, argnums=(0, 1))

    def one_microbatch(ex, loss_target):
        ex = dict(ex)
        diff = ex.pop(TWIN_DIFF_INPUT)
        return grad_fn(weights, diff, {**shared, **ex}, loss_target)

    if N_MICROBATCH == 1:
        loss, (grad_w, grad_x) = one_microbatch(per_example, given["loss_target"])
    else:
        def body(carry, xs):
            loss_sum, grad_sum = carry
            l_k, (gw_k, gx_k) = one_microbatch(xs[0], xs[1])
            with _jax.named_scope("update"):
                return (loss_sum + l_k, _jax.tree.map(_jnp.add, grad_sum, gw_k)), gx_k

        init = (_jnp.zeros((), _jnp.float32), _jax.tree.map(_jnp.zeros_like, weights))
        (loss, grad_w), grad_x = _jax.lax.scan(body, init, (per_example, given["loss_target"]))
    with _jax.named_scope("update"):
        delta_w, new_m, new_v = {}, {}, {}
        for n in TWIN_WEIGHTS:
            delta_w[n], new_m[n], new_v[n] = _adamw(weights[n], grad_w[n], given["m_" + n], given["v_" + n])
    return (loss, grad_x, *[grad_w[n] for n in TWIN_WEIGHTS], *[delta_w[n] for n in TWIN_WEIGHTS],
            *[new_m[n] for n in TWIN_WEIGHTS], *[new_v[n] for n in TWIN_WEIGHTS])
```

```python
import functools
import math

import jax
import jax.numpy as jnp
from jax import lax
from jax.experimental import pallas as pl
from jax.experimental.pallas import tpu as pltpu

F32 = jnp.float32
BF16 = jnp.bfloat16
MESH = pl.DeviceIdType.MESH

HEAD_DIM = 64
BLOCK = 128
A_Q_HEADS = 16
A_KV_HEADS = 2
A_WINDOW = 128
B_PATTERNS = ((128, 1), (512, 4), (2048, 16))
B_GROUP_HEADS = 8
QA_W = A_Q_HEADS * HEAD_DIM
KA_W = A_KV_HEADS * HEAD_DIM
GB_W = B_GROUP_HEADS * HEAD_DIM
QB_W = GB_W * len(B_PATTERNS)
QKV_W = QA_W + 2 * KA_W + 3 * QB_W
QKV_STRIDE = 6144
OFF_QA, OFF_QB, OFF_KB, OFF_VB = 0, QA_W, QA_W + QB_W, QA_W + 2 * QB_W
OFF_KA = QA_W + 3 * QB_W
OFF_VA = OFF_KA + KA_W
ROPE_THETA = 10000.0
LN_EPS = 1e-5
NEG_INF = -1e30
DEPTH = 1
ALPHA = (2 * DEPTH) ** 0.25
SCALE = HEAD_DIM ** -0.5

ADAM_LR, ADAM_B1, ADAM_B2, ADAM_EPS, ADAM_WD, ADAM_STEP = 0.001, 0.9, 0.999, 1e-08, 0.01, 10

VMEM_LIMIT_BYTES = 48 * 1024 * 1024


def _pcall(body, **kw):
    return pl.pallas_call(body, **kw)


def _params(sem=None):
    return pltpu.CompilerParams(dimension_semantics=sem, vmem_limit_bytes=VMEM_LIMIT_BYTES)


def _pick(n, target, quantum=128):
    t = (min(target, n) // quantum) * quantum
    while t >= quantum:
        if n % t == 0:
            return t
        t -= quantum
    return n


def _mm(a, b, *, name, ta=False, tb=False, out_dtype=F32, add=None, tm=512, tn=512, tk=1024):
    if ta:
        K, M = a.shape
    else:
        M, K = a.shape
    if tb:
        Nn, K2 = b.shape
    else:
        K2, Nn = b.shape
    assert K == K2, (a.shape, b.shape)
    tm, tn, tk = _pick(M, tm), _pick(Nn, tn), _pick(K, tk)
    nk = K // tk
    dn = (((0 if ta else 1,), (1 if tb else 0,)), ((), ()))

    def body(*refs):
        if add is None:
            a_ref, b_ref, o_ref, acc = refs
        else:
            a_ref, b_ref, add_ref, o_ref, acc = refs
        k = pl.program_id(2)

        @pl.when(k == 0)
        def _():
            acc[...] = jnp.zeros_like(acc)

        acc[...] += lax.dot_general(a_ref[...].astype(BF16), b_ref[...].astype(BF16), dn,
                                    preferred_element_type=F32)

        @pl.when(k == nk - 1)
        def _():
            r = acc[...]
            if add is not None:
                r = r + add_ref[...]
            o_ref[...] = r.astype(out_dtype)

    a_spec = pl.BlockSpec((tk, tm), lambda i, j, k: (k, i)) if ta else pl.BlockSpec((tm, tk), lambda i, j, k: (i, k))
    b_spec = pl.BlockSpec((tn, tk), lambda i, j, k: (j, k)) if tb else pl.BlockSpec((tk, tn), lambda i, j, k: (k, j))
    o_spec = pl.BlockSpec((tm, tn), lambda i, j, k: (i, j))
    ins, specs = [a, b], [a_spec, b_spec]
    if add is not None:
        ins.append(add)
        specs.append(o_spec)
    return _pcall(
        body, name=name, grid=(M // tm, Nn // tn, nk), in_specs=specs, out_specs=o_spec,
        out_shape=jax.ShapeDtypeStruct((M, Nn), out_dtype),
        scratch_shapes=[pltpu.VMEM((tm, tn), F32)],
        compiler_params=_params(("parallel", "parallel", "arbitrary")),
    )(*ins)


def _rot_half(v):
    w = v.shape[-1]
    lane = lax.broadcasted_iota(jnp.int32, v.shape, v.ndim - 1)
    first = (lane % HEAD_DIM) < (HEAD_DIM // 2)
    return jnp.where(first, pltpu.roll(v, w - HEAD_DIM // 2, v.ndim - 1), pltpu.roll(v, HEAD_DIM // 2, v.ndim - 1))


def _widen(t, w):
    return t if w == t.shape[-1] else jnp.concatenate([t] * (w // t.shape[-1]), axis=-1)


def _rope_tables(positions):
    half = HEAD_DIM // 2
    inv = ROPE_THETA ** (-jnp.arange(half, dtype=F32) / half)
    ang = positions.astype(F32)[..., None] * inv
    cos, sin = jnp.cos(ang), jnp.sin(ang)
    cosf = jnp.concatenate([cos, cos, cos, cos], axis=-1)
    sins = jnp.concatenate([-sin, sin, -sin, sin], axis=-1)
    n = positions.shape[0] * positions.shape[1]
    return cosf.reshape(n, 2 * HEAD_DIM), sins.reshape(n, 2 * HEAD_DIM)


def _inproj(x2, scale, shift, w, cosf, sins, flags, *, T, name):
    N, D = x2.shape
    tm, tn = _pick(T, 512), 256
    tpe = T // tm

    def body(x_ref, sc_ref, sh_ref, w_ref, c_ref, s_ref, f_ref, o_ref, u_ref):
        @pl.when(pl.program_id(1) == 0)
        def _():
            u_ref[...] = (x_ref[...] * (1.0 + sc_ref[0]) + sh_ref[0]).astype(BF16)

        acc = jnp.dot(u_ref[...], w_ref[...], preferred_element_type=F32)
        fl = f_ref[...]
        ce = 1.0 + (_widen(c_ref[...], tn) - 1.0) * fl
        se = _widen(s_ref[...], tn) * fl
        o_ref[...] = (acc * ce + _rot_half(acc) * se).astype(BF16)

    ex = pl.BlockSpec((1, 1, D), lambda i, j: (i // tpe, 0, 0))
    tab = pl.BlockSpec((tm, 2 * HEAD_DIM), lambda i, j: (i, 0))
    return _pcall(
        body, name=name, grid=(N // tm, QKV_W // tn),
        in_specs=[pl.BlockSpec((tm, D), lambda i, j: (i, 0)), ex, ex,
                  pl.BlockSpec((D, tn), lambda i, j: (0, j)), tab, tab,
                  pl.BlockSpec((1, tn), lambda i, j: (0, j))],
        out_specs=[pl.BlockSpec((tm, tn), lambda i, j: (i, j)), pl.BlockSpec((tm, D), lambda i, j: (i, 0))],
        out_shape=[jax.ShapeDtypeStruct((N, QKV_STRIDE), BF16), jax.ShapeDtypeStruct((N, D), BF16)],
        compiler_params=_params(("parallel", "arbitrary")),
    )(x2, scale, shift, w, cosf, sins, flags)


class _Geom:
    def __init__(self, g):
        if g is None:
            self.r, self.nq, self.grp, self.n_back, self.sink = 1, A_Q_HEADS, A_Q_HEADS // A_KV_HEADS, A_WINDOW - 1, True
            self.qidx = lambda j: OFF_QA // QA_W
            self.kidx = lambda j: OFF_KA // KA_W
            self.vidx = lambda j: OFF_VA // KA_W
        else:
            window, r = B_PATTERNS[g]
            self.r, self.nq, self.grp, self.n_back, self.sink = r, B_GROUP_HEADS, 1, window // r, False
            per = QKV_STRIDE // GB_W
            self.qidx = lambda j: j * per + OFF_QB // GB_W + g
            self.kidx = lambda j: j * per + OFF_KB // GB_W + g
            self.vidx = lambda j: j * per + OFF_VB // GB_W + g
        self.qw = self.nq * HEAD_DIM
        self.kw = (self.nq // self.grp) * HEAD_DIM


def _scores(q, kp, kc, mask_p, mask_c, sink):
    dn = (((1,), (1,)), ((), ()))
    sp = lax.dot_general(q, kp, dn, preferred_element_type=F32) * SCALE
    sc = lax.dot_general(q, kc, dn, preferred_element_type=F32) * SCALE
    sp = jnp.where(mask_p, sp, NEG_INF)
    sc = jnp.where(mask_c, sc, NEG_INF)
    m = jnp.maximum(jnp.max(sp, axis=1, keepdims=True), jnp.max(sc, axis=1, keepdims=True))
    if sink is not None:
        m = jnp.maximum(m, sink)
    pp = jnp.exp(sp - m)
    pc = jnp.exp(sc - m)
    den = jnp.sum(pp, axis=1, keepdims=True) + jnp.sum(pc, axis=1, keepdims=True)
    es = None
    if sink is not None:
        es = jnp.exp(sink - m)
        den = den + es
    return pp, pc, m, den, es


def _masks(i, n_back):
    qi = lax.broadcasted_iota(jnp.int32, (BLOCK, BLOCK), 0)
    ki = lax.broadcasted_iota(jnp.int32, (BLOCK, BLOCK), 1)
    mask_c = qi >= ki
    mask_p = jnp.logical_and(qi + BLOCK - ki <= n_back, i > 0)
    return mask_p, mask_c


def _attn_fwd(qkv, sinks, g, *, NB, T, name):
    geo = _Geom(g)
    r, nq, qw, kw = geo.r, geo.nq, geo.qw, geo.kw
    tsub = T // r
    nblk = tsub // BLOCK
    qkv3 = qkv.reshape(NB, tsub, r * QKV_STRIDE)
    with_lse = g is not None
    out_dtype = F32 if with_lse else BF16

    def body(q_ref, kp_ref, kc_ref, vp_ref, vc_ref, sink_ref, *rest):
        if with_lse:
            o_ref, l_ref, o_s, l_s = rest
        else:
            o_ref, o_s = rest
        mask_p, mask_c = _masks(pl.program_id(2), geo.n_back)
        for h in range(nq):
            kh = h // geo.grp
            qs, ks = slice(h * HEAD_DIM, (h + 1) * HEAD_DIM), slice(kh * HEAD_DIM, (kh + 1) * HEAD_DIM)
            sink = sink_ref[h] if geo.sink else None
            pp, pc, m, den, _ = _scores(q_ref[0, :, qs], kp_ref[0, :, ks], kc_ref[0, :, ks], mask_p, mask_c, sink)
            o = jnp.dot(pp.astype(BF16), vp_ref[0, :, ks], preferred_element_type=F32)
            o = o + jnp.dot(pc.astype(BF16), vc_ref[0, :, ks], preferred_element_type=F32)
            o_s[:, qs] = o / den
            if with_lse:
                l_s[:, qs] = jnp.broadcast_to(m + jnp.log(den), (BLOCK, HEAD_DIM))
        o_ref[0] = o_s[...].astype(out_dtype)
        if with_lse:
            l_ref[0] = l_s[...]

    prev = lambda i: jnp.maximum(i - 1, 0)
    in_specs = [
        pl.BlockSpec((1, BLOCK, qw), lambda b, j, i: (b, i, geo.qidx(j))),
        pl.BlockSpec((1, BLOCK, kw), lambda b, j, i: (b, prev(i), geo.kidx(j))),
        pl.BlockSpec((1, BLOCK, kw), lambda b, j, i: (b, i, geo.kidx(j))),
        pl.BlockSpec((1, BLOCK, kw), lambda b, j, i: (b, prev(i), geo.vidx(j))),
        pl.BlockSpec((1, BLOCK, kw), lambda b, j, i: (b, i, geo.vidx(j))),
        pl.BlockSpec(memory_space=pltpu.SMEM),
    ]
    o_spec = pl.BlockSpec((1, BLOCK, qw), lambda b, j, i: (b, i, j))
    o_shape = jax.ShapeDtypeStruct((NB, tsub, r * qw), out_dtype)
    scratch = [pltpu.VMEM((BLOCK, qw), F32)]
    if with_lse:
        out_specs, out_shape = [o_spec, o_spec], [o_shape, o_shape]
        scratch.append(pltpu.VMEM((BLOCK, qw), F32))
    else:
        out_specs, out_shape = o_spec, o_shape
    res = _pcall(
        body, name=name, grid=(NB, r, nblk), in_specs=in_specs, out_specs=out_specs, out_shape=out_shape,
        scratch_shapes=scratch, compiler_params=_params(("parallel", "parallel", "arbitrary")),
    )(qkv3, qkv3, qkv3, qkv3, qkv3, sinks)
    if with_lse:
        return res[0].reshape(NB * T, qw), res[1].reshape(NB * T, qw)
    return res.reshape(NB * T, qw)


def _attn_bwd(qkv, do, dlse, cosf, sins, sinks, g, prev_out, *, NB, T, name):
    geo = _Geom(g)
    r, nq, qw, kw, grp = geo.r, geo.nq, geo.qw, geo.kw, geo.grp
    tsub = T // r
    nblk = tsub // BLOCK
    qkv3 = qkv.reshape(NB, tsub, r * QKV_STRIDE)
    do3 = do.reshape(NB, tsub, r * qw)
    cos3 = cosf.reshape(NB, tsub, r * 2 * HEAD_DIM)
    sin3 = sins.reshape(NB, tsub, r * 2 * HEAD_DIM)
    has_dlse = dlse is not None
    ngrp = len(B_PATTERNS)
    ow_q = qw if g is None else QB_W
    ow_k = kw if g is None else QB_W
    cq = (lambda j: j) if g is None else (lambda j: j * ngrp + g)

    def body(*refs):
        it = iter(refs)
        q_ref, kp_ref, kc_ref, vp_ref, vc_ref, do_ref = (next(it) for _ in range(6))
        dl_ref = next(it) if has_dlse else None
        cq_ref, sq_ref, ck_ref, sk_ref, sink_ref = (next(it) for _ in range(5))
        if prev_out is not None:
            for _ in range(3):
                next(it)
        dq_ref, dk_ref, dv_ref, ds_ref = (next(it) for _ in range(4))
        dq_s, akp, akc, avp, avc, car_k, car_v = (next(it) for _ in range(7))
        b, j, i = pl.program_id(0), pl.program_id(1), pl.program_id(2)

        @pl.when(jnp.logical_and(b == 0, jnp.logical_and(j == 0, i == 0)))
        def _():
            ds_ref[...] = jnp.zeros_like(ds_ref)

        @pl.when(i == 0)
        def _():
            car_k[...] = jnp.zeros_like(car_k)
            car_v[...] = jnp.zeros_like(car_v)

        @pl.when(i == nblk)
        def _():
            akp[...] = jnp.zeros_like(akp)
            akc[...] = jnp.zeros_like(akc)
            avp[...] = jnp.zeros_like(avp)
            avc[...] = jnp.zeros_like(avc)

        @pl.when(i < nblk)
        def _():
            mask_p, mask_c = _masks(i, geo.n_back)
            dnt = (((1,), (1,)), ((), ()))
            dn0 = (((0,), (0,)), ((), ()))
            lane = lax.broadcasted_iota(jnp.int32, (1, 128), 1)
            dsink = jnp.zeros((1, 128), F32)
            for h in range(nq):
                kh = h // grp
                qs, ks = slice(h * HEAD_DIM, (h + 1) * HEAD_DIM), slice(kh * HEAD_DIM, (kh + 1) * HEAD_DIM)
                q, kp, kc = q_ref[0, :, qs], kp_ref[0, :, ks], kc_ref[0, :, ks]
                vp, vc, dout = vp_ref[0, :, ks], vc_ref[0, :, ks], do_ref[0, :, qs]
                sink = sink_ref[h] if geo.sink else None
                pp, pc, m, den, es = _scores(q, kp, kc, mask_p, mask_c, sink)
                inv = 1.0 / den
                pp, pc = pp * inv, pc * inv
                dpp = lax.dot_general(dout, vp, dnt, preferred_element_type=F32)
                dpc = lax.dot_general(dout, vc, dnt, preferred_element_type=F32)
                delta = jnp.sum(pp * dpp, axis=1, keepdims=True) + jnp.sum(pc * dpc, axis=1, keepdims=True)
                if geo.sink:
                    dsink = dsink + jnp.where(lane == h, -jnp.sum(es * inv * delta), 0.0)
                if has_dlse:
                    delta = delta - dl_ref[0, :, h * HEAD_DIM:h * HEAD_DIM + 1]
                dsp = (pp * (dpp - delta) * SCALE).astype(BF16)
                dsc = (pc * (dpc - delta) * SCALE).astype(BF16)
                dq_s[:, qs] = (jnp.dot(dsp, kp, preferred_element_type=F32)
                               + jnp.dot(dsc, kc, preferred_element_type=F32))
                c_kp = lax.dot_general(dsp, q, dn0, preferred_element_type=F32)
                c_kc = lax.dot_general(dsc, q, dn0, preferred_element_type=F32)
                c_vp = lax.dot_general(pp.astype(BF16), dout, dn0, preferred_element_type=F32)
                c_vc = lax.dot_general(pc.astype(BF16), dout, dn0, preferred_element_type=F32)
                if h % grp == 0:
                    akp[:, ks], akc[:, ks], avp[:, ks], avc[:, ks] = c_kp, c_kc, c_vp, c_vc
                else:
                    akp[:, ks] += c_kp
                    akc[:, ks] += c_kc
                    avp[:, ks] += c_vp
                    avc[:, ks] += c_vc
            dq = dq_s[...]
            dq_ref[0] = (dq * _widen(cq_ref[0], qw) - _rot_half(dq) * _widen(sq_ref[0], qw)).astype(BF16)
            if geo.sink:
                ds_ref[0:1, :] += dsink

        dk = car_k[...] + akp[...]
        dk_ref[0] = (dk * _widen(ck_ref[0], kw) - _rot_half(dk) * _widen(sk_ref[0], kw)).astype(BF16)
        dv_ref[0] = (car_v[...] + avp[...]).astype(BF16)
        car_k[...] = akc[...]
        car_v[...] = avc[...]

    cur = lambda i: jnp.minimum(i, nblk - 1)
    prv = lambda i: jnp.maximum(jnp.minimum(i, nblk - 1) - 1, 0)
    outb = lambda i: jnp.maximum(i - 1, 0)
    tabw = 2 * HEAD_DIM
    in_specs = [
        pl.BlockSpec((1, BLOCK, qw), lambda b, j, i: (b, cur(i), geo.qidx(j))),
        pl.BlockSpec((1, BLOCK, kw), lambda b, j, i: (b, prv(i), geo.kidx(j))),
        pl.BlockSpec((1, BLOCK, kw), lambda b, j, i: (b, cur(i), geo.kidx(j))),
        pl.BlockSpec((1, BLOCK, kw), lambda b, j, i: (b, prv(i), geo.vidx(j))),
        pl.BlockSpec((1, BLOCK, kw), lambda b, j, i: (b, cur(i), geo.vidx(j))),
        pl.BlockSpec((1, BLOCK, qw), lambda b, j, i: (b, cur(i), j)),
    ]
    ins = [qkv3, qkv3, qkv3, qkv3, qkv3, do3]
    if has_dlse:
        in_specs.append(pl.BlockSpec((1, BLOCK, qw), lambda b, j, i: (b, cur(i), j)))
        ins.append(dlse.reshape(NB, tsub, r * qw))
    in_specs += [
        pl.BlockSpec((1, BLOCK, tabw), lambda b, j, i: (b, cur(i), j)),
        pl.BlockSpec((1, BLOCK, tabw), lambda b, j, i: (b, cur(i), j)),
        pl.BlockSpec((1, BLOCK, tabw), lambda b, j, i: (b, outb(i), j)),
        pl.BlockSpec((1, BLOCK, tabw), lambda b, j, i: (b, outb(i), j)),
        pl.BlockSpec(memory_space=pltpu.SMEM),
    ]
    ins += [cos3, sin3, cos3, sin3, sinks]
    aliases = {}
    if prev_out is not None:
        base = len(ins)
        for n, arr in enumerate(prev_out):
            in_specs.append(pl.BlockSpec(memory_space=pl.ANY))
            ins.append(arr.reshape(NB, tsub, r * ow_k if n else r * ow_q))
            aliases[base + n] = n
    out_specs = [
        pl.BlockSpec((1, BLOCK, qw), lambda b, j, i: (b, cur(i), cq(j))),
        pl.BlockSpec((1, BLOCK, kw), lambda b, j, i: (b, outb(i), cq(j))),
        pl.BlockSpec((1, BLOCK, kw), lambda b, j, i: (b, outb(i), cq(j))),
        pl.BlockSpec((8, 128), lambda b, j, i: (0, 0)),
    ]
    out_shape = [
        jax.ShapeDtypeStruct((NB, tsub, r * ow_q), BF16),
        jax.ShapeDtypeStruct((NB, tsub, r * ow_k), BF16),
        jax.ShapeDtypeStruct((NB, tsub, r * ow_k), BF16),
        jax.ShapeDtypeStruct((8, 128), F32),
    ]
    scratch = [pltpu.VMEM((BLOCK, qw), F32)] + [pltpu.VMEM((BLOCK, kw), F32) for _ in range(6)]
    dq, dk, dv, dsink = _pcall(
        body, name=name, grid=(NB, r, nblk + 1), in_specs=in_specs, out_specs=out_specs, out_shape=out_shape,
        scratch_shapes=scratch, input_output_aliases=aliases,
        compiler_params=_params(("arbitrary", "arbitrary", "arbitrary")),
    )(*ins)
    return dq.reshape(NB * T, ow_q), dk.reshape(NB * T, ow_k), dv.reshape(NB * T, ow_k), dsink


class _Rows:
    def __init__(self, N, T, tm):
        self.N, self.tm, self.tpe, self.grid = N, tm, T // tm, (N // tm,)

    def row(self, w, col=0):
        return pl.BlockSpec((self.tm, w), lambda i: (i, col))

    def ex(self, w):
        return pl.BlockSpec((1, 1, w), lambda i: (i // self.tpe, 0, 0))

    def const(self, shape):
        return pl.BlockSpec(shape, lambda i: tuple(0 for _ in shape))

    def first_of_example(self):
        return pl.program_id(0) % self.tpe == 0


def _acc(ref, first, val):
    @pl.when(first)
    def _():
        ref[0] = val

    @pl.when(jnp.logical_not(first))
    def _():
        ref[0] += val


def _colsum(v):
    return jnp.sum(v, axis=0, keepdims=True)


def _ln_stats(r):
    mu = jnp.mean(r, axis=-1, keepdims=True)
    xc = r - mu
    var = jnp.mean(xc * xc, axis=-1, keepdims=True)
    rstd = lax.rsqrt(var + LN_EPS)
    return xc * rstd, rstd


def _ln_bwd(dy, xhat, rstd, gain):
    dxh = dy * gain
    return rstd * (dxh - jnp.mean(dxh, axis=-1, keepdims=True) - xhat * jnp.mean(dxh * xhat, axis=-1, keepdims=True))


def _silu_parts(v):
    s = jax.nn.sigmoid(v)
    return v * s, s * (1.0 + v * (1.0 - s))


def _local_step(x, mod, positions, w_in, w_a, w_b, w_o, w_gu, w_d, sinks, ln1_g, ln1_b, ln2_g, ln2_b, target):
    NB, T, D = x.shape
    N = NB * T
    F = w_d.shape[0]
    x2 = x.reshape(N, D)
    tgt2 = target.reshape(N, D)
    shift_m, scale_m, gate_m, shift_f, scale_f, gate_f = [mod[:, None, k * D:(k + 1) * D] for k in range(6)]
    cosf, sins = _rope_tables(positions)
    col = jnp.arange(QKV_W)
    flags = jnp.logical_or(col < OFF_VB, jnp.logical_and(col >= OFF_KA, col < OFF_VA)).astype(F32)[None]
    R = _Rows(N, T, _pick(T, 256))
    sds = jax.ShapeDtypeStruct
    exsum = lambda w=D: sds((NB, 1, w), F32)

    qkv, u = _inproj(x2, scale_m, shift_m, w_in, cosf, sins, flags, T=T, name="inproj_qkv")
    gates = _mm(u, w_in[:, QKV_W:], name="inproj_gates")
    oa = _attn_fwd(qkv, sinks, None, NB=NB, T=T, name="attn_a_fwd")
    ob_parts = [_attn_fwd(qkv, sinks, g, NB=NB, T=T, name=f"attn_b{g}_fwd") for g in range(len(B_PATTERNS))]
    (o1, l1), (o2, l2), (o3, l3) = ob_parts

    def merge_fwd(o1r, o2r, o3r, l1r, l2r, l3r, ob_ref):
        la, lb, lc = l1r[...], l2r[...], l3r[...]
        mx = jnp.maximum(jnp.maximum(la, lb), lc)
        ea, eb, ec = jnp.exp(la - mx), jnp.exp(lb - mx), jnp.exp(lc - mx)
        ob_ref[...] = ((ea * o1r[...] + eb * o2r[...] + ec * o3r[...]) / (ea + eb + ec)).astype(BF16)

    ob = _pcall(merge_fwd, name="merge_fwd", grid=R.grid, in_specs=[R.row(GB_W)] * 6, out_specs=R.row(GB_W),
                out_shape=sds((N, GB_W), BF16), compiler_params=_params(("parallel",)))(o1, o2, o3, l1, l2, l3)

    ya = _mm(oa, w_a, name="branch_a")
    yb = _mm(ob, w_b, name="branch_b")

    def gate_fwd(ya_r, yb_r, ga_r, gb_r, mg_ref):
        mg_ref[...] = (jax.nn.sigmoid(ga_r[...]) * ya_r[...] + jax.nn.sigmoid(gb_r[...]) * yb_r[...]).astype(BF16)

    merged = _pcall(gate_fwd, name="gate_fwd", grid=R.grid, in_specs=[R.row(D), R.row(D), R.row(D, 0), R.row(D, 1)],
                    out_specs=R.row(D), out_shape=sds((N, D), BF16),
                    compiler_params=_params(("parallel",)))(ya, yb, gates, gates)
    y = _mm(merged, w_o, name="out_proj")

    def norm1_fwd(x_r, y_r, gm_r, g_r, b_r, sf_r, hf_r, r1_ref, x1_ref, u2_ref):
        r1 = ALPHA * x_r[...] + (1.0 + gm_r[0]) * y_r[...]
        xhat, _ = _ln_stats(r1)
        x1 = xhat * g_r[...] + b_r[...]
        r1_ref[...] = r1
        x1_ref[...] = x1
        u2_ref[...] = (x1 * (1.0 + sf_r[0]) + hf_r[0]).astype(BF16)

    r1, x1, u2 = _pcall(
        norm1_fwd, name="norm1_fwd", grid=R.grid,
        in_specs=[R.row(D), R.row(D), R.ex(D), R.const((1, D)), R.const((1, D)), R.ex(D), R.ex(D)],
        out_specs=[R.row(D)] * 3, out_shape=[sds((N, D), F32), sds((N, D), F32), sds((N, D), BF16)],
        compiler_params=_params(("parallel",)))(x2, y, gate_m, ln1_g, ln1_b, scale_f, shift_f)

    hgu = _mm(u2, w_gu, name="ffn_up")
    Rh = _Rows(N, T, _pick(T, 128))

    def act_fwd(hg_r, hu_r, a_ref):
        sl, _ = _silu_parts(hg_r[...])
        a_ref[...] = (sl * hu_r[...]).astype(BF16)

    act = _pcall(act_fwd, name="act_fwd", grid=Rh.grid, in_specs=[Rh.row(F, 0), Rh.row(F, 1)], out_specs=Rh.row(F),
                 out_shape=sds((N, F), BF16), compiler_params=_params(("parallel",)))(hgu, hgu)
    y2 = _mm(act, w_d, name="ffn_down")

    def norm2_loss_bwd(x1_r, y2_r, t_r, gf_r, g_r, b_r, dy2_ref, dx1_ref, dgf_ref, dg_ref, db_ref, loss_ref):
        first = R.first_of_example()
        y2v = y2_r[...]
        r2 = ALPHA * x1_r[...] + (1.0 + gf_r[0]) * y2v
        xhat, rstd = _ln_stats(r2)
        err = xhat * g_r[...] + b_r[...] - t_r[...]
        dx2 = err * (1.0 / D)
        dr2 = _ln_bwd(dx2, xhat, rstd, g_r[...])
        dy2_ref[...] = ((1.0 + gf_r[0]) * dr2).astype(BF16)
        dx1_ref[...] = ALPHA * dr2
        _acc(dgf_ref, first, _colsum(dr2 * y2v))
        _acc(dg_ref, first, _colsum(dx2 * xhat))
        _acc(db_ref, first, _colsum(dx2))
        part = 0.5 * jnp.sum(jnp.mean(err * err, axis=-1, keepdims=True))
        _acc(loss_ref, first, jnp.broadcast_to(part, (1, 128)))

    dy2, dx1p, dgate_f, dg2, db2, loss_p = _pcall(
        norm2_loss_bwd, name="norm2_loss_bwd", grid=R.grid,
        in_specs=[R.row(D), R.row(D), R.row(D), R.ex(D), R.const((1, D)), R.const((1, D))],
        out_specs=[R.row(D), R.row(D), R.ex(D), R.ex(D), R.ex(D), R.ex(128)],
        out_shape=[sds((N, D), BF16), sds((N, D), F32), exsum(), exsum(), exsum(), exsum(128)],
        compiler_params=_params(("arbitrary",)))(x1, y2, tgt2, gate_f, ln2_g, ln2_b)

    d_act = _mm(dy2, w_d, tb=True, name="ffn_down_dx")
    g_wd = _mm(act, dy2, ta=True, name="ffn_down_dw")

    def act_bwd(h_r, da_r, dh_ref):
        hg, hu, da = h_r[:, :F], h_r[:, F:], da_r[...]
        sl, dsl = _silu_parts(hg)
        dh_ref[:, :F] = (da * hu * dsl).astype(BF16)
        dh_ref[:, F:] = (da * sl).astype(BF16)

    dh = _pcall(act_bwd, name="act_bwd", grid=Rh.grid, in_specs=[Rh.row(2 * F), Rh.row(F)], out_specs=Rh.row(2 * F),
                out_shape=sds((N, 2 * F), BF16), compiler_params=_params(("parallel",)))(hgu, d_act)
    du2 = _mm(dh, w_gu, tb=True, name="ffn_up_dx")
    g_wgu = _mm(u2, dh, ta=True, name="ffn_up_dw")

    def norm1_bwd(dx1p_r, du2_r, x1_r, r1_r, y_r, sf_r, gm_r, g_r,
                  dxp_ref, dy_ref, dsf_ref, dhf_ref, dgm_ref, dg_ref, db_ref):
        first = R.first_of_example()
        du2v = du2_r[...]
        dx1 = dx1p_r[...] + du2v * (1.0 + sf_r[0])
        xhat, rstd = _ln_stats(r1_r[...])
        dr1 = _ln_bwd(dx1, xhat, rstd, g_r[...])
        dxp_ref[...] = ALPHA * dr1
        dy_ref[...] = ((1.0 + gm_r[0]) * dr1).astype(BF16)
        _acc(dsf_ref, first, _colsum(du2v * x1_r[...]))
        _acc(dhf_ref, first, _colsum(du2v))
        _acc(dgm_ref, first, _colsum(dr1 * y_r[...]))
        _acc(dg_ref, first, _colsum(dx1 * xhat))
        _acc(db_ref, first, _colsum(dx1))

    dxp, dy, dscale_f, dshift_f, dgate_m, dg1, db1 = _pcall(
        norm1_bwd, name="norm1_bwd", grid=R.grid,
        in_specs=[R.row(D)] * 5 + [R.ex(D), R.ex(D), R.const((1, D))],
        out_specs=[R.row(D), R.row(D)] + [R.ex(D)] * 5,
        out_shape=[sds((N, D), F32), sds((N, D), BF16)] + [exsum()] * 5,
        compiler_params=_params(("arbitrary",)))(dx1p, du2, x1, r1, y, scale_f, gate_m, ln1_g)

    dmerged = _mm(dy, w_o, tb=True, name="out_proj_dx")
    g_wo = _mm(merged, dy, ta=True, name="out_proj_dw")

    def gate_bwd(dm_r, ya_r, yb_r, ga_r, gb_r, dya_ref, dyb_ref, dg_ref):
        dm = dm_r[...]
        sa, sb = jax.nn.sigmoid(ga_r[...]), jax.nn.sigmoid(gb_r[...])
        dya_ref[...] = (dm * sa).astype(BF16)
        dyb_ref[...] = (dm * sb).astype(BF16)
        dg_ref[:, :D] = (dm * ya_r[...] * sa * (1.0 - sa)).astype(BF16)
        dg_ref[:, D:] = (dm * yb_r[...] * sb * (1.0 - sb)).astype(BF16)

    dya, dyb, dgates = _pcall(
        gate_bwd, name="gate_bwd", grid=R.grid, in_specs=[R.row(D)] * 3 + [R.row(D, 0), R.row(D, 1)],
        out_specs=[R.row(D), R.row(D), R.row(2 * D)],
        out_shape=[sds((N, D), BF16), sds((N, D), BF16), sds((N, 2 * D), BF16)],
        compiler_params=_params(("parallel",)))(dmerged, ya, yb, gates, gates)

    doa = _mm(dya, w_a, tb=True, out_dtype=BF16, name="branch_a_dx")
    g_wa = _mm(oa, dya, ta=True, name="branch_a_dw")
    dob = _mm(dyb, w_b, tb=True, name="branch_b_dx")
    g_wb = _mm(ob, dyb, ta=True, name="branch_b_dw")

    seg = (jnp.arange(GB_W)[:, None] // HEAD_DIM == jnp.arange(GB_W)[None, :] // HEAD_DIM).astype(BF16)

    def merge_bwd(dob_r, o1r, o2r, o3r, l1r, l2r, l3r, seg_r, d1, d2, d3, e1, e2, e3):
        dob_v = dob_r[...]
        la, lb, lc = l1r[...], l2r[...], l3r[...]
        mx = jnp.maximum(jnp.maximum(la, lb), lc)
        ea, eb, ec = jnp.exp(la - mx), jnp.exp(lb - mx), jnp.exp(lc - mx)
        inv = 1.0 / (ea + eb + ec)
        ws = [ea * inv, eb * inv, ec * inv]

        def headsum(v):
            hi = v.astype(BF16)
            r1_ = v - hi.astype(F32)
            mid = r1_.astype(BF16)
            lo = (r1_ - mid.astype(F32)).astype(BF16)
            sm = seg_r[...]
            return (jnp.dot(hi, sm, preferred_element_type=F32) + jnp.dot(mid, sm, preferred_element_type=F32)
                    + jnp.dot(lo, sm, preferred_element_type=F32))

        dws = [headsum(dob_v * o[...]) for o in (o1r, o2r, o3r)]
        mean = ws[0] * dws[0] + ws[1] * dws[1] + ws[2] * dws[2]
        for w_, dw_, d_ref, e_ref in zip(ws, dws, (d1, d2, d3), (e1, e2, e3)):
            d_ref[...] = (w_ * dob_v).astype(BF16)
            e_ref[...] = w_ * (dw_ - mean)

    mb = _pcall(
        merge_bwd, name="merge_bwd", grid=R.grid, in_specs=[R.row(GB_W)] * 7 + [R.const((GB_W, GB_W))],
        out_specs=[R.row(GB_W)] * 6, out_shape=[sds((N, GB_W), BF16)] * 3 + [sds((N, GB_W), F32)] * 3,
        compiler_params=_params(("parallel",)))(dob, o1, o2, o3, l1, l2, l3, seg)
    do_b, dlse_b = mb[:3], mb[3:]

    dqa, dka, dva, dsink = _attn_bwd(qkv, doa, None, cosf, sins, sinks, None, None, NB=NB, T=T, name="attn_a_bwd")
    prev = None
    for g in range(len(B_PATTERNS)):
        dqb, dkb, dvb, _ = _attn_bwd(qkv, do_b[g], dlse_b[g], cosf, sins, sinks, g, prev, NB=NB, T=T,
                                     name=f"attn_b{g}_bwd")
        prev = (dqb, dkb, dvb)

    segs = [(dqa, OFF_QA, QA_W), (dqb, OFF_QB, QB_W), (dkb, OFF_KB, QB_W), (dvb, OFF_VB, QB_W),
            (dka, OFF_KA, KA_W), (dva, OFF_VA, KA_W), (dgates, QKV_W, 2 * D)]
    du = None
    g_win = []
    for n, (dseg, off, wid) in enumerate(segs):
        du = _mm(dseg, w_in[:, off:off + wid], tb=True, add=du, name=f"inproj_dx{n}")
        g_win.append(_mm(u, dseg, ta=True, name=f"inproj_dw{n}"))

    def x_bwd(dxp_r, du_r, x_r, sm_r, gx_ref, dsm_ref, dhm_ref):
        first = R.first_of_example()
        duv = du_r[...]
        gx_ref[...] = dxp_r[...] + duv * (1.0 + sm_r[0])
        _acc(dsm_ref, first, _colsum(duv * x_r[...]))
        _acc(dhm_ref, first, _colsum(duv))

    gx, dscale_m, dshift_m = _pcall(
        x_bwd, name="x_bwd", grid=R.grid, in_specs=[R.row(D)] * 3 + [R.ex(D)],
        out_specs=[R.row(D), R.ex(D), R.ex(D)], out_shape=[sds((N, D), F32), exsum(), exsum()],
        compiler_params=_params(("arbitrary",)))(dxp, du, x2, scale_m)

    dmod = jnp.concatenate([dshift_m, dscale_m, dgate_m, dshift_f, dscale_f, dgate_f], axis=-1)[:, 0]
    ln_grads = jnp.concatenate([dg1, db1, dg2, db2], axis=1)
    return dict(loss=loss_p[:, 0, 0], grad_x=gx.reshape(NB, T, D), g_win=g_win, g_wa=g_wa, g_wb=g_wb, g_wo=g_wo,
                g_wgu=g_wgu, g_wd=g_wd, dmod=dmod, ln_grads=ln_grads, dsink=dsink[0, :A_Q_HEADS])


def _coords():
    return lax.axis_index("x"), lax.axis_index("y"), lax.axis_index("c")


def _allgather_small(blk, *, name):
    m_per, n = blk.shape

    def body(x_ref, out_ref, send_sems, recv_sems, local_sem):
        x, y, c = _coords()
        me, sibling = (x, y, c), (x, y, 1 - c)
        chips = [(1 - x, y), (x, 1 - y), (1 - x, 1 - y)]

        def rows(px, py, pc):
            return out_ref.at[pl.ds((4 * px + 2 * py + pc) * m_per, m_per), :]

        def copy(k, block, to, src=None):
            return pltpu.make_async_remote_copy(
                src_ref=rows(*block) if src is None else src, dst_ref=rows(*block),
                send_sem=send_sems.at[k], recv_sem=recv_sems.at[k], device_id=to, device_id_type=MESH)

        mine = pltpu.make_async_copy(x_ref, rows(*me), local_sem)
        mine.start()
        first = [copy(0, me, sibling, src=x_ref)]
        first += [copy(1 + j, me, (*chip, c), src=x_ref) for j, chip in enumerate(chips)]
        for cp in first:
            cp.start()
        passed = [copy(4 + j, (*chip, c), sibling) for j, chip in enumerate(chips)]
        for j, chip in enumerate(chips):
            copy(1 + j, (*chip, c), me).wait_recv()
            passed[j].start()
        copy(0, sibling, me).wait_recv()
        for j, chip in enumerate(chips):
            copy(4 + j, (*chip, 1 - c), me).wait_recv()
        for cp in first + passed:
            cp.wait_send()
        mine.wait()

    return _pcall(
        body, name=name, out_shape=jax.ShapeDtypeStruct((8 * m_per, n), blk.dtype),
        in_specs=[pl.BlockSpec(memory_space=pltpu.VMEM)], out_specs=pl.BlockSpec(memory_space=pltpu.VMEM),
        scratch_shapes=[pltpu.SemaphoreType.DMA((7,)), pltpu.SemaphoreType.DMA((7,)), pltpu.SemaphoreType.DMA],
        compiler_params=pltpu.CompilerParams(vmem_limit_bytes=VMEM_LIMIT_BYTES),
    )(blk)


def _exchange(src, dst_shape, dst_dtype, plan, *, name, dst_init=None):
    nloc, nrem = (len(p) for p in plan(0, 0, 0))

    def body(*refs):
        refs = list(refs)
        src_ref = refs.pop(0) if src is not None else None
        if dst_init is not None:
            refs.pop(0)
        dst_ref, send_sems, recv_sems, local_sems = refs
        if src_ref is None:
            src_ref = dst_ref
        x, y, c = _coords()
        local, remote = plan(x, y, c)
        at = lambda ref, idx: ref.at[idx] if idx else ref
        lcs = [pltpu.make_async_copy(at(src_ref, si), at(dst_ref, di), local_sems.at[n])
               for n, (si, di) in enumerate(local)]
        for cp in lcs:
            cp.start()
        sends = [pltpu.make_async_remote_copy(src_ref=at(src_ref, si), dst_ref=at(dst_ref, di),
                                              send_sem=send_sems.at[n], recv_sem=recv_sems.at[n],
                                              device_id=peer, device_id_type=MESH)
                 for n, (si, di, ri, peer) in enumerate(remote)]
        for cp in sends:
            cp.start()
        for n, (si, di, ri, peer) in enumerate(remote):
            pltpu.make_async_remote_copy(src_ref=at(src_ref, si), dst_ref=at(dst_ref, ri),
                                         send_sem=send_sems.at[n], recv_sem=recv_sems.at[n],
                                         device_id=peer, device_id_type=MESH).wait_recv()
        for cp in sends:
            cp.wait_send()
        for cp in lcs:
            cp.wait()

    hbm = pl.BlockSpec(memory_space=pl.ANY)
    ins = ([src] if src is not None else []) + ([dst_init] if dst_init is not None else [])
    aliases = {len(ins) - 1: 0} if dst_init is not None else {}
    return _pcall(
        body, name=name, out_shape=jax.ShapeDtypeStruct(dst_shape, dst_dtype),
        in_specs=[hbm] * len(ins), out_specs=hbm, input_output_aliases=aliases,
        scratch_shapes=[pltpu.SemaphoreType.DMA((max(nrem, 1),)), pltpu.SemaphoreType.DMA((max(nrem, 1),)),
                        pltpu.SemaphoreType.DMA((max(nloc, 1),))],
        compiler_params=pltpu.CompilerParams(has_side_effects=True),
    )(*ins)


def _other_chips(x, y):
    return [(1 - x, y), (x, 1 - y), (1 - x, 1 - y)]


def _gather_weights(packed):
    _, hr, wd = packed.shape

    def plan_ici(x, y, c):
        k = 2 * x + y
        local = [((), (k,))]
        remote = [((c,), (k, c), (2 * px + py, c), (px, py, c)) for px, py in _other_chips(x, y)]
        return local, remote

    def plan_d2d(x, y, c):
        remote = [((2 * px + py, c), (2 * px + py, c), (2 * px + py, 1 - c), (x, y, 1 - c))
                  for px, py in _other_chips(x, y)]
        return [], remote

    full = _exchange(packed, (4, 2, hr, wd), packed.dtype, plan_ici, name="gather_w_ici")
    return _exchange(None, (4, 2, hr, wd), packed.dtype, plan_d2d, name="gather_w_d2d", dst_init=full)


def _add_pairs(a, b, *, name):
    s, hr, wd = a.shape
    tr = _pick(hr, 512, 16)

    def body(a_ref, b_ref, o_ref):
        o_ref[...] = (a_ref[...].astype(F32) + b_ref[...].astype(F32)).astype(BF16)

    spec = pl.BlockSpec((1, tr, wd), lambda j, i: (j, i, 0))
    return _pcall(body, name=name, grid=(s, hr // tr), in_specs=[spec, spec], out_specs=spec,
                  out_shape=jax.ShapeDtypeStruct(a.shape, BF16), compiler_params=_params(("parallel", "parallel")))(a, b)


def _sum_chips(b, *, name):
    s, hr, wd = b.shape
    tr = _pick(hr, 512, 16)

    def body(b_ref, o_ref):
        acc = b_ref[0].astype(F32)
        for k in range(1, s):
            acc = acc + b_ref[k].astype(F32)
        o_ref[...] = acc

    return _pcall(body, name=name, grid=(hr // tr,), in_specs=[pl.BlockSpec((s, tr, wd), lambda i: (0, i, 0))],
                  out_specs=pl.BlockSpec((tr, wd), lambda i: (i, 0)), out_shape=jax.ShapeDtypeStruct((hr, wd), F32),
                  compiler_params=_params(("parallel",)))(b)


def _reduce_scatter_grads(g, ci):
    _, nchip, hr, wd = g.shape

    def plan_pair(x, y, c):
        return [], [((1 - c,), (), (), (x, y, 1 - c))]

    def plan_chips(x, y, c):
        k = 2 * x + y
        local = [((k,), (k,))]
        remote = [((2 * px + py,), (k,), (2 * px + py,), (px, py, c)) for px, py in _other_chips(x, y)]
        return local, remote

    def plan_halves(x, y, c):
        return [((), (c,))], [((), (c,), (1 - c,), (x, y, 1 - c))]

    from_sibling = _exchange(g, (nchip, hr, wd), BF16, plan_pair, name="reduce_g_pair")
    pair = _add_pairs(lax.dynamic_index_in_dim(g, ci, 0, keepdims=False), from_sibling, name="reduce_g_pair_add")
    landed = _exchange(pair, (nchip, hr, wd), BF16, plan_chips, name="reduce_g_chips")
    half = _sum_chips(landed, name="reduce_g_chip_sum")
    return _exchange(half, (2, hr, wd), F32, plan_halves, name="reduce_g_halves")


def _ada_fwd(c_all, w_sh, b_sh, *, name):
    nb, d = c_all.shape
    wcols = w_sh.shape[1]
    tn = _pick(wcols, 512)

    def body(c_ref, w_ref, b_ref, o_ref, a_ref):
        cv = c_ref[...]
        act = cv * jax.nn.sigmoid(cv)
        a_ref[...] = act
        o_ref[...] = jnp.dot(act.astype(BF16), w_ref[...].astype(BF16), preferred_element_type=F32) + b_ref[...]

    return _pcall(
        body, name=name, grid=(wcols // tn,),
        in_specs=[pl.BlockSpec((nb, d), lambda j: (0, 0)), pl.BlockSpec((d, tn), lambda j: (0, j)),
                  pl.BlockSpec((1, tn), lambda j: (0, j))],
        out_specs=[pl.BlockSpec((nb, tn), lambda j: (0, j)), pl.BlockSpec((nb, d), lambda j: (0, 0))],
        out_shape=[jax.ShapeDtypeStruct((nb, wcols), F32), jax.ShapeDtypeStruct((nb, d), F32)],
        compiler_params=_params(("arbitrary",)))(c_all, w_sh, b_sh)


def _sum_devices(g, *, name):
    nd, m, w = g.shape

    def body(g_ref, o_ref):
        acc = g_ref[0]
        for k in range(1, nd):
            acc = acc + g_ref[k]
        o_ref[...] = acc

    return _pcall(body, name=name, out_shape=jax.ShapeDtypeStruct((m, w), F32),
                  compiler_params=pltpu.CompilerParams(vmem_limit_bytes=VMEM_LIMIT_BYTES))(g)


def _adamw(w, g, m, v, *, name):
    rows, cols = w.shape
    tr = _pick(rows, max(8, (1 << 18) // cols), 8)
    c1 = 1.0 / (1.0 - ADAM_B1 ** ADAM_STEP)
    c2 = 1.0 / (1.0 - ADAM_B2 ** ADAM_STEP)

    def body(w_ref, g_ref, m_ref, v_ref, d_ref, nm_ref, nv_ref):
        gv = g_ref[...]
        nm = ADAM_B1 * m_ref[...] + (1.0 - ADAM_B1) * gv
        nv = ADAM_B2 * v_ref[...] + (1.0 - ADAM_B2) * (gv * gv)
        d_ref[...] = -ADAM_LR * ((nm * c1) / (jnp.sqrt(nv * c2) + ADAM_EPS) + ADAM_WD * w_ref[...])
        nm_ref[...] = nm
        nv_ref[...] = nv

    spec = pl.BlockSpec((tr, cols), lambda i: (i, 0))
    shp = jax.ShapeDtypeStruct((rows, cols), F32)
    return _pcall(body, name=name, grid=(rows // tr,), in_specs=[spec] * 4, out_specs=[spec] * 3,
                  out_shape=[shp] * 3, compiler_params=_params(("parallel",)))(w, g, m, v)


PACK_W = 1024


def _pack_rows(parts):
    return jnp.concatenate([p.reshape(-1) for p in parts]).reshape(-1, PACK_W)


def _unpack_rows(flat, shapes):
    out, off = [], 0
    for shp in shapes:
        n = math.prod(shp)
        out.append(flat[off:off + n].reshape(shp))
        off += n
    return out


def _permute_in_cols(w):
    qa, ka, va = w[:, :QA_W], w[:, QA_W:QA_W + KA_W], w[:, QA_W + KA_W:QA_W + 2 * KA_W]
    o = QA_W + 2 * KA_W
    return jnp.concatenate([qa, w[:, o:o + 3 * QB_W], ka, va, w[:, o + 3 * QB_W:]], axis=1)


def kernel(x, c, positions, w_ada, b_ada, w_in, sinks, w_branch_a, w_branch_b, w_o, ln1_g, ln1_b, w_gate_up, w_down, ln2_g, ln2_b, loss_target, m_w_ada, m_b_ada, m_w_in, m_sinks, m_w_branch_a, m_w_branch_b, m_w_o, m_ln1_g, m_ln1_b, m_w_gate_up, m_w_down, m_ln2_g, m_ln2_b, v_w_ada, v_b_ada, v_w_in, v_sinks, v_w_branch_a, v_w_branch_b, v_w_o, v_ln1_g, v_ln1_b, v_w_gate_up, v_w_down, v_ln2_g, v_ln2_b):
    xi, yi, ci = _coords()
    chip = 2 * xi + yi
    dev = 4 * xi + 2 * yi + ci
    NB, T, D = x.shape
    nchip, ndev = 4, 8
    ada_cols = w_ada.shape[2]

    c_blk = jnp.zeros((8, D), F32).at[:NB].set(c)
    c_all = _allgather_small(c_blk, name="gather_c").reshape(ndev, 8, D)[:, :NB].reshape(ndev * NB, D)
    b_sh = lax.dynamic_slice(b_ada, (0, chip * ada_cols), (1, ada_cols))
    mod_part, c_act = _ada_fwd(c_all, w_ada[0], b_sh, name="ada_fwd")
    mod_g = _allgather_small(mod_part, name="gather_mod").reshape(nchip, 2, ndev * NB, ada_cols)[:, 0]
    mod_all = jnp.transpose(mod_g, (1, 0, 2)).reshape(ndev * NB, nchip * ada_cols)
    mod = lax.dynamic_slice(mod_all, (NB * dev, 0), (NB, nchip * ada_cols))

    shard_shapes = [w_in.shape[1:], w_branch_a.shape[1:], w_branch_b.shape[1:], w_o.shape[1:],
                    w_gate_up.shape[1:], w_down.shape[1:]]
    packed = _pack_rows([w_in[0], w_branch_a[0], w_branch_b[0], w_o[0], w_gate_up[0], w_down[0]]).astype(BF16)
    rows = packed.shape[0]
    hr = rows // 2
    gathered = _gather_weights(packed.reshape(2, hr, PACK_W)).reshape(nchip, rows * PACK_W)
    per_chip = [_unpack_rows(gathered[k], shard_shapes) for k in range(nchip)]
    cat = lambda n, axis: jnp.concatenate([per_chip[k][n] for k in range(nchip)], axis=axis)
    w_in_f = _permute_in_cols(cat(0, 1))
    w_a_f, w_b_f, w_o_f, w_gu_f, w_d_f = cat(1, 0), cat(2, 1), cat(3, 0), cat(4, 1), cat(5, 0)

    res = _local_step(x, mod, positions, w_in_f, w_a_f, w_b_f, w_o_f, w_gu_f, w_d_f, sinks[0],
                      ln1_g, ln1_b, ln2_g, ln2_b, loss_target)

    gq = res["g_win"]
    g_in = jnp.concatenate([gq[0], gq[4], gq[5], gq[1], gq[2], gq[3], gq[6]], axis=1)
    split = lambda a, axis: jnp.split(a, nchip, axis=axis)
    pieces = [split(g_in, 1), split(res["g_wa"], 0), split(res["g_wb"], 1), split(res["g_wo"], 0),
              split(res["g_wgu"], 1), split(res["g_wd"], 0)]
    g_packed = jnp.stack([_pack_rows([p[k] for p in pieces]).astype(BF16) for k in range(nchip)])
    g_packed = jnp.transpose(g_packed.reshape(nchip, 2, hr, PACK_W), (1, 0, 2, 3))
    g_red = _reduce_scatter_grads(g_packed, ci).reshape(rows * PACK_W)
    g_w_in, g_w_a, g_w_b, g_w_o, g_w_gu, g_w_d = _unpack_rows(g_red, shard_shapes)

    small_rows = 24
    misc = jnp.zeros((1, D), F32).at[0, :A_Q_HEADS].set(res["dsink"]).at[0, A_Q_HEADS].set(jnp.sum(res["loss"]))
    small = jnp.concatenate([res["dmod"].reshape(NB * 6, D), jnp.sum(res["ln_grads"], axis=0), misc,
                             jnp.zeros((small_rows - NB * 6 - 5, D), F32)], axis=0)
    small_all = _allgather_small(small, name="gather_small").reshape(ndev, small_rows, D)
    dmod_all = small_all[:, :NB * 6].reshape(ndev * NB, 6 * D)
    sums = _sum_devices(small_all, name="sum_small")
    g_b_ada = (sums[0:6] + sums[6:12]).reshape(1, 6 * D)
    g_ln1_g, g_ln1_b, g_ln2_g, g_ln2_b = (sums[12 + n][None] for n in range(4))
    g_sinks = sums[16, :A_Q_HEADS][None]
    loss = sums[16, A_Q_HEADS]
    dmod_sh = lax.dynamic_slice(dmod_all, (0, chip * ada_cols), (ndev * NB, ada_cols))
    g_w_ada = _mm(c_act, dmod_sh, ta=True, name="ada_dw")

    names = ["w_ada", "b_ada", "w_in", "sinks", "w_branch_a", "w_branch_b", "w_o", "ln1_g", "ln1_b",
             "w_gate_up", "w_down", "ln2_g", "ln2_b"]
    ws = [w_ada, b_ada, w_in, sinks, w_branch_a, w_branch_b, w_o, ln1_g, ln1_b, w_gate_up, w_down, ln2_g, ln2_b]
    ms = [m_w_ada, m_b_ada, m_w_in, m_sinks, m_w_branch_a, m_w_branch_b, m_w_o, m_ln1_g, m_ln1_b, m_w_gate_up,
          m_w_down, m_ln2_g, m_ln2_b]
    vs = [v_w_ada, v_b_ada, v_w_in, v_sinks, v_w_branch_a, v_w_branch_b, v_w_o, v_ln1_g, v_ln1_b, v_w_gate_up,
          v_w_down, v_ln2_g, v_ln2_b]
    gs = [g_w_ada, g_b_ada, g_w_in, g_sinks, g_w_a, g_w_b, g_w_o, g_ln1_g, g_ln1_b, g_w_gu, g_w_d, g_ln2_g, g_ln2_b]
    grads, deltas, new_ms, new_vs = [], [], [], []
    for name, w, g, m, v in zip(names, ws, gs, ms, vs):
        shp = w.shape
        w2, m2, v2 = (a.reshape(shp[-2], shp[-1]) for a in (w, m, v))
        g2 = g.reshape(shp[-2], shp[-1])
        d, nm, nv = _adamw(w2, g2, m2, v2, name="adamw_" + name)
        grads.append(g2.reshape(shp))
        deltas.append(d.reshape(shp))
        new_ms.append(nm.reshape(shp))
        new_vs.append(nv.reshape(shp))
    return (loss, res["grad_x"], *grads, *deltas, *new_ms, *new_vs)
```

```python
import functools
import math

import jax
import jax.numpy as jnp
from jax import lax
from jax.experimental import pallas as pl
from jax.experimental.pallas import tpu as pltpu

F32 = jnp.float32
BF16 = jnp.bfloat16
MESH = pl.DeviceIdType.MESH

HEAD_DIM = 64
BLOCK = 128
A_Q_HEADS = 16
A_KV_HEADS = 2
A_WINDOW = 128
B_PATTERNS = ((128, 1), (512, 4), (2048, 16))
B_GROUP_HEADS = 8
QA_W = A_Q_HEADS * HEAD_DIM
KA_W = A_KV_HEADS * HEAD_DIM
GB_W = B_GROUP_HEADS * HEAD_DIM
QB_W = GB_W * len(B_PATTERNS)
QKV_W = QA_W + 2 * KA_W + 3 * QB_W
QKV_STRIDE = 6144
OFF_QA, OFF_QB, OFF_KB, OFF_VB = 0, QA_W, QA_W + QB_W, QA_W + 2 * QB_W
OFF_KA = QA_W + 3 * QB_W
OFF_VA = OFF_KA + KA_W
ROPE_THETA = 10000.0
LN_EPS = 1e-5
NEG_INF = -1e30
DEPTH = 1
ALPHA = (2 * DEPTH) ** 0.25
SCALE = HEAD_DIM ** -0.5

ADAM_LR, ADAM_B1, ADAM_B2, ADAM_EPS, ADAM_WD, ADAM_STEP = 0.001, 0.9, 0.999, 1e-08, 0.01, 10

VMEM_LIMIT_BYTES = 56 * 1024 * 1024


def _pcall(body, **kw):
    return pl.pallas_call(body, **kw)


def _params(sem=None):
    return pltpu.CompilerParams(dimension_semantics=sem, vmem_limit_bytes=VMEM_LIMIT_BYTES)


def _pick(n, target, quantum=128):
    t = (min(target, n) // quantum) * quantum
    while t >= quantum:
        if n % t == 0:
            return t
        t -= quantum
    return n


def _mm(a, b, *, name, ta=False, tb=False, out_dtype=F32, add=None, tm=1024, tn=1536, tk=1536):
    if ta:
        K, M = a.shape
    else:
        M, K = a.shape
    if tb:
        Nn, K2 = b.shape
    else:
        K2, Nn = b.shape
    assert K == K2, (a.shape, b.shape)
    tm, tn, tk = _pick(M, tm), _pick(Nn, tn), _pick(K, tk)
    nk = K // tk
    dn = (((0 if ta else 1,), (1 if tb else 0,)), ((), ()))

    def body(*refs):
        refs = list(refs)
        a_ref, b_ref = refs[:2]
        add_ref = refs[2] if add is not None else None
        o_ref = refs[3] if add is not None else refs[2]
        part = lax.dot_general(a_ref[...].astype(BF16), b_ref[...].astype(BF16), dn, preferred_element_type=F32)

        def finish(r):
            if add is not None:
                r = r + add_ref[...]
            o_ref[...] = r.astype(out_dtype)

        if nk == 1:
            finish(part)
            return
        acc = refs[-1]
        k = pl.program_id(2)

        @pl.when(k == 0)
        def _():
            acc[...] = part

        @pl.when(k > 0)
        def _():
            acc[...] += part

        @pl.when(k == nk - 1)
        def _():
            finish(acc[...])

    a_spec = pl.BlockSpec((tk, tm), lambda i, j, k: (k, i)) if ta else pl.BlockSpec((tm, tk), lambda i, j, k: (i, k))
    b_spec = pl.BlockSpec((tn, tk), lambda i, j, k: (j, k)) if tb else pl.BlockSpec((tk, tn), lambda i, j, k: (k, j))
    o_spec = pl.BlockSpec((tm, tn), lambda i, j, k: (i, j))
    ins, specs = [a, b], [a_spec, b_spec]
    if add is not None:
        ins.append(add)
        specs.append(o_spec)
    return _pcall(
        body, name=name, grid=(M // tm, Nn // tn, nk), in_specs=specs, out_specs=o_spec,
        out_shape=jax.ShapeDtypeStruct((M, Nn), out_dtype),
        scratch_shapes=[pltpu.VMEM((tm, tn), F32)] if nk > 1 else [],
        compiler_params=_params(("parallel", "parallel", "arbitrary")),
    )(*ins)


def _mm_multi(a_list, b_list, *, name, add=None, out_dtype=F32, tm=512):
    M = a_list[0].shape[0]
    tm = _pick(M, tm)
    ns = len(a_list)
    b_arrs, b_specs = [], []
    for b in b_list:
        arr, shp, idx = b if isinstance(b, tuple) else (b, b.shape, (0, 0))
        b_arrs.append(arr)
        b_specs.append(pl.BlockSpec(shp, lambda i, idx=idx: idx))
    Nn = b_specs[0].block_shape[0]
    dn = (((1,), (1,)), ((), ()))

    def body(*refs):
        a_refs, b_refs = refs[:ns], refs[ns:2 * ns]
        acc = None
        for a_ref, b_ref in zip(a_refs, b_refs):
            part = lax.dot_general(a_ref[...].astype(BF16), b_ref[...], dn, preferred_element_type=F32)
            acc = part if acc is None else acc + part
        if add is not None:
            acc = acc + refs[2 * ns][...]
        refs[-1][...] = acc.astype(out_dtype)

    o_spec = pl.BlockSpec((tm, Nn), lambda i: (i, 0))
    specs = [pl.BlockSpec((tm, a.shape[1]), lambda i: (i, 0)) for a in a_list] + b_specs
    ins = list(a_list) + b_arrs
    if add is not None:
        specs.append(o_spec)
        ins.append(add)
    return _pcall(body, name=name, grid=(M // tm,), in_specs=specs, out_specs=o_spec,
                  out_shape=jax.ShapeDtypeStruct((M, Nn), out_dtype), compiler_params=_params(("parallel",)))(*ins)


def _rot_half(v):
    w = v.shape[-1]
    lane = lax.broadcasted_iota(jnp.int32, v.shape, v.ndim - 1)
    first = (lane % HEAD_DIM) < (HEAD_DIM // 2)
    return jnp.where(first, pltpu.roll(v, w - HEAD_DIM // 2, v.ndim - 1), pltpu.roll(v, HEAD_DIM // 2, v.ndim - 1))


def _widen(t, w):
    return t if w == t.shape[-1] else jnp.concatenate([t] * (w // t.shape[-1]), axis=-1)


def _rope_tables(positions):
    half = HEAD_DIM // 2
    inv = ROPE_THETA ** (-jnp.arange(half, dtype=F32) / half)
    ang = positions.astype(F32)[..., None] * inv
    cos, sin = jnp.cos(ang), jnp.sin(ang)
    cosf = jnp.concatenate([cos, cos, cos, cos], axis=-1)
    sins = jnp.concatenate([-sin, sin, -sin, sin], axis=-1)
    n = positions.shape[0] * positions.shape[1]
    return cosf.reshape(n, 2 * HEAD_DIM), sins.reshape(n, 2 * HEAD_DIM)


def _inproj(x2, scale, shift, w, cosf, sins, flags, *, T, name):
    N, D = x2.shape
    tm, tn = _pick(T, 512), 1536
    tpe = T // tm

    def body(x_ref, sc_ref, sh_ref, w_ref, c_ref, s_ref, f_ref, o_ref, u_ref):
        @pl.when(pl.program_id(1) == 0)
        def _():
            u_ref[...] = (x_ref[...] * (1.0 + sc_ref[0]) + sh_ref[0]).astype(BF16)

        acc = jnp.dot(u_ref[...], w_ref[...], preferred_element_type=F32)
        fl = f_ref[...]
        ce = 1.0 + (_widen(c_ref[...], tn) - 1.0) * fl
        se = _widen(s_ref[...], tn) * fl
        o_ref[...] = (acc * ce + _rot_half(acc) * se).astype(BF16)

    ex = pl.BlockSpec((1, 1, D), lambda i, j: (i // tpe, 0, 0))
    tab = pl.BlockSpec((tm, 2 * HEAD_DIM), lambda i, j: (i, 0))
    return _pcall(
        body, name=name, grid=(N // tm, QKV_STRIDE // tn),
        in_specs=[pl.BlockSpec((tm, D), lambda i, j: (i, 0)), ex, ex,
                  pl.BlockSpec((D, tn), lambda i, j: (0, j)), tab, tab,
                  pl.BlockSpec((1, tn), lambda i, j: (0, j))],
        out_specs=[pl.BlockSpec((tm, tn), lambda i, j: (i, j)), pl.BlockSpec((tm, D), lambda i, j: (i, 0))],
        out_shape=[jax.ShapeDtypeStruct((N, QKV_STRIDE), BF16), jax.ShapeDtypeStruct((N, D), BF16)],
        compiler_params=_params(("parallel", "arbitrary")),
    )(x2, scale, shift, w, cosf, sins, flags)


class _Geom:
    def __init__(self, g):
        if g is None:
            self.r, self.nq, self.grp, self.n_back, self.sink = 1, A_Q_HEADS, A_Q_HEADS // A_KV_HEADS, A_WINDOW - 1, True
            self.qidx = lambda j: OFF_QA // QA_W
            self.kidx = lambda j: OFF_KA // KA_W
            self.vidx = lambda j: OFF_VA // KA_W
        else:
            window, r = B_PATTERNS[g]
            self.r, self.nq, self.grp, self.n_back, self.sink = r, B_GROUP_HEADS, 1, window // r, False
            per = QKV_STRIDE // GB_W
            self.qidx = lambda j: j * per + OFF_QB // GB_W + g
            self.kidx = lambda j: j * per + OFF_KB // GB_W + g
            self.vidx = lambda j: j * per + OFF_VB // GB_W + g
        self.qw = self.nq * HEAD_DIM
        self.kw = (self.nq // self.grp) * HEAD_DIM


def _scores(q, kp, kc, mask_p, mask_c, sink):
    dn = (((1,), (1,)), ((), ()))
    sp = lax.dot_general(q, kp, dn, preferred_element_type=F32) * SCALE
    sc = lax.dot_general(q, kc, dn, preferred_element_type=F32) * SCALE
    sp = jnp.where(mask_p, sp, NEG_INF)
    sc = jnp.where(mask_c, sc, NEG_INF)
    m = jnp.maximum(jnp.max(sp, axis=1, keepdims=True), jnp.max(sc, axis=1, keepdims=True))
    if sink is not None:
        m = jnp.maximum(m, sink)
    pp = jnp.exp(sp - m)
    pc = jnp.exp(sc - m)
    den = jnp.sum(pp, axis=1, keepdims=True) + jnp.sum(pc, axis=1, keepdims=True)
    es = None
    if sink is not None:
        es = jnp.exp(sink - m)
        den = den + es
    return pp, pc, m, den, es


def _masks(i, n_back):
    qi = lax.broadcasted_iota(jnp.int32, (BLOCK, BLOCK), 0)
    ki = lax.broadcasted_iota(jnp.int32, (BLOCK, BLOCK), 1)
    mask_c = qi >= ki
    mask_p = jnp.logical_and(qi + BLOCK - ki <= n_back, i > 0)
    return mask_p, mask_c


def _attn_fwd(qkv, sinks, g, *, NB, T, name):
    geo = _Geom(g)
    r, nq, qw, kw = geo.r, geo.nq, geo.qw, geo.kw
    tsub = T // r
    nblk = tsub // BLOCK
    qkv3 = qkv.reshape(NB, tsub, r * QKV_STRIDE)
    with_lse = g is not None
    out_dtype = F32 if with_lse else BF16

    def body(q_ref, kp_ref, kc_ref, vp_ref, vc_ref, sink_ref, *rest):
        if with_lse:
            o_ref, l_ref, o_s, l_s = rest
        else:
            o_ref, o_s = rest
        mask_p, mask_c = _masks(pl.program_id(2), geo.n_back)
        for h in range(nq):
            kh = h // geo.grp
            qs, ks = slice(h * HEAD_DIM, (h + 1) * HEAD_DIM), slice(kh * HEAD_DIM, (kh + 1) * HEAD_DIM)
            sink = sink_ref[h] if geo.sink else None
            pp, pc, m, den, _ = _scores(q_ref[0, :, qs], kp_ref[0, :, ks], kc_ref[0, :, ks], mask_p, mask_c, sink)
            o = jnp.dot(pp.astype(BF16), vp_ref[0, :, ks], preferred_element_type=F32)
            o = o + jnp.dot(pc.astype(BF16), vc_ref[0, :, ks], preferred_element_type=F32)
            o_s[:, qs] = o / den
            if with_lse:
                l_s[:, qs] = jnp.broadcast_to(m + jnp.log(den), (BLOCK, HEAD_DIM))
        o_ref[0] = o_s[...].astype(out_dtype)
        if with_lse:
            l_ref[0] = l_s[...]

    prev = lambda i: jnp.maximum(i - 1, 0)
    in_specs = [
        pl.BlockSpec((1, BLOCK, qw), lambda b, j, i: (b, i, geo.qidx(j))),
        pl.BlockSpec((1, BLOCK, kw), lambda b, j, i: (b, prev(i), geo.kidx(j))),
        pl.BlockSpec((1, BLOCK, kw), lambda b, j, i: (b, i, geo.kidx(j))),
        pl.BlockSpec((1, BLOCK, kw), lambda b, j, i: (b, prev(i), geo.vidx(j))),
        pl.BlockSpec((1, BLOCK, kw), lambda b, j, i: (b, i, geo.vidx(j))),
        pl.BlockSpec(memory_space=pltpu.SMEM),
    ]
    o_spec = pl.BlockSpec((1, BLOCK, qw), lambda b, j, i: (b, i, j))
    o_shape = jax.ShapeDtypeStruct((NB, tsub, r * qw), out_dtype)
    scratch = [pltpu.VMEM((BLOCK, qw), F32)]
    if with_lse:
        out_specs, out_shape = [o_spec, o_spec], [o_shape, o_shape]
        scratch.append(pltpu.VMEM((BLOCK, qw), F32))
    else:
        out_specs, out_shape = o_spec, o_shape
    res = _pcall(
        body, name=name, grid=(NB, r, nblk), in_specs=in_specs, out_specs=out_specs, out_shape=out_shape,
        scratch_shapes=scratch, compiler_params=_params(("parallel", "parallel", "arbitrary")),
    )(qkv3, qkv3, qkv3, qkv3, qkv3, sinks)
    if with_lse:
        return res[0].reshape(NB * T, qw), res[1].reshape(NB * T, qw)
    return res.reshape(NB * T, qw)


def _attn_bwd(qkv, do, dlse, cosf, sins, sinks, g, prev_out, *, NB, T, name):
    geo = _Geom(g)
    r, nq, qw, kw, grp = geo.r, geo.nq, geo.qw, geo.kw, geo.grp
    tsub = T // r
    nblk = tsub // BLOCK
    qkv3 = qkv.reshape(NB, tsub, r * QKV_STRIDE)
    do3 = do.reshape(NB, tsub, r * qw)
    cos3 = cosf.reshape(NB, tsub, r * 2 * HEAD_DIM)
    sin3 = sins.reshape(NB, tsub, r * 2 * HEAD_DIM)
    has_dlse = dlse is not None
    ngrp = len(B_PATTERNS)
    ow_q = qw if g is None else QB_W
    ow_k = kw if g is None else QB_W
    cq = (lambda j: j) if g is None else (lambda j: j * ngrp + g)

    def body(*refs):
        it = iter(refs)
        q_ref, kp_ref, kc_ref, vp_ref, vc_ref, do_ref = (next(it) for _ in range(6))
        dl_ref = next(it) if has_dlse else None
        cq_ref, sq_ref, ck_ref, sk_ref, sink_ref = (next(it) for _ in range(5))
        if prev_out is not None:
            for _ in range(3):
                next(it)
        dq_ref, dk_ref, dv_ref, ds_ref = (next(it) for _ in range(4))
        dq_s, akp, akc, avp, avc, car_k, car_v = (next(it) for _ in range(7))
        b, j, i = pl.program_id(0), pl.program_id(1), pl.program_id(2)

        @pl.when(jnp.logical_and(b == 0, jnp.logical_and(j == 0, i == 0)))
        def _():
            ds_ref[...] = jnp.zeros_like(ds_ref)

        @pl.when(i == 0)
        def _():
            car_k[...] = jnp.zeros_like(car_k)
            car_v[...] = jnp.zeros_like(car_v)

        @pl.when(i == nblk)
        def _():
            akp[...] = jnp.zeros_like(akp)
            akc[...] = jnp.zeros_like(akc)
            avp[...] = jnp.zeros_like(avp)
            avc[...] = jnp.zeros_like(avc)

        @pl.when(i < nblk)
        def _():
            mask_p, mask_c = _masks(i, geo.n_back)
            dnt = (((1,), (1,)), ((), ()))
            dn0 = (((0,), (0,)), ((), ()))
            lane = lax.broadcasted_iota(jnp.int32, (1, 128), 1)
            dsink = jnp.zeros((1, 128), F32)
            for h in range(nq):
                kh = h // grp
                qs, ks = slice(h * HEAD_DIM, (h + 1) * HEAD_DIM), slice(kh * HEAD_DIM, (kh + 1) * HEAD_DIM)
                q, kp, kc = q_ref[0, :, qs], kp_ref[0, :, ks], kc_ref[0, :, ks]
                vp, vc, dout = vp_ref[0, :, ks], vc_ref[0, :, ks], do_ref[0, :, qs]
                sink = sink_ref[h] if geo.sink else None
                pp, pc, m, den, es = _scores(q, kp, kc, mask_p, mask_c, sink)
                inv = 1.0 / den
                pp, pc = pp * inv, pc * inv
                dpp = lax.dot_general(dout, vp, dnt, preferred_element_type=F32)
                dpc = lax.dot_general(dout, vc, dnt, preferred_element_type=F32)
                delta = jnp.sum(pp * dpp, axis=1, keepdims=True) + jnp.sum(pc * dpc, axis=1, keepdims=True)
                if geo.sink:
                    dsink = dsink + jnp.where(lane == h, -jnp.sum(es * inv * delta), 0.0)
                if has_dlse:
                    delta = delta - dl_ref[0, :, h * HEAD_DIM:h * HEAD_DIM + 1]
                dsp = (pp * (dpp - delta) * SCALE).astype(BF16)
                dsc = (pc * (dpc - delta) * SCALE).astype(BF16)
                dq_s[:, qs] = (jnp.dot(dsp, kp, preferred_element_type=F32)
                               + jnp.dot(dsc, kc, preferred_element_type=F32))
                c_kp = lax.dot_general(dsp, q, dn0, preferred_element_type=F32)
                c_kc = lax.dot_general(dsc, q, dn0, preferred_element_type=F32)
                c_vp = lax.dot_general(pp.astype(BF16), dout, dn0, preferred_element_type=F32)
                c_vc = lax.dot_general(pc.astype(BF16), dout, dn0, preferred_element_type=F32)
                if h % grp == 0:
                    akp[:, ks], akc[:, ks], avp[:, ks], avc[:, ks] = c_kp, c_kc, c_vp, c_vc
                else:
                    akp[:, ks] += c_kp
                    akc[:, ks] += c_kc
                    avp[:, ks] += c_vp
                    avc[:, ks] += c_vc
            dq = dq_s[...]
            dq_ref[0] = (dq * _widen(cq_ref[0], qw) - _rot_half(dq) * _widen(sq_ref[0], qw)).astype(BF16)
            if geo.sink:
                ds_ref[0:1, :] += dsink

        dk = car_k[...] + akp[...]
        dk_ref[0] = (dk * _widen(ck_ref[0], kw) - _rot_half(dk) * _widen(sk_ref[0], kw)).astype(BF16)
        dv_ref[0] = (car_v[...] + avp[...]).astype(BF16)
        car_k[...] = akc[...]
        car_v[...] = avc[...]

    cur = lambda i: jnp.minimum(i, nblk - 1)
    prv = lambda i: jnp.maximum(jnp.minimum(i, nblk - 1) - 1, 0)
    outb = lambda i: jnp.maximum(i - 1, 0)
    tabw = 2 * HEAD_DIM
    in_specs = [
        pl.BlockSpec((1, BLOCK, qw), lambda b, j, i: (b, cur(i), geo.qidx(j))),
        pl.BlockSpec((1, BLOCK, kw), lambda b, j, i: (b, prv(i), geo.kidx(j))),
        pl.BlockSpec((1, BLOCK, kw), lambda b, j, i: (b, cur(i), geo.kidx(j))),
        pl.BlockSpec((1, BLOCK, kw), lambda b, j, i: (b, prv(i), geo.vidx(j))),
        pl.BlockSpec((1, BLOCK, kw), lambda b, j, i: (b, cur(i), geo.vidx(j))),
        pl.BlockSpec((1, BLOCK, qw), lambda b, j, i: (b, cur(i), j)),
    ]
    ins = [qkv3, qkv3, qkv3, qkv3, qkv3, do3]
    if has_dlse:
        in_specs.append(pl.BlockSpec((1, BLOCK, qw), lambda b, j, i: (b, cur(i), j)))
        ins.append(dlse.reshape(NB, tsub, r * qw))
    in_specs += [
        pl.BlockSpec((1, BLOCK, tabw), lambda b, j, i: (b, cur(i), j)),
        pl.BlockSpec((1, BLOCK, tabw), lambda b, j, i: (b, cur(i), j)),
        pl.BlockSpec((1, BLOCK, tabw), lambda b, j, i: (b, outb(i), j)),
        pl.BlockSpec((1, BLOCK, tabw), lambda b, j, i: (b, outb(i), j)),
        pl.BlockSpec(memory_space=pltpu.SMEM),
    ]
    ins += [cos3, sin3, cos3, sin3, sinks]
    aliases = {}
    if prev_out is not None:
        base = len(ins)
        for n, arr in enumerate(prev_out):
            in_specs.append(pl.BlockSpec(memory_space=pl.ANY))
            ins.append(arr.reshape(NB, tsub, r * ow_k if n else r * ow_q))
            aliases[base + n] = n
    out_specs = [
        pl.BlockSpec((1, BLOCK, qw), lambda b, j, i: (b, cur(i), cq(j))),
        pl.BlockSpec((1, BLOCK, kw), lambda b, j, i: (b, outb(i), cq(j))),
        pl.BlockSpec((1, BLOCK, kw), lambda b, j, i: (b, outb(i), cq(j))),
        pl.BlockSpec((8, 128), lambda b, j, i: (0, 0)),
    ]
    out_shape = [
        jax.ShapeDtypeStruct((NB, tsub, r * ow_q), BF16),
        jax.ShapeDtypeStruct((NB, tsub, r * ow_k), BF16),
        jax.ShapeDtypeStruct((NB, tsub, r * ow_k), BF16),
        jax.ShapeDtypeStruct((8, 128), F32),
    ]
    scratch = [pltpu.VMEM((BLOCK, qw), F32)] + [pltpu.VMEM((BLOCK, kw), F32) for _ in range(6)]
    dq, dk, dv, dsink = _pcall(
        body, name=name, grid=(NB, r, nblk + 1), in_specs=in_specs, out_specs=out_specs, out_shape=out_shape,
        scratch_shapes=scratch, input_output_aliases=aliases,
        compiler_params=_params(("arbitrary", "arbitrary", "arbitrary")),
    )(*ins)
    return dq.reshape(NB * T, ow_q), dk.reshape(NB * T, ow_k), dv.reshape(NB * T, ow_k), dsink


class _Rows:
    def __init__(self, N, T, tm):
        self.N, self.tm, self.tpe, self.grid = N, tm, T // tm, (N // tm,)

    def row(self, w, col=0):
        return pl.BlockSpec((self.tm, w), lambda i: (i, col))

    def ex(self, w):
        return pl.BlockSpec((1, 1, w), lambda i: (i // self.tpe, 0, 0))

    def const(self, shape):
        return pl.BlockSpec(shape, lambda i: tuple(0 for _ in shape))

    def first_of_example(self):
        return pl.program_id(0) % self.tpe == 0


def _acc(ref, first, val):
    @pl.when(first)
    def _():
        ref[0] = val

    @pl.when(jnp.logical_not(first))
    def _():
        ref[0] += val


def _colsum(v):
    return jnp.sum(v, axis=0, keepdims=True)


def _ln_stats(r):
    mu = jnp.mean(r, axis=-1, keepdims=True)
    xc = r - mu
    var = jnp.mean(xc * xc, axis=-1, keepdims=True)
    rstd = lax.rsqrt(var + LN_EPS)
    return xc * rstd, rstd


def _ln_bwd(dy, xhat, rstd, gain):
    dxh = dy * gain
    return rstd * (dxh - jnp.mean(dxh, axis=-1, keepdims=True) - xhat * jnp.mean(dxh * xhat, axis=-1, keepdims=True))


def _silu_parts(v):
    s = jax.nn.sigmoid(v)
    return v * s, s * (1.0 + v * (1.0 - s))


def _local_step(x, mod, positions, w_in, w_a, w_b, w_o, w_gu, w_d, sinks, ln1_g, ln1_b, ln2_g, ln2_b, target):
    NB, T, D = x.shape
    N = NB * T
    F = w_d.shape[0]
    x2 = x.reshape(N, D)
    tgt2 = target.reshape(N, D)
    shift_m, scale_m, gate_m, shift_f, scale_f, gate_f = [mod[:, None, k * D:(k + 1) * D] for k in range(6)]
    cosf, sins = _rope_tables(positions)
    col = jnp.arange(QKV_STRIDE)
    flags = jnp.logical_or(col < OFF_VB, jnp.logical_and(col >= OFF_KA, col < OFF_VA)).astype(F32)[None]
    R = _Rows(N, T, _pick(T, 256))
    sds = jax.ShapeDtypeStruct
    exsum = lambda w=D: sds((NB, 1, w), F32)

    qkv, u = _inproj(x2, scale_m, shift_m, w_in, cosf, sins, flags, T=T, name="inproj_qkv")
    gates = _mm(u, w_in[:, QKV_W:], name="inproj_gates")
    oa = _attn_fwd(qkv, sinks, None, NB=NB, T=T, name="attn_a_fwd")
    ob_parts = [_attn_fwd(qkv, sinks, g, NB=NB, T=T, name=f"attn_b{g}_fwd") for g in range(len(B_PATTERNS))]
    (o1, l1), (o2, l2), (o3, l3) = ob_parts

    def merge_fwd(o1r, o2r, o3r, l1r, l2r, l3r, ob_ref):
        la, lb, lc = l1r[...], l2r[...], l3r[...]
        mx = jnp.maximum(jnp.maximum(la, lb), lc)
        ea, eb, ec = jnp.exp(la - mx), jnp.exp(lb - mx), jnp.exp(lc - mx)
        ob_ref[...] = ((ea * o1r[...] + eb * o2r[...] + ec * o3r[...]) / (ea + eb + ec)).astype(BF16)

    ob = _pcall(merge_fwd, name="merge_fwd", grid=R.grid, in_specs=[R.row(GB_W)] * 6, out_specs=R.row(GB_W),
                out_shape=sds((N, GB_W), BF16), compiler_params=_params(("parallel",)))(o1, o2, o3, l1, l2, l3)

    ya = _mm(oa, w_a, name="branch_a")
    yb = _mm(ob, w_b, name="branch_b")

    def gate_fwd(ya_r, yb_r, ga_r, gb_r, mg_ref):
        mg_ref[...] = (jax.nn.sigmoid(ga_r[...]) * ya_r[...] + jax.nn.sigmoid(gb_r[...]) * yb_r[...]).astype(BF16)

    merged = _pcall(gate_fwd, name="gate_fwd", grid=R.grid, in_specs=[R.row(D), R.row(D), R.row(D, 0), R.row(D, 1)],
                    out_specs=R.row(D), out_shape=sds((N, D), BF16),
                    compiler_params=_params(("parallel",)))(ya, yb, gates, gates)
    y = _mm(merged, w_o, name="out_proj")

    def norm1_fwd(x_r, y_r, gm_r, g_r, b_r, sf_r, hf_r, r1_ref, x1_ref, u2_ref):
        r1 = ALPHA * x_r[...] + (1.0 + gm_r[0]) * y_r[...]
        xhat, _ = _ln_stats(r1)
        x1 = xhat * g_r[...] + b_r[...]
        r1_ref[...] = r1
        x1_ref[...] = x1
        u2_ref[...] = (x1 * (1.0 + sf_r[0]) + hf_r[0]).astype(BF16)

    r1, x1, u2 = _pcall(
        norm1_fwd, name="norm1_fwd", grid=R.grid,
        in_specs=[R.row(D), R.row(D), R.ex(D), R.const((1, D)), R.const((1, D)), R.ex(D), R.ex(D)],
        out_specs=[R.row(D)] * 3, out_shape=[sds((N, D), F32), sds((N, D), F32), sds((N, D), BF16)],
        compiler_params=_params(("parallel",)))(x2, y, gate_m, ln1_g, ln1_b, scale_f, shift_f)

    tmf, tnf = _pick(N, 512), _pick(F, 1408)
    nft = F // tnf

    def ffn_up(u_r, wg_r, wu_r, hg_ref, hu_ref, a_ref):
        hg = jnp.dot(u_r[...], wg_r[...], preferred_element_type=F32)
        hu = jnp.dot(u_r[...], wu_r[...], preferred_element_type=F32)
        sl, _ = _silu_parts(hg)
        hg_ref[...] = hg.astype(BF16)
        hu_ref[...] = hu.astype(BF16)
        a_ref[...] = (sl * hu).astype(BF16)

    ftile = pl.BlockSpec((tmf, tnf), lambda i, j: (i, j))
    hg, hu, act = _pcall(
        ffn_up, name="ffn_up", grid=(N // tmf, nft),
        in_specs=[pl.BlockSpec((tmf, D), lambda i, j: (i, 0)), pl.BlockSpec((D, tnf), lambda i, j: (0, j)),
                  pl.BlockSpec((D, tnf), lambda i, j: (0, j + nft))],
        out_specs=[ftile] * 3, out_shape=[sds((N, F), BF16)] * 3,
        compiler_params=_params(("parallel", "arbitrary")))(u2, w_gu, w_gu)
    y2 = _mm(act, w_d, name="ffn_down")

    def norm2_loss_bwd(x1_r, y2_r, t_r, gf_r, g_r, b_r, dy2_ref, dx1_ref, dgf_ref, dg_ref, db_ref, loss_ref):
        first = R.first_of_example()
        y2v = y2_r[...]
        r2 = ALPHA * x1_r[...] + (1.0 + gf_r[0]) * y2v
        xhat, rstd = _ln_stats(r2)
        err = xhat * g_r[...] + b_r[...] - t_r[...]
        dx2 = err * (1.0 / D)
        dr2 = _ln_bwd(dx2, xhat, rstd, g_r[...])
        dy2_ref[...] = ((1.0 + gf_r[0]) * dr2).astype(BF16)
        dx1_ref[...] = ALPHA * dr2
        _acc(dgf_ref, first, _colsum(dr2 * y2v))
        _acc(dg_ref, first, _colsum(dx2 * xhat))
        _acc(db_ref, first, _colsum(dx2))
        part = 0.5 * jnp.sum(jnp.mean(err * err, axis=-1, keepdims=True))
        _acc(loss_ref, first, jnp.broadcast_to(part, (1, 128)))

    dy2, dx1p, dgate_f, dg2, db2, loss_p = _pcall(
        norm2_loss_bwd, name="norm2_loss_bwd", grid=R.grid,
        in_specs=[R.row(D), R.row(D), R.row(D), R.ex(D), R.const((1, D)), R.const((1, D))],
        out_specs=[R.row(D), R.row(D), R.ex(D), R.ex(D), R.ex(D), R.ex(128)],
        out_shape=[sds((N, D), BF16), sds((N, D), F32), exsum(), exsum(), exsum(), exsum(128)],
        compiler_params=_params(("arbitrary",)))(x1, y2, tgt2, gate_f, ln2_g, ln2_b)

    g_wd = _mm(act, dy2, ta=True, out_dtype=BF16, name="ffn_down_dw")

    def ffn_down_dx(dy_r, wd_r, hg_r, hu_r, dhg_ref, dhu_ref):
        da = lax.dot_general(dy_r[...], wd_r[...], (((1,), (1,)), ((), ())), preferred_element_type=F32)
        sl, dsl = _silu_parts(hg_r[...].astype(F32))
        dhg_ref[...] = (da * hu_r[...].astype(F32) * dsl).astype(BF16)
        dhu_ref[...] = (da * sl).astype(BF16)

    dhg, dhu = _pcall(
        ffn_down_dx, name="ffn_down_dx", grid=(N // tmf, nft),
        in_specs=[pl.BlockSpec((tmf, D), lambda i, j: (i, 0)), pl.BlockSpec((tnf, D), lambda i, j: (j, 0)), ftile, ftile],
        out_specs=[ftile] * 2, out_shape=[sds((N, F), BF16)] * 2,
        compiler_params=_params(("parallel", "arbitrary")))(dy2, w_d, hg, hu)
    du2 = _mm_multi([dhg, dhu], [(w_gu, (D, F), (0, 0)), (w_gu, (D, F), (0, 1))], name="ffn_up_dx")
    g_wgu = jnp.concatenate([_mm(u2, dhg, ta=True, out_dtype=BF16, name="ffn_gate_dw"),
                             _mm(u2, dhu, ta=True, out_dtype=BF16, name="ffn_up_dw")], axis=1)

    def norm1_bwd(dx1p_r, du2_r, x1_r, r1_r, y_r, sf_r, gm_r, g_r,
                  dxp_ref, dy_ref, dsf_ref, dhf_ref, dgm_ref, dg_ref, db_ref):
        first = R.first_of_example()
        du2v = du2_r[...]
        dx1 = dx1p_r[...] + du2v * (1.0 + sf_r[0])
        xhat, rstd = _ln_stats(r1_r[...])
        dr1 = _ln_bwd(dx1, xhat, rstd, g_r[...])
        dxp_ref[...] = ALPHA * dr1
        dy_ref[...] = ((1.0 + gm_r[0]) * dr1).astype(BF16)
        _acc(dsf_ref, first, _colsum(du2v * x1_r[...]))
        _acc(dhf_ref, first, _colsum(du2v))
        _acc(dgm_ref, first, _colsum(dr1 * y_r[...]))
        _acc(dg_ref, first, _colsum(dx1 * xhat))
        _acc(db_ref, first, _colsum(dx1))

    dxp, dy, dscale_f, dshift_f, dgate_m, dg1, db1 = _pcall(
        norm1_bwd, name="norm1_bwd", grid=R.grid,
        in_specs=[R.row(D)] * 5 + [R.ex(D), R.ex(D), R.const((1, D))],
        out_specs=[R.row(D), R.row(D)] + [R.ex(D)] * 5,
        out_shape=[sds((N, D), F32), sds((N, D), BF16)] + [exsum()] * 5,
        compiler_params=_params(("arbitrary",)))(dx1p, du2, x1, r1, y, scale_f, gate_m, ln1_g)

    dmerged = _mm(dy, w_o, tb=True, name="out_proj_dx")
    g_wo = _mm(merged, dy, ta=True, out_dtype=BF16, name="out_proj_dw")

    def gate_bwd(dm_r, ya_r, yb_r, ga_r, gb_r, dya_ref, dyb_ref, dg_ref):
        dm = dm_r[...]
        sa, sb = jax.nn.sigmoid(ga_r[...]), jax.nn.sigmoid(gb_r[...])
        dya_ref[...] = (dm * sa).astype(BF16)
        dyb_ref[...] = (dm * sb).astype(BF16)
        dg_ref[:, :D] = (dm * ya_r[...] * sa * (1.0 - sa)).astype(BF16)
        dg_ref[:, D:] = (dm * yb_r[...] * sb * (1.0 - sb)).astype(BF16)

    dya, dyb, dgates = _pcall(
        gate_bwd, name="gate_bwd", grid=R.grid, in_specs=[R.row(D)] * 3 + [R.row(D, 0), R.row(D, 1)],
        out_specs=[R.row(D), R.row(D), R.row(2 * D)],
        out_shape=[sds((N, D), BF16), sds((N, D), BF16), sds((N, 2 * D), BF16)],
        compiler_params=_params(("parallel",)))(dmerged, ya, yb, gates, gates)

    doa = _mm(dya, w_a, tb=True, out_dtype=BF16, name="branch_a_dx")
    g_wa = _mm(oa, dya, ta=True, out_dtype=BF16, name="branch_a_dw")
    dob = _mm(dyb, w_b, tb=True, name="branch_b_dx")
    g_wb = _mm(ob, dyb, ta=True, out_dtype=BF16, name="branch_b_dw")

    seg = (jnp.arange(GB_W)[:, None] // HEAD_DIM == jnp.arange(GB_W)[None, :] // HEAD_DIM).astype(BF16)

    def merge_bwd(dob_r, o1r, o2r, o3r, l1r, l2r, l3r, seg_r, d1, d2, d3, e1, e2, e3):
        dob_v = dob_r[...]
        la, lb, lc = l1r[...], l2r[...], l3r[...]
        mx = jnp.maximum(jnp.maximum(la, lb), lc)
        ea, eb, ec = jnp.exp(la - mx), jnp.exp(lb - mx), jnp.exp(lc - mx)
        inv = 1.0 / (ea + eb + ec)
        ws = [ea * inv, eb * inv, ec * inv]

        def headsum(v):
            hi = v.astype(BF16)
            r1_ = v - hi.astype(F32)
            mid = r1_.astype(BF16)
            lo = (r1_ - mid.astype(F32)).astype(BF16)
            sm = seg_r[...]
            return (jnp.dot(hi, sm, preferred_element_type=F32) + jnp.dot(mid, sm, preferred_element_type=F32)
                    + jnp.dot(lo, sm, preferred_element_type=F32))

        dws = [headsum(dob_v * o[...]) for o in (o1r, o2r, o3r)]
        mean = ws[0] * dws[0] + ws[1] * dws[1] + ws[2] * dws[2]
        for w_, dw_, d_ref, e_ref in zip(ws, dws, (d1, d2, d3), (e1, e2, e3)):
            d_ref[...] = (w_ * dob_v).astype(BF16)
            e_ref[...] = w_ * (dw_ - mean)

    mb = _pcall(
        merge_bwd, name="merge_bwd", grid=R.grid, in_specs=[R.row(GB_W)] * 7 + [R.const((GB_W, GB_W))],
        out_specs=[R.row(GB_W)] * 6, out_shape=[sds((N, GB_W), BF16)] * 3 + [sds((N, GB_W), F32)] * 3,
        compiler_params=_params(("parallel",)))(dob, o1, o2, o3, l1, l2, l3, seg)
    do_b, dlse_b = mb[:3], mb[3:]

    dqa, dka, dva, dsink = _attn_bwd(qkv, doa, None, cosf, sins, sinks, None, None, NB=NB, T=T, name="attn_a_bwd")
    prev = None
    for g in range(len(B_PATTERNS)):
        dqb, dkb, dvb, _ = _attn_bwd(qkv, do_b[g], dlse_b[g], cosf, sins, sinks, g, prev, NB=NB, T=T,
                                     name=f"attn_b{g}_bwd")
        prev = (dqb, dkb, dvb)

    segs = [(dqa, OFF_QA, QA_W), (dqb, OFF_QB, QB_W), (dkb, OFF_KB, QB_W), (dvb, OFF_VB, QB_W),
            (dka, OFF_KA, KA_W), (dva, OFF_VA, KA_W), (dgates, QKV_W, 2 * D)]
    wseg = lambda ss: [w_in[:, off:off + wid] for _, off, wid in ss]
    du = _mm_multi([s_[0] for s_ in segs[:3]], wseg(segs[:3]), name="inproj_dx0")
    du = _mm_multi([s_[0] for s_ in segs[3:]], wseg(segs[3:]), add=du, name="inproj_dx1")
    g_win = [_mm(u, dseg, ta=True, out_dtype=BF16, name=f"inproj_dw{n}") for n, (dseg, _, _) in enumerate(segs)]

    def x_bwd(dxp_r, du_r, x_r, sm_r, gx_ref, dsm_ref, dhm_ref):
        first = R.first_of_example()
        duv = du_r[...]
        gx_ref[...] = dxp_r[...] + duv * (1.0 + sm_r[0])
        _acc(dsm_ref, first, _colsum(duv * x_r[...]))
        _acc(dhm_ref, first, _colsum(duv))

    gx, dscale_m, dshift_m = _pcall(
        x_bwd, name="x_bwd", grid=R.grid, in_specs=[R.row(D)] * 3 + [R.ex(D)],
        out_specs=[R.row(D), R.ex(D), R.ex(D)], out_shape=[sds((N, D), F32), exsum(), exsum()],
        compiler_params=_params(("arbitrary",)))(dxp, du, x2, scale_m)

    dmod = jnp.concatenate([dshift_m, dscale_m, dgate_m, dshift_f, dscale_f, dgate_f], axis=-1)[:, 0]
    ln_grads = jnp.concatenate([dg1, db1, dg2, db2], axis=1)
    return dict(loss=loss_p[:, 0, 0], grad_x=gx.reshape(NB, T, D), g_win=g_win, g_wa=g_wa, g_wb=g_wb, g_wo=g_wo,
                g_wgu=g_wgu, g_wd=g_wd, dmod=dmod, ln_grads=ln_grads, dsink=dsink[0, :A_Q_HEADS])


def _coords():
    return lax.axis_index("x"), lax.axis_index("y"), lax.axis_index("c")


def _allgather_small(blk, *, name):
    m_per, n = blk.shape

    def body(x_ref, out_ref, send_sems, recv_sems, local_sem):
        x, y, c = _coords()
        me, sibling = (x, y, c), (x, y, 1 - c)
        chips = [(1 - x, y), (x, 1 - y), (1 - x, 1 - y)]

        def rows(px, py, pc):
            return out_ref.at[pl.ds((4 * px + 2 * py + pc) * m_per, m_per), :]

        def copy(k, block, to, src=None):
            return pltpu.make_async_remote_copy(
                src_ref=rows(*block) if src is None else src, dst_ref=rows(*block),
                send_sem=send_sems.at[k], recv_sem=recv_sems.at[k], device_id=to, device_id_type=MESH)

        mine = pltpu.make_async_copy(x_ref, rows(*me), local_sem)
        mine.start()
        first = [copy(0, me, sibling, src=x_ref)]
        first += [copy(1 + j, me, (*chip, c), src=x_ref) for j, chip in enumerate(chips)]
        for cp in first:
            cp.start()
        passed = [copy(4 + j, (*chip, c), sibling) for j, chip in enumerate(chips)]
        for j, chip in enumerate(chips):
            copy(1 + j, (*chip, c), me).wait_recv()
            passed[j].start()
        copy(0, sibling, me).wait_recv()
        for j, chip in enumerate(chips):
            copy(4 + j, (*chip, 1 - c), me).wait_recv()
        for cp in first + passed:
            cp.wait_send()
        mine.wait()

    return _pcall(
        body, name=name, out_shape=jax.ShapeDtypeStruct((8 * m_per, n), blk.dtype),
        in_specs=[pl.BlockSpec(memory_space=pltpu.VMEM)], out_specs=pl.BlockSpec(memory_space=pltpu.VMEM),
        scratch_shapes=[pltpu.SemaphoreType.DMA((7,)), pltpu.SemaphoreType.DMA((7,)), pltpu.SemaphoreType.DMA],
        compiler_params=pltpu.CompilerParams(vmem_limit_bytes=VMEM_LIMIT_BYTES),
    )(blk)


def _exchange(src, dst_shape, dst_dtype, plan, *, name, dst_init=None):
    nloc, nrem = (len(p) for p in plan(0, 0, 0))

    def body(*refs):
        refs = list(refs)
        src_ref = refs.pop(0) if src is not None else None
        if dst_init is not None:
            refs.pop(0)
        dst_ref, send_sems, recv_sems, local_sems = refs
        if src_ref is None:
            src_ref = dst_ref
        x, y, c = _coords()
        local, remote = plan(x, y, c)
        at = lambda ref, idx: ref.at[idx] if idx else ref
        lcs = [pltpu.make_async_copy(at(src_ref, si), at(dst_ref, di), local_sems.at[n])
               for n, (si, di) in enumerate(local)]
        for cp in lcs:
            cp.start()
        sends = [pltpu.make_async_remote_copy(src_ref=at(src_ref, si), dst_ref=at(dst_ref, di),
                                              send_sem=send_sems.at[n], recv_sem=recv_sems.at[n],
                                              device_id=peer, device_id_type=MESH)
                 for n, (si, di, ri, peer) in enumerate(remote)]
        for cp in sends:
            cp.start()
        for n, (si, di, ri, peer) in enumerate(remote):
            pltpu.make_async_remote_copy(src_ref=at(src_ref, si), dst_ref=at(dst_ref, ri),
                                         send_sem=send_sems.at[n], recv_sem=recv_sems.at[n],
                                         device_id=peer, device_id_type=MESH).wait_recv()
        for cp in sends:
            cp.wait_send()
        for cp in lcs:
            cp.wait()

    hbm = pl.BlockSpec(memory_space=pl.ANY)
    ins = ([src] if src is not None else []) + ([dst_init] if dst_init is not None else [])
    aliases = {len(ins) - 1: 0} if dst_init is not None else {}
    return _pcall(
        body, name=name, out_shape=jax.ShapeDtypeStruct(dst_shape, dst_dtype),
        in_specs=[hbm] * len(ins), out_specs=hbm, input_output_aliases=aliases,
        scratch_shapes=[pltpu.SemaphoreType.DMA((max(nrem, 1),)), pltpu.SemaphoreType.DMA((max(nrem, 1),)),
                        pltpu.SemaphoreType.DMA((max(nloc, 1),))],
        compiler_params=pltpu.CompilerParams(has_side_effects=True),
    )(*ins)


def _other_chips(x, y):
    return [(1 - x, y), (x, 1 - y), (1 - x, 1 - y)]


def _gather_weights(packed):
    _, hr, wd = packed.shape

    def plan_ici(x, y, c):
        k = 2 * x + y
        local = [((), (k,))]
        remote = [((c,), (k, c), (2 * px + py, c), (px, py, c)) for px, py in _other_chips(x, y)]
        return local, remote

    def plan_d2d(x, y, c):
        remote = [((2 * px + py, c), (2 * px + py, c), (2 * px + py, 1 - c), (x, y, 1 - c))
                  for px, py in _other_chips(x, y)]
        return [], remote

    full = _exchange(packed, (4, 2, hr, wd), packed.dtype, plan_ici, name="gather_w_ici")
    return _exchange(None, (4, 2, hr, wd), packed.dtype, plan_d2d, name="gather_w_d2d", dst_init=full)


def _add_pairs(a, b, *, name):
    s, hr, wd = a.shape
    tr = _pick(hr, 600, 16)

    def body(a_ref, b_ref, o_ref):
        o_ref[...] = (a_ref[...].astype(F32) + b_ref[...].astype(F32)).astype(BF16)

    spec = pl.BlockSpec((1, tr, wd), lambda j, i: (j, i, 0))
    return _pcall(body, name=name, grid=(s, hr // tr), in_specs=[spec, spec], out_specs=spec,
                  out_shape=jax.ShapeDtypeStruct(a.shape, BF16), compiler_params=_params(("parallel", "parallel")))(a, b)


def _sum_chips(b, *, name):
    s, hr, wd = b.shape
    tr = _pick(hr, 600, 16)

    def body(b_ref, o_ref):
        acc = b_ref[0].astype(F32)
        for k in range(1, s):
            acc = acc + b_ref[k].astype(F32)
        o_ref[...] = acc

    return _pcall(body, name=name, grid=(hr // tr,), in_specs=[pl.BlockSpec((s, tr, wd), lambda i: (0, i, 0))],
                  out_specs=pl.BlockSpec((tr, wd), lambda i: (i, 0)), out_shape=jax.ShapeDtypeStruct((hr, wd), F32),
                  compiler_params=_params(("parallel",)))(b)


def _reduce_scatter_grads(g, ci):
    _, nchip, hr, wd = g.shape

    def plan_pair(x, y, c):
        return [], [((1 - c,), (), (), (x, y, 1 - c))]

    def plan_chips(x, y, c):
        k = 2 * x + y
        local = [((k,), (k,))]
        remote = [((2 * px + py,), (k,), (2 * px + py,), (px, py, c)) for px, py in _other_chips(x, y)]
        return local, remote

    def plan_halves(x, y, c):
        return [((), (c,))], [((), (c,), (1 - c,), (x, y, 1 - c))]

    from_sibling = _exchange(g, (nchip, hr, wd), BF16, plan_pair, name="reduce_g_pair")
    pair = _add_pairs(lax.dynamic_index_in_dim(g, ci, 0, keepdims=False), from_sibling, name="reduce_g_pair_add")
    landed = _exchange(pair, (nchip, hr, wd), BF16, plan_chips, name="reduce_g_chips")
    half = _sum_chips(landed, name="reduce_g_chip_sum")
    return _exchange(half, (2, hr, wd), F32, plan_halves, name="reduce_g_halves")


def _ada_fwd(c_all, w_sh, b_sh, *, name):
    nb, d = c_all.shape
    wcols = w_sh.shape[1]
    tn = _pick(wcols, 512)

    def body(c_ref, w_ref, b_ref, o_ref, a_ref):
        cv = c_ref[...]
        act = cv * jax.nn.sigmoid(cv)
        a_ref[...] = act
        o_ref[...] = jnp.dot(act.astype(BF16), w_ref[...].astype(BF16), preferred_element_type=F32) + b_ref[...]

    return _pcall(
        body, name=name, grid=(wcols // tn,),
        in_specs=[pl.BlockSpec((nb, d), lambda j: (0, 0)), pl.BlockSpec((d, tn), lambda j: (0, j)),
                  pl.BlockSpec((1, tn), lambda j: (0, j))],
        out_specs=[pl.BlockSpec((nb, tn), lambda j: (0, j)), pl.BlockSpec((nb, d), lambda j: (0, 0))],
        out_shape=[jax.ShapeDtypeStruct((nb, wcols), F32), jax.ShapeDtypeStruct((nb, d), F32)],
        compiler_params=_params(("arbitrary",)))(c_all, w_sh, b_sh)


def _sum_devices(g, *, name):
    nd, m, w = g.shape

    def body(g_ref, o_ref):
        acc = g_ref[0]
        for k in range(1, nd):
            acc = acc + g_ref[k]
        o_ref[...] = acc

    return _pcall(body, name=name, out_shape=jax.ShapeDtypeStruct((m, w), F32),
                  compiler_params=pltpu.CompilerParams(vmem_limit_bytes=VMEM_LIMIT_BYTES))(g)


def _adamw(w, g, m, v, *, name):
    rows, cols = w.shape
    tr = _pick(rows, max(8, (1 << 18) // cols), 8)
    c1 = 1.0 / (1.0 - ADAM_B1 ** ADAM_STEP)
    c2 = 1.0 / (1.0 - ADAM_B2 ** ADAM_STEP)

    def body(w_ref, g_ref, m_ref, v_ref, d_ref, nm_ref, nv_ref):
        gv = g_ref[...]
        nm = ADAM_B1 * m_ref[...] + (1.0 - ADAM_B1) * gv
        nv = ADAM_B2 * v_ref[...] + (1.0 - ADAM_B2) * (gv * gv)
        d_ref[...] = -ADAM_LR * ((nm * c1) / (jnp.sqrt(nv * c2) + ADAM_EPS) + ADAM_WD * w_ref[...])
        nm_ref[...] = nm
        nv_ref[...] = nv

    spec = pl.BlockSpec((tr, cols), lambda i: (i, 0))
    shp = jax.ShapeDtypeStruct((rows, cols), F32)
    return _pcall(body, name=name, grid=(rows // tr,), in_specs=[spec] * 4, out_specs=[spec] * 3,
                  out_shape=[shp] * 3, compiler_params=_params(("parallel",)))(w, g, m, v)


PACK_W = 1024


def _pack_rows(parts):
    return jnp.concatenate([p.reshape(-1) for p in parts]).reshape(-1, PACK_W)


def _unpack_rows(flat, shapes):
    out, off = [], 0
    for shp in shapes:
        n = math.prod(shp)
        out.append(flat[off:off + n].reshape(shp))
        off += n
    return out


def _permute_in_cols(w):
    qa, ka, va = w[:, :QA_W], w[:, QA_W:QA_W + KA_W], w[:, QA_W + KA_W:QA_W + 2 * KA_W]
    o = QA_W + 2 * KA_W
    return jnp.concatenate([qa, w[:, o:o + 3 * QB_W], ka, va, w[:, o + 3 * QB_W:]], axis=1)


def kernel(x, c, positions, w_ada, b_ada, w_in, sinks, w_branch_a, w_branch_b, w_o, ln1_g, ln1_b, w_gate_up, w_down, ln2_g, ln2_b, loss_target, m_w_ada, m_b_ada, m_w_in, m_sinks, m_w_branch_a, m_w_branch_b, m_w_o, m_ln1_g, m_ln1_b, m_w_gate_up, m_w_down, m_ln2_g, m_ln2_b, v_w_ada, v_b_ada, v_w_in, v_sinks, v_w_branch_a, v_w_branch_b, v_w_o, v_ln1_g, v_ln1_b, v_w_gate_up, v_w_down, v_ln2_g, v_ln2_b):
    xi, yi, ci = _coords()
    chip = 2 * xi + yi
    dev = 4 * xi + 2 * yi + ci
    NB, T, D = x.shape
    nchip, ndev = 4, 8
    ada_cols = w_ada.shape[2]

    c_blk = jnp.zeros((8, D), F32).at[:NB].set(c)
    c_all = _allgather_small(c_blk, name="gather_c").reshape(ndev, 8, D)[:, :NB].reshape(ndev * NB, D)
    b_sh = lax.dynamic_slice(b_ada, (0, chip * ada_cols), (1, ada_cols))
    mod_part, c_act = _ada_fwd(c_all, w_ada[0], b_sh, name="ada_fwd")
    mod_g = _allgather_small(mod_part, name="gather_mod").reshape(nchip, 2, ndev * NB, ada_cols)[:, 0]
    mod_all = jnp.transpose(mod_g, (1, 0, 2)).reshape(ndev * NB, nchip * ada_cols)
    mod = lax.dynamic_slice(mod_all, (NB * dev, 0), (NB, nchip * ada_cols))

    shard_shapes = [w_in.shape[1:], w_branch_a.shape[1:], w_branch_b.shape[1:], w_o.shape[1:],
                    w_gate_up.shape[1:], w_down.shape[1:]]
    packed = _pack_rows([w_in[0], w_branch_a[0], w_branch_b[0], w_o[0], w_gate_up[0], w_down[0]]).astype(BF16)
    rows = packed.shape[0]
    hr = rows // 2
    gathered = _gather_weights(packed.reshape(2, hr, PACK_W)).reshape(nchip, rows * PACK_W)
    per_chip = [_unpack_rows(gathered[k], shard_shapes) for k in range(nchip)]
    cat = lambda n, axis: jnp.concatenate([per_chip[k][n] for k in range(nchip)], axis=axis)
    w_in_f = _permute_in_cols(cat(0, 1))
    w_a_f, w_b_f, w_o_f, w_gu_f, w_d_f = cat(1, 0), cat(2, 1), cat(3, 0), cat(4, 1), cat(5, 0)

    res = _local_step(x, mod, positions, w_in_f, w_a_f, w_b_f, w_o_f, w_gu_f, w_d_f, sinks[0],
                      ln1_g, ln1_b, ln2_g, ln2_b, loss_target)

    gq = res["g_win"]
    g_in = jnp.concatenate([gq[0], gq[4], gq[5], gq[1], gq[2], gq[3], gq[6]], axis=1)
    split = lambda a, axis: jnp.split(a, nchip, axis=axis)
    pieces = [split(g_in, 1), split(res["g_wa"], 0), split(res["g_wb"], 1), split(res["g_wo"], 0),
              split(res["g_wgu"], 1), split(res["g_wd"], 0)]
    g_packed = jnp.stack([_pack_rows([p[k] for p in pieces]).astype(BF16) for k in range(nchip)])
    g_packed = jnp.transpose(g_packed.reshape(nchip, 2, hr, PACK_W), (1, 0, 2, 3))
    g_red = _reduce_scatter_grads(g_packed, ci).reshape(rows * PACK_W)
    g_w_in, g_w_a, g_w_b, g_w_o, g_w_gu, g_w_d = _unpack_rows(g_red, shard_shapes)

    small_rows = 24
    misc = jnp.zeros((1, D), F32).at[0, :A_Q_HEADS].set(res["dsink"]).at[0, A_Q_HEADS].set(jnp.sum(res["loss"]))
    small = jnp.concatenate([res["dmod"].reshape(NB * 6, D), jnp.sum(res["ln_grads"], axis=0), misc,
                             jnp.zeros((small_rows - NB * 6 - 5, D), F32)], axis=0)
    small_all = _allgather_small(small, name="gather_small").reshape(ndev, small_rows, D)
    dmod_all = small_all[:, :NB * 6].reshape(ndev * NB, 6 * D)
    sums = _sum_devices(small_all, name="sum_small")
    g_b_ada = (sums[0:6] + sums[6:12]).reshape(1, 6 * D)
    g_ln1_g, g_ln1_b, g_ln2_g, g_ln2_b = (sums[12 + n][None] for n in range(4))
    g_sinks = sums[16, :A_Q_HEADS][None]
    loss = sums[16, A_Q_HEADS]
    dmod_sh = lax.dynamic_slice(dmod_all, (0, chip * ada_cols), (ndev * NB, ada_cols))
    g_w_ada = _mm(c_act, dmod_sh, ta=True, name="ada_dw")

    names = ["w_ada", "b_ada", "w_in", "sinks", "w_branch_a", "w_branch_b", "w_o", "ln1_g", "ln1_b",
             "w_gate_up", "w_down", "ln2_g", "ln2_b"]
    ws = [w_ada, b_ada, w_in, sinks, w_branch_a, w_branch_b, w_o, ln1_g, ln1_b, w_gate_up, w_down, ln2_g, ln2_b]
    ms = [m_w_ada, m_b_ada, m_w_in, m_sinks, m_w_branch_a, m_w_branch_b, m_w_o, m_ln1_g, m_ln1_b, m_w_gate_up,
          m_w_down, m_ln2_g, m_ln2_b]
    vs = [v_w_ada, v_b_ada, v_w_in, v_sinks, v_w_branch_a, v_w_branch_b, v_w_o, v_ln1_g, v_ln1_b, v_w_gate_up,
          v_w_down, v_ln2_g, v_ln2_b]
    gs = [g_w_ada, g_b_ada, g_w_in, g_sinks, g_w_a, g_w_b, g_w_o, g_ln1_g, g_ln1_b, g_w_gu, g_w_d, g_ln2_g, g_ln2_b]
    grads, deltas, new_ms, new_vs = [], [], [], []
    for name, w, g, m, v in zip(names, ws, gs, ms, vs):
        shp = w.shape
        w2, m2, v2 = (a.reshape(shp[-2], shp[-1]) for a in (w, m, v))
        g2 = g.reshape(shp[-2], shp[-1])
        d, nm, nv = _adamw(w2, g2, m2, v2, name="adamw_" + name)
        grads.append(g2.reshape(shp))
        deltas.append(d.reshape(shp))
        new_ms.append(nm.reshape(shp))
        new_vs.append(nv.reshape(shp))
    return (loss, res["grad_x"], *grads, *deltas, *new_ms, *new_vs)
```

```python
import functools
import math

import jax
import jax.numpy as jnp
from jax import lax
from jax.experimental import pallas as pl
from jax.experimental.pallas import tpu as pltpu

F32 = jnp.float32
BF16 = jnp.bfloat16
MESH = pl.DeviceIdType.MESH

HEAD_DIM = 64
PAIR_W = 2 * HEAD_DIM
BLOCK = 128
A_Q_HEADS = 16
A_KV_HEADS = 2
A_WINDOW = 128
B_PATTERNS = ((128, 1), (512, 4), (2048, 16))
B_GROUP_HEADS = 8
QA_W = A_Q_HEADS * HEAD_DIM
KA_W = A_KV_HEADS * HEAD_DIM
GB_W = B_GROUP_HEADS * HEAD_DIM
QB_W = GB_W * len(B_PATTERNS)
A_W = QA_W + 2 * KA_W
VAR_W = 3 * GB_W
N_VAR = 1 + len(B_PATTERNS)
QKV_P = N_VAR * VAR_W
ROPE_THETA = 10000.0
LN_EPS = 1e-5
NEG_INF = -1e30
DEPTH = 1
ALPHA = (2 * DEPTH) ** 0.25
SCALE = HEAD_DIM ** -0.5

ADAM_LR, ADAM_B1, ADAM_B2, ADAM_EPS, ADAM_WD, ADAM_STEP = 0.001, 0.9, 0.999, 1e-08, 0.01, 10

VMEM_LIMIT_BYTES = 56 * 1024 * 1024


def _pcall(body, **kw):
    return pl.pallas_call(body, **kw)


def _params(sem=None):
    return pltpu.CompilerParams(dimension_semantics=sem, vmem_limit_bytes=VMEM_LIMIT_BYTES)


def _pick(n, target, quantum=128):
    t = (min(target, n) // quantum) * quantum
    while t >= quantum:
        if n % t == 0:
            return t
        t -= quantum
    return n


def _mm(a, b, *, name, ta=False, tb=False, out_dtype=F32, add=None, tm=1024, tn=1536, tk=1536):
    if ta:
        K, M = a.shape
    else:
        M, K = a.shape
    if tb:
        Nn, K2 = b.shape
    else:
        K2, Nn = b.shape
    assert K == K2, (a.shape, b.shape)
    tm, tn, tk = _pick(M, tm), _pick(Nn, tn), _pick(K, tk)
    nk = K // tk
    dn = (((0 if ta else 1,), (1 if tb else 0,)), ((), ()))

    def body(*refs):
        refs = list(refs)
        a_ref, b_ref = refs[:2]
        add_ref = refs[2] if add is not None else None
        o_ref = refs[3] if add is not None else refs[2]
        part = lax.dot_general(a_ref[...].astype(BF16), b_ref[...].astype(BF16), dn, preferred_element_type=F32)

        def finish(r):
            if add is not None:
                r = r + add_ref[...]
            o_ref[...] = r.astype(out_dtype)

        if nk == 1:
            finish(part)
            return
        acc = refs[-1]
        k = pl.program_id(2)

        @pl.when(k == 0)
        def _():
            acc[...] = part

        @pl.when(k > 0)
        def _():
            acc[...] += part

        @pl.when(k == nk - 1)
        def _():
            finish(acc[...])

    a_spec = pl.BlockSpec((tk, tm), lambda i, j, k: (k, i)) if ta else pl.BlockSpec((tm, tk), lambda i, j, k: (i, k))
    b_spec = pl.BlockSpec((tn, tk), lambda i, j, k: (j, k)) if tb else pl.BlockSpec((tk, tn), lambda i, j, k: (k, j))
    o_spec = pl.BlockSpec((tm, tn), lambda i, j, k: (i, j))
    ins, specs = [a, b], [a_spec, b_spec]
    if add is not None:
        ins.append(add)
        specs.append(o_spec)
    return _pcall(
        body, name=name, grid=(M // tm, Nn // tn, nk), in_specs=specs, out_specs=o_spec,
        out_shape=jax.ShapeDtypeStruct((M, Nn), out_dtype),
        scratch_shapes=[pltpu.VMEM((tm, tn), F32)] if nk > 1 else [],
        compiler_params=_params(("parallel", "parallel", "arbitrary")),
    )(*ins)


def _mm_multi(a_list, b_list, *, name, add=None, out_dtype=F32, tm=512):
    M = a_list[0].shape[0]
    tm = _pick(M, tm)
    ns = len(a_list)
    b_arrs, b_specs = [], []
    for b in b_list:
        arr, shp, idx = b if isinstance(b, tuple) else (b, b.shape, (0, 0))
        b_arrs.append(arr)
        b_specs.append(pl.BlockSpec(shp, lambda i, idx=idx: idx))
    Nn = b_specs[0].block_shape[0]
    dn = (((1,), (1,)), ((), ()))

    def body(*refs):
        a_refs, b_refs = refs[:ns], refs[ns:2 * ns]
        acc = None
        for a_ref, b_ref in zip(a_refs, b_refs):
            part = lax.dot_general(a_ref[...].astype(BF16), b_ref[...], dn, preferred_element_type=F32)
            acc = part if acc is None else acc + part
        if add is not None:
            acc = acc + refs[2 * ns][...]
        refs[-1][...] = acc.astype(out_dtype)

    o_spec = pl.BlockSpec((tm, Nn), lambda i: (i, 0))
    specs = [pl.BlockSpec((tm, a.shape[1]), lambda i: (i, 0)) for a in a_list] + b_specs
    ins = list(a_list) + b_arrs
    if add is not None:
        specs.append(o_spec)
        ins.append(add)
    return _pcall(body, name=name, grid=(M // tm,), in_specs=specs, out_specs=o_spec,
                  out_shape=jax.ShapeDtypeStruct((M, Nn), out_dtype), compiler_params=_params(("parallel",)))(*ins)


def _lane(shape):
    return lax.broadcasted_iota(jnp.int32, shape, len(shape) - 1)


def _rot_half(v):
    w = v.shape[-1]
    first = (_lane(v.shape) % HEAD_DIM) < (HEAD_DIM // 2)
    return jnp.where(first, pltpu.roll(v, w - HEAD_DIM // 2, v.ndim - 1), pltpu.roll(v, HEAD_DIM // 2, v.ndim - 1))


def _widen(t, w):
    return t if w == t.shape[-1] else jnp.concatenate([t] * (w // t.shape[-1]), axis=-1)


def _unrope(v, cos, sins):
    w = v.shape[-1]
    return v * _widen(cos, w) - _rot_half(v) * _widen(sins, w)


def _rope_tables(positions):
    half = HEAD_DIM // 2
    inv = ROPE_THETA ** (-jnp.arange(half, dtype=F32) / half)
    ang = positions.astype(F32)[..., None] * inv
    cos, sin = jnp.cos(ang), jnp.sin(ang)
    cosf = jnp.concatenate([cos, cos, cos, cos], axis=-1)
    sins = jnp.concatenate([-sin, sin, -sin, sin], axis=-1)
    n = positions.shape[0] * positions.shape[1]
    return cosf.reshape(n, PAIR_W), sins.reshape(n, PAIR_W)


def _inproj(x2, scale, shift, w, cosf, sins, flags, *, T, name):
    N, D = x2.shape
    tm, tn = _pick(T, 512), VAR_W
    tpe = T // tm

    def body(x_ref, sc_ref, sh_ref, w_ref, c_ref, s_ref, f_ref, *outs):
        o_refs, u_ref = outs[:N_VAR], outs[N_VAR]
        j = pl.program_id(1)

        @pl.when(j == 0)
        def _():
            u_ref[...] = (x_ref[...] * (1.0 + sc_ref[0]) + sh_ref[0]).astype(BF16)

        acc = jnp.dot(u_ref[...], w_ref[...], preferred_element_type=F32)
        fl = f_ref[...]
        ce = 1.0 + (_widen(c_ref[...], tn) - 1.0) * fl
        se = _widen(s_ref[...], tn) * fl
        res = (acc * ce + _rot_half(acc) * se).astype(BF16)
        for v in range(N_VAR):
            @pl.when(j == v)
            def _(v=v):
                o_refs[v][...] = res

    ex = pl.BlockSpec((1, 1, D), lambda i, j: (i // tpe, 0, 0))
    tab = pl.BlockSpec((tm, PAIR_W), lambda i, j: (i, 0))
    keep = lambda w_: pl.BlockSpec((tm, w_), lambda i, j: (i, 0))
    return _pcall(
        body, name=name, grid=(N // tm, N_VAR),
        in_specs=[keep(D), ex, ex, pl.BlockSpec((D, tn), lambda i, j: (0, j)), tab, tab,
                  pl.BlockSpec((1, tn), lambda i, j: (0, j))],
        out_specs=[keep(tn)] * N_VAR + [keep(D)],
        out_shape=[jax.ShapeDtypeStruct((N, tn), BF16)] * N_VAR + [jax.ShapeDtypeStruct((N, D), BF16)],
        compiler_params=_params(("parallel", "arbitrary")),
    )(x2, scale, shift, w, cosf, sins, flags)


class _Geom:
    def __init__(self, g):
        if g is None:
            self.r, self.nq, self.n_back, self.sink = 1, A_Q_HEADS, A_WINDOW - 1, True
            self.qw, self.kw = QA_W, KA_W
            self.qidx = lambda j: 0
            self.kidx = lambda j: QA_W // KA_W
            self.vidx = lambda j: QA_W // KA_W + 1
        else:
            window, r = B_PATTERNS[g]
            self.r, self.nq, self.n_back, self.sink = r, B_GROUP_HEADS, window // r, False
            self.qw, self.kw = GB_W, GB_W
            self.qidx = lambda j: 3 * j
            self.kidx = lambda j: 3 * j + 1
            self.vidx = lambda j: 3 * j + 2
        self.ntile = self.qw // PAIR_W


def _stack_heads(t):
    first = _lane(t.shape) < HEAD_DIM
    z = jnp.zeros_like(t)
    return jnp.concatenate([jnp.where(first, t, z), jnp.where(first, z, t)], axis=0)


def _unstack_heads(v2):
    return jnp.where(_lane((BLOCK, PAIR_W)) < HEAD_DIM, v2[:BLOCK], v2[BLOCK:])


def _dup_head(t, kh):
    tf = t.astype(F32)
    keep = (_lane(t.shape) < HEAD_DIM) if kh == 0 else (_lane(t.shape) >= HEAD_DIM)
    return jnp.where(keep, tf, pltpu.roll(tf, HEAD_DIM, 1)).astype(t.dtype)


def _fold_heads(t):
    return t + pltpu.roll(t, HEAD_DIM, 1)


def _softmax_parts(s, i, n_back, sinkcol):
    rows = s.shape[0]
    qi = jnp.bitwise_and(lax.broadcasted_iota(jnp.int32, (rows, 2 * BLOCK), 0), BLOCK - 1)
    ki = lax.broadcasted_iota(jnp.int32, (rows, 2 * BLOCK), 1)
    dist = qi + BLOCK - ki
    valid = jnp.logical_and(jnp.logical_and(dist >= 0, dist <= n_back), jnp.logical_or(ki >= BLOCK, i > 0))
    s = jnp.where(valid, s * SCALE, NEG_INF)
    m = jnp.max(s, axis=1, keepdims=True)
    if sinkcol is not None:
        m = jnp.maximum(m, sinkcol)
    p = jnp.exp(s - m)
    den = jnp.sum(p, axis=1, keepdims=True)
    es = None
    if sinkcol is not None:
        es = jnp.exp(sinkcol - m)
        den = den + es
    return p, m, den, es


_NT = (((1,), (1,)), ((), ()))
_TN = (((0,), (0,)), ((), ()))


def _rows2(prev_ref, cur_ref, cs):
    return jnp.concatenate([prev_ref[0, :, cs], cur_ref[0, :, cs]], axis=0)


def _sink_col(sink_ref, kh, nblocks):
    return jnp.concatenate([jnp.full((BLOCK, 1), sink_ref[kh * nblocks + b], F32) for b in range(nblocks)], axis=0)


def _tile(t):
    return slice(t * PAIR_W, (t + 1) * PAIR_W)


def _attn_fwd(qkv, sinks, g, *, NB, T, name):
    geo = _Geom(g)
    r, qw, kw, ntile = geo.r, geo.qw, geo.kw, geo.ntile
    tsub = T // r
    nblk = tsub // BLOCK
    qkv3 = qkv.reshape(NB, tsub, r * VAR_W)
    with_lse = g is not None
    out_dtype = F32 if with_lse else BF16
    tiles_per_kv = ntile // A_KV_HEADS

    def body(q_ref, kp_ref, kc_ref, vp_ref, vc_ref, sink_ref, o_ref, *rest):
        i = pl.program_id(2)
        if geo.sink:
            kall, vall = _rows2(kp_ref, kc_ref, _tile(0)), _rows2(vp_ref, vc_ref, _tile(0))
            for kh in range(A_KV_HEADS):
                tiles = [kh * tiles_per_kv + t for t in range(tiles_per_kv)]
                q2 = jnp.concatenate([_stack_heads(q_ref[0, :, _tile(t)]) for t in tiles], axis=0)
                s = lax.dot_general(q2, _dup_head(kall, kh), _NT, preferred_element_type=F32)
                p, m, den, _ = _softmax_parts(s, i, geo.n_back, _sink_col(sink_ref, kh, 2 * tiles_per_kv))
                o2 = jnp.dot(p.astype(BF16), _dup_head(vall, kh), preferred_element_type=F32) / den
                for n, t in enumerate(tiles):
                    o_ref[0, :, _tile(t)] = _unstack_heads(o2[2 * BLOCK * n:2 * BLOCK * (n + 1)]).astype(out_dtype)
        else:
            l_ref = rest[0]
            for t in range(ntile):
                q2 = _stack_heads(q_ref[0, :, _tile(t)])
                s = lax.dot_general(q2, _rows2(kp_ref, kc_ref, _tile(t)), _NT, preferred_element_type=F32)
                p, m, den, _ = _softmax_parts(s, i, geo.n_back, None)
                o2 = jnp.dot(p.astype(BF16), _rows2(vp_ref, vc_ref, _tile(t)), preferred_element_type=F32) / den
                o_ref[0, :, _tile(t)] = _unstack_heads(o2)
                l_ref[0, :, _tile(t)] = _unstack_heads(jnp.broadcast_to(m + jnp.log(den), (2 * BLOCK, PAIR_W)))

    prev = lambda i: jnp.maximum(i - 1, 0)
    in_specs = [
        pl.BlockSpec((1, BLOCK, qw), lambda b, j, i: (b, i, geo.qidx(j))),
        pl.BlockSpec((1, BLOCK, kw), lambda b, j, i: (b, prev(i), geo.kidx(j))),
        pl.BlockSpec((1, BLOCK, kw), lambda b, j, i: (b, i, geo.kidx(j))),
        pl.BlockSpec((1, BLOCK, kw), lambda b, j, i: (b, prev(i), geo.vidx(j))),
        pl.BlockSpec((1, BLOCK, kw), lambda b, j, i: (b, i, geo.vidx(j))),
        pl.BlockSpec(memory_space=pltpu.SMEM),
    ]
    o_spec = pl.BlockSpec((1, BLOCK, qw), lambda b, j, i: (b, i, j))
    o_shape = jax.ShapeDtypeStruct((NB, tsub, r * qw), out_dtype)
    res = _pcall(
        body, name=name, grid=(NB, r, nblk), in_specs=in_specs,
        out_specs=[o_spec, o_spec] if with_lse else o_spec, out_shape=[o_shape, o_shape] if with_lse else o_shape,
        compiler_params=_params(("parallel", "parallel", "arbitrary")),
    )(qkv3, qkv3, qkv3, qkv3, qkv3, sinks)
    if with_lse:
        return res[0].reshape(NB * T, qw), res[1].reshape(NB * T, qw)
    return res.reshape(NB * T, qw)


def _attn_bwd(qkv, do, dlse, cosf, sins, sinks, g, *, NB, T, name):
    geo = _Geom(g)
    r, qw, kw, ntile = geo.r, geo.qw, geo.kw, geo.ntile
    tsub = T // r
    nblk = tsub // BLOCK
    qkv3 = qkv.reshape(NB, tsub, r * VAR_W)
    do3 = do.reshape(NB, tsub, r * qw)
    cos3 = cosf.reshape(NB, tsub, r * PAIR_W)
    sin3 = sins.reshape(NB, tsub, r * PAIR_W)
    has_dlse = dlse is not None
    tiles_per_kv = ntile // A_KV_HEADS

    def grads(q2, kk, vv, do2, i, sinkcol, dlcol):
        s = lax.dot_general(q2, kk, _NT, preferred_element_type=F32)
        p, m, den, es = _softmax_parts(s, i, geo.n_back, sinkcol)
        inv = 1.0 / den
        p = p * inv
        dp = lax.dot_general(do2, vv, _NT, preferred_element_type=F32)
        delta = jnp.sum(p * dp, axis=1, keepdims=True)
        sk = es * inv * delta if es is not None else None
        if dlcol is not None:
            delta = delta - dlcol
        ds = (p * (dp - delta) * SCALE).astype(BF16)
        dq2 = jnp.dot(ds, kk, preferred_element_type=F32)
        dkk = lax.dot_general(ds, q2, _TN, preferred_element_type=F32)
        dvv = lax.dot_general(p.astype(BF16), do2, _TN, preferred_element_type=F32)
        return dq2, dkk, dvv, sk

    def body(*refs):
        it = iter(refs)
        q_ref, kp_ref, kc_ref, vp_ref, vc_ref, do_ref = (next(it) for _ in range(6))
        dl_ref = next(it) if has_dlse else None
        c_ref, s_ref, sink_ref, o_ref, ds_ref, dq_s, dk_s, dv_s, car_q, car_k, car_v = (next(it) for _ in range(11))
        b, j, i = pl.program_id(0), pl.program_id(1), pl.program_id(2)

        @pl.when(jnp.logical_and(b == 0, jnp.logical_and(j == 0, i == 0)))
        def _():
            ds_ref[...] = jnp.zeros_like(ds_ref)

        @pl.when(i == 0)
        def _():
            car_q[...] = jnp.zeros_like(car_q)
            car_k[...] = jnp.zeros_like(car_k)
            car_v[...] = jnp.zeros_like(car_v)

        @pl.when(i == nblk)
        def _():
            dk_s[...] = jnp.zeros_like(dk_s)
            dv_s[...] = jnp.zeros_like(dv_s)

        @pl.when(i < nblk)
        def _():
            if geo.sink:
                kall, vall = _rows2(kp_ref, kc_ref, _tile(0)), _rows2(vp_ref, vc_ref, _tile(0))
                lane1 = _lane((1, PAIR_W))
                dsink = jnp.zeros((1, PAIR_W), F32)
                dk_t = dv_t = None
                for kh in range(A_KV_HEADS):
                    tiles = [kh * tiles_per_kv + t for t in range(tiles_per_kv)]
                    q2 = jnp.concatenate([_stack_heads(q_ref[0, :, _tile(t)]) for t in tiles], axis=0)
                    do2 = jnp.concatenate([_stack_heads(do_ref[0, :, _tile(t)]) for t in tiles], axis=0)
                    nb = 2 * tiles_per_kv
                    dq2, dkk, dvv, sk = grads(q2, _dup_head(kall, kh), _dup_head(vall, kh), do2, i,
                                              _sink_col(sink_ref, kh, nb), None)
                    for n, t in enumerate(tiles):
                        dq_s[:, _tile(t)] = _unstack_heads(dq2[2 * BLOCK * n:2 * BLOCK * (n + 1)])
                    for bb in range(nb):
                        dsink = dsink + jnp.where(lane1 == kh * nb + bb, -jnp.sum(sk[BLOCK * bb:BLOCK * (bb + 1)]), 0.0)
                    dkk, dvv = _fold_heads(dkk), _fold_heads(dvv)
                    if kh == 0:
                        dk_t, dv_t = dkk, dvv
                    else:
                        second = _lane(dkk.shape) >= HEAD_DIM
                        dk_t, dv_t = jnp.where(second, dkk, dk_t), jnp.where(second, dvv, dv_t)
                dk_s[...] = dk_t
                dv_s[...] = dv_t
                ds_ref[0:1, :] += dsink
            else:
                for t in range(ntile):
                    q2, do2 = _stack_heads(q_ref[0, :, _tile(t)]), _stack_heads(do_ref[0, :, _tile(t)])
                    dlt = dl_ref[0, :, _tile(t)]
                    dlcol = jnp.concatenate([dlt[:, 0:1], dlt[:, HEAD_DIM:HEAD_DIM + 1]], axis=0)
                    dq2, dkk, dvv, _ = grads(q2, _rows2(kp_ref, kc_ref, _tile(t)), _rows2(vp_ref, vc_ref, _tile(t)),
                                             do2, i, None, dlcol)
                    dq_s[:, _tile(t)] = _unstack_heads(dq2)
                    dk_s[:, _tile(t)] = dkk
                    dv_s[:, _tile(t)] = dvv

        cos, sn = c_ref[0], s_ref[0]
        o_ref[0, :, 0:qw] = _unrope(car_q[...], cos, sn).astype(BF16)
        o_ref[0, :, qw:qw + kw] = _unrope(car_k[...] + dk_s[0:BLOCK, :], cos, sn).astype(BF16)
        o_ref[0, :, qw + kw:qw + 2 * kw] = (car_v[...] + dv_s[0:BLOCK, :]).astype(BF16)
        if qw + 2 * kw < VAR_W:
            o_ref[0, :, qw + 2 * kw:VAR_W] = jnp.zeros((BLOCK, VAR_W - qw - 2 * kw), BF16)
        car_q[...] = dq_s[...]
        car_k[...] = dk_s[BLOCK:2 * BLOCK, :]
        car_v[...] = dv_s[BLOCK:2 * BLOCK, :]

    cur = lambda i: jnp.minimum(i, nblk - 1)
    prv = lambda i: jnp.maximum(jnp.minimum(i, nblk - 1) - 1, 0)
    outb = lambda i: jnp.maximum(i - 1, 0)
    in_specs = [
        pl.BlockSpec((1, BLOCK, qw), lambda b, j, i: (b, cur(i), geo.qidx(j))),
        pl.BlockSpec((1, BLOCK, kw), lambda b, j, i: (b, prv(i), geo.kidx(j))),
        pl.BlockSpec((1, BLOCK, kw), lambda b, j, i: (b, cur(i), geo.kidx(j))),
        pl.BlockSpec((1, BLOCK, kw), lambda b, j, i: (b, prv(i), geo.vidx(j))),
        pl.BlockSpec((1, BLOCK, kw), lambda b, j, i: (b, cur(i), geo.vidx(j))),
        pl.BlockSpec((1, BLOCK, qw), lambda b, j, i: (b, cur(i), j)),
    ]
    ins = [qkv3, qkv3, qkv3, qkv3, qkv3, do3]
    if has_dlse:
        in_specs.append(pl.BlockSpec((1, BLOCK, qw), lambda b, j, i: (b, cur(i), j)))
        ins.append(dlse.reshape(NB, tsub, r * qw))
    in_specs += [
        pl.BlockSpec((1, BLOCK, PAIR_W), lambda b, j, i: (b, outb(i), j)),
        pl.BlockSpec((1, BLOCK, PAIR_W), lambda b, j, i: (b, outb(i), j)),
        pl.BlockSpec(memory_space=pltpu.SMEM),
    ]
    ins += [cos3, sin3, sinks]
    scratch = [pltpu.VMEM((BLOCK, qw), F32), pltpu.VMEM((2 * BLOCK, kw), F32), pltpu.VMEM((2 * BLOCK, kw), F32),
               pltpu.VMEM((BLOCK, qw), F32), pltpu.VMEM((BLOCK, kw), F32), pltpu.VMEM((BLOCK, kw), F32)]
    dqkv, dsink = _pcall(
        body, name=name, grid=(NB, r, nblk + 1), in_specs=in_specs,
        out_specs=[pl.BlockSpec((1, BLOCK, VAR_W), lambda b, j, i: (b, outb(i), j)),
                   pl.BlockSpec((8, PAIR_W), lambda b, j, i: (0, 0))],
        out_shape=[jax.ShapeDtypeStruct((NB, tsub, r * VAR_W), BF16), jax.ShapeDtypeStruct((8, PAIR_W), F32)],
        scratch_shapes=scratch, compiler_params=_params(("arbitrary", "arbitrary", "arbitrary")),
    )(*ins)
    return dqkv.reshape(NB * T, VAR_W), dsink


class _Rows:
    def __init__(self, N, T, tm):
        self.N, self.tm, self.tpe, self.grid = N, tm, T // tm, (N // tm,)

    def row(self, w, col=0):
        return pl.BlockSpec((self.tm, w), lambda i: (i, col))

    def ex(self, w):
        return pl.BlockSpec((1, 1, w), lambda i: (i // self.tpe, 0, 0))

    def const(self, shape):
        return pl.BlockSpec(shape, lambda i: tuple(0 for _ in shape))

    def first_of_example(self):
        return pl.program_id(0) % self.tpe == 0


def _acc(ref, first, val):
    @pl.when(first)
    def _():
        ref[0] = val

    @pl.when(jnp.logical_not(first))
    def _():
        ref[0] += val


def _colsum(v):
    return jnp.sum(v, axis=0, keepdims=True)


def _ln_stats(r):
    mu = jnp.mean(r, axis=-1, keepdims=True)
    xc = r - mu
    var = jnp.mean(xc * xc, axis=-1, keepdims=True)
    rstd = lax.rsqrt(var + LN_EPS)
    return xc * rstd, rstd


def _ln_bwd(dy, xhat, rstd, gain):
    dxh = dy * gain
    return rstd * (dxh - jnp.mean(dxh, axis=-1, keepdims=True) - xhat * jnp.mean(dxh * xhat, axis=-1, keepdims=True))


def _silu_parts(v):
    s = jax.nn.sigmoid(v)
    return v * s, s * (1.0 + v * (1.0 - s))


def _local_step(x, mod, positions, w_in, w_a, w_b, w_o, w_gu, w_d, sinks, ln1_g, ln1_b, ln2_g, ln2_b, target):
    NB, T, D = x.shape
    N = NB * T
    F = w_d.shape[0]
    x2 = x.reshape(N, D)
    tgt2 = target.reshape(N, D)
    shift_m, scale_m, gate_m, shift_f, scale_f, gate_f = [mod[:, None, k * D:(k + 1) * D] for k in range(6)]
    cosf, sins = _rope_tables(positions)
    col = jnp.arange(QKV_P)
    vcol = col % VAR_W
    flags = jnp.where(col < VAR_W, vcol < QA_W + KA_W, vcol < 2 * GB_W).astype(F32)[None]
    R = _Rows(N, T, _pick(T, 256))
    sds = jax.ShapeDtypeStruct
    exsum = lambda w=D: sds((NB, 1, w), F32)
    ngrp = len(B_PATTERNS)

    *qkv, u = _inproj(x2, scale_m, shift_m, w_in, cosf, sins, flags, T=T, name="inproj_qkv")
    gates = _mm(u, w_in[:, QKV_P:], name="inproj_gates")
    oa = _attn_fwd(qkv[0], sinks, None, NB=NB, T=T, name="attn_a_fwd")
    ob_parts = [_attn_fwd(qkv[1 + g], sinks, g, NB=NB, T=T, name=f"attn_b{g}_fwd") for g in range(ngrp)]
    (o1, l1), (o2, l2), (o3, l3) = ob_parts

    def merge_fwd(o1r, o2r, o3r, l1r, l2r, l3r, ob_ref):
        la, lb, lc = l1r[...], l2r[...], l3r[...]
        mx = jnp.maximum(jnp.maximum(la, lb), lc)
        ea, eb, ec = jnp.exp(la - mx), jnp.exp(lb - mx), jnp.exp(lc - mx)
        ob_ref[...] = ((ea * o1r[...] + eb * o2r[...] + ec * o3r[...]) / (ea + eb + ec)).astype(BF16)

    ob = _pcall(merge_fwd, name="merge_fwd", grid=R.grid, in_specs=[R.row(GB_W)] * 6, out_specs=R.row(GB_W),
                out_shape=sds((N, GB_W), BF16), compiler_params=_params(("parallel",)))(o1, o2, o3, l1, l2, l3)

    ya = _mm(oa, w_a, name="branch_a")
    yb = _mm(ob, w_b, name="branch_b")

    def gate_fwd(ya_r, yb_r, ga_r, gb_r, mg_ref):
        mg_ref[...] = (jax.nn.sigmoid(ga_r[...]) * ya_r[...] + jax.nn.sigmoid(gb_r[...]) * yb_r[...]).astype(BF16)

    merged = _pcall(gate_fwd, name="gate_fwd", grid=R.grid, in_specs=[R.row(D), R.row(D), R.row(D, 0), R.row(D, 1)],
                    out_specs=R.row(D), out_shape=sds((N, D), BF16),
                    compiler_params=_params(("parallel",)))(ya, yb, gates, gates)
    y = _mm(merged, w_o, name="out_proj")

    def norm1_fwd(x_r, y_r, gm_r, g_r, b_r, sf_r, hf_r, r1_ref, x1_ref, u2_ref):
        r1 = ALPHA * x_r[...] + (1.0 + gm_r[0]) * y_r[...]
        xhat, _ = _ln_stats(r1)
        x1 = xhat * g_r[...] + b_r[...]
        r1_ref[...] = r1
        x1_ref[...] = x1
        u2_ref[...] = (x1 * (1.0 + sf_r[0]) + hf_r[0]).astype(BF16)

    r1, x1, u2 = _pcall(
        norm1_fwd, name="norm1_fwd", grid=R.grid,
        in_specs=[R.row(D), R.row(D), R.ex(D), R.const((1, D)), R.const((1, D)), R.ex(D), R.ex(D)],
        out_specs=[R.row(D)] * 3, out_shape=[sds((N, D), F32), sds((N, D), F32), sds((N, D), BF16)],
        compiler_params=_params(("parallel",)))(x2, y, gate_m, ln1_g, ln1_b, scale_f, shift_f)

    tmf, tnf = _pick(N, 512), _pick(F, 1408)
    nft = F // tnf

    def ffn_up(u_r, wg_r, wu_r, hg_ref, hu_ref, a_ref):
        hg = jnp.dot(u_r[...], wg_r[...], preferred_element_type=F32)
        hu = jnp.dot(u_r[...], wu_r[...], preferred_element_type=F32)
        sl, _ = _silu_parts(hg)
        hg_ref[...] = hg.astype(BF16)
        hu_ref[...] = hu.astype(BF16)
        a_ref[...] = (sl * hu).astype(BF16)

    ftile = pl.BlockSpec((tmf, tnf), lambda i, j: (i, j))
    hg, hu, act = _pcall(
        ffn_up, name="ffn_up", grid=(N // tmf, nft),
        in_specs=[pl.BlockSpec((tmf, D), lambda i, j: (i, 0)), pl.BlockSpec((D, tnf), lambda i, j: (0, j)),
                  pl.BlockSpec((D, tnf), lambda i, j: (0, j + nft))],
        out_specs=[ftile] * 3, out_shape=[sds((N, F), BF16)] * 3,
        compiler_params=_params(("parallel", "arbitrary")))(u2, w_gu, w_gu)
    y2 = _mm(act, w_d, name="ffn_down")

    def norm2_loss_bwd(x1_r, y2_r, t_r, gf_r, g_r, b_r, dy2_ref, dx1_ref, dgf_ref, dg_ref, db_ref, loss_ref):
        first = R.first_of_example()
        y2v = y2_r[...]
        r2 = ALPHA * x1_r[...] + (1.0 + gf_r[0]) * y2v
        xhat, rstd = _ln_stats(r2)
        err = xhat * g_r[...] + b_r[...] - t_r[...]
        dx2 = err * (1.0 / D)
        dr2 = _ln_bwd(dx2, xhat, rstd, g_r[...])
        dy2_ref[...] = ((1.0 + gf_r[0]) * dr2).astype(BF16)
        dx1_ref[...] = ALPHA * dr2
        _acc(dgf_ref, first, _colsum(dr2 * y2v))
        _acc(dg_ref, first, _colsum(dx2 * xhat))
        _acc(db_ref, first, _colsum(dx2))
        part = 0.5 * jnp.sum(jnp.mean(err * err, axis=-1, keepdims=True))
        _acc(loss_ref, first, jnp.broadcast_to(part, (1, 128)))

    dy2, dx1p, dgate_f, dg2, db2, loss_p = _pcall(
        norm2_loss_bwd, name="norm2_loss_bwd", grid=R.grid,
        in_specs=[R.row(D), R.row(D), R.row(D), R.ex(D), R.const((1, D)), R.const((1, D))],
        out_specs=[R.row(D), R.row(D), R.ex(D), R.ex(D), R.ex(D), R.ex(128)],
        out_shape=[sds((N, D), BF16), sds((N, D), F32), exsum(), exsum(), exsum(), exsum(128)],
        compiler_params=_params(("arbitrary",)))(x1, y2, tgt2, gate_f, ln2_g, ln2_b)

    g_wd = _mm(act, dy2, ta=True, out_dtype=BF16, name="ffn_down_dw")

    def ffn_down_dx(dy_r, wd_r, hg_r, hu_r, dhg_ref, dhu_ref):
        da = lax.dot_general(dy_r[...], wd_r[...], _NT, preferred_element_type=F32)
        sl, dsl = _silu_parts(hg_r[...].astype(F32))
        dhg_ref[...] = (da * hu_r[...].astype(F32) * dsl).astype(BF16)
        dhu_ref[...] = (da * sl).astype(BF16)

    dhg, dhu = _pcall(
        ffn_down_dx, name="ffn_down_dx", grid=(N // tmf, nft),
        in_specs=[pl.BlockSpec((tmf, D), lambda i, j: (i, 0)), pl.BlockSpec((tnf, D), lambda i, j: (j, 0)), ftile, ftile],
        out_specs=[ftile] * 2, out_shape=[sds((N, F), BF16)] * 2,
        compiler_params=_params(("parallel", "arbitrary")))(dy2, w_d, hg, hu)
    du2 = _mm_multi([dhg, dhu], [(w_gu, (D, F), (0, 0)), (w_gu, (D, F), (0, 1))], name="ffn_up_dx")
    g_wgu = jnp.concatenate([_mm(u2, dhg, ta=True, out_dtype=BF16, name="ffn_gate_dw"),
                             _mm(u2, dhu, ta=True, out_dtype=BF16, name="ffn_up_dw")], axis=1)

    def norm1_bwd(dx1p_r, du2_r, x1_r, r1_r, y_r, sf_r, gm_r, g_r,
                  dxp_ref, dy_ref, dsf_ref, dhf_ref, dgm_ref, dg_ref, db_ref):
        first = R.first_of_example()
        du2v = du2_r[...]
        dx1 = dx1p_r[...] + du2v * (1.0 + sf_r[0])
        xhat, rstd = _ln_stats(r1_r[...])
        dr1 = _ln_bwd(dx1, xhat, rstd, g_r[...])
        dxp_ref[...] = ALPHA * dr1
        dy_ref[...] = ((1.0 + gm_r[0]) * dr1).astype(BF16)
        _acc(dsf_ref, first, _colsum(du2v * x1_r[...]))
        _acc(dhf_ref, first, _colsum(du2v))
        _acc(dgm_ref, first, _colsum(dr1 * y_r[...]))
        _acc(dg_ref, first, _colsum(dx1 * xhat))
        _acc(db_ref, first, _colsum(dx1))

    dxp, dy, dscale_f, dshift_f, dgate_m, dg1, db1 = _pcall(
        norm1_bwd, name="norm1_bwd", grid=R.grid,
        in_specs=[R.row(D)] * 5 + [R.ex(D), R.ex(D), R.const((1, D))],
        out_specs=[R.row(D), R.row(D)] + [R.ex(D)] * 5,
        out_shape=[sds((N, D), F32), sds((N, D), BF16)] + [exsum()] * 5,
        compiler_params=_params(("arbitrary",)))(dx1p, du2, x1, r1, y, scale_f, gate_m, ln1_g)

    dmerged = _mm(dy, w_o, tb=True, name="out_proj_dx")
    g_wo = _mm(merged, dy, ta=True, out_dtype=BF16, name="out_proj_dw")

    def gate_bwd(dm_r, ya_r, yb_r, ga_r, gb_r, dya_ref, dyb_ref, dg_ref):
        dm = dm_r[...]
        sa, sb = jax.nn.sigmoid(ga_r[...]), jax.nn.sigmoid(gb_r[...])
        dya_ref[...] = (dm * sa).astype(BF16)
        dyb_ref[...] = (dm * sb).astype(BF16)
        dg_ref[:, :D] = (dm * ya_r[...] * sa * (1.0 - sa)).astype(BF16)
        dg_ref[:, D:] = (dm * yb_r[...] * sb * (1.0 - sb)).astype(BF16)

    dya, dyb, dgates = _pcall(
        gate_bwd, name="gate_bwd", grid=R.grid, in_specs=[R.row(D)] * 3 + [R.row(D, 0), R.row(D, 1)],
        out_specs=[R.row(D), R.row(D), R.row(2 * D)],
        out_shape=[sds((N, D), BF16), sds((N, D), BF16), sds((N, 2 * D), BF16)],
        compiler_params=_params(("parallel",)))(dmerged, ya, yb, gates, gates)

    doa = _mm(dya, w_a, tb=True, out_dtype=BF16, name="branch_a_dx")
    g_wa = _mm(oa, dya, ta=True, out_dtype=BF16, name="branch_a_dw")
    dob = _mm(dyb, w_b, tb=True, name="branch_b_dx")
    g_wb = _mm(ob, dyb, ta=True, out_dtype=BF16, name="branch_b_dw")

    seg = (jnp.arange(GB_W)[:, None] // HEAD_DIM == jnp.arange(GB_W)[None, :] // HEAD_DIM).astype(BF16)

    def merge_bwd(dob_r, o1r, o2r, o3r, l1r, l2r, l3r, seg_r, d1, d2, d3, e1, e2, e3):
        dob_v = dob_r[...]
        la, lb, lc = l1r[...], l2r[...], l3r[...]
        mx = jnp.maximum(jnp.maximum(la, lb), lc)
        ea, eb, ec = jnp.exp(la - mx), jnp.exp(lb - mx), jnp.exp(lc - mx)
        inv = 1.0 / (ea + eb + ec)
        ws = [ea * inv, eb * inv, ec * inv]

        def headsum(v):
            hi = v.astype(BF16)
            r1_ = v - hi.astype(F32)
            mid = r1_.astype(BF16)
            lo = (r1_ - mid.astype(F32)).astype(BF16)
            sm = seg_r[...]
            return (jnp.dot(hi, sm, preferred_element_type=F32) + jnp.dot(mid, sm, preferred_element_type=F32)
                    + jnp.dot(lo, sm, preferred_element_type=F32))

        dws = [headsum(dob_v * o[...]) for o in (o1r, o2r, o3r)]
        mean = ws[0] * dws[0] + ws[1] * dws[1] + ws[2] * dws[2]
        for w_, dw_, d_ref, e_ref in zip(ws, dws, (d1, d2, d3), (e1, e2, e3)):
            d_ref[...] = (w_ * dob_v).astype(BF16)
            e_ref[...] = w_ * (dw_ - mean)

    mb = _pcall(
        merge_bwd, name="merge_bwd", grid=R.grid, in_specs=[R.row(GB_W)] * 7 + [R.const((GB_W, GB_W))],
        out_specs=[R.row(GB_W)] * 6, out_shape=[sds((N, GB_W), BF16)] * 3 + [sds((N, GB_W), F32)] * 3,
        compiler_params=_params(("parallel",)))(dob, o1, o2, o3, l1, l2, l3, seg)
    do_b, dlse_b = mb[:3], mb[3:]

    dqkv_a, dsink = _attn_bwd(qkv[0], doa, None, cosf, sins, sinks, None, NB=NB, T=T, name="attn_a_bwd")
    dqkv = [dqkv_a] + [_attn_bwd(qkv[1 + g], do_b[g], dlse_b[g], cosf, sins, sinks, g, NB=NB, T=T,
                                 name=f"attn_b{g}_bwd")[0] for g in range(ngrp)]

    wvar = lambda v: (w_in, (D, VAR_W), (0, v))
    du = _mm_multi(dqkv[:3], [wvar(v) for v in range(3)], name="inproj_dx0")
    du = _mm_multi([dqkv[3], dgates], [wvar(3), (w_in, (D, 2 * D), (0, QKV_P // (2 * D)))], add=du, name="inproj_dx1")
    g_win = [_mm(u, dseg, ta=True, out_dtype=BF16, name=f"inproj_dw{n}") for n, dseg in enumerate(dqkv + [dgates])]

    def x_bwd(dxp_r, du_r, x_r, sm_r, gx_ref, dsm_ref, dhm_ref):
        first = R.first_of_example()
        duv = du_r[...]
        gx_ref[...] = dxp_r[...] + duv * (1.0 + sm_r[0])
        _acc(dsm_ref, first, _colsum(duv * x_r[...]))
        _acc(dhm_ref, first, _colsum(duv))

    gx, dscale_m, dshift_m = _pcall(
        x_bwd, name="x_bwd", grid=R.grid, in_specs=[R.row(D)] * 3 + [R.ex(D)],
        out_specs=[R.row(D), R.ex(D), R.ex(D)], out_shape=[sds((N, D), F32), exsum(), exsum()],
        compiler_params=_params(("arbitrary",)))(dxp, du, x2, scale_m)

    dmod = jnp.concatenate([dshift_m, dscale_m, dgate_m, dshift_f, dscale_f, dgate_f], axis=-1)[:, 0]
    ln_grads = jnp.concatenate([dg1, db1, dg2, db2], axis=1)
    return dict(loss=loss_p[:, 0, 0], grad_x=gx.reshape(NB, T, D), g_win=g_win, g_wa=g_wa, g_wb=g_wb, g_wo=g_wo,
                g_wgu=g_wgu, g_wd=g_wd, dmod=dmod, ln_grads=ln_grads, dsink=dsink[0, :A_Q_HEADS])


def _coords():
    return lax.axis_index("x"), lax.axis_index("y"), lax.axis_index("c")


def _allgather_small(blk, *, name):
    m_per, n = blk.shape

    def body(x_ref, out_ref, send_sems, recv_sems, local_sem):
        x, y, c = _coords()
        me, sibling = (x, y, c), (x, y, 1 - c)
        chips = [(1 - x, y), (x, 1 - y), (1 - x, 1 - y)]

        def rows(px, py, pc):
            return out_ref.at[pl.ds((4 * px + 2 * py + pc) * m_per, m_per), :]

        def copy(k, block, to, src=None):
            return pltpu.make_async_remote_copy(
                src_ref=rows(*block) if src is None else src, dst_ref=rows(*block),
                send_sem=send_sems.at[k], recv_sem=recv_sems.at[k], device_id=to, device_id_type=MESH)

        mine = pltpu.make_async_copy(x_ref, rows(*me), local_sem)
        mine.start()
        first = [copy(0, me, sibling, src=x_ref)]
        first += [copy(1 + j, me, (*chip, c), src=x_ref) for j, chip in enumerate(chips)]
        for cp in first:
            cp.start()
        passed = [copy(4 + j, (*chip, c), sibling) for j, chip in enumerate(chips)]
        for j, chip in enumerate(chips):
            copy(1 + j, (*chip, c), me).wait_recv()
            passed[j].start()
        copy(0, sibling, me).wait_recv()
        for j, chip in enumerate(chips):
            copy(4 + j, (*chip, 1 - c), me).wait_recv()
        for cp in first + passed:
            cp.wait_send()
        mine.wait()

    return _pcall(
        body, name=name, out_shape=jax.ShapeDtypeStruct((8 * m_per, n), blk.dtype),
        in_specs=[pl.BlockSpec(memory_space=pltpu.VMEM)], out_specs=pl.BlockSpec(memory_space=pltpu.VMEM),
        scratch_shapes=[pltpu.SemaphoreType.DMA((7,)), pltpu.SemaphoreType.DMA((7,)), pltpu.SemaphoreType.DMA],
        compiler_params=pltpu.CompilerParams(vmem_limit_bytes=VMEM_LIMIT_BYTES),
    )(blk)


def _exchange(src, dst_shape, dst_dtype, plan, *, name, dst_init=None):
    nrem = len(plan(0, 0, 0))

    def body(*refs):
        refs = list(refs)
        src_ref = refs.pop(0) if src is not None else None
        if dst_init is not None:
            refs.pop(0)
        dst_ref, send_sems, recv_sems = refs
        if src_ref is None:
            src_ref = dst_ref
        x, y, c = _coords()
        remote = plan(x, y, c)
        at = lambda ref, idx: ref.at[idx] if idx else ref
        sends = [pltpu.make_async_remote_copy(src_ref=at(src_ref, si), dst_ref=at(dst_ref, di),
                                              send_sem=send_sems.at[n], recv_sem=recv_sems.at[n],
                                              device_id=peer, device_id_type=MESH)
                 for n, (si, di, ri, peer) in enumerate(remote)]
        for cp in sends:
            cp.start()
        for n, (si, di, ri, peer) in enumerate(remote):
            pltpu.make_async_remote_copy(src_ref=at(src_ref, si), dst_ref=at(dst_ref, ri),
                                         send_sem=send_sems.at[n], recv_sem=recv_sems.at[n],
                                         device_id=peer, device_id_type=MESH).wait_recv()
        for cp in sends:
            cp.wait_send()

    hbm = pl.BlockSpec(memory_space=pl.ANY)
    ins = ([src] if src is not None else []) + ([dst_init] if dst_init is not None else [])
    aliases = {len(ins) - 1: 0} if dst_init is not None else {}
    return _pcall(
        body, name=name, out_shape=jax.ShapeDtypeStruct(dst_shape, dst_dtype),
        in_specs=[hbm] * len(ins), out_specs=hbm, input_output_aliases=aliases,
        scratch_shapes=[pltpu.SemaphoreType.DMA((nrem,)), pltpu.SemaphoreType.DMA((nrem,))],
    )(*ins)


def _other_chips(x, y):
    return [(1 - x, y), (x, 1 - y), (1 - x, 1 - y)]


def _gather_weights(packed, chip):
    _, hr, wd = packed.shape

    def plan_ici(x, y, c):
        k = 2 * x + y
        return [((c,), (k, c), (2 * px + py, c), (px, py, c)) for px, py in _other_chips(x, y)]

    def plan_d2d(x, y, c):
        return [((2 * px + py, c), (2 * px + py, c), (2 * px + py, 1 - c), (x, y, 1 - c))
                for px, py in _other_chips(x, y)]

    full = _exchange(packed, (4, 2, hr, wd), packed.dtype, plan_ici, name="gather_w_ici")
    full = _exchange(None, (4, 2, hr, wd), packed.dtype, plan_d2d, name="gather_w_d2d", dst_init=full)
    return lax.dynamic_update_index_in_dim(full, packed, chip, 0)


def _add_pairs(a, b, *, name):
    s, hr, wd = a.shape
    tr = _pick(hr, 600, 16)

    def body(a_ref, b_ref, o_ref):
        o_ref[...] = (a_ref[...].astype(F32) + b_ref[...].astype(F32)).astype(BF16)

    spec = pl.BlockSpec((1, tr, wd), lambda j, i: (j, i, 0))
    return _pcall(body, name=name, grid=(s, hr // tr), in_specs=[spec, spec], out_specs=spec,
                  out_shape=jax.ShapeDtypeStruct(a.shape, BF16), compiler_params=_params(("parallel", "parallel")))(a, b)


def _sum_chips(b, *, name):
    s, hr, wd = b.shape
    tr = _pick(hr, 600, 16)

    def body(b_ref, o_ref):
        acc = b_ref[0].astype(F32)
        for k in range(1, s):
            acc = acc + b_ref[k].astype(F32)
        o_ref[...] = acc

    return _pcall(body, name=name, grid=(hr // tr,), in_specs=[pl.BlockSpec((s, tr, wd), lambda i: (0, i, 0))],
                  out_specs=pl.BlockSpec((tr, wd), lambda i: (i, 0)), out_shape=jax.ShapeDtypeStruct((hr, wd), F32),
                  compiler_params=_params(("parallel",)))(b)


def _reduce_scatter_grads(g, chip, ci):
    _, nchip, hr, wd = g.shape

    def plan_pair(x, y, c):
        return [((1 - c,), (), (), (x, y, 1 - c))]

    def plan_chips(x, y, c):
        k = 2 * x + y
        return [((2 * px + py,), (k,), (2 * px + py,), (px, py, c)) for px, py in _other_chips(x, y)]

    def plan_halves(x, y, c):
        return [((), (c,), (1 - c,), (x, y, 1 - c))]

    from_sibling = _exchange(g, (nchip, hr, wd), BF16, plan_pair, name="reduce_g_pair")
    pair = _add_pairs(lax.dynamic_index_in_dim(g, ci, 0, keepdims=False), from_sibling, name="reduce_g_pair_add")
    landed = _exchange(pair, (nchip, hr, wd), BF16, plan_chips, name="reduce_g_chips")
    own = lax.dynamic_index_in_dim(pair, chip, 0, keepdims=True)
    half = _sum_chips(lax.dynamic_update_index_in_dim(landed, own[0], chip, 0), name="reduce_g_chip_sum")
    both = _exchange(half, (2, hr, wd), F32, plan_halves, name="reduce_g_halves")
    return lax.dynamic_update_index_in_dim(both, half, ci, 0)


def _ada_fwd(c_all, w_sh, b_sh, *, name):
    nb, d = c_all.shape
    wcols = w_sh.shape[1]
    tn = _pick(wcols, 512)

    def body(c_ref, w_ref, b_ref, o_ref, a_ref):
        cv = c_ref[...]
        act = cv * jax.nn.sigmoid(cv)
        a_ref[...] = act
        o_ref[...] = jnp.dot(act.astype(BF16), w_ref[...].astype(BF16), preferred_element_type=F32) + b_ref[...]

    return _pcall(
        body, name=name, grid=(wcols // tn,),
        in_specs=[pl.BlockSpec((nb, d), lambda j: (0, 0)), pl.BlockSpec((d, tn), lambda j: (0, j)),
                  pl.BlockSpec((1, tn), lambda j: (0, j))],
        out_specs=[pl.BlockSpec((nb, tn), lambda j: (0, j)), pl.BlockSpec((nb, d), lambda j: (0, 0))],
        out_shape=[jax.ShapeDtypeStruct((nb, wcols), F32), jax.ShapeDtypeStruct((nb, d), F32)],
        compiler_params=_params(("arbitrary",)))(c_all, w_sh, b_sh)


def _sum_devices(g, *, name):
    nd, m, w = g.shape

    def body(g_ref, o_ref):
        acc = g_ref[0]
        for k in range(1, nd):
            acc = acc + g_ref[k]
        o_ref[...] = acc

    return _pcall(body, name=name, out_shape=jax.ShapeDtypeStruct((m, w), F32),
                  compiler_params=pltpu.CompilerParams(vmem_limit_bytes=VMEM_LIMIT_BYTES))(g)


def _adamw(w, g, m, v, *, name):
    rows, cols = w.shape
    tr = _pick(rows, max(8, (1 << 18) // cols), 8)
    c1 = 1.0 / (1.0 - ADAM_B1 ** ADAM_STEP)
    c2 = 1.0 / (1.0 - ADAM_B2 ** ADAM_STEP)

    def body(w_ref, g_ref, m_ref, v_ref, d_ref, nm_ref, nv_ref):
        gv = g_ref[...]
        nm = ADAM_B1 * m_ref[...] + (1.0 - ADAM_B1) * gv
        nv = ADAM_B2 * v_ref[...] + (1.0 - ADAM_B2) * (gv * gv)
        d_ref[...] = -ADAM_LR * ((nm * c1) / (jnp.sqrt(nv * c2) + ADAM_EPS) + ADAM_WD * w_ref[...])
        nm_ref[...] = nm
        nv_ref[...] = nv

    spec = pl.BlockSpec((tr, cols), lambda i: (i, 0))
    shp = jax.ShapeDtypeStruct((rows, cols), F32)
    return _pcall(body, name=name, grid=(rows // tr,), in_specs=[spec] * 4, out_specs=[spec] * 3,
                  out_shape=[shp] * 3, compiler_params=_params(("parallel",)))(w, g, m, v)


PACK_W = 1024


def _pack_rows(parts):
    return jnp.concatenate([p.reshape(-1) for p in parts]).reshape(-1, PACK_W)


def _unpack_rows(flat, shapes):
    out, off = [], 0
    for shp in shapes:
        n = math.prod(shp)
        out.append(flat[off:off + n].reshape(shp))
        off += n
    return out


def _permute_in_cols(w):
    ngrp = len(B_PATTERNS)
    qb, kb, vb = (w[:, A_W + n * QB_W:A_W + (n + 1) * QB_W] for n in range(3))
    parts = [w[:, :A_W], jnp.zeros((w.shape[0], VAR_W - A_W), w.dtype)]
    for g in range(ngrp):
        parts += [t[:, g * GB_W:(g + 1) * GB_W] for t in (qb, kb, vb)]
    return jnp.concatenate(parts + [w[:, A_W + 3 * QB_W:]], axis=1)


def _unpermute_in_grads(pieces):
    ga, groups, gg = pieces[0], pieces[1:-1], pieces[-1]
    cols = [ga[:, :A_W]]
    for n in range(3):
        cols += [gp[:, n * GB_W:(n + 1) * GB_W] for gp in groups]
    return jnp.concatenate(cols + [gg], axis=1)


def kernel(x, c, positions, w_ada, b_ada, w_in, sinks, w_branch_a, w_branch_b, w_o, ln1_g, ln1_b, w_gate_up, w_down, ln2_g, ln2_b, loss_target, m_w_ada, m_b_ada, m_w_in, m_sinks, m_w_branch_a, m_w_branch_b, m_w_o, m_ln1_g, m_ln1_b, m_w_gate_up, m_w_down, m_ln2_g, m_ln2_b, v_w_ada, v_b_ada, v_w_in, v_sinks, v_w_branch_a, v_w_branch_b, v_w_o, v_ln1_g, v_ln1_b, v_w_gate_up, v_w_down, v_ln2_g, v_ln2_b):
    xi, yi, ci = _coords()
    chip = 2 * xi + yi
    dev = 4 * xi + 2 * yi + ci
    NB, T, D = x.shape
    nchip, ndev = 4, 8
    ada_cols = w_ada.shape[2]

    c_blk = jnp.zeros((8, D), F32).at[:NB].set(c)
    c_all = _allgather_small(c_blk, name="gather_c").reshape(ndev, 8, D)[:, :NB].reshape(ndev * NB, D)
    b_sh = lax.dynamic_slice(b_ada, (0, chip * ada_cols), (1, ada_cols))
    mod_part, c_act = _ada_fwd(c_all, w_ada[0], b_sh, name="ada_fwd")
    mod_g = _allgather_small(mod_part, name="gather_mod").reshape(nchip, 2, ndev * NB, ada_cols)[:, 0]
    mod_all = jnp.transpose(mod_g, (1, 0, 2)).reshape(ndev * NB, nchip * ada_cols)
    mod = lax.dynamic_slice(mod_all, (NB * dev, 0), (NB, nchip * ada_cols))

    shard_shapes = [w_in.shape[1:], w_branch_a.shape[1:], w_branch_b.shape[1:], w_o.shape[1:],
                    w_gate_up.shape[1:], w_down.shape[1:]]
    packed = _pack_rows([w_in[0], w_branch_a[0], w_branch_b[0], w_o[0], w_gate_up[0], w_down[0]]).astype(BF16)
    rows = packed.shape[0]
    hr = rows // 2
    gathered = _gather_weights(packed.reshape(2, hr, PACK_W), chip).reshape(nchip, rows * PACK_W)
    per_chip = [_unpack_rows(gathered[k], shard_shapes) for k in range(nchip)]
    cat = lambda n, axis: jnp.concatenate([per_chip[k][n] for k in range(nchip)], axis=axis)
    w_in_f = _permute_in_cols(cat(0, 1))
    w_a_f, w_b_f, w_o_f, w_gu_f, w_d_f = cat(1, 0), cat(2, 1), cat(3, 0), cat(4, 1), cat(5, 0)

    res = _local_step(x, mod, positions, w_in_f, w_a_f, w_b_f, w_o_f, w_gu_f, w_d_f, sinks[0],
                      ln1_g, ln1_b, ln2_g, ln2_b, loss_target)

    g_in = _unpermute_in_grads(res["g_win"])
    split = lambda a, axis: jnp.split(a, nchip, axis=axis)
    pieces = [split(g_in, 1), split(res["g_wa"], 0), split(res["g_wb"], 1), split(res["g_wo"], 0),
              split(res["g_wgu"], 1), split(res["g_wd"], 0)]
    g_packed = jnp.stack([_pack_rows([p[k] for p in pieces]).astype(BF16) for k in range(nchip)])
    g_packed = jnp.transpose(g_packed.reshape(nchip, 2, hr, PACK_W), (1, 0, 2, 3))
    g_red = _reduce_scatter_grads(g_packed, chip, ci).reshape(rows * PACK_W)
    g_w_in, g_w_a, g_w_b, g_w_o, g_w_gu, g_w_d = _unpack_rows(g_red, shard_shapes)

    small_rows = 24
    misc = jnp.zeros((1, D), F32).at[0, :A_Q_HEADS].set(res["dsink"]).at[0, A_Q_HEADS].set(jnp.sum(res["loss"]))
    small = jnp.concatenate([res["dmod"].reshape(NB * 6, D), jnp.sum(res["ln_grads"], axis=0), misc,
                             jnp.zeros((small_rows - NB * 6 - 5, D), F32)], axis=0)
    small_all = _allgather_small(small, name="gather_small").reshape(ndev, small_rows, D)
    dmod_all = small_all[:, :NB * 6].reshape(ndev * NB, 6 * D)
    sums = _sum_devices(small_all, name="sum_small")
    g_b_ada = (sums[0:6] + sums[6:12]).reshape(1, 6 * D)
    g_ln1_g, g_ln1_b, g_ln2_g, g_ln2_b = (sums[12 + n][None] for n in range(4))
    g_sinks = sums[16, :A_Q_HEADS][None]
    loss = sums[16, A_Q_HEADS]
    dmod_sh = lax.dynamic_slice(dmod_all, (0, chip * ada_cols), (ndev * NB, ada_cols))
    g_w_ada = _mm(c_act, dmod_sh, ta=True, name="ada_dw")

    names = ["w_ada", "b_ada", "w_in", "sinks", "w_branch_a", "w_branch_b", "w_o", "ln1_g", "ln1_b",
             "w_gate_up", "w_down", "ln2_g", "ln2_b"]
    ws = [w_ada, b_ada, w_in, sinks, w_branch_a, w_branch_b, w_o, ln1_g, ln1_b, w_gate_up, w_down, ln2_g, ln2_b]
    ms = [m_w_ada, m_b_ada, m_w_in, m_sinks, m_w_branch_a, m_w_branch_b, m_w_o, m_ln1_g, m_ln1_b, m_w_gate_up,
          m_w_down, m_ln2_g, m_ln2_b]
    vs = [v_w_ada, v_b_ada, v_w_in, v_sinks, v_w_branch_a, v_w_branch_b, v_w_o, v_ln1_g, v_ln1_b, v_w_gate_up,
          v_w_down, v_ln2_g, v_ln2_b]
    gs = [g_w_ada, g_b_ada, g_w_in, g_sinks, g_w_a, g_w_b, g_w_o, g_ln1_g, g_ln1_b, g_w_gu, g_w_d, g_ln2_g, g_ln2_b]
    grads, deltas, new_ms, new_vs = [], [], [], []
    for name, w, g, m, v in zip(names, ws, gs, ms, vs):
        shp = w.shape
        w2, m2, v2 = (a.reshape(shp[-2], shp[-1]) for a in (w, m, v))
        g2 = g.reshape(shp[-2], shp[-1])
        d, nm, nv = _adamw(w2, g2, m2, v2, name="adamw_" + name)
        grads.append(g2.reshape(shp))
        deltas.append(d.reshape(shp))
        new_ms.append(nm.reshape(shp))
        new_vs.append(nv.reshape(shp))
    return (loss, res["grad_x"], *grads, *deltas, *new_ms, *new_vs)
```

```python
import jax
import jax.numpy as jnp
from jax import lax
from jax.experimental import pallas as pl
from jax.experimental.pallas import tpu as pltpu

F32 = jnp.float32
BF16 = jnp.bfloat16
MESH = pl.DeviceIdType.MESH

HEAD_DIM = 64
PAIR_W = 2 * HEAD_DIM
BLOCK = 128
A_Q_HEADS = 16
A_KV_HEADS = 2
A_WINDOW = 128
B_PATTERNS = ((128, 1), (512, 4), (2048, 16))
B_GROUP_HEADS = 8
QA_W = A_Q_HEADS * HEAD_DIM
KA_W = A_KV_HEADS * HEAD_DIM
GB_W = B_GROUP_HEADS * HEAD_DIM
QB_W = GB_W * len(B_PATTERNS)
A_W = QA_W + 2 * KA_W
VAR_W = 3 * GB_W
N_VAR = 1 + len(B_PATTERNS)
QKV_P = N_VAR * VAR_W
ROPE_THETA = 10000.0
LN_EPS = 1e-5
NEG_INF = -1e30
DEPTH = 1
ALPHA = (2 * DEPTH) ** 0.25
SCALE = HEAD_DIM ** -0.5

ADAM_LR, ADAM_B1, ADAM_B2, ADAM_EPS, ADAM_WD, ADAM_STEP = 0.001, 0.9, 0.999, 1e-08, 0.01, 10

VMEM_LIMIT_BYTES = 56 * 1024 * 1024


def _pcall(body, **kw):
    return pl.pallas_call(body, **kw)


def _params(sem=None):
    return pltpu.CompilerParams(dimension_semantics=sem, vmem_limit_bytes=VMEM_LIMIT_BYTES)


def _pick(n, target, quantum=128):
    t = (min(target, n) // quantum) * quantum
    while t >= quantum:
        if n % t == 0:
            return t
        t -= quantum
    return n


def _mm(a, b, *, name, ta=False, tb=False, b3=False, out3=0, out_dtype=F32, add=None, tm=1024, tn=1536, tk=1536):
    if ta:
        K, M = a.shape
    else:
        M, K = a.shape
    if b3 and tb:
        Nn, K2, tk = b.shape[1], b.shape[0] * b.shape[2], b.shape[2]
    elif b3:
        K2, Nn, tn = b.shape[1], b.shape[0] * b.shape[2], b.shape[2]
    elif tb:
        Nn, K2 = b.shape
    else:
        K2, Nn = b.shape
    assert K == K2, (a.shape, b.shape)
    if out3:
        tn = Nn // out3
    tm, tn, tk = _pick(M, tm), _pick(Nn, tn), _pick(K, tk)
    nk = K // tk
    dn = (((0 if ta else 1,), (1 if tb else 0,)), ((), ()))

    def body(*refs):
        refs = list(refs)
        a_ref, b_ref = refs[:2]
        add_ref = refs[2] if add is not None else None
        o_ref = refs[3] if add is not None else refs[2]
        part = lax.dot_general(a_ref[...].astype(BF16), b_ref[...].astype(BF16), dn, preferred_element_type=F32)

        def finish(r):
            if add is not None:
                r = r + add_ref[...]
            o_ref[...] = r.astype(out_dtype)

        if nk == 1:
            finish(part)
            return
        acc = refs[-1]
        k = pl.program_id(2)

        @pl.when(k == 0)
        def _():
            acc[...] = part

        @pl.when(k > 0)
        def _():
            acc[...] += part

        @pl.when(k == nk - 1)
        def _():
            finish(acc[...])

    a_spec = pl.BlockSpec((tk, tm), lambda i, j, k: (k, i)) if ta else pl.BlockSpec((tm, tk), lambda i, j, k: (i, k))
    if b3 and tb:
        b_spec = pl.BlockSpec((None, tn, tk), lambda i, j, k: (k, j, 0))
    elif b3:
        b_spec = pl.BlockSpec((None, tk, tn), lambda i, j, k: (j, k, 0))
    elif tb:
        b_spec = pl.BlockSpec((tn, tk), lambda i, j, k: (j, k))
    else:
        b_spec = pl.BlockSpec((tk, tn), lambda i, j, k: (k, j))
    if out3:
        o_spec = pl.BlockSpec((None, tm, tn), lambda i, j, k: (j, i, 0))
    else:
        o_spec = pl.BlockSpec((tm, tn), lambda i, j, k: (i, j))
    ins, specs = [a, b], [a_spec, b_spec]
    if add is not None:
        ins.append(add)
        specs.append(o_spec)
    return _pcall(
        body, name=name, grid=(M // tm, Nn // tn, nk), in_specs=specs, out_specs=o_spec,
        out_shape=jax.ShapeDtypeStruct((out3, M, tn) if out3 else (M, Nn), out_dtype),
        scratch_shapes=[pltpu.VMEM((tm, tn), F32)] if nk > 1 else [],
        compiler_params=_params(("parallel", "parallel", "arbitrary")),
    )(*ins)


def _mm_multi(a_list, b_list, *, name, add=None, out_dtype=F32, tm=512):
    M = a_list[0].shape[0]
    tm = _pick(M, tm)
    ns = len(a_list)
    b_arrs, b_specs = [], []
    for b in b_list:
        arr, shp, idx = b if isinstance(b, tuple) else (b, b.shape, (0, 0))
        b_arrs.append(arr)
        b_specs.append(pl.BlockSpec(shp, lambda i, idx=idx: idx))
    Nn = b_specs[0].block_shape[0]
    dn = (((1,), (1,)), ((), ()))

    def body(*refs):
        a_refs, b_refs = refs[:ns], refs[ns:2 * ns]
        acc = None
        for a_ref, b_ref in zip(a_refs, b_refs):
            part = lax.dot_general(a_ref[...].astype(BF16), b_ref[...], dn, preferred_element_type=F32)
            acc = part if acc is None else acc + part
        if add is not None:
            acc = acc + refs[2 * ns][...]
        refs[-1][...] = acc.astype(out_dtype)

    o_spec = pl.BlockSpec((tm, Nn), lambda i: (i, 0))
    specs = [pl.BlockSpec((tm, a.shape[1]), lambda i: (i, 0)) for a in a_list] + b_specs
    ins = list(a_list) + b_arrs
    if add is not None:
        specs.append(o_spec)
        ins.append(add)
    return _pcall(body, name=name, grid=(M // tm,), in_specs=specs, out_specs=o_spec,
                  out_shape=jax.ShapeDtypeStruct((M, Nn), out_dtype), compiler_params=_params(("parallel",)))(*ins)


def _lane(shape):
    return lax.broadcasted_iota(jnp.int32, shape, len(shape) - 1)


def _rot_half(v):
    w = v.shape[-1]
    first = (_lane(v.shape) % HEAD_DIM) < (HEAD_DIM // 2)
    return jnp.where(first, pltpu.roll(v, w - HEAD_DIM // 2, v.ndim - 1), pltpu.roll(v, HEAD_DIM // 2, v.ndim - 1))


def _widen(t, w):
    return t if w == t.shape[-1] else jnp.concatenate([t] * (w // t.shape[-1]), axis=-1)


def _unrope(v, cos, sins):
    w = v.shape[-1]
    return v * _widen(cos, w) - _rot_half(v) * _widen(sins, w)


def _rope_tables(positions):
    half = HEAD_DIM // 2
    inv = ROPE_THETA ** (-jnp.arange(half, dtype=F32) / half)
    ang = positions.astype(F32)[..., None] * inv
    cos, sin = jnp.cos(ang), jnp.sin(ang)
    cosf = jnp.concatenate([cos, cos, cos, cos], axis=-1)
    sins = jnp.concatenate([-sin, sin, -sin, sin], axis=-1)
    n = positions.shape[0] * positions.shape[1]
    return cosf.reshape(n, PAIR_W), sins.reshape(n, PAIR_W)


def _inproj(x2, scale, shift, w, cosf, sins, flags, *, T, name):
    N, D = x2.shape
    tm, tn = _pick(T, 512), VAR_W
    tpe = T // tm

    def body(x_ref, sc_ref, sh_ref, w_ref, c_ref, s_ref, f_ref, *outs):
        o_refs, u_ref = outs[:N_VAR], outs[N_VAR]
        j = pl.program_id(1)

        @pl.when(j == 0)
        def _():
            u_ref[...] = (x_ref[...] * (1.0 + sc_ref[0]) + sh_ref[0]).astype(BF16)

        acc = jnp.dot(u_ref[...], w_ref[...], preferred_element_type=F32)
        fl = f_ref[...]
        ce = 1.0 + (_widen(c_ref[...], tn) - 1.0) * fl
        se = _widen(s_ref[...], tn) * fl
        res = (acc * ce + _rot_half(acc) * se).astype(BF16)
        for v in range(N_VAR):
            @pl.when(j == v)
            def _(v=v):
                o_refs[v][...] = res

    ex = pl.BlockSpec((1, 1, D), lambda i, j: (i // tpe, 0, 0))
    tab = pl.BlockSpec((tm, PAIR_W), lambda i, j: (i, 0))
    keep = lambda w_: pl.BlockSpec((tm, w_), lambda i, j: (i, 0))
    return _pcall(
        body, name=name, grid=(N // tm, N_VAR),
        in_specs=[keep(D), ex, ex, pl.BlockSpec((D, tn), lambda i, j: (0, j)), tab, tab,
                  pl.BlockSpec((1, tn), lambda i, j: (0, j))],
        out_specs=[keep(tn)] * N_VAR + [keep(D)],
        out_shape=[jax.ShapeDtypeStruct((N, tn), BF16)] * N_VAR + [jax.ShapeDtypeStruct((N, D), BF16)],
        compiler_params=_params(("parallel", "arbitrary")),
    )(x2, scale, shift, w, cosf, sins, flags)


class _Geom:
    def __init__(self, g):
        if g is None:
            self.r, self.nq, self.n_back, self.sink = 1, A_Q_HEADS, A_WINDOW - 1, True
            self.qw, self.kw = QA_W, KA_W
            self.qidx = lambda j: 0
            self.kidx = lambda j: QA_W // KA_W
            self.vidx = lambda j: QA_W // KA_W + 1
        else:
            window, r = B_PATTERNS[g]
            self.r, self.nq, self.n_back, self.sink = r, B_GROUP_HEADS, window // r, False
            self.qw, self.kw = GB_W, GB_W
            self.qidx = lambda j: 3 * j
            self.kidx = lambda j: 3 * j + 1
            self.vidx = lambda j: 3 * j + 2
        self.ntile = self.qw // PAIR_W


def _stack_heads(t):
    first = _lane(t.shape) < HEAD_DIM
    z = jnp.zeros_like(t)
    return jnp.concatenate([jnp.where(first, t, z), jnp.where(first, z, t)], axis=0)


def _unstack_heads(v2):
    return jnp.where(_lane((BLOCK, PAIR_W)) < HEAD_DIM, v2[:BLOCK], v2[BLOCK:])


def _dup_head(t, kh):
    tf = t.astype(F32)
    keep = (_lane(t.shape) < HEAD_DIM) if kh == 0 else (_lane(t.shape) >= HEAD_DIM)
    return jnp.where(keep, tf, pltpu.roll(tf, HEAD_DIM, 1)).astype(t.dtype)


def _fold_heads(t):
    return t + pltpu.roll(t, HEAD_DIM, 1)


def _softmax_parts(s, i, n_back, sinkcol):
    rows = s.shape[0]
    qi = jnp.bitwise_and(lax.broadcasted_iota(jnp.int32, (rows, 2 * BLOCK), 0), BLOCK - 1)
    ki = lax.broadcasted_iota(jnp.int32, (rows, 2 * BLOCK), 1)
    dist = qi + BLOCK - ki
    valid = jnp.logical_and(jnp.logical_and(dist >= 0, dist <= n_back), jnp.logical_or(ki >= BLOCK, i > 0))
    s = jnp.where(valid, s * SCALE, NEG_INF)
    m = jnp.max(s, axis=1, keepdims=True)
    if sinkcol is not None:
        m = jnp.maximum(m, sinkcol)
    p = jnp.exp(s - m)
    den = jnp.sum(p, axis=1, keepdims=True)
    es = None
    if sinkcol is not None:
        es = jnp.exp(sinkcol - m)
        den = den + es
    return p, m, den, es


_NT = (((1,), (1,)), ((), ()))
_TN = (((0,), (0,)), ((), ()))


def _rows2(prev_ref, cur_ref, cs):
    return jnp.concatenate([prev_ref[0, :, cs], cur_ref[0, :, cs]], axis=0)


def _sink_col(sink_ref, kh, nblocks):
    return jnp.concatenate([jnp.full((BLOCK, 1), sink_ref[kh * nblocks + b], F32) for b in range(nblocks)], axis=0)


def _tile(t):
    return slice(t * PAIR_W, (t + 1) * PAIR_W)


def _attn_fwd(qkv, sinks, g, *, NB, T, name):
    geo = _Geom(g)
    r, qw, kw, ntile = geo.r, geo.qw, geo.kw, geo.ntile
    tsub = T // r
    nblk = tsub // BLOCK
    qkv3 = qkv.reshape(NB, tsub, r * VAR_W)
    with_lse = g is not None
    out_dtype = F32 if with_lse else BF16
    tiles_per_kv = ntile // A_KV_HEADS

    def body(q_ref, kp_ref, kc_ref, vp_ref, vc_ref, sink_ref, o_ref, *rest):
        i = pl.program_id(2)
        if geo.sink:
            kall, vall = _rows2(kp_ref, kc_ref, _tile(0)), _rows2(vp_ref, vc_ref, _tile(0))
            for kh in range(A_KV_HEADS):
                tiles = [kh * tiles_per_kv + t for t in range(tiles_per_kv)]
                q2 = jnp.concatenate([_stack_heads(q_ref[0, :, _tile(t)]) for t in tiles], axis=0)
                s = lax.dot_general(q2, _dup_head(kall, kh), _NT, preferred_element_type=F32)
                p, m, den, _ = _softmax_parts(s, i, geo.n_back, _sink_col(sink_ref, kh, 2 * tiles_per_kv))
                o2 = jnp.dot(p.astype(BF16), _dup_head(vall, kh), preferred_element_type=F32) / den
                for n, t in enumerate(tiles):
                    o_ref[0, :, _tile(t)] = _unstack_heads(o2[2 * BLOCK * n:2 * BLOCK * (n + 1)]).astype(out_dtype)
        else:
            l_ref = rest[0]
            for t in range(ntile):
                q2 = _stack_heads(q_ref[0, :, _tile(t)])
                s = lax.dot_general(q2, _rows2(kp_ref, kc_ref, _tile(t)), _NT, preferred_element_type=F32)
                p, m, den, _ = _softmax_parts(s, i, geo.n_back, None)
                o2 = jnp.dot(p.astype(BF16), _rows2(vp_ref, vc_ref, _tile(t)), preferred_element_type=F32) / den
                o_ref[0, :, _tile(t)] = _unstack_heads(o2)
                l_ref[0, :, _tile(t)] = _unstack_heads(jnp.broadcast_to(m + jnp.log(den), (2 * BLOCK, PAIR_W)))

    prev = lambda i: jnp.maximum(i - 1, 0)
    in_specs = [
        pl.BlockSpec((1, BLOCK, qw), lambda b, j, i: (b, i, geo.qidx(j))),
        pl.BlockSpec((1, BLOCK, kw), lambda b, j, i: (b, prev(i), geo.kidx(j))),
        pl.BlockSpec((1, BLOCK, kw), lambda b, j, i: (b, i, geo.kidx(j))),
        pl.BlockSpec((1, BLOCK, kw), lambda b, j, i: (b, prev(i), geo.vidx(j))),
        pl.BlockSpec((1, BLOCK, kw), lambda b, j, i: (b, i, geo.vidx(j))),
        pl.BlockSpec(memory_space=pltpu.SMEM),
    ]
    o_spec = pl.BlockSpec((1, BLOCK, qw), lambda b, j, i: (b, i, j))
    o_shape = jax.ShapeDtypeStruct((NB, tsub, r * qw), out_dtype)
    res = _pcall(
        body, name=name, grid=(NB, r, nblk), in_specs=in_specs,
        out_specs=[o_spec, o_spec] if with_lse else o_spec, out_shape=[o_shape, o_shape] if with_lse else o_shape,
        compiler_params=_params(("parallel", "parallel", "arbitrary")),
    )(qkv3, qkv3, qkv3, qkv3, qkv3, sinks)
    if with_lse:
        return res[0].reshape(NB * T, qw), res[1].reshape(NB * T, qw)
    return res.reshape(NB * T, qw)


def _attn_bwd(qkv, do, dlse, cosf, sins, sinks, g, *, NB, T, name):
    geo = _Geom(g)
    r, qw, kw, ntile = geo.r, geo.qw, geo.kw, geo.ntile
    tsub = T // r
    nblk = tsub // BLOCK
    qkv3 = qkv.reshape(NB, tsub, r * VAR_W)
    do3 = do.reshape(NB, tsub, r * qw)
    cos3 = cosf.reshape(NB, tsub, r * PAIR_W)
    sin3 = sins.reshape(NB, tsub, r * PAIR_W)
    has_dlse = dlse is not None
    tiles_per_kv = ntile // A_KV_HEADS

    def grads(q2, kk, vv, do2, i, sinkcol, dlcol):
        s = lax.dot_general(q2, kk, _NT, preferred_element_type=F32)
        p, m, den, es = _softmax_parts(s, i, geo.n_back, sinkcol)
        inv = 1.0 / den
        p = p * inv
        dp = lax.dot_general(do2, vv, _NT, preferred_element_type=F32)
        delta = jnp.sum(p * dp, axis=1, keepdims=True)
        sk = es * inv * delta if es is not None else None
        if dlcol is not None:
            delta = delta - dlcol
        ds = (p * (dp - delta) * SCALE).astype(BF16)
        dq2 = jnp.dot(ds, kk, preferred_element_type=F32)
        dkk = lax.dot_general(ds, q2, _TN, preferred_element_type=F32)
        dvv = lax.dot_general(p.astype(BF16), do2, _TN, preferred_element_type=F32)
        return dq2, dkk, dvv, sk

    def body(*refs):
        it = iter(refs)
        q_ref, kp_ref, kc_ref, vp_ref, vc_ref, do_ref = (next(it) for _ in range(6))
        dl_ref = next(it) if has_dlse else None
        c_ref, s_ref, sink_ref, o_ref, ds_ref, dq_s, dk_s, dv_s, car_q, car_k, car_v = (next(it) for _ in range(11))
        b, j, i = pl.program_id(0), pl.program_id(1), pl.program_id(2)

        @pl.when(jnp.logical_and(b == 0, jnp.logical_and(j == 0, i == 0)))
        def _():
            ds_ref[...] = jnp.zeros_like(ds_ref)

        @pl.when(i == 0)
        def _():
            car_q[...] = jnp.zeros_like(car_q)
            car_k[...] = jnp.zeros_like(car_k)
            car_v[...] = jnp.zeros_like(car_v)

        @pl.when(i == nblk)
        def _():
            dk_s[...] = jnp.zeros_like(dk_s)
            dv_s[...] = jnp.zeros_like(dv_s)

        @pl.when(i < nblk)
        def _():
            if geo.sink:
                kall, vall = _rows2(kp_ref, kc_ref, _tile(0)), _rows2(vp_ref, vc_ref, _tile(0))
                lane1 = _lane((1, PAIR_W))
                dsink = jnp.zeros((1, PAIR_W), F32)
                dk_t = dv_t = None
                for kh in range(A_KV_HEADS):
                    tiles = [kh * tiles_per_kv + t for t in range(tiles_per_kv)]
                    q2 = jnp.concatenate([_stack_heads(q_ref[0, :, _tile(t)]) for t in tiles], axis=0)
                    do2 = jnp.concatenate([_stack_heads(do_ref[0, :, _tile(t)]) for t in tiles], axis=0)
                    nb = 2 * tiles_per_kv
                    dq2, dkk, dvv, sk = grads(q2, _dup_head(kall, kh), _dup_head(vall, kh), do2, i,
                                              _sink_col(sink_ref, kh, nb), None)
                    for n, t in enumerate(tiles):
                        dq_s[:, _tile(t)] = _unstack_heads(dq2[2 * BLOCK * n:2 * BLOCK * (n + 1)])
                    for bb in range(nb):
                        dsink = dsink + jnp.where(lane1 == kh * nb + bb, -jnp.sum(sk[BLOCK * bb:BLOCK * (bb + 1)]), 0.0)
                    dkk, dvv = _fold_heads(dkk), _fold_heads(dvv)
                    if kh == 0:
                        dk_t, dv_t = dkk, dvv
                    else:
                        second = _lane(dkk.shape) >= HEAD_DIM
                        dk_t, dv_t = jnp.where(second, dkk, dk_t), jnp.where(second, dvv, dv_t)
                dk_s[...] = dk_t
                dv_s[...] = dv_t
                ds_ref[0:1, :] += dsink
            else:
                for t in range(ntile):
                    q2, do2 = _stack_heads(q_ref[0, :, _tile(t)]), _stack_heads(do_ref[0, :, _tile(t)])
                    dlt = dl_ref[0, :, _tile(t)]
                    dlcol = jnp.concatenate([dlt[:, 0:1], dlt[:, HEAD_DIM:HEAD_DIM + 1]], axis=0)
                    dq2, dkk, dvv, _ = grads(q2, _rows2(kp_ref, kc_ref, _tile(t)), _rows2(vp_ref, vc_ref, _tile(t)),
                                             do2, i, None, dlcol)
                    dq_s[:, _tile(t)] = _unstack_heads(dq2)
                    dk_s[:, _tile(t)] = dkk
                    dv_s[:, _tile(t)] = dvv

        cos, sn = c_ref[0], s_ref[0]
        o_ref[0, :, 0:qw] = _unrope(car_q[...], cos, sn).astype(BF16)
        o_ref[0, :, qw:qw + kw] = _unrope(car_k[...] + dk_s[0:BLOCK, :], cos, sn).astype(BF16)
        o_ref[0, :, qw + kw:qw + 2 * kw] = (car_v[...] + dv_s[0:BLOCK, :]).astype(BF16)
        if qw + 2 * kw < VAR_W:
            o_ref[0, :, qw + 2 * kw:VAR_W] = jnp.zeros((BLOCK, VAR_W - qw - 2 * kw), BF16)
        car_q[...] = dq_s[...]
        car_k[...] = dk_s[BLOCK:2 * BLOCK, :]
        car_v[...] = dv_s[BLOCK:2 * BLOCK, :]

    cur = lambda i: jnp.minimum(i, nblk - 1)
    prv = lambda i: jnp.maximum(jnp.minimum(i, nblk - 1) - 1, 0)
    outb = lambda i: jnp.maximum(i - 1, 0)
    in_specs = [
        pl.BlockSpec((1, BLOCK, qw), lambda b, j, i: (b, cur(i), geo.qidx(j))),
        pl.BlockSpec((1, BLOCK, kw), lambda b, j, i: (b, prv(i), geo.kidx(j))),
        pl.BlockSpec((1, BLOCK, kw), lambda b, j, i: (b, cur(i), geo.kidx(j))),
        pl.BlockSpec((1, BLOCK, kw), lambda b, j, i: (b, prv(i), geo.vidx(j))),
        pl.BlockSpec((1, BLOCK, kw), lambda b, j, i: (b, cur(i), geo.vidx(j))),
        pl.BlockSpec((1, BLOCK, qw), lambda b, j, i: (b, cur(i), j)),
    ]
    ins = [qkv3, qkv3, qkv3, qkv3, qkv3, do3]
    if has_dlse:
        in_specs.append(pl.BlockSpec((1, BLOCK, qw), lambda b, j, i: (b, cur(i), j)))
        ins.append(dlse.reshape(NB, tsub, r * qw))
    in_specs += [
        pl.BlockSpec((1, BLOCK, PAIR_W), lambda b, j, i: (b, outb(i), j)),
        pl.BlockSpec((1, BLOCK, PAIR_W), lambda b, j, i: (b, outb(i), j)),
        pl.BlockSpec(memory_space=pltpu.SMEM),
    ]
    ins += [cos3, sin3, sinks]
    scratch = [pltpu.VMEM((BLOCK, qw), F32), pltpu.VMEM((2 * BLOCK, kw), F32), pltpu.VMEM((2 * BLOCK, kw), F32),
               pltpu.VMEM((BLOCK, qw), F32), pltpu.VMEM((BLOCK, kw), F32), pltpu.VMEM((BLOCK, kw), F32)]
    dqkv, dsink = _pcall(
        body, name=name, grid=(NB, r, nblk + 1), in_specs=in_specs,
        out_specs=[pl.BlockSpec((1, BLOCK, VAR_W), lambda b, j, i: (b, outb(i), j)),
                   pl.BlockSpec((8, PAIR_W), lambda b, j, i: (0, 0))],
        out_shape=[jax.ShapeDtypeStruct((NB, tsub, r * VAR_W), BF16), jax.ShapeDtypeStruct((8, PAIR_W), F32)],
        scratch_shapes=scratch, compiler_params=_params(("arbitrary", "arbitrary", "arbitrary")),
    )(*ins)
    return dqkv.reshape(NB * T, VAR_W), dsink


class _Rows:
    def __init__(self, N, T, tm):
        self.N, self.tm, self.tpe, self.grid = N, tm, T // tm, (N // tm,)

    def row(self, w, col=0):
        return pl.BlockSpec((self.tm, w), lambda i: (i, col))

    def ex(self, w):
        return pl.BlockSpec((1, 1, w), lambda i: (i // self.tpe, 0, 0))

    def const(self, shape):
        return pl.BlockSpec(shape, lambda i: tuple(0 for _ in shape))

    def first_of_example(self):
        return pl.program_id(0) % self.tpe == 0


def _acc(ref, first, val):
    @pl.when(first)
    def _():
        ref[0] = val

    @pl.when(jnp.logical_not(first))
    def _():
        ref[0] += val


def _colsum(v):
    return jnp.sum(v, axis=0, keepdims=True)


def _ln_stats(r):
    mu = jnp.mean(r, axis=-1, keepdims=True)
    xc = r - mu
    var = jnp.mean(xc * xc, axis=-1, keepdims=True)
    rstd = lax.rsqrt(var + LN_EPS)
    return xc * rstd, rstd


def _ln_bwd(dy, xhat, rstd, gain):
    dxh = dy * gain
    return rstd * (dxh - jnp.mean(dxh, axis=-1, keepdims=True) - xhat * jnp.mean(dxh * xhat, axis=-1, keepdims=True))


def _silu_parts(v):
    s = jax.nn.sigmoid(v)
    return v * s, s * (1.0 + v * (1.0 - s))


def _local_step(x, mod, positions, w_in, w_a, w_b, w_o, w_gu, w_d, sinks, ln1_g, ln1_b, ln2_g, ln2_b, target):
    NB, T, D = x.shape
    N = NB * T
    F = w_d.shape[0]
    x2 = x.reshape(N, D)
    tgt2 = target.reshape(N, D)
    shift_m, scale_m, gate_m, shift_f, scale_f, gate_f = [mod[:, None, k * D:(k + 1) * D] for k in range(6)]
    cosf, sins = _rope_tables(positions)
    col = jnp.arange(QKV_P)
    vcol = col % VAR_W
    flags = jnp.where(col < VAR_W, vcol < QA_W + KA_W, vcol < 2 * GB_W).astype(F32)[None]
    R = _Rows(N, T, _pick(T, 256))
    sds = jax.ShapeDtypeStruct
    exsum = lambda w=D: sds((NB, 1, w), F32)
    ngrp = len(B_PATTERNS)

    *qkv, u = _inproj(x2, scale_m, shift_m, w_in, cosf, sins, flags, T=T, name="inproj_qkv")
    gates = _mm(u, w_in[:, QKV_P:], name="inproj_gates")
    oa = _attn_fwd(qkv[0], sinks, None, NB=NB, T=T, name="attn_a_fwd")
    ob_parts = [_attn_fwd(qkv[1 + g], sinks, g, NB=NB, T=T, name=f"attn_b{g}_fwd") for g in range(ngrp)]
    (o1, l1), (o2, l2), (o3, l3) = ob_parts

    def merge_fwd(o1r, o2r, o3r, l1r, l2r, l3r, ob_ref):
        la, lb, lc = l1r[...], l2r[...], l3r[...]
        mx = jnp.maximum(jnp.maximum(la, lb), lc)
        ea, eb, ec = jnp.exp(la - mx), jnp.exp(lb - mx), jnp.exp(lc - mx)
        ob_ref[...] = ((ea * o1r[...] + eb * o2r[...] + ec * o3r[...]) / (ea + eb + ec)).astype(BF16)

    ob = _pcall(merge_fwd, name="merge_fwd", grid=R.grid, in_specs=[R.row(GB_W)] * 6, out_specs=R.row(GB_W),
                out_shape=sds((N, GB_W), BF16), compiler_params=_params(("parallel",)))(o1, o2, o3, l1, l2, l3)

    ya = _mm(oa, w_a, name="branch_a")
    yb = _mm(ob, w_b, b3=True, name="branch_b")

    def gate_fwd(ya_r, yb_r, ga_r, gb_r, mg_ref):
        mg_ref[...] = (jax.nn.sigmoid(ga_r[...]) * ya_r[...] + jax.nn.sigmoid(gb_r[...]) * yb_r[...]).astype(BF16)

    merged = _pcall(gate_fwd, name="gate_fwd", grid=R.grid, in_specs=[R.row(D), R.row(D), R.row(D, 0), R.row(D, 1)],
                    out_specs=R.row(D), out_shape=sds((N, D), BF16),
                    compiler_params=_params(("parallel",)))(ya, yb, gates, gates)
    y = _mm(merged, w_o, name="out_proj")

    def norm1_fwd(x_r, y_r, gm_r, g_r, b_r, sf_r, hf_r, r1_ref, x1_ref, u2_ref):
        r1 = ALPHA * x_r[...] + (1.0 + gm_r[0]) * y_r[...]
        xhat, _ = _ln_stats(r1)
        x1 = xhat * g_r[...] + b_r[...]
        r1_ref[...] = r1
        x1_ref[...] = x1
        u2_ref[...] = (x1 * (1.0 + sf_r[0]) + hf_r[0]).astype(BF16)

    r1, x1, u2 = _pcall(
        norm1_fwd, name="norm1_fwd", grid=R.grid,
        in_specs=[R.row(D), R.row(D), R.ex(D), R.const((1, D)), R.const((1, D)), R.ex(D), R.ex(D)],
        out_specs=[R.row(D)] * 3, out_shape=[sds((N, D), F32), sds((N, D), F32), sds((N, D), BF16)],
        compiler_params=_params(("parallel",)))(x2, y, gate_m, ln1_g, ln1_b, scale_f, shift_f)

    tnf = w_gu.shape[2]
    nft = w_gu.shape[0] // 2
    tmf = _pick(N, 512)

    def ffn_up(u_r, wg_r, wu_r, hg_ref, hu_ref, a_ref):
        hg = jnp.dot(u_r[...], wg_r[...], preferred_element_type=F32)
        hu = jnp.dot(u_r[...], wu_r[...], preferred_element_type=F32)
        sl, _ = _silu_parts(hg)
        hg_ref[...] = hg.astype(BF16)
        hu_ref[...] = hu.astype(BF16)
        a_ref[...] = (sl * hu).astype(BF16)

    ftile = pl.BlockSpec((tmf, tnf), lambda i, j: (i, j))
    hg, hu, act = _pcall(
        ffn_up, name="ffn_up", grid=(N // tmf, nft),
        in_specs=[pl.BlockSpec((tmf, D), lambda i, j: (i, 0)), pl.BlockSpec((None, D, tnf), lambda i, j: (j, 0, 0)),
                  pl.BlockSpec((None, D, tnf), lambda i, j: (j + nft, 0, 0))],
        out_specs=[ftile] * 3, out_shape=[sds((N, F), BF16)] * 3,
        compiler_params=_params(("parallel", "arbitrary")))(u2, w_gu, w_gu)
    y2 = _mm(act, w_d, name="ffn_down")

    def norm2_loss_bwd(x1_r, y2_r, t_r, gf_r, g_r, b_r, dy2_ref, dx1_ref, dgf_ref, dg_ref, db_ref, loss_ref):
        first = R.first_of_example()
        y2v = y2_r[...]
        r2 = ALPHA * x1_r[...] + (1.0 + gf_r[0]) * y2v
        xhat, rstd = _ln_stats(r2)
        err = xhat * g_r[...] + b_r[...] - t_r[...]
        dx2 = err * (1.0 / D)
        dr2 = _ln_bwd(dx2, xhat, rstd, g_r[...])
        dy2_ref[...] = ((1.0 + gf_r[0]) * dr2).astype(BF16)
        dx1_ref[...] = ALPHA * dr2
        _acc(dgf_ref, first, _colsum(dr2 * y2v))
        _acc(dg_ref, first, _colsum(dx2 * xhat))
        _acc(db_ref, first, _colsum(dx2))
        part = 0.5 * jnp.sum(jnp.mean(err * err, axis=-1, keepdims=True))
        _acc(loss_ref, first, jnp.broadcast_to(part, (1, 128)))

    dy2, dx1p, dgate_f, dg2, db2, loss_p = _pcall(
        norm2_loss_bwd, name="norm2_loss_bwd", grid=R.grid,
        in_specs=[R.row(D), R.row(D), R.row(D), R.ex(D), R.const((1, D)), R.const((1, D))],
        out_specs=[R.row(D), R.row(D), R.ex(D), R.ex(D), R.ex(D), R.ex(128)],
        out_shape=[sds((N, D), BF16), sds((N, D), F32), exsum(), exsum(), exsum(), exsum(128)],
        compiler_params=_params(("arbitrary",)))(x1, y2, tgt2, gate_f, ln2_g, ln2_b)

    g_wd = _mm(act, dy2, ta=True, out_dtype=BF16, name="ffn_down_dw")

    tmd = _pick(N, 256)

    def ffn_down_dx(dy_r, wd_r, hg_r, hu_r, dh_ref):
        for t in range(nft):
            cs = slice(t * tnf, (t + 1) * tnf)
            da = lax.dot_general(dy_r[...], wd_r[cs, :], _NT, preferred_element_type=F32)
            sl, dsl = _silu_parts(hg_r[:, cs].astype(F32))
            dh_ref[:, cs] = (da * hu_r[:, cs].astype(F32) * dsl).astype(BF16)
            dh_ref[:, F + t * tnf:F + (t + 1) * tnf] = (da * sl).astype(BF16)

    rowd = lambda w_: pl.BlockSpec((tmd, w_), lambda i: (i, 0))
    dh = _pcall(
        ffn_down_dx, name="ffn_down_dx", grid=(N // tmd,),
        in_specs=[rowd(D), pl.BlockSpec((F, D), lambda i: (0, 0)), rowd(F), rowd(F)],
        out_specs=rowd(2 * F), out_shape=sds((N, 2 * F), BF16),
        compiler_params=_params(("parallel",)))(dy2, w_d, hg, hu)
    du2 = _mm(dh, w_gu, tb=True, b3=True, name="ffn_up_dx")
    g_wgu = _mm(u2, dh, ta=True, out3=w_gu.shape[0], out_dtype=BF16, name="ffn_up_dw")

    def norm1_bwd(dx1p_r, du2_r, x1_r, r1_r, y_r, sf_r, gm_r, g_r,
                  dxp_ref, dy_ref, dsf_ref, dhf_ref, dgm_ref, dg_ref, db_ref):
        first = R.first_of_example()
        du2v = du2_r[...]
        dx1 = dx1p_r[...] + du2v * (1.0 + sf_r[0])
        xhat, rstd = _ln_stats(r1_r[...])
        dr1 = _ln_bwd(dx1, xhat, rstd, g_r[...])
        dxp_ref[...] = ALPHA * dr1
        dy_ref[...] = ((1.0 + gm_r[0]) * dr1).astype(BF16)
        _acc(dsf_ref, first, _colsum(du2v * x1_r[...]))
        _acc(dhf_ref, first, _colsum(du2v))
        _acc(dgm_ref, first, _colsum(dr1 * y_r[...]))
        _acc(dg_ref, first, _colsum(dx1 * xhat))
        _acc(db_ref, first, _colsum(dx1))

    dxp, dy, dscale_f, dshift_f, dgate_m, dg1, db1 = _pcall(
        norm1_bwd, name="norm1_bwd", grid=R.grid,
        in_specs=[R.row(D)] * 5 + [R.ex(D), R.ex(D), R.const((1, D))],
        out_specs=[R.row(D), R.row(D)] + [R.ex(D)] * 5,
        out_shape=[sds((N, D), F32), sds((N, D), BF16)] + [exsum()] * 5,
        compiler_params=_params(("arbitrary",)))(dx1p, du2, x1, r1, y, scale_f, gate_m, ln1_g)

    dmerged = _mm(dy, w_o, tb=True, name="out_proj_dx")
    g_wo = _mm(merged, dy, ta=True, out_dtype=BF16, name="out_proj_dw")

    def gate_bwd(dm_r, ya_r, yb_r, ga_r, gb_r, dya_ref, dyb_ref, dg_ref):
        dm = dm_r[...]
        sa, sb = jax.nn.sigmoid(ga_r[...]), jax.nn.sigmoid(gb_r[...])
        dya_ref[...] = (dm * sa).astype(BF16)
        dyb_ref[...] = (dm * sb).astype(BF16)
        dg_ref[:, :D] = (dm * ya_r[...] * sa * (1.0 - sa)).astype(BF16)
        dg_ref[:, D:] = (dm * yb_r[...] * sb * (1.0 - sb)).astype(BF16)

    dya, dyb, dgates = _pcall(
        gate_bwd, name="gate_bwd", grid=R.grid, in_specs=[R.row(D)] * 3 + [R.row(D, 0), R.row(D, 1)],
        out_specs=[R.row(D), R.row(D), R.row(2 * D)],
        out_shape=[sds((N, D), BF16), sds((N, D), BF16), sds((N, 2 * D), BF16)],
        compiler_params=_params(("parallel",)))(dmerged, ya, yb, gates, gates)

    doa = _mm(dya, w_a, tb=True, out_dtype=BF16, name="branch_a_dx")
    g_wa = _mm(oa, dya, ta=True, out_dtype=BF16, name="branch_a_dw")
    dob = _mm(dyb, w_b, tb=True, b3=True, name="branch_b_dx")
    g_wb = _mm(ob, dyb, ta=True, out3=w_b.shape[0], out_dtype=BF16, name="branch_b_dw")

    seg = (jnp.arange(GB_W)[:, None] // HEAD_DIM == jnp.arange(GB_W)[None, :] // HEAD_DIM).astype(BF16)

    def merge_bwd(dob_r, o1r, o2r, o3r, l1r, l2r, l3r, seg_r, d1, d2, d3, e1, e2, e3):
        dob_v = dob_r[...]
        la, lb, lc = l1r[...], l2r[...], l3r[...]
        mx = jnp.maximum(jnp.maximum(la, lb), lc)
        ea, eb, ec = jnp.exp(la - mx), jnp.exp(lb - mx), jnp.exp(lc - mx)
        inv = 1.0 / (ea + eb + ec)
        ws = [ea * inv, eb * inv, ec * inv]

        def headsum(v):
            hi = v.astype(BF16)
            r1_ = v - hi.astype(F32)
            mid = r1_.astype(BF16)
            lo = (r1_ - mid.astype(F32)).astype(BF16)
            sm = seg_r[...]
            return (jnp.dot(hi, sm, preferred_element_type=F32) + jnp.dot(mid, sm, preferred_element_type=F32)
                    + jnp.dot(lo, sm, preferred_element_type=F32))

        dws = [headsum(dob_v * o[...]) for o in (o1r, o2r, o3r)]
        mean = ws[0] * dws[0] + ws[1] * dws[1] + ws[2] * dws[2]
        for w_, dw_, d_ref, e_ref in zip(ws, dws, (d1, d2, d3), (e1, e2, e3)):
            d_ref[...] = (w_ * dob_v).astype(BF16)
            e_ref[...] = w_ * (dw_ - mean)

    mb = _pcall(
        merge_bwd, name="merge_bwd", grid=R.grid, in_specs=[R.row(GB_W)] * 7 + [R.const((GB_W, GB_W))],
        out_specs=[R.row(GB_W)] * 6, out_shape=[sds((N, GB_W), BF16)] * 3 + [sds((N, GB_W), F32)] * 3,
        compiler_params=_params(("parallel",)))(dob, o1, o2, o3, l1, l2, l3, seg)
    do_b, dlse_b = mb[:3], mb[3:]

    dqkv_a, dsink = _attn_bwd(qkv[0], doa, None, cosf, sins, sinks, None, NB=NB, T=T, name="attn_a_bwd")
    dqkv = [dqkv_a] + [_attn_bwd(qkv[1 + g], do_b[g], dlse_b[g], cosf, sins, sinks, g, NB=NB, T=T,
                                 name=f"attn_b{g}_bwd")[0] for g in range(ngrp)]

    wvar = lambda v: (w_in, (D, VAR_W), (0, v))
    du = _mm_multi(dqkv[:3], [wvar(v) for v in range(3)], name="inproj_dx0")
    du = _mm_multi([dqkv[3], dgates], [wvar(3), (w_in, (D, 2 * D), (0, QKV_P // (2 * D)))], add=du, name="inproj_dx1")
    g_win = [_mm(u, dseg, ta=True, out_dtype=BF16, name=f"inproj_dw{n}") for n, dseg in enumerate(dqkv + [dgates])]

    def x_bwd(dxp_r, du_r, x_r, sm_r, gx_ref, dsm_ref, dhm_ref):
        first = R.first_of_example()
        duv = du_r[...]
        gx_ref[...] = dxp_r[...] + duv * (1.0 + sm_r[0])
        _acc(dsm_ref, first, _colsum(duv * x_r[...]))
        _acc(dhm_ref, first, _colsum(duv))

    gx, dscale_m, dshift_m = _pcall(
        x_bwd, name="x_bwd", grid=R.grid, in_specs=[R.row(D)] * 3 + [R.ex(D)],
        out_specs=[R.row(D), R.ex(D), R.ex(D)], out_shape=[sds((N, D), F32), exsum(), exsum()],
        compiler_params=_params(("arbitrary",)))(dxp, du, x2, scale_m)

    dmod = jnp.concatenate([dshift_m, dscale_m, dgate_m, dshift_f, dscale_f, dgate_f], axis=-1)[:, 0]
    ln_grads = jnp.concatenate([dg1, db1, dg2, db2], axis=1)
    return dict(loss=loss_p[:, 0, 0], grad_x=gx.reshape(NB, T, D), g_win=g_win, g_wa=g_wa, g_wb=g_wb, g_wo=g_wo,
                g_wgu=g_wgu, g_wd=g_wd, dmod=dmod, ln_grads=ln_grads, dsink=dsink[0, :A_Q_HEADS])


def _coords():
    return lax.axis_index("x"), lax.axis_index("y"), lax.axis_index("c")


def _allgather_small(blk, *, name):
    m_per, n = blk.shape

    def body(x_ref, out_ref, send_sems, recv_sems, local_sem):
        x, y, c = _coords()
        me, sibling = (x, y, c), (x, y, 1 - c)
        chips = [(1 - x, y), (x, 1 - y), (1 - x, 1 - y)]

        def rows(px, py, pc):
            return out_ref.at[pl.ds((4 * px + 2 * py + pc) * m_per, m_per), :]

        def copy(k, block, to, src=None):
            return pltpu.make_async_remote_copy(
                src_ref=rows(*block) if src is None else src, dst_ref=rows(*block),
                send_sem=send_sems.at[k], recv_sem=recv_sems.at[k], device_id=to, device_id_type=MESH)

        mine = pltpu.make_async_copy(x_ref, rows(*me), local_sem)
        mine.start()
        first = [copy(0, me, sibling, src=x_ref)]
        first += [copy(1 + j, me, (*chip, c), src=x_ref) for j, chip in enumerate(chips)]
        for cp in first:
            cp.start()
        passed = [copy(4 + j, (*chip, c), sibling) for j, chip in enumerate(chips)]
        for j, chip in enumerate(chips):
            copy(1 + j, (*chip, c), me).wait_recv()
            passed[j].start()
        copy(0, sibling, me).wait_recv()
        for j, chip in enumerate(chips):
            copy(4 + j, (*chip, 1 - c), me).wait_recv()
        for cp in first + passed:
            cp.wait_send()
        mine.wait()

    return _pcall(
        body, name=name, out_shape=jax.ShapeDtypeStruct((8 * m_per, n), blk.dtype),
        in_specs=[pl.BlockSpec(memory_space=pltpu.VMEM)], out_specs=pl.BlockSpec(memory_space=pltpu.VMEM),
        scratch_shapes=[pltpu.SemaphoreType.DMA((7,)), pltpu.SemaphoreType.DMA((7,)), pltpu.SemaphoreType.DMA],
        compiler_params=pltpu.CompilerParams(vmem_limit_bytes=VMEM_LIMIT_BYTES),
    )(blk)


def _exchange(srcs, dsts, plan, *, name, dst_inits=None):
    na = len(dsts)
    nrem = len(plan(0, 0, 0))

    def body(*refs):
        refs = list(refs)
        src_refs = [refs.pop(0) for _ in range(na)] if srcs is not None else None
        if dst_inits is not None:
            del refs[:na]
        dst_refs, (send_sems, recv_sems) = refs[:na], refs[na:]
        if src_refs is None:
            src_refs = dst_refs
        x, y, c = _coords()
        remote = plan(x, y, c)
        at = lambda ref, idx: ref.at[idx] if idx else ref

        def copy(a, n, landing):
            si, di, ri, peer = remote[n]
            return pltpu.make_async_remote_copy(
                src_ref=at(src_refs[a], si), dst_ref=at(dst_refs[a], ri if landing else di),
                send_sem=send_sems.at[a * nrem + n], recv_sem=recv_sems.at[a * nrem + n],
                device_id=peer, device_id_type=MESH)

        order = [(a, n) for a in range(na) for n in range(nrem)]
        sends = [copy(a, n, False) for a, n in order]
        for cp in sends:
            cp.start()
        for a, n in order:
            copy(a, n, True).wait_recv()
        for cp in sends:
            cp.wait_send()

    hbm = pl.BlockSpec(memory_space=pl.ANY)
    ins = (list(srcs) if srcs is not None else []) + (list(dst_inits) if dst_inits is not None else [])
    base = na if srcs is not None else 0
    aliases = {base + a: a for a in range(na)} if dst_inits is not None else {}
    return _pcall(
        body, name=name, out_shape=list(dsts), in_specs=[hbm] * len(ins), out_specs=[hbm] * na,
        input_output_aliases=aliases,
        scratch_shapes=[pltpu.SemaphoreType.DMA((na * nrem,)), pltpu.SemaphoreType.DMA((na * nrem,))],
    )(*ins)


def _other_chips(x, y):
    return [(1 - x, y), (x, 1 - y), (1 - x, 1 - y)]


def _gather_weights(shards, chip):
    def plan_ici(x, y, c):
        k = 2 * x + y
        return [((c,), (k, c), (2 * px + py, c), (px, py, c)) for px, py in _other_chips(x, y)]

    def plan_d2d(x, y, c):
        return [((2 * px + py, c), (2 * px + py, c), (2 * px + py, 1 - c), (x, y, 1 - c))
                for px, py in _other_chips(x, y)]

    dsts = [jax.ShapeDtypeStruct((4,) + s.shape, s.dtype) for s in shards]
    full = _exchange(shards, dsts, plan_ici, name="gather_w_ici")
    full = _exchange(None, dsts, plan_d2d, name="gather_w_d2d", dst_inits=full)
    return [lax.dynamic_update_index_in_dim(f, s, chip, 0) for f, s in zip(full, shards)]


def _add_pairs(a, b, *, name):
    s, hr, wd = a.shape
    tr = _pick(hr, 600, 16)

    def body(a_ref, b_ref, o_ref):
        o_ref[...] = (a_ref[...].astype(F32) + b_ref[...].astype(F32)).astype(BF16)

    spec = pl.BlockSpec((1, tr, wd), lambda j, i: (j, i, 0))
    return _pcall(body, name=name, grid=(s, hr // tr), in_specs=[spec, spec], out_specs=spec,
                  out_shape=jax.ShapeDtypeStruct(a.shape, BF16), compiler_params=_params(("parallel", "parallel")))(a, b)


def _sum_chips(b, *, name):
    s, hr, wd = b.shape
    tr = _pick(hr, 600, 16)

    def body(b_ref, o_ref):
        acc = b_ref[0].astype(F32)
        for k in range(1, s):
            acc = acc + b_ref[k].astype(F32)
        o_ref[...] = acc

    return _pcall(body, name=name, grid=(hr // tr,), in_specs=[pl.BlockSpec((s, tr, wd), lambda i: (0, i, 0))],
                  out_specs=pl.BlockSpec((tr, wd), lambda i: (i, 0)), out_shape=jax.ShapeDtypeStruct((hr, wd), F32),
                  compiler_params=_params(("parallel",)))(b)


def _reduce_scatter_grads(gs, chip, ci):
    def plan_pair(x, y, c):
        return [((slice(None), 1 - c), (), (), (x, y, 1 - c))]

    def plan_chips(x, y, c):
        k = 2 * x + y
        return [((2 * px + py,), (k,), (2 * px + py,), (px, py, c)) for px, py in _other_chips(x, y)]

    def plan_halves(x, y, c):
        return [((), (c,), (1 - c,), (x, y, 1 - c))]

    sds = jax.ShapeDtypeStruct
    halves = [sds((g.shape[0],) + g.shape[2:], BF16) for g in gs]
    from_sibling = _exchange(gs, halves, plan_pair, name="reduce_g_pair")
    pairs = [_add_pairs(lax.dynamic_index_in_dim(g, ci, 1, keepdims=False), f, name=f"reduce_g_pair_add{n}")
             for n, (g, f) in enumerate(zip(gs, from_sibling))]
    landed = _exchange(pairs, halves, plan_chips, name="reduce_g_chips")
    mine = []
    for n, (l, p) in enumerate(zip(landed, pairs)):
        own = lax.dynamic_index_in_dim(p, chip, 0, keepdims=False)
        mine.append(_sum_chips(lax.dynamic_update_index_in_dim(l, own, chip, 0), name=f"reduce_g_chip_sum{n}"))
    both = _exchange(mine, [sds((2,) + m.shape, F32) for m in mine], plan_halves, name="reduce_g_halves")
    return [lax.dynamic_update_index_in_dim(b, m, ci, 0).reshape(2 * m.shape[0], m.shape[1])
            for b, m in zip(both, mine)]


def _ada_fwd(c_all, w_sh, b_sh, *, name):
    nb, d = c_all.shape
    wcols = w_sh.shape[1]
    tn = _pick(wcols, 512)

    def body(c_ref, w_ref, b_ref, o_ref, a_ref):
        cv = c_ref[...]
        act = cv * jax.nn.sigmoid(cv)
        a_ref[...] = act
        o_ref[...] = jnp.dot(act.astype(BF16), w_ref[...].astype(BF16), preferred_element_type=F32) + b_ref[...]

    return _pcall(
        body, name=name, grid=(wcols // tn,),
        in_specs=[pl.BlockSpec((nb, d), lambda j: (0, 0)), pl.BlockSpec((d, tn), lambda j: (0, j)),
                  pl.BlockSpec((1, tn), lambda j: (0, j))],
        out_specs=[pl.BlockSpec((nb, tn), lambda j: (0, j)), pl.BlockSpec((nb, d), lambda j: (0, 0))],
        out_shape=[jax.ShapeDtypeStruct((nb, wcols), F32), jax.ShapeDtypeStruct((nb, d), F32)],
        compiler_params=_params(("arbitrary",)))(c_all, w_sh, b_sh)


def _sum_devices(g, *, name):
    nd, m, w = g.shape

    def body(g_ref, o_ref):
        acc = g_ref[0]
        for k in range(1, nd):
            acc = acc + g_ref[k]
        o_ref[...] = acc

    return _pcall(body, name=name, out_shape=jax.ShapeDtypeStruct((m, w), F32),
                  compiler_params=pltpu.CompilerParams(vmem_limit_bytes=VMEM_LIMIT_BYTES))(g)


def _adamw(w, g, m, v, *, name):
    rows, cols = w.shape
    tr = _pick(rows, max(8, (1 << 18) // cols), 8)
    c1 = 1.0 / (1.0 - ADAM_B1 ** ADAM_STEP)
    c2 = 1.0 / (1.0 - ADAM_B2 ** ADAM_STEP)

    def body(w_ref, g_ref, m_ref, v_ref, d_ref, nm_ref, nv_ref):
        gv = g_ref[...]
        nm = ADAM_B1 * m_ref[...] + (1.0 - ADAM_B1) * gv
        nv = ADAM_B2 * v_ref[...] + (1.0 - ADAM_B2) * (gv * gv)
        d_ref[...] = -ADAM_LR * ((nm * c1) / (jnp.sqrt(nv * c2) + ADAM_EPS) + ADAM_WD * w_ref[...])
        nm_ref[...] = nm
        nv_ref[...] = nv

    spec = pl.BlockSpec((tr, cols), lambda i: (i, 0))
    shp = jax.ShapeDtypeStruct((rows, cols), F32)
    return _pcall(body, name=name, grid=(rows // tr,), in_specs=[spec] * 4, out_specs=[spec] * 3,
                  out_shape=[shp] * 3, compiler_params=_params(("parallel",)))(w, g, m, v)


def _permute_in_cols(w):
    ngrp = len(B_PATTERNS)
    qb, kb, vb = (w[:, A_W + n * QB_W:A_W + (n + 1) * QB_W] for n in range(3))
    parts = [w[:, :A_W], jnp.zeros((w.shape[0], VAR_W - A_W), w.dtype)]
    for g in range(ngrp):
        parts += [t[:, g * GB_W:(g + 1) * GB_W] for t in (qb, kb, vb)]
    return jnp.concatenate(parts + [w[:, A_W + 3 * QB_W:]], axis=1)


def _unpermute_in_grads(pieces):
    ga, groups, gg = pieces[0], pieces[1:-1], pieces[-1]
    cols = [ga[:, :A_W]]
    for n in range(3):
        cols += [gp[:, n * GB_W:(n + 1) * GB_W] for gp in groups]
    return jnp.concatenate(cols + [gg], axis=1)


def kernel(x, c, positions, w_ada, b_ada, w_in, sinks, w_branch_a, w_branch_b, w_o, ln1_g, ln1_b, w_gate_up, w_down, ln2_g, ln2_b, loss_target, m_w_ada, m_b_ada, m_w_in, m_sinks, m_w_branch_a, m_w_branch_b, m_w_o, m_ln1_g, m_ln1_b, m_w_gate_up, m_w_down, m_ln2_g, m_ln2_b, v_w_ada, v_b_ada, v_w_in, v_sinks, v_w_branch_a, v_w_branch_b, v_w_o, v_ln1_g, v_ln1_b, v_w_gate_up, v_w_down, v_ln2_g, v_ln2_b):
    xi, yi, ci = _coords()
    chip = 2 * xi + yi
    dev = 4 * xi + 2 * yi + ci
    NB, T, D = x.shape
    nchip, ndev = 4, 8
    ada_cols = w_ada.shape[2]

    c_blk = jnp.zeros((8, D), F32).at[:NB].set(c)
    c_all = _allgather_small(c_blk, name="gather_c").reshape(ndev, 8, D)[:, :NB].reshape(ndev * NB, D)
    b_sh = lax.dynamic_slice(b_ada, (0, chip * ada_cols), (1, ada_cols))
    mod_part, c_act = _ada_fwd(c_all, w_ada[0], b_sh, name="ada_fwd")
    mod_g = _allgather_small(mod_part, name="gather_mod").reshape(nchip, 2, ndev * NB, ada_cols)[:, 0]
    mod_all = jnp.transpose(mod_g, (1, 0, 2)).reshape(ndev * NB, nchip * ada_cols)
    mod = lax.dynamic_slice(mod_all, (NB * dev, 0), (NB, nchip * ada_cols))

    ra, ro, rd = w_branch_a.shape[1], w_o.shape[1], w_down.shape[1]
    rowsh = jnp.concatenate([w_branch_a[0], w_o[0], w_down[0]], axis=0)
    halves = lambda a: a.reshape(a.shape[:-2] + (2, a.shape[-2] // 2, a.shape[-1]))
    whole = lambda a: a.reshape(a.shape[:-3] + (2 * a.shape[-2], a.shape[-1]))
    shards = [halves(w.astype(BF16)) for w in (w_in[0], rowsh, w_branch_b[0], w_gate_up[0])]
    g_in, g_rows, w_b_f, w_gu_f = (whole(a) for a in _gather_weights(shards, chip))
    w_in_f = _permute_in_cols(jnp.concatenate([g_in[k] for k in range(nchip)], axis=1))
    w_a_f = g_rows[:, :ra].reshape(nchip * ra, D)
    w_o_f = g_rows[:, ra:ra + ro].reshape(nchip * ro, D)
    w_d_f = g_rows[:, ra + ro:].reshape(nchip * rd, D)

    res = _local_step(x, mod, positions, w_in_f, w_a_f, w_b_f, w_o_f, w_gu_f, w_d_f, sinks[0],
                      ln1_g, ln1_b, ln2_g, ln2_b, loss_target)

    gr_in = jnp.stack(jnp.split(_unpermute_in_grads(res["g_win"]), nchip, axis=1))
    gr_rows = jnp.concatenate([res["g_wa"].reshape(nchip, ra, D), res["g_wo"].reshape(nchip, ro, D),
                               res["g_wd"].reshape(nchip, rd, D)], axis=1)
    reduced = _reduce_scatter_grads([halves(a) for a in (gr_in, gr_rows, res["g_wb"], res["g_wgu"])], chip, ci)
    g_w_in, g_rows_red, g_w_b, g_w_gu = reduced
    g_w_a, g_w_o, g_w_d = g_rows_red[:ra], g_rows_red[ra:ra + ro], g_rows_red[ra + ro:]

    small_rows = 24
    misc = jnp.zeros((1, D), F32).at[0, :A_Q_HEADS].set(res["dsink"]).at[0, A_Q_HEADS].set(jnp.sum(res["loss"]))
    small = jnp.concatenate([res["dmod"].reshape(NB * 6, D), jnp.sum(res["ln_grads"], axis=0), misc,
                             jnp.zeros((small_rows - NB * 6 - 5, D), F32)], axis=0)
    small_all = _allgather_small(small, name="gather_small").reshape(ndev, small_rows, D)
    dmod_all = small_all[:, :NB * 6].reshape(ndev * NB, 6 * D)
    sums = _sum_devices(small_all, name="sum_small")
    g_b_ada = (sums[0:6] + sums[6:12]).reshape(1, 6 * D)
    g_ln1_g, g_ln1_b, g_ln2_g, g_ln2_b = (sums[12 + n][None] for n in range(4))
    g_sinks = sums[16, :A_Q_HEADS][None]
    loss = sums[16, A_Q_HEADS]
    dmod_sh = lax.dynamic_slice(dmod_all, (0, chip * ada_cols), (ndev * NB, ada_cols))
    g_w_ada = _mm(c_act, dmod_sh, ta=True, name="ada_dw")

    names = ["w_ada", "b_ada", "w_in", "sinks", "w_branch_a", "w_branch_b", "w_o", "ln1_g", "ln1_b",
             "w_gate_up", "w_down", "ln2_g", "ln2_b"]
    ws = [w_ada, b_ada, w_in, sinks, w_branch_a, w_branch_b, w_o, ln1_g, ln1_b, w_gate_up, w_down, ln2_g, ln2_b]
    ms = [m_w_ada, m_b_ada, m_w_in, m_sinks, m_w_branch_a, m_w_branch_b, m_w_o, m_ln1_g, m_ln1_b, m_w_gate_up,
          m_w_down, m_ln2_g, m_ln2_b]
    vs = [v_w_ada, v_b_ada, v_w_in, v_sinks, v_w_branch_a, v_w_branch_b, v_w_o, v_ln1_g, v_ln1_b, v_w_gate_up,
          v_w_down, v_ln2_g, v_ln2_b]
    gs = [g_w_ada, g_b_ada, g_w_in, g_sinks, g_w_a, g_w_b, g_w_o, g_ln1_g, g_ln1_b, g_w_gu, g_w_d, g_ln2_g, g_ln2_b]
    grads, deltas, new_ms, new_vs = [], [], [], []
    for name, w, g, m, v in zip(names, ws, gs, ms, vs):
        shp = w.shape
        w2, m2, v2 = (a.reshape(shp[-2], shp[-1]) for a in (w, m, v))
        g2 = g.reshape(shp[-2], shp[-1])
        d, nm, nv = _adamw(w2, g2, m2, v2, name="adamw_" + name)
        grads.append(g2.reshape(shp))
        deltas.append(d.reshape(shp))
        new_ms.append(nm.reshape(shp))
        new_vs.append(nv.reshape(shp))
    return (loss, res["grad_x"], *grads, *deltas, *new_ms, *new_vs)
```

```python
import functools

import jax
import jax.numpy as jnp
from jax import lax
from jax.experimental import pallas as pl
from jax.experimental.pallas import tpu as pltpu

F32 = jnp.float32
BF16 = jnp.bfloat16
MESH = pl.DeviceIdType.MESH

HEAD_DIM = 64
PAIR_W = 2 * HEAD_DIM
BLOCK = 128
A_Q_HEADS = 16
A_KV_HEADS = 2
A_WINDOW = 128
B_PATTERNS = ((128, 1), (512, 4), (2048, 16))
B_GROUP_HEADS = 8
QA_W = A_Q_HEADS * HEAD_DIM
KA_W = A_KV_HEADS * HEAD_DIM
GB_W = B_GROUP_HEADS * HEAD_DIM
QB_W = GB_W * len(B_PATTERNS)
A_W = QA_W + 2 * KA_W
VAR_W = 3 * GB_W
N_VAR = 1 + len(B_PATTERNS)
QKV_P = N_VAR * VAR_W
ROPE_THETA = 10000.0
LN_EPS = 1e-5
NEG_INF = -1e30
DEPTH = 1
ALPHA = (2 * DEPTH) ** 0.25
SCALE = HEAD_DIM ** -0.5

ADAM_LR, ADAM_B1, ADAM_B2, ADAM_EPS, ADAM_WD, ADAM_STEP = 0.001, 0.9, 0.999, 1e-08, 0.01, 10

VMEM_LIMIT_BYTES = 56 * 1024 * 1024


def _params(sem=None):
    return pltpu.CompilerParams(dimension_semantics=sem, vmem_limit_bytes=VMEM_LIMIT_BYTES)


_RIDES = {}


def _pcall(body, *, name, **kw):
    ride = _RIDES.pop(name, None)
    if ride is None:
        return pl.pallas_call(body, name=name, **kw)
    return _riding_call(body, ride, name=name, **kw)


def _copies(src_refs, dst_refs, send_sems, recv_sems, plan):
    x, y, c = lax.axis_index("x"), lax.axis_index("y"), lax.axis_index("c")
    remote = plan(x, y, c)
    nrem = len(remote)
    at = lambda ref, idx: ref.at[idx] if idx else ref

    def copy(a, n, landing):
        si, di, ri, peer = remote[n]
        return pltpu.make_async_remote_copy(
            src_ref=at(src_refs[a], si), dst_ref=at(dst_refs[a], ri if landing else di),
            send_sem=send_sems.at[a * nrem + n], recv_sem=recv_sems.at[a * nrem + n],
            device_id=peer, device_id_type=MESH)

    order = [(a, n) for a in range(len(dst_refs)) for n in range(nrem)]

    def start():
        for a, n in order:
            copy(a, n, False).start()

    def wait():
        for a, n in order:
            copy(a, n, True).wait_recv()
        for a, n in order:
            copy(a, n, False).wait_send()

    return start, wait


class _Ride:
    def __init__(self, srcs, dsts, plan, dst_inits=None):
        self.srcs, self.dsts, self.plan, self.dst_inits, self.out = srcs, dsts, plan, dst_inits, None


def _riding_call(body, ride, *, name, grid, in_specs, out_specs, out_shape, scratch_shapes=(), **kw):
    single = not isinstance(out_specs, (list, tuple))
    out_specs = [out_specs] if single else list(out_specs)
    out_shape = [out_shape] if single else list(out_shape)
    srcs = ride.srcs() if callable(ride.srcs) else ride.srcs
    inits = ride.dst_inits() if callable(ride.dst_inits) else ride.dst_inits
    xin = (list(srcs) if srcs is not None else []) + (list(inits) if inits is not None else [])
    na, nrem = len(ride.dsts), len(ride.plan(0, 0, 0))
    n_in, n_out, n_scr = len(in_specs), len(out_specs), len(scratch_shapes)

    def wrapped(*refs):
        ins, xins = refs[:n_in], refs[n_in:n_in + len(xin)]
        outs = refs[n_in + len(xin):n_in + len(xin) + n_out]
        xouts = refs[n_in + len(xin) + n_out:n_in + len(xin) + n_out + na]
        scr = refs[n_in + len(xin) + n_out + na:]
        start, wait = _copies(xins[:na] if srcs is not None else xouts, xouts, scr[n_scr], scr[n_scr + 1], ride.plan)
        ids = [pl.program_id(a) for a in range(len(grid))]
        first = functools.reduce(jnp.logical_and, [i == 0 for i in ids])
        last = functools.reduce(jnp.logical_and, [i == g - 1 for i, g in zip(ids, grid)])
        pl.when(first)(start)
        body(*ins, *outs, *scr[:n_scr])
        pl.when(last)(wait)

    hbm = pl.BlockSpec(memory_space=pl.ANY)
    base = n_in + (na if srcs is not None else 0)
    aliases = {base + a: n_out + a for a in range(na)} if inits is not None else {}

    def run(*args):
        res = pl.pallas_call(
            wrapped, name=name, grid=grid, in_specs=list(in_specs) + [hbm] * len(xin),
            out_specs=out_specs + [hbm] * na, out_shape=out_shape + list(ride.dsts),
            scratch_shapes=list(scratch_shapes) + [pltpu.SemaphoreType.DMA((na * nrem,)),
                                                   pltpu.SemaphoreType.DMA((na * nrem,))],
            input_output_aliases=aliases, compiler_params=_params(("arbitrary",) * len(grid)),
        )(*args, *xin)
        ride.out = list(res[n_out:])
        return res[0] if single else list(res[:n_out])

    return run


def _pick(n, target, quantum=128):
    t = (min(target, n) // quantum) * quantum
    while t >= quantum:
        if n % t == 0:
            return t
        t -= quantum
    return n


def _mm(a, b, *, name, ta=False, tb=False, b3=False, out3=0, out_dtype=F32, add=None, tm=1024, tn=1536, tk=1536):
    if ta:
        K, M = a.shape
    else:
        M, K = a.shape
    if b3 and tb:
        Nn, K2, tk = b.shape[1], b.shape[0] * b.shape[2], b.shape[2]
    elif b3:
        K2, Nn, tn = b.shape[1], b.shape[0] * b.shape[2], b.shape[2]
    elif tb:
        Nn, K2 = b.shape
    else:
        K2, Nn = b.shape
    assert K == K2, (a.shape, b.shape)
    if out3:
        tn = Nn // out3
    tm, tn, tk = _pick(M, tm), _pick(Nn, tn), _pick(K, tk)
    nk = K // tk
    dn = (((0 if ta else 1,), (1 if tb else 0,)), ((), ()))

    def body(*refs):
        refs = list(refs)
        a_ref, b_ref = refs[:2]
        add_ref = refs[2] if add is not None else None
        o_ref = refs[3] if add is not None else refs[2]
        part = lax.dot_general(a_ref[...].astype(BF16), b_ref[...].astype(BF16), dn, preferred_element_type=F32)

        def finish(r):
            if add is not None:
                r = r + add_ref[...]
            o_ref[...] = r.astype(out_dtype)

        if nk == 1:
            finish(part)
            return
        acc = refs[-1]
        k = pl.program_id(2)

        @pl.when(k == 0)
        def _():
            acc[...] = part

        @pl.when(k > 0)
        def _():
            acc[...] += part

        @pl.when(k == nk - 1)
        def _():
            finish(acc[...])

    a_spec = pl.BlockSpec((tk, tm), lambda i, j, k: (k, i)) if ta else pl.BlockSpec((tm, tk), lambda i, j, k: (i, k))
    if b3 and tb:
        b_spec = pl.BlockSpec((None, tn, tk), lambda i, j, k: (k, j, 0))
    elif b3:
        b_spec = pl.BlockSpec((None, tk, tn), lambda i, j, k: (j, k, 0))
    elif tb:
        b_spec = pl.BlockSpec((tn, tk), lambda i, j, k: (j, k))
    else:
        b_spec = pl.BlockSpec((tk, tn), lambda i, j, k: (k, j))
    if out3:
        o_spec = pl.BlockSpec((None, tm, tn), lambda i, j, k: (j, i, 0))
    else:
        o_spec = pl.BlockSpec((tm, tn), lambda i, j, k: (i, j))
    ins, specs = [a, b], [a_spec, b_spec]
    if add is not None:
        ins.append(add)
        specs.append(o_spec)
    return _pcall(
        body, name=name, grid=(M // tm, Nn // tn, nk), in_specs=specs, out_specs=o_spec,
        out_shape=jax.ShapeDtypeStruct((out3, M, tn) if out3 else (M, Nn), out_dtype),
        scratch_shapes=[pltpu.VMEM((tm, tn), F32)] if nk > 1 else [],
        compiler_params=_params(("parallel", "parallel", "arbitrary")),
    )(*ins)


def _mm_multi(a_list, b_list, *, name, add=None, out_dtype=F32, tm=512):
    M = a_list[0].shape[0]
    tm = _pick(M, tm)
    ns = len(a_list)
    b_arrs, b_specs = [], []
    for b in b_list:
        arr, shp, idx = b if isinstance(b, tuple) else (b, b.shape, (0, 0))
        b_arrs.append(arr)
        b_specs.append(pl.BlockSpec(shp, lambda i, idx=idx: idx))
    Nn = b_specs[0].block_shape[0]
    dn = (((1,), (1,)), ((), ()))

    def body(*refs):
        a_refs, b_refs = refs[:ns], refs[ns:2 * ns]
        acc = None
        for a_ref, b_ref in zip(a_refs, b_refs):
            part = lax.dot_general(a_ref[...].astype(BF16), b_ref[...], dn, preferred_element_type=F32)
            acc = part if acc is None else acc + part
        if add is not None:
            acc = acc + refs[2 * ns][...]
        refs[-1][...] = acc.astype(out_dtype)

    o_spec = pl.BlockSpec((tm, Nn), lambda i: (i, 0))
    specs = [pl.BlockSpec((tm, a.shape[1]), lambda i: (i, 0)) for a in a_list] + b_specs
    ins = list(a_list) + b_arrs
    if add is not None:
        specs.append(o_spec)
        ins.append(add)
    return _pcall(body, name=name, grid=(M // tm,), in_specs=specs, out_specs=o_spec,
                  out_shape=jax.ShapeDtypeStruct((M, Nn), out_dtype), compiler_params=_params(("parallel",)))(*ins)


def _lane(shape):
    return lax.broadcasted_iota(jnp.int32, shape, len(shape) - 1)


def _rot_half(v):
    w = v.shape[-1]
    first = (_lane(v.shape) % HEAD_DIM) < (HEAD_DIM // 2)
    return jnp.where(first, pltpu.roll(v, w - HEAD_DIM // 2, v.ndim - 1), pltpu.roll(v, HEAD_DIM // 2, v.ndim - 1))


def _widen(t, w):
    return t if w == t.shape[-1] else jnp.concatenate([t] * (w // t.shape[-1]), axis=-1)


def _unrope(v, cos, sins):
    w = v.shape[-1]
    return v * _widen(cos, w) - _rot_half(v) * _widen(sins, w)


def _rope_tables(positions):
    half = HEAD_DIM // 2
    inv = ROPE_THETA ** (-jnp.arange(half, dtype=F32) / half)
    ang = positions.astype(F32)[..., None] * inv
    cos, sin = jnp.cos(ang), jnp.sin(ang)
    cosf = jnp.concatenate([cos, cos, cos, cos], axis=-1)
    sins = jnp.concatenate([-sin, sin, -sin, sin], axis=-1)
    n = positions.shape[0] * positions.shape[1]
    return cosf.reshape(n, PAIR_W), sins.reshape(n, PAIR_W)


def _inproj(x2, scale, shift, w, cosf, sins, flags, *, T, name):
    N, D = x2.shape
    tm, tn = _pick(T, 512), VAR_W
    tpe = T // tm

    def body(x_ref, sc_ref, sh_ref, w_ref, c_ref, s_ref, f_ref, *outs):
        o_refs, u_ref = outs[:N_VAR], outs[N_VAR]
        j = pl.program_id(1)

        @pl.when(j == 0)
        def _():
            u_ref[...] = (x_ref[...] * (1.0 + sc_ref[0]) + sh_ref[0]).astype(BF16)

        acc = jnp.dot(u_ref[...], w_ref[...], preferred_element_type=F32)
        fl = f_ref[...]
        ce = 1.0 + (_widen(c_ref[...], tn) - 1.0) * fl
        se = _widen(s_ref[...], tn) * fl
        res = (acc * ce + _rot_half(acc) * se).astype(BF16)
        for v in range(N_VAR):
            @pl.when(j == v)
            def _(v=v):
                o_refs[v][...] = res

    ex = pl.BlockSpec((1, 1, D), lambda i, j: (i // tpe, 0, 0))
    tab = pl.BlockSpec((tm, PAIR_W), lambda i, j: (i, 0))
    keep = lambda w_: pl.BlockSpec((tm, w_), lambda i, j: (i, 0))
    return _pcall(
        body, name=name, grid=(N // tm, N_VAR),
        in_specs=[keep(D), ex, ex, pl.BlockSpec((D, tn), lambda i, j: (0, j)), tab, tab,
                  pl.BlockSpec((1, tn), lambda i, j: (0, j))],
        out_specs=[keep(tn)] * N_VAR + [keep(D)],
        out_shape=[jax.ShapeDtypeStruct((N, tn), BF16)] * N_VAR + [jax.ShapeDtypeStruct((N, D), BF16)],
        compiler_params=_params(("parallel", "arbitrary")),
    )(x2, scale, shift, w, cosf, sins, flags)


class _Geom:
    def __init__(self, g):
        if g is None:
            self.r, self.nq, self.n_back, self.sink = 1, A_Q_HEADS, A_WINDOW - 1, True
            self.qw, self.kw = QA_W, KA_W
            self.qidx = lambda j: 0
            self.kidx = lambda j: QA_W // KA_W
            self.vidx = lambda j: QA_W // KA_W + 1
        else:
            window, r = B_PATTERNS[g]
            self.r, self.nq, self.n_back, self.sink = r, B_GROUP_HEADS, window // r, False
            self.qw, self.kw = GB_W, GB_W
            self.qidx = lambda j: 3 * j
            self.kidx = lambda j: 3 * j + 1
            self.vidx = lambda j: 3 * j + 2
        self.ntile = self.qw // PAIR_W


def _stack_heads(t):
    first = _lane(t.shape) < HEAD_DIM
    z = jnp.zeros_like(t)
    return jnp.concatenate([jnp.where(first, t, z), jnp.where(first, z, t)], axis=0)


def _unstack_heads(v2):
    return jnp.where(_lane((BLOCK, PAIR_W)) < HEAD_DIM, v2[:BLOCK], v2[BLOCK:])


def _dup_head(t, kh):
    tf = t.astype(F32)
    keep = (_lane(t.shape) < HEAD_DIM) if kh == 0 else (_lane(t.shape) >= HEAD_DIM)
    return jnp.where(keep, tf, pltpu.roll(tf, HEAD_DIM, 1)).astype(t.dtype)


def _fold_heads(t):
    return t + pltpu.roll(t, HEAD_DIM, 1)


def _softmax_parts(s, i, n_back, sinkcol):
    rows = s.shape[0]
    qi = jnp.bitwise_and(lax.broadcasted_iota(jnp.int32, (rows, 2 * BLOCK), 0), BLOCK - 1)
    ki = lax.broadcasted_iota(jnp.int32, (rows, 2 * BLOCK), 1)
    dist = qi + BLOCK - ki
    valid = jnp.logical_and(jnp.logical_and(dist >= 0, dist <= n_back), jnp.logical_or(ki >= BLOCK, i > 0))
    s = jnp.where(valid, s * SCALE, NEG_INF)
    m = jnp.max(s, axis=1, keepdims=True)
    if sinkcol is not None:
        m = jnp.maximum(m, sinkcol)
    p = jnp.exp(s - m)
    den = jnp.sum(p, axis=1, keepdims=True)
    es = None
    if sinkcol is not None:
        es = jnp.exp(sinkcol - m)
        den = den + es
    return p, m, den, es


_NT = (((1,), (1,)), ((), ()))
_TN = (((0,), (0,)), ((), ()))


def _rows2(prev_ref, cur_ref, cs):
    return jnp.concatenate([prev_ref[0, :, cs], cur_ref[0, :, cs]], axis=0)


def _sink_col(sink_ref, kh, nblocks):
    return jnp.concatenate([jnp.full((BLOCK, 1), sink_ref[kh * nblocks + b], F32) for b in range(nblocks)], axis=0)


def _tile(t):
    return slice(t * PAIR_W, (t + 1) * PAIR_W)


def _attn_fwd(qkv, sinks, g, *, NB, T, name):
    geo = _Geom(g)
    r, qw, kw, ntile = geo.r, geo.qw, geo.kw, geo.ntile
    tsub = T // r
    nblk = tsub // BLOCK
    qkv3 = qkv.reshape(NB, tsub, r * VAR_W)
    with_lse = g is not None
    out_dtype = F32 if with_lse else BF16
    tiles_per_kv = ntile // A_KV_HEADS

    def body(q_ref, kp_ref, kc_ref, vp_ref, vc_ref, sink_ref, o_ref, *rest):
        i = pl.program_id(2)
        if geo.sink:
            kall, vall = _rows2(kp_ref, kc_ref, _tile(0)), _rows2(vp_ref, vc_ref, _tile(0))
            for kh in range(A_KV_HEADS):
                tiles = [kh * tiles_per_kv + t for t in range(tiles_per_kv)]
                q2 = jnp.concatenate([_stack_heads(q_ref[0, :, _tile(t)]) for t in tiles], axis=0)
                s = lax.dot_general(q2, _dup_head(kall, kh), _NT, preferred_element_type=F32)
                p, m, den, _ = _softmax_parts(s, i, geo.n_back, _sink_col(sink_ref, kh, 2 * tiles_per_kv))
                o2 = jnp.dot(p.astype(BF16), _dup_head(vall, kh), preferred_element_type=F32) / den
                for n, t in enumerate(tiles):
                    o_ref[0, :, _tile(t)] = _unstack_heads(o2[2 * BLOCK * n:2 * BLOCK * (n + 1)]).astype(out_dtype)
        else:
            l_ref = rest[0]
            for t in range(ntile):
                q2 = _stack_heads(q_ref[0, :, _tile(t)])
                s = lax.dot_general(q2, _rows2(kp_ref, kc_ref, _tile(t)), _NT, preferred_element_type=F32)
                p, m, den, _ = _softmax_parts(s, i, geo.n_back, None)
                o2 = jnp.dot(p.astype(BF16), _rows2(vp_ref, vc_ref, _tile(t)), preferred_element_type=F32) / den
                o_ref[0, :, _tile(t)] = _unstack_heads(o2)
                l_ref[0, :, _tile(t)] = _unstack_heads(jnp.broadcast_to(m + jnp.log(den), (2 * BLOCK, PAIR_W)))

    prev = lambda i: jnp.maximum(i - 1, 0)
    in_specs = [
        pl.BlockSpec((1, BLOCK, qw), lambda b, j, i: (b, i, geo.qidx(j))),
        pl.BlockSpec((1, BLOCK, kw), lambda b, j, i: (b, prev(i), geo.kidx(j))),
        pl.BlockSpec((1, BLOCK, kw), lambda b, j, i: (b, i, geo.kidx(j))),
        pl.BlockSpec((1, BLOCK, kw), lambda b, j, i: (b, prev(i), geo.vidx(j))),
        pl.BlockSpec((1, BLOCK, kw), lambda b, j, i: (b, i, geo.vidx(j))),
        pl.BlockSpec(memory_space=pltpu.SMEM),
    ]
    o_spec = pl.BlockSpec((1, BLOCK, qw), lambda b, j, i: (b, i, j))
    o_shape = jax.ShapeDtypeStruct((NB, tsub, r * qw), out_dtype)
    res = _pcall(
        body, name=name, grid=(NB, r, nblk), in_specs=in_specs,
        out_specs=[o_spec, o_spec] if with_lse else o_spec, out_shape=[o_shape, o_shape] if with_lse else o_shape,
        compiler_params=_params(("parallel", "parallel", "arbitrary")),
    )(qkv3, qkv3, qkv3, qkv3, qkv3, sinks)
    if with_lse:
        return res[0].reshape(NB * T, qw), res[1].reshape(NB * T, qw)
    return res.reshape(NB * T, qw)


def _attn_bwd(qkv, do, dlse, cosf, sins, sinks, g, *, NB, T, name):
    geo = _Geom(g)
    r, qw, kw, ntile = geo.r, geo.qw, geo.kw, geo.ntile
    tsub = T // r
    nblk = tsub // BLOCK
    qkv3 = qkv.reshape(NB, tsub, r * VAR_W)
    do3 = do.reshape(NB, tsub, r * qw)
    cos3 = cosf.reshape(NB, tsub, r * PAIR_W)
    sin3 = sins.reshape(NB, tsub, r * PAIR_W)
    has_dlse = dlse is not None
    tiles_per_kv = ntile // A_KV_HEADS

    def grads(q2, kk, vv, do2, i, sinkcol, dlcol):
        s = lax.dot_general(q2, kk, _NT, preferred_element_type=F32)
        p, m, den, es = _softmax_parts(s, i, geo.n_back, sinkcol)
        inv = 1.0 / den
        p = p * inv
        dp = lax.dot_general(do2, vv, _NT, preferred_element_type=F32)
        delta = jnp.sum(p * dp, axis=1, keepdims=True)
        sk = es * inv * delta if es is not None else None
        if dlcol is not None:
            delta = delta - dlcol
        ds = (p * (dp - delta) * SCALE).astype(BF16)
        dq2 = jnp.dot(ds, kk, preferred_element_type=F32)
        dkk = lax.dot_general(ds, q2, _TN, preferred_element_type=F32)
        dvv = lax.dot_general(p.astype(BF16), do2, _TN, preferred_element_type=F32)
        return dq2, dkk, dvv, sk

    def body(*refs):
        it = iter(refs)
        q_ref, kp_ref, kc_ref, vp_ref, vc_ref, do_ref = (next(it) for _ in range(6))
        dl_ref = next(it) if has_dlse else None
        c_ref, s_ref, sink_ref, o_ref, ds_ref, dq_s, dk_s, dv_s, car_q, car_k, car_v = (next(it) for _ in range(11))
        b, j, i = pl.program_id(0), pl.program_id(1), pl.program_id(2)

        @pl.when(jnp.logical_and(b == 0, jnp.logical_and(j == 0, i == 0)))
        def _():
            ds_ref[...] = jnp.zeros_like(ds_ref)

        @pl.when(i == 0)
        def _():
            car_q[...] = jnp.zeros_like(car_q)
            car_k[...] = jnp.zeros_like(car_k)
            car_v[...] = jnp.zeros_like(car_v)

        @pl.when(i == nblk)
        def _():
            dk_s[...] = jnp.zeros_like(dk_s)
            dv_s[...] = jnp.zeros_like(dv_s)

        @pl.when(i < nblk)
        def _():
            if geo.sink:
                kall, vall = _rows2(kp_ref, kc_ref, _tile(0)), _rows2(vp_ref, vc_ref, _tile(0))
                lane1 = _lane((1, PAIR_W))
                dsink = jnp.zeros((1, PAIR_W), F32)
                dk_t = dv_t = None
                for kh in range(A_KV_HEADS):
                    tiles = [kh * tiles_per_kv + t for t in range(tiles_per_kv)]
                    q2 = jnp.concatenate([_stack_heads(q_ref[0, :, _tile(t)]) for t in tiles], axis=0)
                    do2 = jnp.concatenate([_stack_heads(do_ref[0, :, _tile(t)]) for t in tiles], axis=0)
                    nb = 2 * tiles_per_kv
                    dq2, dkk, dvv, sk = grads(q2, _dup_head(kall, kh), _dup_head(vall, kh), do2, i,
                                              _sink_col(sink_ref, kh, nb), None)
                    for n, t in enumerate(tiles):
                        dq_s[:, _tile(t)] = _unstack_heads(dq2[2 * BLOCK * n:2 * BLOCK * (n + 1)])
                    for bb in range(nb):
                        dsink = dsink + jnp.where(lane1 == kh * nb + bb, -jnp.sum(sk[BLOCK * bb:BLOCK * (bb + 1)]), 0.0)
                    dkk, dvv = _fold_heads(dkk), _fold_heads(dvv)
                    if kh == 0:
                        dk_t, dv_t = dkk, dvv
                    else:
                        second = _lane(dkk.shape) >= HEAD_DIM
                        dk_t, dv_t = jnp.where(second, dkk, dk_t), jnp.where(second, dvv, dv_t)
                dk_s[...] = dk_t
                dv_s[...] = dv_t
                ds_ref[0:1, :] += dsink
            else:
                for t in range(ntile):
                    q2, do2 = _stack_heads(q_ref[0, :, _tile(t)]), _stack_heads(do_ref[0, :, _tile(t)])
                    dlt = dl_ref[0, :, _tile(t)]
                    dlcol = jnp.concatenate([dlt[:, 0:1], dlt[:, HEAD_DIM:HEAD_DIM + 1]], axis=0)
                    dq2, dkk, dvv, _ = grads(q2, _rows2(kp_ref, kc_ref, _tile(t)), _rows2(vp_ref, vc_ref, _tile(t)),
                                             do2, i, None, dlcol)
                    dq_s[:, _tile(t)] = _unstack_heads(dq2)
                    dk_s[:, _tile(t)] = dkk
                    dv_s[:, _tile(t)] = dvv

        cos, sn = c_ref[0], s_ref[0]
        o_ref[0, :, 0:qw] = _unrope(car_q[...], cos, sn).astype(BF16)
        o_ref[0, :, qw:qw + kw] = _unrope(car_k[...] + dk_s[0:BLOCK, :], cos, sn).astype(BF16)
        o_ref[0, :, qw + kw:qw + 2 * kw] = (car_v[...] + dv_s[0:BLOCK, :]).astype(BF16)
        if qw + 2 * kw < VAR_W:
            o_ref[0, :, qw + 2 * kw:VAR_W] = jnp.zeros((BLOCK, VAR_W - qw - 2 * kw), BF16)
        car_q[...] = dq_s[...]
        car_k[...] = dk_s[BLOCK:2 * BLOCK, :]
        car_v[...] = dv_s[BLOCK:2 * BLOCK, :]

    cur = lambda i: jnp.minimum(i, nblk - 1)
    prv = lambda i: jnp.maximum(jnp.minimum(i, nblk - 1) - 1, 0)
    outb = lambda i: jnp.maximum(i - 1, 0)
    in_specs = [
        pl.BlockSpec((1, BLOCK, qw), lambda b, j, i: (b, cur(i), geo.qidx(j))),
        pl.BlockSpec((1, BLOCK, kw), lambda b, j, i: (b, prv(i), geo.kidx(j))),
        pl.BlockSpec((1, BLOCK, kw), lambda b, j, i: (b, cur(i), geo.kidx(j))),
        pl.BlockSpec((1, BLOCK, kw), lambda b, j, i: (b, prv(i), geo.vidx(j))),
        pl.BlockSpec((1, BLOCK, kw), lambda b, j, i: (b, cur(i), geo.vidx(j))),
        pl.BlockSpec((1, BLOCK, qw), lambda b, j, i: (b, cur(i), j)),
    ]
    ins = [qkv3, qkv3, qkv3, qkv3, qkv3, do3]
    if has_dlse:
        in_specs.append(pl.BlockSpec((1, BLOCK, qw), lambda b, j, i: (b, cur(i), j)))
        ins.append(dlse.reshape(NB, tsub, r * qw))
    in_specs += [
        pl.BlockSpec((1, BLOCK, PAIR_W), lambda b, j, i: (b, outb(i), j)),
        pl.BlockSpec((1, BLOCK, PAIR_W), lambda b, j, i: (b, outb(i), j)),
        pl.BlockSpec(memory_space=pltpu.SMEM),
    ]
    ins += [cos3, sin3, sinks]
    scratch = [pltpu.VMEM((BLOCK, qw), F32), pltpu.VMEM((2 * BLOCK, kw), F32), pltpu.VMEM((2 * BLOCK, kw), F32),
               pltpu.VMEM((BLOCK, qw), F32), pltpu.VMEM((BLOCK, kw), F32), pltpu.VMEM((BLOCK, kw), F32)]
    dqkv, dsink = _pcall(
        body, name=name, grid=(NB, r, nblk + 1), in_specs=in_specs,
        out_specs=[pl.BlockSpec((1, BLOCK, VAR_W), lambda b, j, i: (b, outb(i), j)),
                   pl.BlockSpec((8, PAIR_W), lambda b, j, i: (0, 0))],
        out_shape=[jax.ShapeDtypeStruct((NB, tsub, r * VAR_W), BF16), jax.ShapeDtypeStruct((8, PAIR_W), F32)],
        scratch_shapes=scratch, compiler_params=_params(("arbitrary", "arbitrary", "arbitrary")),
    )(*ins)
    return dqkv.reshape(NB * T, VAR_W), dsink


class _Rows:
    def __init__(self, N, T, tm):
        self.N, self.tm, self.tpe, self.grid = N, tm, T // tm, (N // tm,)

    def row(self, w, col=0):
        return pl.BlockSpec((self.tm, w), lambda i: (i, col))

    def ex(self, w):
        return pl.BlockSpec((1, 1, w), lambda i: (i // self.tpe, 0, 0))

    def const(self, shape):
        return pl.BlockSpec(shape, lambda i: tuple(0 for _ in shape))

    def first_of_example(self):
        return pl.program_id(0) % self.tpe == 0


def _acc(ref, first, val):
    @pl.when(first)
    def _():
        ref[0] = val

    @pl.when(jnp.logical_not(first))
    def _():
        ref[0] += val


def _colsum(v):
    return jnp.sum(v, axis=0, keepdims=True)


def _ln_stats(r):
    mu = jnp.mean(r, axis=-1, keepdims=True)
    xc = r - mu
    var = jnp.mean(xc * xc, axis=-1, keepdims=True)
    rstd = lax.rsqrt(var + LN_EPS)
    return xc * rstd, rstd


def _ln_bwd(dy, xhat, rstd, gain):
    dxh = dy * gain
    return rstd * (dxh - jnp.mean(dxh, axis=-1, keepdims=True) - xhat * jnp.mean(dxh * xhat, axis=-1, keepdims=True))


def _silu_parts(v):
    s = jax.nn.sigmoid(v)
    return v * s, s * (1.0 + v * (1.0 - s))


def _local_step(x, mod, positions, w_in, rest_weights, sinks, ln1_g, ln1_b, ln2_g, ln2_b, target, hook=None):
    hook = hook or (lambda event, **data: None)
    NB, T, D = x.shape
    N = NB * T
    x2 = x.reshape(N, D)
    tgt2 = target.reshape(N, D)
    shift_m, scale_m, gate_m, shift_f, scale_f, gate_f = [mod[:, None, k * D:(k + 1) * D] for k in range(6)]
    cosf, sins = _rope_tables(positions)
    col = jnp.arange(QKV_P)
    vcol = col % VAR_W
    flags = jnp.where(col < VAR_W, vcol < QA_W + KA_W, vcol < 2 * GB_W).astype(F32)[None]
    R = _Rows(N, T, _pick(T, 256))
    sds = jax.ShapeDtypeStruct
    exsum = lambda w=D: sds((NB, 1, w), F32)
    ngrp = len(B_PATTERNS)

    *qkv, u = _inproj(x2, scale_m, shift_m, w_in, cosf, sins, flags, T=T, name="inproj_qkv")
    gates = _mm(u, w_in[:, QKV_P:], name="inproj_gates")
    oa = _attn_fwd(qkv[0], sinks, None, NB=NB, T=T, name="attn_a_fwd")
    ob_parts = [_attn_fwd(qkv[1 + g], sinks, g, NB=NB, T=T, name=f"attn_b{g}_fwd") for g in range(ngrp)]
    (o1, l1), (o2, l2), (o3, l3) = ob_parts
    w_a, w_b, w_o, w_gu, w_d = rest_weights()
    F = w_d.shape[0]

    def merge_fwd(o1r, o2r, o3r, l1r, l2r, l3r, ob_ref):
        la, lb, lc = l1r[...], l2r[...], l3r[...]
        mx = jnp.maximum(jnp.maximum(la, lb), lc)
        ea, eb, ec = jnp.exp(la - mx), jnp.exp(lb - mx), jnp.exp(lc - mx)
        ob_ref[...] = ((ea * o1r[...] + eb * o2r[...] + ec * o3r[...]) / (ea + eb + ec)).astype(BF16)

    ob = _pcall(merge_fwd, name="merge_fwd", grid=R.grid, in_specs=[R.row(GB_W)] * 6, out_specs=R.row(GB_W),
                out_shape=sds((N, GB_W), BF16), compiler_params=_params(("parallel",)))(o1, o2, o3, l1, l2, l3)

    ya = _mm(oa, w_a, name="branch_a")
    yb = _mm(ob, w_b, b3=True, name="branch_b")

    def gate_fwd(ya_r, yb_r, ga_r, gb_r, mg_ref):
        mg_ref[...] = (jax.nn.sigmoid(ga_r[...]) * ya_r[...] + jax.nn.sigmoid(gb_r[...]) * yb_r[...]).astype(BF16)

    merged = _pcall(gate_fwd, name="gate_fwd", grid=R.grid, in_specs=[R.row(D), R.row(D), R.row(D, 0), R.row(D, 1)],
                    out_specs=R.row(D), out_shape=sds((N, D), BF16),
                    compiler_params=_params(("parallel",)))(ya, yb, gates, gates)
    y = _mm(merged, w_o, name="out_proj")

    def norm1_fwd(x_r, y_r, gm_r, g_r, b_r, sf_r, hf_r, r1_ref, x1_ref, u2_ref):
        r1 = ALPHA * x_r[...] + (1.0 + gm_r[0]) * y_r[...]
        xhat, _ = _ln_stats(r1)
        x1 = xhat * g_r[...] + b_r[...]
        r1_ref[...] = r1
        x1_ref[...] = x1
        u2_ref[...] = (x1 * (1.0 + sf_r[0]) + hf_r[0]).astype(BF16)

    r1, x1, u2 = _pcall(
        norm1_fwd, name="norm1_fwd", grid=R.grid,
        in_specs=[R.row(D), R.row(D), R.ex(D), R.const((1, D)), R.const((1, D)), R.ex(D), R.ex(D)],
        out_specs=[R.row(D)] * 3, out_shape=[sds((N, D), F32), sds((N, D), F32), sds((N, D), BF16)],
        compiler_params=_params(("parallel",)))(x2, y, gate_m, ln1_g, ln1_b, scale_f, shift_f)

    tnf = w_gu.shape[2]
    nft = w_gu.shape[0] // 2
    tmf = _pick(N, 512)

    def ffn_up(u_r, wg_r, wu_r, hg_ref, hu_ref, a_ref):
        hg = jnp.dot(u_r[...], wg_r[...], preferred_element_type=F32)
        hu = jnp.dot(u_r[...], wu_r[...], preferred_element_type=F32)
        sl, _ = _silu_parts(hg)
        hg_ref[...] = hg.astype(BF16)
        hu_ref[...] = hu.astype(BF16)
        a_ref[...] = (sl * hu).astype(BF16)

    ftile = pl.BlockSpec((tmf, tnf), lambda i, j: (i, j))
    hg, hu, act = _pcall(
        ffn_up, name="ffn_up", grid=(N // tmf, nft),
        in_specs=[pl.BlockSpec((tmf, D), lambda i, j: (i, 0)), pl.BlockSpec((None, D, tnf), lambda i, j: (j, 0, 0)),
                  pl.BlockSpec((None, D, tnf), lambda i, j: (j + nft, 0, 0))],
        out_specs=[ftile] * 3, out_shape=[sds((N, F), BF16)] * 3,
        compiler_params=_params(("parallel", "arbitrary")))(u2, w_gu, w_gu)
    y2 = _mm(act, w_d, name="ffn_down")

    def norm2_loss_bwd(x1_r, y2_r, t_r, gf_r, g_r, b_r, dy2_ref, dx1_ref, dgf_ref, dg_ref, db_ref, loss_ref):
        first = R.first_of_example()
        y2v = y2_r[...]
        r2 = ALPHA * x1_r[...] + (1.0 + gf_r[0]) * y2v
        xhat, rstd = _ln_stats(r2)
        err = xhat * g_r[...] + b_r[...] - t_r[...]
        dx2 = err * (1.0 / D)
        dr2 = _ln_bwd(dx2, xhat, rstd, g_r[...])
        dy2_ref[...] = ((1.0 + gf_r[0]) * dr2).astype(BF16)
        dx1_ref[...] = ALPHA * dr2
        _acc(dgf_ref, first, _colsum(dr2 * y2v))
        _acc(dg_ref, first, _colsum(dx2 * xhat))
        _acc(db_ref, first, _colsum(dx2))
        part = 0.5 * jnp.sum(jnp.mean(err * err, axis=-1, keepdims=True))
        _acc(loss_ref, first, jnp.broadcast_to(part, (1, 128)))

    dy2, dx1p, dgate_f, dg2, db2, loss_p = _pcall(
        norm2_loss_bwd, name="norm2_loss_bwd", grid=R.grid,
        in_specs=[R.row(D), R.row(D), R.row(D), R.ex(D), R.const((1, D)), R.const((1, D))],
        out_specs=[R.row(D), R.row(D), R.ex(D), R.ex(D), R.ex(D), R.ex(128)],
        out_shape=[sds((N, D), BF16), sds((N, D), F32), exsum(), exsum(), exsum(), exsum(128)],
        compiler_params=_params(("arbitrary",)))(x1, y2, tgt2, gate_f, ln2_g, ln2_b)

    g_wd = _mm(act, dy2, ta=True, out_dtype=BF16, name="ffn_down_dw")

    tmd = _pick(N, 256)

    def ffn_down_dx(dy_r, wd_r, hg_r, hu_r, dh_ref):
        for t in range(nft):
            cs = slice(t * tnf, (t + 1) * tnf)
            da = lax.dot_general(dy_r[...], wd_r[cs, :], _NT, preferred_element_type=F32)
            sl, dsl = _silu_parts(hg_r[:, cs].astype(F32))
            dh_ref[:, cs] = (da * hu_r[:, cs].astype(F32) * dsl).astype(BF16)
            dh_ref[:, F + t * tnf:F + (t + 1) * tnf] = (da * sl).astype(BF16)

    rowd = lambda w_: pl.BlockSpec((tmd, w_), lambda i: (i, 0))
    dh = _pcall(
        ffn_down_dx, name="ffn_down_dx", grid=(N // tmd,),
        in_specs=[rowd(D), pl.BlockSpec((F, D), lambda i: (0, 0)), rowd(F), rowd(F)],
        out_specs=rowd(2 * F), out_shape=sds((N, 2 * F), BF16),
        compiler_params=_params(("parallel",)))(dy2, w_d, hg, hu)
    du2 = _mm(dh, w_gu, tb=True, b3=True, name="ffn_up_dx")
    g_wgu = _mm(u2, dh, ta=True, out3=w_gu.shape[0], out_dtype=BF16, name="ffn_up_dw")

    def norm1_bwd(dx1p_r, du2_r, x1_r, r1_r, y_r, sf_r, gm_r, g_r,
                  dxp_ref, dy_ref, dsf_ref, dhf_ref, dgm_ref, dg_ref, db_ref):
        first = R.first_of_example()
        du2v = du2_r[...]
        dx1 = dx1p_r[...] + du2v * (1.0 + sf_r[0])
        xhat, rstd = _ln_stats(r1_r[...])
        dr1 = _ln_bwd(dx1, xhat, rstd, g_r[...])
        dxp_ref[...] = ALPHA * dr1
        dy_ref[...] = ((1.0 + gm_r[0]) * dr1).astype(BF16)
        _acc(dsf_ref, first, _colsum(du2v * x1_r[...]))
        _acc(dhf_ref, first, _colsum(du2v))
        _acc(dgm_ref, first, _colsum(dr1 * y_r[...]))
        _acc(dg_ref, first, _colsum(dx1 * xhat))
        _acc(db_ref, first, _colsum(dx1))

    dxp, dy, dscale_f, dshift_f, dgate_m, dg1, db1 = _pcall(
        norm1_bwd, name="norm1_bwd", grid=R.grid,
        in_specs=[R.row(D)] * 5 + [R.ex(D), R.ex(D), R.const((1, D))],
        out_specs=[R.row(D), R.row(D)] + [R.ex(D)] * 5,
        out_shape=[sds((N, D), F32), sds((N, D), BF16)] + [exsum()] * 5,
        compiler_params=_params(("arbitrary",)))(dx1p, du2, x1, r1, y, scale_f, gate_m, ln1_g)

    dmerged = _mm(dy, w_o, tb=True, name="out_proj_dx")
    g_wo = _mm(merged, dy, ta=True, out_dtype=BF16, name="out_proj_dw")

    def gate_bwd(dm_r, ya_r, yb_r, ga_r, gb_r, dya_ref, dyb_ref, dg_ref):
        dm = dm_r[...]
        sa, sb = jax.nn.sigmoid(ga_r[...]), jax.nn.sigmoid(gb_r[...])
        dya_ref[...] = (dm * sa).astype(BF16)
        dyb_ref[...] = (dm * sb).astype(BF16)
        dg_ref[:, :D] = (dm * ya_r[...] * sa * (1.0 - sa)).astype(BF16)
        dg_ref[:, D:] = (dm * yb_r[...] * sb * (1.0 - sb)).astype(BF16)

    dya, dyb, dgates = _pcall(
        gate_bwd, name="gate_bwd", grid=R.grid, in_specs=[R.row(D)] * 3 + [R.row(D, 0), R.row(D, 1)],
        out_specs=[R.row(D), R.row(D), R.row(2 * D)],
        out_shape=[sds((N, D), BF16), sds((N, D), BF16), sds((N, 2 * D), BF16)],
        compiler_params=_params(("parallel",)))(dmerged, ya, yb, gates, gates)

    doa = _mm(dya, w_a, tb=True, out_dtype=BF16, name="branch_a_dx")
    g_wa = _mm(oa, dya, ta=True, out_dtype=BF16, name="branch_a_dw")
    dob = _mm(dyb, w_b, tb=True, b3=True, name="branch_b_dx")
    g_wb = _mm(ob, dyb, ta=True, out3=w_b.shape[0], out_dtype=BF16, name="branch_b_dw")
    hook("rest_grads", g_wa=g_wa, g_wb=g_wb, g_wo=g_wo, g_wgu=g_wgu, g_wd=g_wd)

    seg = (jnp.arange(GB_W)[:, None] // HEAD_DIM == jnp.arange(GB_W)[None, :] // HEAD_DIM).astype(BF16)

    def merge_bwd(dob_r, o1r, o2r, o3r, l1r, l2r, l3r, seg_r, d1, d2, d3, e1, e2, e3):
        dob_v = dob_r[...]
        la, lb, lc = l1r[...], l2r[...], l3r[...]
        mx = jnp.maximum(jnp.maximum(la, lb), lc)
        ea, eb, ec = jnp.exp(la - mx), jnp.exp(lb - mx), jnp.exp(lc - mx)
        inv = 1.0 / (ea + eb + ec)
        ws = [ea * inv, eb * inv, ec * inv]

        def headsum(v):
            hi = v.astype(BF16)
            r1_ = v - hi.astype(F32)
            mid = r1_.astype(BF16)
            lo = (r1_ - mid.astype(F32)).astype(BF16)
            sm = seg_r[...]
            return (jnp.dot(hi, sm, preferred_element_type=F32) + jnp.dot(mid, sm, preferred_element_type=F32)
                    + jnp.dot(lo, sm, preferred_element_type=F32))

        dws = [headsum(dob_v * o[...]) for o in (o1r, o2r, o3r)]
        mean = ws[0] * dws[0] + ws[1] * dws[1] + ws[2] * dws[2]
        for w_, dw_, d_ref, e_ref in zip(ws, dws, (d1, d2, d3), (e1, e2, e3)):
            d_ref[...] = (w_ * dob_v).astype(BF16)
            e_ref[...] = w_ * (dw_ - mean)

    mb = _pcall(
        merge_bwd, name="merge_bwd", grid=R.grid, in_specs=[R.row(GB_W)] * 7 + [R.const((GB_W, GB_W))],
        out_specs=[R.row(GB_W)] * 6, out_shape=[sds((N, GB_W), BF16)] * 3 + [sds((N, GB_W), F32)] * 3,
        compiler_params=_params(("parallel",)))(dob, o1, o2, o3, l1, l2, l3, seg)
    do_b, dlse_b = mb[:3], mb[3:]
    hook("merge_bwd_done")

    dqkv_a, dsink = _attn_bwd(qkv[0], doa, None, cosf, sins, sinks, None, NB=NB, T=T, name="attn_a_bwd")
    hook("attn_a_bwd_done")
    dqkv = [dqkv_a]
    for g in range(ngrp):
        dqkv.append(_attn_bwd(qkv[1 + g], do_b[g], dlse_b[g], cosf, sins, sinks, g, NB=NB, T=T,
                              name=f"attn_b{g}_bwd")[0])
        hook(f"attn_b{g}_bwd_done")

    g_win = [_mm(u, dseg, ta=True, out_dtype=BF16, name=f"inproj_dw{n}") for n, dseg in enumerate(dqkv + [dgates])]
    hook("win_grads", g_win=g_win)
    wvar = lambda v: (w_in, (D, VAR_W), (0, v))
    du = _mm_multi(dqkv[:3], [wvar(v) for v in range(3)], name="inproj_dx0")
    hook("inproj_dx0_done")
    du = _mm_multi([dqkv[3], dgates], [wvar(3), (w_in, (D, 2 * D), (0, QKV_P // (2 * D)))], add=du, name="inproj_dx1")
    hook("inproj_dx1_done")

    def x_bwd(dxp_r, du_r, x_r, sm_r, gx_ref, dsm_ref, dhm_ref):
        first = R.first_of_example()
        duv = du_r[...]
        gx_ref[...] = dxp_r[...] + duv * (1.0 + sm_r[0])
        _acc(dsm_ref, first, _colsum(duv * x_r[...]))
        _acc(dhm_ref, first, _colsum(duv))

    gx, dscale_m, dshift_m = _pcall(
        x_bwd, name="x_bwd", grid=R.grid, in_specs=[R.row(D)] * 3 + [R.ex(D)],
        out_specs=[R.row(D), R.ex(D), R.ex(D)], out_shape=[sds((N, D), F32), exsum(), exsum()],
        compiler_params=_params(("arbitrary",)))(dxp, du, x2, scale_m)
    hook("x_bwd_done")

    dmod =jnp.concatenate([dshift_m, dscale_m, dgate_m, dshift_f, dscale_f, dgate_f], axis=-1)[:, 0]
    ln_grads = jnp.concatenate([dg1, db1, dg2, db2], axis=1)
    return dict(loss=loss_p[:, 0, 0], grad_x=gx.reshape(NB, T, D), g_win=g_win, g_wa=g_wa, g_wb=g_wb, g_wo=g_wo,
                g_wgu=g_wgu, g_wd=g_wd, dmod=dmod, ln_grads=ln_grads, dsink=dsink[0, :A_Q_HEADS])


def _coords():
    return lax.axis_index("x"), lax.axis_index("y"), lax.axis_index("c")


def _allgather_small(blk, *, name):
    m_per, n = blk.shape

    def body(x_ref, out_ref, send_sems, recv_sems, local_sem):
        x, y, c = _coords()
        me, sibling = (x, y, c), (x, y, 1 - c)
        chips = [(1 - x, y), (x, 1 - y), (1 - x, 1 - y)]

        def rows(px, py, pc):
            return out_ref.at[pl.ds((4 * px + 2 * py + pc) * m_per, m_per), :]

        def copy(k, block, to, src=None):
            return pltpu.make_async_remote_copy(
                src_ref=rows(*block) if src is None else src, dst_ref=rows(*block),
                send_sem=send_sems.at[k], recv_sem=recv_sems.at[k], device_id=to, device_id_type=MESH)

        mine = pltpu.make_async_copy(x_ref, rows(*me), local_sem)
        mine.start()
        first = [copy(0, me, sibling, src=x_ref)]
        first += [copy(1 + j, me, (*chip, c), src=x_ref) for j, chip in enumerate(chips)]
        for cp in first:
            cp.start()
        passed = [copy(4 + j, (*chip, c), sibling) for j, chip in enumerate(chips)]
        for j, chip in enumerate(chips):
            copy(1 + j, (*chip, c), me).wait_recv()
            passed[j].start()
        copy(0, sibling, me).wait_recv()
        for j, chip in enumerate(chips):
            copy(4 + j, (*chip, 1 - c), me).wait_recv()
        for cp in first + passed:
            cp.wait_send()
        mine.wait()

    return _pcall(
        body, name=name, out_shape=jax.ShapeDtypeStruct((8 * m_per, n), blk.dtype),
        in_specs=[pl.BlockSpec(memory_space=pltpu.VMEM)], out_specs=pl.BlockSpec(memory_space=pltpu.VMEM),
        scratch_shapes=[pltpu.SemaphoreType.DMA((7,)), pltpu.SemaphoreType.DMA((7,)), pltpu.SemaphoreType.DMA],
        compiler_params=pltpu.CompilerParams(vmem_limit_bytes=VMEM_LIMIT_BYTES),
    )(blk)


def _exchange(srcs, dsts, plan, *, name, dst_inits=None):
    na = len(dsts)
    nrem = len(plan(0, 0, 0))

    def body(*refs):
        refs = list(refs)
        src_refs = [refs.pop(0) for _ in range(na)] if srcs is not None else None
        if dst_inits is not None:
            del refs[:na]
        dst_refs, (send_sems, recv_sems) = refs[:na], refs[na:]
        start, wait = _copies(dst_refs if src_refs is None else src_refs, dst_refs, send_sems, recv_sems, plan)
        start()
        wait()

    hbm = pl.BlockSpec(memory_space=pl.ANY)
    ins = (list(srcs) if srcs is not None else []) + (list(dst_inits) if dst_inits is not None else [])
    base = na if srcs is not None else 0
    aliases = {base + a: a for a in range(na)} if dst_inits is not None else {}
    return _pcall(
        body, name=name, out_shape=list(dsts), in_specs=[hbm] * len(ins), out_specs=[hbm] * na,
        input_output_aliases=aliases,
        scratch_shapes=[pltpu.SemaphoreType.DMA((na * nrem,)), pltpu.SemaphoreType.DMA((na * nrem,))],
    )(*ins)


def _other_chips(x, y):
    return [(1 - x, y), (x, 1 - y), (1 - x, 1 - y)]


def _round(ride, carrier, name):
    if carrier is not None:
        _RIDES[carrier] = ride
        return
    srcs = ride.srcs() if callable(ride.srcs) else ride.srcs
    inits = ride.dst_inits() if callable(ride.dst_inits) else ride.dst_inits
    ride.out = list(_exchange(srcs, ride.dsts, ride.plan, name=name, dst_inits=inits))


class _Gather:
    def __init__(self, shards, chip, tag, carriers=(None, None)):
        def plan_ici(x, y, c):
            k = 2 * x + y
            return [((c,), (k, c), (2 * px + py, c), (px, py, c)) for px, py in _other_chips(x, y)]

        def plan_d2d(x, y, c):
            return [((2 * px + py, c), (2 * px + py, c), (2 * px + py, 1 - c), (x, y, 1 - c))
                    for px, py in _other_chips(x, y)]

        self.shards, self.chip = shards, chip
        dsts = [jax.ShapeDtypeStruct((4,) + s.shape, s.dtype) for s in shards]
        ici = _Ride(shards, dsts, plan_ici)
        self.d2d = _Ride(None, dsts, plan_d2d, dst_inits=lambda: ici.out)
        _round(ici, carriers[0], f"gather_{tag}_ici")
        _round(self.d2d, carriers[1], f"gather_{tag}_d2d")

    def result(self):
        full = [lax.dynamic_update_index_in_dim(f, s, self.chip, 0) for f, s in zip(self.d2d.out, self.shards)]
        return [f.reshape((4, 2 * f.shape[2], f.shape[3])) for f in full]


def _add_pairs(a, b, *, name):
    s, hr, wd = a.shape
    tr = _pick(hr, 600, 16)

    def body(a_ref, b_ref, o_ref):
        o_ref[...] = (a_ref[...].astype(F32) + b_ref[...].astype(F32)).astype(BF16)

    spec = pl.BlockSpec((1, tr, wd), lambda j, i: (j, i, 0))
    return _pcall(body, name=name, grid=(s, hr // tr), in_specs=[spec, spec], out_specs=spec,
                  out_shape=jax.ShapeDtypeStruct(a.shape, BF16), compiler_params=_params(("parallel", "parallel")))(a, b)


def _sum_chips(b, *, name):
    s, hr, wd = b.shape
    tr = _pick(hr, 600, 16)

    def body(b_ref, o_ref):
        acc = b_ref[0].astype(F32)
        for k in range(1, s):
            acc = acc + b_ref[k].astype(F32)
        o_ref[...] = acc

    return _pcall(body, name=name, grid=(hr // tr,), in_specs=[pl.BlockSpec((s, tr, wd), lambda i: (0, i, 0))],
                  out_specs=pl.BlockSpec((tr, wd), lambda i: (i, 0)), out_shape=jax.ShapeDtypeStruct((hr, wd), F32),
                  compiler_params=_params(("parallel",)))(b)


class _ReduceScatter:
    def __init__(self, gs, chip, ci, tag):
        self.gs, self.chip, self.ci, self.tag = gs, chip, ci, tag
        self.half_t = [jax.ShapeDtypeStruct((g.shape[0],) + g.shape[2:], BF16) for g in gs]

    def pair(self, carrier=None):
        plan = lambda x, y, c: [((slice(None), 1 - c), (), (), (x, y, 1 - c))]
        self.r1 = _Ride(self.gs, self.half_t, plan)
        _round(self.r1, carrier, f"reduce_{self.tag}_pair")

    def chips(self, carrier=None):
        def plan(x, y, c):
            k = 2 * x + y
            return [((2 * px + py,), (k,), (2 * px + py,), (px, py, c)) for px, py in _other_chips(x, y)]

        self.pairs = [_add_pairs(lax.dynamic_index_in_dim(g, self.ci, 1, keepdims=False), f,
                                 name=f"reduce_{self.tag}_pair_add{n}")
                      for n, (g, f) in enumerate(zip(self.gs, self.r1.out))]
        self.r2 = _Ride(self.pairs, self.half_t, plan)
        _round(self.r2, carrier, f"reduce_{self.tag}_chips")

    def halves(self, carrier=None):
        plan = lambda x, y, c: [((), (c,), (1 - c,), (x, y, 1 - c))]
        self.mine = []
        for n, (l, p) in enumerate(zip(self.r2.out, self.pairs)):
            own = lax.dynamic_index_in_dim(p, self.chip, 0, keepdims=False)
            self.mine.append(_sum_chips(lax.dynamic_update_index_in_dim(l, own, self.chip, 0),
                                        name=f"reduce_{self.tag}_chip_sum{n}"))
        self.r3 = _Ride(self.mine, [jax.ShapeDtypeStruct((2,) + m.shape, F32) for m in self.mine], plan)
        _round(self.r3, carrier, f"reduce_{self.tag}_halves")

    def result(self):
        return [lax.dynamic_update_index_in_dim(b, m, self.ci, 0).reshape(2 * m.shape[0], m.shape[1])
                for b, m in zip(self.r3.out, self.mine)]


def _ada_fwd(c_all, w_sh, b_sh, *, name):
    nb, d = c_all.shape
    wcols = w_sh.shape[1]
    tn = _pick(wcols, 512)

    def body(c_ref, w_ref, b_ref, o_ref, a_ref):
        cv = c_ref[...]
        act = cv * jax.nn.sigmoid(cv)
        a_ref[...] = act
        o_ref[...] = jnp.dot(act.astype(BF16), w_ref[...].astype(BF16), preferred_element_type=F32) + b_ref[...]

    return _pcall(
        body, name=name, grid=(wcols // tn,),
        in_specs=[pl.BlockSpec((nb, d), lambda j: (0, 0)), pl.BlockSpec((d, tn), lambda j: (0, j)),
                  pl.BlockSpec((1, tn), lambda j: (0, j))],
        out_specs=[pl.BlockSpec((nb, tn), lambda j: (0, j)), pl.BlockSpec((nb, d), lambda j: (0, 0))],
        out_shape=[jax.ShapeDtypeStruct((nb, wcols), F32), jax.ShapeDtypeStruct((nb, d), F32)],
        compiler_params=_params(("arbitrary",)))(c_all, w_sh, b_sh)


def _sum_devices(g, *, name):
    nd, m, w = g.shape

    def body(g_ref, o_ref):
        acc = g_ref[0]
        for k in range(1, nd):
            acc = acc + g_ref[k]
        o_ref[...] = acc

    return _pcall(body, name=name, out_shape=jax.ShapeDtypeStruct((m, w), F32),
                  compiler_params=pltpu.CompilerParams(vmem_limit_bytes=VMEM_LIMIT_BYTES))(g)


def _adamw(w, g, m, v, *, name):
    rows, cols = w.shape
    tr = _pick(rows, max(8, (1 << 18) // cols), 8)
    c1 = 1.0 / (1.0 - ADAM_B1 ** ADAM_STEP)
    c2 = 1.0 / (1.0 - ADAM_B2 ** ADAM_STEP)

    def body(w_ref, g_ref, m_ref, v_ref, d_ref, nm_ref, nv_ref):
        gv = g_ref[...]
        nm = ADAM_B1 * m_ref[...] + (1.0 - ADAM_B1) * gv
        nv = ADAM_B2 * v_ref[...] + (1.0 - ADAM_B2) * (gv * gv)
        d_ref[...] = -ADAM_LR * ((nm * c1) / (jnp.sqrt(nv * c2) + ADAM_EPS) + ADAM_WD * w_ref[...])
        nm_ref[...] = nm
        nv_ref[...] = nv

    spec = pl.BlockSpec((tr, cols), lambda i: (i, 0))
    shp = jax.ShapeDtypeStruct((rows, cols), F32)
    return _pcall(body, name=name, grid=(rows // tr,), in_specs=[spec] * 4, out_specs=[spec] * 3,
                  out_shape=[shp] * 3, compiler_params=_params(("parallel",)))(w, g, m, v)


def _permute_in_cols(w):
    ngrp = len(B_PATTERNS)
    qb, kb, vb = (w[:, A_W + n * QB_W:A_W + (n + 1) * QB_W] for n in range(3))
    parts = [w[:, :A_W], jnp.zeros((w.shape[0], VAR_W - A_W), w.dtype)]
    for g in range(ngrp):
        parts += [t[:, g * GB_W:(g + 1) * GB_W] for t in (qb, kb, vb)]
    return jnp.concatenate(parts + [w[:, A_W + 3 * QB_W:]], axis=1)


def _unpermute_in_grads(pieces):
    ga, groups, gg = pieces[0], pieces[1:-1], pieces[-1]
    cols = [ga[:, :A_W]]
    for n in range(3):
        cols += [gp[:, n * GB_W:(n + 1) * GB_W] for gp in groups]
    return jnp.concatenate(cols + [gg], axis=1)


def kernel(x, c, positions, w_ada, b_ada, w_in, sinks, w_branch_a, w_branch_b, w_o, ln1_g, ln1_b, w_gate_up, w_down, ln2_g, ln2_b, loss_target, m_w_ada, m_b_ada, m_w_in, m_sinks, m_w_branch_a, m_w_branch_b, m_w_o, m_ln1_g, m_ln1_b, m_w_gate_up, m_w_down, m_ln2_g, m_ln2_b, v_w_ada, v_b_ada, v_w_in, v_sinks, v_w_branch_a, v_w_branch_b, v_w_o, v_ln1_g, v_ln1_b, v_w_gate_up, v_w_down, v_ln2_g, v_ln2_b):
    xi, yi, ci = _coords()
    chip = 2 * xi + yi
    dev = 4 * xi + 2 * yi + ci
    NB, T, D = x.shape
    nchip, ndev = 4, 8
    ada_cols = w_ada.shape[2]

    c_blk = jnp.zeros((8, D), F32).at[:NB].set(c)
    c_all = _allgather_small(c_blk, name="gather_c").reshape(ndev, 8, D)[:, :NB].reshape(ndev * NB, D)
    b_sh = lax.dynamic_slice(b_ada, (0, chip * ada_cols), (1, ada_cols))
    mod_part, c_act = _ada_fwd(c_all, w_ada[0], b_sh, name="ada_fwd")
    mod_g = _allgather_small(mod_part, name="gather_mod").reshape(nchip, 2, ndev * NB, ada_cols)[:, 0]
    mod_all = jnp.transpose(mod_g, (1, 0, 2)).reshape(ndev * NB, nchip * ada_cols)
    mod = lax.dynamic_slice(mod_all, (NB * dev, 0), (NB, nchip * ada_cols))

    ra, ro, rd = w_branch_a.shape[1], w_o.shape[1], w_down.shape[1]
    rowsh = jnp.concatenate([w_branch_a[0], w_o[0], w_down[0]], axis=0)
    halves = lambda a: a.reshape(a.shape[:-2] + (2, a.shape[-2] // 2, a.shape[-1]))
    whole = lambda a: a.reshape(a.shape[:-3] + (2 * a.shape[-2], a.shape[-1]))
    shards = [halves(w.astype(BF16)) for w in (w_in[0], rowsh, w_branch_b[0], w_gate_up[0])]
    (g_in,) = _Gather(shards[:1], chip, "w_in").result()
    w_in_f = _permute_in_cols(jnp.concatenate([g_in[k] for k in range(nchip)], axis=1))
    rest = _Gather(shards[1:], chip, "w_rest", carriers=("inproj_qkv", "attn_a_fwd"))

    def rest_weights():
        g_rows, w_b_f, w_gu_f = rest.result()
        return (g_rows[:, :ra].reshape(nchip * ra, D), w_b_f, g_rows[:, ra:ra + ro].reshape(nchip * ro, D), w_gu_f,
                g_rows[:, ra + ro:].reshape(nchip * rd, D))

    red = {}

    def hook(event, **g):
        if event == "rest_grads":
            gr_rows = jnp.concatenate([g["g_wa"].reshape(nchip, ra, D), g["g_wo"].reshape(nchip, ro, D),
                                       g["g_wd"].reshape(nchip, rd, D)], axis=1)
            red["rest"] = _ReduceScatter([halves(a) for a in (gr_rows, g["g_wb"], g["g_wgu"])], chip, ci, "rest")
            red["rest"].pair(carrier="merge_bwd")
        elif event == "merge_bwd_done":
            red["rest"].chips(carrier="attn_a_bwd")
        elif event == "attn_a_bwd_done":
            red["rest"].halves(carrier="attn_b0_bwd")
        elif event == "win_grads":
            gr_in = jnp.stack(jnp.split(_unpermute_in_grads(g["g_win"]), nchip, axis=1))
            red["w_in"] = _ReduceScatter([halves(gr_in)], chip, ci, "w_in")
            red["w_in"].pair(carrier="inproj_dx0")
        elif event == "inproj_dx0_done":
            red["w_in"].chips(carrier="inproj_dx1")
        elif event == "inproj_dx1_done":
            red["w_in"].halves(carrier="x_bwd")

    res = _local_step(x, mod, positions, w_in_f, rest_weights, sinks[0], ln1_g, ln1_b, ln2_g, ln2_b, loss_target, hook)
    (g_w_in,) = red["w_in"].result()
    g_rows_red, g_w_b, g_w_gu = red["rest"].result()
    g_w_a, g_w_o, g_w_d = g_rows_red[:ra], g_rows_red[ra:ra + ro], g_rows_red[ra + ro:]

    small_rows = 24
    misc = jnp.zeros((1, D), F32).at[0, :A_Q_HEADS].set(res["dsink"]).at[0, A_Q_HEADS].set(jnp.sum(res["loss"]))
    small = jnp.concatenate([res["dmod"].reshape(NB * 6, D), jnp.sum(res["ln_grads"], axis=0), misc,
                             jnp.zeros((small_rows - NB * 6 - 5, D), F32)], axis=0)
    small_all = _allgather_small(small, name="gather_small").reshape(ndev, small_rows, D)
    dmod_all = small_all[:, :NB * 6].reshape(ndev * NB, 6 * D)
    sums = _sum_devices(small_all, name="sum_small")
    g_b_ada = (sums[0:6] + sums[6:12]).reshape(1, 6 * D)
    g_ln1_g, g_ln1_b, g_ln2_g, g_ln2_b = (sums[12 + n][None] for n in range(4))
    g_sinks = sums[16, :A_Q_HEADS][None]
    loss = sums[16, A_Q_HEADS]
    dmod_sh = lax.dynamic_slice(dmod_all, (0, chip * ada_cols), (ndev * NB, ada_cols))
    g_w_ada = _mm(c_act, dmod_sh, ta=True, name="ada_dw")

    names = ["w_ada", "b_ada", "w_in", "sinks", "w_branch_a", "w_branch_b", "w_o", "ln1_g", "ln1_b",
             "w_gate_up", "w_down", "ln2_g", "ln2_b"]
    ws = [w_ada, b_ada, w_in, sinks, w_branch_a, w_branch_b, w_o, ln1_g, ln1_b, w_gate_up, w_down, ln2_g, ln2_b]
    ms = [m_w_ada, m_b_ada, m_w_in, m_sinks, m_w_branch_a, m_w_branch_b, m_w_o, m_ln1_g, m_ln1_b, m_w_gate_up,
          m_w_down, m_ln2_g, m_ln2_b]
    vs = [v_w_ada, v_b_ada, v_w_in, v_sinks, v_w_branch_a, v_w_branch_b, v_w_o, v_ln1_g, v_ln1_b, v_w_gate_up,
          v_w_down, v_ln2_g, v_ln2_b]
    gs = [g_w_ada, g_b_ada, g_w_in, g_sinks, g_w_a, g_w_b, g_w_o, g_ln1_g, g_ln1_b, g_w_gu, g_w_d, g_ln2_g, g_ln2_b]
    grads, deltas, new_ms, new_vs = [], [], [], []
    for name, w, g, m, v in zip(names, ws, gs, ms, vs):
        shp = w.shape
        w2, m2, v2 = (a.reshape(shp[-2], shp[-1]) for a in (w, m, v))
        g2 = g.reshape(shp[-2], shp[-1])
        d, nm, nv = _adamw(w2, g2, m2, v2, name="adamw_" + name)
        grads.append(g2.reshape(shp))
        deltas.append(d.reshape(shp))
        new_ms.append(nm.reshape(shp))
        new_vs.append(nv.reshape(shp))
    return (loss, res["grad_x"], *grads, *deltas, *new_ms, *new_vs)
```

```python
import functools

import jax
import jax.numpy as jnp
from jax import lax
from jax.experimental import pallas as pl
from jax.experimental.pallas import tpu as pltpu

F32 = jnp.float32
BF16 = jnp.bfloat16
MESH = pl.DeviceIdType.MESH

HEAD_DIM = 64
PAIR_W = 2 * HEAD_DIM
BLOCK = 128
A_Q_HEADS = 16
A_KV_HEADS = 2
A_WINDOW = 128
B_PATTERNS = ((128, 1), (512, 4), (2048, 16))
B_GROUP_HEADS = 8
QA_W = A_Q_HEADS * HEAD_DIM
KA_W = A_KV_HEADS * HEAD_DIM
GB_W = B_GROUP_HEADS * HEAD_DIM
QB_W = GB_W * len(B_PATTERNS)
A_W = QA_W + 2 * KA_W
VAR_W = 3 * GB_W
N_VAR = 1 + len(B_PATTERNS)
QKV_P = N_VAR * VAR_W
ROPE_THETA = 10000.0
LN_EPS = 1e-5
NEG_INF = -1e30
DEPTH = 1
ALPHA = (2 * DEPTH) ** 0.25
SCALE = HEAD_DIM ** -0.5

ADAM_LR, ADAM_B1, ADAM_B2, ADAM_EPS, ADAM_WD, ADAM_STEP = 0.001, 0.9, 0.999, 1e-08, 0.01, 10

VMEM_LIMIT_BYTES = 56 * 1024 * 1024


def _params(sem=None):
    return pltpu.CompilerParams(dimension_semantics=sem, vmem_limit_bytes=VMEM_LIMIT_BYTES)


_RIDES = {}


def _pcall(body, *, name, **kw):
    ride = _RIDES.pop(name, None)
    if ride is None:
        return pl.pallas_call(body, name=name, **kw)
    return _riding_call(body, ride, name=name, **kw)


def _copies(src_refs, dst_refs, send_sems, recv_sems, plan):
    x, y, c = lax.axis_index("x"), lax.axis_index("y"), lax.axis_index("c")
    remote = plan(x, y, c)
    nrem = len(remote)
    at = lambda ref, idx: ref.at[idx] if idx else ref

    def copy(a, n, landing):
        si, di, ri, peer = remote[n]
        return pltpu.make_async_remote_copy(
            src_ref=at(src_refs[a], si), dst_ref=at(dst_refs[a], ri if landing else di),
            send_sem=send_sems.at[a * nrem + n], recv_sem=recv_sems.at[a * nrem + n],
            device_id=peer, device_id_type=MESH)

    order = [(a, n) for a in range(len(dst_refs)) for n in range(nrem)]

    def start():
        for a, n in order:
            copy(a, n, False).start()

    def wait():
        for a, n in order:
            copy(a, n, True).wait_recv()
        for a, n in order:
            copy(a, n, False).wait_send()

    return start, wait


class _Ride:
    def __init__(self, srcs, dsts, plan, dst_inits=None):
        self.srcs, self.dsts, self.plan, self.dst_inits, self.out = srcs, dsts, plan, dst_inits, None


def _riding_call(body, ride, *, name, grid, in_specs, out_specs, out_shape, scratch_shapes=(), **kw):
    single = not isinstance(out_specs, (list, tuple))
    out_specs = [out_specs] if single else list(out_specs)
    out_shape = [out_shape] if single else list(out_shape)
    srcs = ride.srcs() if callable(ride.srcs) else ride.srcs
    inits = ride.dst_inits() if callable(ride.dst_inits) else ride.dst_inits
    xin = (list(srcs) if srcs is not None else []) + (list(inits) if inits is not None else [])
    na, nrem = len(ride.dsts), len(ride.plan(0, 0, 0))
    n_in, n_out, n_scr = len(in_specs), len(out_specs), len(scratch_shapes)

    def wrapped(*refs):
        ins, xins = refs[:n_in], refs[n_in:n_in + len(xin)]
        outs = refs[n_in + len(xin):n_in + len(xin) + n_out]
        xouts = refs[n_in + len(xin) + n_out:n_in + len(xin) + n_out + na]
        scr = refs[n_in + len(xin) + n_out + na:]
        start, wait = _copies(xins[:na] if srcs is not None else xouts, xouts, scr[n_scr], scr[n_scr + 1], ride.plan)
        ids = [pl.program_id(a) for a in range(len(grid))]
        first = functools.reduce(jnp.logical_and, [i == 0 for i in ids])
        last = functools.reduce(jnp.logical_and, [i == g - 1 for i, g in zip(ids, grid)])
        pl.when(first)(start)
        body(*ins, *outs, *scr[:n_scr])
        pl.when(last)(wait)

    hbm = pl.BlockSpec(memory_space=pl.ANY)
    base = n_in + (na if srcs is not None else 0)
    aliases = {base + a: n_out + a for a in range(na)} if inits is not None else {}

    def run(*args):
        res = pl.pallas_call(
            wrapped, name=name, grid=grid, in_specs=list(in_specs) + [hbm] * len(xin),
            out_specs=out_specs + [hbm] * na, out_shape=out_shape + list(ride.dsts),
            scratch_shapes=list(scratch_shapes) + [pltpu.SemaphoreType.DMA((na * nrem,)),
                                                   pltpu.SemaphoreType.DMA((na * nrem,))],
            input_output_aliases=aliases, compiler_params=_params(("arbitrary",) * len(grid)),
        )(*args, *xin)
        ride.out = list(res[n_out:])
        return res[0] if single else list(res[:n_out])

    return run


def _pick(n, target, quantum=128):
    t = (min(target, n) // quantum) * quantum
    while t >= quantum:
        if n % t == 0:
            return t
        t -= quantum
    return n


def _mm(a, b, *, name, ta=False, tb=False, b3=False, out3=0, out_dtype=F32, add=None, tm=1024, tn=1536, tk=1536):
    if ta:
        K, M = a.shape
    else:
        M, K = a.shape
    if b3 and tb:
        Nn, K2, tk = b.shape[1], b.shape[0] * b.shape[2], b.shape[2]
    elif b3:
        K2, Nn, tn = b.shape[1], b.shape[0] * b.shape[2], b.shape[2]
    elif tb:
        Nn, K2 = b.shape
    else:
        K2, Nn = b.shape
    assert K == K2, (a.shape, b.shape)
    if out3:
        tn = Nn // out3
    tm, tn, tk = _pick(M, tm), _pick(Nn, tn), _pick(K, tk)
    nk = K // tk
    dn = (((0 if ta else 1,), (1 if tb else 0,)), ((), ()))

    def body(*refs):
        refs = list(refs)
        a_ref, b_ref = refs[:2]
        add_ref = refs[2] if add is not None else None
        o_ref = refs[3] if add is not None else refs[2]
        part = lax.dot_general(a_ref[...].astype(BF16), b_ref[...].astype(BF16), dn, preferred_element_type=F32)

        def finish(r):
            if add is not None:
                r = r + add_ref[...]
            o_ref[...] = r.astype(out_dtype)

        if nk == 1:
            finish(part)
            return
        acc = refs[-1]
        k = pl.program_id(2)

        @pl.when(k == 0)
        def _():
            acc[...] = part

        @pl.when(k > 0)
        def _():
            acc[...] += part

        @pl.when(k == nk - 1)
        def _():
            finish(acc[...])

    a_spec = pl.BlockSpec((tk, tm), lambda i, j, k: (k, i)) if ta else pl.BlockSpec((tm, tk), lambda i, j, k: (i, k))
    if b3 and tb:
        b_spec = pl.BlockSpec((None, tn, tk), lambda i, j, k: (k, j, 0))
    elif b3:
        b_spec = pl.BlockSpec((None, tk, tn), lambda i, j, k: (j, k, 0))
    elif tb:
        b_spec = pl.BlockSpec((tn, tk), lambda i, j, k: (j, k))
    else:
        b_spec = pl.BlockSpec((tk, tn), lambda i, j, k: (k, j))
    if out3:
        o_spec = pl.BlockSpec((None, tm, tn), lambda i, j, k: (j, i, 0))
    else:
        o_spec = pl.BlockSpec((tm, tn), lambda i, j, k: (i, j))
    ins, specs = [a, b], [a_spec, b_spec]
    if add is not None:
        ins.append(add)
        specs.append(o_spec)
    return _pcall(
        body, name=name, grid=(M // tm, Nn // tn, nk), in_specs=specs, out_specs=o_spec,
        out_shape=jax.ShapeDtypeStruct((out3, M, tn) if out3 else (M, Nn), out_dtype),
        scratch_shapes=[pltpu.VMEM((tm, tn), F32)] if nk > 1 else [],
        compiler_params=_params(("parallel", "parallel", "arbitrary")),
    )(*ins)


def _mm_multi(a_list, b_list, *, name, add=None, out_dtype=F32, tm=512):
    M = a_list[0].shape[0]
    tm = _pick(M, tm)
    ns = len(a_list)
    b_arrs, b_specs = [], []
    for b in b_list:
        arr, shp, idx = b if isinstance(b, tuple) else (b, b.shape, (0, 0))
        b_arrs.append(arr)
        b_specs.append(pl.BlockSpec(shp, lambda i, idx=idx: idx))
    Nn = b_specs[0].block_shape[0]
    dn = (((1,), (1,)), ((), ()))

    def body(*refs):
        a_refs, b_refs = refs[:ns], refs[ns:2 * ns]
        acc = None
        for a_ref, b_ref in zip(a_refs, b_refs):
            part = lax.dot_general(a_ref[...].astype(BF16), b_ref[...], dn, preferred_element_type=F32)
            acc = part if acc is None else acc + part
        if add is not None:
            acc = acc + refs[2 * ns][...]
        refs[-1][...] = acc.astype(out_dtype)

    o_spec = pl.BlockSpec((tm, Nn), lambda i: (i, 0))
    specs = [pl.BlockSpec((tm, a.shape[1]), lambda i: (i, 0)) for a in a_list] + b_specs
    ins = list(a_list) + b_arrs
    if add is not None:
        specs.append(o_spec)
        ins.append(add)
    return _pcall(body, name=name, grid=(M // tm,), in_specs=specs, out_specs=o_spec,
                  out_shape=jax.ShapeDtypeStruct((M, Nn), out_dtype), compiler_params=_params(("parallel",)))(*ins)


def _lane(shape):
    return lax.broadcasted_iota(jnp.int32, shape, len(shape) - 1)


def _rot_half(v):
    w = v.shape[-1]
    first = (_lane(v.shape) % HEAD_DIM) < (HEAD_DIM // 2)
    return jnp.where(first, pltpu.roll(v, w - HEAD_DIM // 2, v.ndim - 1), pltpu.roll(v, HEAD_DIM // 2, v.ndim - 1))


def _widen(t, w):
    return t if w == t.shape[-1] else jnp.concatenate([t] * (w // t.shape[-1]), axis=-1)


def _unrope(v, cos, sins):
    w = v.shape[-1]
    return v * _widen(cos, w) - _rot_half(v) * _widen(sins, w)


def _rope_tables(positions):
    half = HEAD_DIM // 2
    inv = ROPE_THETA ** (-jnp.arange(half, dtype=F32) / half)
    ang = positions.astype(F32)[..., None] * inv
    cos, sin = jnp.cos(ang), jnp.sin(ang)
    cosf = jnp.concatenate([cos, cos, cos, cos], axis=-1)
    sins = jnp.concatenate([-sin, sin, -sin, sin], axis=-1)
    n = positions.shape[0] * positions.shape[1]
    return cosf.reshape(n, PAIR_W), sins.reshape(n, PAIR_W)


def _inproj(x2, scale, shift, w, cosf, sins, flags, *, T, name):
    N, D = x2.shape
    tm, tn = _pick(T, 512), VAR_W
    tpe = T // tm

    def body(x_ref, sc_ref, sh_ref, w_ref, c_ref, s_ref, f_ref, *outs):
        o_refs, u_ref = outs[:N_VAR], outs[N_VAR]
        j = pl.program_id(1)

        @pl.when(j == 0)
        def _():
            u_ref[...] = (x_ref[...] * (1.0 + sc_ref[0]) + sh_ref[0]).astype(BF16)

        acc = jnp.dot(u_ref[...], w_ref[...], preferred_element_type=F32)
        fl = f_ref[...]
        ce = 1.0 + (_widen(c_ref[...], tn) - 1.0) * fl
        se = _widen(s_ref[...], tn) * fl
        res = (acc * ce + _rot_half(acc) * se).astype(BF16)
        for v in range(N_VAR):
            @pl.when(j == v)
            def _(v=v):
                o_refs[v][...] = res

    ex = pl.BlockSpec((1, 1, D), lambda i, j: (i // tpe, 0, 0))
    tab = pl.BlockSpec((tm, PAIR_W), lambda i, j: (i, 0))
    keep = lambda w_: pl.BlockSpec((tm, w_), lambda i, j: (i, 0))
    return _pcall(
        body, name=name, grid=(N // tm, N_VAR),
        in_specs=[keep(D), ex, ex, pl.BlockSpec((D, tn), lambda i, j: (0, j)), tab, tab,
                  pl.BlockSpec((1, tn), lambda i, j: (0, j))],
        out_specs=[keep(tn)] * N_VAR + [keep(D)],
        out_shape=[jax.ShapeDtypeStruct((N, tn), BF16)] * N_VAR + [jax.ShapeDtypeStruct((N, D), BF16)],
        compiler_params=_params(("parallel", "arbitrary")),
    )(x2, scale, shift, w, cosf, sins, flags)


class _Geom:
    def __init__(self, g):
        if g is None:
            self.r, self.nq, self.n_back, self.sink = 1, A_Q_HEADS, A_WINDOW - 1, True
            self.qw, self.kw = QA_W, KA_W
            self.qidx = lambda j: 0
            self.kidx = lambda j: QA_W // KA_W
            self.vidx = lambda j: QA_W // KA_W + 1
        else:
            window, r = B_PATTERNS[g]
            self.r, self.nq, self.n_back, self.sink = r, B_GROUP_HEADS, window // r, False
            self.qw, self.kw = GB_W, GB_W
            self.qidx = lambda j: 3 * j
            self.kidx = lambda j: 3 * j + 1
            self.vidx = lambda j: 3 * j + 2
        self.ntile = self.qw // PAIR_W


def _stack_heads(t):
    first = _lane(t.shape) < HEAD_DIM
    z = jnp.zeros_like(t)
    return jnp.concatenate([jnp.where(first, t, z), jnp.where(first, z, t)], axis=0)


def _unstack_heads(v2):
    return jnp.where(_lane((BLOCK, PAIR_W)) < HEAD_DIM, v2[:BLOCK], v2[BLOCK:])


def _dup_head(t, kh):
    tf = t.astype(F32)
    keep = (_lane(t.shape) < HEAD_DIM) if kh == 0 else (_lane(t.shape) >= HEAD_DIM)
    return jnp.where(keep, tf, pltpu.roll(tf, HEAD_DIM, 1)).astype(t.dtype)


def _fold_heads(t):
    return t + pltpu.roll(t, HEAD_DIM, 1)


def _band_mask(rows, i, n_back, single):
    nkeys = BLOCK if single else 2 * BLOCK
    qi = jnp.bitwise_and(lax.broadcasted_iota(jnp.int32, (rows, nkeys), 0), BLOCK - 1)
    ki = lax.broadcasted_iota(jnp.int32, (rows, nkeys), 1)
    if single:
        return qi >= ki
    dist = qi + BLOCK - ki
    return jnp.logical_and(jnp.logical_and(dist >= 0, dist <= n_back), jnp.logical_or(ki >= BLOCK, i > 0))


def _softmax_parts(s, valid, sinkcol):
    s = jnp.where(valid, s * SCALE, NEG_INF)
    m = jnp.max(s, axis=1, keepdims=True)
    if sinkcol is not None:
        m = jnp.maximum(m, sinkcol)
    p = jnp.exp(s - m)
    den = jnp.sum(p, axis=1, keepdims=True)
    es = None
    if sinkcol is not None:
        es = jnp.exp(sinkcol - m)
        den = den + es
    return p, m, den, es


_NT = (((1,), (1,)), ((), ()))
_TN = (((0,), (0,)), ((), ()))


def _rows2(prev_ref, cur_ref, cs, single=False):
    if single:
        return cur_ref[0, :, cs]
    return jnp.concatenate([prev_ref[0, :, cs], cur_ref[0, :, cs]], axis=0)


def _sink_col(sink_ref, kh, nblocks):
    return jnp.concatenate([jnp.full((BLOCK, 1), sink_ref[kh * nblocks + b], F32) for b in range(nblocks)], axis=0)


def _tile(t):
    return slice(t * PAIR_W, (t + 1) * PAIR_W)


def _attn_fwd(qkv, sinks, g, *, NB, T, name):
    geo = _Geom(g)
    r, qw, kw, ntile = geo.r, geo.qw, geo.kw, geo.ntile
    tsub = T // r
    nblk = tsub // BLOCK
    qkv3 = qkv.reshape(NB, tsub, r * VAR_W)
    with_lse = g is not None
    out_dtype = F32 if with_lse else BF16
    tiles_per_kv = ntile // A_KV_HEADS

    single = nblk == 1

    def body(q_ref, kp_ref, kc_ref, vp_ref, vc_ref, sink_ref, o_ref, *rest):
        i = pl.program_id(2)
        if geo.sink:
            kall, vall = _rows2(kp_ref, kc_ref, _tile(0)), _rows2(vp_ref, vc_ref, _tile(0))
            tiles = [[kh * tiles_per_kv + t for t in range(tiles_per_kv)] for kh in range(A_KV_HEADS)]
            q2s = [jnp.concatenate([_stack_heads(q_ref[0, :, _tile(t)]) for t in ts], axis=0) for ts in tiles]
            kks = [_dup_head(kall, kh) for kh in range(A_KV_HEADS)]
            vvs = [_dup_head(vall, kh) for kh in range(A_KV_HEADS)]
            sinkcols = [_sink_col(sink_ref, kh, 2 * tiles_per_kv) for kh in range(A_KV_HEADS)]
        else:
            tiles = [[t] for t in range(ntile)]
            q2s = [_stack_heads(q_ref[0, :, _tile(t)]) for t in range(ntile)]
            kks = [_rows2(kp_ref, kc_ref, _tile(t), single) for t in range(ntile)]
            vvs = [_rows2(vp_ref, vc_ref, _tile(t), single) for t in range(ntile)]
            sinkcols = [None] * ntile
        valid = _band_mask(q2s[0].shape[0], i, geo.n_back, single)
        ss = [lax.dot_general(q2, kk, _NT, preferred_element_type=F32) for q2, kk in zip(q2s, kks)]
        parts = [_softmax_parts(s, valid, sc) for s, sc in zip(ss, sinkcols)]
        o2s = [jnp.dot(p.astype(BF16), vv, preferred_element_type=F32) / den for (p, m, den, _), vv in zip(parts, vvs)]
        for ts, o2, (p, m, den, _) in zip(tiles, o2s, parts):
            for n, t in enumerate(ts):
                o_ref[0, :, _tile(t)] = _unstack_heads(o2[2 * BLOCK * n:2 * BLOCK * (n + 1)]).astype(out_dtype)
            if with_lse:
                rest[0][0, :, _tile(ts[0])] = _unstack_heads(jnp.broadcast_to(m + jnp.log(den), (2 * BLOCK, PAIR_W)))

    prev = lambda i: jnp.maximum(i - 1, 0)
    in_specs = [
        pl.BlockSpec((1, BLOCK, qw), lambda b, j, i: (b, i, geo.qidx(j))),
        pl.BlockSpec((1, BLOCK, kw), lambda b, j, i: (b, prev(i), geo.kidx(j))),
        pl.BlockSpec((1, BLOCK, kw), lambda b, j, i: (b, i, geo.kidx(j))),
        pl.BlockSpec((1, BLOCK, kw), lambda b, j, i: (b, prev(i), geo.vidx(j))),
        pl.BlockSpec((1, BLOCK, kw), lambda b, j, i: (b, i, geo.vidx(j))),
        pl.BlockSpec(memory_space=pltpu.SMEM),
    ]
    o_spec = pl.BlockSpec((1, BLOCK, qw), lambda b, j, i: (b, i, j))
    o_shape = jax.ShapeDtypeStruct((NB, tsub, r * qw), out_dtype)
    res = _pcall(
        body, name=name, grid=(NB, r, nblk), in_specs=in_specs,
        out_specs=[o_spec, o_spec] if with_lse else o_spec, out_shape=[o_shape, o_shape] if with_lse else o_shape,
        compiler_params=_params(("parallel", "parallel", "arbitrary")),
    )(qkv3, qkv3, qkv3, qkv3, qkv3, sinks)
    if with_lse:
        return res[0].reshape(NB * T, qw), res[1].reshape(NB * T, qw)
    return res.reshape(NB * T, qw)


def _attn_bwd(qkv, do, dlse, cosf, sins, sinks, g, *, NB, T, name):
    geo = _Geom(g)
    r, qw, kw, ntile = geo.r, geo.qw, geo.kw, geo.ntile
    tsub = T // r
    nblk = tsub // BLOCK
    qkv3 = qkv.reshape(NB, tsub, r * VAR_W)
    do3 = do.reshape(NB, tsub, r * qw)
    cos3 = cosf.reshape(NB, tsub, r * PAIR_W)
    sin3 = sins.reshape(NB, tsub, r * PAIR_W)
    has_dlse = dlse is not None
    tiles_per_kv = ntile // A_KV_HEADS

    single = nblk == 1
    krows = BLOCK if single else 2 * BLOCK

    def grads(q2s, kks, vvs, do2s, i, sinkcols, dlcols):
        valid = _band_mask(q2s[0].shape[0], i, geo.n_back, single)
        ss = [lax.dot_general(q2, kk, _NT, preferred_element_type=F32) for q2, kk in zip(q2s, kks)]
        dps = [lax.dot_general(do2, vv, _NT, preferred_element_type=F32) for do2, vv in zip(do2s, vvs)]
        ps, dss, sks = [], [], []
        for s, dp, sc, dl in zip(ss, dps, sinkcols, dlcols):
            p, m, den, es = _softmax_parts(s, valid, sc)
            inv = 1.0 / den
            p = p * inv
            delta = jnp.sum(p * dp, axis=1, keepdims=True)
            sks.append(es * inv * delta if es is not None else None)
            if dl is not None:
                delta = delta - dl
            ps.append(p.astype(BF16))
            dss.append((p * (dp - delta) * SCALE).astype(BF16))
        dq2s = [jnp.dot(ds, kk, preferred_element_type=F32) for ds, kk in zip(dss, kks)]
        dkks = [lax.dot_general(ds, q2, _TN, preferred_element_type=F32) for ds, q2 in zip(dss, q2s)]
        dvvs = [lax.dot_general(p, do2, _TN, preferred_element_type=F32) for p, do2 in zip(ps, do2s)]
        return dq2s, dkks, dvvs, sks

    def body(*refs):
        it = iter(refs)
        q_ref, kp_ref, kc_ref, vp_ref, vc_ref, do_ref = (next(it) for _ in range(6))
        dl_ref = next(it) if has_dlse else None
        c_ref, s_ref, sink_ref, o_ref, ds_ref, dq_s, dk_s, dv_s, car_q, car_k, car_v = (next(it) for _ in range(11))
        b, j, i = pl.program_id(0), pl.program_id(1), pl.program_id(2)

        @pl.when(jnp.logical_and(b == 0, jnp.logical_and(j == 0, i == 0)))
        def _():
            ds_ref[...] = jnp.zeros_like(ds_ref)

        @pl.when(i == 0)
        def _():
            car_q[...] = jnp.zeros_like(car_q)
            car_k[...] = jnp.zeros_like(car_k)
            car_v[...] = jnp.zeros_like(car_v)

        @pl.when(i == nblk)
        def _():
            dk_s[...] = jnp.zeros_like(dk_s)
            dv_s[...] = jnp.zeros_like(dv_s)

        @pl.when(i < nblk)
        def _():
            if geo.sink:
                kall, vall = _rows2(kp_ref, kc_ref, _tile(0)), _rows2(vp_ref, vc_ref, _tile(0))
                nb = 2 * tiles_per_kv
                tiles = [[kh * tiles_per_kv + t for t in range(tiles_per_kv)] for kh in range(A_KV_HEADS)]
                stack = lambda ref, ts: jnp.concatenate([_stack_heads(ref[0, :, _tile(t)]) for t in ts], axis=0)
                dq2s, dkks, dvvs, sks = grads(
                    [stack(q_ref, ts) for ts in tiles], [_dup_head(kall, kh) for kh in range(A_KV_HEADS)],
                    [_dup_head(vall, kh) for kh in range(A_KV_HEADS)], [stack(do_ref, ts) for ts in tiles], i,
                    [_sink_col(sink_ref, kh, nb) for kh in range(A_KV_HEADS)], [None] * A_KV_HEADS)
                lane1 = _lane((1, PAIR_W))
                dsink = jnp.zeros((1, PAIR_W), F32)
                for kh, (ts, dq2, sk) in enumerate(zip(tiles, dq2s, sks)):
                    for n, t in enumerate(ts):
                        dq_s[:, _tile(t)] = _unstack_heads(dq2[2 * BLOCK * n:2 * BLOCK * (n + 1)])
                    for bb in range(nb):
                        dsink = dsink + jnp.where(lane1 == kh * nb + bb, -jnp.sum(sk[BLOCK * bb:BLOCK * (bb + 1)]), 0.0)
                second = _lane((krows, PAIR_W)) >= HEAD_DIM
                dk_s[...] = jnp.where(second, _fold_heads(dkks[1]), _fold_heads(dkks[0]))
                dv_s[...] = jnp.where(second, _fold_heads(dvvs[1]), _fold_heads(dvvs[0]))
                ds_ref[0:1, :] += dsink
            else:
                dlts = [dl_ref[0, :, _tile(t)] for t in range(ntile)]
                dq2s, dkks, dvvs, _ = grads(
                    [_stack_heads(q_ref[0, :, _tile(t)]) for t in range(ntile)],
                    [_rows2(kp_ref, kc_ref, _tile(t), single) for t in range(ntile)],
                    [_rows2(vp_ref, vc_ref, _tile(t), single) for t in range(ntile)],
                    [_stack_heads(do_ref[0, :, _tile(t)]) for t in range(ntile)], i, [None] * ntile,
                    [jnp.concatenate([d[:, 0:1], d[:, HEAD_DIM:HEAD_DIM + 1]], axis=0) for d in dlts])
                for t in range(ntile):
                    dq_s[:, _tile(t)] = _unstack_heads(dq2s[t])
                    if single:
                        dk_s[0:BLOCK, _tile(t)] = jnp.zeros((BLOCK, PAIR_W), F32)
                        dv_s[0:BLOCK, _tile(t)] = jnp.zeros((BLOCK, PAIR_W), F32)
                        dk_s[BLOCK:2 * BLOCK, _tile(t)] = dkks[t]
                        dv_s[BLOCK:2 * BLOCK, _tile(t)] = dvvs[t]
                    else:
                        dk_s[:, _tile(t)] = dkks[t]
                        dv_s[:, _tile(t)] = dvvs[t]

        cos, sn = c_ref[0], s_ref[0]
        o_ref[0, :, 0:qw] = _unrope(car_q[...], cos, sn).astype(BF16)
        o_ref[0, :, qw:qw + kw] = _unrope(car_k[...] + dk_s[0:BLOCK, :], cos, sn).astype(BF16)
        o_ref[0, :, qw + kw:qw + 2 * kw] = (car_v[...] + dv_s[0:BLOCK, :]).astype(BF16)
        if qw + 2 * kw < VAR_W:
            o_ref[0, :, qw + 2 * kw:VAR_W] = jnp.zeros((BLOCK, VAR_W - qw - 2 * kw), BF16)
        car_q[...] = dq_s[...]
        car_k[...] = dk_s[BLOCK:2 * BLOCK, :]
        car_v[...] = dv_s[BLOCK:2 * BLOCK, :]

    cur = lambda i: jnp.minimum(i, nblk - 1)
    prv = lambda i: jnp.maximum(jnp.minimum(i, nblk - 1) - 1, 0)
    outb = lambda i: jnp.maximum(i - 1, 0)
    in_specs = [
        pl.BlockSpec((1, BLOCK, qw), lambda b, j, i: (b, cur(i), geo.qidx(j))),
        pl.BlockSpec((1, BLOCK, kw), lambda b, j, i: (b, prv(i), geo.kidx(j))),
        pl.BlockSpec((1, BLOCK, kw), lambda b, j, i: (b, cur(i), geo.kidx(j))),
        pl.BlockSpec((1, BLOCK, kw), lambda b, j, i: (b, prv(i), geo.vidx(j))),
        pl.BlockSpec((1, BLOCK, kw), lambda b, j, i: (b, cur(i), geo.vidx(j))),
        pl.BlockSpec((1, BLOCK, qw), lambda b, j, i: (b, cur(i), j)),
    ]
    ins = [qkv3, qkv3, qkv3, qkv3, qkv3, do3]
    if has_dlse:
        in_specs.append(pl.BlockSpec((1, BLOCK, qw), lambda b, j, i: (b, cur(i), j)))
        ins.append(dlse.reshape(NB, tsub, r * qw))
    in_specs += [
        pl.BlockSpec((1, BLOCK, PAIR_W), lambda b, j, i: (b, outb(i), j)),
        pl.BlockSpec((1, BLOCK, PAIR_W), lambda b, j, i: (b, outb(i), j)),
        pl.BlockSpec(memory_space=pltpu.SMEM),
    ]
    ins += [cos3, sin3, sinks]
    scratch = [pltpu.VMEM((BLOCK, qw), F32), pltpu.VMEM((2 * BLOCK, kw), F32), pltpu.VMEM((2 * BLOCK, kw), F32),
               pltpu.VMEM((BLOCK, qw), F32), pltpu.VMEM((BLOCK, kw), F32), pltpu.VMEM((BLOCK, kw), F32)]
    dqkv, dsink = _pcall(
        body, name=name, grid=(NB, r, nblk + 1), in_specs=in_specs,
        out_specs=[pl.BlockSpec((1, BLOCK, VAR_W), lambda b, j, i: (b, outb(i), j)),
                   pl.BlockSpec((8, PAIR_W), lambda b, j, i: (0, 0))],
        out_shape=[jax.ShapeDtypeStruct((NB, tsub, r * VAR_W), BF16), jax.ShapeDtypeStruct((8, PAIR_W), F32)],
        scratch_shapes=scratch, compiler_params=_params(("arbitrary", "arbitrary", "arbitrary")),
    )(*ins)
    return dqkv.reshape(NB * T, VAR_W), dsink


class _Rows:
    def __init__(self, N, T, tm):
        self.N, self.tm, self.tpe, self.grid = N, tm, T // tm, (N // tm,)

    def row(self, w, col=0):
        return pl.BlockSpec((self.tm, w), lambda i: (i, col))

    def ex(self, w):
        return pl.BlockSpec((1, 1, w), lambda i: (i // self.tpe, 0, 0))

    def const(self, shape):
        return pl.BlockSpec(shape, lambda i: tuple(0 for _ in shape))

    def first_of_example(self):
        return pl.program_id(0) % self.tpe == 0


def _acc(ref, first, val):
    @pl.when(first)
    def _():
        ref[0] = val

    @pl.when(jnp.logical_not(first))
    def _():
        ref[0] += val


def _colsum(v):
    return jnp.sum(v, axis=0, keepdims=True)


def _ln_stats(r):
    mu = jnp.mean(r, axis=-1, keepdims=True)
    xc = r - mu
    var = jnp.mean(xc * xc, axis=-1, keepdims=True)
    rstd = lax.rsqrt(var + LN_EPS)
    return xc * rstd, rstd


def _ln_bwd(dy, xhat, rstd, gain):
    dxh = dy * gain
    return rstd * (dxh - jnp.mean(dxh, axis=-1, keepdims=True) - xhat * jnp.mean(dxh * xhat, axis=-1, keepdims=True))


def _silu_parts(v):
    s = jax.nn.sigmoid(v)
    return v * s, s * (1.0 + v * (1.0 - s))


def _local_step(x, mod, positions, w_in, rest_weights, sinks, ln1_g, ln1_b, ln2_g, ln2_b, target, hook=None):
    hook = hook or (lambda event, **data: None)
    NB, T, D = x.shape
    N = NB * T
    x2 = x.reshape(N, D)
    tgt2 = target.reshape(N, D)
    shift_m, scale_m, gate_m, shift_f, scale_f, gate_f = [mod[:, None, k * D:(k + 1) * D] for k in range(6)]
    cosf, sins = _rope_tables(positions)
    col = jnp.arange(QKV_P)
    vcol = col % VAR_W
    flags = jnp.where(col < VAR_W, vcol < QA_W + KA_W, vcol < 2 * GB_W).astype(F32)[None]
    R = _Rows(N, T, _pick(T, 256))
    sds = jax.ShapeDtypeStruct
    exsum = lambda w=D: sds((NB, 1, w), F32)
    ngrp = len(B_PATTERNS)

    *qkv, u = _inproj(x2, scale_m, shift_m, w_in, cosf, sins, flags, T=T, name="inproj_qkv")
    gates = _mm(u, w_in[:, QKV_P:], name="inproj_gates")
    oa = _attn_fwd(qkv[0], sinks, None, NB=NB, T=T, name="attn_a_fwd")
    ob_parts = [_attn_fwd(qkv[1 + g], sinks, g, NB=NB, T=T, name=f"attn_b{g}_fwd") for g in range(ngrp)]
    (o1, l1), (o2, l2), (o3, l3) = ob_parts
    w_a, w_b, w_o, w_gu, w_d = rest_weights()
    F = w_d.shape[0]

    def merge_fwd(o1r, o2r, o3r, l1r, l2r, l3r, ob_ref):
        la, lb, lc = l1r[...], l2r[...], l3r[...]
        mx = jnp.maximum(jnp.maximum(la, lb), lc)
        ea, eb, ec = jnp.exp(la - mx), jnp.exp(lb - mx), jnp.exp(lc - mx)
        ob_ref[...] = ((ea * o1r[...] + eb * o2r[...] + ec * o3r[...]) / (ea + eb + ec)).astype(BF16)

    ob = _pcall(merge_fwd, name="merge_fwd", grid=R.grid, in_specs=[R.row(GB_W)] * 6, out_specs=R.row(GB_W),
                out_shape=sds((N, GB_W), BF16), compiler_params=_params(("parallel",)))(o1, o2, o3, l1, l2, l3)

    ya = _mm(oa, w_a, name="branch_a")
    yb = _mm(ob, w_b, b3=True, name="branch_b")

    def gate_fwd(ya_r, yb_r, ga_r, gb_r, mg_ref):
        mg_ref[...] = (jax.nn.sigmoid(ga_r[...]) * ya_r[...] + jax.nn.sigmoid(gb_r[...]) * yb_r[...]).astype(BF16)

    merged = _pcall(gate_fwd, name="gate_fwd", grid=R.grid, in_specs=[R.row(D), R.row(D), R.row(D, 0), R.row(D, 1)],
                    out_specs=R.row(D), out_shape=sds((N, D), BF16),
                    compiler_params=_params(("parallel",)))(ya, yb, gates, gates)
    y = _mm(merged, w_o, name="out_proj")

    def norm1_fwd(x_r, y_r, gm_r, g_r, b_r, sf_r, hf_r, r1_ref, x1_ref, u2_ref):
        r1 = ALPHA * x_r[...] + (1.0 + gm_r[0]) * y_r[...]
        xhat, _ = _ln_stats(r1)
        x1 = xhat * g_r[...] + b_r[...]
        r1_ref[...] = r1
        x1_ref[...] = x1
        u2_ref[...] = (x1 * (1.0 + sf_r[0]) + hf_r[0]).astype(BF16)

    r1, x1, u2 = _pcall(
        norm1_fwd, name="norm1_fwd", grid=R.grid,
        in_specs=[R.row(D), R.row(D), R.ex(D), R.const((1, D)), R.const((1, D)), R.ex(D), R.ex(D)],
        out_specs=[R.row(D)] * 3, out_shape=[sds((N, D), F32), sds((N, D), F32), sds((N, D), BF16)],
        compiler_params=_params(("parallel",)))(x2, y, gate_m, ln1_g, ln1_b, scale_f, shift_f)

    tnf = w_gu.shape[2]
    nft = w_gu.shape[0] // 2
    tmf = _pick(N, 512)

    def ffn_up(u_r, wg_r, wu_r, hg_ref, hu_ref, a_ref):
        hg = jnp.dot(u_r[...], wg_r[...], preferred_element_type=F32)
        hu = jnp.dot(u_r[...], wu_r[...], preferred_element_type=F32)
        sl, _ = _silu_parts(hg)
        hg_ref[...] = hg.astype(BF16)
        hu_ref[...] = hu.astype(BF16)
        a_ref[...] = (sl * hu).astype(BF16)

    ftile = pl.BlockSpec((tmf, tnf), lambda i, j: (i, j))
    hg, hu, act = _pcall(
        ffn_up, name="ffn_up", grid=(N // tmf, nft),
        in_specs=[pl.BlockSpec((tmf, D), lambda i, j: (i, 0)), pl.BlockSpec((None, D, tnf), lambda i, j: (j, 0, 0)),
                  pl.BlockSpec((None, D, tnf), lambda i, j: (j + nft, 0, 0))],
        out_specs=[ftile] * 3, out_shape=[sds((N, F), BF16)] * 3,
        compiler_params=_params(("parallel", "arbitrary")))(u2, w_gu, w_gu)
    y2 = _mm(act, w_d, name="ffn_down")

    def norm2_loss_bwd(x1_r, y2_r, t_r, gf_r, g_r, b_r, dy2_ref, dx1_ref, dgf_ref, dg_ref, db_ref, loss_ref):
        first = R.first_of_example()
        y2v = y2_r[...]
        r2 = ALPHA * x1_r[...] + (1.0 + gf_r[0]) * y2v
        xhat, rstd = _ln_stats(r2)
        err = xhat * g_r[...] + b_r[...] - t_r[...]
        dx2 = err * (1.0 / D)
        dr2 = _ln_bwd(dx2, xhat, rstd, g_r[...])
        dy2_ref[...] = ((1.0 + gf_r[0]) * dr2).astype(BF16)
        dx1_ref[...] = ALPHA * dr2
        _acc(dgf_ref, first, _colsum(dr2 * y2v))
        _acc(dg_ref, first, _colsum(dx2 * xhat))
        _acc(db_ref, first, _colsum(dx2))
        part = 0.5 * jnp.sum(jnp.mean(err * err, axis=-1, keepdims=True))
        _acc(loss_ref, first, jnp.broadcast_to(part, (1, 128)))

    dy2, dx1p, dgate_f, dg2, db2, loss_p = _pcall(
        norm2_loss_bwd, name="norm2_loss_bwd", grid=R.grid,
        in_specs=[R.row(D), R.row(D), R.row(D), R.ex(D), R.const((1, D)), R.const((1, D))],
        out_specs=[R.row(D), R.row(D), R.ex(D), R.ex(D), R.ex(D), R.ex(128)],
        out_shape=[sds((N, D), BF16), sds((N, D), F32), exsum(), exsum(), exsum(), exsum(128)],
        compiler_params=_params(("arbitrary",)))(x1, y2, tgt2, gate_f, ln2_g, ln2_b)

    g_wd = _mm(act, dy2, ta=True, out_dtype=BF16, name="ffn_down_dw")

    tmd = _pick(N, 256)

    def ffn_down_dx(dy_r, wd_r, hg_r, hu_r, dh_ref):
        for t in range(nft):
            cs = slice(t * tnf, (t + 1) * tnf)
            da = lax.dot_general(dy_r[...], wd_r[cs, :], _NT, preferred_element_type=F32)
            sl, dsl = _silu_parts(hg_r[:, cs].astype(F32))
            dh_ref[:, cs] = (da * hu_r[:, cs].astype(F32) * dsl).astype(BF16)
            dh_ref[:, F + t * tnf:F + (t + 1) * tnf] = (da * sl).astype(BF16)

    rowd = lambda w_: pl.BlockSpec((tmd, w_), lambda i: (i, 0))
    dh = _pcall(
        ffn_down_dx, name="ffn_down_dx", grid=(N // tmd,),
        in_specs=[rowd(D), pl.BlockSpec((F, D), lambda i: (0, 0)), rowd(F), rowd(F)],
        out_specs=rowd(2 * F), out_shape=sds((N, 2 * F), BF16),
        compiler_params=_params(("parallel",)))(dy2, w_d, hg, hu)
    du2 = _mm(dh, w_gu, tb=True, b3=True, name="ffn_up_dx")
    g_wgu = _mm(u2, dh, ta=True, out3=w_gu.shape[0], out_dtype=BF16, name="ffn_up_dw")

    def norm1_bwd(dx1p_r, du2_r, x1_r, r1_r, y_r, sf_r, gm_r, g_r,
                  dxp_ref, dy_ref, dsf_ref, dhf_ref, dgm_ref, dg_ref, db_ref):
        first = R.first_of_example()
        du2v = du2_r[...]
        dx1 = dx1p_r[...] + du2v * (1.0 + sf_r[0])
        xhat, rstd = _ln_stats(r1_r[...])
        dr1 = _ln_bwd(dx1, xhat, rstd, g_r[...])
        dxp_ref[...] = ALPHA * dr1
        dy_ref[...] = ((1.0 + gm_r[0]) * dr1).astype(BF16)
        _acc(dsf_ref, first, _colsum(du2v * x1_r[...]))
        _acc(dhf_ref, first, _colsum(du2v))
        _acc(dgm_ref, first, _colsum(dr1 * y_r[...]))
        _acc(dg_ref, first, _colsum(dx1 * xhat))
        _acc(db_ref, first, _colsum(dx1))

    dxp, dy, dscale_f, dshift_f, dgate_m, dg1, db1 = _pcall(
        norm1_bwd, name="norm1_bwd", grid=R.grid,
        in_specs=[R.row(D)] * 5 + [R.ex(D), R.ex(D), R.const((1, D))],
        out_specs=[R.row(D), R.row(D)] + [R.ex(D)] * 5,
        out_shape=[sds((N, D), F32), sds((N, D), BF16)] + [exsum()] * 5,
        compiler_params=_params(("arbitrary",)))(dx1p, du2, x1, r1, y, scale_f, gate_m, ln1_g)

    dmerged = _mm(dy, w_o, tb=True, name="out_proj_dx")
    g_wo = _mm(merged, dy, ta=True, out_dtype=BF16, name="out_proj_dw")

    def gate_bwd(dm_r, ya_r, yb_r, ga_r, gb_r, dya_ref, dyb_ref, dg_ref):
        dm = dm_r[...]
        sa, sb = jax.nn.sigmoid(ga_r[...]), jax.nn.sigmoid(gb_r[...])
        dya_ref[...] = (dm * sa).astype(BF16)
        dyb_ref[...] = (dm * sb).astype(BF16)
        dg_ref[:, :D] = (dm * ya_r[...] * sa * (1.0 - sa)).astype(BF16)
        dg_ref[:, D:] = (dm * yb_r[...] * sb * (1.0 - sb)).astype(BF16)

    dya, dyb, dgates = _pcall(
        gate_bwd, name="gate_bwd", grid=R.grid, in_specs=[R.row(D)] * 3 + [R.row(D, 0), R.row(D, 1)],
        out_specs=[R.row(D), R.row(D), R.row(2 * D)],
        out_shape=[sds((N, D), BF16), sds((N, D), BF16), sds((N, 2 * D), BF16)],
        compiler_params=_params(("parallel",)))(dmerged, ya, yb, gates, gates)

    doa = _mm(dya, w_a, tb=True, out_dtype=BF16, name="branch_a_dx")
    g_wa = _mm(oa, dya, ta=True, out_dtype=BF16, name="branch_a_dw")
    dob = _mm(dyb, w_b, tb=True, b3=True, name="branch_b_dx")
    g_wb = _mm(ob, dyb, ta=True, out3=w_b.shape[0], out_dtype=BF16, name="branch_b_dw")
    hook("rest_grads", g_wa=g_wa, g_wb=g_wb, g_wo=g_wo, g_wgu=g_wgu, g_wd=g_wd)

    seg = (jnp.arange(GB_W)[:, None] // HEAD_DIM == jnp.arange(GB_W)[None, :] // HEAD_DIM).astype(BF16)

    def merge_bwd(dob_r, o1r, o2r, o3r, l1r, l2r, l3r, seg_r, d1, d2, d3, e1, e2, e3):
        dob_v = dob_r[...]
        la, lb, lc = l1r[...], l2r[...], l3r[...]
        mx = jnp.maximum(jnp.maximum(la, lb), lc)
        ea, eb, ec = jnp.exp(la - mx), jnp.exp(lb - mx), jnp.exp(lc - mx)
        inv = 1.0 / (ea + eb + ec)
        ws = [ea * inv, eb * inv, ec * inv]

        def headsum(v):
            hi = v.astype(BF16)
            r1_ = v - hi.astype(F32)
            mid = r1_.astype(BF16)
            lo = (r1_ - mid.astype(F32)).astype(BF16)
            sm = seg_r[...]
            return (jnp.dot(hi, sm, preferred_element_type=F32) + jnp.dot(mid, sm, preferred_element_type=F32)
                    + jnp.dot(lo, sm, preferred_element_type=F32))

        dws = [headsum(dob_v * o[...]) for o in (o1r, o2r, o3r)]
        mean = ws[0] * dws[0] + ws[1] * dws[1] + ws[2] * dws[2]
        for w_, dw_, d_ref, e_ref in zip(ws, dws, (d1, d2, d3), (e1, e2, e3)):
            d_ref[...] = (w_ * dob_v).astype(BF16)
            e_ref[...] = w_ * (dw_ - mean)

    mb = _pcall(
        merge_bwd, name="merge_bwd", grid=R.grid, in_specs=[R.row(GB_W)] * 7 + [R.const((GB_W, GB_W))],
        out_specs=[R.row(GB_W)] * 6, out_shape=[sds((N, GB_W), BF16)] * 3 + [sds((N, GB_W), F32)] * 3,
        compiler_params=_params(("parallel",)))(dob, o1, o2, o3, l1, l2, l3, seg)
    do_b, dlse_b = mb[:3], mb[3:]
    hook("merge_bwd_done")

    dqkv_a, dsink = _attn_bwd(qkv[0], doa, None, cosf, sins, sinks, None, NB=NB, T=T, name="attn_a_bwd")
    hook("attn_a_bwd_done")
    dqkv = [dqkv_a]
    for g in range(ngrp):
        dqkv.append(_attn_bwd(qkv[1 + g], do_b[g], dlse_b[g], cosf, sins, sinks, g, NB=NB, T=T,
                              name=f"attn_b{g}_bwd")[0])
        hook(f"attn_b{g}_bwd_done")

    g_win = [_mm(u, dseg, ta=True, out_dtype=BF16, name=f"inproj_dw{n}") for n, dseg in enumerate(dqkv + [dgates])]
    hook("win_grads", g_win=g_win)
    wvar = lambda v: (w_in, (D, VAR_W), (0, v))
    du = _mm_multi(dqkv[:3], [wvar(v) for v in range(3)], name="inproj_dx0")
    hook("inproj_dx0_done")
    du = _mm_multi([dqkv[3], dgates], [wvar(3), (w_in, (D, 2 * D), (0, QKV_P // (2 * D)))], add=du, name="inproj_dx1")
    hook("inproj_dx1_done")

    def x_bwd(dxp_r, du_r, x_r, sm_r, gx_ref, dsm_ref, dhm_ref):
        first = R.first_of_example()
        duv = du_r[...]
        gx_ref[...] = dxp_r[...] + duv * (1.0 + sm_r[0])
        _acc(dsm_ref, first, _colsum(duv * x_r[...]))
        _acc(dhm_ref, first, _colsum(duv))

    gx, dscale_m, dshift_m = _pcall(
        x_bwd, name="x_bwd", grid=R.grid, in_specs=[R.row(D)] * 3 + [R.ex(D)],
        out_specs=[R.row(D), R.ex(D), R.ex(D)], out_shape=[sds((N, D), F32), exsum(), exsum()],
        compiler_params=_params(("arbitrary",)))(dxp, du, x2, scale_m)
    hook("x_bwd_done")

    dmod =jnp.concatenate([dshift_m, dscale_m, dgate_m, dshift_f, dscale_f, dgate_f], axis=-1)[:, 0]
    ln_grads = jnp.concatenate([dg1, db1, dg2, db2], axis=1)
    return dict(loss=loss_p[:, 0, 0], grad_x=gx.reshape(NB, T, D), g_win=g_win, g_wa=g_wa, g_wb=g_wb, g_wo=g_wo,
                g_wgu=g_wgu, g_wd=g_wd, dmod=dmod, ln_grads=ln_grads, dsink=dsink[0, :A_Q_HEADS])


def _coords():
    return lax.axis_index("x"), lax.axis_index("y"), lax.axis_index("c")


def _allgather_small(blk, *, name):
    m_per, n = blk.shape

    def body(x_ref, out_ref, send_sems, recv_sems, local_sem):
        x, y, c = _coords()
        me, sibling = (x, y, c), (x, y, 1 - c)
        chips = [(1 - x, y), (x, 1 - y), (1 - x, 1 - y)]

        def rows(px, py, pc):
            return out_ref.at[pl.ds((4 * px + 2 * py + pc) * m_per, m_per), :]

        def copy(k, block, to, src=None):
            return pltpu.make_async_remote_copy(
                src_ref=rows(*block) if src is None else src, dst_ref=rows(*block),
                send_sem=send_sems.at[k], recv_sem=recv_sems.at[k], device_id=to, device_id_type=MESH)

        mine = pltpu.make_async_copy(x_ref, rows(*me), local_sem)
        mine.start()
        first = [copy(0, me, sibling, src=x_ref)]
        first += [copy(1 + j, me, (*chip, c), src=x_ref) for j, chip in enumerate(chips)]
        for cp in first:
            cp.start()
        passed = [copy(4 + j, (*chip, c), sibling) for j, chip in enumerate(chips)]
        for j, chip in enumerate(chips):
            copy(1 + j, (*chip, c), me).wait_recv()
            passed[j].start()
        copy(0, sibling, me).wait_recv()
        for j, chip in enumerate(chips):
            copy(4 + j, (*chip, 1 - c), me).wait_recv()
        for cp in first + passed:
            cp.wait_send()
        mine.wait()

    return _pcall(
        body, name=name, out_shape=jax.ShapeDtypeStruct((8 * m_per, n), blk.dtype),
        in_specs=[pl.BlockSpec(memory_space=pltpu.VMEM)], out_specs=pl.BlockSpec(memory_space=pltpu.VMEM),
        scratch_shapes=[pltpu.SemaphoreType.DMA((7,)), pltpu.SemaphoreType.DMA((7,)), pltpu.SemaphoreType.DMA],
        compiler_params=pltpu.CompilerParams(vmem_limit_bytes=VMEM_LIMIT_BYTES),
    )(blk)


def _exchange(srcs, dsts, plan, *, name, dst_inits=None):
    na = len(dsts)
    nrem = len(plan(0, 0, 0))

    def body(*refs):
        refs = list(refs)
        src_refs = [refs.pop(0) for _ in range(na)] if srcs is not None else None
        if dst_inits is not None:
            del refs[:na]
        dst_refs, (send_sems, recv_sems) = refs[:na], refs[na:]
        start, wait = _copies(dst_refs if src_refs is None else src_refs, dst_refs, send_sems, recv_sems, plan)
        start()
        wait()

    hbm = pl.BlockSpec(memory_space=pl.ANY)
    ins = (list(srcs) if srcs is not None else []) + (list(dst_inits) if dst_inits is not None else [])
    base = na if srcs is not None else 0
    aliases = {base + a: a for a in range(na)} if dst_inits is not None else {}
    return _pcall(
        body, name=name, out_shape=list(dsts), in_specs=[hbm] * len(ins), out_specs=[hbm] * na,
        input_output_aliases=aliases,
        scratch_shapes=[pltpu.SemaphoreType.DMA((na * nrem,)), pltpu.SemaphoreType.DMA((na * nrem,))],
    )(*ins)


def _other_chips(x, y):
    return [(1 - x, y), (x, 1 - y), (1 - x, 1 - y)]


def _round(ride, carrier, name):
    if carrier is not None:
        _RIDES[carrier] = ride
        return
    srcs = ride.srcs() if callable(ride.srcs) else ride.srcs
    inits = ride.dst_inits() if callable(ride.dst_inits) else ride.dst_inits
    ride.out = list(_exchange(srcs, ride.dsts, ride.plan, name=name, dst_inits=inits))


class _Gather:
    def __init__(self, shards, chip, tag, carriers=(None, None)):
        def plan_ici(x, y, c):
            k = 2 * x + y
            return [((c,), (k, c), (2 * px + py, c), (px, py, c)) for px, py in _other_chips(x, y)]

        def plan_d2d(x, y, c):
            return [((2 * px + py, c), (2 * px + py, c), (2 * px + py, 1 - c), (x, y, 1 - c))
                    for px, py in _other_chips(x, y)]

        self.shards, self.chip = shards, chip
        dsts = [jax.ShapeDtypeStruct((4,) + s.shape, s.dtype) for s in shards]
        ici = _Ride(shards, dsts, plan_ici)
        self.d2d = _Ride(None, dsts, plan_d2d, dst_inits=lambda: ici.out)
        _round(ici, carriers[0], f"gather_{tag}_ici")
        _round(self.d2d, carriers[1], f"gather_{tag}_d2d")

    def result(self):
        full = [lax.dynamic_update_index_in_dim(f, s, self.chip, 0) for f, s in zip(self.d2d.out, self.shards)]
        return [f.reshape((4, 2 * f.shape[2], f.shape[3])) for f in full]


def _add_pairs(a, b, *, name):
    s, hr, wd = a.shape
    tr = _pick(hr, 600, 16)

    def body(a_ref, b_ref, o_ref):
        o_ref[...] = (a_ref[...].astype(F32) + b_ref[...].astype(F32)).astype(BF16)

    spec = pl.BlockSpec((1, tr, wd), lambda j, i: (j, i, 0))
    return _pcall(body, name=name, grid=(s, hr // tr), in_specs=[spec, spec], out_specs=spec,
                  out_shape=jax.ShapeDtypeStruct(a.shape, BF16), compiler_params=_params(("parallel", "parallel")))(a, b)


def _sum_chips(b, *, name):
    s, hr, wd = b.shape
    tr = _pick(hr, 600, 16)

    def body(b_ref, o_ref):
        acc = b_ref[0].astype(F32)
        for k in range(1, s):
            acc = acc + b_ref[k].astype(F32)
        o_ref[...] = acc

    return _pcall(body, name=name, grid=(hr // tr,), in_specs=[pl.BlockSpec((s, tr, wd), lambda i: (0, i, 0))],
                  out_specs=pl.BlockSpec((tr, wd), lambda i: (i, 0)), out_shape=jax.ShapeDtypeStruct((hr, wd), F32),
                  compiler_params=_params(("parallel",)))(b)


class _ReduceScatter:
    def __init__(self, gs, chip, ci, tag):
        self.gs, self.chip, self.ci, self.tag = gs, chip, ci, tag
        self.half_t = [jax.ShapeDtypeStruct((g.shape[0],) + g.shape[2:], BF16) for g in gs]

    def pair(self, carrier=None):
        plan = lambda x, y, c: [((slice(None), 1 - c), (), (), (x, y, 1 - c))]
        self.r1 = _Ride(self.gs, self.half_t, plan)
        _round(self.r1, carrier, f"reduce_{self.tag}_pair")

    def chips(self, carrier=None):
        def plan(x, y, c):
            k = 2 * x + y
            return [((2 * px + py,), (k,), (2 * px + py,), (px, py, c)) for px, py in _other_chips(x, y)]

        self.pairs = [_add_pairs(lax.dynamic_index_in_dim(g, self.ci, 1, keepdims=False), f,
                                 name=f"reduce_{self.tag}_pair_add{n}")
                      for n, (g, f) in enumerate(zip(self.gs, self.r1.out))]
        self.r2 = _Ride(self.pairs, self.half_t, plan)
        _round(self.r2, carrier, f"reduce_{self.tag}_chips")

    def halves(self, carrier=None):
        plan = lambda x, y, c: [((), (c,), (1 - c,), (x, y, 1 - c))]
        self.mine = []
        for n, (l, p) in enumerate(zip(self.r2.out, self.pairs)):
            own = lax.dynamic_index_in_dim(p, self.chip, 0, keepdims=False)
            self.mine.append(_sum_chips(lax.dynamic_update_index_in_dim(l, own, self.chip, 0),
                                        name=f"reduce_{self.tag}_chip_sum{n}"))
        self.r3 = _Ride(self.mine, [jax.ShapeDtypeStruct((2,) + m.shape, F32) for m in self.mine], plan)
        _round(self.r3, carrier, f"reduce_{self.tag}_halves")

    def result(self):
        return [lax.dynamic_update_index_in_dim(b, m, self.ci, 0).reshape(2 * m.shape[0], m.shape[1])
                for b, m in zip(self.r3.out, self.mine)]


def _ada_fwd(c_all, w_sh, b_sh, *, name):
    nb, d = c_all.shape
    wcols = w_sh.shape[1]
    tn = _pick(wcols, 512)

    def body(c_ref, w_ref, b_ref, o_ref, a_ref):
        cv = c_ref[...]
        act = cv * jax.nn.sigmoid(cv)
        a_ref[...] = act
        o_ref[...] = jnp.dot(act.astype(BF16), w_ref[...].astype(BF16), preferred_element_type=F32) + b_ref[...]

    return _pcall(
        body, name=name, grid=(wcols // tn,),
        in_specs=[pl.BlockSpec((nb, d), lambda j: (0, 0)), pl.BlockSpec((d, tn), lambda j: (0, j)),
                  pl.BlockSpec((1, tn), lambda j: (0, j))],
        out_specs=[pl.BlockSpec((nb, tn), lambda j: (0, j)), pl.BlockSpec((nb, d), lambda j: (0, 0))],
        out_shape=[jax.ShapeDtypeStruct((nb, wcols), F32), jax.ShapeDtypeStruct((nb, d), F32)],
        compiler_params=_params(("arbitrary",)))(c_all, w_sh, b_sh)


def _sum_devices(g, *, name):
    nd, m, w = g.shape

    def body(g_ref, o_ref):
        acc = g_ref[0]
        for k in range(1, nd):
            acc = acc + g_ref[k]
        o_ref[...] = acc

    return _pcall(body, name=name, out_shape=jax.ShapeDtypeStruct((m, w), F32),
                  compiler_params=pltpu.CompilerParams(vmem_limit_bytes=VMEM_LIMIT_BYTES))(g)


def _adamw(w, g, m, v, *, name):
    rows, cols = w.shape
    tr = _pick(rows, max(8, (1 << 18) // cols), 8)
    c1 = 1.0 / (1.0 - ADAM_B1 ** ADAM_STEP)
    c2 = 1.0 / (1.0 - ADAM_B2 ** ADAM_STEP)

    def body(w_ref, g_ref, m_ref, v_ref, d_ref, nm_ref, nv_ref):
        gv = g_ref[...]
        nm = ADAM_B1 * m_ref[...] + (1.0 - ADAM_B1) * gv
        nv = ADAM_B2 * v_ref[...] + (1.0 - ADAM_B2) * (gv * gv)
        d_ref[...] = -ADAM_LR * ((nm * c1) / (jnp.sqrt(nv * c2) + ADAM_EPS) + ADAM_WD * w_ref[...])
        nm_ref[...] = nm
        nv_ref[...] = nv

    spec = pl.BlockSpec((tr, cols), lambda i: (i, 0))
    shp = jax.ShapeDtypeStruct((rows, cols), F32)
    return _pcall(body, name=name, grid=(rows // tr,), in_specs=[spec] * 4, out_specs=[spec] * 3,
                  out_shape=[shp] * 3, compiler_params=_params(("parallel",)))(w, g, m, v)


def _permute_in_cols(w):
    ngrp = len(B_PATTERNS)
    qb, kb, vb = (w[:, A_W + n * QB_W:A_W + (n + 1) * QB_W] for n in range(3))
    parts = [w[:, :A_W], jnp.zeros((w.shape[0], VAR_W - A_W), w.dtype)]
    for g in range(ngrp):
        parts += [t[:, g * GB_W:(g + 1) * GB_W] for t in (qb, kb, vb)]
    return jnp.concatenate(parts + [w[:, A_W + 3 * QB_W:]], axis=1)


def _unpermute_in_grads(pieces):
    ga, groups, gg = pieces[0], pieces[1:-1], pieces[-1]
    cols = [ga[:, :A_W]]
    for n in range(3):
        cols += [gp[:, n * GB_W:(n + 1) * GB_W] for gp in groups]
    return jnp.concatenate(cols + [gg], axis=1)


def kernel(x, c, positions, w_ada, b_ada, w_in, sinks, w_branch_a, w_branch_b, w_o, ln1_g, ln1_b, w_gate_up, w_down, ln2_g, ln2_b, loss_target, m_w_ada, m_b_ada, m_w_in, m_sinks, m_w_branch_a, m_w_branch_b, m_w_o, m_ln1_g, m_ln1_b, m_w_gate_up, m_w_down, m_ln2_g, m_ln2_b, v_w_ada, v_b_ada, v_w_in, v_sinks, v_w_branch_a, v_w_branch_b, v_w_o, v_ln1_g, v_ln1_b, v_w_gate_up, v_w_down, v_ln2_g, v_ln2_b):
    xi, yi, ci = _coords()
    chip = 2 * xi + yi
    dev = 4 * xi + 2 * yi + ci
    NB, T, D = x.shape
    nchip, ndev = 4, 8
    ada_cols = w_ada.shape[2]

    c_blk = jnp.zeros((8, D), F32).at[:NB].set(c)
    c_all = _allgather_small(c_blk, name="gather_c").reshape(ndev, 8, D)[:, :NB].reshape(ndev * NB, D)
    b_sh = lax.dynamic_slice(b_ada, (0, chip * ada_cols), (1, ada_cols))
    mod_part, c_act = _ada_fwd(c_all, w_ada[0], b_sh, name="ada_fwd")
    mod_g = _allgather_small(mod_part, name="gather_mod").reshape(nchip, 2, ndev * NB, ada_cols)[:, 0]
    mod_all = jnp.transpose(mod_g, (1, 0, 2)).reshape(ndev * NB, nchip * ada_cols)
    mod = lax.dynamic_slice(mod_all, (NB * dev, 0), (NB, nchip * ada_cols))

    ra, ro, rd = w_branch_a.shape[1], w_o.shape[1], w_down.shape[1]
    rowsh = jnp.concatenate([w_branch_a[0], w_o[0], w_down[0]], axis=0)
    halves = lambda a: a.reshape(a.shape[:-2] + (2, a.shape[-2] // 2, a.shape[-1]))
    whole = lambda a: a.reshape(a.shape[:-3] + (2 * a.shape[-2], a.shape[-1]))
    shards = [halves(w.astype(BF16)) for w in (w_in[0], rowsh, w_branch_b[0], w_gate_up[0])]
    (g_in,) = _Gather(shards[:1], chip, "w_in").result()
    w_in_f = _permute_in_cols(jnp.concatenate([g_in[k] for k in range(nchip)], axis=1))
    rest = _Gather(shards[1:], chip, "w_rest", carriers=("inproj_qkv", "attn_a_fwd"))

    def rest_weights():
        g_rows, w_b_f, w_gu_f = rest.result()
        return (g_rows[:, :ra].reshape(nchip * ra, D), w_b_f, g_rows[:, ra:ra + ro].reshape(nchip * ro, D), w_gu_f,
                g_rows[:, ra + ro:].reshape(nchip * rd, D))

    red = {}

    def hook(event, **g):
        if event == "rest_grads":
            gr_rows = jnp.concatenate([g["g_wa"].reshape(nchip, ra, D), g["g_wo"].reshape(nchip, ro, D),
                                       g["g_wd"].reshape(nchip, rd, D)], axis=1)
            red["rest"] = _ReduceScatter([halves(a) for a in (gr_rows, g["g_wb"], g["g_wgu"])], chip, ci, "rest")
            red["rest"].pair(carrier="merge_bwd")
        elif event == "merge_bwd_done":
            red["rest"].chips(carrier="attn_a_bwd")
        elif event == "attn_a_bwd_done":
            red["rest"].halves(carrier="attn_b0_bwd")
        elif event == "win_grads":
            gr_in = jnp.stack(jnp.split(_unpermute_in_grads(g["g_win"]), nchip, axis=1))
            red["w_in"] = _ReduceScatter([halves(gr_in)], chip, ci, "w_in")
            red["w_in"].pair(carrier="inproj_dx0")
        elif event == "inproj_dx0_done":
            red["w_in"].chips(carrier="inproj_dx1")
        elif event == "inproj_dx1_done":
            red["w_in"].halves(carrier="x_bwd")

    res = _local_step(x, mod, positions, w_in_f, rest_weights, sinks[0], ln1_g, ln1_b, ln2_g, ln2_b, loss_target, hook)
    (g_w_in,) = red["w_in"].result()
    g_rows_red, g_w_b, g_w_gu = red["rest"].result()
    g_w_a, g_w_o, g_w_d = g_rows_red[:ra], g_rows_red[ra:ra + ro], g_rows_red[ra + ro:]

    small_rows = 24
    misc = jnp.zeros((1, D), F32).at[0, :A_Q_HEADS].set(res["dsink"]).at[0, A_Q_HEADS].set(jnp.sum(res["loss"]))
    small = jnp.concatenate([res["dmod"].reshape(NB * 6, D), jnp.sum(res["ln_grads"], axis=0), misc,
                             jnp.zeros((small_rows - NB * 6 - 5, D), F32)], axis=0)
    small_all = _allgather_small(small, name="gather_small").reshape(ndev, small_rows, D)
    dmod_all = small_all[:, :NB * 6].reshape(ndev * NB, 6 * D)
    sums = _sum_devices(small_all, name="sum_small")
    g_b_ada = (sums[0:6] + sums[6:12]).reshape(1, 6 * D)
    g_ln1_g, g_ln1_b, g_ln2_g, g_ln2_b = (sums[12 + n][None] for n in range(4))
    g_sinks = sums[16, :A_Q_HEADS][None]
    loss = sums[16, A_Q_HEADS]
    dmod_sh = lax.dynamic_slice(dmod_all, (0, chip * ada_cols), (ndev * NB, ada_cols))
    g_w_ada = _mm(c_act, dmod_sh, ta=True, name="ada_dw")

    names = ["w_ada", "b_ada", "w_in", "sinks", "w_branch_a", "w_branch_b", "w_o", "ln1_g", "ln1_b",
             "w_gate_up", "w_down", "ln2_g", "ln2_b"]
    ws = [w_ada, b_ada, w_in, sinks, w_branch_a, w_branch_b, w_o, ln1_g, ln1_b, w_gate_up, w_down, ln2_g, ln2_b]
    ms = [m_w_ada, m_b_ada, m_w_in, m_sinks, m_w_branch_a, m_w_branch_b, m_w_o, m_ln1_g, m_ln1_b, m_w_gate_up,
          m_w_down, m_ln2_g, m_ln2_b]
    vs = [v_w_ada, v_b_ada, v_w_in, v_sinks, v_w_branch_a, v_w_branch_b, v_w_o, v_ln1_g, v_ln1_b, v_w_gate_up,
          v_w_down, v_ln2_g, v_ln2_b]
    gs = [g_w_ada, g_b_ada, g_w_in, g_sinks, g_w_a, g_w_b, g_w_o, g_ln1_g, g_ln1_b, g_w_gu, g_w_d, g_ln2_g, g_ln2_b]
    grads, deltas, new_ms, new_vs = [], [], [], []
    for name, w, g, m, v in zip(names, ws, gs, ms, vs):
        shp = w.shape
        w2, m2, v2 = (a.reshape(shp[-2], shp[-1]) for a in (w, m, v))
        g2 = g.reshape(shp[-2], shp[-1])
        d, nm, nv = _adamw(w2, g2, m2, v2, name="adamw_" + name)
        grads.append(g2.reshape(shp))
        deltas.append(d.reshape(shp))
        new_ms.append(nm.reshape(shp))
        new_vs.append(nv.reshape(shp))
    return (loss, res["grad_x"], *grads, *deltas, *new_ms, *new_vs)
```

```python
import functools

import jax
import jax.numpy as jnp
from jax import lax
from jax.experimental import pallas as pl
from jax.experimental.pallas import tpu as pltpu

F32 = jnp.float32
BF16 = jnp.bfloat16
MESH = pl.DeviceIdType.MESH

HEAD_DIM = 64
PAIR_W = 2 * HEAD_DIM
BLOCK = 128
A_Q_HEADS = 16
A_KV_HEADS = 2
A_WINDOW = 128
B_PATTERNS = ((128, 1), (512, 4), (2048, 16))
B_GROUP_HEADS = 8
QA_W = A_Q_HEADS * HEAD_DIM
KA_W = A_KV_HEADS * HEAD_DIM
GB_W = B_GROUP_HEADS * HEAD_DIM
QB_W = GB_W * len(B_PATTERNS)
A_W = QA_W + 2 * KA_W
VAR_W = 3 * GB_W
N_VAR = 1 + len(B_PATTERNS)
QKV_P = N_VAR * VAR_W
ROPE_THETA = 10000.0
LN_EPS = 1e-5
NEG_INF = -1e30
DEPTH = 1
ALPHA = (2 * DEPTH) ** 0.25
SCALE = HEAD_DIM ** -0.5

ADAM_LR, ADAM_B1, ADAM_B2, ADAM_EPS, ADAM_WD, ADAM_STEP = 0.001, 0.9, 0.999, 1e-08, 0.01, 10

VMEM_LIMIT_BYTES = 56 * 1024 * 1024


def _params(sem=None):
    return pltpu.CompilerParams(dimension_semantics=sem, vmem_limit_bytes=VMEM_LIMIT_BYTES)


_RIDES = {}


def _pcall(body, *, name, **kw):
    ride = _RIDES.pop(name, None)
    if ride is None:
        return pl.pallas_call(body, name=name, **kw)
    return _riding_call(body, ride, name=name, **kw)


def _copies(src_refs, dst_refs, send_sems, recv_sems, plan):
    x, y, c = lax.axis_index("x"), lax.axis_index("y"), lax.axis_index("c")
    remote = plan(x, y, c)
    nrem = len(remote)
    at = lambda ref, idx: ref.at[idx] if idx else ref

    def copy(a, n, landing):
        si, di, ri, peer = remote[n]
        return pltpu.make_async_remote_copy(
            src_ref=at(src_refs[a], si), dst_ref=at(dst_refs[a], ri if landing else di),
            send_sem=send_sems.at[a * nrem + n], recv_sem=recv_sems.at[a * nrem + n],
            device_id=peer, device_id_type=MESH)

    order = [(a, n) for a in range(len(dst_refs)) for n in range(nrem)]

    def start():
        for a, n in order:
            copy(a, n, False).start()

    def wait():
        for a, n in order:
            copy(a, n, True).wait_recv()
        for a, n in order:
            copy(a, n, False).wait_send()

    return start, wait


class _Ride:
    def __init__(self, srcs, dsts, plan, dst_inits=None):
        self.srcs, self.dsts, self.plan, self.dst_inits, self.out = srcs, dsts, plan, dst_inits, None


def _riding_call(body, ride, *, name, grid, in_specs, out_specs, out_shape, scratch_shapes=(), **kw):
    single = not isinstance(out_specs, (list, tuple))
    out_specs = [out_specs] if single else list(out_specs)
    out_shape = [out_shape] if single else list(out_shape)
    srcs = ride.srcs() if callable(ride.srcs) else ride.srcs
    inits = ride.dst_inits() if callable(ride.dst_inits) else ride.dst_inits
    xin = (list(srcs) if srcs is not None else []) + (list(inits) if inits is not None else [])
    na, nrem = len(ride.dsts), len(ride.plan(0, 0, 0))
    n_in, n_out, n_scr = len(in_specs), len(out_specs), len(scratch_shapes)

    def wrapped(*refs):
        ins, xins = refs[:n_in], refs[n_in:n_in + len(xin)]
        outs = refs[n_in + len(xin):n_in + len(xin) + n_out]
        xouts = refs[n_in + len(xin) + n_out:n_in + len(xin) + n_out + na]
        scr = refs[n_in + len(xin) + n_out + na:]
        start, wait = _copies(xins[:na] if srcs is not None else xouts, xouts, scr[n_scr], scr[n_scr + 1], ride.plan)
        ids = [pl.program_id(a) for a in range(len(grid))]
        first = functools.reduce(jnp.logical_and, [i == 0 for i in ids])
        last = functools.reduce(jnp.logical_and, [i == g - 1 for i, g in zip(ids, grid)])
        pl.when(first)(start)
        body(*ins, *outs, *scr[:n_scr])
        pl.when(last)(wait)

    hbm = pl.BlockSpec(memory_space=pl.ANY)
    base = n_in + (na if srcs is not None else 0)
    aliases = {base + a: n_out + a for a in range(na)} if inits is not None else {}

    def run(*args):
        res = pl.pallas_call(
            wrapped, name=name, grid=grid, in_specs=list(in_specs) + [hbm] * len(xin),
            out_specs=out_specs + [hbm] * na, out_shape=out_shape + list(ride.dsts),
            scratch_shapes=list(scratch_shapes) + [pltpu.SemaphoreType.DMA((na * nrem,)),
                                                   pltpu.SemaphoreType.DMA((na * nrem,))],
            input_output_aliases=aliases, compiler_params=_params(("arbitrary",) * len(grid)),
        )(*args, *xin)
        ride.out = list(res[n_out:])
        return res[0] if single else list(res[:n_out])

    return run


def _pick(n, target, quantum=128):
    t = (min(target, n) // quantum) * quantum
    while t >= quantum:
        if n % t == 0:
            return t
        t -= quantum
    return n


def _mm(a, b, *, name, ta=False, tb=False, b3=False, out3=0, out_dtype=F32, add=None, tm=1024, tn=1536, tk=1536):
    if ta:
        K, M = a.shape
    else:
        M, K = a.shape
    if b3 and tb:
        Nn, K2, tk = b.shape[1], b.shape[0] * b.shape[2], b.shape[2]
    elif b3:
        K2, Nn, tn = b.shape[1], b.shape[0] * b.shape[2], b.shape[2]
    elif tb:
        Nn, K2 = b.shape
    else:
        K2, Nn = b.shape
    assert K == K2, (a.shape, b.shape)
    if out3:
        tn = Nn // out3
    tm, tn, tk = _pick(M, tm), _pick(Nn, tn), _pick(K, tk)
    nk = K // tk
    dn = (((0 if ta else 1,), (1 if tb else 0,)), ((), ()))

    def body(*refs):
        refs = list(refs)
        a_ref, b_ref = refs[:2]
        add_ref = refs[2] if add is not None else None
        o_ref = refs[3] if add is not None else refs[2]
        part = lax.dot_general(a_ref[...].astype(BF16), b_ref[...].astype(BF16), dn, preferred_element_type=F32)

        def finish(r):
            if add is not None:
                r = r + add_ref[...]
            o_ref[...] = r.astype(out_dtype)

        if nk == 1:
            finish(part)
            return
        acc = refs[-1]
        k = pl.program_id(2)

        @pl.when(k == 0)
        def _():
            acc[...] = part

        @pl.when(k > 0)
        def _():
            acc[...] += part

        @pl.when(k == nk - 1)
        def _():
            finish(acc[...])

    a_spec = pl.BlockSpec((tk, tm), lambda i, j, k: (k, i)) if ta else pl.BlockSpec((tm, tk), lambda i, j, k: (i, k))
    if b3 and tb:
        b_spec = pl.BlockSpec((None, tn, tk), lambda i, j, k: (k, j, 0))
    elif b3:
        b_spec = pl.BlockSpec((None, tk, tn), lambda i, j, k: (j, k, 0))
    elif tb:
        b_spec = pl.BlockSpec((tn, tk), lambda i, j, k: (j, k))
    else:
        b_spec = pl.BlockSpec((tk, tn), lambda i, j, k: (k, j))
    if out3:
        o_spec = pl.BlockSpec((None, tm, tn), lambda i, j, k: (j, i, 0))
    else:
        o_spec = pl.BlockSpec((tm, tn), lambda i, j, k: (i, j))
    ins, specs = [a, b], [a_spec, b_spec]
    if add is not None:
        ins.append(add)
        specs.append(o_spec)
    return _pcall(
        body, name=name, grid=(M // tm, Nn // tn, nk), in_specs=specs, out_specs=o_spec,
        out_shape=jax.ShapeDtypeStruct((out3, M, tn) if out3 else (M, Nn), out_dtype),
        scratch_shapes=[pltpu.VMEM((tm, tn), F32)] if nk > 1 else [],
        compiler_params=_params(("parallel", "parallel", "arbitrary")),
    )(*ins)


def _mm_multi(a_list, b_list, *, name, add=None, out_dtype=F32, tm=512):
    M = a_list[0].shape[0]
    tm = _pick(M, tm)
    ns = len(a_list)
    b_arrs, b_specs = [], []
    for b in b_list:
        arr, shp, idx = b if isinstance(b, tuple) else (b, b.shape, (0, 0))
        b_arrs.append(arr)
        b_specs.append(pl.BlockSpec(shp, lambda i, idx=idx: idx))
    Nn = b_specs[0].block_shape[0]
    dn = (((1,), (1,)), ((), ()))

    def body(*refs):
        a_refs, b_refs = refs[:ns], refs[ns:2 * ns]
        acc = None
        for a_ref, b_ref in zip(a_refs, b_refs):
            part = lax.dot_general(a_ref[...].astype(BF16), b_ref[...], dn, preferred_element_type=F32)
            acc = part if acc is None else acc + part
        if add is not None:
            acc = acc + refs[2 * ns][...]
        refs[-1][...] = acc.astype(out_dtype)

    o_spec = pl.BlockSpec((tm, Nn), lambda i: (i, 0))
    specs = [pl.BlockSpec((tm, a.shape[1]), lambda i: (i, 0)) for a in a_list] + b_specs
    ins = list(a_list) + b_arrs
    if add is not None:
        specs.append(o_spec)
        ins.append(add)
    return _pcall(body, name=name, grid=(M // tm,), in_specs=specs, out_specs=o_spec,
                  out_shape=jax.ShapeDtypeStruct((M, Nn), out_dtype), compiler_params=_params(("parallel",)))(*ins)


def _lane(shape):
    return lax.broadcasted_iota(jnp.int32, shape, len(shape) - 1)


def _rot_half(v):
    w = v.shape[-1]
    first = (_lane(v.shape) % HEAD_DIM) < (HEAD_DIM // 2)
    return jnp.where(first, pltpu.roll(v, w - HEAD_DIM // 2, v.ndim - 1), pltpu.roll(v, HEAD_DIM // 2, v.ndim - 1))


def _widen(t, w):
    return t if w == t.shape[-1] else jnp.concatenate([t] * (w // t.shape[-1]), axis=-1)


def _unrope(v, cos, sins):
    w = v.shape[-1]
    return v * _widen(cos, w) - _rot_half(v) * _widen(sins, w)


def _rope_tables(positions):
    half = HEAD_DIM // 2
    inv = ROPE_THETA ** (-jnp.arange(half, dtype=F32) / half)
    ang = positions.astype(F32)[..., None] * inv
    cos, sin = jnp.cos(ang), jnp.sin(ang)
    cosf = jnp.concatenate([cos, cos, cos, cos], axis=-1)
    sins = jnp.concatenate([-sin, sin, -sin, sin], axis=-1)
    n = positions.shape[0] * positions.shape[1]
    return cosf.reshape(n, PAIR_W), sins.reshape(n, PAIR_W)


def _inproj(x2, scale, shift, w, cosf, sins, flags, *, T, name):
    N, D = x2.shape
    tm, tn = _pick(T, 512), VAR_W
    tpe = T // tm

    def body(x_ref, sc_ref, sh_ref, w_ref, c_ref, s_ref, f_ref, *outs):
        o_refs, u_ref = outs[:N_VAR], outs[N_VAR]
        j = pl.program_id(1)

        @pl.when(j == 0)
        def _():
            u_ref[...] = (x_ref[...] * (1.0 + sc_ref[0]) + sh_ref[0]).astype(BF16)

        acc = jnp.dot(u_ref[...], w_ref[...], preferred_element_type=F32)
        fl = f_ref[...]
        ce = 1.0 + (_widen(c_ref[...], tn) - 1.0) * fl
        se = _widen(s_ref[...], tn) * fl
        res = (acc * ce + _rot_half(acc) * se).astype(BF16)
        for v in range(N_VAR):
            @pl.when(j == v)
            def _(v=v):
                o_refs[v][...] = res

    ex = pl.BlockSpec((1, 1, D), lambda i, j: (i // tpe, 0, 0))
    tab = pl.BlockSpec((tm, PAIR_W), lambda i, j: (i, 0))
    keep = lambda w_: pl.BlockSpec((tm, w_), lambda i, j: (i, 0))
    return _pcall(
        body, name=name, grid=(N // tm, N_VAR),
        in_specs=[keep(D), ex, ex, pl.BlockSpec((D, tn), lambda i, j: (0, j)), tab, tab,
                  pl.BlockSpec((1, tn), lambda i, j: (0, j))],
        out_specs=[keep(tn)] * N_VAR + [keep(D)],
        out_shape=[jax.ShapeDtypeStruct((N, tn), BF16)] * N_VAR + [jax.ShapeDtypeStruct((N, D), BF16)],
        compiler_params=_params(("parallel", "arbitrary")),
    )(x2, scale, shift, w, cosf, sins, flags)


class _Geom:
    def __init__(self, g):
        if g is None:
            self.r, self.nq, self.n_back, self.sink = 1, A_Q_HEADS, A_WINDOW - 1, True
            self.qw, self.kw = QA_W, KA_W
            self.qidx = lambda j: 0
            self.kidx = lambda j: QA_W // KA_W
            self.vidx = lambda j: QA_W // KA_W + 1
        else:
            window, r = B_PATTERNS[g]
            self.r, self.nq, self.n_back, self.sink = r, B_GROUP_HEADS, window // r, False
            self.qw, self.kw = GB_W, GB_W
            self.qidx = lambda j: 3 * j
            self.kidx = lambda j: 3 * j + 1
            self.vidx = lambda j: 3 * j + 2
        self.ntile = self.qw // PAIR_W


def _stack_heads(t, scale=None):
    first = _lane(t.shape) < HEAD_DIM
    z = jnp.zeros_like(t)
    if scale is not None:
        t = t * jnp.asarray(scale, t.dtype)
    return jnp.concatenate([jnp.where(first, t, z), jnp.where(first, z, t)], axis=0)


def _lse_col(t):
    return jnp.concatenate([t[:, 0:1], t[:, HEAD_DIM:HEAD_DIM + 1]], axis=0)


def _unstack_heads(v2):
    return jnp.where(_lane((BLOCK, PAIR_W)) < HEAD_DIM, v2[:BLOCK], v2[BLOCK:])


def _dup_head(t, kh):
    tf = t.astype(F32)
    keep = (_lane(t.shape) < HEAD_DIM) if kh == 0 else (_lane(t.shape) >= HEAD_DIM)
    return jnp.where(keep, tf, pltpu.roll(tf, HEAD_DIM, 1)).astype(t.dtype)


def _fold_heads(t):
    return t + pltpu.roll(t, HEAD_DIM, 1)


def _band_mask(rows, i, n_back, single):
    nkeys = BLOCK if single else 2 * BLOCK
    qi = jnp.bitwise_and(lax.broadcasted_iota(jnp.int32, (rows, nkeys), 0), BLOCK - 1)
    ki = lax.broadcasted_iota(jnp.int32, (rows, nkeys), 1)
    if single:
        return qi >= ki
    dist = qi + BLOCK - ki
    return jnp.logical_and(jnp.logical_and(dist >= 0, dist <= n_back), jnp.logical_or(ki >= BLOCK, i > 0))


def _softmax_parts(s, valid, sinkcol):
    s = jnp.where(valid, s, NEG_INF)
    m = jnp.max(s, axis=1, keepdims=True)
    if sinkcol is not None:
        m = jnp.maximum(m, sinkcol)
    p = jnp.exp(s - m)
    den = jnp.sum(p, axis=1, keepdims=True)
    es = None
    if sinkcol is not None:
        es = jnp.exp(sinkcol - m)
        den = den + es
    return p, m, den, es


_NT = (((1,), (1,)), ((), ()))
_TN = (((0,), (0,)), ((), ()))


def _rows2(prev_ref, cur_ref, cs, single=False):
    if single:
        return cur_ref[0, :, cs]
    return jnp.concatenate([prev_ref[0, :, cs], cur_ref[0, :, cs]], axis=0)


def _sink_col(sink_ref, kh, nblocks):
    return jnp.concatenate([jnp.full((BLOCK, 1), sink_ref[kh * nblocks + b], F32) for b in range(nblocks)], axis=0)


def _tile(t):
    return slice(t * PAIR_W, (t + 1) * PAIR_W)


def _attn_fwd(qkv, sinks, g, *, NB, T, name):
    geo = _Geom(g)
    r, qw, kw, ntile = geo.r, geo.qw, geo.kw, geo.ntile
    tsub = T // r
    nblk = tsub // BLOCK
    qkv3 = qkv.reshape(NB, tsub, r * VAR_W)
    out_dtype = BF16 if g is None else F32
    tiles_per_kv = ntile // A_KV_HEADS

    single = nblk == 1

    def body(q_ref, kp_ref, kc_ref, vp_ref, vc_ref, sink_ref, o_ref, l_ref):
        i = pl.program_id(2)
        if geo.sink:
            kall, vall = _rows2(kp_ref, kc_ref, _tile(0)), _rows2(vp_ref, vc_ref, _tile(0))
            tiles = [[kh * tiles_per_kv + t for t in range(tiles_per_kv)] for kh in range(A_KV_HEADS)]
            q2s = [jnp.concatenate([_stack_heads(q_ref[0, :, _tile(t)], SCALE) for t in ts], axis=0) for ts in tiles]
            kks = [_dup_head(kall, kh) for kh in range(A_KV_HEADS)]
            vvs = [_dup_head(vall, kh) for kh in range(A_KV_HEADS)]
            sinkcols = [_sink_col(sink_ref, kh, 2 * tiles_per_kv) for kh in range(A_KV_HEADS)]
        else:
            tiles = [[t] for t in range(ntile)]
            q2s = [_stack_heads(q_ref[0, :, _tile(t)], SCALE) for t in range(ntile)]
            kks = [_rows2(kp_ref, kc_ref, _tile(t), single) for t in range(ntile)]
            vvs = [_rows2(vp_ref, vc_ref, _tile(t), single) for t in range(ntile)]
            sinkcols = [None] * ntile
        valid = _band_mask(q2s[0].shape[0], i, geo.n_back, single)
        ss = [lax.dot_general(q2, kk, _NT, preferred_element_type=F32) for q2, kk in zip(q2s, kks)]
        parts = [_softmax_parts(s, valid, sc) for s, sc in zip(ss, sinkcols)]
        o2s = [jnp.dot(p.astype(BF16), vv, preferred_element_type=F32) / den for (p, m, den, _), vv in zip(parts, vvs)]
        for ts, o2, (p, m, den, _) in zip(tiles, o2s, parts):
            lse2 = jnp.broadcast_to(m + jnp.log(den), (o2.shape[0], PAIR_W))
            for n, t in enumerate(ts):
                rows = slice(2 * BLOCK * n, 2 * BLOCK * (n + 1))
                o_ref[0, :, _tile(t)] = _unstack_heads(o2[rows]).astype(out_dtype)
                l_ref[0, :, _tile(t)] = _unstack_heads(lse2[rows])

    prev = lambda i: jnp.maximum(i - 1, 0)
    in_specs = [
        pl.BlockSpec((1, BLOCK, qw), lambda b, j, i: (b, i, geo.qidx(j))),
        pl.BlockSpec((1, BLOCK, kw), lambda b, j, i: (b, prev(i), geo.kidx(j))),
        pl.BlockSpec((1, BLOCK, kw), lambda b, j, i: (b, i, geo.kidx(j))),
        pl.BlockSpec((1, BLOCK, kw), lambda b, j, i: (b, prev(i), geo.vidx(j))),
        pl.BlockSpec((1, BLOCK, kw), lambda b, j, i: (b, i, geo.vidx(j))),
        pl.BlockSpec(memory_space=pltpu.SMEM),
    ]
    o_spec = pl.BlockSpec((1, BLOCK, qw), lambda b, j, i: (b, i, j))
    shape = (NB, tsub, r * qw)
    o, lse = _pcall(
        body, name=name, grid=(NB, r, nblk), in_specs=in_specs, out_specs=[o_spec, o_spec],
        out_shape=[jax.ShapeDtypeStruct(shape, out_dtype), jax.ShapeDtypeStruct(shape, F32)],
        compiler_params=_params(("parallel", "parallel", "arbitrary")),
    )(qkv3, qkv3, qkv3, qkv3, qkv3, sinks)
    return o.reshape(NB * T, qw), lse.reshape(NB * T, qw)


def _attn_bwd(qkv, do, lse, dlse, cosf, sins, sinks, g, *, NB, T, name):
    geo = _Geom(g)
    r, qw, kw, ntile = geo.r, geo.qw, geo.kw, geo.ntile
    tsub = T // r
    nblk = tsub // BLOCK
    view = lambda a, w: a.reshape(NB, tsub, r * w)
    has_dlse = dlse is not None
    tiles_per_kv = ntile // A_KV_HEADS

    single = nblk == 1
    krows = BLOCK if single else 2 * BLOCK
    nsteps = 1 if single else nblk + 1

    def grads(q2s, kks, vvs, do2s, i, lsecols, sinkcols, dlcols):
        valid = _band_mask(q2s[0].shape[0], i, geo.n_back, single)
        ss = [lax.dot_general(q2, kk, _NT, preferred_element_type=F32) for q2, kk in zip(q2s, kks)]
        dps = [lax.dot_general(do2, vv, _NT, preferred_element_type=F32) for do2, vv in zip(do2s, vvs)]
        ps, dss, sks = [], [], []
        for s, dp, ls, sc, dl in zip(ss, dps, lsecols, sinkcols, dlcols):
            p = jnp.exp(jnp.where(valid, s, NEG_INF) - ls)
            delta = jnp.sum(p * dp, axis=1, keepdims=True)
            sks.append(jnp.exp(sc - ls) * delta if sc is not None else None)
            if dl is not None:
                delta = delta - dl
            ps.append(p.astype(BF16))
            dss.append((p * (dp - delta)).astype(BF16))
        dq2s = [jnp.dot(ds, kk, preferred_element_type=F32) * SCALE for ds, kk in zip(dss, kks)]
        dkks = [lax.dot_general(ds, q2, _TN, preferred_element_type=F32) for ds, q2 in zip(dss, q2s)]
        dvvs = [lax.dot_general(p, do2, _TN, preferred_element_type=F32) for p, do2 in zip(ps, do2s)]
        return dq2s, dkks, dvvs, sks

    def body(*refs):
        it = iter(refs)
        q_ref, kp_ref, kc_ref, vp_ref, vc_ref, do_ref, l_ref = (next(it) for _ in range(7))
        dl_ref = next(it) if has_dlse else None
        c_ref, s_ref, sink_ref, o_ref, ds_ref, dq_s, dk_s, dv_s, car_q, car_k, car_v = (next(it) for _ in range(11))
        b, j, i = pl.program_id(0), pl.program_id(1), pl.program_id(2)

        @pl.when(jnp.logical_and(b == 0, jnp.logical_and(j == 0, i == 0)))
        def _():
            ds_ref[...] = jnp.zeros_like(ds_ref)

        def compute():
            if geo.sink:
                kall, vall = _rows2(kp_ref, kc_ref, _tile(0)), _rows2(vp_ref, vc_ref, _tile(0))
                nb = 2 * tiles_per_kv
                tiles = [[kh * tiles_per_kv + t for t in range(tiles_per_kv)] for kh in range(A_KV_HEADS)]
                cat = lambda f, ts: jnp.concatenate([f(t) for t in ts], axis=0)
                dq2s, dkks, dvvs, sks = grads(
                    [cat(lambda t: _stack_heads(q_ref[0, :, _tile(t)], SCALE), ts) for ts in tiles],
                    [_dup_head(kall, kh) for kh in range(A_KV_HEADS)],
                    [_dup_head(vall, kh) for kh in range(A_KV_HEADS)],
                    [cat(lambda t: _stack_heads(do_ref[0, :, _tile(t)]), ts) for ts in tiles], i,
                    [cat(lambda t: _lse_col(l_ref[0, :, _tile(t)]), ts) for ts in tiles],
                    [_sink_col(sink_ref, kh, nb) for kh in range(A_KV_HEADS)], [None] * A_KV_HEADS)
                lane1 = _lane((1, PAIR_W))
                dsink = jnp.zeros((1, PAIR_W), F32)
                for kh, (ts, dq2, sk) in enumerate(zip(tiles, dq2s, sks)):
                    for n, t in enumerate(ts):
                        dq_s[:, _tile(t)] = _unstack_heads(dq2[2 * BLOCK * n:2 * BLOCK * (n + 1)])
                    for bb in range(nb):
                        dsink = dsink + jnp.where(lane1 == kh * nb + bb, -jnp.sum(sk[BLOCK * bb:BLOCK * (bb + 1)]), 0.0)
                second = _lane((krows, PAIR_W)) >= HEAD_DIM
                dk_s[...] = jnp.where(second, _fold_heads(dkks[1]), _fold_heads(dkks[0]))
                dv_s[...] = jnp.where(second, _fold_heads(dvvs[1]), _fold_heads(dvvs[0]))
                ds_ref[0:1, :] += dsink
            else:
                dq2s, dkks, dvvs, _ = grads(
                    [_stack_heads(q_ref[0, :, _tile(t)], SCALE) for t in range(ntile)],
                    [_rows2(kp_ref, kc_ref, _tile(t), single) for t in range(ntile)],
                    [_rows2(vp_ref, vc_ref, _tile(t), single) for t in range(ntile)],
                    [_stack_heads(do_ref[0, :, _tile(t)]) for t in range(ntile)], i,
                    [_lse_col(l_ref[0, :, _tile(t)]) for t in range(ntile)], [None] * ntile,
                    [_lse_col(dl_ref[0, :, _tile(t)]) for t in range(ntile)])
                for t in range(ntile):
                    dq_s[:, _tile(t)] = _unstack_heads(dq2s[t])
                    dk_s[0:krows, _tile(t)] = dkks[t]
                    dv_s[0:krows, _tile(t)] = dvvs[t]

        def emit(dq, dk, dv):
            cos, sn = c_ref[0], s_ref[0]
            o_ref[0, :, 0:qw] = _unrope(dq, cos, sn).astype(BF16)
            o_ref[0, :, qw:qw + kw] = _unrope(dk, cos, sn).astype(BF16)
            o_ref[0, :, qw + kw:qw + 2 * kw] = dv.astype(BF16)
            if qw + 2 * kw < VAR_W:
                o_ref[0, :, qw + 2 * kw:VAR_W] = jnp.zeros((BLOCK, VAR_W - qw - 2 * kw), BF16)

        if single:
            compute()
            emit(dq_s[...], dk_s[0:BLOCK, :], dv_s[0:BLOCK, :])
            return

        @pl.when(i == 0)
        def _():
            car_q[...] = jnp.zeros_like(car_q)
            car_k[...] = jnp.zeros_like(car_k)
            car_v[...] = jnp.zeros_like(car_v)

        @pl.when(i == nblk)
        def _():
            dk_s[...] = jnp.zeros_like(dk_s)
            dv_s[...] = jnp.zeros_like(dv_s)

        pl.when(i < nblk)(compute)
        emit(car_q[...], car_k[...] + dk_s[0:BLOCK, :], car_v[...] + dv_s[0:BLOCK, :])
        car_q[...] = dq_s[...]
        car_k[...] = dk_s[BLOCK:2 * BLOCK, :]
        car_v[...] = dv_s[BLOCK:2 * BLOCK, :]

    cur = lambda i: jnp.minimum(i, nblk - 1)
    prv = lambda i: jnp.maximum(jnp.minimum(i, nblk - 1) - 1, 0)
    outb = lambda i: jnp.maximum(i - 1, 0)
    qrow = pl.BlockSpec((1, BLOCK, qw), lambda b, j, i: (b, cur(i), j))
    in_specs = [
        pl.BlockSpec((1, BLOCK, qw), lambda b, j, i: (b, cur(i), geo.qidx(j))),
        pl.BlockSpec((1, BLOCK, kw), lambda b, j, i: (b, prv(i), geo.kidx(j))),
        pl.BlockSpec((1, BLOCK, kw), lambda b, j, i: (b, cur(i), geo.kidx(j))),
        pl.BlockSpec((1, BLOCK, kw), lambda b, j, i: (b, prv(i), geo.vidx(j))),
        pl.BlockSpec((1, BLOCK, kw), lambda b, j, i: (b, cur(i), geo.vidx(j))),
        qrow, qrow,
    ]
    ins = [view(qkv, VAR_W)] * 5 + [view(do, qw), view(lse, qw)]
    if has_dlse:
        in_specs.append(qrow)
        ins.append(view(dlse, qw))
    in_specs += [
        pl.BlockSpec((1, BLOCK, PAIR_W), lambda b, j, i: (b, outb(i), j)),
        pl.BlockSpec((1, BLOCK, PAIR_W), lambda b, j, i: (b, outb(i), j)),
        pl.BlockSpec(memory_space=pltpu.SMEM),
    ]
    ins += [view(cosf, PAIR_W), view(sins, PAIR_W), sinks]
    scratch = [pltpu.VMEM((BLOCK, qw), F32), pltpu.VMEM((2 * BLOCK, kw), F32), pltpu.VMEM((2 * BLOCK, kw), F32),
               pltpu.VMEM((BLOCK, qw), F32), pltpu.VMEM((BLOCK, kw), F32), pltpu.VMEM((BLOCK, kw), F32)]
    dqkv, dsink = _pcall(
        body, name=name, grid=(NB, r, nsteps), in_specs=in_specs,
        out_specs=[pl.BlockSpec((1, BLOCK, VAR_W), lambda b, j, i: (b, outb(i), j)),
                   pl.BlockSpec((8, PAIR_W), lambda b, j, i: (0, 0))],
        out_shape=[jax.ShapeDtypeStruct((NB, tsub, r * VAR_W), BF16), jax.ShapeDtypeStruct((8, PAIR_W), F32)],
        scratch_shapes=scratch, compiler_params=_params(("arbitrary", "arbitrary", "arbitrary")),
    )(*ins)
    return dqkv.reshape(NB * T, VAR_W), dsink


class _Rows:
    def __init__(self, N, T, tm):
        self.N, self.tm, self.tpe, self.grid = N, tm, T // tm, (N // tm,)

    def row(self, w, col=0):
        return pl.BlockSpec((self.tm, w), lambda i: (i, col))

    def ex(self, w):
        return pl.BlockSpec((1, 1, w), lambda i: (i // self.tpe, 0, 0))

    def const(self, shape):
        return pl.BlockSpec(shape, lambda i: tuple(0 for _ in shape))

    def first_of_example(self):
        return pl.program_id(0) % self.tpe == 0


def _acc(ref, first, val):
    @pl.when(first)
    def _():
        ref[0] = val

    @pl.when(jnp.logical_not(first))
    def _():
        ref[0] += val


def _colsum(v):
    return jnp.sum(v, axis=0, keepdims=True)


def _ln_stats(r):
    mu = jnp.mean(r, axis=-1, keepdims=True)
    xc = r - mu
    var = jnp.mean(xc * xc, axis=-1, keepdims=True)
    rstd = lax.rsqrt(var + LN_EPS)
    return xc * rstd, rstd


def _ln_bwd(dy, xhat, rstd, gain):
    dxh = dy * gain
    return rstd * (dxh - jnp.mean(dxh, axis=-1, keepdims=True) - xhat * jnp.mean(dxh * xhat, axis=-1, keepdims=True))


def _silu_parts(v):
    s = jax.nn.sigmoid(v)
    return v * s, s * (1.0 + v * (1.0 - s))


def _local_step(x, mod, positions, w_in, rest_weights, sinks, ln1_g, ln1_b, ln2_g, ln2_b, target, hook=None):
    hook = hook or (lambda event, **data: None)
    NB, T, D = x.shape
    N = NB * T
    x2 = x.reshape(N, D)
    tgt2 = target.reshape(N, D)
    shift_m, scale_m, gate_m, shift_f, scale_f, gate_f = [mod[:, None, k * D:(k + 1) * D] for k in range(6)]
    cosf, sins = _rope_tables(positions)
    col = jnp.arange(QKV_P)
    vcol = col % VAR_W
    flags = jnp.where(col < VAR_W, vcol < QA_W + KA_W, vcol < 2 * GB_W).astype(F32)[None]
    R = _Rows(N, T, _pick(T, 256))
    sds = jax.ShapeDtypeStruct
    exsum = lambda w=D: sds((NB, 1, w), F32)
    ngrp = len(B_PATTERNS)

    *qkv, u = _inproj(x2, scale_m, shift_m, w_in, cosf, sins, flags, T=T, name="inproj_qkv")
    gates = _mm(u, w_in[:, QKV_P:], name="inproj_gates")
    oa, la = _attn_fwd(qkv[0], sinks, None, NB=NB, T=T, name="attn_a_fwd")
    ob_parts = [_attn_fwd(qkv[1 + g], sinks, g, NB=NB, T=T, name=f"attn_b{g}_fwd") for g in range(ngrp)]
    (o1, l1), (o2, l2), (o3, l3) = ob_parts
    w_a, w_b, w_o, w_gu, w_d = rest_weights()
    F = w_d.shape[0]

    def merge_fwd(o1r, o2r, o3r, l1r, l2r, l3r, ob_ref):
        la, lb, lc = l1r[...], l2r[...], l3r[...]
        mx = jnp.maximum(jnp.maximum(la, lb), lc)
        ea, eb, ec = jnp.exp(la - mx), jnp.exp(lb - mx), jnp.exp(lc - mx)
        ob_ref[...] = ((ea * o1r[...] + eb * o2r[...] + ec * o3r[...]) / (ea + eb + ec)).astype(BF16)

    ob = _pcall(merge_fwd, name="merge_fwd", grid=R.grid, in_specs=[R.row(GB_W)] * 6, out_specs=R.row(GB_W),
                out_shape=sds((N, GB_W), BF16), compiler_params=_params(("parallel",)))(o1, o2, o3, l1, l2, l3)

    ya = _mm(oa, w_a, name="branch_a")
    yb = _mm(ob, w_b, b3=True, name="branch_b")

    def gate_fwd(ya_r, yb_r, ga_r, gb_r, mg_ref):
        mg_ref[...] = (jax.nn.sigmoid(ga_r[...]) * ya_r[...] + jax.nn.sigmoid(gb_r[...]) * yb_r[...]).astype(BF16)

    merged = _pcall(gate_fwd, name="gate_fwd", grid=R.grid, in_specs=[R.row(D), R.row(D), R.row(D, 0), R.row(D, 1)],
                    out_specs=R.row(D), out_shape=sds((N, D), BF16),
                    compiler_params=_params(("parallel",)))(ya, yb, gates, gates)
    y = _mm(merged, w_o, name="out_proj")

    def norm1_fwd(x_r, y_r, gm_r, g_r, b_r, sf_r, hf_r, r1_ref, x1_ref, u2_ref):
        r1 = ALPHA * x_r[...] + (1.0 + gm_r[0]) * y_r[...]
        xhat, _ = _ln_stats(r1)
        x1 = xhat * g_r[...] + b_r[...]
        r1_ref[...] = r1
        x1_ref[...] = x1
        u2_ref[...] = (x1 * (1.0 + sf_r[0]) + hf_r[0]).astype(BF16)

    r1, x1, u2 = _pcall(
        norm1_fwd, name="norm1_fwd", grid=R.grid,
        in_specs=[R.row(D), R.row(D), R.ex(D), R.const((1, D)), R.const((1, D)), R.ex(D), R.ex(D)],
        out_specs=[R.row(D)] * 3, out_shape=[sds((N, D), F32), sds((N, D), F32), sds((N, D), BF16)],
        compiler_params=_params(("parallel",)))(x2, y, gate_m, ln1_g, ln1_b, scale_f, shift_f)

    tnf = w_gu.shape[2]
    nft = w_gu.shape[0] // 2
    tmf = _pick(N, 512)

    def ffn_up(u_r, wg_r, wu_r, hg_ref, hu_ref, a_ref):
        hg = jnp.dot(u_r[...], wg_r[...], preferred_element_type=F32)
        hu = jnp.dot(u_r[...], wu_r[...], preferred_element_type=F32)
        sl, _ = _silu_parts(hg)
        hg_ref[...] = hg.astype(BF16)
        hu_ref[...] = hu.astype(BF16)
        a_ref[...] = (sl * hu).astype(BF16)

    ftile = pl.BlockSpec((tmf, tnf), lambda i, j: (i, j))
    hg, hu, act = _pcall(
        ffn_up, name="ffn_up", grid=(N // tmf, nft),
        in_specs=[pl.BlockSpec((tmf, D), lambda i, j: (i, 0)), pl.BlockSpec((None, D, tnf), lambda i, j: (j, 0, 0)),
                  pl.BlockSpec((None, D, tnf), lambda i, j: (j + nft, 0, 0))],
        out_specs=[ftile] * 3, out_shape=[sds((N, F), BF16)] * 3,
        compiler_params=_params(("parallel", "arbitrary")))(u2, w_gu, w_gu)
    y2 = _mm(act, w_d, name="ffn_down")

    def norm2_loss_bwd(x1_r, y2_r, t_r, gf_r, g_r, b_r, dy2_ref, dx1_ref, dgf_ref, dg_ref, db_ref, loss_ref):
        first = R.first_of_example()
        y2v = y2_r[...]
        r2 = ALPHA * x1_r[...] + (1.0 + gf_r[0]) * y2v
        xhat, rstd = _ln_stats(r2)
        err = xhat * g_r[...] + b_r[...] - t_r[...]
        dx2 = err * (1.0 / D)
        dr2 = _ln_bwd(dx2, xhat, rstd, g_r[...])
        dy2_ref[...] = ((1.0 + gf_r[0]) * dr2).astype(BF16)
        dx1_ref[...] = ALPHA * dr2
        _acc(dgf_ref, first, _colsum(dr2 * y2v))
        _acc(dg_ref, first, _colsum(dx2 * xhat))
        _acc(db_ref, first, _colsum(dx2))
        part = 0.5 * jnp.sum(jnp.mean(err * err, axis=-1, keepdims=True))
        _acc(loss_ref, first, jnp.broadcast_to(part, (1, 128)))

    dy2, dx1p, dgate_f, dg2, db2, loss_p = _pcall(
        norm2_loss_bwd, name="norm2_loss_bwd", grid=R.grid,
        in_specs=[R.row(D), R.row(D), R.row(D), R.ex(D), R.const((1, D)), R.const((1, D))],
        out_specs=[R.row(D), R.row(D), R.ex(D), R.ex(D), R.ex(D), R.ex(128)],
        out_shape=[sds((N, D), BF16), sds((N, D), F32), exsum(), exsum(), exsum(), exsum(128)],
        compiler_params=_params(("arbitrary",)))(x1, y2, tgt2, gate_f, ln2_g, ln2_b)

    g_wd = _mm(act, dy2, ta=True, out_dtype=BF16, name="ffn_down_dw")

    tmd = _pick(N, 256)

    def ffn_down_dx(dy_r, wd_r, hg_r, hu_r, dh_ref):
        for t in range(nft):
            cs = slice(t * tnf, (t + 1) * tnf)
            da = lax.dot_general(dy_r[...], wd_r[cs, :], _NT, preferred_element_type=F32)
            sl, dsl = _silu_parts(hg_r[:, cs].astype(F32))
            dh_ref[:, cs] = (da * hu_r[:, cs].astype(F32) * dsl).astype(BF16)
            dh_ref[:, F + t * tnf:F + (t + 1) * tnf] = (da * sl).astype(BF16)

    rowd = lambda w_: pl.BlockSpec((tmd, w_), lambda i: (i, 0))
    dh = _pcall(
        ffn_down_dx, name="ffn_down_dx", grid=(N // tmd,),
        in_specs=[rowd(D), pl.BlockSpec((F, D), lambda i: (0, 0)), rowd(F), rowd(F)],
        out_specs=rowd(2 * F), out_shape=sds((N, 2 * F), BF16),
        compiler_params=_params(("parallel",)))(dy2, w_d, hg, hu)
    du2 = _mm(dh, w_gu, tb=True, b3=True, name="ffn_up_dx")
    g_wgu = _mm(u2, dh, ta=True, out3=w_gu.shape[0], out_dtype=BF16, name="ffn_up_dw")

    def norm1_bwd(dx1p_r, du2_r, x1_r, r1_r, y_r, sf_r, gm_r, g_r,
                  dxp_ref, dy_ref, dsf_ref, dhf_ref, dgm_ref, dg_ref, db_ref):
        first = R.first_of_example()
        du2v = du2_r[...]
        dx1 = dx1p_r[...] + du2v * (1.0 + sf_r[0])
        xhat, rstd = _ln_stats(r1_r[...])
        dr1 = _ln_bwd(dx1, xhat, rstd, g_r[...])
        dxp_ref[...] = ALPHA * dr1
        dy_ref[...] = ((1.0 + gm_r[0]) * dr1).astype(BF16)
        _acc(dsf_ref, first, _colsum(du2v * x1_r[...]))
        _acc(dhf_ref, first, _colsum(du2v))
        _acc(dgm_ref, first, _colsum(dr1 * y_r[...]))
        _acc(dg_ref, first, _colsum(dx1 * xhat))
        _acc(db_ref, first, _colsum(dx1))

    dxp, dy, dscale_f, dshift_f, dgate_m, dg1, db1 = _pcall(
        norm1_bwd, name="norm1_bwd", grid=R.grid,
        in_specs=[R.row(D)] * 5 + [R.ex(D), R.ex(D), R.const((1, D))],
        out_specs=[R.row(D), R.row(D)] + [R.ex(D)] * 5,
        out_shape=[sds((N, D), F32), sds((N, D), BF16)] + [exsum()] * 5,
        compiler_params=_params(("arbitrary",)))(dx1p, du2, x1, r1, y, scale_f, gate_m, ln1_g)

    dmerged = _mm(dy, w_o, tb=True, name="out_proj_dx")
    g_wo = _mm(merged, dy, ta=True, out_dtype=BF16, name="out_proj_dw")

    def gate_bwd(dm_r, ya_r, yb_r, ga_r, gb_r, dya_ref, dyb_ref, dg_ref):
        dm = dm_r[...]
        sa, sb = jax.nn.sigmoid(ga_r[...]), jax.nn.sigmoid(gb_r[...])
        dya_ref[...] = (dm * sa).astype(BF16)
        dyb_ref[...] = (dm * sb).astype(BF16)
        dg_ref[:, :D] = (dm * ya_r[...] * sa * (1.0 - sa)).astype(BF16)
        dg_ref[:, D:] = (dm * yb_r[...] * sb * (1.0 - sb)).astype(BF16)

    dya, dyb, dgates = _pcall(
        gate_bwd, name="gate_bwd", grid=R.grid, in_specs=[R.row(D)] * 3 + [R.row(D, 0), R.row(D, 1)],
        out_specs=[R.row(D), R.row(D), R.row(2 * D)],
        out_shape=[sds((N, D), BF16), sds((N, D), BF16), sds((N, 2 * D), BF16)],
        compiler_params=_params(("parallel",)))(dmerged, ya, yb, gates, gates)

    doa = _mm(dya, w_a, tb=True, out_dtype=BF16, name="branch_a_dx")
    g_wa = _mm(oa, dya, ta=True, out_dtype=BF16, name="branch_a_dw")
    dob = _mm(dyb, w_b, tb=True, b3=True, name="branch_b_dx")
    g_wb = _mm(ob, dyb, ta=True, out3=w_b.shape[0], out_dtype=BF16, name="branch_b_dw")
    hook("rest_grads", g_wa=g_wa, g_wb=g_wb, g_wo=g_wo, g_wgu=g_wgu, g_wd=g_wd)

    seg = (jnp.arange(GB_W)[:, None] // HEAD_DIM == jnp.arange(GB_W)[None, :] // HEAD_DIM).astype(BF16)

    def merge_bwd(dob_r, o1r, o2r, o3r, l1r, l2r, l3r, seg_r, d1, d2, d3, e1, e2, e3):
        dob_v = dob_r[...]
        la, lb, lc = l1r[...], l2r[...], l3r[...]
        mx = jnp.maximum(jnp.maximum(la, lb), lc)
        ea, eb, ec = jnp.exp(la - mx), jnp.exp(lb - mx), jnp.exp(lc - mx)
        inv = 1.0 / (ea + eb + ec)
        ws = [ea * inv, eb * inv, ec * inv]

        def headsum(v):
            hi = v.astype(BF16)
            r1_ = v - hi.astype(F32)
            mid = r1_.astype(BF16)
            lo = (r1_ - mid.astype(F32)).astype(BF16)
            sm = seg_r[...]
            return (jnp.dot(hi, sm, preferred_element_type=F32) + jnp.dot(mid, sm, preferred_element_type=F32)
                    + jnp.dot(lo, sm, preferred_element_type=F32))

        dws = [headsum(dob_v * o[...]) for o in (o1r, o2r, o3r)]
        mean = ws[0] * dws[0] + ws[1] * dws[1] + ws[2] * dws[2]
        for w_, dw_, d_ref, e_ref in zip(ws, dws, (d1, d2, d3), (e1, e2, e3)):
            d_ref[...] = (w_ * dob_v).astype(BF16)
            e_ref[...] = w_ * (dw_ - mean)

    mb = _pcall(
        merge_bwd, name="merge_bwd", grid=R.grid, in_specs=[R.row(GB_W)] * 7 + [R.const((GB_W, GB_W))],
        out_specs=[R.row(GB_W)] * 6, out_shape=[sds((N, GB_W), BF16)] * 3 + [sds((N, GB_W), F32)] * 3,
        compiler_params=_params(("parallel",)))(dob, o1, o2, o3, l1, l2, l3, seg)
    do_b, dlse_b = mb[:3], mb[3:]
    hook("merge_bwd_done")

    dqkv_a, dsink = _attn_bwd(qkv[0], doa, la, None, cosf, sins, sinks, None, NB=NB, T=T, name="attn_a_bwd")
    hook("attn_a_bwd_done")
    dqkv = [dqkv_a]
    for g in range(ngrp):
        dqkv.append(_attn_bwd(qkv[1 + g], do_b[g], (l1, l2, l3)[g], dlse_b[g], cosf, sins, sinks, g, NB=NB, T=T,
                              name=f"attn_b{g}_bwd")[0])
        hook(f"attn_b{g}_bwd_done")

    g_win = [_mm(u, dseg, ta=True, out_dtype=BF16, name=f"inproj_dw{n}") for n, dseg in enumerate(dqkv + [dgates])]
    hook("win_grads", g_win=g_win)
    wvar = lambda v: (w_in, (D, VAR_W), (0, v))
    du = _mm_multi(dqkv[:3], [wvar(v) for v in range(3)], name="inproj_dx0")
    hook("inproj_dx0_done")
    du = _mm_multi([dqkv[3], dgates], [wvar(3), (w_in, (D, 2 * D), (0, QKV_P // (2 * D)))], add=du, name="inproj_dx1")
    hook("inproj_dx1_done")

    def x_bwd(dxp_r, du_r, x_r, sm_r, gx_ref, dsm_ref, dhm_ref):
        first = R.first_of_example()
        duv = du_r[...]
        gx_ref[...] = dxp_r[...] + duv * (1.0 + sm_r[0])
        _acc(dsm_ref, first, _colsum(duv * x_r[...]))
        _acc(dhm_ref, first, _colsum(duv))

    gx, dscale_m, dshift_m = _pcall(
        x_bwd, name="x_bwd", grid=R.grid, in_specs=[R.row(D)] * 3 + [R.ex(D)],
        out_specs=[R.row(D), R.ex(D), R.ex(D)], out_shape=[sds((N, D), F32), exsum(), exsum()],
        compiler_params=_params(("arbitrary",)))(dxp, du, x2, scale_m)
    hook("x_bwd_done")

    dmod =jnp.concatenate([dshift_m, dscale_m, dgate_m, dshift_f, dscale_f, dgate_f], axis=-1)[:, 0]
    ln_grads = jnp.concatenate([dg1, db1, dg2, db2], axis=1)
    return dict(loss=loss_p[:, 0, 0], grad_x=gx.reshape(NB, T, D), g_win=g_win, g_wa=g_wa, g_wb=g_wb, g_wo=g_wo,
                g_wgu=g_wgu, g_wd=g_wd, dmod=dmod, ln_grads=ln_grads, dsink=dsink[0, :A_Q_HEADS])


def _coords():
    return lax.axis_index("x"), lax.axis_index("y"), lax.axis_index("c")


def _allgather_small(blk, *, name):
    m_per, n = blk.shape

    def body(x_ref, out_ref, send_sems, recv_sems, local_sem):
        x, y, c = _coords()
        me, sibling = (x, y, c), (x, y, 1 - c)
        chips = [(1 - x, y), (x, 1 - y), (1 - x, 1 - y)]

        def rows(px, py, pc):
            return out_ref.at[pl.ds((4 * px + 2 * py + pc) * m_per, m_per), :]

        def copy(k, block, to, src=None):
            return pltpu.make_async_remote_copy(
                src_ref=rows(*block) if src is None else src, dst_ref=rows(*block),
                send_sem=send_sems.at[k], recv_sem=recv_sems.at[k], device_id=to, device_id_type=MESH)

        mine = pltpu.make_async_copy(x_ref, rows(*me), local_sem)
        mine.start()
        first = [copy(0, me, sibling, src=x_ref)]
        first += [copy(1 + j, me, (*chip, c), src=x_ref) for j, chip in enumerate(chips)]
        for cp in first:
            cp.start()
        passed = [copy(4 + j, (*chip, c), sibling) for j, chip in enumerate(chips)]
        for j, chip in enumerate(chips):
            copy(1 + j, (*chip, c), me).wait_recv()
            passed[j].start()
        copy(0, sibling, me).wait_recv()
        for j, chip in enumerate(chips):
            copy(4 + j, (*chip, 1 - c), me).wait_recv()
        for cp in first + passed:
            cp.wait_send()
        mine.wait()

    return _pcall(
        body, name=name, out_shape=jax.ShapeDtypeStruct((8 * m_per, n), blk.dtype),
        in_specs=[pl.BlockSpec(memory_space=pltpu.VMEM)], out_specs=pl.BlockSpec(memory_space=pltpu.VMEM),
        scratch_shapes=[pltpu.SemaphoreType.DMA((7,)), pltpu.SemaphoreType.DMA((7,)), pltpu.SemaphoreType.DMA],
        compiler_params=pltpu.CompilerParams(vmem_limit_bytes=VMEM_LIMIT_BYTES),
    )(blk)


def _exchange(srcs, dsts, plan, *, name, dst_inits=None):
    na = len(dsts)
    nrem = len(plan(0, 0, 0))

    def body(*refs):
        refs = list(refs)
        src_refs = [refs.pop(0) for _ in range(na)] if srcs is not None else None
        if dst_inits is not None:
            del refs[:na]
        dst_refs, (send_sems, recv_sems) = refs[:na], refs[na:]
        start, wait = _copies(dst_refs if src_refs is None else src_refs, dst_refs, send_sems, recv_sems, plan)
        start()
        wait()

    hbm = pl.BlockSpec(memory_space=pl.ANY)
    ins = (list(srcs) if srcs is not None else []) + (list(dst_inits) if dst_inits is not None else [])
    base = na if srcs is not None else 0
    aliases = {base + a: a for a in range(na)} if dst_inits is not None else {}
    return _pcall(
        body, name=name, out_shape=list(dsts), in_specs=[hbm] * len(ins), out_specs=[hbm] * na,
        input_output_aliases=aliases,
        scratch_shapes=[pltpu.SemaphoreType.DMA((na * nrem,)), pltpu.SemaphoreType.DMA((na * nrem,))],
    )(*ins)


def _other_chips(x, y):
    return [(1 - x, y), (x, 1 - y), (1 - x, 1 - y)]


def _round(ride, carrier, name):
    if carrier is not None:
        _RIDES[carrier] = ride
        return
    srcs = ride.srcs() if callable(ride.srcs) else ride.srcs
    inits = ride.dst_inits() if callable(ride.dst_inits) else ride.dst_inits
    ride.out = list(_exchange(srcs, ride.dsts, ride.plan, name=name, dst_inits=inits))


class _Gather:
    def __init__(self, shards, chip, tag, carriers=(None, None)):
        def plan_ici(x, y, c):
            k = 2 * x + y
            return [((c,), (k, c), (2 * px + py, c), (px, py, c)) for px, py in _other_chips(x, y)]

        def plan_d2d(x, y, c):
            return [((2 * px + py, c), (2 * px + py, c), (2 * px + py, 1 - c), (x, y, 1 - c))
                    for px, py in _other_chips(x, y)]

        self.shards, self.chip = shards, chip
        dsts = [jax.ShapeDtypeStruct((4,) + s.shape, s.dtype) for s in shards]
        ici = _Ride(shards, dsts, plan_ici)
        self.d2d = _Ride(None, dsts, plan_d2d, dst_inits=lambda: ici.out)
        _round(ici, carriers[0], f"gather_{tag}_ici")
        _round(self.d2d, carriers[1], f"gather_{tag}_d2d")

    def result(self):
        full = [lax.dynamic_update_index_in_dim(f, s, self.chip, 0) for f, s in zip(self.d2d.out, self.shards)]
        return [f.reshape((4, 2 * f.shape[2], f.shape[3])) for f in full]


def _add_pairs(a, b, *, name):
    s, hr, wd = a.shape
    tr = _pick(hr, 600, 16)

    def body(a_ref, b_ref, o_ref):
        o_ref[...] = (a_ref[...].astype(F32) + b_ref[...].astype(F32)).astype(BF16)

    spec = pl.BlockSpec((1, tr, wd), lambda j, i: (j, i, 0))
    return _pcall(body, name=name, grid=(s, hr // tr), in_specs=[spec, spec], out_specs=spec,
                  out_shape=jax.ShapeDtypeStruct(a.shape, BF16), compiler_params=_params(("parallel", "parallel")))(a, b)


def _sum_chips(b, *, name):
    s, hr, wd = b.shape
    tr = _pick(hr, 600, 16)

    def body(b_ref, o_ref):
        acc = b_ref[0].astype(F32)
        for k in range(1, s):
            acc = acc + b_ref[k].astype(F32)
        o_ref[...] = acc

    return _pcall(body, name=name, grid=(hr // tr,), in_specs=[pl.BlockSpec((s, tr, wd), lambda i: (0, i, 0))],
                  out_specs=pl.BlockSpec((tr, wd), lambda i: (i, 0)), out_shape=jax.ShapeDtypeStruct((hr, wd), F32),
                  compiler_params=_params(("parallel",)))(b)


class _ReduceScatter:
    def __init__(self, gs, chip, ci, tag):
        self.gs, self.chip, self.ci, self.tag = gs, chip, ci, tag
        self.half_t = [jax.ShapeDtypeStruct((g.shape[0],) + g.shape[2:], BF16) for g in gs]

    def pair(self, carrier=None):
        plan = lambda x, y, c: [((slice(None), 1 - c), (), (), (x, y, 1 - c))]
        self.r1 = _Ride(self.gs, self.half_t, plan)
        _round(self.r1, carrier, f"reduce_{self.tag}_pair")

    def chips(self, carrier=None):
        def plan(x, y, c):
            k = 2 * x + y
            return [((2 * px + py,), (k,), (2 * px + py,), (px, py, c)) for px, py in _other_chips(x, y)]

        self.pairs = [_add_pairs(lax.dynamic_index_in_dim(g, self.ci, 1, keepdims=False), f,
                                 name=f"reduce_{self.tag}_pair_add{n}")
                      for n, (g, f) in enumerate(zip(self.gs, self.r1.out))]
        self.r2 = _Ride(self.pairs, self.half_t, plan)
        _round(self.r2, carrier, f"reduce_{self.tag}_chips")

    def halves(self, carrier=None):
        plan = lambda x, y, c: [((), (c,), (1 - c,), (x, y, 1 - c))]
        self.mine = []
        for n, (l, p) in enumerate(zip(self.r2.out, self.pairs)):
            own = lax.dynamic_index_in_dim(p, self.chip, 0, keepdims=False)
            self.mine.append(_sum_chips(lax.dynamic_update_index_in_dim(l, own, self.chip, 0),
                                        name=f"reduce_{self.tag}_chip_sum{n}"))
        self.r3 = _Ride(self.mine, [jax.ShapeDtypeStruct((2,) + m.shape, F32) for m in self.mine], plan)
        _round(self.r3, carrier, f"reduce_{self.tag}_halves")

    def result(self):
        return [lax.dynamic_update_index_in_dim(b, m, self.ci, 0).reshape(2 * m.shape[0], m.shape[1])
                for b, m in zip(self.r3.out, self.mine)]


def _ada_fwd(c_all, w_sh, b_sh, *, name):
    nb, d = c_all.shape
    wcols = w_sh.shape[1]
    tn = _pick(wcols, 512)

    def body(c_ref, w_ref, b_ref, o_ref, a_ref):
        cv = c_ref[...]
        act = cv * jax.nn.sigmoid(cv)
        a_ref[...] = act
        o_ref[...] = jnp.dot(act.astype(BF16), w_ref[...].astype(BF16), preferred_element_type=F32) + b_ref[...]

    return _pcall(
        body, name=name, grid=(wcols // tn,),
        in_specs=[pl.BlockSpec((nb, d), lambda j: (0, 0)), pl.BlockSpec((d, tn), lambda j: (0, j)),
                  pl.BlockSpec((1, tn), lambda j: (0, j))],
        out_specs=[pl.BlockSpec((nb, tn), lambda j: (0, j)), pl.BlockSpec((nb, d), lambda j: (0, 0))],
        out_shape=[jax.ShapeDtypeStruct((nb, wcols), F32), jax.ShapeDtypeStruct((nb, d), F32)],
        compiler_params=_params(("arbitrary",)))(c_all, w_sh, b_sh)


def _sum_devices(g, *, name):
    nd, m, w = g.shape

    def body(g_ref, o_ref):
        acc = g_ref[0]
        for k in range(1, nd):
            acc = acc + g_ref[k]
        o_ref[...] = acc

    return _pcall(body, name=name, out_shape=jax.ShapeDtypeStruct((m, w), F32),
                  compiler_params=pltpu.CompilerParams(vmem_limit_bytes=VMEM_LIMIT_BYTES))(g)


def _adamw(w, g, m, v, *, name):
    rows, cols = w.shape
    tr = _pick(rows, max(8, (1 << 18) // cols), 8)
    c1 = 1.0 / (1.0 - ADAM_B1 ** ADAM_STEP)
    c2 = 1.0 / (1.0 - ADAM_B2 ** ADAM_STEP)

    def body(w_ref, g_ref, m_ref, v_ref, d_ref, nm_ref, nv_ref):
        gv = g_ref[...]
        nm = ADAM_B1 * m_ref[...] + (1.0 - ADAM_B1) * gv
        nv = ADAM_B2 * v_ref[...] + (1.0 - ADAM_B2) * (gv * gv)
        d_ref[...] = -ADAM_LR * ((nm * c1) / (jnp.sqrt(nv * c2) + ADAM_EPS) + ADAM_WD * w_ref[...])
        nm_ref[...] = nm
        nv_ref[...] = nv

    spec = pl.BlockSpec((tr, cols), lambda i: (i, 0))
    shp = jax.ShapeDtypeStruct((rows, cols), F32)
    return _pcall(body, name=name, grid=(rows // tr,), in_specs=[spec] * 4, out_specs=[spec] * 3,
                  out_shape=[shp] * 3, compiler_params=_params(("parallel",)))(w, g, m, v)


def _permute_in_cols(w):
    ngrp = len(B_PATTERNS)
    qb, kb, vb = (w[:, A_W + n * QB_W:A_W + (n + 1) * QB_W] for n in range(3))
    parts = [w[:, :A_W], jnp.zeros((w.shape[0], VAR_W - A_W), w.dtype)]
    for g in range(ngrp):
        parts += [t[:, g * GB_W:(g + 1) * GB_W] for t in (qb, kb, vb)]
    return jnp.concatenate(parts + [w[:, A_W + 3 * QB_W:]], axis=1)


def _unpermute_in_grads(pieces):
    ga, groups, gg = pieces[0], pieces[1:-1], pieces[-1]
    cols = [ga[:, :A_W]]
    for n in range(3):
        cols += [gp[:, n * GB_W:(n + 1) * GB_W] for gp in groups]
    return jnp.concatenate(cols + [gg], axis=1)


def kernel(x, c, positions, w_ada, b_ada, w_in, sinks, w_branch_a, w_branch_b, w_o, ln1_g, ln1_b, w_gate_up, w_down, ln2_g, ln2_b, loss_target, m_w_ada, m_b_ada, m_w_in, m_sinks, m_w_branch_a, m_w_branch_b, m_w_o, m_ln1_g, m_ln1_b, m_w_gate_up, m_w_down, m_ln2_g, m_ln2_b, v_w_ada, v_b_ada, v_w_in, v_sinks, v_w_branch_a, v_w_branch_b, v_w_o, v_ln1_g, v_ln1_b, v_w_gate_up, v_w_down, v_ln2_g, v_ln2_b):
    xi, yi, ci = _coords()
    chip = 2 * xi + yi
    dev = 4 * xi + 2 * yi + ci
    NB, T, D = x.shape
    nchip, ndev = 4, 8
    ada_cols = w_ada.shape[2]

    c_blk = jnp.zeros((8, D), F32).at[:NB].set(c)
    c_all = _allgather_small(c_blk, name="gather_c").reshape(ndev, 8, D)[:, :NB].reshape(ndev * NB, D)
    b_sh = lax.dynamic_slice(b_ada, (0, chip * ada_cols), (1, ada_cols))
    mod_part, c_act = _ada_fwd(c_all, w_ada[0], b_sh, name="ada_fwd")
    mod_g = _allgather_small(mod_part, name="gather_mod").reshape(nchip, 2, ndev * NB, ada_cols)[:, 0]
    mod_all = jnp.transpose(mod_g, (1, 0, 2)).reshape(ndev * NB, nchip * ada_cols)
    mod = lax.dynamic_slice(mod_all, (NB * dev, 0), (NB, nchip * ada_cols))

    ra, ro, rd = w_branch_a.shape[1], w_o.shape[1], w_down.shape[1]
    rowsh = jnp.concatenate([w_branch_a[0], w_o[0], w_down[0]], axis=0)
    halves = lambda a: a.reshape(a.shape[:-2] + (2, a.shape[-2] // 2, a.shape[-1]))
    whole = lambda a: a.reshape(a.shape[:-3] + (2 * a.shape[-2], a.shape[-1]))
    shards = [halves(w.astype(BF16)) for w in (w_in[0], rowsh, w_branch_b[0], w_gate_up[0])]
    (g_in,) = _Gather(shards[:1], chip, "w_in").result()
    w_in_f = _permute_in_cols(jnp.concatenate([g_in[k] for k in range(nchip)], axis=1))
    rest = _Gather(shards[1:], chip, "w_rest", carriers=("inproj_qkv", "attn_a_fwd"))

    def rest_weights():
        g_rows, w_b_f, w_gu_f = rest.result()
        return (g_rows[:, :ra].reshape(nchip * ra, D), w_b_f, g_rows[:, ra:ra + ro].reshape(nchip * ro, D), w_gu_f,
                g_rows[:, ra + ro:].reshape(nchip * rd, D))

    red = {}

    def hook(event, **g):
        if event == "rest_grads":
            gr_rows = jnp.concatenate([g["g_wa"].reshape(nchip, ra, D), g["g_wo"].reshape(nchip, ro, D),
                                       g["g_wd"].reshape(nchip, rd, D)], axis=1)
            red["rest"] = _ReduceScatter([halves(a) for a in (gr_rows, g["g_wb"], g["g_wgu"])], chip, ci, "rest")
            red["rest"].pair(carrier="merge_bwd")
        elif event == "merge_bwd_done":
            red["rest"].chips(carrier="attn_a_bwd")
        elif event == "attn_a_bwd_done":
            red["rest"].halves(carrier="attn_b0_bwd")
        elif event == "win_grads":
            gr_in = jnp.stack(jnp.split(_unpermute_in_grads(g["g_win"]), nchip, axis=1))
            red["w_in"] = _ReduceScatter([halves(gr_in)], chip, ci, "w_in")
            red["w_in"].pair(carrier="inproj_dx0")
        elif event == "inproj_dx0_done":
            red["w_in"].chips(carrier="inproj_dx1")
        elif event == "inproj_dx1_done":
            red["w_in"].halves(carrier="x_bwd")

    res = _local_step(x, mod, positions, w_in_f, rest_weights, sinks[0], ln1_g, ln1_b, ln2_g, ln2_b, loss_target, hook)
    (g_w_in,) = red["w_in"].result()
    g_rows_red, g_w_b, g_w_gu = red["rest"].result()
    g_w_a, g_w_o, g_w_d = g_rows_red[:ra], g_rows_red[ra:ra + ro], g_rows_red[ra + ro:]

    small_rows = 24
    misc = jnp.zeros((1, D), F32).at[0, :A_Q_HEADS].set(res["dsink"]).at[0, A_Q_HEADS].set(jnp.sum(res["loss"]))
    small = jnp.concatenate([res["dmod"].reshape(NB * 6, D), jnp.sum(res["ln_grads"], axis=0), misc,
                             jnp.zeros((small_rows - NB * 6 - 5, D), F32)], axis=0)
    small_all = _allgather_small(small, name="gather_small").reshape(ndev, small_rows, D)
    dmod_all = small_all[:, :NB * 6].reshape(ndev * NB, 6 * D)
    sums = _sum_devices(small_all, name="sum_small")
    g_b_ada = (sums[0:6] + sums[6:12]).reshape(1, 6 * D)
    g_ln1_g, g_ln1_b, g_ln2_g, g_ln2_b = (sums[12 + n][None] for n in range(4))
    g_sinks = sums[16, :A_Q_HEADS][None]
    loss = sums[16, A_Q_HEADS]
    dmod_sh = lax.dynamic_slice(dmod_all, (0, chip * ada_cols), (ndev * NB, ada_cols))
    g_w_ada = _mm(c_act, dmod_sh, ta=True, name="ada_dw")

    names = ["w_ada", "b_ada", "w_in", "sinks", "w_branch_a", "w_branch_b", "w_o", "ln1_g", "ln1_b",
             "w_gate_up", "w_down", "ln2_g", "ln2_b"]
    ws = [w_ada, b_ada, w_in, sinks, w_branch_a, w_branch_b, w_o, ln1_g, ln1_b, w_gate_up, w_down, ln2_g, ln2_b]
    ms = [m_w_ada, m_b_ada, m_w_in, m_sinks, m_w_branch_a, m_w_branch_b, m_w_o, m_ln1_g, m_ln1_b, m_w_gate_up,
          m_w_down, m_ln2_g, m_ln2_b]
    vs = [v_w_ada, v_b_ada, v_w_in, v_sinks, v_w_branch_a, v_w_branch_b, v_w_o, v_ln1_g, v_ln1_b, v_w_gate_up,
          v_w_down, v_ln2_g, v_ln2_b]
    gs = [g_w_ada, g_b_ada, g_w_in, g_sinks, g_w_a, g_w_b, g_w_o, g_ln1_g, g_ln1_b, g_w_gu, g_w_d, g_ln2_g, g_ln2_b]
    grads, deltas, new_ms, new_vs = [], [], [], []
    for name, w, g, m, v in zip(names, ws, gs, ms, vs):
        shp = w.shape
        w2, m2, v2 = (a.reshape(shp[-2], shp[-1]) for a in (w, m, v))
        g2 = g.reshape(shp[-2], shp[-1])
        d, nm, nv = _adamw(w2, g2, m2, v2, name="adamw_" + name)
        grads.append(g2.reshape(shp))
        deltas.append(d.reshape(shp))
        new_ms.append(nm.reshape(shp))
        new_vs.append(nv.reshape(shp))
    return (loss, res["grad_x"], *grads, *deltas, *new_ms, *new_vs)
```

```python
import functools

import jax
import jax.numpy as jnp
from jax import lax
from jax.experimental import pallas as pl
from jax.experimental.pallas import tpu as pltpu

F32 = jnp.float32
BF16 = jnp.bfloat16
MESH = pl.DeviceIdType.MESH

HEAD_DIM = 64
PAIR_W = 2 * HEAD_DIM
BLOCK = 128
A_Q_HEADS = 16
A_KV_HEADS = 2
A_WINDOW = 128
B_PATTERNS = ((128, 1), (512, 4), (2048, 16))
B_GROUP_HEADS = 8
QA_W = A_Q_HEADS * HEAD_DIM
KA_W = A_KV_HEADS * HEAD_DIM
GB_W = B_GROUP_HEADS * HEAD_DIM
QB_W = GB_W * len(B_PATTERNS)
A_W = QA_W + 2 * KA_W
VAR_W = 3 * GB_W
N_VAR = 1 + len(B_PATTERNS)
QKV_P = N_VAR * VAR_W
ROPE_THETA = 10000.0
LN_EPS = 1e-5
NEG_INF = -1e30
DEPTH = 1
ALPHA = (2 * DEPTH) ** 0.25
SCALE = HEAD_DIM ** -0.5

ADAM_LR, ADAM_B1, ADAM_B2, ADAM_EPS, ADAM_WD, ADAM_STEP = 0.001, 0.9, 0.999, 1e-08, 0.01, 10

VMEM_LIMIT_BYTES = 56 * 1024 * 1024


def _params(sem=None):
    return pltpu.CompilerParams(dimension_semantics=sem, vmem_limit_bytes=VMEM_LIMIT_BYTES)


_RIDES = {}


def _pcall(body, *, name, **kw):
    rides = _RIDES.pop(name, None)
    if rides is None:
        return pl.pallas_call(body, name=name, **kw)
    return _riding_call(body, rides, name=name, **kw)


def _copies(src_refs, dst_refs, send_sems, recv_sems, plan):
    x, y, c = lax.axis_index("x"), lax.axis_index("y"), lax.axis_index("c")
    remote = plan(x, y, c)
    nrem = len(remote)
    at = lambda ref, idx: ref.at[idx] if idx else ref

    def copy(a, n, landing):
        si, di, ri, peer = remote[n]
        return pltpu.make_async_remote_copy(
            src_ref=at(src_refs[a], si), dst_ref=at(dst_refs[a], ri if landing else di),
            send_sem=send_sems.at[a * nrem + n], recv_sem=recv_sems.at[a * nrem + n],
            device_id=peer, device_id_type=MESH)

    order = [(a, n) for a in range(len(dst_refs)) for n in range(nrem)]

    def start():
        for a, n in order:
            copy(a, n, False).start()

    def wait():
        for a, n in order:
            copy(a, n, True).wait_recv()
        for a, n in order:
            copy(a, n, False).wait_send()

    return start, wait


class _Ride:
    def __init__(self, srcs, dsts, plan, dst_inits=None):
        self.srcs, self.dsts, self.plan, self.dst_inits, self.out = srcs, dsts, plan, dst_inits, None


def _riding_call(body, rides, *, name, grid, in_specs, out_specs, out_shape, scratch_shapes=(), **kw):
    single = not isinstance(out_specs, (list, tuple))
    out_specs = [out_specs] if single else list(out_specs)
    out_shape = [out_shape] if single else list(out_shape)
    n_in, n_out, n_scr = len(in_specs), len(out_specs), len(scratch_shapes)
    xin, xdsts, sems, aliases, layout = [], [], [], {}, []
    for ride in rides:
        srcs = ride.srcs() if callable(ride.srcs) else ride.srcs
        inits = ride.dst_inits() if callable(ride.dst_inits) else ride.dst_inits
        na, nrem = len(ride.dsts), len(ride.plan(0, 0, 0))
        src_at = len(xin) if srcs is not None else None
        xin += list(srcs) if srcs is not None else []
        if inits is not None:
            aliases.update({n_in + len(xin) + a: n_out + len(xdsts) + a for a in range(na)})
            xin += list(inits)
        layout.append((src_at, len(xdsts), na))
        xdsts += list(ride.dsts)
        sems += [pltpu.SemaphoreType.DMA((na * nrem,)), pltpu.SemaphoreType.DMA((na * nrem,))]

    def wrapped(*refs):
        ins, xins = refs[:n_in], refs[n_in:n_in + len(xin)]
        outs = refs[n_in + len(xin):n_in + len(xin) + n_out]
        xouts = refs[n_in + len(xin) + n_out:n_in + len(xin) + n_out + len(xdsts)]
        scr = refs[n_in + len(xin) + n_out + len(xdsts):]
        rounds = []
        for k, (ride, (src_at, dst_at, na)) in enumerate(zip(rides, layout)):
            dsts = xouts[dst_at:dst_at + na]
            srcs = dsts if src_at is None else xins[src_at:src_at + na]
            rounds.append(_copies(srcs, dsts, scr[n_scr + 2 * k], scr[n_scr + 2 * k + 1], ride.plan))
        ids = [pl.program_id(a) for a in range(len(grid))]
        first = functools.reduce(jnp.logical_and, [i == 0 for i in ids])
        last = functools.reduce(jnp.logical_and, [i == g - 1 for i, g in zip(ids, grid)])

        @pl.when(first)
        def _():
            for start, _ in rounds:
                start()

        body(*ins, *outs, *scr[:n_scr])

        @pl.when(last)
        def _():
            for _, wait in rounds:
                wait()

    hbm = pl.BlockSpec(memory_space=pl.ANY)

    def run(*args):
        res = pl.pallas_call(
            wrapped, name=name, grid=grid, in_specs=list(in_specs) + [hbm] * len(xin),
            out_specs=out_specs + [hbm] * len(xdsts), out_shape=out_shape + xdsts,
            scratch_shapes=list(scratch_shapes) + sems, input_output_aliases=aliases,
            compiler_params=_params(("arbitrary",) * len(grid)),
        )(*args, *xin)
        for ride, (_, dst_at, na) in zip(rides, layout):
            ride.out = list(res[n_out + dst_at:n_out + dst_at + na])
        return res[0] if single else list(res[:n_out])

    return run


def _pick(n, target, quantum=128):
    t = (min(target, n) // quantum) * quantum
    while t >= quantum:
        if n % t == 0:
            return t
        t -= quantum
    return n


def _mm(a, b, *, name, ta=False, tb=False, b3=False, out3=0, out_dtype=F32, add=None, tm=1024, tn=1536, tk=1536):
    if ta:
        K, M = a.shape
    else:
        M, K = a.shape
    if b3 and tb:
        Nn, K2, tk = b.shape[1], b.shape[0] * b.shape[2], b.shape[2]
    elif b3:
        K2, Nn, tn = b.shape[1], b.shape[0] * b.shape[2], b.shape[2]
    elif tb:
        Nn, K2 = b.shape
    else:
        K2, Nn = b.shape
    assert K == K2, (a.shape, b.shape)
    if out3:
        tn = Nn // out3
    tm, tn, tk = _pick(M, tm), _pick(Nn, tn), _pick(K, tk)
    nk = K // tk
    dn = (((0 if ta else 1,), (1 if tb else 0,)), ((), ()))

    def body(*refs):
        refs = list(refs)
        a_ref, b_ref = refs[:2]
        add_ref = refs[2] if add is not None else None
        o_ref = refs[3] if add is not None else refs[2]
        part = lax.dot_general(a_ref[...].astype(BF16), b_ref[...].astype(BF16), dn, preferred_element_type=F32)

        def finish(r):
            if add is not None:
                r = r + add_ref[...]
            o_ref[...] = r.astype(out_dtype)

        if nk == 1:
            finish(part)
            return
        acc = refs[-1]
        k = pl.program_id(2)

        @pl.when(k == 0)
        def _():
            acc[...] = part

        @pl.when(k > 0)
        def _():
            acc[...] += part

        @pl.when(k == nk - 1)
        def _():
            finish(acc[...])

    a_spec = pl.BlockSpec((tk, tm), lambda i, j, k: (k, i)) if ta else pl.BlockSpec((tm, tk), lambda i, j, k: (i, k))
    if b3 and tb:
        b_spec = pl.BlockSpec((None, tn, tk), lambda i, j, k: (k, j, 0))
    elif b3:
        b_spec = pl.BlockSpec((None, tk, tn), lambda i, j, k: (j, k, 0))
    elif tb:
        b_spec = pl.BlockSpec((tn, tk), lambda i, j, k: (j, k))
    else:
        b_spec = pl.BlockSpec((tk, tn), lambda i, j, k: (k, j))
    if out3:
        o_spec = pl.BlockSpec((None, tm, tn), lambda i, j, k: (j, i, 0))
    else:
        o_spec = pl.BlockSpec((tm, tn), lambda i, j, k: (i, j))
    ins, specs = [a, b], [a_spec, b_spec]
    if add is not None:
        ins.append(add)
        specs.append(o_spec)
    return _pcall(
        body, name=name, grid=(M // tm, Nn // tn, nk), in_specs=specs, out_specs=o_spec,
        out_shape=jax.ShapeDtypeStruct((out3, M, tn) if out3 else (M, Nn), out_dtype),
        scratch_shapes=[pltpu.VMEM((tm, tn), F32)] if nk > 1 else [],
        compiler_params=_params(("parallel", "parallel", "arbitrary")),
    )(*ins)


def _mm_multi(a_list, b_list, *, name, add=None, out_dtype=F32, tm=512):
    M = a_list[0].shape[0]
    tm = _pick(M, tm)
    ns = len(a_list)
    b_arrs, b_specs = [], []
    for b in b_list:
        arr, shp, idx = b if isinstance(b, tuple) else (b, b.shape, (0, 0))
        b_arrs.append(arr)
        b_specs.append(pl.BlockSpec(shp, lambda i, idx=idx: idx))
    Nn = b_specs[0].block_shape[0]
    dn = (((1,), (1,)), ((), ()))

    def body(*refs):
        a_refs, b_refs = refs[:ns], refs[ns:2 * ns]
        acc = None
        for a_ref, b_ref in zip(a_refs, b_refs):
            part = lax.dot_general(a_ref[...].astype(BF16), b_ref[...], dn, preferred_element_type=F32)
            acc = part if acc is None else acc + part
        if add is not None:
            acc = acc + refs[2 * ns][...]
        refs[-1][...] = acc.astype(out_dtype)

    o_spec = pl.BlockSpec((tm, Nn), lambda i: (i, 0))
    specs = [pl.BlockSpec((tm, a.shape[1]), lambda i: (i, 0)) for a in a_list] + b_specs
    ins = list(a_list) + b_arrs
    if add is not None:
        specs.append(o_spec)
        ins.append(add)
    return _pcall(body, name=name, grid=(M // tm,), in_specs=specs, out_specs=o_spec,
                  out_shape=jax.ShapeDtypeStruct((M, Nn), out_dtype), compiler_params=_params(("parallel",)))(*ins)


def _lane(shape):
    return lax.broadcasted_iota(jnp.int32, shape, len(shape) - 1)


def _rot_half(v):
    w = v.shape[-1]
    first = (_lane(v.shape) % HEAD_DIM) < (HEAD_DIM // 2)
    return jnp.where(first, pltpu.roll(v, w - HEAD_DIM // 2, v.ndim - 1), pltpu.roll(v, HEAD_DIM // 2, v.ndim - 1))


def _widen(t, w):
    return t if w == t.shape[-1] else jnp.concatenate([t] * (w // t.shape[-1]), axis=-1)


def _unrope(v, cos, sins):
    w = v.shape[-1]
    return v * _widen(cos, w) - _rot_half(v) * _widen(sins, w)


def _rope_tables(positions):
    half = HEAD_DIM // 2
    inv = ROPE_THETA ** (-jnp.arange(half, dtype=F32) / half)
    ang = positions.astype(F32)[..., None] * inv
    cos, sin = jnp.cos(ang), jnp.sin(ang)
    cosf = jnp.concatenate([cos, cos, cos, cos], axis=-1)
    sins = jnp.concatenate([-sin, sin, -sin, sin], axis=-1)
    n = positions.shape[0] * positions.shape[1]
    return cosf.reshape(n, PAIR_W), sins.reshape(n, PAIR_W)


def _inproj(x2, scale, shift, w, cosf, sins, flags, *, T, name):
    N, D = x2.shape
    tm, tn = _pick(T, 512), VAR_W
    tpe = T // tm

    def body(x_ref, sc_ref, sh_ref, w_ref, c_ref, s_ref, f_ref, *outs):
        o_refs, u_ref = outs[:N_VAR], outs[N_VAR]
        j = pl.program_id(1)

        @pl.when(j == 0)
        def _():
            u_ref[...] = (x_ref[...] * (1.0 + sc_ref[0]) + sh_ref[0]).astype(BF16)

        acc = jnp.dot(u_ref[...], w_ref[...], preferred_element_type=F32)
        fl = f_ref[...]
        ce = 1.0 + (_widen(c_ref[...], tn) - 1.0) * fl
        se = _widen(s_ref[...], tn) * fl
        res = (acc * ce + _rot_half(acc) * se).astype(BF16)
        for v in range(N_VAR):
            @pl.when(j == v)
            def _(v=v):
                o_refs[v][...] = res

    ex = pl.BlockSpec((1, 1, D), lambda i, j: (i // tpe, 0, 0))
    tab = pl.BlockSpec((tm, PAIR_W), lambda i, j: (i, 0))
    keep = lambda w_: pl.BlockSpec((tm, w_), lambda i, j: (i, 0))
    return _pcall(
        body, name=name, grid=(N // tm, N_VAR),
        in_specs=[keep(D), ex, ex, pl.BlockSpec((D, tn), lambda i, j: (0, j)), tab, tab,
                  pl.BlockSpec((1, tn), lambda i, j: (0, j))],
        out_specs=[keep(tn)] * N_VAR + [keep(D)],
        out_shape=[jax.ShapeDtypeStruct((N, tn), BF16)] * N_VAR + [jax.ShapeDtypeStruct((N, D), BF16)],
        compiler_params=_params(("parallel", "arbitrary")),
    )(x2, scale, shift, w, cosf, sins, flags)


class _Geom:
    def __init__(self, g):
        if g is None:
            self.r, self.nq, self.n_back, self.sink = 1, A_Q_HEADS, A_WINDOW - 1, True
            self.qw, self.kw = QA_W, KA_W
            self.qidx = lambda j: 0
            self.kidx = lambda j: QA_W // KA_W
            self.vidx = lambda j: QA_W // KA_W + 1
        else:
            window, r = B_PATTERNS[g]
            self.r, self.nq, self.n_back, self.sink = r, B_GROUP_HEADS, window // r, False
            self.qw, self.kw = GB_W, GB_W
            self.qidx = lambda j: 3 * j
            self.kidx = lambda j: 3 * j + 1
            self.vidx = lambda j: 3 * j + 2
        self.ntile = self.qw // PAIR_W


def _stack_heads(t, scale=None):
    first = _lane(t.shape) < HEAD_DIM
    z = jnp.zeros_like(t)
    if scale is not None:
        t = t * jnp.asarray(scale, t.dtype)
    return jnp.concatenate([jnp.where(first, t, z), jnp.where(first, z, t)], axis=0)


def _lse_col(t):
    return jnp.concatenate([t[:, 0:1], t[:, HEAD_DIM:HEAD_DIM + 1]], axis=0)


def _unstack_heads(v2):
    return jnp.where(_lane((BLOCK, PAIR_W)) < HEAD_DIM, v2[:BLOCK], v2[BLOCK:])


def _dup_head(t, kh):
    tf = t.astype(F32)
    keep = (_lane(t.shape) < HEAD_DIM) if kh == 0 else (_lane(t.shape) >= HEAD_DIM)
    return jnp.where(keep, tf, pltpu.roll(tf, HEAD_DIM, 1)).astype(t.dtype)


def _fold_heads(t):
    return t + pltpu.roll(t, HEAD_DIM, 1)


def _band_mask(rows, i, n_back, single):
    nkeys = BLOCK if single else 2 * BLOCK
    qi = jnp.bitwise_and(lax.broadcasted_iota(jnp.int32, (rows, nkeys), 0), BLOCK - 1)
    ki = lax.broadcasted_iota(jnp.int32, (rows, nkeys), 1)
    if single:
        return qi >= ki
    dist = qi + BLOCK - ki
    return jnp.logical_and(jnp.logical_and(dist >= 0, dist <= n_back), jnp.logical_or(ki >= BLOCK, i > 0))


def _softmax_parts(s, valid, sinkcol):
    s = jnp.where(valid, s, NEG_INF)
    m = jnp.max(s, axis=1, keepdims=True)
    if sinkcol is not None:
        m = jnp.maximum(m, sinkcol)
    p = jnp.exp(s - m)
    den = jnp.sum(p, axis=1, keepdims=True)
    es = None
    if sinkcol is not None:
        es = jnp.exp(sinkcol - m)
        den = den + es
    return p, m, den, es


_NT = (((1,), (1,)), ((), ()))
_TN = (((0,), (0,)), ((), ()))


def _rows2(prev_ref, cur_ref, cs, single=False):
    if single:
        return cur_ref[0, :, cs]
    return jnp.concatenate([prev_ref[0, :, cs], cur_ref[0, :, cs]], axis=0)


def _sink_col(sink_ref, kh, nblocks):
    return jnp.concatenate([jnp.full((BLOCK, 1), sink_ref[kh * nblocks + b], F32) for b in range(nblocks)], axis=0)


def _tile(t):
    return slice(t * PAIR_W, (t + 1) * PAIR_W)


def _attn_fwd(qkv, sinks, g, *, NB, T, name):
    geo = _Geom(g)
    r, qw, kw, ntile = geo.r, geo.qw, geo.kw, geo.ntile
    tsub = T // r
    nblk = tsub // BLOCK
    qkv3 = qkv.reshape(NB, tsub, r * VAR_W)
    out_dtype = BF16 if g is None else F32
    tiles_per_kv = ntile // A_KV_HEADS

    single = nblk == 1

    def body(q_ref, kp_ref, kc_ref, vp_ref, vc_ref, sink_ref, o_ref, l_ref):
        i = pl.program_id(2)
        if geo.sink:
            kall, vall = _rows2(kp_ref, kc_ref, _tile(0)), _rows2(vp_ref, vc_ref, _tile(0))
            tiles = [[kh * tiles_per_kv + t for t in range(tiles_per_kv)] for kh in range(A_KV_HEADS)]
            q2s = [jnp.concatenate([_stack_heads(q_ref[0, :, _tile(t)], SCALE) for t in ts], axis=0) for ts in tiles]
            kks = [_dup_head(kall, kh) for kh in range(A_KV_HEADS)]
            vvs = [_dup_head(vall, kh) for kh in range(A_KV_HEADS)]
            sinkcols = [_sink_col(sink_ref, kh, 2 * tiles_per_kv) for kh in range(A_KV_HEADS)]
        else:
            tiles = [[t] for t in range(ntile)]
            q2s = [_stack_heads(q_ref[0, :, _tile(t)], SCALE) for t in range(ntile)]
            kks = [_rows2(kp_ref, kc_ref, _tile(t), single) for t in range(ntile)]
            vvs = [_rows2(vp_ref, vc_ref, _tile(t), single) for t in range(ntile)]
            sinkcols = [None] * ntile
        valid = _band_mask(q2s[0].shape[0], i, geo.n_back, single)
        ss = [lax.dot_general(q2, kk, _NT, preferred_element_type=F32) for q2, kk in zip(q2s, kks)]
        parts = [_softmax_parts(s, valid, sc) for s, sc in zip(ss, sinkcols)]
        o2s = [jnp.dot(p.astype(BF16), vv, preferred_element_type=F32) / den for (p, m, den, _), vv in zip(parts, vvs)]
        for ts, o2, (p, m, den, _) in zip(tiles, o2s, parts):
            lse2 = jnp.broadcast_to(m + jnp.log(den), (o2.shape[0], PAIR_W))
            for n, t in enumerate(ts):
                rows = slice(2 * BLOCK * n, 2 * BLOCK * (n + 1))
                o_ref[0, :, _tile(t)] = _unstack_heads(o2[rows]).astype(out_dtype)
                l_ref[0, :, _tile(t)] = _unstack_heads(lse2[rows])

    prev = lambda i: jnp.maximum(i - 1, 0)
    in_specs = [
        pl.BlockSpec((1, BLOCK, qw), lambda b, j, i: (b, i, geo.qidx(j))),
        pl.BlockSpec((1, BLOCK, kw), lambda b, j, i: (b, prev(i), geo.kidx(j))),
        pl.BlockSpec((1, BLOCK, kw), lambda b, j, i: (b, i, geo.kidx(j))),
        pl.BlockSpec((1, BLOCK, kw), lambda b, j, i: (b, prev(i), geo.vidx(j))),
        pl.BlockSpec((1, BLOCK, kw), lambda b, j, i: (b, i, geo.vidx(j))),
        pl.BlockSpec(memory_space=pltpu.SMEM),
    ]
    o_spec = pl.BlockSpec((1, BLOCK, qw), lambda b, j, i: (b, i, j))
    shape = (NB, tsub, r * qw)
    o, lse = _pcall(
        body, name=name, grid=(NB, r, nblk), in_specs=in_specs, out_specs=[o_spec, o_spec],
        out_shape=[jax.ShapeDtypeStruct(shape, out_dtype), jax.ShapeDtypeStruct(shape, F32)],
        compiler_params=_params(("parallel", "parallel", "arbitrary")),
    )(qkv3, qkv3, qkv3, qkv3, qkv3, sinks)
    return o.reshape(NB * T, qw), lse.reshape(NB * T, qw), lse


def _attn_bwd(qkv, do, lse, dlse, cosf, sins, sinks, g, *, NB, T, name):
    geo = _Geom(g)
    r, qw, kw, ntile = geo.r, geo.qw, geo.kw, geo.ntile
    tsub = T // r
    nblk = tsub // BLOCK
    view = lambda a, w: a.reshape(NB, tsub, r * w)
    has_dlse = dlse is not None
    tiles_per_kv = ntile // A_KV_HEADS

    single = nblk == 1
    krows = BLOCK if single else 2 * BLOCK
    nsteps = 1 if single else nblk + 1

    def grads(q2s, kks, vvs, do2s, i, lsecols, sinkcols, dlcols):
        valid = _band_mask(q2s[0].shape[0], i, geo.n_back, single)
        ss = [lax.dot_general(q2, kk, _NT, preferred_element_type=F32) for q2, kk in zip(q2s, kks)]
        dps = [lax.dot_general(do2, vv, _NT, preferred_element_type=F32) for do2, vv in zip(do2s, vvs)]
        ps, dss, sks = [], [], []
        for s, dp, ls, sc, dl in zip(ss, dps, lsecols, sinkcols, dlcols):
            p = jnp.exp(jnp.where(valid, s, NEG_INF) - ls)
            delta = jnp.sum(p * dp, axis=1, keepdims=True)
            sks.append(jnp.exp(sc - ls) * delta if sc is not None else None)
            if dl is not None:
                delta = delta - dl
            ps.append(p.astype(BF16))
            dss.append((p * (dp - delta)).astype(BF16))
        dq2s = [jnp.dot(ds, kk, preferred_element_type=F32) * SCALE for ds, kk in zip(dss, kks)]
        dkks = [lax.dot_general(ds, q2, _TN, preferred_element_type=F32) for ds, q2 in zip(dss, q2s)]
        dvvs = [lax.dot_general(p, do2, _TN, preferred_element_type=F32) for p, do2 in zip(ps, do2s)]
        return dq2s, dkks, dvvs, sks

    def body(*refs):
        it = iter(refs)
        q_ref, kp_ref, kc_ref, vp_ref, vc_ref, do_ref, l_ref = (next(it) for _ in range(7))
        dl_ref = next(it) if has_dlse else None
        c_ref, s_ref, sink_ref, o_ref, ds_ref, dq_s, dk_s, dv_s, car_q, car_k, car_v = (next(it) for _ in range(11))
        b, j, i = pl.program_id(0), pl.program_id(1), pl.program_id(2)

        @pl.when(jnp.logical_and(b == 0, jnp.logical_and(j == 0, i == 0)))
        def _():
            ds_ref[...] = jnp.zeros_like(ds_ref)

        def compute():
            if geo.sink:
                kall, vall = _rows2(kp_ref, kc_ref, _tile(0)), _rows2(vp_ref, vc_ref, _tile(0))
                nb = 2 * tiles_per_kv
                tiles = [[kh * tiles_per_kv + t for t in range(tiles_per_kv)] for kh in range(A_KV_HEADS)]
                cat = lambda f, ts: jnp.concatenate([f(t) for t in ts], axis=0)
                dq2s, dkks, dvvs, sks = grads(
                    [cat(lambda t: _stack_heads(q_ref[0, :, _tile(t)], SCALE), ts) for ts in tiles],
                    [_dup_head(kall, kh) for kh in range(A_KV_HEADS)],
                    [_dup_head(vall, kh) for kh in range(A_KV_HEADS)],
                    [cat(lambda t: _stack_heads(do_ref[0, :, _tile(t)]), ts) for ts in tiles], i,
                    [cat(lambda t: _lse_col(l_ref[0, :, _tile(t)]), ts) for ts in tiles],
                    [_sink_col(sink_ref, kh, nb) for kh in range(A_KV_HEADS)], [None] * A_KV_HEADS)
                lane1 = _lane((1, PAIR_W))
                dsink = jnp.zeros((1, PAIR_W), F32)
                for kh, (ts, dq2, sk) in enumerate(zip(tiles, dq2s, sks)):
                    for n, t in enumerate(ts):
                        dq_s[:, _tile(t)] = _unstack_heads(dq2[2 * BLOCK * n:2 * BLOCK * (n + 1)])
                    for bb in range(nb):
                        dsink = dsink + jnp.where(lane1 == kh * nb + bb, -jnp.sum(sk[BLOCK * bb:BLOCK * (bb + 1)]), 0.0)
                second = _lane((krows, PAIR_W)) >= HEAD_DIM
                dk_s[...] = jnp.where(second, _fold_heads(dkks[1]), _fold_heads(dkks[0]))
                dv_s[...] = jnp.where(second, _fold_heads(dvvs[1]), _fold_heads(dvvs[0]))
                ds_ref[0:1, :] += dsink
            else:
                dq2s, dkks, dvvs, _ = grads(
                    [_stack_heads(q_ref[0, :, _tile(t)], SCALE) for t in range(ntile)],
                    [_rows2(kp_ref, kc_ref, _tile(t), single) for t in range(ntile)],
                    [_rows2(vp_ref, vc_ref, _tile(t), single) for t in range(ntile)],
                    [_stack_heads(do_ref[0, :, _tile(t)]) for t in range(ntile)], i,
                    [_lse_col(l_ref[0, :, _tile(t)]) for t in range(ntile)], [None] * ntile,
                    [_lse_col(dl_ref[0, :, _tile(t)]) for t in range(ntile)])
                for t in range(ntile):
                    dq_s[:, _tile(t)] = _unstack_heads(dq2s[t])
                    dk_s[0:krows, _tile(t)] = dkks[t]
                    dv_s[0:krows, _tile(t)] = dvvs[t]

        def emit(dq, dk, dv):
            cos, sn = c_ref[0], s_ref[0]
            o_ref[0, :, 0:qw] = _unrope(dq, cos, sn).astype(BF16)
            o_ref[0, :, qw:qw + kw] = _unrope(dk, cos, sn).astype(BF16)
            o_ref[0, :, qw + kw:qw + 2 * kw] = dv.astype(BF16)
            if qw + 2 * kw < VAR_W:
                o_ref[0, :, qw + 2 * kw:VAR_W] = jnp.zeros((BLOCK, VAR_W - qw - 2 * kw), BF16)

        if single:
            compute()
            emit(dq_s[...], dk_s[0:BLOCK, :], dv_s[0:BLOCK, :])
            return

        @pl.when(i == 0)
        def _():
            car_q[...] = jnp.zeros_like(car_q)
            car_k[...] = jnp.zeros_like(car_k)
            car_v[...] = jnp.zeros_like(car_v)

        @pl.when(i == nblk)
        def _():
            dk_s[...] = jnp.zeros_like(dk_s)
            dv_s[...] = jnp.zeros_like(dv_s)

        pl.when(i < nblk)(compute)
        emit(car_q[...], car_k[...] + dk_s[0:BLOCK, :], car_v[...] + dv_s[0:BLOCK, :])
        car_q[...] = dq_s[...]
        car_k[...] = dk_s[BLOCK:2 * BLOCK, :]
        car_v[...] = dv_s[BLOCK:2 * BLOCK, :]

    cur = lambda i: jnp.minimum(i, nblk - 1)
    prv = lambda i: jnp.maximum(jnp.minimum(i, nblk - 1) - 1, 0)
    outb = lambda i: jnp.maximum(i - 1, 0)
    qrow = pl.BlockSpec((1, BLOCK, qw), lambda b, j, i: (b, cur(i), j))
    in_specs = [
        pl.BlockSpec((1, BLOCK, qw), lambda b, j, i: (b, cur(i), geo.qidx(j))),
        pl.BlockSpec((1, BLOCK, kw), lambda b, j, i: (b, prv(i), geo.kidx(j))),
        pl.BlockSpec((1, BLOCK, kw), lambda b, j, i: (b, cur(i), geo.kidx(j))),
        pl.BlockSpec((1, BLOCK, kw), lambda b, j, i: (b, prv(i), geo.vidx(j))),
        pl.BlockSpec((1, BLOCK, kw), lambda b, j, i: (b, cur(i), geo.vidx(j))),
        qrow, qrow,
    ]
    ins = [view(qkv, VAR_W)] * 5 + [view(do, qw), view(lse, qw)]
    if has_dlse:
        in_specs.append(qrow)
        ins.append(view(dlse, qw))
    in_specs += [
        pl.BlockSpec((1, BLOCK, PAIR_W), lambda b, j, i: (b, outb(i), j)),
        pl.BlockSpec((1, BLOCK, PAIR_W), lambda b, j, i: (b, outb(i), j)),
        pl.BlockSpec(memory_space=pltpu.SMEM),
    ]
    ins += [view(cosf, PAIR_W), view(sins, PAIR_W), sinks]
    scratch = [pltpu.VMEM((BLOCK, qw), F32), pltpu.VMEM((2 * BLOCK, kw), F32), pltpu.VMEM((2 * BLOCK, kw), F32),
               pltpu.VMEM((BLOCK, qw), F32), pltpu.VMEM((BLOCK, kw), F32), pltpu.VMEM((BLOCK, kw), F32)]
    dqkv, dsink = _pcall(
        body, name=name, grid=(NB, r, nsteps), in_specs=in_specs,
        out_specs=[pl.BlockSpec((1, BLOCK, VAR_W), lambda b, j, i: (b, outb(i), j)),
                   pl.BlockSpec((8, PAIR_W), lambda b, j, i: (0, 0))],
        out_shape=[jax.ShapeDtypeStruct((NB, tsub, r * VAR_W), BF16), jax.ShapeDtypeStruct((8, PAIR_W), F32)],
        scratch_shapes=scratch, compiler_params=_params(("arbitrary", "arbitrary", "arbitrary")),
    )(*ins)
    return dqkv.reshape(NB * T, VAR_W), dsink


class _Rows:
    def __init__(self, N, T, tm):
        self.N, self.tm, self.tpe, self.grid = N, tm, T // tm, (N // tm,)

    def row(self, w, col=0):
        return pl.BlockSpec((self.tm, w), lambda i: (i, col))

    def ex(self, w):
        return pl.BlockSpec((1, 1, w), lambda i: (i // self.tpe, 0, 0))

    def const(self, shape):
        return pl.BlockSpec(shape, lambda i: tuple(0 for _ in shape))

    def first_of_example(self):
        return pl.program_id(0) % self.tpe == 0


def _acc(ref, first, val):
    @pl.when(first)
    def _():
        ref[0] = val

    @pl.when(jnp.logical_not(first))
    def _():
        ref[0] += val


def _colsum(v):
    return jnp.sum(v, axis=0, keepdims=True)


def _ln_stats(r):
    mu = jnp.mean(r, axis=-1, keepdims=True)
    xc = r - mu
    var = jnp.mean(xc * xc, axis=-1, keepdims=True)
    rstd = lax.rsqrt(var + LN_EPS)
    return xc * rstd, rstd


def _ln_bwd(dy, xhat, rstd, gain):
    dxh = dy * gain
    return rstd * (dxh - jnp.mean(dxh, axis=-1, keepdims=True) - xhat * jnp.mean(dxh * xhat, axis=-1, keepdims=True))


def _silu_parts(v):
    s = jax.nn.sigmoid(v)
    return v * s, s * (1.0 + v * (1.0 - s))


def _local_step(x, mod, positions, w_in, rest_weights, sinks, ln1_g, ln1_b, ln2_g, ln2_b, target, hook=None):
    hook = hook or (lambda event, **data: None)
    NB, T, D = x.shape
    N = NB * T
    x2 = x.reshape(N, D)
    tgt2 = target.reshape(N, D)
    shift_m, scale_m, gate_m, shift_f, scale_f, gate_f = [mod[:, None, k * D:(k + 1) * D] for k in range(6)]
    cosf, sins = _rope_tables(positions)
    col = jnp.arange(QKV_P)
    vcol = col % VAR_W
    flags = jnp.where(col < VAR_W, vcol < QA_W + KA_W, vcol < 2 * GB_W).astype(F32)[None]
    R = _Rows(N, T, _pick(T, 256))
    sds = jax.ShapeDtypeStruct
    exsum = lambda w=D: sds((NB, 1, w), F32)
    ngrp = len(B_PATTERNS)

    *qkv, u = _inproj(x2, scale_m, shift_m, w_in, cosf, sins, flags, T=T, name="inproj_qkv")
    gates = _mm(u, w_in[:, QKV_P:], name="inproj_gates")
    oa, _, la = _attn_fwd(qkv[0], sinks, None, NB=NB, T=T, name="attn_a_fwd")
    ob_parts = [_attn_fwd(qkv[1 + g], sinks, g, NB=NB, T=T, name=f"attn_b{g}_fwd") for g in range(ngrp)]
    (o1, l1, lv1), (o2, l2, lv2), (o3, l3, lv3) = ob_parts
    w_a, w_b, w_o, w_gu, w_d = rest_weights()
    F = w_d.shape[0]

    def merge_fwd(o1r, o2r, o3r, l1r, l2r, l3r, ob_ref):
        la, lb, lc = l1r[...], l2r[...], l3r[...]
        mx = jnp.maximum(jnp.maximum(la, lb), lc)
        ea, eb, ec = jnp.exp(la - mx), jnp.exp(lb - mx), jnp.exp(lc - mx)
        ob_ref[...] = ((ea * o1r[...] + eb * o2r[...] + ec * o3r[...]) / (ea + eb + ec)).astype(BF16)

    ob = _pcall(merge_fwd, name="merge_fwd", grid=R.grid, in_specs=[R.row(GB_W)] * 6, out_specs=R.row(GB_W),
                out_shape=sds((N, GB_W), BF16), compiler_params=_params(("parallel",)))(o1, o2, o3, l1, l2, l3)

    ya = _mm(oa, w_a, name="branch_a")
    yb = _mm(ob, w_b, b3=True, name="branch_b")

    def gate_fwd(ya_r, yb_r, ga_r, gb_r, mg_ref):
        mg_ref[...] = (jax.nn.sigmoid(ga_r[...]) * ya_r[...] + jax.nn.sigmoid(gb_r[...]) * yb_r[...]).astype(BF16)

    merged = _pcall(gate_fwd, name="gate_fwd", grid=R.grid, in_specs=[R.row(D), R.row(D), R.row(D, 0), R.row(D, 1)],
                    out_specs=R.row(D), out_shape=sds((N, D), BF16),
                    compiler_params=_params(("parallel",)))(ya, yb, gates, gates)
    y = _mm(merged, w_o, name="out_proj")

    def norm1_fwd(x_r, y_r, gm_r, g_r, b_r, sf_r, hf_r, r1_ref, x1_ref, u2_ref):
        r1 = ALPHA * x_r[...] + (1.0 + gm_r[0]) * y_r[...]
        xhat, _ = _ln_stats(r1)
        x1 = xhat * g_r[...] + b_r[...]
        r1_ref[...] = r1
        x1_ref[...] = x1
        u2_ref[...] = (x1 * (1.0 + sf_r[0]) + hf_r[0]).astype(BF16)

    r1, x1, u2 = _pcall(
        norm1_fwd, name="norm1_fwd", grid=R.grid,
        in_specs=[R.row(D), R.row(D), R.ex(D), R.const((1, D)), R.const((1, D)), R.ex(D), R.ex(D)],
        out_specs=[R.row(D)] * 3, out_shape=[sds((N, D), F32), sds((N, D), F32), sds((N, D), BF16)],
        compiler_params=_params(("parallel",)))(x2, y, gate_m, ln1_g, ln1_b, scale_f, shift_f)

    tnf = w_gu.shape[2]
    nft = w_gu.shape[0] // 2
    tmf = _pick(N, 512)

    def ffn_up(u_r, wg_r, wu_r, hg_ref, hu_ref, a_ref):
        hg = jnp.dot(u_r[...], wg_r[...], preferred_element_type=F32)
        hu = jnp.dot(u_r[...], wu_r[...], preferred_element_type=F32)
        sl, _ = _silu_parts(hg)
        hg_ref[...] = hg.astype(BF16)
        hu_ref[...] = hu.astype(BF16)
        a_ref[...] = (sl * hu).astype(BF16)

    ftile = pl.BlockSpec((tmf, tnf), lambda i, j: (i, j))
    hg, hu, act = _pcall(
        ffn_up, name="ffn_up", grid=(N // tmf, nft),
        in_specs=[pl.BlockSpec((tmf, D), lambda i, j: (i, 0)), pl.BlockSpec((None, D, tnf), lambda i, j: (j, 0, 0)),
                  pl.BlockSpec((None, D, tnf), lambda i, j: (j + nft, 0, 0))],
        out_specs=[ftile] * 3, out_shape=[sds((N, F), BF16)] * 3,
        compiler_params=_params(("parallel", "arbitrary")))(u2, w_gu, w_gu)
    y2 = _mm(act, w_d, name="ffn_down")

    def norm2_loss_bwd(x1_r, y2_r, t_r, gf_r, g_r, b_r, dy2_ref, dx1_ref, dgf_ref, dg_ref, db_ref, loss_ref):
        first = R.first_of_example()
        y2v = y2_r[...]
        r2 = ALPHA * x1_r[...] + (1.0 + gf_r[0]) * y2v
        xhat, rstd = _ln_stats(r2)
        err = xhat * g_r[...] + b_r[...] - t_r[...]
        dx2 = err * (1.0 / D)
        dr2 = _ln_bwd(dx2, xhat, rstd, g_r[...])
        dy2_ref[...] = ((1.0 + gf_r[0]) * dr2).astype(BF16)
        dx1_ref[...] = ALPHA * dr2
        _acc(dgf_ref, first, _colsum(dr2 * y2v))
        _acc(dg_ref, first, _colsum(dx2 * xhat))
        _acc(db_ref, first, _colsum(dx2))
        part = 0.5 * jnp.sum(jnp.mean(err * err, axis=-1, keepdims=True))
        _acc(loss_ref, first, jnp.broadcast_to(part, (1, 128)))

    dy2, dx1p, dgate_f, dg2, db2, loss_p = _pcall(
        norm2_loss_bwd, name="norm2_loss_bwd", grid=R.grid,
        in_specs=[R.row(D), R.row(D), R.row(D), R.ex(D), R.const((1, D)), R.const((1, D))],
        out_specs=[R.row(D), R.row(D), R.ex(D), R.ex(D), R.ex(D), R.ex(128)],
        out_shape=[sds((N, D), BF16), sds((N, D), F32), exsum(), exsum(), exsum(), exsum(128)],
        compiler_params=_params(("arbitrary",)))(x1, y2, tgt2, gate_f, ln2_g, ln2_b)

    g_wd = _mm(act, dy2, ta=True, out_dtype=BF16, name="ffn_down_dw")

    tmd = _pick(N, 256)

    def ffn_down_dx(dy_r, wd_r, hg_r, hu_r, dh_ref):
        for t in range(nft):
            cs = slice(t * tnf, (t + 1) * tnf)
            da = lax.dot_general(dy_r[...], wd_r[cs, :], _NT, preferred_element_type=F32)
            sl, dsl = _silu_parts(hg_r[:, cs].astype(F32))
            dh_ref[:, cs] = (da * hu_r[:, cs].astype(F32) * dsl).astype(BF16)
            dh_ref[:, F + t * tnf:F + (t + 1) * tnf] = (da * sl).astype(BF16)

    rowd = lambda w_: pl.BlockSpec((tmd, w_), lambda i: (i, 0))
    dh = _pcall(
        ffn_down_dx, name="ffn_down_dx", grid=(N // tmd,),
        in_specs=[rowd(D), pl.BlockSpec((F, D), lambda i: (0, 0)), rowd(F), rowd(F)],
        out_specs=rowd(2 * F), out_shape=sds((N, 2 * F), BF16),
        compiler_params=_params(("parallel",)))(dy2, w_d, hg, hu)
    du2 = _mm(dh, w_gu, tb=True, b3=True, name="ffn_up_dx")
    g_wgu = _mm(u2, dh, ta=True, out3=w_gu.shape[0], out_dtype=BF16, name="ffn_up_dw")

    def norm1_bwd(dx1p_r, du2_r, x1_r, r1_r, y_r, sf_r, gm_r, g_r,
                  dxp_ref, dy_ref, dsf_ref, dhf_ref, dgm_ref, dg_ref, db_ref):
        first = R.first_of_example()
        du2v = du2_r[...]
        dx1 = dx1p_r[...] + du2v * (1.0 + sf_r[0])
        xhat, rstd = _ln_stats(r1_r[...])
        dr1 = _ln_bwd(dx1, xhat, rstd, g_r[...])
        dxp_ref[...] = ALPHA * dr1
        dy_ref[...] = ((1.0 + gm_r[0]) * dr1).astype(BF16)
        _acc(dsf_ref, first, _colsum(du2v * x1_r[...]))
        _acc(dhf_ref, first, _colsum(du2v))
        _acc(dgm_ref, first, _colsum(dr1 * y_r[...]))
        _acc(dg_ref, first, _colsum(dx1 * xhat))
        _acc(db_ref, first, _colsum(dx1))

    dxp, dy, dscale_f, dshift_f, dgate_m, dg1, db1 = _pcall(
        norm1_bwd, name="norm1_bwd", grid=R.grid,
        in_specs=[R.row(D)] * 5 + [R.ex(D), R.ex(D), R.const((1, D))],
        out_specs=[R.row(D), R.row(D)] + [R.ex(D)] * 5,
        out_shape=[sds((N, D), F32), sds((N, D), BF16)] + [exsum()] * 5,
        compiler_params=_params(("arbitrary",)))(dx1p, du2, x1, r1, y, scale_f, gate_m, ln1_g)

    dmerged = _mm(dy, w_o, tb=True, name="out_proj_dx")
    g_wo = _mm(merged, dy, ta=True, out_dtype=BF16, name="out_proj_dw")

    def gate_bwd(dm_r, ya_r, yb_r, ga_r, gb_r, dya_ref, dyb_ref, dg_ref):
        dm = dm_r[...]
        sa, sb = jax.nn.sigmoid(ga_r[...]), jax.nn.sigmoid(gb_r[...])
        dya_ref[...] = (dm * sa).astype(BF16)
        dyb_ref[...] = (dm * sb).astype(BF16)
        dg_ref[:, :D] = (dm * ya_r[...] * sa * (1.0 - sa)).astype(BF16)
        dg_ref[:, D:] = (dm * yb_r[...] * sb * (1.0 - sb)).astype(BF16)

    dya, dyb, dgates = _pcall(
        gate_bwd, name="gate_bwd", grid=R.grid, in_specs=[R.row(D)] * 3 + [R.row(D, 0), R.row(D, 1)],
        out_specs=[R.row(D), R.row(D), R.row(2 * D)],
        out_shape=[sds((N, D), BF16), sds((N, D), BF16), sds((N, 2 * D), BF16)],
        compiler_params=_params(("parallel",)))(dmerged, ya, yb, gates, gates)

    doa = _mm(dya, w_a, tb=True, out_dtype=BF16, name="branch_a_dx")
    g_wa = _mm(oa, dya, ta=True, out_dtype=BF16, name="branch_a_dw")
    dob = _mm(dyb, w_b, tb=True, b3=True, name="branch_b_dx")
    g_wb = _mm(ob, dyb, ta=True, out3=w_b.shape[0], out_dtype=BF16, name="branch_b_dw")
    hook("rest_grads", g_wa=g_wa, g_wb=g_wb, g_wo=g_wo, g_wgu=g_wgu, g_wd=g_wd)

    seg = (jnp.arange(GB_W)[:, None] // HEAD_DIM == jnp.arange(GB_W)[None, :] // HEAD_DIM).astype(BF16)

    def merge_bwd(dob_r, o1r, o2r, o3r, l1r, l2r, l3r, seg_r, d1, d2, d3, e1, e2, e3):
        dob_v = dob_r[...]
        la, lb, lc = l1r[...], l2r[...], l3r[...]
        mx = jnp.maximum(jnp.maximum(la, lb), lc)
        ea, eb, ec = jnp.exp(la - mx), jnp.exp(lb - mx), jnp.exp(lc - mx)
        inv = 1.0 / (ea + eb + ec)
        ws = [ea * inv, eb * inv, ec * inv]

        def headsum(v):
            hi = v.astype(BF16)
            r1_ = v - hi.astype(F32)
            mid = r1_.astype(BF16)
            lo = (r1_ - mid.astype(F32)).astype(BF16)
            sm = seg_r[...]
            return (jnp.dot(hi, sm, preferred_element_type=F32) + jnp.dot(mid, sm, preferred_element_type=F32)
                    + jnp.dot(lo, sm, preferred_element_type=F32))

        dws = [headsum(dob_v * o[...]) for o in (o1r, o2r, o3r)]
        mean = ws[0] * dws[0] + ws[1] * dws[1] + ws[2] * dws[2]
        for w_, dw_, d_ref, e_ref in zip(ws, dws, (d1, d2, d3), (e1, e2, e3)):
            d_ref[...] = (w_ * dob_v).astype(BF16)
            e_ref[...] = w_ * (dw_ - mean)

    mb = _pcall(
        merge_bwd, name="merge_bwd", grid=R.grid, in_specs=[R.row(GB_W)] * 7 + [R.const((GB_W, GB_W))],
        out_specs=[R.row(GB_W)] * 6, out_shape=[sds((N, GB_W), BF16)] * 3 + [sds((N, GB_W), F32)] * 3,
        compiler_params=_params(("parallel",)))(dob, o1, o2, o3, l1, l2, l3, seg)
    do_b, dlse_b = mb[:3], mb[3:]
    hook("merge_bwd_done")

    dqkv_a, dsink = _attn_bwd(qkv[0], doa, la, None, cosf, sins, sinks, None, NB=NB, T=T, name="attn_a_bwd")
    hook("attn_a_bwd_done")
    dqkv = [dqkv_a]
    for g in range(ngrp):
        dqkv.append(_attn_bwd(qkv[1 + g], do_b[g], (lv1, lv2, lv3)[g], dlse_b[g], cosf, sins, sinks, g, NB=NB, T=T,
                              name=f"attn_b{g}_bwd")[0])
        hook(f"attn_b{g}_bwd_done")

    g_win = [_mm(u, dseg, ta=True, out_dtype=BF16, name=f"inproj_dw{n}") for n, dseg in enumerate(dqkv + [dgates])]
    hook("win_grads", g_win=g_win)
    wvar = lambda v: (w_in, (D, VAR_W), (0, v))
    du = _mm_multi(dqkv[:1], [wvar(0)], name="inproj_dx0")
    hook("inproj_dx0_done")
    du = _mm_multi(dqkv[1:] + [dgates], [wvar(v) for v in range(1, N_VAR)] + [(w_in, (D, 2 * D), (0, QKV_P // (2 * D)))],
                   add=du, tm=256, name="inproj_dx1")
    hook("inproj_dx1_done")

    def x_bwd(dxp_r, du_r, x_r, sm_r, gx_ref, dsm_ref, dhm_ref):
        first = R.first_of_example()
        duv = du_r[...]
        gx_ref[...] = dxp_r[...] + duv * (1.0 + sm_r[0])
        _acc(dsm_ref, first, _colsum(duv * x_r[...]))
        _acc(dhm_ref, first, _colsum(duv))

    gx, dscale_m, dshift_m = _pcall(
        x_bwd, name="x_bwd", grid=R.grid, in_specs=[R.row(D)] * 3 + [R.ex(D)],
        out_specs=[R.row(D), R.ex(D), R.ex(D)], out_shape=[sds((N, D), F32), exsum(), exsum()],
        compiler_params=_params(("arbitrary",)))(dxp, du, x2, scale_m)
    hook("x_bwd_done")

    dmod =jnp.concatenate([dshift_m, dscale_m, dgate_m, dshift_f, dscale_f, dgate_f], axis=-1)[:, 0]
    ln_grads = jnp.concatenate([dg1, db1, dg2, db2], axis=1)
    return dict(loss=loss_p[:, 0, 0], grad_x=gx.reshape(NB, T, D), g_win=g_win, g_wa=g_wa, g_wb=g_wb, g_wo=g_wo,
                g_wgu=g_wgu, g_wd=g_wd, dmod=dmod, ln_grads=ln_grads, dsink=dsink[0, :A_Q_HEADS])


def _coords():
    return lax.axis_index("x"), lax.axis_index("y"), lax.axis_index("c")


def _allgather_small(blk, *, name):
    m_per, n = blk.shape

    def body(x_ref, out_ref, send_sems, recv_sems, local_sem):
        x, y, c = _coords()
        me, sibling = (x, y, c), (x, y, 1 - c)
        chips = [(1 - x, y), (x, 1 - y), (1 - x, 1 - y)]

        def rows(px, py, pc):
            return out_ref.at[pl.ds((4 * px + 2 * py + pc) * m_per, m_per), :]

        def copy(k, block, to, src=None):
            return pltpu.make_async_remote_copy(
                src_ref=rows(*block) if src is None else src, dst_ref=rows(*block),
                send_sem=send_sems.at[k], recv_sem=recv_sems.at[k], device_id=to, device_id_type=MESH)

        mine = pltpu.make_async_copy(x_ref, rows(*me), local_sem)
        mine.start()
        first = [copy(0, me, sibling, src=x_ref)]
        first += [copy(1 + j, me, (*chip, c), src=x_ref) for j, chip in enumerate(chips)]
        for cp in first:
            cp.start()
        passed = [copy(4 + j, (*chip, c), sibling) for j, chip in enumerate(chips)]
        for j, chip in enumerate(chips):
            copy(1 + j, (*chip, c), me).wait_recv()
            passed[j].start()
        copy(0, sibling, me).wait_recv()
        for j, chip in enumerate(chips):
            copy(4 + j, (*chip, 1 - c), me).wait_recv()
        for cp in first + passed:
            cp.wait_send()
        mine.wait()

    return _pcall(
        body, name=name, out_shape=jax.ShapeDtypeStruct((8 * m_per, n), blk.dtype),
        in_specs=[pl.BlockSpec(memory_space=pltpu.VMEM)], out_specs=pl.BlockSpec(memory_space=pltpu.VMEM),
        scratch_shapes=[pltpu.SemaphoreType.DMA((7,)), pltpu.SemaphoreType.DMA((7,)), pltpu.SemaphoreType.DMA],
        compiler_params=pltpu.CompilerParams(vmem_limit_bytes=VMEM_LIMIT_BYTES),
    )(blk)


def _exchange(srcs, dsts, plan, *, name, dst_inits=None):
    na = len(dsts)
    nrem = len(plan(0, 0, 0))

    def body(*refs):
        refs = list(refs)
        src_refs = [refs.pop(0) for _ in range(na)] if srcs is not None else None
        if dst_inits is not None:
            del refs[:na]
        dst_refs, (send_sems, recv_sems) = refs[:na], refs[na:]
        start, wait = _copies(dst_refs if src_refs is None else src_refs, dst_refs, send_sems, recv_sems, plan)
        start()
        wait()

    hbm = pl.BlockSpec(memory_space=pl.ANY)
    ins = (list(srcs) if srcs is not None else []) + (list(dst_inits) if dst_inits is not None else [])
    base = na if srcs is not None else 0
    aliases = {base + a: a for a in range(na)} if dst_inits is not None else {}
    return _pcall(
        body, name=name, out_shape=list(dsts), in_specs=[hbm] * len(ins), out_specs=[hbm] * na,
        input_output_aliases=aliases,
        scratch_shapes=[pltpu.SemaphoreType.DMA((na * nrem,)), pltpu.SemaphoreType.DMA((na * nrem,))],
    )(*ins)


def _other_chips(x, y):
    return [(1 - x, y), (x, 1 - y), (1 - x, 1 - y)]


def _round(ride, carrier, name):
    if carrier is not None:
        _RIDES.setdefault(carrier, []).append(ride)
        return
    srcs = ride.srcs() if callable(ride.srcs) else ride.srcs
    inits = ride.dst_inits() if callable(ride.dst_inits) else ride.dst_inits
    ride.out = list(_exchange(srcs, ride.dsts, ride.plan, name=name, dst_inits=inits))


class _Gather:
    def __init__(self, shards, chip, tag, carriers=(None, None)):
        def plan_ici(x, y, c):
            k = 2 * x + y
            return [((c,), (k, c), (2 * px + py, c), (px, py, c)) for px, py in _other_chips(x, y)]

        def plan_d2d(x, y, c):
            return [((2 * px + py, c), (2 * px + py, c), (2 * px + py, 1 - c), (x, y, 1 - c))
                    for px, py in _other_chips(x, y)]

        self.shards, self.chip = shards, chip
        dsts = [jax.ShapeDtypeStruct((4,) + s.shape, s.dtype) for s in shards]
        ici = _Ride(shards, dsts, plan_ici)
        self.d2d = _Ride(None, dsts, plan_d2d, dst_inits=lambda: ici.out)
        _round(ici, carriers[0], f"gather_{tag}_ici")
        _round(self.d2d, carriers[1], f"gather_{tag}_d2d")

    def result(self):
        full = [lax.dynamic_update_index_in_dim(f, s, self.chip, 0) for f, s in zip(self.d2d.out, self.shards)]
        return [f.reshape((4, 2 * f.shape[2], f.shape[3])) for f in full]


def _add_pairs(a, b, *, name):
    s, hr, wd = a.shape
    tr = _pick(hr, 600, 16)

    def body(a_ref, b_ref, o_ref):
        o_ref[...] = (a_ref[...].astype(F32) + b_ref[...].astype(F32)).astype(BF16)

    spec = pl.BlockSpec((1, tr, wd), lambda j, i: (j, i, 0))
    return _pcall(body, name=name, grid=(s, hr // tr), in_specs=[spec, spec], out_specs=spec,
                  out_shape=jax.ShapeDtypeStruct(a.shape, BF16), compiler_params=_params(("parallel", "parallel")))(a, b)


def _sum_chips(b, *, name):
    s, hr, wd = b.shape
    tr = _pick(hr, 600, 16)

    def body(b_ref, o_ref):
        acc = b_ref[0].astype(F32)
        for k in range(1, s):
            acc = acc + b_ref[k].astype(F32)
        o_ref[...] = acc

    return _pcall(body, name=name, grid=(hr // tr,), in_specs=[pl.BlockSpec((s, tr, wd), lambda i: (0, i, 0))],
                  out_specs=pl.BlockSpec((tr, wd), lambda i: (i, 0)), out_shape=jax.ShapeDtypeStruct((hr, wd), F32),
                  compiler_params=_params(("parallel",)))(b)


class _ReduceScatter:
    def __init__(self, gs, chip, ci, tag):
        self.gs, self.chip, self.ci, self.tag = gs, chip, ci, tag
        self.half_t = [jax.ShapeDtypeStruct((g.shape[0],) + g.shape[2:], BF16) for g in gs]

    def pair(self, carrier=None):
        plan = lambda x, y, c: [((slice(None), 1 - c), (), (), (x, y, 1 - c))]
        self.r1 = _Ride(self.gs, self.half_t, plan)
        _round(self.r1, carrier, f"reduce_{self.tag}_pair")

    def chips(self, carrier=None):
        def plan(x, y, c):
            k = 2 * x + y
            return [((2 * px + py,), (k,), (2 * px + py,), (px, py, c)) for px, py in _other_chips(x, y)]

        self.pairs = [_add_pairs(lax.dynamic_index_in_dim(g, self.ci, 1, keepdims=False), f,
                                 name=f"reduce_{self.tag}_pair_add{n}")
                      for n, (g, f) in enumerate(zip(self.gs, self.r1.out))]
        self.r2 = _Ride(self.pairs, self.half_t, plan)
        _round(self.r2, carrier, f"reduce_{self.tag}_chips")

    def halves(self, carrier=None):
        plan = lambda x, y, c: [((), (c,), (1 - c,), (x, y, 1 - c))]
        self.mine = []
        for n, (l, p) in enumerate(zip(self.r2.out, self.pairs)):
            own = lax.dynamic_index_in_dim(p, self.chip, 0, keepdims=False)
            self.mine.append(_sum_chips(lax.dynamic_update_index_in_dim(l, own, self.chip, 0),
                                        name=f"reduce_{self.tag}_chip_sum{n}"))
        self.r3 = _Ride(self.mine, [jax.ShapeDtypeStruct((2,) + m.shape, F32) for m in self.mine], plan)
        _round(self.r3, carrier, f"reduce_{self.tag}_halves")

    def result(self):
        return [lax.dynamic_update_index_in_dim(b, m, self.ci, 0).reshape(2 * m.shape[0], m.shape[1])
                for b, m in zip(self.r3.out, self.mine)]


def _ada_fwd(c_all, w_sh, b_sh, *, name):
    nb, d = c_all.shape
    wcols = w_sh.shape[1]
    tn = _pick(wcols, 512)

    def body(c_ref, w_ref, b_ref, o_ref, a_ref):
        cv = c_ref[...]
        act = cv * jax.nn.sigmoid(cv)
        a_ref[...] = act
        o_ref[...] = jnp.dot(act.astype(BF16), w_ref[...].astype(BF16), preferred_element_type=F32) + b_ref[...]

    return _pcall(
        body, name=name, grid=(wcols // tn,),
        in_specs=[pl.BlockSpec((nb, d), lambda j: (0, 0)), pl.BlockSpec((d, tn), lambda j: (0, j)),
                  pl.BlockSpec((1, tn), lambda j: (0, j))],
        out_specs=[pl.BlockSpec((nb, tn), lambda j: (0, j)), pl.BlockSpec((nb, d), lambda j: (0, 0))],
        out_shape=[jax.ShapeDtypeStruct((nb, wcols), F32), jax.ShapeDtypeStruct((nb, d), F32)],
        compiler_params=_params(("arbitrary",)))(c_all, w_sh, b_sh)


def _sum_devices(g, *, name):
    nd, m, w = g.shape

    def body(g_ref, o_ref):
        acc = g_ref[0]
        for k in range(1, nd):
            acc = acc + g_ref[k]
        o_ref[...] = acc

    return _pcall(body, name=name, out_shape=jax.ShapeDtypeStruct((m, w), F32),
                  compiler_params=pltpu.CompilerParams(vmem_limit_bytes=VMEM_LIMIT_BYTES))(g)


def _adamw(w, g, m, v, *, name):
    rows, cols = w.shape[-2:]
    tr = _pick(rows, max(8, (1 << 18) // cols), 8)
    c1 = 1.0 / (1.0 - ADAM_B1 ** ADAM_STEP)
    c2 = 1.0 / (1.0 - ADAM_B2 ** ADAM_STEP)

    def body(w_ref, g_ref, m_ref, v_ref, d_ref, nm_ref, nv_ref):
        gv = g_ref[...]
        nm = ADAM_B1 * m_ref[...] + (1.0 - ADAM_B1) * gv
        nv = ADAM_B2 * v_ref[...] + (1.0 - ADAM_B2) * (gv * gv)
        d_ref[...] = -ADAM_LR * ((nm * c1) / (jnp.sqrt(nv * c2) + ADAM_EPS) + ADAM_WD * w_ref[...])
        nm_ref[...] = nm
        nv_ref[...] = nv

    gspec = pl.BlockSpec((tr, cols), lambda i: (i, 0))
    spec = pl.BlockSpec((None, tr, cols), lambda i: (0, i, 0)) if w.ndim == 3 else gspec
    shp = jax.ShapeDtypeStruct(w.shape, F32)
    return _pcall(body, name=name, grid=(rows // tr,), in_specs=[spec, gspec, spec, spec], out_specs=[spec] * 3,
                  out_shape=[shp] * 3, compiler_params=_params(("parallel",)))(w, g, m, v)


def _permute_in_cols(w):
    ngrp = len(B_PATTERNS)
    qb, kb, vb = (w[:, A_W + n * QB_W:A_W + (n + 1) * QB_W] for n in range(3))
    parts = [w[:, :A_W], jnp.zeros((w.shape[0], VAR_W - A_W), w.dtype)]
    for g in range(ngrp):
        parts += [t[:, g * GB_W:(g + 1) * GB_W] for t in (qb, kb, vb)]
    return jnp.concatenate(parts + [w[:, A_W + 3 * QB_W:]], axis=1)


def _unpermute_in_grads(pieces):
    ga, groups, gg = pieces[0], pieces[1:-1], pieces[-1]
    cols = [ga[:, :A_W]]
    for n in range(3):
        cols += [gp[:, n * GB_W:(n + 1) * GB_W] for gp in groups]
    return jnp.concatenate(cols + [gg], axis=1)


def kernel(x, c, positions, w_ada, b_ada, w_in, sinks, w_branch_a, w_branch_b, w_o, ln1_g, ln1_b, w_gate_up, w_down, ln2_g, ln2_b, loss_target, m_w_ada, m_b_ada, m_w_in, m_sinks, m_w_branch_a, m_w_branch_b, m_w_o, m_ln1_g, m_ln1_b, m_w_gate_up, m_w_down, m_ln2_g, m_ln2_b, v_w_ada, v_b_ada, v_w_in, v_sinks, v_w_branch_a, v_w_branch_b, v_w_o, v_ln1_g, v_ln1_b, v_w_gate_up, v_w_down, v_ln2_g, v_ln2_b):
    xi, yi, ci = _coords()
    chip = 2 * xi + yi
    dev = 4 * xi + 2 * yi + ci
    NB, T, D = x.shape
    nchip, ndev = 4, 8
    ada_cols = w_ada.shape[2]

    c_blk = jnp.zeros((8, D), F32).at[:NB].set(c)
    c_all = _allgather_small(c_blk, name="gather_c").reshape(ndev, 8, D)[:, :NB].reshape(ndev * NB, D)
    b_sh = lax.dynamic_slice(b_ada, (0, chip * ada_cols), (1, ada_cols))
    mod_part, c_act = _ada_fwd(c_all, w_ada[0], b_sh, name="ada_fwd")
    mod_g = _allgather_small(mod_part, name="gather_mod").reshape(nchip, 2, ndev * NB, ada_cols)[:, 0]
    mod_all = jnp.transpose(mod_g, (1, 0, 2)).reshape(ndev * NB, nchip * ada_cols)
    mod = lax.dynamic_slice(mod_all, (NB * dev, 0), (NB, nchip * ada_cols))

    ra, ro, rd = w_branch_a.shape[1], w_o.shape[1], w_down.shape[1]
    rowsh = jnp.concatenate([w_branch_a[0], w_o[0], w_down[0]], axis=0)
    halves = lambda a: a.reshape(a.shape[:-2] + (2, a.shape[-2] // 2, a.shape[-1]))
    whole = lambda a: a.reshape(a.shape[:-3] + (2 * a.shape[-2], a.shape[-1]))
    shards = [halves(w.astype(BF16)) for w in (w_in[0], rowsh, w_branch_b[0], w_gate_up[0])]
    (g_in,) = _Gather(shards[:1], chip, "w_in").result()
    w_in_f = _permute_in_cols(jnp.concatenate([g_in[k] for k in range(nchip)], axis=1))
    mix = _Gather(shards[1:3], chip, "w_mix", carriers=("inproj_qkv", "attn_a_fwd"))
    ffn = _Gather(shards[3:], chip, "w_ffn", carriers=("attn_a_fwd", "attn_b0_fwd"))

    def rest_weights():
        (g_rows, w_b_f), (w_gu_f,) = mix.result(), ffn.result()
        return (g_rows[:, :ra].reshape(nchip * ra, D), w_b_f, g_rows[:, ra:ra + ro].reshape(nchip * ro, D), w_gu_f,
                g_rows[:, ra + ro:].reshape(nchip * rd, D))

    red = {}

    def hook(event, **g):
        if event == "rest_grads":
            gr_rows = jnp.concatenate([g["g_wa"].reshape(nchip, ra, D), g["g_wo"].reshape(nchip, ro, D),
                                       g["g_wd"].reshape(nchip, rd, D)], axis=1)
            red["rest"] = _ReduceScatter([halves(a) for a in (gr_rows, g["g_wb"], g["g_wgu"])], chip, ci, "rest")
            red["rest"].pair(carrier="merge_bwd")
        elif event == "merge_bwd_done":
            red["rest"].chips(carrier="attn_a_bwd")
        elif event == "attn_a_bwd_done":
            red["rest"].halves(carrier="attn_b0_bwd")
        elif event == "win_grads":
            gr_in = jnp.stack(jnp.split(_unpermute_in_grads(g["g_win"]), nchip, axis=1))
            red["w_in"] = _ReduceScatter([halves(gr_in)], chip, ci, "w_in")
            red["w_in"].pair(carrier="inproj_dx0")
        elif event == "inproj_dx0_done":
            red["w_in"].chips(carrier="inproj_dx1")
        elif event == "inproj_dx1_done":
            red["w_in"].halves(carrier="x_bwd")

    res = _local_step(x, mod, positions, w_in_f, rest_weights, sinks[0], ln1_g, ln1_b, ln2_g, ln2_b, loss_target, hook)
    (g_w_in,) = red["w_in"].result()
    g_rows_red, g_w_b, g_w_gu = red["rest"].result()
    g_w_a, g_w_o, g_w_d = g_rows_red[:ra], g_rows_red[ra:ra + ro], g_rows_red[ra + ro:]

    small_rows = 24
    misc = jnp.zeros((1, D), F32).at[0, :A_Q_HEADS].set(res["dsink"]).at[0, A_Q_HEADS].set(jnp.sum(res["loss"]))
    small = jnp.concatenate([res["dmod"].reshape(NB * 6, D), jnp.sum(res["ln_grads"], axis=0), misc,
                             jnp.zeros((small_rows - NB * 6 - 5, D), F32)], axis=0)
    small_all = _allgather_small(small, name="gather_small").reshape(ndev, small_rows, D)
    dmod_all = small_all[:, :NB * 6].reshape(ndev * NB, 6 * D)
    sums = _sum_devices(small_all, name="sum_small")
    g_b_ada = (sums[0:6] + sums[6:12]).reshape(1, 6 * D)
    g_ln1_g, g_ln1_b, g_ln2_g, g_ln2_b = (sums[12 + n][None] for n in range(4))
    g_sinks = sums[16, :A_Q_HEADS][None]
    loss = sums[16, A_Q_HEADS]
    dmod_sh = lax.dynamic_slice(dmod_all, (0, chip * ada_cols), (ndev * NB, ada_cols))
    g_w_ada = _mm(c_act, dmod_sh, ta=True, name="ada_dw")

    names = ["w_ada", "b_ada", "w_in", "sinks", "w_branch_a", "w_branch_b", "w_o", "ln1_g", "ln1_b",
             "w_gate_up", "w_down", "ln2_g", "ln2_b"]
    ws = [w_ada, b_ada, w_in, sinks, w_branch_a, w_branch_b, w_o, ln1_g, ln1_b, w_gate_up, w_down, ln2_g, ln2_b]
    ms = [m_w_ada, m_b_ada, m_w_in, m_sinks, m_w_branch_a, m_w_branch_b, m_w_o, m_ln1_g, m_ln1_b, m_w_gate_up,
          m_w_down, m_ln2_g, m_ln2_b]
    vs = [v_w_ada, v_b_ada, v_w_in, v_sinks, v_w_branch_a, v_w_branch_b, v_w_o, v_ln1_g, v_ln1_b, v_w_gate_up,
          v_w_down, v_ln2_g, v_ln2_b]
    gs = [g_w_ada, g_b_ada, g_w_in, g_sinks, g_w_a, g_w_b, g_w_o, g_ln1_g, g_ln1_b, g_w_gu, g_w_d, g_ln2_g, g_ln2_b]
    grads, deltas, new_ms, new_vs = [], [], [], []
    for name, w, g, m, v in zip(names, ws, gs, ms, vs):
        g2 = g.reshape(w.shape[-2:])
        d, nm, nv = _adamw(w, g2, m, v, name="adamw_" + name)
        grads.append(g2.reshape(w.shape))
        deltas.append(d)
        new_ms.append(nm)
        new_vs.append(nv)
    return (loss, res["grad_x"], *grads, *deltas, *new_ms, *new_vs)
```

```python
import functools

import jax
import jax.numpy as jnp
from jax import lax
from jax.experimental import pallas as pl
from jax.experimental.pallas import tpu as pltpu

F32 = jnp.float32
BF16 = jnp.bfloat16
MESH = pl.DeviceIdType.MESH

HEAD_DIM = 64
PAIR_W = 2 * HEAD_DIM
BLOCK = 128
A_Q_HEADS = 16
A_KV_HEADS = 2
A_WINDOW = 128
B_PATTERNS = ((128, 1), (512, 4), (2048, 16))
B_GROUP_HEADS = 8
QA_W = A_Q_HEADS * HEAD_DIM
KA_W = A_KV_HEADS * HEAD_DIM
GB_W = B_GROUP_HEADS * HEAD_DIM
QB_W = GB_W * len(B_PATTERNS)
A_W = QA_W + 2 * KA_W
VAR_W = 3 * GB_W
N_VAR = 1 + len(B_PATTERNS)
QKV_P = N_VAR * VAR_W
ROPE_THETA = 10000.0
LN_EPS = 1e-5
NEG_INF = -1e30
DEPTH = 1
ALPHA = (2 * DEPTH) ** 0.25
SCALE = HEAD_DIM ** -0.5

ADAM_LR, ADAM_B1, ADAM_B2, ADAM_EPS, ADAM_WD, ADAM_STEP = 0.001, 0.9, 0.999, 1e-08, 0.01, 10

VMEM_LIMIT_BYTES = 56 * 1024 * 1024
MM_TILE_BYTES = 36 * 1024 * 1024
MM_WHOLE_K = 4096


def _params(sem=None):
    return pltpu.CompilerParams(dimension_semantics=sem, vmem_limit_bytes=VMEM_LIMIT_BYTES)


_RIDES = {}


def _pcall(body, *, name, **kw):
    rides = _RIDES.pop(name, None)
    if rides is None:
        return pl.pallas_call(body, name=name, **kw)
    return _riding_call(body, rides, name=name, **kw)


def _copies(src_refs, dst_refs, send_sems, recv_sems, plan):
    x, y, c = lax.axis_index("x"), lax.axis_index("y"), lax.axis_index("c")
    remote = plan(x, y, c)
    nrem = len(remote)
    at = lambda ref, idx: ref.at[idx] if idx else ref

    def copy(a, n, landing):
        si, di, ri, peer = remote[n]
        return pltpu.make_async_remote_copy(
            src_ref=at(src_refs[a], si), dst_ref=at(dst_refs[a], ri if landing else di),
            send_sem=send_sems.at[a * nrem + n], recv_sem=recv_sems.at[a * nrem + n],
            device_id=peer, device_id_type=MESH)

    order = [(a, n) for a in range(len(dst_refs)) for n in range(nrem)]

    def start():
        for a, n in order:
            copy(a, n, False).start()

    def wait():
        for a, n in order:
            copy(a, n, True).wait_recv()
        for a, n in order:
            copy(a, n, False).wait_send()

    return start, wait


class _Ride:
    def __init__(self, srcs, dsts, plan, dst_inits=None):
        self.srcs, self.dsts, self.plan, self.dst_inits, self.out = srcs, dsts, plan, dst_inits, None


def _riding_call(body, rides, *, name, grid, in_specs, out_specs, out_shape, scratch_shapes=(), **kw):
    single = not isinstance(out_specs, (list, tuple))
    out_specs = [out_specs] if single else list(out_specs)
    out_shape = [out_shape] if single else list(out_shape)
    n_in, n_out, n_scr = len(in_specs), len(out_specs), len(scratch_shapes)
    xin, xdsts, sems, aliases, layout = [], [], [], {}, []
    for ride in rides:
        srcs = ride.srcs() if callable(ride.srcs) else ride.srcs
        inits = ride.dst_inits() if callable(ride.dst_inits) else ride.dst_inits
        na, nrem = len(ride.dsts), len(ride.plan(0, 0, 0))
        src_at = len(xin) if srcs is not None else None
        xin += list(srcs) if srcs is not None else []
        if inits is not None:
            aliases.update({n_in + len(xin) + a: n_out + len(xdsts) + a for a in range(na)})
            xin += list(inits)
        layout.append((src_at, len(xdsts), na))
        xdsts += list(ride.dsts)
        sems += [pltpu.SemaphoreType.DMA((na * nrem,)), pltpu.SemaphoreType.DMA((na * nrem,))]

    def wrapped(*refs):
        ins, xins = refs[:n_in], refs[n_in:n_in + len(xin)]
        outs = refs[n_in + len(xin):n_in + len(xin) + n_out]
        xouts = refs[n_in + len(xin) + n_out:n_in + len(xin) + n_out + len(xdsts)]
        scr = refs[n_in + len(xin) + n_out + len(xdsts):]
        rounds = []
        for k, (ride, (src_at, dst_at, na)) in enumerate(zip(rides, layout)):
            dsts = xouts[dst_at:dst_at + na]
            srcs = dsts if src_at is None else xins[src_at:src_at + na]
            rounds.append(_copies(srcs, dsts, scr[n_scr + 2 * k], scr[n_scr + 2 * k + 1], ride.plan))
        ids = [pl.program_id(a) for a in range(len(grid))]
        first = functools.reduce(jnp.logical_and, [i == 0 for i in ids])
        last = functools.reduce(jnp.logical_and, [i == g - 1 for i, g in zip(ids, grid)])

        @pl.when(first)
        def _():
            for start, _ in rounds:
                start()

        body(*ins, *outs, *scr[:n_scr])

        @pl.when(last)
        def _():
            for _, wait in rounds:
                wait()

    hbm = pl.BlockSpec(memory_space=pl.ANY)

    def run(*args):
        res = pl.pallas_call(
            wrapped, name=name, grid=grid, in_specs=list(in_specs) + [hbm] * len(xin),
            out_specs=out_specs + [hbm] * len(xdsts), out_shape=out_shape + xdsts,
            scratch_shapes=list(scratch_shapes) + sems, input_output_aliases=aliases,
            compiler_params=_params(("arbitrary",) * len(grid)),
        )(*args, *xin)
        for ride, (_, dst_at, na) in zip(rides, layout):
            ride.out = list(res[n_out + dst_at:n_out + dst_at + na])
        return res[0] if single else list(res[:n_out])

    return run


def _pick(n, target, quantum=128):
    t = (min(target, n) // quantum) * quantum
    while t >= quantum:
        if n % t == 0:
            return t
        t -= quantum
    return n


def _mm(a, b, *, name, ta=False, tb=False, b3=False, out3=0, out_dtype=F32, add=None, tm=1024, tn=1536, tk=1536):
    if ta:
        K, M = a.shape
    else:
        M, K = a.shape
    if b3 and tb:
        Nn, K2, tk = b.shape[1], b.shape[0] * b.shape[2], b.shape[2]
    elif b3:
        K2, Nn, tn = b.shape[1], b.shape[0] * b.shape[2], b.shape[2]
    elif tb:
        Nn, K2 = b.shape
    else:
        K2, Nn = b.shape
    assert K == K2, (a.shape, b.shape)
    if out3:
        tn = Nn // out3
    tm, tn, tk = _pick(M, tm), _pick(Nn, tn), _pick(K, tk)
    if not (b3 and tb) and K <= MM_WHOLE_K:
        tk = K
        fits = lambda: 4 * tk * (tm + tn) + 8 * tm * tn * (2 if add is not None else 1) <= MM_TILE_BYTES
        while not fits():
            if (tm >= tn or b3 or out3) and tm > 256:
                tm = _pick(M, tm - 128)
            elif not (b3 or out3) and tn > 256:
                tn = _pick(Nn, tn - 128)
            else:
                break
    nk = K // tk
    j_outer = K * Nn + (Nn // tn) * M * K < M * K + (M // tm) * K * Nn
    dn = (((0 if ta else 1,), (1 if tb else 0,)), ((), ()))

    def body(*refs):
        refs = list(refs)
        a_ref, b_ref = refs[:2]
        add_ref = refs[2] if add is not None else None
        o_ref = refs[3] if add is not None else refs[2]
        part = lax.dot_general(a_ref[...].astype(BF16), b_ref[...].astype(BF16), dn, preferred_element_type=F32)

        def finish(r):
            if add is not None:
                r = r + add_ref[...]
            o_ref[...] = r.astype(out_dtype)

        if nk == 1:
            finish(part)
            return
        acc = refs[-1]
        k = pl.program_id(2)

        @pl.when(k == 0)
        def _():
            acc[...] = part

        @pl.when(k > 0)
        def _():
            acc[...] += part

        @pl.when(k == nk - 1)
        def _():
            finish(acc[...])

    def spec(shape, index):
        return pl.BlockSpec(shape, (lambda j, i, k: index(i, j, k)) if j_outer else index)

    a_spec = spec((tk, tm), lambda i, j, k: (k, i)) if ta else spec((tm, tk), lambda i, j, k: (i, k))
    if b3 and tb:
        b_spec = spec((None, tn, tk), lambda i, j, k: (k, j, 0))
    elif b3:
        b_spec = spec((None, tk, tn), lambda i, j, k: (j, k, 0))
    elif tb:
        b_spec = spec((tn, tk), lambda i, j, k: (j, k))
    else:
        b_spec = spec((tk, tn), lambda i, j, k: (k, j))
    if out3:
        o_spec = spec((None, tm, tn), lambda i, j, k: (j, i, 0))
    else:
        o_spec = spec((tm, tn), lambda i, j, k: (i, j))
    ins, specs = [a, b], [a_spec, b_spec]
    if add is not None:
        ins.append(add)
        specs.append(o_spec)
    grid = (Nn // tn, M // tm, nk) if j_outer else (M // tm, Nn // tn, nk)
    return _pcall(
        body, name=name, grid=grid, in_specs=specs, out_specs=o_spec,
        out_shape=jax.ShapeDtypeStruct((out3, M, tn) if out3 else (M, Nn), out_dtype),
        scratch_shapes=[pltpu.VMEM((tm, tn), F32)] if nk > 1 else [],
        compiler_params=_params(("parallel", "parallel", "arbitrary")),
    )(*ins)


def _mm_multi(a_list, b_list, *, name, add=None, out_dtype=F32, tm=512):
    M = a_list[0].shape[0]
    tm = _pick(M, tm)
    ns = len(a_list)
    b_arrs, b_specs = [], []
    for b in b_list:
        arr, shp, idx = b if isinstance(b, tuple) else (b, b.shape, (0, 0))
        b_arrs.append(arr)
        b_specs.append(pl.BlockSpec(shp, lambda i, idx=idx: idx))
    Nn = b_specs[0].block_shape[0]
    dn = (((1,), (1,)), ((), ()))

    def body(*refs):
        a_refs, b_refs = refs[:ns], refs[ns:2 * ns]
        acc = None
        for a_ref, b_ref in zip(a_refs, b_refs):
            part = lax.dot_general(a_ref[...].astype(BF16), b_ref[...], dn, preferred_element_type=F32)
            acc = part if acc is None else acc + part
        if add is not None:
            acc = acc + refs[2 * ns][...]
        refs[-1][...] = acc.astype(out_dtype)

    o_spec = pl.BlockSpec((tm, Nn), lambda i: (i, 0))
    specs = [pl.BlockSpec((tm, a.shape[1]), lambda i: (i, 0)) for a in a_list] + b_specs
    ins = list(a_list) + b_arrs
    if add is not None:
        specs.append(o_spec)
        ins.append(add)
    return _pcall(body, name=name, grid=(M // tm,), in_specs=specs, out_specs=o_spec,
                  out_shape=jax.ShapeDtypeStruct((M, Nn), out_dtype), compiler_params=_params(("parallel",)))(*ins)


def _lane(shape):
    return lax.broadcasted_iota(jnp.int32, shape, len(shape) - 1)


def _rot_half(v):
    w = v.shape[-1]
    first = (_lane(v.shape) % HEAD_DIM) < (HEAD_DIM // 2)
    return jnp.where(first, pltpu.roll(v, w - HEAD_DIM // 2, v.ndim - 1), pltpu.roll(v, HEAD_DIM // 2, v.ndim - 1))


def _widen(t, w):
    return t if w == t.shape[-1] else jnp.concatenate([t] * (w // t.shape[-1]), axis=-1)


def _unrope(v, cos, sins):
    w = v.shape[-1]
    return v * _widen(cos, w) - _rot_half(v) * _widen(sins, w)


def _rope_tables(positions):
    half = HEAD_DIM // 2
    inv = ROPE_THETA ** (-jnp.arange(half, dtype=F32) / half)
    ang = positions.astype(F32)[..., None] * inv
    cos, sin = jnp.cos(ang), jnp.sin(ang)
    cosf = jnp.concatenate([cos, cos, cos, cos], axis=-1)
    sins = jnp.concatenate([-sin, sin, -sin, sin], axis=-1)
    n = positions.shape[0] * positions.shape[1]
    return cosf.reshape(n, PAIR_W), sins.reshape(n, PAIR_W)


def _inproj(x2, scale, shift, w, cosf, sins, flags, *, T, name):
    N, D = x2.shape
    tm, tn = _pick(T, 512), VAR_W
    tpe = T // tm

    def body(x_ref, sc_ref, sh_ref, w_ref, c_ref, s_ref, f_ref, *outs):
        o_refs, u_ref = outs[:N_VAR], outs[N_VAR]
        j = pl.program_id(1)

        @pl.when(j == 0)
        def _():
            u_ref[...] = (x_ref[...] * (1.0 + sc_ref[0]) + sh_ref[0]).astype(BF16)

        acc = jnp.dot(u_ref[...], w_ref[...], preferred_element_type=F32)
        fl = f_ref[...]
        ce = 1.0 + (_widen(c_ref[...], tn) - 1.0) * fl
        se = _widen(s_ref[...], tn) * fl
        res = (acc * ce + _rot_half(acc) * se).astype(BF16)
        for v in range(N_VAR):
            @pl.when(j == v)
            def _(v=v):
                o_refs[v][...] = res

    ex = pl.BlockSpec((1, 1, D), lambda i, j: (i // tpe, 0, 0))
    tab = pl.BlockSpec((tm, PAIR_W), lambda i, j: (i, 0))
    keep = lambda w_: pl.BlockSpec((tm, w_), lambda i, j: (i, 0))
    return _pcall(
        body, name=name, grid=(N // tm, N_VAR),
        in_specs=[keep(D), ex, ex, pl.BlockSpec((D, tn), lambda i, j: (0, j)), tab, tab,
                  pl.BlockSpec((1, tn), lambda i, j: (0, j))],
        out_specs=[keep(tn)] * N_VAR + [keep(D)],
        out_shape=[jax.ShapeDtypeStruct((N, tn), BF16)] * N_VAR + [jax.ShapeDtypeStruct((N, D), BF16)],
        compiler_params=_params(("parallel", "arbitrary")),
    )(x2, scale, shift, w, cosf, sins, flags)


class _Geom:
    def __init__(self, g):
        if g is None:
            self.r, self.nq, self.n_back, self.sink = 1, A_Q_HEADS, A_WINDOW - 1, True
            self.qw, self.kw = QA_W, KA_W
            self.qidx = lambda j: 0
            self.kidx = lambda j: QA_W // KA_W
            self.vidx = lambda j: QA_W // KA_W + 1
        else:
            window, r = B_PATTERNS[g]
            self.r, self.nq, self.n_back, self.sink = r, B_GROUP_HEADS, window // r, False
            self.qw, self.kw = GB_W, GB_W
            self.qidx = lambda j: 3 * j
            self.kidx = lambda j: 3 * j + 1
            self.vidx = lambda j: 3 * j + 2
        self.ntile = self.qw // PAIR_W


def _stack_heads(t, scale=None):
    first = _lane(t.shape) < HEAD_DIM
    z = jnp.zeros_like(t)
    if scale is not None:
        t = t * jnp.asarray(scale, t.dtype)
    return jnp.concatenate([jnp.where(first, t, z), jnp.where(first, z, t)], axis=0)


def _lse_col(t):
    return jnp.concatenate([t[:, 0:1], t[:, HEAD_DIM:HEAD_DIM + 1]], axis=0)


def _unstack_heads(v2):
    return jnp.where(_lane((BLOCK, PAIR_W)) < HEAD_DIM, v2[:BLOCK], v2[BLOCK:])


def _dup_head(t, kh):
    tf = t.astype(F32)
    keep = (_lane(t.shape) < HEAD_DIM) if kh == 0 else (_lane(t.shape) >= HEAD_DIM)
    return jnp.where(keep, tf, pltpu.roll(tf, HEAD_DIM, 1)).astype(t.dtype)


def _fold_heads(t):
    return t + pltpu.roll(t, HEAD_DIM, 1)


def _band_mask(rows, i, n_back, single):
    nkeys = BLOCK if single else 2 * BLOCK
    qi = jnp.bitwise_and(lax.broadcasted_iota(jnp.int32, (rows, nkeys), 0), BLOCK - 1)
    ki = lax.broadcasted_iota(jnp.int32, (rows, nkeys), 1)
    if single:
        return qi >= ki
    dist = qi + BLOCK - ki
    return jnp.logical_and(jnp.logical_and(dist >= 0, dist <= n_back), jnp.logical_or(ki >= BLOCK, i > 0))


def _softmax_parts(s, valid, sinkcol):
    s = jnp.where(valid, s, NEG_INF)
    m = jnp.max(s, axis=1, keepdims=True)
    if sinkcol is not None:
        m = jnp.maximum(m, sinkcol)
    p = jnp.exp(s - m)
    den = jnp.sum(p, axis=1, keepdims=True)
    es = None
    if sinkcol is not None:
        es = jnp.exp(sinkcol - m)
        den = den + es
    return p, m, den, es


_NT = (((1,), (1,)), ((), ()))
_TN = (((0,), (0,)), ((), ()))


def _rows2(prev_ref, cur_ref, cs, single=False):
    if single:
        return cur_ref[0, :, cs]
    return jnp.concatenate([prev_ref[0, :, cs], cur_ref[0, :, cs]], axis=0)


def _sink_col(sink_ref, kh, nblocks):
    return jnp.concatenate([jnp.full((BLOCK, 1), sink_ref[kh * nblocks + b], F32) for b in range(nblocks)], axis=0)


def _tile(t):
    return slice(t * PAIR_W, (t + 1) * PAIR_W)


def _attn_fwd(qkv, sinks, g, *, NB, T, name):
    geo = _Geom(g)
    r, qw, kw, ntile = geo.r, geo.qw, geo.kw, geo.ntile
    tsub = T // r
    nblk = tsub // BLOCK
    qkv3 = qkv.reshape(NB, tsub, r * VAR_W)
    out_dtype = BF16 if g is None else F32
    tiles_per_kv = ntile // A_KV_HEADS

    single = nblk == 1

    def body(q_ref, kp_ref, kc_ref, vp_ref, vc_ref, sink_ref, o_ref, l_ref):
        i = pl.program_id(2)
        if geo.sink:
            kall, vall = _rows2(kp_ref, kc_ref, _tile(0)), _rows2(vp_ref, vc_ref, _tile(0))
            tiles = [[kh * tiles_per_kv + t for t in range(tiles_per_kv)] for kh in range(A_KV_HEADS)]
            q2s = [jnp.concatenate([_stack_heads(q_ref[0, :, _tile(t)], SCALE) for t in ts], axis=0) for ts in tiles]
            kks = [_dup_head(kall, kh) for kh in range(A_KV_HEADS)]
            vvs = [_dup_head(vall, kh) for kh in range(A_KV_HEADS)]
            sinkcols = [_sink_col(sink_ref, kh, 2 * tiles_per_kv) for kh in range(A_KV_HEADS)]
        else:
            tiles = [[t] for t in range(ntile)]
            q2s = [_stack_heads(q_ref[0, :, _tile(t)], SCALE) for t in range(ntile)]
            kks = [_rows2(kp_ref, kc_ref, _tile(t), single) for t in range(ntile)]
            vvs = [_rows2(vp_ref, vc_ref, _tile(t), single) for t in range(ntile)]
            sinkcols = [None] * ntile
        valid = _band_mask(q2s[0].shape[0], i, geo.n_back, single)
        ss = [lax.dot_general(q2, kk, _NT, preferred_element_type=F32) for q2, kk in zip(q2s, kks)]
        parts = [_softmax_parts(s, valid, sc) for s, sc in zip(ss, sinkcols)]
        o2s = [jnp.dot(p.astype(BF16), vv, preferred_element_type=F32) / den for (p, m, den, _), vv in zip(parts, vvs)]
        for ts, o2, (p, m, den, _) in zip(tiles, o2s, parts):
            lse2 = jnp.broadcast_to(m + jnp.log(den), (o2.shape[0], PAIR_W))
            for n, t in enumerate(ts):
                rows = slice(2 * BLOCK * n, 2 * BLOCK * (n + 1))
                o_ref[0, :, _tile(t)] = _unstack_heads(o2[rows]).astype(out_dtype)
                l_ref[0, :, _tile(t)] = _unstack_heads(lse2[rows])

    prev = lambda i: jnp.maximum(i - 1, 0)
    in_specs = [
        pl.BlockSpec((1, BLOCK, qw), lambda b, j, i: (b, i, geo.qidx(j))),
        pl.BlockSpec((1, BLOCK, kw), lambda b, j, i: (b, prev(i), geo.kidx(j))),
        pl.BlockSpec((1, BLOCK, kw), lambda b, j, i: (b, i, geo.kidx(j))),
        pl.BlockSpec((1, BLOCK, kw), lambda b, j, i: (b, prev(i), geo.vidx(j))),
        pl.BlockSpec((1, BLOCK, kw), lambda b, j, i: (b, i, geo.vidx(j))),
        pl.BlockSpec(memory_space=pltpu.SMEM),
    ]
    o_spec = pl.BlockSpec((1, BLOCK, qw), lambda b, j, i: (b, i, j))
    shape = (NB, tsub, r * qw)
    o, lse = _pcall(
        body, name=name, grid=(NB, r, nblk), in_specs=in_specs, out_specs=[o_spec, o_spec],
        out_shape=[jax.ShapeDtypeStruct(shape, out_dtype), jax.ShapeDtypeStruct(shape, F32)],
        compiler_params=_params(("parallel", "parallel", "arbitrary")),
    )(qkv3, qkv3, qkv3, qkv3, qkv3, sinks)
    return o.reshape(NB * T, qw), lse.reshape(NB * T, qw), lse


def _attn_bwd(qkv, do, lse, dlse, cosf, sins, sinks, g, *, NB, T, name):
    geo = _Geom(g)
    r, qw, kw, ntile = geo.r, geo.qw, geo.kw, geo.ntile
    tsub = T // r
    nblk = tsub // BLOCK
    view = lambda a, w: a.reshape(NB, tsub, r * w)
    has_dlse = dlse is not None
    tiles_per_kv = ntile // A_KV_HEADS

    single = nblk == 1
    krows = BLOCK if single else 2 * BLOCK
    nsteps = 1 if single else nblk + 1

    def grads(q2s, kks, vvs, do2s, i, lsecols, sinkcols, dlcols):
        valid = _band_mask(q2s[0].shape[0], i, geo.n_back, single)
        ss = [lax.dot_general(q2, kk, _NT, preferred_element_type=F32) for q2, kk in zip(q2s, kks)]
        dps = [lax.dot_general(do2, vv, _NT, preferred_element_type=F32) for do2, vv in zip(do2s, vvs)]
        ps, dss, sks = [], [], []
        for s, dp, ls, sc, dl in zip(ss, dps, lsecols, sinkcols, dlcols):
            p = jnp.exp(jnp.where(valid, s, NEG_INF) - ls)
            delta = jnp.sum(p * dp, axis=1, keepdims=True)
            sks.append(jnp.exp(sc - ls) * delta if sc is not None else None)
            if dl is not None:
                delta = delta - dl
            ps.append(p.astype(BF16))
            dss.append((p * (dp - delta)).astype(BF16))
        dq2s = [jnp.dot(ds, kk, preferred_element_type=F32) * SCALE for ds, kk in zip(dss, kks)]
        dkks = [lax.dot_general(ds, q2, _TN, preferred_element_type=F32) for ds, q2 in zip(dss, q2s)]
        dvvs = [lax.dot_general(p, do2, _TN, preferred_element_type=F32) for p, do2 in zip(ps, do2s)]
        return dq2s, dkks, dvvs, sks

    def body(*refs):
        it = iter(refs)
        q_ref, kp_ref, kc_ref, vp_ref, vc_ref, do_ref, l_ref = (next(it) for _ in range(7))
        dl_ref = next(it) if has_dlse else None
        c_ref, s_ref, sink_ref, o_ref, ds_ref, dq_s, dk_s, dv_s, car_q, car_k, car_v = (next(it) for _ in range(11))
        b, j, i = pl.program_id(0), pl.program_id(1), pl.program_id(2)

        @pl.when(jnp.logical_and(b == 0, jnp.logical_and(j == 0, i == 0)))
        def _():
            ds_ref[...] = jnp.zeros_like(ds_ref)

        def compute():
            if geo.sink:
                kall, vall = _rows2(kp_ref, kc_ref, _tile(0)), _rows2(vp_ref, vc_ref, _tile(0))
                nb = 2 * tiles_per_kv
                tiles = [[kh * tiles_per_kv + t for t in range(tiles_per_kv)] for kh in range(A_KV_HEADS)]
                cat = lambda f, ts: jnp.concatenate([f(t) for t in ts], axis=0)
                dq2s, dkks, dvvs, sks = grads(
                    [cat(lambda t: _stack_heads(q_ref[0, :, _tile(t)], SCALE), ts) for ts in tiles],
                    [_dup_head(kall, kh) for kh in range(A_KV_HEADS)],
                    [_dup_head(vall, kh) for kh in range(A_KV_HEADS)],
                    [cat(lambda t: _stack_heads(do_ref[0, :, _tile(t)]), ts) for ts in tiles], i,
                    [cat(lambda t: _lse_col(l_ref[0, :, _tile(t)]), ts) for ts in tiles],
                    [_sink_col(sink_ref, kh, nb) for kh in range(A_KV_HEADS)], [None] * A_KV_HEADS)
                lane1 = _lane((1, PAIR_W))
                dsink = jnp.zeros((1, PAIR_W), F32)
                for kh, (ts, dq2, sk) in enumerate(zip(tiles, dq2s, sks)):
                    for n, t in enumerate(ts):
                        dq_s[:, _tile(t)] = _unstack_heads(dq2[2 * BLOCK * n:2 * BLOCK * (n + 1)])
                    for bb in range(nb):
                        dsink = dsink + jnp.where(lane1 == kh * nb + bb, -jnp.sum(sk[BLOCK * bb:BLOCK * (bb + 1)]), 0.0)
                second = _lane((krows, PAIR_W)) >= HEAD_DIM
                dk_s[...] = jnp.where(second, _fold_heads(dkks[1]), _fold_heads(dkks[0]))
                dv_s[...] = jnp.where(second, _fold_heads(dvvs[1]), _fold_heads(dvvs[0]))
                ds_ref[0:1, :] += dsink
            else:
                dq2s, dkks, dvvs, _ = grads(
                    [_stack_heads(q_ref[0, :, _tile(t)], SCALE) for t in range(ntile)],
                    [_rows2(kp_ref, kc_ref, _tile(t), single) for t in range(ntile)],
                    [_rows2(vp_ref, vc_ref, _tile(t), single) for t in range(ntile)],
                    [_stack_heads(do_ref[0, :, _tile(t)]) for t in range(ntile)], i,
                    [_lse_col(l_ref[0, :, _tile(t)]) for t in range(ntile)], [None] * ntile,
                    [_lse_col(dl_ref[0, :, _tile(t)]) for t in range(ntile)])
                for t in range(ntile):
                    dq_s[:, _tile(t)] = _unstack_heads(dq2s[t])
                    dk_s[0:krows, _tile(t)] = dkks[t]
                    dv_s[0:krows, _tile(t)] = dvvs[t]

        def emit(dq, dk, dv):
            cos, sn = c_ref[0], s_ref[0]
            o_ref[0, :, 0:qw] = _unrope(dq, cos, sn).astype(BF16)
            o_ref[0, :, qw:qw + kw] = _unrope(dk, cos, sn).astype(BF16)
            o_ref[0, :, qw + kw:qw + 2 * kw] = dv.astype(BF16)
            if qw + 2 * kw < VAR_W:
                o_ref[0, :, qw + 2 * kw:VAR_W] = jnp.zeros((BLOCK, VAR_W - qw - 2 * kw), BF16)

        if single:
            compute()
            emit(dq_s[...], dk_s[0:BLOCK, :], dv_s[0:BLOCK, :])
            return

        @pl.when(i == 0)
        def _():
            car_q[...] = jnp.zeros_like(car_q)
            car_k[...] = jnp.zeros_like(car_k)
            car_v[...] = jnp.zeros_like(car_v)

        @pl.when(i == nblk)
        def _():
            dk_s[...] = jnp.zeros_like(dk_s)
            dv_s[...] = jnp.zeros_like(dv_s)

        pl.when(i < nblk)(compute)
        emit(car_q[...], car_k[...] + dk_s[0:BLOCK, :], car_v[...] + dv_s[0:BLOCK, :])
        car_q[...] = dq_s[...]
        car_k[...] = dk_s[BLOCK:2 * BLOCK, :]
        car_v[...] = dv_s[BLOCK:2 * BLOCK, :]

    cur = lambda i: jnp.minimum(i, nblk - 1)
    prv = lambda i: jnp.maximum(jnp.minimum(i, nblk - 1) - 1, 0)
    outb = lambda i: jnp.maximum(i - 1, 0)
    qrow = pl.BlockSpec((1, BLOCK, qw), lambda b, j, i: (b, cur(i), j))
    in_specs = [
        pl.BlockSpec((1, BLOCK, qw), lambda b, j, i: (b, cur(i), geo.qidx(j))),
        pl.BlockSpec((1, BLOCK, kw), lambda b, j, i: (b, prv(i), geo.kidx(j))),
        pl.BlockSpec((1, BLOCK, kw), lambda b, j, i: (b, cur(i), geo.kidx(j))),
        pl.BlockSpec((1, BLOCK, kw), lambda b, j, i: (b, prv(i), geo.vidx(j))),
        pl.BlockSpec((1, BLOCK, kw), lambda b, j, i: (b, cur(i), geo.vidx(j))),
        qrow, qrow,
    ]
    ins = [view(qkv, VAR_W)] * 5 + [view(do, qw), view(lse, qw)]
    if has_dlse:
        in_specs.append(qrow)
        ins.append(view(dlse, qw))
    in_specs += [
        pl.BlockSpec((1, BLOCK, PAIR_W), lambda b, j, i: (b, outb(i), j)),
        pl.BlockSpec((1, BLOCK, PAIR_W), lambda b, j, i: (b, outb(i), j)),
        pl.BlockSpec(memory_space=pltpu.SMEM),
    ]
    ins += [view(cosf, PAIR_W), view(sins, PAIR_W), sinks]
    scratch = [pltpu.VMEM((BLOCK, qw), F32), pltpu.VMEM((2 * BLOCK, kw), F32), pltpu.VMEM((2 * BLOCK, kw), F32),
               pltpu.VMEM((BLOCK, qw), F32), pltpu.VMEM((BLOCK, kw), F32), pltpu.VMEM((BLOCK, kw), F32)]
    dqkv, dsink = _pcall(
        body, name=name, grid=(NB, r, nsteps), in_specs=in_specs,
        out_specs=[pl.BlockSpec((1, BLOCK, VAR_W), lambda b, j, i: (b, outb(i), j)),
                   pl.BlockSpec((8, PAIR_W), lambda b, j, i: (0, 0))],
        out_shape=[jax.ShapeDtypeStruct((NB, tsub, r * VAR_W), BF16), jax.ShapeDtypeStruct((8, PAIR_W), F32)],
        scratch_shapes=scratch, compiler_params=_params(("arbitrary", "arbitrary", "arbitrary")),
    )(*ins)
    return dqkv.reshape(NB * T, VAR_W), dsink


class _Rows:
    def __init__(self, N, T, tm):
        self.N, self.tm, self.tpe, self.grid = N, tm, T // tm, (N // tm,)

    def row(self, w, col=0):
        return pl.BlockSpec((self.tm, w), lambda i: (i, col))

    def ex(self, w):
        return pl.BlockSpec((1, 1, w), lambda i: (i // self.tpe, 0, 0))

    def const(self, shape):
        return pl.BlockSpec(shape, lambda i: tuple(0 for _ in shape))

    def first_of_example(self):
        return pl.program_id(0) % self.tpe == 0


def _acc(ref, first, val):
    @pl.when(first)
    def _():
        ref[0] = val

    @pl.when(jnp.logical_not(first))
    def _():
        ref[0] += val


def _colsum(v):
    return jnp.sum(v, axis=0, keepdims=True)


def _ln_stats(r):
    mu = jnp.mean(r, axis=-1, keepdims=True)
    xc = r - mu
    var = jnp.mean(xc * xc, axis=-1, keepdims=True)
    rstd = lax.rsqrt(var + LN_EPS)
    return xc * rstd, rstd


def _ln_bwd(dy, xhat, rstd, gain):
    dxh = dy * gain
    return rstd * (dxh - jnp.mean(dxh, axis=-1, keepdims=True) - xhat * jnp.mean(dxh * xhat, axis=-1, keepdims=True))


def _silu_parts(v):
    s = jax.nn.sigmoid(v)
    return v * s, s * (1.0 + v * (1.0 - s))


def _local_step(x, mod, positions, w_in, rest_weights, sinks, ln1_g, ln1_b, ln2_g, ln2_b, target, hook=None):
    hook = hook or (lambda event, **data: None)
    NB, T, D = x.shape
    N = NB * T
    x2 = x.reshape(N, D)
    tgt2 = target.reshape(N, D)
    shift_m, scale_m, gate_m, shift_f, scale_f, gate_f = [mod[:, None, k * D:(k + 1) * D] for k in range(6)]
    cosf, sins = _rope_tables(positions)
    col = jnp.arange(QKV_P)
    vcol = col % VAR_W
    flags = jnp.where(col < VAR_W, vcol < QA_W + KA_W, vcol < 2 * GB_W).astype(F32)[None]
    R = _Rows(N, T, _pick(T, 256))
    sds = jax.ShapeDtypeStruct
    exsum = lambda w=D: sds((NB, 1, w), F32)
    ngrp = len(B_PATTERNS)

    *qkv, u = _inproj(x2, scale_m, shift_m, w_in, cosf, sins, flags, T=T, name="inproj_qkv")
    gates = _mm(u, w_in[:, QKV_P:], name="inproj_gates")
    oa, _, la = _attn_fwd(qkv[0], sinks, None, NB=NB, T=T, name="attn_a_fwd")
    ob_parts = [_attn_fwd(qkv[1 + g], sinks, g, NB=NB, T=T, name=f"attn_b{g}_fwd") for g in range(ngrp)]
    (o1, l1, lv1), (o2, l2, lv2), (o3, l3, lv3) = ob_parts
    w_a, w_b, w_o, w_gu, w_d = rest_weights()
    F = w_d.shape[0]

    def merge_fwd(o1r, o2r, o3r, l1r, l2r, l3r, ob_ref):
        la, lb, lc = l1r[...], l2r[...], l3r[...]
        mx = jnp.maximum(jnp.maximum(la, lb), lc)
        ea, eb, ec = jnp.exp(la - mx), jnp.exp(lb - mx), jnp.exp(lc - mx)
        ob_ref[...] = ((ea * o1r[...] + eb * o2r[...] + ec * o3r[...]) / (ea + eb + ec)).astype(BF16)

    ob = _pcall(merge_fwd, name="merge_fwd", grid=R.grid, in_specs=[R.row(GB_W)] * 6, out_specs=R.row(GB_W),
                out_shape=sds((N, GB_W), BF16), compiler_params=_params(("parallel",)))(o1, o2, o3, l1, l2, l3)

    ya = _mm(oa, w_a, name="branch_a")
    yb = _mm(ob, w_b, b3=True, name="branch_b")

    def gate_fwd(ya_r, yb_r, ga_r, gb_r, mg_ref):
        mg_ref[...] = (jax.nn.sigmoid(ga_r[...]) * ya_r[...] + jax.nn.sigmoid(gb_r[...]) * yb_r[...]).astype(BF16)

    merged = _pcall(gate_fwd, name="gate_fwd", grid=R.grid, in_specs=[R.row(D), R.row(D), R.row(D, 0), R.row(D, 1)],
                    out_specs=R.row(D), out_shape=sds((N, D), BF16),
                    compiler_params=_params(("parallel",)))(ya, yb, gates, gates)
    y = _mm(merged, w_o, name="out_proj")

    def norm1_fwd(x_r, y_r, gm_r, g_r, b_r, sf_r, hf_r, r1_ref, x1_ref, u2_ref):
        r1 = ALPHA * x_r[...] + (1.0 + gm_r[0]) * y_r[...]
        xhat, _ = _ln_stats(r1)
        x1 = xhat * g_r[...] + b_r[...]
        r1_ref[...] = r1
        x1_ref[...] = x1
        u2_ref[...] = (x1 * (1.0 + sf_r[0]) + hf_r[0]).astype(BF16)

    r1, x1, u2 = _pcall(
        norm1_fwd, name="norm1_fwd", grid=R.grid,
        in_specs=[R.row(D), R.row(D), R.ex(D), R.const((1, D)), R.const((1, D)), R.ex(D), R.ex(D)],
        out_specs=[R.row(D)] * 3, out_shape=[sds((N, D), F32), sds((N, D), F32), sds((N, D), BF16)],
        compiler_params=_params(("parallel",)))(x2, y, gate_m, ln1_g, ln1_b, scale_f, shift_f)

    tnf = w_gu.shape[2]
    nft = w_gu.shape[0] // 2
    tmf = _pick(N, 512)

    def ffn_up(u_r, wg_r, wu_r, hg_ref, hu_ref, a_ref):
        hg = jnp.dot(u_r[...], wg_r[...], preferred_element_type=F32)
        hu = jnp.dot(u_r[...], wu_r[...], preferred_element_type=F32)
        sl, _ = _silu_parts(hg)
        hg_ref[...] = hg.astype(BF16)
        hu_ref[...] = hu.astype(BF16)
        a_ref[...] = (sl * hu).astype(BF16)

    ftile = pl.BlockSpec((tmf, tnf), lambda j, i: (i, j))
    hg, hu, act = _pcall(
        ffn_up, name="ffn_up", grid=(nft, N // tmf),
        in_specs=[pl.BlockSpec((tmf, D), lambda j, i: (i, 0)), pl.BlockSpec((None, D, tnf), lambda j, i: (j, 0, 0)),
                  pl.BlockSpec((None, D, tnf), lambda j, i: (j + nft, 0, 0))],
        out_specs=[ftile] * 3, out_shape=[sds((N, F), BF16)] * 3,
        compiler_params=_params(("arbitrary", "parallel")))(u2, w_gu, w_gu)
    y2 = _mm(act, w_d, name="ffn_down")

    def norm2_loss_bwd(x1_r, y2_r, t_r, gf_r, g_r, b_r, dy2_ref, dx1_ref, dgf_ref, dg_ref, db_ref, loss_ref):
        first = R.first_of_example()
        y2v = y2_r[...]
        r2 = ALPHA * x1_r[...] + (1.0 + gf_r[0]) * y2v
        xhat, rstd = _ln_stats(r2)
        err = xhat * g_r[...] + b_r[...] - t_r[...]
        dx2 = err * (1.0 / D)
        dr2 = _ln_bwd(dx2, xhat, rstd, g_r[...])
        dy2_ref[...] = ((1.0 + gf_r[0]) * dr2).astype(BF16)
        dx1_ref[...] = ALPHA * dr2
        _acc(dgf_ref, first, _colsum(dr2 * y2v))
        _acc(dg_ref, first, _colsum(dx2 * xhat))
        _acc(db_ref, first, _colsum(dx2))
        part = 0.5 * jnp.sum(jnp.mean(err * err, axis=-1, keepdims=True))
        _acc(loss_ref, first, jnp.broadcast_to(part, (1, 128)))

    dy2, dx1p, dgate_f, dg2, db2, loss_p = _pcall(
        norm2_loss_bwd, name="norm2_loss_bwd", grid=R.grid,
        in_specs=[R.row(D), R.row(D), R.row(D), R.ex(D), R.const((1, D)), R.const((1, D))],
        out_specs=[R.row(D), R.row(D), R.ex(D), R.ex(D), R.ex(D), R.ex(128)],
        out_shape=[sds((N, D), BF16), sds((N, D), F32), exsum(), exsum(), exsum(), exsum(128)],
        compiler_params=_params(("arbitrary",)))(x1, y2, tgt2, gate_f, ln2_g, ln2_b)

    g_wd = _mm(act, dy2, ta=True, out_dtype=BF16, name="ffn_down_dw")

    tmd = _pick(N, 256)

    def ffn_down_dx(dy_r, wd_r, hg_r, hu_r, dh_ref):
        for t in range(nft):
            cs = slice(t * tnf, (t + 1) * tnf)
            da = lax.dot_general(dy_r[...], wd_r[cs, :], _NT, preferred_element_type=F32)
            sl, dsl = _silu_parts(hg_r[:, cs].astype(F32))
            dh_ref[:, cs] = (da * hu_r[:, cs].astype(F32) * dsl).astype(BF16)
            dh_ref[:, F + t * tnf:F + (t + 1) * tnf] = (da * sl).astype(BF16)

    rowd = lambda w_: pl.BlockSpec((tmd, w_), lambda i: (i, 0))
    dh = _pcall(
        ffn_down_dx, name="ffn_down_dx", grid=(N // tmd,),
        in_specs=[rowd(D), pl.BlockSpec((F, D), lambda i: (0, 0)), rowd(F), rowd(F)],
        out_specs=rowd(2 * F), out_shape=sds((N, 2 * F), BF16),
        compiler_params=_params(("parallel",)))(dy2, w_d, hg, hu)
    du2 = _mm(dh, w_gu, tb=True, b3=True, name="ffn_up_dx")
    g_wgu = _mm(u2, dh, ta=True, out3=w_gu.shape[0], out_dtype=BF16, name="ffn_up_dw")

    def norm1_bwd(dx1p_r, du2_r, x1_r, r1_r, y_r, sf_r, gm_r, g_r,
                  dxp_ref, dy_ref, dsf_ref, dhf_ref, dgm_ref, dg_ref, db_ref):
        first = R.first_of_example()
        du2v = du2_r[...]
        dx1 = dx1p_r[...] + du2v * (1.0 + sf_r[0])
        xhat, rstd = _ln_stats(r1_r[...])
        dr1 = _ln_bwd(dx1, xhat, rstd, g_r[...])
        dxp_ref[...] = ALPHA * dr1
        dy_ref[...] = ((1.0 + gm_r[0]) * dr1).astype(BF16)
        _acc(dsf_ref, first, _colsum(du2v * x1_r[...]))
        _acc(dhf_ref, first, _colsum(du2v))
        _acc(dgm_ref, first, _colsum(dr1 * y_r[...]))
        _acc(dg_ref, first, _colsum(dx1 * xhat))
        _acc(db_ref, first, _colsum(dx1))

    dxp, dy, dscale_f, dshift_f, dgate_m, dg1, db1 = _pcall(
        norm1_bwd, name="norm1_bwd", grid=R.grid,
        in_specs=[R.row(D)] * 5 + [R.ex(D), R.ex(D), R.const((1, D))],
        out_specs=[R.row(D), R.row(D)] + [R.ex(D)] * 5,
        out_shape=[sds((N, D), F32), sds((N, D), BF16)] + [exsum()] * 5,
        compiler_params=_params(("arbitrary",)))(dx1p, du2, x1, r1, y, scale_f, gate_m, ln1_g)

    dmerged = _mm(dy, w_o, tb=True, name="out_proj_dx")
    g_wo = _mm(merged, dy, ta=True, out_dtype=BF16, name="out_proj_dw")

    def gate_bwd(dm_r, ya_r, yb_r, ga_r, gb_r, dya_ref, dyb_ref, dg_ref):
        dm = dm_r[...]
        sa, sb = jax.nn.sigmoid(ga_r[...]), jax.nn.sigmoid(gb_r[...])
        dya_ref[...] = (dm * sa).astype(BF16)
        dyb_ref[...] = (dm * sb).astype(BF16)
        dg_ref[:, :D] = (dm * ya_r[...] * sa * (1.0 - sa)).astype(BF16)
        dg_ref[:, D:] = (dm * yb_r[...] * sb * (1.0 - sb)).astype(BF16)

    dya, dyb, dgates = _pcall(
        gate_bwd, name="gate_bwd", grid=R.grid, in_specs=[R.row(D)] * 3 + [R.row(D, 0), R.row(D, 1)],
        out_specs=[R.row(D), R.row(D), R.row(2 * D)],
        out_shape=[sds((N, D), BF16), sds((N, D), BF16), sds((N, 2 * D), BF16)],
        compiler_params=_params(("parallel",)))(dmerged, ya, yb, gates, gates)

    doa = _mm(dya, w_a, tb=True, out_dtype=BF16, name="branch_a_dx")
    g_wa = _mm(oa, dya, ta=True, out_dtype=BF16, name="branch_a_dw")
    dob = _mm(dyb, w_b, tb=True, b3=True, name="branch_b_dx")
    g_wb = _mm(ob, dyb, ta=True, out3=w_b.shape[0], out_dtype=BF16, name="branch_b_dw")
    hook("rest_grads", g_wa=g_wa, g_wb=g_wb, g_wo=g_wo, g_wgu=g_wgu, g_wd=g_wd)

    seg = (jnp.arange(GB_W)[:, None] // HEAD_DIM == jnp.arange(GB_W)[None, :] // HEAD_DIM).astype(BF16)

    def merge_bwd(dob_r, o1r, o2r, o3r, l1r, l2r, l3r, seg_r, d1, d2, d3, e1, e2, e3):
        dob_v = dob_r[...]
        la, lb, lc = l1r[...], l2r[...], l3r[...]
        mx = jnp.maximum(jnp.maximum(la, lb), lc)
        ea, eb, ec = jnp.exp(la - mx), jnp.exp(lb - mx), jnp.exp(lc - mx)
        inv = 1.0 / (ea + eb + ec)
        ws = [ea * inv, eb * inv, ec * inv]

        def headsum(v):
            hi = v.astype(BF16)
            r1_ = v - hi.astype(F32)
            mid = r1_.astype(BF16)
            lo = (r1_ - mid.astype(F32)).astype(BF16)
            sm = seg_r[...]
            return (jnp.dot(hi, sm, preferred_element_type=F32) + jnp.dot(mid, sm, preferred_element_type=F32)
                    + jnp.dot(lo, sm, preferred_element_type=F32))

        dws = [headsum(dob_v * o[...]) for o in (o1r, o2r, o3r)]
        mean = ws[0] * dws[0] + ws[1] * dws[1] + ws[2] * dws[2]
        for w_, dw_, d_ref, e_ref in zip(ws, dws, (d1, d2, d3), (e1, e2, e3)):
            d_ref[...] = (w_ * dob_v).astype(BF16)
            e_ref[...] = w_ * (dw_ - mean)

    mb = _pcall(
        merge_bwd, name="merge_bwd", grid=R.grid, in_specs=[R.row(GB_W)] * 7 + [R.const((GB_W, GB_W))],
        out_specs=[R.row(GB_W)] * 6, out_shape=[sds((N, GB_W), BF16)] * 3 + [sds((N, GB_W), F32)] * 3,
        compiler_params=_params(("parallel",)))(dob, o1, o2, o3, l1, l2, l3, seg)
    do_b, dlse_b = mb[:3], mb[3:]
    hook("merge_bwd_done")

    dqkv_a, dsink = _attn_bwd(qkv[0], doa, la, None, cosf, sins, sinks, None, NB=NB, T=T, name="attn_a_bwd")
    hook("attn_a_bwd_done")
    dqkv = [dqkv_a]
    for g in range(ngrp):
        dqkv.append(_attn_bwd(qkv[1 + g], do_b[g], (lv1, lv2, lv3)[g], dlse_b[g], cosf, sins, sinks, g, NB=NB, T=T,
                              name=f"attn_b{g}_bwd")[0])
        hook(f"attn_b{g}_bwd_done")

    g_win = [_mm(u, dseg, ta=True, out_dtype=BF16, name=f"inproj_dw{n}") for n, dseg in enumerate(dqkv + [dgates])]
    hook("win_grads", g_win=g_win)
    wvar = lambda v: (w_in, (D, VAR_W), (0, v))
    du = _mm_multi(dqkv[:1], [wvar(0)], name="inproj_dx0")
    hook("inproj_dx0_done")
    du = _mm_multi(dqkv[1:] + [dgates], [wvar(v) for v in range(1, N_VAR)] + [(w_in, (D, 2 * D), (0, QKV_P // (2 * D)))],
                   add=du, tm=256, name="inproj_dx1")
    hook("inproj_dx1_done")

    def x_bwd(dxp_r, du_r, x_r, sm_r, gx_ref, dsm_ref, dhm_ref):
        first = R.first_of_example()
        duv = du_r[...]
        gx_ref[...] = dxp_r[...] + duv * (1.0 + sm_r[0])
        _acc(dsm_ref, first, _colsum(duv * x_r[...]))
        _acc(dhm_ref, first, _colsum(duv))

    gx, dscale_m, dshift_m = _pcall(
        x_bwd, name="x_bwd", grid=R.grid, in_specs=[R.row(D)] * 3 + [R.ex(D)],
        out_specs=[R.row(D), R.ex(D), R.ex(D)], out_shape=[sds((N, D), F32), exsum(), exsum()],
        compiler_params=_params(("arbitrary",)))(dxp, du, x2, scale_m)
    hook("x_bwd_done")

    dmod =jnp.concatenate([dshift_m, dscale_m, dgate_m, dshift_f, dscale_f, dgate_f], axis=-1)[:, 0]
    ln_grads = jnp.concatenate([dg1, db1, dg2, db2], axis=1)
    return dict(loss=loss_p[:, 0, 0], grad_x=gx.reshape(NB, T, D), g_win=g_win, g_wa=g_wa, g_wb=g_wb, g_wo=g_wo,
                g_wgu=g_wgu, g_wd=g_wd, dmod=dmod, ln_grads=ln_grads, dsink=dsink[0, :A_Q_HEADS])


def _coords():
    return lax.axis_index("x"), lax.axis_index("y"), lax.axis_index("c")


def _allgather_small(blk, *, name):
    m_per, n = blk.shape

    def body(x_ref, out_ref, send_sems, recv_sems, local_sem):
        x, y, c = _coords()
        me, sibling = (x, y, c), (x, y, 1 - c)
        chips = [(1 - x, y), (x, 1 - y), (1 - x, 1 - y)]

        def rows(px, py, pc):
            return out_ref.at[pl.ds((4 * px + 2 * py + pc) * m_per, m_per), :]

        def copy(k, block, to, src=None):
            return pltpu.make_async_remote_copy(
                src_ref=rows(*block) if src is None else src, dst_ref=rows(*block),
                send_sem=send_sems.at[k], recv_sem=recv_sems.at[k], device_id=to, device_id_type=MESH)

        mine = pltpu.make_async_copy(x_ref, rows(*me), local_sem)
        mine.start()
        first = [copy(0, me, sibling, src=x_ref)]
        first += [copy(1 + j, me, (*chip, c), src=x_ref) for j, chip in enumerate(chips)]
        for cp in first:
            cp.start()
        passed = [copy(4 + j, (*chip, c), sibling) for j, chip in enumerate(chips)]
        for j, chip in enumerate(chips):
            copy(1 + j, (*chip, c), me).wait_recv()
            passed[j].start()
        copy(0, sibling, me).wait_recv()
        for j, chip in enumerate(chips):
            copy(4 + j, (*chip, 1 - c), me).wait_recv()
        for cp in first + passed:
            cp.wait_send()
        mine.wait()

    return _pcall(
        body, name=name, out_shape=jax.ShapeDtypeStruct((8 * m_per, n), blk.dtype),
        in_specs=[pl.BlockSpec(memory_space=pltpu.VMEM)], out_specs=pl.BlockSpec(memory_space=pltpu.VMEM),
        scratch_shapes=[pltpu.SemaphoreType.DMA((7,)), pltpu.SemaphoreType.DMA((7,)), pltpu.SemaphoreType.DMA],
        compiler_params=pltpu.CompilerParams(vmem_limit_bytes=VMEM_LIMIT_BYTES),
    )(blk)


def _exchange(srcs, dsts, plan, *, name, dst_inits=None):
    na = len(dsts)
    nrem = len(plan(0, 0, 0))

    def body(*refs):
        refs = list(refs)
        src_refs = [refs.pop(0) for _ in range(na)] if srcs is not None else None
        if dst_inits is not None:
            del refs[:na]
        dst_refs, (send_sems, recv_sems) = refs[:na], refs[na:]
        start, wait = _copies(dst_refs if src_refs is None else src_refs, dst_refs, send_sems, recv_sems, plan)
        start()
        wait()

    hbm = pl.BlockSpec(memory_space=pl.ANY)
    ins = (list(srcs) if srcs is not None else []) + (list(dst_inits) if dst_inits is not None else [])
    base = na if srcs is not None else 0
    aliases = {base + a: a for a in range(na)} if dst_inits is not None else {}
    return _pcall(
        body, name=name, out_shape=list(dsts), in_specs=[hbm] * len(ins), out_specs=[hbm] * na,
        input_output_aliases=aliases,
        scratch_shapes=[pltpu.SemaphoreType.DMA((na * nrem,)), pltpu.SemaphoreType.DMA((na * nrem,))],
    )(*ins)


def _other_chips(x, y):
    return [(1 - x, y), (x, 1 - y), (1 - x, 1 - y)]


def _round(ride, carrier, name):
    if carrier is not None:
        _RIDES.setdefault(carrier, []).append(ride)
        return
    srcs = ride.srcs() if callable(ride.srcs) else ride.srcs
    inits = ride.dst_inits() if callable(ride.dst_inits) else ride.dst_inits
    ride.out = list(_exchange(srcs, ride.dsts, ride.plan, name=name, dst_inits=inits))


class _Gather:
    def __init__(self, shards, chip, tag, carriers=(None, None)):
        def plan_ici(x, y, c):
            k = 2 * x + y
            return [((c,), (k, c), (2 * px + py, c), (px, py, c)) for px, py in _other_chips(x, y)]

        def plan_d2d(x, y, c):
            return [((2 * px + py, c), (2 * px + py, c), (2 * px + py, 1 - c), (x, y, 1 - c))
                    for px, py in _other_chips(x, y)]

        self.shards, self.chip = shards, chip
        dsts = [jax.ShapeDtypeStruct((4,) + s.shape, s.dtype) for s in shards]
        ici = _Ride(shards, dsts, plan_ici)
        self.d2d = _Ride(None, dsts, plan_d2d, dst_inits=lambda: ici.out)
        _round(ici, carriers[0], f"gather_{tag}_ici")
        _round(self.d2d, carriers[1], f"gather_{tag}_d2d")

    def result(self):
        full = [lax.dynamic_update_index_in_dim(f, s, self.chip, 0) for f, s in zip(self.d2d.out, self.shards)]
        return [f.reshape((4, 2 * f.shape[2], f.shape[3])) for f in full]


def _add_pairs(a, b, *, name):
    s, hr, wd = a.shape
    tr = _pick(hr, 600, 16)

    def body(a_ref, b_ref, o_ref):
        o_ref[...] = (a_ref[...].astype(F32) + b_ref[...].astype(F32)).astype(BF16)

    spec = pl.BlockSpec((1, tr, wd), lambda j, i: (j, i, 0))
    return _pcall(body, name=name, grid=(s, hr // tr), in_specs=[spec, spec], out_specs=spec,
                  out_shape=jax.ShapeDtypeStruct(a.shape, BF16), compiler_params=_params(("parallel", "parallel")))(a, b)


def _sum_chips(b, *, name):
    s, hr, wd = b.shape
    tr = _pick(hr, 600, 16)

    def body(b_ref, o_ref):
        acc = b_ref[0].astype(F32)
        for k in range(1, s):
            acc = acc + b_ref[k].astype(F32)
        o_ref[...] = acc

    return _pcall(body, name=name, grid=(hr // tr,), in_specs=[pl.BlockSpec((s, tr, wd), lambda i: (0, i, 0))],
                  out_specs=pl.BlockSpec((tr, wd), lambda i: (i, 0)), out_shape=jax.ShapeDtypeStruct((hr, wd), F32),
                  compiler_params=_params(("parallel",)))(b)


class _ReduceScatter:
    def __init__(self, gs, chip, ci, tag):
        self.gs, self.chip, self.ci, self.tag = gs, chip, ci, tag
        self.half_t = [jax.ShapeDtypeStruct((g.shape[0],) + g.shape[2:], BF16) for g in gs]

    def pair(self, carrier=None):
        plan = lambda x, y, c: [((slice(None), 1 - c), (), (), (x, y, 1 - c))]
        self.r1 = _Ride(self.gs, self.half_t, plan)
        _round(self.r1, carrier, f"reduce_{self.tag}_pair")

    def chips(self, carrier=None):
        def plan(x, y, c):
            k = 2 * x + y
            return [((2 * px + py,), (k,), (2 * px + py,), (px, py, c)) for px, py in _other_chips(x, y)]

        self.pairs = [_add_pairs(lax.dynamic_index_in_dim(g, self.ci, 1, keepdims=False), f,
                                 name=f"reduce_{self.tag}_pair_add{n}")
                      for n, (g, f) in enumerate(zip(self.gs, self.r1.out))]
        self.r2 = _Ride(self.pairs, self.half_t, plan)
        _round(self.r2, carrier, f"reduce_{self.tag}_chips")

    def halves(self, carrier=None):
        plan = lambda x, y, c: [((), (c,), (1 - c,), (x, y, 1 - c))]
        self.mine = []
        for n, (l, p) in enumerate(zip(self.r2.out, self.pairs)):
            own = lax.dynamic_index_in_dim(p, self.chip, 0, keepdims=False)
            self.mine.append(_sum_chips(lax.dynamic_update_index_in_dim(l, own, self.chip, 0),
                                        name=f"reduce_{self.tag}_chip_sum{n}"))
        self.r3 = _Ride(self.mine, [jax.ShapeDtypeStruct((2,) + m.shape, F32) for m in self.mine], plan)
        _round(self.r3, carrier, f"reduce_{self.tag}_halves")

    def result(self):
        return [lax.dynamic_update_index_in_dim(b, m, self.ci, 0).reshape(2 * m.shape[0], m.shape[1])
                for b, m in zip(self.r3.out, self.mine)]


def _ada_fwd(c_all, w_sh, b_sh, *, name):
    nb, d = c_all.shape
    wcols = w_sh.shape[1]
    tn = _pick(wcols, 512)

    def body(c_ref, w_ref, b_ref, o_ref, a_ref):
        cv = c_ref[...]
        act = cv * jax.nn.sigmoid(cv)
        a_ref[...] = act
        o_ref[...] = jnp.dot(act.astype(BF16), w_ref[...].astype(BF16), preferred_element_type=F32) + b_ref[...]

    return _pcall(
        body, name=name, grid=(wcols // tn,),
        in_specs=[pl.BlockSpec((nb, d), lambda j: (0, 0)), pl.BlockSpec((d, tn), lambda j: (0, j)),
                  pl.BlockSpec((1, tn), lambda j: (0, j))],
        out_specs=[pl.BlockSpec((nb, tn), lambda j: (0, j)), pl.BlockSpec((nb, d), lambda j: (0, 0))],
        out_shape=[jax.ShapeDtypeStruct((nb, wcols), F32), jax.ShapeDtypeStruct((nb, d), F32)],
        compiler_params=_params(("arbitrary",)))(c_all, w_sh, b_sh)


def _sum_devices(g, *, name):
    nd, m, w = g.shape

    def body(g_ref, o_ref):
        acc = g_ref[0]
        for k in range(1, nd):
            acc = acc + g_ref[k]
        o_ref[...] = acc

    return _pcall(body, name=name, out_shape=jax.ShapeDtypeStruct((m, w), F32),
                  compiler_params=pltpu.CompilerParams(vmem_limit_bytes=VMEM_LIMIT_BYTES))(g)


def _adamw(w, g, m, v, *, name):
    rows, cols = w.shape[-2:]
    tr = _pick(rows, max(8, (1 << 18) // cols), 8)
    c1 = 1.0 / (1.0 - ADAM_B1 ** ADAM_STEP)
    c2 = 1.0 / (1.0 - ADAM_B2 ** ADAM_STEP)

    def body(w_ref, g_ref, m_ref, v_ref, d_ref, nm_ref, nv_ref):
        gv = g_ref[...]
        nm = ADAM_B1 * m_ref[...] + (1.0 - ADAM_B1) * gv
        nv = ADAM_B2 * v_ref[...] + (1.0 - ADAM_B2) * (gv * gv)
        d_ref[...] = -ADAM_LR * ((nm * c1) / (jnp.sqrt(nv * c2) + ADAM_EPS) + ADAM_WD * w_ref[...])
        nm_ref[...] = nm
        nv_ref[...] = nv

    gspec = pl.BlockSpec((tr, cols), lambda i: (i, 0))
    spec = pl.BlockSpec((None, tr, cols), lambda i: (0, i, 0)) if w.ndim == 3 else gspec
    shp = jax.ShapeDtypeStruct(w.shape, F32)
    return _pcall(body, name=name, grid=(rows // tr,), in_specs=[spec, gspec, spec, spec], out_specs=[spec] * 3,
                  out_shape=[shp] * 3, compiler_params=_params(("parallel",)))(w, g, m, v)


def _permute_in_cols(w):
    ngrp = len(B_PATTERNS)
    qb, kb, vb = (w[:, A_W + n * QB_W:A_W + (n + 1) * QB_W] for n in range(3))
    parts = [w[:, :A_W], jnp.zeros((w.shape[0], VAR_W - A_W), w.dtype)]
    for g in range(ngrp):
        parts += [t[:, g * GB_W:(g + 1) * GB_W] for t in (qb, kb, vb)]
    return jnp.concatenate(parts + [w[:, A_W + 3 * QB_W:]], axis=1)


def _unpermute_in_grads(pieces):
    ga, groups, gg = pieces[0], pieces[1:-1], pieces[-1]
    cols = [ga[:, :A_W]]
    for n in range(3):
        cols += [gp[:, n * GB_W:(n + 1) * GB_W] for gp in groups]
    return jnp.concatenate(cols + [gg], axis=1)


def kernel(x, c, positions, w_ada, b_ada, w_in, sinks, w_branch_a, w_branch_b, w_o, ln1_g, ln1_b, w_gate_up, w_down, ln2_g, ln2_b, loss_target, m_w_ada, m_b_ada, m_w_in, m_sinks, m_w_branch_a, m_w_branch_b, m_w_o, m_ln1_g, m_ln1_b, m_w_gate_up, m_w_down, m_ln2_g, m_ln2_b, v_w_ada, v_b_ada, v_w_in, v_sinks, v_w_branch_a, v_w_branch_b, v_w_o, v_ln1_g, v_ln1_b, v_w_gate_up, v_w_down, v_ln2_g, v_ln2_b):
    xi, yi, ci = _coords()
    chip = 2 * xi + yi
    dev = 4 * xi + 2 * yi + ci
    NB, T, D = x.shape
    nchip, ndev = 4, 8
    ada_cols = w_ada.shape[2]

    c_blk = jnp.zeros((8, D), F32).at[:NB].set(c)
    c_all = _allgather_small(c_blk, name="gather_c").reshape(ndev, 8, D)[:, :NB].reshape(ndev * NB, D)
    b_sh = lax.dynamic_slice(b_ada, (0, chip * ada_cols), (1, ada_cols))
    mod_part, c_act = _ada_fwd(c_all, w_ada[0], b_sh, name="ada_fwd")
    mod_g = _allgather_small(mod_part, name="gather_mod").reshape(nchip, 2, ndev * NB, ada_cols)[:, 0]
    mod_all = jnp.transpose(mod_g, (1, 0, 2)).reshape(ndev * NB, nchip * ada_cols)
    mod = lax.dynamic_slice(mod_all, (NB * dev, 0), (NB, nchip * ada_cols))

    ra, ro, rd = w_branch_a.shape[1], w_o.shape[1], w_down.shape[1]
    rowsh = jnp.concatenate([w_branch_a[0], w_o[0], w_down[0]], axis=0)
    halves = lambda a: a.reshape(a.shape[:-2] + (2, a.shape[-2] // 2, a.shape[-1]))
    whole = lambda a: a.reshape(a.shape[:-3] + (2 * a.shape[-2], a.shape[-1]))
    shards = [halves(w.astype(BF16)) for w in (w_in[0], rowsh, w_branch_b[0], w_gate_up[0])]
    (g_in,) = _Gather(shards[:1], chip, "w_in").result()
    w_in_f = _permute_in_cols(jnp.concatenate([g_in[k] for k in range(nchip)], axis=1))
    mix = _Gather(shards[1:3], chip, "w_mix", carriers=("inproj_qkv", "attn_a_fwd"))
    ffn = _Gather(shards[3:], chip, "w_ffn", carriers=("attn_a_fwd", "attn_b0_fwd"))

    def rest_weights():
        (g_rows, w_b_f), (w_gu_f,) = mix.result(), ffn.result()
        return (g_rows[:, :ra].reshape(nchip * ra, D), w_b_f, g_rows[:, ra:ra + ro].reshape(nchip * ro, D), w_gu_f,
                g_rows[:, ra + ro:].reshape(nchip * rd, D))

    red = {}

    def hook(event, **g):
        if event == "rest_grads":
            gr_rows = jnp.concatenate([g["g_wa"].reshape(nchip, ra, D), g["g_wo"].reshape(nchip, ro, D),
                                       g["g_wd"].reshape(nchip, rd, D)], axis=1)
            red["rest"] = _ReduceScatter([halves(a) for a in (gr_rows, g["g_wb"], g["g_wgu"])], chip, ci, "rest")
            red["rest"].pair(carrier="merge_bwd")
        elif event == "merge_bwd_done":
            red["rest"].chips(carrier="attn_a_bwd")
        elif event == "attn_a_bwd_done":
            red["rest"].halves(carrier="attn_b0_bwd")
        elif event == "win_grads":
            gr_in = jnp.stack(jnp.split(_unpermute_in_grads(g["g_win"]), nchip, axis=1))
            red["w_in"] = _ReduceScatter([halves(gr_in)], chip, ci, "w_in")
            red["w_in"].pair(carrier="inproj_dx0")
        elif event == "inproj_dx0_done":
            red["w_in"].chips(carrier="inproj_dx1")
        elif event == "inproj_dx1_done":
            red["w_in"].halves(carrier="x_bwd")

    res = _local_step(x, mod, positions, w_in_f, rest_weights, sinks[0], ln1_g, ln1_b, ln2_g, ln2_b, loss_target, hook)
    (g_w_in,) = red["w_in"].result()
    g_rows_red, g_w_b, g_w_gu = red["rest"].result()
    g_w_a, g_w_o, g_w_d = g_rows_red[:ra], g_rows_red[ra:ra + ro], g_rows_red[ra + ro:]

    small_rows = 24
    misc = jnp.zeros((1, D), F32).at[0, :A_Q_HEADS].set(res["dsink"]).at[0, A_Q_HEADS].set(jnp.sum(res["loss"]))
    small = jnp.concatenate([res["dmod"].reshape(NB * 6, D), jnp.sum(res["ln_grads"], axis=0), misc,
                             jnp.zeros((small_rows - NB * 6 - 5, D), F32)], axis=0)
    small_all = _allgather_small(small, name="gather_small").reshape(ndev, small_rows, D)
    dmod_all = small_all[:, :NB * 6].reshape(ndev * NB, 6 * D)
    sums = _sum_devices(small_all, name="sum_small")
    g_b_ada = (sums[0:6] + sums[6:12]).reshape(1, 6 * D)
    g_ln1_g, g_ln1_b, g_ln2_g, g_ln2_b = (sums[12 + n][None] for n in range(4))
    g_sinks = sums[16, :A_Q_HEADS][None]
    loss = sums[16, A_Q_HEADS]
    dmod_sh = lax.dynamic_slice(dmod_all, (0, chip * ada_cols), (ndev * NB, ada_cols))
    g_w_ada = _mm(c_act, dmod_sh, ta=True, name="ada_dw")

    names = ["w_ada", "b_ada", "w_in", "sinks", "w_branch_a", "w_branch_b", "w_o", "ln1_g", "ln1_b",
             "w_gate_up", "w_down", "ln2_g", "ln2_b"]
    ws = [w_ada, b_ada, w_in, sinks, w_branch_a, w_branch_b, w_o, ln1_g, ln1_b, w_gate_up, w_down, ln2_g, ln2_b]
    ms = [m_w_ada, m_b_ada, m_w_in, m_sinks, m_w_branch_a, m_w_branch_b, m_w_o, m_ln1_g, m_ln1_b, m_w_gate_up,
          m_w_down, m_ln2_g, m_ln2_b]
    vs = [v_w_ada, v_b_ada, v_w_in, v_sinks, v_w_branch_a, v_w_branch_b, v_w_o, v_ln1_g, v_ln1_b, v_w_gate_up,
          v_w_down, v_ln2_g, v_ln2_b]
    gs = [g_w_ada, g_b_ada, g_w_in, g_sinks, g_w_a, g_w_b, g_w_o, g_ln1_g, g_ln1_b, g_w_gu, g_w_d, g_ln2_g, g_ln2_b]
    grads, deltas, new_ms, new_vs = [], [], [], []
    for name, w, g, m, v in zip(names, ws, gs, ms, vs):
        g2 = g.reshape(w.shape[-2:])
        d, nm, nv = _adamw(w, g2, m, v, name="adamw_" + name)
        grads.append(g2.reshape(w.shape))
        deltas.append(d)
        new_ms.append(nm)
        new_vs.append(nv)
    return (loss, res["grad_x"], *grads, *deltas, *new_ms, *new_vs)
```

```python
import functools

import jax
import jax.numpy as jnp
from jax import lax
from jax.experimental import pallas as pl
from jax.experimental.pallas import tpu as pltpu

F32 = jnp.float32
BF16 = jnp.bfloat16
MESH = pl.DeviceIdType.MESH

HEAD_DIM = 64
LANES = 128
PAIR_W = 2 * HEAD_DIM
BLOCK = 128
A_Q_HEADS = 16
A_KV_HEADS = 2
A_WINDOW = 128
B_PATTERNS = ((128, 1), (512, 4), (2048, 16))
B_GROUP_HEADS = 8
QA_W = A_Q_HEADS * HEAD_DIM
KA_W = A_KV_HEADS * HEAD_DIM
GB_W = B_GROUP_HEADS * HEAD_DIM
QB_W = GB_W * len(B_PATTERNS)
A_W = QA_W + 2 * KA_W
VAR_W = 3 * GB_W
N_VAR = 1 + len(B_PATTERNS)
VAR_DIL = (1,) + tuple(r for _, r in B_PATTERNS)
QKV_P = N_VAR * VAR_W
ROPE_THETA = 10000.0
LN_EPS = 1e-5
NEG_INF = -1e30
DEPTH = 1
ALPHA = (2 * DEPTH) ** 0.25
SCALE = HEAD_DIM ** -0.5

ADAM_LR, ADAM_B1, ADAM_B2, ADAM_EPS, ADAM_WD, ADAM_STEP = 0.001, 0.9, 0.999, 1e-08, 0.01, 10

VMEM_LIMIT_BYTES = 56 * 1024 * 1024
MM_TILE_BYTES = 36 * 1024 * 1024
MM_WHOLE_K = 4096


def _params(sem=None):
    return pltpu.CompilerParams(dimension_semantics=sem, vmem_limit_bytes=VMEM_LIMIT_BYTES)


_RIDES = {}


def _pcall(body, *, name, **kw):
    rides = _RIDES.pop(name, None)
    if rides is None:
        return pl.pallas_call(body, name=name, **kw)
    return _riding_call(body, rides, name=name, **kw)


def _copies(src_refs, dst_refs, send_sems, recv_sems, plan):
    x, y, c = lax.axis_index("x"), lax.axis_index("y"), lax.axis_index("c")
    remote = plan(x, y, c)
    nrem = len(remote)
    at = lambda ref, idx: ref.at[idx] if idx else ref

    def copy(a, n, landing):
        si, di, ri, peer = remote[n]
        return pltpu.make_async_remote_copy(
            src_ref=at(src_refs[a], si), dst_ref=at(dst_refs[a], ri if landing else di),
            send_sem=send_sems.at[a * nrem + n], recv_sem=recv_sems.at[a * nrem + n],
            device_id=peer, device_id_type=MESH)

    order = [(a, n) for a in range(len(dst_refs)) for n in range(nrem)]

    def start():
        for a, n in order:
            copy(a, n, False).start()

    def wait():
        for a, n in order:
            copy(a, n, True).wait_recv()
        for a, n in order:
            copy(a, n, False).wait_send()

    return start, wait


class _Ride:
    def __init__(self, srcs, dsts, plan, dst_inits=None):
        self.srcs, self.dsts, self.plan, self.dst_inits, self.out = srcs, dsts, plan, dst_inits, None


def _riding_call(body, rides, *, name, grid, in_specs, out_specs, out_shape, scratch_shapes=(), **kw):
    single = not isinstance(out_specs, (list, tuple))
    out_specs = [out_specs] if single else list(out_specs)
    out_shape = [out_shape] if single else list(out_shape)
    n_in, n_out, n_scr = len(in_specs), len(out_specs), len(scratch_shapes)
    xin, xdsts, sems, aliases, layout = [], [], [], {}, []
    for ride in rides:
        srcs = ride.srcs() if callable(ride.srcs) else ride.srcs
        inits = ride.dst_inits() if callable(ride.dst_inits) else ride.dst_inits
        na, nrem = len(ride.dsts), len(ride.plan(0, 0, 0))
        src_at = len(xin) if srcs is not None else None
        xin += list(srcs) if srcs is not None else []
        if inits is not None:
            aliases.update({n_in + len(xin) + a: n_out + len(xdsts) + a for a in range(na)})
            xin += list(inits)
        layout.append((src_at, len(xdsts), na))
        xdsts += list(ride.dsts)
        sems += [pltpu.SemaphoreType.DMA((na * nrem,)), pltpu.SemaphoreType.DMA((na * nrem,))]

    def wrapped(*refs):
        ins, xins = refs[:n_in], refs[n_in:n_in + len(xin)]
        outs = refs[n_in + len(xin):n_in + len(xin) + n_out]
        xouts = refs[n_in + len(xin) + n_out:n_in + len(xin) + n_out + len(xdsts)]
        scr = refs[n_in + len(xin) + n_out + len(xdsts):]
        rounds = []
        for k, (ride, (src_at, dst_at, na)) in enumerate(zip(rides, layout)):
            dsts = xouts[dst_at:dst_at + na]
            srcs = dsts if src_at is None else xins[src_at:src_at + na]
            rounds.append(_copies(srcs, dsts, scr[n_scr + 2 * k], scr[n_scr + 2 * k + 1], ride.plan))
        ids = [pl.program_id(a) for a in range(len(grid))]
        first = functools.reduce(jnp.logical_and, [i == 0 for i in ids])
        last = functools.reduce(jnp.logical_and, [i == g - 1 for i, g in zip(ids, grid)])

        @pl.when(first)
        def _():
            for start, _ in rounds:
                start()

        body(*ins, *outs, *scr[:n_scr])

        @pl.when(last)
        def _():
            for _, wait in rounds:
                wait()

    hbm = pl.BlockSpec(memory_space=pl.ANY)

    def run(*args):
        res = pl.pallas_call(
            wrapped, name=name, grid=grid, in_specs=list(in_specs) + [hbm] * len(xin),
            out_specs=out_specs + [hbm] * len(xdsts), out_shape=out_shape + xdsts,
            scratch_shapes=list(scratch_shapes) + sems, input_output_aliases=aliases,
            compiler_params=_params(("arbitrary",) * len(grid)),
        )(*args, *xin)
        for ride, (_, dst_at, na) in zip(rides, layout):
            ride.out = list(res[n_out + dst_at:n_out + dst_at + na])
        return res[0] if single else list(res[:n_out])

    return run


def _pick(n, target, quantum=128):
    t = (min(target, n) // quantum) * quantum
    while t >= quantum:
        if n % t == 0:
            return t
        t -= quantum
    return n


def _mm(a, b, *, name, ta=False, tb=False, b3=False, out3=0, out_dtype=F32, add=None, tm=1024, tn=1536, tk=1536):
    if ta:
        K, M = a.shape
    else:
        M, K = a.shape
    if b3 and tb:
        Nn, K2, tk = b.shape[1], b.shape[0] * b.shape[2], b.shape[2]
    elif b3:
        K2, Nn, tn = b.shape[1], b.shape[0] * b.shape[2], b.shape[2]
    elif tb:
        Nn, K2 = b.shape
    else:
        K2, Nn = b.shape
    assert K == K2, (a.shape, b.shape)
    if out3:
        tn = Nn // out3
    tm, tn, tk = _pick(M, tm), _pick(Nn, tn), _pick(K, tk)
    if not (b3 and tb) and K <= MM_WHOLE_K:
        tk = K
        fits = lambda: 4 * tk * (tm + tn) + 8 * tm * tn * (2 if add is not None else 1) <= MM_TILE_BYTES
        while not fits():
            if (tm >= tn or b3 or out3) and tm > 256:
                tm = _pick(M, tm - 128)
            elif not (b3 or out3) and tn > 256:
                tn = _pick(Nn, tn - 128)
            else:
                break
    nk = K // tk
    j_outer = K * Nn + (Nn // tn) * M * K < M * K + (M // tm) * K * Nn
    dn = (((0 if ta else 1,), (1 if tb else 0,)), ((), ()))

    def body(*refs):
        refs = list(refs)
        a_ref, b_ref = refs[:2]
        add_ref = refs[2] if add is not None else None
        o_ref = refs[3] if add is not None else refs[2]
        part = lax.dot_general(a_ref[...].astype(BF16), b_ref[...].astype(BF16), dn, preferred_element_type=F32)

        def finish(r):
            if add is not None:
                r = r + add_ref[...]
            o_ref[...] = r.astype(out_dtype)

        if nk == 1:
            finish(part)
            return
        acc = refs[-1]
        k = pl.program_id(2)

        @pl.when(k == 0)
        def _():
            acc[...] = part

        @pl.when(k > 0)
        def _():
            acc[...] += part

        @pl.when(k == nk - 1)
        def _():
            finish(acc[...])

    def spec(shape, index):
        return pl.BlockSpec(shape, (lambda j, i, k: index(i, j, k)) if j_outer else index)

    a_spec = spec((tk, tm), lambda i, j, k: (k, i)) if ta else spec((tm, tk), lambda i, j, k: (i, k))
    if b3 and tb:
        b_spec = spec((None, tn, tk), lambda i, j, k: (k, j, 0))
    elif b3:
        b_spec = spec((None, tk, tn), lambda i, j, k: (j, k, 0))
    elif tb:
        b_spec = spec((tn, tk), lambda i, j, k: (j, k))
    else:
        b_spec = spec((tk, tn), lambda i, j, k: (k, j))
    if out3:
        o_spec = spec((None, tm, tn), lambda i, j, k: (j, i, 0))
    else:
        o_spec = spec((tm, tn), lambda i, j, k: (i, j))
    ins, specs = [a, b], [a_spec, b_spec]
    if add is not None:
        ins.append(add)
        specs.append(o_spec)
    grid = (Nn // tn, M // tm, nk) if j_outer else (M // tm, Nn // tn, nk)
    return _pcall(
        body, name=name, grid=grid, in_specs=specs, out_specs=o_spec,
        out_shape=jax.ShapeDtypeStruct((out3, M, tn) if out3 else (M, Nn), out_dtype),
        scratch_shapes=[pltpu.VMEM((tm, tn), F32)] if nk > 1 else [],
        compiler_params=_params(("parallel", "parallel", "arbitrary")),
    )(*ins)


def _mm_multi(a_list, b_list, *, name, add=None, out_dtype=F32, tm=512):
    M = a_list[0].shape[0]
    tm = _pick(M, tm)
    ns = len(a_list)
    b_arrs, b_specs = [], []
    for b in b_list:
        arr, shp, idx = b if isinstance(b, tuple) else (b, b.shape, (0, 0))
        b_arrs.append(arr)
        b_specs.append(pl.BlockSpec(shp, lambda i, idx=idx: idx))
    Nn = b_specs[0].block_shape[0]
    dn = (((1,), (1,)), ((), ()))

    def body(*refs):
        a_refs, b_refs = refs[:ns], refs[ns:2 * ns]
        acc = None
        for a_ref, b_ref in zip(a_refs, b_refs):
            part = lax.dot_general(a_ref[...].astype(BF16), b_ref[...], dn, preferred_element_type=F32)
            acc = part if acc is None else acc + part
        if add is not None:
            acc = acc + refs[2 * ns][...]
        refs[-1][...] = acc.astype(out_dtype)

    o_spec = pl.BlockSpec((tm, Nn), lambda i: (i, 0))
    specs = [pl.BlockSpec((tm, a.shape[1]), lambda i: (i, 0)) for a in a_list] + b_specs
    ins = list(a_list) + b_arrs
    if add is not None:
        specs.append(o_spec)
        ins.append(add)
    return _pcall(body, name=name, grid=(M // tm,), in_specs=specs, out_specs=o_spec,
                  out_shape=jax.ShapeDtypeStruct((M, Nn), out_dtype), compiler_params=_params(("parallel",)))(*ins)


def _lane(shape):
    return lax.broadcasted_iota(jnp.int32, shape, len(shape) - 1)


def _rot_half(v):
    w = v.shape[-1]
    first = (_lane(v.shape) % HEAD_DIM) < (HEAD_DIM // 2)
    return jnp.where(first, pltpu.roll(v, w - HEAD_DIM // 2, v.ndim - 1), pltpu.roll(v, HEAD_DIM // 2, v.ndim - 1))


def _widen(t, w):
    return t if w == t.shape[-1] else jnp.concatenate([t] * (w // t.shape[-1]), axis=-1)


def _unrope(v, cos, sins):
    w = v.shape[-1]
    return v * _widen(cos, w) - _rot_half(v) * _widen(sins, w)


def _rope_tables(positions):
    half = HEAD_DIM // 2
    inv = ROPE_THETA ** (-jnp.arange(half, dtype=F32) / half)
    ang = positions.astype(F32)[..., None] * inv
    cos, sin = jnp.cos(ang), jnp.sin(ang)
    cosf = jnp.concatenate([cos, cos, cos, cos], axis=-1)
    sins = jnp.concatenate([-sin, sin, -sin, sin], axis=-1)
    n = positions.shape[0] * positions.shape[1]
    return cosf.reshape(n, PAIR_W), sins.reshape(n, PAIR_W)


def _inproj(x2, scale, shift, w, cosf, sins, flags, *, T, name):
    N, D = x2.shape
    tm, tn = _pick(T, 512), VAR_W
    tpe = T // tm

    def body(x_ref, sc_ref, sh_ref, w_ref, c_ref, s_ref, f_ref, *outs):
        o_refs, u_ref = outs[:N_VAR], outs[N_VAR]
        j = pl.program_id(1)

        @pl.when(j == 0)
        def _():
            u_ref[...] = (x_ref[...] * (1.0 + sc_ref[0]) + sh_ref[0]).astype(BF16)

        acc = jnp.dot(u_ref[...], w_ref[...], preferred_element_type=F32)
        fl = f_ref[...]
        ce = 1.0 + (_widen(c_ref[...], tn) - 1.0) * fl
        se = _widen(s_ref[...], tn) * fl
        res = acc * ce + _rot_half(acc) * se
        for v in range(N_VAR):
            @pl.when(j == v)
            def _(v=v):
                _to_view(res, o_refs[v], outs[N_VAR + 1], VAR_DIL[v])

    ex = pl.BlockSpec((1, 1, D), lambda i, j: (i // tpe, 0, 0))
    tab = pl.BlockSpec((tm, PAIR_W), lambda i, j: (i, 0))
    keep = lambda w_: pl.BlockSpec((tm, w_), lambda i, j: (i, 0))
    vspec = lambda r: pl.BlockSpec((None, tm // r, r * tn), lambda i, j: (i // tpe, i % tpe, 0))
    vshape = lambda r: jax.ShapeDtypeStruct((N // T, T // r, r * tn), BF16)
    return _pcall(
        body, name=name, grid=(N // tm, N_VAR),
        in_specs=[keep(D), ex, ex, pl.BlockSpec((D, tn), lambda i, j: (0, j)), tab, tab,
                  pl.BlockSpec((1, tn), lambda i, j: (0, j))],
        out_specs=[vspec(r) for r in VAR_DIL] + [keep(D)],
        out_shape=[vshape(r) for r in VAR_DIL] + [jax.ShapeDtypeStruct((N, D), BF16)],
        scratch_shapes=[pltpu.VMEM((tn // LANES, tm, LANES), F32)],
        compiler_params=_params(("parallel", "arbitrary")),
    )(x2, scale, shift, w, cosf, sins, flags)


class _Geom:
    def __init__(self, g):
        if g is None:
            self.r, self.nq, self.n_back, self.sink = 1, A_Q_HEADS, A_WINDOW - 1, True
            self.qw, self.kw = QA_W, KA_W
            self.qidx = lambda j: 0
            self.kidx = lambda j: QA_W // KA_W
            self.vidx = lambda j: QA_W // KA_W + 1
        else:
            window, r = B_PATTERNS[g]
            self.r, self.nq, self.n_back, self.sink = r, B_GROUP_HEADS, window // r, False
            self.qw, self.kw = GB_W, GB_W
            self.qidx = lambda j: 3 * j
            self.kidx = lambda j: 3 * j + 1
            self.vidx = lambda j: 3 * j + 2
        self.ntile = self.qw // PAIR_W


def _stack_heads(t, scale=None):
    first = _lane(t.shape) < HEAD_DIM
    z = jnp.zeros_like(t)
    if scale is not None:
        t = t * jnp.asarray(scale, t.dtype)
    return jnp.concatenate([jnp.where(first, t, z), jnp.where(first, z, t)], axis=0)


def _lse_col(t):
    return jnp.concatenate([t[:, 0:1], t[:, HEAD_DIM:HEAD_DIM + 1]], axis=0)


def _unstack_heads(v2):
    return jnp.where(_lane((BLOCK, PAIR_W)) < HEAD_DIM, v2[:BLOCK], v2[BLOCK:])


def _dup_head(t, kh):
    tf = t.astype(F32)
    keep = (_lane(t.shape) < HEAD_DIM) if kh == 0 else (_lane(t.shape) >= HEAD_DIM)
    return jnp.where(keep, tf, pltpu.roll(tf, HEAD_DIM, 1)).astype(t.dtype)


def _fold_heads(t):
    return t + pltpu.roll(t, HEAD_DIM, 1)


def _band_mask(rows, i, n_back, single):
    nkeys = BLOCK if single else 2 * BLOCK
    qi = jnp.bitwise_and(lax.broadcasted_iota(jnp.int32, (rows, nkeys), 0), BLOCK - 1)
    ki = lax.broadcasted_iota(jnp.int32, (rows, nkeys), 1)
    if single:
        return qi >= ki
    dist = qi + BLOCK - ki
    return jnp.logical_and(jnp.logical_and(dist >= 0, dist <= n_back), jnp.logical_or(ki >= BLOCK, i > 0))


def _softmax_parts(s, valid, sinkcol):
    s = jnp.where(valid, s, NEG_INF)
    m = jnp.max(s, axis=1, keepdims=True)
    if sinkcol is not None:
        m = jnp.maximum(m, sinkcol)
    p = jnp.exp(s - m)
    den = jnp.sum(p, axis=1, keepdims=True)
    es = None
    if sinkcol is not None:
        es = jnp.exp(sinkcol - m)
        den = den + es
    return p, m, den, es


_NT = (((1,), (1,)), ((), ()))
_TN = (((0,), (0,)), ((), ()))


def _rows2(prev_ref, cur_ref, cs, single=False):
    if single:
        return cur_ref[0, :, cs]
    return jnp.concatenate([prev_ref[0, :, cs], cur_ref[0, :, cs]], axis=0)


def _sink_col(sink_ref, kh, nblocks):
    return jnp.concatenate([jnp.full((BLOCK, 1), sink_ref[kh * nblocks + b], F32) for b in range(nblocks)], axis=0)


def _tile(t):
    return slice(t * PAIR_W, (t + 1) * PAIR_W)


def _attn_fwd(qkv, sinks, g, *, NB, T, name):
    geo = _Geom(g)
    r, qw, kw, ntile = geo.r, geo.qw, geo.kw, geo.ntile
    tsub = T // r
    nblk = tsub // BLOCK
    qkv3 = qkv.reshape(NB, tsub, r * VAR_W)
    out_dtype = BF16 if g is None else F32
    tiles_per_kv = ntile // A_KV_HEADS

    single = nblk == 1

    def body(q_ref, kp_ref, kc_ref, vp_ref, vc_ref, sink_ref, o_ref, l_ref):
        i = pl.program_id(2)
        if geo.sink:
            kall, vall = _rows2(kp_ref, kc_ref, _tile(0)), _rows2(vp_ref, vc_ref, _tile(0))
            tiles = [[kh * tiles_per_kv + t for t in range(tiles_per_kv)] for kh in range(A_KV_HEADS)]
            q2s = [jnp.concatenate([_stack_heads(q_ref[0, :, _tile(t)], SCALE) for t in ts], axis=0) for ts in tiles]
            kks = [_dup_head(kall, kh) for kh in range(A_KV_HEADS)]
            vvs = [_dup_head(vall, kh) for kh in range(A_KV_HEADS)]
            sinkcols = [_sink_col(sink_ref, kh, 2 * tiles_per_kv) for kh in range(A_KV_HEADS)]
        else:
            tiles = [[t] for t in range(ntile)]
            q2s = [_stack_heads(q_ref[0, :, _tile(t)], SCALE) for t in range(ntile)]
            kks = [_rows2(kp_ref, kc_ref, _tile(t), single) for t in range(ntile)]
            vvs = [_rows2(vp_ref, vc_ref, _tile(t), single) for t in range(ntile)]
            sinkcols = [None] * ntile
        valid = _band_mask(q2s[0].shape[0], i, geo.n_back, single)
        ss = [lax.dot_general(q2, kk, _NT, preferred_element_type=F32) for q2, kk in zip(q2s, kks)]
        parts = [_softmax_parts(s, valid, sc) for s, sc in zip(ss, sinkcols)]
        o2s = [jnp.dot(p.astype(BF16), vv, preferred_element_type=F32) / den for (p, m, den, _), vv in zip(parts, vvs)]
        for ts, o2, (p, m, den, _) in zip(tiles, o2s, parts):
            lse2 = jnp.broadcast_to(m + jnp.log(den), (o2.shape[0], PAIR_W))
            for n, t in enumerate(ts):
                rows = slice(2 * BLOCK * n, 2 * BLOCK * (n + 1))
                o_ref[0, :, _tile(t)] = _unstack_heads(o2[rows]).astype(out_dtype)
                l_ref[0, :, _tile(t)] = _unstack_heads(lse2[rows])

    prev = lambda i: jnp.maximum(i - 1, 0)
    in_specs = [
        pl.BlockSpec((1, BLOCK, qw), lambda b, j, i: (b, i, geo.qidx(j))),
        pl.BlockSpec((1, BLOCK, kw), lambda b, j, i: (b, prev(i), geo.kidx(j))),
        pl.BlockSpec((1, BLOCK, kw), lambda b, j, i: (b, i, geo.kidx(j))),
        pl.BlockSpec((1, BLOCK, kw), lambda b, j, i: (b, prev(i), geo.vidx(j))),
        pl.BlockSpec((1, BLOCK, kw), lambda b, j, i: (b, i, geo.vidx(j))),
        pl.BlockSpec(memory_space=pltpu.SMEM),
    ]
    o_spec = pl.BlockSpec((1, BLOCK, qw), lambda b, j, i: (b, i, j))
    shape = (NB, tsub, r * qw)
    o, lse = _pcall(
        body, name=name, grid=(NB, r, nblk), in_specs=in_specs, out_specs=[o_spec, o_spec],
        out_shape=[jax.ShapeDtypeStruct(shape, out_dtype), jax.ShapeDtypeStruct(shape, F32)],
        compiler_params=_params(("parallel", "parallel", "arbitrary")),
    )(qkv3, qkv3, qkv3, qkv3, qkv3, sinks)
    return o, lse


def _attn_bwd(qkv, do, lse, dlse, cosf, sins, sinks, g, *, NB, T, name):
    geo = _Geom(g)
    r, qw, kw, ntile = geo.r, geo.qw, geo.kw, geo.ntile
    tsub = T // r
    nblk = tsub // BLOCK
    view = lambda a, w: a.reshape(NB, tsub, r * w)
    has_dlse = dlse is not None
    tiles_per_kv = ntile // A_KV_HEADS

    single = nblk == 1
    krows = BLOCK if single else 2 * BLOCK
    nsteps = 1 if single else nblk + 1

    def grads(q2s, kks, vvs, do2s, i, lsecols, sinkcols, dlcols):
        valid = _band_mask(q2s[0].shape[0], i, geo.n_back, single)
        ss = [lax.dot_general(q2, kk, _NT, preferred_element_type=F32) for q2, kk in zip(q2s, kks)]
        dps = [lax.dot_general(do2, vv, _NT, preferred_element_type=F32) for do2, vv in zip(do2s, vvs)]
        ps, dss, sks = [], [], []
        for s, dp, ls, sc, dl in zip(ss, dps, lsecols, sinkcols, dlcols):
            p = jnp.exp(jnp.where(valid, s, NEG_INF) - ls)
            delta = jnp.sum(p * dp, axis=1, keepdims=True)
            sks.append(jnp.exp(sc - ls) * delta if sc is not None else None)
            if dl is not None:
                delta = delta - dl
            ps.append(p.astype(BF16))
            dss.append((p * (dp - delta)).astype(BF16))
        dq2s = [jnp.dot(ds, kk, preferred_element_type=F32) * SCALE for ds, kk in zip(dss, kks)]
        dkks = [lax.dot_general(ds, q2, _TN, preferred_element_type=F32) for ds, q2 in zip(dss, q2s)]
        dvvs = [lax.dot_general(p, do2, _TN, preferred_element_type=F32) for p, do2 in zip(ps, do2s)]
        return dq2s, dkks, dvvs, sks

    def body(*refs):
        it = iter(refs)
        q_ref, kp_ref, kc_ref, vp_ref, vc_ref, do_ref, l_ref = (next(it) for _ in range(7))
        dl_ref = next(it) if has_dlse else None
        c_ref, s_ref, sink_ref, o_ref, ds_ref, dq_s, dk_s, dv_s, car_q, car_k, car_v = (next(it) for _ in range(11))
        b, j, i = pl.program_id(0), pl.program_id(1), pl.program_id(2)

        @pl.when(jnp.logical_and(b == 0, jnp.logical_and(j == 0, i == 0)))
        def _():
            ds_ref[...] = jnp.zeros_like(ds_ref)

        def compute():
            if geo.sink:
                kall, vall = _rows2(kp_ref, kc_ref, _tile(0)), _rows2(vp_ref, vc_ref, _tile(0))
                nb = 2 * tiles_per_kv
                tiles = [[kh * tiles_per_kv + t for t in range(tiles_per_kv)] for kh in range(A_KV_HEADS)]
                cat = lambda f, ts: jnp.concatenate([f(t) for t in ts], axis=0)
                dq2s, dkks, dvvs, sks = grads(
                    [cat(lambda t: _stack_heads(q_ref[0, :, _tile(t)], SCALE), ts) for ts in tiles],
                    [_dup_head(kall, kh) for kh in range(A_KV_HEADS)],
                    [_dup_head(vall, kh) for kh in range(A_KV_HEADS)],
                    [cat(lambda t: _stack_heads(do_ref[0, :, _tile(t)]), ts) for ts in tiles], i,
                    [cat(lambda t: _lse_col(l_ref[0, :, _tile(t)]), ts) for ts in tiles],
                    [_sink_col(sink_ref, kh, nb) for kh in range(A_KV_HEADS)], [None] * A_KV_HEADS)
                lane1 = _lane((1, PAIR_W))
                dsink = jnp.zeros((1, PAIR_W), F32)
                for kh, (ts, dq2, sk) in enumerate(zip(tiles, dq2s, sks)):
                    for n, t in enumerate(ts):
                        dq_s[:, _tile(t)] = _unstack_heads(dq2[2 * BLOCK * n:2 * BLOCK * (n + 1)])
                    for bb in range(nb):
                        dsink = dsink + jnp.where(lane1 == kh * nb + bb, -jnp.sum(sk[BLOCK * bb:BLOCK * (bb + 1)]), 0.0)
                second = _lane((krows, PAIR_W)) >= HEAD_DIM
                dk_s[...] = jnp.where(second, _fold_heads(dkks[1]), _fold_heads(dkks[0]))
                dv_s[...] = jnp.where(second, _fold_heads(dvvs[1]), _fold_heads(dvvs[0]))
                ds_ref[0:1, :] += dsink
            else:
                dq2s, dkks, dvvs, _ = grads(
                    [_stack_heads(q_ref[0, :, _tile(t)], SCALE) for t in range(ntile)],
                    [_rows2(kp_ref, kc_ref, _tile(t), single) for t in range(ntile)],
                    [_rows2(vp_ref, vc_ref, _tile(t), single) for t in range(ntile)],
                    [_stack_heads(do_ref[0, :, _tile(t)]) for t in range(ntile)], i,
                    [_lse_col(l_ref[0, :, _tile(t)]) for t in range(ntile)], [None] * ntile,
                    [_lse_col(dl_ref[0, :, _tile(t)]) for t in range(ntile)])
                for t in range(ntile):
                    dq_s[:, _tile(t)] = _unstack_heads(dq2s[t])
                    dk_s[0:krows, _tile(t)] = dkks[t]
                    dv_s[0:krows, _tile(t)] = dvvs[t]

        def emit(dq, dk, dv):
            cos, sn = c_ref[0], s_ref[0]
            o_ref[0, :, 0:qw] = _unrope(dq, cos, sn).astype(BF16)
            o_ref[0, :, qw:qw + kw] = _unrope(dk, cos, sn).astype(BF16)
            o_ref[0, :, qw + kw:qw + 2 * kw] = dv.astype(BF16)
            if qw + 2 * kw < VAR_W:
                o_ref[0, :, qw + 2 * kw:VAR_W] = jnp.zeros((BLOCK, VAR_W - qw - 2 * kw), BF16)

        if single:
            compute()
            emit(dq_s[...], dk_s[0:BLOCK, :], dv_s[0:BLOCK, :])
            return

        @pl.when(i == 0)
        def _():
            car_q[...] = jnp.zeros_like(car_q)
            car_k[...] = jnp.zeros_like(car_k)
            car_v[...] = jnp.zeros_like(car_v)

        @pl.when(i == nblk)
        def _():
            dk_s[...] = jnp.zeros_like(dk_s)
            dv_s[...] = jnp.zeros_like(dv_s)

        pl.when(i < nblk)(compute)
        emit(car_q[...], car_k[...] + dk_s[0:BLOCK, :], car_v[...] + dv_s[0:BLOCK, :])
        car_q[...] = dq_s[...]
        car_k[...] = dk_s[BLOCK:2 * BLOCK, :]
        car_v[...] = dv_s[BLOCK:2 * BLOCK, :]

    cur = lambda i: jnp.minimum(i, nblk - 1)
    prv = lambda i: jnp.maximum(jnp.minimum(i, nblk - 1) - 1, 0)
    outb = lambda i: jnp.maximum(i - 1, 0)
    qrow = pl.BlockSpec((1, BLOCK, qw), lambda b, j, i: (b, cur(i), j))
    in_specs = [
        pl.BlockSpec((1, BLOCK, qw), lambda b, j, i: (b, cur(i), geo.qidx(j))),
        pl.BlockSpec((1, BLOCK, kw), lambda b, j, i: (b, prv(i), geo.kidx(j))),
        pl.BlockSpec((1, BLOCK, kw), lambda b, j, i: (b, cur(i), geo.kidx(j))),
        pl.BlockSpec((1, BLOCK, kw), lambda b, j, i: (b, prv(i), geo.vidx(j))),
        pl.BlockSpec((1, BLOCK, kw), lambda b, j, i: (b, cur(i), geo.vidx(j))),
        qrow, qrow,
    ]
    ins = [view(qkv, VAR_W)] * 5 + [view(do, qw), view(lse, qw)]
    if has_dlse:
        in_specs.append(qrow)
        ins.append(view(dlse, qw))
    in_specs += [
        pl.BlockSpec((1, BLOCK, PAIR_W), lambda b, j, i: (b, outb(i), j)),
        pl.BlockSpec((1, BLOCK, PAIR_W), lambda b, j, i: (b, outb(i), j)),
        pl.BlockSpec(memory_space=pltpu.SMEM),
    ]
    ins += [view(cosf, PAIR_W), view(sins, PAIR_W), sinks]
    scratch = [pltpu.VMEM((BLOCK, qw), F32), pltpu.VMEM((2 * BLOCK, kw), F32), pltpu.VMEM((2 * BLOCK, kw), F32),
               pltpu.VMEM((BLOCK, qw), F32), pltpu.VMEM((BLOCK, kw), F32), pltpu.VMEM((BLOCK, kw), F32)]
    dqkv, dsink = _pcall(
        body, name=name, grid=(NB, r, nsteps), in_specs=in_specs,
        out_specs=[pl.BlockSpec((1, BLOCK, VAR_W), lambda b, j, i: (b, outb(i), j)),
                   pl.BlockSpec((8, PAIR_W), lambda b, j, i: (0, 0))],
        out_shape=[jax.ShapeDtypeStruct((NB, tsub, r * VAR_W), BF16), jax.ShapeDtypeStruct((8, PAIR_W), F32)],
        scratch_shapes=scratch, compiler_params=_params(("arbitrary", "arbitrary", "arbitrary")),
    )(*ins)
    return dqkv.reshape(NB * T, VAR_W), dsink


class _Rows:
    def __init__(self, N, T, tm):
        self.N, self.tm, self.tpe, self.grid = N, tm, T // tm, (N // tm,)

    def row(self, w, col=0):
        return pl.BlockSpec((self.tm, w), lambda i: (i, col))

    def ex(self, w):
        return pl.BlockSpec((1, 1, w), lambda i: (i // self.tpe, 0, 0))

    def const(self, shape):
        return pl.BlockSpec(shape, lambda i: tuple(0 for _ in shape))

    def view(self, w, r):
        return pl.BlockSpec((None, self.tm // r, r * w), lambda i: (i // self.tpe, i % self.tpe, 0))

    def first_of_example(self):
        return pl.program_id(0) % self.tpe == 0


def _acc(ref, first, val):
    @pl.when(first)
    def _():
        ref[0] = val

    @pl.when(jnp.logical_not(first))
    def _():
        ref[0] += val


def _colsum(v):
    return jnp.sum(v, axis=0, keepdims=True)


def _ln_stats(r):
    mu = jnp.mean(r, axis=-1, keepdims=True)
    xc = r - mu
    var = jnp.mean(xc * xc, axis=-1, keepdims=True)
    rstd = lax.rsqrt(var + LN_EPS)
    return xc * rstd, rstd


def _ln_bwd(dy, xhat, rstd, gain):
    dxh = dy * gain
    return rstd * (dxh - jnp.mean(dxh, axis=-1, keepdims=True) - xhat * jnp.mean(dxh * xhat, axis=-1, keepdims=True))


def _from_view(ref, scr, r):
    if r == 1:
        return ref[...]
    rows, w = ref.shape[0], ref.shape[1] // r
    for j in range(r):
        for c in range(w // LANES):
            scr.at[c][pl.ds(j, rows, stride=r), :] = ref[:, j * w + c * LANES:j * w + (c + 1) * LANES]
    return jnp.concatenate([scr[c] for c in range(w // LANES)], axis=1)


def _to_view(val, ref, scr, r):
    if r == 1:
        ref[...] = val.astype(ref.dtype)
        return
    rows, w = ref.shape[0], ref.shape[1] // r
    for c in range(w // LANES):
        scr[c] = val[:, c * LANES:(c + 1) * LANES]
    for j in range(r):
        for c in range(w // LANES):
            ref[:, j * w + c * LANES:j * w + (c + 1) * LANES] = scr.at[c][pl.ds(j, rows, stride=r), :].astype(ref.dtype)


def _silu_parts(v):
    s = jax.nn.sigmoid(v)
    return v * s, s * (1.0 + v * (1.0 - s))


def _local_step(x, mod, positions, w_in, rest_weights, sinks, ln1_g, ln1_b, ln2_g, ln2_b, target, hook=None):
    hook = hook or (lambda event, **data: None)
    NB, T, D = x.shape
    N = NB * T
    x2 = x.reshape(N, D)
    tgt2 = target.reshape(N, D)
    shift_m, scale_m, gate_m, shift_f, scale_f, gate_f = [mod[:, None, k * D:(k + 1) * D] for k in range(6)]
    cosf, sins = _rope_tables(positions)
    col = jnp.arange(QKV_P)
    vcol = col % VAR_W
    flags = jnp.where(col < VAR_W, vcol < QA_W + KA_W, vcol < 2 * GB_W).astype(F32)[None]
    R = _Rows(N, T, _pick(T, 256))
    sds = jax.ShapeDtypeStruct
    exsum = lambda w=D: sds((NB, 1, w), F32)
    ngrp = len(B_PATTERNS)

    *qkv, u = _inproj(x2, scale_m, shift_m, w_in, cosf, sins, flags, T=T, name="inproj_qkv")
    gates = _mm(u, w_in[:, QKV_P:], name="inproj_gates")
    oa, la = _attn_fwd(qkv[0], sinks, None, NB=NB, T=T, name="attn_a_fwd")
    oa = oa.reshape(N, QA_W)
    ob_parts = [_attn_fwd(qkv[1 + g], sinks, g, NB=NB, T=T, name=f"attn_b{g}_fwd") for g in range(ngrp)]
    (o1, l1), (o2, l2), (o3, l3) = ob_parts
    w_a, w_b, w_o, w_gu, w_d = rest_weights()
    F = w_d.shape[0]
    dil = [r_ for _, r_ in B_PATTERNS]
    views = [R.view(GB_W, r_) for r_ in dil]
    tokbuf = pltpu.VMEM((GB_W // LANES, R.tm, LANES), F32)

    def merge_fwd(o1r, o2r, o3r, l1r, l2r, l3r, ob_ref, *bufs):
        os_ = [_from_view(ref, bufs[n], dil[n]) for n, ref in enumerate((o1r, o2r, o3r))]
        la, lb, lc = [_from_view(ref, bufs[3 + n], dil[n]) for n, ref in enumerate((l1r, l2r, l3r))]
        mx = jnp.maximum(jnp.maximum(la, lb), lc)
        ea, eb, ec = jnp.exp(la - mx), jnp.exp(lb - mx), jnp.exp(lc - mx)
        ob_ref[...] = ((ea * os_[0] + eb * os_[1] + ec * os_[2]) / (ea + eb + ec)).astype(BF16)

    ob = _pcall(merge_fwd, name="merge_fwd", grid=R.grid, in_specs=views + views, out_specs=R.row(GB_W),
                out_shape=sds((N, GB_W), BF16), scratch_shapes=[tokbuf] * 6,
                compiler_params=_params(("parallel",)))(o1, o2, o3, l1, l2, l3)

    ya = _mm(oa, w_a, name="branch_a")
    yb = _mm(ob, w_b, b3=True, name="branch_b")

    def gate_fwd(ya_r, yb_r, ga_r, gb_r, mg_ref):
        mg_ref[...] = (jax.nn.sigmoid(ga_r[...]) * ya_r[...] + jax.nn.sigmoid(gb_r[...]) * yb_r[...]).astype(BF16)

    merged = _pcall(gate_fwd, name="gate_fwd", grid=R.grid, in_specs=[R.row(D), R.row(D), R.row(D, 0), R.row(D, 1)],
                    out_specs=R.row(D), out_shape=sds((N, D), BF16),
                    compiler_params=_params(("parallel",)))(ya, yb, gates, gates)
    y = _mm(merged, w_o, name="out_proj")

    def norm1_fwd(x_r, y_r, gm_r, g_r, b_r, sf_r, hf_r, r1_ref, x1_ref, u2_ref):
        r1 = ALPHA * x_r[...] + (1.0 + gm_r[0]) * y_r[...]
        xhat, _ = _ln_stats(r1)
        x1 = xhat * g_r[...] + b_r[...]
        r1_ref[...] = r1
        x1_ref[...] = x1
        u2_ref[...] = (x1 * (1.0 + sf_r[0]) + hf_r[0]).astype(BF16)

    r1, x1, u2 = _pcall(
        norm1_fwd, name="norm1_fwd", grid=R.grid,
        in_specs=[R.row(D), R.row(D), R.ex(D), R.const((1, D)), R.const((1, D)), R.ex(D), R.ex(D)],
        out_specs=[R.row(D)] * 3, out_shape=[sds((N, D), F32), sds((N, D), F32), sds((N, D), BF16)],
        compiler_params=_params(("parallel",)))(x2, y, gate_m, ln1_g, ln1_b, scale_f, shift_f)

    tnf = w_gu.shape[2]
    nft = w_gu.shape[0] // 2
    tmf = _pick(N, 512)

    def ffn_up(u_r, wg_r, wu_r, hg_ref, hu_ref, a_ref):
        hg = jnp.dot(u_r[...], wg_r[...], preferred_element_type=F32)
        hu = jnp.dot(u_r[...], wu_r[...], preferred_element_type=F32)
        sl, _ = _silu_parts(hg)
        hg_ref[...] = hg.astype(BF16)
        hu_ref[...] = hu.astype(BF16)
        a_ref[...] = (sl * hu).astype(BF16)

    ftile = pl.BlockSpec((tmf, tnf), lambda j, i: (i, j))
    hg, hu, act = _pcall(
        ffn_up, name="ffn_up", grid=(nft, N // tmf),
        in_specs=[pl.BlockSpec((tmf, D), lambda j, i: (i, 0)), pl.BlockSpec((None, D, tnf), lambda j, i: (j, 0, 0)),
                  pl.BlockSpec((None, D, tnf), lambda j, i: (j + nft, 0, 0))],
        out_specs=[ftile] * 3, out_shape=[sds((N, F), BF16)] * 3,
        compiler_params=_params(("arbitrary", "parallel")))(u2, w_gu, w_gu)
    y2 = _mm(act, w_d, name="ffn_down")

    def norm2_loss_bwd(x1_r, y2_r, t_r, gf_r, g_r, b_r, dy2_ref, dx1_ref, dgf_ref, dg_ref, db_ref, loss_ref):
        first = R.first_of_example()
        y2v = y2_r[...]
        r2 = ALPHA * x1_r[...] + (1.0 + gf_r[0]) * y2v
        xhat, rstd = _ln_stats(r2)
        err = xhat * g_r[...] + b_r[...] - t_r[...]
        dx2 = err * (1.0 / D)
        dr2 = _ln_bwd(dx2, xhat, rstd, g_r[...])
        dy2_ref[...] = ((1.0 + gf_r[0]) * dr2).astype(BF16)
        dx1_ref[...] = ALPHA * dr2
        _acc(dgf_ref, first, _colsum(dr2 * y2v))
        _acc(dg_ref, first, _colsum(dx2 * xhat))
        _acc(db_ref, first, _colsum(dx2))
        part = 0.5 * jnp.sum(jnp.mean(err * err, axis=-1, keepdims=True))
        _acc(loss_ref, first, jnp.broadcast_to(part, (1, 128)))

    dy2, dx1p, dgate_f, dg2, db2, loss_p = _pcall(
        norm2_loss_bwd, name="norm2_loss_bwd", grid=R.grid,
        in_specs=[R.row(D), R.row(D), R.row(D), R.ex(D), R.const((1, D)), R.const((1, D))],
        out_specs=[R.row(D), R.row(D), R.ex(D), R.ex(D), R.ex(D), R.ex(128)],
        out_shape=[sds((N, D), BF16), sds((N, D), F32), exsum(), exsum(), exsum(), exsum(128)],
        compiler_params=_params(("arbitrary",)))(x1, y2, tgt2, gate_f, ln2_g, ln2_b)

    g_wd = _mm(act, dy2, ta=True, out_dtype=BF16, name="ffn_down_dw")

    tmd = _pick(N, 256)

    def ffn_down_dx(dy_r, wd_r, hg_r, hu_r, dh_ref):
        for t in range(nft):
            cs = slice(t * tnf, (t + 1) * tnf)
            da = lax.dot_general(dy_r[...], wd_r[cs, :], _NT, preferred_element_type=F32)
            sl, dsl = _silu_parts(hg_r[:, cs].astype(F32))
            dh_ref[:, cs] = (da * hu_r[:, cs].astype(F32) * dsl).astype(BF16)
            dh_ref[:, F + t * tnf:F + (t + 1) * tnf] = (da * sl).astype(BF16)

    rowd = lambda w_: pl.BlockSpec((tmd, w_), lambda i: (i, 0))
    dh = _pcall(
        ffn_down_dx, name="ffn_down_dx", grid=(N // tmd,),
        in_specs=[rowd(D), pl.BlockSpec((F, D), lambda i: (0, 0)), rowd(F), rowd(F)],
        out_specs=rowd(2 * F), out_shape=sds((N, 2 * F), BF16),
        compiler_params=_params(("parallel",)))(dy2, w_d, hg, hu)
    du2 = _mm(dh, w_gu, tb=True, b3=True, name="ffn_up_dx")
    g_wgu = _mm(u2, dh, ta=True, out3=w_gu.shape[0], out_dtype=BF16, name="ffn_up_dw")

    def norm1_bwd(dx1p_r, du2_r, x1_r, r1_r, y_r, sf_r, gm_r, g_r,
                  dxp_ref, dy_ref, dsf_ref, dhf_ref, dgm_ref, dg_ref, db_ref):
        first = R.first_of_example()
        du2v = du2_r[...]
        dx1 = dx1p_r[...] + du2v * (1.0 + sf_r[0])
        xhat, rstd = _ln_stats(r1_r[...])
        dr1 = _ln_bwd(dx1, xhat, rstd, g_r[...])
        dxp_ref[...] = ALPHA * dr1
        dy_ref[...] = ((1.0 + gm_r[0]) * dr1).astype(BF16)
        _acc(dsf_ref, first, _colsum(du2v * x1_r[...]))
        _acc(dhf_ref, first, _colsum(du2v))
        _acc(dgm_ref, first, _colsum(dr1 * y_r[...]))
        _acc(dg_ref, first, _colsum(dx1 * xhat))
        _acc(db_ref, first, _colsum(dx1))

    dxp, dy, dscale_f, dshift_f, dgate_m, dg1, db1 = _pcall(
        norm1_bwd, name="norm1_bwd", grid=R.grid,
        in_specs=[R.row(D)] * 5 + [R.ex(D), R.ex(D), R.const((1, D))],
        out_specs=[R.row(D), R.row(D)] + [R.ex(D)] * 5,
        out_shape=[sds((N, D), F32), sds((N, D), BF16)] + [exsum()] * 5,
        compiler_params=_params(("arbitrary",)))(dx1p, du2, x1, r1, y, scale_f, gate_m, ln1_g)

    dmerged = _mm(dy, w_o, tb=True, name="out_proj_dx")
    g_wo = _mm(merged, dy, ta=True, out_dtype=BF16, name="out_proj_dw")

    def gate_bwd(dm_r, ya_r, yb_r, ga_r, gb_r, dya_ref, dyb_ref, dg_ref):
        dm = dm_r[...]
        sa, sb = jax.nn.sigmoid(ga_r[...]), jax.nn.sigmoid(gb_r[...])
        dya_ref[...] = (dm * sa).astype(BF16)
        dyb_ref[...] = (dm * sb).astype(BF16)
        dg_ref[:, :D] = (dm * ya_r[...] * sa * (1.0 - sa)).astype(BF16)
        dg_ref[:, D:] = (dm * yb_r[...] * sb * (1.0 - sb)).astype(BF16)

    dya, dyb, dgates = _pcall(
        gate_bwd, name="gate_bwd", grid=R.grid, in_specs=[R.row(D)] * 3 + [R.row(D, 0), R.row(D, 1)],
        out_specs=[R.row(D), R.row(D), R.row(2 * D)],
        out_shape=[sds((N, D), BF16), sds((N, D), BF16), sds((N, 2 * D), BF16)],
        compiler_params=_params(("parallel",)))(dmerged, ya, yb, gates, gates)

    doa = _mm(dya, w_a, tb=True, out_dtype=BF16, name="branch_a_dx")
    g_wa = _mm(oa, dya, ta=True, out_dtype=BF16, name="branch_a_dw")
    dob = _mm(dyb, w_b, tb=True, b3=True, name="branch_b_dx")
    g_wb = _mm(ob, dyb, ta=True, out3=w_b.shape[0], out_dtype=BF16, name="branch_b_dw")
    hook("rest_grads", g_wa=g_wa, g_wb=g_wb, g_wo=g_wo, g_wgu=g_wgu, g_wd=g_wd)

    seg = (jnp.arange(GB_W)[:, None] // HEAD_DIM == jnp.arange(GB_W)[None, :] // HEAD_DIM).astype(BF16)

    def merge_bwd(dob_r, o1r, o2r, o3r, l1r, l2r, l3r, seg_r, d1, d2, d3, e1, e2, e3, *bufs):
        dob_v = dob_r[...]
        os_ = [_from_view(ref, bufs[n], dil[n]) for n, ref in enumerate((o1r, o2r, o3r))]
        la, lb, lc = [_from_view(ref, bufs[3 + n], dil[n]) for n, ref in enumerate((l1r, l2r, l3r))]
        mx = jnp.maximum(jnp.maximum(la, lb), lc)
        ea, eb, ec = jnp.exp(la - mx), jnp.exp(lb - mx), jnp.exp(lc - mx)
        inv = 1.0 / (ea + eb + ec)
        ws = [ea * inv, eb * inv, ec * inv]

        def headsum(v):
            hi = v.astype(BF16)
            r1_ = v - hi.astype(F32)
            mid = r1_.astype(BF16)
            lo = (r1_ - mid.astype(F32)).astype(BF16)
            sm = seg_r[...]
            return (jnp.dot(hi, sm, preferred_element_type=F32) + jnp.dot(mid, sm, preferred_element_type=F32)
                    + jnp.dot(lo, sm, preferred_element_type=F32))

        dws = [headsum(dob_v * o) for o in os_]
        mean = ws[0] * dws[0] + ws[1] * dws[1] + ws[2] * dws[2]
        for n, (w_, dw_, d_ref, e_ref) in enumerate(zip(ws, dws, (d1, d2, d3), (e1, e2, e3))):
            _to_view(w_ * dob_v, d_ref, bufs[6], dil[n])
            _to_view(w_ * (dw_ - mean), e_ref, bufs[7], dil[n])

    vshape = lambda r_, dt: sds((NB, T // r_, r_ * GB_W), dt)
    mb = _pcall(
        merge_bwd, name="merge_bwd", grid=R.grid, in_specs=[R.row(GB_W)] + views + views + [R.const((GB_W, GB_W))],
        out_specs=views + views, out_shape=[vshape(r_, BF16) for r_ in dil] + [vshape(r_, F32) for r_ in dil],
        scratch_shapes=[tokbuf] * 8, compiler_params=_params(("parallel",)))(dob, o1, o2, o3, l1, l2, l3, seg)
    do_b, dlse_b = mb[:3], mb[3:]
    hook("merge_bwd_done")

    dqkv_a, dsink = _attn_bwd(qkv[0], doa, la, None, cosf, sins, sinks, None, NB=NB, T=T, name="attn_a_bwd")
    hook("attn_a_bwd_done")
    dqkv = [dqkv_a]
    for g in range(ngrp):
        dqkv.append(_attn_bwd(qkv[1 + g], do_b[g], (l1, l2, l3)[g], dlse_b[g], cosf, sins, sinks, g, NB=NB, T=T,
                              name=f"attn_b{g}_bwd")[0])
        hook(f"attn_b{g}_bwd_done")

    g_win = [_mm(u, dseg, ta=True, out_dtype=BF16, name=f"inproj_dw{n}") for n, dseg in enumerate(dqkv + [dgates])]
    hook("win_grads", g_win=g_win)
    wvar = lambda v: (w_in, (D, VAR_W), (0, v))
    du = _mm_multi(dqkv[:1], [wvar(0)], name="inproj_dx0")
    hook("inproj_dx0_done")
    du = _mm_multi(dqkv[1:] + [dgates], [wvar(v) for v in range(1, N_VAR)] + [(w_in, (D, 2 * D), (0, QKV_P // (2 * D)))],
                   add=du, tm=256, name="inproj_dx1")
    hook("inproj_dx1_done")

    def x_bwd(dxp_r, du_r, x_r, sm_r, gx_ref, dsm_ref, dhm_ref):
        first = R.first_of_example()
        duv = du_r[...]
        gx_ref[...] = dxp_r[...] + duv * (1.0 + sm_r[0])
        _acc(dsm_ref, first, _colsum(duv * x_r[...]))
        _acc(dhm_ref, first, _colsum(duv))

    gx, dscale_m, dshift_m = _pcall(
        x_bwd, name="x_bwd", grid=R.grid, in_specs=[R.row(D)] * 3 + [R.ex(D)],
        out_specs=[R.row(D), R.ex(D), R.ex(D)], out_shape=[sds((N, D), F32), exsum(), exsum()],
        compiler_params=_params(("arbitrary",)))(dxp, du, x2, scale_m)
    hook("x_bwd_done")

    dmod =jnp.concatenate([dshift_m, dscale_m, dgate_m, dshift_f, dscale_f, dgate_f], axis=-1)[:, 0]
    ln_grads = jnp.concatenate([dg1, db1, dg2, db2], axis=1)
    return dict(loss=loss_p[:, 0, 0], grad_x=gx.reshape(NB, T, D), g_win=g_win, g_wa=g_wa, g_wb=g_wb, g_wo=g_wo,
                g_wgu=g_wgu, g_wd=g_wd, dmod=dmod, ln_grads=ln_grads, dsink=dsink[0, :A_Q_HEADS])


def _coords():
    return lax.axis_index("x"), lax.axis_index("y"), lax.axis_index("c")


def _allgather_small(blk, *, name):
    m_per, n = blk.shape

    def body(x_ref, out_ref, send_sems, recv_sems, local_sem):
        x, y, c = _coords()
        me, sibling = (x, y, c), (x, y, 1 - c)
        chips = [(1 - x, y), (x, 1 - y), (1 - x, 1 - y)]

        def rows(px, py, pc):
            return out_ref.at[pl.ds((4 * px + 2 * py + pc) * m_per, m_per), :]

        def copy(k, block, to, src=None):
            return pltpu.make_async_remote_copy(
                src_ref=rows(*block) if src is None else src, dst_ref=rows(*block),
                send_sem=send_sems.at[k], recv_sem=recv_sems.at[k], device_id=to, device_id_type=MESH)

        mine = pltpu.make_async_copy(x_ref, rows(*me), local_sem)
        mine.start()
        first = [copy(0, me, sibling, src=x_ref)]
        first += [copy(1 + j, me, (*chip, c), src=x_ref) for j, chip in enumerate(chips)]
        for cp in first:
            cp.start()
        passed = [copy(4 + j, (*chip, c), sibling) for j, chip in enumerate(chips)]
        for j, chip in enumerate(chips):
            copy(1 + j, (*chip, c), me).wait_recv()
            passed[j].start()
        copy(0, sibling, me).wait_recv()
        for j, chip in enumerate(chips):
            copy(4 + j, (*chip, 1 - c), me).wait_recv()
        for cp in first + passed:
            cp.wait_send()
        mine.wait()

    return _pcall(
        body, name=name, out_shape=jax.ShapeDtypeStruct((8 * m_per, n), blk.dtype),
        in_specs=[pl.BlockSpec(memory_space=pltpu.VMEM)], out_specs=pl.BlockSpec(memory_space=pltpu.VMEM),
        scratch_shapes=[pltpu.SemaphoreType.DMA((7,)), pltpu.SemaphoreType.DMA((7,)), pltpu.SemaphoreType.DMA],
        compiler_params=pltpu.CompilerParams(vmem_limit_bytes=VMEM_LIMIT_BYTES),
    )(blk)


def _exchange(srcs, dsts, plan, *, name, dst_inits=None):
    na = len(dsts)
    nrem = len(plan(0, 0, 0))

    def body(*refs):
        refs = list(refs)
        src_refs = [refs.pop(0) for _ in range(na)] if srcs is not None else None
        if dst_inits is not None:
            del refs[:na]
        dst_refs, (send_sems, recv_sems) = refs[:na], refs[na:]
        start, wait = _copies(dst_refs if src_refs is None else src_refs, dst_refs, send_sems, recv_sems, plan)
        start()
        wait()

    hbm = pl.BlockSpec(memory_space=pl.ANY)
    ins = (list(srcs) if srcs is not None else []) + (list(dst_inits) if dst_inits is not None else [])
    base = na if srcs is not None else 0
    aliases = {base + a: a for a in range(na)} if dst_inits is not None else {}
    return _pcall(
        body, name=name, out_shape=list(dsts), in_specs=[hbm] * len(ins), out_specs=[hbm] * na,
        input_output_aliases=aliases,
        scratch_shapes=[pltpu.SemaphoreType.DMA((na * nrem,)), pltpu.SemaphoreType.DMA((na * nrem,))],
    )(*ins)


def _other_chips(x, y):
    return [(1 - x, y), (x, 1 - y), (1 - x, 1 - y)]


def _round(ride, carrier, name):
    if carrier is not None:
        _RIDES.setdefault(carrier, []).append(ride)
        return
    srcs = ride.srcs() if callable(ride.srcs) else ride.srcs
    inits = ride.dst_inits() if callable(ride.dst_inits) else ride.dst_inits
    ride.out = list(_exchange(srcs, ride.dsts, ride.plan, name=name, dst_inits=inits))


class _Gather:
    def __init__(self, shards, chip, tag, carriers=(None, None)):
        def plan_ici(x, y, c):
            k = 2 * x + y
            return [((c,), (k, c), (2 * px + py, c), (px, py, c)) for px, py in _other_chips(x, y)]

        def plan_d2d(x, y, c):
            return [((2 * px + py, c), (2 * px + py, c), (2 * px + py, 1 - c), (x, y, 1 - c))
                    for px, py in _other_chips(x, y)]

        self.shards, self.chip = shards, chip
        dsts = [jax.ShapeDtypeStruct((4,) + s.shape, s.dtype) for s in shards]
        ici = _Ride(shards, dsts, plan_ici)
        self.d2d = _Ride(None, dsts, plan_d2d, dst_inits=lambda: ici.out)
        _round(ici, carriers[0], f"gather_{tag}_ici")
        _round(self.d2d, carriers[1], f"gather_{tag}_d2d")

    def result(self):
        full = [lax.dynamic_update_index_in_dim(f, s, self.chip, 0) for f, s in zip(self.d2d.out, self.shards)]
        return [f.reshape((4, 2 * f.shape[2], f.shape[3])) for f in full]


def _add_pairs(a, b, *, name):
    s, hr, wd = a.shape
    tr = _pick(hr, 600, 16)

    def body(a_ref, b_ref, o_ref):
        o_ref[...] = (a_ref[...].astype(F32) + b_ref[...].astype(F32)).astype(BF16)

    spec = pl.BlockSpec((1, tr, wd), lambda j, i: (j, i, 0))
    return _pcall(body, name=name, grid=(s, hr // tr), in_specs=[spec, spec], out_specs=spec,
                  out_shape=jax.ShapeDtypeStruct(a.shape, BF16), compiler_params=_params(("parallel", "parallel")))(a, b)


def _sum_chips(b, *, name):
    s, hr, wd = b.shape
    tr = _pick(hr, 600, 16)

    def body(b_ref, o_ref):
        acc = b_ref[0].astype(F32)
        for k in range(1, s):
            acc = acc + b_ref[k].astype(F32)
        o_ref[...] = acc

    return _pcall(body, name=name, grid=(hr // tr,), in_specs=[pl.BlockSpec((s, tr, wd), lambda i: (0, i, 0))],
                  out_specs=pl.BlockSpec((tr, wd), lambda i: (i, 0)), out_shape=jax.ShapeDtypeStruct((hr, wd), F32),
                  compiler_params=_params(("parallel",)))(b)


class _ReduceScatter:
    def __init__(self, gs, chip, ci, tag):
        self.gs, self.chip, self.ci, self.tag = gs, chip, ci, tag
        self.half_t = [jax.ShapeDtypeStruct((g.shape[0],) + g.shape[2:], BF16) for g in gs]

    def pair(self, carrier=None):
        plan = lambda x, y, c: [((slice(None), 1 - c), (), (), (x, y, 1 - c))]
        self.r1 = _Ride(self.gs, self.half_t, plan)
        _round(self.r1, carrier, f"reduce_{self.tag}_pair")

    def chips(self, carrier=None):
        def plan(x, y, c):
            k = 2 * x + y
            return [((2 * px + py,), (k,), (2 * px + py,), (px, py, c)) for px, py in _other_chips(x, y)]

        self.pairs = [_add_pairs(lax.dynamic_index_in_dim(g, self.ci, 1, keepdims=False), f,
                                 name=f"reduce_{self.tag}_pair_add{n}")
                      for n, (g, f) in enumerate(zip(self.gs, self.r1.out))]
        self.r2 = _Ride(self.pairs, self.half_t, plan)
        _round(self.r2, carrier, f"reduce_{self.tag}_chips")

    def halves(self, carrier=None):
        plan = lambda x, y, c: [((), (c,), (1 - c,), (x, y, 1 - c))]
        self.mine = []
        for n, (l, p) in enumerate(zip(self.r2.out, self.pairs)):
            own = lax.dynamic_index_in_dim(p, self.chip, 0, keepdims=False)
            self.mine.append(_sum_chips(lax.dynamic_update_index_in_dim(l, own, self.chip, 0),
                                        name=f"reduce_{self.tag}_chip_sum{n}"))
        self.r3 = _Ride(self.mine, [jax.ShapeDtypeStruct((2,) + m.shape, F32) for m in self.mine], plan)
        _round(self.r3, carrier, f"reduce_{self.tag}_halves")

    def result(self):
        return [lax.dynamic_update_index_in_dim(b, m, self.ci, 0).reshape(2 * m.shape[0], m.shape[1])
                for b, m in zip(self.r3.out, self.mine)]


def _ada_fwd(c_all, w_sh, b_sh, *, name):
    nb, d = c_all.shape
    wcols = w_sh.shape[1]
    tn = _pick(wcols, 512)

    def body(c_ref, w_ref, b_ref, o_ref, a_ref):
        cv = c_ref[...]
        act = cv * jax.nn.sigmoid(cv)
        a_ref[...] = act
        o_ref[...] = jnp.dot(act.astype(BF16), w_ref[...].astype(BF16), preferred_element_type=F32) + b_ref[...]

    return _pcall(
        body, name=name, grid=(wcols // tn,),
        in_specs=[pl.BlockSpec((nb, d), lambda j: (0, 0)), pl.BlockSpec((d, tn), lambda j: (0, j)),
                  pl.BlockSpec((1, tn), lambda j: (0, j))],
        out_specs=[pl.BlockSpec((nb, tn), lambda j: (0, j)), pl.BlockSpec((nb, d), lambda j: (0, 0))],
        out_shape=[jax.ShapeDtypeStruct((nb, wcols), F32), jax.ShapeDtypeStruct((nb, d), F32)],
        compiler_params=_params(("arbitrary",)))(c_all, w_sh, b_sh)


def _sum_devices(g, *, name):
    nd, m, w = g.shape

    def body(g_ref, o_ref):
        acc = g_ref[0]
        for k in range(1, nd):
            acc = acc + g_ref[k]
        o_ref[...] = acc

    return _pcall(body, name=name, out_shape=jax.ShapeDtypeStruct((m, w), F32),
                  compiler_params=pltpu.CompilerParams(vmem_limit_bytes=VMEM_LIMIT_BYTES))(g)


def _adamw(w, g, m, v, *, name):
    rows, cols = w.shape[-2:]
    tr = _pick(rows, max(8, (1 << 18) // cols), 8)
    c1 = 1.0 / (1.0 - ADAM_B1 ** ADAM_STEP)
    c2 = 1.0 / (1.0 - ADAM_B2 ** ADAM_STEP)

    def body(w_ref, g_ref, m_ref, v_ref, d_ref, nm_ref, nv_ref):
        gv = g_ref[...]
        nm = ADAM_B1 * m_ref[...] + (1.0 - ADAM_B1) * gv
        nv = ADAM_B2 * v_ref[...] + (1.0 - ADAM_B2) * (gv * gv)
        d_ref[...] = -ADAM_LR * ((nm * c1) / (jnp.sqrt(nv * c2) + ADAM_EPS) + ADAM_WD * w_ref[...])
        nm_ref[...] = nm
        nv_ref[...] = nv

    gspec = pl.BlockSpec((tr, cols), lambda i: (i, 0))
    spec = pl.BlockSpec((None, tr, cols), lambda i: (0, i, 0)) if w.ndim == 3 else gspec
    shp = jax.ShapeDtypeStruct(w.shape, F32)
    return _pcall(body, name=name, grid=(rows // tr,), in_specs=[spec, gspec, spec, spec], out_specs=[spec] * 3,
                  out_shape=[shp] * 3, compiler_params=_params(("parallel",)))(w, g, m, v)


def _permute_in_cols(w):
    ngrp = len(B_PATTERNS)
    qb, kb, vb = (w[:, A_W + n * QB_W:A_W + (n + 1) * QB_W] for n in range(3))
    parts = [w[:, :A_W], jnp.zeros((w.shape[0], VAR_W - A_W), w.dtype)]
    for g in range(ngrp):
        parts += [t[:, g * GB_W:(g + 1) * GB_W] for t in (qb, kb, vb)]
    return jnp.concatenate(parts + [w[:, A_W + 3 * QB_W:]], axis=1)


def _unpermute_in_grads(pieces):
    ga, groups, gg = pieces[0], pieces[1:-1], pieces[-1]
    cols = [ga[:, :A_W]]
    for n in range(3):
        cols += [gp[:, n * GB_W:(n + 1) * GB_W] for gp in groups]
    return jnp.concatenate(cols + [gg], axis=1)


def kernel(x, c, positions, w_ada, b_ada, w_in, sinks, w_branch_a, w_branch_b, w_o, ln1_g, ln1_b, w_gate_up, w_down, ln2_g, ln2_b, loss_target, m_w_ada, m_b_ada, m_w_in, m_sinks, m_w_branch_a, m_w_branch_b, m_w_o, m_ln1_g, m_ln1_b, m_w_gate_up, m_w_down, m_ln2_g, m_ln2_b, v_w_ada, v_b_ada, v_w_in, v_sinks, v_w_branch_a, v_w_branch_b, v_w_o, v_ln1_g, v_ln1_b, v_w_gate_up, v_w_down, v_ln2_g, v_ln2_b):
    xi, yi, ci = _coords()
    chip = 2 * xi + yi
    dev = 4 * xi + 2 * yi + ci
    NB, T, D = x.shape
    nchip, ndev = 4, 8
    ada_cols = w_ada.shape[2]

    c_blk = jnp.zeros((8, D), F32).at[:NB].set(c)
    c_all = _allgather_small(c_blk, name="gather_c").reshape(ndev, 8, D)[:, :NB].reshape(ndev * NB, D)
    b_sh = lax.dynamic_slice(b_ada, (0, chip * ada_cols), (1, ada_cols))
    mod_part, c_act = _ada_fwd(c_all, w_ada[0], b_sh, name="ada_fwd")
    mod_g = _allgather_small(mod_part, name="gather_mod").reshape(nchip, 2, ndev * NB, ada_cols)[:, 0]
    mod_all = jnp.transpose(mod_g, (1, 0, 2)).reshape(ndev * NB, nchip * ada_cols)
    mod = lax.dynamic_slice(mod_all, (NB * dev, 0), (NB, nchip * ada_cols))

    ra, ro, rd = w_branch_a.shape[1], w_o.shape[1], w_down.shape[1]
    rowsh = jnp.concatenate([w_branch_a[0], w_o[0], w_down[0]], axis=0)
    halves = lambda a: a.reshape(a.shape[:-2] + (2, a.shape[-2] // 2, a.shape[-1]))
    whole = lambda a: a.reshape(a.shape[:-3] + (2 * a.shape[-2], a.shape[-1]))
    shards = [halves(w.astype(BF16)) for w in (w_in[0], rowsh, w_branch_b[0], w_gate_up[0])]
    (g_in,) = _Gather(shards[:1], chip, "w_in").result()
    w_in_f = _permute_in_cols(jnp.concatenate([g_in[k] for k in range(nchip)], axis=1))
    mix = _Gather(shards[1:3], chip, "w_mix", carriers=("inproj_qkv", "attn_a_fwd"))
    ffn = _Gather(shards[3:], chip, "w_ffn", carriers=("attn_a_fwd", "attn_b0_fwd"))

    def rest_weights():
        (g_rows, w_b_f), (w_gu_f,) = mix.result(), ffn.result()
        return (g_rows[:, :ra].reshape(nchip * ra, D), w_b_f, g_rows[:, ra:ra + ro].reshape(nchip * ro, D), w_gu_f,
                g_rows[:, ra + ro:].reshape(nchip * rd, D))

    red = {}

    def hook(event, **g):
        if event == "rest_grads":
            gr_rows = jnp.concatenate([g["g_wa"].reshape(nchip, ra, D), g["g_wo"].reshape(nchip, ro, D),
                                       g["g_wd"].reshape(nchip, rd, D)], axis=1)
            red["rest"] = _ReduceScatter([halves(a) for a in (gr_rows, g["g_wb"], g["g_wgu"])], chip, ci, "rest")
            red["rest"].pair(carrier="merge_bwd")
        elif event == "merge_bwd_done":
            red["rest"].chips(carrier="attn_a_bwd")
        elif event == "attn_a_bwd_done":
            red["rest"].halves(carrier="attn_b0_bwd")
        elif event == "win_grads":
            gr_in = jnp.stack(jnp.split(_unpermute_in_grads(g["g_win"]), nchip, axis=1))
            red["w_in"] = _ReduceScatter([halves(gr_in)], chip, ci, "w_in")
            red["w_in"].pair(carrier="inproj_dx0")
        elif event == "inproj_dx0_done":
            red["w_in"].chips(carrier="inproj_dx1")
        elif event == "inproj_dx1_done":
            red["w_in"].halves(carrier="x_bwd")

    res = _local_step(x, mod, positions, w_in_f, rest_weights, sinks[0], ln1_g, ln1_b, ln2_g, ln2_b, loss_target, hook)
    (g_w_in,) = red["w_in"].result()
    g_rows_red, g_w_b, g_w_gu = red["rest"].result()
    g_w_a, g_w_o, g_w_d = g_rows_red[:ra], g_rows_red[ra:ra + ro], g_rows_red[ra + ro:]

    small_rows = 24
    misc = jnp.zeros((1, D), F32).at[0, :A_Q_HEADS].set(res["dsink"]).at[0, A_Q_HEADS].set(jnp.sum(res["loss"]))
    small = jnp.concatenate([res["dmod"].reshape(NB * 6, D), jnp.sum(res["ln_grads"], axis=0), misc,
                             jnp.zeros((small_rows - NB * 6 - 5, D), F32)], axis=0)
    small_all = _allgather_small(small, name="gather_small").reshape(ndev, small_rows, D)
    dmod_all = small_all[:, :NB * 6].reshape(ndev * NB, 6 * D)
    sums = _sum_devices(small_all, name="sum_small")
    g_b_ada = (sums[0:6] + sums[6:12]).reshape(1, 6 * D)
    g_ln1_g, g_ln1_b, g_ln2_g, g_ln2_b = (sums[12 + n][None] for n in range(4))
    g_sinks = sums[16, :A_Q_HEADS][None]
    loss = sums[16, A_Q_HEADS]
    dmod_sh = lax.dynamic_slice(dmod_all, (0, chip * ada_cols), (ndev * NB, ada_cols))
    g_w_ada = _mm(c_act, dmod_sh, ta=True, name="ada_dw")

    names = ["w_ada", "b_ada", "w_in", "sinks", "w_branch_a", "w_branch_b", "w_o", "ln1_g", "ln1_b",
             "w_gate_up", "w_down", "ln2_g", "ln2_b"]
    ws = [w_ada, b_ada, w_in, sinks, w_branch_a, w_branch_b, w_o, ln1_g, ln1_b, w_gate_up, w_down, ln2_g, ln2_b]
    ms = [m_w_ada, m_b_ada, m_w_in, m_sinks, m_w_branch_a, m_w_branch_b, m_w_o, m_ln1_g, m_ln1_b, m_w_gate_up,
          m_w_down, m_ln2_g, m_ln2_b]
    vs = [v_w_ada, v_b_ada, v_w_in, v_sinks, v_w_branch_a, v_w_branch_b, v_w_o, v_ln1_g, v_ln1_b, v_w_gate_up,
          v_w_down, v_ln2_g, v_ln2_b]
    gs = [g_w_ada, g_b_ada, g_w_in, g_sinks, g_w_a, g_w_b, g_w_o, g_ln1_g, g_ln1_b, g_w_gu, g_w_d, g_ln2_g, g_ln2_b]
    grads, deltas, new_ms, new_vs = [], [], [], []
    for name, w, g, m, v in zip(names, ws, gs, ms, vs):
        g2 = g.reshape(w.shape[-2:])
        d, nm, nv = _adamw(w, g2, m, v, name="adamw_" + name)
        grads.append(g2.reshape(w.shape))
        deltas.append(d)
        new_ms.append(nm)
        new_vs.append(nv)
    return (loss, res["grad_x"], *grads, *deltas, *new_ms, *new_vs)
```

```python
import functools

import jax
import jax.numpy as jnp
from jax import lax
from jax.experimental import pallas as pl
from jax.experimental.pallas import tpu as pltpu

F32 = jnp.float32
BF16 = jnp.bfloat16
MESH = pl.DeviceIdType.MESH

HEAD_DIM = 64
LANES = 128
PAIR_W = 2 * HEAD_DIM
BLOCK = 128
A_Q_HEADS = 16
A_KV_HEADS = 2
A_WINDOW = 128
B_PATTERNS = ((128, 1), (512, 4), (2048, 16))
B_GROUP_HEADS = 8
QA_W = A_Q_HEADS * HEAD_DIM
KA_W = A_KV_HEADS * HEAD_DIM
GB_W = B_GROUP_HEADS * HEAD_DIM
QB_W = GB_W * len(B_PATTERNS)
A_W = QA_W + 2 * KA_W
VAR_W = 3 * GB_W
N_VAR = 1 + len(B_PATTERNS)
VAR_DIL = (1,) + tuple(r for _, r in B_PATTERNS)
QKV_P = N_VAR * VAR_W
ROPE_THETA = 10000.0
LN_EPS = 1e-5
NEG_INF = -1e30
DEPTH = 1
ALPHA = (2 * DEPTH) ** 0.25
SCALE = HEAD_DIM ** -0.5

ADAM_LR, ADAM_B1, ADAM_B2, ADAM_EPS, ADAM_WD, ADAM_STEP = 0.001, 0.9, 0.999, 1e-08, 0.01, 10

VMEM_LIMIT_BYTES = 56 * 1024 * 1024
MM_TILE_BYTES = 36 * 1024 * 1024
MM_WHOLE_K = 4096


def _params(sem=None):
    return pltpu.CompilerParams(dimension_semantics=sem, vmem_limit_bytes=VMEM_LIMIT_BYTES)


_RIDES = {}


def _pcall(body, *, name, **kw):
    rides = _RIDES.pop(name, None)
    if rides is None:
        return pl.pallas_call(body, name=name, **kw)
    return _riding_call(body, rides, name=name, **kw)


def _copies(src_refs, dst_refs, send_sems, recv_sems, plan):
    x, y, c = lax.axis_index("x"), lax.axis_index("y"), lax.axis_index("c")
    remote = plan(x, y, c)
    nrem = len(remote)
    at = lambda ref, idx: ref.at[idx] if idx else ref

    def copy(a, n, landing):
        si, di, ri, peer = remote[n]
        return pltpu.make_async_remote_copy(
            src_ref=at(src_refs[a], si), dst_ref=at(dst_refs[a], ri if landing else di),
            send_sem=send_sems.at[a * nrem + n], recv_sem=recv_sems.at[a * nrem + n],
            device_id=peer, device_id_type=MESH)

    order = [(a, n) for a in range(len(dst_refs)) for n in range(nrem)]

    def start():
        for a, n in order:
            copy(a, n, False).start()

    def wait():
        for a, n in order:
            copy(a, n, True).wait_recv()
        for a, n in order:
            copy(a, n, False).wait_send()

    return start, wait


class _Ride:
    def __init__(self, srcs, dsts, plan, dst_inits=None):
        self.srcs, self.dsts, self.plan, self.dst_inits, self.out = srcs, dsts, plan, dst_inits, None


def _riding_call(body, rides, *, name, grid, in_specs, out_specs, out_shape, scratch_shapes=(), **kw):
    single = not isinstance(out_specs, (list, tuple))
    out_specs = [out_specs] if single else list(out_specs)
    out_shape = [out_shape] if single else list(out_shape)
    n_in, n_out, n_scr = len(in_specs), len(out_specs), len(scratch_shapes)
    xin, xdsts, sems, aliases, layout = [], [], [], {}, []
    for ride in rides:
        srcs = ride.srcs() if callable(ride.srcs) else ride.srcs
        inits = ride.dst_inits() if callable(ride.dst_inits) else ride.dst_inits
        na, nrem = len(ride.dsts), len(ride.plan(0, 0, 0))
        src_at = len(xin) if srcs is not None else None
        xin += list(srcs) if srcs is not None else []
        if inits is not None:
            aliases.update({n_in + len(xin) + a: n_out + len(xdsts) + a for a in range(na)})
            xin += list(inits)
        layout.append((src_at, len(xdsts), na))
        xdsts += list(ride.dsts)
        sems += [pltpu.SemaphoreType.DMA((na * nrem,)), pltpu.SemaphoreType.DMA((na * nrem,))]

    def wrapped(*refs):
        ins, xins = refs[:n_in], refs[n_in:n_in + len(xin)]
        outs = refs[n_in + len(xin):n_in + len(xin) + n_out]
        xouts = refs[n_in + len(xin) + n_out:n_in + len(xin) + n_out + len(xdsts)]
        scr = refs[n_in + len(xin) + n_out + len(xdsts):]
        rounds = []
        for k, (ride, (src_at, dst_at, na)) in enumerate(zip(rides, layout)):
            dsts = xouts[dst_at:dst_at + na]
            srcs = dsts if src_at is None else xins[src_at:src_at + na]
            rounds.append(_copies(srcs, dsts, scr[n_scr + 2 * k], scr[n_scr + 2 * k + 1], ride.plan))
        ids = [pl.program_id(a) for a in range(len(grid))]
        first = functools.reduce(jnp.logical_and, [i == 0 for i in ids])
        last = functools.reduce(jnp.logical_and, [i == g - 1 for i, g in zip(ids, grid)])

        @pl.when(first)
        def _():
            for start, _ in rounds:
                start()

        body(*ins, *outs, *scr[:n_scr])

        @pl.when(last)
        def _():
            for _, wait in rounds:
                wait()

    hbm = pl.BlockSpec(memory_space=pl.ANY)

    def run(*args):
        res = pl.pallas_call(
            wrapped, name=name, grid=grid, in_specs=list(in_specs) + [hbm] * len(xin),
            out_specs=out_specs + [hbm] * len(xdsts), out_shape=out_shape + xdsts,
            scratch_shapes=list(scratch_shapes) + sems, input_output_aliases=aliases,
            compiler_params=_params(("arbitrary",) * len(grid)),
        )(*args, *xin)
        for ride, (_, dst_at, na) in zip(rides, layout):
            ride.out = list(res[n_out + dst_at:n_out + dst_at + na])
        return res[0] if single else list(res[:n_out])

    return run


def _pick(n, target, quantum=128):
    t = (min(target, n) // quantum) * quantum
    while t >= quantum:
        if n % t == 0:
            return t
        t -= quantum
    return n


def _mm(a, b, *, name, ta=False, tb=False, b3=False, out3=0, out_dtype=F32, add=None, tm=1024, tn=1536, tk=1536):
    if ta:
        K, M = a.shape
    else:
        M, K = a.shape
    if b3 and tb:
        Nn, K2, tk = b.shape[1], b.shape[0] * b.shape[2], b.shape[2]
    elif b3:
        K2, Nn, tn = b.shape[1], b.shape[0] * b.shape[2], b.shape[2]
    elif tb:
        Nn, K2 = b.shape
    else:
        K2, Nn = b.shape
    assert K == K2, (a.shape, b.shape)
    if out3:
        tn = Nn // out3
    tm, tn, tk = _pick(M, tm), _pick(Nn, tn), _pick(K, tk)
    if not (b3 and tb) and K <= MM_WHOLE_K:
        tk = K
        fits = lambda: 4 * tk * (tm + tn) + 8 * tm * tn * (2 if add is not None else 1) <= MM_TILE_BYTES
        while not fits():
            if (tm >= tn or b3 or out3) and tm > 256:
                tm = _pick(M, tm - 128)
            elif not (b3 or out3) and tn > 256:
                tn = _pick(Nn, tn - 128)
            else:
                break
    nk = K // tk
    j_outer = K * Nn + (Nn // tn) * M * K < M * K + (M // tm) * K * Nn
    dn = (((0 if ta else 1,), (1 if tb else 0,)), ((), ()))

    def body(*refs):
        refs = list(refs)
        a_ref, b_ref = refs[:2]
        add_ref = refs[2] if add is not None else None
        o_ref = refs[3] if add is not None else refs[2]
        part = lax.dot_general(a_ref[...].astype(BF16), b_ref[...].astype(BF16), dn, preferred_element_type=F32)

        def finish(r):
            if add is not None:
                r = r + add_ref[...]
            o_ref[...] = r.astype(out_dtype)

        if nk == 1:
            finish(part)
            return
        acc = refs[-1]
        k = pl.program_id(2)

        @pl.when(k == 0)
        def _():
            acc[...] = part

        @pl.when(k > 0)
        def _():
            acc[...] += part

        @pl.when(k == nk - 1)
        def _():
            finish(acc[...])

    def spec(shape, index):
        return pl.BlockSpec(shape, (lambda j, i, k: index(i, j, k)) if j_outer else index)

    a_spec = spec((tk, tm), lambda i, j, k: (k, i)) if ta else spec((tm, tk), lambda i, j, k: (i, k))
    if b3 and tb:
        b_spec = spec((None, tn, tk), lambda i, j, k: (k, j, 0))
    elif b3:
        b_spec = spec((None, tk, tn), lambda i, j, k: (j, k, 0))
    elif tb:
        b_spec = spec((tn, tk), lambda i, j, k: (j, k))
    else:
        b_spec = spec((tk, tn), lambda i, j, k: (k, j))
    if out3:
        o_spec = spec((None, tm, tn), lambda i, j, k: (j, i, 0))
    else:
        o_spec = spec((tm, tn), lambda i, j, k: (i, j))
    ins, specs = [a, b], [a_spec, b_spec]
    if add is not None:
        ins.append(add)
        specs.append(o_spec)
    grid = (Nn // tn, M // tm, nk) if j_outer else (M // tm, Nn // tn, nk)
    return _pcall(
        body, name=name, grid=grid, in_specs=specs, out_specs=o_spec,
        out_shape=jax.ShapeDtypeStruct((out3, M, tn) if out3 else (M, Nn), out_dtype),
        scratch_shapes=[pltpu.VMEM((tm, tn), F32)] if nk > 1 else [],
        compiler_params=_params(("parallel", "parallel", "arbitrary")),
    )(*ins)


def _mm_multi(a_list, b_list, *, name, add=None, out_dtype=F32, tm=512):
    M = a_list[0].shape[0]
    tm = _pick(M, tm)
    ns = len(a_list)
    b_arrs, b_specs = [], []
    for b in b_list:
        arr, shp, idx = b if isinstance(b, tuple) else (b, b.shape, (0, 0))
        b_arrs.append(arr)
        b_specs.append(pl.BlockSpec(shp, lambda i, idx=idx: idx))
    Nn = b_specs[0].block_shape[1]
    dn = (((1,), (0,)), ((), ()))

    def body(*refs):
        a_refs, b_refs = refs[:ns], refs[ns:2 * ns]
        acc = None
        for a_ref, b_ref in zip(a_refs, b_refs):
            part = lax.dot_general(a_ref[...].astype(BF16), b_ref[...], dn, preferred_element_type=F32)
            acc = part if acc is None else acc + part
        if add is not None:
            acc = acc + refs[2 * ns][...]
        refs[-1][...] = acc.astype(out_dtype)

    o_spec = pl.BlockSpec((tm, Nn), lambda i: (i, 0))
    specs = [pl.BlockSpec((tm, a.shape[1]), lambda i: (i, 0)) for a in a_list] + b_specs
    ins = list(a_list) + b_arrs
    if add is not None:
        specs.append(o_spec)
        ins.append(add)
    return _pcall(body, name=name, grid=(M // tm,), in_specs=specs, out_specs=o_spec,
                  out_shape=jax.ShapeDtypeStruct((M, Nn), out_dtype), compiler_params=_params(("parallel",)))(*ins)


def _lane(shape):
    return lax.broadcasted_iota(jnp.int32, shape, len(shape) - 1)


def _rot_half(v):
    w = v.shape[-1]
    first = (_lane(v.shape) % HEAD_DIM) < (HEAD_DIM // 2)
    return jnp.where(first, pltpu.roll(v, w - HEAD_DIM // 2, v.ndim - 1), pltpu.roll(v, HEAD_DIM // 2, v.ndim - 1))


def _widen(t, w):
    return t if w == t.shape[-1] else jnp.concatenate([t] * (w // t.shape[-1]), axis=-1)


def _unrope(v, cos, sins):
    w = v.shape[-1]
    return v * _widen(cos, w) - _rot_half(v) * _widen(sins, w)


def _rope_tables(positions):
    half = HEAD_DIM // 2
    inv = ROPE_THETA ** (-jnp.arange(half, dtype=F32) / half)
    ang = positions.astype(F32)[..., None] * inv
    cos, sin = jnp.cos(ang), jnp.sin(ang)
    cosf = jnp.concatenate([cos, cos, cos, cos], axis=-1)
    sins = jnp.concatenate([-sin, sin, -sin, sin], axis=-1)
    n = positions.shape[0] * positions.shape[1]
    return cosf.reshape(n, PAIR_W), sins.reshape(n, PAIR_W)


def _inproj(x2, scale, shift, w, cosf, sins, flags, *, T, name):
    N, D = x2.shape
    tm, tn = _pick(T, 512), VAR_W
    tpe = T // tm

    def body(x_ref, sc_ref, sh_ref, w_ref, c_ref, s_ref, f_ref, *outs):
        o_refs, u_ref = outs[:N_VAR], outs[N_VAR]
        j = pl.program_id(1)

        @pl.when(j == 0)
        def _():
            u_ref[...] = (x_ref[...] * (1.0 + sc_ref[0]) + sh_ref[0]).astype(BF16)

        acc = lax.dot_general(u_ref[...], w_ref[...], (((1,), (1,)), ((), ())), preferred_element_type=F32)
        fl = f_ref[...]
        ce = 1.0 + (_widen(c_ref[...], tn) - 1.0) * fl
        se = _widen(s_ref[...], tn) * fl
        res = acc * ce + _rot_half(acc) * se
        for v in range(N_VAR):
            @pl.when(j == v)
            def _(v=v):
                _to_view(res, o_refs[v], outs[N_VAR + 1], VAR_DIL[v])

    ex = pl.BlockSpec((1, 1, D), lambda i, j: (i // tpe, 0, 0))
    tab = pl.BlockSpec((tm, PAIR_W), lambda i, j: (i, 0))
    keep = lambda w_: pl.BlockSpec((tm, w_), lambda i, j: (i, 0))
    vspec = lambda r: pl.BlockSpec((None, tm // r, r * tn), lambda i, j: (i // tpe, i % tpe, 0))
    vshape = lambda r: jax.ShapeDtypeStruct((N // T, T // r, r * tn), BF16)
    return _pcall(
        body, name=name, grid=(N // tm, N_VAR),
        in_specs=[keep(D), ex, ex, pl.BlockSpec((tn, D), lambda i, j: (j, 0)), tab, tab,
                  pl.BlockSpec((1, tn), lambda i, j: (0, j))],
        out_specs=[vspec(r) for r in VAR_DIL] + [keep(D)],
        out_shape=[vshape(r) for r in VAR_DIL] + [jax.ShapeDtypeStruct((N, D), BF16)],
        scratch_shapes=[pltpu.VMEM((tn // LANES, tm, LANES), F32)],
        compiler_params=_params(("parallel", "arbitrary")),
    )(x2, scale, shift, w, cosf, sins, flags)


class _Geom:
    def __init__(self, g):
        if g is None:
            self.r, self.nq, self.n_back, self.sink = 1, A_Q_HEADS, A_WINDOW - 1, True
            self.qw, self.kw = QA_W, KA_W
            self.qidx = lambda j: 0
            self.kidx = lambda j: QA_W // KA_W
            self.vidx = lambda j: QA_W // KA_W + 1
        else:
            window, r = B_PATTERNS[g]
            self.r, self.nq, self.n_back, self.sink = r, B_GROUP_HEADS, window // r, False
            self.qw, self.kw = GB_W, GB_W
            self.qidx = lambda j: 3 * j
            self.kidx = lambda j: 3 * j + 1
            self.vidx = lambda j: 3 * j + 2
        self.ntile = self.qw // PAIR_W


def _stack_heads(t, scale=None):
    first = _lane(t.shape) < HEAD_DIM
    z = jnp.zeros_like(t)
    if scale is not None:
        t = t * jnp.asarray(scale, t.dtype)
    return jnp.concatenate([jnp.where(first, t, z), jnp.where(first, z, t)], axis=0)


def _lse_col(t):
    return jnp.concatenate([t[:, 0:1], t[:, HEAD_DIM:HEAD_DIM + 1]], axis=0)


def _unstack_heads(v2):
    return jnp.where(_lane((BLOCK, PAIR_W)) < HEAD_DIM, v2[:BLOCK], v2[BLOCK:])


def _dup_head(t, kh):
    tf = t.astype(F32)
    keep = (_lane(t.shape) < HEAD_DIM) if kh == 0 else (_lane(t.shape) >= HEAD_DIM)
    return jnp.where(keep, tf, pltpu.roll(tf, HEAD_DIM, 1)).astype(t.dtype)


def _fold_heads(t):
    return t + pltpu.roll(t, HEAD_DIM, 1)


def _band_mask(rows, i, n_back, single):
    nkeys = BLOCK if single else 2 * BLOCK
    qi = jnp.bitwise_and(lax.broadcasted_iota(jnp.int32, (rows, nkeys), 0), BLOCK - 1)
    ki = lax.broadcasted_iota(jnp.int32, (rows, nkeys), 1)
    if single:
        return qi >= ki
    dist = qi + BLOCK - ki
    return jnp.logical_and(jnp.logical_and(dist >= 0, dist <= n_back), jnp.logical_or(ki >= BLOCK, i > 0))


def _softmax_parts(s, valid, sinkcol):
    s = jnp.where(valid, s, NEG_INF)
    m = jnp.max(s, axis=1, keepdims=True)
    if sinkcol is not None:
        m = jnp.maximum(m, sinkcol)
    p = jnp.exp(s - m)
    den = jnp.sum(p, axis=1, keepdims=True)
    es = None
    if sinkcol is not None:
        es = jnp.exp(sinkcol - m)
        den = den + es
    return p, m, den, es


_NT = (((1,), (1,)), ((), ()))
_TN = (((0,), (0,)), ((), ()))


def _rows2(prev_ref, cur_ref, cs, single=False):
    if single:
        return cur_ref[0, :, cs]
    return jnp.concatenate([prev_ref[0, :, cs], cur_ref[0, :, cs]], axis=0)


def _sink_col(sink_ref, kh, nblocks):
    return jnp.concatenate([jnp.full((BLOCK, 1), sink_ref[kh * nblocks + b], F32) for b in range(nblocks)], axis=0)


def _tile(t):
    return slice(t * PAIR_W, (t + 1) * PAIR_W)


def _attn_fwd(qkv, sinks, g, *, NB, T, name):
    geo = _Geom(g)
    r, qw, kw, ntile = geo.r, geo.qw, geo.kw, geo.ntile
    tsub = T // r
    nblk = tsub // BLOCK
    qkv3 = qkv.reshape(NB, tsub, r * VAR_W)
    out_dtype = BF16 if g is None else F32
    tiles_per_kv = ntile // A_KV_HEADS

    single = nblk == 1

    def body(q_ref, kp_ref, kc_ref, vp_ref, vc_ref, sink_ref, o_ref, l_ref):
        i = pl.program_id(2)
        if geo.sink:
            kall, vall = _rows2(kp_ref, kc_ref, _tile(0)), _rows2(vp_ref, vc_ref, _tile(0))
            tiles = [[kh * tiles_per_kv + t for t in range(tiles_per_kv)] for kh in range(A_KV_HEADS)]
            q2s = [jnp.concatenate([_stack_heads(q_ref[0, :, _tile(t)], SCALE) for t in ts], axis=0) for ts in tiles]
            kks = [_dup_head(kall, kh) for kh in range(A_KV_HEADS)]
            vvs = [_dup_head(vall, kh) for kh in range(A_KV_HEADS)]
            sinkcols = [_sink_col(sink_ref, kh, 2 * tiles_per_kv) for kh in range(A_KV_HEADS)]
        else:
            tiles = [[t] for t in range(ntile)]
            q2s = [_stack_heads(q_ref[0, :, _tile(t)], SCALE) for t in range(ntile)]
            kks = [_rows2(kp_ref, kc_ref, _tile(t), single) for t in range(ntile)]
            vvs = [_rows2(vp_ref, vc_ref, _tile(t), single) for t in range(ntile)]
            sinkcols = [None] * ntile
        valid = _band_mask(q2s[0].shape[0], i, geo.n_back, single)
        ss = [lax.dot_general(q2, kk, _NT, preferred_element_type=F32) for q2, kk in zip(q2s, kks)]
        parts = [_softmax_parts(s, valid, sc) for s, sc in zip(ss, sinkcols)]
        o2s = [jnp.dot(p.astype(BF16), vv, preferred_element_type=F32) / den for (p, m, den, _), vv in zip(parts, vvs)]
        for ts, o2, (p, m, den, _) in zip(tiles, o2s, parts):
            lse2 = jnp.broadcast_to(m + jnp.log(den), (o2.shape[0], PAIR_W))
            for n, t in enumerate(ts):
                rows = slice(2 * BLOCK * n, 2 * BLOCK * (n + 1))
                o_ref[0, :, _tile(t)] = _unstack_heads(o2[rows]).astype(out_dtype)
                l_ref[0, :, _tile(t)] = _unstack_heads(lse2[rows])

    prev = lambda i: jnp.maximum(i - 1, 0)
    in_specs = [
        pl.BlockSpec((1, BLOCK, qw), lambda b, j, i: (b, i, geo.qidx(j))),
        pl.BlockSpec((1, BLOCK, kw), lambda b, j, i: (b, prev(i), geo.kidx(j))),
        pl.BlockSpec((1, BLOCK, kw), lambda b, j, i: (b, i, geo.kidx(j))),
        pl.BlockSpec((1, BLOCK, kw), lambda b, j, i: (b, prev(i), geo.vidx(j))),
        pl.BlockSpec((1, BLOCK, kw), lambda b, j, i: (b, i, geo.vidx(j))),
        pl.BlockSpec(memory_space=pltpu.SMEM),
    ]
    o_spec = pl.BlockSpec((1, BLOCK, qw), lambda b, j, i: (b, i, j))
    shape = (NB, tsub, r * qw)
    o, lse = _pcall(
        body, name=name, grid=(NB, r, nblk), in_specs=in_specs, out_specs=[o_spec, o_spec],
        out_shape=[jax.ShapeDtypeStruct(shape, out_dtype), jax.ShapeDtypeStruct(shape, F32)],
        compiler_params=_params(("parallel", "parallel", "arbitrary")),
    )(qkv3, qkv3, qkv3, qkv3, qkv3, sinks)
    return o, lse


def _attn_bwd(qkv, do, lse, dlse, cosf, sins, sinks, g, *, NB, T, name):
    geo = _Geom(g)
    r, qw, kw, ntile = geo.r, geo.qw, geo.kw, geo.ntile
    tsub = T // r
    nblk = tsub // BLOCK
    view = lambda a, w: a.reshape(NB, tsub, r * w)
    has_dlse = dlse is not None
    tiles_per_kv = ntile // A_KV_HEADS

    single = nblk == 1
    krows = BLOCK if single else 2 * BLOCK
    nsteps = 1 if single else nblk + 1

    def grads(q2s, kks, vvs, do2s, i, lsecols, sinkcols, dlcols):
        valid = _band_mask(q2s[0].shape[0], i, geo.n_back, single)
        ss = [lax.dot_general(q2, kk, _NT, preferred_element_type=F32) for q2, kk in zip(q2s, kks)]
        dps = [lax.dot_general(do2, vv, _NT, preferred_element_type=F32) for do2, vv in zip(do2s, vvs)]
        ps, dss, sks = [], [], []
        for s, dp, ls, sc, dl in zip(ss, dps, lsecols, sinkcols, dlcols):
            p = jnp.exp(jnp.where(valid, s, NEG_INF) - ls)
            delta = jnp.sum(p * dp, axis=1, keepdims=True)
            sks.append(jnp.exp(sc - ls) * delta if sc is not None else None)
            if dl is not None:
                delta = delta - dl
            ps.append(p.astype(BF16))
            dss.append((p * (dp - delta)).astype(BF16))
        dq2s = [jnp.dot(ds, kk, preferred_element_type=F32) * SCALE for ds, kk in zip(dss, kks)]
        dkks = [lax.dot_general(ds, q2, _TN, preferred_element_type=F32) for ds, q2 in zip(dss, q2s)]
        dvvs = [lax.dot_general(p, do2, _TN, preferred_element_type=F32) for p, do2 in zip(ps, do2s)]
        return dq2s, dkks, dvvs, sks

    def body(*refs):
        it = iter(refs)
        q_ref, kp_ref, kc_ref, vp_ref, vc_ref, do_ref, l_ref = (next(it) for _ in range(7))
        dl_ref = next(it) if has_dlse else None
        c_ref, s_ref, sink_ref, o_ref, ds_ref, dq_s, dk_s, dv_s, car_q, car_k, car_v = (next(it) for _ in range(11))
        b, j, i = pl.program_id(0), pl.program_id(1), pl.program_id(2)

        @pl.when(jnp.logical_and(b == 0, jnp.logical_and(j == 0, i == 0)))
        def _():
            ds_ref[...] = jnp.zeros_like(ds_ref)

        def compute():
            if geo.sink:
                kall, vall = _rows2(kp_ref, kc_ref, _tile(0)), _rows2(vp_ref, vc_ref, _tile(0))
                nb = 2 * tiles_per_kv
                tiles = [[kh * tiles_per_kv + t for t in range(tiles_per_kv)] for kh in range(A_KV_HEADS)]
                cat = lambda f, ts: jnp.concatenate([f(t) for t in ts], axis=0)
                dq2s, dkks, dvvs, sks = grads(
                    [cat(lambda t: _stack_heads(q_ref[0, :, _tile(t)], SCALE), ts) for ts in tiles],
                    [_dup_head(kall, kh) for kh in range(A_KV_HEADS)],
                    [_dup_head(vall, kh) for kh in range(A_KV_HEADS)],
                    [cat(lambda t: _stack_heads(do_ref[0, :, _tile(t)]), ts) for ts in tiles], i,
                    [cat(lambda t: _lse_col(l_ref[0, :, _tile(t)]), ts) for ts in tiles],
                    [_sink_col(sink_ref, kh, nb) for kh in range(A_KV_HEADS)], [None] * A_KV_HEADS)
                lane1 = _lane((1, PAIR_W))
                dsink = jnp.zeros((1, PAIR_W), F32)
                for kh, (ts, dq2, sk) in enumerate(zip(tiles, dq2s, sks)):
                    for n, t in enumerate(ts):
                        dq_s[:, _tile(t)] = _unstack_heads(dq2[2 * BLOCK * n:2 * BLOCK * (n + 1)])
                    for bb in range(nb):
                        dsink = dsink + jnp.where(lane1 == kh * nb + bb, -jnp.sum(sk[BLOCK * bb:BLOCK * (bb + 1)]), 0.0)
                second = _lane((krows, PAIR_W)) >= HEAD_DIM
                dk_s[...] = jnp.where(second, _fold_heads(dkks[1]), _fold_heads(dkks[0]))
                dv_s[...] = jnp.where(second, _fold_heads(dvvs[1]), _fold_heads(dvvs[0]))
                ds_ref[0:1, :] += dsink
            else:
                dq2s, dkks, dvvs, _ = grads(
                    [_stack_heads(q_ref[0, :, _tile(t)], SCALE) for t in range(ntile)],
                    [_rows2(kp_ref, kc_ref, _tile(t), single) for t in range(ntile)],
                    [_rows2(vp_ref, vc_ref, _tile(t), single) for t in range(ntile)],
                    [_stack_heads(do_ref[0, :, _tile(t)]) for t in range(ntile)], i,
                    [_lse_col(l_ref[0, :, _tile(t)]) for t in range(ntile)], [None] * ntile,
                    [_lse_col(dl_ref[0, :, _tile(t)]) for t in range(ntile)])
                for t in range(ntile):
                    dq_s[:, _tile(t)] = _unstack_heads(dq2s[t])
                    dk_s[0:krows, _tile(t)] = dkks[t]
                    dv_s[0:krows, _tile(t)] = dvvs[t]

        def emit(dq, dk, dv):
            cos, sn = c_ref[0], s_ref[0]
            o_ref[0, :, 0:qw] = _unrope(dq, cos, sn).astype(BF16)
            o_ref[0, :, qw:qw + kw] = _unrope(dk, cos, sn).astype(BF16)
            o_ref[0, :, qw + kw:qw + 2 * kw] = dv.astype(BF16)
            if qw + 2 * kw < VAR_W:
                o_ref[0, :, qw + 2 * kw:VAR_W] = jnp.zeros((BLOCK, VAR_W - qw - 2 * kw), BF16)

        if single:
            compute()
            emit(dq_s[...], dk_s[0:BLOCK, :], dv_s[0:BLOCK, :])
            return

        @pl.when(i == 0)
        def _():
            car_q[...] = jnp.zeros_like(car_q)
            car_k[...] = jnp.zeros_like(car_k)
            car_v[...] = jnp.zeros_like(car_v)

        @pl.when(i == nblk)
        def _():
            dk_s[...] = jnp.zeros_like(dk_s)
            dv_s[...] = jnp.zeros_like(dv_s)

        pl.when(i < nblk)(compute)
        emit(car_q[...], car_k[...] + dk_s[0:BLOCK, :], car_v[...] + dv_s[0:BLOCK, :])
        car_q[...] = dq_s[...]
        car_k[...] = dk_s[BLOCK:2 * BLOCK, :]
        car_v[...] = dv_s[BLOCK:2 * BLOCK, :]

    cur = lambda i: jnp.minimum(i, nblk - 1)
    prv = lambda i: jnp.maximum(jnp.minimum(i, nblk - 1) - 1, 0)
    outb = lambda i: jnp.maximum(i - 1, 0)
    qrow = pl.BlockSpec((1, BLOCK, qw), lambda b, j, i: (b, cur(i), j))
    in_specs = [
        pl.BlockSpec((1, BLOCK, qw), lambda b, j, i: (b, cur(i), geo.qidx(j))),
        pl.BlockSpec((1, BLOCK, kw), lambda b, j, i: (b, prv(i), geo.kidx(j))),
        pl.BlockSpec((1, BLOCK, kw), lambda b, j, i: (b, cur(i), geo.kidx(j))),
        pl.BlockSpec((1, BLOCK, kw), lambda b, j, i: (b, prv(i), geo.vidx(j))),
        pl.BlockSpec((1, BLOCK, kw), lambda b, j, i: (b, cur(i), geo.vidx(j))),
        qrow, qrow,
    ]
    ins = [view(qkv, VAR_W)] * 5 + [view(do, qw), view(lse, qw)]
    if has_dlse:
        in_specs.append(qrow)
        ins.append(view(dlse, qw))
    in_specs += [
        pl.BlockSpec((1, BLOCK, PAIR_W), lambda b, j, i: (b, outb(i), j)),
        pl.BlockSpec((1, BLOCK, PAIR_W), lambda b, j, i: (b, outb(i), j)),
        pl.BlockSpec(memory_space=pltpu.SMEM),
    ]
    ins += [view(cosf, PAIR_W), view(sins, PAIR_W), sinks]
    scratch = [pltpu.VMEM((BLOCK, qw), F32), pltpu.VMEM((2 * BLOCK, kw), F32), pltpu.VMEM((2 * BLOCK, kw), F32),
               pltpu.VMEM((BLOCK, qw), F32), pltpu.VMEM((BLOCK, kw), F32), pltpu.VMEM((BLOCK, kw), F32)]
    dqkv, dsink = _pcall(
        body, name=name, grid=(NB, r, nsteps), in_specs=in_specs,
        out_specs=[pl.BlockSpec((1, BLOCK, VAR_W), lambda b, j, i: (b, outb(i), j)),
                   pl.BlockSpec((8, PAIR_W), lambda b, j, i: (0, 0))],
        out_shape=[jax.ShapeDtypeStruct((NB, tsub, r * VAR_W), BF16), jax.ShapeDtypeStruct((8, PAIR_W), F32)],
        scratch_shapes=scratch, compiler_params=_params(("arbitrary", "arbitrary", "arbitrary")),
    )(*ins)
    return dqkv.reshape(NB * T, VAR_W), dsink


class _Rows:
    def __init__(self, N, T, tm):
        self.N, self.tm, self.tpe, self.grid = N, tm, T // tm, (N // tm,)

    def row(self, w, col=0):
        return pl.BlockSpec((self.tm, w), lambda i: (i, col))

    def ex(self, w):
        return pl.BlockSpec((1, 1, w), lambda i: (i // self.tpe, 0, 0))

    def const(self, shape):
        return pl.BlockSpec(shape, lambda i: tuple(0 for _ in shape))

    def view(self, w, r):
        return pl.BlockSpec((None, self.tm // r, r * w), lambda i: (i // self.tpe, i % self.tpe, 0))

    def first_of_example(self):
        return pl.program_id(0) % self.tpe == 0


def _acc(ref, first, val):
    @pl.when(first)
    def _():
        ref[0] = val

    @pl.when(jnp.logical_not(first))
    def _():
        ref[0] += val


def _colsum(v):
    return jnp.sum(v, axis=0, keepdims=True)


def _ln_stats(r):
    mu = jnp.mean(r, axis=-1, keepdims=True)
    xc = r - mu
    var = jnp.mean(xc * xc, axis=-1, keepdims=True)
    rstd = lax.rsqrt(var + LN_EPS)
    return xc * rstd, rstd


def _ln_bwd(dy, xhat, rstd, gain):
    dxh = dy * gain
    return rstd * (dxh - jnp.mean(dxh, axis=-1, keepdims=True) - xhat * jnp.mean(dxh * xhat, axis=-1, keepdims=True))


def _from_view(ref, scr, r):
    if r == 1:
        return ref[...]
    rows, w = ref.shape[0], ref.shape[1] // r
    for j in range(r):
        for c in range(w // LANES):
            scr.at[c][pl.ds(j, rows, stride=r), :] = ref[:, j * w + c * LANES:j * w + (c + 1) * LANES]
    return jnp.concatenate([scr[c] for c in range(w // LANES)], axis=1)


def _to_view(val, ref, scr, r):
    if r == 1:
        ref[...] = val.astype(ref.dtype)
        return
    rows, w = ref.shape[0], ref.shape[1] // r
    for c in range(w // LANES):
        scr[c] = val[:, c * LANES:(c + 1) * LANES]
    for j in range(r):
        for c in range(w // LANES):
            ref[:, j * w + c * LANES:j * w + (c + 1) * LANES] = scr.at[c][pl.ds(j, rows, stride=r), :].astype(ref.dtype)


def _silu_parts(v):
    s = jax.nn.sigmoid(v)
    return v * s, s * (1.0 + v * (1.0 - s))


def _local_step(x, mod, positions, w_in, rest_weights, sinks, ln1_g, ln1_b, ln2_g, ln2_b, target, hook=None):
    hook = hook or (lambda event, **data: None)
    NB, T, D = x.shape
    N = NB * T
    x2 = x.reshape(N, D)
    tgt2 = target.reshape(N, D)
    shift_m, scale_m, gate_m, shift_f, scale_f, gate_f = [mod[:, None, k * D:(k + 1) * D] for k in range(6)]
    cosf, sins = _rope_tables(positions)
    col = jnp.arange(QKV_P)
    vcol = col % VAR_W
    flags = jnp.where(col < VAR_W, vcol < QA_W + KA_W, vcol < 2 * GB_W).astype(F32)[None]
    R = _Rows(N, T, _pick(T, 256))
    sds = jax.ShapeDtypeStruct
    exsum = lambda w=D: sds((NB, 1, w), F32)
    ngrp = len(B_PATTERNS)

    *qkv, u = _inproj(x2, scale_m, shift_m, w_in, cosf, sins, flags, T=T, name="inproj_qkv")
    gates = _mm(u, w_in[QKV_P:], tb=True, name="inproj_gates")
    oa, la = _attn_fwd(qkv[0], sinks, None, NB=NB, T=T, name="attn_a_fwd")
    oa = oa.reshape(N, QA_W)
    ob_parts = [_attn_fwd(qkv[1 + g], sinks, g, NB=NB, T=T, name=f"attn_b{g}_fwd") for g in range(ngrp)]
    (o1, l1), (o2, l2), (o3, l3) = ob_parts
    w_a, w_b, w_o, w_gu, w_d = rest_weights()
    F = w_d.shape[0]
    dil = [r_ for _, r_ in B_PATTERNS]
    views = [R.view(GB_W, r_) for r_ in dil]
    tokbuf = pltpu.VMEM((GB_W // LANES, R.tm, LANES), F32)

    def merge_fwd(o1r, o2r, o3r, l1r, l2r, l3r, ob_ref, *bufs):
        os_ = [_from_view(ref, bufs[n], dil[n]) for n, ref in enumerate((o1r, o2r, o3r))]
        la, lb, lc = [_from_view(ref, bufs[3 + n], dil[n]) for n, ref in enumerate((l1r, l2r, l3r))]
        mx = jnp.maximum(jnp.maximum(la, lb), lc)
        ea, eb, ec = jnp.exp(la - mx), jnp.exp(lb - mx), jnp.exp(lc - mx)
        ob_ref[...] = ((ea * os_[0] + eb * os_[1] + ec * os_[2]) / (ea + eb + ec)).astype(BF16)

    ob = _pcall(merge_fwd, name="merge_fwd", grid=R.grid, in_specs=views + views, out_specs=R.row(GB_W),
                out_shape=sds((N, GB_W), BF16), scratch_shapes=[tokbuf] * 6,
                compiler_params=_params(("parallel",)))(o1, o2, o3, l1, l2, l3)

    ya = _mm(oa, w_a, name="branch_a")
    yb = _mm(ob, w_b, b3=True, name="branch_b")

    def gate_fwd(ya_r, yb_r, ga_r, gb_r, mg_ref):
        mg_ref[...] = (jax.nn.sigmoid(ga_r[...]) * ya_r[...] + jax.nn.sigmoid(gb_r[...]) * yb_r[...]).astype(BF16)

    merged = _pcall(gate_fwd, name="gate_fwd", grid=R.grid, in_specs=[R.row(D), R.row(D), R.row(D, 0), R.row(D, 1)],
                    out_specs=R.row(D), out_shape=sds((N, D), BF16),
                    compiler_params=_params(("parallel",)))(ya, yb, gates, gates)
    y = _mm(merged, w_o, name="out_proj")

    def norm1_fwd(x_r, y_r, gm_r, g_r, b_r, sf_r, hf_r, r1_ref, x1_ref, u2_ref):
        r1 = ALPHA * x_r[...] + (1.0 + gm_r[0]) * y_r[...]
        xhat, _ = _ln_stats(r1)
        x1 = xhat * g_r[...] + b_r[...]
        r1_ref[...] = r1
        x1_ref[...] = x1
        u2_ref[...] = (x1 * (1.0 + sf_r[0]) + hf_r[0]).astype(BF16)

    r1, x1, u2 = _pcall(
        norm1_fwd, name="norm1_fwd", grid=R.grid,
        in_specs=[R.row(D), R.row(D), R.ex(D), R.const((1, D)), R.const((1, D)), R.ex(D), R.ex(D)],
        out_specs=[R.row(D)] * 3, out_shape=[sds((N, D), F32), sds((N, D), F32), sds((N, D), BF16)],
        compiler_params=_params(("parallel",)))(x2, y, gate_m, ln1_g, ln1_b, scale_f, shift_f)

    tnf = w_gu.shape[2]
    nft = w_gu.shape[0] // 2
    tmf = _pick(N, 512)

    def ffn_up(u_r, wg_r, wu_r, hg_ref, hu_ref, a_ref):
        hg = jnp.dot(u_r[...], wg_r[...], preferred_element_type=F32)
        hu = jnp.dot(u_r[...], wu_r[...], preferred_element_type=F32)
        sl, _ = _silu_parts(hg)
        hg_ref[...] = hg.astype(BF16)
        hu_ref[...] = hu.astype(BF16)
        a_ref[...] = (sl * hu).astype(BF16)

    ftile = pl.BlockSpec((tmf, tnf), lambda j, i: (i, j))
    hg, hu, act = _pcall(
        ffn_up, name="ffn_up", grid=(nft, N // tmf),
        in_specs=[pl.BlockSpec((tmf, D), lambda j, i: (i, 0)), pl.BlockSpec((None, D, tnf), lambda j, i: (j, 0, 0)),
                  pl.BlockSpec((None, D, tnf), lambda j, i: (j + nft, 0, 0))],
        out_specs=[ftile] * 3, out_shape=[sds((N, F), BF16)] * 3,
        compiler_params=_params(("arbitrary", "parallel")))(u2, w_gu, w_gu)
    y2 = _mm(act, w_d, name="ffn_down")

    def norm2_loss_bwd(x1_r, y2_r, t_r, gf_r, g_r, b_r, dy2_ref, dx1_ref, dgf_ref, dg_ref, db_ref, loss_ref):
        first = R.first_of_example()
        y2v = y2_r[...]
        r2 = ALPHA * x1_r[...] + (1.0 + gf_r[0]) * y2v
        xhat, rstd = _ln_stats(r2)
        err = xhat * g_r[...] + b_r[...] - t_r[...]
        dx2 = err * (1.0 / D)
        dr2 = _ln_bwd(dx2, xhat, rstd, g_r[...])
        dy2_ref[...] = ((1.0 + gf_r[0]) * dr2).astype(BF16)
        dx1_ref[...] = ALPHA * dr2
        _acc(dgf_ref, first, _colsum(dr2 * y2v))
        _acc(dg_ref, first, _colsum(dx2 * xhat))
        _acc(db_ref, first, _colsum(dx2))
        part = 0.5 * jnp.sum(jnp.mean(err * err, axis=-1, keepdims=True))
        _acc(loss_ref, first, jnp.broadcast_to(part, (1, 128)))

    dy2, dx1p, dgate_f, dg2, db2, loss_p = _pcall(
        norm2_loss_bwd, name="norm2_loss_bwd", grid=R.grid,
        in_specs=[R.row(D), R.row(D), R.row(D), R.ex(D), R.const((1, D)), R.const((1, D))],
        out_specs=[R.row(D), R.row(D), R.ex(D), R.ex(D), R.ex(D), R.ex(128)],
        out_shape=[sds((N, D), BF16), sds((N, D), F32), exsum(), exsum(), exsum(), exsum(128)],
        compiler_params=_params(("arbitrary",)))(x1, y2, tgt2, gate_f, ln2_g, ln2_b)

    g_wd = _mm(act, dy2, ta=True, out_dtype=BF16, name="ffn_down_dw")

    tmd = _pick(N, 256)

    def ffn_down_dx(dy_r, wd_r, hg_r, hu_r, dh_ref):
        for t in range(nft):
            cs = slice(t * tnf, (t + 1) * tnf)
            da = lax.dot_general(dy_r[...], wd_r[cs, :], _NT, preferred_element_type=F32)
            sl, dsl = _silu_parts(hg_r[:, cs].astype(F32))
            dh_ref[:, cs] = (da * hu_r[:, cs].astype(F32) * dsl).astype(BF16)
            dh_ref[:, F + t * tnf:F + (t + 1) * tnf] = (da * sl).astype(BF16)

    rowd = lambda w_: pl.BlockSpec((tmd, w_), lambda i: (i, 0))
    dh = _pcall(
        ffn_down_dx, name="ffn_down_dx", grid=(N // tmd,),
        in_specs=[rowd(D), pl.BlockSpec((F, D), lambda i: (0, 0)), rowd(F), rowd(F)],
        out_specs=rowd(2 * F), out_shape=sds((N, 2 * F), BF16),
        compiler_params=_params(("parallel",)))(dy2, w_d, hg, hu)
    du2 = _mm(dh, w_gu, tb=True, b3=True, name="ffn_up_dx")
    g_wgu = _mm(u2, dh, ta=True, out3=w_gu.shape[0], out_dtype=BF16, name="ffn_up_dw")

    def norm1_bwd(dx1p_r, du2_r, x1_r, r1_r, y_r, sf_r, gm_r, g_r,
                  dxp_ref, dy_ref, dsf_ref, dhf_ref, dgm_ref, dg_ref, db_ref):
        first = R.first_of_example()
        du2v = du2_r[...]
        dx1 = dx1p_r[...] + du2v * (1.0 + sf_r[0])
        xhat, rstd = _ln_stats(r1_r[...])
        dr1 = _ln_bwd(dx1, xhat, rstd, g_r[...])
        dxp_ref[...] = ALPHA * dr1
        dy_ref[...] = ((1.0 + gm_r[0]) * dr1).astype(BF16)
        _acc(dsf_ref, first, _colsum(du2v * x1_r[...]))
        _acc(dhf_ref, first, _colsum(du2v))
        _acc(dgm_ref, first, _colsum(dr1 * y_r[...]))
        _acc(dg_ref, first, _colsum(dx1 * xhat))
        _acc(db_ref, first, _colsum(dx1))

    dxp, dy, dscale_f, dshift_f, dgate_m, dg1, db1 = _pcall(
        norm1_bwd, name="norm1_bwd", grid=R.grid,
        in_specs=[R.row(D)] * 5 + [R.ex(D), R.ex(D), R.const((1, D))],
        out_specs=[R.row(D), R.row(D)] + [R.ex(D)] * 5,
        out_shape=[sds((N, D), F32), sds((N, D), BF16)] + [exsum()] * 5,
        compiler_params=_params(("arbitrary",)))(dx1p, du2, x1, r1, y, scale_f, gate_m, ln1_g)

    dmerged = _mm(dy, w_o, tb=True, name="out_proj_dx")
    g_wo = _mm(merged, dy, ta=True, out_dtype=BF16, name="out_proj_dw")

    def gate_bwd(dm_r, ya_r, yb_r, ga_r, gb_r, dya_ref, dyb_ref, dg_ref):
        dm = dm_r[...]
        sa, sb = jax.nn.sigmoid(ga_r[...]), jax.nn.sigmoid(gb_r[...])
        dya_ref[...] = (dm * sa).astype(BF16)
        dyb_ref[...] = (dm * sb).astype(BF16)
        dg_ref[:, :D] = (dm * ya_r[...] * sa * (1.0 - sa)).astype(BF16)
        dg_ref[:, D:] = (dm * yb_r[...] * sb * (1.0 - sb)).astype(BF16)

    dya, dyb, dgates = _pcall(
        gate_bwd, name="gate_bwd", grid=R.grid, in_specs=[R.row(D)] * 3 + [R.row(D, 0), R.row(D, 1)],
        out_specs=[R.row(D), R.row(D), R.row(2 * D)],
        out_shape=[sds((N, D), BF16), sds((N, D), BF16), sds((N, 2 * D), BF16)],
        compiler_params=_params(("parallel",)))(dmerged, ya, yb, gates, gates)

    doa = _mm(dya, w_a, tb=True, out_dtype=BF16, name="branch_a_dx")
    g_wa = _mm(oa, dya, ta=True, out_dtype=BF16, name="branch_a_dw")
    dob = _mm(dyb, w_b, tb=True, b3=True, name="branch_b_dx")
    g_wb = _mm(ob, dyb, ta=True, out3=w_b.shape[0], out_dtype=BF16, name="branch_b_dw")
    hook("rest_grads", g_wa=g_wa, g_wb=g_wb, g_wo=g_wo, g_wgu=g_wgu, g_wd=g_wd)

    seg = (jnp.arange(GB_W)[:, None] // HEAD_DIM == jnp.arange(GB_W)[None, :] // HEAD_DIM).astype(BF16)

    def merge_bwd(dob_r, o1r, o2r, o3r, l1r, l2r, l3r, seg_r, d1, d2, d3, e1, e2, e3, *bufs):
        dob_v = dob_r[...]
        os_ = [_from_view(ref, bufs[n], dil[n]) for n, ref in enumerate((o1r, o2r, o3r))]
        la, lb, lc = [_from_view(ref, bufs[3 + n], dil[n]) for n, ref in enumerate((l1r, l2r, l3r))]
        mx = jnp.maximum(jnp.maximum(la, lb), lc)
        ea, eb, ec = jnp.exp(la - mx), jnp.exp(lb - mx), jnp.exp(lc - mx)
        inv = 1.0 / (ea + eb + ec)
        ws = [ea * inv, eb * inv, ec * inv]

        def headsum(v):
            hi = v.astype(BF16)
            r1_ = v - hi.astype(F32)
            mid = r1_.astype(BF16)
            lo = (r1_ - mid.astype(F32)).astype(BF16)
            sm = seg_r[...]
            return (jnp.dot(hi, sm, preferred_element_type=F32) + jnp.dot(mid, sm, preferred_element_type=F32)
                    + jnp.dot(lo, sm, preferred_element_type=F32))

        dws = [headsum(dob_v * o) for o in os_]
        mean = ws[0] * dws[0] + ws[1] * dws[1] + ws[2] * dws[2]
        for n, (w_, dw_, d_ref, e_ref) in enumerate(zip(ws, dws, (d1, d2, d3), (e1, e2, e3))):
            _to_view(w_ * dob_v, d_ref, bufs[6], dil[n])
            _to_view(w_ * (dw_ - mean), e_ref, bufs[7], dil[n])

    vshape = lambda r_, dt: sds((NB, T // r_, r_ * GB_W), dt)
    mb = _pcall(
        merge_bwd, name="merge_bwd", grid=R.grid, in_specs=[R.row(GB_W)] + views + views + [R.const((GB_W, GB_W))],
        out_specs=views + views, out_shape=[vshape(r_, BF16) for r_ in dil] + [vshape(r_, F32) for r_ in dil],
        scratch_shapes=[tokbuf] * 8, compiler_params=_params(("parallel",)))(dob, o1, o2, o3, l1, l2, l3, seg)
    do_b, dlse_b = mb[:3], mb[3:]
    hook("merge_bwd_done")

    dqkv_a, dsink = _attn_bwd(qkv[0], doa, la, None, cosf, sins, sinks, None, NB=NB, T=T, name="attn_a_bwd")
    hook("attn_a_bwd_done")
    dqkv = [dqkv_a]
    for g in range(ngrp):
        dqkv.append(_attn_bwd(qkv[1 + g], do_b[g], (l1, l2, l3)[g], dlse_b[g], cosf, sins, sinks, g, NB=NB, T=T,
                              name=f"attn_b{g}_bwd")[0])
        hook(f"attn_b{g}_bwd_done")

    g_win = [_mm(dseg, u, ta=True, out_dtype=BF16, name=f"inproj_dw{n}") for n, dseg in enumerate(dqkv + [dgates])]
    hook("win_grads", g_win=g_win)
    wvar = lambda v: (w_in, (VAR_W, D), (v, 0))
    du = _mm_multi(dqkv[:1], [wvar(0)], name="inproj_dx0")
    hook("inproj_dx0_done")
    du = _mm_multi(dqkv[1:] + [dgates], [wvar(v) for v in range(1, N_VAR)] + [(w_in, (2 * D, D), (QKV_P // (2 * D), 0))],
                   add=du, tm=256, name="inproj_dx1")
    hook("inproj_dx1_done")

    def x_bwd(dxp_r, du_r, x_r, sm_r, gx_ref, dsm_ref, dhm_ref):
        first = R.first_of_example()
        duv = du_r[...]
        gx_ref[...] = dxp_r[...] + duv * (1.0 + sm_r[0])
        _acc(dsm_ref, first, _colsum(duv * x_r[...]))
        _acc(dhm_ref, first, _colsum(duv))

    gx, dscale_m, dshift_m = _pcall(
        x_bwd, name="x_bwd", grid=R.grid, in_specs=[R.row(D)] * 3 + [R.ex(D)],
        out_specs=[R.row(D), R.ex(D), R.ex(D)], out_shape=[sds((N, D), F32), exsum(), exsum()],
        compiler_params=_params(("arbitrary",)))(dxp, du, x2, scale_m)
    hook("x_bwd_done")

    dmod =jnp.concatenate([dshift_m, dscale_m, dgate_m, dshift_f, dscale_f, dgate_f], axis=-1)[:, 0]
    ln_grads = jnp.concatenate([dg1, db1, dg2, db2], axis=1)
    return dict(loss=loss_p[:, 0, 0], grad_x=gx.reshape(NB, T, D), g_win=g_win, g_wa=g_wa, g_wb=g_wb, g_wo=g_wo,
                g_wgu=g_wgu, g_wd=g_wd, dmod=dmod, ln_grads=ln_grads, dsink=dsink[0, :A_Q_HEADS])


def _coords():
    return lax.axis_index("x"), lax.axis_index("y"), lax.axis_index("c")


def _allgather_small(blk, *, name):
    m_per, n = blk.shape

    def body(x_ref, out_ref, send_sems, recv_sems, local_sem):
        x, y, c = _coords()
        me, sibling = (x, y, c), (x, y, 1 - c)
        chips = [(1 - x, y), (x, 1 - y), (1 - x, 1 - y)]

        def rows(px, py, pc):
            return out_ref.at[pl.ds((4 * px + 2 * py + pc) * m_per, m_per), :]

        def copy(k, block, to, src=None):
            return pltpu.make_async_remote_copy(
                src_ref=rows(*block) if src is None else src, dst_ref=rows(*block),
                send_sem=send_sems.at[k], recv_sem=recv_sems.at[k], device_id=to, device_id_type=MESH)

        mine = pltpu.make_async_copy(x_ref, rows(*me), local_sem)
        mine.start()
        first = [copy(0, me, sibling, src=x_ref)]
        first += [copy(1 + j, me, (*chip, c), src=x_ref) for j, chip in enumerate(chips)]
        for cp in first:
            cp.start()
        passed = [copy(4 + j, (*chip, c), sibling) for j, chip in enumerate(chips)]
        for j, chip in enumerate(chips):
            copy(1 + j, (*chip, c), me).wait_recv()
            passed[j].start()
        copy(0, sibling, me).wait_recv()
        for j, chip in enumerate(chips):
            copy(4 + j, (*chip, 1 - c), me).wait_recv()
        for cp in first + passed:
            cp.wait_send()
        mine.wait()

    return _pcall(
        body, name=name, out_shape=jax.ShapeDtypeStruct((8 * m_per, n), blk.dtype),
        in_specs=[pl.BlockSpec(memory_space=pltpu.VMEM)], out_specs=pl.BlockSpec(memory_space=pltpu.VMEM),
        scratch_shapes=[pltpu.SemaphoreType.DMA((7,)), pltpu.SemaphoreType.DMA((7,)), pltpu.SemaphoreType.DMA],
        compiler_params=pltpu.CompilerParams(vmem_limit_bytes=VMEM_LIMIT_BYTES),
    )(blk)


def _exchange(srcs, dsts, plan, *, name, dst_inits=None):
    na = len(dsts)
    nrem = len(plan(0, 0, 0))

    def body(*refs):
        refs = list(refs)
        src_refs = [refs.pop(0) for _ in range(na)] if srcs is not None else None
        if dst_inits is not None:
            del refs[:na]
        dst_refs, (send_sems, recv_sems) = refs[:na], refs[na:]
        start, wait = _copies(dst_refs if src_refs is None else src_refs, dst_refs, send_sems, recv_sems, plan)
        start()
        wait()

    hbm = pl.BlockSpec(memory_space=pl.ANY)
    ins = (list(srcs) if srcs is not None else []) + (list(dst_inits) if dst_inits is not None else [])
    base = na if srcs is not None else 0
    aliases = {base + a: a for a in range(na)} if dst_inits is not None else {}
    return _pcall(
        body, name=name, out_shape=list(dsts), in_specs=[hbm] * len(ins), out_specs=[hbm] * na,
        input_output_aliases=aliases,
        scratch_shapes=[pltpu.SemaphoreType.DMA((na * nrem,)), pltpu.SemaphoreType.DMA((na * nrem,))],
    )(*ins)


def _other_chips(x, y):
    return [(1 - x, y), (x, 1 - y), (1 - x, 1 - y)]


def _round(ride, carrier, name):
    if carrier is not None:
        _RIDES.setdefault(carrier, []).append(ride)
        return
    srcs = ride.srcs() if callable(ride.srcs) else ride.srcs
    inits = ride.dst_inits() if callable(ride.dst_inits) else ride.dst_inits
    ride.out = list(_exchange(srcs, ride.dsts, ride.plan, name=name, dst_inits=inits))


class _Gather:
    def __init__(self, shards, chip, tag, carriers=(None, None)):
        def plan_ici(x, y, c):
            k = 2 * x + y
            return [((c,), (k, c), (2 * px + py, c), (px, py, c)) for px, py in _other_chips(x, y)]

        def plan_d2d(x, y, c):
            return [((2 * px + py, c), (2 * px + py, c), (2 * px + py, 1 - c), (x, y, 1 - c))
                    for px, py in _other_chips(x, y)]

        self.shards, self.chip = shards, chip
        dsts = [jax.ShapeDtypeStruct((4,) + s.shape, s.dtype) for s in shards]
        ici = _Ride(shards, dsts, plan_ici)
        self.d2d = _Ride(None, dsts, plan_d2d, dst_inits=lambda: ici.out)
        _round(ici, carriers[0], f"gather_{tag}_ici")
        _round(self.d2d, carriers[1], f"gather_{tag}_d2d")

    def result(self):
        full = [lax.dynamic_update_index_in_dim(f, s, self.chip, 0) for f, s in zip(self.d2d.out, self.shards)]
        return [f.reshape((4, 2 * f.shape[2], f.shape[3])) for f in full]


def _add_pairs(a, b, *, name):
    s, hr, wd = a.shape
    tr = _pick(hr, 600, 16)

    def body(a_ref, b_ref, o_ref):
        o_ref[...] = (a_ref[...].astype(F32) + b_ref[...].astype(F32)).astype(BF16)

    spec = pl.BlockSpec((1, tr, wd), lambda j, i: (j, i, 0))
    return _pcall(body, name=name, grid=(s, hr // tr), in_specs=[spec, spec], out_specs=spec,
                  out_shape=jax.ShapeDtypeStruct(a.shape, BF16), compiler_params=_params(("parallel", "parallel")))(a, b)


def _sum_chips(b, *, name):
    s, hr, wd = b.shape
    tr = _pick(hr, 600, 16)

    def body(b_ref, o_ref):
        acc = b_ref[0].astype(F32)
        for k in range(1, s):
            acc = acc + b_ref[k].astype(F32)
        o_ref[...] = acc

    return _pcall(body, name=name, grid=(hr // tr,), in_specs=[pl.BlockSpec((s, tr, wd), lambda i: (0, i, 0))],
                  out_specs=pl.BlockSpec((tr, wd), lambda i: (i, 0)), out_shape=jax.ShapeDtypeStruct((hr, wd), F32),
                  compiler_params=_params(("parallel",)))(b)


class _ReduceScatter:
    def __init__(self, gs, chip, ci, tag):
        self.gs, self.chip, self.ci, self.tag = gs, chip, ci, tag
        self.half_t = [jax.ShapeDtypeStruct((g.shape[0],) + g.shape[2:], BF16) for g in gs]

    def pair(self, carrier=None):
        plan = lambda x, y, c: [((slice(None), 1 - c), (), (), (x, y, 1 - c))]
        self.r1 = _Ride(self.gs, self.half_t, plan)
        _round(self.r1, carrier, f"reduce_{self.tag}_pair")

    def chips(self, carrier=None):
        def plan(x, y, c):
            k = 2 * x + y
            return [((2 * px + py,), (k,), (2 * px + py,), (px, py, c)) for px, py in _other_chips(x, y)]

        self.pairs = [_add_pairs(lax.dynamic_index_in_dim(g, self.ci, 1, keepdims=False), f,
                                 name=f"reduce_{self.tag}_pair_add{n}")
                      for n, (g, f) in enumerate(zip(self.gs, self.r1.out))]
        self.r2 = _Ride(self.pairs, self.half_t, plan)
        _round(self.r2, carrier, f"reduce_{self.tag}_chips")

    def halves(self, carrier=None):
        plan = lambda x, y, c: [((), (c,), (1 - c,), (x, y, 1 - c))]
        self.mine = []
        for n, (l, p) in enumerate(zip(self.r2.out, self.pairs)):
            own = lax.dynamic_index_in_dim(p, self.chip, 0, keepdims=False)
            self.mine.append(_sum_chips(lax.dynamic_update_index_in_dim(l, own, self.chip, 0),
                                        name=f"reduce_{self.tag}_chip_sum{n}"))
        self.r3 = _Ride(self.mine, [jax.ShapeDtypeStruct((2,) + m.shape, F32) for m in self.mine], plan)
        _round(self.r3, carrier, f"reduce_{self.tag}_halves")

    def result(self):
        return [lax.dynamic_update_index_in_dim(b, m, self.ci, 0).reshape(2 * m.shape[0], m.shape[1])
                for b, m in zip(self.r3.out, self.mine)]


def _ada_fwd(c_all, w_sh, b_sh, *, name):
    nb, d = c_all.shape
    wcols = w_sh.shape[1]
    tn = _pick(wcols, 512)

    def body(c_ref, w_ref, b_ref, o_ref, a_ref):
        cv = c_ref[...]
        act = cv * jax.nn.sigmoid(cv)
        a_ref[...] = act
        o_ref[...] = jnp.dot(act.astype(BF16), w_ref[...].astype(BF16), preferred_element_type=F32) + b_ref[...]

    return _pcall(
        body, name=name, grid=(wcols // tn,),
        in_specs=[pl.BlockSpec((nb, d), lambda j: (0, 0)), pl.BlockSpec((d, tn), lambda j: (0, j)),
                  pl.BlockSpec((1, tn), lambda j: (0, j))],
        out_specs=[pl.BlockSpec((nb, tn), lambda j: (0, j)), pl.BlockSpec((nb, d), lambda j: (0, 0))],
        out_shape=[jax.ShapeDtypeStruct((nb, wcols), F32), jax.ShapeDtypeStruct((nb, d), F32)],
        compiler_params=_params(("arbitrary",)))(c_all, w_sh, b_sh)


def _sum_devices(g, *, name):
    nd, m, w = g.shape

    def body(g_ref, o_ref):
        acc = g_ref[0]
        for k in range(1, nd):
            acc = acc + g_ref[k]
        o_ref[...] = acc

    return _pcall(body, name=name, out_shape=jax.ShapeDtypeStruct((m, w), F32),
                  compiler_params=pltpu.CompilerParams(vmem_limit_bytes=VMEM_LIMIT_BYTES))(g)


def _adamw(w, g, m, v, *, name):
    rows, cols = w.shape[-2:]
    tr = _pick(rows, max(8, (1 << 18) // cols), 8)
    c1 = 1.0 / (1.0 - ADAM_B1 ** ADAM_STEP)
    c2 = 1.0 / (1.0 - ADAM_B2 ** ADAM_STEP)

    def body(w_ref, g_ref, m_ref, v_ref, d_ref, nm_ref, nv_ref):
        gv = g_ref[...]
        nm = ADAM_B1 * m_ref[...] + (1.0 - ADAM_B1) * gv
        nv = ADAM_B2 * v_ref[...] + (1.0 - ADAM_B2) * (gv * gv)
        d_ref[...] = -ADAM_LR * ((nm * c1) / (jnp.sqrt(nv * c2) + ADAM_EPS) + ADAM_WD * w_ref[...])
        nm_ref[...] = nm
        nv_ref[...] = nv

    gspec = pl.BlockSpec((tr, cols), lambda i: (i, 0))
    spec = pl.BlockSpec((None, tr, cols), lambda i: (0, i, 0)) if w.ndim == 3 else gspec
    shp = jax.ShapeDtypeStruct(w.shape, F32)
    return _pcall(body, name=name, grid=(rows // tr,), in_specs=[spec, gspec, spec, spec], out_specs=[spec] * 3,
                  out_shape=[shp] * 3, compiler_params=_params(("parallel",)))(w, g, m, v)


def _permute_in_rows(wt):
    ngrp = len(B_PATTERNS)
    qb, kb, vb = (wt[A_W + n * QB_W:A_W + (n + 1) * QB_W] for n in range(3))
    parts = [wt[:A_W], jnp.zeros((VAR_W - A_W, wt.shape[1]), wt.dtype)]
    for g in range(ngrp):
        parts += [t[g * GB_W:(g + 1) * GB_W] for t in (qb, kb, vb)]
    return jnp.concatenate(parts + [wt[A_W + 3 * QB_W:]], axis=0)


def _unpermute_in_grads(pieces):
    ga, groups, gg = pieces[0], pieces[1:-1], pieces[-1]
    rows = [ga[:A_W]]
    for n in range(3):
        rows += [gp[n * GB_W:(n + 1) * GB_W] for gp in groups]
    return jnp.concatenate(rows + [gg], axis=0)


def kernel(x, c, positions, w_ada, b_ada, w_in, sinks, w_branch_a, w_branch_b, w_o, ln1_g, ln1_b, w_gate_up, w_down, ln2_g, ln2_b, loss_target, m_w_ada, m_b_ada, m_w_in, m_sinks, m_w_branch_a, m_w_branch_b, m_w_o, m_ln1_g, m_ln1_b, m_w_gate_up, m_w_down, m_ln2_g, m_ln2_b, v_w_ada, v_b_ada, v_w_in, v_sinks, v_w_branch_a, v_w_branch_b, v_w_o, v_ln1_g, v_ln1_b, v_w_gate_up, v_w_down, v_ln2_g, v_ln2_b):
    xi, yi, ci = _coords()
    chip = 2 * xi + yi
    dev = 4 * xi + 2 * yi + ci
    NB, T, D = x.shape
    nchip, ndev = 4, 8
    ada_cols = w_ada.shape[2]

    c_blk = jnp.zeros((8, D), F32).at[:NB].set(c)
    c_all = _allgather_small(c_blk, name="gather_c").reshape(ndev, 8, D)[:, :NB].reshape(ndev * NB, D)
    b_sh = lax.dynamic_slice(b_ada, (0, chip * ada_cols), (1, ada_cols))
    mod_part, c_act = _ada_fwd(c_all, w_ada[0], b_sh, name="ada_fwd")
    mod_g = _allgather_small(mod_part, name="gather_mod").reshape(nchip, 2, ndev * NB, ada_cols)[:, 0]
    mod_all = jnp.transpose(mod_g, (1, 0, 2)).reshape(ndev * NB, nchip * ada_cols)
    mod = lax.dynamic_slice(mod_all, (NB * dev, 0), (NB, nchip * ada_cols))

    ra, ro, rd = w_branch_a.shape[1], w_o.shape[1], w_down.shape[1]
    rowsh = jnp.concatenate([w_branch_a[0], w_o[0], w_down[0]], axis=0)
    halves = lambda a: a.reshape(a.shape[:-2] + (2, a.shape[-2] // 2, a.shape[-1]))
    whole = lambda a: a.reshape(a.shape[:-3] + (2 * a.shape[-2], a.shape[-1]))
    tr = lambda a: jnp.swapaxes(a, -1, -2)
    shards = [halves(w.astype(BF16)) for w in (tr(w_in[0]), rowsh, w_branch_b[0], w_gate_up[0])]
    (g_in,) = _Gather(shards[:1], chip, "w_in").result()
    w_in_f = _permute_in_rows(g_in.reshape(nchip * g_in.shape[1], D))
    mix = _Gather(shards[1:3], chip, "w_mix", carriers=("inproj_qkv", "attn_a_fwd"))
    ffn = _Gather(shards[3:], chip, "w_ffn", carriers=("attn_a_fwd", "attn_b0_fwd"))

    def rest_weights():
        (g_rows, w_b_f), (w_gu_f,) = mix.result(), ffn.result()
        return (g_rows[:, :ra].reshape(nchip * ra, D), w_b_f, g_rows[:, ra:ra + ro].reshape(nchip * ro, D), w_gu_f,
                g_rows[:, ra + ro:].reshape(nchip * rd, D))

    red = {}

    def hook(event, **g):
        if event == "rest_grads":
            gr_rows = jnp.concatenate([g["g_wa"].reshape(nchip, ra, D), g["g_wo"].reshape(nchip, ro, D),
                                       g["g_wd"].reshape(nchip, rd, D)], axis=1)
            red["rest"] = _ReduceScatter([halves(a) for a in (gr_rows, g["g_wb"], g["g_wgu"])], chip, ci, "rest")
            red["rest"].pair(carrier="merge_bwd")
        elif event == "merge_bwd_done":
            red["rest"].chips(carrier="attn_a_bwd")
        elif event == "attn_a_bwd_done":
            red["rest"].halves(carrier="attn_b0_bwd")
        elif event == "win_grads":
            gr_in = _unpermute_in_grads(g["g_win"])
            red["w_in"] = _ReduceScatter([halves(gr_in.reshape(nchip, gr_in.shape[0] // nchip, D))], chip, ci, "w_in")
            red["w_in"].pair(carrier="inproj_dx0")
        elif event == "inproj_dx0_done":
            red["w_in"].chips(carrier="inproj_dx1")
        elif event == "inproj_dx1_done":
            red["w_in"].halves(carrier="x_bwd")

    res = _local_step(x, mod, positions, w_in_f, rest_weights, sinks[0], ln1_g, ln1_b, ln2_g, ln2_b, loss_target, hook)
    (g_w_in,) = red["w_in"].result()
    g_rows_red, g_w_b, g_w_gu = red["rest"].result()
    g_w_a, g_w_o, g_w_d = g_rows_red[:ra], g_rows_red[ra:ra + ro], g_rows_red[ra + ro:]

    small_rows = 24
    misc = jnp.zeros((1, D), F32).at[0, :A_Q_HEADS].set(res["dsink"]).at[0, A_Q_HEADS].set(jnp.sum(res["loss"]))
    small = jnp.concatenate([res["dmod"].reshape(NB * 6, D), jnp.sum(res["ln_grads"], axis=0), misc,
                             jnp.zeros((small_rows - NB * 6 - 5, D), F32)], axis=0)
    small_all = _allgather_small(small, name="gather_small").reshape(ndev, small_rows, D)
    dmod_all = small_all[:, :NB * 6].reshape(ndev * NB, 6 * D)
    sums = _sum_devices(small_all, name="sum_small")
    g_b_ada = (sums[0:6] + sums[6:12]).reshape(1, 6 * D)
    g_ln1_g, g_ln1_b, g_ln2_g, g_ln2_b = (sums[12 + n][None] for n in range(4))
    g_sinks = sums[16, :A_Q_HEADS][None]
    loss = sums[16, A_Q_HEADS]
    dmod_sh = lax.dynamic_slice(dmod_all, (0, chip * ada_cols), (ndev * NB, ada_cols))
    g_w_ada = _mm(c_act, dmod_sh, ta=True, name="ada_dw")

    names = ["w_ada", "b_ada", "w_in", "sinks", "w_branch_a", "w_branch_b", "w_o", "ln1_g", "ln1_b",
             "w_gate_up", "w_down", "ln2_g", "ln2_b"]
    ws = [w_ada, b_ada, w_in, sinks, w_branch_a, w_branch_b, w_o, ln1_g, ln1_b, w_gate_up, w_down, ln2_g, ln2_b]
    ms = [m_w_ada, m_b_ada, m_w_in, m_sinks, m_w_branch_a, m_w_branch_b, m_w_o, m_ln1_g, m_ln1_b, m_w_gate_up,
          m_w_down, m_ln2_g, m_ln2_b]
    vs = [v_w_ada, v_b_ada, v_w_in, v_sinks, v_w_branch_a, v_w_branch_b, v_w_o, v_ln1_g, v_ln1_b, v_w_gate_up,
          v_w_down, v_ln2_g, v_ln2_b]
    gs = [g_w_ada, g_b_ada, g_w_in, g_sinks, g_w_a, g_w_b, g_w_o, g_ln1_g, g_ln1_b, g_w_gu, g_w_d, g_ln2_g, g_ln2_b]
    grads, deltas, new_ms, new_vs = [], [], [], []
    for name, w, g, m, v in zip(names, ws, gs, ms, vs):
        flip = tr if name == "w_in" else (lambda a: a)
        w, m, v = flip(w), flip(m), flip(v)
        g2 = g.reshape(w.shape[-2:])
        d, nm, nv = _adamw(w, g2, m, v, name="adamw_" + name)
        grads.append(flip(g2.reshape(w.shape)))
        deltas.append(flip(d))
        new_ms.append(flip(nm))
        new_vs.append(flip(nv))
    return (loss, res["grad_x"], *grads, *deltas, *new_ms, *new_vs)
```

```python
import functools

import jax
import jax.numpy as jnp
from jax import lax
from jax.experimental import pallas as pl
from jax.experimental.pallas import tpu as pltpu

F32 = jnp.float32
BF16 = jnp.bfloat16
MESH = pl.DeviceIdType.MESH

HEAD_DIM = 64
LANES = 128
PAIR_W = 2 * HEAD_DIM
BLOCK = 128
A_Q_HEADS = 16
A_KV_HEADS = 2
A_WINDOW = 128
B_PATTERNS = ((128, 1), (512, 4), (2048, 16))
B_GROUP_HEADS = 8
QA_W = A_Q_HEADS * HEAD_DIM
KA_W = A_KV_HEADS * HEAD_DIM
GB_W = B_GROUP_HEADS * HEAD_DIM
QB_W = GB_W * len(B_PATTERNS)
A_W = QA_W + 2 * KA_W
VAR_W = 3 * GB_W
N_VAR = 1 + len(B_PATTERNS)
VAR_DIL = (1,) + tuple(r for _, r in B_PATTERNS)
QKV_P = N_VAR * VAR_W
ROPE_THETA = 10000.0
LN_EPS = 1e-5
NEG_INF = -1e30
DEPTH = 1
ALPHA = (2 * DEPTH) ** 0.25
SCALE = HEAD_DIM ** -0.5

ADAM_LR, ADAM_B1, ADAM_B2, ADAM_EPS, ADAM_WD, ADAM_STEP = 0.001, 0.9, 0.999, 1e-08, 0.01, 10

VMEM_LIMIT_BYTES = 56 * 1024 * 1024
MM_TILE_BYTES = 36 * 1024 * 1024
MM_WHOLE_K = 4096


def _params(sem=None):
    return pltpu.CompilerParams(dimension_semantics=sem, vmem_limit_bytes=VMEM_LIMIT_BYTES)


_RIDES = {}


def _pcall(body, *, name, **kw):
    rides = _RIDES.pop(name, None)
    if rides is None:
        return pl.pallas_call(body, name=name, **kw)
    return _riding_call(body, rides, name=name, **kw)


def _copies(src_refs, dst_refs, send_sems, recv_sems, plan):
    x, y, c = lax.axis_index("x"), lax.axis_index("y"), lax.axis_index("c")
    remote = plan(x, y, c)
    nrem = len(remote)
    at = lambda ref, idx: ref.at[idx] if idx else ref

    def copy(a, n, landing):
        si, di, ri, peer = remote[n]
        return pltpu.make_async_remote_copy(
            src_ref=at(src_refs[a], si), dst_ref=at(dst_refs[a], ri if landing else di),
            send_sem=send_sems.at[a * nrem + n], recv_sem=recv_sems.at[a * nrem + n],
            device_id=peer, device_id_type=MESH)

    order = [(a, n) for a in range(len(dst_refs)) for n in range(nrem)]

    def start():
        for a, n in order:
            copy(a, n, False).start()

    def wait():
        for a, n in order:
            copy(a, n, True).wait_recv()
        for a, n in order:
            copy(a, n, False).wait_send()

    return start, wait


class _Ride:
    def __init__(self, srcs, dsts, plan, dst_inits=None):
        self.srcs, self.dsts, self.plan, self.dst_inits, self.out = srcs, dsts, plan, dst_inits, None


def _riding_call(body, rides, *, name, grid, in_specs, out_specs, out_shape, scratch_shapes=(), **kw):
    single = not isinstance(out_specs, (list, tuple))
    out_specs = [out_specs] if single else list(out_specs)
    out_shape = [out_shape] if single else list(out_shape)
    n_in, n_out, n_scr = len(in_specs), len(out_specs), len(scratch_shapes)
    xin, xdsts, sems, aliases, layout = [], [], [], {}, []
    for ride in rides:
        srcs = ride.srcs() if callable(ride.srcs) else ride.srcs
        inits = ride.dst_inits() if callable(ride.dst_inits) else ride.dst_inits
        na, nrem = len(ride.dsts), len(ride.plan(0, 0, 0))
        src_at = len(xin) if srcs is not None else None
        xin += list(srcs) if srcs is not None else []
        if inits is not None:
            aliases.update({n_in + len(xin) + a: n_out + len(xdsts) + a for a in range(na)})
            xin += list(inits)
        layout.append((src_at, len(xdsts), na))
        xdsts += list(ride.dsts)
        sems += [pltpu.SemaphoreType.DMA((na * nrem,)), pltpu.SemaphoreType.DMA((na * nrem,))]

    def wrapped(*refs):
        ins, xins = refs[:n_in], refs[n_in:n_in + len(xin)]
        outs = refs[n_in + len(xin):n_in + len(xin) + n_out]
        xouts = refs[n_in + len(xin) + n_out:n_in + len(xin) + n_out + len(xdsts)]
        scr = refs[n_in + len(xin) + n_out + len(xdsts):]
        rounds = []
        for k, (ride, (src_at, dst_at, na)) in enumerate(zip(rides, layout)):
            dsts = xouts[dst_at:dst_at + na]
            srcs = dsts if src_at is None else xins[src_at:src_at + na]
            rounds.append(_copies(srcs, dsts, scr[n_scr + 2 * k], scr[n_scr + 2 * k + 1], ride.plan))
        ids = [pl.program_id(a) for a in range(len(grid))]
        first = functools.reduce(jnp.logical_and, [i == 0 for i in ids])
        last = functools.reduce(jnp.logical_and, [i == g - 1 for i, g in zip(ids, grid)])

        @pl.when(first)
        def _():
            for start, _ in rounds:
                start()

        body(*ins, *outs, *scr[:n_scr])

        @pl.when(last)
        def _():
            for _, wait in rounds:
                wait()

    hbm = pl.BlockSpec(memory_space=pl.ANY)

    def run(*args):
        res = pl.pallas_call(
            wrapped, name=name, grid=grid, in_specs=list(in_specs) + [hbm] * len(xin),
            out_specs=out_specs + [hbm] * len(xdsts), out_shape=out_shape + xdsts,
            scratch_shapes=list(scratch_shapes) + sems, input_output_aliases=aliases,
            compiler_params=_params(("arbitrary",) * len(grid)),
        )(*args, *xin)
        for ride, (_, dst_at, na) in zip(rides, layout):
            ride.out = list(res[n_out + dst_at:n_out + dst_at + na])
        return res[0] if single else list(res[:n_out])

    return run


def _pick(n, target, quantum=128):
    t = (min(target, n) // quantum) * quantum
    while t >= quantum:
        if n % t == 0:
            return t
        t -= quantum
    return n


def _mm(a, b, *, name, ta=False, tb=False, b3=False, out3=0, out_dtype=F32, add=None, tm=1024, tn=1536, tk=1536):
    if ta:
        K, M = a.shape
    else:
        M, K = a.shape
    if b3 and tb:
        Nn, K2, tk = b.shape[1], b.shape[0] * b.shape[2], b.shape[2]
    elif b3:
        K2, Nn, tn = b.shape[1], b.shape[0] * b.shape[2], b.shape[2]
    elif tb:
        Nn, K2 = b.shape
    else:
        K2, Nn = b.shape
    assert K == K2, (a.shape, b.shape)
    if out3:
        tn = Nn // out3
    tm, tn, tk = _pick(M, tm), _pick(Nn, tn), _pick(K, tk)
    if not (b3 and tb) and K <= MM_WHOLE_K:
        tk = K
        fits = lambda: 4 * tk * (tm + tn) + 8 * tm * tn * (2 if add is not None else 1) <= MM_TILE_BYTES
        while not fits():
            if (tm >= tn or b3 or out3) and tm > 256:
                tm = _pick(M, tm - 128)
            elif not (b3 or out3) and tn > 256:
                tn = _pick(Nn, tn - 128)
            else:
                break
    nk = K // tk
    j_outer = K * Nn + (Nn // tn) * M * K < M * K + (M // tm) * K * Nn
    dn = (((0 if ta else 1,), (1 if tb else 0,)), ((), ()))

    def body(*refs):
        refs = list(refs)
        a_ref, b_ref = refs[:2]
        add_ref = refs[2] if add is not None else None
        o_ref = refs[3] if add is not None else refs[2]
        part = lax.dot_general(a_ref[...].astype(BF16), b_ref[...].astype(BF16), dn, preferred_element_type=F32)

        def finish(r):
            if add is not None:
                r = r + add_ref[...]
            o_ref[...] = r.astype(out_dtype)

        if nk == 1:
            finish(part)
            return
        acc = refs[-1]
        k = pl.program_id(2)

        @pl.when(k == 0)
        def _():
            acc[...] = part

        @pl.when(k > 0)
        def _():
            acc[...] += part

        @pl.when(k == nk - 1)
        def _():
            finish(acc[...])

    def spec(shape, index):
        return pl.BlockSpec(shape, (lambda j, i, k: index(i, j, k)) if j_outer else index)

    a_spec = spec((tk, tm), lambda i, j, k: (k, i)) if ta else spec((tm, tk), lambda i, j, k: (i, k))
    if b3 and tb:
        b_spec = spec((None, tn, tk), lambda i, j, k: (k, j, 0))
    elif b3:
        b_spec = spec((None, tk, tn), lambda i, j, k: (j, k, 0))
    elif tb:
        b_spec = spec((tn, tk), lambda i, j, k: (j, k))
    else:
        b_spec = spec((tk, tn), lambda i, j, k: (k, j))
    if out3:
        o_spec = spec((None, tm, tn), lambda i, j, k: (j, i, 0))
    else:
        o_spec = spec((tm, tn), lambda i, j, k: (i, j))
    ins, specs = [a, b], [a_spec, b_spec]
    if add is not None:
        ins.append(add)
        specs.append(o_spec)
    grid = (Nn // tn, M // tm, nk) if j_outer else (M // tm, Nn // tn, nk)
    return _pcall(
        body, name=name, grid=grid, in_specs=specs, out_specs=o_spec,
        out_shape=jax.ShapeDtypeStruct((out3, M, tn) if out3 else (M, Nn), out_dtype),
        scratch_shapes=[pltpu.VMEM((tm, tn), F32)] if nk > 1 else [],
        compiler_params=_params(("parallel", "parallel", "arbitrary")),
    )(*ins)


def _mm_multi(a_list, b_list, *, name, add=None, out_dtype=F32, tm=512):
    M = a_list[0].shape[0]
    tm = _pick(M, tm)
    ns = len(a_list)
    b_arrs, b_specs = [], []
    for b in b_list:
        arr, shp, idx = b if isinstance(b, tuple) else (b, b.shape, (0, 0))
        b_arrs.append(arr)
        b_specs.append(pl.BlockSpec(shp, lambda i, idx=idx: idx))
    Nn = b_specs[0].block_shape[1]
    dn = (((1,), (0,)), ((), ()))

    def body(*refs):
        a_refs, b_refs = refs[:ns], refs[ns:2 * ns]
        acc = None
        for a_ref, b_ref in zip(a_refs, b_refs):
            part = lax.dot_general(a_ref[...].astype(BF16), b_ref[...], dn, preferred_element_type=F32)
            acc = part if acc is None else acc + part
        if add is not None:
            acc = acc + refs[2 * ns][...]
        refs[-1][...] = acc.astype(out_dtype)

    o_spec = pl.BlockSpec((tm, Nn), lambda i: (i, 0))
    specs = [pl.BlockSpec((tm, a.shape[1]), lambda i: (i, 0)) for a in a_list] + b_specs
    ins = list(a_list) + b_arrs
    if add is not None:
        specs.append(o_spec)
        ins.append(add)
    return _pcall(body, name=name, grid=(M // tm,), in_specs=specs, out_specs=o_spec,
                  out_shape=jax.ShapeDtypeStruct((M, Nn), out_dtype), compiler_params=_params(("parallel",)))(*ins)


def _lane(shape):
    return lax.broadcasted_iota(jnp.int32, shape, len(shape) - 1)


def _rot_half(v):
    w = v.shape[-1]
    first = (_lane(v.shape) % HEAD_DIM) < (HEAD_DIM // 2)
    return jnp.where(first, pltpu.roll(v, w - HEAD_DIM // 2, v.ndim - 1), pltpu.roll(v, HEAD_DIM // 2, v.ndim - 1))


def _widen(t, w):
    return t if w == t.shape[-1] else jnp.concatenate([t] * (w // t.shape[-1]), axis=-1)


def _unrope(v, cos, sins):
    w = v.shape[-1]
    return v * _widen(cos, w) - _rot_half(v) * _widen(sins, w)


def _rope_tables(positions):
    half = HEAD_DIM // 2
    inv = ROPE_THETA ** (-jnp.arange(half, dtype=F32) / half)
    ang = positions.astype(F32)[..., None] * inv
    cos, sin = jnp.cos(ang), jnp.sin(ang)
    cosf = jnp.concatenate([cos, cos, cos, cos], axis=-1)
    sins = jnp.concatenate([-sin, sin, -sin, sin], axis=-1)
    n = positions.shape[0] * positions.shape[1]
    return cosf.reshape(n, PAIR_W), sins.reshape(n, PAIR_W)


def _inproj(x2, scale, shift, w, cosf, sins, flags, *, T, name):
    N, D = x2.shape
    tm, tn = _pick(T, 512), VAR_W
    tpe = T // tm

    def body(x_ref, sc_ref, sh_ref, w_ref, c_ref, s_ref, f_ref, *outs):
        o_refs, u_ref = outs[:N_VAR], outs[N_VAR]
        j = pl.program_id(1)

        @pl.when(j == 0)
        def _():
            u_ref[...] = (x_ref[...] * (1.0 + sc_ref[0]) + sh_ref[0]).astype(BF16)

        acc = lax.dot_general(u_ref[...], w_ref[...], (((1,), (1,)), ((), ())), preferred_element_type=F32)
        fl = f_ref[...]
        ce = 1.0 + (_widen(c_ref[...], tn) - 1.0) * fl
        se = _widen(s_ref[...], tn) * fl
        res = acc * ce + _rot_half(acc) * se
        for v in range(N_VAR):
            @pl.when(j == v)
            def _(v=v):
                _to_view(res, o_refs[v], outs[N_VAR + 1], VAR_DIL[v])

    ex = pl.BlockSpec((1, 1, D), lambda i, j: (i // tpe, 0, 0))
    tab = pl.BlockSpec((tm, PAIR_W), lambda i, j: (i, 0))
    keep = lambda w_: pl.BlockSpec((tm, w_), lambda i, j: (i, 0))
    vspec = lambda r: pl.BlockSpec((None, tm // r, r * tn), lambda i, j: (i // tpe, i % tpe, 0))
    vshape = lambda r: jax.ShapeDtypeStruct((N // T, T // r, r * tn), BF16)
    return _pcall(
        body, name=name, grid=(N // tm, N_VAR),
        in_specs=[keep(D), ex, ex, pl.BlockSpec((tn, D), lambda i, j: (j, 0)), tab, tab,
                  pl.BlockSpec((1, tn), lambda i, j: (0, j))],
        out_specs=[vspec(r) for r in VAR_DIL] + [keep(D)],
        out_shape=[vshape(r) for r in VAR_DIL] + [jax.ShapeDtypeStruct((N, D), BF16)],
        scratch_shapes=[pltpu.VMEM((tn // LANES, tm, LANES), F32)],
        compiler_params=_params(("parallel", "arbitrary")),
    )(x2, scale, shift, w, cosf, sins, flags)


class _Geom:
    def __init__(self, g):
        if g is None:
            self.r, self.nq, self.n_back, self.sink = 1, A_Q_HEADS, A_WINDOW - 1, True
            self.qw, self.kw = QA_W, KA_W
            self.qidx = lambda j: 0
            self.kidx = lambda j: QA_W // KA_W
            self.vidx = lambda j: QA_W // KA_W + 1
        else:
            window, r = B_PATTERNS[g]
            self.r, self.nq, self.n_back, self.sink = r, B_GROUP_HEADS, window // r, False
            self.qw, self.kw = GB_W, GB_W
            self.qidx = lambda j: 3 * j
            self.kidx = lambda j: 3 * j + 1
            self.vidx = lambda j: 3 * j + 2
        self.ntile = self.qw // PAIR_W


def _stack_heads(t, scale=None):
    first = _lane(t.shape) < HEAD_DIM
    z = jnp.zeros_like(t)
    if scale is not None:
        t = t * jnp.asarray(scale, t.dtype)
    return jnp.concatenate([jnp.where(first, t, z), jnp.where(first, z, t)], axis=0)


def _lse_col(t):
    return jnp.concatenate([t[:, 0:1], t[:, HEAD_DIM:HEAD_DIM + 1]], axis=0)


def _unstack_heads(v2):
    return jnp.where(_lane((BLOCK, PAIR_W)) < HEAD_DIM, v2[:BLOCK], v2[BLOCK:])


def _dup_head(t, kh):
    tf = t.astype(F32)
    keep = (_lane(t.shape) < HEAD_DIM) if kh == 0 else (_lane(t.shape) >= HEAD_DIM)
    return jnp.where(keep, tf, pltpu.roll(tf, HEAD_DIM, 1)).astype(t.dtype)


def _fold_heads(t):
    return t + pltpu.roll(t, HEAD_DIM, 1)


def _band_mask(rows, i, n_back, single):
    nkeys = BLOCK if single else 2 * BLOCK
    qi = jnp.bitwise_and(lax.broadcasted_iota(jnp.int32, (rows, nkeys), 0), BLOCK - 1)
    ki = lax.broadcasted_iota(jnp.int32, (rows, nkeys), 1)
    if single:
        return qi >= ki
    dist = qi + BLOCK - ki
    return jnp.logical_and(jnp.logical_and(dist >= 0, dist <= n_back), jnp.logical_or(ki >= BLOCK, i > 0))


def _per_block(col, scalars, fn):
    return jnp.concatenate([fn(col[b * BLOCK:(b + 1) * BLOCK], sc) for b, sc in enumerate(scalars)], axis=0)


def _sink_slot(rows):
    qi = jnp.bitwise_and(lax.broadcasted_iota(jnp.int32, (rows, 2 * BLOCK), 0), BLOCK - 1)
    return qi == lax.broadcasted_iota(jnp.int32, (rows, 2 * BLOCK), 1)


def _sink_scores(rows, sinks):
    blk = lax.broadcasted_iota(jnp.int32, (rows, 2 * BLOCK), 0) // BLOCK
    out = jnp.full((rows, 2 * BLOCK), sinks[-1], F32)
    for b in range(len(sinks) - 2, -1, -1):
        out = jnp.where(blk == b, sinks[b], out)
    return out


def _softmax_parts(s, valid, sinks):
    s = jnp.where(valid, s, NEG_INF)
    if sinks is not None:
        slot = _sink_slot(s.shape[0])
        s = jnp.where(slot, _sink_scores(s.shape[0], sinks), s)
    m = jnp.max(s, axis=1, keepdims=True)
    p = jnp.exp(s - m)
    den = jnp.sum(p, axis=1, keepdims=True)
    if sinks is not None:
        p = jnp.where(slot, 0.0, p)
    return p, m, den


_NT = (((1,), (1,)), ((), ()))
_TN = (((0,), (0,)), ((), ()))


def _rows2(prev_ref, cur_ref, cs, single=False):
    if single:
        return cur_ref[0, :, cs]
    return jnp.concatenate([prev_ref[0, :, cs], cur_ref[0, :, cs]], axis=0)


def _sink_scalars(sink_ref, first, nblocks):
    return [sink_ref[first + b] for b in range(nblocks)]


def _tile(t):
    return slice(t * PAIR_W, (t + 1) * PAIR_W)


def _attn_fwd(qkv, sinks, g, *, NB, T, name):
    geo = _Geom(g)
    r, qw, kw, ntile = geo.r, geo.qw, geo.kw, geo.ntile
    tsub = T // r
    nblk = tsub // BLOCK
    qkv3 = qkv.reshape(NB, tsub, r * VAR_W)
    out_dtype = BF16 if g is None else F32
    tiles_per_kv = ntile // A_KV_HEADS

    single = nblk == 1

    def body(q_ref, kp_ref, kc_ref, vp_ref, vc_ref, sink_ref, o_ref, l_ref):
        i = pl.program_id(2)
        if geo.sink:
            kall, vall = _rows2(kp_ref, kc_ref, _tile(0)), _rows2(vp_ref, vc_ref, _tile(0))
            kdup = [_dup_head(kall, kh) for kh in range(A_KV_HEADS)]
            vdup = [_dup_head(vall, kh) for kh in range(A_KV_HEADS)]
            tiles = [[t] for t in range(ntile)]
            q2s = [_stack_heads(q_ref[0, :, _tile(t)], SCALE) for t in range(ntile)]
            kks = [kdup[t // tiles_per_kv] for t in range(ntile)]
            vvs = [vdup[t // tiles_per_kv] for t in range(ntile)]
            sinkcols = [_sink_scalars(sink_ref, 2 * t, 2) for t in range(ntile)]
        else:
            tiles = [[t] for t in range(ntile)]
            q2s = [_stack_heads(q_ref[0, :, _tile(t)], SCALE) for t in range(ntile)]
            kks = [_rows2(kp_ref, kc_ref, _tile(t), single) for t in range(ntile)]
            vvs = [_rows2(vp_ref, vc_ref, _tile(t), single) for t in range(ntile)]
            sinkcols = [None] * ntile
        valid = _band_mask(q2s[0].shape[0], i, geo.n_back, single)
        ss = [lax.dot_general(q2, kk, _NT, preferred_element_type=F32) for q2, kk in zip(q2s, kks)]
        parts = [_softmax_parts(s, valid, sc) for s, sc in zip(ss, sinkcols)]
        o2s = [jnp.dot(p.astype(BF16), vv, preferred_element_type=F32) / den for (p, m, den), vv in zip(parts, vvs)]
        for ts, o2, (p, m, den) in zip(tiles, o2s, parts):
            lse2 = jnp.broadcast_to(m + jnp.log(den), (o2.shape[0], PAIR_W))
            for n, t in enumerate(ts):
                rows = slice(2 * BLOCK * n, 2 * BLOCK * (n + 1))
                o_ref[0, :, _tile(t)] = _unstack_heads(o2[rows]).astype(out_dtype)
                l_ref[0, :, _tile(t)] = _unstack_heads(lse2[rows])

    prev = lambda i: jnp.maximum(i - 1, 0)
    in_specs = [
        pl.BlockSpec((1, BLOCK, qw), lambda b, j, i: (b, i, geo.qidx(j))),
        pl.BlockSpec((1, BLOCK, kw), lambda b, j, i: (b, prev(i), geo.kidx(j))),
        pl.BlockSpec((1, BLOCK, kw), lambda b, j, i: (b, i, geo.kidx(j))),
        pl.BlockSpec((1, BLOCK, kw), lambda b, j, i: (b, prev(i), geo.vidx(j))),
        pl.BlockSpec((1, BLOCK, kw), lambda b, j, i: (b, i, geo.vidx(j))),
        pl.BlockSpec(memory_space=pltpu.SMEM),
    ]
    o_spec = pl.BlockSpec((1, BLOCK, qw), lambda b, j, i: (b, i, j))
    shape = (NB, tsub, r * qw)
    o, lse = _pcall(
        body, name=name, grid=(NB, r, nblk), in_specs=in_specs, out_specs=[o_spec, o_spec],
        out_shape=[jax.ShapeDtypeStruct(shape, out_dtype), jax.ShapeDtypeStruct(shape, F32)],
        compiler_params=_params(("parallel", "parallel", "arbitrary")),
    )(qkv3, qkv3, qkv3, qkv3, qkv3, sinks)
    return o, lse


def _attn_bwd(qkv, do, lse, dlse, cosf, sins, sinks, g, *, NB, T, name):
    geo = _Geom(g)
    r, qw, kw, ntile = geo.r, geo.qw, geo.kw, geo.ntile
    tsub = T // r
    nblk = tsub // BLOCK
    view = lambda a, w: a.reshape(NB, tsub, r * w)
    has_dlse = dlse is not None
    tiles_per_kv = ntile // A_KV_HEADS

    single = nblk == 1
    krows = BLOCK if single else 2 * BLOCK
    nsteps = 1 if single else nblk + 1

    def grads(q2s, kks, vvs, do2s, i, lsecols, sinkcols, dlcols):
        valid = _band_mask(q2s[0].shape[0], i, geo.n_back, single)
        ss = [lax.dot_general(q2, kk, _NT, preferred_element_type=F32) for q2, kk in zip(q2s, kks)]
        dps = [lax.dot_general(do2, vv, _NT, preferred_element_type=F32) for do2, vv in zip(do2s, vvs)]
        ps, dss, sks = [], [], []
        for s, dp, ls, sc, dl in zip(ss, dps, lsecols, sinkcols, dlcols):
            sv = jnp.where(valid, s, NEG_INF)
            if sc is not None:
                slot = _sink_slot(s.shape[0])
                sv = jnp.where(slot, _sink_scores(s.shape[0], sc), sv)
                dp = jnp.where(slot, 0.0, dp)
            p = jnp.exp(sv - ls)
            delta = jnp.sum(p * dp, axis=1, keepdims=True)
            if dl is not None:
                delta = delta - dl
            ds = p * (dp - delta)
            if sc is not None:
                blk = lambda a, b: a[b * BLOCK:(b + 1) * BLOCK]
                sks.append([jnp.sum(jnp.where(blk(slot, b), blk(ds, b), 0.0)) for b in range(len(sc))])
                ds, p = jnp.where(slot, 0.0, ds), jnp.where(slot, 0.0, p)
            else:
                sks.append(None)
            ps.append(p.astype(BF16))
            dss.append(ds.astype(BF16))
        dq2s = [jnp.dot(ds, kk, preferred_element_type=F32) * SCALE for ds, kk in zip(dss, kks)]
        dkks = [lax.dot_general(ds, q2, _TN, preferred_element_type=F32) for ds, q2 in zip(dss, q2s)]
        dvvs = [lax.dot_general(p, do2, _TN, preferred_element_type=F32) for p, do2 in zip(ps, do2s)]
        return dq2s, dkks, dvvs, sks

    def body(*refs):
        it = iter(refs)
        q_ref, kp_ref, kc_ref, vp_ref, vc_ref, do_ref, l_ref = (next(it) for _ in range(7))
        dl_ref = next(it) if has_dlse else None
        c_ref, s_ref, sink_ref, o_ref, ds_ref, dq_s, dk_s, dv_s, car_q, car_k, car_v = (next(it) for _ in range(11))
        b, j, i = pl.program_id(0), pl.program_id(1), pl.program_id(2)

        @pl.when(jnp.logical_and(b == 0, jnp.logical_and(j == 0, i == 0)))
        def _():
            ds_ref[...] = jnp.zeros_like(ds_ref)

        def compute():
            if geo.sink:
                kall, vall = _rows2(kp_ref, kc_ref, _tile(0)), _rows2(vp_ref, vc_ref, _tile(0))
                nb = 2 * tiles_per_kv
                tiles = [[kh * tiles_per_kv + t for t in range(tiles_per_kv)] for kh in range(A_KV_HEADS)]
                cat = lambda f, ts: jnp.concatenate([f(t) for t in ts], axis=0)
                dq2s, dkks, dvvs, sks = grads(
                    [cat(lambda t: _stack_heads(q_ref[0, :, _tile(t)], SCALE), ts) for ts in tiles],
                    [_dup_head(kall, kh) for kh in range(A_KV_HEADS)],
                    [_dup_head(vall, kh) for kh in range(A_KV_HEADS)],
                    [cat(lambda t: _stack_heads(do_ref[0, :, _tile(t)]), ts) for ts in tiles], i,
                    [cat(lambda t: _lse_col(l_ref[0, :, _tile(t)]), ts) for ts in tiles],
                    [_sink_scalars(sink_ref, kh * nb, nb) for kh in range(A_KV_HEADS)], [None] * A_KV_HEADS)
                lane1 = _lane((1, PAIR_W))
                dsink = jnp.zeros((1, PAIR_W), F32)
                for kh, (ts, dq2, sk) in enumerate(zip(tiles, dq2s, sks)):
                    for n, t in enumerate(ts):
                        dq_s[:, _tile(t)] = _unstack_heads(dq2[2 * BLOCK * n:2 * BLOCK * (n + 1)])
                    for bb in range(nb):
                        dsink = dsink + jnp.where(lane1 == kh * nb + bb, sk[bb], 0.0)
                second = _lane((krows, PAIR_W)) >= HEAD_DIM
                dk_s[...] = jnp.where(second, _fold_heads(dkks[1]), _fold_heads(dkks[0]))
                dv_s[...] = jnp.where(second, _fold_heads(dvvs[1]), _fold_heads(dvvs[0]))
                ds_ref[0:1, :] += dsink
            else:
                dq2s, dkks, dvvs, _ = grads(
                    [_stack_heads(q_ref[0, :, _tile(t)], SCALE) for t in range(ntile)],
                    [_rows2(kp_ref, kc_ref, _tile(t), single) for t in range(ntile)],
                    [_rows2(vp_ref, vc_ref, _tile(t), single) for t in range(ntile)],
                    [_stack_heads(do_ref[0, :, _tile(t)]) for t in range(ntile)], i,
                    [_lse_col(l_ref[0, :, _tile(t)]) for t in range(ntile)], [None] * ntile,
                    [_lse_col(dl_ref[0, :, _tile(t)]) for t in range(ntile)])
                for t in range(ntile):
                    dq_s[:, _tile(t)] = _unstack_heads(dq2s[t])
                    dk_s[0:krows, _tile(t)] = dkks[t]
                    dv_s[0:krows, _tile(t)] = dvvs[t]

        def emit(dq, dk, dv):
            cos, sn = c_ref[0], s_ref[0]
            o_ref[0, :, 0:qw] = _unrope(dq, cos, sn).astype(BF16)
            o_ref[0, :, qw:qw + kw] = _unrope(dk, cos, sn).astype(BF16)
            o_ref[0, :, qw + kw:qw + 2 * kw] = dv.astype(BF16)
            if qw + 2 * kw < VAR_W:
                o_ref[0, :, qw + 2 * kw:VAR_W] = jnp.zeros((BLOCK, VAR_W - qw - 2 * kw), BF16)

        if single:
            compute()
            emit(dq_s[...], dk_s[0:BLOCK, :], dv_s[0:BLOCK, :])
            return

        @pl.when(i == 0)
        def _():
            car_q[...] = jnp.zeros_like(car_q)
            car_k[...] = jnp.zeros_like(car_k)
            car_v[...] = jnp.zeros_like(car_v)

        @pl.when(i == nblk)
        def _():
            dk_s[...] = jnp.zeros_like(dk_s)
            dv_s[...] = jnp.zeros_like(dv_s)

        pl.when(i < nblk)(compute)
        emit(car_q[...], car_k[...] + dk_s[0:BLOCK, :], car_v[...] + dv_s[0:BLOCK, :])
        car_q[...] = dq_s[...]
        car_k[...] = dk_s[BLOCK:2 * BLOCK, :]
        car_v[...] = dv_s[BLOCK:2 * BLOCK, :]

    cur = lambda i: jnp.minimum(i, nblk - 1)
    prv = lambda i: jnp.maximum(jnp.minimum(i, nblk - 1) - 1, 0)
    outb = lambda i: jnp.maximum(i - 1, 0)
    qrow = pl.BlockSpec((1, BLOCK, qw), lambda b, j, i: (b, cur(i), j))
    in_specs = [
        pl.BlockSpec((1, BLOCK, qw), lambda b, j, i: (b, cur(i), geo.qidx(j))),
        pl.BlockSpec((1, BLOCK, kw), lambda b, j, i: (b, prv(i), geo.kidx(j))),
        pl.BlockSpec((1, BLOCK, kw), lambda b, j, i: (b, cur(i), geo.kidx(j))),
        pl.BlockSpec((1, BLOCK, kw), lambda b, j, i: (b, prv(i), geo.vidx(j))),
        pl.BlockSpec((1, BLOCK, kw), lambda b, j, i: (b, cur(i), geo.vidx(j))),
        qrow, qrow,
    ]
    ins = [view(qkv, VAR_W)] * 5 + [view(do, qw), view(lse, qw)]
    if has_dlse:
        in_specs.append(qrow)
        ins.append(view(dlse, qw))
    in_specs += [
        pl.BlockSpec((1, BLOCK, PAIR_W), lambda b, j, i: (b, outb(i), j)),
        pl.BlockSpec((1, BLOCK, PAIR_W), lambda b, j, i: (b, outb(i), j)),
        pl.BlockSpec(memory_space=pltpu.SMEM),
    ]
    ins += [view(cosf, PAIR_W), view(sins, PAIR_W), sinks]
    scratch = [pltpu.VMEM((BLOCK, qw), F32), pltpu.VMEM((2 * BLOCK, kw), F32), pltpu.VMEM((2 * BLOCK, kw), F32),
               pltpu.VMEM((BLOCK, qw), F32), pltpu.VMEM((BLOCK, kw), F32), pltpu.VMEM((BLOCK, kw), F32)]
    dqkv, dsink = _pcall(
        body, name=name, grid=(NB, r, nsteps), in_specs=in_specs,
        out_specs=[pl.BlockSpec((1, BLOCK, VAR_W), lambda b, j, i: (b, outb(i), j)),
                   pl.BlockSpec((8, PAIR_W), lambda b, j, i: (0, 0))],
        out_shape=[jax.ShapeDtypeStruct((NB, tsub, r * VAR_W), BF16), jax.ShapeDtypeStruct((8, PAIR_W), F32)],
        scratch_shapes=scratch, compiler_params=_params(("arbitrary", "arbitrary", "arbitrary")),
    )(*ins)
    return dqkv.reshape(NB * T, VAR_W), dsink


class _Rows:
    def __init__(self, N, T, tm):
        self.N, self.tm, self.tpe, self.grid = N, tm, T // tm, (N // tm,)

    def row(self, w, col=0):
        return pl.BlockSpec((self.tm, w), lambda i: (i, col))

    def ex(self, w):
        return pl.BlockSpec((1, 1, w), lambda i: (i // self.tpe, 0, 0))

    def const(self, shape):
        return pl.BlockSpec(shape, lambda i: tuple(0 for _ in shape))

    def view(self, w, r):
        return pl.BlockSpec((None, self.tm // r, r * w), lambda i: (i // self.tpe, i % self.tpe, 0))

    def first_of_example(self):
        return pl.program_id(0) % self.tpe == 0


def _acc(ref, first, val):
    @pl.when(first)
    def _():
        ref[0] = val

    @pl.when(jnp.logical_not(first))
    def _():
        ref[0] += val


def _colsum(v):
    return jnp.sum(v, axis=0, keepdims=True)


def _ln_stats(r):
    mu = jnp.mean(r, axis=-1, keepdims=True)
    xc = r - mu
    var = jnp.mean(xc * xc, axis=-1, keepdims=True)
    rstd = lax.rsqrt(var + LN_EPS)
    return xc * rstd, rstd


def _ln_bwd(dy, xhat, rstd, gain):
    dxh = dy * gain
    return rstd * (dxh - jnp.mean(dxh, axis=-1, keepdims=True) - xhat * jnp.mean(dxh * xhat, axis=-1, keepdims=True))


def _from_view(ref, scr, r):
    if r == 1:
        return ref[...]
    rows, w = ref.shape[0], ref.shape[1] // r
    for j in range(r):
        for c in range(w // LANES):
            scr.at[c][pl.ds(j, rows, stride=r), :] = ref[:, j * w + c * LANES:j * w + (c + 1) * LANES]
    return jnp.concatenate([scr[c] for c in range(w // LANES)], axis=1)


def _to_view(val, ref, scr, r):
    if r == 1:
        ref[...] = val.astype(ref.dtype)
        return
    rows, w = ref.shape[0], ref.shape[1] // r
    for c in range(w // LANES):
        scr[c] = val[:, c * LANES:(c + 1) * LANES]
    for j in range(r):
        for c in range(w // LANES):
            ref[:, j * w + c * LANES:j * w + (c + 1) * LANES] = scr.at[c][pl.ds(j, rows, stride=r), :].astype(ref.dtype)


def _silu_parts(v):
    s = jax.nn.sigmoid(v)
    return v * s, s * (1.0 + v * (1.0 - s))


def _local_step(x, mod, positions, w_in, rest_weights, sinks, ln1_g, ln1_b, ln2_g, ln2_b, target, hook=None):
    hook = hook or (lambda event, **data: None)
    NB, T, D = x.shape
    N = NB * T
    x2 = x.reshape(N, D)
    tgt2 = target.reshape(N, D)
    shift_m, scale_m, gate_m, shift_f, scale_f, gate_f = [mod[:, None, k * D:(k + 1) * D] for k in range(6)]
    cosf, sins = _rope_tables(positions)
    col = jnp.arange(QKV_P)
    vcol = col % VAR_W
    flags = jnp.where(col < VAR_W, vcol < QA_W + KA_W, vcol < 2 * GB_W).astype(F32)[None]
    R = _Rows(N, T, _pick(T, 256))
    sds = jax.ShapeDtypeStruct
    exsum = lambda w=D: sds((NB, 1, w), F32)
    ngrp = len(B_PATTERNS)

    *qkv, u = _inproj(x2, scale_m, shift_m, w_in, cosf, sins, flags, T=T, name="inproj_qkv")
    gates = _mm(u, w_in[QKV_P:], tb=True, name="inproj_gates")
    oa, la = _attn_fwd(qkv[0], sinks, None, NB=NB, T=T, name="attn_a_fwd")
    oa = oa.reshape(N, QA_W)
    ob_parts = [_attn_fwd(qkv[1 + g], sinks, g, NB=NB, T=T, name=f"attn_b{g}_fwd") for g in range(ngrp)]
    (o1, l1), (o2, l2), (o3, l3) = ob_parts
    w_a, w_b, w_o, w_gu, w_d = rest_weights()
    F = w_d.shape[0]
    dil = [r_ for _, r_ in B_PATTERNS]
    views = [R.view(GB_W, r_) for r_ in dil]
    tokbuf = pltpu.VMEM((GB_W // LANES, R.tm, LANES), F32)

    def merge_fwd(o1r, o2r, o3r, l1r, l2r, l3r, ob_ref, *bufs):
        os_ = [_from_view(ref, bufs[n], dil[n]) for n, ref in enumerate((o1r, o2r, o3r))]
        la, lb, lc = [_from_view(ref, bufs[3 + n], dil[n]) for n, ref in enumerate((l1r, l2r, l3r))]
        mx = jnp.maximum(jnp.maximum(la, lb), lc)
        ea, eb, ec = jnp.exp(la - mx), jnp.exp(lb - mx), jnp.exp(lc - mx)
        ob_ref[...] = ((ea * os_[0] + eb * os_[1] + ec * os_[2]) / (ea + eb + ec)).astype(BF16)

    ob = _pcall(merge_fwd, name="merge_fwd", grid=R.grid, in_specs=views + views, out_specs=R.row(GB_W),
                out_shape=sds((N, GB_W), BF16), scratch_shapes=[tokbuf] * 6,
                compiler_params=_params(("parallel",)))(o1, o2, o3, l1, l2, l3)

    ya = _mm(oa, w_a, name="branch_a")
    yb = _mm(ob, w_b, b3=True, name="branch_b")

    def gate_fwd(ya_r, yb_r, ga_r, gb_r, mg_ref):
        mg_ref[...] = (jax.nn.sigmoid(ga_r[...]) * ya_r[...] + jax.nn.sigmoid(gb_r[...]) * yb_r[...]).astype(BF16)

    merged = _pcall(gate_fwd, name="gate_fwd", grid=R.grid, in_specs=[R.row(D), R.row(D), R.row(D, 0), R.row(D, 1)],
                    out_specs=R.row(D), out_shape=sds((N, D), BF16),
                    compiler_params=_params(("parallel",)))(ya, yb, gates, gates)
    y = _mm(merged, w_o, name="out_proj")

    def norm1_fwd(x_r, y_r, gm_r, g_r, b_r, sf_r, hf_r, r1_ref, x1_ref, u2_ref):
        r1 = ALPHA * x_r[...] + (1.0 + gm_r[0]) * y_r[...]
        xhat, _ = _ln_stats(r1)
        x1 = xhat * g_r[...] + b_r[...]
        r1_ref[...] = r1
        x1_ref[...] = x1
        u2_ref[...] = (x1 * (1.0 + sf_r[0]) + hf_r[0]).astype(BF16)

    r1, x1, u2 = _pcall(
        norm1_fwd, name="norm1_fwd", grid=R.grid,
        in_specs=[R.row(D), R.row(D), R.ex(D), R.const((1, D)), R.const((1, D)), R.ex(D), R.ex(D)],
        out_specs=[R.row(D)] * 3, out_shape=[sds((N, D), F32), sds((N, D), F32), sds((N, D), BF16)],
        compiler_params=_params(("parallel",)))(x2, y, gate_m, ln1_g, ln1_b, scale_f, shift_f)

    tnf = w_gu.shape[2]
    nft = w_gu.shape[0] // 2
    tmf = _pick(N, 512)

    def ffn_up(u_r, wg_r, wu_r, hg_ref, hu_ref, a_ref):
        hg = jnp.dot(u_r[...], wg_r[...], preferred_element_type=F32)
        hu = jnp.dot(u_r[...], wu_r[...], preferred_element_type=F32)
        sl, _ = _silu_parts(hg)
        hg_ref[...] = hg.astype(BF16)
        hu_ref[...] = hu.astype(BF16)
        a_ref[...] = (sl * hu).astype(BF16)

    ftile = pl.BlockSpec((tmf, tnf), lambda j, i: (i, j))
    hg, hu, act = _pcall(
        ffn_up, name="ffn_up", grid=(nft, N // tmf),
        in_specs=[pl.BlockSpec((tmf, D), lambda j, i: (i, 0)), pl.BlockSpec((None, D, tnf), lambda j, i: (j, 0, 0)),
                  pl.BlockSpec((None, D, tnf), lambda j, i: (j + nft, 0, 0))],
        out_specs=[ftile] * 3, out_shape=[sds((N, F), BF16)] * 3,
        compiler_params=_params(("arbitrary", "parallel")))(u2, w_gu, w_gu)
    y2 = _mm(act, w_d, name="ffn_down")

    def norm2_loss_bwd(x1_r, y2_r, t_r, gf_r, g_r, b_r, dy2_ref, dx1_ref, dgf_ref, dg_ref, db_ref, loss_ref):
        first = R.first_of_example()
        y2v = y2_r[...]
        r2 = ALPHA * x1_r[...] + (1.0 + gf_r[0]) * y2v
        xhat, rstd = _ln_stats(r2)
        err = xhat * g_r[...] + b_r[...] - t_r[...]
        dx2 = err * (1.0 / D)
        dr2 = _ln_bwd(dx2, xhat, rstd, g_r[...])
        dy2_ref[...] = ((1.0 + gf_r[0]) * dr2).astype(BF16)
        dx1_ref[...] = ALPHA * dr2
        _acc(dgf_ref, first, _colsum(dr2 * y2v))
        _acc(dg_ref, first, _colsum(dx2 * xhat))
        _acc(db_ref, first, _colsum(dx2))
        part = 0.5 * jnp.sum(jnp.mean(err * err, axis=-1, keepdims=True))
        _acc(loss_ref, first, jnp.broadcast_to(part, (1, 128)))

    dy2, dx1p, dgate_f, dg2, db2, loss_p = _pcall(
        norm2_loss_bwd, name="norm2_loss_bwd", grid=R.grid,
        in_specs=[R.row(D), R.row(D), R.row(D), R.ex(D), R.const((1, D)), R.const((1, D))],
        out_specs=[R.row(D), R.row(D), R.ex(D), R.ex(D), R.ex(D), R.ex(128)],
        out_shape=[sds((N, D), BF16), sds((N, D), F32), exsum(), exsum(), exsum(), exsum(128)],
        compiler_params=_params(("arbitrary",)))(x1, y2, tgt2, gate_f, ln2_g, ln2_b)

    g_wd = _mm(act, dy2, ta=True, out_dtype=BF16, name="ffn_down_dw")

    tmd = _pick(N, 256)

    def ffn_down_dx(dy_r, wd_r, hg_r, hu_r, dh_ref):
        for t in range(nft):
            cs = slice(t * tnf, (t + 1) * tnf)
            da = lax.dot_general(dy_r[...], wd_r[cs, :], _NT, preferred_element_type=F32)
            sl, dsl = _silu_parts(hg_r[:, cs].astype(F32))
            dh_ref[:, cs] = (da * hu_r[:, cs].astype(F32) * dsl).astype(BF16)
            dh_ref[:, F + t * tnf:F + (t + 1) * tnf] = (da * sl).astype(BF16)

    rowd = lambda w_: pl.BlockSpec((tmd, w_), lambda i: (i, 0))
    dh = _pcall(
        ffn_down_dx, name="ffn_down_dx", grid=(N // tmd,),
        in_specs=[rowd(D), pl.BlockSpec((F, D), lambda i: (0, 0)), rowd(F), rowd(F)],
        out_specs=rowd(2 * F), out_shape=sds((N, 2 * F), BF16),
        compiler_params=_params(("parallel",)))(dy2, w_d, hg, hu)
    du2 = _mm(dh, w_gu, tb=True, b3=True, name="ffn_up_dx")
    g_wgu = _mm(u2, dh, ta=True, out3=w_gu.shape[0], out_dtype=BF16, name="ffn_up_dw")

    def norm1_bwd(dx1p_r, du2_r, x1_r, r1_r, y_r, sf_r, gm_r, g_r,
                  dxp_ref, dy_ref, dsf_ref, dhf_ref, dgm_ref, dg_ref, db_ref):
        first = R.first_of_example()
        du2v = du2_r[...]
        dx1 = dx1p_r[...] + du2v * (1.0 + sf_r[0])
        xhat, rstd = _ln_stats(r1_r[...])
        dr1 = _ln_bwd(dx1, xhat, rstd, g_r[...])
        dxp_ref[...] = ALPHA * dr1
        dy_ref[...] = ((1.0 + gm_r[0]) * dr1).astype(BF16)
        _acc(dsf_ref, first, _colsum(du2v * x1_r[...]))
        _acc(dhf_ref, first, _colsum(du2v))
        _acc(dgm_ref, first, _colsum(dr1 * y_r[...]))
        _acc(dg_ref, first, _colsum(dx1 * xhat))
        _acc(db_ref, first, _colsum(dx1))

    dxp, dy, dscale_f, dshift_f, dgate_m, dg1, db1 = _pcall(
        norm1_bwd, name="norm1_bwd", grid=R.grid,
        in_specs=[R.row(D)] * 5 + [R.ex(D), R.ex(D), R.const((1, D))],
        out_specs=[R.row(D), R.row(D)] + [R.ex(D)] * 5,
        out_shape=[sds((N, D), F32), sds((N, D), BF16)] + [exsum()] * 5,
        compiler_params=_params(("arbitrary",)))(dx1p, du2, x1, r1, y, scale_f, gate_m, ln1_g)

    dmerged = _mm(dy, w_o, tb=True, name="out_proj_dx")
    g_wo = _mm(merged, dy, ta=True, out_dtype=BF16, name="out_proj_dw")

    def gate_bwd(dm_r, ya_r, yb_r, ga_r, gb_r, dya_ref, dyb_ref, dg_ref):
        dm = dm_r[...]
        sa, sb = jax.nn.sigmoid(ga_r[...]), jax.nn.sigmoid(gb_r[...])
        dya_ref[...] = (dm * sa).astype(BF16)
        dyb_ref[...] = (dm * sb).astype(BF16)
        dg_ref[:, :D] = (dm * ya_r[...] * sa * (1.0 - sa)).astype(BF16)
        dg_ref[:, D:] = (dm * yb_r[...] * sb * (1.0 - sb)).astype(BF16)

    dya, dyb, dgates = _pcall(
        gate_bwd, name="gate_bwd", grid=R.grid, in_specs=[R.row(D)] * 3 + [R.row(D, 0), R.row(D, 1)],
        out_specs=[R.row(D), R.row(D), R.row(2 * D)],
        out_shape=[sds((N, D), BF16), sds((N, D), BF16), sds((N, 2 * D), BF16)],
        compiler_params=_params(("parallel",)))(dmerged, ya, yb, gates, gates)

    doa = _mm(dya, w_a, tb=True, out_dtype=BF16, name="branch_a_dx")
    g_wa = _mm(oa, dya, ta=True, out_dtype=BF16, name="branch_a_dw")
    dob = _mm(dyb, w_b, tb=True, b3=True, name="branch_b_dx")
    g_wb = _mm(ob, dyb, ta=True, out3=w_b.shape[0], out_dtype=BF16, name="branch_b_dw")
    hook("rest_grads", g_wa=g_wa, g_wb=g_wb, g_wo=g_wo, g_wgu=g_wgu, g_wd=g_wd)

    seg = (jnp.arange(GB_W)[:, None] // HEAD_DIM == jnp.arange(GB_W)[None, :] // HEAD_DIM).astype(BF16)

    def merge_bwd(dob_r, o1r, o2r, o3r, l1r, l2r, l3r, seg_r, d1, d2, d3, e1, e2, e3, *bufs):
        dob_v = dob_r[...]
        os_ = [_from_view(ref, bufs[n], dil[n]) for n, ref in enumerate((o1r, o2r, o3r))]
        la, lb, lc = [_from_view(ref, bufs[3 + n], dil[n]) for n, ref in enumerate((l1r, l2r, l3r))]
        mx = jnp.maximum(jnp.maximum(la, lb), lc)
        ea, eb, ec = jnp.exp(la - mx), jnp.exp(lb - mx), jnp.exp(lc - mx)
        inv = 1.0 / (ea + eb + ec)
        ws = [ea * inv, eb * inv, ec * inv]

        def headsum(v):
            hi = v.astype(BF16)
            r1_ = v - hi.astype(F32)
            mid = r1_.astype(BF16)
            lo = (r1_ - mid.astype(F32)).astype(BF16)
            sm = seg_r[...]
            return (jnp.dot(hi, sm, preferred_element_type=F32) + jnp.dot(mid, sm, preferred_element_type=F32)
                    + jnp.dot(lo, sm, preferred_element_type=F32))

        dws = [headsum(dob_v * o) for o in os_]
        mean = ws[0] * dws[0] + ws[1] * dws[1] + ws[2] * dws[2]
        for n, (w_, dw_, d_ref, e_ref) in enumerate(zip(ws, dws, (d1, d2, d3), (e1, e2, e3))):
            _to_view(w_ * dob_v, d_ref, bufs[6], dil[n])
            _to_view(w_ * (dw_ - mean), e_ref, bufs[7], dil[n])

    vshape = lambda r_, dt: sds((NB, T // r_, r_ * GB_W), dt)
    mb = _pcall(
        merge_bwd, name="merge_bwd", grid=R.grid, in_specs=[R.row(GB_W)] + views + views + [R.const((GB_W, GB_W))],
        out_specs=views + views, out_shape=[vshape(r_, BF16) for r_ in dil] + [vshape(r_, F32) for r_ in dil],
        scratch_shapes=[tokbuf] * 8, compiler_params=_params(("parallel",)))(dob, o1, o2, o3, l1, l2, l3, seg)
    do_b, dlse_b = mb[:3], mb[3:]
    hook("merge_bwd_done")

    dqkv_a, dsink = _attn_bwd(qkv[0], doa, la, None, cosf, sins, sinks, None, NB=NB, T=T, name="attn_a_bwd")
    hook("attn_a_bwd_done")
    dqkv = [dqkv_a]
    for g in range(ngrp):
        dqkv.append(_attn_bwd(qkv[1 + g], do_b[g], (l1, l2, l3)[g], dlse_b[g], cosf, sins, sinks, g, NB=NB, T=T,
                              name=f"attn_b{g}_bwd")[0])
        hook(f"attn_b{g}_bwd_done")

    g_win = [_mm(dseg, u, ta=True, out_dtype=BF16, name=f"inproj_dw{n}") for n, dseg in enumerate(dqkv + [dgates])]
    hook("win_grads", g_win=g_win)
    wvar = lambda v: (w_in, (VAR_W, D), (v, 0))
    du = _mm_multi(dqkv[:1], [wvar(0)], name="inproj_dx0")
    hook("inproj_dx0_done")
    du = _mm_multi(dqkv[1:] + [dgates], [wvar(v) for v in range(1, N_VAR)] + [(w_in, (2 * D, D), (QKV_P // (2 * D), 0))],
                   add=du, tm=256, name="inproj_dx1")
    hook("inproj_dx1_done")

    def x_bwd(dxp_r, du_r, x_r, sm_r, gx_ref, dsm_ref, dhm_ref):
        first = R.first_of_example()
        duv = du_r[...]
        gx_ref[...] = dxp_r[...] + duv * (1.0 + sm_r[0])
        _acc(dsm_ref, first, _colsum(duv * x_r[...]))
        _acc(dhm_ref, first, _colsum(duv))

    gx, dscale_m, dshift_m = _pcall(
        x_bwd, name="x_bwd", grid=R.grid, in_specs=[R.row(D)] * 3 + [R.ex(D)],
        out_specs=[R.row(D), R.ex(D), R.ex(D)], out_shape=[sds((N, D), F32), exsum(), exsum()],
        compiler_params=_params(("arbitrary",)))(dxp, du, x2, scale_m)
    hook("x_bwd_done")

    dmod =jnp.concatenate([dshift_m, dscale_m, dgate_m, dshift_f, dscale_f, dgate_f], axis=-1)[:, 0]
    ln_grads = jnp.concatenate([dg1, db1, dg2, db2], axis=1)
    return dict(loss=loss_p[:, 0, 0], grad_x=gx.reshape(NB, T, D), g_win=g_win, g_wa=g_wa, g_wb=g_wb, g_wo=g_wo,
                g_wgu=g_wgu, g_wd=g_wd, dmod=dmod, ln_grads=ln_grads, dsink=dsink[0, :A_Q_HEADS])


def _coords():
    return lax.axis_index("x"), lax.axis_index("y"), lax.axis_index("c")


def _allgather_small(blk, *, name):
    m_per, n = blk.shape

    def body(x_ref, out_ref, send_sems, recv_sems, local_sem):
        x, y, c = _coords()
        me, sibling = (x, y, c), (x, y, 1 - c)
        chips = [(1 - x, y), (x, 1 - y), (1 - x, 1 - y)]

        def rows(px, py, pc):
            return out_ref.at[pl.ds((4 * px + 2 * py + pc) * m_per, m_per), :]

        def copy(k, block, to, src=None):
            return pltpu.make_async_remote_copy(
                src_ref=rows(*block) if src is None else src, dst_ref=rows(*block),
                send_sem=send_sems.at[k], recv_sem=recv_sems.at[k], device_id=to, device_id_type=MESH)

        mine = pltpu.make_async_copy(x_ref, rows(*me), local_sem)
        mine.start()
        first = [copy(0, me, sibling, src=x_ref)]
        first += [copy(1 + j, me, (*chip, c), src=x_ref) for j, chip in enumerate(chips)]
        for cp in first:
            cp.start()
        passed = [copy(4 + j, (*chip, c), sibling) for j, chip in enumerate(chips)]
        for j, chip in enumerate(chips):
            copy(1 + j, (*chip, c), me).wait_recv()
            passed[j].start()
        copy(0, sibling, me).wait_recv()
        for j, chip in enumerate(chips):
            copy(4 + j, (*chip, 1 - c), me).wait_recv()
        for cp in first + passed:
            cp.wait_send()
        mine.wait()

    return _pcall(
        body, name=name, out_shape=jax.ShapeDtypeStruct((8 * m_per, n), blk.dtype),
        in_specs=[pl.BlockSpec(memory_space=pltpu.VMEM)], out_specs=pl.BlockSpec(memory_space=pltpu.VMEM),
        scratch_shapes=[pltpu.SemaphoreType.DMA((7,)), pltpu.SemaphoreType.DMA((7,)), pltpu.SemaphoreType.DMA],
        compiler_params=pltpu.CompilerParams(vmem_limit_bytes=VMEM_LIMIT_BYTES),
    )(blk)


def _exchange(srcs, dsts, plan, *, name, dst_inits=None):
    na = len(dsts)
    nrem = len(plan(0, 0, 0))

    def body(*refs):
        refs = list(refs)
        src_refs = [refs.pop(0) for _ in range(na)] if srcs is not None else None
        if dst_inits is not None:
            del refs[:na]
        dst_refs, (send_sems, recv_sems) = refs[:na], refs[na:]
        start, wait = _copies(dst_refs if src_refs is None else src_refs, dst_refs, send_sems, recv_sems, plan)
        start()
        wait()

    hbm = pl.BlockSpec(memory_space=pl.ANY)
    ins = (list(srcs) if srcs is not None else []) + (list(dst_inits) if dst_inits is not None else [])
    base = na if srcs is not None else 0
    aliases = {base + a: a for a in range(na)} if dst_inits is not None else {}
    return _pcall(
        body, name=name, out_shape=list(dsts), in_specs=[hbm] * len(ins), out_specs=[hbm] * na,
        input_output_aliases=aliases,
        scratch_shapes=[pltpu.SemaphoreType.DMA((na * nrem,)), pltpu.SemaphoreType.DMA((na * nrem,))],
    )(*ins)


def _other_chips(x, y):
    return [(1 - x, y), (x, 1 - y), (1 - x, 1 - y)]


def _round(ride, carrier, name):
    if carrier is not None:
        _RIDES.setdefault(carrier, []).append(ride)
        return
    srcs = ride.srcs() if callable(ride.srcs) else ride.srcs
    inits = ride.dst_inits() if callable(ride.dst_inits) else ride.dst_inits
    ride.out = list(_exchange(srcs, ride.dsts, ride.plan, name=name, dst_inits=inits))


class _Gather:
    def __init__(self, shards, chip, tag, carriers=(None, None)):
        def plan_ici(x, y, c):
            k = 2 * x + y
            return [((c,), (k, c), (2 * px + py, c), (px, py, c)) for px, py in _other_chips(x, y)]

        def plan_d2d(x, y, c):
            return [((2 * px + py, c), (2 * px + py, c), (2 * px + py, 1 - c), (x, y, 1 - c))
                    for px, py in _other_chips(x, y)]

        self.shards, self.chip = shards, chip
        dsts = [jax.ShapeDtypeStruct((4,) + s.shape, s.dtype) for s in shards]
        ici = _Ride(shards, dsts, plan_ici)
        self.d2d = _Ride(None, dsts, plan_d2d, dst_inits=lambda: ici.out)
        _round(ici, carriers[0], f"gather_{tag}_ici")
        _round(self.d2d, carriers[1], f"gather_{tag}_d2d")

    def result(self):
        full = [lax.dynamic_update_index_in_dim(f, s, self.chip, 0) for f, s in zip(self.d2d.out, self.shards)]
        return [f.reshape((4, 2 * f.shape[2], f.shape[3])) for f in full]


def _add_pairs(a, b, *, name):
    s, hr, wd = a.shape
    tr = _pick(hr, 600, 16)

    def body(a_ref, b_ref, o_ref):
        o_ref[...] = (a_ref[...].astype(F32) + b_ref[...].astype(F32)).astype(BF16)

    spec = pl.BlockSpec((1, tr, wd), lambda j, i: (j, i, 0))
    return _pcall(body, name=name, grid=(s, hr // tr), in_specs=[spec, spec], out_specs=spec,
                  out_shape=jax.ShapeDtypeStruct(a.shape, BF16), compiler_params=_params(("parallel", "parallel")))(a, b)


def _sum_chips(b, *, name):
    s, hr, wd = b.shape
    tr = _pick(hr, 600, 16)

    def body(b_ref, o_ref):
        acc = b_ref[0].astype(F32)
        for k in range(1, s):
            acc = acc + b_ref[k].astype(F32)
        o_ref[...] = acc

    return _pcall(body, name=name, grid=(hr // tr,), in_specs=[pl.BlockSpec((s, tr, wd), lambda i: (0, i, 0))],
                  out_specs=pl.BlockSpec((tr, wd), lambda i: (i, 0)), out_shape=jax.ShapeDtypeStruct((hr, wd), F32),
                  compiler_params=_params(("parallel",)))(b)


class _ReduceScatter:
    def __init__(self, gs, chip, ci, tag):
        self.gs, self.chip, self.ci, self.tag = gs, chip, ci, tag
        self.half_t = [jax.ShapeDtypeStruct((g.shape[0],) + g.shape[2:], BF16) for g in gs]

    def pair(self, carrier=None):
        plan = lambda x, y, c: [((slice(None), 1 - c), (), (), (x, y, 1 - c))]
        self.r1 = _Ride(self.gs, self.half_t, plan)
        _round(self.r1, carrier, f"reduce_{self.tag}_pair")

    def chips(self, carrier=None):
        def plan(x, y, c):
            k = 2 * x + y
            return [((2 * px + py,), (k,), (2 * px + py,), (px, py, c)) for px, py in _other_chips(x, y)]

        self.pairs = [_add_pairs(lax.dynamic_index_in_dim(g, self.ci, 1, keepdims=False), f,
                                 name=f"reduce_{self.tag}_pair_add{n}")
                      for n, (g, f) in enumerate(zip(self.gs, self.r1.out))]
        self.r2 = _Ride(self.pairs, self.half_t, plan)
        _round(self.r2, carrier, f"reduce_{self.tag}_chips")

    def halves(self, carrier=None):
        plan = lambda x, y, c: [((), (c,), (1 - c,), (x, y, 1 - c))]
        self.mine = []
        for n, (l, p) in enumerate(zip(self.r2.out, self.pairs)):
            own = lax.dynamic_index_in_dim(p, self.chip, 0, keepdims=False)
            self.mine.append(_sum_chips(lax.dynamic_update_index_in_dim(l, own, self.chip, 0),
                                        name=f"reduce_{self.tag}_chip_sum{n}"))
        self.r3 = _Ride(self.mine, [jax.ShapeDtypeStruct((2,) + m.shape, F32) for m in self.mine], plan)
        _round(self.r3, carrier, f"reduce_{self.tag}_halves")

    def result(self):
        return [lax.dynamic_update_index_in_dim(b, m, self.ci, 0).reshape(2 * m.shape[0], m.shape[1])
                for b, m in zip(self.r3.out, self.mine)]


def _ada_fwd(c_all, w_sh, b_sh, *, name):
    nb, d = c_all.shape
    wcols = w_sh.shape[1]
    tn = _pick(wcols, 512)

    def body(c_ref, w_ref, b_ref, o_ref, a_ref):
        cv = c_ref[...]
        act = cv * jax.nn.sigmoid(cv)
        a_ref[...] = act
        o_ref[...] = jnp.dot(act.astype(BF16), w_ref[...].astype(BF16), preferred_element_type=F32) + b_ref[...]

    return _pcall(
        body, name=name, grid=(wcols // tn,),
        in_specs=[pl.BlockSpec((nb, d), lambda j: (0, 0)), pl.BlockSpec((d, tn), lambda j: (0, j)),
                  pl.BlockSpec((1, tn), lambda j: (0, j))],
        out_specs=[pl.BlockSpec((nb, tn), lambda j: (0, j)), pl.BlockSpec((nb, d), lambda j: (0, 0))],
        out_shape=[jax.ShapeDtypeStruct((nb, wcols), F32), jax.ShapeDtypeStruct((nb, d), F32)],
        compiler_params=_params(("arbitrary",)))(c_all, w_sh, b_sh)


def _sum_devices(g, *, name):
    nd, m, w = g.shape

    def body(g_ref, o_ref):
        acc = g_ref[0]
        for k in range(1, nd):
            acc = acc + g_ref[k]
        o_ref[...] = acc

    return _pcall(body, name=name, out_shape=jax.ShapeDtypeStruct((m, w), F32),
                  compiler_params=pltpu.CompilerParams(vmem_limit_bytes=VMEM_LIMIT_BYTES))(g)


def _adamw(w, g, m, v, *, name):
    rows, cols = w.shape[-2:]
    tr = _pick(rows, max(8, (1 << 18) // cols), 8)
    c1 = 1.0 / (1.0 - ADAM_B1 ** ADAM_STEP)
    c2 = 1.0 / (1.0 - ADAM_B2 ** ADAM_STEP)

    def body(w_ref, g_ref, m_ref, v_ref, d_ref, nm_ref, nv_ref):
        gv = g_ref[...]
        nm = ADAM_B1 * m_ref[...] + (1.0 - ADAM_B1) * gv
        nv = ADAM_B2 * v_ref[...] + (1.0 - ADAM_B2) * (gv * gv)
        d_ref[...] = -ADAM_LR * ((nm * c1) / (jnp.sqrt(nv * c2) + ADAM_EPS) + ADAM_WD * w_ref[...])
        nm_ref[...] = nm
        nv_ref[...] = nv

    gspec = pl.BlockSpec((tr, cols), lambda i: (i, 0))
    spec = pl.BlockSpec((None, tr, cols), lambda i: (0, i, 0)) if w.ndim == 3 else gspec
    shp = jax.ShapeDtypeStruct(w.shape, F32)
    return _pcall(body, name=name, grid=(rows // tr,), in_specs=[spec, gspec, spec, spec], out_specs=[spec] * 3,
                  out_shape=[shp] * 3, compiler_params=_params(("parallel",)))(w, g, m, v)


def _permute_in_rows(wt):
    ngrp = len(B_PATTERNS)
    qb, kb, vb = (wt[A_W + n * QB_W:A_W + (n + 1) * QB_W] for n in range(3))
    parts = [wt[:A_W], jnp.zeros((VAR_W - A_W, wt.shape[1]), wt.dtype)]
    for g in range(ngrp):
        parts += [t[g * GB_W:(g + 1) * GB_W] for t in (qb, kb, vb)]
    return jnp.concatenate(parts + [wt[A_W + 3 * QB_W:]], axis=0)


def _unpermute_in_grads(pieces):
    ga, groups, gg = pieces[0], pieces[1:-1], pieces[-1]
    rows = [ga[:A_W]]
    for n in range(3):
        rows += [gp[n * GB_W:(n + 1) * GB_W] for gp in groups]
    return jnp.concatenate(rows + [gg], axis=0)


def kernel(x, c, positions, w_ada, b_ada, w_in, sinks, w_branch_a, w_branch_b, w_o, ln1_g, ln1_b, w_gate_up, w_down, ln2_g, ln2_b, loss_target, m_w_ada, m_b_ada, m_w_in, m_sinks, m_w_branch_a, m_w_branch_b, m_w_o, m_ln1_g, m_ln1_b, m_w_gate_up, m_w_down, m_ln2_g, m_ln2_b, v_w_ada, v_b_ada, v_w_in, v_sinks, v_w_branch_a, v_w_branch_b, v_w_o, v_ln1_g, v_ln1_b, v_w_gate_up, v_w_down, v_ln2_g, v_ln2_b):
    xi, yi, ci = _coords()
    chip = 2 * xi + yi
    dev = 4 * xi + 2 * yi + ci
    NB, T, D = x.shape
    nchip, ndev = 4, 8
    ada_cols = w_ada.shape[2]

    c_blk = jnp.zeros((8, D), F32).at[:NB].set(c)
    c_all = _allgather_small(c_blk, name="gather_c").reshape(ndev, 8, D)[:, :NB].reshape(ndev * NB, D)
    b_sh = lax.dynamic_slice(b_ada, (0, chip * ada_cols), (1, ada_cols))
    mod_part, c_act = _ada_fwd(c_all, w_ada[0], b_sh, name="ada_fwd")
    mod_g = _allgather_small(mod_part, name="gather_mod").reshape(nchip, 2, ndev * NB, ada_cols)[:, 0]
    mod_all = jnp.transpose(mod_g, (1, 0, 2)).reshape(ndev * NB, nchip * ada_cols)
    mod = lax.dynamic_slice(mod_all, (NB * dev, 0), (NB, nchip * ada_cols))

    ra, ro, rd = w_branch_a.shape[1], w_o.shape[1], w_down.shape[1]
    rowsh = jnp.concatenate([w_branch_a[0], w_o[0], w_down[0]], axis=0)
    halves = lambda a: a.reshape(a.shape[:-2] + (2, a.shape[-2] // 2, a.shape[-1]))
    whole = lambda a: a.reshape(a.shape[:-3] + (2 * a.shape[-2], a.shape[-1]))
    tr = lambda a: jnp.swapaxes(a, -1, -2)
    shards = [halves(w.astype(BF16)) for w in (tr(w_in[0]), rowsh, w_branch_b[0], w_gate_up[0])]
    (g_in,) = _Gather(shards[:1], chip, "w_in").result()
    w_in_f = _permute_in_rows(g_in.reshape(nchip * g_in.shape[1], D))
    mix = _Gather(shards[1:3], chip, "w_mix", carriers=("inproj_qkv", "attn_a_fwd"))
    ffn = _Gather(shards[3:], chip, "w_ffn", carriers=("attn_a_fwd", "attn_b0_fwd"))

    def rest_weights():
        (g_rows, w_b_f), (w_gu_f,) = mix.result(), ffn.result()
        return (g_rows[:, :ra].reshape(nchip * ra, D), w_b_f, g_rows[:, ra:ra + ro].reshape(nchip * ro, D), w_gu_f,
                g_rows[:, ra + ro:].reshape(nchip * rd, D))

    red = {}

    def hook(event, **g):
        if event == "rest_grads":
            gr_rows = jnp.concatenate([g["g_wa"].reshape(nchip, ra, D), g["g_wo"].reshape(nchip, ro, D),
                                       g["g_wd"].reshape(nchip, rd, D)], axis=1)
            red["rest"] = _ReduceScatter([halves(a) for a in (gr_rows, g["g_wb"], g["g_wgu"])], chip, ci, "rest")
            red["rest"].pair(carrier="merge_bwd")
        elif event == "merge_bwd_done":
            red["rest"].chips(carrier="attn_a_bwd")
        elif event == "attn_a_bwd_done":
            red["rest"].halves(carrier="attn_b0_bwd")
        elif event == "win_grads":
            gr_in = _unpermute_in_grads(g["g_win"])
            red["w_in"] = _ReduceScatter([halves(gr_in.reshape(nchip, gr_in.shape[0] // nchip, D))], chip, ci, "w_in")
            red["w_in"].pair(carrier="inproj_dx0")
        elif event == "inproj_dx0_done":
            red["w_in"].chips(carrier="inproj_dx1")
        elif event == "inproj_dx1_done":
            red["w_in"].halves(carrier="x_bwd")

    res = _local_step(x, mod, positions, w_in_f, rest_weights, sinks[0], ln1_g, ln1_b, ln2_g, ln2_b, loss_target, hook)
    (g_w_in,) = red["w_in"].result()
    g_rows_red, g_w_b, g_w_gu = red["rest"].result()
    g_w_a, g_w_o, g_w_d = g_rows_red[:ra], g_rows_red[ra:ra + ro], g_rows_red[ra + ro:]

    small_rows = 24
    misc = jnp.zeros((1, D), F32).at[0, :A_Q_HEADS].set(res["dsink"]).at[0, A_Q_HEADS].set(jnp.sum(res["loss"]))
    small = jnp.concatenate([res["dmod"].reshape(NB * 6, D), jnp.sum(res["ln_grads"], axis=0), misc,
                             jnp.zeros((small_rows - NB * 6 - 5, D), F32)], axis=0)
    small_all = _allgather_small(small, name="gather_small").reshape(ndev, small_rows, D)
    dmod_all = small_all[:, :NB * 6].reshape(ndev * NB, 6 * D)
    sums = _sum_devices(small_all, name="sum_small")
    g_b_ada = (sums[0:6] + sums[6:12]).reshape(1, 6 * D)
    g_ln1_g, g_ln1_b, g_ln2_g, g_ln2_b = (sums[12 + n][None] for n in range(4))
    g_sinks = sums[16, :A_Q_HEADS][None]
    loss = sums[16, A_Q_HEADS]
    dmod_sh = lax.dynamic_slice(dmod_all, (0, chip * ada_cols), (ndev * NB, ada_cols))
    g_w_ada = _mm(c_act, dmod_sh, ta=True, name="ada_dw")

    names = ["w_ada", "b_ada", "w_in", "sinks", "w_branch_a", "w_branch_b", "w_o", "ln1_g", "ln1_b",
             "w_gate_up", "w_down", "ln2_g", "ln2_b"]
    ws = [w_ada, b_ada, w_in, sinks, w_branch_a, w_branch_b, w_o, ln1_g, ln1_b, w_gate_up, w_down, ln2_g, ln2_b]
    ms = [m_w_ada, m_b_ada, m_w_in, m_sinks, m_w_branch_a, m_w_branch_b, m_w_o, m_ln1_g, m_ln1_b, m_w_gate_up,
          m_w_down, m_ln2_g, m_ln2_b]
    vs = [v_w_ada, v_b_ada, v_w_in, v_sinks, v_w_branch_a, v_w_branch_b, v_w_o, v_ln1_g, v_ln1_b, v_w_gate_up,
          v_w_down, v_ln2_g, v_ln2_b]
    gs = [g_w_ada, g_b_ada, g_w_in, g_sinks, g_w_a, g_w_b, g_w_o, g_ln1_g, g_ln1_b, g_w_gu, g_w_d, g_ln2_g, g_ln2_b]
    grads, deltas, new_ms, new_vs = [], [], [], []
    for name, w, g, m, v in zip(names, ws, gs, ms, vs):
        flip = tr if name == "w_in" else (lambda a: a)
        w, m, v = flip(w), flip(m), flip(v)
        g2 = g.reshape(w.shape[-2:])
        d, nm, nv = _adamw(w, g2, m, v, name="adamw_" + name)
        grads.append(flip(g2.reshape(w.shape)))
        deltas.append(flip(d))
        new_ms.append(flip(nm))
        new_vs.append(flip(nv))
    return (loss, res["grad_x"], *grads, *deltas, *new_ms, *new_vs)
```

```python
import functools

import jax
import jax.numpy as jnp
from jax import lax
from jax.experimental import pallas as pl
from jax.experimental.pallas import tpu as pltpu

F32 = jnp.float32
BF16 = jnp.bfloat16
MESH = pl.DeviceIdType.MESH

HEAD_DIM = 64
LANES = 128
PAIR_W = 2 * HEAD_DIM
BLOCK = 128
A_Q_HEADS = 16
A_KV_HEADS = 2
A_WINDOW = 128
B_PATTERNS = ((128, 1), (512, 4), (2048, 16))
B_GROUP_HEADS = 8
QA_W = A_Q_HEADS * HEAD_DIM
KA_W = A_KV_HEADS * HEAD_DIM
GB_W = B_GROUP_HEADS * HEAD_DIM
QB_W = GB_W * len(B_PATTERNS)
A_W = QA_W + 2 * KA_W
VAR_W = 3 * GB_W
N_VAR = 1 + len(B_PATTERNS)
VAR_DIL = (1,) + tuple(r for _, r in B_PATTERNS)
QKV_P = N_VAR * VAR_W
ROPE_THETA = 10000.0
LN_EPS = 1e-5
NEG_INF = -1e30
DEPTH = 1
ALPHA = (2 * DEPTH) ** 0.25
SCALE = HEAD_DIM ** -0.5

ADAM_LR, ADAM_B1, ADAM_B2, ADAM_EPS, ADAM_WD, ADAM_STEP = 0.001, 0.9, 0.999, 1e-08, 0.01, 10

VMEM_LIMIT_BYTES = 56 * 1024 * 1024
MM_TILE_BYTES = 36 * 1024 * 1024
MM_WHOLE_K = 4096


def _params(sem=None):
    return pltpu.CompilerParams(dimension_semantics=sem, vmem_limit_bytes=VMEM_LIMIT_BYTES)


_RIDES = {}


def _pcall(body, *, name, **kw):
    rides = _RIDES.pop(name, None)
    if rides is None:
        return pl.pallas_call(body, name=name, **kw)
    return _riding_call(body, rides, name=name, **kw)


def _copies(src_refs, dst_refs, send_sems, recv_sems, plan):
    x, y, c = lax.axis_index("x"), lax.axis_index("y"), lax.axis_index("c")
    remote = plan(x, y, c)
    nrem = len(remote)
    at = lambda ref, idx: ref.at[idx] if idx else ref

    def copy(a, n, landing):
        si, di, ri, peer = remote[n]
        return pltpu.make_async_remote_copy(
            src_ref=at(src_refs[a], si), dst_ref=at(dst_refs[a], ri if landing else di),
            send_sem=send_sems.at[a * nrem + n], recv_sem=recv_sems.at[a * nrem + n],
            device_id=peer, device_id_type=MESH)

    order = [(a, n) for a in range(len(dst_refs)) for n in range(nrem)]

    def start():
        for a, n in order:
            copy(a, n, False).start()

    def wait():
        for a, n in order:
            copy(a, n, True).wait_recv()
        for a, n in order:
            copy(a, n, False).wait_send()

    return start, wait


class _Ride:
    def __init__(self, srcs, dsts, plan, dst_inits=None):
        self.srcs, self.dsts, self.plan, self.dst_inits, self.out = srcs, dsts, plan, dst_inits, None


def _riding_call(body, rides, *, name, grid, in_specs, out_specs, out_shape, scratch_shapes=(), **kw):
    single = not isinstance(out_specs, (list, tuple))
    out_specs = [out_specs] if single else list(out_specs)
    out_shape = [out_shape] if single else list(out_shape)
    n_in, n_out, n_scr = len(in_specs), len(out_specs), len(scratch_shapes)
    xin, xdsts, sems, aliases, layout = [], [], [], {}, []
    for ride in rides:
        srcs = ride.srcs() if callable(ride.srcs) else ride.srcs
        inits = ride.dst_inits() if callable(ride.dst_inits) else ride.dst_inits
        na, nrem = len(ride.dsts), len(ride.plan(0, 0, 0))
        src_at = len(xin) if srcs is not None else None
        xin += list(srcs) if srcs is not None else []
        if inits is not None:
            aliases.update({n_in + len(xin) + a: n_out + len(xdsts) + a for a in range(na)})
            xin += list(inits)
        layout.append((src_at, len(xdsts), na))
        xdsts += list(ride.dsts)
        sems += [pltpu.SemaphoreType.DMA((na * nrem,)), pltpu.SemaphoreType.DMA((na * nrem,))]

    def wrapped(*refs):
        ins, xins = refs[:n_in], refs[n_in:n_in + len(xin)]
        outs = refs[n_in + len(xin):n_in + len(xin) + n_out]
        xouts = refs[n_in + len(xin) + n_out:n_in + len(xin) + n_out + len(xdsts)]
        scr = refs[n_in + len(xin) + n_out + len(xdsts):]
        rounds = []
        for k, (ride, (src_at, dst_at, na)) in enumerate(zip(rides, layout)):
            dsts = xouts[dst_at:dst_at + na]
            srcs = dsts if src_at is None else xins[src_at:src_at + na]
            rounds.append(_copies(srcs, dsts, scr[n_scr + 2 * k], scr[n_scr + 2 * k + 1], ride.plan))
        ids = [pl.program_id(a) for a in range(len(grid))]
        first = functools.reduce(jnp.logical_and, [i == 0 for i in ids])
        last = functools.reduce(jnp.logical_and, [i == g - 1 for i, g in zip(ids, grid)])

        @pl.when(first)
        def _():
            for start, _ in rounds:
                start()

        body(*ins, *outs, *scr[:n_scr])

        @pl.when(last)
        def _():
            for _, wait in rounds:
                wait()

    hbm = pl.BlockSpec(memory_space=pl.ANY)

    def run(*args):
        res = pl.pallas_call(
            wrapped, name=name, grid=grid, in_specs=list(in_specs) + [hbm] * len(xin),
            out_specs=out_specs + [hbm] * len(xdsts), out_shape=out_shape + xdsts,
            scratch_shapes=list(scratch_shapes) + sems, input_output_aliases=aliases,
            compiler_params=_params(("arbitrary",) * len(grid)),
        )(*args, *xin)
        for ride, (_, dst_at, na) in zip(rides, layout):
            ride.out = list(res[n_out + dst_at:n_out + dst_at + na])
        return res[0] if single else list(res[:n_out])

    return run


def _pick(n, target, quantum=128):
    t = (min(target, n) // quantum) * quantum
    while t >= quantum:
        if n % t == 0:
            return t
        t -= quantum
    return n


def _mm(a, b, *, name, ta=False, tb=False, b3=False, out3=0, out_dtype=F32, add=None, tm=1024, tn=1536, tk=1536):
    if ta:
        K, M = a.shape
    else:
        M, K = a.shape
    if b3 and tb:
        Nn, K2, tk = b.shape[1], b.shape[0] * b.shape[2], b.shape[2]
    elif b3:
        K2, Nn, tn = b.shape[1], b.shape[0] * b.shape[2], b.shape[2]
    elif tb:
        Nn, K2 = b.shape
    else:
        K2, Nn = b.shape
    assert K == K2, (a.shape, b.shape)
    if out3:
        tn = Nn // out3
    tm, tn, tk = _pick(M, tm), _pick(Nn, tn), _pick(K, tk)
    if not (b3 and tb) and K <= MM_WHOLE_K:
        tk = K
        fits = lambda: 4 * tk * (tm + tn) + 8 * tm * tn * (2 if add is not None else 1) <= MM_TILE_BYTES
        while not fits():
            if (tm >= tn or b3 or out3) and tm > 256:
                tm = _pick(M, tm - 128)
            elif not (b3 or out3) and tn > 256:
                tn = _pick(Nn, tn - 128)
            else:
                break
    nk = K // tk
    j_outer = K * Nn + (Nn // tn) * M * K < M * K + (M // tm) * K * Nn
    dn = (((0 if ta else 1,), (1 if tb else 0,)), ((), ()))

    def body(*refs):
        refs = list(refs)
        a_ref, b_ref = refs[:2]
        add_ref = refs[2] if add is not None else None
        o_ref = refs[3] if add is not None else refs[2]
        part = lax.dot_general(a_ref[...].astype(BF16), b_ref[...].astype(BF16), dn, preferred_element_type=F32)

        def finish(r):
            if add is not None:
                r = r + add_ref[...]
            o_ref[...] = r.astype(out_dtype)

        if nk == 1:
            finish(part)
            return
        acc = refs[-1]
        k = pl.program_id(2)

        @pl.when(k == 0)
        def _():
            acc[...] = part

        @pl.when(k > 0)
        def _():
            acc[...] += part

        @pl.when(k == nk - 1)
        def _():
            finish(acc[...])

    def spec(shape, index):
        return pl.BlockSpec(shape, (lambda j, i, k: index(i, j, k)) if j_outer else index)

    a_spec = spec((tk, tm), lambda i, j, k: (k, i)) if ta else spec((tm, tk), lambda i, j, k: (i, k))
    if b3 and tb:
        b_spec = spec((None, tn, tk), lambda i, j, k: (k, j, 0))
    elif b3:
        b_spec = spec((None, tk, tn), lambda i, j, k: (j, k, 0))
    elif tb:
        b_spec = spec((tn, tk), lambda i, j, k: (j, k))
    else:
        b_spec = spec((tk, tn), lambda i, j, k: (k, j))
    if out3:
        o_spec = spec((None, tm, tn), lambda i, j, k: (j, i, 0))
    else:
        o_spec = spec((tm, tn), lambda i, j, k: (i, j))
    ins, specs = [a, b], [a_spec, b_spec]
    if add is not None:
        ins.append(add)
        specs.append(o_spec)
    grid = (Nn // tn, M // tm, nk) if j_outer else (M // tm, Nn // tn, nk)
    return _pcall(
        body, name=name, grid=grid, in_specs=specs, out_specs=o_spec,
        out_shape=jax.ShapeDtypeStruct((out3, M, tn) if out3 else (M, Nn), out_dtype),
        scratch_shapes=[pltpu.VMEM((tm, tn), F32)] if nk > 1 else [],
        compiler_params=_params(("parallel", "parallel", "arbitrary")),
    )(*ins)


def _mm_multi(a_list, b_list, *, name, add=None, out_dtype=F32, tm=512):
    M = a_list[0].shape[0]
    tm = _pick(M, tm)
    ns = len(a_list)
    b_arrs, b_specs = [], []
    for b in b_list:
        arr, shp, idx = b if isinstance(b, tuple) else (b, b.shape, (0, 0))
        b_arrs.append(arr)
        b_specs.append(pl.BlockSpec(shp, lambda i, idx=idx: idx))
    Nn = b_specs[0].block_shape[1]
    dn = (((1,), (0,)), ((), ()))

    def body(*refs):
        a_refs, b_refs = refs[:ns], refs[ns:2 * ns]
        acc = None
        for a_ref, b_ref in zip(a_refs, b_refs):
            part = lax.dot_general(a_ref[...].astype(BF16), b_ref[...], dn, preferred_element_type=F32)
            acc = part if acc is None else acc + part
        if add is not None:
            acc = acc + refs[2 * ns][...]
        refs[-1][...] = acc.astype(out_dtype)

    o_spec = pl.BlockSpec((tm, Nn), lambda i: (i, 0))
    specs = [pl.BlockSpec((tm, a.shape[1]), lambda i: (i, 0)) for a in a_list] + b_specs
    ins = list(a_list) + b_arrs
    if add is not None:
        specs.append(o_spec)
        ins.append(add)
    return _pcall(body, name=name, grid=(M // tm,), in_specs=specs, out_specs=o_spec,
                  out_shape=jax.ShapeDtypeStruct((M, Nn), out_dtype), compiler_params=_params(("parallel",)))(*ins)


def _lane(shape):
    return lax.broadcasted_iota(jnp.int32, shape, len(shape) - 1)


def _rot_half(v):
    w = v.shape[-1]
    first = (_lane(v.shape) % HEAD_DIM) < (HEAD_DIM // 2)
    return jnp.where(first, pltpu.roll(v, w - HEAD_DIM // 2, v.ndim - 1), pltpu.roll(v, HEAD_DIM // 2, v.ndim - 1))


def _widen(t, w):
    return t if w == t.shape[-1] else jnp.concatenate([t] * (w // t.shape[-1]), axis=-1)


def _unrope(v, cos, sins):
    w = v.shape[-1]
    return v * _widen(cos, w) - _rot_half(v) * _widen(sins, w)


def _rope_tables(positions):
    half = HEAD_DIM // 2
    inv = ROPE_THETA ** (-jnp.arange(half, dtype=F32) / half)
    ang = positions.astype(F32)[..., None] * inv
    cos, sin = jnp.cos(ang), jnp.sin(ang)
    cosf = jnp.concatenate([cos, cos, cos, cos], axis=-1)
    sins = jnp.concatenate([-sin, sin, -sin, sin], axis=-1)
    n = positions.shape[0] * positions.shape[1]
    return cosf.reshape(n, PAIR_W), sins.reshape(n, PAIR_W)


def _inproj(x2, scale, shift, w, cosf, sins, flags, *, T, name):
    N, D = x2.shape
    tm, tn = _pick(T, 512), VAR_W
    tpe = T // tm

    def body(x_ref, sc_ref, sh_ref, w_ref, c_ref, s_ref, f_ref, *outs):
        o_refs, u_ref = outs[:N_VAR], outs[N_VAR]
        j = pl.program_id(1)

        @pl.when(j == 0)
        def _():
            u_ref[...] = (x_ref[...] * (1.0 + sc_ref[0]) + sh_ref[0]).astype(BF16)

        acc = lax.dot_general(u_ref[...], w_ref[...], (((1,), (1,)), ((), ())), preferred_element_type=F32)
        fl = f_ref[...]
        ce = 1.0 + (_widen(c_ref[...], tn) - 1.0) * fl
        se = _widen(s_ref[...], tn) * fl
        res = acc * ce + _rot_half(acc) * se
        for v in range(N_VAR):
            @pl.when(j == v)
            def _(v=v):
                _to_view(res, o_refs[v], outs[N_VAR + 1], VAR_DIL[v])

    ex = pl.BlockSpec((1, 1, D), lambda i, j: (i // tpe, 0, 0))
    tab = pl.BlockSpec((tm, PAIR_W), lambda i, j: (i, 0))
    keep = lambda w_: pl.BlockSpec((tm, w_), lambda i, j: (i, 0))
    vspec = lambda r: pl.BlockSpec((None, tm // r, r * tn), lambda i, j: (i // tpe, i % tpe, 0))
    vshape = lambda r: jax.ShapeDtypeStruct((N // T, T // r, r * tn), BF16)
    return _pcall(
        body, name=name, grid=(N // tm, N_VAR),
        in_specs=[keep(D), ex, ex, pl.BlockSpec((tn, D), lambda i, j: (j, 0)), tab, tab,
                  pl.BlockSpec((1, tn), lambda i, j: (0, j))],
        out_specs=[vspec(r) for r in VAR_DIL] + [keep(D)],
        out_shape=[vshape(r) for r in VAR_DIL] + [jax.ShapeDtypeStruct((N, D), BF16)],
        scratch_shapes=[pltpu.VMEM((tn // LANES, tm, LANES), F32)],
        compiler_params=_params(("parallel", "arbitrary")),
    )(x2, scale, shift, w, cosf, sins, flags)


class _Geom:
    def __init__(self, g):
        if g is None:
            self.r, self.nq, self.n_back, self.sink = 1, A_Q_HEADS, A_WINDOW - 1, True
            self.qw, self.kw = QA_W, KA_W
            self.qidx = lambda j: 0
            self.kidx = lambda j: QA_W // KA_W
            self.vidx = lambda j: QA_W // KA_W + 1
        else:
            window, r = B_PATTERNS[g]
            self.r, self.nq, self.n_back, self.sink = r, B_GROUP_HEADS, window // r, False
            self.qw, self.kw = GB_W, GB_W
            self.qidx = lambda j: 3 * j
            self.kidx = lambda j: 3 * j + 1
            self.vidx = lambda j: 3 * j + 2
        self.ntile = self.qw // PAIR_W


def _stack_heads(t, scale=None):
    first = _lane(t.shape) < HEAD_DIM
    z = jnp.zeros_like(t)
    if scale is not None:
        t = t * jnp.asarray(scale, t.dtype)
    return jnp.concatenate([jnp.where(first, t, z), jnp.where(first, z, t)], axis=0)


def _lse_col(t):
    return jnp.concatenate([t[:, 0:1], t[:, HEAD_DIM:HEAD_DIM + 1]], axis=0)


def _lse_rows(t, width):
    first = _lane(t.shape) < HEAD_DIM
    other = pltpu.roll(t, HEAD_DIM, 1)
    full = jnp.concatenate([jnp.where(first, t, other), jnp.where(first, other, t)], axis=0)
    return _widen(full, width)


def _unstack_heads(v2):
    return jnp.where(_lane((BLOCK, PAIR_W)) < HEAD_DIM, v2[:BLOCK], v2[BLOCK:])


def _dup_head(t, kh):
    tf = t.astype(F32)
    keep = (_lane(t.shape) < HEAD_DIM) if kh == 0 else (_lane(t.shape) >= HEAD_DIM)
    return jnp.where(keep, tf, pltpu.roll(tf, HEAD_DIM, 1)).astype(t.dtype)


def _fold_heads(t):
    return t + pltpu.roll(t, HEAD_DIM, 1)


def _band_mask(rows, i, n_back, single):
    nkeys = BLOCK if single else 2 * BLOCK
    qi = jnp.bitwise_and(lax.broadcasted_iota(jnp.int32, (rows, nkeys), 0), BLOCK - 1)
    ki = lax.broadcasted_iota(jnp.int32, (rows, nkeys), 1)
    if single:
        return qi >= ki
    dist = qi + BLOCK - ki
    return jnp.logical_and(jnp.logical_and(dist >= 0, dist <= n_back), jnp.logical_or(ki >= BLOCK, i > 0))


def _per_block(col, scalars, fn):
    return jnp.concatenate([fn(col[b * BLOCK:(b + 1) * BLOCK], sc) for b, sc in enumerate(scalars)], axis=0)


def _sink_slot(rows):
    qi = jnp.bitwise_and(lax.broadcasted_iota(jnp.int32, (rows, 2 * BLOCK), 0), BLOCK - 1)
    return qi == lax.broadcasted_iota(jnp.int32, (rows, 2 * BLOCK), 1)


def _sink_scores(rows, sinks):
    blk = lax.broadcasted_iota(jnp.int32, (rows, 2 * BLOCK), 0) // BLOCK
    out = jnp.full((rows, 2 * BLOCK), sinks[-1], F32)
    for b in range(len(sinks) - 2, -1, -1):
        out = jnp.where(blk == b, sinks[b], out)
    return out


def _softmax_parts(s, valid, sinks):
    s = jnp.where(valid, s, NEG_INF)
    if sinks is not None:
        slot = _sink_slot(s.shape[0])
        s = jnp.where(slot, _sink_scores(s.shape[0], sinks), s)
    m = jnp.max(s, axis=1, keepdims=True)
    p = jnp.exp(s - m)
    den = jnp.sum(p, axis=1, keepdims=True)
    if sinks is not None:
        p = jnp.where(slot, 0.0, p)
    return p, m, den


_NT = (((1,), (1,)), ((), ()))
_TN = (((0,), (0,)), ((), ()))


def _rows2(prev_ref, cur_ref, cs, single=False):
    if single:
        return cur_ref[0, :, cs]
    return jnp.concatenate([prev_ref[0, :, cs], cur_ref[0, :, cs]], axis=0)


def _sink_scalars(sink_ref, first, nblocks):
    return [sink_ref[first + b] for b in range(nblocks)]


def _tile(t):
    return slice(t * PAIR_W, (t + 1) * PAIR_W)


def _attn_fwd(qkv, sinks, g, *, NB, T, name):
    geo = _Geom(g)
    r, qw, kw, ntile = geo.r, geo.qw, geo.kw, geo.ntile
    tsub = T // r
    nblk = tsub // BLOCK
    qkv3 = qkv.reshape(NB, tsub, r * VAR_W)
    out_dtype = BF16 if g is None else F32
    tiles_per_kv = ntile // A_KV_HEADS

    single = nblk == 1

    def body(q_ref, kp_ref, kc_ref, vp_ref, vc_ref, sink_ref, o_ref, l_ref):
        i = pl.program_id(2)
        if geo.sink:
            kall, vall = _rows2(kp_ref, kc_ref, _tile(0)), _rows2(vp_ref, vc_ref, _tile(0))
            kdup = [_dup_head(kall, kh) for kh in range(A_KV_HEADS)]
            vdup = [_dup_head(vall, kh) for kh in range(A_KV_HEADS)]
            tiles = [[t] for t in range(ntile)]
            q2s = [_stack_heads(q_ref[0, :, _tile(t)], SCALE) for t in range(ntile)]
            kks = [kdup[t // tiles_per_kv] for t in range(ntile)]
            vvs = [vdup[t // tiles_per_kv] for t in range(ntile)]
            sinkcols = [_sink_scalars(sink_ref, 2 * t, 2) for t in range(ntile)]
        else:
            tiles = [[t] for t in range(ntile)]
            q2s = [_stack_heads(q_ref[0, :, _tile(t)], SCALE) for t in range(ntile)]
            kks = [_rows2(kp_ref, kc_ref, _tile(t), single) for t in range(ntile)]
            vvs = [_rows2(vp_ref, vc_ref, _tile(t), single) for t in range(ntile)]
            sinkcols = [None] * ntile
        valid = _band_mask(q2s[0].shape[0], i, geo.n_back, single)
        ss = [lax.dot_general(q2, kk, _NT, preferred_element_type=F32) for q2, kk in zip(q2s, kks)]
        parts = [_softmax_parts(s, valid, sc) for s, sc in zip(ss, sinkcols)]
        o2s = [jnp.dot(p.astype(BF16), vv, preferred_element_type=F32) / den for (p, m, den), vv in zip(parts, vvs)]
        for ts, o2, (p, m, den) in zip(tiles, o2s, parts):
            lse2 = jnp.broadcast_to(m + jnp.log(den), (o2.shape[0], PAIR_W))
            for n, t in enumerate(ts):
                rows = slice(2 * BLOCK * n, 2 * BLOCK * (n + 1))
                o_ref[0, :, _tile(t)] = _unstack_heads(o2[rows]).astype(out_dtype)
                l_ref[0, :, _tile(t)] = _unstack_heads(lse2[rows])

    prev = lambda i: jnp.maximum(i - 1, 0)
    in_specs = [
        pl.BlockSpec((1, BLOCK, qw), lambda b, j, i: (b, i, geo.qidx(j))),
        pl.BlockSpec((1, BLOCK, kw), lambda b, j, i: (b, prev(i), geo.kidx(j))),
        pl.BlockSpec((1, BLOCK, kw), lambda b, j, i: (b, i, geo.kidx(j))),
        pl.BlockSpec((1, BLOCK, kw), lambda b, j, i: (b, prev(i), geo.vidx(j))),
        pl.BlockSpec((1, BLOCK, kw), lambda b, j, i: (b, i, geo.vidx(j))),
        pl.BlockSpec(memory_space=pltpu.SMEM),
    ]
    o_spec = pl.BlockSpec((1, BLOCK, qw), lambda b, j, i: (b, i, j))
    shape = (NB, tsub, r * qw)
    o, lse = _pcall(
        body, name=name, grid=(NB, r, nblk), in_specs=in_specs, out_specs=[o_spec, o_spec],
        out_shape=[jax.ShapeDtypeStruct(shape, out_dtype), jax.ShapeDtypeStruct(shape, F32)],
        compiler_params=_params(("parallel", "parallel", "arbitrary")),
    )(qkv3, qkv3, qkv3, qkv3, qkv3, sinks)
    return o, lse


def _attn_bwd(qkv, do, lse, dlse, cosf, sins, sinks, g, *, NB, T, name):
    geo = _Geom(g)
    r, qw, kw, ntile = geo.r, geo.qw, geo.kw, geo.ntile
    tsub = T // r
    nblk = tsub // BLOCK
    view = lambda a, w: a.reshape(NB, tsub, r * w)
    has_dlse = dlse is not None
    tiles_per_kv = ntile // A_KV_HEADS

    single = nblk == 1
    krows = BLOCK if single else 2 * BLOCK
    nsteps = 1 if single else nblk + 1

    def grads(q2s, kks, vvs, do2s, i, lsecols, sinkcols, dlcols):
        valid = _band_mask(q2s[0].shape[0], i, geo.n_back, single)
        ss = [lax.dot_general(q2, kk, _NT, preferred_element_type=F32) for q2, kk in zip(q2s, kks)]
        dps = [lax.dot_general(do2, vv, _NT, preferred_element_type=F32) for do2, vv in zip(do2s, vvs)]
        ps, dss, sks = [], [], []
        for s, dp, ls, sc, dl in zip(ss, dps, lsecols, sinkcols, dlcols):
            sv = jnp.where(valid, s, NEG_INF)
            if sc is not None:
                slot = _sink_slot(s.shape[0])
                sv = jnp.where(slot, _sink_scores(s.shape[0], sc), sv)
                dp = jnp.where(slot, 0.0, dp)
            p = jnp.exp(sv - ls)
            delta = jnp.sum(p * dp, axis=1, keepdims=True)
            if dl is not None:
                delta = delta - dl
            ds = p * (dp - delta)
            if sc is not None:
                blk = lambda a, b: a[b * BLOCK:(b + 1) * BLOCK]
                sks.append([jnp.sum(jnp.where(blk(slot, b), blk(ds, b), 0.0)) for b in range(len(sc))])
                ds, p = jnp.where(slot, 0.0, ds), jnp.where(slot, 0.0, p)
            else:
                sks.append(None)
            ps.append(p.astype(BF16))
            dss.append(ds.astype(BF16))
        dq2s = [jnp.dot(ds, kk, preferred_element_type=F32) * SCALE for ds, kk in zip(dss, kks)]
        dkks = [lax.dot_general(ds, q2, _TN, preferred_element_type=F32) for ds, q2 in zip(dss, q2s)]
        dvvs = [lax.dot_general(p, do2, _TN, preferred_element_type=F32) for p, do2 in zip(ps, do2s)]
        return dq2s, dkks, dvvs, sks

    def body(*refs):
        it = iter(refs)
        q_ref, kp_ref, kc_ref, vp_ref, vc_ref, do_ref, l_ref = (next(it) for _ in range(7))
        dl_ref = next(it) if has_dlse else None
        c_ref, s_ref, sink_ref, o_ref, ds_ref, dq_s, dk_s, dv_s, car_q, car_k, car_v = (next(it) for _ in range(11))
        b, j, i = pl.program_id(0), pl.program_id(1), pl.program_id(2)

        @pl.when(jnp.logical_and(b == 0, jnp.logical_and(j == 0, i == 0)))
        def _():
            ds_ref[...] = jnp.zeros_like(ds_ref)

        def compute():
            if geo.sink:
                kall, vall = _rows2(kp_ref, kc_ref, _tile(0)), _rows2(vp_ref, vc_ref, _tile(0))
                nb = 2 * tiles_per_kv
                tiles = [[kh * tiles_per_kv + t for t in range(tiles_per_kv)] for kh in range(A_KV_HEADS)]
                cat = lambda f, ts: jnp.concatenate([f(t) for t in ts], axis=0)
                dq2s, dkks, dvvs, sks = grads(
                    [cat(lambda t: _stack_heads(q_ref[0, :, _tile(t)], SCALE), ts) for ts in tiles],
                    [_dup_head(kall, kh) for kh in range(A_KV_HEADS)],
                    [_dup_head(vall, kh) for kh in range(A_KV_HEADS)],
                    [cat(lambda t: _stack_heads(do_ref[0, :, _tile(t)]), ts) for ts in tiles], i,
                    [cat(lambda t: _lse_rows(l_ref[0, :, _tile(t)], krows), ts) for ts in tiles],
                    [_sink_scalars(sink_ref, kh * nb, nb) for kh in range(A_KV_HEADS)], [None] * A_KV_HEADS)
                lane1 = _lane((1, PAIR_W))
                dsink = jnp.zeros((1, PAIR_W), F32)
                for kh, (ts, dq2, sk) in enumerate(zip(tiles, dq2s, sks)):
                    for n, t in enumerate(ts):
                        dq_s[:, _tile(t)] = _unstack_heads(dq2[2 * BLOCK * n:2 * BLOCK * (n + 1)])
                    for bb in range(nb):
                        dsink = dsink + jnp.where(lane1 == kh * nb + bb, sk[bb], 0.0)
                second = _lane((krows, PAIR_W)) >= HEAD_DIM
                dk_s[...] = jnp.where(second, _fold_heads(dkks[1]), _fold_heads(dkks[0]))
                dv_s[...] = jnp.where(second, _fold_heads(dvvs[1]), _fold_heads(dvvs[0]))
                ds_ref[0:1, :] += dsink
            else:
                dq2s, dkks, dvvs, _ = grads(
                    [_stack_heads(q_ref[0, :, _tile(t)], SCALE) for t in range(ntile)],
                    [_rows2(kp_ref, kc_ref, _tile(t), single) for t in range(ntile)],
                    [_rows2(vp_ref, vc_ref, _tile(t), single) for t in range(ntile)],
                    [_stack_heads(do_ref[0, :, _tile(t)]) for t in range(ntile)], i,
                    [_lse_rows(l_ref[0, :, _tile(t)], krows) for t in range(ntile)], [None] * ntile,
                    [_lse_col(dl_ref[0, :, _tile(t)]) for t in range(ntile)])
                for t in range(ntile):
                    dq_s[:, _tile(t)] = _unstack_heads(dq2s[t])
                    dk_s[0:krows, _tile(t)] = dkks[t]
                    dv_s[0:krows, _tile(t)] = dvvs[t]

        def emit(dq, dk, dv):
            cos, sn = c_ref[0], s_ref[0]
            o_ref[0, :, 0:qw] = _unrope(dq, cos, sn).astype(BF16)
            o_ref[0, :, qw:qw + kw] = _unrope(dk, cos, sn).astype(BF16)
            o_ref[0, :, qw + kw:qw + 2 * kw] = dv.astype(BF16)
            if qw + 2 * kw < VAR_W:
                o_ref[0, :, qw + 2 * kw:VAR_W] = jnp.zeros((BLOCK, VAR_W - qw - 2 * kw), BF16)

        if single:
            compute()
            emit(dq_s[...], dk_s[0:BLOCK, :], dv_s[0:BLOCK, :])
            return

        @pl.when(i == 0)
        def _():
            car_q[...] = jnp.zeros_like(car_q)
            car_k[...] = jnp.zeros_like(car_k)
            car_v[...] = jnp.zeros_like(car_v)

        @pl.when(i == nblk)
        def _():
            dk_s[...] = jnp.zeros_like(dk_s)
            dv_s[...] = jnp.zeros_like(dv_s)

        pl.when(i < nblk)(compute)
        emit(car_q[...], car_k[...] + dk_s[0:BLOCK, :], car_v[...] + dv_s[0:BLOCK, :])
        car_q[...] = dq_s[...]
        car_k[...] = dk_s[BLOCK:2 * BLOCK, :]
        car_v[...] = dv_s[BLOCK:2 * BLOCK, :]

    cur = lambda i: jnp.minimum(i, nblk - 1)
    prv = lambda i: jnp.maximum(jnp.minimum(i, nblk - 1) - 1, 0)
    outb = lambda i: jnp.maximum(i - 1, 0)
    qrow = pl.BlockSpec((1, BLOCK, qw), lambda b, j, i: (b, cur(i), j))
    in_specs = [
        pl.BlockSpec((1, BLOCK, qw), lambda b, j, i: (b, cur(i), geo.qidx(j))),
        pl.BlockSpec((1, BLOCK, kw), lambda b, j, i: (b, prv(i), geo.kidx(j))),
        pl.BlockSpec((1, BLOCK, kw), lambda b, j, i: (b, cur(i), geo.kidx(j))),
        pl.BlockSpec((1, BLOCK, kw), lambda b, j, i: (b, prv(i), geo.vidx(j))),
        pl.BlockSpec((1, BLOCK, kw), lambda b, j, i: (b, cur(i), geo.vidx(j))),
        qrow, qrow,
    ]
    ins = [view(qkv, VAR_W)] * 5 + [view(do, qw), view(lse, qw)]
    if has_dlse:
        in_specs.append(qrow)
        ins.append(view(dlse, qw))
    in_specs += [
        pl.BlockSpec((1, BLOCK, PAIR_W), lambda b, j, i: (b, outb(i), j)),
        pl.BlockSpec((1, BLOCK, PAIR_W), lambda b, j, i: (b, outb(i), j)),
        pl.BlockSpec(memory_space=pltpu.SMEM),
    ]
    ins += [view(cosf, PAIR_W), view(sins, PAIR_W), sinks]
    scratch = [pltpu.VMEM((BLOCK, qw), F32), pltpu.VMEM((2 * BLOCK, kw), F32), pltpu.VMEM((2 * BLOCK, kw), F32),
               pltpu.VMEM((BLOCK, qw), F32), pltpu.VMEM((BLOCK, kw), F32), pltpu.VMEM((BLOCK, kw), F32)]
    dqkv, dsink = _pcall(
        body, name=name, grid=(NB, r, nsteps), in_specs=in_specs,
        out_specs=[pl.BlockSpec((1, BLOCK, VAR_W), lambda b, j, i: (b, outb(i), j)),
                   pl.BlockSpec((8, PAIR_W), lambda b, j, i: (0, 0))],
        out_shape=[jax.ShapeDtypeStruct((NB, tsub, r * VAR_W), BF16), jax.ShapeDtypeStruct((8, PAIR_W), F32)],
        scratch_shapes=scratch, compiler_params=_params(("arbitrary", "arbitrary", "arbitrary")),
    )(*ins)
    return dqkv.reshape(NB * T, VAR_W), dsink


class _Rows:
    def __init__(self, N, T, tm):
        self.N, self.tm, self.tpe, self.grid = N, tm, T // tm, (N // tm,)

    def row(self, w, col=0):
        return pl.BlockSpec((self.tm, w), lambda i: (i, col))

    def ex(self, w):
        return pl.BlockSpec((1, 1, w), lambda i: (i // self.tpe, 0, 0))

    def const(self, shape):
        return pl.BlockSpec(shape, lambda i: tuple(0 for _ in shape))

    def view(self, w, r):
        return pl.BlockSpec((None, self.tm // r, r * w), lambda i: (i // self.tpe, i % self.tpe, 0))

    def first_of_example(self):
        return pl.program_id(0) % self.tpe == 0


def _acc(ref, first, val):
    @pl.when(first)
    def _():
        ref[0] = val

    @pl.when(jnp.logical_not(first))
    def _():
        ref[0] += val


def _colsum(v):
    return jnp.sum(v, axis=0, keepdims=True)


def _ln_stats(r):
    mu = jnp.mean(r, axis=-1, keepdims=True)
    xc = r - mu
    var = jnp.mean(xc * xc, axis=-1, keepdims=True)
    rstd = lax.rsqrt(var + LN_EPS)
    return xc * rstd, rstd


def _ln_bwd(dy, xhat, rstd, gain):
    dxh = dy * gain
    return rstd * (dxh - jnp.mean(dxh, axis=-1, keepdims=True) - xhat * jnp.mean(dxh * xhat, axis=-1, keepdims=True))


def _from_view(ref, scr, r):
    if r == 1:
        return ref[...]
    rows, w = ref.shape[0], ref.shape[1] // r
    for j in range(r):
        for c in range(w // LANES):
            scr.at[c][pl.ds(j, rows, stride=r), :] = ref[:, j * w + c * LANES:j * w + (c + 1) * LANES]
    return jnp.concatenate([scr[c] for c in range(w // LANES)], axis=1)


def _to_view(val, ref, scr, r):
    if r == 1:
        ref[...] = val.astype(ref.dtype)
        return
    rows, w = ref.shape[0], ref.shape[1] // r
    for c in range(w // LANES):
        scr[c] = val[:, c * LANES:(c + 1) * LANES]
    for j in range(r):
        for c in range(w // LANES):
            ref[:, j * w + c * LANES:j * w + (c + 1) * LANES] = scr.at[c][pl.ds(j, rows, stride=r), :].astype(ref.dtype)


def _silu_parts(v):
    s = jax.nn.sigmoid(v)
    return v * s, s * (1.0 + v * (1.0 - s))


def _local_step(x, mod, positions, w_in, rest_weights, sinks, ln1_g, ln1_b, ln2_g, ln2_b, target, hook=None):
    hook = hook or (lambda event, **data: None)
    NB, T, D = x.shape
    N = NB * T
    x2 = x.reshape(N, D)
    tgt2 = target.reshape(N, D)
    shift_m, scale_m, gate_m, shift_f, scale_f, gate_f = [mod[:, None, k * D:(k + 1) * D] for k in range(6)]
    cosf, sins = _rope_tables(positions)
    col = jnp.arange(QKV_P)
    vcol = col % VAR_W
    flags = jnp.where(col < VAR_W, vcol < QA_W + KA_W, vcol < 2 * GB_W).astype(F32)[None]
    R = _Rows(N, T, _pick(T, 256))
    sds = jax.ShapeDtypeStruct
    exsum = lambda w=D: sds((NB, 1, w), F32)
    ngrp = len(B_PATTERNS)

    *qkv, u = _inproj(x2, scale_m, shift_m, w_in, cosf, sins, flags, T=T, name="inproj_qkv")
    gates = _mm(u, w_in[QKV_P:], tb=True, name="inproj_gates")
    oa, la = _attn_fwd(qkv[0], sinks, None, NB=NB, T=T, name="attn_a_fwd")
    oa = oa.reshape(N, QA_W)
    ob_parts = [_attn_fwd(qkv[1 + g], sinks, g, NB=NB, T=T, name=f"attn_b{g}_fwd") for g in range(ngrp)]
    (o1, l1), (o2, l2), (o3, l3) = ob_parts
    w_a, w_b, w_o, w_gu, w_d = rest_weights()
    F = w_d.shape[0]
    dil = [r_ for _, r_ in B_PATTERNS]
    views = [R.view(GB_W, r_) for r_ in dil]
    tokbuf = pltpu.VMEM((GB_W // LANES, R.tm, LANES), F32)

    def merge_fwd(o1r, o2r, o3r, l1r, l2r, l3r, ob_ref, *bufs):
        os_ = [_from_view(ref, bufs[n], dil[n]) for n, ref in enumerate((o1r, o2r, o3r))]
        la, lb, lc = [_from_view(ref, bufs[3 + n], dil[n]) for n, ref in enumerate((l1r, l2r, l3r))]
        mx = jnp.maximum(jnp.maximum(la, lb), lc)
        ea, eb, ec = jnp.exp(la - mx), jnp.exp(lb - mx), jnp.exp(lc - mx)
        ob_ref[...] = ((ea * os_[0] + eb * os_[1] + ec * os_[2]) / (ea + eb + ec)).astype(BF16)

    ob = _pcall(merge_fwd, name="merge_fwd", grid=R.grid, in_specs=views + views, out_specs=R.row(GB_W),
                out_shape=sds((N, GB_W), BF16), scratch_shapes=[tokbuf] * 6,
                compiler_params=_params(("parallel",)))(o1, o2, o3, l1, l2, l3)

    ya = _mm(oa, w_a, name="branch_a")
    yb = _mm(ob, w_b, b3=True, name="branch_b")

    def gate_fwd(ya_r, yb_r, ga_r, gb_r, mg_ref):
        mg_ref[...] = (jax.nn.sigmoid(ga_r[...]) * ya_r[...] + jax.nn.sigmoid(gb_r[...]) * yb_r[...]).astype(BF16)

    merged = _pcall(gate_fwd, name="gate_fwd", grid=R.grid, in_specs=[R.row(D), R.row(D), R.row(D, 0), R.row(D, 1)],
                    out_specs=R.row(D), out_shape=sds((N, D), BF16),
                    compiler_params=_params(("parallel",)))(ya, yb, gates, gates)
    y = _mm(merged, w_o, name="out_proj")

    def norm1_fwd(x_r, y_r, gm_r, g_r, b_r, sf_r, hf_r, r1_ref, x1_ref, u2_ref):
        r1 = ALPHA * x_r[...] + (1.0 + gm_r[0]) * y_r[...]
        xhat, _ = _ln_stats(r1)
        x1 = xhat * g_r[...] + b_r[...]
        r1_ref[...] = r1
        x1_ref[...] = x1
        u2_ref[...] = (x1 * (1.0 + sf_r[0]) + hf_r[0]).astype(BF16)

    r1, x1, u2 = _pcall(
        norm1_fwd, name="norm1_fwd", grid=R.grid,
        in_specs=[R.row(D), R.row(D), R.ex(D), R.const((1, D)), R.const((1, D)), R.ex(D), R.ex(D)],
        out_specs=[R.row(D)] * 3, out_shape=[sds((N, D), F32), sds((N, D), F32), sds((N, D), BF16)],
        compiler_params=_params(("parallel",)))(x2, y, gate_m, ln1_g, ln1_b, scale_f, shift_f)

    tnf = w_gu.shape[2]
    nft = w_gu.shape[0] // 2
    tmf = _pick(N, 512)

    def ffn_up(u_r, wg_r, wu_r, hg_ref, hu_ref, a_ref):
        hg = jnp.dot(u_r[...], wg_r[...], preferred_element_type=F32)
        hu = jnp.dot(u_r[...], wu_r[...], preferred_element_type=F32)
        sl, _ = _silu_parts(hg)
        hg_ref[...] = hg.astype(BF16)
        hu_ref[...] = hu.astype(BF16)
        a_ref[...] = (sl * hu).astype(BF16)

    ftile = pl.BlockSpec((tmf, tnf), lambda j, i: (i, j))
    hg, hu, act = _pcall(
        ffn_up, name="ffn_up", grid=(nft, N // tmf),
        in_specs=[pl.BlockSpec((tmf, D), lambda j, i: (i, 0)), pl.BlockSpec((None, D, tnf), lambda j, i: (j, 0, 0)),
                  pl.BlockSpec((None, D, tnf), lambda j, i: (j + nft, 0, 0))],
        out_specs=[ftile] * 3, out_shape=[sds((N, F), BF16)] * 3,
        compiler_params=_params(("arbitrary", "parallel")))(u2, w_gu, w_gu)
    y2 = _mm(act, w_d, name="ffn_down")

    def norm2_loss_bwd(x1_r, y2_r, t_r, gf_r, g_r, b_r, dy2_ref, dx1_ref, dgf_ref, dg_ref, db_ref, loss_ref):
        first = R.first_of_example()
        y2v = y2_r[...]
        r2 = ALPHA * x1_r[...] + (1.0 + gf_r[0]) * y2v
        xhat, rstd = _ln_stats(r2)
        err = xhat * g_r[...] + b_r[...] - t_r[...]
        dx2 = err * (1.0 / D)
        dr2 = _ln_bwd(dx2, xhat, rstd, g_r[...])
        dy2_ref[...] = ((1.0 + gf_r[0]) * dr2).astype(BF16)
        dx1_ref[...] = ALPHA * dr2
        _acc(dgf_ref, first, _colsum(dr2 * y2v))
        _acc(dg_ref, first, _colsum(dx2 * xhat))
        _acc(db_ref, first, _colsum(dx2))
        part = 0.5 * jnp.sum(jnp.mean(err * err, axis=-1, keepdims=True))
        _acc(loss_ref, first, jnp.broadcast_to(part, (1, 128)))

    dy2, dx1p, dgate_f, dg2, db2, loss_p = _pcall(
        norm2_loss_bwd, name="norm2_loss_bwd", grid=R.grid,
        in_specs=[R.row(D), R.row(D), R.row(D), R.ex(D), R.const((1, D)), R.const((1, D))],
        out_specs=[R.row(D), R.row(D), R.ex(D), R.ex(D), R.ex(D), R.ex(128)],
        out_shape=[sds((N, D), BF16), sds((N, D), F32), exsum(), exsum(), exsum(), exsum(128)],
        compiler_params=_params(("arbitrary",)))(x1, y2, tgt2, gate_f, ln2_g, ln2_b)

    g_wd = _mm(act, dy2, ta=True, out_dtype=BF16, name="ffn_down_dw")

    tmd = _pick(N, 256)

    def ffn_down_dx(dy_r, wd_r, hg_r, hu_r, dh_ref):
        for t in range(nft):
            cs = slice(t * tnf, (t + 1) * tnf)
            da = lax.dot_general(dy_r[...], wd_r[cs, :], _NT, preferred_element_type=F32)
            sl, dsl = _silu_parts(hg_r[:, cs].astype(F32))
            dh_ref[:, cs] = (da * hu_r[:, cs].astype(F32) * dsl).astype(BF16)
            dh_ref[:, F + t * tnf:F + (t + 1) * tnf] = (da * sl).astype(BF16)

    rowd = lambda w_: pl.BlockSpec((tmd, w_), lambda i: (i, 0))
    dh = _pcall(
        ffn_down_dx, name="ffn_down_dx", grid=(N // tmd,),
        in_specs=[rowd(D), pl.BlockSpec((F, D), lambda i: (0, 0)), rowd(F), rowd(F)],
        out_specs=rowd(2 * F), out_shape=sds((N, 2 * F), BF16),
        compiler_params=_params(("parallel",)))(dy2, w_d, hg, hu)
    du2 = _mm(dh, w_gu, tb=True, b3=True, name="ffn_up_dx")
    g_wgu = _mm(u2, dh, ta=True, out3=w_gu.shape[0], out_dtype=BF16, name="ffn_up_dw")

    def norm1_bwd(dx1p_r, du2_r, x1_r, r1_r, y_r, sf_r, gm_r, g_r,
                  dxp_ref, dy_ref, dsf_ref, dhf_ref, dgm_ref, dg_ref, db_ref):
        first = R.first_of_example()
        du2v = du2_r[...]
        dx1 = dx1p_r[...] + du2v * (1.0 + sf_r[0])
        xhat, rstd = _ln_stats(r1_r[...])
        dr1 = _ln_bwd(dx1, xhat, rstd, g_r[...])
        dxp_ref[...] = ALPHA * dr1
        dy_ref[...] = ((1.0 + gm_r[0]) * dr1).astype(BF16)
        _acc(dsf_ref, first, _colsum(du2v * x1_r[...]))
        _acc(dhf_ref, first, _colsum(du2v))
        _acc(dgm_ref, first, _colsum(dr1 * y_r[...]))
        _acc(dg_ref, first, _colsum(dx1 * xhat))
        _acc(db_ref, first, _colsum(dx1))

    dxp, dy, dscale_f, dshift_f, dgate_m, dg1, db1 = _pcall(
        norm1_bwd, name="norm1_bwd", grid=R.grid,
        in_specs=[R.row(D)] * 5 + [R.ex(D), R.ex(D), R.const((1, D))],
        out_specs=[R.row(D), R.row(D)] + [R.ex(D)] * 5,
        out_shape=[sds((N, D), F32), sds((N, D), BF16)] + [exsum()] * 5,
        compiler_params=_params(("arbitrary",)))(dx1p, du2, x1, r1, y, scale_f, gate_m, ln1_g)

    dmerged = _mm(dy, w_o, tb=True, name="out_proj_dx")
    g_wo = _mm(merged, dy, ta=True, out_dtype=BF16, name="out_proj_dw")

    def gate_bwd(dm_r, ya_r, yb_r, ga_r, gb_r, dya_ref, dyb_ref, dg_ref):
        dm = dm_r[...]
        sa, sb = jax.nn.sigmoid(ga_r[...]), jax.nn.sigmoid(gb_r[...])
        dya_ref[...] = (dm * sa).astype(BF16)
        dyb_ref[...] = (dm * sb).astype(BF16)
        dg_ref[:, :D] = (dm * ya_r[...] * sa * (1.0 - sa)).astype(BF16)
        dg_ref[:, D:] = (dm * yb_r[...] * sb * (1.0 - sb)).astype(BF16)

    dya, dyb, dgates = _pcall(
        gate_bwd, name="gate_bwd", grid=R.grid, in_specs=[R.row(D)] * 3 + [R.row(D, 0), R.row(D, 1)],
        out_specs=[R.row(D), R.row(D), R.row(2 * D)],
        out_shape=[sds((N, D), BF16), sds((N, D), BF16), sds((N, 2 * D), BF16)],
        compiler_params=_params(("parallel",)))(dmerged, ya, yb, gates, gates)

    doa = _mm(dya, w_a, tb=True, out_dtype=BF16, name="branch_a_dx")
    g_wa = _mm(oa, dya, ta=True, out_dtype=BF16, name="branch_a_dw")
    dob = _mm(dyb, w_b, tb=True, b3=True, name="branch_b_dx")
    g_wb = _mm(ob, dyb, ta=True, out3=w_b.shape[0], out_dtype=BF16, name="branch_b_dw")
    hook("rest_grads", g_wa=g_wa, g_wb=g_wb, g_wo=g_wo, g_wgu=g_wgu, g_wd=g_wd)

    seg = (jnp.arange(GB_W)[:, None] // HEAD_DIM == jnp.arange(GB_W)[None, :] // HEAD_DIM).astype(BF16)

    def merge_bwd(dob_r, o1r, o2r, o3r, l1r, l2r, l3r, seg_r, d1, d2, d3, e1, e2, e3, *bufs):
        dob_v = dob_r[...]
        os_ = [_from_view(ref, bufs[n], dil[n]) for n, ref in enumerate((o1r, o2r, o3r))]
        la, lb, lc = [_from_view(ref, bufs[3 + n], dil[n]) for n, ref in enumerate((l1r, l2r, l3r))]
        mx = jnp.maximum(jnp.maximum(la, lb), lc)
        ea, eb, ec = jnp.exp(la - mx), jnp.exp(lb - mx), jnp.exp(lc - mx)
        inv = 1.0 / (ea + eb + ec)
        ws = [ea * inv, eb * inv, ec * inv]

        def headsum(v):
            hi = v.astype(BF16)
            r1_ = v - hi.astype(F32)
            mid = r1_.astype(BF16)
            lo = (r1_ - mid.astype(F32)).astype(BF16)
            sm = seg_r[...]
            return (jnp.dot(hi, sm, preferred_element_type=F32) + jnp.dot(mid, sm, preferred_element_type=F32)
                    + jnp.dot(lo, sm, preferred_element_type=F32))

        dws = [headsum(dob_v * o) for o in os_]
        mean = ws[0] * dws[0] + ws[1] * dws[1] + ws[2] * dws[2]
        for n, (w_, dw_, d_ref, e_ref) in enumerate(zip(ws, dws, (d1, d2, d3), (e1, e2, e3))):
            _to_view(w_ * dob_v, d_ref, bufs[6], dil[n])
            _to_view(w_ * (dw_ - mean), e_ref, bufs[7], dil[n])

    vshape = lambda r_, dt: sds((NB, T // r_, r_ * GB_W), dt)
    mb = _pcall(
        merge_bwd, name="merge_bwd", grid=R.grid, in_specs=[R.row(GB_W)] + views + views + [R.const((GB_W, GB_W))],
        out_specs=views + views, out_shape=[vshape(r_, BF16) for r_ in dil] + [vshape(r_, F32) for r_ in dil],
        scratch_shapes=[tokbuf] * 8, compiler_params=_params(("parallel",)))(dob, o1, o2, o3, l1, l2, l3, seg)
    do_b, dlse_b = mb[:3], mb[3:]
    hook("merge_bwd_done")

    dqkv_a, dsink = _attn_bwd(qkv[0], doa, la, None, cosf, sins, sinks, None, NB=NB, T=T, name="attn_a_bwd")
    hook("attn_a_bwd_done")
    dqkv = [dqkv_a]
    for g in range(ngrp):
        dqkv.append(_attn_bwd(qkv[1 + g], do_b[g], (l1, l2, l3)[g], dlse_b[g], cosf, sins, sinks, g, NB=NB, T=T,
                              name=f"attn_b{g}_bwd")[0])
        hook(f"attn_b{g}_bwd_done")

    g_win = [_mm(dseg, u, ta=True, out_dtype=BF16, name=f"inproj_dw{n}") for n, dseg in enumerate(dqkv + [dgates])]
    hook("win_grads", g_win=g_win)
    wvar = lambda v: (w_in, (VAR_W, D), (v, 0))
    du = _mm_multi(dqkv[:1], [wvar(0)], name="inproj_dx0")
    hook("inproj_dx0_done")
    du = _mm_multi(dqkv[1:] + [dgates], [wvar(v) for v in range(1, N_VAR)] + [(w_in, (2 * D, D), (QKV_P // (2 * D), 0))],
                   add=du, tm=256, name="inproj_dx1")
    hook("inproj_dx1_done")

    def x_bwd(dxp_r, du_r, x_r, sm_r, gx_ref, dsm_ref, dhm_ref):
        first = R.first_of_example()
        duv = du_r[...]
        gx_ref[...] = dxp_r[...] + duv * (1.0 + sm_r[0])
        _acc(dsm_ref, first, _colsum(duv * x_r[...]))
        _acc(dhm_ref, first, _colsum(duv))

    gx, dscale_m, dshift_m = _pcall(
        x_bwd, name="x_bwd", grid=R.grid, in_specs=[R.row(D)] * 3 + [R.ex(D)],
        out_specs=[R.row(D), R.ex(D), R.ex(D)], out_shape=[sds((N, D), F32), exsum(), exsum()],
        compiler_params=_params(("arbitrary",)))(dxp, du, x2, scale_m)
    hook("x_bwd_done")

    dmod =jnp.concatenate([dshift_m, dscale_m, dgate_m, dshift_f, dscale_f, dgate_f], axis=-1)[:, 0]
    ln_grads = jnp.concatenate([dg1, db1, dg2, db2], axis=1)
    return dict(loss=loss_p[:, 0, 0], grad_x=gx.reshape(NB, T, D), g_win=g_win, g_wa=g_wa, g_wb=g_wb, g_wo=g_wo,
                g_wgu=g_wgu, g_wd=g_wd, dmod=dmod, ln_grads=ln_grads, dsink=dsink[0, :A_Q_HEADS])


def _coords():
    return lax.axis_index("x"), lax.axis_index("y"), lax.axis_index("c")


def _allgather_small(blk, *, name):
    m_per, n = blk.shape

    def body(x_ref, out_ref, send_sems, recv_sems, local_sem):
        x, y, c = _coords()
        me, sibling = (x, y, c), (x, y, 1 - c)
        chips = [(1 - x, y), (x, 1 - y), (1 - x, 1 - y)]

        def rows(px, py, pc):
            return out_ref.at[pl.ds((4 * px + 2 * py + pc) * m_per, m_per), :]

        def copy(k, block, to, src=None):
            return pltpu.make_async_remote_copy(
                src_ref=rows(*block) if src is None else src, dst_ref=rows(*block),
                send_sem=send_sems.at[k], recv_sem=recv_sems.at[k], device_id=to, device_id_type=MESH)

        mine = pltpu.make_async_copy(x_ref, rows(*me), local_sem)
        mine.start()
        first = [copy(0, me, sibling, src=x_ref)]
        first += [copy(1 + j, me, (*chip, c), src=x_ref) for j, chip in enumerate(chips)]
        for cp in first:
            cp.start()
        passed = [copy(4 + j, (*chip, c), sibling) for j, chip in enumerate(chips)]
        for j, chip in enumerate(chips):
            copy(1 + j, (*chip, c), me).wait_recv()
            passed[j].start()
        copy(0, sibling, me).wait_recv()
        for j, chip in enumerate(chips):
            copy(4 + j, (*chip, 1 - c), me).wait_recv()
        for cp in first + passed:
            cp.wait_send()
        mine.wait()

    return _pcall(
        body, name=name, out_shape=jax.ShapeDtypeStruct((8 * m_per, n), blk.dtype),
        in_specs=[pl.BlockSpec(memory_space=pltpu.VMEM)], out_specs=pl.BlockSpec(memory_space=pltpu.VMEM),
        scratch_shapes=[pltpu.SemaphoreType.DMA((7,)), pltpu.SemaphoreType.DMA((7,)), pltpu.SemaphoreType.DMA],
        compiler_params=pltpu.CompilerParams(vmem_limit_bytes=VMEM_LIMIT_BYTES),
    )(blk)


def _exchange(srcs, dsts, plan, *, name, dst_inits=None):
    na = len(dsts)
    nrem = len(plan(0, 0, 0))

    def body(*refs):
        refs = list(refs)
        src_refs = [refs.pop(0) for _ in range(na)] if srcs is not None else None
        if dst_inits is not None:
            del refs[:na]
        dst_refs, (send_sems, recv_sems) = refs[:na], refs[na:]
        start, wait = _copies(dst_refs if src_refs is None else src_refs, dst_refs, send_sems, recv_sems, plan)
        start()
        wait()

    hbm = pl.BlockSpec(memory_space=pl.ANY)
    ins = (list(srcs) if srcs is not None else []) + (list(dst_inits) if dst_inits is not None else [])
    base = na if srcs is not None else 0
    aliases = {base + a: a for a in range(na)} if dst_inits is not None else {}
    return _pcall(
        body, name=name, out_shape=list(dsts), in_specs=[hbm] * len(ins), out_specs=[hbm] * na,
        input_output_aliases=aliases,
        scratch_shapes=[pltpu.SemaphoreType.DMA((na * nrem,)), pltpu.SemaphoreType.DMA((na * nrem,))],
    )(*ins)


def _other_chips(x, y):
    return [(1 - x, y), (x, 1 - y), (1 - x, 1 - y)]


def _round(ride, carrier, name):
    if carrier is not None:
        _RIDES.setdefault(carrier, []).append(ride)
        return
    srcs = ride.srcs() if callable(ride.srcs) else ride.srcs
    inits = ride.dst_inits() if callable(ride.dst_inits) else ride.dst_inits
    ride.out = list(_exchange(srcs, ride.dsts, ride.plan, name=name, dst_inits=inits))


class _Gather:
    def __init__(self, shards, chip, tag, carriers=(None, None)):
        def plan_ici(x, y, c):
            k = 2 * x + y
            return [((c,), (k, c), (2 * px + py, c), (px, py, c)) for px, py in _other_chips(x, y)]

        def plan_d2d(x, y, c):
            return [((2 * px + py, c), (2 * px + py, c), (2 * px + py, 1 - c), (x, y, 1 - c))
                    for px, py in _other_chips(x, y)]

        self.shards, self.chip = shards, chip
        dsts = [jax.ShapeDtypeStruct((4,) + s.shape, s.dtype) for s in shards]
        ici = _Ride(shards, dsts, plan_ici)
        self.d2d = _Ride(None, dsts, plan_d2d, dst_inits=lambda: ici.out)
        _round(ici, carriers[0], f"gather_{tag}_ici")
        _round(self.d2d, carriers[1], f"gather_{tag}_d2d")

    def result(self):
        full = [lax.dynamic_update_index_in_dim(f, s, self.chip, 0) for f, s in zip(self.d2d.out, self.shards)]
        return [f.reshape((4, 2 * f.shape[2], f.shape[3])) for f in full]


def _index_operand(i):
    return jnp.reshape(i, (1,)).astype(jnp.int32)


def _add_pairs(g, f, ci, *, name):
    s, _, hr, wd = g.shape
    tr = _pick(hr, 600, 16)

    def body(c_ref, a_ref, b_ref, o_ref):
        o_ref[...] = (a_ref[...].astype(F32) + b_ref[...].astype(F32)).astype(BF16)

    spec = pl.BlockSpec((1, tr, wd), lambda j, i, c: (j, i, 0))
    grid_spec = pltpu.PrefetchScalarGridSpec(
        num_scalar_prefetch=1, grid=(s, hr // tr),
        in_specs=[pl.BlockSpec((1, None, tr, wd), lambda j, i, c: (j, c[0], i, 0)), spec], out_specs=spec)
    return _pcall(body, name=name, grid_spec=grid_spec, out_shape=jax.ShapeDtypeStruct(f.shape, BF16),
                  compiler_params=_params(("parallel", "parallel")))(_index_operand(ci), g, f)


def _sum_chips(landed, pairs, chip, *, name):
    s, hr, wd = landed.shape
    tr = _pick(hr, 600, 16)

    def body(k_ref, l_ref, p_ref, o_ref):
        acc = None
        for k in range(s):
            part = jnp.where(k_ref[0] == k, p_ref[k], l_ref[k]).astype(F32)
            acc = part if acc is None else acc + part
        o_ref[...] = acc

    spec = pl.BlockSpec((s, tr, wd), lambda i, k: (0, i, 0))
    grid_spec = pltpu.PrefetchScalarGridSpec(
        num_scalar_prefetch=1, grid=(hr // tr,), in_specs=[spec, spec],
        out_specs=pl.BlockSpec((tr, wd), lambda i, k: (i, 0)))
    return _pcall(body, name=name, grid_spec=grid_spec, out_shape=jax.ShapeDtypeStruct((hr, wd), F32),
                  compiler_params=_params(("parallel",)))(_index_operand(chip), landed, pairs)


class _ReduceScatter:
    def __init__(self, gs, chip, ci, tag):
        self.gs, self.chip, self.ci, self.tag = gs, chip, ci, tag
        self.half_t = [jax.ShapeDtypeStruct((g.shape[0],) + g.shape[2:], BF16) for g in gs]

    def pair(self, carrier=None):
        plan = lambda x, y, c: [((slice(None), 1 - c), (), (), (x, y, 1 - c))]
        self.r1 = _Ride(self.gs, self.half_t, plan)
        _round(self.r1, carrier, f"reduce_{self.tag}_pair")

    def chips(self, carrier=None):
        def plan(x, y, c):
            k = 2 * x + y
            return [((2 * px + py,), (k,), (2 * px + py,), (px, py, c)) for px, py in _other_chips(x, y)]

        self.pairs = [_add_pairs(g, f, self.ci, name=f"reduce_{self.tag}_pair_add{n}")
                      for n, (g, f) in enumerate(zip(self.gs, self.r1.out))]
        self.r2 = _Ride(self.pairs, self.half_t, plan)
        _round(self.r2, carrier, f"reduce_{self.tag}_chips")

    def halves(self, carrier=None):
        plan = lambda x, y, c: [((), (c,), (1 - c,), (x, y, 1 - c))]
        self.mine = [_sum_chips(l, p, self.chip, name=f"reduce_{self.tag}_chip_sum{n}")
                     for n, (l, p) in enumerate(zip(self.r2.out, self.pairs))]
        self.r3 = _Ride(self.mine, [jax.ShapeDtypeStruct((2,) + m.shape, F32) for m in self.mine], plan)
        _round(self.r3, carrier, f"reduce_{self.tag}_halves")

    def result(self):
        return [lax.dynamic_update_index_in_dim(b, m, self.ci, 0).reshape(2 * m.shape[0], m.shape[1])
                for b, m in zip(self.r3.out, self.mine)]


def _ada_fwd(c_all, w_sh, b_sh, *, name):
    nb, d = c_all.shape
    wcols = w_sh.shape[1]
    tn = _pick(wcols, 512)

    def body(c_ref, w_ref, b_ref, o_ref, a_ref):
        cv = c_ref[...]
        act = cv * jax.nn.sigmoid(cv)
        a_ref[...] = act
        o_ref[...] = jnp.dot(act.astype(BF16), w_ref[...].astype(BF16), preferred_element_type=F32) + b_ref[...]

    return _pcall(
        body, name=name, grid=(wcols // tn,),
        in_specs=[pl.BlockSpec((nb, d), lambda j: (0, 0)), pl.BlockSpec((d, tn), lambda j: (0, j)),
                  pl.BlockSpec((1, tn), lambda j: (0, j))],
        out_specs=[pl.BlockSpec((nb, tn), lambda j: (0, j)), pl.BlockSpec((nb, d), lambda j: (0, 0))],
        out_shape=[jax.ShapeDtypeStruct((nb, wcols), F32), jax.ShapeDtypeStruct((nb, d), F32)],
        compiler_params=_params(("arbitrary",)))(c_all, w_sh, b_sh)


def _sum_devices(g, *, name):
    nd, m, w = g.shape

    def body(g_ref, o_ref):
        acc = g_ref[0]
        for k in range(1, nd):
            acc = acc + g_ref[k]
        o_ref[...] = acc

    return _pcall(body, name=name, out_shape=jax.ShapeDtypeStruct((m, w), F32),
                  compiler_params=pltpu.CompilerParams(vmem_limit_bytes=VMEM_LIMIT_BYTES))(g)


def _adamw(w, g, m, v, *, name):
    rows, cols = w.shape[-2:]
    tr = _pick(rows, max(8, (1 << 18) // cols), 8)
    c1 = 1.0 / (1.0 - ADAM_B1 ** ADAM_STEP)
    c2 = 1.0 / (1.0 - ADAM_B2 ** ADAM_STEP)

    def body(w_ref, g_ref, m_ref, v_ref, d_ref, nm_ref, nv_ref):
        gv = g_ref[...]
        nm = ADAM_B1 * m_ref[...] + (1.0 - ADAM_B1) * gv
        nv = ADAM_B2 * v_ref[...] + (1.0 - ADAM_B2) * (gv * gv)
        d_ref[...] = -ADAM_LR * ((nm * c1) / (jnp.sqrt(nv * c2) + ADAM_EPS) + ADAM_WD * w_ref[...])
        nm_ref[...] = nm
        nv_ref[...] = nv

    gspec = pl.BlockSpec((tr, cols), lambda i: (i, 0))
    spec = pl.BlockSpec((None, tr, cols), lambda i: (0, i, 0)) if w.ndim == 3 else gspec
    shp = jax.ShapeDtypeStruct(w.shape, F32)
    return _pcall(body, name=name, grid=(rows // tr,), in_specs=[spec, gspec, spec, spec], out_specs=[spec] * 3,
                  out_shape=[shp] * 3, compiler_params=_params(("parallel",)))(w, g, m, v)


def _permute_in_rows(wt):
    ngrp = len(B_PATTERNS)
    qb, kb, vb = (wt[A_W + n * QB_W:A_W + (n + 1) * QB_W] for n in range(3))
    parts = [wt[:A_W], jnp.zeros((VAR_W - A_W, wt.shape[1]), wt.dtype)]
    for g in range(ngrp):
        parts += [t[g * GB_W:(g + 1) * GB_W] for t in (qb, kb, vb)]
    return jnp.concatenate(parts + [wt[A_W + 3 * QB_W:]], axis=0)


def _unpermute_in_grads(pieces):
    ga, groups, gg = pieces[0], pieces[1:-1], pieces[-1]
    rows = [ga[:A_W]]
    for n in range(3):
        rows += [gp[n * GB_W:(n + 1) * GB_W] for gp in groups]
    return jnp.concatenate(rows + [gg], axis=0)


def kernel(x, c, positions, w_ada, b_ada, w_in, sinks, w_branch_a, w_branch_b, w_o, ln1_g, ln1_b, w_gate_up, w_down, ln2_g, ln2_b, loss_target, m_w_ada, m_b_ada, m_w_in, m_sinks, m_w_branch_a, m_w_branch_b, m_w_o, m_ln1_g, m_ln1_b, m_w_gate_up, m_w_down, m_ln2_g, m_ln2_b, v_w_ada, v_b_ada, v_w_in, v_sinks, v_w_branch_a, v_w_branch_b, v_w_o, v_ln1_g, v_ln1_b, v_w_gate_up, v_w_down, v_ln2_g, v_ln2_b):
    xi, yi, ci = _coords()
    chip = 2 * xi + yi
    dev = 4 * xi + 2 * yi + ci
    NB, T, D = x.shape
    nchip, ndev = 4, 8
    ada_cols = w_ada.shape[2]

    c_blk = jnp.zeros((8, D), F32).at[:NB].set(c)
    c_all = _allgather_small(c_blk, name="gather_c").reshape(ndev, 8, D)[:, :NB].reshape(ndev * NB, D)
    b_sh = lax.dynamic_slice(b_ada, (0, chip * ada_cols), (1, ada_cols))
    mod_part, c_act = _ada_fwd(c_all, w_ada[0], b_sh, name="ada_fwd")
    mod_g = _allgather_small(mod_part, name="gather_mod").reshape(nchip, 2, ndev * NB, ada_cols)[:, 0]
    mod_all = jnp.transpose(mod_g, (1, 0, 2)).reshape(ndev * NB, nchip * ada_cols)
    mod = lax.dynamic_slice(mod_all, (NB * dev, 0), (NB, nchip * ada_cols))

    ra, ro, rd = w_branch_a.shape[1], w_o.shape[1], w_down.shape[1]
    rowsh = jnp.concatenate([w_branch_a[0], w_o[0], w_down[0]], axis=0)
    halves = lambda a: a.reshape(a.shape[:-2] + (2, a.shape[-2] // 2, a.shape[-1]))
    whole = lambda a: a.reshape(a.shape[:-3] + (2 * a.shape[-2], a.shape[-1]))
    tr = lambda a: jnp.swapaxes(a, -1, -2)
    shards = [halves(w.astype(BF16)) for w in (tr(w_in[0]), rowsh, w_branch_b[0], w_gate_up[0])]
    (g_in,) = _Gather(shards[:1], chip, "w_in").result()
    w_in_f = _permute_in_rows(g_in.reshape(nchip * g_in.shape[1], D))
    mix = _Gather(shards[1:3], chip, "w_mix", carriers=("inproj_qkv", "attn_a_fwd"))
    ffn = _Gather(shards[3:], chip, "w_ffn", carriers=("attn_a_fwd", "attn_b0_fwd"))

    def rest_weights():
        (g_rows, w_b_f), (w_gu_f,) = mix.result(), ffn.result()
        return (g_rows[:, :ra].reshape(nchip * ra, D), w_b_f, g_rows[:, ra:ra + ro].reshape(nchip * ro, D), w_gu_f,
                g_rows[:, ra + ro:].reshape(nchip * rd, D))

    red = {}

    def hook(event, **g):
        if event == "rest_grads":
            gr_rows = jnp.concatenate([g["g_wa"].reshape(nchip, ra, D), g["g_wo"].reshape(nchip, ro, D),
                                       g["g_wd"].reshape(nchip, rd, D)], axis=1)
            red["rest"] = _ReduceScatter([halves(a) for a in (gr_rows, g["g_wb"], g["g_wgu"])], chip, ci, "rest")
            red["rest"].pair(carrier="merge_bwd")
        elif event == "merge_bwd_done":
            red["rest"].chips(carrier="attn_a_bwd")
        elif event == "attn_a_bwd_done":
            red["rest"].halves(carrier="attn_b0_bwd")
        elif event == "win_grads":
            gr_in = _unpermute_in_grads(g["g_win"])
            red["w_in"] = _ReduceScatter([halves(gr_in.reshape(nchip, gr_in.shape[0] // nchip, D))], chip, ci, "w_in")
            red["w_in"].pair(carrier="inproj_dx0")
        elif event == "inproj_dx0_done":
            red["w_in"].chips(carrier="inproj_dx1")
        elif event == "inproj_dx1_done":
            red["w_in"].halves(carrier="x_bwd")

    res = _local_step(x, mod, positions, w_in_f, rest_weights, sinks[0], ln1_g, ln1_b, ln2_g, ln2_b, loss_target, hook)
    (g_w_in,) = red["w_in"].result()
    g_rows_red, g_w_b, g_w_gu = red["rest"].result()
    g_w_a, g_w_o, g_w_d = g_rows_red[:ra], g_rows_red[ra:ra + ro], g_rows_red[ra + ro:]

    small_rows = 24
    misc = jnp.zeros((1, D), F32).at[0, :A_Q_HEADS].set(res["dsink"]).at[0, A_Q_HEADS].set(jnp.sum(res["loss"]))
    small = jnp.concatenate([res["dmod"].reshape(NB * 6, D), jnp.sum(res["ln_grads"], axis=0), misc,
                             jnp.zeros((small_rows - NB * 6 - 5, D), F32)], axis=0)
    small_all = _allgather_small(small, name="gather_small").reshape(ndev, small_rows, D)
    dmod_all = small_all[:, :NB * 6].reshape(ndev * NB, 6 * D)
    sums = _sum_devices(small_all, name="sum_small")
    g_b_ada = (sums[0:6] + sums[6:12]).reshape(1, 6 * D)
    g_ln1_g, g_ln1_b, g_ln2_g, g_ln2_b = (sums[12 + n][None] for n in range(4))
    g_sinks = sums[16, :A_Q_HEADS][None]
    loss = sums[16, A_Q_HEADS]
    dmod_sh = lax.dynamic_slice(dmod_all, (0, chip * ada_cols), (ndev * NB, ada_cols))
    g_w_ada = _mm(c_act, dmod_sh, ta=True, name="ada_dw")

    names = ["w_ada", "b_ada", "w_in", "sinks", "w_branch_a", "w_branch_b", "w_o", "ln1_g", "ln1_b",
             "w_gate_up", "w_down", "ln2_g", "ln2_b"]
    ws = [w_ada, b_ada, w_in, sinks, w_branch_a, w_branch_b, w_o, ln1_g, ln1_b, w_gate_up, w_down, ln2_g, ln2_b]
    ms = [m_w_ada, m_b_ada, m_w_in, m_sinks, m_w_branch_a, m_w_branch_b, m_w_o, m_ln1_g, m_ln1_b, m_w_gate_up,
          m_w_down, m_ln2_g, m_ln2_b]
    vs = [v_w_ada, v_b_ada, v_w_in, v_sinks, v_w_branch_a, v_w_branch_b, v_w_o, v_ln1_g, v_ln1_b, v_w_gate_up,
          v_w_down, v_ln2_g, v_ln2_b]
    gs = [g_w_ada, g_b_ada, g_w_in, g_sinks, g_w_a, g_w_b, g_w_o, g_ln1_g, g_ln1_b, g_w_gu, g_w_d, g_ln2_g, g_ln2_b]
    grads, deltas, new_ms, new_vs = [], [], [], []
    for name, w, g, m, v in zip(names, ws, gs, ms, vs):
        flip = tr if name == "w_in" else (lambda a: a)
        w, m, v = flip(w), flip(m), flip(v)
        g2 = g.reshape(w.shape[-2:])
        d, nm, nv = _adamw(w, g2, m, v, name="adamw_" + name)
        grads.append(flip(g2.reshape(w.shape)))
        deltas.append(flip(d))
        new_ms.append(flip(nm))
        new_vs.append(flip(nv))
    return (loss, res["grad_x"], *grads, *deltas, *new_ms, *new_vs)
```

```python
import functools

import jax
import jax.numpy as jnp
from jax import lax
from jax.experimental import pallas as pl
from jax.experimental.pallas import tpu as pltpu

F32 = jnp.float32
BF16 = jnp.bfloat16
MESH = pl.DeviceIdType.MESH

HEAD_DIM = 64
LANES = 128
PAIR_W = 2 * HEAD_DIM
BLOCK = 128
A_Q_HEADS = 16
A_KV_HEADS = 2
A_WINDOW = 128
B_PATTERNS = ((128, 1), (512, 4), (2048, 16))
B_GROUP_HEADS = 8
QA_W = A_Q_HEADS * HEAD_DIM
KA_W = A_KV_HEADS * HEAD_DIM
GB_W = B_GROUP_HEADS * HEAD_DIM
QB_W = GB_W * len(B_PATTERNS)
A_W = QA_W + 2 * KA_W
VAR_W = 3 * GB_W
N_VAR = 1 + len(B_PATTERNS)
VAR_DIL = (1,) + tuple(r for _, r in B_PATTERNS)
QKV_P = N_VAR * VAR_W
ROPE_THETA = 10000.0
LN_EPS = 1e-5
NEG_INF = -1e30
DEPTH = 1
ALPHA = (2 * DEPTH) ** 0.25
SCALE = HEAD_DIM ** -0.5

ADAM_LR, ADAM_B1, ADAM_B2, ADAM_EPS, ADAM_WD, ADAM_STEP = 0.001, 0.9, 0.999, 1e-08, 0.01, 10

VMEM_LIMIT_BYTES = 56 * 1024 * 1024
MM_TILE_BYTES = 36 * 1024 * 1024
MM_WHOLE_K = 4096


def _params(sem=None):
    return pltpu.CompilerParams(dimension_semantics=sem, vmem_limit_bytes=VMEM_LIMIT_BYTES)


_RIDES = {}


def _pcall(body, *, name, **kw):
    rides = _RIDES.pop(name, None)
    if rides is None:
        return pl.pallas_call(body, name=name, **kw)
    return _riding_call(body, rides, name=name, **kw)


def _copies(src_refs, dst_refs, send_sems, recv_sems, plan):
    x, y, c = lax.axis_index("x"), lax.axis_index("y"), lax.axis_index("c")
    remote = plan(x, y, c)
    nrem = len(remote)
    at = lambda ref, idx: ref.at[idx] if idx else ref

    def copy(a, n, landing):
        si, di, ri, peer = remote[n]
        return pltpu.make_async_remote_copy(
            src_ref=at(src_refs[a], si), dst_ref=at(dst_refs[a], ri if landing else di),
            send_sem=send_sems.at[a * nrem + n], recv_sem=recv_sems.at[a * nrem + n],
            device_id=peer, device_id_type=MESH)

    order = [(a, n) for a in range(len(dst_refs)) for n in range(nrem)]

    def start():
        for a, n in order:
            copy(a, n, False).start()

    def wait():
        for a, n in order:
            copy(a, n, True).wait_recv()
        for a, n in order:
            copy(a, n, False).wait_send()

    return start, wait


class _Ride:
    def __init__(self, srcs, dsts, plan, dst_inits=None):
        self.srcs, self.dsts, self.plan, self.dst_inits, self.out = srcs, dsts, plan, dst_inits, None


def _riding_call(body, rides, *, name, grid, in_specs, out_specs, out_shape, scratch_shapes=(), **kw):
    single = not isinstance(out_specs, (list, tuple))
    out_specs = [out_specs] if single else list(out_specs)
    out_shape = [out_shape] if single else list(out_shape)
    n_in, n_out, n_scr = len(in_specs), len(out_specs), len(scratch_shapes)
    xin, xdsts, sems, aliases, layout = [], [], [], {}, []
    for ride in rides:
        srcs = ride.srcs() if callable(ride.srcs) else ride.srcs
        inits = ride.dst_inits() if callable(ride.dst_inits) else ride.dst_inits
        na, nrem = len(ride.dsts), len(ride.plan(0, 0, 0))
        src_at = len(xin) if srcs is not None else None
        xin += list(srcs) if srcs is not None else []
        if inits is not None:
            aliases.update({n_in + len(xin) + a: n_out + len(xdsts) + a for a in range(na)})
            xin += list(inits)
        layout.append((src_at, len(xdsts), na))
        xdsts += list(ride.dsts)
        sems += [pltpu.SemaphoreType.DMA((na * nrem,)), pltpu.SemaphoreType.DMA((na * nrem,))]

    def wrapped(*refs):
        ins, xins = refs[:n_in], refs[n_in:n_in + len(xin)]
        outs = refs[n_in + len(xin):n_in + len(xin) + n_out]
        xouts = refs[n_in + len(xin) + n_out:n_in + len(xin) + n_out + len(xdsts)]
        scr = refs[n_in + len(xin) + n_out + len(xdsts):]
        rounds = []
        for k, (ride, (src_at, dst_at, na)) in enumerate(zip(rides, layout)):
            dsts = xouts[dst_at:dst_at + na]
            srcs = dsts if src_at is None else xins[src_at:src_at + na]
            rounds.append(_copies(srcs, dsts, scr[n_scr + 2 * k], scr[n_scr + 2 * k + 1], ride.plan))
        ids = [pl.program_id(a) for a in range(len(grid))]
        first = functools.reduce(jnp.logical_and, [i == 0 for i in ids])
        last = functools.reduce(jnp.logical_and, [i == g - 1 for i, g in zip(ids, grid)])

        @pl.when(first)
        def _():
            for start, _ in rounds:
                start()

        body(*ins, *outs, *scr[:n_scr])

        @pl.when(last)
        def _():
            for _, wait in rounds:
                wait()

    hbm = pl.BlockSpec(memory_space=pl.ANY)

    def run(*args):
        res = pl.pallas_call(
            wrapped, name=name, grid=grid, in_specs=list(in_specs) + [hbm] * len(xin),
            out_specs=out_specs + [hbm] * len(xdsts), out_shape=out_shape + xdsts,
            scratch_shapes=list(scratch_shapes) + sems, input_output_aliases=aliases,
            compiler_params=_params(("arbitrary",) * len(grid)),
        )(*args, *xin)
        for ride, (_, dst_at, na) in zip(rides, layout):
            ride.out = list(res[n_out + dst_at:n_out + dst_at + na])
        return res[0] if single else list(res[:n_out])

    return run


def _pick(n, target, quantum=128):
    t = (min(target, n) // quantum) * quantum
    while t >= quantum:
        if n % t == 0:
            return t
        t -= quantum
    return n


def _mm(a, b, *, name, ta=False, tb=False, b3=False, out3=0, out_dtype=F32, add=None, tm=1024, tn=1536, tk=1536):
    if ta:
        K, M = a.shape
    else:
        M, K = a.shape
    if b3 and tb:
        Nn, K2, tk = b.shape[1], b.shape[0] * b.shape[2], b.shape[2]
    elif b3:
        K2, Nn, tn = b.shape[1], b.shape[0] * b.shape[2], b.shape[2]
    elif tb:
        Nn, K2 = b.shape
    else:
        K2, Nn = b.shape
    assert K == K2, (a.shape, b.shape)
    if out3:
        tn = Nn // out3
    tm, tn, tk = _pick(M, tm), _pick(Nn, tn), _pick(K, tk)
    if not (b3 and tb) and K <= MM_WHOLE_K:
        tk = K
        fits = lambda: 4 * tk * (tm + tn) + 8 * tm * tn * (2 if add is not None else 1) <= MM_TILE_BYTES
        while not fits():
            if (tm >= tn or b3 or out3) and tm > 256:
                tm = _pick(M, tm - 128)
            elif not (b3 or out3) and tn > 256:
                tn = _pick(Nn, tn - 128)
            else:
                break
    nk = K // tk
    j_outer = K * Nn + (Nn // tn) * M * K < M * K + (M // tm) * K * Nn
    dn = (((0 if ta else 1,), (1 if tb else 0,)), ((), ()))

    def body(*refs):
        refs = list(refs)
        a_ref, b_ref = refs[:2]
        add_ref = refs[2] if add is not None else None
        o_ref = refs[3] if add is not None else refs[2]
        part = lax.dot_general(a_ref[...].astype(BF16), b_ref[...].astype(BF16), dn, preferred_element_type=F32)

        def finish(r):
            if add is not None:
                r = r + add_ref[...]
            o_ref[...] = r.astype(out_dtype)

        if nk == 1:
            finish(part)
            return
        acc = refs[-1]
        k = pl.program_id(2)

        @pl.when(k == 0)
        def _():
            acc[...] = part

        @pl.when(k > 0)
        def _():
            acc[...] += part

        @pl.when(k == nk - 1)
        def _():
            finish(acc[...])

    def spec(shape, index):
        return pl.BlockSpec(shape, (lambda j, i, k: index(i, j, k)) if j_outer else index)

    a_spec = spec((tk, tm), lambda i, j, k: (k, i)) if ta else spec((tm, tk), lambda i, j, k: (i, k))
    if b3 and tb:
        b_spec = spec((None, tn, tk), lambda i, j, k: (k, j, 0))
    elif b3:
        b_spec = spec((None, tk, tn), lambda i, j, k: (j, k, 0))
    elif tb:
        b_spec = spec((tn, tk), lambda i, j, k: (j, k))
    else:
        b_spec = spec((tk, tn), lambda i, j, k: (k, j))
    if out3:
        o_spec = spec((None, tm, tn), lambda i, j, k: (j, i, 0))
    else:
        o_spec = spec((tm, tn), lambda i, j, k: (i, j))
    ins, specs = [a, b], [a_spec, b_spec]
    if add is not None:
        ins.append(add)
        specs.append(o_spec)
    grid = (Nn // tn, M // tm, nk) if j_outer else (M // tm, Nn // tn, nk)
    return _pcall(
        body, name=name, grid=grid, in_specs=specs, out_specs=o_spec,
        out_shape=jax.ShapeDtypeStruct((out3, M, tn) if out3 else (M, Nn), out_dtype),
        scratch_shapes=[pltpu.VMEM((tm, tn), F32)] if nk > 1 else [],
        compiler_params=_params(("parallel", "parallel", "arbitrary")),
    )(*ins)


def _mm_multi(a_list, b_list, *, name, add=None, out_dtype=F32, tm=512):
    M = a_list[0].shape[0]
    tm = _pick(M, tm)
    ns = len(a_list)
    b_arrs, b_specs = [], []
    for b in b_list:
        arr, shp, idx = b if isinstance(b, tuple) else (b, b.shape, (0, 0))
        b_arrs.append(arr)
        b_specs.append(pl.BlockSpec(shp, lambda i, idx=idx: idx))
    Nn = b_specs[0].block_shape[1]
    dn = (((1,), (0,)), ((), ()))

    def body(*refs):
        a_refs, b_refs = refs[:ns], refs[ns:2 * ns]
        acc = None
        for a_ref, b_ref in zip(a_refs, b_refs):
            part = lax.dot_general(a_ref[...].astype(BF16), b_ref[...], dn, preferred_element_type=F32)
            acc = part if acc is None else acc + part
        if add is not None:
            acc = acc + refs[2 * ns][...]
        refs[-1][...] = acc.astype(out_dtype)

    o_spec = pl.BlockSpec((tm, Nn), lambda i: (i, 0))
    specs = [pl.BlockSpec((tm, a.shape[1]), lambda i: (i, 0)) for a in a_list] + b_specs
    ins = list(a_list) + b_arrs
    if add is not None:
        specs.append(o_spec)
        ins.append(add)
    return _pcall(body, name=name, grid=(M // tm,), in_specs=specs, out_specs=o_spec,
                  out_shape=jax.ShapeDtypeStruct((M, Nn), out_dtype), compiler_params=_params(("parallel",)))(*ins)


def _lane(shape):
    return lax.broadcasted_iota(jnp.int32, shape, len(shape) - 1)


def _rot_half(v):
    w = v.shape[-1]
    first = (_lane(v.shape) % HEAD_DIM) < (HEAD_DIM // 2)
    return jnp.where(first, pltpu.roll(v, w - HEAD_DIM // 2, v.ndim - 1), pltpu.roll(v, HEAD_DIM // 2, v.ndim - 1))


def _widen(t, w):
    return t if w == t.shape[-1] else jnp.concatenate([t] * (w // t.shape[-1]), axis=-1)


def _unrope(v, cos, sins):
    w = v.shape[-1]
    return v * _widen(cos, w) - _rot_half(v) * _widen(sins, w)


def _rope_tables(positions):
    half = HEAD_DIM // 2
    inv = ROPE_THETA ** (-jnp.arange(half, dtype=F32) / half)
    ang = positions.astype(F32)[..., None] * inv
    cos, sin = jnp.cos(ang), jnp.sin(ang)
    cosf = jnp.concatenate([cos, cos, cos, cos], axis=-1)
    sins = jnp.concatenate([-sin, sin, -sin, sin], axis=-1)
    n = positions.shape[0] * positions.shape[1]
    return cosf.reshape(n, PAIR_W), sins.reshape(n, PAIR_W)


def _inproj(x2, scale, shift, w, cosf, sins, flags, *, T, name):
    N, D = x2.shape
    tm, tn = _pick(T, 512), VAR_W
    tpe = T // tm

    def body(x_ref, sc_ref, sh_ref, w_ref, c_ref, s_ref, f_ref, *outs):
        o_refs, u_ref = outs[:N_VAR], outs[N_VAR]
        j = pl.program_id(1)

        @pl.when(j == 0)
        def _():
            u_ref[...] = (x_ref[...] * (1.0 + sc_ref[0]) + sh_ref[0]).astype(BF16)

        acc = lax.dot_general(u_ref[...], w_ref[...], (((1,), (1,)), ((), ())), preferred_element_type=F32)
        fl = f_ref[...]
        ce = 1.0 + (_widen(c_ref[...], tn) - 1.0) * fl
        se = _widen(s_ref[...], tn) * fl
        res = acc * ce + _rot_half(acc) * se
        for v in range(N_VAR):
            @pl.when(j == v)
            def _(v=v):
                _to_view(res, o_refs[v], outs[N_VAR + 1], VAR_DIL[v])

    ex = pl.BlockSpec((1, 1, D), lambda i, j: (i // tpe, 0, 0))
    tab = pl.BlockSpec((tm, PAIR_W), lambda i, j: (i, 0))
    keep = lambda w_: pl.BlockSpec((tm, w_), lambda i, j: (i, 0))
    vspec = lambda r: pl.BlockSpec((None, tm // r, r * tn), lambda i, j: (i // tpe, i % tpe, 0))
    vshape = lambda r: jax.ShapeDtypeStruct((N // T, T // r, r * tn), BF16)
    return _pcall(
        body, name=name, grid=(N // tm, N_VAR),
        in_specs=[keep(D), ex, ex, pl.BlockSpec((tn, D), lambda i, j: (j, 0)), tab, tab,
                  pl.BlockSpec((1, tn), lambda i, j: (0, j))],
        out_specs=[vspec(r) for r in VAR_DIL] + [keep(D)],
        out_shape=[vshape(r) for r in VAR_DIL] + [jax.ShapeDtypeStruct((N, D), BF16)],
        scratch_shapes=[pltpu.VMEM((tn // LANES, tm, LANES), F32)],
        compiler_params=_params(("parallel", "arbitrary")),
    )(x2, scale, shift, w, cosf, sins, flags)


class _Geom:
    def __init__(self, g):
        if g is None:
            self.r, self.nq, self.n_back, self.sink = 1, A_Q_HEADS, A_WINDOW - 1, True
            self.qw, self.kw = QA_W, KA_W
            self.qidx = lambda j: 0
            self.kidx = lambda j: QA_W // KA_W
            self.vidx = lambda j: QA_W // KA_W + 1
        else:
            window, r = B_PATTERNS[g]
            self.r, self.nq, self.n_back, self.sink = r, B_GROUP_HEADS, window // r, False
            self.qw, self.kw = GB_W, GB_W
            self.qidx = lambda j: 3 * j
            self.kidx = lambda j: 3 * j + 1
            self.vidx = lambda j: 3 * j + 2
        self.ntile = self.qw // PAIR_W


def _stack_heads(t, scale=None):
    first = _lane(t.shape) < HEAD_DIM
    z = jnp.zeros_like(t)
    if scale is not None:
        t = t * jnp.asarray(scale, t.dtype)
    return jnp.concatenate([jnp.where(first, t, z), jnp.where(first, z, t)], axis=0)


def _lse_col(t):
    return jnp.concatenate([t[:, 0:1], t[:, HEAD_DIM:HEAD_DIM + 1]], axis=0)


def _lse_rows(t, width):
    first = _lane(t.shape) < HEAD_DIM
    other = pltpu.roll(t, HEAD_DIM, 1)
    full = jnp.concatenate([jnp.where(first, t, other), jnp.where(first, other, t)], axis=0)
    return _widen(full, width)


def _unstack_heads(v2):
    return jnp.where(_lane((BLOCK, PAIR_W)) < HEAD_DIM, v2[:BLOCK], v2[BLOCK:])


def _dup_head(t, kh):
    tf = t.astype(F32)
    keep = (_lane(t.shape) < HEAD_DIM) if kh == 0 else (_lane(t.shape) >= HEAD_DIM)
    return jnp.where(keep, tf, pltpu.roll(tf, HEAD_DIM, 1)).astype(t.dtype)


def _fold_heads(t):
    return t + pltpu.roll(t, HEAD_DIM, 1)


def _band_mask(rows, i, n_back, single):
    nkeys = BLOCK if single else 2 * BLOCK
    qi = jnp.bitwise_and(lax.broadcasted_iota(jnp.int32, (rows, nkeys), 0), BLOCK - 1)
    ki = lax.broadcasted_iota(jnp.int32, (rows, nkeys), 1)
    if single:
        return qi >= ki
    dist = qi + BLOCK - ki
    return jnp.logical_and(jnp.logical_and(dist >= 0, dist <= n_back), jnp.logical_or(ki >= BLOCK, i > 0))


def _per_block(col, scalars, fn):
    return jnp.concatenate([fn(col[b * BLOCK:(b + 1) * BLOCK], sc) for b, sc in enumerate(scalars)], axis=0)


def _sink_slot(rows):
    qi = jnp.bitwise_and(lax.broadcasted_iota(jnp.int32, (rows, 2 * BLOCK), 0), BLOCK - 1)
    return qi == lax.broadcasted_iota(jnp.int32, (rows, 2 * BLOCK), 1)


def _sink_scores(rows, sinks):
    blk = lax.broadcasted_iota(jnp.int32, (rows, 2 * BLOCK), 0) // BLOCK
    out = jnp.full((rows, 2 * BLOCK), sinks[-1], F32)
    for b in range(len(sinks) - 2, -1, -1):
        out = jnp.where(blk == b, sinks[b], out)
    return out


def _softmax_parts(s, valid, sinks):
    s = jnp.where(valid, s, NEG_INF)
    if sinks is not None:
        slot = _sink_slot(s.shape[0])
        s = jnp.where(slot, _sink_scores(s.shape[0], sinks), s)
    m = jnp.max(s, axis=1, keepdims=True)
    p = jnp.exp(s - m)
    den = jnp.sum(p, axis=1, keepdims=True)
    if sinks is not None:
        p = jnp.where(slot, 0.0, p)
    return p, m, den


_NT = (((1,), (1,)), ((), ()))
_TN = (((0,), (0,)), ((), ()))


def _rows2(prev_ref, cur_ref, cs, single=False):
    if single:
        return cur_ref[0, :, cs]
    return jnp.concatenate([prev_ref[0, :, cs], cur_ref[0, :, cs]], axis=0)


def _sink_scalars(sink_ref, first, nblocks):
    return [sink_ref[first + b] for b in range(nblocks)]


def _tile(t):
    return slice(t * PAIR_W, (t + 1) * PAIR_W)


def _attn_fwd(qkv, sinks, g, *, NB, T, name):
    geo = _Geom(g)
    r, qw, kw, ntile = geo.r, geo.qw, geo.kw, geo.ntile
    tsub = T // r
    nblk = tsub // BLOCK
    qkv3 = qkv.reshape(NB, tsub, r * VAR_W)
    out_dtype = BF16 if g is None else F32
    tiles_per_kv = ntile // A_KV_HEADS

    single = nblk == 1

    def body(q_ref, kp_ref, kc_ref, vp_ref, vc_ref, sink_ref, o_ref, l_ref):
        i = pl.program_id(2)
        if geo.sink:
            kall, vall = _rows2(kp_ref, kc_ref, _tile(0)), _rows2(vp_ref, vc_ref, _tile(0))
            kdup = [_dup_head(kall, kh) for kh in range(A_KV_HEADS)]
            vdup = [_dup_head(vall, kh) for kh in range(A_KV_HEADS)]
            tiles = [[t] for t in range(ntile)]
            q2s = [_stack_heads(q_ref[0, :, _tile(t)], SCALE) for t in range(ntile)]
            kks = [kdup[t // tiles_per_kv] for t in range(ntile)]
            vvs = [vdup[t // tiles_per_kv] for t in range(ntile)]
            sinkcols = [_sink_scalars(sink_ref, 2 * t, 2) for t in range(ntile)]
        else:
            tiles = [[t] for t in range(ntile)]
            q2s = [_stack_heads(q_ref[0, :, _tile(t)], SCALE) for t in range(ntile)]
            kks = [_rows2(kp_ref, kc_ref, _tile(t), single) for t in range(ntile)]
            vvs = [_rows2(vp_ref, vc_ref, _tile(t), single) for t in range(ntile)]
            sinkcols = [None] * ntile
        valid = _band_mask(q2s[0].shape[0], i, geo.n_back, single)
        ss = [lax.dot_general(q2, kk, _NT, preferred_element_type=F32) for q2, kk in zip(q2s, kks)]
        parts = [_softmax_parts(s, valid, sc) for s, sc in zip(ss, sinkcols)]
        o2s = [jnp.dot(p.astype(BF16), vv, preferred_element_type=F32) / den for (p, m, den), vv in zip(parts, vvs)]
        for ts, o2, (p, m, den) in zip(tiles, o2s, parts):
            lse2 = jnp.broadcast_to(m + jnp.log(den), (o2.shape[0], PAIR_W))
            for n, t in enumerate(ts):
                rows = slice(2 * BLOCK * n, 2 * BLOCK * (n + 1))
                o_ref[0, :, _tile(t)] = _unstack_heads(o2[rows]).astype(out_dtype)
                l_ref[0, :, _tile(t)] = _unstack_heads(lse2[rows])

    prev = lambda i: jnp.maximum(i - 1, 0)
    in_specs = [
        pl.BlockSpec((1, BLOCK, qw), lambda b, j, i: (b, i, geo.qidx(j))),
        pl.BlockSpec((1, BLOCK, kw), lambda b, j, i: (b, prev(i), geo.kidx(j))),
        pl.BlockSpec((1, BLOCK, kw), lambda b, j, i: (b, i, geo.kidx(j))),
        pl.BlockSpec((1, BLOCK, kw), lambda b, j, i: (b, prev(i), geo.vidx(j))),
        pl.BlockSpec((1, BLOCK, kw), lambda b, j, i: (b, i, geo.vidx(j))),
        pl.BlockSpec(memory_space=pltpu.SMEM),
    ]
    o_spec = pl.BlockSpec((1, BLOCK, qw), lambda b, j, i: (b, i, j))
    shape = (NB, tsub, r * qw)
    o, lse = _pcall(
        body, name=name, grid=(NB, r, nblk), in_specs=in_specs, out_specs=[o_spec, o_spec],
        out_shape=[jax.ShapeDtypeStruct(shape, out_dtype), jax.ShapeDtypeStruct(shape, F32)],
        compiler_params=_params(("parallel", "parallel", "arbitrary")),
    )(qkv3, qkv3, qkv3, qkv3, qkv3, sinks)
    return o, lse


def _attn_bwd(qkv, do, lse, dlse, cosf, sins, sinks, g, *, NB, T, name):
    geo = _Geom(g)
    r, qw, kw, ntile = geo.r, geo.qw, geo.kw, geo.ntile
    tsub = T // r
    nblk = tsub // BLOCK
    view = lambda a, w: a.reshape(NB, tsub, r * w)
    has_dlse = dlse is not None
    tiles_per_kv = ntile // A_KV_HEADS

    single = nblk == 1
    krows = BLOCK if single else 2 * BLOCK
    nsteps = 1 if single else nblk + 1

    def grads(q2s, kks, vvs, do2s, i, lsecols, sinkcols, dlcols):
        valid = _band_mask(q2s[0].shape[0], i, geo.n_back, single)
        ss = [lax.dot_general(q2, kk, _NT, preferred_element_type=F32) for q2, kk in zip(q2s, kks)]
        dps = [lax.dot_general(do2, vv, _NT, preferred_element_type=F32) for do2, vv in zip(do2s, vvs)]
        ps, dss, sks = [], [], []
        for s, dp, ls, sc, dl in zip(ss, dps, lsecols, sinkcols, dlcols):
            sv = jnp.where(valid, s, NEG_INF)
            if sc is not None:
                slot = _sink_slot(s.shape[0])
                sv = jnp.where(slot, _sink_scores(s.shape[0], sc), sv)
                dp = jnp.where(slot, 0.0, dp)
            p = jnp.exp(sv - ls)
            delta = jnp.sum(p * dp, axis=1, keepdims=True)
            if dl is not None:
                delta = delta - dl
            ds = p * (dp - delta)
            if sc is not None:
                blk = lambda a, b: a[b * BLOCK:(b + 1) * BLOCK]
                sks.append([jnp.sum(jnp.where(blk(slot, b), blk(ds, b), 0.0)) for b in range(len(sc))])
                ds, p = jnp.where(slot, 0.0, ds), jnp.where(slot, 0.0, p)
            else:
                sks.append(None)
            ps.append(p.astype(BF16))
            dss.append(ds.astype(BF16))
        dq2s = [jnp.dot(ds, kk, preferred_element_type=F32) * SCALE for ds, kk in zip(dss, kks)]
        dkks = [lax.dot_general(ds, q2, _TN, preferred_element_type=F32) for ds, q2 in zip(dss, q2s)]
        dvvs = [lax.dot_general(p, do2, _TN, preferred_element_type=F32) for p, do2 in zip(ps, do2s)]
        return dq2s, dkks, dvvs, sks

    def body(*refs):
        it = iter(refs)
        q_ref, kp_ref, kc_ref, vp_ref, vc_ref, do_ref, l_ref = (next(it) for _ in range(7))
        dl_ref = next(it) if has_dlse else None
        c_ref, s_ref, sink_ref, o_ref, ds_ref, dq_s, dk_s, dv_s, car_q, car_k, car_v = (next(it) for _ in range(11))
        b, j, i = pl.program_id(0), pl.program_id(1), pl.program_id(2)

        @pl.when(jnp.logical_and(b == 0, jnp.logical_and(j == 0, i == 0)))
        def _():
            ds_ref[...] = jnp.zeros_like(ds_ref)

        def compute():
            if geo.sink:
                kall, vall = _rows2(kp_ref, kc_ref, _tile(0)), _rows2(vp_ref, vc_ref, _tile(0))
                nb = 2 * tiles_per_kv
                tiles = [[kh * tiles_per_kv + t for t in range(tiles_per_kv)] for kh in range(A_KV_HEADS)]
                cat = lambda f, ts: jnp.concatenate([f(t) for t in ts], axis=0)
                dq2s, dkks, dvvs, sks = grads(
                    [cat(lambda t: _stack_heads(q_ref[0, :, _tile(t)], SCALE), ts) for ts in tiles],
                    [_dup_head(kall, kh) for kh in range(A_KV_HEADS)],
                    [_dup_head(vall, kh) for kh in range(A_KV_HEADS)],
                    [cat(lambda t: _stack_heads(do_ref[0, :, _tile(t)]), ts) for ts in tiles], i,
                    [cat(lambda t: _lse_rows(l_ref[0, :, _tile(t)], krows), ts) for ts in tiles],
                    [_sink_scalars(sink_ref, kh * nb, nb) for kh in range(A_KV_HEADS)], [None] * A_KV_HEADS)
                lane1 = _lane((1, PAIR_W))
                dsink = jnp.zeros((1, PAIR_W), F32)
                for kh, (ts, dq2, sk) in enumerate(zip(tiles, dq2s, sks)):
                    for n, t in enumerate(ts):
                        dq_s[:, _tile(t)] = _unstack_heads(dq2[2 * BLOCK * n:2 * BLOCK * (n + 1)])
                    for bb in range(nb):
                        dsink = dsink + jnp.where(lane1 == kh * nb + bb, sk[bb], 0.0)
                second = _lane((krows, PAIR_W)) >= HEAD_DIM
                dk_s[...] = jnp.where(second, _fold_heads(dkks[1]), _fold_heads(dkks[0]))
                dv_s[...] = jnp.where(second, _fold_heads(dvvs[1]), _fold_heads(dvvs[0]))
                ds_ref[0:1, :] += dsink
            else:
                dq2s, dkks, dvvs, _ = grads(
                    [_stack_heads(q_ref[0, :, _tile(t)], SCALE) for t in range(ntile)],
                    [_rows2(kp_ref, kc_ref, _tile(t), single) for t in range(ntile)],
                    [_rows2(vp_ref, vc_ref, _tile(t), single) for t in range(ntile)],
                    [_stack_heads(do_ref[0, :, _tile(t)]) for t in range(ntile)], i,
                    [_lse_rows(l_ref[0, :, _tile(t)], krows) for t in range(ntile)], [None] * ntile,
                    [_lse_col(dl_ref[0, :, _tile(t)]) for t in range(ntile)])
                for t in range(ntile):
                    dq_s[:, _tile(t)] = _unstack_heads(dq2s[t])
                    dk_s[0:krows, _tile(t)] = dkks[t]
                    dv_s[0:krows, _tile(t)] = dvvs[t]

        def emit(dq, dk, dv):
            cos, sn = c_ref[0], s_ref[0]
            o_ref[0, :, 0:qw] = _unrope(dq, cos, sn).astype(BF16)
            o_ref[0, :, qw:qw + kw] = _unrope(dk, cos, sn).astype(BF16)
            o_ref[0, :, qw + kw:qw + 2 * kw] = dv.astype(BF16)
            if qw + 2 * kw < VAR_W:
                o_ref[0, :, qw + 2 * kw:VAR_W] = jnp.zeros((BLOCK, VAR_W - qw - 2 * kw), BF16)

        if single:
            compute()
            emit(dq_s[...], dk_s[0:BLOCK, :], dv_s[0:BLOCK, :])
            return

        @pl.when(i == 0)
        def _():
            car_q[...] = jnp.zeros_like(car_q)
            car_k[...] = jnp.zeros_like(car_k)
            car_v[...] = jnp.zeros_like(car_v)

        @pl.when(i == nblk)
        def _():
            dk_s[...] = jnp.zeros_like(dk_s)
            dv_s[...] = jnp.zeros_like(dv_s)

        pl.when(i < nblk)(compute)
        emit(car_q[...], car_k[...] + dk_s[0:BLOCK, :], car_v[...] + dv_s[0:BLOCK, :])
        car_q[...] = dq_s[...]
        car_k[...] = dk_s[BLOCK:2 * BLOCK, :]
        car_v[...] = dv_s[BLOCK:2 * BLOCK, :]

    cur = lambda i: jnp.minimum(i, nblk - 1)
    prv = lambda i: jnp.maximum(jnp.minimum(i, nblk - 1) - 1, 0)
    outb = lambda i: jnp.maximum(i - 1, 0)
    qrow = pl.BlockSpec((1, BLOCK, qw), lambda b, j, i: (b, cur(i), j))
    in_specs = [
        pl.BlockSpec((1, BLOCK, qw), lambda b, j, i: (b, cur(i), geo.qidx(j))),
        pl.BlockSpec((1, BLOCK, kw), lambda b, j, i: (b, prv(i), geo.kidx(j))),
        pl.BlockSpec((1, BLOCK, kw), lambda b, j, i: (b, cur(i), geo.kidx(j))),
        pl.BlockSpec((1, BLOCK, kw), lambda b, j, i: (b, prv(i), geo.vidx(j))),
        pl.BlockSpec((1, BLOCK, kw), lambda b, j, i: (b, cur(i), geo.vidx(j))),
        qrow, qrow,
    ]
    ins = [view(qkv, VAR_W)] * 5 + [view(do, qw), view(lse, qw)]
    if has_dlse:
        in_specs.append(qrow)
        ins.append(view(dlse, qw))
    in_specs += [
        pl.BlockSpec((1, BLOCK, PAIR_W), lambda b, j, i: (b, outb(i), j)),
        pl.BlockSpec((1, BLOCK, PAIR_W), lambda b, j, i: (b, outb(i), j)),
        pl.BlockSpec(memory_space=pltpu.SMEM),
    ]
    ins += [view(cosf, PAIR_W), view(sins, PAIR_W), sinks]
    scratch = [pltpu.VMEM((BLOCK, qw), F32), pltpu.VMEM((2 * BLOCK, kw), F32), pltpu.VMEM((2 * BLOCK, kw), F32),
               pltpu.VMEM((BLOCK, qw), F32), pltpu.VMEM((BLOCK, kw), F32), pltpu.VMEM((BLOCK, kw), F32)]
    dqkv, dsink = _pcall(
        body, name=name, grid=(NB, r, nsteps), in_specs=in_specs,
        out_specs=[pl.BlockSpec((1, BLOCK, VAR_W), lambda b, j, i: (b, outb(i), j)),
                   pl.BlockSpec((8, PAIR_W), lambda b, j, i: (0, 0))],
        out_shape=[jax.ShapeDtypeStruct((NB, tsub, r * VAR_W), BF16), jax.ShapeDtypeStruct((8, PAIR_W), F32)],
        scratch_shapes=scratch, compiler_params=_params(("arbitrary", "arbitrary", "arbitrary")),
    )(*ins)
    return dqkv.reshape(NB * T, VAR_W), dsink


class _Rows:
    def __init__(self, N, T, tm):
        self.N, self.tm, self.tpe, self.grid = N, tm, T // tm, (N // tm,)

    def row(self, w, col=0):
        return pl.BlockSpec((self.tm, w), lambda i: (i, col))

    def ex(self, w):
        return pl.BlockSpec((1, 1, w), lambda i: (i // self.tpe, 0, 0))

    def const(self, shape):
        return pl.BlockSpec(shape, lambda i: tuple(0 for _ in shape))

    def view(self, w, r):
        return pl.BlockSpec((None, self.tm // r, r * w), lambda i: (i // self.tpe, i % self.tpe, 0))

    def first_of_example(self):
        return pl.program_id(0) % self.tpe == 0


def _acc(ref, first, val):
    @pl.when(first)
    def _():
        ref[0] = val

    @pl.when(jnp.logical_not(first))
    def _():
        ref[0] += val


def _colsum(v):
    return jnp.sum(v, axis=0, keepdims=True)


def _ln_stats(r):
    mu = jnp.mean(r, axis=-1, keepdims=True)
    xc = r - mu
    var = jnp.mean(xc * xc, axis=-1, keepdims=True)
    rstd = lax.rsqrt(var + LN_EPS)
    return xc * rstd, rstd


def _ln_bwd(dy, xhat, rstd, gain):
    dxh = dy * gain
    return rstd * (dxh - jnp.mean(dxh, axis=-1, keepdims=True) - xhat * jnp.mean(dxh * xhat, axis=-1, keepdims=True))


def _from_view(ref, scr, r):
    if r == 1:
        return ref[...]
    rows, w = ref.shape[0], ref.shape[1] // r
    for j in range(r):
        for c in range(w // LANES):
            scr.at[c][pl.ds(j, rows, stride=r), :] = ref[:, j * w + c * LANES:j * w + (c + 1) * LANES]
    return jnp.concatenate([scr[c] for c in range(w // LANES)], axis=1)


def _to_view(val, ref, scr, r):
    if r == 1:
        ref[...] = val.astype(ref.dtype)
        return
    rows, w = ref.shape[0], ref.shape[1] // r
    for c in range(w // LANES):
        scr[c] = val[:, c * LANES:(c + 1) * LANES]
    for j in range(r):
        for c in range(w // LANES):
            ref[:, j * w + c * LANES:j * w + (c + 1) * LANES] = scr.at[c][pl.ds(j, rows, stride=r), :].astype(ref.dtype)


def _silu_parts(v):
    s = jax.nn.sigmoid(v)
    return v * s, s * (1.0 + v * (1.0 - s))


def _local_step(x, mod, positions, w_in, rest_weights, sinks, ln1_g, ln1_b, ln2_g, ln2_b, target, hook=None):
    hook = hook or (lambda event, **data: None)
    NB, T, D = x.shape
    N = NB * T
    x2 = x.reshape(N, D)
    tgt2 = target.reshape(N, D)
    shift_m, scale_m, gate_m, shift_f, scale_f, gate_f = [mod[:, None, k * D:(k + 1) * D] for k in range(6)]
    cosf, sins = _rope_tables(positions)
    col = jnp.arange(QKV_P)
    vcol = col % VAR_W
    flags = jnp.where(col < VAR_W, vcol < QA_W + KA_W, vcol < 2 * GB_W).astype(F32)[None]
    R = _Rows(N, T, _pick(T, 256))
    sds = jax.ShapeDtypeStruct
    exsum = lambda w=D: sds((NB, 1, w), F32)
    ngrp = len(B_PATTERNS)

    *qkv, u = _inproj(x2, scale_m, shift_m, w_in, cosf, sins, flags, T=T, name="inproj_qkv")
    gates = _mm(u, w_in[QKV_P:], tb=True, out_dtype=BF16, name="inproj_gates")
    oa, la = _attn_fwd(qkv[0], sinks, None, NB=NB, T=T, name="attn_a_fwd")
    oa = oa.reshape(N, QA_W)
    ob_parts = [_attn_fwd(qkv[1 + g], sinks, g, NB=NB, T=T, name=f"attn_b{g}_fwd") for g in range(ngrp)]
    (o1, l1), (o2, l2), (o3, l3) = ob_parts
    w_a, w_b, w_o, w_gu, w_d = rest_weights()
    F = w_d.shape[0]
    dil = [r_ for _, r_ in B_PATTERNS]
    views = [R.view(GB_W, r_) for r_ in dil]
    tokbuf = pltpu.VMEM((GB_W // LANES, R.tm, LANES), F32)

    def merge_fwd(o1r, o2r, o3r, l1r, l2r, l3r, ob_ref, *bufs):
        os_ = [_from_view(ref, bufs[n], dil[n]) for n, ref in enumerate((o1r, o2r, o3r))]
        la, lb, lc = [_from_view(ref, bufs[3 + n], dil[n]) for n, ref in enumerate((l1r, l2r, l3r))]
        mx = jnp.maximum(jnp.maximum(la, lb), lc)
        ea, eb, ec = jnp.exp(la - mx), jnp.exp(lb - mx), jnp.exp(lc - mx)
        ob_ref[...] = ((ea * os_[0] + eb * os_[1] + ec * os_[2]) / (ea + eb + ec)).astype(BF16)

    ob = _pcall(merge_fwd, name="merge_fwd", grid=R.grid, in_specs=views + views, out_specs=R.row(GB_W),
                out_shape=sds((N, GB_W), BF16), scratch_shapes=[tokbuf] * 6,
                compiler_params=_params(("parallel",)))(o1, o2, o3, l1, l2, l3)

    ya = _mm(oa, w_a, out_dtype=BF16, name="branch_a")
    yb = _mm(ob, w_b, b3=True, out_dtype=BF16, name="branch_b")
    f32 = lambda ref: ref[...].astype(F32)

    def gate_fwd(ya_r, yb_r, ga_r, gb_r, mg_ref):
        mg_ref[...] = (jax.nn.sigmoid(f32(ga_r)) * f32(ya_r) + jax.nn.sigmoid(f32(gb_r)) * f32(yb_r)).astype(BF16)

    merged = _pcall(gate_fwd, name="gate_fwd", grid=R.grid, in_specs=[R.row(D), R.row(D), R.row(D, 0), R.row(D, 1)],
                    out_specs=R.row(D), out_shape=sds((N, D), BF16),
                    compiler_params=_params(("parallel",)))(ya, yb, gates, gates)
    y = _mm(merged, w_o, name="out_proj")

    def norm1_fwd(x_r, y_r, gm_r, g_r, b_r, sf_r, hf_r, r1_ref, u2_ref):
        r1 = ALPHA * x_r[...] + (1.0 + gm_r[0]) * y_r[...]
        xhat, _ = _ln_stats(r1)
        x1 = xhat * g_r[...] + b_r[...]
        r1_ref[...] = r1
        u2_ref[...] = (x1 * (1.0 + sf_r[0]) + hf_r[0]).astype(BF16)

    r1, u2 = _pcall(
        norm1_fwd, name="norm1_fwd", grid=R.grid,
        in_specs=[R.row(D), R.row(D), R.ex(D), R.const((1, D)), R.const((1, D)), R.ex(D), R.ex(D)],
        out_specs=[R.row(D)] * 2, out_shape=[sds((N, D), F32), sds((N, D), BF16)],
        compiler_params=_params(("parallel",)))(x2, y, gate_m, ln1_g, ln1_b, scale_f, shift_f)

    tnf = w_gu.shape[2]
    nft = w_gu.shape[0] // 2
    tmf = _pick(N, 512)

    def ffn_up(u_r, wg_r, wu_r, hg_ref, hu_ref, a_ref):
        hg = jnp.dot(u_r[...], wg_r[...], preferred_element_type=F32)
        hu = jnp.dot(u_r[...], wu_r[...], preferred_element_type=F32)
        sl, _ = _silu_parts(hg)
        hg_ref[...] = hg.astype(BF16)
        hu_ref[...] = hu.astype(BF16)
        a_ref[...] = (sl * hu).astype(BF16)

    ftile = pl.BlockSpec((tmf, tnf), lambda j, i: (i, j))
    hg, hu, act = _pcall(
        ffn_up, name="ffn_up", grid=(nft, N // tmf),
        in_specs=[pl.BlockSpec((tmf, D), lambda j, i: (i, 0)), pl.BlockSpec((None, D, tnf), lambda j, i: (j, 0, 0)),
                  pl.BlockSpec((None, D, tnf), lambda j, i: (j + nft, 0, 0))],
        out_specs=[ftile] * 3, out_shape=[sds((N, F), BF16)] * 3,
        compiler_params=_params(("arbitrary", "parallel")))(u2, w_gu, w_gu)
    y2 = _mm(act, w_d, name="ffn_down")

    def norm2_loss_bwd(r1_r, g1_r, b1_r, y2_r, t_r, gf_r, g_r, b_r, dy2_ref, dx1_ref, dgf_ref, dg_ref, db_ref, loss_ref):
        first = R.first_of_example()
        y2v = y2_r[...]
        x1 = _ln_stats(r1_r[...])[0] * g1_r[...] + b1_r[...]
        r2 = ALPHA * x1 + (1.0 + gf_r[0]) * y2v
        xhat, rstd = _ln_stats(r2)
        err = xhat * g_r[...] + b_r[...] - t_r[...]
        dx2 = err * (1.0 / D)
        dr2 = _ln_bwd(dx2, xhat, rstd, g_r[...])
        dy2_ref[...] = ((1.0 + gf_r[0]) * dr2).astype(BF16)
        dx1_ref[...] = ALPHA * dr2
        _acc(dgf_ref, first, _colsum(dr2 * y2v))
        _acc(dg_ref, first, _colsum(dx2 * xhat))
        _acc(db_ref, first, _colsum(dx2))
        part = 0.5 * jnp.sum(jnp.mean(err * err, axis=-1, keepdims=True))
        _acc(loss_ref, first, jnp.broadcast_to(part, (1, 128)))

    dy2, dx1p, dgate_f, dg2, db2, loss_p = _pcall(
        norm2_loss_bwd, name="norm2_loss_bwd", grid=R.grid,
        in_specs=[R.row(D), R.const((1, D)), R.const((1, D)), R.row(D), R.row(D), R.ex(D), R.const((1, D)),
                  R.const((1, D))],
        out_specs=[R.row(D), R.row(D), R.ex(D), R.ex(D), R.ex(D), R.ex(128)],
        out_shape=[sds((N, D), BF16), sds((N, D), F32), exsum(), exsum(), exsum(), exsum(128)],
        compiler_params=_params(("arbitrary",)))(r1, ln1_g, ln1_b, y2, tgt2, gate_f, ln2_g, ln2_b)

    g_wd = _mm(act, dy2, ta=True, out_dtype=BF16, name="ffn_down_dw")

    tmd = _pick(N, 256)

    def ffn_down_dx(dy_r, wd_r, hg_r, hu_r, dh_ref):
        for t in range(nft):
            cs = slice(t * tnf, (t + 1) * tnf)
            da = lax.dot_general(dy_r[...], wd_r[cs, :], _NT, preferred_element_type=F32)
            sl, dsl = _silu_parts(hg_r[:, cs].astype(F32))
            dh_ref[:, cs] = (da * hu_r[:, cs].astype(F32) * dsl).astype(BF16)
            dh_ref[:, F + t * tnf:F + (t + 1) * tnf] = (da * sl).astype(BF16)

    rowd = lambda w_: pl.BlockSpec((tmd, w_), lambda i: (i, 0))
    dh = _pcall(
        ffn_down_dx, name="ffn_down_dx", grid=(N // tmd,),
        in_specs=[rowd(D), pl.BlockSpec((F, D), lambda i: (0, 0)), rowd(F), rowd(F)],
        out_specs=rowd(2 * F), out_shape=sds((N, 2 * F), BF16),
        compiler_params=_params(("parallel",)))(dy2, w_d, hg, hu)
    def ffn_up_dx(dh_r, w_r, o_ref):
        acc = None
        for s_ in range(w_gu.shape[0]):
            part = lax.dot_general(dh_r[:, s_ * tnf:(s_ + 1) * tnf], w_r[s_], _NT, preferred_element_type=F32)
            acc = part if acc is None else acc + part
        o_ref[...] = acc

    du2 = _pcall(
        ffn_up_dx, name="ffn_up_dx", grid=(N // tmf,),
        in_specs=[pl.BlockSpec((tmf, 2 * F), lambda i: (i, 0)), pl.BlockSpec(w_gu.shape, lambda i: (0, 0, 0))],
        out_specs=pl.BlockSpec((tmf, D), lambda i: (i, 0)), out_shape=sds((N, D), F32),
        compiler_params=_params(("parallel",)))(dh, w_gu)
    g_wgu = _mm(u2, dh, ta=True, out3=w_gu.shape[0], out_dtype=BF16, name="ffn_up_dw")

    def norm1_bwd(dx1p_r, du2_r, r1_r, y_r, sf_r, gm_r, g_r, b_r,
                  dxp_ref, dy_ref, dsf_ref, dhf_ref, dgm_ref, dg_ref, db_ref):
        first = R.first_of_example()
        du2v = du2_r[...]
        dx1 = dx1p_r[...] + du2v * (1.0 + sf_r[0])
        xhat, rstd = _ln_stats(r1_r[...])
        dr1 = _ln_bwd(dx1, xhat, rstd, g_r[...])
        dxp_ref[...] = ALPHA * dr1
        dy_ref[...] = ((1.0 + gm_r[0]) * dr1).astype(BF16)
        _acc(dsf_ref, first, _colsum(du2v * (xhat * g_r[...] + b_r[...])))
        _acc(dhf_ref, first, _colsum(du2v))
        _acc(dgm_ref, first, _colsum(dr1 * y_r[...]))
        _acc(dg_ref, first, _colsum(dx1 * xhat))
        _acc(db_ref, first, _colsum(dx1))

    dxp, dy, dscale_f, dshift_f, dgate_m, dg1, db1 = _pcall(
        norm1_bwd, name="norm1_bwd", grid=R.grid,
        in_specs=[R.row(D)] * 4 + [R.ex(D), R.ex(D), R.const((1, D)), R.const((1, D))],
        out_specs=[R.row(D), R.row(D)] + [R.ex(D)] * 5,
        out_shape=[sds((N, D), F32), sds((N, D), BF16)] + [exsum()] * 5,
        compiler_params=_params(("arbitrary",)))(dx1p, du2, r1, y, scale_f, gate_m, ln1_g, ln1_b)

    dmerged = _mm(dy, w_o, tb=True, out_dtype=BF16, name="out_proj_dx")
    g_wo = _mm(merged, dy, ta=True, out_dtype=BF16, name="out_proj_dw")

    def gate_bwd(dm_r, ya_r, yb_r, ga_r, gb_r, dya_ref, dyb_ref, dg_ref):
        dm = f32(dm_r)
        sa, sb = jax.nn.sigmoid(f32(ga_r)), jax.nn.sigmoid(f32(gb_r))
        dya_ref[...] = (dm * sa).astype(BF16)
        dyb_ref[...] = (dm * sb).astype(BF16)
        dg_ref[:, :D] = (dm * f32(ya_r) * sa * (1.0 - sa)).astype(BF16)
        dg_ref[:, D:] = (dm * f32(yb_r) * sb * (1.0 - sb)).astype(BF16)

    dya, dyb, dgates = _pcall(
        gate_bwd, name="gate_bwd", grid=R.grid, in_specs=[R.row(D)] * 3 + [R.row(D, 0), R.row(D, 1)],
        out_specs=[R.row(D), R.row(D), R.row(2 * D)],
        out_shape=[sds((N, D), BF16), sds((N, D), BF16), sds((N, 2 * D), BF16)],
        compiler_params=_params(("parallel",)))(dmerged, ya, yb, gates, gates)

    doa = _mm(dya, w_a, tb=True, out_dtype=BF16, name="branch_a_dx")
    g_wa = _mm(oa, dya, ta=True, out_dtype=BF16, name="branch_a_dw")
    dob = _mm(dyb, w_b, tb=True, b3=True, name="branch_b_dx")
    g_wb = _mm(ob, dyb, ta=True, out3=w_b.shape[0], out_dtype=BF16, name="branch_b_dw")
    hook("rest_grads", g_wa=g_wa, g_wb=g_wb, g_wo=g_wo, g_wgu=g_wgu, g_wd=g_wd)

    seg = (jnp.arange(GB_W)[:, None] // HEAD_DIM == jnp.arange(GB_W)[None, :] // HEAD_DIM).astype(BF16)

    def merge_bwd(dob_r, o1r, o2r, o3r, l1r, l2r, l3r, seg_r, d1, d2, d3, e1, e2, e3, *bufs):
        dob_v = dob_r[...]
        os_ = [_from_view(ref, bufs[n], dil[n]) for n, ref in enumerate((o1r, o2r, o3r))]
        la, lb, lc = [_from_view(ref, bufs[3 + n], dil[n]) for n, ref in enumerate((l1r, l2r, l3r))]
        mx = jnp.maximum(jnp.maximum(la, lb), lc)
        ea, eb, ec = jnp.exp(la - mx), jnp.exp(lb - mx), jnp.exp(lc - mx)
        inv = 1.0 / (ea + eb + ec)
        ws = [ea * inv, eb * inv, ec * inv]

        def headsum(v):
            hi = v.astype(BF16)
            r1_ = v - hi.astype(F32)
            mid = r1_.astype(BF16)
            lo = (r1_ - mid.astype(F32)).astype(BF16)
            sm = seg_r[...]
            return (jnp.dot(hi, sm, preferred_element_type=F32) + jnp.dot(mid, sm, preferred_element_type=F32)
                    + jnp.dot(lo, sm, preferred_element_type=F32))

        dws = [headsum(dob_v * o) for o in os_]
        mean = ws[0] * dws[0] + ws[1] * dws[1] + ws[2] * dws[2]
        for n, (w_, dw_, d_ref, e_ref) in enumerate(zip(ws, dws, (d1, d2, d3), (e1, e2, e3))):
            _to_view(w_ * dob_v, d_ref, bufs[6], dil[n])
            _to_view(w_ * (dw_ - mean), e_ref, bufs[7], dil[n])

    vshape = lambda r_, dt: sds((NB, T // r_, r_ * GB_W), dt)
    mb = _pcall(
        merge_bwd, name="merge_bwd", grid=R.grid, in_specs=[R.row(GB_W)] + views + views + [R.const((GB_W, GB_W))],
        out_specs=views + views, out_shape=[vshape(r_, BF16) for r_ in dil] + [vshape(r_, F32) for r_ in dil],
        scratch_shapes=[tokbuf] * 8, compiler_params=_params(("parallel",)))(dob, o1, o2, o3, l1, l2, l3, seg)
    do_b, dlse_b = mb[:3], mb[3:]
    hook("merge_bwd_done")

    dqkv_a, dsink = _attn_bwd(qkv[0], doa, la, None, cosf, sins, sinks, None, NB=NB, T=T, name="attn_a_bwd")
    hook("attn_a_bwd_done")
    dqkv = [dqkv_a]
    for g in range(ngrp):
        dqkv.append(_attn_bwd(qkv[1 + g], do_b[g], (l1, l2, l3)[g], dlse_b[g], cosf, sins, sinks, g, NB=NB, T=T,
                              name=f"attn_b{g}_bwd")[0])
        hook(f"attn_b{g}_bwd_done")

    g_win = [_mm(dseg, u, ta=True, out_dtype=BF16, name=f"inproj_dw{n}") for n, dseg in enumerate(dqkv + [dgates])]
    hook("win_grads", g_win=g_win)
    wvar = lambda v: (w_in, (VAR_W, D), (v, 0))
    du = _mm_multi(dqkv[:1], [wvar(0)], name="inproj_dx0")
    hook("inproj_dx0_done")
    du = _mm_multi(dqkv[1:] + [dgates], [wvar(v) for v in range(1, N_VAR)] + [(w_in, (2 * D, D), (QKV_P // (2 * D), 0))],
                   add=du, tm=256, name="inproj_dx1")
    hook("inproj_dx1_done")

    def x_bwd(dxp_r, du_r, x_r, sm_r, gx_ref, dsm_ref, dhm_ref):
        first = R.first_of_example()
        duv = du_r[...]
        gx_ref[...] = dxp_r[...] + duv * (1.0 + sm_r[0])
        _acc(dsm_ref, first, _colsum(duv * x_r[...]))
        _acc(dhm_ref, first, _colsum(duv))

    gx, dscale_m, dshift_m = _pcall(
        x_bwd, name="x_bwd", grid=R.grid, in_specs=[R.row(D)] * 3 + [R.ex(D)],
        out_specs=[R.row(D), R.ex(D), R.ex(D)], out_shape=[sds((N, D), F32), exsum(), exsum()],
        compiler_params=_params(("arbitrary",)))(dxp, du, x2, scale_m)
    hook("x_bwd_done")

    dmod =jnp.concatenate([dshift_m, dscale_m, dgate_m, dshift_f, dscale_f, dgate_f], axis=-1)[:, 0]
    ln_grads = jnp.concatenate([dg1, db1, dg2, db2], axis=1)
    return dict(loss=loss_p[:, 0, 0], grad_x=gx.reshape(NB, T, D), g_win=g_win, g_wa=g_wa, g_wb=g_wb, g_wo=g_wo,
                g_wgu=g_wgu, g_wd=g_wd, dmod=dmod, ln_grads=ln_grads, dsink=dsink[0, :A_Q_HEADS])


def _coords():
    return lax.axis_index("x"), lax.axis_index("y"), lax.axis_index("c")


def _allgather_small(blk, *, name):
    m_per, n = blk.shape

    def body(x_ref, out_ref, send_sems, recv_sems, local_sem):
        x, y, c = _coords()
        me, sibling = (x, y, c), (x, y, 1 - c)
        chips = [(1 - x, y), (x, 1 - y), (1 - x, 1 - y)]

        def rows(px, py, pc):
            return out_ref.at[pl.ds((4 * px + 2 * py + pc) * m_per, m_per), :]

        def copy(k, block, to, src=None):
            return pltpu.make_async_remote_copy(
                src_ref=rows(*block) if src is None else src, dst_ref=rows(*block),
                send_sem=send_sems.at[k], recv_sem=recv_sems.at[k], device_id=to, device_id_type=MESH)

        mine = pltpu.make_async_copy(x_ref, rows(*me), local_sem)
        mine.start()
        first = [copy(0, me, sibling, src=x_ref)]
        first += [copy(1 + j, me, (*chip, c), src=x_ref) for j, chip in enumerate(chips)]
        for cp in first:
            cp.start()
        passed = [copy(4 + j, (*chip, c), sibling) for j, chip in enumerate(chips)]
        for j, chip in enumerate(chips):
            copy(1 + j, (*chip, c), me).wait_recv()
            passed[j].start()
        copy(0, sibling, me).wait_recv()
        for j, chip in enumerate(chips):
            copy(4 + j, (*chip, 1 - c), me).wait_recv()
        for cp in first + passed:
            cp.wait_send()
        mine.wait()

    return _pcall(
        body, name=name, out_shape=jax.ShapeDtypeStruct((8 * m_per, n), blk.dtype),
        in_specs=[pl.BlockSpec(memory_space=pltpu.VMEM)], out_specs=pl.BlockSpec(memory_space=pltpu.VMEM),
        scratch_shapes=[pltpu.SemaphoreType.DMA((7,)), pltpu.SemaphoreType.DMA((7,)), pltpu.SemaphoreType.DMA],
        compiler_params=pltpu.CompilerParams(vmem_limit_bytes=VMEM_LIMIT_BYTES),
    )(blk)


def _exchange(srcs, dsts, plan, *, name, dst_inits=None):
    na = len(dsts)
    nrem = len(plan(0, 0, 0))

    def body(*refs):
        refs = list(refs)
        src_refs = [refs.pop(0) for _ in range(na)] if srcs is not None else None
        if dst_inits is not None:
            del refs[:na]
        dst_refs, (send_sems, recv_sems) = refs[:na], refs[na:]
        start, wait = _copies(dst_refs if src_refs is None else src_refs, dst_refs, send_sems, recv_sems, plan)
        start()
        wait()

    hbm = pl.BlockSpec(memory_space=pl.ANY)
    ins = (list(srcs) if srcs is not None else []) + (list(dst_inits) if dst_inits is not None else [])
    base = na if srcs is not None else 0
    aliases = {base + a: a for a in range(na)} if dst_inits is not None else {}
    return _pcall(
        body, name=name, out_shape=list(dsts), in_specs=[hbm] * len(ins), out_specs=[hbm] * na,
        input_output_aliases=aliases,
        scratch_shapes=[pltpu.SemaphoreType.DMA((na * nrem,)), pltpu.SemaphoreType.DMA((na * nrem,))],
    )(*ins)


def _other_chips(x, y):
    return [(1 - x, y), (x, 1 - y), (1 - x, 1 - y)]


def _round(ride, carrier, name):
    if carrier is not None:
        _RIDES.setdefault(carrier, []).append(ride)
        return
    srcs = ride.srcs() if callable(ride.srcs) else ride.srcs
    inits = ride.dst_inits() if callable(ride.dst_inits) else ride.dst_inits
    ride.out = list(_exchange(srcs, ride.dsts, ride.plan, name=name, dst_inits=inits))


class _Gather:
    def __init__(self, shards, chip, tag, carriers=(None, None)):
        def plan_ici(x, y, c):
            k = 2 * x + y
            return [((c,), (k, c), (2 * px + py, c), (px, py, c)) for px, py in _other_chips(x, y)]

        def plan_d2d(x, y, c):
            return [((2 * px + py, c), (2 * px + py, c), (2 * px + py, 1 - c), (x, y, 1 - c))
                    for px, py in _other_chips(x, y)]

        self.shards, self.chip = shards, chip
        dsts = [jax.ShapeDtypeStruct((4,) + s.shape, s.dtype) for s in shards]
        ici = _Ride(shards, dsts, plan_ici)
        self.d2d = _Ride(None, dsts, plan_d2d, dst_inits=lambda: ici.out)
        _round(ici, carriers[0], f"gather_{tag}_ici")
        _round(self.d2d, carriers[1], f"gather_{tag}_d2d")

    def result(self):
        full = [lax.dynamic_update_index_in_dim(f, s, self.chip, 0) for f, s in zip(self.d2d.out, self.shards)]
        return [f.reshape((4, 2 * f.shape[2], f.shape[3])) for f in full]


def _index_operand(i):
    return jnp.reshape(i, (1,)).astype(jnp.int32)


def _add_pairs(g, f, ci, *, name):
    s, _, hr, wd = g.shape
    tr = _pick(hr, 600, 16)

    def body(c_ref, a_ref, b_ref, o_ref):
        o_ref[...] = (a_ref[...].astype(F32) + b_ref[...].astype(F32)).astype(BF16)

    spec = pl.BlockSpec((1, tr, wd), lambda j, i, c: (j, i, 0))
    grid_spec = pltpu.PrefetchScalarGridSpec(
        num_scalar_prefetch=1, grid=(s, hr // tr),
        in_specs=[pl.BlockSpec((1, None, tr, wd), lambda j, i, c: (j, c[0], i, 0)), spec], out_specs=spec)
    return _pcall(body, name=name, grid_spec=grid_spec, out_shape=jax.ShapeDtypeStruct(f.shape, BF16),
                  compiler_params=_params(("parallel", "parallel")))(_index_operand(ci), g, f)


def _sum_chips(landed, pairs, chip, *, name):
    s, hr, wd = landed.shape
    tr = _pick(hr, 600, 16)

    def body(k_ref, l_ref, p_ref, o_ref):
        acc = None
        for k in range(s):
            part = jnp.where(k_ref[0] == k, p_ref[k], l_ref[k]).astype(F32)
            acc = part if acc is None else acc + part
        o_ref[...] = acc

    spec = pl.BlockSpec((s, tr, wd), lambda i, k: (0, i, 0))
    grid_spec = pltpu.PrefetchScalarGridSpec(
        num_scalar_prefetch=1, grid=(hr // tr,), in_specs=[spec, spec],
        out_specs=pl.BlockSpec((tr, wd), lambda i, k: (i, 0)))
    return _pcall(body, name=name, grid_spec=grid_spec, out_shape=jax.ShapeDtypeStruct((hr, wd), F32),
                  compiler_params=_params(("parallel",)))(_index_operand(chip), landed, pairs)


class _ReduceScatter:
    def __init__(self, gs, chip, ci, tag):
        self.gs, self.chip, self.ci, self.tag = gs, chip, ci, tag
        self.half_t = [jax.ShapeDtypeStruct((g.shape[0],) + g.shape[2:], BF16) for g in gs]

    def pair(self, carrier=None):
        plan = lambda x, y, c: [((slice(None), 1 - c), (), (), (x, y, 1 - c))]
        self.r1 = _Ride(self.gs, self.half_t, plan)
        _round(self.r1, carrier, f"reduce_{self.tag}_pair")

    def chips(self, carrier=None):
        def plan(x, y, c):
            k = 2 * x + y
            return [((2 * px + py,), (k,), (2 * px + py,), (px, py, c)) for px, py in _other_chips(x, y)]

        self.pairs = [_add_pairs(g, f, self.ci, name=f"reduce_{self.tag}_pair_add{n}")
                      for n, (g, f) in enumerate(zip(self.gs, self.r1.out))]
        self.r2 = _Ride(self.pairs, self.half_t, plan)
        _round(self.r2, carrier, f"reduce_{self.tag}_chips")

    def halves(self, carrier=None):
        plan = lambda x, y, c: [((), (c,), (1 - c,), (x, y, 1 - c))]
        self.mine = [_sum_chips(l, p, self.chip, name=f"reduce_{self.tag}_chip_sum{n}")
                     for n, (l, p) in enumerate(zip(self.r2.out, self.pairs))]
        self.r3 = _Ride(self.mine, [jax.ShapeDtypeStruct((2,) + m.shape, F32) for m in self.mine], plan)
        _round(self.r3, carrier, f"reduce_{self.tag}_halves")

    def result(self):
        return [lax.dynamic_update_index_in_dim(b, m, self.ci, 0).reshape(2 * m.shape[0], m.shape[1])
                for b, m in zip(self.r3.out, self.mine)]


def _ada_fwd(c_all, w_sh, b_sh, *, name):
    nb, d = c_all.shape
    wcols = w_sh.shape[1]
    tn = _pick(wcols, 512)

    def body(c_ref, w_ref, b_ref, o_ref, a_ref):
        cv = c_ref[...]
        act = cv * jax.nn.sigmoid(cv)
        a_ref[...] = act
        o_ref[...] = jnp.dot(act.astype(BF16), w_ref[...].astype(BF16), preferred_element_type=F32) + b_ref[...]

    return _pcall(
        body, name=name, grid=(wcols // tn,),
        in_specs=[pl.BlockSpec((nb, d), lambda j: (0, 0)), pl.BlockSpec((d, tn), lambda j: (0, j)),
                  pl.BlockSpec((1, tn), lambda j: (0, j))],
        out_specs=[pl.BlockSpec((nb, tn), lambda j: (0, j)), pl.BlockSpec((nb, d), lambda j: (0, 0))],
        out_shape=[jax.ShapeDtypeStruct((nb, wcols), F32), jax.ShapeDtypeStruct((nb, d), F32)],
        compiler_params=_params(("arbitrary",)))(c_all, w_sh, b_sh)


def _sum_devices(g, *, name):
    nd, m, w = g.shape

    def body(g_ref, o_ref):
        acc = g_ref[0]
        for k in range(1, nd):
            acc = acc + g_ref[k]
        o_ref[...] = acc

    return _pcall(body, name=name, out_shape=jax.ShapeDtypeStruct((m, w), F32),
                  compiler_params=pltpu.CompilerParams(vmem_limit_bytes=VMEM_LIMIT_BYTES))(g)


def _adamw(w, g, m, v, *, name):
    rows, cols = w.shape[-2:]
    tr = _pick(rows, max(8, (1 << 18) // cols), 8)
    c1 = 1.0 / (1.0 - ADAM_B1 ** ADAM_STEP)
    c2 = 1.0 / (1.0 - ADAM_B2 ** ADAM_STEP)

    def body(w_ref, g_ref, m_ref, v_ref, d_ref, nm_ref, nv_ref):
        gv = g_ref[...]
        nm = ADAM_B1 * m_ref[...] + (1.0 - ADAM_B1) * gv
        nv = ADAM_B2 * v_ref[...] + (1.0 - ADAM_B2) * (gv * gv)
        d_ref[...] = -ADAM_LR * ((nm * c1) / (jnp.sqrt(nv * c2) + ADAM_EPS) + ADAM_WD * w_ref[...])
        nm_ref[...] = nm
        nv_ref[...] = nv

    gspec = pl.BlockSpec((tr, cols), lambda i: (i, 0))
    spec = pl.BlockSpec((None, tr, cols), lambda i: (0, i, 0)) if w.ndim == 3 else gspec
    shp = jax.ShapeDtypeStruct(w.shape, F32)
    return _pcall(body, name=name, grid=(rows // tr,), in_specs=[spec, gspec, spec, spec], out_specs=[spec] * 3,
                  out_shape=[shp] * 3, compiler_params=_params(("parallel",)))(w, g, m, v)


def _permute_in_rows(wt):
    ngrp = len(B_PATTERNS)
    qb, kb, vb = (wt[A_W + n * QB_W:A_W + (n + 1) * QB_W] for n in range(3))
    parts = [wt[:A_W], jnp.zeros((VAR_W - A_W, wt.shape[1]), wt.dtype)]
    for g in range(ngrp):
        parts += [t[g * GB_W:(g + 1) * GB_W] for t in (qb, kb, vb)]
    return jnp.concatenate(parts + [wt[A_W + 3 * QB_W:]], axis=0)


def _unpermute_in_grads(pieces):
    ga, groups, gg = pieces[0], pieces[1:-1], pieces[-1]
    rows = [ga[:A_W]]
    for n in range(3):
        rows += [gp[n * GB_W:(n + 1) * GB_W] for gp in groups]
    return jnp.concatenate(rows + [gg], axis=0)


def kernel(x, c, positions, w_ada, b_ada, w_in, sinks, w_branch_a, w_branch_b, w_o, ln1_g, ln1_b, w_gate_up, w_down, ln2_g, ln2_b, loss_target, m_w_ada, m_b_ada, m_w_in, m_sinks, m_w_branch_a, m_w_branch_b, m_w_o, m_ln1_g, m_ln1_b, m_w_gate_up, m_w_down, m_ln2_g, m_ln2_b, v_w_ada, v_b_ada, v_w_in, v_sinks, v_w_branch_a, v_w_branch_b, v_w_o, v_ln1_g, v_ln1_b, v_w_gate_up, v_w_down, v_ln2_g, v_ln2_b):
    xi, yi, ci = _coords()
    chip = 2 * xi + yi
    dev = 4 * xi + 2 * yi + ci
    NB, T, D = x.shape
    nchip, ndev = 4, 8
    ada_cols = w_ada.shape[2]

    c_blk = jnp.zeros((8, D), F32).at[:NB].set(c)
    c_all = _allgather_small(c_blk, name="gather_c").reshape(ndev, 8, D)[:, :NB].reshape(ndev * NB, D)
    b_sh = lax.dynamic_slice(b_ada, (0, chip * ada_cols), (1, ada_cols))
    mod_part, c_act = _ada_fwd(c_all, w_ada[0], b_sh, name="ada_fwd")
    mod_g = _allgather_small(mod_part, name="gather_mod").reshape(nchip, 2, ndev * NB, ada_cols)[:, 0]
    mod_all = jnp.transpose(mod_g, (1, 0, 2)).reshape(ndev * NB, nchip * ada_cols)
    mod = lax.dynamic_slice(mod_all, (NB * dev, 0), (NB, nchip * ada_cols))

    ra, ro, rd = w_branch_a.shape[1], w_o.shape[1], w_down.shape[1]
    rowsh = jnp.concatenate([w_branch_a[0], w_o[0], w_down[0]], axis=0)
    halves = lambda a: a.reshape(a.shape[:-2] + (2, a.shape[-2] // 2, a.shape[-1]))
    whole = lambda a: a.reshape(a.shape[:-3] + (2 * a.shape[-2], a.shape[-1]))
    tr = lambda a: jnp.swapaxes(a, -1, -2)
    shards = [halves(w.astype(BF16)) for w in (tr(w_in[0]), rowsh, w_branch_b[0], w_gate_up[0])]
    (g_in,) = _Gather(shards[:1], chip, "w_in").result()
    w_in_f = _permute_in_rows(g_in.reshape(nchip * g_in.shape[1], D))
    mix = _Gather(shards[1:3], chip, "w_mix", carriers=("inproj_qkv", "attn_a_fwd"))
    ffn = _Gather(shards[3:], chip, "w_ffn", carriers=("attn_a_fwd", "attn_b0_fwd"))

    def rest_weights():
        (g_rows, w_b_f), (w_gu_f,) = mix.result(), ffn.result()
        return (g_rows[:, :ra].reshape(nchip * ra, D), w_b_f, g_rows[:, ra:ra + ro].reshape(nchip * ro, D), w_gu_f,
                g_rows[:, ra + ro:].reshape(nchip * rd, D))

    red = {}

    def hook(event, **g):
        if event == "rest_grads":
            gr_rows = jnp.concatenate([g["g_wa"].reshape(nchip, ra, D), g["g_wo"].reshape(nchip, ro, D),
                                       g["g_wd"].reshape(nchip, rd, D)], axis=1)
            red["ffn"] = _ReduceScatter([halves(g["g_wgu"])], chip, ci, "ffn")
            red["mix"] = _ReduceScatter([halves(gr_rows), halves(g["g_wb"])], chip, ci, "mix")
            red["ffn"].pair(carrier="merge_bwd")
            red["mix"].pair(carrier="merge_bwd")
        elif event == "merge_bwd_done":
            red["ffn"].chips(carrier="attn_a_bwd")
            red["mix"].chips(carrier="attn_b0_bwd")
        elif event == "attn_a_bwd_done":
            red["ffn"].halves(carrier="attn_b0_bwd")
        elif event == "attn_b0_bwd_done":
            red["mix"].halves(carrier="attn_b1_bwd")
        elif event == "win_grads":
            gr_in = _unpermute_in_grads(g["g_win"])
            red["w_in"] = _ReduceScatter([halves(gr_in.reshape(nchip, gr_in.shape[0] // nchip, D))], chip, ci, "w_in")
            red["w_in"].pair(carrier="inproj_dx0")
        elif event == "inproj_dx0_done":
            red["w_in"].chips(carrier="inproj_dx1")
        elif event == "inproj_dx1_done":
            red["w_in"].halves(carrier="x_bwd")

    res = _local_step(x, mod, positions, w_in_f, rest_weights, sinks[0], ln1_g, ln1_b, ln2_g, ln2_b, loss_target, hook)
    (g_w_in,) = red["w_in"].result()
    (g_rows_red, g_w_b), (g_w_gu,) = red["mix"].result(), red["ffn"].result()
    g_w_a, g_w_o, g_w_d = g_rows_red[:ra], g_rows_red[ra:ra + ro], g_rows_red[ra + ro:]

    small_rows = 24
    misc = jnp.zeros((1, D), F32).at[0, :A_Q_HEADS].set(res["dsink"]).at[0, A_Q_HEADS].set(jnp.sum(res["loss"]))
    small = jnp.concatenate([res["dmod"].reshape(NB * 6, D), jnp.sum(res["ln_grads"], axis=0), misc,
                             jnp.zeros((small_rows - NB * 6 - 5, D), F32)], axis=0)
    small_all = _allgather_small(small, name="gather_small").reshape(ndev, small_rows, D)
    dmod_all = small_all[:, :NB * 6].reshape(ndev * NB, 6 * D)
    sums = _sum_devices(small_all, name="sum_small")
    g_b_ada = (sums[0:6] + sums[6:12]).reshape(1, 6 * D)
    g_ln1_g, g_ln1_b, g_ln2_g, g_ln2_b = (sums[12 + n][None] for n in range(4))
    g_sinks = sums[16, :A_Q_HEADS][None]
    loss = sums[16, A_Q_HEADS]
    dmod_sh = lax.dynamic_slice(dmod_all, (0, chip * ada_cols), (ndev * NB, ada_cols))
    g_w_ada = _mm(c_act, dmod_sh, ta=True, name="ada_dw")

    names = ["w_ada", "b_ada", "w_in", "sinks", "w_branch_a", "w_branch_b", "w_o", "ln1_g", "ln1_b",
             "w_gate_up", "w_down", "ln2_g", "ln2_b"]
    ws = [w_ada, b_ada, w_in, sinks, w_branch_a, w_branch_b, w_o, ln1_g, ln1_b, w_gate_up, w_down, ln2_g, ln2_b]
    ms = [m_w_ada, m_b_ada, m_w_in, m_sinks, m_w_branch_a, m_w_branch_b, m_w_o, m_ln1_g, m_ln1_b, m_w_gate_up,
          m_w_down, m_ln2_g, m_ln2_b]
    vs = [v_w_ada, v_b_ada, v_w_in, v_sinks, v_w_branch_a, v_w_branch_b, v_w_o, v_ln1_g, v_ln1_b, v_w_gate_up,
          v_w_down, v_ln2_g, v_ln2_b]
    gs = [g_w_ada, g_b_ada, g_w_in, g_sinks, g_w_a, g_w_b, g_w_o, g_ln1_g, g_ln1_b, g_w_gu, g_w_d, g_ln2_g, g_ln2_b]
    grads, deltas, new_ms, new_vs = [], [], [], []
    for name, w, g, m, v in zip(names, ws, gs, ms, vs):
        flip = tr if name == "w_in" else (lambda a: a)
        w, m, v = flip(w), flip(m), flip(v)
        g2 = g.reshape(w.shape[-2:])
        d, nm, nv = _adamw(w, g2, m, v, name="adamw_" + name)
        grads.append(flip(g2.reshape(w.shape)))
        deltas.append(flip(d))
        new_ms.append(flip(nm))
        new_vs.append(flip(nv))
    return (loss, res["grad_x"], *grads, *deltas, *new_ms, *new_vs)
```

```python
import functools

import jax
import jax.numpy as jnp
from jax import lax
from jax.experimental import pallas as pl
from jax.experimental.pallas import tpu as pltpu

F32 = jnp.float32
BF16 = jnp.bfloat16
MESH = pl.DeviceIdType.MESH

HEAD_DIM = 64
LANES = 128
PAIR_W = 2 * HEAD_DIM
BLOCK = 128
A_Q_HEADS = 16
A_KV_HEADS = 2
A_WINDOW = 128
B_PATTERNS = ((128, 1), (512, 4), (2048, 16))
B_GROUP_HEADS = 8
QA_W = A_Q_HEADS * HEAD_DIM
KA_W = A_KV_HEADS * HEAD_DIM
GB_W = B_GROUP_HEADS * HEAD_DIM
QB_W = GB_W * len(B_PATTERNS)
A_W = QA_W + 2 * KA_W
VAR_W = 3 * GB_W
N_VAR = 1 + len(B_PATTERNS)
VAR_DIL = (1,) + tuple(r for _, r in B_PATTERNS)
QKV_P = N_VAR * VAR_W
ROPE_THETA = 10000.0
LN_EPS = 1e-5
NEG_INF = -1e30
DEPTH = 1
ALPHA = (2 * DEPTH) ** 0.25
SCALE = HEAD_DIM ** -0.5

ADAM_LR, ADAM_B1, ADAM_B2, ADAM_EPS, ADAM_WD, ADAM_STEP = 0.001, 0.9, 0.999, 1e-08, 0.01, 10

VMEM_LIMIT_BYTES = 56 * 1024 * 1024
MM_TILE_BYTES = 36 * 1024 * 1024
MM_WHOLE_K = 4096


def _params(sem=None):
    return pltpu.CompilerParams(dimension_semantics=sem, vmem_limit_bytes=VMEM_LIMIT_BYTES)


_RIDES = {}


def _pcall(body, *, name, **kw):
    rides = _RIDES.pop(name, None)
    if rides is None:
        return pl.pallas_call(body, name=name, **kw)
    return _riding_call(body, rides, name=name, **kw)


def _copies(src_refs, dst_refs, send_sems, recv_sems, plan):
    x, y, c = lax.axis_index("x"), lax.axis_index("y"), lax.axis_index("c")
    remote = plan(x, y, c)
    nrem = len(remote)
    at = lambda ref, idx: ref.at[idx] if idx else ref

    def copy(a, n, landing):
        si, di, ri, peer = remote[n]
        return pltpu.make_async_remote_copy(
            src_ref=at(src_refs[a], si), dst_ref=at(dst_refs[a], ri if landing else di),
            send_sem=send_sems.at[a * nrem + n], recv_sem=recv_sems.at[a * nrem + n],
            device_id=peer, device_id_type=MESH)

    order = [(a, n) for a in range(len(dst_refs)) for n in range(nrem)]

    def start():
        for a, n in order:
            copy(a, n, False).start()

    def wait():
        for a, n in order:
            copy(a, n, True).wait_recv()
        for a, n in order:
            copy(a, n, False).wait_send()

    return start, wait


class _Ride:
    def __init__(self, srcs, dsts, plan, dst_inits=None):
        self.srcs, self.dsts, self.plan, self.dst_inits, self.out = srcs, dsts, plan, dst_inits, None


def _riding_call(body, rides, *, name, grid, in_specs, out_specs, out_shape, scratch_shapes=(), **kw):
    single = not isinstance(out_specs, (list, tuple))
    out_specs = [out_specs] if single else list(out_specs)
    out_shape = [out_shape] if single else list(out_shape)
    n_in, n_out, n_scr = len(in_specs), len(out_specs), len(scratch_shapes)
    xin, xdsts, sems, aliases, layout = [], [], [], {}, []
    for ride in rides:
        srcs = ride.srcs() if callable(ride.srcs) else ride.srcs
        inits = ride.dst_inits() if callable(ride.dst_inits) else ride.dst_inits
        na, nrem = len(ride.dsts), len(ride.plan(0, 0, 0))
        src_at = len(xin) if srcs is not None else None
        xin += list(srcs) if srcs is not None else []
        if inits is not None:
            aliases.update({n_in + len(xin) + a: n_out + len(xdsts) + a for a in range(na)})
            xin += list(inits)
        layout.append((src_at, len(xdsts), na))
        xdsts += list(ride.dsts)
        sems += [pltpu.SemaphoreType.DMA((na * nrem,)), pltpu.SemaphoreType.DMA((na * nrem,))]

    def wrapped(*refs):
        ins, xins = refs[:n_in], refs[n_in:n_in + len(xin)]
        outs = refs[n_in + len(xin):n_in + len(xin) + n_out]
        xouts = refs[n_in + len(xin) + n_out:n_in + len(xin) + n_out + len(xdsts)]
        scr = refs[n_in + len(xin) + n_out + len(xdsts):]
        rounds = []
        for k, (ride, (src_at, dst_at, na)) in enumerate(zip(rides, layout)):
            dsts = xouts[dst_at:dst_at + na]
            srcs = dsts if src_at is None else xins[src_at:src_at + na]
            rounds.append(_copies(srcs, dsts, scr[n_scr + 2 * k], scr[n_scr + 2 * k + 1], ride.plan))
        ids = [pl.program_id(a) for a in range(len(grid))]
        first = functools.reduce(jnp.logical_and, [i == 0 for i in ids])
        last = functools.reduce(jnp.logical_and, [i == g - 1 for i, g in zip(ids, grid)])

        @pl.when(first)
        def _():
            for start, _ in rounds:
                start()

        body(*ins, *outs, *scr[:n_scr])

        @pl.when(last)
        def _():
            for _, wait in rounds:
                wait()

    hbm = pl.BlockSpec(memory_space=pl.ANY)

    def run(*args):
        res = pl.pallas_call(
            wrapped, name=name, grid=grid, in_specs=list(in_specs) + [hbm] * len(xin),
            out_specs=out_specs + [hbm] * len(xdsts), out_shape=out_shape + xdsts,
            scratch_shapes=list(scratch_shapes) + sems, input_output_aliases=aliases,
            compiler_params=_params(("arbitrary",) * len(grid)),
        )(*args, *xin)
        for ride, (_, dst_at, na) in zip(rides, layout):
            ride.out = list(res[n_out + dst_at:n_out + dst_at + na])
        return res[0] if single else list(res[:n_out])

    return run


def _pick(n, target, quantum=128):
    t = (min(target, n) // quantum) * quantum
    while t >= quantum:
        if n % t == 0:
            return t
        t -= quantum
    return n


def _mm(a, b, *, name, ta=False, tb=False, b3=False, out3=0, out_dtype=F32, add=None, tm=1024, tn=1536, tk=1536):
    if ta:
        K, M = a.shape
    else:
        M, K = a.shape
    if b3 and tb:
        Nn, K2, tk = b.shape[1], b.shape[0] * b.shape[2], b.shape[2]
    elif b3:
        K2, Nn, tn = b.shape[1], b.shape[0] * b.shape[2], b.shape[2]
    elif tb:
        Nn, K2 = b.shape
    else:
        K2, Nn = b.shape
    assert K == K2, (a.shape, b.shape)
    if out3:
        tn = Nn // out3
    tm, tn, tk = _pick(M, tm), _pick(Nn, tn), _pick(K, tk)
    if not (b3 and tb) and K <= MM_WHOLE_K:
        tk = K
        fits = lambda: 4 * tk * (tm + tn) + 8 * tm * tn * (2 if add is not None else 1) <= MM_TILE_BYTES
        while not fits():
            if (tm >= tn or b3 or out3) and tm > 256:
                tm = _pick(M, tm - 128)
            elif not (b3 or out3) and tn > 256:
                tn = _pick(Nn, tn - 128)
            else:
                break
    nk = K // tk
    j_outer = K * Nn + (Nn // tn) * M * K < M * K + (M // tm) * K * Nn
    dn = (((0 if ta else 1,), (1 if tb else 0,)), ((), ()))

    def body(*refs):
        refs = list(refs)
        a_ref, b_ref = refs[:2]
        add_ref = refs[2] if add is not None else None
        o_ref = refs[3] if add is not None else refs[2]
        part = lax.dot_general(a_ref[...].astype(BF16), b_ref[...].astype(BF16), dn, preferred_element_type=F32)

        def finish(r):
            if add is not None:
                r = r + add_ref[...]
            o_ref[...] = r.astype(out_dtype)

        if nk == 1:
            finish(part)
            return
        acc = refs[-1]
        k = pl.program_id(2)

        @pl.when(k == 0)
        def _():
            acc[...] = part

        @pl.when(k > 0)
        def _():
            acc[...] += part

        @pl.when(k == nk - 1)
        def _():
            finish(acc[...])

    def spec(shape, index):
        return pl.BlockSpec(shape, (lambda j, i, k: index(i, j, k)) if j_outer else index)

    a_spec = spec((tk, tm), lambda i, j, k: (k, i)) if ta else spec((tm, tk), lambda i, j, k: (i, k))
    if b3 and tb:
        b_spec = spec((None, tn, tk), lambda i, j, k: (k, j, 0))
    elif b3:
        b_spec = spec((None, tk, tn), lambda i, j, k: (j, k, 0))
    elif tb:
        b_spec = spec((tn, tk), lambda i, j, k: (j, k))
    else:
        b_spec = spec((tk, tn), lambda i, j, k: (k, j))
    if out3:
        o_spec = spec((None, tm, tn), lambda i, j, k: (j, i, 0))
    else:
        o_spec = spec((tm, tn), lambda i, j, k: (i, j))
    ins, specs = [a, b], [a_spec, b_spec]
    if add is not None:
        ins.append(add)
        specs.append(o_spec)
    grid = (Nn // tn, M // tm, nk) if j_outer else (M // tm, Nn // tn, nk)
    return _pcall(
        body, name=name, grid=grid, in_specs=specs, out_specs=o_spec,
        out_shape=jax.ShapeDtypeStruct((out3, M, tn) if out3 else (M, Nn), out_dtype),
        scratch_shapes=[pltpu.VMEM((tm, tn), F32)] if nk > 1 else [],
        compiler_params=_params(("parallel", "parallel", "arbitrary")),
    )(*ins)


def _mm_multi(a_list, b_list, *, name, add=None, out_dtype=F32, tm=512):
    M = a_list[0].shape[0]
    tm = _pick(M, tm)
    ns = len(a_list)
    b_arrs, b_specs = [], []
    for b in b_list:
        arr, shp, idx = b if isinstance(b, tuple) else (b, b.shape, (0, 0))
        b_arrs.append(arr)
        b_specs.append(pl.BlockSpec(shp, lambda i, idx=idx: idx))
    Nn = b_specs[0].block_shape[1]
    dn = (((1,), (0,)), ((), ()))

    def body(*refs):
        a_refs, b_refs = refs[:ns], refs[ns:2 * ns]
        acc = None
        for a_ref, b_ref in zip(a_refs, b_refs):
            part = lax.dot_general(a_ref[...].astype(BF16), b_ref[...], dn, preferred_element_type=F32)
            acc = part if acc is None else acc + part
        if add is not None:
            acc = acc + refs[2 * ns][...]
        refs[-1][...] = acc.astype(out_dtype)

    o_spec = pl.BlockSpec((tm, Nn), lambda i: (i, 0))
    specs = [pl.BlockSpec((tm, a.shape[1]), lambda i: (i, 0)) for a in a_list] + b_specs
    ins = list(a_list) + b_arrs
    if add is not None:
        specs.append(o_spec)
        ins.append(add)
    return _pcall(body, name=name, grid=(M // tm,), in_specs=specs, out_specs=o_spec,
                  out_shape=jax.ShapeDtypeStruct((M, Nn), out_dtype), compiler_params=_params(("parallel",)))(*ins)


def _lane(shape):
    return lax.broadcasted_iota(jnp.int32, shape, len(shape) - 1)


def _rot_half(v):
    w = v.shape[-1]
    first = (_lane(v.shape) % HEAD_DIM) < (HEAD_DIM // 2)
    return jnp.where(first, pltpu.roll(v, w - HEAD_DIM // 2, v.ndim - 1), pltpu.roll(v, HEAD_DIM // 2, v.ndim - 1))


def _widen(t, w):
    return t if w == t.shape[-1] else jnp.concatenate([t] * (w // t.shape[-1]), axis=-1)


def _unrope(v, cos, sins):
    w = v.shape[-1]
    return v * _widen(cos, w) - _rot_half(v) * _widen(sins, w)


def _rope_tables(positions):
    half = HEAD_DIM // 2
    inv = ROPE_THETA ** (-jnp.arange(half, dtype=F32) / half)
    ang = positions.astype(F32)[..., None] * inv
    cos, sin = jnp.cos(ang), jnp.sin(ang)
    cosf = jnp.concatenate([cos, cos, cos, cos], axis=-1)
    sins = jnp.concatenate([-sin, sin, -sin, sin], axis=-1)
    n = positions.shape[0] * positions.shape[1]
    return cosf.reshape(n, PAIR_W), sins.reshape(n, PAIR_W)


def _inproj(x2, scale, shift, w, cosf, sins, flags, *, T, name):
    N, D = x2.shape
    tm, tn = _pick(T, 512), VAR_W
    tpe = T // tm

    def body(x_ref, sc_ref, sh_ref, w_ref, c_ref, s_ref, f_ref, *outs):
        o_refs, u_ref = outs[:N_VAR], outs[N_VAR]
        j = pl.program_id(1)

        @pl.when(j == 0)
        def _():
            u_ref[...] = (x_ref[...] * (1.0 + sc_ref[0]) + sh_ref[0]).astype(BF16)

        acc = lax.dot_general(u_ref[...], w_ref[...], (((1,), (1,)), ((), ())), preferred_element_type=F32)
        fl = f_ref[...]
        ce = 1.0 + (_widen(c_ref[...], tn) - 1.0) * fl
        se = _widen(s_ref[...], tn) * fl
        res = acc * ce + _rot_half(acc) * se
        for v in range(N_VAR):
            @pl.when(j == v)
            def _(v=v):
                _to_view(res, o_refs[v], outs[N_VAR + 1], VAR_DIL[v])

    ex = pl.BlockSpec((1, 1, D), lambda i, j: (i // tpe, 0, 0))
    tab = pl.BlockSpec((tm, PAIR_W), lambda i, j: (i, 0))
    keep = lambda w_: pl.BlockSpec((tm, w_), lambda i, j: (i, 0))
    vspec = lambda r: pl.BlockSpec((None, tm // r, r * tn), lambda i, j: (i // tpe, i % tpe, 0))
    vshape = lambda r: jax.ShapeDtypeStruct((N // T, T // r, r * tn), BF16)
    return _pcall(
        body, name=name, grid=(N // tm, N_VAR),
        in_specs=[keep(D), ex, ex, pl.BlockSpec((tn, D), lambda i, j: (j, 0)), tab, tab,
                  pl.BlockSpec((1, tn), lambda i, j: (0, j))],
        out_specs=[vspec(r) for r in VAR_DIL] + [keep(D)],
        out_shape=[vshape(r) for r in VAR_DIL] + [jax.ShapeDtypeStruct((N, D), BF16)],
        scratch_shapes=[pltpu.VMEM((tn // LANES, tm, LANES), F32)],
        compiler_params=_params(("parallel", "arbitrary")),
    )(x2, scale, shift, w, cosf, sins, flags)


class _Geom:
    def __init__(self, g):
        if g is None:
            self.r, self.nq, self.n_back, self.sink = 1, A_Q_HEADS, A_WINDOW - 1, True
            self.qw, self.kw = QA_W, KA_W
            self.qidx = lambda j: 0
            self.kidx = lambda j: QA_W // KA_W
            self.vidx = lambda j: QA_W // KA_W + 1
        else:
            window, r = B_PATTERNS[g]
            self.r, self.nq, self.n_back, self.sink = r, B_GROUP_HEADS, window // r, False
            self.qw, self.kw = GB_W, GB_W
            self.qidx = lambda j: 3 * j
            self.kidx = lambda j: 3 * j + 1
            self.vidx = lambda j: 3 * j + 2
        self.ntile = self.qw // PAIR_W


def _stack_heads(t, scale=None):
    first = _lane(t.shape) < HEAD_DIM
    z = jnp.zeros_like(t)
    if scale is not None:
        t = t * jnp.asarray(scale, t.dtype)
    return jnp.concatenate([jnp.where(first, t, z), jnp.where(first, z, t)], axis=0)


def _lse_col(t):
    return jnp.concatenate([t[:, 0:1], t[:, HEAD_DIM:HEAD_DIM + 1]], axis=0)


def _lse_rows(t, width):
    first = _lane(t.shape) < HEAD_DIM
    other = pltpu.roll(t, HEAD_DIM, 1)
    full = jnp.concatenate([jnp.where(first, t, other), jnp.where(first, other, t)], axis=0)
    return _widen(full, width)


def _unstack_heads(v2):
    return jnp.where(_lane((BLOCK, PAIR_W)) < HEAD_DIM, v2[:BLOCK], v2[BLOCK:])


def _dup_head(t, kh):
    tf = t.astype(F32)
    keep = (_lane(t.shape) < HEAD_DIM) if kh == 0 else (_lane(t.shape) >= HEAD_DIM)
    return jnp.where(keep, tf, pltpu.roll(tf, HEAD_DIM, 1)).astype(t.dtype)


def _fold_heads(t):
    return t + pltpu.roll(t, HEAD_DIM, 1)


def _band_mask(rows, i, n_back, single):
    nkeys = BLOCK if single else 2 * BLOCK
    qi = jnp.bitwise_and(lax.broadcasted_iota(jnp.int32, (rows, nkeys), 0), BLOCK - 1)
    ki = lax.broadcasted_iota(jnp.int32, (rows, nkeys), 1)
    if single:
        return qi >= ki
    dist = qi + BLOCK - ki
    return jnp.logical_and(jnp.logical_and(dist >= 0, dist <= n_back), jnp.logical_or(ki >= BLOCK, i > 0))


def _per_block(col, scalars, fn):
    return jnp.concatenate([fn(col[b * BLOCK:(b + 1) * BLOCK], sc) for b, sc in enumerate(scalars)], axis=0)


def _sink_slot(rows):
    qi = jnp.bitwise_and(lax.broadcasted_iota(jnp.int32, (rows, 2 * BLOCK), 0), BLOCK - 1)
    return qi == lax.broadcasted_iota(jnp.int32, (rows, 2 * BLOCK), 1)


def _sink_scores(rows, sinks):
    blk = lax.broadcasted_iota(jnp.int32, (rows, 2 * BLOCK), 0) // BLOCK
    out = jnp.full((rows, 2 * BLOCK), sinks[-1], F32)
    for b in range(len(sinks) - 2, -1, -1):
        out = jnp.where(blk == b, sinks[b], out)
    return out


def _softmax_parts(s, valid, sinks):
    s = jnp.where(valid, s, NEG_INF)
    if sinks is not None:
        slot = _sink_slot(s.shape[0])
        s = jnp.where(slot, _sink_scores(s.shape[0], sinks), s)
    m = jnp.max(s, axis=1, keepdims=True)
    p = jnp.exp(s - m)
    den = jnp.sum(p, axis=1, keepdims=True)
    if sinks is not None:
        p = jnp.where(slot, 0.0, p)
    return p, m, den


_NT = (((1,), (1,)), ((), ()))
_TN = (((0,), (0,)), ((), ()))


def _rows2(prev_ref, cur_ref, cs, single=False):
    if single:
        return cur_ref[0, :, cs]
    return jnp.concatenate([prev_ref[0, :, cs], cur_ref[0, :, cs]], axis=0)


def _sink_scalars(sink_ref, first, nblocks):
    return [sink_ref[first + b] for b in range(nblocks)]


def _tile(t):
    return slice(t * PAIR_W, (t + 1) * PAIR_W)


def _attn_fwd(qkv, sinks, g, *, NB, T, name):
    geo = _Geom(g)
    r, qw, kw, ntile = geo.r, geo.qw, geo.kw, geo.ntile
    tsub = T // r
    nblk = tsub // BLOCK
    qkv3 = qkv.reshape(NB, tsub, r * VAR_W)
    out_dtype = BF16 if g is None else F32
    tiles_per_kv = ntile // A_KV_HEADS

    single = nblk == 1

    def body(q_ref, kp_ref, kc_ref, vp_ref, vc_ref, sink_ref, o_ref, l_ref):
        i = pl.program_id(2)
        if geo.sink:
            kall, vall = _rows2(kp_ref, kc_ref, _tile(0)), _rows2(vp_ref, vc_ref, _tile(0))
            kdup = [_dup_head(kall, kh) for kh in range(A_KV_HEADS)]
            vdup = [_dup_head(vall, kh) for kh in range(A_KV_HEADS)]
            tiles = [[t] for t in range(ntile)]
            q2s = [_stack_heads(q_ref[0, :, _tile(t)], SCALE) for t in range(ntile)]
            kks = [kdup[t // tiles_per_kv] for t in range(ntile)]
            vvs = [vdup[t // tiles_per_kv] for t in range(ntile)]
            sinkcols = [_sink_scalars(sink_ref, 2 * t, 2) for t in range(ntile)]
        else:
            tiles = [[t] for t in range(ntile)]
            q2s = [_stack_heads(q_ref[0, :, _tile(t)], SCALE) for t in range(ntile)]
            kks = [_rows2(kp_ref, kc_ref, _tile(t), single) for t in range(ntile)]
            vvs = [_rows2(vp_ref, vc_ref, _tile(t), single) for t in range(ntile)]
            sinkcols = [None] * ntile
        valid = _band_mask(q2s[0].shape[0], i, geo.n_back, single)
        ss = [lax.dot_general(q2, kk, _NT, preferred_element_type=F32) for q2, kk in zip(q2s, kks)]
        parts = [_softmax_parts(s, valid, sc) for s, sc in zip(ss, sinkcols)]
        o2s = [jnp.dot(p.astype(BF16), vv, preferred_element_type=F32) / den for (p, m, den), vv in zip(parts, vvs)]
        for ts, o2, (p, m, den) in zip(tiles, o2s, parts):
            lse2 = jnp.broadcast_to(m + jnp.log(den), (o2.shape[0], PAIR_W))
            for n, t in enumerate(ts):
                rows = slice(2 * BLOCK * n, 2 * BLOCK * (n + 1))
                o_ref[0, :, _tile(t)] = _unstack_heads(o2[rows]).astype(out_dtype)
                l_ref[0, :, _tile(t)] = _unstack_heads(lse2[rows])

    prev = lambda i: jnp.maximum(i - 1, 0)
    in_specs = [
        pl.BlockSpec((1, BLOCK, qw), lambda b, j, i: (b, i, geo.qidx(j))),
        pl.BlockSpec((1, BLOCK, kw), lambda b, j, i: (b, prev(i), geo.kidx(j))),
        pl.BlockSpec((1, BLOCK, kw), lambda b, j, i: (b, i, geo.kidx(j))),
        pl.BlockSpec((1, BLOCK, kw), lambda b, j, i: (b, prev(i), geo.vidx(j))),
        pl.BlockSpec((1, BLOCK, kw), lambda b, j, i: (b, i, geo.vidx(j))),
        pl.BlockSpec(memory_space=pltpu.SMEM),
    ]
    o_spec = pl.BlockSpec((1, BLOCK, qw), lambda b, j, i: (b, i, j))
    shape = (NB, tsub, r * qw)
    o, lse = _pcall(
        body, name=name, grid=(NB, r, nblk), in_specs=in_specs, out_specs=[o_spec, o_spec],
        out_shape=[jax.ShapeDtypeStruct(shape, out_dtype), jax.ShapeDtypeStruct(shape, F32)],
        compiler_params=_params(("parallel", "parallel", "arbitrary")),
    )(qkv3, qkv3, qkv3, qkv3, qkv3, sinks)
    return o, lse


def _attn_bwd(qkv, do, lse, dlse, cosf, sins, sinks, g, *, NB, T, name):
    geo = _Geom(g)
    r, qw, kw, ntile = geo.r, geo.qw, geo.kw, geo.ntile
    tsub = T // r
    nblk = tsub // BLOCK
    view = lambda a, w: a.reshape(NB, tsub, r * w)
    has_dlse = dlse is not None
    tiles_per_kv = ntile // A_KV_HEADS

    single = nblk == 1
    krows = BLOCK if single else 2 * BLOCK
    nsteps = 1 if single else nblk + 1

    def grads(q2s, kks, vvs, do2s, i, lserows, sinkcols, dlrows):
        nrow = q2s[0].shape[0]
        ki = lax.broadcasted_iota(jnp.int32, (krows, nrow), 0)
        qi = jnp.bitwise_and(lax.broadcasted_iota(jnp.int32, (krows, nrow), 1), BLOCK - 1)
        if single:
            valid = qi >= ki
        else:
            dist = qi + BLOCK - ki
            valid = jnp.logical_and(jnp.logical_and(dist >= 0, dist <= geo.n_back), jnp.logical_or(ki >= BLOCK, i > 0))
        sts = [lax.dot_general(kk, q2, _NT, preferred_element_type=F32) for q2, kk in zip(q2s, kks)]
        dpts = [lax.dot_general(vv, do2, _NT, preferred_element_type=F32) for do2, vv in zip(do2s, vvs)]
        pts, dsts, sks = [], [], []
        for st, dpt, ls, sc, dl in zip(sts, dpts, lserows, sinkcols, dlrows):
            sv = jnp.where(valid, st, NEG_INF)
            if sc is not None:
                slot = ki == qi
                blk = lax.broadcasted_iota(jnp.int32, (krows, nrow), 1) // BLOCK
                sink = jnp.full((krows, nrow), sc[-1], F32)
                for b in range(len(sc) - 2, -1, -1):
                    sink = jnp.where(blk == b, sc[b], sink)
                sv = jnp.where(slot, sink, sv)
                dpt = jnp.where(slot, 0.0, dpt)
            pt = jnp.exp(sv - ls)
            delta = jnp.sum(pt * dpt, axis=0, keepdims=True)
            if dl is not None:
                delta = delta - dl
            dst = pt * (dpt - delta)
            if sc is not None:
                cols = lambda a, b: a[:, b * BLOCK:(b + 1) * BLOCK]
                sks.append([jnp.sum(jnp.where(cols(slot, b), cols(dst, b), 0.0)) for b in range(len(sc))])
                dst, pt = jnp.where(slot, 0.0, dst), jnp.where(slot, 0.0, pt)
            else:
                sks.append(None)
            pts.append(pt.astype(BF16))
            dsts.append(dst.astype(BF16))
        dq2s = [lax.dot_general(dst, kk, _TN, preferred_element_type=F32) * SCALE for dst, kk in zip(dsts, kks)]
        dkks = [jnp.dot(dst, q2, preferred_element_type=F32) for dst, q2 in zip(dsts, q2s)]
        dvvs = [jnp.dot(pt, do2, preferred_element_type=F32) for pt, do2 in zip(pts, do2s)]
        return dq2s, dkks, dvvs, sks

    def stat_row(t):
        tt = t.T
        return jnp.concatenate([tt[0:1, :], tt[HEAD_DIM:HEAD_DIM + 1, :]], axis=1)

    def body(*refs):
        it = iter(refs)
        q_ref, kp_ref, kc_ref, vp_ref, vc_ref, do_ref, l_ref = (next(it) for _ in range(7))
        dl_ref = next(it) if has_dlse else None
        c_ref, s_ref, sink_ref, o_ref, ds_ref, dq_s, dk_s, dv_s, car_q, car_k, car_v = (next(it) for _ in range(11))
        b, j, i = pl.program_id(0), pl.program_id(1), pl.program_id(2)

        @pl.when(jnp.logical_and(b == 0, jnp.logical_and(j == 0, i == 0)))
        def _():
            ds_ref[...] = jnp.zeros_like(ds_ref)

        def compute():
            if geo.sink:
                kall, vall = _rows2(kp_ref, kc_ref, _tile(0)), _rows2(vp_ref, vc_ref, _tile(0))
                nb = 2 * tiles_per_kv
                tiles = [[kh * tiles_per_kv + t for t in range(tiles_per_kv)] for kh in range(A_KV_HEADS)]
                cat = lambda f, ts: jnp.concatenate([f(t) for t in ts], axis=0)
                dq2s, dkks, dvvs, sks = grads(
                    [cat(lambda t: _stack_heads(q_ref[0, :, _tile(t)], SCALE), ts) for ts in tiles],
                    [_dup_head(kall, kh) for kh in range(A_KV_HEADS)],
                    [_dup_head(vall, kh) for kh in range(A_KV_HEADS)],
                    [cat(lambda t: _stack_heads(do_ref[0, :, _tile(t)]), ts) for ts in tiles], i,
                    [jnp.concatenate([stat_row(l_ref[0, :, _tile(t)]) for t in ts], axis=1) for ts in tiles],
                    [_sink_scalars(sink_ref, kh * nb, nb) for kh in range(A_KV_HEADS)], [None] * A_KV_HEADS)
                lane1 = _lane((1, PAIR_W))
                dsink = jnp.zeros((1, PAIR_W), F32)
                for kh, (ts, dq2, sk) in enumerate(zip(tiles, dq2s, sks)):
                    for n, t in enumerate(ts):
                        dq_s[:, _tile(t)] = _unstack_heads(dq2[2 * BLOCK * n:2 * BLOCK * (n + 1)])
                    for bb in range(nb):
                        dsink = dsink + jnp.where(lane1 == kh * nb + bb, sk[bb], 0.0)
                second = _lane((krows, PAIR_W)) >= HEAD_DIM
                dk_s[...] = jnp.where(second, _fold_heads(dkks[1]), _fold_heads(dkks[0]))
                dv_s[...] = jnp.where(second, _fold_heads(dvvs[1]), _fold_heads(dvvs[0]))
                ds_ref[0:1, :] += dsink
            else:
                dq2s, dkks, dvvs, _ = grads(
                    [_stack_heads(q_ref[0, :, _tile(t)], SCALE) for t in range(ntile)],
                    [_rows2(kp_ref, kc_ref, _tile(t), single) for t in range(ntile)],
                    [_rows2(vp_ref, vc_ref, _tile(t), single) for t in range(ntile)],
                    [_stack_heads(do_ref[0, :, _tile(t)]) for t in range(ntile)], i,
                    [stat_row(l_ref[0, :, _tile(t)]) for t in range(ntile)], [None] * ntile,
                    [stat_row(dl_ref[0, :, _tile(t)]) for t in range(ntile)])
                for t in range(ntile):
                    dq_s[:, _tile(t)] = _unstack_heads(dq2s[t])
                    dk_s[0:krows, _tile(t)] = dkks[t]
                    dv_s[0:krows, _tile(t)] = dvvs[t]

        def emit(dq, dk, dv):
            cos, sn = c_ref[0], s_ref[0]
            o_ref[0, :, 0:qw] = _unrope(dq, cos, sn).astype(BF16)
            o_ref[0, :, qw:qw + kw] = _unrope(dk, cos, sn).astype(BF16)
            o_ref[0, :, qw + kw:qw + 2 * kw] = dv.astype(BF16)
            if qw + 2 * kw < VAR_W:
                o_ref[0, :, qw + 2 * kw:VAR_W] = jnp.zeros((BLOCK, VAR_W - qw - 2 * kw), BF16)

        if single:
            compute()
            emit(dq_s[...], dk_s[0:BLOCK, :], dv_s[0:BLOCK, :])
            return

        @pl.when(i == 0)
        def _():
            car_q[...] = jnp.zeros_like(car_q)
            car_k[...] = jnp.zeros_like(car_k)
            car_v[...] = jnp.zeros_like(car_v)

        @pl.when(i == nblk)
        def _():
            dk_s[...] = jnp.zeros_like(dk_s)
            dv_s[...] = jnp.zeros_like(dv_s)

        pl.when(i < nblk)(compute)
        emit(car_q[...], car_k[...] + dk_s[0:BLOCK, :], car_v[...] + dv_s[0:BLOCK, :])
        car_q[...] = dq_s[...]
        car_k[...] = dk_s[BLOCK:2 * BLOCK, :]
        car_v[...] = dv_s[BLOCK:2 * BLOCK, :]

    cur = lambda i: jnp.minimum(i, nblk - 1)
    prv = lambda i: jnp.maximum(jnp.minimum(i, nblk - 1) - 1, 0)
    outb = lambda i: jnp.maximum(i - 1, 0)
    qrow = pl.BlockSpec((1, BLOCK, qw), lambda b, j, i: (b, cur(i), j))
    in_specs = [
        pl.BlockSpec((1, BLOCK, qw), lambda b, j, i: (b, cur(i), geo.qidx(j))),
        pl.BlockSpec((1, BLOCK, kw), lambda b, j, i: (b, prv(i), geo.kidx(j))),
        pl.BlockSpec((1, BLOCK, kw), lambda b, j, i: (b, cur(i), geo.kidx(j))),
        pl.BlockSpec((1, BLOCK, kw), lambda b, j, i: (b, prv(i), geo.vidx(j))),
        pl.BlockSpec((1, BLOCK, kw), lambda b, j, i: (b, cur(i), geo.vidx(j))),
        qrow, qrow,
    ]
    ins = [view(qkv, VAR_W)] * 5 + [view(do, qw), view(lse, qw)]
    if has_dlse:
        in_specs.append(qrow)
        ins.append(view(dlse, qw))
    in_specs += [
        pl.BlockSpec((1, BLOCK, PAIR_W), lambda b, j, i: (b, outb(i), j)),
        pl.BlockSpec((1, BLOCK, PAIR_W), lambda b, j, i: (b, outb(i), j)),
        pl.BlockSpec(memory_space=pltpu.SMEM),
    ]
    ins += [view(cosf, PAIR_W), view(sins, PAIR_W), sinks]
    scratch = [pltpu.VMEM((BLOCK, qw), F32), pltpu.VMEM((2 * BLOCK, kw), F32), pltpu.VMEM((2 * BLOCK, kw), F32),
               pltpu.VMEM((BLOCK, qw), F32), pltpu.VMEM((BLOCK, kw), F32), pltpu.VMEM((BLOCK, kw), F32)]
    dqkv, dsink = _pcall(
        body, name=name, grid=(NB, r, nsteps), in_specs=in_specs,
        out_specs=[pl.BlockSpec((1, BLOCK, VAR_W), lambda b, j, i: (b, outb(i), j)),
                   pl.BlockSpec((8, PAIR_W), lambda b, j, i: (0, 0))],
        out_shape=[jax.ShapeDtypeStruct((NB, tsub, r * VAR_W), BF16), jax.ShapeDtypeStruct((8, PAIR_W), F32)],
        scratch_shapes=scratch, compiler_params=_params(("arbitrary", "arbitrary", "arbitrary")),
    )(*ins)
    return dqkv.reshape(NB * T, VAR_W), dsink


class _Rows:
    def __init__(self, N, T, tm):
        self.N, self.tm, self.tpe, self.grid = N, tm, T // tm, (N // tm,)

    def row(self, w, col=0):
        return pl.BlockSpec((self.tm, w), lambda i: (i, col))

    def ex(self, w):
        return pl.BlockSpec((1, 1, w), lambda i: (i // self.tpe, 0, 0))

    def const(self, shape):
        return pl.BlockSpec(shape, lambda i: tuple(0 for _ in shape))

    def view(self, w, r):
        return pl.BlockSpec((None, self.tm // r, r * w), lambda i: (i // self.tpe, i % self.tpe, 0))

    def first_of_example(self):
        return pl.program_id(0) % self.tpe == 0


def _acc(ref, first, val):
    @pl.when(first)
    def _():
        ref[0] = val

    @pl.when(jnp.logical_not(first))
    def _():
        ref[0] += val


def _colsum(v):
    return jnp.sum(v, axis=0, keepdims=True)


def _ln_stats(r):
    mu = jnp.mean(r, axis=-1, keepdims=True)
    xc = r - mu
    var = jnp.mean(xc * xc, axis=-1, keepdims=True)
    rstd = lax.rsqrt(var + LN_EPS)
    return xc * rstd, rstd


def _ln_bwd(dy, xhat, rstd, gain):
    dxh = dy * gain
    return rstd * (dxh - jnp.mean(dxh, axis=-1, keepdims=True) - xhat * jnp.mean(dxh * xhat, axis=-1, keepdims=True))


def _from_view(ref, scr, r):
    if r == 1:
        return ref[...]
    rows, w = ref.shape[0], ref.shape[1] // r
    for j in range(r):
        for c in range(w // LANES):
            scr.at[c][pl.ds(j, rows, stride=r), :] = ref[:, j * w + c * LANES:j * w + (c + 1) * LANES]
    return jnp.concatenate([scr[c] for c in range(w // LANES)], axis=1)


def _to_view(val, ref, scr, r):
    if r == 1:
        ref[...] = val.astype(ref.dtype)
        return
    rows, w = ref.shape[0], ref.shape[1] // r
    for c in range(w // LANES):
        scr[c] = val[:, c * LANES:(c + 1) * LANES]
    for j in range(r):
        for c in range(w // LANES):
            ref[:, j * w + c * LANES:j * w + (c + 1) * LANES] = scr.at[c][pl.ds(j, rows, stride=r), :].astype(ref.dtype)


def _silu_parts(v):
    s = jax.nn.sigmoid(v)
    return v * s, s * (1.0 + v * (1.0 - s))


def _local_step(x, mod, positions, w_in, rest_weights, sinks, ln1_g, ln1_b, ln2_g, ln2_b, target, hook=None):
    hook = hook or (lambda event, **data: None)
    NB, T, D = x.shape
    N = NB * T
    x2 = x.reshape(N, D)
    tgt2 = target.reshape(N, D)
    shift_m, scale_m, gate_m, shift_f, scale_f, gate_f = [mod[:, None, k * D:(k + 1) * D] for k in range(6)]
    cosf, sins = _rope_tables(positions)
    col = jnp.arange(QKV_P)
    vcol = col % VAR_W
    flags = jnp.where(col < VAR_W, vcol < QA_W + KA_W, vcol < 2 * GB_W).astype(F32)[None]
    R = _Rows(N, T, _pick(T, 256))
    sds = jax.ShapeDtypeStruct
    exsum = lambda w=D: sds((NB, 1, w), F32)
    ngrp = len(B_PATTERNS)

    *qkv, u = _inproj(x2, scale_m, shift_m, w_in, cosf, sins, flags, T=T, name="inproj_qkv")
    gates = _mm(u, w_in[QKV_P:], tb=True, out_dtype=BF16, name="inproj_gates")
    oa, la = _attn_fwd(qkv[0], sinks, None, NB=NB, T=T, name="attn_a_fwd")
    oa = oa.reshape(N, QA_W)
    ob_parts = [_attn_fwd(qkv[1 + g], sinks, g, NB=NB, T=T, name=f"attn_b{g}_fwd") for g in range(ngrp)]
    (o1, l1), (o2, l2), (o3, l3) = ob_parts
    w_a, w_b, w_o, w_gu, w_d = rest_weights()
    F = w_d.shape[0]
    dil = [r_ for _, r_ in B_PATTERNS]
    views = [R.view(GB_W, r_) for r_ in dil]
    tokbuf = pltpu.VMEM((GB_W // LANES, R.tm, LANES), F32)

    def merge_fwd(o1r, o2r, o3r, l1r, l2r, l3r, ob_ref, *bufs):
        os_ = [_from_view(ref, bufs[n], dil[n]) for n, ref in enumerate((o1r, o2r, o3r))]
        la, lb, lc = [_from_view(ref, bufs[3 + n], dil[n]) for n, ref in enumerate((l1r, l2r, l3r))]
        mx = jnp.maximum(jnp.maximum(la, lb), lc)
        ea, eb, ec = jnp.exp(la - mx), jnp.exp(lb - mx), jnp.exp(lc - mx)
        ob_ref[...] = ((ea * os_[0] + eb * os_[1] + ec * os_[2]) / (ea + eb + ec)).astype(BF16)

    ob = _pcall(merge_fwd, name="merge_fwd", grid=R.grid, in_specs=views + views, out_specs=R.row(GB_W),
                out_shape=sds((N, GB_W), BF16), scratch_shapes=[tokbuf] * 6,
                compiler_params=_params(("parallel",)))(o1, o2, o3, l1, l2, l3)

    ya = _mm(oa, w_a, out_dtype=BF16, name="branch_a")
    yb = _mm(ob, w_b, b3=True, out_dtype=BF16, name="branch_b")
    f32 = lambda ref: ref[...].astype(F32)

    def gate_fwd(ya_r, yb_r, ga_r, gb_r, mg_ref):
        mg_ref[...] = (jax.nn.sigmoid(f32(ga_r)) * f32(ya_r) + jax.nn.sigmoid(f32(gb_r)) * f32(yb_r)).astype(BF16)

    merged = _pcall(gate_fwd, name="gate_fwd", grid=R.grid, in_specs=[R.row(D), R.row(D), R.row(D, 0), R.row(D, 1)],
                    out_specs=R.row(D), out_shape=sds((N, D), BF16),
                    compiler_params=_params(("parallel",)))(ya, yb, gates, gates)
    y = _mm(merged, w_o, name="out_proj")

    def norm1_fwd(x_r, y_r, gm_r, g_r, b_r, sf_r, hf_r, r1_ref, u2_ref):
        r1 = ALPHA * x_r[...] + (1.0 + gm_r[0]) * y_r[...]
        xhat, _ = _ln_stats(r1)
        x1 = xhat * g_r[...] + b_r[...]
        r1_ref[...] = r1
        u2_ref[...] = (x1 * (1.0 + sf_r[0]) + hf_r[0]).astype(BF16)

    r1, u2 = _pcall(
        norm1_fwd, name="norm1_fwd", grid=R.grid,
        in_specs=[R.row(D), R.row(D), R.ex(D), R.const((1, D)), R.const((1, D)), R.ex(D), R.ex(D)],
        out_specs=[R.row(D)] * 2, out_shape=[sds((N, D), F32), sds((N, D), BF16)],
        compiler_params=_params(("parallel",)))(x2, y, gate_m, ln1_g, ln1_b, scale_f, shift_f)

    tnf = w_gu.shape[2]
    nft = w_gu.shape[0] // 2
    tmf = _pick(N, 512)

    def ffn_up(u_r, wg_r, wu_r, hg_ref, hu_ref, a_ref):
        hg = jnp.dot(u_r[...], wg_r[...], preferred_element_type=F32)
        hu = jnp.dot(u_r[...], wu_r[...], preferred_element_type=F32)
        sl, _ = _silu_parts(hg)
        hg_ref[...] = hg.astype(BF16)
        hu_ref[...] = hu.astype(BF16)
        a_ref[...] = (sl * hu).astype(BF16)

    ftile = pl.BlockSpec((tmf, tnf), lambda j, i: (i, j))
    hg, hu, act = _pcall(
        ffn_up, name="ffn_up", grid=(nft, N // tmf),
        in_specs=[pl.BlockSpec((tmf, D), lambda j, i: (i, 0)), pl.BlockSpec((None, D, tnf), lambda j, i: (j, 0, 0)),
                  pl.BlockSpec((None, D, tnf), lambda j, i: (j + nft, 0, 0))],
        out_specs=[ftile] * 3, out_shape=[sds((N, F), BF16)] * 3,
        compiler_params=_params(("arbitrary", "parallel")))(u2, w_gu, w_gu)
    y2 = _mm(act, w_d, name="ffn_down")

    def norm2_loss_bwd(r1_r, g1_r, b1_r, y2_r, t_r, gf_r, g_r, b_r, dy2_ref, dx1_ref, dgf_ref, dg_ref, db_ref, loss_ref):
        first = R.first_of_example()
        y2v = y2_r[...]
        x1 = _ln_stats(r1_r[...])[0] * g1_r[...] + b1_r[...]
        r2 = ALPHA * x1 + (1.0 + gf_r[0]) * y2v
        xhat, rstd = _ln_stats(r2)
        err = xhat * g_r[...] + b_r[...] - t_r[...]
        dx2 = err * (1.0 / D)
        dr2 = _ln_bwd(dx2, xhat, rstd, g_r[...])
        dy2_ref[...] = ((1.0 + gf_r[0]) * dr2).astype(BF16)
        dx1_ref[...] = ALPHA * dr2
        _acc(dgf_ref, first, _colsum(dr2 * y2v))
        _acc(dg_ref, first, _colsum(dx2 * xhat))
        _acc(db_ref, first, _colsum(dx2))
        part = 0.5 * jnp.sum(jnp.mean(err * err, axis=-1, keepdims=True))
        _acc(loss_ref, first, jnp.broadcast_to(part, (1, 128)))

    dy2, dx1p, dgate_f, dg2, db2, loss_p = _pcall(
        norm2_loss_bwd, name="norm2_loss_bwd", grid=R.grid,
        in_specs=[R.row(D), R.const((1, D)), R.const((1, D)), R.row(D), R.row(D), R.ex(D), R.const((1, D)),
                  R.const((1, D))],
        out_specs=[R.row(D), R.row(D), R.ex(D), R.ex(D), R.ex(D), R.ex(128)],
        out_shape=[sds((N, D), BF16), sds((N, D), F32), exsum(), exsum(), exsum(), exsum(128)],
        compiler_params=_params(("arbitrary",)))(r1, ln1_g, ln1_b, y2, tgt2, gate_f, ln2_g, ln2_b)

    g_wd = _mm(act, dy2, ta=True, out_dtype=BF16, name="ffn_down_dw")

    tmd = _pick(N, 256)

    def ffn_down_dx(dy_r, wd_r, hg_r, hu_r, dh_ref):
        for t in range(nft):
            cs = slice(t * tnf, (t + 1) * tnf)
            da = lax.dot_general(dy_r[...], wd_r[cs, :], _NT, preferred_element_type=F32)
            sl, dsl = _silu_parts(hg_r[:, cs].astype(F32))
            dh_ref[:, cs] = (da * hu_r[:, cs].astype(F32) * dsl).astype(BF16)
            dh_ref[:, F + t * tnf:F + (t + 1) * tnf] = (da * sl).astype(BF16)

    rowd = lambda w_: pl.BlockSpec((tmd, w_), lambda i: (i, 0))
    dh = _pcall(
        ffn_down_dx, name="ffn_down_dx", grid=(N // tmd,),
        in_specs=[rowd(D), pl.BlockSpec((F, D), lambda i: (0, 0)), rowd(F), rowd(F)],
        out_specs=rowd(2 * F), out_shape=sds((N, 2 * F), BF16),
        compiler_params=_params(("parallel",)))(dy2, w_d, hg, hu)
    def ffn_up_dx(dh_r, w_r, o_ref):
        acc = None
        for s_ in range(w_gu.shape[0]):
            part = lax.dot_general(dh_r[:, s_ * tnf:(s_ + 1) * tnf], w_r[s_], _NT, preferred_element_type=F32)
            acc = part if acc is None else acc + part
        o_ref[...] = acc

    du2 = _pcall(
        ffn_up_dx, name="ffn_up_dx", grid=(N // tmf,),
        in_specs=[pl.BlockSpec((tmf, 2 * F), lambda i: (i, 0)), pl.BlockSpec(w_gu.shape, lambda i: (0, 0, 0))],
        out_specs=pl.BlockSpec((tmf, D), lambda i: (i, 0)), out_shape=sds((N, D), F32),
        compiler_params=_params(("parallel",)))(dh, w_gu)
    g_wgu = _mm(u2, dh, ta=True, out3=w_gu.shape[0], out_dtype=BF16, name="ffn_up_dw")

    def norm1_bwd(dx1p_r, du2_r, r1_r, y_r, sf_r, gm_r, g_r, b_r,
                  dxp_ref, dy_ref, dsf_ref, dhf_ref, dgm_ref, dg_ref, db_ref):
        first = R.first_of_example()
        du2v = du2_r[...]
        dx1 = dx1p_r[...] + du2v * (1.0 + sf_r[0])
        xhat, rstd = _ln_stats(r1_r[...])
        dr1 = _ln_bwd(dx1, xhat, rstd, g_r[...])
        dxp_ref[...] = ALPHA * dr1
        dy_ref[...] = ((1.0 + gm_r[0]) * dr1).astype(BF16)
        _acc(dsf_ref, first, _colsum(du2v * (xhat * g_r[...] + b_r[...])))
        _acc(dhf_ref, first, _colsum(du2v))
        _acc(dgm_ref, first, _colsum(dr1 * y_r[...]))
        _acc(dg_ref, first, _colsum(dx1 * xhat))
        _acc(db_ref, first, _colsum(dx1))

    dxp, dy, dscale_f, dshift_f, dgate_m, dg1, db1 = _pcall(
        norm1_bwd, name="norm1_bwd", grid=R.grid,
        in_specs=[R.row(D)] * 4 + [R.ex(D), R.ex(D), R.const((1, D)), R.const((1, D))],
        out_specs=[R.row(D), R.row(D)] + [R.ex(D)] * 5,
        out_shape=[sds((N, D), F32), sds((N, D), BF16)] + [exsum()] * 5,
        compiler_params=_params(("arbitrary",)))(dx1p, du2, r1, y, scale_f, gate_m, ln1_g, ln1_b)

    dmerged = _mm(dy, w_o, tb=True, out_dtype=BF16, name="out_proj_dx")
    g_wo = _mm(merged, dy, ta=True, out_dtype=BF16, name="out_proj_dw")

    def gate_bwd(dm_r, ya_r, yb_r, ga_r, gb_r, dya_ref, dyb_ref, dg_ref):
        dm = f32(dm_r)
        sa, sb = jax.nn.sigmoid(f32(ga_r)), jax.nn.sigmoid(f32(gb_r))
        dya_ref[...] = (dm * sa).astype(BF16)
        dyb_ref[...] = (dm * sb).astype(BF16)
        dg_ref[:, :D] = (dm * f32(ya_r) * sa * (1.0 - sa)).astype(BF16)
        dg_ref[:, D:] = (dm * f32(yb_r) * sb * (1.0 - sb)).astype(BF16)

    dya, dyb, dgates = _pcall(
        gate_bwd, name="gate_bwd", grid=R.grid, in_specs=[R.row(D)] * 3 + [R.row(D, 0), R.row(D, 1)],
        out_specs=[R.row(D), R.row(D), R.row(2 * D)],
        out_shape=[sds((N, D), BF16), sds((N, D), BF16), sds((N, 2 * D), BF16)],
        compiler_params=_params(("parallel",)))(dmerged, ya, yb, gates, gates)

    doa = _mm(dya, w_a, tb=True, out_dtype=BF16, name="branch_a_dx")
    g_wa = _mm(oa, dya, ta=True, out_dtype=BF16, name="branch_a_dw")
    dob = _mm(dyb, w_b, tb=True, b3=True, name="branch_b_dx")
    g_wb = _mm(ob, dyb, ta=True, out3=w_b.shape[0], out_dtype=BF16, name="branch_b_dw")
    hook("rest_grads", g_wa=g_wa, g_wb=g_wb, g_wo=g_wo, g_wgu=g_wgu, g_wd=g_wd)

    seg = (jnp.arange(GB_W)[:, None] // HEAD_DIM == jnp.arange(GB_W)[None, :] // HEAD_DIM).astype(BF16)

    def merge_bwd(dob_r, o1r, o2r, o3r, l1r, l2r, l3r, seg_r, d1, d2, d3, e1, e2, e3, *bufs):
        dob_v = dob_r[...]
        os_ = [_from_view(ref, bufs[n], dil[n]) for n, ref in enumerate((o1r, o2r, o3r))]
        la, lb, lc = [_from_view(ref, bufs[3 + n], dil[n]) for n, ref in enumerate((l1r, l2r, l3r))]
        mx = jnp.maximum(jnp.maximum(la, lb), lc)
        ea, eb, ec = jnp.exp(la - mx), jnp.exp(lb - mx), jnp.exp(lc - mx)
        inv = 1.0 / (ea + eb + ec)
        ws = [ea * inv, eb * inv, ec * inv]

        def headsum(v):
            hi = v.astype(BF16)
            r1_ = v - hi.astype(F32)
            mid = r1_.astype(BF16)
            lo = (r1_ - mid.astype(F32)).astype(BF16)
            sm = seg_r[...]
            return (jnp.dot(hi, sm, preferred_element_type=F32) + jnp.dot(mid, sm, preferred_element_type=F32)
                    + jnp.dot(lo, sm, preferred_element_type=F32))

        dws = [headsum(dob_v * o) for o in os_]
        mean = ws[0] * dws[0] + ws[1] * dws[1] + ws[2] * dws[2]
        for n, (w_, dw_, d_ref, e_ref) in enumerate(zip(ws, dws, (d1, d2, d3), (e1, e2, e3))):
            _to_view(w_ * dob_v, d_ref, bufs[6], dil[n])
            _to_view(w_ * (dw_ - mean), e_ref, bufs[7], dil[n])

    vshape = lambda r_, dt: sds((NB, T // r_, r_ * GB_W), dt)
    mb = _pcall(
        merge_bwd, name="merge_bwd", grid=R.grid, in_specs=[R.row(GB_W)] + views + views + [R.const((GB_W, GB_W))],
        out_specs=views + views, out_shape=[vshape(r_, BF16) for r_ in dil] + [vshape(r_, F32) for r_ in dil],
        scratch_shapes=[tokbuf] * 8, compiler_params=_params(("parallel",)))(dob, o1, o2, o3, l1, l2, l3, seg)
    do_b, dlse_b = mb[:3], mb[3:]
    hook("merge_bwd_done")

    dqkv_a, dsink = _attn_bwd(qkv[0], doa, la, None, cosf, sins, sinks, None, NB=NB, T=T, name="attn_a_bwd")
    hook("attn_a_bwd_done")
    dqkv = [dqkv_a]
    for g in range(ngrp):
        dqkv.append(_attn_bwd(qkv[1 + g], do_b[g], (l1, l2, l3)[g], dlse_b[g], cosf, sins, sinks, g, NB=NB, T=T,
                              name=f"attn_b{g}_bwd")[0])
        hook(f"attn_b{g}_bwd_done")

    g_win = [_mm(dseg, u, ta=True, out_dtype=BF16, name=f"inproj_dw{n}") for n, dseg in enumerate(dqkv + [dgates])]
    hook("win_grads", g_win=g_win)
    wvar = lambda v: (w_in, (VAR_W, D), (v, 0))
    du = _mm_multi(dqkv[:1], [wvar(0)], name="inproj_dx0")
    hook("inproj_dx0_done")
    du = _mm_multi(dqkv[1:] + [dgates], [wvar(v) for v in range(1, N_VAR)] + [(w_in, (2 * D, D), (QKV_P // (2 * D), 0))],
                   add=du, tm=256, name="inproj_dx1")
    hook("inproj_dx1_done")

    def x_bwd(dxp_r, du_r, x_r, sm_r, gx_ref, dsm_ref, dhm_ref):
        first = R.first_of_example()
        duv = du_r[...]
        gx_ref[...] = dxp_r[...] + duv * (1.0 + sm_r[0])
        _acc(dsm_ref, first, _colsum(duv * x_r[...]))
        _acc(dhm_ref, first, _colsum(duv))

    gx, dscale_m, dshift_m = _pcall(
        x_bwd, name="x_bwd", grid=R.grid, in_specs=[R.row(D)] * 3 + [R.ex(D)],
        out_specs=[R.row(D), R.ex(D), R.ex(D)], out_shape=[sds((N, D), F32), exsum(), exsum()],
        compiler_params=_params(("arbitrary",)))(dxp, du, x2, scale_m)
    hook("x_bwd_done")

    dmod =jnp.concatenate([dshift_m, dscale_m, dgate_m, dshift_f, dscale_f, dgate_f], axis=-1)[:, 0]
    ln_grads = jnp.concatenate([dg1, db1, dg2, db2], axis=1)
    return dict(loss=loss_p[:, 0, 0], grad_x=gx.reshape(NB, T, D), g_win=g_win, g_wa=g_wa, g_wb=g_wb, g_wo=g_wo,
                g_wgu=g_wgu, g_wd=g_wd, dmod=dmod, ln_grads=ln_grads, dsink=dsink[0, :A_Q_HEADS])


def _coords():
    return lax.axis_index("x"), lax.axis_index("y"), lax.axis_index("c")


def _allgather_small(blk, *, name):
    m_per, n = blk.shape

    def body(x_ref, out_ref, send_sems, recv_sems, local_sem):
        x, y, c = _coords()
        me, sibling = (x, y, c), (x, y, 1 - c)
        chips = [(1 - x, y), (x, 1 - y), (1 - x, 1 - y)]

        def rows(px, py, pc):
            return out_ref.at[pl.ds((4 * px + 2 * py + pc) * m_per, m_per), :]

        def copy(k, block, to, src=None):
            return pltpu.make_async_remote_copy(
                src_ref=rows(*block) if src is None else src, dst_ref=rows(*block),
                send_sem=send_sems.at[k], recv_sem=recv_sems.at[k], device_id=to, device_id_type=MESH)

        mine = pltpu.make_async_copy(x_ref, rows(*me), local_sem)
        mine.start()
        first = [copy(0, me, sibling, src=x_ref)]
        first += [copy(1 + j, me, (*chip, c), src=x_ref) for j, chip in enumerate(chips)]
        for cp in first:
            cp.start()
        passed = [copy(4 + j, (*chip, c), sibling) for j, chip in enumerate(chips)]
        for j, chip in enumerate(chips):
            copy(1 + j, (*chip, c), me).wait_recv()
            passed[j].start()
        copy(0, sibling, me).wait_recv()
        for j, chip in enumerate(chips):
            copy(4 + j, (*chip, 1 - c), me).wait_recv()
        for cp in first + passed:
            cp.wait_send()
        mine.wait()

    return _pcall(
        body, name=name, out_shape=jax.ShapeDtypeStruct((8 * m_per, n), blk.dtype),
        in_specs=[pl.BlockSpec(memory_space=pltpu.VMEM)], out_specs=pl.BlockSpec(memory_space=pltpu.VMEM),
        scratch_shapes=[pltpu.SemaphoreType.DMA((7,)), pltpu.SemaphoreType.DMA((7,)), pltpu.SemaphoreType.DMA],
        compiler_params=pltpu.CompilerParams(vmem_limit_bytes=VMEM_LIMIT_BYTES),
    )(blk)


def _exchange(srcs, dsts, plan, *, name, dst_inits=None):
    na = len(dsts)
    nrem = len(plan(0, 0, 0))

    def body(*refs):
        refs = list(refs)
        src_refs = [refs.pop(0) for _ in range(na)] if srcs is not None else None
        if dst_inits is not None:
            del refs[:na]
        dst_refs, (send_sems, recv_sems) = refs[:na], refs[na:]
        start, wait = _copies(dst_refs if src_refs is None else src_refs, dst_refs, send_sems, recv_sems, plan)
        start()
        wait()

    hbm = pl.BlockSpec(memory_space=pl.ANY)
    ins = (list(srcs) if srcs is not None else []) + (list(dst_inits) if dst_inits is not None else [])
    base = na if srcs is not None else 0
    aliases = {base + a: a for a in range(na)} if dst_inits is not None else {}
    return _pcall(
        body, name=name, out_shape=list(dsts), in_specs=[hbm] * len(ins), out_specs=[hbm] * na,
        input_output_aliases=aliases,
        scratch_shapes=[pltpu.SemaphoreType.DMA((na * nrem,)), pltpu.SemaphoreType.DMA((na * nrem,))],
    )(*ins)


def _other_chips(x, y):
    return [(1 - x, y), (x, 1 - y), (1 - x, 1 - y)]


def _round(ride, carrier, name):
    if carrier is not None:
        _RIDES.setdefault(carrier, []).append(ride)
        return
    srcs = ride.srcs() if callable(ride.srcs) else ride.srcs
    inits = ride.dst_inits() if callable(ride.dst_inits) else ride.dst_inits
    ride.out = list(_exchange(srcs, ride.dsts, ride.plan, name=name, dst_inits=inits))


class _Gather:
    def __init__(self, shards, chip, tag, carriers=(None, None)):
        def plan_ici(x, y, c):
            k = 2 * x + y
            return [((c,), (k, c), (2 * px + py, c), (px, py, c)) for px, py in _other_chips(x, y)]

        def plan_d2d(x, y, c):
            return [((2 * px + py, c), (2 * px + py, c), (2 * px + py, 1 - c), (x, y, 1 - c))
                    for px, py in _other_chips(x, y)]

        self.shards, self.chip = shards, chip
        dsts = [jax.ShapeDtypeStruct((4,) + s.shape, s.dtype) for s in shards]
        ici = _Ride(shards, dsts, plan_ici)
        self.d2d = _Ride(None, dsts, plan_d2d, dst_inits=lambda: ici.out)
        _round(ici, carriers[0], f"gather_{tag}_ici")
        _round(self.d2d, carriers[1], f"gather_{tag}_d2d")

    def result(self):
        full = [lax.dynamic_update_index_in_dim(f, s, self.chip, 0) for f, s in zip(self.d2d.out, self.shards)]
        return [f.reshape((4, 2 * f.shape[2], f.shape[3])) for f in full]


def _index_operand(i):
    return jnp.reshape(i, (1,)).astype(jnp.int32)


def _add_pairs(g, f, ci, *, name):
    s, _, hr, wd = g.shape
    tr = _pick(hr, 600, 16)

    def body(c_ref, a_ref, b_ref, o_ref):
        o_ref[...] = (a_ref[...].astype(F32) + b_ref[...].astype(F32)).astype(BF16)

    spec = pl.BlockSpec((1, tr, wd), lambda j, i, c: (j, i, 0))
    grid_spec = pltpu.PrefetchScalarGridSpec(
        num_scalar_prefetch=1, grid=(s, hr // tr),
        in_specs=[pl.BlockSpec((1, None, tr, wd), lambda j, i, c: (j, c[0], i, 0)), spec], out_specs=spec)
    return _pcall(body, name=name, grid_spec=grid_spec, out_shape=jax.ShapeDtypeStruct(f.shape, BF16),
                  compiler_params=_params(("parallel", "parallel")))(_index_operand(ci), g, f)


def _sum_chips(landed, pairs, chip, *, name):
    s, hr, wd = landed.shape
    tr = _pick(hr, 600, 16)

    def body(k_ref, l_ref, p_ref, o_ref):
        acc = None
        for k in range(s):
            part = jnp.where(k_ref[0] == k, p_ref[k], l_ref[k]).astype(F32)
            acc = part if acc is None else acc + part
        o_ref[...] = acc

    spec = pl.BlockSpec((s, tr, wd), lambda i, k: (0, i, 0))
    grid_spec = pltpu.PrefetchScalarGridSpec(
        num_scalar_prefetch=1, grid=(hr // tr,), in_specs=[spec, spec],
        out_specs=pl.BlockSpec((tr, wd), lambda i, k: (i, 0)))
    return _pcall(body, name=name, grid_spec=grid_spec, out_shape=jax.ShapeDtypeStruct((hr, wd), F32),
                  compiler_params=_params(("parallel",)))(_index_operand(chip), landed, pairs)


class _ReduceScatter:
    def __init__(self, gs, chip, ci, tag):
        self.gs, self.chip, self.ci, self.tag = gs, chip, ci, tag
        self.half_t = [jax.ShapeDtypeStruct((g.shape[0],) + g.shape[2:], BF16) for g in gs]

    def pair(self, carrier=None):
        plan = lambda x, y, c: [((slice(None), 1 - c), (), (), (x, y, 1 - c))]
        self.r1 = _Ride(self.gs, self.half_t, plan)
        _round(self.r1, carrier, f"reduce_{self.tag}_pair")

    def chips(self, carrier=None):
        def plan(x, y, c):
            k = 2 * x + y
            return [((2 * px + py,), (k,), (2 * px + py,), (px, py, c)) for px, py in _other_chips(x, y)]

        self.pairs = [_add_pairs(g, f, self.ci, name=f"reduce_{self.tag}_pair_add{n}")
                      for n, (g, f) in enumerate(zip(self.gs, self.r1.out))]
        self.r2 = _Ride(self.pairs, self.half_t, plan)
        _round(self.r2, carrier, f"reduce_{self.tag}_chips")

    def halves(self, carrier=None):
        plan = lambda x, y, c: [((), (c,), (1 - c,), (x, y, 1 - c))]
        self.mine = [_sum_chips(l, p, self.chip, name=f"reduce_{self.tag}_chip_sum{n}")
                     for n, (l, p) in enumerate(zip(self.r2.out, self.pairs))]
        self.r3 = _Ride(self.mine, [jax.ShapeDtypeStruct((2,) + m.shape, F32) for m in self.mine], plan)
        _round(self.r3, carrier, f"reduce_{self.tag}_halves")

    def result(self):
        return [lax.dynamic_update_index_in_dim(b, m, self.ci, 0).reshape(2 * m.shape[0], m.shape[1])
                for b, m in zip(self.r3.out, self.mine)]


def _ada_fwd(c_all, w_sh, b_sh, *, name):
    nb, d = c_all.shape
    wcols = w_sh.shape[1]
    tn = _pick(wcols, 512)

    def body(c_ref, w_ref, b_ref, o_ref, a_ref):
        cv = c_ref[...]
        act = cv * jax.nn.sigmoid(cv)
        a_ref[...] = act
        o_ref[...] = jnp.dot(act.astype(BF16), w_ref[...].astype(BF16), preferred_element_type=F32) + b_ref[...]

    return _pcall(
        body, name=name, grid=(wcols // tn,),
        in_specs=[pl.BlockSpec((nb, d), lambda j: (0, 0)), pl.BlockSpec((d, tn), lambda j: (0, j)),
                  pl.BlockSpec((1, tn), lambda j: (0, j))],
        out_specs=[pl.BlockSpec((nb, tn), lambda j: (0, j)), pl.BlockSpec((nb, d), lambda j: (0, 0))],
        out_shape=[jax.ShapeDtypeStruct((nb, wcols), F32), jax.ShapeDtypeStruct((nb, d), F32)],
        compiler_params=_params(("arbitrary",)))(c_all, w_sh, b_sh)


def _sum_devices(g, *, name):
    nd, m, w = g.shape

    def body(g_ref, o_ref):
        acc = g_ref[0]
        for k in range(1, nd):
            acc = acc + g_ref[k]
        o_ref[...] = acc

    return _pcall(body, name=name, out_shape=jax.ShapeDtypeStruct((m, w), F32),
                  compiler_params=pltpu.CompilerParams(vmem_limit_bytes=VMEM_LIMIT_BYTES))(g)


def _adamw(w, g, m, v, *, name):
    rows, cols = w.shape[-2:]
    tr = _pick(rows, max(8, (1 << 18) // cols), 8)
    c1 = 1.0 / (1.0 - ADAM_B1 ** ADAM_STEP)
    c2 = 1.0 / (1.0 - ADAM_B2 ** ADAM_STEP)

    def body(w_ref, g_ref, m_ref, v_ref, d_ref, nm_ref, nv_ref):
        gv = g_ref[...]
        nm = ADAM_B1 * m_ref[...] + (1.0 - ADAM_B1) * gv
        nv = ADAM_B2 * v_ref[...] + (1.0 - ADAM_B2) * (gv * gv)
        d_ref[...] = -ADAM_LR * ((nm * c1) / (jnp.sqrt(nv * c2) + ADAM_EPS) + ADAM_WD * w_ref[...])
        nm_ref[...] = nm
        nv_ref[...] = nv

    gspec = pl.BlockSpec((tr, cols), lambda i: (i, 0))
    spec = pl.BlockSpec((None, tr, cols), lambda i: (0, i, 0)) if w.ndim == 3 else gspec
    shp = jax.ShapeDtypeStruct(w.shape, F32)
    return _pcall(body, name=name, grid=(rows // tr,), in_specs=[spec, gspec, spec, spec], out_specs=[spec] * 3,
                  out_shape=[shp] * 3, compiler_params=_params(("parallel",)))(w, g, m, v)


def _permute_in_rows(wt):
    ngrp = len(B_PATTERNS)
    qb, kb, vb = (wt[A_W + n * QB_W:A_W + (n + 1) * QB_W] for n in range(3))
    parts = [wt[:A_W], jnp.zeros((VAR_W - A_W, wt.shape[1]), wt.dtype)]
    for g in range(ngrp):
        parts += [t[g * GB_W:(g + 1) * GB_W] for t in (qb, kb, vb)]
    return jnp.concatenate(parts + [wt[A_W + 3 * QB_W:]], axis=0)


def _unpermute_in_grads(pieces):
    ga, groups, gg = pieces[0], pieces[1:-1], pieces[-1]
    rows = [ga[:A_W]]
    for n in range(3):
        rows += [gp[n * GB_W:(n + 1) * GB_W] for gp in groups]
    return jnp.concatenate(rows + [gg], axis=0)


def kernel(x, c, positions, w_ada, b_ada, w_in, sinks, w_branch_a, w_branch_b, w_o, ln1_g, ln1_b, w_gate_up, w_down, ln2_g, ln2_b, loss_target, m_w_ada, m_b_ada, m_w_in, m_sinks, m_w_branch_a, m_w_branch_b, m_w_o, m_ln1_g, m_ln1_b, m_w_gate_up, m_w_down, m_ln2_g, m_ln2_b, v_w_ada, v_b_ada, v_w_in, v_sinks, v_w_branch_a, v_w_branch_b, v_w_o, v_ln1_g, v_ln1_b, v_w_gate_up, v_w_down, v_ln2_g, v_ln2_b):
    xi, yi, ci = _coords()
    chip = 2 * xi + yi
    dev = 4 * xi + 2 * yi + ci
    NB, T, D = x.shape
    nchip, ndev = 4, 8
    ada_cols = w_ada.shape[2]

    c_blk = jnp.zeros((8, D), F32).at[:NB].set(c)
    c_all = _allgather_small(c_blk, name="gather_c").reshape(ndev, 8, D)[:, :NB].reshape(ndev * NB, D)
    b_sh = lax.dynamic_slice(b_ada, (0, chip * ada_cols), (1, ada_cols))
    mod_part, c_act = _ada_fwd(c_all, w_ada[0], b_sh, name="ada_fwd")
    mod_g = _allgather_small(mod_part, name="gather_mod").reshape(nchip, 2, ndev * NB, ada_cols)[:, 0]
    mod_all = jnp.transpose(mod_g, (1, 0, 2)).reshape(ndev * NB, nchip * ada_cols)
    mod = lax.dynamic_slice(mod_all, (NB * dev, 0), (NB, nchip * ada_cols))

    ra, ro, rd = w_branch_a.shape[1], w_o.shape[1], w_down.shape[1]
    rowsh = jnp.concatenate([w_branch_a[0], w_o[0], w_down[0]], axis=0)
    halves = lambda a: a.reshape(a.shape[:-2] + (2, a.shape[-2] // 2, a.shape[-1]))
    whole = lambda a: a.reshape(a.shape[:-3] + (2 * a.shape[-2], a.shape[-1]))
    tr = lambda a: jnp.swapaxes(a, -1, -2)
    shards = [halves(w.astype(BF16)) for w in (tr(w_in[0]), rowsh, w_branch_b[0], w_gate_up[0])]
    (g_in,) = _Gather(shards[:1], chip, "w_in").result()
    w_in_f = _permute_in_rows(g_in.reshape(nchip * g_in.shape[1], D))
    mix = _Gather(shards[1:3], chip, "w_mix", carriers=("inproj_qkv", "attn_a_fwd"))
    ffn = _Gather(shards[3:], chip, "w_ffn", carriers=("attn_a_fwd", "attn_b0_fwd"))

    def rest_weights():
        (g_rows, w_b_f), (w_gu_f,) = mix.result(), ffn.result()
        return (g_rows[:, :ra].reshape(nchip * ra, D), w_b_f, g_rows[:, ra:ra + ro].reshape(nchip * ro, D), w_gu_f,
                g_rows[:, ra + ro:].reshape(nchip * rd, D))

    red = {}

    def hook(event, **g):
        if event == "rest_grads":
            gr_rows = jnp.concatenate([g["g_wa"].reshape(nchip, ra, D), g["g_wo"].reshape(nchip, ro, D),
                                       g["g_wd"].reshape(nchip, rd, D)], axis=1)
            red["ffn"] = _ReduceScatter([halves(g["g_wgu"])], chip, ci, "ffn")
            red["mix"] = _ReduceScatter([halves(gr_rows), halves(g["g_wb"])], chip, ci, "mix")
            red["ffn"].pair(carrier="merge_bwd")
            red["mix"].pair(carrier="merge_bwd")
        elif event == "merge_bwd_done":
            red["ffn"].chips(carrier="attn_a_bwd")
            red["mix"].chips(carrier="attn_b0_bwd")
        elif event == "attn_a_bwd_done":
            red["ffn"].halves(carrier="attn_b0_bwd")
        elif event == "attn_b0_bwd_done":
            red["mix"].halves(carrier="attn_b1_bwd")
        elif event == "win_grads":
            gr_in = _unpermute_in_grads(g["g_win"])
            red["w_in"] = _ReduceScatter([halves(gr_in.reshape(nchip, gr_in.shape[0] // nchip, D))], chip, ci, "w_in")
            red["w_in"].pair(carrier="inproj_dx0")
        elif event == "inproj_dx0_done":
            red["w_in"].chips(carrier="inproj_dx1")
        elif event == "inproj_dx1_done":
            red["w_in"].halves(carrier="x_bwd")

    res = _local_step(x, mod, positions, w_in_f, rest_weights, sinks[0], ln1_g, ln1_b, ln2_g, ln2_b, loss_target, hook)
    (g_w_in,) = red["w_in"].result()
    (g_rows_red, g_w_b), (g_w_gu,) = red["mix"].result(), red["ffn"].result()
    g_w_a, g_w_o, g_w_d = g_rows_red[:ra], g_rows_red[ra:ra + ro], g_rows_red[ra + ro:]

    small_rows = 24
    misc = jnp.zeros((1, D), F32).at[0, :A_Q_HEADS].set(res["dsink"]).at[0, A_Q_HEADS].set(jnp.sum(res["loss"]))
    small = jnp.concatenate([res["dmod"].reshape(NB * 6, D), jnp.sum(res["ln_grads"], axis=0), misc,
                             jnp.zeros((small_rows - NB * 6 - 5, D), F32)], axis=0)
    small_all = _allgather_small(small, name="gather_small").reshape(ndev, small_rows, D)
    dmod_all = small_all[:, :NB * 6].reshape(ndev * NB, 6 * D)
    sums = _sum_devices(small_all, name="sum_small")
    g_b_ada = (sums[0:6] + sums[6:12]).reshape(1, 6 * D)
    g_ln1_g, g_ln1_b, g_ln2_g, g_ln2_b = (sums[12 + n][None] for n in range(4))
    g_sinks = sums[16, :A_Q_HEADS][None]
    loss = sums[16, A_Q_HEADS]
    dmod_sh = lax.dynamic_slice(dmod_all, (0, chip * ada_cols), (ndev * NB, ada_cols))
    g_w_ada = _mm(c_act, dmod_sh, ta=True, name="ada_dw")

    names = ["w_ada", "b_ada", "w_in", "sinks", "w_branch_a", "w_branch_b", "w_o", "ln1_g", "ln1_b",
             "w_gate_up", "w_down", "ln2_g", "ln2_b"]
    ws = [w_ada, b_ada, w_in, sinks, w_branch_a, w_branch_b, w_o, ln1_g, ln1_b, w_gate_up, w_down, ln2_g, ln2_b]
    ms = [m_w_ada, m_b_ada, m_w_in, m_sinks, m_w_branch_a, m_w_branch_b, m_w_o, m_ln1_g, m_ln1_b, m_w_gate_up,
          m_w_down, m_ln2_g, m_ln2_b]
    vs = [v_w_ada, v_b_ada, v_w_in, v_sinks, v_w_branch_a, v_w_branch_b, v_w_o, v_ln1_g, v_ln1_b, v_w_gate_up,
          v_w_down, v_ln2_g, v_ln2_b]
    gs = [g_w_ada, g_b_ada, g_w_in, g_sinks, g_w_a, g_w_b, g_w_o, g_ln1_g, g_ln1_b, g_w_gu, g_w_d, g_ln2_g, g_ln2_b]
    grads, deltas, new_ms, new_vs = [], [], [], []
    for name, w, g, m, v in zip(names, ws, gs, ms, vs):
        flip = tr if name == "w_in" else (lambda a: a)
        w, m, v = flip(w), flip(m), flip(v)
        g2 = g.reshape(w.shape[-2:])
        d, nm, nv = _adamw(w, g2, m, v, name="adamw_" + name)
        grads.append(flip(g2.reshape(w.shape)))
        deltas.append(flip(d))
        new_ms.append(flip(nm))
        new_vs.append(flip(nv))
    return (loss, res["grad_x"], *grads, *deltas, *new_ms, *new_vs)
```

```python
import functools

import jax
import jax.numpy as jnp
from jax import lax
from jax.experimental import pallas as pl
from jax.experimental.pallas import tpu as pltpu

F32 = jnp.float32
BF16 = jnp.bfloat16
MESH = pl.DeviceIdType.MESH

HEAD_DIM = 64
LANES = 128
PAIR_W = 2 * HEAD_DIM
BLOCK = 128
A_Q_HEADS = 16
A_KV_HEADS = 2
A_WINDOW = 128
B_PATTERNS = ((128, 1), (512, 4), (2048, 16))
B_GROUP_HEADS = 8
QA_W = A_Q_HEADS * HEAD_DIM
KA_W = A_KV_HEADS * HEAD_DIM
GB_W = B_GROUP_HEADS * HEAD_DIM
QB_W = GB_W * len(B_PATTERNS)
A_W = QA_W + 2 * KA_W
VAR_W = 3 * GB_W
N_VAR = 1 + len(B_PATTERNS)
VAR_DIL = (1,) + tuple(r for _, r in B_PATTERNS)
QKV_P = N_VAR * VAR_W
ROPE_THETA = 10000.0
LN_EPS = 1e-5
NEG_INF = -1e30
DEPTH = 1
ALPHA = (2 * DEPTH) ** 0.25
SCALE = HEAD_DIM ** -0.5

ADAM_LR, ADAM_B1, ADAM_B2, ADAM_EPS, ADAM_WD, ADAM_STEP = 0.001, 0.9, 0.999, 1e-08, 0.01, 10

VMEM_LIMIT_BYTES = 56 * 1024 * 1024
MM_TILE_BYTES = 36 * 1024 * 1024
MM_WHOLE_K = 4096


def _params(sem=None):
    return pltpu.CompilerParams(dimension_semantics=sem, vmem_limit_bytes=VMEM_LIMIT_BYTES)


_RIDES = {}


def _pcall(body, *, name, **kw):
    rides = _RIDES.pop(name, None)
    if rides is None:
        return pl.pallas_call(body, name=name, **kw)
    return _riding_call(body, rides, name=name, **kw)


def _copies(src_refs, dst_refs, send_sems, recv_sems, plan):
    x, y, c = lax.axis_index("x"), lax.axis_index("y"), lax.axis_index("c")
    remote = plan(x, y, c)
    nrem = len(remote)
    at = lambda ref, idx: ref.at[idx] if idx else ref

    def copy(a, n, landing):
        si, di, ri, peer = remote[n]
        return pltpu.make_async_remote_copy(
            src_ref=at(src_refs[a], si), dst_ref=at(dst_refs[a], ri if landing else di),
            send_sem=send_sems.at[a * nrem + n], recv_sem=recv_sems.at[a * nrem + n],
            device_id=peer, device_id_type=MESH)

    order = [(a, n) for a in range(len(dst_refs)) for n in range(nrem)]

    def start():
        for a, n in order:
            copy(a, n, False).start()

    def wait():
        for a, n in order:
            copy(a, n, True).wait_recv()
        for a, n in order:
            copy(a, n, False).wait_send()

    return start, wait


class _Ride:
    def __init__(self, srcs, dsts, plan, dst_inits=None):
        self.srcs, self.dsts, self.plan, self.dst_inits, self.out = srcs, dsts, plan, dst_inits, None


def _riding_call(body, rides, *, name, grid, in_specs, out_specs, out_shape, scratch_shapes=(), **kw):
    single = not isinstance(out_specs, (list, tuple))
    out_specs = [out_specs] if single else list(out_specs)
    out_shape = [out_shape] if single else list(out_shape)
    n_in, n_out, n_scr = len(in_specs), len(out_specs), len(scratch_shapes)
    xin, xdsts, sems, aliases, layout = [], [], [], {}, []
    for ride in rides:
        srcs = ride.srcs() if callable(ride.srcs) else ride.srcs
        inits = ride.dst_inits() if callable(ride.dst_inits) else ride.dst_inits
        na, nrem = len(ride.dsts), len(ride.plan(0, 0, 0))
        src_at = len(xin) if srcs is not None else None
        xin += list(srcs) if srcs is not None else []
        if inits is not None:
            aliases.update({n_in + len(xin) + a: n_out + len(xdsts) + a for a in range(na)})
            xin += list(inits)
        layout.append((src_at, len(xdsts), na))
        xdsts += list(ride.dsts)
        sems += [pltpu.SemaphoreType.DMA((na * nrem,)), pltpu.SemaphoreType.DMA((na * nrem,))]

    def wrapped(*refs):
        ins, xins = refs[:n_in], refs[n_in:n_in + len(xin)]
        outs = refs[n_in + len(xin):n_in + len(xin) + n_out]
        xouts = refs[n_in + len(xin) + n_out:n_in + len(xin) + n_out + len(xdsts)]
        scr = refs[n_in + len(xin) + n_out + len(xdsts):]
        rounds = []
        for k, (ride, (src_at, dst_at, na)) in enumerate(zip(rides, layout)):
            dsts = xouts[dst_at:dst_at + na]
            srcs = dsts if src_at is None else xins[src_at:src_at + na]
            rounds.append(_copies(srcs, dsts, scr[n_scr + 2 * k], scr[n_scr + 2 * k + 1], ride.plan))
        ids = [pl.program_id(a) for a in range(len(grid))]
        first = functools.reduce(jnp.logical_and, [i == 0 for i in ids])
        last = functools.reduce(jnp.logical_and, [i == g - 1 for i, g in zip(ids, grid)])

        @pl.when(first)
        def _():
            for start, _ in rounds:
                start()

        body(*ins, *outs, *scr[:n_scr])

        @pl.when(last)
        def _():
            for _, wait in rounds:
                wait()

    hbm = pl.BlockSpec(memory_space=pl.ANY)

    def run(*args):
        res = pl.pallas_call(
            wrapped, name=name, grid=grid, in_specs=list(in_specs) + [hbm] * len(xin),
            out_specs=out_specs + [hbm] * len(xdsts), out_shape=out_shape + xdsts,
            scratch_shapes=list(scratch_shapes) + sems, input_output_aliases=aliases,
            compiler_params=_params(("arbitrary",) * len(grid)),
        )(*args, *xin)
        for ride, (_, dst_at, na) in zip(rides, layout):
            ride.out = list(res[n_out + dst_at:n_out + dst_at + na])
        return res[0] if single else list(res[:n_out])

    return run


def _pick(n, target, quantum=128):
    t = (min(target, n) // quantum) * quantum
    while t >= quantum:
        if n % t == 0:
            return t
        t -= quantum
    return n


def _mm(a, b, *, name, ta=False, tb=False, b3=False, out3=0, out_dtype=F32, add=None, tm=1024, tn=1536, tk=1536):
    if ta:
        K, M = a.shape
    else:
        M, K = a.shape
    if b3 and tb:
        Nn, K2, tk = b.shape[1], b.shape[0] * b.shape[2], b.shape[2]
    elif b3:
        K2, Nn, tn = b.shape[1], b.shape[0] * b.shape[2], b.shape[2]
    elif tb:
        Nn, K2 = b.shape
    else:
        K2, Nn = b.shape
    assert K == K2, (a.shape, b.shape)
    if out3:
        tn = Nn // out3
    tm, tn, tk = _pick(M, tm), _pick(Nn, tn), _pick(K, tk)
    if not (b3 and tb) and K <= MM_WHOLE_K:
        tk = K
        fits = lambda: 4 * tk * (tm + tn) + 8 * tm * tn * (2 if add is not None else 1) <= MM_TILE_BYTES
        while not fits():
            if (tm >= tn or b3 or out3) and tm > 256:
                tm = _pick(M, tm - 128)
            elif not (b3 or out3) and tn > 256:
                tn = _pick(Nn, tn - 128)
            else:
                break
    nk = K // tk
    j_outer = K * Nn + (Nn // tn) * M * K < M * K + (M // tm) * K * Nn
    dn = (((0 if ta else 1,), (1 if tb else 0,)), ((), ()))

    def body(*refs):
        refs = list(refs)
        a_ref, b_ref = refs[:2]
        add_ref = refs[2] if add is not None else None
        o_ref = refs[3] if add is not None else refs[2]
        part = lax.dot_general(a_ref[...].astype(BF16), b_ref[...].astype(BF16), dn, preferred_element_type=F32)

        def finish(r):
            if add is not None:
                r = r + add_ref[...]
            o_ref[...] = r.astype(out_dtype)

        if nk == 1:
            finish(part)
            return
        acc = refs[-1]
        k = pl.program_id(2)

        @pl.when(k == 0)
        def _():
            acc[...] = part

        @pl.when(k > 0)
        def _():
            acc[...] += part

        @pl.when(k == nk - 1)
        def _():
            finish(acc[...])

    def spec(shape, index):
        return pl.BlockSpec(shape, (lambda j, i, k: index(i, j, k)) if j_outer else index)

    a_spec = spec((tk, tm), lambda i, j, k: (k, i)) if ta else spec((tm, tk), lambda i, j, k: (i, k))
    if b3 and tb:
        b_spec = spec((None, tn, tk), lambda i, j, k: (k, j, 0))
    elif b3:
        b_spec = spec((None, tk, tn), lambda i, j, k: (j, k, 0))
    elif tb:
        b_spec = spec((tn, tk), lambda i, j, k: (j, k))
    else:
        b_spec = spec((tk, tn), lambda i, j, k: (k, j))
    if out3:
        o_spec = spec((None, tm, tn), lambda i, j, k: (j, i, 0))
    else:
        o_spec = spec((tm, tn), lambda i, j, k: (i, j))
    ins, specs = [a, b], [a_spec, b_spec]
    if add is not None:
        ins.append(add)
        specs.append(o_spec)
    grid = (Nn // tn, M // tm, nk) if j_outer else (M // tm, Nn // tn, nk)
    return _pcall(
        body, name=name, grid=grid, in_specs=specs, out_specs=o_spec,
        out_shape=jax.ShapeDtypeStruct((out3, M, tn) if out3 else (M, Nn), out_dtype),
        scratch_shapes=[pltpu.VMEM((tm, tn), F32)] if nk > 1 else [],
        compiler_params=_params(("parallel", "parallel", "arbitrary")),
    )(*ins)


def _mm_multi(a_list, b_list, *, name, M, T=None, add=None, out_dtype=F32, tm=512):
    tm = _pick(T or M, tm)
    ns = len(a_list)
    dils = [a[1] if isinstance(a, tuple) else 0 for a in a_list]
    a_arrs = [a[0] if isinstance(a, tuple) else a for a in a_list]
    widths = [a.shape[-1] // max(r, 1) for a, r in zip(a_arrs, dils)]
    b_arrs, b_specs = [], []
    for b in b_list:
        arr, shp, idx = b if isinstance(b, tuple) else (b, b.shape, (0, 0))
        b_arrs.append(arr)
        b_specs.append(pl.BlockSpec(shp, lambda i, idx=idx: idx))
    Nn = b_specs[0].block_shape[1]
    dn = (((1,), (0,)), ((), ()))
    nin = 2 * ns + (1 if add is not None else 0)

    def body(*refs):
        a_refs, b_refs, scr = refs[:ns], refs[ns:2 * ns], list(refs[nin + 1:])
        acc = None
        for a_ref, b_ref, r in zip(a_refs, b_refs, dils):
            av = _from_view(a_ref, scr.pop(0), r) if r > 1 else a_ref[...]
            part = lax.dot_general(av.astype(BF16), b_ref[...], dn, preferred_element_type=F32)
            acc = part if acc is None else acc + part
        if add is not None:
            acc = acc + refs[2 * ns][...]
        refs[nin][...] = acc.astype(out_dtype)

    tpe = (T or M) // tm
    a_specs = [pl.BlockSpec((None, tm // r, r * w), lambda i: (i // tpe, i % tpe, 0)) if r
               else pl.BlockSpec((tm, w), lambda i: (i, 0)) for r, w in zip(dils, widths)]
    o_spec = pl.BlockSpec((tm, Nn), lambda i: (i, 0))
    specs = a_specs + b_specs
    ins = a_arrs + b_arrs
    if add is not None:
        specs.append(o_spec)
        ins.append(add)
    scratch = [pltpu.VMEM((w // LANES, tm, LANES), F32) for r, w in zip(dils, widths) if r > 1]
    return _pcall(body, name=name, grid=(M // tm,), in_specs=specs, out_specs=o_spec, scratch_shapes=scratch,
                  out_shape=jax.ShapeDtypeStruct((M, Nn), out_dtype), compiler_params=_params(("parallel",)))(*ins)


def _dw_view(d3, u, r, *, name, tk=1024):
    NB, tsub, rw = d3.shape
    W, T, D = rw // r, tsub * r, u.shape[1]
    tk = _pick(T, tk)
    tpe, nk = T // tk, NB * T // tk

    def body(d_ref, u_ref, o_ref, acc, scr):
        k = pl.program_id(0)
        dv = _from_view(d_ref, scr, r).astype(BF16)
        part = lax.dot_general(dv, u_ref[...], _TN, preferred_element_type=F32)

        @pl.when(k == 0)
        def _():
            acc[...] = part

        @pl.when(k > 0)
        def _():
            acc[...] += part

        @pl.when(k == nk - 1)
        def _():
            o_ref[...] = acc[...].astype(o_ref.dtype)

    return _pcall(
        body, name=name, grid=(nk,),
        in_specs=[pl.BlockSpec((None, tk // r, rw), lambda k: (k // tpe, k % tpe, 0)), pl.BlockSpec((tk, D), lambda k: (k, 0))],
        out_specs=pl.BlockSpec((W, D), lambda k: (0, 0)), out_shape=jax.ShapeDtypeStruct((W, D), BF16),
        scratch_shapes=[pltpu.VMEM((W, D), F32), pltpu.VMEM((W // LANES, tk, LANES), F32)],
        compiler_params=_params(("arbitrary",)))(d3, u)


def _lane(shape):
    return lax.broadcasted_iota(jnp.int32, shape, len(shape) - 1)


def _rot_half(v):
    w = v.shape[-1]
    first = (_lane(v.shape) % HEAD_DIM) < (HEAD_DIM // 2)
    return jnp.where(first, pltpu.roll(v, w - HEAD_DIM // 2, v.ndim - 1), pltpu.roll(v, HEAD_DIM // 2, v.ndim - 1))


def _widen(t, w):
    return t if w == t.shape[-1] else jnp.concatenate([t] * (w // t.shape[-1]), axis=-1)


def _unrope(v, cos, sins):
    w = v.shape[-1]
    return v * _widen(cos, w) - _rot_half(v) * _widen(sins, w)


def _rope_tables(positions):
    half = HEAD_DIM // 2
    inv = ROPE_THETA ** (-jnp.arange(half, dtype=F32) / half)
    ang = positions.astype(F32)[..., None] * inv
    cos, sin = jnp.cos(ang), jnp.sin(ang)
    cosf = jnp.concatenate([cos, cos, cos, cos], axis=-1)
    sins = jnp.concatenate([-sin, sin, -sin, sin], axis=-1)
    n = positions.shape[0] * positions.shape[1]
    return cosf.reshape(n, PAIR_W), sins.reshape(n, PAIR_W)


def _inproj(x2, scale, shift, w, cosf, sins, flags, *, T, name):
    N, D = x2.shape
    tm, tn = _pick(T, 512), VAR_W
    tpe = T // tm

    def body(x_ref, sc_ref, sh_ref, w_ref, c_ref, s_ref, f_ref, *outs):
        o_refs, u_ref = outs[:N_VAR], outs[N_VAR]
        j = pl.program_id(1)

        @pl.when(j == 0)
        def _():
            u_ref[...] = (x_ref[...] * (1.0 + sc_ref[0]) + sh_ref[0]).astype(BF16)

        acc = lax.dot_general(u_ref[...], w_ref[...], (((1,), (1,)), ((), ())), preferred_element_type=F32)
        fl = f_ref[...]
        ce = 1.0 + (_widen(c_ref[...], tn) - 1.0) * fl
        se = _widen(s_ref[...], tn) * fl
        res = acc * ce + _rot_half(acc) * se
        for v in range(N_VAR):
            @pl.when(j == v)
            def _(v=v):
                _to_view(res, o_refs[v], outs[N_VAR + 1], VAR_DIL[v])

    ex = pl.BlockSpec((1, 1, D), lambda i, j: (i // tpe, 0, 0))
    tab = pl.BlockSpec((tm, PAIR_W), lambda i, j: (i, 0))
    keep = lambda w_: pl.BlockSpec((tm, w_), lambda i, j: (i, 0))
    vspec = lambda r: pl.BlockSpec((None, tm // r, r * tn), lambda i, j: (i // tpe, i % tpe, 0))
    vshape = lambda r: jax.ShapeDtypeStruct((N // T, T // r, r * tn), BF16)
    return _pcall(
        body, name=name, grid=(N // tm, N_VAR),
        in_specs=[keep(D), ex, ex, pl.BlockSpec((tn, D), lambda i, j: (j, 0)), tab, tab,
                  pl.BlockSpec((1, tn), lambda i, j: (0, j))],
        out_specs=[vspec(r) for r in VAR_DIL] + [keep(D)],
        out_shape=[vshape(r) for r in VAR_DIL] + [jax.ShapeDtypeStruct((N, D), BF16)],
        scratch_shapes=[pltpu.VMEM((tn // LANES, tm, LANES), F32)],
        compiler_params=_params(("parallel", "arbitrary")),
    )(x2, scale, shift, w, cosf, sins, flags)


class _Geom:
    def __init__(self, g):
        if g is None:
            self.r, self.nq, self.n_back, self.sink = 1, A_Q_HEADS, A_WINDOW - 1, True
            self.qw, self.kw = QA_W, KA_W
            self.qidx = lambda j: 0
            self.kidx = lambda j: QA_W // KA_W
            self.vidx = lambda j: QA_W // KA_W + 1
        else:
            window, r = B_PATTERNS[g]
            self.r, self.nq, self.n_back, self.sink = r, B_GROUP_HEADS, window // r, False
            self.qw, self.kw = GB_W, GB_W
            self.qidx = lambda j: 3 * j
            self.kidx = lambda j: 3 * j + 1
            self.vidx = lambda j: 3 * j + 2
        self.ntile = self.qw // PAIR_W


def _stack_heads(t, scale=None):
    first = _lane(t.shape) < HEAD_DIM
    z = jnp.zeros_like(t)
    if scale is not None:
        t = t * jnp.asarray(scale, t.dtype)
    return jnp.concatenate([jnp.where(first, t, z), jnp.where(first, z, t)], axis=0)


def _lse_col(t):
    return jnp.concatenate([t[:, 0:1], t[:, HEAD_DIM:HEAD_DIM + 1]], axis=0)


def _lse_rows(t, width):
    first = _lane(t.shape) < HEAD_DIM
    other = pltpu.roll(t, HEAD_DIM, 1)
    full = jnp.concatenate([jnp.where(first, t, other), jnp.where(first, other, t)], axis=0)
    return _widen(full, width)


def _unstack_heads(v2):
    return jnp.where(_lane((BLOCK, PAIR_W)) < HEAD_DIM, v2[:BLOCK], v2[BLOCK:])


def _dup_head(t, kh):
    tf = t.astype(F32)
    keep = (_lane(t.shape) < HEAD_DIM) if kh == 0 else (_lane(t.shape) >= HEAD_DIM)
    return jnp.where(keep, tf, pltpu.roll(tf, HEAD_DIM, 1)).astype(t.dtype)


def _fold_heads(t):
    return t + pltpu.roll(t, HEAD_DIM, 1)


def _band_mask(rows, i, n_back, single):
    nkeys = BLOCK if single else 2 * BLOCK
    qi = jnp.bitwise_and(lax.broadcasted_iota(jnp.int32, (rows, nkeys), 0), BLOCK - 1)
    ki = lax.broadcasted_iota(jnp.int32, (rows, nkeys), 1)
    if single:
        return qi >= ki
    dist = qi + BLOCK - ki
    return jnp.logical_and(jnp.logical_and(dist >= 0, dist <= n_back), jnp.logical_or(ki >= BLOCK, i > 0))


def _per_block(col, scalars, fn):
    return jnp.concatenate([fn(col[b * BLOCK:(b + 1) * BLOCK], sc) for b, sc in enumerate(scalars)], axis=0)


def _sink_slot(rows):
    qi = jnp.bitwise_and(lax.broadcasted_iota(jnp.int32, (rows, 2 * BLOCK), 0), BLOCK - 1)
    return qi == lax.broadcasted_iota(jnp.int32, (rows, 2 * BLOCK), 1)


def _sink_scores(rows, sinks):
    blk = lax.broadcasted_iota(jnp.int32, (rows, 2 * BLOCK), 0) // BLOCK
    out = jnp.full((rows, 2 * BLOCK), sinks[-1], F32)
    for b in range(len(sinks) - 2, -1, -1):
        out = jnp.where(blk == b, sinks[b], out)
    return out


def _softmax_parts(s, valid, sinks):
    s = jnp.where(valid, s, NEG_INF)
    if sinks is not None:
        slot = _sink_slot(s.shape[0])
        s = jnp.where(slot, _sink_scores(s.shape[0], sinks), s)
    m = jnp.max(s, axis=1, keepdims=True)
    p = jnp.exp(s - m)
    den = jnp.sum(p, axis=1, keepdims=True)
    if sinks is not None:
        p = jnp.where(slot, 0.0, p)
    return p, m, den


_NT = (((1,), (1,)), ((), ()))
_TN = (((0,), (0,)), ((), ()))


def _rows2(prev_ref, cur_ref, cs, single=False):
    if single:
        return cur_ref[0, :, cs]
    return jnp.concatenate([prev_ref[0, :, cs], cur_ref[0, :, cs]], axis=0)


def _sink_scalars(sink_ref, first, nblocks):
    return [sink_ref[first + b] for b in range(nblocks)]


def _tile(t):
    return slice(t * PAIR_W, (t + 1) * PAIR_W)


def _attn_fwd(qkv, sinks, g, *, NB, T, name):
    geo = _Geom(g)
    r, qw, kw, ntile = geo.r, geo.qw, geo.kw, geo.ntile
    tsub = T // r
    nblk = tsub // BLOCK
    qkv3 = qkv.reshape(NB, tsub, r * VAR_W)
    out_dtype = BF16 if g is None else F32
    tiles_per_kv = ntile // A_KV_HEADS

    single = nblk == 1

    def body(q_ref, kp_ref, kc_ref, vp_ref, vc_ref, sink_ref, o_ref, l_ref):
        i = pl.program_id(2)
        if geo.sink:
            kall, vall = _rows2(kp_ref, kc_ref, _tile(0)), _rows2(vp_ref, vc_ref, _tile(0))
            kdup = [_dup_head(kall, kh) for kh in range(A_KV_HEADS)]
            vdup = [_dup_head(vall, kh) for kh in range(A_KV_HEADS)]
            tiles = [[t] for t in range(ntile)]
            q2s = [_stack_heads(q_ref[0, :, _tile(t)], SCALE) for t in range(ntile)]
            kks = [kdup[t // tiles_per_kv] for t in range(ntile)]
            vvs = [vdup[t // tiles_per_kv] for t in range(ntile)]
            sinkcols = [_sink_scalars(sink_ref, 2 * t, 2) for t in range(ntile)]
        else:
            tiles = [[t] for t in range(ntile)]
            q2s = [_stack_heads(q_ref[0, :, _tile(t)], SCALE) for t in range(ntile)]
            kks = [_rows2(kp_ref, kc_ref, _tile(t), single) for t in range(ntile)]
            vvs = [_rows2(vp_ref, vc_ref, _tile(t), single) for t in range(ntile)]
            sinkcols = [None] * ntile
        valid = _band_mask(q2s[0].shape[0], i, geo.n_back, single)
        ss = [lax.dot_general(q2, kk, _NT, preferred_element_type=F32) for q2, kk in zip(q2s, kks)]
        parts = [_softmax_parts(s, valid, sc) for s, sc in zip(ss, sinkcols)]
        o2s = [jnp.dot(p.astype(BF16), vv, preferred_element_type=F32) / den for (p, m, den), vv in zip(parts, vvs)]
        for ts, o2, (p, m, den) in zip(tiles, o2s, parts):
            lse2 = jnp.broadcast_to(m + jnp.log(den), (o2.shape[0], PAIR_W))
            for n, t in enumerate(ts):
                rows = slice(2 * BLOCK * n, 2 * BLOCK * (n + 1))
                o_ref[0, :, _tile(t)] = _unstack_heads(o2[rows]).astype(out_dtype)
                l_ref[0, :, _tile(t)] = _unstack_heads(lse2[rows])

    prev = lambda i: jnp.maximum(i - 1, 0)
    in_specs = [
        pl.BlockSpec((1, BLOCK, qw), lambda b, j, i: (b, i, geo.qidx(j))),
        pl.BlockSpec((1, BLOCK, kw), lambda b, j, i: (b, prev(i), geo.kidx(j))),
        pl.BlockSpec((1, BLOCK, kw), lambda b, j, i: (b, i, geo.kidx(j))),
        pl.BlockSpec((1, BLOCK, kw), lambda b, j, i: (b, prev(i), geo.vidx(j))),
        pl.BlockSpec((1, BLOCK, kw), lambda b, j, i: (b, i, geo.vidx(j))),
        pl.BlockSpec(memory_space=pltpu.SMEM),
    ]
    o_spec = pl.BlockSpec((1, BLOCK, qw), lambda b, j, i: (b, i, j))
    shape = (NB, tsub, r * qw)
    o, lse = _pcall(
        body, name=name, grid=(NB, r, nblk), in_specs=in_specs, out_specs=[o_spec, o_spec],
        out_shape=[jax.ShapeDtypeStruct(shape, out_dtype), jax.ShapeDtypeStruct(shape, F32)],
        compiler_params=_params(("parallel", "parallel", "arbitrary")),
    )(qkv3, qkv3, qkv3, qkv3, qkv3, sinks)
    return o, lse


def _attn_bwd(qkv, do, lse, dlse, cosf, sins, sinks, g, *, NB, T, name):
    geo = _Geom(g)
    r, qw, kw, ntile = geo.r, geo.qw, geo.kw, geo.ntile
    tsub = T // r
    nblk = tsub // BLOCK
    view = lambda a, w: a.reshape(NB, tsub, r * w)
    has_dlse = dlse is not None
    tiles_per_kv = ntile // A_KV_HEADS

    single = nblk == 1
    krows = BLOCK if single else 2 * BLOCK
    nsteps = 1 if single else nblk + 1

    def grads(q2s, kks, vvs, do2s, i, lserows, sinkcols, dlrows):
        nrow = q2s[0].shape[0]
        ki = lax.broadcasted_iota(jnp.int32, (krows, nrow), 0)
        qi = jnp.bitwise_and(lax.broadcasted_iota(jnp.int32, (krows, nrow), 1), BLOCK - 1)
        if single:
            valid = qi >= ki
        else:
            dist = qi + BLOCK - ki
            valid = jnp.logical_and(jnp.logical_and(dist >= 0, dist <= geo.n_back), jnp.logical_or(ki >= BLOCK, i > 0))
        sts = [lax.dot_general(kk, q2, _NT, preferred_element_type=F32) for q2, kk in zip(q2s, kks)]
        dpts = [lax.dot_general(vv, do2, _NT, preferred_element_type=F32) for do2, vv in zip(do2s, vvs)]
        pts, dsts, sks = [], [], []
        for st, dpt, ls, sc, dl in zip(sts, dpts, lserows, sinkcols, dlrows):
            sv = jnp.where(valid, st, NEG_INF)
            if sc is not None:
                slot = ki == qi
                blk = lax.broadcasted_iota(jnp.int32, (krows, nrow), 1) // BLOCK
                sink = jnp.full((krows, nrow), sc[-1], F32)
                for b in range(len(sc) - 2, -1, -1):
                    sink = jnp.where(blk == b, sc[b], sink)
                sv = jnp.where(slot, sink, sv)
                dpt = jnp.where(slot, 0.0, dpt)
            pt = jnp.exp(sv - ls)
            delta = jnp.sum(pt * dpt, axis=0, keepdims=True)
            if dl is not None:
                delta = delta - dl
            dst = pt * (dpt - delta)
            if sc is not None:
                cols = lambda a, b: a[:, b * BLOCK:(b + 1) * BLOCK]
                sks.append([jnp.sum(jnp.where(cols(slot, b), cols(dst, b), 0.0)) for b in range(len(sc))])
                dst, pt = jnp.where(slot, 0.0, dst), jnp.where(slot, 0.0, pt)
            else:
                sks.append(None)
            pts.append(pt.astype(BF16))
            dsts.append(dst.astype(BF16))
        dq2s = [lax.dot_general(dst, kk, _TN, preferred_element_type=F32) * SCALE for dst, kk in zip(dsts, kks)]
        dkks = [jnp.dot(dst, q2, preferred_element_type=F32) for dst, q2 in zip(dsts, q2s)]
        dvvs = [jnp.dot(pt, do2, preferred_element_type=F32) for pt, do2 in zip(pts, do2s)]
        return dq2s, dkks, dvvs, sks

    def stat_row(t):
        tt = t.T
        return jnp.concatenate([tt[0:1, :], tt[HEAD_DIM:HEAD_DIM + 1, :]], axis=1)

    def body(*refs):
        it = iter(refs)
        q_ref, kp_ref, kc_ref, vp_ref, vc_ref, do_ref, l_ref = (next(it) for _ in range(7))
        dl_ref = next(it) if has_dlse else None
        c_ref, s_ref, sink_ref, o_ref, ds_ref, dq_s, dk_s, dv_s, car_q, car_k, car_v = (next(it) for _ in range(11))
        b, j, i = pl.program_id(0), pl.program_id(1), pl.program_id(2)

        @pl.when(jnp.logical_and(b == 0, jnp.logical_and(j == 0, i == 0)))
        def _():
            ds_ref[...] = jnp.zeros_like(ds_ref)

        def compute():
            if geo.sink:
                kall, vall = _rows2(kp_ref, kc_ref, _tile(0)), _rows2(vp_ref, vc_ref, _tile(0))
                nb = 2 * tiles_per_kv
                tiles = [[kh * tiles_per_kv + t for t in range(tiles_per_kv)] for kh in range(A_KV_HEADS)]
                cat = lambda f, ts: jnp.concatenate([f(t) for t in ts], axis=0)
                dq2s, dkks, dvvs, sks = grads(
                    [cat(lambda t: _stack_heads(q_ref[0, :, _tile(t)], SCALE), ts) for ts in tiles],
                    [_dup_head(kall, kh) for kh in range(A_KV_HEADS)],
                    [_dup_head(vall, kh) for kh in range(A_KV_HEADS)],
                    [cat(lambda t: _stack_heads(do_ref[0, :, _tile(t)]), ts) for ts in tiles], i,
                    [jnp.concatenate([stat_row(l_ref[0, :, _tile(t)]) for t in ts], axis=1) for ts in tiles],
                    [_sink_scalars(sink_ref, kh * nb, nb) for kh in range(A_KV_HEADS)], [None] * A_KV_HEADS)
                lane1 = _lane((1, PAIR_W))
                dsink = jnp.zeros((1, PAIR_W), F32)
                for kh, (ts, dq2, sk) in enumerate(zip(tiles, dq2s, sks)):
                    for n, t in enumerate(ts):
                        dq_s[:, _tile(t)] = _unstack_heads(dq2[2 * BLOCK * n:2 * BLOCK * (n + 1)])
                    for bb in range(nb):
                        dsink = dsink + jnp.where(lane1 == kh * nb + bb, sk[bb], 0.0)
                second = _lane((krows, PAIR_W)) >= HEAD_DIM
                dk_s[...] = jnp.where(second, _fold_heads(dkks[1]), _fold_heads(dkks[0]))
                dv_s[...] = jnp.where(second, _fold_heads(dvvs[1]), _fold_heads(dvvs[0]))
                ds_ref[0:1, :] += dsink
            else:
                dq2s, dkks, dvvs, _ = grads(
                    [_stack_heads(q_ref[0, :, _tile(t)], SCALE) for t in range(ntile)],
                    [_rows2(kp_ref, kc_ref, _tile(t), single) for t in range(ntile)],
                    [_rows2(vp_ref, vc_ref, _tile(t), single) for t in range(ntile)],
                    [_stack_heads(do_ref[0, :, _tile(t)]) for t in range(ntile)], i,
                    [stat_row(l_ref[0, :, _tile(t)]) for t in range(ntile)], [None] * ntile,
                    [stat_row(dl_ref[0, :, _tile(t)]) for t in range(ntile)])
                for t in range(ntile):
                    dq_s[:, _tile(t)] = _unstack_heads(dq2s[t])
                    dk_s[0:krows, _tile(t)] = dkks[t]
                    dv_s[0:krows, _tile(t)] = dvvs[t]

        def emit(dq, dk, dv):
            cos, sn = c_ref[0], s_ref[0]
            o_ref[0, :, 0:qw] = _unrope(dq, cos, sn).astype(BF16)
            o_ref[0, :, qw:qw + kw] = _unrope(dk, cos, sn).astype(BF16)
            o_ref[0, :, qw + kw:qw + 2 * kw] = dv.astype(BF16)
            if qw + 2 * kw < VAR_W:
                o_ref[0, :, qw + 2 * kw:VAR_W] = jnp.zeros((BLOCK, VAR_W - qw - 2 * kw), BF16)

        if single:
            compute()
            emit(dq_s[...], dk_s[0:BLOCK, :], dv_s[0:BLOCK, :])
            return

        @pl.when(i == 0)
        def _():
            car_q[...] = jnp.zeros_like(car_q)
            car_k[...] = jnp.zeros_like(car_k)
            car_v[...] = jnp.zeros_like(car_v)

        @pl.when(i == nblk)
        def _():
            dk_s[...] = jnp.zeros_like(dk_s)
            dv_s[...] = jnp.zeros_like(dv_s)

        pl.when(i < nblk)(compute)
        emit(car_q[...], car_k[...] + dk_s[0:BLOCK, :], car_v[...] + dv_s[0:BLOCK, :])
        car_q[...] = dq_s[...]
        car_k[...] = dk_s[BLOCK:2 * BLOCK, :]
        car_v[...] = dv_s[BLOCK:2 * BLOCK, :]

    cur = lambda i: jnp.minimum(i, nblk - 1)
    prv = lambda i: jnp.maximum(jnp.minimum(i, nblk - 1) - 1, 0)
    outb = lambda i: jnp.maximum(i - 1, 0)
    qrow = pl.BlockSpec((1, BLOCK, qw), lambda b, j, i: (b, cur(i), j))
    in_specs = [
        pl.BlockSpec((1, BLOCK, qw), lambda b, j, i: (b, cur(i), geo.qidx(j))),
        pl.BlockSpec((1, BLOCK, kw), lambda b, j, i: (b, prv(i), geo.kidx(j))),
        pl.BlockSpec((1, BLOCK, kw), lambda b, j, i: (b, cur(i), geo.kidx(j))),
        pl.BlockSpec((1, BLOCK, kw), lambda b, j, i: (b, prv(i), geo.vidx(j))),
        pl.BlockSpec((1, BLOCK, kw), lambda b, j, i: (b, cur(i), geo.vidx(j))),
        qrow, qrow,
    ]
    ins = [view(qkv, VAR_W)] * 5 + [view(do, qw), view(lse, qw)]
    if has_dlse:
        in_specs.append(qrow)
        ins.append(view(dlse, qw))
    in_specs += [
        pl.BlockSpec((1, BLOCK, PAIR_W), lambda b, j, i: (b, outb(i), j)),
        pl.BlockSpec((1, BLOCK, PAIR_W), lambda b, j, i: (b, outb(i), j)),
        pl.BlockSpec(memory_space=pltpu.SMEM),
    ]
    ins += [view(cosf, PAIR_W), view(sins, PAIR_W), sinks]
    scratch = [pltpu.VMEM((BLOCK, qw), F32), pltpu.VMEM((2 * BLOCK, kw), F32), pltpu.VMEM((2 * BLOCK, kw), F32),
               pltpu.VMEM((BLOCK, qw), F32), pltpu.VMEM((BLOCK, kw), F32), pltpu.VMEM((BLOCK, kw), F32)]
    dqkv, dsink = _pcall(
        body, name=name, grid=(NB, r, nsteps), in_specs=in_specs,
        out_specs=[pl.BlockSpec((1, BLOCK, VAR_W), lambda b, j, i: (b, outb(i), j)),
                   pl.BlockSpec((8, PAIR_W), lambda b, j, i: (0, 0))],
        out_shape=[jax.ShapeDtypeStruct((NB, tsub, r * VAR_W), BF16), jax.ShapeDtypeStruct((8, PAIR_W), F32)],
        scratch_shapes=scratch, compiler_params=_params(("arbitrary", "arbitrary", "arbitrary")),
    )(*ins)
    return dqkv, dsink


class _Rows:
    def __init__(self, N, T, tm):
        self.N, self.tm, self.tpe, self.grid = N, tm, T // tm, (N // tm,)

    def row(self, w, col=0):
        return pl.BlockSpec((self.tm, w), lambda i: (i, col))

    def ex(self, w):
        return pl.BlockSpec((1, 1, w), lambda i: (i // self.tpe, 0, 0))

    def const(self, shape):
        return pl.BlockSpec(shape, lambda i: tuple(0 for _ in shape))

    def view(self, w, r):
        return pl.BlockSpec((None, self.tm // r, r * w), lambda i: (i // self.tpe, i % self.tpe, 0))

    def first_of_example(self):
        return pl.program_id(0) % self.tpe == 0


def _acc(ref, first, val):
    @pl.when(first)
    def _():
        ref[0] = val

    @pl.when(jnp.logical_not(first))
    def _():
        ref[0] += val


def _colsum(v):
    return jnp.sum(v, axis=0, keepdims=True)


def _ln_stats(r):
    mu = jnp.mean(r, axis=-1, keepdims=True)
    xc = r - mu
    var = jnp.mean(xc * xc, axis=-1, keepdims=True)
    rstd = lax.rsqrt(var + LN_EPS)
    return xc * rstd, rstd


def _ln_bwd(dy, xhat, rstd, gain):
    dxh = dy * gain
    return rstd * (dxh - jnp.mean(dxh, axis=-1, keepdims=True) - xhat * jnp.mean(dxh * xhat, axis=-1, keepdims=True))


def _from_view(ref, scr, r):
    if r == 1:
        return ref[...]
    rows, w = ref.shape[0], ref.shape[1] // r
    for j in range(r):
        for c in range(w // LANES):
            scr.at[c][pl.ds(j, rows, stride=r), :] = ref[:, j * w + c * LANES:j * w + (c + 1) * LANES].astype(F32)
    return jnp.concatenate([scr[c] for c in range(w // LANES)], axis=1)


def _to_view(val, ref, scr, r):
    if r == 1:
        ref[...] = val.astype(ref.dtype)
        return
    rows, w = ref.shape[0], ref.shape[1] // r
    for c in range(w // LANES):
        scr[c] = val[:, c * LANES:(c + 1) * LANES]
    for j in range(r):
        for c in range(w // LANES):
            ref[:, j * w + c * LANES:j * w + (c + 1) * LANES] = scr.at[c][pl.ds(j, rows, stride=r), :].astype(ref.dtype)


def _silu_parts(v):
    s = jax.nn.sigmoid(v)
    return v * s, s * (1.0 + v * (1.0 - s))


def _local_step(x, mod, positions, w_in, rest_weights, sinks, ln1_g, ln1_b, ln2_g, ln2_b, target, hook=None):
    hook = hook or (lambda event, **data: None)
    NB, T, D = x.shape
    N = NB * T
    x2 = x.reshape(N, D)
    tgt2 = target.reshape(N, D)
    shift_m, scale_m, gate_m, shift_f, scale_f, gate_f = [mod[:, None, k * D:(k + 1) * D] for k in range(6)]
    cosf, sins = _rope_tables(positions)
    col = jnp.arange(QKV_P)
    vcol = col % VAR_W
    flags = jnp.where(col < VAR_W, vcol < QA_W + KA_W, vcol < 2 * GB_W).astype(F32)[None]
    R = _Rows(N, T, _pick(T, 256))
    sds = jax.ShapeDtypeStruct
    exsum = lambda w=D: sds((NB, 1, w), F32)
    ngrp = len(B_PATTERNS)

    *qkv, u = _inproj(x2, scale_m, shift_m, w_in, cosf, sins, flags, T=T, name="inproj_qkv")
    gates = _mm(u, w_in[QKV_P:], tb=True, out_dtype=BF16, name="inproj_gates")
    oa, la = _attn_fwd(qkv[0], sinks, None, NB=NB, T=T, name="attn_a_fwd")
    oa = oa.reshape(N, QA_W)
    ob_parts = [_attn_fwd(qkv[1 + g], sinks, g, NB=NB, T=T, name=f"attn_b{g}_fwd") for g in range(ngrp)]
    (o1, l1), (o2, l2), (o3, l3) = ob_parts
    w_a, w_b, w_o, w_gu, w_d = rest_weights()
    F = w_d.shape[0]
    dil = [r_ for _, r_ in B_PATTERNS]
    views = [R.view(GB_W, r_) for r_ in dil]
    tokbuf = pltpu.VMEM((GB_W // LANES, R.tm, LANES), F32)

    def merge_fwd(o1r, o2r, o3r, l1r, l2r, l3r, ob_ref, *bufs):
        os_ = [_from_view(ref, bufs[n], dil[n]) for n, ref in enumerate((o1r, o2r, o3r))]
        la, lb, lc = [_from_view(ref, bufs[3 + n], dil[n]) for n, ref in enumerate((l1r, l2r, l3r))]
        mx = jnp.maximum(jnp.maximum(la, lb), lc)
        ea, eb, ec = jnp.exp(la - mx), jnp.exp(lb - mx), jnp.exp(lc - mx)
        ob_ref[...] = ((ea * os_[0] + eb * os_[1] + ec * os_[2]) / (ea + eb + ec)).astype(BF16)

    ob = _pcall(merge_fwd, name="merge_fwd", grid=R.grid, in_specs=views + views, out_specs=R.row(GB_W),
                out_shape=sds((N, GB_W), BF16), scratch_shapes=[tokbuf] * 6,
                compiler_params=_params(("parallel",)))(o1, o2, o3, l1, l2, l3)

    ya = _mm(oa, w_a, out_dtype=BF16, name="branch_a")
    yb = _mm(ob, w_b, b3=True, out_dtype=BF16, name="branch_b")
    f32 = lambda ref: ref[...].astype(F32)

    def gate_fwd(ya_r, yb_r, ga_r, gb_r, mg_ref):
        mg_ref[...] = (jax.nn.sigmoid(f32(ga_r)) * f32(ya_r) + jax.nn.sigmoid(f32(gb_r)) * f32(yb_r)).astype(BF16)

    merged = _pcall(gate_fwd, name="gate_fwd", grid=R.grid, in_specs=[R.row(D), R.row(D), R.row(D, 0), R.row(D, 1)],
                    out_specs=R.row(D), out_shape=sds((N, D), BF16),
                    compiler_params=_params(("parallel",)))(ya, yb, gates, gates)
    y = _mm(merged, w_o, name="out_proj")

    def norm1_fwd(x_r, y_r, gm_r, g_r, b_r, sf_r, hf_r, r1_ref, u2_ref):
        r1 = ALPHA * x_r[...] + (1.0 + gm_r[0]) * y_r[...]
        xhat, _ = _ln_stats(r1)
        x1 = xhat * g_r[...] + b_r[...]
        r1_ref[...] = r1
        u2_ref[...] = (x1 * (1.0 + sf_r[0]) + hf_r[0]).astype(BF16)

    r1, u2 = _pcall(
        norm1_fwd, name="norm1_fwd", grid=R.grid,
        in_specs=[R.row(D), R.row(D), R.ex(D), R.const((1, D)), R.const((1, D)), R.ex(D), R.ex(D)],
        out_specs=[R.row(D)] * 2, out_shape=[sds((N, D), F32), sds((N, D), BF16)],
        compiler_params=_params(("parallel",)))(x2, y, gate_m, ln1_g, ln1_b, scale_f, shift_f)

    tnf = w_gu.shape[2]
    nft = w_gu.shape[0] // 2
    tmf = _pick(N, 512)

    def ffn_up(u_r, wg_r, wu_r, hg_ref, hu_ref, a_ref):
        hg = jnp.dot(u_r[...], wg_r[...], preferred_element_type=F32)
        hu = jnp.dot(u_r[...], wu_r[...], preferred_element_type=F32)
        sl, _ = _silu_parts(hg)
        hg_ref[...] = hg.astype(BF16)
        hu_ref[...] = hu.astype(BF16)
        a_ref[...] = (sl * hu).astype(BF16)

    ftile = pl.BlockSpec((tmf, tnf), lambda j, i: (i, j))
    hg, hu, act = _pcall(
        ffn_up, name="ffn_up", grid=(nft, N // tmf),
        in_specs=[pl.BlockSpec((tmf, D), lambda j, i: (i, 0)), pl.BlockSpec((None, D, tnf), lambda j, i: (j, 0, 0)),
                  pl.BlockSpec((None, D, tnf), lambda j, i: (j + nft, 0, 0))],
        out_specs=[ftile] * 3, out_shape=[sds((N, F), BF16)] * 3,
        compiler_params=_params(("arbitrary", "parallel")))(u2, w_gu, w_gu)
    y2 = _mm(act, w_d, name="ffn_down")

    def norm2_loss_bwd(r1_r, g1_r, b1_r, y2_r, t_r, gf_r, g_r, b_r, dy2_ref, dx1_ref, dgf_ref, dg_ref, db_ref, loss_ref):
        first = R.first_of_example()
        y2v = y2_r[...]
        x1 = _ln_stats(r1_r[...])[0] * g1_r[...] + b1_r[...]
        r2 = ALPHA * x1 + (1.0 + gf_r[0]) * y2v
        xhat, rstd = _ln_stats(r2)
        err = xhat * g_r[...] + b_r[...] - t_r[...]
        dx2 = err * (1.0 / D)
        dr2 = _ln_bwd(dx2, xhat, rstd, g_r[...])
        dy2_ref[...] = ((1.0 + gf_r[0]) * dr2).astype(BF16)
        dx1_ref[...] = ALPHA * dr2
        _acc(dgf_ref, first, _colsum(dr2 * y2v))
        _acc(dg_ref, first, _colsum(dx2 * xhat))
        _acc(db_ref, first, _colsum(dx2))
        part = 0.5 * jnp.sum(jnp.mean(err * err, axis=-1, keepdims=True))
        _acc(loss_ref, first, jnp.broadcast_to(part, (1, 128)))

    dy2, dx1p, dgate_f, dg2, db2, loss_p = _pcall(
        norm2_loss_bwd, name="norm2_loss_bwd", grid=R.grid,
        in_specs=[R.row(D), R.const((1, D)), R.const((1, D)), R.row(D), R.row(D), R.ex(D), R.const((1, D)),
                  R.const((1, D))],
        out_specs=[R.row(D), R.row(D), R.ex(D), R.ex(D), R.ex(D), R.ex(128)],
        out_shape=[sds((N, D), BF16), sds((N, D), F32), exsum(), exsum(), exsum(), exsum(128)],
        compiler_params=_params(("arbitrary",)))(r1, ln1_g, ln1_b, y2, tgt2, gate_f, ln2_g, ln2_b)

    g_wd = _mm(act, dy2, ta=True, out_dtype=BF16, name="ffn_down_dw")

    tmd = _pick(N, 256)

    fchunk = _pick(F, 768)

    def ffn_down_dx(dy_r, wd_r, hg_r, hu_r, dh_ref):
        for t in range(F // fchunk):
            cs = slice(t * fchunk, (t + 1) * fchunk)
            da = lax.dot_general(dy_r[...], wd_r[cs, :], _NT, preferred_element_type=F32)
            sl, dsl = _silu_parts(hg_r[:, cs].astype(F32))
            dh_ref[:, cs] = (da * hu_r[:, cs].astype(F32) * dsl).astype(BF16)
            dh_ref[:, F + t * fchunk:F + (t + 1) * fchunk] = (da * sl).astype(BF16)

    rowd = lambda w_: pl.BlockSpec((tmd, w_), lambda i: (i, 0))
    dh = _pcall(
        ffn_down_dx, name="ffn_down_dx", grid=(N // tmd,),
        in_specs=[rowd(D), pl.BlockSpec((F, D), lambda i: (0, 0)), rowd(F), rowd(F)],
        out_specs=rowd(2 * F), out_shape=sds((N, 2 * F), BF16),
        compiler_params=_params(("parallel",)))(dy2, w_d, hg, hu)
    def ffn_up_dx(dh_r, w_r, o_ref):
        acc = None
        for s_ in range(w_gu.shape[0]):
            part = lax.dot_general(dh_r[:, s_ * tnf:(s_ + 1) * tnf], w_r[s_], _NT, preferred_element_type=F32)
            acc = part if acc is None else acc + part
        o_ref[...] = acc

    du2 = _pcall(
        ffn_up_dx, name="ffn_up_dx", grid=(N // tmf,),
        in_specs=[pl.BlockSpec((tmf, 2 * F), lambda i: (i, 0)), pl.BlockSpec(w_gu.shape, lambda i: (0, 0, 0))],
        out_specs=pl.BlockSpec((tmf, D), lambda i: (i, 0)), out_shape=sds((N, D), F32),
        compiler_params=_params(("parallel",)))(dh, w_gu)
    g_wgu = _mm(u2, dh, ta=True, out3=w_gu.shape[0], out_dtype=BF16, name="ffn_up_dw")

    def norm1_bwd(dx1p_r, du2_r, r1_r, y_r, sf_r, gm_r, g_r, b_r,
                  dxp_ref, dy_ref, dsf_ref, dhf_ref, dgm_ref, dg_ref, db_ref):
        first = R.first_of_example()
        du2v = du2_r[...]
        dx1 = dx1p_r[...] + du2v * (1.0 + sf_r[0])
        xhat, rstd = _ln_stats(r1_r[...])
        dr1 = _ln_bwd(dx1, xhat, rstd, g_r[...])
        dxp_ref[...] = ALPHA * dr1
        dy_ref[...] = ((1.0 + gm_r[0]) * dr1).astype(BF16)
        _acc(dsf_ref, first, _colsum(du2v * (xhat * g_r[...] + b_r[...])))
        _acc(dhf_ref, first, _colsum(du2v))
        _acc(dgm_ref, first, _colsum(dr1 * y_r[...]))
        _acc(dg_ref, first, _colsum(dx1 * xhat))
        _acc(db_ref, first, _colsum(dx1))

    dxp, dy, dscale_f, dshift_f, dgate_m, dg1, db1 = _pcall(
        norm1_bwd, name="norm1_bwd", grid=R.grid,
        in_specs=[R.row(D)] * 4 + [R.ex(D), R.ex(D), R.const((1, D)), R.const((1, D))],
        out_specs=[R.row(D), R.row(D)] + [R.ex(D)] * 5,
        out_shape=[sds((N, D), F32), sds((N, D), BF16)] + [exsum()] * 5,
        compiler_params=_params(("arbitrary",)))(dx1p, du2, r1, y, scale_f, gate_m, ln1_g, ln1_b)

    dmerged = _mm(dy, w_o, tb=True, out_dtype=BF16, name="out_proj_dx")
    g_wo = _mm(merged, dy, ta=True, out_dtype=BF16, name="out_proj_dw")

    def gate_bwd(dm_r, ya_r, yb_r, ga_r, gb_r, dya_ref, dyb_ref, dg_ref):
        dm = f32(dm_r)
        sa, sb = jax.nn.sigmoid(f32(ga_r)), jax.nn.sigmoid(f32(gb_r))
        dya_ref[...] = (dm * sa).astype(BF16)
        dyb_ref[...] = (dm * sb).astype(BF16)
        dg_ref[:, :D] = (dm * f32(ya_r) * sa * (1.0 - sa)).astype(BF16)
        dg_ref[:, D:] = (dm * f32(yb_r) * sb * (1.0 - sb)).astype(BF16)

    dya, dyb, dgates = _pcall(
        gate_bwd, name="gate_bwd", grid=R.grid, in_specs=[R.row(D)] * 3 + [R.row(D, 0), R.row(D, 1)],
        out_specs=[R.row(D), R.row(D), R.row(2 * D)],
        out_shape=[sds((N, D), BF16), sds((N, D), BF16), sds((N, 2 * D), BF16)],
        compiler_params=_params(("parallel",)))(dmerged, ya, yb, gates, gates)

    doa = _mm(dya, w_a, tb=True, out_dtype=BF16, name="branch_a_dx")
    g_wa = _mm(oa, dya, ta=True, out_dtype=BF16, name="branch_a_dw")
    dob = _mm(dyb, w_b, tb=True, b3=True, name="branch_b_dx")
    g_wb = _mm(ob, dyb, ta=True, out3=w_b.shape[0], out_dtype=BF16, name="branch_b_dw")
    hook("rest_grads", g_wa=g_wa, g_wb=g_wb, g_wo=g_wo, g_wgu=g_wgu, g_wd=g_wd)

    seg = (jnp.arange(GB_W)[:, None] // HEAD_DIM == jnp.arange(GB_W)[None, :] // HEAD_DIM).astype(BF16)

    def merge_bwd(dob_r, o1r, o2r, o3r, l1r, l2r, l3r, seg_r, d1, d2, d3, e1, e2, e3, *bufs):
        dob_v = dob_r[...]
        os_ = [_from_view(ref, bufs[n], dil[n]) for n, ref in enumerate((o1r, o2r, o3r))]
        la, lb, lc = [_from_view(ref, bufs[3 + n], dil[n]) for n, ref in enumerate((l1r, l2r, l3r))]
        mx = jnp.maximum(jnp.maximum(la, lb), lc)
        ea, eb, ec = jnp.exp(la - mx), jnp.exp(lb - mx), jnp.exp(lc - mx)
        inv = 1.0 / (ea + eb + ec)
        ws = [ea * inv, eb * inv, ec * inv]

        def headsum(v):
            hi = v.astype(BF16)
            r1_ = v - hi.astype(F32)
            mid = r1_.astype(BF16)
            lo = (r1_ - mid.astype(F32)).astype(BF16)
            sm = seg_r[...]
            return (jnp.dot(hi, sm, preferred_element_type=F32) + jnp.dot(mid, sm, preferred_element_type=F32)
                    + jnp.dot(lo, sm, preferred_element_type=F32))

        dws = [headsum(dob_v * o) for o in os_]
        mean = ws[0] * dws[0] + ws[1] * dws[1] + ws[2] * dws[2]
        for n, (w_, dw_, d_ref, e_ref) in enumerate(zip(ws, dws, (d1, d2, d3), (e1, e2, e3))):
            _to_view(w_ * dob_v, d_ref, bufs[6], dil[n])
            _to_view(w_ * (dw_ - mean), e_ref, bufs[7], dil[n])

    vshape = lambda r_, dt: sds((NB, T // r_, r_ * GB_W), dt)
    mb = _pcall(
        merge_bwd, name="merge_bwd", grid=R.grid, in_specs=[R.row(GB_W)] + views + views + [R.const((GB_W, GB_W))],
        out_specs=views + views, out_shape=[vshape(r_, BF16) for r_ in dil] + [vshape(r_, F32) for r_ in dil],
        scratch_shapes=[tokbuf] * 8, compiler_params=_params(("parallel",)))(dob, o1, o2, o3, l1, l2, l3, seg)
    do_b, dlse_b = mb[:3], mb[3:]
    hook("merge_bwd_done")

    dqkv_a, dsink = _attn_bwd(qkv[0], doa, la, None, cosf, sins, sinks, None, NB=NB, T=T, name="attn_a_bwd")
    hook("attn_a_bwd_done")
    dqkv = [dqkv_a]
    for g in range(ngrp):
        dqkv.append(_attn_bwd(qkv[1 + g], do_b[g], (l1, l2, l3)[g], dlse_b[g], cosf, sins, sinks, g, NB=NB, T=T,
                              name=f"attn_b{g}_bwd")[0])
        hook(f"attn_b{g}_bwd_done")

    g_win = [_mm(d3.reshape(N, VAR_W), u, ta=True, out_dtype=BF16, name=f"inproj_dw{v}") if VAR_DIL[v] == 1
             else _dw_view(d3, u, VAR_DIL[v], name=f"inproj_dw{v}") for v, d3 in enumerate(dqkv)]
    g_win.append(_mm(dgates, u, ta=True, out_dtype=BF16, name=f"inproj_dw{N_VAR}"))
    hook("win_grads", g_win=g_win)
    wvar = lambda v: (w_in, (VAR_W, D), (v, 0))
    dview = lambda v: (dqkv[v], VAR_DIL[v])
    du = _mm_multi([dview(0)], [wvar(0)], M=N, T=T, name="inproj_dx0")
    hook("inproj_dx0_done")
    du = _mm_multi([dview(v) for v in range(1, N_VAR)] + [dgates],
                   [wvar(v) for v in range(1, N_VAR)] + [(w_in, (2 * D, D), (QKV_P // (2 * D), 0))],
                   M=N, T=T, add=du, tm=256, name="inproj_dx1")
    hook("inproj_dx1_done")

    def x_bwd(dxp_r, du_r, x_r, sm_r, gx_ref, dsm_ref, dhm_ref):
        first = R.first_of_example()
        duv = du_r[...]
        gx_ref[...] = dxp_r[...] + duv * (1.0 + sm_r[0])
        _acc(dsm_ref, first, _colsum(duv * x_r[...]))
        _acc(dhm_ref, first, _colsum(duv))

    gx, dscale_m, dshift_m = _pcall(
        x_bwd, name="x_bwd", grid=R.grid, in_specs=[R.row(D)] * 3 + [R.ex(D)],
        out_specs=[R.row(D), R.ex(D), R.ex(D)], out_shape=[sds((N, D), F32), exsum(), exsum()],
        compiler_params=_params(("arbitrary",)))(dxp, du, x2, scale_m)
    hook("x_bwd_done")

    dmod =jnp.concatenate([dshift_m, dscale_m, dgate_m, dshift_f, dscale_f, dgate_f], axis=-1)[:, 0]
    ln_grads = jnp.concatenate([dg1, db1, dg2, db2], axis=1)
    return dict(loss=loss_p[:, 0, 0], grad_x=gx.reshape(NB, T, D), g_win=g_win, g_wa=g_wa, g_wb=g_wb, g_wo=g_wo,
                g_wgu=g_wgu, g_wd=g_wd, dmod=dmod, ln_grads=ln_grads, dsink=dsink[0, :A_Q_HEADS])


def _coords():
    return lax.axis_index("x"), lax.axis_index("y"), lax.axis_index("c")


def _allgather_small(blk, *, name):
    m_per, n = blk.shape

    def body(x_ref, out_ref, send_sems, recv_sems, local_sem):
        x, y, c = _coords()
        me, sibling = (x, y, c), (x, y, 1 - c)
        chips = [(1 - x, y), (x, 1 - y), (1 - x, 1 - y)]

        def rows(px, py, pc):
            return out_ref.at[pl.ds((4 * px + 2 * py + pc) * m_per, m_per), :]

        def copy(k, block, to, src=None):
            return pltpu.make_async_remote_copy(
                src_ref=rows(*block) if src is None else src, dst_ref=rows(*block),
                send_sem=send_sems.at[k], recv_sem=recv_sems.at[k], device_id=to, device_id_type=MESH)

        mine = pltpu.make_async_copy(x_ref, rows(*me), local_sem)
        mine.start()
        first = [copy(0, me, sibling, src=x_ref)]
        first += [copy(1 + j, me, (*chip, c), src=x_ref) for j, chip in enumerate(chips)]
        for cp in first:
            cp.start()
        passed = [copy(4 + j, (*chip, c), sibling) for j, chip in enumerate(chips)]
        for j, chip in enumerate(chips):
            copy(1 + j, (*chip, c), me).wait_recv()
            passed[j].start()
        copy(0, sibling, me).wait_recv()
        for j, chip in enumerate(chips):
            copy(4 + j, (*chip, 1 - c), me).wait_recv()
        for cp in first + passed:
            cp.wait_send()
        mine.wait()

    return _pcall(
        body, name=name, out_shape=jax.ShapeDtypeStruct((8 * m_per, n), blk.dtype),
        in_specs=[pl.BlockSpec(memory_space=pltpu.VMEM)], out_specs=pl.BlockSpec(memory_space=pltpu.VMEM),
        scratch_shapes=[pltpu.SemaphoreType.DMA((7,)), pltpu.SemaphoreType.DMA((7,)), pltpu.SemaphoreType.DMA],
        compiler_params=pltpu.CompilerParams(vmem_limit_bytes=VMEM_LIMIT_BYTES),
    )(blk)


def _exchange(srcs, dsts, plan, *, name, dst_inits=None):
    na = len(dsts)
    nrem = len(plan(0, 0, 0))

    def body(*refs):
        refs = list(refs)
        src_refs = [refs.pop(0) for _ in range(na)] if srcs is not None else None
        if dst_inits is not None:
            del refs[:na]
        dst_refs, (send_sems, recv_sems) = refs[:na], refs[na:]
        start, wait = _copies(dst_refs if src_refs is None else src_refs, dst_refs, send_sems, recv_sems, plan)
        start()
        wait()

    hbm = pl.BlockSpec(memory_space=pl.ANY)
    ins = (list(srcs) if srcs is not None else []) + (list(dst_inits) if dst_inits is not None else [])
    base = na if srcs is not None else 0
    aliases = {base + a: a for a in range(na)} if dst_inits is not None else {}
    return _pcall(
        body, name=name, out_shape=list(dsts), in_specs=[hbm] * len(ins), out_specs=[hbm] * na,
        input_output_aliases=aliases,
        scratch_shapes=[pltpu.SemaphoreType.DMA((na * nrem,)), pltpu.SemaphoreType.DMA((na * nrem,))],
    )(*ins)


def _other_chips(x, y):
    return [(1 - x, y), (x, 1 - y), (1 - x, 1 - y)]


def _round(ride, carrier, name):
    if carrier is not None:
        _RIDES.setdefault(carrier, []).append(ride)
        return
    srcs = ride.srcs() if callable(ride.srcs) else ride.srcs
    inits = ride.dst_inits() if callable(ride.dst_inits) else ride.dst_inits
    ride.out = list(_exchange(srcs, ride.dsts, ride.plan, name=name, dst_inits=inits))


class _Gather:
    def __init__(self, shards, chip, tag, carriers=(None, None)):
        def plan_ici(x, y, c):
            k = 2 * x + y
            return [((c,), (k, c), (2 * px + py, c), (px, py, c)) for px, py in _other_chips(x, y)]

        def plan_d2d(x, y, c):
            return [((2 * px + py, c), (2 * px + py, c), (2 * px + py, 1 - c), (x, y, 1 - c))
                    for px, py in _other_chips(x, y)]

        self.shards, self.chip = shards, chip
        dsts = [jax.ShapeDtypeStruct((4,) + s.shape, s.dtype) for s in shards]
        ici = _Ride(shards, dsts, plan_ici)
        self.d2d = _Ride(None, dsts, plan_d2d, dst_inits=lambda: ici.out)
        _round(ici, carriers[0], f"gather_{tag}_ici")
        _round(self.d2d, carriers[1], f"gather_{tag}_d2d")

    def result(self):
        full = [lax.dynamic_update_index_in_dim(f, s, self.chip, 0) for f, s in zip(self.d2d.out, self.shards)]
        return [f.reshape((4, 2 * f.shape[2], f.shape[3])) for f in full]


def _index_operand(i):
    return jnp.reshape(i, (1,)).astype(jnp.int32)


def _add_pairs(g, f, ci, *, name):
    s, _, hr, wd = g.shape
    tr = _pick(hr, 600, 16)

    def body(c_ref, a_ref, b_ref, o_ref):
        o_ref[...] = (a_ref[...].astype(F32) + b_ref[...].astype(F32)).astype(BF16)

    spec = pl.BlockSpec((1, tr, wd), lambda j, i, c: (j, i, 0))
    grid_spec = pltpu.PrefetchScalarGridSpec(
        num_scalar_prefetch=1, grid=(s, hr // tr),
        in_specs=[pl.BlockSpec((1, None, tr, wd), lambda j, i, c: (j, c[0], i, 0)), spec], out_specs=spec)
    return _pcall(body, name=name, grid_spec=grid_spec, out_shape=jax.ShapeDtypeStruct(f.shape, BF16),
                  compiler_params=_params(("parallel", "parallel")))(_index_operand(ci), g, f)


def _sum_chips(landed, pairs, chip, *, name):
    s, hr, wd = landed.shape
    tr = _pick(hr, 600, 16)

    def body(k_ref, l_ref, p_ref, o_ref):
        acc = None
        for k in range(s):
            part = jnp.where(k_ref[0] == k, p_ref[k], l_ref[k]).astype(F32)
            acc = part if acc is None else acc + part
        o_ref[...] = acc

    spec = pl.BlockSpec((s, tr, wd), lambda i, k: (0, i, 0))
    grid_spec = pltpu.PrefetchScalarGridSpec(
        num_scalar_prefetch=1, grid=(hr // tr,), in_specs=[spec, spec],
        out_specs=pl.BlockSpec((tr, wd), lambda i, k: (i, 0)))
    return _pcall(body, name=name, grid_spec=grid_spec, out_shape=jax.ShapeDtypeStruct((hr, wd), F32),
                  compiler_params=_params(("parallel",)))(_index_operand(chip), landed, pairs)


class _ReduceScatter:
    def __init__(self, gs, chip, ci, tag):
        self.gs, self.chip, self.ci, self.tag = gs, chip, ci, tag
        self.half_t = [jax.ShapeDtypeStruct((g.shape[0],) + g.shape[2:], BF16) for g in gs]

    def pair(self, carrier=None):
        plan = lambda x, y, c: [((slice(None), 1 - c), (), (), (x, y, 1 - c))]
        self.r1 = _Ride(self.gs, self.half_t, plan)
        _round(self.r1, carrier, f"reduce_{self.tag}_pair")

    def chips(self, carrier=None):
        def plan(x, y, c):
            k = 2 * x + y
            return [((2 * px + py,), (k,), (2 * px + py,), (px, py, c)) for px, py in _other_chips(x, y)]

        self.pairs = [_add_pairs(g, f, self.ci, name=f"reduce_{self.tag}_pair_add{n}")
                      for n, (g, f) in enumerate(zip(self.gs, self.r1.out))]
        self.r2 = _Ride(self.pairs, self.half_t, plan)
        _round(self.r2, carrier, f"reduce_{self.tag}_chips")

    def halves(self, carrier=None):
        plan = lambda x, y, c: [((), (c,), (1 - c,), (x, y, 1 - c))]
        self.mine = [_sum_chips(l, p, self.chip, name=f"reduce_{self.tag}_chip_sum{n}")
                     for n, (l, p) in enumerate(zip(self.r2.out, self.pairs))]
        self.r3 = _Ride(self.mine, [jax.ShapeDtypeStruct((2,) + m.shape, F32) for m in self.mine], plan)
        _round(self.r3, carrier, f"reduce_{self.tag}_halves")

    def result(self):
        return [lax.dynamic_update_index_in_dim(b, m, self.ci, 0).reshape(2 * m.shape[0], m.shape[1])
                for b, m in zip(self.r3.out, self.mine)]


def _ada_fwd(c_all, w_sh, b_sh, *, name):
    nb, d = c_all.shape
    wcols = w_sh.shape[1]
    tn = _pick(wcols, 512)

    def body(c_ref, w_ref, b_ref, o_ref, a_ref):
        cv = c_ref[...]
        act = cv * jax.nn.sigmoid(cv)
        a_ref[...] = act
        o_ref[...] = jnp.dot(act.astype(BF16), w_ref[...].astype(BF16), preferred_element_type=F32) + b_ref[...]

    return _pcall(
        body, name=name, grid=(wcols // tn,),
        in_specs=[pl.BlockSpec((nb, d), lambda j: (0, 0)), pl.BlockSpec((d, tn), lambda j: (0, j)),
                  pl.BlockSpec((1, tn), lambda j: (0, j))],
        out_specs=[pl.BlockSpec((nb, tn), lambda j: (0, j)), pl.BlockSpec((nb, d), lambda j: (0, 0))],
        out_shape=[jax.ShapeDtypeStruct((nb, wcols), F32), jax.ShapeDtypeStruct((nb, d), F32)],
        compiler_params=_params(("arbitrary",)))(c_all, w_sh, b_sh)


def _sum_devices(g, *, name):
    nd, m, w = g.shape

    def body(g_ref, o_ref):
        acc = g_ref[0]
        for k in range(1, nd):
            acc = acc + g_ref[k]
        o_ref[...] = acc

    return _pcall(body, name=name, out_shape=jax.ShapeDtypeStruct((m, w), F32),
                  compiler_params=pltpu.CompilerParams(vmem_limit_bytes=VMEM_LIMIT_BYTES))(g)


def _adamw(w, g, m, v, *, name):
    rows, cols = w.shape[-2:]
    tr = _pick(rows, max(8, (1 << 18) // cols), 8)
    c1 = 1.0 / (1.0 - ADAM_B1 ** ADAM_STEP)
    c2 = 1.0 / (1.0 - ADAM_B2 ** ADAM_STEP)

    def body(w_ref, g_ref, m_ref, v_ref, d_ref, nm_ref, nv_ref):
        gv = g_ref[...]
        nm = ADAM_B1 * m_ref[...] + (1.0 - ADAM_B1) * gv
        nv = ADAM_B2 * v_ref[...] + (1.0 - ADAM_B2) * (gv * gv)
        d_ref[...] = -ADAM_LR * ((nm * c1) / (jnp.sqrt(nv * c2) + ADAM_EPS) + ADAM_WD * w_ref[...])
        nm_ref[...] = nm
        nv_ref[...] = nv

    gspec = pl.BlockSpec((tr, cols), lambda i: (i, 0))
    spec = pl.BlockSpec((None, tr, cols), lambda i: (0, i, 0)) if w.ndim == 3 else gspec
    shp = jax.ShapeDtypeStruct(w.shape, F32)
    return _pcall(body, name=name, grid=(rows // tr,), in_specs=[spec, gspec, spec, spec], out_specs=[spec] * 3,
                  out_shape=[shp] * 3, compiler_params=_params(("parallel",)))(w, g, m, v)


def _permute_in_rows(wt):
    ngrp = len(B_PATTERNS)
    qb, kb, vb = (wt[A_W + n * QB_W:A_W + (n + 1) * QB_W] for n in range(3))
    parts = [wt[:A_W], jnp.zeros((VAR_W - A_W, wt.shape[1]), wt.dtype)]
    for g in range(ngrp):
        parts += [t[g * GB_W:(g + 1) * GB_W] for t in (qb, kb, vb)]
    return jnp.concatenate(parts + [wt[A_W + 3 * QB_W:]], axis=0)


def _unpermute_in_grads(pieces):
    ga, groups, gg = pieces[0], pieces[1:-1], pieces[-1]
    rows = [ga[:A_W]]
    for n in range(3):
        rows += [gp[n * GB_W:(n + 1) * GB_W] for gp in groups]
    return jnp.concatenate(rows + [gg], axis=0)


def kernel(x, c, positions, w_ada, b_ada, w_in, sinks, w_branch_a, w_branch_b, w_o, ln1_g, ln1_b, w_gate_up, w_down, ln2_g, ln2_b, loss_target, m_w_ada, m_b_ada, m_w_in, m_sinks, m_w_branch_a, m_w_branch_b, m_w_o, m_ln1_g, m_ln1_b, m_w_gate_up, m_w_down, m_ln2_g, m_ln2_b, v_w_ada, v_b_ada, v_w_in, v_sinks, v_w_branch_a, v_w_branch_b, v_w_o, v_ln1_g, v_ln1_b, v_w_gate_up, v_w_down, v_ln2_g, v_ln2_b):
    xi, yi, ci = _coords()
    chip = 2 * xi + yi
    dev = 4 * xi + 2 * yi + ci
    NB, T, D = x.shape
    nchip, ndev = 4, 8
    ada_cols = w_ada.shape[2]

    c_blk = jnp.zeros((8, D), F32).at[:NB].set(c)
    c_all = _allgather_small(c_blk, name="gather_c").reshape(ndev, 8, D)[:, :NB].reshape(ndev * NB, D)
    b_sh = lax.dynamic_slice(b_ada, (0, chip * ada_cols), (1, ada_cols))
    mod_part, c_act = _ada_fwd(c_all, w_ada[0], b_sh, name="ada_fwd")
    mod_g = _allgather_small(mod_part, name="gather_mod").reshape(nchip, 2, ndev * NB, ada_cols)[:, 0]
    mod_all = jnp.transpose(mod_g, (1, 0, 2)).reshape(ndev * NB, nchip * ada_cols)
    mod = lax.dynamic_slice(mod_all, (NB * dev, 0), (NB, nchip * ada_cols))

    ra, ro, rd = w_branch_a.shape[1], w_o.shape[1], w_down.shape[1]
    rowsh = jnp.concatenate([w_branch_a[0], w_o[0], w_down[0]], axis=0)
    halves = lambda a: a.reshape(a.shape[:-2] + (2, a.shape[-2] // 2, a.shape[-1]))
    whole = lambda a: a.reshape(a.shape[:-3] + (2 * a.shape[-2], a.shape[-1]))
    tr = lambda a: jnp.swapaxes(a, -1, -2)
    shards = [halves(w.astype(BF16)) for w in (tr(w_in[0]), rowsh, w_branch_b[0], w_gate_up[0])]
    (g_in,) = _Gather(shards[:1], chip, "w_in").result()
    w_in_f = _permute_in_rows(g_in.reshape(nchip * g_in.shape[1], D))
    mix = _Gather(shards[1:3], chip, "w_mix", carriers=("inproj_qkv", "attn_a_fwd"))
    ffn = _Gather(shards[3:], chip, "w_ffn", carriers=("attn_a_fwd", "attn_b0_fwd"))

    def rest_weights():
        (g_rows, w_b_f), (w_gu_f,) = mix.result(), ffn.result()
        return (g_rows[:, :ra].reshape(nchip * ra, D), w_b_f, g_rows[:, ra:ra + ro].reshape(nchip * ro, D), w_gu_f,
                g_rows[:, ra + ro:].reshape(nchip * rd, D))

    red = {}

    def hook(event, **g):
        if event == "rest_grads":
            gr_rows = jnp.concatenate([g["g_wa"].reshape(nchip, ra, D), g["g_wo"].reshape(nchip, ro, D),
                                       g["g_wd"].reshape(nchip, rd, D)], axis=1)
            red["ffn"] = _ReduceScatter([halves(g["g_wgu"])], chip, ci, "ffn")
            red["mix"] = _ReduceScatter([halves(gr_rows), halves(g["g_wb"])], chip, ci, "mix")
            red["ffn"].pair(carrier="merge_bwd")
            red["mix"].pair(carrier="merge_bwd")
        elif event == "merge_bwd_done":
            red["ffn"].chips(carrier="attn_a_bwd")
            red["mix"].chips(carrier="attn_b0_bwd")
        elif event == "attn_a_bwd_done":
            red["ffn"].halves(carrier="attn_b0_bwd")
        elif event == "attn_b0_bwd_done":
            red["mix"].halves(carrier="attn_b1_bwd")
        elif event == "win_grads":
            gr_in = _unpermute_in_grads(g["g_win"])
            red["w_in"] = _ReduceScatter([halves(gr_in.reshape(nchip, gr_in.shape[0] // nchip, D))], chip, ci, "w_in")
            red["w_in"].pair(carrier="inproj_dx0")
        elif event == "inproj_dx0_done":
            red["w_in"].chips(carrier="inproj_dx1")
        elif event == "inproj_dx1_done":
            red["w_in"].halves(carrier="x_bwd")

    res = _local_step(x, mod, positions, w_in_f, rest_weights, sinks[0], ln1_g, ln1_b, ln2_g, ln2_b, loss_target, hook)
    (g_w_in,) = red["w_in"].result()
    (g_rows_red, g_w_b), (g_w_gu,) = red["mix"].result(), red["ffn"].result()
    g_w_a, g_w_o, g_w_d = g_rows_red[:ra], g_rows_red[ra:ra + ro], g_rows_red[ra + ro:]

    small_rows = 24
    misc = jnp.zeros((1, D), F32).at[0, :A_Q_HEADS].set(res["dsink"]).at[0, A_Q_HEADS].set(jnp.sum(res["loss"]))
    small = jnp.concatenate([res["dmod"].reshape(NB * 6, D), jnp.sum(res["ln_grads"], axis=0), misc,
                             jnp.zeros((small_rows - NB * 6 - 5, D), F32)], axis=0)
    small_all = _allgather_small(small, name="gather_small").reshape(ndev, small_rows, D)
    dmod_all = small_all[:, :NB * 6].reshape(ndev * NB, 6 * D)
    sums = _sum_devices(small_all, name="sum_small")
    g_b_ada = (sums[0:6] + sums[6:12]).reshape(1, 6 * D)
    g_ln1_g, g_ln1_b, g_ln2_g, g_ln2_b = (sums[12 + n][None] for n in range(4))
    g_sinks = sums[16, :A_Q_HEADS][None]
    loss = sums[16, A_Q_HEADS]
    dmod_sh = lax.dynamic_slice(dmod_all, (0, chip * ada_cols), (ndev * NB, ada_cols))
    g_w_ada = _mm(c_act, dmod_sh, ta=True, name="ada_dw")

    names = ["w_ada", "b_ada", "w_in", "sinks", "w_branch_a", "w_branch_b", "w_o", "ln1_g", "ln1_b",
             "w_gate_up", "w_down", "ln2_g", "ln2_b"]
    ws = [w_ada, b_ada, w_in, sinks, w_branch_a, w_branch_b, w_o, ln1_g, ln1_b, w_gate_up, w_down, ln2_g, ln2_b]
    ms = [m_w_ada, m_b_ada, m_w_in, m_sinks, m_w_branch_a, m_w_branch_b, m_w_o, m_ln1_g, m_ln1_b, m_w_gate_up,
          m_w_down, m_ln2_g, m_ln2_b]
    vs = [v_w_ada, v_b_ada, v_w_in, v_sinks, v_w_branch_a, v_w_branch_b, v_w_o, v_ln1_g, v_ln1_b, v_w_gate_up,
          v_w_down, v_ln2_g, v_ln2_b]
    gs = [g_w_ada, g_b_ada, g_w_in, g_sinks, g_w_a, g_w_b, g_w_o, g_ln1_g, g_ln1_b, g_w_gu, g_w_d, g_ln2_g, g_ln2_b]
    grads, deltas, new_ms, new_vs = [], [], [], []
    for name, w, g, m, v in zip(names, ws, gs, ms, vs):
        flip = tr if name == "w_in" else (lambda a: a)
        w, m, v = flip(w), flip(m), flip(v)
        g2 = g.reshape(w.shape[-2:])
        d, nm, nv = _adamw(w, g2, m, v, name="adamw_" + name)
        grads.append(flip(g2.reshape(w.shape)))
        deltas.append(flip(d))
        new_ms.append(flip(nm))
        new_vs.append(flip(nv))
    return (loss, res["grad_x"], *grads, *deltas, *new_ms, *new_vs)
```

```python
import functools

import jax
import jax.numpy as jnp
from jax import lax
from jax.experimental import pallas as pl
from jax.experimental.pallas import tpu as pltpu

F32 = jnp.float32
BF16 = jnp.bfloat16
MESH = pl.DeviceIdType.MESH

HEAD_DIM = 64
LANES = 128
PAIR_W = 2 * HEAD_DIM
BLOCK = 128
A_Q_HEADS = 16
A_KV_HEADS = 2
A_WINDOW = 128
B_PATTERNS = ((128, 1), (512, 4), (2048, 16))
B_GROUP_HEADS = 8
QA_W = A_Q_HEADS * HEAD_DIM
KA_W = A_KV_HEADS * HEAD_DIM
GB_W = B_GROUP_HEADS * HEAD_DIM
QB_W = GB_W * len(B_PATTERNS)
A_W = QA_W + 2 * KA_W
VAR_W = 3 * GB_W
N_VAR = 1 + len(B_PATTERNS)
VAR_DIL = (1,) + tuple(r for _, r in B_PATTERNS)
QKV_P = N_VAR * VAR_W
ROPE_THETA = 10000.0
LN_EPS = 1e-5
NEG_INF = -1e30
DEPTH = 1
ALPHA = (2 * DEPTH) ** 0.25
SCALE = HEAD_DIM ** -0.5

ADAM_LR, ADAM_B1, ADAM_B2, ADAM_EPS, ADAM_WD, ADAM_STEP = 0.001, 0.9, 0.999, 1e-08, 0.01, 10

VMEM_LIMIT_BYTES = 56 * 1024 * 1024
MM_TILE_BYTES = 36 * 1024 * 1024
MM_WHOLE_K = 4096


def _params(sem=None):
    return pltpu.CompilerParams(dimension_semantics=sem, vmem_limit_bytes=VMEM_LIMIT_BYTES)


_RIDES = {}


def _pcall(body, *, name, **kw):
    rides = _RIDES.pop(name, None)
    if rides is None:
        return pl.pallas_call(body, name=name, **kw)
    return _riding_call(body, rides, name=name, **kw)


def _copies(src_refs, dst_refs, send_sems, recv_sems, plan):
    x, y, c = lax.axis_index("x"), lax.axis_index("y"), lax.axis_index("c")
    remote = plan(x, y, c)
    nrem = len(remote)
    at = lambda ref, idx: ref.at[idx] if idx else ref

    def copy(a, n, landing):
        si, di, ri, peer = remote[n]
        return pltpu.make_async_remote_copy(
            src_ref=at(src_refs[a], si), dst_ref=at(dst_refs[a], ri if landing else di),
            send_sem=send_sems.at[a * nrem + n], recv_sem=recv_sems.at[a * nrem + n],
            device_id=peer, device_id_type=MESH)

    order = [(a, n) for a in range(len(dst_refs)) for n in range(nrem)]

    def start():
        for a, n in order:
            copy(a, n, False).start()

    def wait():
        for a, n in order:
            copy(a, n, True).wait_recv()
        for a, n in order:
            copy(a, n, False).wait_send()

    return start, wait


class _Ride:
    def __init__(self, srcs, dsts, plan, dst_inits=None):
        self.srcs, self.dsts, self.plan, self.dst_inits, self.out = srcs, dsts, plan, dst_inits, None


def _riding_call(body, rides, *, name, in_specs, out_specs, out_shape, grid=(), scratch_shapes=(), **kw):
    single = not isinstance(out_specs, (list, tuple))
    out_specs = [out_specs] if single else list(out_specs)
    out_shape = [out_shape] if single else list(out_shape)
    n_in, n_out, n_scr = len(in_specs), len(out_specs), len(scratch_shapes)
    xin, xdsts, sems, aliases, layout = [], [], [], {}, []
    for ride in rides:
        srcs = ride.srcs() if callable(ride.srcs) else ride.srcs
        inits = ride.dst_inits() if callable(ride.dst_inits) else ride.dst_inits
        na, nrem = len(ride.dsts), len(ride.plan(0, 0, 0))
        src_at = len(xin) if srcs is not None else None
        xin += list(srcs) if srcs is not None else []
        if inits is not None:
            aliases.update({n_in + len(xin) + a: n_out + len(xdsts) + a for a in range(na)})
            xin += list(inits)
        layout.append((src_at, len(xdsts), na))
        xdsts += list(ride.dsts)
        sems += [pltpu.SemaphoreType.DMA((na * nrem,)), pltpu.SemaphoreType.DMA((na * nrem,))]

    def wrapped(*refs):
        ins, xins = refs[:n_in], refs[n_in:n_in + len(xin)]
        outs = refs[n_in + len(xin):n_in + len(xin) + n_out]
        xouts = refs[n_in + len(xin) + n_out:n_in + len(xin) + n_out + len(xdsts)]
        scr = refs[n_in + len(xin) + n_out + len(xdsts):]
        rounds = []
        for k, (ride, (src_at, dst_at, na)) in enumerate(zip(rides, layout)):
            dsts = xouts[dst_at:dst_at + na]
            srcs = dsts if src_at is None else xins[src_at:src_at + na]
            rounds.append(_copies(srcs, dsts, scr[n_scr + 2 * k], scr[n_scr + 2 * k + 1], ride.plan))
        ids = [pl.program_id(a) for a in range(len(grid))]
        first = functools.reduce(jnp.logical_and, [i == 0 for i in ids], True)
        last = functools.reduce(jnp.logical_and, [i == g - 1 for i, g in zip(ids, grid)], True)

        def start_all():
            for start, _ in rounds:
                start()

        def wait_all():
            for _, wait in rounds:
                wait()

        start_all() if not grid else pl.when(first)(start_all)
        body(*ins, *outs, *scr[:n_scr])
        wait_all() if not grid else pl.when(last)(wait_all)

    hbm = pl.BlockSpec(memory_space=pl.ANY)
    gridkw = dict(grid=grid) if grid else {}

    def run(*args):
        res = pl.pallas_call(
            wrapped, name=name, in_specs=list(in_specs) + [hbm] * len(xin),
            out_specs=out_specs + [hbm] * len(xdsts), out_shape=out_shape + xdsts,
            scratch_shapes=list(scratch_shapes) + sems, input_output_aliases=aliases,
            compiler_params=_params(("arbitrary",) * len(grid) if grid else None), **gridkw,
        )(*args, *xin)
        for ride, (_, dst_at, na) in zip(rides, layout):
            ride.out = list(res[n_out + dst_at:n_out + dst_at + na])
        return res[0] if single else list(res[:n_out])

    return run


def _pick(n, target, quantum=128):
    t = (min(target, n) // quantum) * quantum
    while t >= quantum:
        if n % t == 0:
            return t
        t -= quantum
    return n


def _mm(a, b, *, name, ta=False, tb=False, b3=False, out3=0, out_dtype=F32, add=None, tm=1024, tn=1536, tk=1536):
    if ta:
        K, M = a.shape
    else:
        M, K = a.shape
    if b3 and tb:
        Nn, K2, tk = b.shape[1], b.shape[0] * b.shape[2], b.shape[2]
    elif b3:
        K2, Nn, tn = b.shape[1], b.shape[0] * b.shape[2], b.shape[2]
    elif tb:
        Nn, K2 = b.shape
    else:
        K2, Nn = b.shape
    assert K == K2, (a.shape, b.shape)
    if out3:
        tn = Nn // out3
    tm, tn, tk = _pick(M, tm), _pick(Nn, tn), _pick(K, tk)
    if not (b3 and tb) and K <= MM_WHOLE_K:
        tk = K
        fits = lambda: 4 * tk * (tm + tn) + 8 * tm * tn * (2 if add is not None else 1) <= MM_TILE_BYTES
        while not fits():
            if (tm >= tn or b3 or out3) and tm > 256:
                tm = _pick(M, tm - 128)
            elif not (b3 or out3) and tn > 256:
                tn = _pick(Nn, tn - 128)
            else:
                break
    nk = K // tk
    j_outer = K * Nn + (Nn // tn) * M * K < M * K + (M // tm) * K * Nn
    dn = (((0 if ta else 1,), (1 if tb else 0,)), ((), ()))

    def body(*refs):
        refs = list(refs)
        a_ref, b_ref = refs[:2]
        add_ref = refs[2] if add is not None else None
        o_ref = refs[3] if add is not None else refs[2]
        part = lax.dot_general(a_ref[...].astype(BF16), b_ref[...].astype(BF16), dn, preferred_element_type=F32)

        def finish(r):
            if add is not None:
                r = r + add_ref[...]
            o_ref[...] = r.astype(out_dtype)

        if nk == 1:
            finish(part)
            return
        acc = refs[-1]
        k = pl.program_id(2)

        @pl.when(k == 0)
        def _():
            acc[...] = part

        @pl.when(k > 0)
        def _():
            acc[...] += part

        @pl.when(k == nk - 1)
        def _():
            finish(acc[...])

    def spec(shape, index):
        return pl.BlockSpec(shape, (lambda j, i, k: index(i, j, k)) if j_outer else index)

    a_spec = spec((tk, tm), lambda i, j, k: (k, i)) if ta else spec((tm, tk), lambda i, j, k: (i, k))
    if b3 and tb:
        b_spec = spec((None, tn, tk), lambda i, j, k: (k, j, 0))
    elif b3:
        b_spec = spec((None, tk, tn), lambda i, j, k: (j, k, 0))
    elif tb:
        b_spec = spec((tn, tk), lambda i, j, k: (j, k))
    else:
        b_spec = spec((tk, tn), lambda i, j, k: (k, j))
    if out3:
        o_spec = spec((None, tm, tn), lambda i, j, k: (j, i, 0))
    else:
        o_spec = spec((tm, tn), lambda i, j, k: (i, j))
    ins, specs = [a, b], [a_spec, b_spec]
    if add is not None:
        ins.append(add)
        specs.append(o_spec)
    grid = (Nn // tn, M // tm, nk) if j_outer else (M // tm, Nn // tn, nk)
    return _pcall(
        body, name=name, grid=grid, in_specs=specs, out_specs=o_spec,
        out_shape=jax.ShapeDtypeStruct((out3, M, tn) if out3 else (M, Nn), out_dtype),
        scratch_shapes=[pltpu.VMEM((tm, tn), F32)] if nk > 1 else [],
        compiler_params=_params(("parallel", "parallel", "arbitrary")),
    )(*ins)


def _mm_multi(a_list, b_list, *, name, M, T=None, add=None, out_dtype=F32, tm=512):
    tm = _pick(T or M, tm)
    ns = len(a_list)
    dils = [a[1] if isinstance(a, tuple) else 0 for a in a_list]
    a_arrs = [a[0] if isinstance(a, tuple) else a for a in a_list]
    widths = [a.shape[-1] // max(r, 1) for a, r in zip(a_arrs, dils)]
    b_arrs, b_specs = [], []
    for b in b_list:
        arr, shp, idx = b if isinstance(b, tuple) else (b, b.shape, (0, 0))
        b_arrs.append(arr)
        b_specs.append(pl.BlockSpec(shp, lambda i, idx=idx: idx))
    Nn = b_specs[0].block_shape[1]
    dn = (((1,), (0,)), ((), ()))
    nin = 2 * ns + (1 if add is not None else 0)

    def body(*refs):
        a_refs, b_refs, scr = refs[:ns], refs[ns:2 * ns], list(refs[nin + 1:])
        acc = None
        for a_ref, b_ref, r in zip(a_refs, b_refs, dils):
            av = _from_view(a_ref, scr.pop(0), r) if r > 1 else a_ref[...]
            part = lax.dot_general(av.astype(BF16), b_ref[...], dn, preferred_element_type=F32)
            acc = part if acc is None else acc + part
        if add is not None:
            acc = acc + refs[2 * ns][...]
        refs[nin][...] = acc.astype(out_dtype)

    tpe = (T or M) // tm
    a_specs = [pl.BlockSpec((None, tm // r, r * w), lambda i: (i // tpe, i % tpe, 0)) if r
               else pl.BlockSpec((tm, w), lambda i: (i, 0)) for r, w in zip(dils, widths)]
    o_spec = pl.BlockSpec((tm, Nn), lambda i: (i, 0))
    specs = a_specs + b_specs
    ins = a_arrs + b_arrs
    if add is not None:
        specs.append(o_spec)
        ins.append(add)
    scratch = [pltpu.VMEM((w // LANES, tm, LANES), F32) for r, w in zip(dils, widths) if r > 1]
    return _pcall(body, name=name, grid=(M // tm,), in_specs=specs, out_specs=o_spec, scratch_shapes=scratch,
                  out_shape=jax.ShapeDtypeStruct((M, Nn), out_dtype), compiler_params=_params(("parallel",)))(*ins)


def _dw_view(d3, u, r, *, name, tk=1024):
    NB, tsub, rw = d3.shape
    W, T, D = rw // r, tsub * r, u.shape[1]
    tk = _pick(T, tk)
    tpe, nk = T // tk, NB * T // tk

    def body(d_ref, u_ref, o_ref, acc, scr):
        k = pl.program_id(0)
        dv = _from_view(d_ref, scr, r).astype(BF16)
        part = lax.dot_general(dv, u_ref[...], _TN, preferred_element_type=F32)

        @pl.when(k == 0)
        def _():
            acc[...] = part

        @pl.when(k > 0)
        def _():
            acc[...] += part

        @pl.when(k == nk - 1)
        def _():
            o_ref[...] = acc[...].astype(o_ref.dtype)

    return _pcall(
        body, name=name, grid=(nk,),
        in_specs=[pl.BlockSpec((None, tk // r, rw), lambda k: (k // tpe, k % tpe, 0)), pl.BlockSpec((tk, D), lambda k: (k, 0))],
        out_specs=pl.BlockSpec((W, D), lambda k: (0, 0)), out_shape=jax.ShapeDtypeStruct((W, D), BF16),
        scratch_shapes=[pltpu.VMEM((W, D), F32), pltpu.VMEM((W // LANES, tk, LANES), F32)],
        compiler_params=_params(("arbitrary",)))(d3, u)


def _lane(shape):
    return lax.broadcasted_iota(jnp.int32, shape, len(shape) - 1)


def _rot_half(v):
    w = v.shape[-1]
    first = (_lane(v.shape) % HEAD_DIM) < (HEAD_DIM // 2)
    return jnp.where(first, pltpu.roll(v, w - HEAD_DIM // 2, v.ndim - 1), pltpu.roll(v, HEAD_DIM // 2, v.ndim - 1))


def _widen(t, w):
    return t if w == t.shape[-1] else jnp.concatenate([t] * (w // t.shape[-1]), axis=-1)


def _unrope(v, cos, sins):
    w = v.shape[-1]
    return v * _widen(cos, w) - _rot_half(v) * _widen(sins, w)


def _rope_tables(positions):
    half = HEAD_DIM // 2
    inv = ROPE_THETA ** (-jnp.arange(half, dtype=F32) / half)
    ang = positions.astype(F32)[..., None] * inv
    cos, sin = jnp.cos(ang), jnp.sin(ang)
    cosf = jnp.concatenate([cos, cos, cos, cos], axis=-1)
    sins = jnp.concatenate([-sin, sin, -sin, sin], axis=-1)
    n = positions.shape[0] * positions.shape[1]
    return cosf.reshape(n, PAIR_W), sins.reshape(n, PAIR_W)


def _inproj(x2, scale, shift, w, cosf, sins, flags, *, T, name):
    N, D = x2.shape
    tm, tn = _pick(T, 512), VAR_W
    tpe = T // tm

    def body(x_ref, sc_ref, sh_ref, w_ref, c_ref, s_ref, f_ref, *outs):
        o_refs, u_ref = outs[:N_VAR], outs[N_VAR]
        j = pl.program_id(1)

        @pl.when(j == 0)
        def _():
            u_ref[...] = (x_ref[...] * (1.0 + sc_ref[0]) + sh_ref[0]).astype(BF16)

        acc = lax.dot_general(u_ref[...], w_ref[...], (((1,), (1,)), ((), ())), preferred_element_type=F32)
        fl = f_ref[...]
        ce = 1.0 + (_widen(c_ref[...], tn) - 1.0) * fl
        se = _widen(s_ref[...], tn) * fl
        res = acc * ce + _rot_half(acc) * se
        for v in range(N_VAR):
            @pl.when(j == v)
            def _(v=v):
                _to_view(res, o_refs[v], outs[N_VAR + 1], VAR_DIL[v])

    ex = pl.BlockSpec((1, 1, D), lambda i, j: (i // tpe, 0, 0))
    tab = pl.BlockSpec((tm, PAIR_W), lambda i, j: (i, 0))
    keep = lambda w_: pl.BlockSpec((tm, w_), lambda i, j: (i, 0))
    vspec = lambda r: pl.BlockSpec((None, tm // r, r * tn), lambda i, j: (i // tpe, i % tpe, 0))
    vshape = lambda r: jax.ShapeDtypeStruct((N // T, T // r, r * tn), BF16)
    return _pcall(
        body, name=name, grid=(N // tm, N_VAR),
        in_specs=[keep(D), ex, ex, pl.BlockSpec((tn, D), lambda i, j: (j, 0)), tab, tab,
                  pl.BlockSpec((1, tn), lambda i, j: (0, j))],
        out_specs=[vspec(r) for r in VAR_DIL] + [keep(D)],
        out_shape=[vshape(r) for r in VAR_DIL] + [jax.ShapeDtypeStruct((N, D), BF16)],
        scratch_shapes=[pltpu.VMEM((tn // LANES, tm, LANES), F32)],
        compiler_params=_params(("parallel", "arbitrary")),
    )(x2, scale, shift, w, cosf, sins, flags)


class _Geom:
    def __init__(self, g):
        if g is None:
            self.r, self.nq, self.n_back, self.sink = 1, A_Q_HEADS, A_WINDOW - 1, True
            self.qw, self.kw = QA_W, KA_W
            self.qidx = lambda j: 0
            self.kidx = lambda j: QA_W // KA_W
            self.vidx = lambda j: QA_W // KA_W + 1
        else:
            window, r = B_PATTERNS[g]
            self.r, self.nq, self.n_back, self.sink = r, B_GROUP_HEADS, window // r, False
            self.qw, self.kw = GB_W, GB_W
            self.qidx = lambda j: 3 * j
            self.kidx = lambda j: 3 * j + 1
            self.vidx = lambda j: 3 * j + 2
        self.ntile = self.qw // PAIR_W


def _stack_heads(t, scale=None):
    first = _lane(t.shape) < HEAD_DIM
    z = jnp.zeros_like(t)
    if scale is not None:
        t = t * jnp.asarray(scale, t.dtype)
    return jnp.concatenate([jnp.where(first, t, z), jnp.where(first, z, t)], axis=0)


def _lse_col(t):
    return jnp.concatenate([t[:, 0:1], t[:, HEAD_DIM:HEAD_DIM + 1]], axis=0)


def _lse_rows(t, width):
    first = _lane(t.shape) < HEAD_DIM
    other = pltpu.roll(t, HEAD_DIM, 1)
    full = jnp.concatenate([jnp.where(first, t, other), jnp.where(first, other, t)], axis=0)
    return _widen(full, width)


def _unstack_heads(v2):
    return jnp.where(_lane((BLOCK, PAIR_W)) < HEAD_DIM, v2[:BLOCK], v2[BLOCK:])


def _dup_head(t, kh):
    tf = t.astype(F32)
    keep = (_lane(t.shape) < HEAD_DIM) if kh == 0 else (_lane(t.shape) >= HEAD_DIM)
    return jnp.where(keep, tf, pltpu.roll(tf, HEAD_DIM, 1)).astype(t.dtype)


def _fold_heads(t):
    return t + pltpu.roll(t, HEAD_DIM, 1)


def _band_mask(rows, i, n_back, single):
    nkeys = BLOCK if single else 2 * BLOCK
    qi = jnp.bitwise_and(lax.broadcasted_iota(jnp.int32, (rows, nkeys), 0), BLOCK - 1)
    ki = lax.broadcasted_iota(jnp.int32, (rows, nkeys), 1)
    if single:
        return qi >= ki
    dist = qi + BLOCK - ki
    return jnp.logical_and(jnp.logical_and(dist >= 0, dist <= n_back), jnp.logical_or(ki >= BLOCK, i > 0))


def _per_block(col, scalars, fn):
    return jnp.concatenate([fn(col[b * BLOCK:(b + 1) * BLOCK], sc) for b, sc in enumerate(scalars)], axis=0)


def _sink_slot(rows):
    qi = jnp.bitwise_and(lax.broadcasted_iota(jnp.int32, (rows, 2 * BLOCK), 0), BLOCK - 1)
    return qi == lax.broadcasted_iota(jnp.int32, (rows, 2 * BLOCK), 1)


def _sink_scores(rows, sinks):
    blk = lax.broadcasted_iota(jnp.int32, (rows, 2 * BLOCK), 0) // BLOCK
    out = jnp.full((rows, 2 * BLOCK), sinks[-1], F32)
    for b in range(len(sinks) - 2, -1, -1):
        out = jnp.where(blk == b, sinks[b], out)
    return out


def _softmax_parts(s, valid, sinks):
    s = jnp.where(valid, s, NEG_INF)
    if sinks is not None:
        slot = _sink_slot(s.shape[0])
        s = jnp.where(slot, _sink_scores(s.shape[0], sinks), s)
    m = jnp.max(s, axis=1, keepdims=True)
    p = jnp.exp(s - m)
    den = jnp.sum(p, axis=1, keepdims=True)
    if sinks is not None:
        p = jnp.where(slot, 0.0, p)
    return p, m, den


_NT = (((1,), (1,)), ((), ()))
_TN = (((0,), (0,)), ((), ()))


def _rows2(prev_ref, cur_ref, cs, single=False):
    if single:
        return cur_ref[0, :, cs]
    return jnp.concatenate([prev_ref[0, :, cs], cur_ref[0, :, cs]], axis=0)


def _sink_scalars(sink_ref, first, nblocks):
    return [sink_ref[first + b] for b in range(nblocks)]


def _tile(t):
    return slice(t * PAIR_W, (t + 1) * PAIR_W)


def _attn_fwd(qkv, sinks, g, *, NB, T, name):
    geo = _Geom(g)
    r, qw, kw, ntile = geo.r, geo.qw, geo.kw, geo.ntile
    tsub = T // r
    nblk = tsub // BLOCK
    qkv3 = qkv.reshape(NB, tsub, r * VAR_W)
    out_dtype = BF16 if g is None else F32
    tiles_per_kv = ntile // A_KV_HEADS

    single = nblk == 1

    def body(q_ref, kp_ref, kc_ref, vp_ref, vc_ref, sink_ref, o_ref, l_ref):
        i = pl.program_id(2)
        if geo.sink:
            kall, vall = _rows2(kp_ref, kc_ref, _tile(0)), _rows2(vp_ref, vc_ref, _tile(0))
            kdup = [_dup_head(kall, kh) for kh in range(A_KV_HEADS)]
            vdup = [_dup_head(vall, kh) for kh in range(A_KV_HEADS)]
            tiles = [[t] for t in range(ntile)]
            q2s = [_stack_heads(q_ref[0, :, _tile(t)], SCALE) for t in range(ntile)]
            kks = [kdup[t // tiles_per_kv] for t in range(ntile)]
            vvs = [vdup[t // tiles_per_kv] for t in range(ntile)]
            sinkcols = [_sink_scalars(sink_ref, 2 * t, 2) for t in range(ntile)]
        else:
            tiles = [[t] for t in range(ntile)]
            q2s = [_stack_heads(q_ref[0, :, _tile(t)], SCALE) for t in range(ntile)]
            kks = [_rows2(kp_ref, kc_ref, _tile(t), single) for t in range(ntile)]
            vvs = [_rows2(vp_ref, vc_ref, _tile(t), single) for t in range(ntile)]
            sinkcols = [None] * ntile
        valid = _band_mask(q2s[0].shape[0], i, geo.n_back, single)
        ss = [lax.dot_general(q2, kk, _NT, preferred_element_type=F32) for q2, kk in zip(q2s, kks)]
        parts = [_softmax_parts(s, valid, sc) for s, sc in zip(ss, sinkcols)]
        o2s = [jnp.dot(p.astype(BF16), vv, preferred_element_type=F32) / den for (p, m, den), vv in zip(parts, vvs)]
        for ts, o2, (p, m, den) in zip(tiles, o2s, parts):
            lse2 = jnp.broadcast_to(m + jnp.log(den), (o2.shape[0], PAIR_W))
            for n, t in enumerate(ts):
                rows = slice(2 * BLOCK * n, 2 * BLOCK * (n + 1))
                o_ref[0, :, _tile(t)] = _unstack_heads(o2[rows]).astype(out_dtype)
                l_ref[0, :, _tile(t)] = _unstack_heads(lse2[rows])

    prev = lambda i: jnp.maximum(i - 1, 0)
    in_specs = [
        pl.BlockSpec((1, BLOCK, qw), lambda b, j, i: (b, i, geo.qidx(j))),
        pl.BlockSpec((1, BLOCK, kw), lambda b, j, i: (b, prev(i), geo.kidx(j))),
        pl.BlockSpec((1, BLOCK, kw), lambda b, j, i: (b, i, geo.kidx(j))),
        pl.BlockSpec((1, BLOCK, kw), lambda b, j, i: (b, prev(i), geo.vidx(j))),
        pl.BlockSpec((1, BLOCK, kw), lambda b, j, i: (b, i, geo.vidx(j))),
        pl.BlockSpec(memory_space=pltpu.SMEM),
    ]
    o_spec = pl.BlockSpec((1, BLOCK, qw), lambda b, j, i: (b, i, j))
    shape = (NB, tsub, r * qw)
    o, lse = _pcall(
        body, name=name, grid=(NB, r, nblk), in_specs=in_specs, out_specs=[o_spec, o_spec],
        out_shape=[jax.ShapeDtypeStruct(shape, out_dtype), jax.ShapeDtypeStruct(shape, F32)],
        compiler_params=_params(("parallel", "parallel", "arbitrary")),
    )(qkv3, qkv3, qkv3, qkv3, qkv3, sinks)
    return o, lse


def _attn_bwd(qkv, do, lse, dlse, cosf, sins, sinks, g, *, NB, T, name):
    geo = _Geom(g)
    r, qw, kw, ntile = geo.r, geo.qw, geo.kw, geo.ntile
    tsub = T // r
    nblk = tsub // BLOCK
    view = lambda a, w: a.reshape(NB, tsub, r * w)
    has_dlse = dlse is not None
    tiles_per_kv = ntile // A_KV_HEADS

    single = nblk == 1
    krows = BLOCK if single else 2 * BLOCK
    nsteps = 1 if single else nblk + 1

    def grads(q2s, kks, vvs, do2s, i, lserows, sinkcols, dlrows):
        nrow = q2s[0].shape[0]
        ki = lax.broadcasted_iota(jnp.int32, (krows, nrow), 0)
        qi = jnp.bitwise_and(lax.broadcasted_iota(jnp.int32, (krows, nrow), 1), BLOCK - 1)
        if single:
            valid = qi >= ki
        else:
            dist = qi + BLOCK - ki
            valid = jnp.logical_and(jnp.logical_and(dist >= 0, dist <= geo.n_back), jnp.logical_or(ki >= BLOCK, i > 0))
        sts = [lax.dot_general(kk, q2, _NT, preferred_element_type=F32) for q2, kk in zip(q2s, kks)]
        dpts = [lax.dot_general(vv, do2, _NT, preferred_element_type=F32) for do2, vv in zip(do2s, vvs)]
        pts, dsts, sks = [], [], []
        for st, dpt, ls, sc, dl in zip(sts, dpts, lserows, sinkcols, dlrows):
            sv = jnp.where(valid, st, NEG_INF)
            if sc is not None:
                slot = ki == qi
                blk = lax.broadcasted_iota(jnp.int32, (krows, nrow), 1) // BLOCK
                sink = jnp.full((krows, nrow), sc[-1], F32)
                for b in range(len(sc) - 2, -1, -1):
                    sink = jnp.where(blk == b, sc[b], sink)
                sv = jnp.where(slot, sink, sv)
                dpt = jnp.where(slot, 0.0, dpt)
            pt = jnp.exp(sv - ls)
            delta = jnp.sum(pt * dpt, axis=0, keepdims=True)
            if dl is not None:
                delta = delta - dl
            dst = pt * (dpt - delta)
            if sc is not None:
                cols = lambda a, b: a[:, b * BLOCK:(b + 1) * BLOCK]
                sks.append([jnp.sum(jnp.where(cols(slot, b), cols(dst, b), 0.0)) for b in range(len(sc))])
                dst, pt = jnp.where(slot, 0.0, dst), jnp.where(slot, 0.0, pt)
            else:
                sks.append(None)
            pts.append(pt.astype(BF16))
            dsts.append(dst.astype(BF16))
        dq2s = [lax.dot_general(dst, kk, _TN, preferred_element_type=F32) * SCALE for dst, kk in zip(dsts, kks)]
        dkks = [jnp.dot(dst, q2, preferred_element_type=F32) for dst, q2 in zip(dsts, q2s)]
        dvvs = [jnp.dot(pt, do2, preferred_element_type=F32) for pt, do2 in zip(pts, do2s)]
        return dq2s, dkks, dvvs, sks

    def stat_row(t):
        tt = t.T
        return jnp.concatenate([tt[0:1, :], tt[HEAD_DIM:HEAD_DIM + 1, :]], axis=1)

    def body(*refs):
        it = iter(refs)
        q_ref, kp_ref, kc_ref, vp_ref, vc_ref, do_ref, l_ref = (next(it) for _ in range(7))
        dl_ref = next(it) if has_dlse else None
        c_ref, s_ref, sink_ref, o_ref, ds_ref, dq_s, dk_s, dv_s, car_q, car_k, car_v = (next(it) for _ in range(11))
        b, j, i = pl.program_id(0), pl.program_id(1), pl.program_id(2)

        @pl.when(jnp.logical_and(b == 0, jnp.logical_and(j == 0, i == 0)))
        def _():
            ds_ref[...] = jnp.zeros_like(ds_ref)

        def compute():
            if geo.sink:
                kall, vall = _rows2(kp_ref, kc_ref, _tile(0)), _rows2(vp_ref, vc_ref, _tile(0))
                nb = 2 * tiles_per_kv
                tiles = [[kh * tiles_per_kv + t for t in range(tiles_per_kv)] for kh in range(A_KV_HEADS)]
                cat = lambda f, ts: jnp.concatenate([f(t) for t in ts], axis=0)
                dq2s, dkks, dvvs, sks = grads(
                    [cat(lambda t: _stack_heads(q_ref[0, :, _tile(t)], SCALE), ts) for ts in tiles],
                    [_dup_head(kall, kh) for kh in range(A_KV_HEADS)],
                    [_dup_head(vall, kh) for kh in range(A_KV_HEADS)],
                    [cat(lambda t: _stack_heads(do_ref[0, :, _tile(t)]), ts) for ts in tiles], i,
                    [jnp.concatenate([stat_row(l_ref[0, :, _tile(t)]) for t in ts], axis=1) for ts in tiles],
                    [_sink_scalars(sink_ref, kh * nb, nb) for kh in range(A_KV_HEADS)], [None] * A_KV_HEADS)
                lane1 = _lane((1, PAIR_W))
                dsink = jnp.zeros((1, PAIR_W), F32)
                for kh, (ts, dq2, sk) in enumerate(zip(tiles, dq2s, sks)):
                    for n, t in enumerate(ts):
                        dq_s[:, _tile(t)] = _unstack_heads(dq2[2 * BLOCK * n:2 * BLOCK * (n + 1)])
                    for bb in range(nb):
                        dsink = dsink + jnp.where(lane1 == kh * nb + bb, sk[bb], 0.0)
                second = _lane((krows, PAIR_W)) >= HEAD_DIM
                dk_s[...] = jnp.where(second, _fold_heads(dkks[1]), _fold_heads(dkks[0]))
                dv_s[...] = jnp.where(second, _fold_heads(dvvs[1]), _fold_heads(dvvs[0]))
                ds_ref[0:1, :] += dsink
            else:
                dq2s, dkks, dvvs, _ = grads(
                    [_stack_heads(q_ref[0, :, _tile(t)], SCALE) for t in range(ntile)],
                    [_rows2(kp_ref, kc_ref, _tile(t), single) for t in range(ntile)],
                    [_rows2(vp_ref, vc_ref, _tile(t), single) for t in range(ntile)],
                    [_stack_heads(do_ref[0, :, _tile(t)]) for t in range(ntile)], i,
                    [stat_row(l_ref[0, :, _tile(t)]) for t in range(ntile)], [None] * ntile,
                    [stat_row(dl_ref[0, :, _tile(t)]) for t in range(ntile)])
                for t in range(ntile):
                    dq_s[:, _tile(t)] = _unstack_heads(dq2s[t])
                    dk_s[0:krows, _tile(t)] = dkks[t]
                    dv_s[0:krows, _tile(t)] = dvvs[t]

        def emit(dq, dk, dv):
            cos, sn = c_ref[0], s_ref[0]
            o_ref[0, :, 0:qw] = _unrope(dq, cos, sn).astype(BF16)
            o_ref[0, :, qw:qw + kw] = _unrope(dk, cos, sn).astype(BF16)
            o_ref[0, :, qw + kw:qw + 2 * kw] = dv.astype(BF16)
            if qw + 2 * kw < VAR_W:
                o_ref[0, :, qw + 2 * kw:VAR_W] = jnp.zeros((BLOCK, VAR_W - qw - 2 * kw), BF16)

        if single:
            compute()
            emit(dq_s[...], dk_s[0:BLOCK, :], dv_s[0:BLOCK, :])
            return

        @pl.when(i == 0)
        def _():
            car_q[...] = jnp.zeros_like(car_q)
            car_k[...] = jnp.zeros_like(car_k)
            car_v[...] = jnp.zeros_like(car_v)

        @pl.when(i == nblk)
        def _():
            dk_s[...] = jnp.zeros_like(dk_s)
            dv_s[...] = jnp.zeros_like(dv_s)

        pl.when(i < nblk)(compute)
        emit(car_q[...], car_k[...] + dk_s[0:BLOCK, :], car_v[...] + dv_s[0:BLOCK, :])
        car_q[...] = dq_s[...]
        car_k[...] = dk_s[BLOCK:2 * BLOCK, :]
        car_v[...] = dv_s[BLOCK:2 * BLOCK, :]

    cur = lambda i: jnp.minimum(i, nblk - 1)
    prv = lambda i: jnp.maximum(jnp.minimum(i, nblk - 1) - 1, 0)
    outb = lambda i: jnp.maximum(i - 1, 0)
    qrow = pl.BlockSpec((1, BLOCK, qw), lambda b, j, i: (b, cur(i), j))
    in_specs = [
        pl.BlockSpec((1, BLOCK, qw), lambda b, j, i: (b, cur(i), geo.qidx(j))),
        pl.BlockSpec((1, BLOCK, kw), lambda b, j, i: (b, prv(i), geo.kidx(j))),
        pl.BlockSpec((1, BLOCK, kw), lambda b, j, i: (b, cur(i), geo.kidx(j))),
        pl.BlockSpec((1, BLOCK, kw), lambda b, j, i: (b, prv(i), geo.vidx(j))),
        pl.BlockSpec((1, BLOCK, kw), lambda b, j, i: (b, cur(i), geo.vidx(j))),
        qrow, qrow,
    ]
    ins = [view(qkv, VAR_W)] * 5 + [view(do, qw), view(lse, qw)]
    if has_dlse:
        in_specs.append(qrow)
        ins.append(view(dlse, qw))
    in_specs += [
        pl.BlockSpec((1, BLOCK, PAIR_W), lambda b, j, i: (b, outb(i), j)),
        pl.BlockSpec((1, BLOCK, PAIR_W), lambda b, j, i: (b, outb(i), j)),
        pl.BlockSpec(memory_space=pltpu.SMEM),
    ]
    ins += [view(cosf, PAIR_W), view(sins, PAIR_W), sinks]
    scratch = [pltpu.VMEM((BLOCK, qw), F32), pltpu.VMEM((2 * BLOCK, kw), F32), pltpu.VMEM((2 * BLOCK, kw), F32),
               pltpu.VMEM((BLOCK, qw), F32), pltpu.VMEM((BLOCK, kw), F32), pltpu.VMEM((BLOCK, kw), F32)]
    dqkv, dsink = _pcall(
        body, name=name, grid=(NB, r, nsteps), in_specs=in_specs,
        out_specs=[pl.BlockSpec((1, BLOCK, VAR_W), lambda b, j, i: (b, outb(i), j)),
                   pl.BlockSpec((8, PAIR_W), lambda b, j, i: (0, 0))],
        out_shape=[jax.ShapeDtypeStruct((NB, tsub, r * VAR_W), BF16), jax.ShapeDtypeStruct((8, PAIR_W), F32)],
        scratch_shapes=scratch, compiler_params=_params(("arbitrary", "arbitrary", "arbitrary")),
    )(*ins)
    return dqkv, dsink


class _Rows:
    def __init__(self, N, T, tm):
        self.N, self.tm, self.tpe, self.grid = N, tm, T // tm, (N // tm,)

    def row(self, w, col=0):
        return pl.BlockSpec((self.tm, w), lambda i: (i, col))

    def ex(self, w):
        return pl.BlockSpec((1, 1, w), lambda i: (i // self.tpe, 0, 0))

    def const(self, shape):
        return pl.BlockSpec(shape, lambda i: tuple(0 for _ in shape))

    def view(self, w, r):
        return pl.BlockSpec((None, self.tm // r, r * w), lambda i: (i // self.tpe, i % self.tpe, 0))

    def first_of_example(self):
        return pl.program_id(0) % self.tpe == 0


def _acc(ref, first, val):
    @pl.when(first)
    def _():
        ref[0] = val

    @pl.when(jnp.logical_not(first))
    def _():
        ref[0] += val


def _colsum(v):
    return jnp.sum(v, axis=0, keepdims=True)


def _ln_stats(r):
    mu = jnp.mean(r, axis=-1, keepdims=True)
    xc = r - mu
    var = jnp.mean(xc * xc, axis=-1, keepdims=True)
    rstd = lax.rsqrt(var + LN_EPS)
    return xc * rstd, rstd


def _ln_bwd(dy, xhat, rstd, gain):
    dxh = dy * gain
    return rstd * (dxh - jnp.mean(dxh, axis=-1, keepdims=True) - xhat * jnp.mean(dxh * xhat, axis=-1, keepdims=True))


def _from_view(ref, scr, r):
    if r == 1:
        return ref[...]
    rows, w = ref.shape[0], ref.shape[1] // r
    for j in range(r):
        for c in range(w // LANES):
            scr.at[c][pl.ds(j, rows, stride=r), :] = ref[:, j * w + c * LANES:j * w + (c + 1) * LANES].astype(F32)
    return jnp.concatenate([scr[c] for c in range(w // LANES)], axis=1)


def _to_view(val, ref, scr, r):
    if r == 1:
        ref[...] = val.astype(ref.dtype)
        return
    rows, w = ref.shape[0], ref.shape[1] // r
    for c in range(w // LANES):
        scr[c] = val[:, c * LANES:(c + 1) * LANES]
    for j in range(r):
        for c in range(w // LANES):
            ref[:, j * w + c * LANES:j * w + (c + 1) * LANES] = scr.at[c][pl.ds(j, rows, stride=r), :].astype(ref.dtype)


def _silu_parts(v):
    s = jax.nn.sigmoid(v)
    return v * s, s * (1.0 + v * (1.0 - s))


def _local_step(x, mod, positions, w_in, rest_weights, sinks, ln1_g, ln1_b, ln2_g, ln2_b, target, hook=None):
    hook = hook or (lambda event, **data: None)
    NB, T, D = x.shape
    N = NB * T
    x2 = x.reshape(N, D)
    tgt2 = target.reshape(N, D)
    shift_m, scale_m, gate_m, shift_f, scale_f, gate_f = [mod[:, None, k * D:(k + 1) * D] for k in range(6)]
    cosf, sins = _rope_tables(positions)
    col = jnp.arange(QKV_P)
    vcol = col % VAR_W
    flags = jnp.where(col < VAR_W, vcol < QA_W + KA_W, vcol < 2 * GB_W).astype(F32)[None]
    R = _Rows(N, T, _pick(T, 256))
    sds = jax.ShapeDtypeStruct
    exsum = lambda w=D: sds((NB, 1, w), F32)
    ngrp = len(B_PATTERNS)

    *qkv, u = _inproj(x2, scale_m, shift_m, w_in, cosf, sins, flags, T=T, name="inproj_qkv")
    gates = _mm(u, w_in[QKV_P:], tb=True, out_dtype=BF16, name="inproj_gates")
    oa, la = _attn_fwd(qkv[0], sinks, None, NB=NB, T=T, name="attn_a_fwd")
    oa = oa.reshape(N, QA_W)
    ob_parts = [_attn_fwd(qkv[1 + g], sinks, g, NB=NB, T=T, name=f"attn_b{g}_fwd") for g in range(ngrp)]
    (o1, l1), (o2, l2), (o3, l3) = ob_parts
    w_a, w_b, w_o, w_gu, w_d = rest_weights()
    F = w_d.shape[0]
    dil = [r_ for _, r_ in B_PATTERNS]
    views = [R.view(GB_W, r_) for r_ in dil]
    tokbuf = pltpu.VMEM((GB_W // LANES, R.tm, LANES), F32)

    def merge_fwd(o1r, o2r, o3r, l1r, l2r, l3r, ob_ref, *bufs):
        os_ = [_from_view(ref, bufs[n], dil[n]) for n, ref in enumerate((o1r, o2r, o3r))]
        la, lb, lc = [_from_view(ref, bufs[3 + n], dil[n]) for n, ref in enumerate((l1r, l2r, l3r))]
        mx = jnp.maximum(jnp.maximum(la, lb), lc)
        ea, eb, ec = jnp.exp(la - mx), jnp.exp(lb - mx), jnp.exp(lc - mx)
        ob_ref[...] = ((ea * os_[0] + eb * os_[1] + ec * os_[2]) / (ea + eb + ec)).astype(BF16)

    ob = _pcall(merge_fwd, name="merge_fwd", grid=R.grid, in_specs=views + views, out_specs=R.row(GB_W),
                out_shape=sds((N, GB_W), BF16), scratch_shapes=[tokbuf] * 6,
                compiler_params=_params(("parallel",)))(o1, o2, o3, l1, l2, l3)

    ya = _mm(oa, w_a, out_dtype=BF16, name="branch_a")
    yb = _mm(ob, w_b, b3=True, out_dtype=BF16, name="branch_b")
    f32 = lambda ref: ref[...].astype(F32)

    def gate_fwd(ya_r, yb_r, ga_r, gb_r, mg_ref):
        mg_ref[...] = (jax.nn.sigmoid(f32(ga_r)) * f32(ya_r) + jax.nn.sigmoid(f32(gb_r)) * f32(yb_r)).astype(BF16)

    merged = _pcall(gate_fwd, name="gate_fwd", grid=R.grid, in_specs=[R.row(D), R.row(D), R.row(D, 0), R.row(D, 1)],
                    out_specs=R.row(D), out_shape=sds((N, D), BF16),
                    compiler_params=_params(("parallel",)))(ya, yb, gates, gates)
    y = _mm(merged, w_o, name="out_proj")

    def norm1_fwd(x_r, y_r, gm_r, g_r, b_r, sf_r, hf_r, r1_ref, u2_ref):
        r1 = ALPHA * x_r[...] + (1.0 + gm_r[0]) * y_r[...]
        xhat, _ = _ln_stats(r1)
        x1 = xhat * g_r[...] + b_r[...]
        r1_ref[...] = r1
        u2_ref[...] = (x1 * (1.0 + sf_r[0]) + hf_r[0]).astype(BF16)

    r1, u2 = _pcall(
        norm1_fwd, name="norm1_fwd", grid=R.grid,
        in_specs=[R.row(D), R.row(D), R.ex(D), R.const((1, D)), R.const((1, D)), R.ex(D), R.ex(D)],
        out_specs=[R.row(D)] * 2, out_shape=[sds((N, D), F32), sds((N, D), BF16)],
        compiler_params=_params(("parallel",)))(x2, y, gate_m, ln1_g, ln1_b, scale_f, shift_f)

    tnf = w_gu.shape[2]
    nft = w_gu.shape[0] // 2
    tmf = _pick(N, 512)

    def ffn_up(u_r, wg_r, wu_r, hg_ref, hu_ref, a_ref):
        hg = jnp.dot(u_r[...], wg_r[...], preferred_element_type=F32)
        hu = jnp.dot(u_r[...], wu_r[...], preferred_element_type=F32)
        sl, _ = _silu_parts(hg)
        hg_ref[...] = hg.astype(BF16)
        hu_ref[...] = hu.astype(BF16)
        a_ref[...] = (sl * hu).astype(BF16)

    ftile = pl.BlockSpec((tmf, tnf), lambda j, i: (i, j))
    hg, hu, act = _pcall(
        ffn_up, name="ffn_up", grid=(nft, N // tmf),
        in_specs=[pl.BlockSpec((tmf, D), lambda j, i: (i, 0)), pl.BlockSpec((None, D, tnf), lambda j, i: (j, 0, 0)),
                  pl.BlockSpec((None, D, tnf), lambda j, i: (j + nft, 0, 0))],
        out_specs=[ftile] * 3, out_shape=[sds((N, F), BF16)] * 3,
        compiler_params=_params(("arbitrary", "parallel")))(u2, w_gu, w_gu)
    y2 = _mm(act, w_d, name="ffn_down")

    def norm2_loss_bwd(r1_r, g1_r, b1_r, y2_r, t_r, gf_r, g_r, b_r, dy2_ref, dx1_ref, dgf_ref, dg_ref, db_ref, loss_ref):
        first = R.first_of_example()
        y2v = y2_r[...]
        x1 = _ln_stats(r1_r[...])[0] * g1_r[...] + b1_r[...]
        r2 = ALPHA * x1 + (1.0 + gf_r[0]) * y2v
        xhat, rstd = _ln_stats(r2)
        err = xhat * g_r[...] + b_r[...] - t_r[...]
        dx2 = err * (1.0 / D)
        dr2 = _ln_bwd(dx2, xhat, rstd, g_r[...])
        dy2_ref[...] = ((1.0 + gf_r[0]) * dr2).astype(BF16)
        dx1_ref[...] = ALPHA * dr2
        _acc(dgf_ref, first, _colsum(dr2 * y2v))
        _acc(dg_ref, first, _colsum(dx2 * xhat))
        _acc(db_ref, first, _colsum(dx2))
        part = 0.5 * jnp.sum(jnp.mean(err * err, axis=-1, keepdims=True))
        _acc(loss_ref, first, jnp.broadcast_to(part, (1, 128)))

    dy2, dx1p, dgate_f, dg2, db2, loss_p = _pcall(
        norm2_loss_bwd, name="norm2_loss_bwd", grid=R.grid,
        in_specs=[R.row(D), R.const((1, D)), R.const((1, D)), R.row(D), R.row(D), R.ex(D), R.const((1, D)),
                  R.const((1, D))],
        out_specs=[R.row(D), R.row(D), R.ex(D), R.ex(D), R.ex(D), R.ex(128)],
        out_shape=[sds((N, D), BF16), sds((N, D), F32), exsum(), exsum(), exsum(), exsum(128)],
        compiler_params=_params(("arbitrary",)))(r1, ln1_g, ln1_b, y2, tgt2, gate_f, ln2_g, ln2_b)

    g_wd = _mm(act, dy2, ta=True, out_dtype=BF16, name="ffn_down_dw")

    tmd = _pick(N, 256)

    fchunk = _pick(F, 768)

    def ffn_down_dx(dy_r, wd_r, hg_r, hu_r, dh_ref):
        for t in range(F // fchunk):
            cs = slice(t * fchunk, (t + 1) * fchunk)
            da = lax.dot_general(dy_r[...], wd_r[cs, :], _NT, preferred_element_type=F32)
            sl, dsl = _silu_parts(hg_r[:, cs].astype(F32))
            dh_ref[:, cs] = (da * hu_r[:, cs].astype(F32) * dsl).astype(BF16)
            dh_ref[:, F + t * fchunk:F + (t + 1) * fchunk] = (da * sl).astype(BF16)

    rowd = lambda w_: pl.BlockSpec((tmd, w_), lambda i: (i, 0))
    dh = _pcall(
        ffn_down_dx, name="ffn_down_dx", grid=(N // tmd,),
        in_specs=[rowd(D), pl.BlockSpec((F, D), lambda i: (0, 0)), rowd(F), rowd(F)],
        out_specs=rowd(2 * F), out_shape=sds((N, 2 * F), BF16),
        compiler_params=_params(("parallel",)))(dy2, w_d, hg, hu)
    def ffn_up_dx(dh_r, w_r, o_ref):
        acc = None
        for s_ in range(w_gu.shape[0]):
            part = lax.dot_general(dh_r[:, s_ * tnf:(s_ + 1) * tnf], w_r[s_], _NT, preferred_element_type=F32)
            acc = part if acc is None else acc + part
        o_ref[...] = acc

    du2 = _pcall(
        ffn_up_dx, name="ffn_up_dx", grid=(N // tmf,),
        in_specs=[pl.BlockSpec((tmf, 2 * F), lambda i: (i, 0)), pl.BlockSpec(w_gu.shape, lambda i: (0, 0, 0))],
        out_specs=pl.BlockSpec((tmf, D), lambda i: (i, 0)), out_shape=sds((N, D), F32),
        compiler_params=_params(("parallel",)))(dh, w_gu)
    g_wgu = _mm(u2, dh, ta=True, out3=w_gu.shape[0], out_dtype=BF16, name="ffn_up_dw")

    def norm1_bwd(dx1p_r, du2_r, r1_r, y_r, sf_r, gm_r, g_r, b_r,
                  dxp_ref, dy_ref, dsf_ref, dhf_ref, dgm_ref, dg_ref, db_ref):
        first = R.first_of_example()
        du2v = du2_r[...]
        dx1 = dx1p_r[...] + du2v * (1.0 + sf_r[0])
        xhat, rstd = _ln_stats(r1_r[...])
        dr1 = _ln_bwd(dx1, xhat, rstd, g_r[...])
        dxp_ref[...] = ALPHA * dr1
        dy_ref[...] = ((1.0 + gm_r[0]) * dr1).astype(BF16)
        _acc(dsf_ref, first, _colsum(du2v * (xhat * g_r[...] + b_r[...])))
        _acc(dhf_ref, first, _colsum(du2v))
        _acc(dgm_ref, first, _colsum(dr1 * y_r[...]))
        _acc(dg_ref, first, _colsum(dx1 * xhat))
        _acc(db_ref, first, _colsum(dx1))

    dxp, dy, dscale_f, dshift_f, dgate_m, dg1, db1 = _pcall(
        norm1_bwd, name="norm1_bwd", grid=R.grid,
        in_specs=[R.row(D)] * 4 + [R.ex(D), R.ex(D), R.const((1, D)), R.const((1, D))],
        out_specs=[R.row(D), R.row(D)] + [R.ex(D)] * 5,
        out_shape=[sds((N, D), F32), sds((N, D), BF16)] + [exsum()] * 5,
        compiler_params=_params(("arbitrary",)))(dx1p, du2, r1, y, scale_f, gate_m, ln1_g, ln1_b)

    dmerged = _mm(dy, w_o, tb=True, out_dtype=BF16, name="out_proj_dx")
    g_wo = _mm(merged, dy, ta=True, out_dtype=BF16, name="out_proj_dw")

    def gate_bwd(dm_r, ya_r, yb_r, ga_r, gb_r, dya_ref, dyb_ref, dg_ref):
        dm = f32(dm_r)
        sa, sb = jax.nn.sigmoid(f32(ga_r)), jax.nn.sigmoid(f32(gb_r))
        dya_ref[...] = (dm * sa).astype(BF16)
        dyb_ref[...] = (dm * sb).astype(BF16)
        dg_ref[:, :D] = (dm * f32(ya_r) * sa * (1.0 - sa)).astype(BF16)
        dg_ref[:, D:] = (dm * f32(yb_r) * sb * (1.0 - sb)).astype(BF16)

    dya, dyb, dgates = _pcall(
        gate_bwd, name="gate_bwd", grid=R.grid, in_specs=[R.row(D)] * 3 + [R.row(D, 0), R.row(D, 1)],
        out_specs=[R.row(D), R.row(D), R.row(2 * D)],
        out_shape=[sds((N, D), BF16), sds((N, D), BF16), sds((N, 2 * D), BF16)],
        compiler_params=_params(("parallel",)))(dmerged, ya, yb, gates, gates)

    doa = _mm(dya, w_a, tb=True, out_dtype=BF16, name="branch_a_dx")
    g_wa = _mm(oa, dya, ta=True, out_dtype=BF16, name="branch_a_dw")
    dob = _mm(dyb, w_b, tb=True, b3=True, name="branch_b_dx")
    g_wb = _mm(ob, dyb, ta=True, out3=w_b.shape[0], out_dtype=BF16, name="branch_b_dw")
    hook("rest_grads", g_wa=g_wa, g_wb=g_wb, g_wo=g_wo, g_wgu=g_wgu, g_wd=g_wd)

    seg = (jnp.arange(GB_W)[:, None] // HEAD_DIM == jnp.arange(GB_W)[None, :] // HEAD_DIM).astype(BF16)

    def merge_bwd(dob_r, o1r, o2r, o3r, l1r, l2r, l3r, seg_r, d1, d2, d3, e1, e2, e3, *bufs):
        dob_v = dob_r[...]
        os_ = [_from_view(ref, bufs[n], dil[n]) for n, ref in enumerate((o1r, o2r, o3r))]
        la, lb, lc = [_from_view(ref, bufs[3 + n], dil[n]) for n, ref in enumerate((l1r, l2r, l3r))]
        mx = jnp.maximum(jnp.maximum(la, lb), lc)
        ea, eb, ec = jnp.exp(la - mx), jnp.exp(lb - mx), jnp.exp(lc - mx)
        inv = 1.0 / (ea + eb + ec)
        ws = [ea * inv, eb * inv, ec * inv]

        def headsum(v):
            hi = v.astype(BF16)
            r1_ = v - hi.astype(F32)
            mid = r1_.astype(BF16)
            lo = (r1_ - mid.astype(F32)).astype(BF16)
            sm = seg_r[...]
            return (jnp.dot(hi, sm, preferred_element_type=F32) + jnp.dot(mid, sm, preferred_element_type=F32)
                    + jnp.dot(lo, sm, preferred_element_type=F32))

        dws = [headsum(dob_v * o) for o in os_]
        mean = ws[0] * dws[0] + ws[1] * dws[1] + ws[2] * dws[2]
        for n, (w_, dw_, d_ref, e_ref) in enumerate(zip(ws, dws, (d1, d2, d3), (e1, e2, e3))):
            _to_view(w_ * dob_v, d_ref, bufs[6], dil[n])
            _to_view(w_ * (dw_ - mean), e_ref, bufs[7], dil[n])

    vshape = lambda r_, dt: sds((NB, T // r_, r_ * GB_W), dt)
    mb = _pcall(
        merge_bwd, name="merge_bwd", grid=R.grid, in_specs=[R.row(GB_W)] + views + views + [R.const((GB_W, GB_W))],
        out_specs=views + views, out_shape=[vshape(r_, BF16) for r_ in dil] + [vshape(r_, F32) for r_ in dil],
        scratch_shapes=[tokbuf] * 8, compiler_params=_params(("parallel",)))(dob, o1, o2, o3, l1, l2, l3, seg)
    do_b, dlse_b = mb[:3], mb[3:]
    hook("merge_bwd_done")

    dqkv_a, dsink = _attn_bwd(qkv[0], doa, la, None, cosf, sins, sinks, None, NB=NB, T=T, name="attn_a_bwd")
    hook("attn_a_bwd_done")
    dqkv = [dqkv_a]
    for g in range(ngrp):
        dqkv.append(_attn_bwd(qkv[1 + g], do_b[g], (l1, l2, l3)[g], dlse_b[g], cosf, sins, sinks, g, NB=NB, T=T,
                              name=f"attn_b{g}_bwd")[0])
        hook(f"attn_b{g}_bwd_done")

    g_win = [_mm(d3.reshape(N, VAR_W), u, ta=True, out_dtype=BF16, name=f"inproj_dw{v}") if VAR_DIL[v] == 1
             else _dw_view(d3, u, VAR_DIL[v], name=f"inproj_dw{v}") for v, d3 in enumerate(dqkv)]
    g_win.append(_mm(dgates, u, ta=True, out_dtype=BF16, name=f"inproj_dw{N_VAR}"))
    hook("win_grads", g_win=g_win)
    wvar = lambda v: (w_in, (VAR_W, D), (v, 0))
    dview = lambda v: (dqkv[v], VAR_DIL[v])
    du = _mm_multi([dview(0)], [wvar(0)], M=N, T=T, name="inproj_dx0")
    hook("inproj_dx0_done")
    du = _mm_multi([dview(v) for v in range(1, N_VAR)] + [dgates],
                   [wvar(v) for v in range(1, N_VAR)] + [(w_in, (2 * D, D), (QKV_P // (2 * D), 0))],
                   M=N, T=T, add=du, tm=256, name="inproj_dx1")
    hook("inproj_dx1_done")

    def x_bwd(dxp_r, du_r, x_r, sm_r, gx_ref, dsm_ref, dhm_ref):
        first = R.first_of_example()
        duv = du_r[...]
        gx_ref[...] = dxp_r[...] + duv * (1.0 + sm_r[0])
        _acc(dsm_ref, first, _colsum(duv * x_r[...]))
        _acc(dhm_ref, first, _colsum(duv))

    gx, dscale_m, dshift_m = _pcall(
        x_bwd, name="x_bwd", grid=R.grid, in_specs=[R.row(D)] * 3 + [R.ex(D)],
        out_specs=[R.row(D), R.ex(D), R.ex(D)], out_shape=[sds((N, D), F32), exsum(), exsum()],
        compiler_params=_params(("arbitrary",)))(dxp, du, x2, scale_m)
    hook("x_bwd_done")

    dmod =jnp.concatenate([dshift_m, dscale_m, dgate_m, dshift_f, dscale_f, dgate_f], axis=-1)[:, 0]
    ln_grads = jnp.concatenate([dg1, db1, dg2, db2], axis=1)
    return dict(loss=loss_p[:, 0, 0], grad_x=gx.reshape(NB, T, D), g_win=g_win, g_wa=g_wa, g_wb=g_wb, g_wo=g_wo,
                g_wgu=g_wgu, g_wd=g_wd, dmod=dmod, ln_grads=ln_grads, dsink=dsink[0, :A_Q_HEADS])


def _coords():
    return lax.axis_index("x"), lax.axis_index("y"), lax.axis_index("c")


def _allgather_small(blk, *, name):
    m_per, n = blk.shape

    def body(x_ref, out_ref, send_sems, recv_sems, local_sem):
        x, y, c = _coords()
        me, sibling = (x, y, c), (x, y, 1 - c)
        chips = [(1 - x, y), (x, 1 - y), (1 - x, 1 - y)]

        def rows(px, py, pc):
            return out_ref.at[pl.ds((4 * px + 2 * py + pc) * m_per, m_per), :]

        def copy(k, block, to, src=None):
            return pltpu.make_async_remote_copy(
                src_ref=rows(*block) if src is None else src, dst_ref=rows(*block),
                send_sem=send_sems.at[k], recv_sem=recv_sems.at[k], device_id=to, device_id_type=MESH)

        mine = pltpu.make_async_copy(x_ref, rows(*me), local_sem)
        mine.start()
        first = [copy(0, me, sibling, src=x_ref)]
        first += [copy(1 + j, me, (*chip, c), src=x_ref) for j, chip in enumerate(chips)]
        for cp in first:
            cp.start()
        passed = [copy(4 + j, (*chip, c), sibling) for j, chip in enumerate(chips)]
        for j, chip in enumerate(chips):
            copy(1 + j, (*chip, c), me).wait_recv()
            passed[j].start()
        copy(0, sibling, me).wait_recv()
        for j, chip in enumerate(chips):
            copy(4 + j, (*chip, 1 - c), me).wait_recv()
        for cp in first + passed:
            cp.wait_send()
        mine.wait()

    return _pcall(
        body, name=name, out_shape=jax.ShapeDtypeStruct((8 * m_per, n), blk.dtype),
        in_specs=[pl.BlockSpec(memory_space=pltpu.VMEM)], out_specs=pl.BlockSpec(memory_space=pltpu.VMEM),
        scratch_shapes=[pltpu.SemaphoreType.DMA((7,)), pltpu.SemaphoreType.DMA((7,)), pltpu.SemaphoreType.DMA],
        compiler_params=pltpu.CompilerParams(vmem_limit_bytes=VMEM_LIMIT_BYTES),
    )(blk)


def _exchange(srcs, dsts, plan, *, name, dst_inits=None):
    na = len(dsts)
    nrem = len(plan(0, 0, 0))

    def body(*refs):
        refs = list(refs)
        src_refs = [refs.pop(0) for _ in range(na)] if srcs is not None else None
        if dst_inits is not None:
            del refs[:na]
        dst_refs, (send_sems, recv_sems) = refs[:na], refs[na:]
        start, wait = _copies(dst_refs if src_refs is None else src_refs, dst_refs, send_sems, recv_sems, plan)
        start()
        wait()

    hbm = pl.BlockSpec(memory_space=pl.ANY)
    ins = (list(srcs) if srcs is not None else []) + (list(dst_inits) if dst_inits is not None else [])
    base = na if srcs is not None else 0
    aliases = {base + a: a for a in range(na)} if dst_inits is not None else {}
    return _pcall(
        body, name=name, out_shape=list(dsts), in_specs=[hbm] * len(ins), out_specs=[hbm] * na,
        input_output_aliases=aliases,
        scratch_shapes=[pltpu.SemaphoreType.DMA((na * nrem,)), pltpu.SemaphoreType.DMA((na * nrem,))],
    )(*ins)


def _other_chips(x, y):
    return [(1 - x, y), (x, 1 - y), (1 - x, 1 - y)]


def _round(ride, carrier, name):
    if carrier is not None:
        _RIDES.setdefault(carrier, []).append(ride)
        return
    srcs = ride.srcs() if callable(ride.srcs) else ride.srcs
    inits = ride.dst_inits() if callable(ride.dst_inits) else ride.dst_inits
    ride.out = list(_exchange(srcs, ride.dsts, ride.plan, name=name, dst_inits=inits))


class _Gather:
    def __init__(self, shards, chip, tag, carriers=(None, None)):
        def plan_ici(x, y, c):
            k = 2 * x + y
            return [((c,), (k, c), (2 * px + py, c), (px, py, c)) for px, py in _other_chips(x, y)]

        def plan_d2d(x, y, c):
            return [((2 * px + py, c), (2 * px + py, c), (2 * px + py, 1 - c), (x, y, 1 - c))
                    for px, py in _other_chips(x, y)]

        self.shards, self.chip = shards, chip
        dsts = [jax.ShapeDtypeStruct((4,) + s.shape, s.dtype) for s in shards]
        ici = _Ride(shards, dsts, plan_ici)
        self.d2d = _Ride(None, dsts, plan_d2d, dst_inits=lambda: ici.out)
        _round(ici, carriers[0], f"gather_{tag}_ici")
        _round(self.d2d, carriers[1], f"gather_{tag}_d2d")

    def result(self):
        full = [lax.dynamic_update_index_in_dim(f, s, self.chip, 0) for f, s in zip(self.d2d.out, self.shards)]
        return [f.reshape((4, 2 * f.shape[2], f.shape[3])) for f in full]


def _index_operand(i):
    return jnp.reshape(i, (1,)).astype(jnp.int32)


def _add_pairs(g, f, ci, *, name):
    s, _, hr, wd = g.shape
    tr = _pick(hr, 600, 16)

    def body(c_ref, a_ref, b_ref, o_ref):
        o_ref[...] = (a_ref[...].astype(F32) + b_ref[...].astype(F32)).astype(BF16)

    spec = pl.BlockSpec((1, tr, wd), lambda j, i, c: (j, i, 0))
    grid_spec = pltpu.PrefetchScalarGridSpec(
        num_scalar_prefetch=1, grid=(s, hr // tr),
        in_specs=[pl.BlockSpec((1, None, tr, wd), lambda j, i, c: (j, c[0], i, 0)), spec], out_specs=spec)
    return _pcall(body, name=name, grid_spec=grid_spec, out_shape=jax.ShapeDtypeStruct(f.shape, BF16),
                  compiler_params=_params(("parallel", "parallel")))(_index_operand(ci), g, f)


def _sum_chips(landed, pairs, chip, *, name):
    s, hr, wd = landed.shape
    tr = _pick(hr, 600, 16)

    def body(k_ref, l_ref, p_ref, o_ref):
        acc = None
        for k in range(s):
            part = jnp.where(k_ref[0] == k, p_ref[k], l_ref[k]).astype(F32)
            acc = part if acc is None else acc + part
        o_ref[...] = acc

    spec = pl.BlockSpec((s, tr, wd), lambda i, k: (0, i, 0))
    grid_spec = pltpu.PrefetchScalarGridSpec(
        num_scalar_prefetch=1, grid=(hr // tr,), in_specs=[spec, spec],
        out_specs=pl.BlockSpec((tr, wd), lambda i, k: (i, 0)))
    return _pcall(body, name=name, grid_spec=grid_spec, out_shape=jax.ShapeDtypeStruct((hr, wd), F32),
                  compiler_params=_params(("parallel",)))(_index_operand(chip), landed, pairs)


class _ReduceScatter:
    def __init__(self, gs, chip, ci, tag):
        self.gs, self.chip, self.ci, self.tag = gs, chip, ci, tag
        self.half_t = [jax.ShapeDtypeStruct((g.shape[0],) + g.shape[2:], BF16) for g in gs]

    def pair(self, carrier=None):
        plan = lambda x, y, c: [((slice(None), 1 - c), (), (), (x, y, 1 - c))]
        self.r1 = _Ride(self.gs, self.half_t, plan)
        _round(self.r1, carrier, f"reduce_{self.tag}_pair")

    def chips(self, carrier=None):
        def plan(x, y, c):
            k = 2 * x + y
            return [((2 * px + py,), (k,), (2 * px + py,), (px, py, c)) for px, py in _other_chips(x, y)]

        self.pairs = [_add_pairs(g, f, self.ci, name=f"reduce_{self.tag}_pair_add{n}")
                      for n, (g, f) in enumerate(zip(self.gs, self.r1.out))]
        self.r2 = _Ride(self.pairs, self.half_t, plan)
        _round(self.r2, carrier, f"reduce_{self.tag}_chips")

    def halves(self, carrier=None):
        plan = lambda x, y, c: [((), (c,), (1 - c,), (x, y, 1 - c))]
        self.mine = [_sum_chips(l, p, self.chip, name=f"reduce_{self.tag}_chip_sum{n}")
                     for n, (l, p) in enumerate(zip(self.r2.out, self.pairs))]
        self.r3 = _Ride(self.mine, [jax.ShapeDtypeStruct((2,) + m.shape, F32) for m in self.mine], plan)
        _round(self.r3, carrier, f"reduce_{self.tag}_halves")

    def result(self):
        return [lax.dynamic_update_index_in_dim(b, m, self.ci, 0).reshape(2 * m.shape[0], m.shape[1])
                for b, m in zip(self.r3.out, self.mine)]


def _ada_fwd(c_all, w_sh, b_sh, *, name):
    nb, d = c_all.shape
    wcols = w_sh.shape[1]
    tn = _pick(wcols, 512)

    def body(c_ref, w_ref, b_ref, o_ref, a_ref):
        cv = c_ref[...]
        act = cv * jax.nn.sigmoid(cv)
        a_ref[...] = act
        o_ref[...] = jnp.dot(act.astype(BF16), w_ref[...].astype(BF16), preferred_element_type=F32) + b_ref[...]

    return _pcall(
        body, name=name, grid=(wcols // tn,),
        in_specs=[pl.BlockSpec((nb, d), lambda j: (0, 0)), pl.BlockSpec((d, tn), lambda j: (0, j)),
                  pl.BlockSpec((1, tn), lambda j: (0, j))],
        out_specs=[pl.BlockSpec((nb, tn), lambda j: (0, j)), pl.BlockSpec((nb, d), lambda j: (0, 0))],
        out_shape=[jax.ShapeDtypeStruct((nb, wcols), F32), jax.ShapeDtypeStruct((nb, d), F32)],
        compiler_params=_params(("arbitrary",)))(c_all, w_sh, b_sh)


def _sum_devices(g, *, name):
    nd, m, w = g.shape

    def body(g_ref, o_ref):
        acc = g_ref[0]
        for k in range(1, nd):
            acc = acc + g_ref[k]
        o_ref[...] = acc

    return _pcall(body, name=name, out_shape=jax.ShapeDtypeStruct((m, w), F32),
                  compiler_params=pltpu.CompilerParams(vmem_limit_bytes=VMEM_LIMIT_BYTES))(g)


def _adamw(w, g, m, v, *, name):
    rows, cols = w.shape[-2:]
    tr = _pick(rows, max(8, (1 << 18) // cols), 8)
    c1 = 1.0 / (1.0 - ADAM_B1 ** ADAM_STEP)
    c2 = 1.0 / (1.0 - ADAM_B2 ** ADAM_STEP)

    def body(w_ref, g_ref, m_ref, v_ref, d_ref, nm_ref, nv_ref):
        gv = g_ref[...]
        nm = ADAM_B1 * m_ref[...] + (1.0 - ADAM_B1) * gv
        nv = ADAM_B2 * v_ref[...] + (1.0 - ADAM_B2) * (gv * gv)
        d_ref[...] = -ADAM_LR * ((nm * c1) / (jnp.sqrt(nv * c2) + ADAM_EPS) + ADAM_WD * w_ref[...])
        nm_ref[...] = nm
        nv_ref[...] = nv

    gspec = pl.BlockSpec((tr, cols), lambda i: (i, 0))
    spec = pl.BlockSpec((None, tr, cols), lambda i: (0, i, 0)) if w.ndim == 3 else gspec
    shp = jax.ShapeDtypeStruct(w.shape, F32)
    return _pcall(body, name=name, grid=(rows // tr,), in_specs=[spec, gspec, spec, spec], out_specs=[spec] * 3,
                  out_shape=[shp] * 3, compiler_params=_params(("parallel",)))(w, g, m, v)


def _permute_in_rows(wt):
    ngrp = len(B_PATTERNS)
    qb, kb, vb = (wt[A_W + n * QB_W:A_W + (n + 1) * QB_W] for n in range(3))
    parts = [wt[:A_W], jnp.zeros((VAR_W - A_W, wt.shape[1]), wt.dtype)]
    for g in range(ngrp):
        parts += [t[g * GB_W:(g + 1) * GB_W] for t in (qb, kb, vb)]
    return jnp.concatenate(parts + [wt[A_W + 3 * QB_W:]], axis=0)


def _unpermute_in_grads(pieces):
    ga, groups, gg = pieces[0], pieces[1:-1], pieces[-1]
    rows = [ga[:A_W]]
    for n in range(3):
        rows += [gp[n * GB_W:(n + 1) * GB_W] for gp in groups]
    return jnp.concatenate(rows + [gg], axis=0)


def kernel(x, c, positions, w_ada, b_ada, w_in, sinks, w_branch_a, w_branch_b, w_o, ln1_g, ln1_b, w_gate_up, w_down, ln2_g, ln2_b, loss_target, m_w_ada, m_b_ada, m_w_in, m_sinks, m_w_branch_a, m_w_branch_b, m_w_o, m_ln1_g, m_ln1_b, m_w_gate_up, m_w_down, m_ln2_g, m_ln2_b, v_w_ada, v_b_ada, v_w_in, v_sinks, v_w_branch_a, v_w_branch_b, v_w_o, v_ln1_g, v_ln1_b, v_w_gate_up, v_w_down, v_ln2_g, v_ln2_b):
    xi, yi, ci = _coords()
    chip = 2 * xi + yi
    dev = 4 * xi + 2 * yi + ci
    NB, T, D = x.shape
    nchip, ndev = 4, 8
    ada_cols = w_ada.shape[2]

    ra, ro, rd = w_branch_a.shape[1], w_o.shape[1], w_down.shape[1]
    rowsh = jnp.concatenate([w_branch_a[0], w_o[0], w_down[0]], axis=0)
    halves = lambda a: a.reshape(a.shape[:-2] + (2, a.shape[-2] // 2, a.shape[-1]))
    tr = lambda a: jnp.swapaxes(a, -1, -2)
    shards = [halves(w.astype(BF16)) for w in (tr(w_in[0]), rowsh, w_branch_b[0], w_gate_up[0])]
    gin = _Gather(shards[:1], chip, "w_in", carriers=("gather_c", "gather_mod"))

    c_blk = jnp.zeros((8, D), F32).at[:NB].set(c)
    c_all = _allgather_small(c_blk, name="gather_c").reshape(ndev, 8, D)[:, :NB].reshape(ndev * NB, D)
    b_sh = lax.dynamic_slice(b_ada, (0, chip * ada_cols), (1, ada_cols))
    mod_part, c_act = _ada_fwd(c_all, w_ada[0], b_sh, name="ada_fwd")
    mod_g = _allgather_small(mod_part, name="gather_mod").reshape(nchip, 2, ndev * NB, ada_cols)[:, 0]
    mod_all = jnp.transpose(mod_g, (1, 0, 2)).reshape(ndev * NB, nchip * ada_cols)
    mod = lax.dynamic_slice(mod_all, (NB * dev, 0), (NB, nchip * ada_cols))

    (g_in,) = gin.result()
    w_in_f = _permute_in_rows(g_in.reshape(nchip * g_in.shape[1], D))
    mix = _Gather(shards[1:3], chip, "w_mix", carriers=("inproj_qkv", "attn_a_fwd"))
    ffn = _Gather(shards[3:], chip, "w_ffn", carriers=("attn_a_fwd", "attn_b0_fwd"))

    def rest_weights():
        (g_rows, w_b_f), (w_gu_f,) = mix.result(), ffn.result()
        return (g_rows[:, :ra].reshape(nchip * ra, D), w_b_f, g_rows[:, ra:ra + ro].reshape(nchip * ro, D), w_gu_f,
                g_rows[:, ra + ro:].reshape(nchip * rd, D))

    red = {}

    def hook(event, **g):
        if event == "rest_grads":
            gr_rows = jnp.concatenate([g["g_wa"].reshape(nchip, ra, D), g["g_wo"].reshape(nchip, ro, D),
                                       g["g_wd"].reshape(nchip, rd, D)], axis=1)
            red["ffn"] = _ReduceScatter([halves(g["g_wgu"])], chip, ci, "ffn")
            red["mix"] = _ReduceScatter([halves(gr_rows), halves(g["g_wb"])], chip, ci, "mix")
            red["ffn"].pair(carrier="merge_bwd")
            red["mix"].pair(carrier="merge_bwd")
        elif event == "merge_bwd_done":
            red["ffn"].chips(carrier="attn_a_bwd")
            red["mix"].chips(carrier="attn_b0_bwd")
        elif event == "attn_a_bwd_done":
            red["ffn"].halves(carrier="attn_b0_bwd")
        elif event == "attn_b0_bwd_done":
            red["mix"].halves(carrier="attn_b1_bwd")
        elif event == "win_grads":
            gr_in = _unpermute_in_grads(g["g_win"])
            red["w_in"] = _ReduceScatter([halves(gr_in.reshape(nchip, gr_in.shape[0] // nchip, D))], chip, ci, "w_in")
            red["w_in"].pair(carrier="inproj_dx0")
        elif event == "inproj_dx0_done":
            red["w_in"].chips(carrier="inproj_dx1")
        elif event == "inproj_dx1_done":
            red["w_in"].halves(carrier="gather_small")

    res = _local_step(x, mod, positions, w_in_f, rest_weights, sinks[0], ln1_g, ln1_b, ln2_g, ln2_b, loss_target, hook)
    (g_rows_red, g_w_b), (g_w_gu,) = red["mix"].result(), red["ffn"].result()
    g_w_a, g_w_o, g_w_d = g_rows_red[:ra], g_rows_red[ra:ra + ro], g_rows_red[ra + ro:]

    small_rows = 24
    misc = jnp.zeros((1, D), F32).at[0, :A_Q_HEADS].set(res["dsink"]).at[0, A_Q_HEADS].set(jnp.sum(res["loss"]))
    small = jnp.concatenate([res["dmod"].reshape(NB * 6, D), jnp.sum(res["ln_grads"], axis=0), misc,
                             jnp.zeros((small_rows - NB * 6 - 5, D), F32)], axis=0)
    small_all = _allgather_small(small, name="gather_small").reshape(ndev, small_rows, D)
    (g_w_in,) = red["w_in"].result()
    dmod_all = small_all[:, :NB * 6].reshape(ndev * NB, 6 * D)
    sums = _sum_devices(small_all, name="sum_small")
    g_b_ada = (sums[0:6] + sums[6:12]).reshape(1, 6 * D)
    g_ln1_g, g_ln1_b, g_ln2_g, g_ln2_b = (sums[12 + n][None] for n in range(4))
    g_sinks = sums[16, :A_Q_HEADS][None]
    loss = sums[16, A_Q_HEADS]
    dmod_sh = lax.dynamic_slice(dmod_all, (0, chip * ada_cols), (ndev * NB, ada_cols))
    g_w_ada = _mm(c_act, dmod_sh, ta=True, name="ada_dw")

    names = ["w_ada", "b_ada", "w_in", "sinks", "w_branch_a", "w_branch_b", "w_o", "ln1_g", "ln1_b",
             "w_gate_up", "w_down", "ln2_g", "ln2_b"]
    ws = [w_ada, b_ada, w_in, sinks, w_branch_a, w_branch_b, w_o, ln1_g, ln1_b, w_gate_up, w_down, ln2_g, ln2_b]
    ms = [m_w_ada, m_b_ada, m_w_in, m_sinks, m_w_branch_a, m_w_branch_b, m_w_o, m_ln1_g, m_ln1_b, m_w_gate_up,
          m_w_down, m_ln2_g, m_ln2_b]
    vs = [v_w_ada, v_b_ada, v_w_in, v_sinks, v_w_branch_a, v_w_branch_b, v_w_o, v_ln1_g, v_ln1_b, v_w_gate_up,
          v_w_down, v_ln2_g, v_ln2_b]
    gs = [g_w_ada, g_b_ada, g_w_in, g_sinks, g_w_a, g_w_b, g_w_o, g_ln1_g, g_ln1_b, g_w_gu, g_w_d, g_ln2_g, g_ln2_b]
    grads, deltas, new_ms, new_vs = [], [], [], []
    for name, w, g, m, v in zip(names, ws, gs, ms, vs):
        flip = tr if name == "w_in" else (lambda a: a)
        w, m, v = flip(w), flip(m), flip(v)
        g2 = g.reshape(w.shape[-2:])
        d, nm, nv = _adamw(w, g2, m, v, name="adamw_" + name)
        grads.append(flip(g2.reshape(w.shape)))
        deltas.append(flip(d))
        new_ms.append(flip(nm))
        new_vs.append(flip(nv))
    return (loss, res["grad_x"], *grads, *deltas, *new_ms, *new_vs)
```

```python
import functools

import jax
import jax.numpy as jnp
from jax import lax
from jax.experimental import pallas as pl
from jax.experimental.pallas import tpu as pltpu

F32 = jnp.float32
BF16 = jnp.bfloat16
MESH = pl.DeviceIdType.MESH

HEAD_DIM = 64
LANES = 128
PAIR_W = 2 * HEAD_DIM
BLOCK = 128
A_Q_HEADS = 16
A_KV_HEADS = 2
A_WINDOW = 128
B_PATTERNS = ((128, 1), (512, 4), (2048, 16))
B_GROUP_HEADS = 8
QA_W = A_Q_HEADS * HEAD_DIM
KA_W = A_KV_HEADS * HEAD_DIM
GB_W = B_GROUP_HEADS * HEAD_DIM
QB_W = GB_W * len(B_PATTERNS)
A_W = QA_W + 2 * KA_W
VAR_W = 3 * GB_W
N_VAR = 1 + len(B_PATTERNS)
VAR_DIL = (1,) + tuple(r for _, r in B_PATTERNS)
QKV_P = N_VAR * VAR_W
ROPE_THETA = 10000.0
LN_EPS = 1e-5
NEG_INF = -1e30
DEPTH = 1
ALPHA = (2 * DEPTH) ** 0.25
SCALE = HEAD_DIM ** -0.5

ADAM_LR, ADAM_B1, ADAM_B2, ADAM_EPS, ADAM_WD, ADAM_STEP = 0.001, 0.9, 0.999, 1e-08, 0.01, 10

VMEM_LIMIT_BYTES = 56 * 1024 * 1024
MM_TILE_BYTES = 36 * 1024 * 1024
MM_WHOLE_K = 4096


def _params(sem=None):
    return pltpu.CompilerParams(dimension_semantics=sem, vmem_limit_bytes=VMEM_LIMIT_BYTES)


_RIDES = {}


def _pcall(body, *, name, **kw):
    rides = _RIDES.pop(name, None)
    if rides is None:
        return pl.pallas_call(body, name=name, **kw)
    return _riding_call(body, rides, name=name, **kw)


def _copies(src_refs, dst_refs, send_sems, recv_sems, plan):
    x, y, c = lax.axis_index("x"), lax.axis_index("y"), lax.axis_index("c")
    remote = plan(x, y, c)
    nrem = len(remote)
    at = lambda ref, idx: ref.at[idx] if idx else ref

    def copy(a, n, landing):
        si, di, ri, peer = remote[n]
        return pltpu.make_async_remote_copy(
            src_ref=at(src_refs[a], si), dst_ref=at(dst_refs[a], ri if landing else di),
            send_sem=send_sems.at[a * nrem + n], recv_sem=recv_sems.at[a * nrem + n],
            device_id=peer, device_id_type=MESH)

    order = [(a, n) for a in range(len(dst_refs)) for n in range(nrem)]

    def start():
        for a, n in order:
            copy(a, n, False).start()

    def wait():
        for a, n in order:
            copy(a, n, True).wait_recv()
        for a, n in order:
            copy(a, n, False).wait_send()

    return start, wait


class _Ride:
    def __init__(self, srcs, dsts, plan, dst_inits=None):
        self.srcs, self.dsts, self.plan, self.dst_inits, self.out = srcs, dsts, plan, dst_inits, None


def _riding_call(body, rides, *, name, in_specs, out_specs, out_shape, grid=(), scratch_shapes=(), **kw):
    single = not isinstance(out_specs, (list, tuple))
    out_specs = [out_specs] if single else list(out_specs)
    out_shape = [out_shape] if single else list(out_shape)
    n_in, n_out, n_scr = len(in_specs), len(out_specs), len(scratch_shapes)
    xin, xdsts, sems, aliases, layout = [], [], [], {}, []
    for ride in rides:
        srcs = ride.srcs() if callable(ride.srcs) else ride.srcs
        inits = ride.dst_inits() if callable(ride.dst_inits) else ride.dst_inits
        na, nrem = len(ride.dsts), len(ride.plan(0, 0, 0))
        src_at = len(xin) if srcs is not None else None
        xin += list(srcs) if srcs is not None else []
        if inits is not None:
            aliases.update({n_in + len(xin) + a: n_out + len(xdsts) + a for a in range(na)})
            xin += list(inits)
        layout.append((src_at, len(xdsts), na))
        xdsts += list(ride.dsts)
        sems += [pltpu.SemaphoreType.DMA((na * nrem,)), pltpu.SemaphoreType.DMA((na * nrem,))]

    def wrapped(*refs):
        ins, xins = refs[:n_in], refs[n_in:n_in + len(xin)]
        outs = refs[n_in + len(xin):n_in + len(xin) + n_out]
        xouts = refs[n_in + len(xin) + n_out:n_in + len(xin) + n_out + len(xdsts)]
        scr = refs[n_in + len(xin) + n_out + len(xdsts):]
        rounds = []
        for k, (ride, (src_at, dst_at, na)) in enumerate(zip(rides, layout)):
            dsts = xouts[dst_at:dst_at + na]
            srcs = dsts if src_at is None else xins[src_at:src_at + na]
            rounds.append(_copies(srcs, dsts, scr[n_scr + 2 * k], scr[n_scr + 2 * k + 1], ride.plan))
        ids = [pl.program_id(a) for a in range(len(grid))]
        first = functools.reduce(jnp.logical_and, [i == 0 for i in ids], True)
        last = functools.reduce(jnp.logical_and, [i == g - 1 for i, g in zip(ids, grid)], True)

        def start_all():
            for start, _ in rounds:
                start()

        def wait_all():
            for _, wait in rounds:
                wait()

        start_all() if not grid else pl.when(first)(start_all)
        body(*ins, *outs, *scr[:n_scr])
        wait_all() if not grid else pl.when(last)(wait_all)

    hbm = pl.BlockSpec(memory_space=pl.ANY)
    gridkw = dict(grid=grid) if grid else {}

    def run(*args):
        res = pl.pallas_call(
            wrapped, name=name, in_specs=list(in_specs) + [hbm] * len(xin),
            out_specs=out_specs + [hbm] * len(xdsts), out_shape=out_shape + xdsts,
            scratch_shapes=list(scratch_shapes) + sems, input_output_aliases=aliases,
            compiler_params=_params(("arbitrary",) * len(grid) if grid else None), **gridkw,
        )(*args, *xin)
        for ride, (_, dst_at, na) in zip(rides, layout):
            ride.out = list(res[n_out + dst_at:n_out + dst_at + na])
        return res[0] if single else list(res[:n_out])

    return run


def _pick(n, target, quantum=128):
    t = (min(target, n) // quantum) * quantum
    while t >= quantum:
        if n % t == 0:
            return t
        t -= quantum
    return n


def _mm(a, b, *, name, ta=False, tb=False, b3=False, out3=0, out_dtype=F32, add=None, tm=1024, tn=1536, tk=1536):
    if ta:
        K, M = a.shape
    else:
        M, K = a.shape
    if b3 and tb:
        Nn, K2, tk = b.shape[1], b.shape[0] * b.shape[2], b.shape[2]
    elif b3:
        K2, Nn, tn = b.shape[1], b.shape[0] * b.shape[2], b.shape[2]
    elif tb:
        Nn, K2 = b.shape
    else:
        K2, Nn = b.shape
    assert K == K2, (a.shape, b.shape)
    if out3:
        tn = Nn // out3
    tm, tn, tk = _pick(M, tm), _pick(Nn, tn), _pick(K, tk)
    if not (b3 and tb) and K <= MM_WHOLE_K:
        tk = K
        fits = lambda: 4 * tk * (tm + tn) + 8 * tm * tn * (2 if add is not None else 1) <= MM_TILE_BYTES
        while not fits():
            if (tm >= tn or b3 or out3) and tm > 256:
                tm = _pick(M, tm - 128)
            elif not (b3 or out3) and tn > 256:
                tn = _pick(Nn, tn - 128)
            else:
                break
    nk = K // tk
    j_outer = K * Nn + (Nn // tn) * M * K < M * K + (M // tm) * K * Nn
    dn = (((0 if ta else 1,), (1 if tb else 0,)), ((), ()))

    def body(*refs):
        refs = list(refs)
        a_ref, b_ref = refs[:2]
        add_ref = refs[2] if add is not None else None
        o_ref = refs[3] if add is not None else refs[2]
        part = lax.dot_general(a_ref[...].astype(BF16), b_ref[...].astype(BF16), dn, preferred_element_type=F32)

        def finish(r):
            if add is not None:
                r = r + add_ref[...]
            o_ref[...] = r.astype(out_dtype)

        if nk == 1:
            finish(part)
            return
        acc = refs[-1]
        k = pl.program_id(2)

        @pl.when(k == 0)
        def _():
            acc[...] = part

        @pl.when(k > 0)
        def _():
            acc[...] += part

        @pl.when(k == nk - 1)
        def _():
            finish(acc[...])

    def spec(shape, index):
        return pl.BlockSpec(shape, (lambda j, i, k: index(i, j, k)) if j_outer else index)

    a_spec = spec((tk, tm), lambda i, j, k: (k, i)) if ta else spec((tm, tk), lambda i, j, k: (i, k))
    if b3 and tb:
        b_spec = spec((None, tn, tk), lambda i, j, k: (k, j, 0))
    elif b3:
        b_spec = spec((None, tk, tn), lambda i, j, k: (j, k, 0))
    elif tb:
        b_spec = spec((tn, tk), lambda i, j, k: (j, k))
    else:
        b_spec = spec((tk, tn), lambda i, j, k: (k, j))
    if out3:
        o_spec = spec((None, tm, tn), lambda i, j, k: (j, i, 0))
    else:
        o_spec = spec((tm, tn), lambda i, j, k: (i, j))
    ins, specs = [a, b], [a_spec, b_spec]
    if add is not None:
        ins.append(add)
        specs.append(o_spec)
    grid = (Nn // tn, M // tm, nk) if j_outer else (M // tm, Nn // tn, nk)
    return _pcall(
        body, name=name, grid=grid, in_specs=specs, out_specs=o_spec,
        out_shape=jax.ShapeDtypeStruct((out3, M, tn) if out3 else (M, Nn), out_dtype),
        scratch_shapes=[pltpu.VMEM((tm, tn), F32)] if nk > 1 else [],
        compiler_params=_params(("parallel", "parallel", "arbitrary")),
    )(*ins)


def _mm_multi(a_list, b_list, *, name, M, T=None, add=None, out_dtype=F32, tm=512):
    tm = _pick(T or M, tm)
    ns = len(a_list)
    dils = [a[1] if isinstance(a, tuple) else 0 for a in a_list]
    a_arrs = [a[0] if isinstance(a, tuple) else a for a in a_list]
    widths = [a.shape[-1] // max(r, 1) for a, r in zip(a_arrs, dils)]
    b_arrs, b_specs = [], []
    for b in b_list:
        arr, shp, idx = b if isinstance(b, tuple) else (b, b.shape, (0, 0))
        b_arrs.append(arr)
        b_specs.append(pl.BlockSpec(shp, lambda i, idx=idx: idx))
    Nn = b_specs[0].block_shape[1]
    dn = (((1,), (0,)), ((), ()))
    nin = 2 * ns + (1 if add is not None else 0)

    def body(*refs):
        a_refs, b_refs, scr = refs[:ns], refs[ns:2 * ns], list(refs[nin + 1:])
        acc = None
        for a_ref, b_ref, r in zip(a_refs, b_refs, dils):
            av = _from_view(a_ref, scr.pop(0), r) if r > 1 else a_ref[...]
            part = lax.dot_general(av.astype(BF16), b_ref[...], dn, preferred_element_type=F32)
            acc = part if acc is None else acc + part
        if add is not None:
            acc = acc + refs[2 * ns][...]
        refs[nin][...] = acc.astype(out_dtype)

    tpe = (T or M) // tm
    a_specs = [pl.BlockSpec((None, tm // r, r * w), lambda i: (i // tpe, i % tpe, 0)) if r
               else pl.BlockSpec((tm, w), lambda i: (i, 0)) for r, w in zip(dils, widths)]
    o_spec = pl.BlockSpec((tm, Nn), lambda i: (i, 0))
    specs = a_specs + b_specs
    ins = a_arrs + b_arrs
    if add is not None:
        specs.append(o_spec)
        ins.append(add)
    scratch = [pltpu.VMEM((w // LANES, tm, LANES), F32) for r, w in zip(dils, widths) if r > 1]
    return _pcall(body, name=name, grid=(M // tm,), in_specs=specs, out_specs=o_spec, scratch_shapes=scratch,
                  out_shape=jax.ShapeDtypeStruct((M, Nn), out_dtype), compiler_params=_params(("parallel",)))(*ins)


def _dw_view(d3, u, r, *, name, tk=1024):
    NB, tsub, rw = d3.shape
    W, T, D = rw // r, tsub * r, u.shape[1]
    tk = _pick(T, tk)
    tpe, nk = T // tk, NB * T // tk

    def body(d_ref, u_ref, o_ref, acc, scr):
        k = pl.program_id(0)
        dv = _from_view(d_ref, scr, r).astype(BF16)
        part = lax.dot_general(dv, u_ref[...], _TN, preferred_element_type=F32)

        @pl.when(k == 0)
        def _():
            acc[...] = part

        @pl.when(k > 0)
        def _():
            acc[...] += part

        @pl.when(k == nk - 1)
        def _():
            o_ref[...] = acc[...].astype(o_ref.dtype)

    return _pcall(
        body, name=name, grid=(nk,),
        in_specs=[pl.BlockSpec((None, tk // r, rw), lambda k: (k // tpe, k % tpe, 0)), pl.BlockSpec((tk, D), lambda k: (k, 0))],
        out_specs=pl.BlockSpec((W, D), lambda k: (0, 0)), out_shape=jax.ShapeDtypeStruct((W, D), BF16),
        scratch_shapes=[pltpu.VMEM((W, D), F32), pltpu.VMEM((W // LANES, tk, LANES), F32)],
        compiler_params=_params(("arbitrary",)))(d3, u)


def _lane(shape):
    return lax.broadcasted_iota(jnp.int32, shape, len(shape) - 1)


def _rot_half(v):
    w = v.shape[-1]
    first = (_lane(v.shape) % HEAD_DIM) < (HEAD_DIM // 2)
    return jnp.where(first, pltpu.roll(v, w - HEAD_DIM // 2, v.ndim - 1), pltpu.roll(v, HEAD_DIM // 2, v.ndim - 1))


def _widen(t, w):
    return t if w == t.shape[-1] else jnp.concatenate([t] * (w // t.shape[-1]), axis=-1)


def _unrope(v, cos, sins):
    w = v.shape[-1]
    return v * _widen(cos, w) - _rot_half(v) * _widen(sins, w)


def _rope_tables(positions):
    half = HEAD_DIM // 2
    inv = ROPE_THETA ** (-jnp.arange(half, dtype=F32) / half)
    ang = positions.astype(F32)[..., None] * inv
    cos, sin = jnp.cos(ang), jnp.sin(ang)
    cosf = jnp.concatenate([cos, cos, cos, cos], axis=-1)
    sins = jnp.concatenate([-sin, sin, -sin, sin], axis=-1)
    n = positions.shape[0] * positions.shape[1]
    return cosf.reshape(n, PAIR_W), sins.reshape(n, PAIR_W)


def _inproj(x2, scale, shift, w, cosf, sins, flags, *, T, name):
    N, D = x2.shape
    tm, tn = _pick(T, 512), VAR_W
    tpe = T // tm

    def body(x_ref, sc_ref, sh_ref, w_ref, c_ref, s_ref, f_ref, *outs):
        o_refs, u_ref = outs[:N_VAR], outs[N_VAR]
        j = pl.program_id(1)

        @pl.when(j == 0)
        def _():
            u_ref[...] = (x_ref[...] * (1.0 + sc_ref[0]) + sh_ref[0]).astype(BF16)

        acc = lax.dot_general(u_ref[...], w_ref[...], (((1,), (1,)), ((), ())), preferred_element_type=F32)
        fl = f_ref[...]
        ce = 1.0 + (_widen(c_ref[...], tn) - 1.0) * fl
        se = _widen(s_ref[...], tn) * fl
        res = acc * ce + _rot_half(acc) * se
        for v in range(N_VAR):
            @pl.when(j == v)
            def _(v=v):
                _to_view(res, o_refs[v], outs[N_VAR + 1], VAR_DIL[v])

    ex = pl.BlockSpec((1, 1, D), lambda i, j: (i // tpe, 0, 0))
    tab = pl.BlockSpec((tm, PAIR_W), lambda i, j: (i, 0))
    keep = lambda w_: pl.BlockSpec((tm, w_), lambda i, j: (i, 0))
    vspec = lambda r: pl.BlockSpec((None, tm // r, r * tn), lambda i, j: (i // tpe, i % tpe, 0))
    vshape = lambda r: jax.ShapeDtypeStruct((N // T, T // r, r * tn), BF16)
    return _pcall(
        body, name=name, grid=(N // tm, N_VAR),
        in_specs=[keep(D), ex, ex, pl.BlockSpec((tn, D), lambda i, j: (j, 0)), tab, tab,
                  pl.BlockSpec((1, tn), lambda i, j: (0, j))],
        out_specs=[vspec(r) for r in VAR_DIL] + [keep(D)],
        out_shape=[vshape(r) for r in VAR_DIL] + [jax.ShapeDtypeStruct((N, D), BF16)],
        scratch_shapes=[pltpu.VMEM((tn // LANES, tm, LANES), F32)],
        compiler_params=_params(("parallel", "arbitrary")),
    )(x2, scale, shift, w, cosf, sins, flags)


class _Geom:
    def __init__(self, g):
        if g is None:
            self.r, self.nq, self.n_back, self.sink = 1, A_Q_HEADS, A_WINDOW - 1, True
            self.qw, self.kw = QA_W, KA_W
            self.qidx = lambda j: 0
            self.kidx = lambda j: QA_W // KA_W
            self.vidx = lambda j: QA_W // KA_W + 1
        else:
            window, r = B_PATTERNS[g]
            self.r, self.nq, self.n_back, self.sink = r, B_GROUP_HEADS, window // r, False
            self.qw, self.kw = GB_W, GB_W
            self.qidx = lambda j: 3 * j
            self.kidx = lambda j: 3 * j + 1
            self.vidx = lambda j: 3 * j + 2
        self.ntile = self.qw // PAIR_W


def _stack_heads(t, scale=None):
    first = _lane(t.shape) < HEAD_DIM
    z = jnp.zeros_like(t)
    if scale is not None:
        t = t * jnp.asarray(scale, t.dtype)
    return jnp.concatenate([jnp.where(first, t, z), jnp.where(first, z, t)], axis=0)


def _lse_col(t):
    return jnp.concatenate([t[:, 0:1], t[:, HEAD_DIM:HEAD_DIM + 1]], axis=0)


def _lse_rows(t, width):
    first = _lane(t.shape) < HEAD_DIM
    other = pltpu.roll(t, HEAD_DIM, 1)
    full = jnp.concatenate([jnp.where(first, t, other), jnp.where(first, other, t)], axis=0)
    return _widen(full, width)


def _unstack_heads(v2):
    return jnp.where(_lane((BLOCK, PAIR_W)) < HEAD_DIM, v2[:BLOCK], v2[BLOCK:])


def _dup_head(t, kh):
    tf = t.astype(F32)
    keep = (_lane(t.shape) < HEAD_DIM) if kh == 0 else (_lane(t.shape) >= HEAD_DIM)
    return jnp.where(keep, tf, pltpu.roll(tf, HEAD_DIM, 1)).astype(t.dtype)


def _fold_heads(t):
    return t + pltpu.roll(t, HEAD_DIM, 1)


def _band_mask(rows, i, n_back, single):
    nkeys = BLOCK if single else 2 * BLOCK
    qi = jnp.bitwise_and(lax.broadcasted_iota(jnp.int32, (rows, nkeys), 0), BLOCK - 1)
    ki = lax.broadcasted_iota(jnp.int32, (rows, nkeys), 1)
    if single:
        return qi >= ki
    dist = qi + BLOCK - ki
    return jnp.logical_and(jnp.logical_and(dist >= 0, dist <= n_back), jnp.logical_or(ki >= BLOCK, i > 0))


def _per_block(col, scalars, fn):
    return jnp.concatenate([fn(col[b * BLOCK:(b + 1) * BLOCK], sc) for b, sc in enumerate(scalars)], axis=0)


def _sink_slot(rows):
    qi = jnp.bitwise_and(lax.broadcasted_iota(jnp.int32, (rows, 2 * BLOCK), 0), BLOCK - 1)
    return qi == lax.broadcasted_iota(jnp.int32, (rows, 2 * BLOCK), 1)


def _sink_scores(rows, sinks):
    blk = lax.broadcasted_iota(jnp.int32, (rows, 2 * BLOCK), 0) // BLOCK
    out = jnp.full((rows, 2 * BLOCK), sinks[-1], F32)
    for b in range(len(sinks) - 2, -1, -1):
        out = jnp.where(blk == b, sinks[b], out)
    return out


def _softmax_parts(s, valid, sinks):
    s = jnp.where(valid, s, NEG_INF)
    if sinks is not None:
        slot = _sink_slot(s.shape[0])
        s = jnp.where(slot, _sink_scores(s.shape[0], sinks), s)
    m = jnp.max(s, axis=1, keepdims=True)
    p = jnp.exp(s - m)
    den = jnp.sum(p, axis=1, keepdims=True)
    if sinks is not None:
        p = jnp.where(slot, 0.0, p)
    return p, m, den


_NT = (((1,), (1,)), ((), ()))
_TN = (((0,), (0,)), ((), ()))


def _rows2(prev_ref, cur_ref, cs, single=False):
    if single:
        return cur_ref[0, :, cs]
    return jnp.concatenate([prev_ref[0, :, cs], cur_ref[0, :, cs]], axis=0)


def _sink_scalars(sink_ref, first, nblocks):
    return [sink_ref[first + b] for b in range(nblocks)]


def _tile(t):
    return slice(t * PAIR_W, (t + 1) * PAIR_W)


def _attn_fwd(qkv, sinks, g, *, NB, T, name):
    geo = _Geom(g)
    r, qw, kw, ntile = geo.r, geo.qw, geo.kw, geo.ntile
    tsub = T // r
    nblk = tsub // BLOCK
    qkv3 = qkv.reshape(NB, tsub, r * VAR_W)
    out_dtype = BF16 if g is None else F32
    tiles_per_kv = ntile // A_KV_HEADS

    single = nblk == 1

    def body(q_ref, kp_ref, kc_ref, vp_ref, vc_ref, sink_ref, o_ref, l_ref):
        i = pl.program_id(2)
        if geo.sink:
            kall, vall = _rows2(kp_ref, kc_ref, _tile(0)), _rows2(vp_ref, vc_ref, _tile(0))
            kdup = [_dup_head(kall, kh) for kh in range(A_KV_HEADS)]
            vdup = [_dup_head(vall, kh) for kh in range(A_KV_HEADS)]
            tiles = [[t] for t in range(ntile)]
            q2s = [_stack_heads(q_ref[0, :, _tile(t)], SCALE) for t in range(ntile)]
            kks = [kdup[t // tiles_per_kv] for t in range(ntile)]
            vvs = [vdup[t // tiles_per_kv] for t in range(ntile)]
            sinkcols = [_sink_scalars(sink_ref, 2 * t, 2) for t in range(ntile)]
        else:
            tiles = [[t] for t in range(ntile)]
            q2s = [_stack_heads(q_ref[0, :, _tile(t)], SCALE) for t in range(ntile)]
            kks = [_rows2(kp_ref, kc_ref, _tile(t), single) for t in range(ntile)]
            vvs = [_rows2(vp_ref, vc_ref, _tile(t), single) for t in range(ntile)]
            sinkcols = [None] * ntile
        valid = _band_mask(q2s[0].shape[0], i, geo.n_back, single)
        ss = [lax.dot_general(q2, kk, _NT, preferred_element_type=F32) for q2, kk in zip(q2s, kks)]
        parts = [_softmax_parts(s, valid, sc) for s, sc in zip(ss, sinkcols)]
        o2s = [jnp.dot(p.astype(BF16), vv, preferred_element_type=F32) / den for (p, m, den), vv in zip(parts, vvs)]
        for ts, o2, (p, m, den) in zip(tiles, o2s, parts):
            lse2 = jnp.broadcast_to(m + jnp.log(den), (o2.shape[0], PAIR_W))
            for n, t in enumerate(ts):
                rows = slice(2 * BLOCK * n, 2 * BLOCK * (n + 1))
                o_ref[0, :, _tile(t)] = _unstack_heads(o2[rows]).astype(out_dtype)
                l_ref[0, :, _tile(t)] = _unstack_heads(lse2[rows])

    prev = lambda i: jnp.maximum(i - 1, 0)
    in_specs = [
        pl.BlockSpec((1, BLOCK, qw), lambda b, j, i: (b, i, geo.qidx(j))),
        pl.BlockSpec((1, BLOCK, kw), lambda b, j, i: (b, prev(i), geo.kidx(j))),
        pl.BlockSpec((1, BLOCK, kw), lambda b, j, i: (b, i, geo.kidx(j))),
        pl.BlockSpec((1, BLOCK, kw), lambda b, j, i: (b, prev(i), geo.vidx(j))),
        pl.BlockSpec((1, BLOCK, kw), lambda b, j, i: (b, i, geo.vidx(j))),
        pl.BlockSpec(memory_space=pltpu.SMEM),
    ]
    o_spec = pl.BlockSpec((1, BLOCK, qw), lambda b, j, i: (b, i, j))
    shape = (NB, tsub, r * qw)
    o, lse = _pcall(
        body, name=name, grid=(NB, r, nblk), in_specs=in_specs, out_specs=[o_spec, o_spec],
        out_shape=[jax.ShapeDtypeStruct(shape, out_dtype), jax.ShapeDtypeStruct(shape, F32)],
        compiler_params=_params(("parallel", "parallel", "arbitrary")),
    )(qkv3, qkv3, qkv3, qkv3, qkv3, sinks)
    return o, lse


def _attn_bwd(qkv, do, lse, dlse, cosf, sins, sinks, g, *, NB, T, name):
    geo = _Geom(g)
    r, qw, kw, ntile = geo.r, geo.qw, geo.kw, geo.ntile
    tsub = T // r
    nblk = tsub // BLOCK
    view = lambda a, w: a.reshape(NB, tsub, r * w)
    has_dlse = dlse is not None
    tiles_per_kv = ntile // A_KV_HEADS

    single = nblk == 1
    krows = BLOCK if single else 2 * BLOCK
    nsteps = 1 if single else nblk + 1

    def grads(q2s, kks, vvs, do2s, i, lserows, sinkcols, dlrows):
        nrow = q2s[0].shape[0]
        ki = lax.broadcasted_iota(jnp.int32, (krows, nrow), 0)
        qi = jnp.bitwise_and(lax.broadcasted_iota(jnp.int32, (krows, nrow), 1), BLOCK - 1)
        if single:
            valid = qi >= ki
        else:
            dist = qi + BLOCK - ki
            valid = jnp.logical_and(jnp.logical_and(dist >= 0, dist <= geo.n_back), jnp.logical_or(ki >= BLOCK, i > 0))
        sts = [lax.dot_general(kk, q2, _NT, preferred_element_type=F32) for q2, kk in zip(q2s, kks)]
        dpts = [lax.dot_general(vv, do2, _NT, preferred_element_type=F32) for do2, vv in zip(do2s, vvs)]
        pts, dsts, sks = [], [], []
        for st, dpt, ls, sc, dl in zip(sts, dpts, lserows, sinkcols, dlrows):
            sv = jnp.where(valid, st, NEG_INF)
            if sc is not None:
                slot = ki == qi
                blk = lax.broadcasted_iota(jnp.int32, (krows, nrow), 1) // BLOCK
                sink = jnp.full((krows, nrow), sc[-1], F32)
                for b in range(len(sc) - 2, -1, -1):
                    sink = jnp.where(blk == b, sc[b], sink)
                sv = jnp.where(slot, sink, sv)
                dpt = jnp.where(slot, 0.0, dpt)
            pt = jnp.exp(sv - ls)
            delta = jnp.sum(pt * dpt, axis=0, keepdims=True)
            if dl is not None:
                delta = delta - dl
            dst = pt * (dpt - delta)
            if sc is not None:
                cols = lambda a, b: a[:, b * BLOCK:(b + 1) * BLOCK]
                sks.append([jnp.sum(jnp.where(cols(slot, b), cols(dst, b), 0.0)) for b in range(len(sc))])
                dst, pt = jnp.where(slot, 0.0, dst), jnp.where(slot, 0.0, pt)
            else:
                sks.append(None)
            pts.append(pt.astype(BF16))
            dsts.append(dst.astype(BF16))
        dq2s = [lax.dot_general(dst, kk, _TN, preferred_element_type=F32) * SCALE for dst, kk in zip(dsts, kks)]
        dkks = [jnp.dot(dst, q2, preferred_element_type=F32) for dst, q2 in zip(dsts, q2s)]
        dvvs = [jnp.dot(pt, do2, preferred_element_type=F32) for pt, do2 in zip(pts, do2s)]
        return dq2s, dkks, dvvs, sks

    def stat_row(t):
        tt = t.T
        return jnp.concatenate([tt[0:1, :], tt[HEAD_DIM:HEAD_DIM + 1, :]], axis=1)

    def body(*refs):
        it = iter(refs)
        q_ref, kp_ref, kc_ref, vp_ref, vc_ref, do_ref, l_ref = (next(it) for _ in range(7))
        dl_ref = next(it) if has_dlse else None
        c_ref, s_ref, sink_ref, o_ref, ds_ref, dq_s, dk_s, dv_s, car_q, car_k, car_v = (next(it) for _ in range(11))
        b, j, i = pl.program_id(0), pl.program_id(1), pl.program_id(2)

        @pl.when(jnp.logical_and(b == 0, jnp.logical_and(j == 0, i == 0)))
        def _():
            ds_ref[...] = jnp.zeros_like(ds_ref)

        def compute():
            if geo.sink:
                kall, vall = _rows2(kp_ref, kc_ref, _tile(0)), _rows2(vp_ref, vc_ref, _tile(0))
                nb = 2 * tiles_per_kv
                tiles = [[kh * tiles_per_kv + t for t in range(tiles_per_kv)] for kh in range(A_KV_HEADS)]
                cat = lambda f, ts: jnp.concatenate([f(t) for t in ts], axis=0)
                dq2s, dkks, dvvs, sks = grads(
                    [cat(lambda t: _stack_heads(q_ref[0, :, _tile(t)], SCALE), ts) for ts in tiles],
                    [_dup_head(kall, kh) for kh in range(A_KV_HEADS)],
                    [_dup_head(vall, kh) for kh in range(A_KV_HEADS)],
                    [cat(lambda t: _stack_heads(do_ref[0, :, _tile(t)]), ts) for ts in tiles], i,
                    [jnp.concatenate([stat_row(l_ref[0, :, _tile(t)]) for t in ts], axis=1) for ts in tiles],
                    [_sink_scalars(sink_ref, kh * nb, nb) for kh in range(A_KV_HEADS)], [None] * A_KV_HEADS)
                lane1 = _lane((1, PAIR_W))
                dsink = jnp.zeros((1, PAIR_W), F32)
                for kh, (ts, dq2, sk) in enumerate(zip(tiles, dq2s, sks)):
                    for n, t in enumerate(ts):
                        dq_s[:, _tile(t)] = _unstack_heads(dq2[2 * BLOCK * n:2 * BLOCK * (n + 1)])
                    for bb in range(nb):
                        dsink = dsink + jnp.where(lane1 == kh * nb + bb, sk[bb], 0.0)
                second = _lane((krows, PAIR_W)) >= HEAD_DIM
                dk_s[...] = jnp.where(second, _fold_heads(dkks[1]), _fold_heads(dkks[0]))
                dv_s[...] = jnp.where(second, _fold_heads(dvvs[1]), _fold_heads(dvvs[0]))
                ds_ref[0:1, :] += dsink
            else:
                dq2s, dkks, dvvs, _ = grads(
                    [_stack_heads(q_ref[0, :, _tile(t)], SCALE) for t in range(ntile)],
                    [_rows2(kp_ref, kc_ref, _tile(t), single) for t in range(ntile)],
                    [_rows2(vp_ref, vc_ref, _tile(t), single) for t in range(ntile)],
                    [_stack_heads(do_ref[0, :, _tile(t)]) for t in range(ntile)], i,
                    [stat_row(l_ref[0, :, _tile(t)]) for t in range(ntile)], [None] * ntile,
                    [stat_row(dl_ref[0, :, _tile(t)]) for t in range(ntile)])
                for t in range(ntile):
                    dq_s[:, _tile(t)] = _unstack_heads(dq2s[t])
                    dk_s[0:krows, _tile(t)] = dkks[t]
                    dv_s[0:krows, _tile(t)] = dvvs[t]

        def emit(dq, dk, dv):
            cos, sn = c_ref[0], s_ref[0]
            o_ref[0, :, 0:qw] = _unrope(dq, cos, sn).astype(BF16)
            o_ref[0, :, qw:qw + kw] = _unrope(dk, cos, sn).astype(BF16)
            o_ref[0, :, qw + kw:qw + 2 * kw] = dv.astype(BF16)
            if qw + 2 * kw < VAR_W:
                o_ref[0, :, qw + 2 * kw:VAR_W] = jnp.zeros((BLOCK, VAR_W - qw - 2 * kw), BF16)

        if single:
            compute()
            emit(dq_s[...], dk_s[0:BLOCK, :], dv_s[0:BLOCK, :])
            return

        @pl.when(i == 0)
        def _():
            car_q[...] = jnp.zeros_like(car_q)
            car_k[...] = jnp.zeros_like(car_k)
            car_v[...] = jnp.zeros_like(car_v)

        @pl.when(i == nblk)
        def _():
            dk_s[...] = jnp.zeros_like(dk_s)
            dv_s[...] = jnp.zeros_like(dv_s)

        pl.when(i < nblk)(compute)
        emit(car_q[...], car_k[...] + dk_s[0:BLOCK, :], car_v[...] + dv_s[0:BLOCK, :])
        car_q[...] = dq_s[...]
        car_k[...] = dk_s[BLOCK:2 * BLOCK, :]
        car_v[...] = dv_s[BLOCK:2 * BLOCK, :]

    cur = lambda i: jnp.minimum(i, nblk - 1)
    prv = lambda i: jnp.maximum(jnp.minimum(i, nblk - 1) - 1, 0)
    outb = lambda i: jnp.maximum(i - 1, 0)
    qrow = pl.BlockSpec((1, BLOCK, qw), lambda b, j, i: (b, cur(i), j))
    in_specs = [
        pl.BlockSpec((1, BLOCK, qw), lambda b, j, i: (b, cur(i), geo.qidx(j))),
        pl.BlockSpec((1, BLOCK, kw), lambda b, j, i: (b, prv(i), geo.kidx(j))),
        pl.BlockSpec((1, BLOCK, kw), lambda b, j, i: (b, cur(i), geo.kidx(j))),
        pl.BlockSpec((1, BLOCK, kw), lambda b, j, i: (b, prv(i), geo.vidx(j))),
        pl.BlockSpec((1, BLOCK, kw), lambda b, j, i: (b, cur(i), geo.vidx(j))),
        qrow, qrow,
    ]
    ins = [view(qkv, VAR_W)] * 5 + [view(do, qw), view(lse, qw)]
    if has_dlse:
        in_specs.append(qrow)
        ins.append(view(dlse, qw))
    in_specs += [
        pl.BlockSpec((1, BLOCK, PAIR_W), lambda b, j, i: (b, outb(i), j)),
        pl.BlockSpec((1, BLOCK, PAIR_W), lambda b, j, i: (b, outb(i), j)),
        pl.BlockSpec(memory_space=pltpu.SMEM),
    ]
    ins += [view(cosf, PAIR_W), view(sins, PAIR_W), sinks]
    scratch = [pltpu.VMEM((BLOCK, qw), F32), pltpu.VMEM((2 * BLOCK, kw), F32), pltpu.VMEM((2 * BLOCK, kw), F32),
               pltpu.VMEM((BLOCK, qw), F32), pltpu.VMEM((BLOCK, kw), F32), pltpu.VMEM((BLOCK, kw), F32)]
    dqkv, dsink = _pcall(
        body, name=name, grid=(NB, r, nsteps), in_specs=in_specs,
        out_specs=[pl.BlockSpec((1, BLOCK, VAR_W), lambda b, j, i: (b, outb(i), j)),
                   pl.BlockSpec((8, PAIR_W), lambda b, j, i: (0, 0))],
        out_shape=[jax.ShapeDtypeStruct((NB, tsub, r * VAR_W), BF16), jax.ShapeDtypeStruct((8, PAIR_W), F32)],
        scratch_shapes=scratch, compiler_params=_params(("arbitrary", "arbitrary", "arbitrary")),
    )(*ins)
    return dqkv, dsink


class _Rows:
    def __init__(self, N, T, tm):
        self.N, self.tm, self.tpe, self.grid = N, tm, T // tm, (N // tm,)

    def row(self, w, col=0):
        return pl.BlockSpec((self.tm, w), lambda i: (i, col))

    def ex(self, w):
        return pl.BlockSpec((1, 1, w), lambda i: (i // self.tpe, 0, 0))

    def const(self, shape):
        return pl.BlockSpec(shape, lambda i: tuple(0 for _ in shape))

    def view(self, w, r):
        return pl.BlockSpec((None, self.tm // r, r * w), lambda i: (i // self.tpe, i % self.tpe, 0))

    def first_of_example(self):
        return pl.program_id(0) % self.tpe == 0


def _acc(ref, first, val):
    @pl.when(first)
    def _():
        ref[0] = val

    @pl.when(jnp.logical_not(first))
    def _():
        ref[0] += val


def _colsum(v):
    return jnp.sum(v, axis=0, keepdims=True)


def _ln_stats(r):
    mu = jnp.mean(r, axis=-1, keepdims=True)
    xc = r - mu
    var = jnp.mean(xc * xc, axis=-1, keepdims=True)
    rstd = lax.rsqrt(var + LN_EPS)
    return xc * rstd, rstd


def _ln_bwd(dy, xhat, rstd, gain):
    dxh = dy * gain
    return rstd * (dxh - jnp.mean(dxh, axis=-1, keepdims=True) - xhat * jnp.mean(dxh * xhat, axis=-1, keepdims=True))


def _from_view(ref, scr, r):
    if r == 1:
        return ref[...]
    rows, w = ref.shape[0], ref.shape[1] // r
    for j in range(r):
        for c in range(w // LANES):
            scr.at[c][pl.ds(j, rows, stride=r), :] = ref[:, j * w + c * LANES:j * w + (c + 1) * LANES].astype(F32)
    return jnp.concatenate([scr[c] for c in range(w // LANES)], axis=1)


def _to_view(val, ref, scr, r):
    if r == 1:
        ref[...] = val.astype(ref.dtype)
        return
    rows, w = ref.shape[0], ref.shape[1] // r
    for c in range(w // LANES):
        scr[c] = val[:, c * LANES:(c + 1) * LANES]
    for j in range(r):
        for c in range(w // LANES):
            ref[:, j * w + c * LANES:j * w + (c + 1) * LANES] = scr.at[c][pl.ds(j, rows, stride=r), :].astype(ref.dtype)


def _silu_parts(v):
    s = jax.nn.sigmoid(v)
    return v * s, s * (1.0 + v * (1.0 - s))


def _local_step(x, mod, positions, w_in, rest_weights, sinks, ln1_g, ln1_b, ln2_g, ln2_b, target, hook=None):
    hook = hook or (lambda event, **data: None)
    NB, T, D = x.shape
    N = NB * T
    x2 = x.reshape(N, D)
    tgt2 = target.reshape(N, D)
    shift_m, scale_m, gate_m, shift_f, scale_f, gate_f = [mod[:, None, k * D:(k + 1) * D] for k in range(6)]
    cosf, sins = _rope_tables(positions)
    col = jnp.arange(QKV_P)
    vcol = col % VAR_W
    flags = jnp.where(col < VAR_W, vcol < QA_W + KA_W, vcol < 2 * GB_W).astype(F32)[None]
    R = _Rows(N, T, _pick(T, 256))
    sds = jax.ShapeDtypeStruct
    exsum = lambda w=D: sds((NB, 1, w), F32)
    ngrp = len(B_PATTERNS)

    *qkv, u = _inproj(x2, scale_m, shift_m, w_in, cosf, sins, flags, T=T, name="inproj_qkv")
    gates = _mm(u, w_in[QKV_P:], tb=True, out_dtype=BF16, name="inproj_gates")
    oa, la = _attn_fwd(qkv[0], sinks, None, NB=NB, T=T, name="attn_a_fwd")
    oa = oa.reshape(N, QA_W)
    ob_parts = [_attn_fwd(qkv[1 + g], sinks, g, NB=NB, T=T, name=f"attn_b{g}_fwd") for g in range(ngrp)]
    (o1, l1), (o2, l2), (o3, l3) = ob_parts
    w_a, w_b, w_o, w_gu, w_d = rest_weights()
    F = w_d.shape[0]
    dil = [r_ for _, r_ in B_PATTERNS]
    views = [R.view(GB_W, r_) for r_ in dil]
    tokbuf = pltpu.VMEM((GB_W // LANES, R.tm, LANES), F32)

    def merge_fwd(o1r, o2r, o3r, l1r, l2r, l3r, ob_ref, *bufs):
        os_ = [_from_view(ref, bufs[n], dil[n]) for n, ref in enumerate((o1r, o2r, o3r))]
        la, lb, lc = [_from_view(ref, bufs[3 + n], dil[n]) for n, ref in enumerate((l1r, l2r, l3r))]
        mx = jnp.maximum(jnp.maximum(la, lb), lc)
        ea, eb, ec = jnp.exp(la - mx), jnp.exp(lb - mx), jnp.exp(lc - mx)
        ob_ref[...] = ((ea * os_[0] + eb * os_[1] + ec * os_[2]) / (ea + eb + ec)).astype(BF16)

    ob = _pcall(merge_fwd, name="merge_fwd", grid=R.grid, in_specs=views + views, out_specs=R.row(GB_W),
                out_shape=sds((N, GB_W), BF16), scratch_shapes=[tokbuf] * 6,
                compiler_params=_params(("parallel",)))(o1, o2, o3, l1, l2, l3)

    f32 = lambda ref: ref[...].astype(F32)
    Rm = _Rows(N, T, _pick(T, 512))

    def mix_out(oa_r, ob_r, ga_r, gb_r, x_r, gm_r, g_r, b_r, sf_r, hf_r, wa_r, wb_r, wo_r,
                ya_ref, yb_ref, mg_ref, y_ref, r1_ref, u2_ref):
        ya = jnp.dot(oa_r[...], wa_r[...], preferred_element_type=F32).astype(BF16)
        yb = jnp.concatenate([jnp.dot(ob_r[...], wb_r[s_], preferred_element_type=F32)
                              for s_ in range(w_b.shape[0])], axis=1).astype(BF16)
        merged = (jax.nn.sigmoid(f32(ga_r)) * ya.astype(F32) + jax.nn.sigmoid(f32(gb_r)) * yb.astype(F32)).astype(BF16)
        y = jnp.dot(merged, wo_r[...], preferred_element_type=F32)
        r1 = ALPHA * x_r[...] + (1.0 + gm_r[0]) * y
        xhat, _ = _ln_stats(r1)
        x1 = xhat * g_r[...] + b_r[...]
        ya_ref[...], yb_ref[...], mg_ref[...], y_ref[...], r1_ref[...] = ya, yb, merged, y, r1
        u2_ref[...] = (x1 * (1.0 + sf_r[0]) + hf_r[0]).astype(BF16)

    ya, yb, merged, y, r1, u2 = _pcall(
        mix_out, name="mix_out", grid=Rm.grid,
        in_specs=[Rm.row(QA_W), Rm.row(GB_W), Rm.row(D, 0), Rm.row(D, 1), Rm.row(D), Rm.ex(D), Rm.const((1, D)),
                  Rm.const((1, D)), Rm.ex(D), Rm.ex(D), Rm.const(w_a.shape), Rm.const(w_b.shape), Rm.const(w_o.shape)],
        out_specs=[Rm.row(D)] * 6,
        out_shape=[sds((N, D), BF16)] * 3 + [sds((N, D), F32)] * 2 + [sds((N, D), BF16)],
        compiler_params=_params(("parallel",)))(oa, ob, gates, gates, x2, gate_m, ln1_g, ln1_b, scale_f, shift_f,
                                                w_a, w_b, w_o)

    tnf = w_gu.shape[2]
    nft = w_gu.shape[0] // 2
    tmf = _pick(N, 512)

    def ffn_up(u_r, wg_r, wu_r, hg_ref, hu_ref, a_ref):
        hg = jnp.dot(u_r[...], wg_r[...], preferred_element_type=F32)
        hu = jnp.dot(u_r[...], wu_r[...], preferred_element_type=F32)
        sl, _ = _silu_parts(hg)
        hg_ref[...] = hg.astype(BF16)
        hu_ref[...] = hu.astype(BF16)
        a_ref[...] = (sl * hu).astype(BF16)

    ftile = pl.BlockSpec((tmf, tnf), lambda j, i: (i, j))
    hg, hu, act = _pcall(
        ffn_up, name="ffn_up", grid=(nft, N // tmf),
        in_specs=[pl.BlockSpec((tmf, D), lambda j, i: (i, 0)), pl.BlockSpec((None, D, tnf), lambda j, i: (j, 0, 0)),
                  pl.BlockSpec((None, D, tnf), lambda j, i: (j + nft, 0, 0))],
        out_specs=[ftile] * 3, out_shape=[sds((N, F), BF16)] * 3,
        compiler_params=_params(("arbitrary", "parallel")))(u2, w_gu, w_gu)
    y2 = _mm(act, w_d, name="ffn_down")

    def norm2_loss_bwd(r1_r, g1_r, b1_r, y2_r, t_r, gf_r, g_r, b_r, dy2_ref, dx1_ref, dgf_ref, dg_ref, db_ref, loss_ref):
        first = R.first_of_example()
        y2v = y2_r[...]
        x1 = _ln_stats(r1_r[...])[0] * g1_r[...] + b1_r[...]
        r2 = ALPHA * x1 + (1.0 + gf_r[0]) * y2v
        xhat, rstd = _ln_stats(r2)
        err = xhat * g_r[...] + b_r[...] - t_r[...]
        dx2 = err * (1.0 / D)
        dr2 = _ln_bwd(dx2, xhat, rstd, g_r[...])
        dy2_ref[...] = ((1.0 + gf_r[0]) * dr2).astype(BF16)
        dx1_ref[...] = ALPHA * dr2
        _acc(dgf_ref, first, _colsum(dr2 * y2v))
        _acc(dg_ref, first, _colsum(dx2 * xhat))
        _acc(db_ref, first, _colsum(dx2))
        part = 0.5 * jnp.sum(jnp.mean(err * err, axis=-1, keepdims=True))
        _acc(loss_ref, first, jnp.broadcast_to(part, (1, 128)))

    dy2, dx1p, dgate_f, dg2, db2, loss_p = _pcall(
        norm2_loss_bwd, name="norm2_loss_bwd", grid=R.grid,
        in_specs=[R.row(D), R.const((1, D)), R.const((1, D)), R.row(D), R.row(D), R.ex(D), R.const((1, D)),
                  R.const((1, D))],
        out_specs=[R.row(D), R.row(D), R.ex(D), R.ex(D), R.ex(D), R.ex(128)],
        out_shape=[sds((N, D), BF16), sds((N, D), F32), exsum(), exsum(), exsum(), exsum(128)],
        compiler_params=_params(("arbitrary",)))(r1, ln1_g, ln1_b, y2, tgt2, gate_f, ln2_g, ln2_b)

    g_wd = _mm(act, dy2, ta=True, out_dtype=BF16, name="ffn_down_dw")

    tmd = _pick(N, 256)

    fchunk = _pick(F, 768)

    def ffn_down_dx(dy_r, wd_r, hg_r, hu_r, dh_ref):
        for t in range(F // fchunk):
            cs = slice(t * fchunk, (t + 1) * fchunk)
            da = lax.dot_general(dy_r[...], wd_r[cs, :], _NT, preferred_element_type=F32)
            sl, dsl = _silu_parts(hg_r[:, cs].astype(F32))
            dh_ref[:, cs] = (da * hu_r[:, cs].astype(F32) * dsl).astype(BF16)
            dh_ref[:, F + t * fchunk:F + (t + 1) * fchunk] = (da * sl).astype(BF16)

    rowd = lambda w_: pl.BlockSpec((tmd, w_), lambda i: (i, 0))
    dh = _pcall(
        ffn_down_dx, name="ffn_down_dx", grid=(N // tmd,),
        in_specs=[rowd(D), pl.BlockSpec((F, D), lambda i: (0, 0)), rowd(F), rowd(F)],
        out_specs=rowd(2 * F), out_shape=sds((N, 2 * F), BF16),
        compiler_params=_params(("parallel",)))(dy2, w_d, hg, hu)
    def ffn_up_dx(dh_r, w_r, o_ref):
        acc = None
        for s_ in range(w_gu.shape[0]):
            part = lax.dot_general(dh_r[:, s_ * tnf:(s_ + 1) * tnf], w_r[s_], _NT, preferred_element_type=F32)
            acc = part if acc is None else acc + part
        o_ref[...] = acc

    du2 = _pcall(
        ffn_up_dx, name="ffn_up_dx", grid=(N // tmf,),
        in_specs=[pl.BlockSpec((tmf, 2 * F), lambda i: (i, 0)), pl.BlockSpec(w_gu.shape, lambda i: (0, 0, 0))],
        out_specs=pl.BlockSpec((tmf, D), lambda i: (i, 0)), out_shape=sds((N, D), F32),
        compiler_params=_params(("parallel",)))(dh, w_gu)
    g_wgu = _mm(u2, dh, ta=True, out3=w_gu.shape[0], out_dtype=BF16, name="ffn_up_dw")

    def norm1_bwd(dx1p_r, du2_r, r1_r, y_r, sf_r, gm_r, g_r, b_r,
                  dxp_ref, dy_ref, dsf_ref, dhf_ref, dgm_ref, dg_ref, db_ref):
        first = R.first_of_example()
        du2v = du2_r[...]
        dx1 = dx1p_r[...] + du2v * (1.0 + sf_r[0])
        xhat, rstd = _ln_stats(r1_r[...])
        dr1 = _ln_bwd(dx1, xhat, rstd, g_r[...])
        dxp_ref[...] = ALPHA * dr1
        dy_ref[...] = ((1.0 + gm_r[0]) * dr1).astype(BF16)
        _acc(dsf_ref, first, _colsum(du2v * (xhat * g_r[...] + b_r[...])))
        _acc(dhf_ref, first, _colsum(du2v))
        _acc(dgm_ref, first, _colsum(dr1 * y_r[...]))
        _acc(dg_ref, first, _colsum(dx1 * xhat))
        _acc(db_ref, first, _colsum(dx1))

    dxp, dy, dscale_f, dshift_f, dgate_m, dg1, db1 = _pcall(
        norm1_bwd, name="norm1_bwd", grid=R.grid,
        in_specs=[R.row(D)] * 4 + [R.ex(D), R.ex(D), R.const((1, D)), R.const((1, D))],
        out_specs=[R.row(D), R.row(D)] + [R.ex(D)] * 5,
        out_shape=[sds((N, D), F32), sds((N, D), BF16)] + [exsum()] * 5,
        compiler_params=_params(("arbitrary",)))(dx1p, du2, r1, y, scale_f, gate_m, ln1_g, ln1_b)

    g_wo = _mm(merged, dy, ta=True, out_dtype=BF16, name="out_proj_dw")

    def mix_out_bwd(dy_r, ya_r, yb_r, ga_r, gb_r, wo_r, wa_r, wb_r, dya_ref, dyb_ref, dg_ref, doa_ref, dob_ref):
        dm = lax.dot_general(dy_r[...], wo_r[...], _NT, preferred_element_type=F32).astype(BF16).astype(F32)
        sa, sb = jax.nn.sigmoid(f32(ga_r)), jax.nn.sigmoid(f32(gb_r))
        dya, dyb = (dm * sa).astype(BF16), (dm * sb).astype(BF16)
        dya_ref[...], dyb_ref[...] = dya, dyb
        dg_ref[:, :D] = (dm * f32(ya_r) * sa * (1.0 - sa)).astype(BF16)
        dg_ref[:, D:] = (dm * f32(yb_r) * sb * (1.0 - sb)).astype(BF16)
        doa_ref[...] = lax.dot_general(dya, wa_r[...], _NT, preferred_element_type=F32).astype(BF16)
        ds_ = D // w_b.shape[0]
        dob = None
        for s_ in range(w_b.shape[0]):
            part = lax.dot_general(dyb[:, s_ * ds_:(s_ + 1) * ds_], wb_r[s_], _NT, preferred_element_type=F32)
            dob = part if dob is None else dob + part
        dob_ref[...] = dob

    dya, dyb, dgates, doa, dob = _pcall(
        mix_out_bwd, name="mix_out_bwd", grid=Rm.grid,
        in_specs=[Rm.row(D)] * 3 + [Rm.row(D, 0), Rm.row(D, 1), Rm.const(w_o.shape), Rm.const(w_a.shape),
                                    Rm.const(w_b.shape)],
        out_specs=[Rm.row(D), Rm.row(D), Rm.row(2 * D), Rm.row(QA_W), Rm.row(GB_W)],
        out_shape=[sds((N, D), BF16), sds((N, D), BF16), sds((N, 2 * D), BF16), sds((N, QA_W), BF16), sds((N, GB_W), F32)],
        compiler_params=_params(("parallel",)))(dy, ya, yb, gates, gates, w_o, w_a, w_b)

    g_wa = _mm(oa, dya, ta=True, out_dtype=BF16, name="branch_a_dw")
    g_wb = _mm(ob, dyb, ta=True, out3=w_b.shape[0], out_dtype=BF16, name="branch_b_dw")
    hook("rest_grads", g_wa=g_wa, g_wb=g_wb, g_wo=g_wo, g_wgu=g_wgu, g_wd=g_wd)

    seg = (jnp.arange(GB_W)[:, None] // HEAD_DIM == jnp.arange(GB_W)[None, :] // HEAD_DIM).astype(BF16)

    def merge_bwd(dob_r, o1r, o2r, o3r, l1r, l2r, l3r, seg_r, d1, d2, d3, e1, e2, e3, *bufs):
        dob_v = dob_r[...]
        os_ = [_from_view(ref, bufs[n], dil[n]) for n, ref in enumerate((o1r, o2r, o3r))]
        la, lb, lc = [_from_view(ref, bufs[3 + n], dil[n]) for n, ref in enumerate((l1r, l2r, l3r))]
        mx = jnp.maximum(jnp.maximum(la, lb), lc)
        ea, eb, ec = jnp.exp(la - mx), jnp.exp(lb - mx), jnp.exp(lc - mx)
        inv = 1.0 / (ea + eb + ec)
        ws = [ea * inv, eb * inv, ec * inv]

        def headsum(v):
            hi = v.astype(BF16)
            r1_ = v - hi.astype(F32)
            mid = r1_.astype(BF16)
            lo = (r1_ - mid.astype(F32)).astype(BF16)
            sm = seg_r[...]
            return (jnp.dot(hi, sm, preferred_element_type=F32) + jnp.dot(mid, sm, preferred_element_type=F32)
                    + jnp.dot(lo, sm, preferred_element_type=F32))

        dws = [headsum(dob_v * o) for o in os_]
        mean = ws[0] * dws[0] + ws[1] * dws[1] + ws[2] * dws[2]
        for n, (w_, dw_, d_ref, e_ref) in enumerate(zip(ws, dws, (d1, d2, d3), (e1, e2, e3))):
            _to_view(w_ * dob_v, d_ref, bufs[6], dil[n])
            _to_view(w_ * (dw_ - mean), e_ref, bufs[7], dil[n])

    vshape = lambda r_, dt: sds((NB, T // r_, r_ * GB_W), dt)
    mb = _pcall(
        merge_bwd, name="merge_bwd", grid=R.grid, in_specs=[R.row(GB_W)] + views + views + [R.const((GB_W, GB_W))],
        out_specs=views + views, out_shape=[vshape(r_, BF16) for r_ in dil] + [vshape(r_, F32) for r_ in dil],
        scratch_shapes=[tokbuf] * 8, compiler_params=_params(("parallel",)))(dob, o1, o2, o3, l1, l2, l3, seg)
    do_b, dlse_b = mb[:3], mb[3:]
    hook("merge_bwd_done")

    dqkv_a, dsink = _attn_bwd(qkv[0], doa, la, None, cosf, sins, sinks, None, NB=NB, T=T, name="attn_a_bwd")
    hook("attn_a_bwd_done")
    dqkv = [dqkv_a]
    for g in range(ngrp):
        dqkv.append(_attn_bwd(qkv[1 + g], do_b[g], (l1, l2, l3)[g], dlse_b[g], cosf, sins, sinks, g, NB=NB, T=T,
                              name=f"attn_b{g}_bwd")[0])
        hook(f"attn_b{g}_bwd_done")

    g_win = [_mm(d3.reshape(N, VAR_W), u, ta=True, out_dtype=BF16, name=f"inproj_dw{v}") if VAR_DIL[v] == 1
             else _dw_view(d3, u, VAR_DIL[v], name=f"inproj_dw{v}") for v, d3 in enumerate(dqkv)]
    g_win.append(_mm(dgates, u, ta=True, out_dtype=BF16, name=f"inproj_dw{N_VAR}"))
    hook("win_grads", g_win=g_win)
    wvar = lambda v: (w_in, (VAR_W, D), (v, 0))
    dview = lambda v: (dqkv[v], VAR_DIL[v])
    du = _mm_multi([dview(0)], [wvar(0)], M=N, T=T, name="inproj_dx0")
    hook("inproj_dx0_done")
    du = _mm_multi([dview(v) for v in range(1, N_VAR)] + [dgates],
                   [wvar(v) for v in range(1, N_VAR)] + [(w_in, (2 * D, D), (QKV_P // (2 * D), 0))],
                   M=N, T=T, add=du, tm=256, name="inproj_dx1")
    hook("inproj_dx1_done")

    def x_bwd(dxp_r, du_r, x_r, sm_r, gx_ref, dsm_ref, dhm_ref):
        first = R.first_of_example()
        duv = du_r[...]
        gx_ref[...] = dxp_r[...] + duv * (1.0 + sm_r[0])
        _acc(dsm_ref, first, _colsum(duv * x_r[...]))
        _acc(dhm_ref, first, _colsum(duv))

    gx, dscale_m, dshift_m = _pcall(
        x_bwd, name="x_bwd", grid=R.grid, in_specs=[R.row(D)] * 3 + [R.ex(D)],
        out_specs=[R.row(D), R.ex(D), R.ex(D)], out_shape=[sds((N, D), F32), exsum(), exsum()],
        compiler_params=_params(("arbitrary",)))(dxp, du, x2, scale_m)
    hook("x_bwd_done")

    dmod =jnp.concatenate([dshift_m, dscale_m, dgate_m, dshift_f, dscale_f, dgate_f], axis=-1)[:, 0]
    ln_grads = jnp.concatenate([dg1, db1, dg2, db2], axis=1)
    return dict(loss=loss_p[:, 0, 0], grad_x=gx.reshape(NB, T, D), g_win=g_win, g_wa=g_wa, g_wb=g_wb, g_wo=g_wo,
                g_wgu=g_wgu, g_wd=g_wd, dmod=dmod, ln_grads=ln_grads, dsink=dsink[0, :A_Q_HEADS])


def _coords():
    return lax.axis_index("x"), lax.axis_index("y"), lax.axis_index("c")


def _allgather_small(blk, *, name):
    m_per, n = blk.shape

    def body(x_ref, out_ref, send_sems, recv_sems, local_sem):
        x, y, c = _coords()
        me, sibling = (x, y, c), (x, y, 1 - c)
        chips = [(1 - x, y), (x, 1 - y), (1 - x, 1 - y)]

        def rows(px, py, pc):
            return out_ref.at[pl.ds((4 * px + 2 * py + pc) * m_per, m_per), :]

        def copy(k, block, to, src=None):
            return pltpu.make_async_remote_copy(
                src_ref=rows(*block) if src is None else src, dst_ref=rows(*block),
                send_sem=send_sems.at[k], recv_sem=recv_sems.at[k], device_id=to, device_id_type=MESH)

        mine = pltpu.make_async_copy(x_ref, rows(*me), local_sem)
        mine.start()
        first = [copy(0, me, sibling, src=x_ref)]
        first += [copy(1 + j, me, (*chip, c), src=x_ref) for j, chip in enumerate(chips)]
        for cp in first:
            cp.start()
        passed = [copy(4 + j, (*chip, c), sibling) for j, chip in enumerate(chips)]
        for j, chip in enumerate(chips):
            copy(1 + j, (*chip, c), me).wait_recv()
            passed[j].start()
        copy(0, sibling, me).wait_recv()
        for j, chip in enumerate(chips):
            copy(4 + j, (*chip, 1 - c), me).wait_recv()
        for cp in first + passed:
            cp.wait_send()
        mine.wait()

    return _pcall(
        body, name=name, out_shape=jax.ShapeDtypeStruct((8 * m_per, n), blk.dtype),
        in_specs=[pl.BlockSpec(memory_space=pltpu.VMEM)], out_specs=pl.BlockSpec(memory_space=pltpu.VMEM),
        scratch_shapes=[pltpu.SemaphoreType.DMA((7,)), pltpu.SemaphoreType.DMA((7,)), pltpu.SemaphoreType.DMA],
        compiler_params=pltpu.CompilerParams(vmem_limit_bytes=VMEM_LIMIT_BYTES),
    )(blk)


def _exchange(srcs, dsts, plan, *, name, dst_inits=None):
    na = len(dsts)
    nrem = len(plan(0, 0, 0))

    def body(*refs):
        refs = list(refs)
        src_refs = [refs.pop(0) for _ in range(na)] if srcs is not None else None
        if dst_inits is not None:
            del refs[:na]
        dst_refs, (send_sems, recv_sems) = refs[:na], refs[na:]
        start, wait = _copies(dst_refs if src_refs is None else src_refs, dst_refs, send_sems, recv_sems, plan)
        start()
        wait()

    hbm = pl.BlockSpec(memory_space=pl.ANY)
    ins = (list(srcs) if srcs is not None else []) + (list(dst_inits) if dst_inits is not None else [])
    base = na if srcs is not None else 0
    aliases = {base + a: a for a in range(na)} if dst_inits is not None else {}
    return _pcall(
        body, name=name, out_shape=list(dsts), in_specs=[hbm] * len(ins), out_specs=[hbm] * na,
        input_output_aliases=aliases,
        scratch_shapes=[pltpu.SemaphoreType.DMA((na * nrem,)), pltpu.SemaphoreType.DMA((na * nrem,))],
    )(*ins)


def _other_chips(x, y):
    return [(1 - x, y), (x, 1 - y), (1 - x, 1 - y)]


def _round(ride, carrier, name):
    if carrier is not None:
        _RIDES.setdefault(carrier, []).append(ride)
        return
    srcs = ride.srcs() if callable(ride.srcs) else ride.srcs
    inits = ride.dst_inits() if callable(ride.dst_inits) else ride.dst_inits
    ride.out = list(_exchange(srcs, ride.dsts, ride.plan, name=name, dst_inits=inits))


class _Gather:
    def __init__(self, shards, chip, tag, carriers=(None, None)):
        def plan_ici(x, y, c):
            k = 2 * x + y
            return [((c,), (k, c), (2 * px + py, c), (px, py, c)) for px, py in _other_chips(x, y)]

        def plan_d2d(x, y, c):
            return [((2 * px + py, c), (2 * px + py, c), (2 * px + py, 1 - c), (x, y, 1 - c))
                    for px, py in _other_chips(x, y)]

        self.shards, self.chip = shards, chip
        dsts = [jax.ShapeDtypeStruct((4,) + s.shape, s.dtype) for s in shards]
        ici = _Ride(shards, dsts, plan_ici)
        self.d2d = _Ride(None, dsts, plan_d2d, dst_inits=lambda: ici.out)
        _round(ici, carriers[0], f"gather_{tag}_ici")
        _round(self.d2d, carriers[1], f"gather_{tag}_d2d")

    def result(self):
        full = [lax.dynamic_update_index_in_dim(f, s, self.chip, 0) for f, s in zip(self.d2d.out, self.shards)]
        return [f.reshape((4, 2 * f.shape[2], f.shape[3])) for f in full]


def _index_operand(i):
    return jnp.reshape(i, (1,)).astype(jnp.int32)


def _add_pairs(g, f, ci, *, name):
    s, _, hr, wd = g.shape
    tr = _pick(hr, 600, 16)

    def body(c_ref, a_ref, b_ref, o_ref):
        o_ref[...] = (a_ref[...].astype(F32) + b_ref[...].astype(F32)).astype(BF16)

    spec = pl.BlockSpec((1, tr, wd), lambda j, i, c: (j, i, 0))
    grid_spec = pltpu.PrefetchScalarGridSpec(
        num_scalar_prefetch=1, grid=(s, hr // tr),
        in_specs=[pl.BlockSpec((1, None, tr, wd), lambda j, i, c: (j, c[0], i, 0)), spec], out_specs=spec)
    return _pcall(body, name=name, grid_spec=grid_spec, out_shape=jax.ShapeDtypeStruct(f.shape, BF16),
                  compiler_params=_params(("parallel", "parallel")))(_index_operand(ci), g, f)


def _sum_chips(landed, pairs, chip, *, name):
    s, hr, wd = landed.shape
    tr = _pick(hr, 600, 16)

    def body(k_ref, l_ref, p_ref, o_ref):
        acc = None
        for k in range(s):
            part = jnp.where(k_ref[0] == k, p_ref[k], l_ref[k]).astype(F32)
            acc = part if acc is None else acc + part
        o_ref[...] = acc

    spec = pl.BlockSpec((s, tr, wd), lambda i, k: (0, i, 0))
    grid_spec = pltpu.PrefetchScalarGridSpec(
        num_scalar_prefetch=1, grid=(hr // tr,), in_specs=[spec, spec],
        out_specs=pl.BlockSpec((tr, wd), lambda i, k: (i, 0)))
    return _pcall(body, name=name, grid_spec=grid_spec, out_shape=jax.ShapeDtypeStruct((hr, wd), F32),
                  compiler_params=_params(("parallel",)))(_index_operand(chip), landed, pairs)


class _ReduceScatter:
    def __init__(self, gs, chip, ci, tag):
        self.gs, self.chip, self.ci, self.tag = gs, chip, ci, tag
        self.half_t = [jax.ShapeDtypeStruct((g.shape[0],) + g.shape[2:], BF16) for g in gs]

    def pair(self, carrier=None):
        plan = lambda x, y, c: [((slice(None), 1 - c), (), (), (x, y, 1 - c))]
        self.r1 = _Ride(self.gs, self.half_t, plan)
        _round(self.r1, carrier, f"reduce_{self.tag}_pair")

    def chips(self, carrier=None):
        def plan(x, y, c):
            k = 2 * x + y
            return [((2 * px + py,), (k,), (2 * px + py,), (px, py, c)) for px, py in _other_chips(x, y)]

        self.pairs = [_add_pairs(g, f, self.ci, name=f"reduce_{self.tag}_pair_add{n}")
                      for n, (g, f) in enumerate(zip(self.gs, self.r1.out))]
        self.r2 = _Ride(self.pairs, self.half_t, plan)
        _round(self.r2, carrier, f"reduce_{self.tag}_chips")

    def halves(self, carrier=None):
        plan = lambda x, y, c: [((), (c,), (1 - c,), (x, y, 1 - c))]
        self.mine = [_sum_chips(l, p, self.chip, name=f"reduce_{self.tag}_chip_sum{n}")
                     for n, (l, p) in enumerate(zip(self.r2.out, self.pairs))]
        self.r3 = _Ride(self.mine, [jax.ShapeDtypeStruct((2,) + m.shape, F32) for m in self.mine], plan)
        _round(self.r3, carrier, f"reduce_{self.tag}_halves")

    def result(self):
        return [lax.dynamic_update_index_in_dim(b, m, self.ci, 0).reshape(2 * m.shape[0], m.shape[1])
                for b, m in zip(self.r3.out, self.mine)]


def _ada_fwd(c_all, w_sh, b_sh, *, name):
    nb, d = c_all.shape
    wcols = w_sh.shape[1]
    tn = _pick(wcols, 512)

    def body(c_ref, w_ref, b_ref, o_ref, a_ref):
        cv = c_ref[...]
        act = cv * jax.nn.sigmoid(cv)
        a_ref[...] = act
        o_ref[...] = jnp.dot(act.astype(BF16), w_ref[...].astype(BF16), preferred_element_type=F32) + b_ref[...]

    return _pcall(
        body, name=name, grid=(wcols // tn,),
        in_specs=[pl.BlockSpec((nb, d), lambda j: (0, 0)), pl.BlockSpec((d, tn), lambda j: (0, j)),
                  pl.BlockSpec((1, tn), lambda j: (0, j))],
        out_specs=[pl.BlockSpec((nb, tn), lambda j: (0, j)), pl.BlockSpec((nb, d), lambda j: (0, 0))],
        out_shape=[jax.ShapeDtypeStruct((nb, wcols), F32), jax.ShapeDtypeStruct((nb, d), F32)],
        compiler_params=_params(("arbitrary",)))(c_all, w_sh, b_sh)


def _sum_devices(g, *, name):
    nd, m, w = g.shape

    def body(g_ref, o_ref):
        acc = g_ref[0]
        for k in range(1, nd):
            acc = acc + g_ref[k]
        o_ref[...] = acc

    return _pcall(body, name=name, out_shape=jax.ShapeDtypeStruct((m, w), F32),
                  compiler_params=pltpu.CompilerParams(vmem_limit_bytes=VMEM_LIMIT_BYTES))(g)


def _adamw(w, g, m, v, *, name):
    rows, cols = w.shape[-2:]
    tr = _pick(rows, max(8, (1 << 18) // cols), 8)
    c1 = 1.0 / (1.0 - ADAM_B1 ** ADAM_STEP)
    c2 = 1.0 / (1.0 - ADAM_B2 ** ADAM_STEP)

    def body(w_ref, g_ref, m_ref, v_ref, d_ref, nm_ref, nv_ref):
        gv = g_ref[...]
        nm = ADAM_B1 * m_ref[...] + (1.0 - ADAM_B1) * gv
        nv = ADAM_B2 * v_ref[...] + (1.0 - ADAM_B2) * (gv * gv)
        d_ref[...] = -ADAM_LR * ((nm * c1) / (jnp.sqrt(nv * c2) + ADAM_EPS) + ADAM_WD * w_ref[...])
        nm_ref[...] = nm
        nv_ref[...] = nv

    gspec = pl.BlockSpec((tr, cols), lambda i: (i, 0))
    spec = pl.BlockSpec((None, tr, cols), lambda i: (0, i, 0)) if w.ndim == 3 else gspec
    shp = jax.ShapeDtypeStruct(w.shape, F32)
    return _pcall(body, name=name, grid=(rows // tr,), in_specs=[spec, gspec, spec, spec], out_specs=[spec] * 3,
                  out_shape=[shp] * 3, compiler_params=_params(("parallel",)))(w, g, m, v)


def _permute_in_rows(wt):
    ngrp = len(B_PATTERNS)
    qb, kb, vb = (wt[A_W + n * QB_W:A_W + (n + 1) * QB_W] for n in range(3))
    parts = [wt[:A_W], jnp.zeros((VAR_W - A_W, wt.shape[1]), wt.dtype)]
    for g in range(ngrp):
        parts += [t[g * GB_W:(g + 1) * GB_W] for t in (qb, kb, vb)]
    return jnp.concatenate(parts + [wt[A_W + 3 * QB_W:]], axis=0)


def _unpermute_in_grads(pieces):
    ga, groups, gg = pieces[0], pieces[1:-1], pieces[-1]
    rows = [ga[:A_W]]
    for n in range(3):
        rows += [gp[n * GB_W:(n + 1) * GB_W] for gp in groups]
    return jnp.concatenate(rows + [gg], axis=0)


def kernel(x, c, positions, w_ada, b_ada, w_in, sinks, w_branch_a, w_branch_b, w_o, ln1_g, ln1_b, w_gate_up, w_down, ln2_g, ln2_b, loss_target, m_w_ada, m_b_ada, m_w_in, m_sinks, m_w_branch_a, m_w_branch_b, m_w_o, m_ln1_g, m_ln1_b, m_w_gate_up, m_w_down, m_ln2_g, m_ln2_b, v_w_ada, v_b_ada, v_w_in, v_sinks, v_w_branch_a, v_w_branch_b, v_w_o, v_ln1_g, v_ln1_b, v_w_gate_up, v_w_down, v_ln2_g, v_ln2_b):
    xi, yi, ci = _coords()
    chip = 2 * xi + yi
    dev = 4 * xi + 2 * yi + ci
    NB, T, D = x.shape
    nchip, ndev = 4, 8
    ada_cols = w_ada.shape[2]

    ra, ro, rd = w_branch_a.shape[1], w_o.shape[1], w_down.shape[1]
    rowsh = jnp.concatenate([w_branch_a[0], w_o[0], w_down[0]], axis=0)
    halves = lambda a: a.reshape(a.shape[:-2] + (2, a.shape[-2] // 2, a.shape[-1]))
    tr = lambda a: jnp.swapaxes(a, -1, -2)
    shards = [halves(w.astype(BF16)) for w in (tr(w_in[0]), rowsh, w_branch_b[0], w_gate_up[0])]
    gin = _Gather(shards[:1], chip, "w_in", carriers=("gather_c", "gather_mod"))

    c_blk = jnp.zeros((8, D), F32).at[:NB].set(c)
    c_all = _allgather_small(c_blk, name="gather_c").reshape(ndev, 8, D)[:, :NB].reshape(ndev * NB, D)
    b_sh = lax.dynamic_slice(b_ada, (0, chip * ada_cols), (1, ada_cols))
    mod_part, c_act = _ada_fwd(c_all, w_ada[0], b_sh, name="ada_fwd")
    mod_g = _allgather_small(mod_part, name="gather_mod").reshape(nchip, 2, ndev * NB, ada_cols)[:, 0]
    mod_all = jnp.transpose(mod_g, (1, 0, 2)).reshape(ndev * NB, nchip * ada_cols)
    mod = lax.dynamic_slice(mod_all, (NB * dev, 0), (NB, nchip * ada_cols))

    (g_in,) = gin.result()
    w_in_f = _permute_in_rows(g_in.reshape(nchip * g_in.shape[1], D))
    mix = _Gather(shards[1:3], chip, "w_mix", carriers=("inproj_qkv", "attn_a_fwd"))
    ffn = _Gather(shards[3:], chip, "w_ffn", carriers=("attn_a_fwd", "attn_b0_fwd"))

    def rest_weights():
        (g_rows, w_b_f), (w_gu_f,) = mix.result(), ffn.result()
        return (g_rows[:, :ra].reshape(nchip * ra, D), w_b_f, g_rows[:, ra:ra + ro].reshape(nchip * ro, D), w_gu_f,
                g_rows[:, ra + ro:].reshape(nchip * rd, D))

    red = {}

    def hook(event, **g):
        if event == "rest_grads":
            gr_rows = jnp.concatenate([g["g_wa"].reshape(nchip, ra, D), g["g_wo"].reshape(nchip, ro, D),
                                       g["g_wd"].reshape(nchip, rd, D)], axis=1)
            red["ffn"] = _ReduceScatter([halves(g["g_wgu"])], chip, ci, "ffn")
            red["mix"] = _ReduceScatter([halves(gr_rows), halves(g["g_wb"])], chip, ci, "mix")
            red["ffn"].pair(carrier="merge_bwd")
            red["mix"].pair(carrier="merge_bwd")
        elif event == "merge_bwd_done":
            red["ffn"].chips(carrier="attn_a_bwd")
            red["mix"].chips(carrier="attn_b0_bwd")
        elif event == "attn_a_bwd_done":
            red["ffn"].halves(carrier="attn_b0_bwd")
        elif event == "attn_b0_bwd_done":
            red["mix"].halves(carrier="attn_b1_bwd")
        elif event == "win_grads":
            gr_in = _unpermute_in_grads(g["g_win"])
            red["w_in"] = _ReduceScatter([halves(gr_in.reshape(nchip, gr_in.shape[0] // nchip, D))], chip, ci, "w_in")
            red["w_in"].pair(carrier="inproj_dx0")
        elif event == "inproj_dx0_done":
            red["w_in"].chips(carrier="inproj_dx1")
        elif event == "inproj_dx1_done":
            red["w_in"].halves(carrier="gather_small")

    res = _local_step(x, mod, positions, w_in_f, rest_weights, sinks[0], ln1_g, ln1_b, ln2_g, ln2_b, loss_target, hook)
    (g_rows_red, g_w_b), (g_w_gu,) = red["mix"].result(), red["ffn"].result()
    g_w_a, g_w_o, g_w_d = g_rows_red[:ra], g_rows_red[ra:ra + ro], g_rows_red[ra + ro:]

    small_rows = 24
    misc = jnp.zeros((1, D), F32).at[0, :A_Q_HEADS].set(res["dsink"]).at[0, A_Q_HEADS].set(jnp.sum(res["loss"]))
    small = jnp.concatenate([res["dmod"].reshape(NB * 6, D), jnp.sum(res["ln_grads"], axis=0), misc,
                             jnp.zeros((small_rows - NB * 6 - 5, D), F32)], axis=0)
    small_all = _allgather_small(small, name="gather_small").reshape(ndev, small_rows, D)
    (g_w_in,) = red["w_in"].result()
    dmod_all = small_all[:, :NB * 6].reshape(ndev * NB, 6 * D)
    sums = _sum_devices(small_all, name="sum_small")
    g_b_ada = (sums[0:6] + sums[6:12]).reshape(1, 6 * D)
    g_ln1_g, g_ln1_b, g_ln2_g, g_ln2_b = (sums[12 + n][None] for n in range(4))
    g_sinks = sums[16, :A_Q_HEADS][None]
    loss = sums[16, A_Q_HEADS]
    dmod_sh = lax.dynamic_slice(dmod_all, (0, chip * ada_cols), (ndev * NB, ada_cols))
    g_w_ada = _mm(c_act, dmod_sh, ta=True, name="ada_dw")

    names = ["w_ada", "b_ada", "w_in", "sinks", "w_branch_a", "w_branch_b", "w_o", "ln1_g", "ln1_b",
             "w_gate_up", "w_down", "ln2_g", "ln2_b"]
    ws = [w_ada, b_ada, w_in, sinks, w_branch_a, w_branch_b, w_o, ln1_g, ln1_b, w_gate_up, w_down, ln2_g, ln2_b]
    ms = [m_w_ada, m_b_ada, m_w_in, m_sinks, m_w_branch_a, m_w_branch_b, m_w_o, m_ln1_g, m_ln1_b, m_w_gate_up,
          m_w_down, m_ln2_g, m_ln2_b]
    vs = [v_w_ada, v_b_ada, v_w_in, v_sinks, v_w_branch_a, v_w_branch_b, v_w_o, v_ln1_g, v_ln1_b, v_w_gate_up,
          v_w_down, v_ln2_g, v_ln2_b]
    gs = [g_w_ada, g_b_ada, g_w_in, g_sinks, g_w_a, g_w_b, g_w_o, g_ln1_g, g_ln1_b, g_w_gu, g_w_d, g_ln2_g, g_ln2_b]
    grads, deltas, new_ms, new_vs = [], [], [], []
    for name, w, g, m, v in zip(names, ws, gs, ms, vs):
        flip = tr if name == "w_in" else (lambda a: a)
        w, m, v = flip(w), flip(m), flip(v)
        g2 = g.reshape(w.shape[-2:])
        d, nm, nv = _adamw(w, g2, m, v, name="adamw_" + name)
        grads.append(flip(g2.reshape(w.shape)))
        deltas.append(flip(d))
        new_ms.append(flip(nm))
        new_vs.append(flip(nv))
    return (loss, res["grad_x"], *grads, *deltas, *new_ms, *new_vs)
```

```python
import functools

import jax
import jax.numpy as jnp
from jax import lax
from jax.experimental import pallas as pl
from jax.experimental.pallas import tpu as pltpu

F32 = jnp.float32
BF16 = jnp.bfloat16
MESH = pl.DeviceIdType.MESH

HEAD_DIM = 64
LANES = 128
PAIR_W = 2 * HEAD_DIM
BLOCK = 128
A_Q_HEADS = 16
A_KV_HEADS = 2
A_WINDOW = 128
B_PATTERNS = ((128, 1), (512, 4), (2048, 16))
B_GROUP_HEADS = 8
QA_W = A_Q_HEADS * HEAD_DIM
KA_W = A_KV_HEADS * HEAD_DIM
GB_W = B_GROUP_HEADS * HEAD_DIM
QB_W = GB_W * len(B_PATTERNS)
A_W = QA_W + 2 * KA_W
VAR_W = 3 * GB_W
N_VAR = 1 + len(B_PATTERNS)
VAR_DIL = (1,) + tuple(r for _, r in B_PATTERNS)
QKV_P = N_VAR * VAR_W
ROPE_THETA = 10000.0
LN_EPS = 1e-5
NEG_INF = -1e30
DEPTH = 1
ALPHA = (2 * DEPTH) ** 0.25
SCALE = HEAD_DIM ** -0.5

ADAM_LR, ADAM_B1, ADAM_B2, ADAM_EPS, ADAM_WD, ADAM_STEP = 0.001, 0.9, 0.999, 1e-08, 0.01, 10

VMEM_LIMIT_BYTES = 56 * 1024 * 1024
MM_TILE_BYTES = 36 * 1024 * 1024
MM_WHOLE_K = 4096


def _params(sem=None):
    return pltpu.CompilerParams(dimension_semantics=sem, vmem_limit_bytes=VMEM_LIMIT_BYTES)


_RIDES = {}


def _pcall(body, *, name, **kw):
    rides = _RIDES.pop(name, None)
    if rides is None:
        return pl.pallas_call(body, name=name, **kw)
    return _riding_call(body, rides, name=name, **kw)


def _copies(src_refs, dst_refs, send_sems, recv_sems, plan):
    x, y, c = lax.axis_index("x"), lax.axis_index("y"), lax.axis_index("c")
    remote = plan(x, y, c)
    nrem = len(remote)
    at = lambda ref, idx: ref.at[idx] if idx else ref

    def copy(a, n, landing):
        si, di, ri, peer = remote[n]
        return pltpu.make_async_remote_copy(
            src_ref=at(src_refs[a], si), dst_ref=at(dst_refs[a], ri if landing else di),
            send_sem=send_sems.at[a * nrem + n], recv_sem=recv_sems.at[a * nrem + n],
            device_id=peer, device_id_type=MESH)

    order = [(a, n) for a in range(len(dst_refs)) for n in range(nrem)]

    def start():
        for a, n in order:
            copy(a, n, False).start()

    def wait():
        for a, n in order:
            copy(a, n, True).wait_recv()
        for a, n in order:
            copy(a, n, False).wait_send()

    return start, wait


class _Ride:
    def __init__(self, srcs, dsts, plan, dst_inits=None):
        self.srcs, self.dsts, self.plan, self.dst_inits, self.out = srcs, dsts, plan, dst_inits, None


def _riding_call(body, rides, *, name, in_specs, out_specs, out_shape, grid=(), scratch_shapes=(), **kw):
    single = not isinstance(out_specs, (list, tuple))
    out_specs = [out_specs] if single else list(out_specs)
    out_shape = [out_shape] if single else list(out_shape)
    n_in, n_out, n_scr = len(in_specs), len(out_specs), len(scratch_shapes)
    xin, xdsts, sems, aliases, layout = [], [], [], {}, []
    for ride in rides:
        srcs = ride.srcs() if callable(ride.srcs) else ride.srcs
        inits = ride.dst_inits() if callable(ride.dst_inits) else ride.dst_inits
        na, nrem = len(ride.dsts), len(ride.plan(0, 0, 0))
        src_at = len(xin) if srcs is not None else None
        xin += list(srcs) if srcs is not None else []
        if inits is not None:
            aliases.update({n_in + len(xin) + a: n_out + len(xdsts) + a for a in range(na)})
            xin += list(inits)
        layout.append((src_at, len(xdsts), na))
        xdsts += list(ride.dsts)
        sems += [pltpu.SemaphoreType.DMA((na * nrem,)), pltpu.SemaphoreType.DMA((na * nrem,))]

    def wrapped(*refs):
        ins, xins = refs[:n_in], refs[n_in:n_in + len(xin)]
        outs = refs[n_in + len(xin):n_in + len(xin) + n_out]
        xouts = refs[n_in + len(xin) + n_out:n_in + len(xin) + n_out + len(xdsts)]
        scr = refs[n_in + len(xin) + n_out + len(xdsts):]
        rounds = []
        for k, (ride, (src_at, dst_at, na)) in enumerate(zip(rides, layout)):
            dsts = xouts[dst_at:dst_at + na]
            srcs = dsts if src_at is None else xins[src_at:src_at + na]
            rounds.append(_copies(srcs, dsts, scr[n_scr + 2 * k], scr[n_scr + 2 * k + 1], ride.plan))
        ids = [pl.program_id(a) for a in range(len(grid))]
        first = functools.reduce(jnp.logical_and, [i == 0 for i in ids], True)
        last = functools.reduce(jnp.logical_and, [i == g - 1 for i, g in zip(ids, grid)], True)

        def start_all():
            for start, _ in rounds:
                start()

        def wait_all():
            for _, wait in rounds:
                wait()

        start_all() if not grid else pl.when(first)(start_all)
        body(*ins, *outs, *scr[:n_scr])
        wait_all() if not grid else pl.when(last)(wait_all)

    hbm = pl.BlockSpec(memory_space=pl.ANY)
    gridkw = dict(grid=grid) if grid else {}

    def run(*args):
        res = pl.pallas_call(
            wrapped, name=name, in_specs=list(in_specs) + [hbm] * len(xin),
            out_specs=out_specs + [hbm] * len(xdsts), out_shape=out_shape + xdsts,
            scratch_shapes=list(scratch_shapes) + sems, input_output_aliases=aliases,
            compiler_params=_params(("arbitrary",) * len(grid) if grid else None), **gridkw,
        )(*args, *xin)
        for ride, (_, dst_at, na) in zip(rides, layout):
            ride.out = list(res[n_out + dst_at:n_out + dst_at + na])
        return res[0] if single else list(res[:n_out])

    return run


def _pick(n, target, quantum=128):
    t = (min(target, n) // quantum) * quantum
    while t >= quantum:
        if n % t == 0:
            return t
        t -= quantum
    return n


def _mm(a, b, *, name, ta=False, tb=False, b3=False, out3=0, out_dtype=F32, add=None, tm=1024, tn=1536, tk=1536):
    if ta:
        K, M = a.shape
    else:
        M, K = a.shape
    if b3 and tb:
        Nn, K2, tk = b.shape[1], b.shape[0] * b.shape[2], b.shape[2]
    elif b3:
        K2, Nn, tn = b.shape[1], b.shape[0] * b.shape[2], b.shape[2]
    elif tb:
        Nn, K2 = b.shape
    else:
        K2, Nn = b.shape
    assert K == K2, (a.shape, b.shape)
    if out3:
        tn = Nn // out3
    tm, tn, tk = _pick(M, tm), _pick(Nn, tn), _pick(K, tk)
    if not (b3 and tb) and K <= MM_WHOLE_K:
        tk = K
        fits = lambda: 4 * tk * (tm + tn) + 8 * tm * tn * (2 if add is not None else 1) <= MM_TILE_BYTES
        while not fits():
            if (tm >= tn or b3 or out3) and tm > 256:
                tm = _pick(M, tm - 128)
            elif not (b3 or out3) and tn > 256:
                tn = _pick(Nn, tn - 128)
            else:
                break
    nk = K // tk
    j_outer = K * Nn + (Nn // tn) * M * K < M * K + (M // tm) * K * Nn
    dn = (((0 if ta else 1,), (1 if tb else 0,)), ((), ()))

    def body(*refs):
        refs = list(refs)
        a_ref, b_ref = refs[:2]
        add_ref = refs[2] if add is not None else None
        o_ref = refs[3] if add is not None else refs[2]
        part = lax.dot_general(a_ref[...].astype(BF16), b_ref[...].astype(BF16), dn, preferred_element_type=F32)

        def finish(r):
            if add is not None:
                r = r + add_ref[...]
            o_ref[...] = r.astype(out_dtype)

        if nk == 1:
            finish(part)
            return
        acc = refs[-1]
        k = pl.program_id(2)

        @pl.when(k == 0)
        def _():
            acc[...] = part

        @pl.when(k > 0)
        def _():
            acc[...] += part

        @pl.when(k == nk - 1)
        def _():
            finish(acc[...])

    def spec(shape, index):
        return pl.BlockSpec(shape, (lambda j, i, k: index(i, j, k)) if j_outer else index)

    a_spec = spec((tk, tm), lambda i, j, k: (k, i)) if ta else spec((tm, tk), lambda i, j, k: (i, k))
    if b3 and tb:
        b_spec = spec((None, tn, tk), lambda i, j, k: (k, j, 0))
    elif b3:
        b_spec = spec((None, tk, tn), lambda i, j, k: (j, k, 0))
    elif tb:
        b_spec = spec((tn, tk), lambda i, j, k: (j, k))
    else:
        b_spec = spec((tk, tn), lambda i, j, k: (k, j))
    if out3:
        o_spec = spec((None, tm, tn), lambda i, j, k: (j, i, 0))
    else:
        o_spec = spec((tm, tn), lambda i, j, k: (i, j))
    ins, specs = [a, b], [a_spec, b_spec]
    if add is not None:
        ins.append(add)
        specs.append(o_spec)
    grid = (Nn // tn, M // tm, nk) if j_outer else (M // tm, Nn // tn, nk)
    return _pcall(
        body, name=name, grid=grid, in_specs=specs, out_specs=o_spec,
        out_shape=jax.ShapeDtypeStruct((out3, M, tn) if out3 else (M, Nn), out_dtype),
        scratch_shapes=[pltpu.VMEM((tm, tn), F32)] if nk > 1 else [],
        compiler_params=_params(("parallel", "parallel", "arbitrary")),
    )(*ins)


def _mm_multi(a_list, b_list, *, name, M, T=None, add=None, out_dtype=F32, tm=512, post=None):
    tm = _pick(T or M, tm)
    ns = len(a_list)
    dils = [a[1] if isinstance(a, tuple) else 0 for a in a_list]
    a_arrs = [a[0] if isinstance(a, tuple) else a for a in a_list]
    widths = [a.shape[-1] // max(r, 1) for a, r in zip(a_arrs, dils)]
    b_arrs, b_specs = [], []
    for b in b_list:
        arr, shp, idx = b if isinstance(b, tuple) else (b, b.shape, (0, 0))
        b_arrs.append(arr)
        b_specs.append(pl.BlockSpec(shp, lambda i, idx=idx: idx))
    Nn = b_specs[0].block_shape[1]
    dn = (((1,), (0,)), ((), ()))
    nmm = 2 * ns + (1 if add is not None else 0)
    p_arrs, p_in_specs, p_out_specs, p_out_shape, p_fn = post or ([], [], None, None, None)
    nin = nmm + len(p_arrs)
    nout = len(p_out_specs) if post else 1

    def body(*refs):
        a_refs, b_refs, scr = refs[:ns], refs[ns:2 * ns], list(refs[nin + nout:])
        acc = None
        for a_ref, b_ref, r in zip(a_refs, b_refs, dils):
            av = _from_view(a_ref, scr.pop(0), r) if r > 1 else a_ref[...]
            part = lax.dot_general(av.astype(BF16), b_ref[...], dn, preferred_element_type=F32)
            acc = part if acc is None else acc + part
        if add is not None:
            acc = acc + refs[2 * ns][...]
        if post:
            p_fn(acc, refs[nmm:nin], refs[nin:nin + nout])
        else:
            refs[nin][...] = acc.astype(out_dtype)

    tpe = (T or M) // tm
    a_specs = [pl.BlockSpec((None, tm // r, r * w), lambda i: (i // tpe, i % tpe, 0)) if r
               else pl.BlockSpec((tm, w), lambda i: (i, 0)) for r, w in zip(dils, widths)]
    o_spec = pl.BlockSpec((tm, Nn), lambda i: (i, 0))
    specs = a_specs + b_specs
    ins = a_arrs + b_arrs
    if add is not None:
        specs.append(o_spec)
        ins.append(add)
    scratch = [pltpu.VMEM((w // LANES, tm, LANES), F32) for r, w in zip(dils, widths) if r > 1]
    return _pcall(body, name=name, grid=(M // tm,), in_specs=specs + list(p_in_specs),
                  out_specs=list(p_out_specs) if post else o_spec, scratch_shapes=scratch,
                  out_shape=list(p_out_shape) if post else jax.ShapeDtypeStruct((M, Nn), out_dtype),
                  compiler_params=_params(("arbitrary",) if post else ("parallel",)))(*ins, *p_arrs)


def _dw_view(d3, u, r, *, name, tk=1024):
    NB, tsub, rw = d3.shape
    W, T, D = rw // r, tsub * r, u.shape[1]
    tk = _pick(T, tk)
    tpe, nk = T // tk, NB * T // tk

    def body(d_ref, u_ref, o_ref, acc, scr):
        k = pl.program_id(0)
        dv = _from_view(d_ref, scr, r).astype(BF16)
        part = lax.dot_general(dv, u_ref[...], _TN, preferred_element_type=F32)

        @pl.when(k == 0)
        def _():
            acc[...] = part

        @pl.when(k > 0)
        def _():
            acc[...] += part

        @pl.when(k == nk - 1)
        def _():
            o_ref[...] = acc[...].astype(o_ref.dtype)

    return _pcall(
        body, name=name, grid=(nk,),
        in_specs=[pl.BlockSpec((None, tk // r, rw), lambda k: (k // tpe, k % tpe, 0)), pl.BlockSpec((tk, D), lambda k: (k, 0))],
        out_specs=pl.BlockSpec((W, D), lambda k: (0, 0)), out_shape=jax.ShapeDtypeStruct((W, D), BF16),
        scratch_shapes=[pltpu.VMEM((W, D), F32), pltpu.VMEM((W // LANES, tk, LANES), F32)],
        compiler_params=_params(("arbitrary",)))(d3, u)


def _lane(shape):
    return lax.broadcasted_iota(jnp.int32, shape, len(shape) - 1)


def _rot_half(v):
    w = v.shape[-1]
    first = (_lane(v.shape) % HEAD_DIM) < (HEAD_DIM // 2)
    return jnp.where(first, pltpu.roll(v, w - HEAD_DIM // 2, v.ndim - 1), pltpu.roll(v, HEAD_DIM // 2, v.ndim - 1))


def _widen(t, w):
    return t if w == t.shape[-1] else jnp.concatenate([t] * (w // t.shape[-1]), axis=-1)


def _unrope(v, cos, sins):
    w = v.shape[-1]
    return v * _widen(cos, w) - _rot_half(v) * _widen(sins, w)


def _rope_tables(positions):
    half = HEAD_DIM // 2
    inv = ROPE_THETA ** (-jnp.arange(half, dtype=F32) / half)
    ang = positions.astype(F32)[..., None] * inv
    cos, sin = jnp.cos(ang), jnp.sin(ang)
    cosf = jnp.concatenate([cos, cos, cos, cos], axis=-1)
    sins = jnp.concatenate([-sin, sin, -sin, sin], axis=-1)
    n = positions.shape[0] * positions.shape[1]
    return cosf.reshape(n, PAIR_W), sins.reshape(n, PAIR_W)


def _inproj(x2, scale, shift, w, cosf, sins, flags, *, T, name):
    N, D = x2.shape
    tm, tn = _pick(T, 512), VAR_W
    tpe = T // tm

    def body(x_ref, sc_ref, sh_ref, w_ref, c_ref, s_ref, f_ref, *outs):
        o_refs, u_ref = outs[:N_VAR], outs[N_VAR]
        j = pl.program_id(1)

        @pl.when(j == 0)
        def _():
            u_ref[...] = (x_ref[...] * (1.0 + sc_ref[0]) + sh_ref[0]).astype(BF16)

        acc = lax.dot_general(u_ref[...], w_ref[...], (((1,), (1,)), ((), ())), preferred_element_type=F32)
        fl = f_ref[...]
        ce = 1.0 + (_widen(c_ref[...], tn) - 1.0) * fl
        se = _widen(s_ref[...], tn) * fl
        res = acc * ce + _rot_half(acc) * se
        for v in range(N_VAR):
            @pl.when(j == v)
            def _(v=v):
                _to_view(res, o_refs[v], outs[N_VAR + 1], VAR_DIL[v])

    ex = pl.BlockSpec((1, 1, D), lambda i, j: (i // tpe, 0, 0))
    tab = pl.BlockSpec((tm, PAIR_W), lambda i, j: (i, 0))
    keep = lambda w_: pl.BlockSpec((tm, w_), lambda i, j: (i, 0))
    vspec = lambda r: pl.BlockSpec((None, tm // r, r * tn), lambda i, j: (i // tpe, i % tpe, 0))
    vshape = lambda r: jax.ShapeDtypeStruct((N // T, T // r, r * tn), BF16)
    return _pcall(
        body, name=name, grid=(N // tm, N_VAR),
        in_specs=[keep(D), ex, ex, pl.BlockSpec((tn, D), lambda i, j: (j, 0)), tab, tab,
                  pl.BlockSpec((1, tn), lambda i, j: (0, j))],
        out_specs=[vspec(r) for r in VAR_DIL] + [keep(D)],
        out_shape=[vshape(r) for r in VAR_DIL] + [jax.ShapeDtypeStruct((N, D), BF16)],
        scratch_shapes=[pltpu.VMEM((tn // LANES, tm, LANES), F32)],
        compiler_params=_params(("parallel", "arbitrary")),
    )(x2, scale, shift, w, cosf, sins, flags)


class _Geom:
    def __init__(self, g):
        if g is None:
            self.r, self.nq, self.n_back, self.sink = 1, A_Q_HEADS, A_WINDOW - 1, True
            self.qw, self.kw = QA_W, KA_W
            self.qidx = lambda j: 0
            self.kidx = lambda j: QA_W // KA_W
            self.vidx = lambda j: QA_W // KA_W + 1
        else:
            window, r = B_PATTERNS[g]
            self.r, self.nq, self.n_back, self.sink = r, B_GROUP_HEADS, window // r, False
            self.qw, self.kw = GB_W, GB_W
            self.qidx = lambda j: 3 * j
            self.kidx = lambda j: 3 * j + 1
            self.vidx = lambda j: 3 * j + 2
        self.ntile = self.qw // PAIR_W


def _stack_heads(t, scale=None):
    first = _lane(t.shape) < HEAD_DIM
    z = jnp.zeros_like(t)
    if scale is not None:
        t = t * jnp.asarray(scale, t.dtype)
    return jnp.concatenate([jnp.where(first, t, z), jnp.where(first, z, t)], axis=0)


def _lse_col(t):
    return jnp.concatenate([t[:, 0:1], t[:, HEAD_DIM:HEAD_DIM + 1]], axis=0)


def _lse_rows(t, width):
    first = _lane(t.shape) < HEAD_DIM
    other = pltpu.roll(t, HEAD_DIM, 1)
    full = jnp.concatenate([jnp.where(first, t, other), jnp.where(first, other, t)], axis=0)
    return _widen(full, width)


def _unstack_heads(v2):
    return jnp.where(_lane((BLOCK, PAIR_W)) < HEAD_DIM, v2[:BLOCK], v2[BLOCK:])


def _dup_head(t, kh):
    tf = t.astype(F32)
    keep = (_lane(t.shape) < HEAD_DIM) if kh == 0 else (_lane(t.shape) >= HEAD_DIM)
    return jnp.where(keep, tf, pltpu.roll(tf, HEAD_DIM, 1)).astype(t.dtype)


def _fold_heads(t):
    return t + pltpu.roll(t, HEAD_DIM, 1)


def _band_mask(rows, i, n_back, single):
    nkeys = BLOCK if single else 2 * BLOCK
    qi = jnp.bitwise_and(lax.broadcasted_iota(jnp.int32, (rows, nkeys), 0), BLOCK - 1)
    ki = lax.broadcasted_iota(jnp.int32, (rows, nkeys), 1)
    if single:
        return qi >= ki
    dist = qi + BLOCK - ki
    return jnp.logical_and(jnp.logical_and(dist >= 0, dist <= n_back), jnp.logical_or(ki >= BLOCK, i > 0))


def _per_block(col, scalars, fn):
    return jnp.concatenate([fn(col[b * BLOCK:(b + 1) * BLOCK], sc) for b, sc in enumerate(scalars)], axis=0)


def _sink_slot(rows):
    qi = jnp.bitwise_and(lax.broadcasted_iota(jnp.int32, (rows, 2 * BLOCK), 0), BLOCK - 1)
    return qi == lax.broadcasted_iota(jnp.int32, (rows, 2 * BLOCK), 1)


def _sink_scores(rows, sinks):
    blk = lax.broadcasted_iota(jnp.int32, (rows, 2 * BLOCK), 0) // BLOCK
    out = jnp.full((rows, 2 * BLOCK), sinks[-1], F32)
    for b in range(len(sinks) - 2, -1, -1):
        out = jnp.where(blk == b, sinks[b], out)
    return out


def _softmax_parts(s, valid, sinks):
    s = jnp.where(valid, s, NEG_INF)
    if sinks is not None:
        slot = _sink_slot(s.shape[0])
        s = jnp.where(slot, _sink_scores(s.shape[0], sinks), s)
    m = jnp.max(s, axis=1, keepdims=True)
    p = jnp.exp(s - m)
    den = jnp.sum(p, axis=1, keepdims=True)
    if sinks is not None:
        p = jnp.where(slot, 0.0, p)
    return p, m, den


_NT = (((1,), (1,)), ((), ()))
_TN = (((0,), (0,)), ((), ()))


def _rows2(prev_ref, cur_ref, cs, single=False):
    if single:
        return cur_ref[0, :, cs]
    return jnp.concatenate([prev_ref[0, :, cs], cur_ref[0, :, cs]], axis=0)


def _sink_scalars(sink_ref, first, nblocks):
    return [sink_ref[first + b] for b in range(nblocks)]


def _tile(t):
    return slice(t * PAIR_W, (t + 1) * PAIR_W)


def _attn_fwd(qkv, sinks, g, *, NB, T, name):
    geo = _Geom(g)
    r, qw, kw, ntile = geo.r, geo.qw, geo.kw, geo.ntile
    tsub = T // r
    nblk = tsub // BLOCK
    qkv3 = qkv.reshape(NB, tsub, r * VAR_W)
    out_dtype = BF16 if g is None else F32
    tiles_per_kv = ntile // A_KV_HEADS

    single = nblk == 1

    def body(q_ref, kp_ref, kc_ref, vp_ref, vc_ref, sink_ref, o_ref, l_ref):
        i = pl.program_id(2)
        if geo.sink:
            kall, vall = _rows2(kp_ref, kc_ref, _tile(0)), _rows2(vp_ref, vc_ref, _tile(0))
            kdup = [_dup_head(kall, kh) for kh in range(A_KV_HEADS)]
            vdup = [_dup_head(vall, kh) for kh in range(A_KV_HEADS)]
            tiles = [[t] for t in range(ntile)]
            q2s = [_stack_heads(q_ref[0, :, _tile(t)], SCALE) for t in range(ntile)]
            kks = [kdup[t // tiles_per_kv] for t in range(ntile)]
            vvs = [vdup[t // tiles_per_kv] for t in range(ntile)]
            sinkcols = [_sink_scalars(sink_ref, 2 * t, 2) for t in range(ntile)]
        else:
            tiles = [[t] for t in range(ntile)]
            q2s = [_stack_heads(q_ref[0, :, _tile(t)], SCALE) for t in range(ntile)]
            kks = [_rows2(kp_ref, kc_ref, _tile(t), single) for t in range(ntile)]
            vvs = [_rows2(vp_ref, vc_ref, _tile(t), single) for t in range(ntile)]
            sinkcols = [None] * ntile
        valid = _band_mask(q2s[0].shape[0], i, geo.n_back, single)
        ss = [lax.dot_general(q2, kk, _NT, preferred_element_type=F32) for q2, kk in zip(q2s, kks)]
        parts = [_softmax_parts(s, valid, sc) for s, sc in zip(ss, sinkcols)]
        o2s = [jnp.dot(p.astype(BF16), vv, preferred_element_type=F32) / den for (p, m, den), vv in zip(parts, vvs)]
        for ts, o2, (p, m, den) in zip(tiles, o2s, parts):
            lse2 = jnp.broadcast_to(m + jnp.log(den), (o2.shape[0], PAIR_W))
            for n, t in enumerate(ts):
                rows = slice(2 * BLOCK * n, 2 * BLOCK * (n + 1))
                o_ref[0, :, _tile(t)] = _unstack_heads(o2[rows]).astype(out_dtype)
                l_ref[0, :, _tile(t)] = _unstack_heads(lse2[rows])

    prev = lambda i: jnp.maximum(i - 1, 0)
    in_specs = [
        pl.BlockSpec((1, BLOCK, qw), lambda b, j, i: (b, i, geo.qidx(j))),
        pl.BlockSpec((1, BLOCK, kw), lambda b, j, i: (b, prev(i), geo.kidx(j))),
        pl.BlockSpec((1, BLOCK, kw), lambda b, j, i: (b, i, geo.kidx(j))),
        pl.BlockSpec((1, BLOCK, kw), lambda b, j, i: (b, prev(i), geo.vidx(j))),
        pl.BlockSpec((1, BLOCK, kw), lambda b, j, i: (b, i, geo.vidx(j))),
        pl.BlockSpec(memory_space=pltpu.SMEM),
    ]
    o_spec = pl.BlockSpec((1, BLOCK, qw), lambda b, j, i: (b, i, j))
    shape = (NB, tsub, r * qw)
    o, lse = _pcall(
        body, name=name, grid=(NB, r, nblk), in_specs=in_specs, out_specs=[o_spec, o_spec],
        out_shape=[jax.ShapeDtypeStruct(shape, out_dtype), jax.ShapeDtypeStruct(shape, F32)],
        compiler_params=_params(("parallel", "parallel", "arbitrary")),
    )(qkv3, qkv3, qkv3, qkv3, qkv3, sinks)
    return o, lse


def _attn_bwd(qkv, do, lse, dlse, cosf, sins, sinks, g, *, NB, T, name):
    geo = _Geom(g)
    r, qw, kw, ntile = geo.r, geo.qw, geo.kw, geo.ntile
    tsub = T // r
    nblk = tsub // BLOCK
    view = lambda a, w: a.reshape(NB, tsub, r * w)
    has_dlse = dlse is not None
    tiles_per_kv = ntile // A_KV_HEADS

    single = nblk == 1
    krows = BLOCK if single else 2 * BLOCK
    nsteps = 1 if single else nblk + 1

    def grads(q2s, kks, vvs, do2s, i, lserows, sinkcols, dlrows):
        nrow = q2s[0].shape[0]
        ki = lax.broadcasted_iota(jnp.int32, (krows, nrow), 0)
        qi = jnp.bitwise_and(lax.broadcasted_iota(jnp.int32, (krows, nrow), 1), BLOCK - 1)
        if single:
            valid = qi >= ki
        else:
            dist = qi + BLOCK - ki
            valid = jnp.logical_and(jnp.logical_and(dist >= 0, dist <= geo.n_back), jnp.logical_or(ki >= BLOCK, i > 0))
        sts = [lax.dot_general(kk, q2, _NT, preferred_element_type=F32) for q2, kk in zip(q2s, kks)]
        dpts = [lax.dot_general(vv, do2, _NT, preferred_element_type=F32) for do2, vv in zip(do2s, vvs)]
        pts, dsts, sks = [], [], []
        for st, dpt, ls, sc, dl in zip(sts, dpts, lserows, sinkcols, dlrows):
            sv = jnp.where(valid, st, NEG_INF)
            if sc is not None:
                slot = ki == qi
                blk = lax.broadcasted_iota(jnp.int32, (krows, nrow), 1) // BLOCK
                sink = jnp.full((krows, nrow), sc[-1], F32)
                for b in range(len(sc) - 2, -1, -1):
                    sink = jnp.where(blk == b, sc[b], sink)
                sv = jnp.where(slot, sink, sv)
                dpt = jnp.where(slot, 0.0, dpt)
            pt = jnp.exp(sv - ls)
            delta = jnp.sum(pt * dpt, axis=0, keepdims=True)
            if dl is not None:
                delta = delta - dl
            dst = pt * (dpt - delta)
            if sc is not None:
                cols = lambda a, b: a[:, b * BLOCK:(b + 1) * BLOCK]
                sks.append([jnp.sum(jnp.where(cols(slot, b), cols(dst, b), 0.0)) for b in range(len(sc))])
                dst, pt = jnp.where(slot, 0.0, dst), jnp.where(slot, 0.0, pt)
            else:
                sks.append(None)
            pts.append(pt.astype(BF16))
            dsts.append(dst.astype(BF16))
        dq2s = [lax.dot_general(dst, kk, _TN, preferred_element_type=F32) * SCALE for dst, kk in zip(dsts, kks)]
        dkks = [jnp.dot(dst, q2, preferred_element_type=F32) for dst, q2 in zip(dsts, q2s)]
        dvvs = [jnp.dot(pt, do2, preferred_element_type=F32) for pt, do2 in zip(pts, do2s)]
        return dq2s, dkks, dvvs, sks

    def stat_row(t):
        tt = t.T
        return jnp.concatenate([tt[0:1, :], tt[HEAD_DIM:HEAD_DIM + 1, :]], axis=1)

    def body(*refs):
        it = iter(refs)
        q_ref, kp_ref, kc_ref, vp_ref, vc_ref, do_ref, l_ref = (next(it) for _ in range(7))
        dl_ref = next(it) if has_dlse else None
        c_ref, s_ref, sink_ref, o_ref, ds_ref, dq_s, dk_s, dv_s, car_q, car_k, car_v = (next(it) for _ in range(11))
        b, j, i = pl.program_id(0), pl.program_id(1), pl.program_id(2)

        @pl.when(jnp.logical_and(b == 0, jnp.logical_and(j == 0, i == 0)))
        def _():
            ds_ref[...] = jnp.zeros_like(ds_ref)

        def compute():
            if geo.sink:
                kall, vall = _rows2(kp_ref, kc_ref, _tile(0)), _rows2(vp_ref, vc_ref, _tile(0))
                nb = 2 * tiles_per_kv
                tiles = [[kh * tiles_per_kv + t for t in range(tiles_per_kv)] for kh in range(A_KV_HEADS)]
                cat = lambda f, ts: jnp.concatenate([f(t) for t in ts], axis=0)
                dq2s, dkks, dvvs, sks = grads(
                    [cat(lambda t: _stack_heads(q_ref[0, :, _tile(t)], SCALE), ts) for ts in tiles],
                    [_dup_head(kall, kh) for kh in range(A_KV_HEADS)],
                    [_dup_head(vall, kh) for kh in range(A_KV_HEADS)],
                    [cat(lambda t: _stack_heads(do_ref[0, :, _tile(t)]), ts) for ts in tiles], i,
                    [jnp.concatenate([stat_row(l_ref[0, :, _tile(t)]) for t in ts], axis=1) for ts in tiles],
                    [_sink_scalars(sink_ref, kh * nb, nb) for kh in range(A_KV_HEADS)], [None] * A_KV_HEADS)
                lane1 = _lane((1, PAIR_W))
                dsink = jnp.zeros((1, PAIR_W), F32)
                for kh, (ts, dq2, sk) in enumerate(zip(tiles, dq2s, sks)):
                    for n, t in enumerate(ts):
                        dq_s[:, _tile(t)] = _unstack_heads(dq2[2 * BLOCK * n:2 * BLOCK * (n + 1)])
                    for bb in range(nb):
                        dsink = dsink + jnp.where(lane1 == kh * nb + bb, sk[bb], 0.0)
                second = _lane((krows, PAIR_W)) >= HEAD_DIM
                dk_s[...] = jnp.where(second, _fold_heads(dkks[1]), _fold_heads(dkks[0]))
                dv_s[...] = jnp.where(second, _fold_heads(dvvs[1]), _fold_heads(dvvs[0]))
                ds_ref[0:1, :] += dsink
            else:
                dq2s, dkks, dvvs, _ = grads(
                    [_stack_heads(q_ref[0, :, _tile(t)], SCALE) for t in range(ntile)],
                    [_rows2(kp_ref, kc_ref, _tile(t), single) for t in range(ntile)],
                    [_rows2(vp_ref, vc_ref, _tile(t), single) for t in range(ntile)],
                    [_stack_heads(do_ref[0, :, _tile(t)]) for t in range(ntile)], i,
                    [stat_row(l_ref[0, :, _tile(t)]) for t in range(ntile)], [None] * ntile,
                    [stat_row(dl_ref[0, :, _tile(t)]) for t in range(ntile)])
                for t in range(ntile):
                    dq_s[:, _tile(t)] = _unstack_heads(dq2s[t])
                    dk_s[0:krows, _tile(t)] = dkks[t]
                    dv_s[0:krows, _tile(t)] = dvvs[t]

        def emit(dq, dk, dv):
            cos, sn = c_ref[0], s_ref[0]
            o_ref[0, :, 0:qw] = _unrope(dq, cos, sn).astype(BF16)
            o_ref[0, :, qw:qw + kw] = _unrope(dk, cos, sn).astype(BF16)
            o_ref[0, :, qw + kw:qw + 2 * kw] = dv.astype(BF16)
            if qw + 2 * kw < VAR_W:
                o_ref[0, :, qw + 2 * kw:VAR_W] = jnp.zeros((BLOCK, VAR_W - qw - 2 * kw), BF16)

        if single:
            compute()
            emit(dq_s[...], dk_s[0:BLOCK, :], dv_s[0:BLOCK, :])
            return

        @pl.when(i == 0)
        def _():
            car_q[...] = jnp.zeros_like(car_q)
            car_k[...] = jnp.zeros_like(car_k)
            car_v[...] = jnp.zeros_like(car_v)

        @pl.when(i == nblk)
        def _():
            dk_s[...] = jnp.zeros_like(dk_s)
            dv_s[...] = jnp.zeros_like(dv_s)

        pl.when(i < nblk)(compute)
        emit(car_q[...], car_k[...] + dk_s[0:BLOCK, :], car_v[...] + dv_s[0:BLOCK, :])
        car_q[...] = dq_s[...]
        car_k[...] = dk_s[BLOCK:2 * BLOCK, :]
        car_v[...] = dv_s[BLOCK:2 * BLOCK, :]

    cur = lambda i: jnp.minimum(i, nblk - 1)
    prv = lambda i: jnp.maximum(jnp.minimum(i, nblk - 1) - 1, 0)
    outb = lambda i: jnp.maximum(i - 1, 0)
    qrow = pl.BlockSpec((1, BLOCK, qw), lambda b, j, i: (b, cur(i), j))
    in_specs = [
        pl.BlockSpec((1, BLOCK, qw), lambda b, j, i: (b, cur(i), geo.qidx(j))),
        pl.BlockSpec((1, BLOCK, kw), lambda b, j, i: (b, prv(i), geo.kidx(j))),
        pl.BlockSpec((1, BLOCK, kw), lambda b, j, i: (b, cur(i), geo.kidx(j))),
        pl.BlockSpec((1, BLOCK, kw), lambda b, j, i: (b, prv(i), geo.vidx(j))),
        pl.BlockSpec((1, BLOCK, kw), lambda b, j, i: (b, cur(i), geo.vidx(j))),
        qrow, qrow,
    ]
    ins = [view(qkv, VAR_W)] * 5 + [view(do, qw), view(lse, qw)]
    if has_dlse:
        in_specs.append(qrow)
        ins.append(view(dlse, qw))
    in_specs += [
        pl.BlockSpec((1, BLOCK, PAIR_W), lambda b, j, i: (b, outb(i), j)),
        pl.BlockSpec((1, BLOCK, PAIR_W), lambda b, j, i: (b, outb(i), j)),
        pl.BlockSpec(memory_space=pltpu.SMEM),
    ]
    ins += [view(cosf, PAIR_W), view(sins, PAIR_W), sinks]
    scratch = [pltpu.VMEM((BLOCK, qw), F32), pltpu.VMEM((2 * BLOCK, kw), F32), pltpu.VMEM((2 * BLOCK, kw), F32),
               pltpu.VMEM((BLOCK, qw), F32), pltpu.VMEM((BLOCK, kw), F32), pltpu.VMEM((BLOCK, kw), F32)]
    dqkv, dsink = _pcall(
        body, name=name, grid=(NB, r, nsteps), in_specs=in_specs,
        out_specs=[pl.BlockSpec((1, BLOCK, VAR_W), lambda b, j, i: (b, outb(i), j)),
                   pl.BlockSpec((8, PAIR_W), lambda b, j, i: (0, 0))],
        out_shape=[jax.ShapeDtypeStruct((NB, tsub, r * VAR_W), BF16), jax.ShapeDtypeStruct((8, PAIR_W), F32)],
        scratch_shapes=scratch, compiler_params=_params(("arbitrary", "arbitrary", "arbitrary")),
    )(*ins)
    return dqkv, dsink


class _Rows:
    def __init__(self, N, T, tm):
        self.N, self.tm, self.tpe, self.grid = N, tm, T // tm, (N // tm,)

    def row(self, w, col=0):
        return pl.BlockSpec((self.tm, w), lambda i: (i, col))

    def ex(self, w):
        return pl.BlockSpec((1, 1, w), lambda i: (i // self.tpe, 0, 0))

    def const(self, shape):
        return pl.BlockSpec(shape, lambda i: tuple(0 for _ in shape))

    def view(self, w, r):
        return pl.BlockSpec((None, self.tm // r, r * w), lambda i: (i // self.tpe, i % self.tpe, 0))

    def first_of_example(self):
        return pl.program_id(0) % self.tpe == 0


def _acc(ref, first, val):
    @pl.when(first)
    def _():
        ref[0] = val

    @pl.when(jnp.logical_not(first))
    def _():
        ref[0] += val


def _colsum(v):
    return jnp.sum(v, axis=0, keepdims=True)


def _ln_stats(r):
    mu = jnp.mean(r, axis=-1, keepdims=True)
    xc = r - mu
    var = jnp.mean(xc * xc, axis=-1, keepdims=True)
    rstd = lax.rsqrt(var + LN_EPS)
    return xc * rstd, rstd


def _ln_bwd(dy, xhat, rstd, gain):
    dxh = dy * gain
    return rstd * (dxh - jnp.mean(dxh, axis=-1, keepdims=True) - xhat * jnp.mean(dxh * xhat, axis=-1, keepdims=True))


def _from_view(ref, scr, r):
    if r == 1:
        return ref[...]
    rows, w = ref.shape[0], ref.shape[1] // r
    for j in range(r):
        for c in range(w // LANES):
            scr.at[c][pl.ds(j, rows, stride=r), :] = ref[:, j * w + c * LANES:j * w + (c + 1) * LANES].astype(F32)
    return jnp.concatenate([scr[c] for c in range(w // LANES)], axis=1)


def _to_view(val, ref, scr, r):
    if r == 1:
        ref[...] = val.astype(ref.dtype)
        return
    rows, w = ref.shape[0], ref.shape[1] // r
    for c in range(w // LANES):
        scr[c] = val[:, c * LANES:(c + 1) * LANES]
    for j in range(r):
        for c in range(w // LANES):
            ref[:, j * w + c * LANES:j * w + (c + 1) * LANES] = scr.at[c][pl.ds(j, rows, stride=r), :].astype(ref.dtype)


def _silu_parts(v):
    s = jax.nn.sigmoid(v)
    return v * s, s * (1.0 + v * (1.0 - s))


def _local_step(x, mod, positions, w_in, rest_weights, sinks, ln1_g, ln1_b, ln2_g, ln2_b, target, hook=None):
    hook = hook or (lambda event, **data: None)
    NB, T, D = x.shape
    N = NB * T
    x2 = x.reshape(N, D)
    tgt2 = target.reshape(N, D)
    shift_m, scale_m, gate_m, shift_f, scale_f, gate_f = [mod[:, None, k * D:(k + 1) * D] for k in range(6)]
    cosf, sins = _rope_tables(positions)
    col = jnp.arange(QKV_P)
    vcol = col % VAR_W
    flags = jnp.where(col < VAR_W, vcol < QA_W + KA_W, vcol < 2 * GB_W).astype(F32)[None]
    R = _Rows(N, T, _pick(T, 256))
    sds = jax.ShapeDtypeStruct
    exsum = lambda w=D: sds((NB, 1, w), F32)
    ngrp = len(B_PATTERNS)

    *qkv, u = _inproj(x2, scale_m, shift_m, w_in, cosf, sins, flags, T=T, name="inproj_qkv")
    gates = _mm(u, w_in[QKV_P:], tb=True, out_dtype=BF16, name="inproj_gates")
    oa, la = _attn_fwd(qkv[0], sinks, None, NB=NB, T=T, name="attn_a_fwd")
    oa = oa.reshape(N, QA_W)
    ob_parts = [_attn_fwd(qkv[1 + g], sinks, g, NB=NB, T=T, name=f"attn_b{g}_fwd") for g in range(ngrp)]
    (o1, l1), (o2, l2), (o3, l3) = ob_parts
    w_a, w_b, w_o, w_gu, w_d = rest_weights()
    F = w_d.shape[0]
    dil = [r_ for _, r_ in B_PATTERNS]
    views = [R.view(GB_W, r_) for r_ in dil]
    tokbuf = pltpu.VMEM((GB_W // LANES, R.tm, LANES), F32)

    def merge_fwd(o1r, o2r, o3r, l1r, l2r, l3r, ob_ref, *bufs):
        os_ = [_from_view(ref, bufs[n], dil[n]) for n, ref in enumerate((o1r, o2r, o3r))]
        la, lb, lc = [_from_view(ref, bufs[3 + n], dil[n]) for n, ref in enumerate((l1r, l2r, l3r))]
        mx = jnp.maximum(jnp.maximum(la, lb), lc)
        ea, eb, ec = jnp.exp(la - mx), jnp.exp(lb - mx), jnp.exp(lc - mx)
        ob_ref[...] = ((ea * os_[0] + eb * os_[1] + ec * os_[2]) / (ea + eb + ec)).astype(BF16)

    ob = _pcall(merge_fwd, name="merge_fwd", grid=R.grid, in_specs=views + views, out_specs=R.row(GB_W),
                out_shape=sds((N, GB_W), BF16), scratch_shapes=[tokbuf] * 6,
                compiler_params=_params(("parallel",)))(o1, o2, o3, l1, l2, l3)

    f32 = lambda ref: ref[...].astype(F32)
    Rm = _Rows(N, T, _pick(T, 512))

    def mix_out(oa_r, ob_r, ga_r, gb_r, x_r, gm_r, g_r, b_r, sf_r, hf_r, wa_r, wb_r, wo_r,
                ya_ref, yb_ref, mg_ref, y_ref, r1_ref, u2_ref):
        ya = jnp.dot(oa_r[...], wa_r[...], preferred_element_type=F32).astype(BF16)
        yb = jnp.concatenate([jnp.dot(ob_r[...], wb_r[s_], preferred_element_type=F32)
                              for s_ in range(w_b.shape[0])], axis=1).astype(BF16)
        merged = (jax.nn.sigmoid(f32(ga_r)) * ya.astype(F32) + jax.nn.sigmoid(f32(gb_r)) * yb.astype(F32)).astype(BF16)
        y = jnp.dot(merged, wo_r[...], preferred_element_type=F32)
        r1 = ALPHA * x_r[...] + (1.0 + gm_r[0]) * y
        xhat, _ = _ln_stats(r1)
        x1 = xhat * g_r[...] + b_r[...]
        ya_ref[...], yb_ref[...], mg_ref[...], y_ref[...], r1_ref[...] = ya, yb, merged, y, r1
        u2_ref[...] = (x1 * (1.0 + sf_r[0]) + hf_r[0]).astype(BF16)

    ya, yb, merged, y, r1, u2 = _pcall(
        mix_out, name="mix_out", grid=Rm.grid,
        in_specs=[Rm.row(QA_W), Rm.row(GB_W), Rm.row(D, 0), Rm.row(D, 1), Rm.row(D), Rm.ex(D), Rm.const((1, D)),
                  Rm.const((1, D)), Rm.ex(D), Rm.ex(D), Rm.const(w_a.shape), Rm.const(w_b.shape), Rm.const(w_o.shape)],
        out_specs=[Rm.row(D)] * 6,
        out_shape=[sds((N, D), BF16)] * 3 + [sds((N, D), F32)] * 2 + [sds((N, D), BF16)],
        compiler_params=_params(("parallel",)))(oa, ob, gates, gates, x2, gate_m, ln1_g, ln1_b, scale_f, shift_f,
                                                w_a, w_b, w_o)

    tnf = w_gu.shape[2]
    nft = w_gu.shape[0] // 2
    tmf = _pick(N, 512)

    def ffn_up(u_r, wg_r, wu_r, hg_ref, hu_ref, a_ref):
        hg = jnp.dot(u_r[...], wg_r[...], preferred_element_type=F32)
        hu = jnp.dot(u_r[...], wu_r[...], preferred_element_type=F32)
        sl, _ = _silu_parts(hg)
        hg_ref[...] = hg.astype(BF16)
        hu_ref[...] = hu.astype(BF16)
        a_ref[...] = (sl * hu).astype(BF16)

    ftile = pl.BlockSpec((tmf, tnf), lambda j, i: (i, j))
    hg, hu, act = _pcall(
        ffn_up, name="ffn_up", grid=(nft, N // tmf),
        in_specs=[pl.BlockSpec((tmf, D), lambda j, i: (i, 0)), pl.BlockSpec((None, D, tnf), lambda j, i: (j, 0, 0)),
                  pl.BlockSpec((None, D, tnf), lambda j, i: (j + nft, 0, 0))],
        out_specs=[ftile] * 3, out_shape=[sds((N, F), BF16)] * 3,
        compiler_params=_params(("arbitrary", "parallel")))(u2, w_gu, w_gu)
    def ffn_down_norm2(act_r, wd_r, r1_r, g1_r, b1_r, t_r, gf_r, g_r, b_r,
                       dy2_ref, dx1_ref, dgf_ref, dg_ref, db_ref, loss_ref):
        first = R.first_of_example()
        y2v = jnp.dot(act_r[...], wd_r[...], preferred_element_type=F32)
        x1 = _ln_stats(r1_r[...])[0] * g1_r[...] + b1_r[...]
        r2 = ALPHA * x1 + (1.0 + gf_r[0]) * y2v
        xhat, rstd = _ln_stats(r2)
        err = xhat * g_r[...] + b_r[...] - t_r[...]
        dx2 = err * (1.0 / D)
        dr2 = _ln_bwd(dx2, xhat, rstd, g_r[...])
        dy2_ref[...] = ((1.0 + gf_r[0]) * dr2).astype(BF16)
        dx1_ref[...] = ALPHA * dr2
        _acc(dgf_ref, first, _colsum(dr2 * y2v))
        _acc(dg_ref, first, _colsum(dx2 * xhat))
        _acc(db_ref, first, _colsum(dx2))
        part = 0.5 * jnp.sum(jnp.mean(err * err, axis=-1, keepdims=True))
        _acc(loss_ref, first, jnp.broadcast_to(part, (1, 128)))

    dy2, dx1p, dgate_f, dg2, db2, loss_p = _pcall(
        ffn_down_norm2, name="ffn_down_norm2", grid=R.grid,
        in_specs=[R.row(F), R.const((F, D)), R.row(D), R.const((1, D)), R.const((1, D)), R.row(D), R.ex(D),
                  R.const((1, D)), R.const((1, D))],
        out_specs=[R.row(D), R.row(D), R.ex(D), R.ex(D), R.ex(D), R.ex(128)],
        out_shape=[sds((N, D), BF16), sds((N, D), F32), exsum(), exsum(), exsum(), exsum(128)],
        compiler_params=_params(("arbitrary",)))(act, w_d, r1, ln1_g, ln1_b, tgt2, gate_f, ln2_g, ln2_b)

    g_wd = _mm(act, dy2, ta=True, out_dtype=BF16, name="ffn_down_dw")

    tmd = _pick(N, 256)

    fchunk = _pick(F, 768)

    def ffn_down_dx(dy_r, wd_r, hg_r, hu_r, dh_ref):
        for t in range(F // fchunk):
            cs = slice(t * fchunk, (t + 1) * fchunk)
            da = lax.dot_general(dy_r[...], wd_r[cs, :], _NT, preferred_element_type=F32)
            sl, dsl = _silu_parts(hg_r[:, cs].astype(F32))
            dh_ref[:, cs] = (da * hu_r[:, cs].astype(F32) * dsl).astype(BF16)
            dh_ref[:, F + t * fchunk:F + (t + 1) * fchunk] = (da * sl).astype(BF16)

    rowd = lambda w_: pl.BlockSpec((tmd, w_), lambda i: (i, 0))
    dh = _pcall(
        ffn_down_dx, name="ffn_down_dx", grid=(N // tmd,),
        in_specs=[rowd(D), pl.BlockSpec((F, D), lambda i: (0, 0)), rowd(F), rowd(F)],
        out_specs=rowd(2 * F), out_shape=sds((N, 2 * F), BF16),
        compiler_params=_params(("parallel",)))(dy2, w_d, hg, hu)
    g_wgu = _mm(u2, dh, ta=True, out3=w_gu.shape[0], out_dtype=BF16, name="ffn_up_dw")

    def ffn_up_dx_norm1(dh_r, w_r, dx1p_r, r1_r, y_r, sf_r, gm_r, g_r, b_r,
                        dxp_ref, dy_ref, dsf_ref, dhf_ref, dgm_ref, dg_ref, db_ref):
        first = R.first_of_example()
        du2v = None
        for s_ in range(w_gu.shape[0]):
            part = lax.dot_general(dh_r[:, s_ * tnf:(s_ + 1) * tnf], w_r[s_], _NT, preferred_element_type=F32)
            du2v = part if du2v is None else du2v + part
        dx1 = dx1p_r[...] + du2v * (1.0 + sf_r[0])
        xhat, rstd = _ln_stats(r1_r[...])
        dr1 = _ln_bwd(dx1, xhat, rstd, g_r[...])
        dxp_ref[...] = ALPHA * dr1
        dy_ref[...] = ((1.0 + gm_r[0]) * dr1).astype(BF16)
        _acc(dsf_ref, first, _colsum(du2v * (xhat * g_r[...] + b_r[...])))
        _acc(dhf_ref, first, _colsum(du2v))
        _acc(dgm_ref, first, _colsum(dr1 * y_r[...]))
        _acc(dg_ref, first, _colsum(dx1 * xhat))
        _acc(db_ref, first, _colsum(dx1))

    dxp, dy, dscale_f, dshift_f, dgate_m, dg1, db1 = _pcall(
        ffn_up_dx_norm1, name="ffn_up_dx_norm1", grid=R.grid,
        in_specs=[R.row(2 * F), R.const(w_gu.shape)] + [R.row(D)] * 3 + [R.ex(D), R.ex(D), R.const((1, D)),
                                                                        R.const((1, D))],
        out_specs=[R.row(D), R.row(D)] + [R.ex(D)] * 5,
        out_shape=[sds((N, D), F32), sds((N, D), BF16)] + [exsum()] * 5,
        compiler_params=_params(("arbitrary",)))(dh, w_gu, dx1p, r1, y, scale_f, gate_m, ln1_g, ln1_b)

    g_wo = _mm(merged, dy, ta=True, out_dtype=BF16, name="out_proj_dw")

    def mix_out_bwd(dy_r, ya_r, yb_r, ga_r, gb_r, wo_r, wa_r, wb_r, dya_ref, dyb_ref, dg_ref, doa_ref, dob_ref):
        dm = lax.dot_general(dy_r[...], wo_r[...], _NT, preferred_element_type=F32).astype(BF16).astype(F32)
        sa, sb = jax.nn.sigmoid(f32(ga_r)), jax.nn.sigmoid(f32(gb_r))
        dya, dyb = (dm * sa).astype(BF16), (dm * sb).astype(BF16)
        dya_ref[...], dyb_ref[...] = dya, dyb
        dg_ref[:, :D] = (dm * f32(ya_r) * sa * (1.0 - sa)).astype(BF16)
        dg_ref[:, D:] = (dm * f32(yb_r) * sb * (1.0 - sb)).astype(BF16)
        doa_ref[...] = lax.dot_general(dya, wa_r[...], _NT, preferred_element_type=F32).astype(BF16)
        ds_ = D // w_b.shape[0]
        dob = None
        for s_ in range(w_b.shape[0]):
            part = lax.dot_general(dyb[:, s_ * ds_:(s_ + 1) * ds_], wb_r[s_], _NT, preferred_element_type=F32)
            dob = part if dob is None else dob + part
        dob_ref[...] = dob

    dya, dyb, dgates, doa, dob = _pcall(
        mix_out_bwd, name="mix_out_bwd", grid=Rm.grid,
        in_specs=[Rm.row(D)] * 3 + [Rm.row(D, 0), Rm.row(D, 1), Rm.const(w_o.shape), Rm.const(w_a.shape),
                                    Rm.const(w_b.shape)],
        out_specs=[Rm.row(D), Rm.row(D), Rm.row(2 * D), Rm.row(QA_W), Rm.row(GB_W)],
        out_shape=[sds((N, D), BF16), sds((N, D), BF16), sds((N, 2 * D), BF16), sds((N, QA_W), BF16), sds((N, GB_W), F32)],
        compiler_params=_params(("parallel",)))(dy, ya, yb, gates, gates, w_o, w_a, w_b)

    g_wa = _mm(oa, dya, ta=True, out_dtype=BF16, name="branch_a_dw")
    g_wb = _mm(ob, dyb, ta=True, out3=w_b.shape[0], out_dtype=BF16, name="branch_b_dw")
    hook("rest_grads", g_wa=g_wa, g_wb=g_wb, g_wo=g_wo, g_wgu=g_wgu, g_wd=g_wd)

    seg = (jnp.arange(GB_W)[:, None] // HEAD_DIM == jnp.arange(GB_W)[None, :] // HEAD_DIM).astype(BF16)

    def merge_bwd(dob_r, o1r, o2r, o3r, l1r, l2r, l3r, seg_r, d1, d2, d3, e1, e2, e3, *bufs):
        dob_v = dob_r[...]
        os_ = [_from_view(ref, bufs[n], dil[n]) for n, ref in enumerate((o1r, o2r, o3r))]
        la, lb, lc = [_from_view(ref, bufs[3 + n], dil[n]) for n, ref in enumerate((l1r, l2r, l3r))]
        mx = jnp.maximum(jnp.maximum(la, lb), lc)
        ea, eb, ec = jnp.exp(la - mx), jnp.exp(lb - mx), jnp.exp(lc - mx)
        inv = 1.0 / (ea + eb + ec)
        ws = [ea * inv, eb * inv, ec * inv]

        def headsum(v):
            hi = v.astype(BF16)
            r1_ = v - hi.astype(F32)
            mid = r1_.astype(BF16)
            lo = (r1_ - mid.astype(F32)).astype(BF16)
            sm = seg_r[...]
            return (jnp.dot(hi, sm, preferred_element_type=F32) + jnp.dot(mid, sm, preferred_element_type=F32)
                    + jnp.dot(lo, sm, preferred_element_type=F32))

        dws = [headsum(dob_v * o) for o in os_]
        mean = ws[0] * dws[0] + ws[1] * dws[1] + ws[2] * dws[2]
        for n, (w_, dw_, d_ref, e_ref) in enumerate(zip(ws, dws, (d1, d2, d3), (e1, e2, e3))):
            _to_view(w_ * dob_v, d_ref, bufs[6], dil[n])
            _to_view(w_ * (dw_ - mean), e_ref, bufs[7], dil[n])

    vshape = lambda r_, dt: sds((NB, T // r_, r_ * GB_W), dt)
    mb = _pcall(
        merge_bwd, name="merge_bwd", grid=R.grid, in_specs=[R.row(GB_W)] + views + views + [R.const((GB_W, GB_W))],
        out_specs=views + views, out_shape=[vshape(r_, BF16) for r_ in dil] + [vshape(r_, F32) for r_ in dil],
        scratch_shapes=[tokbuf] * 8, compiler_params=_params(("parallel",)))(dob, o1, o2, o3, l1, l2, l3, seg)
    do_b, dlse_b = mb[:3], mb[3:]
    hook("merge_bwd_done")

    dqkv_a, dsink = _attn_bwd(qkv[0], doa, la, None, cosf, sins, sinks, None, NB=NB, T=T, name="attn_a_bwd")
    hook("attn_a_bwd_done")
    dqkv = [dqkv_a]
    for g in range(ngrp):
        dqkv.append(_attn_bwd(qkv[1 + g], do_b[g], (l1, l2, l3)[g], dlse_b[g], cosf, sins, sinks, g, NB=NB, T=T,
                              name=f"attn_b{g}_bwd")[0])
        hook(f"attn_b{g}_bwd_done")

    g_win = [_mm(d3.reshape(N, VAR_W), u, ta=True, out_dtype=BF16, name=f"inproj_dw{v}") if VAR_DIL[v] == 1
             else _dw_view(d3, u, VAR_DIL[v], name=f"inproj_dw{v}") for v, d3 in enumerate(dqkv)]
    g_win.append(_mm(dgates, u, ta=True, out_dtype=BF16, name=f"inproj_dw{N_VAR}"))
    hook("win_grads", g_win=g_win)
    wvar = lambda v: (w_in, (VAR_W, D), (v, 0))
    dview = lambda v: (dqkv[v], VAR_DIL[v])
    du = _mm_multi([dview(0)], [wvar(0)], M=N, T=T, name="inproj_dx0")
    hook("inproj_dx0_done")
    def x_bwd(duv, ins, outs):
        (dxp_r, x_r, sm_r), (gx_ref, dsm_ref, dhm_ref) = ins, outs
        first = R.first_of_example()
        gx_ref[...] = dxp_r[...] + duv * (1.0 + sm_r[0])
        _acc(dsm_ref, first, _colsum(duv * x_r[...]))
        _acc(dhm_ref, first, _colsum(duv))

    gx, dscale_m, dshift_m = _mm_multi(
        [dview(v) for v in range(1, N_VAR)] + [dgates],
        [wvar(v) for v in range(1, N_VAR)] + [(w_in, (2 * D, D), (QKV_P // (2 * D), 0))],
        M=N, T=T, add=du, tm=R.tm, name="inproj_dx1",
        post=([dxp, x2, scale_m], [R.row(D), R.row(D), R.ex(D)], [R.row(D), R.ex(D), R.ex(D)],
              [sds((N, D), F32), exsum(), exsum()], x_bwd))
    hook("inproj_dx1_done")

    dmod =jnp.concatenate([dshift_m, dscale_m, dgate_m, dshift_f, dscale_f, dgate_f], axis=-1)[:, 0]
    ln_grads = jnp.concatenate([dg1, db1, dg2, db2], axis=1)
    return dict(loss=loss_p[:, 0, 0], grad_x=gx.reshape(NB, T, D), g_win=g_win, g_wa=g_wa, g_wb=g_wb, g_wo=g_wo,
                g_wgu=g_wgu, g_wd=g_wd, dmod=dmod, ln_grads=ln_grads, dsink=dsink[0, :A_Q_HEADS])


def _coords():
    return lax.axis_index("x"), lax.axis_index("y"), lax.axis_index("c")


def _allgather_small(blk, *, name):
    m_per, n = blk.shape

    def body(x_ref, out_ref, send_sems, recv_sems, local_sem):
        x, y, c = _coords()
        me, sibling = (x, y, c), (x, y, 1 - c)
        chips = [(1 - x, y), (x, 1 - y), (1 - x, 1 - y)]

        def rows(px, py, pc):
            return out_ref.at[pl.ds((4 * px + 2 * py + pc) * m_per, m_per), :]

        def copy(k, block, to, src=None):
            return pltpu.make_async_remote_copy(
                src_ref=rows(*block) if src is None else src, dst_ref=rows(*block),
                send_sem=send_sems.at[k], recv_sem=recv_sems.at[k], device_id=to, device_id_type=MESH)

        mine = pltpu.make_async_copy(x_ref, rows(*me), local_sem)
        mine.start()
        first = [copy(0, me, sibling, src=x_ref)]
        first += [copy(1 + j, me, (*chip, c), src=x_ref) for j, chip in enumerate(chips)]
        for cp in first:
            cp.start()
        passed = [copy(4 + j, (*chip, c), sibling) for j, chip in enumerate(chips)]
        for j, chip in enumerate(chips):
            copy(1 + j, (*chip, c), me).wait_recv()
            passed[j].start()
        copy(0, sibling, me).wait_recv()
        for j, chip in enumerate(chips):
            copy(4 + j, (*chip, 1 - c), me).wait_recv()
        for cp in first + passed:
            cp.wait_send()
        mine.wait()

    return _pcall(
        body, name=name, out_shape=jax.ShapeDtypeStruct((8 * m_per, n), blk.dtype),
        in_specs=[pl.BlockSpec(memory_space=pltpu.VMEM)], out_specs=pl.BlockSpec(memory_space=pltpu.VMEM),
        scratch_shapes=[pltpu.SemaphoreType.DMA((7,)), pltpu.SemaphoreType.DMA((7,)), pltpu.SemaphoreType.DMA],
        compiler_params=pltpu.CompilerParams(vmem_limit_bytes=VMEM_LIMIT_BYTES),
    )(blk)


def _exchange(srcs, dsts, plan, *, name, dst_inits=None):
    na = len(dsts)
    nrem = len(plan(0, 0, 0))

    def body(*refs):
        refs = list(refs)
        src_refs = [refs.pop(0) for _ in range(na)] if srcs is not None else None
        if dst_inits is not None:
            del refs[:na]
        dst_refs, (send_sems, recv_sems) = refs[:na], refs[na:]
        start, wait = _copies(dst_refs if src_refs is None else src_refs, dst_refs, send_sems, recv_sems, plan)
        start()
        wait()

    hbm = pl.BlockSpec(memory_space=pl.ANY)
    ins = (list(srcs) if srcs is not None else []) + (list(dst_inits) if dst_inits is not None else [])
    base = na if srcs is not None else 0
    aliases = {base + a: a for a in range(na)} if dst_inits is not None else {}
    return _pcall(
        body, name=name, out_shape=list(dsts), in_specs=[hbm] * len(ins), out_specs=[hbm] * na,
        input_output_aliases=aliases,
        scratch_shapes=[pltpu.SemaphoreType.DMA((na * nrem,)), pltpu.SemaphoreType.DMA((na * nrem,))],
    )(*ins)


def _other_chips(x, y):
    return [(1 - x, y), (x, 1 - y), (1 - x, 1 - y)]


def _round(ride, carrier, name):
    if carrier is not None:
        _RIDES.setdefault(carrier, []).append(ride)
        return
    srcs = ride.srcs() if callable(ride.srcs) else ride.srcs
    inits = ride.dst_inits() if callable(ride.dst_inits) else ride.dst_inits
    ride.out = list(_exchange(srcs, ride.dsts, ride.plan, name=name, dst_inits=inits))


class _Gather:
    def __init__(self, shards, chip, tag, carriers=(None, None)):
        def plan_ici(x, y, c):
            k = 2 * x + y
            return [((c,), (k, c), (2 * px + py, c), (px, py, c)) for px, py in _other_chips(x, y)]

        def plan_d2d(x, y, c):
            return [((2 * px + py, c), (2 * px + py, c), (2 * px + py, 1 - c), (x, y, 1 - c))
                    for px, py in _other_chips(x, y)]

        self.shards, self.chip = shards, chip
        dsts = [jax.ShapeDtypeStruct((4,) + s.shape, s.dtype) for s in shards]
        ici = _Ride(shards, dsts, plan_ici)
        self.d2d = _Ride(None, dsts, plan_d2d, dst_inits=lambda: ici.out)
        _round(ici, carriers[0], f"gather_{tag}_ici")
        _round(self.d2d, carriers[1], f"gather_{tag}_d2d")

    def result(self):
        full = [lax.dynamic_update_index_in_dim(f, s, self.chip, 0) for f, s in zip(self.d2d.out, self.shards)]
        return [f.reshape((4, 2 * f.shape[2], f.shape[3])) for f in full]


def _index_operand(i):
    return jnp.reshape(i, (1,)).astype(jnp.int32)


def _add_pairs(g, f, ci, *, name):
    s, _, hr, wd = g.shape
    tr = _pick(hr, 600, 16)

    def body(c_ref, a_ref, b_ref, o_ref):
        o_ref[...] = (a_ref[...].astype(F32) + b_ref[...].astype(F32)).astype(BF16)

    spec = pl.BlockSpec((1, tr, wd), lambda j, i, c: (j, i, 0))
    grid_spec = pltpu.PrefetchScalarGridSpec(
        num_scalar_prefetch=1, grid=(s, hr // tr),
        in_specs=[pl.BlockSpec((1, None, tr, wd), lambda j, i, c: (j, c[0], i, 0)), spec], out_specs=spec)
    return _pcall(body, name=name, grid_spec=grid_spec, out_shape=jax.ShapeDtypeStruct(f.shape, BF16),
                  compiler_params=_params(("parallel", "parallel")))(_index_operand(ci), g, f)


def _sum_chips(landed, pairs, chip, *, name):
    s, hr, wd = landed.shape
    tr = _pick(hr, 600, 16)

    def body(k_ref, l_ref, p_ref, o_ref):
        acc = None
        for k in range(s):
            part = jnp.where(k_ref[0] == k, p_ref[k], l_ref[k]).astype(F32)
            acc = part if acc is None else acc + part
        o_ref[...] = acc

    spec = pl.BlockSpec((s, tr, wd), lambda i, k: (0, i, 0))
    grid_spec = pltpu.PrefetchScalarGridSpec(
        num_scalar_prefetch=1, grid=(hr // tr,), in_specs=[spec, spec],
        out_specs=pl.BlockSpec((tr, wd), lambda i, k: (i, 0)))
    return _pcall(body, name=name, grid_spec=grid_spec, out_shape=jax.ShapeDtypeStruct((hr, wd), F32),
                  compiler_params=_params(("parallel",)))(_index_operand(chip), landed, pairs)


class _ReduceScatter:
    def __init__(self, gs, chip, ci, tag):
        self.gs, self.chip, self.ci, self.tag = gs, chip, ci, tag
        self.half_t = [jax.ShapeDtypeStruct((g.shape[0],) + g.shape[2:], BF16) for g in gs]

    def pair(self, carrier=None):
        plan = lambda x, y, c: [((slice(None), 1 - c), (), (), (x, y, 1 - c))]
        self.r1 = _Ride(self.gs, self.half_t, plan)
        _round(self.r1, carrier, f"reduce_{self.tag}_pair")

    def chips(self, carrier=None):
        def plan(x, y, c):
            k = 2 * x + y
            return [((2 * px + py,), (k,), (2 * px + py,), (px, py, c)) for px, py in _other_chips(x, y)]

        self.pairs = [_add_pairs(g, f, self.ci, name=f"reduce_{self.tag}_pair_add{n}")
                      for n, (g, f) in enumerate(zip(self.gs, self.r1.out))]
        self.r2 = _Ride(self.pairs, self.half_t, plan)
        _round(self.r2, carrier, f"reduce_{self.tag}_chips")

    def halves(self, carrier=None):
        plan = lambda x, y, c: [((), (c,), (1 - c,), (x, y, 1 - c))]
        self.mine = [_sum_chips(l, p, self.chip, name=f"reduce_{self.tag}_chip_sum{n}")
                     for n, (l, p) in enumerate(zip(self.r2.out, self.pairs))]
        self.r3 = _Ride(self.mine, [jax.ShapeDtypeStruct((2,) + m.shape, F32) for m in self.mine], plan)
        _round(self.r3, carrier, f"reduce_{self.tag}_halves")

    def result(self):
        return [lax.dynamic_update_index_in_dim(b, m, self.ci, 0).reshape(2 * m.shape[0], m.shape[1])
                for b, m in zip(self.r3.out, self.mine)]


def _ada_fwd(c_all, w_sh, b_sh, *, name):
    nb, d = c_all.shape
    wcols = w_sh.shape[1]
    tn = _pick(wcols, 512)

    def body(c_ref, w_ref, b_ref, o_ref, a_ref):
        cv = c_ref[...]
        act = cv * jax.nn.sigmoid(cv)
        a_ref[...] = act
        o_ref[...] = jnp.dot(act.astype(BF16), w_ref[...].astype(BF16), preferred_element_type=F32) + b_ref[...]

    return _pcall(
        body, name=name, grid=(wcols // tn,),
        in_specs=[pl.BlockSpec((nb, d), lambda j: (0, 0)), pl.BlockSpec((d, tn), lambda j: (0, j)),
                  pl.BlockSpec((1, tn), lambda j: (0, j))],
        out_specs=[pl.BlockSpec((nb, tn), lambda j: (0, j)), pl.BlockSpec((nb, d), lambda j: (0, 0))],
        out_shape=[jax.ShapeDtypeStruct((nb, wcols), F32), jax.ShapeDtypeStruct((nb, d), F32)],
        compiler_params=_params(("arbitrary",)))(c_all, w_sh, b_sh)


def _sum_devices(g, *, name):
    nd, m, w = g.shape

    def body(g_ref, o_ref):
        acc = g_ref[0]
        for k in range(1, nd):
            acc = acc + g_ref[k]
        o_ref[...] = acc

    return _pcall(body, name=name, out_shape=jax.ShapeDtypeStruct((m, w), F32),
                  compiler_params=pltpu.CompilerParams(vmem_limit_bytes=VMEM_LIMIT_BYTES))(g)


def _adamw(w, g, m, v, *, name):
    rows, cols = w.shape[-2:]
    tr = _pick(rows, max(8, (1 << 18) // cols), 8)
    c1 = 1.0 / (1.0 - ADAM_B1 ** ADAM_STEP)
    c2 = 1.0 / (1.0 - ADAM_B2 ** ADAM_STEP)

    def body(w_ref, g_ref, m_ref, v_ref, d_ref, nm_ref, nv_ref):
        gv = g_ref[...]
        nm = ADAM_B1 * m_ref[...] + (1.0 - ADAM_B1) * gv
        nv = ADAM_B2 * v_ref[...] + (1.0 - ADAM_B2) * (gv * gv)
        d_ref[...] = -ADAM_LR * ((nm * c1) / (jnp.sqrt(nv * c2) + ADAM_EPS) + ADAM_WD * w_ref[...])
        nm_ref[...] = nm
        nv_ref[...] = nv

    gspec = pl.BlockSpec((tr, cols), lambda i: (i, 0))
    spec = pl.BlockSpec((None, tr, cols), lambda i: (0, i, 0)) if w.ndim == 3 else gspec
    shp = jax.ShapeDtypeStruct(w.shape, F32)
    return _pcall(body, name=name, grid=(rows // tr,), in_specs=[spec, gspec, spec, spec], out_specs=[spec] * 3,
                  out_shape=[shp] * 3, compiler_params=_params(("parallel",)))(w, g, m, v)


def _permute_in_rows(wt):
    ngrp = len(B_PATTERNS)
    qb, kb, vb = (wt[A_W + n * QB_W:A_W + (n + 1) * QB_W] for n in range(3))
    parts = [wt[:A_W], jnp.zeros((VAR_W - A_W, wt.shape[1]), wt.dtype)]
    for g in range(ngrp):
        parts += [t[g * GB_W:(g + 1) * GB_W] for t in (qb, kb, vb)]
    return jnp.concatenate(parts + [wt[A_W + 3 * QB_W:]], axis=0)


def _unpermute_in_grads(pieces):
    ga, groups, gg = pieces[0], pieces[1:-1], pieces[-1]
    rows = [ga[:A_W]]
    for n in range(3):
        rows += [gp[n * GB_W:(n + 1) * GB_W] for gp in groups]
    return jnp.concatenate(rows + [gg], axis=0)


def kernel(x, c, positions, w_ada, b_ada, w_in, sinks, w_branch_a, w_branch_b, w_o, ln1_g, ln1_b, w_gate_up, w_down, ln2_g, ln2_b, loss_target, m_w_ada, m_b_ada, m_w_in, m_sinks, m_w_branch_a, m_w_branch_b, m_w_o, m_ln1_g, m_ln1_b, m_w_gate_up, m_w_down, m_ln2_g, m_ln2_b, v_w_ada, v_b_ada, v_w_in, v_sinks, v_w_branch_a, v_w_branch_b, v_w_o, v_ln1_g, v_ln1_b, v_w_gate_up, v_w_down, v_ln2_g, v_ln2_b):
    xi, yi, ci = _coords()
    chip = 2 * xi + yi
    dev = 4 * xi + 2 * yi + ci
    NB, T, D = x.shape
    nchip, ndev = 4, 8
    ada_cols = w_ada.shape[2]

    ra, ro, rd = w_branch_a.shape[1], w_o.shape[1], w_down.shape[1]
    rowsh = jnp.concatenate([w_branch_a[0], w_o[0], w_down[0]], axis=0)
    halves = lambda a: a.reshape(a.shape[:-2] + (2, a.shape[-2] // 2, a.shape[-1]))
    tr = lambda a: jnp.swapaxes(a, -1, -2)
    shards = [halves(w.astype(BF16)) for w in (tr(w_in[0]), rowsh, w_branch_b[0], w_gate_up[0])]
    gin = _Gather(shards[:1], chip, "w_in", carriers=("gather_c", "gather_mod"))

    c_blk = jnp.zeros((8, D), F32).at[:NB].set(c)
    c_all = _allgather_small(c_blk, name="gather_c").reshape(ndev, 8, D)[:, :NB].reshape(ndev * NB, D)
    b_sh = lax.dynamic_slice(b_ada, (0, chip * ada_cols), (1, ada_cols))
    mod_part, c_act = _ada_fwd(c_all, w_ada[0], b_sh, name="ada_fwd")
    mod_g = _allgather_small(mod_part, name="gather_mod").reshape(nchip, 2, ndev * NB, ada_cols)[:, 0]
    mod_all = jnp.transpose(mod_g, (1, 0, 2)).reshape(ndev * NB, nchip * ada_cols)
    mod = lax.dynamic_slice(mod_all, (NB * dev, 0), (NB, nchip * ada_cols))

    (g_in,) = gin.result()
    w_in_f = _permute_in_rows(g_in.reshape(nchip * g_in.shape[1], D))
    mix = _Gather(shards[1:3], chip, "w_mix", carriers=("inproj_qkv", "attn_a_fwd"))
    ffn = _Gather(shards[3:], chip, "w_ffn", carriers=("attn_a_fwd", "attn_b0_fwd"))

    def rest_weights():
        (g_rows, w_b_f), (w_gu_f,) = mix.result(), ffn.result()
        return (g_rows[:, :ra].reshape(nchip * ra, D), w_b_f, g_rows[:, ra:ra + ro].reshape(nchip * ro, D), w_gu_f,
                g_rows[:, ra + ro:].reshape(nchip * rd, D))

    red = {}

    def hook(event, **g):
        if event == "rest_grads":
            gr_rows = jnp.concatenate([g["g_wa"].reshape(nchip, ra, D), g["g_wo"].reshape(nchip, ro, D),
                                       g["g_wd"].reshape(nchip, rd, D)], axis=1)
            red["ffn"] = _ReduceScatter([halves(g["g_wgu"])], chip, ci, "ffn")
            red["mix"] = _ReduceScatter([halves(gr_rows), halves(g["g_wb"])], chip, ci, "mix")
            red["ffn"].pair(carrier="merge_bwd")
            red["mix"].pair(carrier="merge_bwd")
        elif event == "merge_bwd_done":
            red["ffn"].chips(carrier="attn_a_bwd")
            red["mix"].chips(carrier="attn_b0_bwd")
        elif event == "attn_a_bwd_done":
            red["ffn"].halves(carrier="attn_b0_bwd")
        elif event == "attn_b0_bwd_done":
            red["mix"].halves(carrier="attn_b1_bwd")
        elif event == "win_grads":
            gr_in = _unpermute_in_grads(g["g_win"])
            red["w_in"] = _ReduceScatter([halves(gr_in.reshape(nchip, gr_in.shape[0] // nchip, D))], chip, ci, "w_in")
            red["w_in"].pair(carrier="inproj_dx0")
        elif event == "inproj_dx0_done":
            red["w_in"].chips(carrier="inproj_dx1")
        elif event == "inproj_dx1_done":
            red["w_in"].halves(carrier="gather_small")

    res = _local_step(x, mod, positions, w_in_f, rest_weights, sinks[0], ln1_g, ln1_b, ln2_g, ln2_b, loss_target, hook)
    (g_rows_red, g_w_b), (g_w_gu,) = red["mix"].result(), red["ffn"].result()
    g_w_a, g_w_o, g_w_d = g_rows_red[:ra], g_rows_red[ra:ra + ro], g_rows_red[ra + ro:]

    small_rows = 24
    misc = jnp.zeros((1, D), F32).at[0, :A_Q_HEADS].set(res["dsink"]).at[0, A_Q_HEADS].set(jnp.sum(res["loss"]))
    small = jnp.concatenate([res["dmod"].reshape(NB * 6, D), jnp.sum(res["ln_grads"], axis=0), misc,
                             jnp.zeros((small_rows - NB * 6 - 5, D), F32)], axis=0)
    small_all = _allgather_small(small, name="gather_small").reshape(ndev, small_rows, D)
    (g_w_in,) = red["w_in"].result()
    dmod_all = small_all[:, :NB * 6].reshape(ndev * NB, 6 * D)
    sums = _sum_devices(small_all, name="sum_small")
    g_b_ada = (sums[0:6] + sums[6:12]).reshape(1, 6 * D)
    g_ln1_g, g_ln1_b, g_ln2_g, g_ln2_b = (sums[12 + n][None] for n in range(4))
    g_sinks = sums[16, :A_Q_HEADS][None]
    loss = sums[16, A_Q_HEADS]
    dmod_sh = lax.dynamic_slice(dmod_all, (0, chip * ada_cols), (ndev * NB, ada_cols))
    g_w_ada = _mm(c_act, dmod_sh, ta=True, name="ada_dw")

    names = ["w_ada", "b_ada", "w_in", "sinks", "w_branch_a", "w_branch_b", "w_o", "ln1_g", "ln1_b",
             "w_gate_up", "w_down", "ln2_g", "ln2_b"]
    ws = [w_ada, b_ada, w_in, sinks, w_branch_a, w_branch_b, w_o, ln1_g, ln1_b, w_gate_up, w_down, ln2_g, ln2_b]
    ms = [m_w_ada, m_b_ada, m_w_in, m_sinks, m_w_branch_a, m_w_branch_b, m_w_o, m_ln1_g, m_ln1_b, m_w_gate_up,
          m_w_down, m_ln2_g, m_ln2_b]
    vs = [v_w_ada, v_b_ada, v_w_in, v_sinks, v_w_branch_a, v_w_branch_b, v_w_o, v_ln1_g, v_ln1_b, v_w_gate_up,
          v_w_down, v_ln2_g, v_ln2_b]
    gs = [g_w_ada, g_b_ada, g_w_in, g_sinks, g_w_a, g_w_b, g_w_o, g_ln1_g, g_ln1_b, g_w_gu, g_w_d, g_ln2_g, g_ln2_b]
    grads, deltas, new_ms, new_vs = [], [], [], []
    for name, w, g, m, v in zip(names, ws, gs, ms, vs):
        flip = tr if name == "w_in" else (lambda a: a)
        w, m, v = flip(w), flip(m), flip(v)
        g2 = g.reshape(w.shape[-2:])
        d, nm, nv = _adamw(w, g2, m, v, name="adamw_" + name)
        grads.append(flip(g2.reshape(w.shape)))
        deltas.append(flip(d))
        new_ms.append(flip(nm))
        new_vs.append(flip(nv))
    return (loss, res["grad_x"], *grads, *deltas, *new_ms, *new_vs)
```

```python
import functools

import jax
import jax.numpy as jnp
from jax import lax
from jax.experimental import pallas as pl
from jax.experimental.pallas import tpu as pltpu

F32 = jnp.float32
BF16 = jnp.bfloat16
MESH = pl.DeviceIdType.MESH

HEAD_DIM = 64
LANES = 128
PAIR_W = 2 * HEAD_DIM
BLOCK = 128
A_Q_HEADS = 16
A_KV_HEADS = 2
A_WINDOW = 128
B_PATTERNS = ((128, 1), (512, 4), (2048, 16))
B_GROUP_HEADS = 8
QA_W = A_Q_HEADS * HEAD_DIM
KA_W = A_KV_HEADS * HEAD_DIM
GB_W = B_GROUP_HEADS * HEAD_DIM
QB_W = GB_W * len(B_PATTERNS)
A_W = QA_W + 2 * KA_W
VAR_W = 3 * GB_W
N_VAR = 1 + len(B_PATTERNS)
VAR_DIL = (1,) + tuple(r for _, r in B_PATTERNS)
QKV_P = N_VAR * VAR_W
ROPE_THETA = 10000.0
LN_EPS = 1e-5
NEG_INF = -1e30
DEPTH = 1
ALPHA = (2 * DEPTH) ** 0.25
SCALE = HEAD_DIM ** -0.5

ADAM_LR, ADAM_B1, ADAM_B2, ADAM_EPS, ADAM_WD, ADAM_STEP = 0.001, 0.9, 0.999, 1e-08, 0.01, 10

VMEM_LIMIT_BYTES = 56 * 1024 * 1024
MM_TILE_BYTES = 36 * 1024 * 1024
MM_WHOLE_K = 4096


def _params(sem=None):
    return pltpu.CompilerParams(dimension_semantics=sem, vmem_limit_bytes=VMEM_LIMIT_BYTES)


_RIDES = {}


def _pcall(body, *, name, **kw):
    rides = _RIDES.pop(name, None)
    if rides is None:
        return pl.pallas_call(body, name=name, **kw)
    return _riding_call(body, rides, name=name, **kw)


def _copies(src_refs, dst_refs, send_sems, recv_sems, plan):
    x, y, c = lax.axis_index("x"), lax.axis_index("y"), lax.axis_index("c")
    remote = plan(x, y, c)
    nrem = len(remote)
    at = lambda ref, idx: ref.at[idx] if idx else ref

    def copy(a, n, landing):
        si, di, ri, peer = remote[n]
        return pltpu.make_async_remote_copy(
            src_ref=at(src_refs[a], si), dst_ref=at(dst_refs[a], ri if landing else di),
            send_sem=send_sems.at[a * nrem + n], recv_sem=recv_sems.at[a * nrem + n],
            device_id=peer, device_id_type=MESH)

    order = [(a, n) for a in range(len(dst_refs)) for n in range(nrem)]

    def start():
        for a, n in order:
            copy(a, n, False).start()

    def wait():
        for a, n in order:
            copy(a, n, True).wait_recv()
        for a, n in order:
            copy(a, n, False).wait_send()

    return start, wait


class _Ride:
    def __init__(self, srcs, dsts, plan, dst_inits=None):
        self.srcs, self.dsts, self.plan, self.dst_inits, self.out = srcs, dsts, plan, dst_inits, None


def _riding_call(body, rides, *, name, in_specs, out_specs, out_shape, grid=(), scratch_shapes=(), **kw):
    single = not isinstance(out_specs, (list, tuple))
    out_specs = [out_specs] if single else list(out_specs)
    out_shape = [out_shape] if single else list(out_shape)
    n_in, n_out, n_scr = len(in_specs), len(out_specs), len(scratch_shapes)
    xin, xdsts, sems, aliases, layout = [], [], [], {}, []
    for ride in rides:
        srcs = ride.srcs() if callable(ride.srcs) else ride.srcs
        inits = ride.dst_inits() if callable(ride.dst_inits) else ride.dst_inits
        na, nrem = len(ride.dsts), len(ride.plan(0, 0, 0))
        src_at = len(xin) if srcs is not None else None
        xin += list(srcs) if srcs is not None else []
        if inits is not None:
            aliases.update({n_in + len(xin) + a: n_out + len(xdsts) + a for a in range(na)})
            xin += list(inits)
        layout.append((src_at, len(xdsts), na))
        xdsts += list(ride.dsts)
        sems += [pltpu.SemaphoreType.DMA((na * nrem,)), pltpu.SemaphoreType.DMA((na * nrem,))]

    def wrapped(*refs):
        ins, xins = refs[:n_in], refs[n_in:n_in + len(xin)]
        outs = refs[n_in + len(xin):n_in + len(xin) + n_out]
        xouts = refs[n_in + len(xin) + n_out:n_in + len(xin) + n_out + len(xdsts)]
        scr = refs[n_in + len(xin) + n_out + len(xdsts):]
        rounds = []
        for k, (ride, (src_at, dst_at, na)) in enumerate(zip(rides, layout)):
            dsts = xouts[dst_at:dst_at + na]
            srcs = dsts if src_at is None else xins[src_at:src_at + na]
            rounds.append(_copies(srcs, dsts, scr[n_scr + 2 * k], scr[n_scr + 2 * k + 1], ride.plan))
        ids = [pl.program_id(a) for a in range(len(grid))]
        first = functools.reduce(jnp.logical_and, [i == 0 for i in ids], True)
        last = functools.reduce(jnp.logical_and, [i == g - 1 for i, g in zip(ids, grid)], True)

        def start_all():
            for start, _ in rounds:
                start()

        def wait_all():
            for _, wait in rounds:
                wait()

        start_all() if not grid else pl.when(first)(start_all)
        body(*ins, *outs, *scr[:n_scr])
        wait_all() if not grid else pl.when(last)(wait_all)

    hbm = pl.BlockSpec(memory_space=pl.ANY)
    gridkw = dict(grid=grid) if grid else {}

    def run(*args):
        res = pl.pallas_call(
            wrapped, name=name, in_specs=list(in_specs) + [hbm] * len(xin),
            out_specs=out_specs + [hbm] * len(xdsts), out_shape=out_shape + xdsts,
            scratch_shapes=list(scratch_shapes) + sems, input_output_aliases=aliases,
            compiler_params=_params(("arbitrary",) * len(grid) if grid else None), **gridkw,
        )(*args, *xin)
        for ride, (_, dst_at, na) in zip(rides, layout):
            ride.out = list(res[n_out + dst_at:n_out + dst_at + na])
        return res[0] if single else list(res[:n_out])

    return run


def _pick(n, target, quantum=128):
    t = (min(target, n) // quantum) * quantum
    while t >= quantum:
        if n % t == 0:
            return t
        t -= quantum
    return n


def _mm(a, b, *, name, ta=False, tb=False, b3=False, out3=0, out_dtype=F32, add=None, tm=1024, tn=1536, tk=1536):
    if ta:
        K, M = a.shape
    else:
        M, K = a.shape
    if b3 and tb:
        Nn, K2, tk = b.shape[1], b.shape[0] * b.shape[2], b.shape[2]
    elif b3:
        K2, Nn, tn = b.shape[1], b.shape[0] * b.shape[2], b.shape[2]
    elif tb:
        Nn, K2 = b.shape
    else:
        K2, Nn = b.shape
    assert K == K2, (a.shape, b.shape)
    if out3:
        tn = Nn // out3
    tm, tn, tk = _pick(M, tm), _pick(Nn, tn), _pick(K, tk)
    if not (b3 and tb) and K <= MM_WHOLE_K:
        tk = K
        fits = lambda: 4 * tk * (tm + tn) + 8 * tm * tn * (2 if add is not None else 1) <= MM_TILE_BYTES
        while not fits():
            if (tm >= tn or b3 or out3) and tm > 256:
                tm = _pick(M, tm - 128)
            elif not (b3 or out3) and tn > 256:
                tn = _pick(Nn, tn - 128)
            else:
                break
    nk = K // tk
    j_outer = K * Nn + (Nn // tn) * M * K < M * K + (M // tm) * K * Nn
    dn = (((0 if ta else 1,), (1 if tb else 0,)), ((), ()))

    def body(*refs):
        refs = list(refs)
        a_ref, b_ref = refs[:2]
        add_ref = refs[2] if add is not None else None
        o_ref = refs[3] if add is not None else refs[2]
        part = lax.dot_general(a_ref[...].astype(BF16), b_ref[...].astype(BF16), dn, preferred_element_type=F32)

        def finish(r):
            if add is not None:
                r = r + add_ref[...]
            o_ref[...] = r.astype(out_dtype)

        if nk == 1:
            finish(part)
            return
        acc = refs[-1]
        k = pl.program_id(2)

        @pl.when(k == 0)
        def _():
            acc[...] = part

        @pl.when(k > 0)
        def _():
            acc[...] += part

        @pl.when(k == nk - 1)
        def _():
            finish(acc[...])

    def spec(shape, index):
        return pl.BlockSpec(shape, (lambda j, i, k: index(i, j, k)) if j_outer else index)

    a_spec = spec((tk, tm), lambda i, j, k: (k, i)) if ta else spec((tm, tk), lambda i, j, k: (i, k))
    if b3 and tb:
        b_spec = spec((None, tn, tk), lambda i, j, k: (k, j, 0))
    elif b3:
        b_spec = spec((None, tk, tn), lambda i, j, k: (j, k, 0))
    elif tb:
        b_spec = spec((tn, tk), lambda i, j, k: (j, k))
    else:
        b_spec = spec((tk, tn), lambda i, j, k: (k, j))
    if out3:
        o_spec = spec((None, tm, tn), lambda i, j, k: (j, i, 0))
    else:
        o_spec = spec((tm, tn), lambda i, j, k: (i, j))
    ins, specs = [a, b], [a_spec, b_spec]
    if add is not None:
        ins.append(add)
        specs.append(o_spec)
    grid = (Nn // tn, M // tm, nk) if j_outer else (M // tm, Nn // tn, nk)
    return _pcall(
        body, name=name, grid=grid, in_specs=specs, out_specs=o_spec,
        out_shape=jax.ShapeDtypeStruct((out3, M, tn) if out3 else (M, Nn), out_dtype),
        scratch_shapes=[pltpu.VMEM((tm, tn), F32)] if nk > 1 else [],
        compiler_params=_params(("parallel", "parallel", "arbitrary")),
    )(*ins)


def _mm_multi(a_list, b_list, *, name, M, T=None, add=None, out_dtype=F32, tm=512, post=None):
    tm = _pick(T or M, tm)
    ns = len(a_list)
    dils = [a[1] if isinstance(a, tuple) else 0 for a in a_list]
    a_arrs = [a[0] if isinstance(a, tuple) else a for a in a_list]
    widths = [a.shape[-1] // max(r, 1) for a, r in zip(a_arrs, dils)]
    b_arrs, b_specs = [], []
    for b in b_list:
        arr, shp, idx = b if isinstance(b, tuple) else (b, b.shape, (0, 0))
        b_arrs.append(arr)
        b_specs.append(pl.BlockSpec(shp, lambda i, idx=idx: idx))
    Nn = b_specs[0].block_shape[1]
    dn = (((1,), (0,)), ((), ()))
    nmm = 2 * ns + (1 if add is not None else 0)
    p_arrs, p_in_specs, p_out_specs, p_out_shape, p_fn = post or ([], [], None, None, None)
    nin = nmm + len(p_arrs)
    nout = len(p_out_specs) if post else 1

    def body(*refs):
        a_refs, b_refs, scr = refs[:ns], refs[ns:2 * ns], list(refs[nin + nout:])
        acc = None
        for a_ref, b_ref, r in zip(a_refs, b_refs, dils):
            av = _from_view(a_ref, scr.pop(0), r) if r > 1 else a_ref[...]
            part = lax.dot_general(av.astype(BF16), b_ref[...], dn, preferred_element_type=F32)
            acc = part if acc is None else acc + part
        if add is not None:
            acc = acc + refs[2 * ns][...]
        if post:
            p_fn(acc, refs[nmm:nin], refs[nin:nin + nout])
        else:
            refs[nin][...] = acc.astype(out_dtype)

    tpe = (T or M) // tm
    a_specs = [pl.BlockSpec((None, tm // r, r * w), lambda i: (i // tpe, i % tpe, 0)) if r
               else pl.BlockSpec((tm, w), lambda i: (i, 0)) for r, w in zip(dils, widths)]
    o_spec = pl.BlockSpec((tm, Nn), lambda i: (i, 0))
    specs = a_specs + b_specs
    ins = a_arrs + b_arrs
    if add is not None:
        specs.append(o_spec)
        ins.append(add)
    scratch = [pltpu.VMEM((w // LANES, tm, LANES), F32) for r, w in zip(dils, widths) if r > 1]
    return _pcall(body, name=name, grid=(M // tm,), in_specs=specs + list(p_in_specs),
                  out_specs=list(p_out_specs) if post else o_spec, scratch_shapes=scratch,
                  out_shape=list(p_out_shape) if post else jax.ShapeDtypeStruct((M, Nn), out_dtype),
                  compiler_params=_params(("arbitrary",) if post else ("parallel",)))(*ins, *p_arrs)


def _dw_view(d3, u, r, *, name, tk=1024):
    NB, tsub, rw = d3.shape
    W, T, D = rw // r, tsub * r, u.shape[1]
    tk = _pick(T, tk)
    tpe, nk = T // tk, NB * T // tk

    def body(d_ref, u_ref, o_ref, acc, scr):
        k = pl.program_id(0)
        dv = _from_view(d_ref, scr, r).astype(BF16)
        part = lax.dot_general(dv, u_ref[...], _TN, preferred_element_type=F32)

        @pl.when(k == 0)
        def _():
            acc[...] = part

        @pl.when(k > 0)
        def _():
            acc[...] += part

        @pl.when(k == nk - 1)
        def _():
            o_ref[...] = acc[...].astype(o_ref.dtype)

    return _pcall(
        body, name=name, grid=(nk,),
        in_specs=[pl.BlockSpec((None, tk // r, rw), lambda k: (k // tpe, k % tpe, 0)), pl.BlockSpec((tk, D), lambda k: (k, 0))],
        out_specs=pl.BlockSpec((W, D), lambda k: (0, 0)), out_shape=jax.ShapeDtypeStruct((W, D), BF16),
        scratch_shapes=[pltpu.VMEM((W, D), F32), pltpu.VMEM((W // LANES, tk, LANES), F32)],
        compiler_params=_params(("arbitrary",)))(d3, u)


def _lane(shape):
    return lax.broadcasted_iota(jnp.int32, shape, len(shape) - 1)


def _rot_half(v):
    w = v.shape[-1]
    first = (_lane(v.shape) % HEAD_DIM) < (HEAD_DIM // 2)
    return jnp.where(first, pltpu.roll(v, w - HEAD_DIM // 2, v.ndim - 1), pltpu.roll(v, HEAD_DIM // 2, v.ndim - 1))


def _widen(t, w):
    return t if w == t.shape[-1] else jnp.concatenate([t] * (w // t.shape[-1]), axis=-1)


def _unrope(v, cos, sins):
    w = v.shape[-1]
    return v * _widen(cos, w) - _rot_half(v) * _widen(sins, w)


def _rope_tables(positions):
    half = HEAD_DIM // 2
    inv = ROPE_THETA ** (-jnp.arange(half, dtype=F32) / half)
    ang = positions.astype(F32)[..., None] * inv
    cos, sin = jnp.cos(ang), jnp.sin(ang)
    cosf = jnp.concatenate([cos, cos, cos, cos], axis=-1)
    sins = jnp.concatenate([-sin, sin, -sin, sin], axis=-1)
    n = positions.shape[0] * positions.shape[1]
    return cosf.reshape(n, PAIR_W), sins.reshape(n, PAIR_W)


def _inproj(x2, scale, shift, w, cosf, sins, flags, *, T, name):
    N, D = x2.shape
    tm, tn = _pick(T, 512), VAR_W
    tpe = T // tm

    def body(x_ref, sc_ref, sh_ref, w_ref, c_ref, s_ref, f_ref, *outs):
        o_refs, u_ref = outs[:N_VAR], outs[N_VAR]
        j = pl.program_id(1)

        @pl.when(j == 0)
        def _():
            u_ref[...] = (x_ref[...] * (1.0 + sc_ref[0]) + sh_ref[0]).astype(BF16)

        acc = lax.dot_general(u_ref[...], w_ref[...], (((1,), (1,)), ((), ())), preferred_element_type=F32)
        fl = f_ref[...]
        ce = 1.0 + (_widen(c_ref[...], tn) - 1.0) * fl
        se = _widen(s_ref[...], tn) * fl
        res = acc * ce + _rot_half(acc) * se
        for v in range(N_VAR):
            @pl.when(j == v)
            def _(v=v):
                _to_view(res, o_refs[v], outs[N_VAR + 1], VAR_DIL[v])

    ex = pl.BlockSpec((1, 1, D), lambda i, j: (i // tpe, 0, 0))
    tab = pl.BlockSpec((tm, PAIR_W), lambda i, j: (i, 0))
    keep = lambda w_: pl.BlockSpec((tm, w_), lambda i, j: (i, 0))
    vspec = lambda r: pl.BlockSpec((None, tm // r, r * tn), lambda i, j: (i // tpe, i % tpe, 0))
    vshape = lambda r: jax.ShapeDtypeStruct((N // T, T // r, r * tn), BF16)
    return _pcall(
        body, name=name, grid=(N // tm, N_VAR),
        in_specs=[keep(D), ex, ex, pl.BlockSpec((tn, D), lambda i, j: (j, 0)), tab, tab,
                  pl.BlockSpec((1, tn), lambda i, j: (0, j))],
        out_specs=[vspec(r) for r in VAR_DIL] + [keep(D)],
        out_shape=[vshape(r) for r in VAR_DIL] + [jax.ShapeDtypeStruct((N, D), BF16)],
        scratch_shapes=[pltpu.VMEM((tn // LANES, tm, LANES), F32)],
        compiler_params=_params(("parallel", "arbitrary")),
    )(x2, scale, shift, w, cosf, sins, flags)


class _Geom:
    def __init__(self, g):
        if g is None:
            self.r, self.nq, self.n_back, self.sink = 1, A_Q_HEADS, A_WINDOW - 1, True
            self.qw, self.kw = QA_W, KA_W
            self.qidx = lambda j: 0
            self.kidx = lambda j: QA_W // KA_W
            self.vidx = lambda j: QA_W // KA_W + 1
        else:
            window, r = B_PATTERNS[g]
            self.r, self.nq, self.n_back, self.sink = r, B_GROUP_HEADS, window // r, False
            self.qw, self.kw = GB_W, GB_W
            self.qidx = lambda j: 3 * j
            self.kidx = lambda j: 3 * j + 1
            self.vidx = lambda j: 3 * j + 2
        self.ntile = self.qw // PAIR_W


def _stack_heads(t, scale=None):
    first = _lane(t.shape) < HEAD_DIM
    z = jnp.zeros_like(t)
    if scale is not None:
        t = t * jnp.asarray(scale, t.dtype)
    return jnp.concatenate([jnp.where(first, t, z), jnp.where(first, z, t)], axis=0)


def _lse_col(t):
    return jnp.concatenate([t[:, 0:1], t[:, HEAD_DIM:HEAD_DIM + 1]], axis=0)


def _lse_rows(t, width):
    first = _lane(t.shape) < HEAD_DIM
    other = pltpu.roll(t, HEAD_DIM, 1)
    full = jnp.concatenate([jnp.where(first, t, other), jnp.where(first, other, t)], axis=0)
    return _widen(full, width)


def _unstack_heads(v2):
    return jnp.where(_lane((BLOCK, PAIR_W)) < HEAD_DIM, v2[:BLOCK], v2[BLOCK:])


def _dup_head(t, kh):
    tf = t.astype(F32)
    keep = (_lane(t.shape) < HEAD_DIM) if kh == 0 else (_lane(t.shape) >= HEAD_DIM)
    return jnp.where(keep, tf, pltpu.roll(tf, HEAD_DIM, 1)).astype(t.dtype)


def _fold_heads(t):
    return t + pltpu.roll(t, HEAD_DIM, 1)


def _band_mask(rows, i, n_back, single):
    nkeys = BLOCK if single else 2 * BLOCK
    qi = jnp.bitwise_and(lax.broadcasted_iota(jnp.int32, (rows, nkeys), 0), BLOCK - 1)
    ki = lax.broadcasted_iota(jnp.int32, (rows, nkeys), 1)
    if single:
        return qi >= ki
    dist = qi + BLOCK - ki
    return jnp.logical_and(jnp.logical_and(dist >= 0, dist <= n_back), jnp.logical_or(ki >= BLOCK, i > 0))


def _per_block(col, scalars, fn):
    return jnp.concatenate([fn(col[b * BLOCK:(b + 1) * BLOCK], sc) for b, sc in enumerate(scalars)], axis=0)


def _sink_slot(rows):
    qi = jnp.bitwise_and(lax.broadcasted_iota(jnp.int32, (rows, 2 * BLOCK), 0), BLOCK - 1)
    return qi == lax.broadcasted_iota(jnp.int32, (rows, 2 * BLOCK), 1)


def _sink_scores(rows, sinks):
    blk = lax.broadcasted_iota(jnp.int32, (rows, 2 * BLOCK), 0) // BLOCK
    out = jnp.full((rows, 2 * BLOCK), sinks[-1], F32)
    for b in range(len(sinks) - 2, -1, -1):
        out = jnp.where(blk == b, sinks[b], out)
    return out


def _softmax_parts(s, valid, sinks):
    s = jnp.where(valid, s, NEG_INF)
    if sinks is not None:
        slot = _sink_slot(s.shape[0])
        s = jnp.where(slot, _sink_scores(s.shape[0], sinks), s)
    m = jnp.max(s, axis=1, keepdims=True)
    p = jnp.exp(s - m)
    den = jnp.sum(p, axis=1, keepdims=True)
    if sinks is not None:
        p = jnp.where(slot, 0.0, p)
    return p, m, den


_NT = (((1,), (1,)), ((), ()))
_TN = (((0,), (0,)), ((), ()))


def _rows2(prev_ref, cur_ref, cs, single=False):
    if single:
        return cur_ref[0, :, cs]
    return jnp.concatenate([prev_ref[0, :, cs], cur_ref[0, :, cs]], axis=0)


def _sink_scalars(sink_ref, first, nblocks):
    return [sink_ref[first + b] for b in range(nblocks)]


def _tile(t):
    return slice(t * PAIR_W, (t + 1) * PAIR_W)


def _attn_fwd(qkv, sinks, g, *, NB, T, name):
    geo = _Geom(g)
    r, qw, kw, ntile = geo.r, geo.qw, geo.kw, geo.ntile
    tsub = T // r
    nblk = tsub // BLOCK
    qkv3 = qkv.reshape(NB, tsub, r * VAR_W)
    out_dtype = BF16 if g is None else F32
    tiles_per_kv = ntile // A_KV_HEADS

    single = nblk == 1

    def body(q_ref, kp_ref, kc_ref, vp_ref, vc_ref, sink_ref, o_ref, l_ref):
        i = pl.program_id(2)
        if geo.sink:
            kall, vall = _rows2(kp_ref, kc_ref, _tile(0)), _rows2(vp_ref, vc_ref, _tile(0))
            kdup = [_dup_head(kall, kh) for kh in range(A_KV_HEADS)]
            vdup = [_dup_head(vall, kh) for kh in range(A_KV_HEADS)]
            tiles = [[t] for t in range(ntile)]
            q2s = [_stack_heads(q_ref[0, :, _tile(t)], SCALE) for t in range(ntile)]
            kks = [kdup[t // tiles_per_kv] for t in range(ntile)]
            vvs = [vdup[t // tiles_per_kv] for t in range(ntile)]
            sinkcols = [_sink_scalars(sink_ref, 2 * t, 2) for t in range(ntile)]
        else:
            tiles = [[t] for t in range(ntile)]
            q2s = [_stack_heads(q_ref[0, :, _tile(t)], SCALE) for t in range(ntile)]
            kks = [_rows2(kp_ref, kc_ref, _tile(t), single) for t in range(ntile)]
            vvs = [_rows2(vp_ref, vc_ref, _tile(t), single) for t in range(ntile)]
            sinkcols = [None] * ntile
        valid = _band_mask(q2s[0].shape[0], i, geo.n_back, single)
        ss = [lax.dot_general(q2, kk, _NT, preferred_element_type=F32) for q2, kk in zip(q2s, kks)]
        parts = [_softmax_parts(s, valid, sc) for s, sc in zip(ss, sinkcols)]
        o2s = [jnp.dot(p.astype(BF16), vv, preferred_element_type=F32) / den for (p, m, den), vv in zip(parts, vvs)]
        for ts, o2, (p, m, den) in zip(tiles, o2s, parts):
            lse2 = jnp.broadcast_to(m + jnp.log(den), (o2.shape[0], PAIR_W))
            for n, t in enumerate(ts):
                rows = slice(2 * BLOCK * n, 2 * BLOCK * (n + 1))
                o_ref[0, :, _tile(t)] = _unstack_heads(o2[rows]).astype(out_dtype)
                l_ref[0, :, _tile(t)] = _unstack_heads(lse2[rows])

    prev = lambda i: jnp.maximum(i - 1, 0)
    in_specs = [
        pl.BlockSpec((1, BLOCK, qw), lambda b, j, i: (b, i, geo.qidx(j))),
        pl.BlockSpec((1, BLOCK, kw), lambda b, j, i: (b, prev(i), geo.kidx(j))),
        pl.BlockSpec((1, BLOCK, kw), lambda b, j, i: (b, i, geo.kidx(j))),
        pl.BlockSpec((1, BLOCK, kw), lambda b, j, i: (b, prev(i), geo.vidx(j))),
        pl.BlockSpec((1, BLOCK, kw), lambda b, j, i: (b, i, geo.vidx(j))),
        pl.BlockSpec(memory_space=pltpu.SMEM),
    ]
    o_spec = pl.BlockSpec((1, BLOCK, qw), lambda b, j, i: (b, i, j))
    shape = (NB, tsub, r * qw)
    o, lse = _pcall(
        body, name=name, grid=(NB, r, nblk), in_specs=in_specs, out_specs=[o_spec, o_spec],
        out_shape=[jax.ShapeDtypeStruct(shape, out_dtype), jax.ShapeDtypeStruct(shape, F32)],
        compiler_params=_params(("parallel", "parallel", "arbitrary")),
    )(qkv3, qkv3, qkv3, qkv3, qkv3, sinks)
    return o, lse


def _attn_bwd(qkv, do, lse, dlse, cosf, sins, sinks, g, *, NB, T, name):
    geo = _Geom(g)
    r, qw, kw, ntile = geo.r, geo.qw, geo.kw, geo.ntile
    tsub = T // r
    nblk = tsub // BLOCK
    view = lambda a, w: a.reshape(NB, tsub, r * w)
    has_dlse = dlse is not None
    tiles_per_kv = ntile // A_KV_HEADS

    single = nblk == 1
    krows = BLOCK if single else 2 * BLOCK
    nsteps = 1 if single else nblk + 1

    def grads(q2s, kks, vvs, do2s, i, lserows, sinkcols, dlrows):
        nrow = q2s[0].shape[0]
        ki = lax.broadcasted_iota(jnp.int32, (krows, nrow), 0)
        qi = jnp.bitwise_and(lax.broadcasted_iota(jnp.int32, (krows, nrow), 1), BLOCK - 1)
        if single:
            valid = qi >= ki
        else:
            dist = qi + BLOCK - ki
            valid = jnp.logical_and(jnp.logical_and(dist >= 0, dist <= geo.n_back), jnp.logical_or(ki >= BLOCK, i > 0))
        sts = [lax.dot_general(kk, q2, _NT, preferred_element_type=F32) for q2, kk in zip(q2s, kks)]
        dpts = [lax.dot_general(vv, do2, _NT, preferred_element_type=F32) for do2, vv in zip(do2s, vvs)]
        pts, dsts, sks = [], [], []
        for st, dpt, ls, sc, dl in zip(sts, dpts, lserows, sinkcols, dlrows):
            sv = jnp.where(valid, st, NEG_INF)
            if sc is not None:
                slot = ki == qi
                blk = lax.broadcasted_iota(jnp.int32, (krows, nrow), 1) // BLOCK
                sink = jnp.full((krows, nrow), sc[-1], F32)
                for b in range(len(sc) - 2, -1, -1):
                    sink = jnp.where(blk == b, sc[b], sink)
                sv = jnp.where(slot, sink, sv)
                dpt = jnp.where(slot, 0.0, dpt)
            pt = jnp.exp(sv - ls)
            delta = jnp.sum(pt * dpt, axis=0, keepdims=True)
            if dl is not None:
                delta = delta - dl
            dst = pt * (dpt - delta)
            if sc is not None:
                cols = lambda a, b: a[:, b * BLOCK:(b + 1) * BLOCK]
                sks.append([jnp.sum(jnp.where(cols(slot, b), cols(dst, b), 0.0)) for b in range(len(sc))])
                dst, pt = jnp.where(slot, 0.0, dst), jnp.where(slot, 0.0, pt)
            else:
                sks.append(None)
            pts.append(pt.astype(BF16))
            dsts.append(dst.astype(BF16))
        dq2s = [lax.dot_general(dst, kk, _TN, preferred_element_type=F32) * SCALE for dst, kk in zip(dsts, kks)]
        dkks = [jnp.dot(dst, q2, preferred_element_type=F32) for dst, q2 in zip(dsts, q2s)]
        dvvs = [jnp.dot(pt, do2, preferred_element_type=F32) for pt, do2 in zip(pts, do2s)]
        return dq2s, dkks, dvvs, sks

    def stat_row(t):
        tt = t.T
        return jnp.concatenate([tt[0:1, :], tt[HEAD_DIM:HEAD_DIM + 1, :]], axis=1)

    def body(*refs):
        it = iter(refs)
        q_ref, kp_ref, kc_ref, vp_ref, vc_ref, do_ref, l_ref = (next(it) for _ in range(7))
        dl_ref = next(it) if has_dlse else None
        c_ref, s_ref, sink_ref, o_ref, ds_ref, dq_s, dk_s, dv_s, car_q, car_k, car_v = (next(it) for _ in range(11))
        b, j, i = pl.program_id(0), pl.program_id(1), pl.program_id(2)

        @pl.when(jnp.logical_and(b == 0, jnp.logical_and(j == 0, i == 0)))
        def _():
            ds_ref[...] = jnp.zeros_like(ds_ref)

        def compute():
            if geo.sink:
                kall, vall = _rows2(kp_ref, kc_ref, _tile(0)), _rows2(vp_ref, vc_ref, _tile(0))
                nb = 2 * tiles_per_kv
                tiles = [[kh * tiles_per_kv + t for t in range(tiles_per_kv)] for kh in range(A_KV_HEADS)]
                cat = lambda f, ts: jnp.concatenate([f(t) for t in ts], axis=0)
                dq2s, dkks, dvvs, sks = grads(
                    [cat(lambda t: _stack_heads(q_ref[0, :, _tile(t)], SCALE), ts) for ts in tiles],
                    [_dup_head(kall, kh) for kh in range(A_KV_HEADS)],
                    [_dup_head(vall, kh) for kh in range(A_KV_HEADS)],
                    [cat(lambda t: _stack_heads(do_ref[0, :, _tile(t)]), ts) for ts in tiles], i,
                    [jnp.concatenate([stat_row(l_ref[0, :, _tile(t)]) for t in ts], axis=1) for ts in tiles],
                    [_sink_scalars(sink_ref, kh * nb, nb) for kh in range(A_KV_HEADS)], [None] * A_KV_HEADS)
                lane1 = _lane((1, PAIR_W))
                dsink = jnp.zeros((1, PAIR_W), F32)
                for kh, (ts, dq2, sk) in enumerate(zip(tiles, dq2s, sks)):
                    for n, t in enumerate(ts):
                        dq_s[:, _tile(t)] = _unstack_heads(dq2[2 * BLOCK * n:2 * BLOCK * (n + 1)])
                    for bb in range(nb):
                        dsink = dsink + jnp.where(lane1 == kh * nb + bb, sk[bb], 0.0)
                second = _lane((krows, PAIR_W)) >= HEAD_DIM
                dk_s[...] = jnp.where(second, _fold_heads(dkks[1]), _fold_heads(dkks[0]))
                dv_s[...] = jnp.where(second, _fold_heads(dvvs[1]), _fold_heads(dvvs[0]))
                ds_ref[0:1, :] += dsink
            else:
                dq2s, dkks, dvvs, _ = grads(
                    [_stack_heads(q_ref[0, :, _tile(t)], SCALE) for t in range(ntile)],
                    [_rows2(kp_ref, kc_ref, _tile(t), single) for t in range(ntile)],
                    [_rows2(vp_ref, vc_ref, _tile(t), single) for t in range(ntile)],
                    [_stack_heads(do_ref[0, :, _tile(t)]) for t in range(ntile)], i,
                    [stat_row(l_ref[0, :, _tile(t)]) for t in range(ntile)], [None] * ntile,
                    [stat_row(dl_ref[0, :, _tile(t)]) for t in range(ntile)])
                for t in range(ntile):
                    dq_s[:, _tile(t)] = _unstack_heads(dq2s[t])
                    dk_s[0:krows, _tile(t)] = dkks[t]
                    dv_s[0:krows, _tile(t)] = dvvs[t]

        def emit(dq, dk, dv):
            cos, sn = c_ref[0], s_ref[0]
            o_ref[0, :, 0:qw] = _unrope(dq, cos, sn).astype(BF16)
            o_ref[0, :, qw:qw + kw] = _unrope(dk, cos, sn).astype(BF16)
            o_ref[0, :, qw + kw:qw + 2 * kw] = dv.astype(BF16)
            if qw + 2 * kw < VAR_W:
                o_ref[0, :, qw + 2 * kw:VAR_W] = jnp.zeros((BLOCK, VAR_W - qw - 2 * kw), BF16)

        if single:
            compute()
            emit(dq_s[...], dk_s[0:BLOCK, :], dv_s[0:BLOCK, :])
            return

        @pl.when(i == 0)
        def _():
            car_q[...] = jnp.zeros_like(car_q)
            car_k[...] = jnp.zeros_like(car_k)
            car_v[...] = jnp.zeros_like(car_v)

        @pl.when(i == nblk)
        def _():
            dk_s[...] = jnp.zeros_like(dk_s)
            dv_s[...] = jnp.zeros_like(dv_s)

        pl.when(i < nblk)(compute)
        emit(car_q[...], car_k[...] + dk_s[0:BLOCK, :], car_v[...] + dv_s[0:BLOCK, :])
        car_q[...] = dq_s[...]
        car_k[...] = dk_s[BLOCK:2 * BLOCK, :]
        car_v[...] = dv_s[BLOCK:2 * BLOCK, :]

    cur = lambda i: jnp.minimum(i, nblk - 1)
    prv = lambda i: jnp.maximum(jnp.minimum(i, nblk - 1) - 1, 0)
    outb = lambda i: jnp.maximum(i - 1, 0)
    qrow = pl.BlockSpec((1, BLOCK, qw), lambda b, j, i: (b, cur(i), j))
    in_specs = [
        pl.BlockSpec((1, BLOCK, qw), lambda b, j, i: (b, cur(i), geo.qidx(j))),
        pl.BlockSpec((1, BLOCK, kw), lambda b, j, i: (b, prv(i), geo.kidx(j))),
        pl.BlockSpec((1, BLOCK, kw), lambda b, j, i: (b, cur(i), geo.kidx(j))),
        pl.BlockSpec((1, BLOCK, kw), lambda b, j, i: (b, prv(i), geo.vidx(j))),
        pl.BlockSpec((1, BLOCK, kw), lambda b, j, i: (b, cur(i), geo.vidx(j))),
        qrow, qrow,
    ]
    ins = [view(qkv, VAR_W)] * 5 + [view(do, qw), view(lse, qw)]
    if has_dlse:
        in_specs.append(qrow)
        ins.append(view(dlse, qw))
    in_specs += [
        pl.BlockSpec((1, BLOCK, PAIR_W), lambda b, j, i: (b, outb(i), j)),
        pl.BlockSpec((1, BLOCK, PAIR_W), lambda b, j, i: (b, outb(i), j)),
        pl.BlockSpec(memory_space=pltpu.SMEM),
    ]
    ins += [view(cosf, PAIR_W), view(sins, PAIR_W), sinks]
    scratch = [pltpu.VMEM((BLOCK, qw), F32), pltpu.VMEM((2 * BLOCK, kw), F32), pltpu.VMEM((2 * BLOCK, kw), F32),
               pltpu.VMEM((BLOCK, qw), F32), pltpu.VMEM((BLOCK, kw), F32), pltpu.VMEM((BLOCK, kw), F32)]
    dqkv, dsink = _pcall(
        body, name=name, grid=(NB, r, nsteps), in_specs=in_specs,
        out_specs=[pl.BlockSpec((1, BLOCK, VAR_W), lambda b, j, i: (b, outb(i), j)),
                   pl.BlockSpec((8, PAIR_W), lambda b, j, i: (0, 0))],
        out_shape=[jax.ShapeDtypeStruct((NB, tsub, r * VAR_W), BF16), jax.ShapeDtypeStruct((8, PAIR_W), F32)],
        scratch_shapes=scratch, compiler_params=_params(("arbitrary", "arbitrary", "arbitrary")),
    )(*ins)
    return dqkv, dsink


class _Rows:
    def __init__(self, N, T, tm):
        self.N, self.tm, self.tpe, self.grid = N, tm, T // tm, (N // tm,)

    def row(self, w, col=0):
        return pl.BlockSpec((self.tm, w), lambda i: (i, col))

    def ex(self, w):
        return pl.BlockSpec((1, 1, w), lambda i: (i // self.tpe, 0, 0))

    def const(self, shape):
        return pl.BlockSpec(shape, lambda i: tuple(0 for _ in shape))

    def view(self, w, r):
        return pl.BlockSpec((None, self.tm // r, r * w), lambda i: (i // self.tpe, i % self.tpe, 0))

    def first_of_example(self):
        return pl.program_id(0) % self.tpe == 0


def _acc(ref, first, val):
    @pl.when(first)
    def _():
        ref[0] = val

    @pl.when(jnp.logical_not(first))
    def _():
        ref[0] += val


def _colsum(v):
    return jnp.sum(v, axis=0, keepdims=True)


def _ln_stats(r):
    mu = jnp.mean(r, axis=-1, keepdims=True)
    xc = r - mu
    var = jnp.mean(xc * xc, axis=-1, keepdims=True)
    rstd = lax.rsqrt(var + LN_EPS)
    return xc * rstd, rstd


def _ln_bwd(dy, xhat, rstd, gain):
    dxh = dy * gain
    return rstd * (dxh - jnp.mean(dxh, axis=-1, keepdims=True) - xhat * jnp.mean(dxh * xhat, axis=-1, keepdims=True))


def _from_view(ref, scr, r):
    if r == 1:
        return ref[...]
    rows, w = ref.shape[0], ref.shape[1] // r
    for j in range(r):
        for c in range(w // LANES):
            scr.at[c][pl.ds(j, rows, stride=r), :] = ref[:, j * w + c * LANES:j * w + (c + 1) * LANES].astype(F32)
    return jnp.concatenate([scr[c] for c in range(w // LANES)], axis=1)


def _to_view(val, ref, scr, r):
    if r == 1:
        ref[...] = val.astype(ref.dtype)
        return
    rows, w = ref.shape[0], ref.shape[1] // r
    for c in range(w // LANES):
        scr[c] = val[:, c * LANES:(c + 1) * LANES]
    for j in range(r):
        for c in range(w // LANES):
            ref[:, j * w + c * LANES:j * w + (c + 1) * LANES] = scr.at[c][pl.ds(j, rows, stride=r), :].astype(ref.dtype)


def _silu_parts(v):
    s = jax.nn.sigmoid(v)
    return v * s, s * (1.0 + v * (1.0 - s))


def _local_step(x, mod, positions, w_in, rest_weights, sinks, ln1_g, ln1_b, ln2_g, ln2_b, target, hook=None):
    hook = hook or (lambda event, **data: None)
    NB, T, D = x.shape
    N = NB * T
    x2 = x.reshape(N, D)
    tgt2 = target.reshape(N, D)
    shift_m, scale_m, gate_m, shift_f, scale_f, gate_f = [mod[:, None, k * D:(k + 1) * D] for k in range(6)]
    cosf, sins = _rope_tables(positions)
    col = jnp.arange(QKV_P)
    vcol = col % VAR_W
    flags = jnp.where(col < VAR_W, vcol < QA_W + KA_W, vcol < 2 * GB_W).astype(F32)[None]
    R = _Rows(N, T, _pick(T, 256))
    sds = jax.ShapeDtypeStruct
    exsum = lambda w=D: sds((NB, 1, w), F32)
    ngrp = len(B_PATTERNS)

    *qkv, u = _inproj(x2, scale_m, shift_m, w_in, cosf, sins, flags, T=T, name="inproj_qkv")
    gates = _mm(u, w_in[QKV_P:], tb=True, out_dtype=BF16, name="inproj_gates")
    oa, la = _attn_fwd(qkv[0], sinks, None, NB=NB, T=T, name="attn_a_fwd")
    oa = oa.reshape(N, QA_W)
    ob_parts = [_attn_fwd(qkv[1 + g], sinks, g, NB=NB, T=T, name=f"attn_b{g}_fwd") for g in range(ngrp)]
    (o1, l1), (o2, l2), (o3, l3) = ob_parts
    w_a, w_b, w_o, w_gu, w_d = rest_weights()
    F = w_d.shape[0]
    dil = [r_ for _, r_ in B_PATTERNS]
    views = [R.view(GB_W, r_) for r_ in dil]
    tokbuf = pltpu.VMEM((GB_W // LANES, R.tm, LANES), F32)

    def merge_fwd(o1r, o2r, o3r, l1r, l2r, l3r, ob_ref, *bufs):
        os_ = [_from_view(ref, bufs[n], dil[n]) for n, ref in enumerate((o1r, o2r, o3r))]
        la, lb, lc = [_from_view(ref, bufs[3 + n], dil[n]) for n, ref in enumerate((l1r, l2r, l3r))]
        mx = jnp.maximum(jnp.maximum(la, lb), lc)
        ea, eb, ec = jnp.exp(la - mx), jnp.exp(lb - mx), jnp.exp(lc - mx)
        ob_ref[...] = ((ea * os_[0] + eb * os_[1] + ec * os_[2]) / (ea + eb + ec)).astype(BF16)

    ob = _pcall(merge_fwd, name="merge_fwd", grid=R.grid, in_specs=views + views, out_specs=R.row(GB_W),
                out_shape=sds((N, GB_W), BF16), scratch_shapes=[tokbuf] * 6,
                compiler_params=_params(("parallel",)))(o1, o2, o3, l1, l2, l3)

    f32 = lambda ref: ref[...].astype(F32)
    Rm = _Rows(N, T, _pick(T, 512))

    def mix_out(oa_r, ob_r, ga_r, gb_r, x_r, gm_r, g_r, b_r, sf_r, hf_r, wa_r, wb_r, wo_r,
                ya_ref, yb_ref, mg_ref, y_ref, r1_ref, u2_ref):
        ya = jnp.dot(oa_r[...], wa_r[...], preferred_element_type=F32).astype(BF16)
        yb = jnp.concatenate([jnp.dot(ob_r[...], wb_r[s_], preferred_element_type=F32)
                              for s_ in range(w_b.shape[0])], axis=1).astype(BF16)
        merged = (jax.nn.sigmoid(f32(ga_r)) * ya.astype(F32) + jax.nn.sigmoid(f32(gb_r)) * yb.astype(F32)).astype(BF16)
        y = jnp.dot(merged, wo_r[...], preferred_element_type=F32)
        r1 = ALPHA * x_r[...] + (1.0 + gm_r[0]) * y
        xhat, _ = _ln_stats(r1)
        x1 = xhat * g_r[...] + b_r[...]
        ya_ref[...], yb_ref[...], mg_ref[...], y_ref[...], r1_ref[...] = ya, yb, merged, y, r1
        u2_ref[...] = (x1 * (1.0 + sf_r[0]) + hf_r[0]).astype(BF16)

    ya, yb, merged, y, r1, u2 = _pcall(
        mix_out, name="mix_out", grid=Rm.grid,
        in_specs=[Rm.row(QA_W), Rm.row(GB_W), Rm.row(D, 0), Rm.row(D, 1), Rm.row(D), Rm.ex(D), Rm.const((1, D)),
                  Rm.const((1, D)), Rm.ex(D), Rm.ex(D), Rm.const(w_a.shape), Rm.const(w_b.shape), Rm.const(w_o.shape)],
        out_specs=[Rm.row(D)] * 6,
        out_shape=[sds((N, D), BF16)] * 3 + [sds((N, D), F32)] * 2 + [sds((N, D), BF16)],
        compiler_params=_params(("parallel",)))(oa, ob, gates, gates, x2, gate_m, ln1_g, ln1_b, scale_f, shift_f,
                                                w_a, w_b, w_o)

    tnf = w_gu.shape[2]
    nft = w_gu.shape[0] // 2
    tmf = _pick(N, 512)

    def ffn_up(u_r, wg_r, wu_r, hg_ref, hu_ref, a_ref):
        hg = jnp.dot(u_r[...], wg_r[...], preferred_element_type=F32)
        hu = jnp.dot(u_r[...], wu_r[...], preferred_element_type=F32)
        sl, _ = _silu_parts(hg)
        hg_ref[...] = hg.astype(BF16)
        hu_ref[...] = hu.astype(BF16)
        a_ref[...] = (sl * hu).astype(BF16)

    ftile = pl.BlockSpec((tmf, tnf), lambda j, i: (i, j))
    hg, hu, act = _pcall(
        ffn_up, name="ffn_up", grid=(nft, N // tmf),
        in_specs=[pl.BlockSpec((tmf, D), lambda j, i: (i, 0)), pl.BlockSpec((None, D, tnf), lambda j, i: (j, 0, 0)),
                  pl.BlockSpec((None, D, tnf), lambda j, i: (j + nft, 0, 0))],
        out_specs=[ftile] * 3, out_shape=[sds((N, F), BF16)] * 3,
        compiler_params=_params(("arbitrary", "parallel")))(u2, w_gu, w_gu)
    def ffn_down_norm2(act_r, wd_r, r1_r, g1_r, b1_r, t_r, gf_r, g_r, b_r,
                       dy2_ref, dx1_ref, dgf_ref, dg_ref, db_ref, loss_ref):
        first = R.first_of_example()
        y2v = jnp.dot(act_r[...], wd_r[...], preferred_element_type=F32)
        x1 = _ln_stats(r1_r[...])[0] * g1_r[...] + b1_r[...]
        r2 = ALPHA * x1 + (1.0 + gf_r[0]) * y2v
        xhat, rstd = _ln_stats(r2)
        err = xhat * g_r[...] + b_r[...] - t_r[...]
        dx2 = err * (1.0 / D)
        dr2 = _ln_bwd(dx2, xhat, rstd, g_r[...])
        dy2_ref[...] = ((1.0 + gf_r[0]) * dr2).astype(BF16)
        dx1_ref[...] = ALPHA * dr2
        _acc(dgf_ref, first, _colsum(dr2 * y2v))
        _acc(dg_ref, first, _colsum(dx2 * xhat))
        _acc(db_ref, first, _colsum(dx2))
        part = 0.5 * jnp.sum(jnp.mean(err * err, axis=-1, keepdims=True))
        _acc(loss_ref, first, jnp.broadcast_to(part, (1, 128)))

    dy2, dx1p, dgate_f, dg2, db2, loss_p = _pcall(
        ffn_down_norm2, name="ffn_down_norm2", grid=R.grid,
        in_specs=[R.row(F), R.const((F, D)), R.row(D), R.const((1, D)), R.const((1, D)), R.row(D), R.ex(D),
                  R.const((1, D)), R.const((1, D))],
        out_specs=[R.row(D), R.row(D), R.ex(D), R.ex(D), R.ex(D), R.ex(128)],
        out_shape=[sds((N, D), BF16), sds((N, D), F32), exsum(), exsum(), exsum(), exsum(128)],
        compiler_params=_params(("arbitrary",)))(act, w_d, r1, ln1_g, ln1_b, tgt2, gate_f, ln2_g, ln2_b)

    g_wd = _mm(act, dy2, ta=True, out_dtype=BF16, name="ffn_down_dw")

    tmd = _pick(N, 256)

    fchunk = _pick(F, 768)

    def ffn_down_dx(dy_r, wd_r, hg_r, hu_r, dh_ref):
        for t in range(F // fchunk):
            cs = slice(t * fchunk, (t + 1) * fchunk)
            da = lax.dot_general(dy_r[...], wd_r[cs, :], _NT, preferred_element_type=F32)
            sl, dsl = _silu_parts(hg_r[:, cs].astype(F32))
            dh_ref[:, cs] = (da * hu_r[:, cs].astype(F32) * dsl).astype(BF16)
            dh_ref[:, F + t * fchunk:F + (t + 1) * fchunk] = (da * sl).astype(BF16)

    rowd = lambda w_: pl.BlockSpec((tmd, w_), lambda i: (i, 0))
    dh = _pcall(
        ffn_down_dx, name="ffn_down_dx", grid=(N // tmd,),
        in_specs=[rowd(D), pl.BlockSpec((F, D), lambda i: (0, 0)), rowd(F), rowd(F)],
        out_specs=rowd(2 * F), out_shape=sds((N, 2 * F), BF16),
        compiler_params=_params(("parallel",)))(dy2, w_d, hg, hu)
    g_wgu = _mm(u2, dh, ta=True, out3=w_gu.shape[0], out_dtype=BF16, name="ffn_up_dw")

    def ffn_up_dx_norm1(dh_r, w_r, dx1p_r, r1_r, y_r, sf_r, gm_r, g_r, b_r,
                        dxp_ref, dy_ref, dsf_ref, dhf_ref, dgm_ref, dg_ref, db_ref):
        first = R.first_of_example()
        du2v = None
        for s_ in range(w_gu.shape[0]):
            part = lax.dot_general(dh_r[:, s_ * tnf:(s_ + 1) * tnf], w_r[s_], _NT, preferred_element_type=F32)
            du2v = part if du2v is None else du2v + part
        dx1 = dx1p_r[...] + du2v * (1.0 + sf_r[0])
        xhat, rstd = _ln_stats(r1_r[...])
        dr1 = _ln_bwd(dx1, xhat, rstd, g_r[...])
        dxp_ref[...] = ALPHA * dr1
        dy_ref[...] = ((1.0 + gm_r[0]) * dr1).astype(BF16)
        _acc(dsf_ref, first, _colsum(du2v * (xhat * g_r[...] + b_r[...])))
        _acc(dhf_ref, first, _colsum(du2v))
        _acc(dgm_ref, first, _colsum(dr1 * y_r[...]))
        _acc(dg_ref, first, _colsum(dx1 * xhat))
        _acc(db_ref, first, _colsum(dx1))

    dxp, dy, dscale_f, dshift_f, dgate_m, dg1, db1 = _pcall(
        ffn_up_dx_norm1, name="ffn_up_dx_norm1", grid=R.grid,
        in_specs=[R.row(2 * F), R.const(w_gu.shape)] + [R.row(D)] * 3 + [R.ex(D), R.ex(D), R.const((1, D)),
                                                                        R.const((1, D))],
        out_specs=[R.row(D), R.row(D)] + [R.ex(D)] * 5,
        out_shape=[sds((N, D), F32), sds((N, D), BF16)] + [exsum()] * 5,
        compiler_params=_params(("arbitrary",)))(dh, w_gu, dx1p, r1, y, scale_f, gate_m, ln1_g, ln1_b)

    g_wo = _mm(merged, dy, ta=True, out_dtype=BF16, name="out_proj_dw")

    def mix_out_bwd(dy_r, ya_r, yb_r, ga_r, gb_r, wo_r, wa_r, wb_r, dya_ref, dyb_ref, dg_ref, doa_ref, dob_ref):
        dm = lax.dot_general(dy_r[...], wo_r[...], _NT, preferred_element_type=F32).astype(BF16).astype(F32)
        sa, sb = jax.nn.sigmoid(f32(ga_r)), jax.nn.sigmoid(f32(gb_r))
        dya, dyb = (dm * sa).astype(BF16), (dm * sb).astype(BF16)
        dya_ref[...], dyb_ref[...] = dya, dyb
        dg_ref[:, :D] = (dm * f32(ya_r) * sa * (1.0 - sa)).astype(BF16)
        dg_ref[:, D:] = (dm * f32(yb_r) * sb * (1.0 - sb)).astype(BF16)
        doa_ref[...] = lax.dot_general(dya, wa_r[...], _NT, preferred_element_type=F32).astype(BF16)
        ds_ = D // w_b.shape[0]
        dob = None
        for s_ in range(w_b.shape[0]):
            part = lax.dot_general(dyb[:, s_ * ds_:(s_ + 1) * ds_], wb_r[s_], _NT, preferred_element_type=F32)
            dob = part if dob is None else dob + part
        dob_ref[...] = dob

    dya, dyb, dgates, doa, dob = _pcall(
        mix_out_bwd, name="mix_out_bwd", grid=Rm.grid,
        in_specs=[Rm.row(D)] * 3 + [Rm.row(D, 0), Rm.row(D, 1), Rm.const(w_o.shape), Rm.const(w_a.shape),
                                    Rm.const(w_b.shape)],
        out_specs=[Rm.row(D), Rm.row(D), Rm.row(2 * D), Rm.row(QA_W), Rm.row(GB_W)],
        out_shape=[sds((N, D), BF16), sds((N, D), BF16), sds((N, 2 * D), BF16), sds((N, QA_W), BF16), sds((N, GB_W), F32)],
        compiler_params=_params(("parallel",)))(dy, ya, yb, gates, gates, w_o, w_a, w_b)

    g_wa = _mm(oa, dya, ta=True, out_dtype=BF16, name="branch_a_dw")
    g_wb = _mm(ob, dyb, ta=True, out3=w_b.shape[0], out_dtype=BF16, name="branch_b_dw")
    hook("rest_grads", g_wa=g_wa, g_wb=g_wb, g_wo=g_wo, g_wgu=g_wgu, g_wd=g_wd)

    seg = (jnp.arange(GB_W)[:, None] // HEAD_DIM == jnp.arange(GB_W)[None, :] // HEAD_DIM).astype(BF16)

    def merge_bwd(dob_r, o1r, o2r, o3r, l1r, l2r, l3r, seg_r, d1, d2, d3, e1, e2, e3, *bufs):
        dob_v = dob_r[...]
        os_ = [_from_view(ref, bufs[n], dil[n]) for n, ref in enumerate((o1r, o2r, o3r))]
        la, lb, lc = [_from_view(ref, bufs[3 + n], dil[n]) for n, ref in enumerate((l1r, l2r, l3r))]
        mx = jnp.maximum(jnp.maximum(la, lb), lc)
        ea, eb, ec = jnp.exp(la - mx), jnp.exp(lb - mx), jnp.exp(lc - mx)
        inv = 1.0 / (ea + eb + ec)
        ws = [ea * inv, eb * inv, ec * inv]

        def headsum(v):
            hi = v.astype(BF16)
            r1_ = v - hi.astype(F32)
            mid = r1_.astype(BF16)
            lo = (r1_ - mid.astype(F32)).astype(BF16)
            sm = seg_r[...]
            return (jnp.dot(hi, sm, preferred_element_type=F32) + jnp.dot(mid, sm, preferred_element_type=F32)
                    + jnp.dot(lo, sm, preferred_element_type=F32))

        dws = [headsum(dob_v * o) for o in os_]
        mean = ws[0] * dws[0] + ws[1] * dws[1] + ws[2] * dws[2]
        for n, (w_, dw_, d_ref, e_ref) in enumerate(zip(ws, dws, (d1, d2, d3), (e1, e2, e3))):
            _to_view(w_ * dob_v, d_ref, bufs[6], dil[n])
            _to_view(w_ * (dw_ - mean), e_ref, bufs[7], dil[n])

    vshape = lambda r_, dt: sds((NB, T // r_, r_ * GB_W), dt)
    mb = _pcall(
        merge_bwd, name="merge_bwd", grid=R.grid, in_specs=[R.row(GB_W)] + views + views + [R.const((GB_W, GB_W))],
        out_specs=views + views, out_shape=[vshape(r_, BF16) for r_ in dil] + [vshape(r_, F32) for r_ in dil],
        scratch_shapes=[tokbuf] * 8, compiler_params=_params(("parallel",)))(dob, o1, o2, o3, l1, l2, l3, seg)
    do_b, dlse_b = mb[:3], mb[3:]
    hook("merge_bwd_done")

    dqkv_a, dsink = _attn_bwd(qkv[0], doa, la, None, cosf, sins, sinks, None, NB=NB, T=T, name="attn_a_bwd")
    hook("attn_a_bwd_done")
    dqkv = [dqkv_a]
    for g in range(ngrp):
        dqkv.append(_attn_bwd(qkv[1 + g], do_b[g], (l1, l2, l3)[g], dlse_b[g], cosf, sins, sinks, g, NB=NB, T=T,
                              name=f"attn_b{g}_bwd")[0])
        hook(f"attn_b{g}_bwd_done")

    g_win = [_mm(d3.reshape(N, VAR_W), u, ta=True, out_dtype=BF16, name=f"inproj_dw{v}") if VAR_DIL[v] == 1
             else _dw_view(d3, u, VAR_DIL[v], name=f"inproj_dw{v}") for v, d3 in enumerate(dqkv)]
    g_win.append(_mm(dgates, u, ta=True, out_dtype=BF16, name=f"inproj_dw{N_VAR}"))
    hook("win_grads", g_win=g_win)
    wvar = lambda v: (w_in, (VAR_W, D), (v, 0))
    dview = lambda v: (dqkv[v], VAR_DIL[v])
    du = _mm_multi([dview(0)], [wvar(0)], M=N, T=T, name="inproj_dx0")
    hook("inproj_dx0_done")
    def x_bwd(duv, ins, outs):
        (dxp_r, x_r, sm_r), (gx_ref, dsm_ref, dhm_ref) = ins, outs
        first = R.first_of_example()
        gx_ref[...] = dxp_r[...] + duv * (1.0 + sm_r[0])
        _acc(dsm_ref, first, _colsum(duv * x_r[...]))
        _acc(dhm_ref, first, _colsum(duv))

    gx, dscale_m, dshift_m = _mm_multi(
        [dview(v) for v in range(1, N_VAR)] + [dgates],
        [wvar(v) for v in range(1, N_VAR)] + [(w_in, (2 * D, D), (QKV_P // (2 * D), 0))],
        M=N, T=T, add=du, tm=R.tm, name="inproj_dx1",
        post=([dxp, x2, scale_m], [R.row(D), R.row(D), R.ex(D)], [R.row(D), R.ex(D), R.ex(D)],
              [sds((N, D), F32), exsum(), exsum()], x_bwd))
    hook("inproj_dx1_done")

    dmod =jnp.concatenate([dshift_m, dscale_m, dgate_m, dshift_f, dscale_f, dgate_f], axis=-1)[:, 0]
    ln_grads = jnp.concatenate([dg1, db1, dg2, db2], axis=1)
    return dict(loss=loss_p[:, 0, 0], grad_x=gx.reshape(NB, T, D), g_win=g_win, g_wa=g_wa, g_wb=g_wb, g_wo=g_wo,
                g_wgu=g_wgu, g_wd=g_wd, dmod=dmod, ln_grads=ln_grads, dsink=dsink[0, :A_Q_HEADS])


def _coords():
    return lax.axis_index("x"), lax.axis_index("y"), lax.axis_index("c")


def _allgather_small(blk, *, name):
    m_per, n = blk.shape

    def body(x_ref, out_ref, send_sems, recv_sems, local_sem):
        x, y, c = _coords()
        me, sibling = (x, y, c), (x, y, 1 - c)
        chips = [(1 - x, y), (x, 1 - y), (1 - x, 1 - y)]

        def rows(px, py, pc):
            return out_ref.at[pl.ds((4 * px + 2 * py + pc) * m_per, m_per), :]

        def copy(k, block, to, src=None):
            return pltpu.make_async_remote_copy(
                src_ref=rows(*block) if src is None else src, dst_ref=rows(*block),
                send_sem=send_sems.at[k], recv_sem=recv_sems.at[k], device_id=to, device_id_type=MESH)

        mine = pltpu.make_async_copy(x_ref, rows(*me), local_sem)
        mine.start()
        first = [copy(0, me, sibling, src=x_ref)]
        first += [copy(1 + j, me, (*chip, c), src=x_ref) for j, chip in enumerate(chips)]
        for cp in first:
            cp.start()
        passed = [copy(4 + j, (*chip, c), sibling) for j, chip in enumerate(chips)]
        for j, chip in enumerate(chips):
            copy(1 + j, (*chip, c), me).wait_recv()
            passed[j].start()
        copy(0, sibling, me).wait_recv()
        for j, chip in enumerate(chips):
            copy(4 + j, (*chip, 1 - c), me).wait_recv()
        for cp in first + passed:
            cp.wait_send()
        mine.wait()

    return _pcall(
        body, name=name, out_shape=jax.ShapeDtypeStruct((8 * m_per, n), blk.dtype),
        in_specs=[pl.BlockSpec(memory_space=pltpu.VMEM)], out_specs=pl.BlockSpec(memory_space=pltpu.VMEM),
        scratch_shapes=[pltpu.SemaphoreType.DMA((7,)), pltpu.SemaphoreType.DMA((7,)), pltpu.SemaphoreType.DMA],
        compiler_params=pltpu.CompilerParams(vmem_limit_bytes=VMEM_LIMIT_BYTES),
    )(blk)


def _exchange(srcs, dsts, plan, *, name, dst_inits=None):
    na = len(dsts)
    nrem = len(plan(0, 0, 0))

    def body(*refs):
        refs = list(refs)
        src_refs = [refs.pop(0) for _ in range(na)] if srcs is not None else None
        if dst_inits is not None:
            del refs[:na]
        dst_refs, (send_sems, recv_sems) = refs[:na], refs[na:]
        start, wait = _copies(dst_refs if src_refs is None else src_refs, dst_refs, send_sems, recv_sems, plan)
        start()
        wait()

    hbm = pl.BlockSpec(memory_space=pl.ANY)
    ins = (list(srcs) if srcs is not None else []) + (list(dst_inits) if dst_inits is not None else [])
    base = na if srcs is not None else 0
    aliases = {base + a: a for a in range(na)} if dst_inits is not None else {}
    return _pcall(
        body, name=name, out_shape=list(dsts), in_specs=[hbm] * len(ins), out_specs=[hbm] * na,
        input_output_aliases=aliases,
        scratch_shapes=[pltpu.SemaphoreType.DMA((na * nrem,)), pltpu.SemaphoreType.DMA((na * nrem,))],
    )(*ins)


def _other_chips(x, y):
    return [(1 - x, y), (x, 1 - y), (1 - x, 1 - y)]


def _round(ride, carrier, name):
    if carrier is not None:
        _RIDES.setdefault(carrier, []).append(ride)
        return
    srcs = ride.srcs() if callable(ride.srcs) else ride.srcs
    inits = ride.dst_inits() if callable(ride.dst_inits) else ride.dst_inits
    ride.out = list(_exchange(srcs, ride.dsts, ride.plan, name=name, dst_inits=inits))


class _Gather:
    def __init__(self, shards, chip, tag, carriers=(None, None)):
        def plan_ici(x, y, c):
            k = 2 * x + y
            return [((c,), (k, c), (2 * px + py, c), (px, py, c)) for px, py in _other_chips(x, y)]

        def plan_d2d(x, y, c):
            return [((2 * px + py, c), (2 * px + py, c), (2 * px + py, 1 - c), (x, y, 1 - c))
                    for px, py in _other_chips(x, y)]

        def plan_near(x, y, c):
            k = 2 * x + y
            return [((c,), (k, c), (2 * px + py, c), (px, py, c)) for px, py in ((1 - x, y), (x, 1 - y))]

        def plan_far(x, y, c):
            kx, ky, kd = 2 * (1 - x) + y, 2 * x + (1 - y), 2 * (1 - x) + (1 - y)
            hp = shards[0].shape[1] // 2
            top, bottom = pl.ds(0, hp), pl.ds(hp, hp)
            return [((kx, c, top), (kx, c, top), (kd, c, top), (x, 1 - y, c)),
                    ((ky, c, bottom), (ky, c, bottom), (kd, c, bottom), (1 - x, y, c))]

        self.shards, self.chip = shards, chip
        dsts = [jax.ShapeDtypeStruct((4,) + s.shape, s.dtype) for s in shards]
        if len(carriers) == 3:
            near = _Ride(shards, dsts, plan_near)
            ici = _Ride(None, dsts, plan_far, dst_inits=lambda: near.out)
            _round(near, carriers[0], f"gather_{tag}_near")
            _round(ici, carriers[1], f"gather_{tag}_far")
        else:
            ici = _Ride(shards, dsts, plan_ici)
            _round(ici, carriers[0], f"gather_{tag}_ici")
        self.d2d = _Ride(None, dsts, plan_d2d, dst_inits=lambda: ici.out)
        _round(self.d2d, carriers[-1], f"gather_{tag}_d2d")

    def result(self):
        full = [lax.dynamic_update_index_in_dim(f, s, self.chip, 0) for f, s in zip(self.d2d.out, self.shards)]
        return [f.reshape((4, 2 * f.shape[2], f.shape[3])) for f in full]


def _index_operand(i):
    return jnp.reshape(i, (1,)).astype(jnp.int32)


def _add_pairs(g, f, ci, *, name):
    s, _, hr, wd = g.shape
    tr = _pick(hr, 600, 16)

    def body(c_ref, a_ref, b_ref, o_ref):
        o_ref[...] = (a_ref[...].astype(F32) + b_ref[...].astype(F32)).astype(BF16)

    spec = pl.BlockSpec((1, tr, wd), lambda j, i, c: (j, i, 0))
    grid_spec = pltpu.PrefetchScalarGridSpec(
        num_scalar_prefetch=1, grid=(s, hr // tr),
        in_specs=[pl.BlockSpec((1, None, tr, wd), lambda j, i, c: (j, c[0], i, 0)), spec], out_specs=spec)
    return _pcall(body, name=name, grid_spec=grid_spec, out_shape=jax.ShapeDtypeStruct(f.shape, BF16),
                  compiler_params=_params(("parallel", "parallel")))(_index_operand(ci), g, f)


def _sum_chips(landed, pairs, chip, *, name):
    s, hr, wd = landed.shape
    tr = _pick(hr, 600, 16)

    def body(k_ref, l_ref, p_ref, o_ref):
        acc = None
        for k in range(s):
            part = jnp.where(k_ref[0] == k, p_ref[k], l_ref[k]).astype(F32)
            acc = part if acc is None else acc + part
        o_ref[...] = acc

    spec = pl.BlockSpec((s, tr, wd), lambda i, k: (0, i, 0))
    grid_spec = pltpu.PrefetchScalarGridSpec(
        num_scalar_prefetch=1, grid=(hr // tr,), in_specs=[spec, spec],
        out_specs=pl.BlockSpec((tr, wd), lambda i, k: (i, 0)))
    return _pcall(body, name=name, grid_spec=grid_spec, out_shape=jax.ShapeDtypeStruct((hr, wd), F32),
                  compiler_params=_params(("parallel",)))(_index_operand(chip), landed, pairs)


class _ReduceScatter:
    def __init__(self, gs, chip, ci, tag):
        self.gs, self.chip, self.ci, self.tag = gs, chip, ci, tag
        self.half_t = [jax.ShapeDtypeStruct((g.shape[0],) + g.shape[2:], BF16) for g in gs]

    def pair(self, carrier=None):
        plan = lambda x, y, c: [((slice(None), 1 - c), (), (), (x, y, 1 - c))]
        self.r1 = _Ride(self.gs, self.half_t, plan)
        _round(self.r1, carrier, f"reduce_{self.tag}_pair")

    def chips(self, carrier=None):
        def plan(x, y, c):
            k = 2 * x + y
            return [((2 * px + py,), (k,), (2 * px + py,), (px, py, c)) for px, py in _other_chips(x, y)]

        self.pairs = [_add_pairs(g, f, self.ci, name=f"reduce_{self.tag}_pair_add{n}")
                      for n, (g, f) in enumerate(zip(self.gs, self.r1.out))]
        self.r2 = _Ride(self.pairs, self.half_t, plan)
        _round(self.r2, carrier, f"reduce_{self.tag}_chips")

    def halves(self, carrier=None):
        plan = lambda x, y, c: [((), (c,), (1 - c,), (x, y, 1 - c))]
        self.mine = [_sum_chips(l, p, self.chip, name=f"reduce_{self.tag}_chip_sum{n}")
                     for n, (l, p) in enumerate(zip(self.r2.out, self.pairs))]
        self.r3 = _Ride(self.mine, [jax.ShapeDtypeStruct((2,) + m.shape, F32) for m in self.mine], plan)
        _round(self.r3, carrier, f"reduce_{self.tag}_halves")

    def result(self):
        return [lax.dynamic_update_index_in_dim(b, m, self.ci, 0).reshape(2 * m.shape[0], m.shape[1])
                for b, m in zip(self.r3.out, self.mine)]


def _ada_fwd(c_all, w_sh, b_sh, *, name):
    nb, d = c_all.shape
    wcols = w_sh.shape[1]
    tn = _pick(wcols, 512)

    def body(c_ref, w_ref, b_ref, o_ref, a_ref):
        cv = c_ref[...]
        act = cv * jax.nn.sigmoid(cv)
        a_ref[...] = act
        o_ref[...] = jnp.dot(act.astype(BF16), w_ref[...].astype(BF16), preferred_element_type=F32) + b_ref[...]

    return _pcall(
        body, name=name, grid=(wcols // tn,),
        in_specs=[pl.BlockSpec((nb, d), lambda j: (0, 0)), pl.BlockSpec((d, tn), lambda j: (0, j)),
                  pl.BlockSpec((1, tn), lambda j: (0, j))],
        out_specs=[pl.BlockSpec((nb, tn), lambda j: (0, j)), pl.BlockSpec((nb, d), lambda j: (0, 0))],
        out_shape=[jax.ShapeDtypeStruct((nb, wcols), F32), jax.ShapeDtypeStruct((nb, d), F32)],
        compiler_params=_params(("arbitrary",)))(c_all, w_sh, b_sh)


def _sum_devices(g, *, name):
    nd, m, w = g.shape

    def body(g_ref, o_ref):
        acc = g_ref[0]
        for k in range(1, nd):
            acc = acc + g_ref[k]
        o_ref[...] = acc

    return _pcall(body, name=name, out_shape=jax.ShapeDtypeStruct((m, w), F32),
                  compiler_params=pltpu.CompilerParams(vmem_limit_bytes=VMEM_LIMIT_BYTES))(g)


def _adamw(w, g, m, v, *, name):
    rows, cols = w.shape[-2:]
    tr = _pick(rows, max(8, (1 << 18) // cols), 8)
    c1 = 1.0 / (1.0 - ADAM_B1 ** ADAM_STEP)
    c2 = 1.0 / (1.0 - ADAM_B2 ** ADAM_STEP)

    def body(w_ref, g_ref, m_ref, v_ref, d_ref, nm_ref, nv_ref):
        gv = g_ref[...]
        nm = ADAM_B1 * m_ref[...] + (1.0 - ADAM_B1) * gv
        nv = ADAM_B2 * v_ref[...] + (1.0 - ADAM_B2) * (gv * gv)
        d_ref[...] = -ADAM_LR * ((nm * c1) / (jnp.sqrt(nv * c2) + ADAM_EPS) + ADAM_WD * w_ref[...])
        nm_ref[...] = nm
        nv_ref[...] = nv

    gspec = pl.BlockSpec((tr, cols), lambda i: (i, 0))
    spec = pl.BlockSpec((None, tr, cols), lambda i: (0, i, 0)) if w.ndim == 3 else gspec
    shp = jax.ShapeDtypeStruct(w.shape, F32)
    return _pcall(body, name=name, grid=(rows // tr,), in_specs=[spec, gspec, spec, spec], out_specs=[spec] * 3,
                  out_shape=[shp] * 3, compiler_params=_params(("parallel",)))(w, g, m, v)


def _permute_in_rows(wt):
    ngrp = len(B_PATTERNS)
    qb, kb, vb = (wt[A_W + n * QB_W:A_W + (n + 1) * QB_W] for n in range(3))
    parts = [wt[:A_W], jnp.zeros((VAR_W - A_W, wt.shape[1]), wt.dtype)]
    for g in range(ngrp):
        parts += [t[g * GB_W:(g + 1) * GB_W] for t in (qb, kb, vb)]
    return jnp.concatenate(parts + [wt[A_W + 3 * QB_W:]], axis=0)


def _unpermute_in_grads(pieces):
    ga, groups, gg = pieces[0], pieces[1:-1], pieces[-1]
    rows = [ga[:A_W]]
    for n in range(3):
        rows += [gp[n * GB_W:(n + 1) * GB_W] for gp in groups]
    return jnp.concatenate(rows + [gg], axis=0)


def kernel(x, c, positions, w_ada, b_ada, w_in, sinks, w_branch_a, w_branch_b, w_o, ln1_g, ln1_b, w_gate_up, w_down, ln2_g, ln2_b, loss_target, m_w_ada, m_b_ada, m_w_in, m_sinks, m_w_branch_a, m_w_branch_b, m_w_o, m_ln1_g, m_ln1_b, m_w_gate_up, m_w_down, m_ln2_g, m_ln2_b, v_w_ada, v_b_ada, v_w_in, v_sinks, v_w_branch_a, v_w_branch_b, v_w_o, v_ln1_g, v_ln1_b, v_w_gate_up, v_w_down, v_ln2_g, v_ln2_b):
    xi, yi, ci = _coords()
    chip = 2 * xi + yi
    dev = 4 * xi + 2 * yi + ci
    NB, T, D = x.shape
    nchip, ndev = 4, 8
    ada_cols = w_ada.shape[2]

    ra, ro, rd = w_branch_a.shape[1], w_o.shape[1], w_down.shape[1]
    rowsh = jnp.concatenate([w_branch_a[0], w_o[0], w_down[0]], axis=0)
    halves = lambda a: a.reshape(a.shape[:-2] + (2, a.shape[-2] // 2, a.shape[-1]))
    tr = lambda a: jnp.swapaxes(a, -1, -2)
    shards = [halves(w.astype(BF16)) for w in (tr(w_in[0]), rowsh, w_branch_b[0], w_gate_up[0])]
    gin = _Gather(shards[:1], chip, "w_in", carriers=("gather_c", "ada_fwd", "gather_mod"))

    c_blk = jnp.zeros((8, D), F32).at[:NB].set(c)
    c_all = _allgather_small(c_blk, name="gather_c").reshape(ndev, 8, D)[:, :NB].reshape(ndev * NB, D)
    b_sh = lax.dynamic_slice(b_ada, (0, chip * ada_cols), (1, ada_cols))
    mod_part, c_act = _ada_fwd(c_all, w_ada[0], b_sh, name="ada_fwd")
    mod_g = _allgather_small(mod_part, name="gather_mod").reshape(nchip, 2, ndev * NB, ada_cols)[:, 0]
    mod_all = jnp.transpose(mod_g, (1, 0, 2)).reshape(ndev * NB, nchip * ada_cols)
    mod = lax.dynamic_slice(mod_all, (NB * dev, 0), (NB, nchip * ada_cols))

    (g_in,) = gin.result()
    w_in_f = _permute_in_rows(g_in.reshape(nchip * g_in.shape[1], D))
    mix = _Gather(shards[1:3], chip, "w_mix", carriers=("inproj_qkv", "attn_a_fwd"))
    ffn = _Gather(shards[3:], chip, "w_ffn", carriers=("attn_a_fwd", "attn_b0_fwd"))

    def rest_weights():
        (g_rows, w_b_f), (w_gu_f,) = mix.result(), ffn.result()
        return (g_rows[:, :ra].reshape(nchip * ra, D), w_b_f, g_rows[:, ra:ra + ro].reshape(nchip * ro, D), w_gu_f,
                g_rows[:, ra + ro:].reshape(nchip * rd, D))

    red = {}

    def hook(event, **g):
        if event == "rest_grads":
            gr_rows = jnp.concatenate([g["g_wa"].reshape(nchip, ra, D), g["g_wo"].reshape(nchip, ro, D),
                                       g["g_wd"].reshape(nchip, rd, D)], axis=1)
            red["ffn"] = _ReduceScatter([halves(g["g_wgu"])], chip, ci, "ffn")
            red["mix"] = _ReduceScatter([halves(gr_rows), halves(g["g_wb"])], chip, ci, "mix")
            red["ffn"].pair(carrier="merge_bwd")
            red["mix"].pair(carrier="merge_bwd")
        elif event == "merge_bwd_done":
            red["ffn"].chips(carrier="attn_a_bwd")
            red["mix"].chips(carrier="attn_b0_bwd")
        elif event == "attn_a_bwd_done":
            red["ffn"].halves(carrier="attn_b0_bwd")
        elif event == "attn_b0_bwd_done":
            red["mix"].halves(carrier="attn_b1_bwd")
        elif event == "win_grads":
            gr_in = _unpermute_in_grads(g["g_win"])
            red["w_in"] = _ReduceScatter([halves(gr_in.reshape(nchip, gr_in.shape[0] // nchip, D))], chip, ci, "w_in")
            red["w_in"].pair(carrier="inproj_dx0")
        elif event == "inproj_dx0_done":
            red["w_in"].chips(carrier="inproj_dx1")
        elif event == "inproj_dx1_done":
            red["w_in"].halves(carrier="gather_small")

    res = _local_step(x, mod, positions, w_in_f, rest_weights, sinks[0], ln1_g, ln1_b, ln2_g, ln2_b, loss_target, hook)
    (g_rows_red, g_w_b), (g_w_gu,) = red["mix"].result(), red["ffn"].result()
    g_w_a, g_w_o, g_w_d = g_rows_red[:ra], g_rows_red[ra:ra + ro], g_rows_red[ra + ro:]

    small_rows = 24
    misc = jnp.zeros((1, D), F32).at[0, :A_Q_HEADS].set(res["dsink"]).at[0, A_Q_HEADS].set(jnp.sum(res["loss"]))
    small = jnp.concatenate([res["dmod"].reshape(NB * 6, D), jnp.sum(res["ln_grads"], axis=0), misc,
                             jnp.zeros((small_rows - NB * 6 - 5, D), F32)], axis=0)
    small_all = _allgather_small(small, name="gather_small").reshape(ndev, small_rows, D)
    (g_w_in,) = red["w_in"].result()
    dmod_all = small_all[:, :NB * 6].reshape(ndev * NB, 6 * D)
    sums = _sum_devices(small_all, name="sum_small")
    g_b_ada = (sums[0:6] + sums[6:12]).reshape(1, 6 * D)
    g_ln1_g, g_ln1_b, g_ln2_g, g_ln2_b = (sums[12 + n][None] for n in range(4))
    g_sinks = sums[16, :A_Q_HEADS][None]
    loss = sums[16, A_Q_HEADS]
    dmod_sh = lax.dynamic_slice(dmod_all, (0, chip * ada_cols), (ndev * NB, ada_cols))
    g_w_ada = _mm(c_act, dmod_sh, ta=True, name="ada_dw")

    names = ["w_ada", "b_ada", "w_in", "sinks", "w_branch_a", "w_branch_b", "w_o", "ln1_g", "ln1_b",
             "w_gate_up", "w_down", "ln2_g", "ln2_b"]
    ws = [w_ada, b_ada, w_in, sinks, w_branch_a, w_branch_b, w_o, ln1_g, ln1_b, w_gate_up, w_down, ln2_g, ln2_b]
    ms = [m_w_ada, m_b_ada, m_w_in, m_sinks, m_w_branch_a, m_w_branch_b, m_w_o, m_ln1_g, m_ln1_b, m_w_gate_up,
          m_w_down, m_ln2_g, m_ln2_b]
    vs = [v_w_ada, v_b_ada, v_w_in, v_sinks, v_w_branch_a, v_w_branch_b, v_w_o, v_ln1_g, v_ln1_b, v_w_gate_up,
          v_w_down, v_ln2_g, v_ln2_b]
    gs = [g_w_ada, g_b_ada, g_w_in, g_sinks, g_w_a, g_w_b, g_w_o, g_ln1_g, g_ln1_b, g_w_gu, g_w_d, g_ln2_g, g_ln2_b]
    grads, deltas, new_ms, new_vs = [], [], [], []
    for name, w, g, m, v in zip(names, ws, gs, ms, vs):
        flip = tr if name == "w_in" else (lambda a: a)
        w, m, v = flip(w), flip(m), flip(v)
        g2 = g.reshape(w.shape[-2:])
        d, nm, nv = _adamw(w, g2, m, v, name="adamw_" + name)
        grads.append(flip(g2.reshape(w.shape)))
        deltas.append(flip(d))
        new_ms.append(flip(nm))
        new_vs.append(flip(nv))
    return (loss, res["grad_x"], *grads, *deltas, *new_ms, *new_vs)
```

```python
import functools

import jax
import jax.numpy as jnp
from jax import lax
from jax.experimental import pallas as pl
from jax.experimental.pallas import tpu as pltpu

F32 = jnp.float32
BF16 = jnp.bfloat16
MESH = pl.DeviceIdType.MESH

HEAD_DIM = 64
LANES = 128
PAIR_W = 2 * HEAD_DIM
BLOCK = 128
A_Q_HEADS = 16
A_KV_HEADS = 2
A_WINDOW = 128
B_PATTERNS = ((128, 1), (512, 4), (2048, 16))
B_GROUP_HEADS = 8
QA_W = A_Q_HEADS * HEAD_DIM
KA_W = A_KV_HEADS * HEAD_DIM
GB_W = B_GROUP_HEADS * HEAD_DIM
QB_W = GB_W * len(B_PATTERNS)
A_W = QA_W + 2 * KA_W
VAR_W = 3 * GB_W
N_VAR = 1 + len(B_PATTERNS)
VAR_DIL = (1,) + tuple(r for _, r in B_PATTERNS)
A_BWD_SPLIT = 4
QKV_P = N_VAR * VAR_W
ROPE_THETA = 10000.0
LN_EPS = 1e-5
NEG_INF = -1e30
DEPTH = 1
ALPHA = (2 * DEPTH) ** 0.25
SCALE = HEAD_DIM ** -0.5

ADAM_LR, ADAM_B1, ADAM_B2, ADAM_EPS, ADAM_WD, ADAM_STEP = 0.001, 0.9, 0.999, 1e-08, 0.01, 10

VMEM_LIMIT_BYTES = 56 * 1024 * 1024
MM_TILE_BYTES = 36 * 1024 * 1024
MM_WHOLE_K = 4096


def _params(sem=None):
    return pltpu.CompilerParams(dimension_semantics=sem, vmem_limit_bytes=VMEM_LIMIT_BYTES)


_RIDES = {}


def _pcall(body, *, name, **kw):
    rides = _RIDES.pop(name, None)
    if rides is None:
        return pl.pallas_call(body, name=name, **kw)
    return _riding_call(body, rides, name=name, **kw)


def _copies(src_refs, dst_refs, send_sems, recv_sems, plan):
    x, y, c = lax.axis_index("x"), lax.axis_index("y"), lax.axis_index("c")
    remote = plan(x, y, c)
    nrem = len(remote)
    at = lambda ref, idx: ref.at[idx] if idx else ref

    def copy(a, n, landing):
        si, di, ri, peer = remote[n]
        return pltpu.make_async_remote_copy(
            src_ref=at(src_refs[a], si), dst_ref=at(dst_refs[a], ri if landing else di),
            send_sem=send_sems.at[a * nrem + n], recv_sem=recv_sems.at[a * nrem + n],
            device_id=peer, device_id_type=MESH)

    order = [(a, n) for a in range(len(dst_refs)) for n in range(nrem)]

    def start():
        for a, n in order:
            copy(a, n, False).start()

    def wait():
        for a, n in order:
            copy(a, n, True).wait_recv()
        for a, n in order:
            copy(a, n, False).wait_send()

    return start, wait


class _Ride:
    def __init__(self, srcs, dsts, plan, dst_inits=None):
        self.srcs, self.dsts, self.plan, self.dst_inits, self.out = srcs, dsts, plan, dst_inits, None


def _riding_call(body, rides, *, name, in_specs, out_specs, out_shape, grid=(), scratch_shapes=(), **kw):
    single = not isinstance(out_specs, (list, tuple))
    out_specs = [out_specs] if single else list(out_specs)
    out_shape = [out_shape] if single else list(out_shape)
    n_in, n_out, n_scr = len(in_specs), len(out_specs), len(scratch_shapes)
    xin, xdsts, sems, aliases, layout = [], [], [], {}, []
    for ride in rides:
        srcs = ride.srcs() if callable(ride.srcs) else ride.srcs
        inits = ride.dst_inits() if callable(ride.dst_inits) else ride.dst_inits
        na, nrem = len(ride.dsts), len(ride.plan(0, 0, 0))
        src_at = len(xin) if srcs is not None else None
        xin += list(srcs) if srcs is not None else []
        if inits is not None:
            aliases.update({n_in + len(xin) + a: n_out + len(xdsts) + a for a in range(na)})
            xin += list(inits)
        layout.append((src_at, len(xdsts), na))
        xdsts += list(ride.dsts)
        sems += [pltpu.SemaphoreType.DMA((na * nrem,)), pltpu.SemaphoreType.DMA((na * nrem,))]

    def wrapped(*refs):
        ins, xins = refs[:n_in], refs[n_in:n_in + len(xin)]
        outs = refs[n_in + len(xin):n_in + len(xin) + n_out]
        xouts = refs[n_in + len(xin) + n_out:n_in + len(xin) + n_out + len(xdsts)]
        scr = refs[n_in + len(xin) + n_out + len(xdsts):]
        rounds = []
        for k, (ride, (src_at, dst_at, na)) in enumerate(zip(rides, layout)):
            dsts = xouts[dst_at:dst_at + na]
            srcs = dsts if src_at is None else xins[src_at:src_at + na]
            rounds.append(_copies(srcs, dsts, scr[n_scr + 2 * k], scr[n_scr + 2 * k + 1], ride.plan))
        ids = [pl.program_id(a) for a in range(len(grid))]
        first = functools.reduce(jnp.logical_and, [i == 0 for i in ids], True)
        last = functools.reduce(jnp.logical_and, [i == g - 1 for i, g in zip(ids, grid)], True)

        def start_all():
            for start, _ in rounds:
                start()

        def wait_all():
            for _, wait in rounds:
                wait()

        start_all() if not grid else pl.when(first)(start_all)
        body(*ins, *outs, *scr[:n_scr])
        wait_all() if not grid else pl.when(last)(wait_all)

    hbm = pl.BlockSpec(memory_space=pl.ANY)
    gridkw = dict(grid=grid) if grid else {}

    def run(*args):
        res = pl.pallas_call(
            wrapped, name=name, in_specs=list(in_specs) + [hbm] * len(xin),
            out_specs=out_specs + [hbm] * len(xdsts), out_shape=out_shape + xdsts,
            scratch_shapes=list(scratch_shapes) + sems, input_output_aliases=aliases,
            compiler_params=_params(("arbitrary",) * len(grid) if grid else None), **gridkw,
        )(*args, *xin)
        for ride, (_, dst_at, na) in zip(rides, layout):
            ride.out = list(res[n_out + dst_at:n_out + dst_at + na])
        return res[0] if single else list(res[:n_out])

    return run


def _pick(n, target, quantum=128):
    t = (min(target, n) // quantum) * quantum
    while t >= quantum:
        if n % t == 0:
            return t
        t -= quantum
    return n


def _mm(a, b, *, name, ta=False, tb=False, b3=False, out3=0, out_dtype=F32, add=None, tm=1024, tn=1536, tk=1536):
    if ta:
        K, M = a.shape
    else:
        M, K = a.shape
    if b3 and tb:
        Nn, K2, tk = b.shape[1], b.shape[0] * b.shape[2], b.shape[2]
    elif b3:
        K2, Nn, tn = b.shape[1], b.shape[0] * b.shape[2], b.shape[2]
    elif tb:
        Nn, K2 = b.shape
    else:
        K2, Nn = b.shape
    assert K == K2, (a.shape, b.shape)
    if out3:
        tn = Nn // out3
    tm, tn, tk = _pick(M, tm), _pick(Nn, tn), _pick(K, tk)
    if not (b3 and tb) and K <= MM_WHOLE_K:
        tk = K
        fits = lambda: 4 * tk * (tm + tn) + 8 * tm * tn * (2 if add is not None else 1) <= MM_TILE_BYTES
        while not fits():
            if (tm >= tn or b3 or out3) and tm > 256:
                tm = _pick(M, tm - 128)
            elif not (b3 or out3) and tn > 256:
                tn = _pick(Nn, tn - 128)
            else:
                break
    nk = K // tk
    j_outer = K * Nn + (Nn // tn) * M * K < M * K + (M // tm) * K * Nn
    dn = (((0 if ta else 1,), (1 if tb else 0,)), ((), ()))

    def body(*refs):
        refs = list(refs)
        a_ref, b_ref = refs[:2]
        add_ref = refs[2] if add is not None else None
        o_ref = refs[3] if add is not None else refs[2]
        part = lax.dot_general(a_ref[...].astype(BF16), b_ref[...].astype(BF16), dn, preferred_element_type=F32)

        def finish(r):
            if add is not None:
                r = r + add_ref[...]
            o_ref[...] = r.astype(out_dtype)

        if nk == 1:
            finish(part)
            return
        acc = refs[-1]
        k = pl.program_id(2)

        @pl.when(k == 0)
        def _():
            acc[...] = part

        @pl.when(k > 0)
        def _():
            acc[...] += part

        @pl.when(k == nk - 1)
        def _():
            finish(acc[...])

    def spec(shape, index):
        return pl.BlockSpec(shape, (lambda j, i, k: index(i, j, k)) if j_outer else index)

    a_spec = spec((tk, tm), lambda i, j, k: (k, i)) if ta else spec((tm, tk), lambda i, j, k: (i, k))
    if b3 and tb:
        b_spec = spec((None, tn, tk), lambda i, j, k: (k, j, 0))
    elif b3:
        b_spec = spec((None, tk, tn), lambda i, j, k: (j, k, 0))
    elif tb:
        b_spec = spec((tn, tk), lambda i, j, k: (j, k))
    else:
        b_spec = spec((tk, tn), lambda i, j, k: (k, j))
    if out3:
        o_spec = spec((None, tm, tn), lambda i, j, k: (j, i, 0))
    else:
        o_spec = spec((tm, tn), lambda i, j, k: (i, j))
    ins, specs = [a, b], [a_spec, b_spec]
    if add is not None:
        ins.append(add)
        specs.append(o_spec)
    grid = (Nn // tn, M // tm, nk) if j_outer else (M // tm, Nn // tn, nk)
    return _pcall(
        body, name=name, grid=grid, in_specs=specs, out_specs=o_spec,
        out_shape=jax.ShapeDtypeStruct((out3, M, tn) if out3 else (M, Nn), out_dtype),
        scratch_shapes=[pltpu.VMEM((tm, tn), F32)] if nk > 1 else [],
        compiler_params=_params(("parallel", "parallel", "arbitrary")),
    )(*ins)


def _mm_multi(a_list, b_list, *, name, M, T=None, add=None, out_dtype=F32, tm=512, post=None):
    tm = _pick(T or M, tm)
    ns = len(a_list)
    dils = [a[1] if isinstance(a, tuple) else 0 for a in a_list]
    a_arrs = [a[0] if isinstance(a, tuple) else a for a in a_list]
    widths = [a.shape[-1] // max(r, 1) for a, r in zip(a_arrs, dils)]
    b_arrs, b_specs = [], []
    for b in b_list:
        arr, shp, idx = b if isinstance(b, tuple) else (b, b.shape, (0, 0))
        b_arrs.append(arr)
        b_specs.append(pl.BlockSpec(shp, lambda i, idx=idx: idx))
    Nn = b_specs[0].block_shape[1]
    dn = (((1,), (0,)), ((), ()))
    nmm = 2 * ns + (1 if add is not None else 0)
    p_arrs, p_in_specs, p_out_specs, p_out_shape, p_fn = post or ([], [], None, None, None)
    nin = nmm + len(p_arrs)
    nout = len(p_out_specs) if post else 1

    def body(*refs):
        a_refs, b_refs, scr = refs[:ns], refs[ns:2 * ns], list(refs[nin + nout:])
        acc = None
        for a_ref, b_ref, r in zip(a_refs, b_refs, dils):
            av = _from_view(a_ref, scr.pop(0), r) if r > 1 else a_ref[...]
            part = lax.dot_general(av.astype(BF16), b_ref[...], dn, preferred_element_type=F32)
            acc = part if acc is None else acc + part
        if add is not None:
            acc = acc + refs[2 * ns][...]
        if post:
            p_fn(acc, refs[nmm:nin], refs[nin:nin + nout])
        else:
            refs[nin][...] = acc.astype(out_dtype)

    tpe = (T or M) // tm
    a_specs = [pl.BlockSpec((None, tm // r, r * w), lambda i: (i // tpe, i % tpe, 0)) if r
               else pl.BlockSpec((tm, w), lambda i: (i, 0)) for r, w in zip(dils, widths)]
    o_spec = pl.BlockSpec((tm, Nn), lambda i: (i, 0))
    specs = a_specs + b_specs
    ins = a_arrs + b_arrs
    if add is not None:
        specs.append(o_spec)
        ins.append(add)
    scratch = [pltpu.VMEM((w // LANES, tm, LANES), F32) for r, w in zip(dils, widths) if r > 1]
    return _pcall(body, name=name, grid=(M // tm,), in_specs=specs + list(p_in_specs),
                  out_specs=list(p_out_specs) if post else o_spec, scratch_shapes=scratch,
                  out_shape=list(p_out_shape) if post else jax.ShapeDtypeStruct((M, Nn), out_dtype),
                  compiler_params=_params(("arbitrary",) if post else ("parallel",)))(*ins, *p_arrs)


def _dw_view(d3, u, r, *, name, tk=1024):
    NB, tsub, rw = d3.shape
    W, T, D = rw // r, tsub * r, u.shape[1]
    tk = _pick(T, tk)
    tpe, nk = T // tk, NB * T // tk

    def body(d_ref, u_ref, o_ref, acc, scr):
        k = pl.program_id(0)
        dv = _from_view(d_ref, scr, r).astype(BF16)
        part = lax.dot_general(dv, u_ref[...], _TN, preferred_element_type=F32)

        @pl.when(k == 0)
        def _():
            acc[...] = part

        @pl.when(k > 0)
        def _():
            acc[...] += part

        @pl.when(k == nk - 1)
        def _():
            o_ref[...] = acc[...].astype(o_ref.dtype)

    return _pcall(
        body, name=name, grid=(nk,),
        in_specs=[pl.BlockSpec((None, tk // r, rw), lambda k: (k // tpe, k % tpe, 0)), pl.BlockSpec((tk, D), lambda k: (k, 0))],
        out_specs=pl.BlockSpec((W, D), lambda k: (0, 0)), out_shape=jax.ShapeDtypeStruct((W, D), BF16),
        scratch_shapes=[pltpu.VMEM((W, D), F32), pltpu.VMEM((W // LANES, tk, LANES), F32)],
        compiler_params=_params(("arbitrary",)))(d3, u)


def _lane(shape):
    return lax.broadcasted_iota(jnp.int32, shape, len(shape) - 1)


def _rot_half(v):
    w = v.shape[-1]
    first = (_lane(v.shape) % HEAD_DIM) < (HEAD_DIM // 2)
    return jnp.where(first, pltpu.roll(v, w - HEAD_DIM // 2, v.ndim - 1), pltpu.roll(v, HEAD_DIM // 2, v.ndim - 1))


def _widen(t, w):
    return t if w == t.shape[-1] else jnp.concatenate([t] * (w // t.shape[-1]), axis=-1)


def _unrope(v, cos, sins):
    w = v.shape[-1]
    return v * _widen(cos, w) - _rot_half(v) * _widen(sins, w)


def _rope_tables(positions):
    half = HEAD_DIM // 2
    inv = ROPE_THETA ** (-jnp.arange(half, dtype=F32) / half)
    ang = positions.astype(F32)[..., None] * inv
    cos, sin = jnp.cos(ang), jnp.sin(ang)
    cosf = jnp.concatenate([cos, cos, cos, cos], axis=-1)
    sins = jnp.concatenate([-sin, sin, -sin, sin], axis=-1)
    n = positions.shape[0] * positions.shape[1]
    return cosf.reshape(n, PAIR_W), sins.reshape(n, PAIR_W)


def _inproj(x2, scale, shift, w, cosf, sins, flags, *, T, name):
    N, D = x2.shape
    tm, tn = _pick(T, 512), VAR_W
    tpe = T // tm

    def body(x_ref, sc_ref, sh_ref, w_ref, c_ref, s_ref, f_ref, *outs):
        o_refs, u_ref = outs[:N_VAR], outs[N_VAR]
        j = pl.program_id(1)

        @pl.when(j == 0)
        def _():
            u_ref[...] = (x_ref[...] * (1.0 + sc_ref[0]) + sh_ref[0]).astype(BF16)

        acc = lax.dot_general(u_ref[...], w_ref[...], (((1,), (1,)), ((), ())), preferred_element_type=F32)
        fl = f_ref[...]
        ce = 1.0 + (_widen(c_ref[...], tn) - 1.0) * fl
        se = _widen(s_ref[...], tn) * fl
        res = acc * ce + _rot_half(acc) * se
        for v in range(N_VAR):
            @pl.when(j == v)
            def _(v=v):
                _to_view(res, o_refs[v], outs[N_VAR + 1], VAR_DIL[v])

    ex = pl.BlockSpec((1, 1, D), lambda i, j: (i // tpe, 0, 0))
    tab = pl.BlockSpec((tm, PAIR_W), lambda i, j: (i, 0))
    keep = lambda w_: pl.BlockSpec((tm, w_), lambda i, j: (i, 0))
    vspec = lambda r: pl.BlockSpec((None, tm // r, r * tn), lambda i, j: (i // tpe, i % tpe, 0))
    vshape = lambda r: jax.ShapeDtypeStruct((N // T, T // r, r * tn), BF16)
    return _pcall(
        body, name=name, grid=(N // tm, N_VAR),
        in_specs=[keep(D), ex, ex, pl.BlockSpec((tn, D), lambda i, j: (j, 0)), tab, tab,
                  pl.BlockSpec((1, tn), lambda i, j: (0, j))],
        out_specs=[vspec(r) for r in VAR_DIL] + [keep(D)],
        out_shape=[vshape(r) for r in VAR_DIL] + [jax.ShapeDtypeStruct((N, D), BF16)],
        scratch_shapes=[pltpu.VMEM((tn // LANES, tm, LANES), F32)],
        compiler_params=_params(("parallel", "arbitrary")),
    )(x2, scale, shift, w, cosf, sins, flags)


class _Geom:
    def __init__(self, g):
        if g is None:
            self.r, self.nq, self.n_back, self.sink = 1, A_Q_HEADS, A_WINDOW - 1, True
            self.qw, self.kw = QA_W, KA_W
            self.qidx = lambda j: 0
            self.kidx = lambda j: QA_W // KA_W
            self.vidx = lambda j: QA_W // KA_W + 1
        else:
            window, r = B_PATTERNS[g]
            self.r, self.nq, self.n_back, self.sink = r, B_GROUP_HEADS, window // r, False
            self.qw, self.kw = GB_W, GB_W
            self.qidx = lambda j: 3 * j
            self.kidx = lambda j: 3 * j + 1
            self.vidx = lambda j: 3 * j + 2
        self.ntile = self.qw // PAIR_W


def _stack_heads(t, scale=None):
    first = _lane(t.shape) < HEAD_DIM
    z = jnp.zeros_like(t)
    if scale is not None:
        t = t * jnp.asarray(scale, t.dtype)
    return jnp.concatenate([jnp.where(first, t, z), jnp.where(first, z, t)], axis=0)


def _lse_col(t):
    return jnp.concatenate([t[:, 0:1], t[:, HEAD_DIM:HEAD_DIM + 1]], axis=0)


def _lse_rows(t, width):
    first = _lane(t.shape) < HEAD_DIM
    other = pltpu.roll(t, HEAD_DIM, 1)
    full = jnp.concatenate([jnp.where(first, t, other), jnp.where(first, other, t)], axis=0)
    return _widen(full, width)


def _unstack_heads(v2):
    return jnp.where(_lane((BLOCK, PAIR_W)) < HEAD_DIM, v2[:BLOCK], v2[BLOCK:])


def _dup_head(t, kh):
    tf = t.astype(F32)
    keep = (_lane(t.shape) < HEAD_DIM) if kh == 0 else (_lane(t.shape) >= HEAD_DIM)
    return jnp.where(keep, tf, pltpu.roll(tf, HEAD_DIM, 1)).astype(t.dtype)


def _fold_heads(t):
    return t + pltpu.roll(t, HEAD_DIM, 1)


def _band_mask(rows, i, n_back, single):
    nkeys = BLOCK if single else 2 * BLOCK
    qi = jnp.bitwise_and(lax.broadcasted_iota(jnp.int32, (rows, nkeys), 0), BLOCK - 1)
    ki = lax.broadcasted_iota(jnp.int32, (rows, nkeys), 1)
    if single:
        return qi >= ki
    dist = qi + BLOCK - ki
    return jnp.logical_and(jnp.logical_and(dist >= 0, dist <= n_back), jnp.logical_or(ki >= BLOCK, i > 0))


def _per_block(col, scalars, fn):
    return jnp.concatenate([fn(col[b * BLOCK:(b + 1) * BLOCK], sc) for b, sc in enumerate(scalars)], axis=0)


def _sink_slot(rows):
    qi = jnp.bitwise_and(lax.broadcasted_iota(jnp.int32, (rows, 2 * BLOCK), 0), BLOCK - 1)
    return qi == lax.broadcasted_iota(jnp.int32, (rows, 2 * BLOCK), 1)


def _sink_scores(rows, sinks):
    blk = lax.broadcasted_iota(jnp.int32, (rows, 2 * BLOCK), 0) // BLOCK
    out = jnp.full((rows, 2 * BLOCK), sinks[-1], F32)
    for b in range(len(sinks) - 2, -1, -1):
        out = jnp.where(blk == b, sinks[b], out)
    return out


def _softmax_parts(s, valid, sinks):
    s = jnp.where(valid, s, NEG_INF)
    if sinks is not None:
        slot = _sink_slot(s.shape[0])
        s = jnp.where(slot, _sink_scores(s.shape[0], sinks), s)
    m = jnp.max(s, axis=1, keepdims=True)
    p = jnp.exp(s - m)
    den = jnp.sum(p, axis=1, keepdims=True)
    if sinks is not None:
        p = jnp.where(slot, 0.0, p)
    return p, m, den


_NT = (((1,), (1,)), ((), ()))
_TN = (((0,), (0,)), ((), ()))


def _rows2(prev_ref, cur_ref, cs, single=False):
    if single:
        return cur_ref[0, :, cs]
    return jnp.concatenate([prev_ref[0, :, cs], cur_ref[0, :, cs]], axis=0)


def _sink_scalars(sink_ref, first, nblocks):
    return [sink_ref[first + b] for b in range(nblocks)]


def _tile(t):
    return slice(t * PAIR_W, (t + 1) * PAIR_W)


def _attn_fwd(qkv, sinks, g, *, NB, T, name):
    geo = _Geom(g)
    r, qw, kw, ntile = geo.r, geo.qw, geo.kw, geo.ntile
    tsub = T // r
    nblk = tsub // BLOCK
    qkv3 = qkv.reshape(NB, tsub, r * VAR_W)
    out_dtype = BF16 if g is None else F32
    tiles_per_kv = ntile // A_KV_HEADS

    single = nblk == 1

    def body(q_ref, kp_ref, kc_ref, vp_ref, vc_ref, sink_ref, o_ref, l_ref):
        i = pl.program_id(2)
        if geo.sink:
            kall, vall = _rows2(kp_ref, kc_ref, _tile(0)), _rows2(vp_ref, vc_ref, _tile(0))
            kdup = [_dup_head(kall, kh) for kh in range(A_KV_HEADS)]
            vdup = [_dup_head(vall, kh) for kh in range(A_KV_HEADS)]
            tiles = [[t] for t in range(ntile)]
            q2s = [_stack_heads(q_ref[0, :, _tile(t)], SCALE) for t in range(ntile)]
            kks = [kdup[t // tiles_per_kv] for t in range(ntile)]
            vvs = [vdup[t // tiles_per_kv] for t in range(ntile)]
            sinkcols = [_sink_scalars(sink_ref, 2 * t, 2) for t in range(ntile)]
        else:
            tiles = [[t] for t in range(ntile)]
            q2s = [_stack_heads(q_ref[0, :, _tile(t)], SCALE) for t in range(ntile)]
            kks = [_rows2(kp_ref, kc_ref, _tile(t), single) for t in range(ntile)]
            vvs = [_rows2(vp_ref, vc_ref, _tile(t), single) for t in range(ntile)]
            sinkcols = [None] * ntile
        valid = _band_mask(q2s[0].shape[0], i, geo.n_back, single)
        ss = [lax.dot_general(q2, kk, _NT, preferred_element_type=F32) for q2, kk in zip(q2s, kks)]
        parts = [_softmax_parts(s, valid, sc) for s, sc in zip(ss, sinkcols)]
        o2s = [jnp.dot(p.astype(BF16), vv, preferred_element_type=F32) / den for (p, m, den), vv in zip(parts, vvs)]
        for ts, o2, (p, m, den) in zip(tiles, o2s, parts):
            lse2 = jnp.broadcast_to(m + jnp.log(den), (o2.shape[0], PAIR_W))
            for n, t in enumerate(ts):
                rows = slice(2 * BLOCK * n, 2 * BLOCK * (n + 1))
                o_ref[0, :, _tile(t)] = _unstack_heads(o2[rows]).astype(out_dtype)
                l_ref[0, :, _tile(t)] = _unstack_heads(lse2[rows])

    prev = lambda i: jnp.maximum(i - 1, 0)
    in_specs = [
        pl.BlockSpec((1, BLOCK, qw), lambda b, j, i: (b, i, geo.qidx(j))),
        pl.BlockSpec((1, BLOCK, kw), lambda b, j, i: (b, prev(i), geo.kidx(j))),
        pl.BlockSpec((1, BLOCK, kw), lambda b, j, i: (b, i, geo.kidx(j))),
        pl.BlockSpec((1, BLOCK, kw), lambda b, j, i: (b, prev(i), geo.vidx(j))),
        pl.BlockSpec((1, BLOCK, kw), lambda b, j, i: (b, i, geo.vidx(j))),
        pl.BlockSpec(memory_space=pltpu.SMEM),
    ]
    o_spec = pl.BlockSpec((1, BLOCK, qw), lambda b, j, i: (b, i, j))
    shape = (NB, tsub, r * qw)
    o, lse = _pcall(
        body, name=name, grid=(NB, r, nblk), in_specs=in_specs, out_specs=[o_spec, o_spec],
        out_shape=[jax.ShapeDtypeStruct(shape, out_dtype), jax.ShapeDtypeStruct(shape, F32)],
        compiler_params=_params(("parallel", "parallel", "arbitrary")),
    )(qkv3, qkv3, qkv3, qkv3, qkv3, sinks)
    return o, lse


def _attn_bwd(qkv, do, lse, dlse, cosf, sins, sinks, g, *, NB, T, name):
    geo = _Geom(g)
    r, qw, kw, ntile = geo.r, geo.qw, geo.kw, geo.ntile
    tsub = T // r
    nblk = tsub // BLOCK
    view = lambda a, w: a.reshape(NB, tsub, r * w)
    has_dlse = dlse is not None
    tiles_per_kv = ntile // A_KV_HEADS

    single = nblk == 1
    krows = BLOCK if single else 2 * BLOCK
    nsteps = 1 if single else nblk + 1

    def grads(q2s, kks, vvs, do2s, i, lserows, sinkcols, dlrows):
        nrow = q2s[0].shape[0]
        ki = lax.broadcasted_iota(jnp.int32, (krows, nrow), 0)
        qi = jnp.bitwise_and(lax.broadcasted_iota(jnp.int32, (krows, nrow), 1), BLOCK - 1)
        if single:
            valid = qi >= ki
        else:
            dist = qi + BLOCK - ki
            valid = jnp.logical_and(jnp.logical_and(dist >= 0, dist <= geo.n_back), jnp.logical_or(ki >= BLOCK, i > 0))
        sts = [lax.dot_general(kk, q2, _NT, preferred_element_type=F32) for q2, kk in zip(q2s, kks)]
        dpts = [lax.dot_general(vv, do2, _NT, preferred_element_type=F32) for do2, vv in zip(do2s, vvs)]
        pts, dsts, sks = [], [], []
        for st, dpt, ls, sc, dl in zip(sts, dpts, lserows, sinkcols, dlrows):
            sv = jnp.where(valid, st, NEG_INF)
            if sc is not None:
                slot = ki == qi
                blk = lax.broadcasted_iota(jnp.int32, (krows, nrow), 1) // BLOCK
                sink = jnp.full((krows, nrow), sc[-1], F32)
                for b in range(len(sc) - 2, -1, -1):
                    sink = jnp.where(blk == b, sc[b], sink)
                sv = jnp.where(slot, sink, sv)
                dpt = jnp.where(slot, 0.0, dpt)
            pt = jnp.exp(sv - ls)
            delta = jnp.sum(pt * dpt, axis=0, keepdims=True)
            if dl is not None:
                delta = delta - dl
            dst = pt * (dpt - delta)
            if sc is not None:
                cols = lambda a, b: a[:, b * BLOCK:(b + 1) * BLOCK]
                sks.append([jnp.sum(jnp.where(cols(slot, b), cols(dst, b), 0.0)) for b in range(len(sc))])
                dst, pt = jnp.where(slot, 0.0, dst), jnp.where(slot, 0.0, pt)
            else:
                sks.append(None)
            pts.append(pt.astype(BF16))
            dsts.append(dst.astype(BF16))
        dq2s = [lax.dot_general(dst, kk, _TN, preferred_element_type=F32) * SCALE for dst, kk in zip(dsts, kks)]
        dkks = [jnp.dot(dst, q2, preferred_element_type=F32) for dst, q2 in zip(dsts, q2s)]
        dvvs = [jnp.dot(pt, do2, preferred_element_type=F32) for pt, do2 in zip(pts, do2s)]
        return dq2s, dkks, dvvs, sks

    def stat_row(t):
        tt = t.T
        return jnp.concatenate([tt[0:1, :], tt[HEAD_DIM:HEAD_DIM + 1, :]], axis=1)

    def body(*refs):
        it = iter(refs)
        q_ref, kp_ref, kc_ref, vp_ref, vc_ref, do_ref, l_ref = (next(it) for _ in range(7))
        dl_ref = next(it) if has_dlse else None
        c_ref, s_ref, sink_ref, o_ref, ds_ref, dq_s, dk_s, dv_s, car_q, car_k, car_v = (next(it) for _ in range(11))
        b, j, i = pl.program_id(0), pl.program_id(1), pl.program_id(2)

        @pl.when(jnp.logical_and(b == 0, jnp.logical_and(j == 0, i == 0)))
        def _():
            ds_ref[...] = jnp.zeros_like(ds_ref)

        def compute():
            if geo.sink:
                kall, vall = _rows2(kp_ref, kc_ref, _tile(0)), _rows2(vp_ref, vc_ref, _tile(0))
                tps = tiles_per_kv // A_BWD_SPLIT
                nb = 2 * tps
                tiles = [[kh * tiles_per_kv + s_ * tps + t for t in range(tps)]
                         for kh in range(A_KV_HEADS) for s_ in range(A_BWD_SPLIT)]
                kdup = [_dup_head(kall, kh) for kh in range(A_KV_HEADS)]
                vdup = [_dup_head(vall, kh) for kh in range(A_KV_HEADS)]
                cat = lambda f, ts: jnp.concatenate([f(t) for t in ts], axis=0)
                dq2s, dkks, dvvs, sks = grads(
                    [cat(lambda t: _stack_heads(q_ref[0, :, _tile(t)], SCALE), ts) for ts in tiles],
                    [kdup[n // A_BWD_SPLIT] for n in range(len(tiles))],
                    [vdup[n // A_BWD_SPLIT] for n in range(len(tiles))],
                    [cat(lambda t: _stack_heads(do_ref[0, :, _tile(t)]), ts) for ts in tiles], i,
                    [jnp.concatenate([stat_row(l_ref[0, :, _tile(t)]) for t in ts], axis=1) for ts in tiles],
                    [_sink_scalars(sink_ref, 2 * ts[0], nb) for ts in tiles], [None] * len(tiles))
                lane1 = _lane((1, PAIR_W))
                dsink = jnp.zeros((1, PAIR_W), F32)
                for ts, dq2, sk in zip(tiles, dq2s, sks):
                    for n, t in enumerate(ts):
                        dq_s[:, _tile(t)] = _unstack_heads(dq2[2 * BLOCK * n:2 * BLOCK * (n + 1)])
                    for bb in range(nb):
                        dsink = dsink + jnp.where(lane1 == 2 * ts[0] + bb, sk[bb], 0.0)
                per_kv = lambda parts, kh: functools.reduce(jnp.add, parts[kh * A_BWD_SPLIT:(kh + 1) * A_BWD_SPLIT])
                second = _lane((krows, PAIR_W)) >= HEAD_DIM
                dk_s[...] = jnp.where(second, _fold_heads(per_kv(dkks, 1)), _fold_heads(per_kv(dkks, 0)))
                dv_s[...] = jnp.where(second, _fold_heads(per_kv(dvvs, 1)), _fold_heads(per_kv(dvvs, 0)))
                ds_ref[0:1, :] += dsink
            else:
                dq2s, dkks, dvvs, _ = grads(
                    [_stack_heads(q_ref[0, :, _tile(t)], SCALE) for t in range(ntile)],
                    [_rows2(kp_ref, kc_ref, _tile(t), single) for t in range(ntile)],
                    [_rows2(vp_ref, vc_ref, _tile(t), single) for t in range(ntile)],
                    [_stack_heads(do_ref[0, :, _tile(t)]) for t in range(ntile)], i,
                    [stat_row(l_ref[0, :, _tile(t)]) for t in range(ntile)], [None] * ntile,
                    [stat_row(dl_ref[0, :, _tile(t)]) for t in range(ntile)])
                for t in range(ntile):
                    dq_s[:, _tile(t)] = _unstack_heads(dq2s[t])
                    dk_s[0:krows, _tile(t)] = dkks[t]
                    dv_s[0:krows, _tile(t)] = dvvs[t]

        def emit(dq, dk, dv):
            cos, sn = c_ref[0], s_ref[0]
            o_ref[0, :, 0:qw] = _unrope(dq, cos, sn).astype(BF16)
            o_ref[0, :, qw:qw + kw] = _unrope(dk, cos, sn).astype(BF16)
            o_ref[0, :, qw + kw:qw + 2 * kw] = dv.astype(BF16)
            if qw + 2 * kw < VAR_W:
                o_ref[0, :, qw + 2 * kw:VAR_W] = jnp.zeros((BLOCK, VAR_W - qw - 2 * kw), BF16)

        if single:
            compute()
            emit(dq_s[...], dk_s[0:BLOCK, :], dv_s[0:BLOCK, :])
            return

        @pl.when(i == 0)
        def _():
            car_q[...] = jnp.zeros_like(car_q)
            car_k[...] = jnp.zeros_like(car_k)
            car_v[...] = jnp.zeros_like(car_v)

        @pl.when(i == nblk)
        def _():
            dk_s[...] = jnp.zeros_like(dk_s)
            dv_s[...] = jnp.zeros_like(dv_s)

        pl.when(i < nblk)(compute)
        emit(car_q[...], car_k[...] + dk_s[0:BLOCK, :], car_v[...] + dv_s[0:BLOCK, :])
        car_q[...] = dq_s[...]
        car_k[...] = dk_s[BLOCK:2 * BLOCK, :]
        car_v[...] = dv_s[BLOCK:2 * BLOCK, :]

    cur = lambda i: jnp.minimum(i, nblk - 1)
    prv = lambda i: jnp.maximum(jnp.minimum(i, nblk - 1) - 1, 0)
    outb = lambda i: jnp.maximum(i - 1, 0)
    qrow = pl.BlockSpec((1, BLOCK, qw), lambda b, j, i: (b, cur(i), j))
    in_specs = [
        pl.BlockSpec((1, BLOCK, qw), lambda b, j, i: (b, cur(i), geo.qidx(j))),
        pl.BlockSpec((1, BLOCK, kw), lambda b, j, i: (b, prv(i), geo.kidx(j))),
        pl.BlockSpec((1, BLOCK, kw), lambda b, j, i: (b, cur(i), geo.kidx(j))),
        pl.BlockSpec((1, BLOCK, kw), lambda b, j, i: (b, prv(i), geo.vidx(j))),
        pl.BlockSpec((1, BLOCK, kw), lambda b, j, i: (b, cur(i), geo.vidx(j))),
        qrow, qrow,
    ]
    ins = [view(qkv, VAR_W)] * 5 + [view(do, qw), view(lse, qw)]
    if has_dlse:
        in_specs.append(qrow)
        ins.append(view(dlse, qw))
    in_specs += [
        pl.BlockSpec((1, BLOCK, PAIR_W), lambda b, j, i: (b, outb(i), j)),
        pl.BlockSpec((1, BLOCK, PAIR_W), lambda b, j, i: (b, outb(i), j)),
        pl.BlockSpec(memory_space=pltpu.SMEM),
    ]
    ins += [view(cosf, PAIR_W), view(sins, PAIR_W), sinks]
    scratch = [pltpu.VMEM((BLOCK, qw), F32), pltpu.VMEM((2 * BLOCK, kw), F32), pltpu.VMEM((2 * BLOCK, kw), F32),
               pltpu.VMEM((BLOCK, qw), F32), pltpu.VMEM((BLOCK, kw), F32), pltpu.VMEM((BLOCK, kw), F32)]
    dqkv, dsink = _pcall(
        body, name=name, grid=(NB, r, nsteps), in_specs=in_specs,
        out_specs=[pl.BlockSpec((1, BLOCK, VAR_W), lambda b, j, i: (b, outb(i), j)),
                   pl.BlockSpec((8, PAIR_W), lambda b, j, i: (0, 0))],
        out_shape=[jax.ShapeDtypeStruct((NB, tsub, r * VAR_W), BF16), jax.ShapeDtypeStruct((8, PAIR_W), F32)],
        scratch_shapes=scratch, compiler_params=_params(("arbitrary", "arbitrary", "arbitrary")),
    )(*ins)
    return dqkv, dsink


class _Rows:
    def __init__(self, N, T, tm):
        self.N, self.tm, self.tpe, self.grid = N, tm, T // tm, (N // tm,)

    def row(self, w, col=0):
        return pl.BlockSpec((self.tm, w), lambda i: (i, col))

    def ex(self, w):
        return pl.BlockSpec((1, 1, w), lambda i: (i // self.tpe, 0, 0))

    def const(self, shape):
        return pl.BlockSpec(shape, lambda i: tuple(0 for _ in shape))

    def view(self, w, r):
        return pl.BlockSpec((None, self.tm // r, r * w), lambda i: (i // self.tpe, i % self.tpe, 0))

    def first_of_example(self):
        return pl.program_id(0) % self.tpe == 0


def _acc(ref, first, val):
    @pl.when(first)
    def _():
        ref[0] = val

    @pl.when(jnp.logical_not(first))
    def _():
        ref[0] += val


def _colsum(v):
    return jnp.sum(v, axis=0, keepdims=True)


def _ln_stats(r):
    mu = jnp.mean(r, axis=-1, keepdims=True)
    xc = r - mu
    var = jnp.mean(xc * xc, axis=-1, keepdims=True)
    rstd = lax.rsqrt(var + LN_EPS)
    return xc * rstd, rstd


def _ln_bwd(dy, xhat, rstd, gain):
    dxh = dy * gain
    return rstd * (dxh - jnp.mean(dxh, axis=-1, keepdims=True) - xhat * jnp.mean(dxh * xhat, axis=-1, keepdims=True))


def _from_view(ref, scr, r):
    if r == 1:
        return ref[...]
    rows, w = ref.shape[0], ref.shape[1] // r
    for j in range(r):
        for c in range(w // LANES):
            scr.at[c][pl.ds(j, rows, stride=r), :] = ref[:, j * w + c * LANES:j * w + (c + 1) * LANES].astype(F32)
    return jnp.concatenate([scr[c] for c in range(w // LANES)], axis=1)


def _to_view(val, ref, scr, r):
    if r == 1:
        ref[...] = val.astype(ref.dtype)
        return
    rows, w = ref.shape[0], ref.shape[1] // r
    for c in range(w // LANES):
        scr[c] = val[:, c * LANES:(c + 1) * LANES]
    for j in range(r):
        for c in range(w // LANES):
            ref[:, j * w + c * LANES:j * w + (c + 1) * LANES] = scr.at[c][pl.ds(j, rows, stride=r), :].astype(ref.dtype)


def _silu_parts(v):
    s = jax.nn.sigmoid(v)
    return v * s, s * (1.0 + v * (1.0 - s))


def _local_step(x, mod, positions, w_in, rest_weights, sinks, ln1_g, ln1_b, ln2_g, ln2_b, target, hook=None):
    hook = hook or (lambda event, **data: None)
    NB, T, D = x.shape
    N = NB * T
    x2 = x.reshape(N, D)
    tgt2 = target.reshape(N, D)
    shift_m, scale_m, gate_m, shift_f, scale_f, gate_f = [mod[:, None, k * D:(k + 1) * D] for k in range(6)]
    cosf, sins = _rope_tables(positions)
    col = jnp.arange(QKV_P)
    vcol = col % VAR_W
    flags = jnp.where(col < VAR_W, vcol < QA_W + KA_W, vcol < 2 * GB_W).astype(F32)[None]
    R = _Rows(N, T, _pick(T, 256))
    sds = jax.ShapeDtypeStruct
    exsum = lambda w=D: sds((NB, 1, w), F32)
    ngrp = len(B_PATTERNS)

    *qkv, u = _inproj(x2, scale_m, shift_m, w_in, cosf, sins, flags, T=T, name="inproj_qkv")
    gates = _mm(u, w_in[QKV_P:], tb=True, out_dtype=BF16, name="inproj_gates")
    oa, la = _attn_fwd(qkv[0], sinks, None, NB=NB, T=T, name="attn_a_fwd")
    oa = oa.reshape(N, QA_W)
    ob_parts = [_attn_fwd(qkv[1 + g], sinks, g, NB=NB, T=T, name=f"attn_b{g}_fwd") for g in range(ngrp)]
    (o1, l1), (o2, l2), (o3, l3) = ob_parts
    w_a, w_b, w_o, w_gu, w_d = rest_weights()
    F = w_d.shape[0]
    dil = [r_ for _, r_ in B_PATTERNS]
    views = [R.view(GB_W, r_) for r_ in dil]
    tokbuf = pltpu.VMEM((GB_W // LANES, R.tm, LANES), F32)

    def merge_fwd(o1r, o2r, o3r, l1r, l2r, l3r, ob_ref, *bufs):
        os_ = [_from_view(ref, bufs[n], dil[n]) for n, ref in enumerate((o1r, o2r, o3r))]
        la, lb, lc = [_from_view(ref, bufs[3 + n], dil[n]) for n, ref in enumerate((l1r, l2r, l3r))]
        mx = jnp.maximum(jnp.maximum(la, lb), lc)
        ea, eb, ec = jnp.exp(la - mx), jnp.exp(lb - mx), jnp.exp(lc - mx)
        ob_ref[...] = ((ea * os_[0] + eb * os_[1] + ec * os_[2]) / (ea + eb + ec)).astype(BF16)

    ob = _pcall(merge_fwd, name="merge_fwd", grid=R.grid, in_specs=views + views, out_specs=R.row(GB_W),
                out_shape=sds((N, GB_W), BF16), scratch_shapes=[tokbuf] * 6,
                compiler_params=_params(("parallel",)))(o1, o2, o3, l1, l2, l3)

    f32 = lambda ref: ref[...].astype(F32)
    Rm = _Rows(N, T, _pick(T, 512))

    def mix_out(oa_r, ob_r, ga_r, gb_r, x_r, gm_r, g_r, b_r, sf_r, hf_r, wa_r, wb_r, wo_r,
                ya_ref, yb_ref, mg_ref, y_ref, r1_ref, u2_ref):
        ya = jnp.dot(oa_r[...], wa_r[...], preferred_element_type=F32).astype(BF16)
        yb = jnp.concatenate([jnp.dot(ob_r[...], wb_r[s_], preferred_element_type=F32)
                              for s_ in range(w_b.shape[0])], axis=1).astype(BF16)
        merged = (jax.nn.sigmoid(f32(ga_r)) * ya.astype(F32) + jax.nn.sigmoid(f32(gb_r)) * yb.astype(F32)).astype(BF16)
        y = jnp.dot(merged, wo_r[...], preferred_element_type=F32)
        r1 = ALPHA * x_r[...] + (1.0 + gm_r[0]) * y
        xhat, _ = _ln_stats(r1)
        x1 = xhat * g_r[...] + b_r[...]
        ya_ref[...], yb_ref[...], mg_ref[...], y_ref[...], r1_ref[...] = ya, yb, merged, y, r1
        u2_ref[...] = (x1 * (1.0 + sf_r[0]) + hf_r[0]).astype(BF16)

    ya, yb, merged, y, r1, u2 = _pcall(
        mix_out, name="mix_out", grid=Rm.grid,
        in_specs=[Rm.row(QA_W), Rm.row(GB_W), Rm.row(D, 0), Rm.row(D, 1), Rm.row(D), Rm.ex(D), Rm.const((1, D)),
                  Rm.const((1, D)), Rm.ex(D), Rm.ex(D), Rm.const(w_a.shape), Rm.const(w_b.shape), Rm.const(w_o.shape)],
        out_specs=[Rm.row(D)] * 6,
        out_shape=[sds((N, D), BF16)] * 3 + [sds((N, D), F32)] * 2 + [sds((N, D), BF16)],
        compiler_params=_params(("parallel",)))(oa, ob, gates, gates, x2, gate_m, ln1_g, ln1_b, scale_f, shift_f,
                                                w_a, w_b, w_o)

    w_gu = w_gu() if callable(w_gu) else w_gu
    tnf = w_gu.shape[2]
    nft = w_gu.shape[0] // 2
    tmf = _pick(N, 512)

    def ffn_up(u_r, wg_r, wu_r, hg_ref, hu_ref, a_ref):
        hg = jnp.dot(u_r[...], wg_r[...], preferred_element_type=F32)
        hu = jnp.dot(u_r[...], wu_r[...], preferred_element_type=F32)
        sl, _ = _silu_parts(hg)
        hg_ref[...] = hg.astype(BF16)
        hu_ref[...] = hu.astype(BF16)
        a_ref[...] = (sl * hu).astype(BF16)

    ftile = pl.BlockSpec((tmf, tnf), lambda j, i: (i, j))
    hg, hu, act = _pcall(
        ffn_up, name="ffn_up", grid=(nft, N // tmf),
        in_specs=[pl.BlockSpec((tmf, D), lambda j, i: (i, 0)), pl.BlockSpec((None, D, tnf), lambda j, i: (j, 0, 0)),
                  pl.BlockSpec((None, D, tnf), lambda j, i: (j + nft, 0, 0))],
        out_specs=[ftile] * 3, out_shape=[sds((N, F), BF16)] * 3,
        compiler_params=_params(("arbitrary", "parallel")))(u2, w_gu, w_gu)
    def ffn_down_norm2(act_r, wd_r, r1_r, g1_r, b1_r, t_r, gf_r, g_r, b_r,
                       dy2_ref, dx1_ref, dgf_ref, dg_ref, db_ref, loss_ref):
        first = R.first_of_example()
        y2v = jnp.dot(act_r[...], wd_r[...], preferred_element_type=F32)
        x1 = _ln_stats(r1_r[...])[0] * g1_r[...] + b1_r[...]
        r2 = ALPHA * x1 + (1.0 + gf_r[0]) * y2v
        xhat, rstd = _ln_stats(r2)
        err = xhat * g_r[...] + b_r[...] - t_r[...]
        dx2 = err * (1.0 / D)
        dr2 = _ln_bwd(dx2, xhat, rstd, g_r[...])
        dy2_ref[...] = ((1.0 + gf_r[0]) * dr2).astype(BF16)
        dx1_ref[...] = ALPHA * dr2
        _acc(dgf_ref, first, _colsum(dr2 * y2v))
        _acc(dg_ref, first, _colsum(dx2 * xhat))
        _acc(db_ref, first, _colsum(dx2))
        part = 0.5 * jnp.sum(jnp.mean(err * err, axis=-1, keepdims=True))
        _acc(loss_ref, first, jnp.broadcast_to(part, (1, 128)))

    dy2, dx1p, dgate_f, dg2, db2, loss_p = _pcall(
        ffn_down_norm2, name="ffn_down_norm2", grid=R.grid,
        in_specs=[R.row(F), R.const((F, D)), R.row(D), R.const((1, D)), R.const((1, D)), R.row(D), R.ex(D),
                  R.const((1, D)), R.const((1, D))],
        out_specs=[R.row(D), R.row(D), R.ex(D), R.ex(D), R.ex(D), R.ex(128)],
        out_shape=[sds((N, D), BF16), sds((N, D), F32), exsum(), exsum(), exsum(), exsum(128)],
        compiler_params=_params(("arbitrary",)))(act, w_d, r1, ln1_g, ln1_b, tgt2, gate_f, ln2_g, ln2_b)

    g_wd = _mm(act, dy2, ta=True, out_dtype=BF16, name="ffn_down_dw")

    tmd = _pick(N, 256)

    fchunk = _pick(F, 768)

    def ffn_down_dx(dy_r, wd_r, hg_r, hu_r, dh_ref):
        for t in range(F // fchunk):
            cs = slice(t * fchunk, (t + 1) * fchunk)
            da = lax.dot_general(dy_r[...], wd_r[cs, :], _NT, preferred_element_type=F32)
            sl, dsl = _silu_parts(hg_r[:, cs].astype(F32))
            dh_ref[:, cs] = (da * hu_r[:, cs].astype(F32) * dsl).astype(BF16)
            dh_ref[:, F + t * fchunk:F + (t + 1) * fchunk] = (da * sl).astype(BF16)

    rowd = lambda w_: pl.BlockSpec((tmd, w_), lambda i: (i, 0))
    dh = _pcall(
        ffn_down_dx, name="ffn_down_dx", grid=(N // tmd,),
        in_specs=[rowd(D), pl.BlockSpec((F, D), lambda i: (0, 0)), rowd(F), rowd(F)],
        out_specs=rowd(2 * F), out_shape=sds((N, 2 * F), BF16),
        compiler_params=_params(("parallel",)))(dy2, w_d, hg, hu)
    g_wgu = _mm(u2, dh, ta=True, out3=w_gu.shape[0], out_dtype=BF16, name="ffn_up_dw")

    def ffn_up_dx_norm1(dh_r, w_r, dx1p_r, r1_r, y_r, sf_r, gm_r, g_r, b_r,
                        dxp_ref, dy_ref, dsf_ref, dhf_ref, dgm_ref, dg_ref, db_ref):
        first = R.first_of_example()
        du2v = None
        for s_ in range(w_gu.shape[0]):
            part = lax.dot_general(dh_r[:, s_ * tnf:(s_ + 1) * tnf], w_r[s_], _NT, preferred_element_type=F32)
            du2v = part if du2v is None else du2v + part
        dx1 = dx1p_r[...] + du2v * (1.0 + sf_r[0])
        xhat, rstd = _ln_stats(r1_r[...])
        dr1 = _ln_bwd(dx1, xhat, rstd, g_r[...])
        dxp_ref[...] = ALPHA * dr1
        dy_ref[...] = ((1.0 + gm_r[0]) * dr1).astype(BF16)
        _acc(dsf_ref, first, _colsum(du2v * (xhat * g_r[...] + b_r[...])))
        _acc(dhf_ref, first, _colsum(du2v))
        _acc(dgm_ref, first, _colsum(dr1 * y_r[...]))
        _acc(dg_ref, first, _colsum(dx1 * xhat))
        _acc(db_ref, first, _colsum(dx1))

    dxp, dy, dscale_f, dshift_f, dgate_m, dg1, db1 = _pcall(
        ffn_up_dx_norm1, name="ffn_up_dx_norm1", grid=R.grid,
        in_specs=[R.row(2 * F), R.const(w_gu.shape)] + [R.row(D)] * 3 + [R.ex(D), R.ex(D), R.const((1, D)),
                                                                        R.const((1, D))],
        out_specs=[R.row(D), R.row(D)] + [R.ex(D)] * 5,
        out_shape=[sds((N, D), F32), sds((N, D), BF16)] + [exsum()] * 5,
        compiler_params=_params(("arbitrary",)))(dh, w_gu, dx1p, r1, y, scale_f, gate_m, ln1_g, ln1_b)

    g_wo = _mm(merged, dy, ta=True, out_dtype=BF16, name="out_proj_dw")

    def mix_out_bwd(dy_r, ya_r, yb_r, ga_r, gb_r, wo_r, wa_r, wb_r, dya_ref, dyb_ref, dg_ref, doa_ref, dob_ref):
        dm = lax.dot_general(dy_r[...], wo_r[...], _NT, preferred_element_type=F32).astype(BF16).astype(F32)
        sa, sb = jax.nn.sigmoid(f32(ga_r)), jax.nn.sigmoid(f32(gb_r))
        dya, dyb = (dm * sa).astype(BF16), (dm * sb).astype(BF16)
        dya_ref[...], dyb_ref[...] = dya, dyb
        dg_ref[:, :D] = (dm * f32(ya_r) * sa * (1.0 - sa)).astype(BF16)
        dg_ref[:, D:] = (dm * f32(yb_r) * sb * (1.0 - sb)).astype(BF16)
        doa_ref[...] = lax.dot_general(dya, wa_r[...], _NT, preferred_element_type=F32).astype(BF16)
        ds_ = D // w_b.shape[0]
        dob = None
        for s_ in range(w_b.shape[0]):
            part = lax.dot_general(dyb[:, s_ * ds_:(s_ + 1) * ds_], wb_r[s_], _NT, preferred_element_type=F32)
            dob = part if dob is None else dob + part
        dob_ref[...] = dob

    dya, dyb, dgates, doa, dob = _pcall(
        mix_out_bwd, name="mix_out_bwd", grid=Rm.grid,
        in_specs=[Rm.row(D)] * 3 + [Rm.row(D, 0), Rm.row(D, 1), Rm.const(w_o.shape), Rm.const(w_a.shape),
                                    Rm.const(w_b.shape)],
        out_specs=[Rm.row(D), Rm.row(D), Rm.row(2 * D), Rm.row(QA_W), Rm.row(GB_W)],
        out_shape=[sds((N, D), BF16), sds((N, D), BF16), sds((N, 2 * D), BF16), sds((N, QA_W), BF16), sds((N, GB_W), F32)],
        compiler_params=_params(("parallel",)))(dy, ya, yb, gates, gates, w_o, w_a, w_b)

    g_wa = _mm(oa, dya, ta=True, out_dtype=BF16, name="branch_a_dw")
    g_wb = _mm(ob, dyb, ta=True, out3=w_b.shape[0], out_dtype=BF16, name="branch_b_dw")
    hook("rest_grads", g_wa=g_wa, g_wb=g_wb, g_wo=g_wo, g_wgu=g_wgu, g_wd=g_wd)

    seg = (jnp.arange(GB_W)[:, None] // HEAD_DIM == jnp.arange(GB_W)[None, :] // HEAD_DIM).astype(BF16)

    def merge_bwd(dob_r, o1r, o2r, o3r, l1r, l2r, l3r, seg_r, d1, d2, d3, e1, e2, e3, *bufs):
        dob_v = dob_r[...]
        os_ = [_from_view(ref, bufs[n], dil[n]) for n, ref in enumerate((o1r, o2r, o3r))]
        la, lb, lc = [_from_view(ref, bufs[3 + n], dil[n]) for n, ref in enumerate((l1r, l2r, l3r))]
        mx = jnp.maximum(jnp.maximum(la, lb), lc)
        ea, eb, ec = jnp.exp(la - mx), jnp.exp(lb - mx), jnp.exp(lc - mx)
        inv = 1.0 / (ea + eb + ec)
        ws = [ea * inv, eb * inv, ec * inv]

        def headsum(v):
            hi = v.astype(BF16)
            r1_ = v - hi.astype(F32)
            mid = r1_.astype(BF16)
            lo = (r1_ - mid.astype(F32)).astype(BF16)
            sm = seg_r[...]
            return (jnp.dot(hi, sm, preferred_element_type=F32) + jnp.dot(mid, sm, preferred_element_type=F32)
                    + jnp.dot(lo, sm, preferred_element_type=F32))

        dws = [headsum(dob_v * o) for o in os_]
        mean = ws[0] * dws[0] + ws[1] * dws[1] + ws[2] * dws[2]
        for n, (w_, dw_, d_ref, e_ref) in enumerate(zip(ws, dws, (d1, d2, d3), (e1, e2, e3))):
            _to_view(w_ * dob_v, d_ref, bufs[6], dil[n])
            _to_view(w_ * (dw_ - mean), e_ref, bufs[7], dil[n])

    vshape = lambda r_, dt: sds((NB, T // r_, r_ * GB_W), dt)
    mb = _pcall(
        merge_bwd, name="merge_bwd", grid=R.grid, in_specs=[R.row(GB_W)] + views + views + [R.const((GB_W, GB_W))],
        out_specs=views + views, out_shape=[vshape(r_, BF16) for r_ in dil] + [vshape(r_, F32) for r_ in dil],
        scratch_shapes=[tokbuf] * 8, compiler_params=_params(("parallel",)))(dob, o1, o2, o3, l1, l2, l3, seg)
    do_b, dlse_b = mb[:3], mb[3:]
    hook("merge_bwd_done")

    dqkv_a, dsink = _attn_bwd(qkv[0], doa, la, None, cosf, sins, sinks, None, NB=NB, T=T, name="attn_a_bwd")
    hook("attn_a_bwd_done")
    dqkv = [dqkv_a]
    for g in range(ngrp):
        dqkv.append(_attn_bwd(qkv[1 + g], do_b[g], (l1, l2, l3)[g], dlse_b[g], cosf, sins, sinks, g, NB=NB, T=T,
                              name=f"attn_b{g}_bwd")[0])
        hook(f"attn_b{g}_bwd_done")

    g_win = [_mm(d3.reshape(N, VAR_W), u, ta=True, out_dtype=BF16, name=f"inproj_dw{v}") if VAR_DIL[v] == 1
             else _dw_view(d3, u, VAR_DIL[v], name=f"inproj_dw{v}") for v, d3 in enumerate(dqkv)]
    g_win.append(_mm(dgates, u, ta=True, out_dtype=BF16, name=f"inproj_dw{N_VAR}"))
    hook("win_grads", g_win=g_win)
    wvar = lambda v: (w_in, (VAR_W, D), (v, 0))
    dview = lambda v: (dqkv[v], VAR_DIL[v])
    du = _mm_multi([dview(0)], [wvar(0)], M=N, T=T, name="inproj_dx0")
    hook("inproj_dx0_done")
    def x_bwd(duv, ins, outs):
        (dxp_r, x_r, sm_r), (gx_ref, dsm_ref, dhm_ref) = ins, outs
        first = R.first_of_example()
        gx_ref[...] = dxp_r[...] + duv * (1.0 + sm_r[0])
        _acc(dsm_ref, first, _colsum(duv * x_r[...]))
        _acc(dhm_ref, first, _colsum(duv))

    gx, dscale_m, dshift_m = _mm_multi(
        [dview(v) for v in range(1, N_VAR)] + [dgates],
        [wvar(v) for v in range(1, N_VAR)] + [(w_in, (2 * D, D), (QKV_P // (2 * D), 0))],
        M=N, T=T, add=du, tm=R.tm, name="inproj_dx1",
        post=([dxp, x2, scale_m], [R.row(D), R.row(D), R.ex(D)], [R.row(D), R.ex(D), R.ex(D)],
              [sds((N, D), F32), exsum(), exsum()], x_bwd))
    hook("inproj_dx1_done")

    dmod =jnp.concatenate([dshift_m, dscale_m, dgate_m, dshift_f, dscale_f, dgate_f], axis=-1)[:, 0]
    ln_grads = jnp.concatenate([dg1, db1, dg2, db2], axis=1)
    return dict(loss=loss_p[:, 0, 0], grad_x=gx.reshape(NB, T, D), g_win=g_win, g_wa=g_wa, g_wb=g_wb, g_wo=g_wo,
                g_wgu=g_wgu, g_wd=g_wd, dmod=dmod, ln_grads=ln_grads, dsink=dsink[0, :A_Q_HEADS])


def _coords():
    return lax.axis_index("x"), lax.axis_index("y"), lax.axis_index("c")


def _allgather_small(blk, *, name):
    m_per, n = blk.shape

    def body(x_ref, out_ref, send_sems, recv_sems, local_sem):
        x, y, c = _coords()
        me, sibling = (x, y, c), (x, y, 1 - c)
        chips = [(1 - x, y), (x, 1 - y), (1 - x, 1 - y)]

        def rows(px, py, pc):
            return out_ref.at[pl.ds((4 * px + 2 * py + pc) * m_per, m_per), :]

        def copy(k, block, to, src=None):
            return pltpu.make_async_remote_copy(
                src_ref=rows(*block) if src is None else src, dst_ref=rows(*block),
                send_sem=send_sems.at[k], recv_sem=recv_sems.at[k], device_id=to, device_id_type=MESH)

        mine = pltpu.make_async_copy(x_ref, rows(*me), local_sem)
        mine.start()
        first = [copy(0, me, sibling, src=x_ref)]
        first += [copy(1 + j, me, (*chip, c), src=x_ref) for j, chip in enumerate(chips)]
        for cp in first:
            cp.start()
        passed = [copy(4 + j, (*chip, c), sibling) for j, chip in enumerate(chips)]
        for j, chip in enumerate(chips):
            copy(1 + j, (*chip, c), me).wait_recv()
            passed[j].start()
        copy(0, sibling, me).wait_recv()
        for j, chip in enumerate(chips):
            copy(4 + j, (*chip, 1 - c), me).wait_recv()
        for cp in first + passed:
            cp.wait_send()
        mine.wait()

    return _pcall(
        body, name=name, out_shape=jax.ShapeDtypeStruct((8 * m_per, n), blk.dtype),
        in_specs=[pl.BlockSpec(memory_space=pltpu.VMEM)], out_specs=pl.BlockSpec(memory_space=pltpu.VMEM),
        scratch_shapes=[pltpu.SemaphoreType.DMA((7,)), pltpu.SemaphoreType.DMA((7,)), pltpu.SemaphoreType.DMA],
        compiler_params=pltpu.CompilerParams(vmem_limit_bytes=VMEM_LIMIT_BYTES),
    )(blk)


def _exchange(srcs, dsts, plan, *, name, dst_inits=None):
    na = len(dsts)
    nrem = len(plan(0, 0, 0))

    def body(*refs):
        refs = list(refs)
        src_refs = [refs.pop(0) for _ in range(na)] if srcs is not None else None
        if dst_inits is not None:
            del refs[:na]
        dst_refs, (send_sems, recv_sems) = refs[:na], refs[na:]
        start, wait = _copies(dst_refs if src_refs is None else src_refs, dst_refs, send_sems, recv_sems, plan)
        start()
        wait()

    hbm = pl.BlockSpec(memory_space=pl.ANY)
    ins = (list(srcs) if srcs is not None else []) + (list(dst_inits) if dst_inits is not None else [])
    base = na if srcs is not None else 0
    aliases = {base + a: a for a in range(na)} if dst_inits is not None else {}
    return _pcall(
        body, name=name, out_shape=list(dsts), in_specs=[hbm] * len(ins), out_specs=[hbm] * na,
        input_output_aliases=aliases,
        scratch_shapes=[pltpu.SemaphoreType.DMA((na * nrem,)), pltpu.SemaphoreType.DMA((na * nrem,))],
    )(*ins)


def _other_chips(x, y):
    return [(1 - x, y), (x, 1 - y), (1 - x, 1 - y)]


def _round(ride, carrier, name):
    if carrier is not None:
        _RIDES.setdefault(carrier, []).append(ride)
        return
    srcs = ride.srcs() if callable(ride.srcs) else ride.srcs
    inits = ride.dst_inits() if callable(ride.dst_inits) else ride.dst_inits
    ride.out = list(_exchange(srcs, ride.dsts, ride.plan, name=name, dst_inits=inits))


class _Gather:
    def __init__(self, shards, chip, tag, carriers=(None, None)):
        def plan_ici(x, y, c):
            k = 2 * x + y
            return [((c,), (k, c), (2 * px + py, c), (px, py, c)) for px, py in _other_chips(x, y)]

        def plan_d2d(x, y, c):
            return [((2 * px + py, c), (2 * px + py, c), (2 * px + py, 1 - c), (x, y, 1 - c))
                    for px, py in _other_chips(x, y)]

        def plan_near(x, y, c):
            k = 2 * x + y
            return [((c,), (k, c), (2 * px + py, c), (px, py, c)) for px, py in ((1 - x, y), (x, 1 - y))]

        def plan_far(x, y, c):
            kx, ky, kd = 2 * (1 - x) + y, 2 * x + (1 - y), 2 * (1 - x) + (1 - y)
            hp = shards[0].shape[1] // 2
            top, bottom = pl.ds(0, hp), pl.ds(hp, hp)
            return [((kx, c, top), (kx, c, top), (kd, c, top), (x, 1 - y, c)),
                    ((ky, c, bottom), (ky, c, bottom), (kd, c, bottom), (1 - x, y, c))]

        self.shards, self.chip = shards, chip
        dsts = [jax.ShapeDtypeStruct((4,) + s.shape, s.dtype) for s in shards]
        if len(carriers) == 3:
            near = _Ride(shards, dsts, plan_near)
            ici = _Ride(None, dsts, plan_far, dst_inits=lambda: near.out)
            _round(near, carriers[0], f"gather_{tag}_near")
            _round(ici, carriers[1], f"gather_{tag}_far")
        else:
            ici = _Ride(shards, dsts, plan_ici)
            _round(ici, carriers[0], f"gather_{tag}_ici")
        self.d2d = _Ride(None, dsts, plan_d2d, dst_inits=lambda: ici.out)
        _round(self.d2d, carriers[-1], f"gather_{tag}_d2d")

    def result(self):
        full = [lax.dynamic_update_index_in_dim(f, s, self.chip, 0) for f, s in zip(self.d2d.out, self.shards)]
        return [f.reshape((4, 2 * f.shape[2], f.shape[3])) for f in full]


def _index_operand(i):
    return jnp.reshape(i, (1,)).astype(jnp.int32)


def _add_pairs(g, f, ci, *, name):
    s, _, hr, wd = g.shape
    tr = _pick(hr, 600, 16)

    def body(c_ref, a_ref, b_ref, o_ref):
        o_ref[...] = (a_ref[...].astype(F32) + b_ref[...].astype(F32)).astype(BF16)

    spec = pl.BlockSpec((1, tr, wd), lambda j, i, c: (j, i, 0))
    grid_spec = pltpu.PrefetchScalarGridSpec(
        num_scalar_prefetch=1, grid=(s, hr // tr),
        in_specs=[pl.BlockSpec((1, None, tr, wd), lambda j, i, c: (j, c[0], i, 0)), spec], out_specs=spec)
    return _pcall(body, name=name, grid_spec=grid_spec, out_shape=jax.ShapeDtypeStruct(f.shape, BF16),
                  compiler_params=_params(("parallel", "parallel")))(_index_operand(ci), g, f)


def _sum_chips(landed, pairs, chip, *, name):
    s, hr, wd = landed.shape
    tr = _pick(hr, 600, 16)

    def body(k_ref, l_ref, p_ref, o_ref):
        acc = None
        for k in range(s):
            part = jnp.where(k_ref[0] == k, p_ref[k], l_ref[k]).astype(F32)
            acc = part if acc is None else acc + part
        o_ref[...] = acc

    spec = pl.BlockSpec((s, tr, wd), lambda i, k: (0, i, 0))
    grid_spec = pltpu.PrefetchScalarGridSpec(
        num_scalar_prefetch=1, grid=(hr // tr,), in_specs=[spec, spec],
        out_specs=pl.BlockSpec((tr, wd), lambda i, k: (i, 0)))
    return _pcall(body, name=name, grid_spec=grid_spec, out_shape=jax.ShapeDtypeStruct((hr, wd), F32),
                  compiler_params=_params(("parallel",)))(_index_operand(chip), landed, pairs)


class _ReduceScatter:
    def __init__(self, gs, chip, ci, tag):
        self.gs, self.chip, self.ci, self.tag = gs, chip, ci, tag
        self.half_t = [jax.ShapeDtypeStruct((g.shape[0],) + g.shape[2:], BF16) for g in gs]

    def pair(self, carrier=None):
        plan = lambda x, y, c: [((slice(None), 1 - c), (), (), (x, y, 1 - c))]
        self.r1 = _Ride(self.gs, self.half_t, plan)
        _round(self.r1, carrier, f"reduce_{self.tag}_pair")

    def chips(self, carrier=None):
        def plan(x, y, c):
            k = 2 * x + y
            return [((2 * px + py,), (k,), (2 * px + py,), (px, py, c)) for px, py in _other_chips(x, y)]

        self.pairs = [_add_pairs(g, f, self.ci, name=f"reduce_{self.tag}_pair_add{n}")
                      for n, (g, f) in enumerate(zip(self.gs, self.r1.out))]
        self.r2 = _Ride(self.pairs, self.half_t, plan)
        _round(self.r2, carrier, f"reduce_{self.tag}_chips")

    def halves(self, carrier=None):
        plan = lambda x, y, c: [((), (c,), (1 - c,), (x, y, 1 - c))]
        self.mine = [_sum_chips(l, p, self.chip, name=f"reduce_{self.tag}_chip_sum{n}")
                     for n, (l, p) in enumerate(zip(self.r2.out, self.pairs))]
        self.r3 = _Ride(self.mine, [jax.ShapeDtypeStruct((2,) + m.shape, F32) for m in self.mine], plan)
        _round(self.r3, carrier, f"reduce_{self.tag}_halves")

    def result(self):
        return [lax.dynamic_update_index_in_dim(b, m, self.ci, 0).reshape(2 * m.shape[0], m.shape[1])
                for b, m in zip(self.r3.out, self.mine)]


def _ada_fwd(c_all, w_sh, b_sh, *, name):
    nb, d = c_all.shape
    wcols = w_sh.shape[1]
    tn = _pick(wcols, 512)

    def body(c_ref, w_ref, b_ref, o_ref, a_ref):
        cv = c_ref[...]
        act = cv * jax.nn.sigmoid(cv)
        a_ref[...] = act
        o_ref[...] = jnp.dot(act.astype(BF16), w_ref[...].astype(BF16), preferred_element_type=F32) + b_ref[...]

    return _pcall(
        body, name=name, grid=(wcols // tn,),
        in_specs=[pl.BlockSpec((nb, d), lambda j: (0, 0)), pl.BlockSpec((d, tn), lambda j: (0, j)),
                  pl.BlockSpec((1, tn), lambda j: (0, j))],
        out_specs=[pl.BlockSpec((nb, tn), lambda j: (0, j)), pl.BlockSpec((nb, d), lambda j: (0, 0))],
        out_shape=[jax.ShapeDtypeStruct((nb, wcols), F32), jax.ShapeDtypeStruct((nb, d), F32)],
        compiler_params=_params(("arbitrary",)))(c_all, w_sh, b_sh)


def _sum_devices(g, *, name):
    nd, m, w = g.shape

    def body(g_ref, o_ref):
        acc = g_ref[0]
        for k in range(1, nd):
            acc = acc + g_ref[k]
        o_ref[...] = acc

    return _pcall(body, name=name, out_shape=jax.ShapeDtypeStruct((m, w), F32),
                  compiler_params=pltpu.CompilerParams(vmem_limit_bytes=VMEM_LIMIT_BYTES))(g)


def _adamw(w, g, m, v, *, name):
    rows, cols = w.shape[-2:]
    tr = _pick(rows, max(8, (1 << 18) // cols), 8)
    c1 = 1.0 / (1.0 - ADAM_B1 ** ADAM_STEP)
    c2 = 1.0 / (1.0 - ADAM_B2 ** ADAM_STEP)

    def body(w_ref, g_ref, m_ref, v_ref, d_ref, nm_ref, nv_ref):
        gv = g_ref[...]
        nm = ADAM_B1 * m_ref[...] + (1.0 - ADAM_B1) * gv
        nv = ADAM_B2 * v_ref[...] + (1.0 - ADAM_B2) * (gv * gv)
        d_ref[...] = -ADAM_LR * ((nm * c1) / (jnp.sqrt(nv * c2) + ADAM_EPS) + ADAM_WD * w_ref[...])
        nm_ref[...] = nm
        nv_ref[...] = nv

    gspec = pl.BlockSpec((tr, cols), lambda i: (i, 0))
    spec = pl.BlockSpec((None, tr, cols), lambda i: (0, i, 0)) if w.ndim == 3 else gspec
    shp = jax.ShapeDtypeStruct(w.shape, F32)
    return _pcall(body, name=name, grid=(rows // tr,), in_specs=[spec, gspec, spec, spec], out_specs=[spec] * 3,
                  out_shape=[shp] * 3, compiler_params=_params(("parallel",)))(w, g, m, v)


def _permute_in_rows(wt):
    ngrp = len(B_PATTERNS)
    qb, kb, vb = (wt[A_W + n * QB_W:A_W + (n + 1) * QB_W] for n in range(3))
    parts = [wt[:A_W], jnp.zeros((VAR_W - A_W, wt.shape[1]), wt.dtype)]
    for g in range(ngrp):
        parts += [t[g * GB_W:(g + 1) * GB_W] for t in (qb, kb, vb)]
    return jnp.concatenate(parts + [wt[A_W + 3 * QB_W:]], axis=0)


def _unpermute_in_grads(pieces):
    ga, groups, gg = pieces[0], pieces[1:-1], pieces[-1]
    rows = [ga[:A_W]]
    for n in range(3):
        rows += [gp[n * GB_W:(n + 1) * GB_W] for gp in groups]
    return jnp.concatenate(rows + [gg], axis=0)


def kernel(x, c, positions, w_ada, b_ada, w_in, sinks, w_branch_a, w_branch_b, w_o, ln1_g, ln1_b, w_gate_up, w_down, ln2_g, ln2_b, loss_target, m_w_ada, m_b_ada, m_w_in, m_sinks, m_w_branch_a, m_w_branch_b, m_w_o, m_ln1_g, m_ln1_b, m_w_gate_up, m_w_down, m_ln2_g, m_ln2_b, v_w_ada, v_b_ada, v_w_in, v_sinks, v_w_branch_a, v_w_branch_b, v_w_o, v_ln1_g, v_ln1_b, v_w_gate_up, v_w_down, v_ln2_g, v_ln2_b):
    xi, yi, ci = _coords()
    chip = 2 * xi + yi
    dev = 4 * xi + 2 * yi + ci
    NB, T, D = x.shape
    nchip, ndev = 4, 8
    ada_cols = w_ada.shape[2]

    ra, ro, rd = w_branch_a.shape[1], w_o.shape[1], w_down.shape[1]
    rowsh = jnp.concatenate([w_branch_a[0], w_o[0], w_down[0]], axis=0)
    halves = lambda a: a.reshape(a.shape[:-2] + (2, a.shape[-2] // 2, a.shape[-1]))
    tr = lambda a: jnp.swapaxes(a, -1, -2)
    shards = [halves(w.astype(BF16)) for w in (tr(w_in[0]), rowsh, w_branch_b[0], w_gate_up[0])]
    gin = _Gather(shards[:1], chip, "w_in", carriers=("gather_c", "ada_fwd", "gather_mod"))

    c_blk = jnp.zeros((8, D), F32).at[:NB].set(c)
    c_all = _allgather_small(c_blk, name="gather_c").reshape(ndev, 8, D)[:, :NB].reshape(ndev * NB, D)
    b_sh = lax.dynamic_slice(b_ada, (0, chip * ada_cols), (1, ada_cols))
    mod_part, c_act = _ada_fwd(c_all, w_ada[0], b_sh, name="ada_fwd")
    mod_g = _allgather_small(mod_part, name="gather_mod").reshape(nchip, 2, ndev * NB, ada_cols)[:, 0]
    mod_all = jnp.transpose(mod_g, (1, 0, 2)).reshape(ndev * NB, nchip * ada_cols)
    mod = lax.dynamic_slice(mod_all, (NB * dev, 0), (NB, nchip * ada_cols))

    (g_in,) = gin.result()
    w_in_f = _permute_in_rows(g_in.reshape(nchip * g_in.shape[1], D))
    mix = _Gather(shards[1:3], chip, "w_mix", carriers=("inproj_qkv", "attn_a_fwd"))
    ffn = _Gather(shards[3:], chip, "w_ffn", carriers=("attn_a_fwd", "attn_b0_fwd", "merge_fwd"))

    def rest_weights():
        g_rows, w_b_f = mix.result()
        return (g_rows[:, :ra].reshape(nchip * ra, D), w_b_f, g_rows[:, ra:ra + ro].reshape(nchip * ro, D),
                lambda: ffn.result()[0], g_rows[:, ra + ro:].reshape(nchip * rd, D))

    red = {}

    def hook(event, **g):
        if event == "rest_grads":
            gr_rows = jnp.concatenate([g["g_wa"].reshape(nchip, ra, D), g["g_wo"].reshape(nchip, ro, D),
                                       g["g_wd"].reshape(nchip, rd, D)], axis=1)
            red["ffn"] = _ReduceScatter([halves(g["g_wgu"])], chip, ci, "ffn")
            red["mix"] = _ReduceScatter([halves(gr_rows), halves(g["g_wb"])], chip, ci, "mix")
            red["ffn"].pair(carrier="merge_bwd")
            red["mix"].pair(carrier="merge_bwd")
        elif event == "merge_bwd_done":
            red["ffn"].chips(carrier="attn_a_bwd")
            red["mix"].chips(carrier="attn_b0_bwd")
        elif event == "attn_a_bwd_done":
            red["ffn"].halves(carrier="attn_b0_bwd")
        elif event == "attn_b0_bwd_done":
            red["mix"].halves(carrier="attn_b1_bwd")
        elif event == "win_grads":
            gr_in = _unpermute_in_grads(g["g_win"])
            red["w_in"] = _ReduceScatter([halves(gr_in.reshape(nchip, gr_in.shape[0] // nchip, D))], chip, ci, "w_in")
            red["w_in"].pair(carrier="inproj_dx0")
        elif event == "inproj_dx0_done":
            red["w_in"].chips(carrier="inproj_dx1")
        elif event == "inproj_dx1_done":
            red["w_in"].halves(carrier="gather_small")

    res = _local_step(x, mod, positions, w_in_f, rest_weights, sinks[0], ln1_g, ln1_b, ln2_g, ln2_b, loss_target, hook)
    (g_rows_red, g_w_b), (g_w_gu,) = red["mix"].result(), red["ffn"].result()
    g_w_a, g_w_o, g_w_d = g_rows_red[:ra], g_rows_red[ra:ra + ro], g_rows_red[ra + ro:]

    small_rows = 24
    misc = jnp.zeros((1, D), F32).at[0, :A_Q_HEADS].set(res["dsink"]).at[0, A_Q_HEADS].set(jnp.sum(res["loss"]))
    small = jnp.concatenate([res["dmod"].reshape(NB * 6, D), jnp.sum(res["ln_grads"], axis=0), misc,
                             jnp.zeros((small_rows - NB * 6 - 5, D), F32)], axis=0)
    small_all = _allgather_small(small, name="gather_small").reshape(ndev, small_rows, D)
    (g_w_in,) = red["w_in"].result()
    dmod_all = small_all[:, :NB * 6].reshape(ndev * NB, 6 * D)
    sums = _sum_devices(small_all, name="sum_small")
    g_b_ada = (sums[0:6] + sums[6:12]).reshape(1, 6 * D)
    g_ln1_g, g_ln1_b, g_ln2_g, g_ln2_b = (sums[12 + n][None] for n in range(4))
    g_sinks = sums[16, :A_Q_HEADS][None]
    loss = sums[16, A_Q_HEADS]
    dmod_sh = lax.dynamic_slice(dmod_all, (0, chip * ada_cols), (ndev * NB, ada_cols))
    g_w_ada = _mm(c_act, dmod_sh, ta=True, name="ada_dw")

    names = ["w_ada", "b_ada", "w_in", "sinks", "w_branch_a", "w_branch_b", "w_o", "ln1_g", "ln1_b",
             "w_gate_up", "w_down", "ln2_g", "ln2_b"]
    ws = [w_ada, b_ada, w_in, sinks, w_branch_a, w_branch_b, w_o, ln1_g, ln1_b, w_gate_up, w_down, ln2_g, ln2_b]
    ms = [m_w_ada, m_b_ada, m_w_in, m_sinks, m_w_branch_a, m_w_branch_b, m_w_o, m_ln1_g, m_ln1_b, m_w_gate_up,
          m_w_down, m_ln2_g, m_ln2_b]
    vs = [v_w_ada, v_b_ada, v_w_in, v_sinks, v_w_branch_a, v_w_branch_b, v_w_o, v_ln1_g, v_ln1_b, v_w_gate_up,
          v_w_down, v_ln2_g, v_ln2_b]
    gs = [g_w_ada, g_b_ada, g_w_in, g_sinks, g_w_a, g_w_b, g_w_o, g_ln1_g, g_ln1_b, g_w_gu, g_w_d, g_ln2_g, g_ln2_b]
    grads, deltas, new_ms, new_vs = [], [], [], []
    for name, w, g, m, v in zip(names, ws, gs, ms, vs):
        flip = tr if name == "w_in" else (lambda a: a)
        w, m, v = flip(w), flip(m), flip(v)
        g2 = g.reshape(w.shape[-2:])
        d, nm, nv = _adamw(w, g2, m, v, name="adamw_" + name)
        grads.append(flip(g2.reshape(w.shape)))
        deltas.append(flip(d))
        new_ms.append(flip(nm))
        new_vs.append(flip(nv))
    return (loss, res["grad_x"], *grads, *deltas, *new_ms, *new_vs)
```

```python
import functools

import jax
import jax.numpy as jnp
from jax import lax
from jax.experimental import pallas as pl
from jax.experimental.pallas import tpu as pltpu

F32 = jnp.float32
BF16 = jnp.bfloat16
MESH = pl.DeviceIdType.MESH

HEAD_DIM = 64
LANES = 128
PAIR_W = 2 * HEAD_DIM
BLOCK = 128
A_Q_HEADS = 16
A_KV_HEADS = 2
A_WINDOW = 128
B_PATTERNS = ((128, 1), (512, 4), (2048, 16))
B_GROUP_HEADS = 8
QA_W = A_Q_HEADS * HEAD_DIM
KA_W = A_KV_HEADS * HEAD_DIM
GB_W = B_GROUP_HEADS * HEAD_DIM
QB_W = GB_W * len(B_PATTERNS)
A_W = QA_W + 2 * KA_W
VAR_W = 3 * GB_W
N_VAR = 1 + len(B_PATTERNS)
VAR_DIL = (1,) + tuple(r for _, r in B_PATTERNS)
A_BWD_SPLIT = 4
QKV_P = N_VAR * VAR_W
ROPE_THETA = 10000.0
LN_EPS = 1e-5
NEG_INF = -1e30
DEPTH = 1
ALPHA = (2 * DEPTH) ** 0.25
SCALE = HEAD_DIM ** -0.5

ADAM_LR, ADAM_B1, ADAM_B2, ADAM_EPS, ADAM_WD, ADAM_STEP = 0.001, 0.9, 0.999, 1e-08, 0.01, 10

VMEM_LIMIT_BYTES = 56 * 1024 * 1024
MM_TILE_BYTES = 36 * 1024 * 1024
MM_WHOLE_K = 4096


def _params(sem=None):
    return pltpu.CompilerParams(dimension_semantics=sem, vmem_limit_bytes=VMEM_LIMIT_BYTES)


_RIDES = {}


def _pcall(body, *, name, **kw):
    rides = _RIDES.pop(name, None)
    if rides is None:
        return pl.pallas_call(body, name=name, **kw)
    return _riding_call(body, rides, name=name, **kw)


def _copies(src_refs, dst_refs, send_sems, recv_sems, plan):
    x, y, c = lax.axis_index("x"), lax.axis_index("y"), lax.axis_index("c")
    remote = plan(x, y, c)
    nrem = len(remote)
    at = lambda ref, idx: ref.at[idx] if idx else ref

    def copy(a, n, landing):
        si, di, ri, peer = remote[n]
        return pltpu.make_async_remote_copy(
            src_ref=at(src_refs[a], si), dst_ref=at(dst_refs[a], ri if landing else di),
            send_sem=send_sems.at[a * nrem + n], recv_sem=recv_sems.at[a * nrem + n],
            device_id=peer, device_id_type=MESH)

    order = [(a, n) for a in range(len(dst_refs)) for n in range(nrem)]

    def start():
        for a, n in order:
            copy(a, n, False).start()

    def wait():
        for a, n in order:
            copy(a, n, True).wait_recv()
        for a, n in order:
            copy(a, n, False).wait_send()

    return start, wait


class _Ride:
    def __init__(self, srcs, dsts, plan, dst_inits=None):
        self.srcs, self.dsts, self.plan, self.dst_inits, self.out = srcs, dsts, plan, dst_inits, None


def _riding_call(body, rides, *, name, in_specs, out_specs, out_shape, grid=(), scratch_shapes=(), **kw):
    single = not isinstance(out_specs, (list, tuple))
    out_specs = [out_specs] if single else list(out_specs)
    out_shape = [out_shape] if single else list(out_shape)
    n_in, n_out, n_scr = len(in_specs), len(out_specs), len(scratch_shapes)
    xin, xdsts, sems, aliases, layout = [], [], [], {}, []
    for ride in rides:
        srcs = ride.srcs() if callable(ride.srcs) else ride.srcs
        inits = ride.dst_inits() if callable(ride.dst_inits) else ride.dst_inits
        na, nrem = len(ride.dsts), len(ride.plan(0, 0, 0))
        src_at = len(xin) if srcs is not None else None
        xin += list(srcs) if srcs is not None else []
        if inits is not None:
            aliases.update({n_in + len(xin) + a: n_out + len(xdsts) + a for a in range(na)})
            xin += list(inits)
        layout.append((src_at, len(xdsts), na))
        xdsts += list(ride.dsts)
        sems += [pltpu.SemaphoreType.DMA((na * nrem,)), pltpu.SemaphoreType.DMA((na * nrem,))]

    def wrapped(*refs):
        ins, xins = refs[:n_in], refs[n_in:n_in + len(xin)]
        outs = refs[n_in + len(xin):n_in + len(xin) + n_out]
        xouts = refs[n_in + len(xin) + n_out:n_in + len(xin) + n_out + len(xdsts)]
        scr = refs[n_in + len(xin) + n_out + len(xdsts):]
        rounds = []
        for k, (ride, (src_at, dst_at, na)) in enumerate(zip(rides, layout)):
            dsts = xouts[dst_at:dst_at + na]
            srcs = dsts if src_at is None else xins[src_at:src_at + na]
            rounds.append(_copies(srcs, dsts, scr[n_scr + 2 * k], scr[n_scr + 2 * k + 1], ride.plan))
        ids = [pl.program_id(a) for a in range(len(grid))]
        first = functools.reduce(jnp.logical_and, [i == 0 for i in ids], True)
        last = functools.reduce(jnp.logical_and, [i == g - 1 for i, g in zip(ids, grid)], True)

        def start_all():
            for start, _ in rounds:
                start()

        def wait_all():
            for _, wait in rounds:
                wait()

        start_all() if not grid else pl.when(first)(start_all)
        body(*ins, *outs, *scr[:n_scr])
        wait_all() if not grid else pl.when(last)(wait_all)

    hbm = pl.BlockSpec(memory_space=pl.ANY)
    gridkw = dict(grid=grid) if grid else {}

    def run(*args):
        res = pl.pallas_call(
            wrapped, name=name, in_specs=list(in_specs) + [hbm] * len(xin),
            out_specs=out_specs + [hbm] * len(xdsts), out_shape=out_shape + xdsts,
            scratch_shapes=list(scratch_shapes) + sems, input_output_aliases=aliases,
            compiler_params=_params(("arbitrary",) * len(grid) if grid else None), **gridkw,
        )(*args, *xin)
        for ride, (_, dst_at, na) in zip(rides, layout):
            ride.out = list(res[n_out + dst_at:n_out + dst_at + na])
        return res[0] if single else list(res[:n_out])

    return run


def _pick(n, target, quantum=128):
    t = (min(target, n) // quantum) * quantum
    while t >= quantum:
        if n % t == 0:
            return t
        t -= quantum
    return n


def _mm(a, b, *, name, ta=False, tb=False, b3=False, out3=0, out_dtype=F32, add=None, tm=1024, tn=1536, tk=1536):
    if ta:
        K, M = a.shape
    else:
        M, K = a.shape
    if b3 and tb:
        Nn, K2, tk = b.shape[1], b.shape[0] * b.shape[2], b.shape[2]
    elif b3:
        K2, Nn, tn = b.shape[1], b.shape[0] * b.shape[2], b.shape[2]
    elif tb:
        Nn, K2 = b.shape
    else:
        K2, Nn = b.shape
    assert K == K2, (a.shape, b.shape)
    if out3:
        tn = Nn // out3
    tm, tn, tk = _pick(M, tm), _pick(Nn, tn), _pick(K, tk)
    if not (b3 and tb) and K <= MM_WHOLE_K:
        tk = K
        fits = lambda: 4 * tk * (tm + tn) + 8 * tm * tn * (2 if add is not None else 1) <= MM_TILE_BYTES
        while not fits():
            if (tm >= tn or b3 or out3) and tm > 256:
                tm = _pick(M, tm - 128)
            elif not (b3 or out3) and tn > 256:
                tn = _pick(Nn, tn - 128)
            else:
                break
    nk = K // tk
    j_outer = K * Nn + (Nn // tn) * M * K < M * K + (M // tm) * K * Nn
    dn = (((0 if ta else 1,), (1 if tb else 0,)), ((), ()))

    def body(*refs):
        refs = list(refs)
        a_ref, b_ref = refs[:2]
        add_ref = refs[2] if add is not None else None
        o_ref = refs[3] if add is not None else refs[2]
        part = lax.dot_general(a_ref[...].astype(BF16), b_ref[...].astype(BF16), dn, preferred_element_type=F32)

        def finish(r):
            if add is not None:
                r = r + add_ref[...]
            o_ref[...] = r.astype(out_dtype)

        if nk == 1:
            finish(part)
            return
        acc = refs[-1]
        k = pl.program_id(2)

        @pl.when(k == 0)
        def _():
            acc[...] = part

        @pl.when(k > 0)
        def _():
            acc[...] += part

        @pl.when(k == nk - 1)
        def _():
            finish(acc[...])

    def spec(shape, index):
        return pl.BlockSpec(shape, (lambda j, i, k: index(i, j, k)) if j_outer else index)

    a_spec = spec((tk, tm), lambda i, j, k: (k, i)) if ta else spec((tm, tk), lambda i, j, k: (i, k))
    if b3 and tb:
        b_spec = spec((None, tn, tk), lambda i, j, k: (k, j, 0))
    elif b3:
        b_spec = spec((None, tk, tn), lambda i, j, k: (j, k, 0))
    elif tb:
        b_spec = spec((tn, tk), lambda i, j, k: (j, k))
    else:
        b_spec = spec((tk, tn), lambda i, j, k: (k, j))
    if out3:
        o_spec = spec((None, tm, tn), lambda i, j, k: (j, i, 0))
    else:
        o_spec = spec((tm, tn), lambda i, j, k: (i, j))
    ins, specs = [a, b], [a_spec, b_spec]
    if add is not None:
        ins.append(add)
        specs.append(o_spec)
    grid = (Nn // tn, M // tm, nk) if j_outer else (M // tm, Nn // tn, nk)
    return _pcall(
        body, name=name, grid=grid, in_specs=specs, out_specs=o_spec,
        out_shape=jax.ShapeDtypeStruct((out3, M, tn) if out3 else (M, Nn), out_dtype),
        scratch_shapes=[pltpu.VMEM((tm, tn), F32)] if nk > 1 else [],
        compiler_params=_params(("parallel", "parallel", "arbitrary")),
    )(*ins)


def _mm_multi(a_list, b_list, *, name, M, T=None, add=None, out_dtype=F32, tm=512, post=None):
    tm = _pick(T or M, tm)
    ns = len(a_list)
    dils = [a[1] if isinstance(a, tuple) else 0 for a in a_list]
    a_arrs = [a[0] if isinstance(a, tuple) else a for a in a_list]
    widths = [a.shape[-1] // max(r, 1) for a, r in zip(a_arrs, dils)]
    b_arrs, b_specs = [], []
    for b in b_list:
        arr, shp, idx = b if isinstance(b, tuple) else (b, b.shape, (0, 0))
        b_arrs.append(arr)
        b_specs.append(pl.BlockSpec(shp, lambda i, idx=idx: idx))
    Nn = b_specs[0].block_shape[1]
    dn = (((1,), (0,)), ((), ()))
    nmm = 2 * ns + (1 if add is not None else 0)
    p_arrs, p_in_specs, p_out_specs, p_out_shape, p_fn = post or ([], [], None, None, None)
    nin = nmm + len(p_arrs)
    nout = len(p_out_specs) if post else 1

    def body(*refs):
        a_refs, b_refs, scr = refs[:ns], refs[ns:2 * ns], list(refs[nin + nout:])
        acc = None
        for a_ref, b_ref, r in zip(a_refs, b_refs, dils):
            av = _from_view(a_ref, scr.pop(0), r) if r > 1 else a_ref[...]
            part = lax.dot_general(av.astype(BF16), b_ref[...], dn, preferred_element_type=F32)
            acc = part if acc is None else acc + part
        if add is not None:
            acc = acc + refs[2 * ns][...]
        if post:
            p_fn(acc, refs[nmm:nin], refs[nin:nin + nout])
        else:
            refs[nin][...] = acc.astype(out_dtype)

    tpe = (T or M) // tm
    a_specs = [pl.BlockSpec((None, tm // r, r * w), lambda i: (i // tpe, i % tpe, 0)) if r
               else pl.BlockSpec((tm, w), lambda i: (i, 0)) for r, w in zip(dils, widths)]
    o_spec = pl.BlockSpec((tm, Nn), lambda i: (i, 0))
    specs = a_specs + b_specs
    ins = a_arrs + b_arrs
    if add is not None:
        specs.append(o_spec)
        ins.append(add)
    scratch = [pltpu.VMEM((w // LANES, tm, LANES), F32) for r, w in zip(dils, widths) if r > 1]
    return _pcall(body, name=name, grid=(M // tm,), in_specs=specs + list(p_in_specs),
                  out_specs=list(p_out_specs) if post else o_spec, scratch_shapes=scratch,
                  out_shape=list(p_out_shape) if post else jax.ShapeDtypeStruct((M, Nn), out_dtype),
                  compiler_params=_params(("arbitrary",) if post else ("parallel",)))(*ins, *p_arrs)


def _dw_view(d3, u, r, *, name, tk=1024):
    NB, tsub, rw = d3.shape
    W, T, D = rw // r, tsub * r, u.shape[1]
    tk = _pick(T, tk)
    tpe, nk = T // tk, NB * T // tk

    def body(d_ref, u_ref, o_ref, acc, scr):
        k = pl.program_id(0)
        dv = _from_view(d_ref, scr, r).astype(BF16)
        part = lax.dot_general(dv, u_ref[...], _TN, preferred_element_type=F32)

        @pl.when(k == 0)
        def _():
            acc[...] = part

        @pl.when(k > 0)
        def _():
            acc[...] += part

        @pl.when(k == nk - 1)
        def _():
            o_ref[...] = acc[...].astype(o_ref.dtype)

    return _pcall(
        body, name=name, grid=(nk,),
        in_specs=[pl.BlockSpec((None, tk // r, rw), lambda k: (k // tpe, k % tpe, 0)), pl.BlockSpec((tk, D), lambda k: (k, 0))],
        out_specs=pl.BlockSpec((W, D), lambda k: (0, 0)), out_shape=jax.ShapeDtypeStruct((W, D), BF16),
        scratch_shapes=[pltpu.VMEM((W, D), F32), pltpu.VMEM((W // LANES, tk, LANES), F32)],
        compiler_params=_params(("arbitrary",)))(d3, u)


def _lane(shape):
    return lax.broadcasted_iota(jnp.int32, shape, len(shape) - 1)


def _rot_half(v):
    w = v.shape[-1]
    first = (_lane(v.shape) % HEAD_DIM) < (HEAD_DIM // 2)
    return jnp.where(first, pltpu.roll(v, w - HEAD_DIM // 2, v.ndim - 1), pltpu.roll(v, HEAD_DIM // 2, v.ndim - 1))


def _widen(t, w):
    return t if w == t.shape[-1] else jnp.concatenate([t] * (w // t.shape[-1]), axis=-1)


def _unrope(v, cos, sins):
    w = v.shape[-1]
    return v * _widen(cos, w) - _rot_half(v) * _widen(sins, w)


def _rope_tables(positions):
    half = HEAD_DIM // 2
    inv = ROPE_THETA ** (-jnp.arange(half, dtype=F32) / half)
    ang = positions.astype(F32)[..., None] * inv
    cos, sin = jnp.cos(ang), jnp.sin(ang)
    cosf = jnp.concatenate([cos, cos, cos, cos], axis=-1)
    sins = jnp.concatenate([-sin, sin, -sin, sin], axis=-1)
    n = positions.shape[0] * positions.shape[1]
    return cosf.reshape(n, PAIR_W), sins.reshape(n, PAIR_W)


def _inproj(x2, scale, shift, w, cosf, sins, flags, *, T, name):
    N, D = x2.shape
    tm, tn = _pick(T, 512), VAR_W
    tpe = T // tm

    def body(x_ref, sc_ref, sh_ref, w_ref, c_ref, s_ref, f_ref, *outs):
        o_refs, u_ref = outs[:N_VAR], outs[N_VAR]
        j = pl.program_id(1)

        @pl.when(j == 0)
        def _():
            u_ref[...] = (x_ref[...] * (1.0 + sc_ref[0]) + sh_ref[0]).astype(BF16)

        acc = lax.dot_general(u_ref[...], w_ref[...], (((1,), (1,)), ((), ())), preferred_element_type=F32)
        fl = f_ref[...]
        ce = 1.0 + (_widen(c_ref[...], tn) - 1.0) * fl
        se = _widen(s_ref[...], tn) * fl
        res = acc * ce + _rot_half(acc) * se
        for v in range(N_VAR):
            @pl.when(j == v)
            def _(v=v):
                _to_view(res, o_refs[v], outs[N_VAR + 1], VAR_DIL[v])

    ex = pl.BlockSpec((1, 1, D), lambda i, j: (i // tpe, 0, 0))
    tab = pl.BlockSpec((tm, PAIR_W), lambda i, j: (i, 0))
    keep = lambda w_: pl.BlockSpec((tm, w_), lambda i, j: (i, 0))
    vspec = lambda r: pl.BlockSpec((None, tm // r, r * tn), lambda i, j: (i // tpe, i % tpe, 0))
    vshape = lambda r: jax.ShapeDtypeStruct((N // T, T // r, r * tn), BF16)
    return _pcall(
        body, name=name, grid=(N // tm, N_VAR),
        in_specs=[keep(D), ex, ex, pl.BlockSpec((tn, D), lambda i, j: (j, 0)), tab, tab,
                  pl.BlockSpec((1, tn), lambda i, j: (0, j))],
        out_specs=[vspec(r) for r in VAR_DIL] + [keep(D)],
        out_shape=[vshape(r) for r in VAR_DIL] + [jax.ShapeDtypeStruct((N, D), BF16)],
        scratch_shapes=[pltpu.VMEM((tn // LANES, tm, LANES), F32)],
        compiler_params=_params(("parallel", "arbitrary")),
    )(x2, scale, shift, w, cosf, sins, flags)


class _Geom:
    def __init__(self, g):
        if g is None:
            self.r, self.nq, self.n_back, self.sink = 1, A_Q_HEADS, A_WINDOW - 1, True
            self.qw, self.kw = QA_W, KA_W
            self.qidx = lambda j: 0
            self.kidx = lambda j: QA_W // KA_W
            self.vidx = lambda j: QA_W // KA_W + 1
        else:
            window, r = B_PATTERNS[g]
            self.r, self.nq, self.n_back, self.sink = r, B_GROUP_HEADS, window // r, False
            self.qw, self.kw = GB_W, GB_W
            self.qidx = lambda j: 3 * j
            self.kidx = lambda j: 3 * j + 1
            self.vidx = lambda j: 3 * j + 2
        self.ntile = self.qw // PAIR_W


def _stack_heads(t, scale=None):
    first = _lane(t.shape) < HEAD_DIM
    z = jnp.zeros_like(t)
    if scale is not None:
        t = t * jnp.asarray(scale, t.dtype)
    return jnp.concatenate([jnp.where(first, t, z), jnp.where(first, z, t)], axis=0)


def _lse_col(t):
    return jnp.concatenate([t[:, 0:1], t[:, HEAD_DIM:HEAD_DIM + 1]], axis=0)


def _lse_rows(t, width):
    first = _lane(t.shape) < HEAD_DIM
    other = pltpu.roll(t, HEAD_DIM, 1)
    full = jnp.concatenate([jnp.where(first, t, other), jnp.where(first, other, t)], axis=0)
    return _widen(full, width)


def _unstack_heads(v2):
    return jnp.where(_lane((BLOCK, PAIR_W)) < HEAD_DIM, v2[:BLOCK], v2[BLOCK:])


def _dup_head(t, kh):
    tf = t.astype(F32)
    keep = (_lane(t.shape) < HEAD_DIM) if kh == 0 else (_lane(t.shape) >= HEAD_DIM)
    return jnp.where(keep, tf, pltpu.roll(tf, HEAD_DIM, 1)).astype(t.dtype)


def _fold_heads(t):
    return t + pltpu.roll(t, HEAD_DIM, 1)


def _band_mask(rows, i, n_back, single):
    nkeys = BLOCK if single else 2 * BLOCK
    qi = jnp.bitwise_and(lax.broadcasted_iota(jnp.int32, (rows, nkeys), 0), BLOCK - 1)
    ki = lax.broadcasted_iota(jnp.int32, (rows, nkeys), 1)
    if single:
        return qi >= ki
    dist = qi + BLOCK - ki
    return jnp.logical_and(jnp.logical_and(dist >= 0, dist <= n_back), jnp.logical_or(ki >= BLOCK, i > 0))


def _per_block(col, scalars, fn):
    return jnp.concatenate([fn(col[b * BLOCK:(b + 1) * BLOCK], sc) for b, sc in enumerate(scalars)], axis=0)


def _sink_slot(rows):
    qi = jnp.bitwise_and(lax.broadcasted_iota(jnp.int32, (rows, 2 * BLOCK), 0), BLOCK - 1)
    return qi == lax.broadcasted_iota(jnp.int32, (rows, 2 * BLOCK), 1)


def _sink_scores(rows, sinks):
    blk = lax.broadcasted_iota(jnp.int32, (rows, 2 * BLOCK), 0) // BLOCK
    out = jnp.full((rows, 2 * BLOCK), sinks[-1], F32)
    for b in range(len(sinks) - 2, -1, -1):
        out = jnp.where(blk == b, sinks[b], out)
    return out


def _softmax_parts(s, valid, sinks):
    s = jnp.where(valid, s, NEG_INF)
    if sinks is not None:
        slot = _sink_slot(s.shape[0])
        s = jnp.where(slot, _sink_scores(s.shape[0], sinks), s)
    m = jnp.max(s, axis=1, keepdims=True)
    p = jnp.exp(s - m)
    den = jnp.sum(p, axis=1, keepdims=True)
    if sinks is not None:
        p = jnp.where(slot, 0.0, p)
    return p, m, den


_NT = (((1,), (1,)), ((), ()))
_TN = (((0,), (0,)), ((), ()))


def _rows2(prev_ref, cur_ref, cs, single=False):
    if single:
        return cur_ref[0, :, cs]
    return jnp.concatenate([prev_ref[0, :, cs], cur_ref[0, :, cs]], axis=0)


def _sink_scalars(sink_ref, first, nblocks):
    return [sink_ref[first + b] for b in range(nblocks)]


def _tile(t):
    return slice(t * PAIR_W, (t + 1) * PAIR_W)


def _attn_fwd(qkv, sinks, g, *, NB, T, name):
    geo = _Geom(g)
    r, qw, kw, ntile = geo.r, geo.qw, geo.kw, geo.ntile
    tsub = T // r
    nblk = tsub // BLOCK
    qkv3 = qkv.reshape(NB, tsub, r * VAR_W)
    out_dtype = BF16 if g is None else F32
    tiles_per_kv = ntile // A_KV_HEADS

    single = nblk == 1

    def body(q_ref, kp_ref, kc_ref, vp_ref, vc_ref, sink_ref, o_ref, l_ref):
        i = pl.program_id(2)
        if geo.sink:
            kall, vall = _rows2(kp_ref, kc_ref, _tile(0)), _rows2(vp_ref, vc_ref, _tile(0))
            kdup = [_dup_head(kall, kh) for kh in range(A_KV_HEADS)]
            vdup = [_dup_head(vall, kh) for kh in range(A_KV_HEADS)]
            tiles = [[t] for t in range(ntile)]
            q2s = [_stack_heads(q_ref[0, :, _tile(t)], SCALE) for t in range(ntile)]
            kks = [kdup[t // tiles_per_kv] for t in range(ntile)]
            vvs = [vdup[t // tiles_per_kv] for t in range(ntile)]
            sinkcols = [_sink_scalars(sink_ref, 2 * t, 2) for t in range(ntile)]
        else:
            tiles = [[t] for t in range(ntile)]
            q2s = [_stack_heads(q_ref[0, :, _tile(t)], SCALE) for t in range(ntile)]
            kks = [_rows2(kp_ref, kc_ref, _tile(t), single) for t in range(ntile)]
            vvs = [_rows2(vp_ref, vc_ref, _tile(t), single) for t in range(ntile)]
            sinkcols = [None] * ntile
        valid = _band_mask(q2s[0].shape[0], i, geo.n_back, single)
        ss = [lax.dot_general(q2, kk, _NT, preferred_element_type=F32) for q2, kk in zip(q2s, kks)]
        parts = [_softmax_parts(s, valid, sc) for s, sc in zip(ss, sinkcols)]
        o2s = [jnp.dot(p.astype(BF16), vv, preferred_element_type=F32) / den for (p, m, den), vv in zip(parts, vvs)]
        for ts, o2, (p, m, den) in zip(tiles, o2s, parts):
            lse2 = jnp.broadcast_to(m + jnp.log(den), (o2.shape[0], PAIR_W))
            for n, t in enumerate(ts):
                rows = slice(2 * BLOCK * n, 2 * BLOCK * (n + 1))
                o_ref[0, :, _tile(t)] = _unstack_heads(o2[rows]).astype(out_dtype)
                l_ref[0, :, _tile(t)] = _unstack_heads(lse2[rows])

    prev = lambda i: jnp.maximum(i - 1, 0)
    in_specs = [
        pl.BlockSpec((1, BLOCK, qw), lambda b, j, i: (b, i, geo.qidx(j))),
        pl.BlockSpec((1, BLOCK, kw), lambda b, j, i: (b, prev(i), geo.kidx(j))),
        pl.BlockSpec((1, BLOCK, kw), lambda b, j, i: (b, i, geo.kidx(j))),
        pl.BlockSpec((1, BLOCK, kw), lambda b, j, i: (b, prev(i), geo.vidx(j))),
        pl.BlockSpec((1, BLOCK, kw), lambda b, j, i: (b, i, geo.vidx(j))),
        pl.BlockSpec(memory_space=pltpu.SMEM),
    ]
    o_spec = pl.BlockSpec((1, BLOCK, qw), lambda b, j, i: (b, i, j))
    shape = (NB, tsub, r * qw)
    o, lse = _pcall(
        body, name=name, grid=(NB, r, nblk), in_specs=in_specs, out_specs=[o_spec, o_spec],
        out_shape=[jax.ShapeDtypeStruct(shape, out_dtype), jax.ShapeDtypeStruct(shape, F32)],
        compiler_params=_params(("parallel", "parallel", "arbitrary")),
    )(qkv3, qkv3, qkv3, qkv3, qkv3, sinks)
    return o, lse


def _attn_fwd_b(qkvs, *, NB, T, name):
    geos = [_Geom(g) for g in range(len(B_PATTERNS))]
    steps = T // BLOCK
    nt = GB_W // PAIR_W
    ng = len(geos)

    def where(geo, s):
        nblk = T // geo.r // BLOCK
        return s // steps, (s % steps) // nblk, (s % steps) % nblk

    def body(*refs):
        ins, outs = refs[:5 * ng], refs[5 * ng:]
        s = pl.program_id(0)
        q2s, kks, vvs, valids = [], [], [], []
        for n, geo in enumerate(geos):
            q_ref, kp_ref, kc_ref, vp_ref, vc_ref = ins[5 * n:5 * n + 5]
            single = T // geo.r // BLOCK == 1
            valid = _band_mask(2 * BLOCK, where(geo, s)[2], geo.n_back, single)
            for t in range(nt):
                q2s.append(_stack_heads(q_ref[0, :, _tile(t)], SCALE))
                kks.append(_rows2(kp_ref, kc_ref, _tile(t), single))
                vvs.append(_rows2(vp_ref, vc_ref, _tile(t), single))
                valids.append(valid)
        ss = [lax.dot_general(q2, kk, _NT, preferred_element_type=F32) for q2, kk in zip(q2s, kks)]
        parts = [_softmax_parts(sc, valid, None) for sc, valid in zip(ss, valids)]
        o2s = [jnp.dot(p.astype(BF16), vv, preferred_element_type=F32) / den for (p, m, den), vv in zip(parts, vvs)]
        for n in range(ng):
            o_ref, l_ref = outs[2 * n], outs[2 * n + 1]
            for t in range(nt):
                o2, (p, m, den) = o2s[n * nt + t], parts[n * nt + t]
                o_ref[0, :, _tile(t)] = _unstack_heads(o2)
                l_ref[0, :, _tile(t)] = _unstack_heads(jnp.broadcast_to(m + jnp.log(den), (2 * BLOCK, PAIR_W)))

    in_specs, ins, out_specs, out_shape = [], [], [], []
    for geo, qkv in zip(geos, qkvs):
        tsub = T // geo.r
        pos = lambda s, geo=geo: where(geo, s)
        prev = lambda i: jnp.maximum(i - 1, 0)
        blk = lambda col, back, pos=pos: pl.BlockSpec(
            (1, BLOCK, GB_W), lambda s: (pos(s)[0], prev(pos(s)[2]) if back else pos(s)[2], col(pos(s)[1])))
        in_specs += [blk(geo.qidx, False), blk(geo.kidx, True), blk(geo.kidx, False), blk(geo.vidx, True),
                     blk(geo.vidx, False)]
        ins += [qkv.reshape(NB, tsub, geo.r * VAR_W)] * 5
        out_specs += [blk(lambda j: j, False)] * 2
        out_shape += [jax.ShapeDtypeStruct((NB, tsub, geo.r * GB_W), F32)] * 2
    res = _pcall(body, name=name, grid=(NB * steps,), in_specs=in_specs, out_specs=out_specs, out_shape=out_shape,
                 compiler_params=_params(("arbitrary",)))(*ins)
    return [(res[2 * n], res[2 * n + 1]) for n in range(ng)]


def _attn_bwd(qkv, do, lse, dlse, cosf, sins, sinks, g, *, NB, T, name):
    geo = _Geom(g)
    r, qw, kw, ntile = geo.r, geo.qw, geo.kw, geo.ntile
    tsub = T // r
    nblk = tsub // BLOCK
    view = lambda a, w: a.reshape(NB, tsub, r * w)
    has_dlse = dlse is not None
    tiles_per_kv = ntile // A_KV_HEADS

    single = nblk == 1
    krows = BLOCK if single else 2 * BLOCK
    nsteps = 1 if single else nblk + 1

    def grads(q2s, kks, vvs, do2s, i, lserows, sinkcols, dlrows):
        nrow = q2s[0].shape[0]
        ki = lax.broadcasted_iota(jnp.int32, (krows, nrow), 0)
        qi = jnp.bitwise_and(lax.broadcasted_iota(jnp.int32, (krows, nrow), 1), BLOCK - 1)
        if single:
            valid = qi >= ki
        else:
            dist = qi + BLOCK - ki
            valid = jnp.logical_and(jnp.logical_and(dist >= 0, dist <= geo.n_back), jnp.logical_or(ki >= BLOCK, i > 0))
        sts = [lax.dot_general(kk, q2, _NT, preferred_element_type=F32) for q2, kk in zip(q2s, kks)]
        dpts = [lax.dot_general(vv, do2, _NT, preferred_element_type=F32) for do2, vv in zip(do2s, vvs)]
        pts, dsts, sks = [], [], []
        for st, dpt, ls, sc, dl in zip(sts, dpts, lserows, sinkcols, dlrows):
            sv = jnp.where(valid, st, NEG_INF)
            if sc is not None:
                slot = ki == qi
                blk = lax.broadcasted_iota(jnp.int32, (krows, nrow), 1) // BLOCK
                sink = jnp.full((krows, nrow), sc[-1], F32)
                for b in range(len(sc) - 2, -1, -1):
                    sink = jnp.where(blk == b, sc[b], sink)
                sv = jnp.where(slot, sink, sv)
                dpt = jnp.where(slot, 0.0, dpt)
            pt = jnp.exp(sv - ls)
            delta = jnp.sum(pt * dpt, axis=0, keepdims=True)
            if dl is not None:
                delta = delta - dl
            dst = pt * (dpt - delta)
            if sc is not None:
                cols = lambda a, b: a[:, b * BLOCK:(b + 1) * BLOCK]
                sks.append([jnp.sum(jnp.where(cols(slot, b), cols(dst, b), 0.0)) for b in range(len(sc))])
                dst, pt = jnp.where(slot, 0.0, dst), jnp.where(slot, 0.0, pt)
            else:
                sks.append(None)
            pts.append(pt.astype(BF16))
            dsts.append(dst.astype(BF16))
        dq2s = [lax.dot_general(dst, kk, _TN, preferred_element_type=F32) * SCALE for dst, kk in zip(dsts, kks)]
        dkks = [jnp.dot(dst, q2, preferred_element_type=F32) for dst, q2 in zip(dsts, q2s)]
        dvvs = [jnp.dot(pt, do2, preferred_element_type=F32) for pt, do2 in zip(pts, do2s)]
        return dq2s, dkks, dvvs, sks

    def stat_row(t):
        tt = t.T
        return jnp.concatenate([tt[0:1, :], tt[HEAD_DIM:HEAD_DIM + 1, :]], axis=1)

    def body(*refs):
        it = iter(refs)
        q_ref, kp_ref, kc_ref, vp_ref, vc_ref, do_ref, l_ref = (next(it) for _ in range(7))
        dl_ref = next(it) if has_dlse else None
        c_ref, s_ref, sink_ref, o_ref, ds_ref, dq_s, dk_s, dv_s, car_q, car_k, car_v = (next(it) for _ in range(11))
        b, j, i = pl.program_id(0), pl.program_id(1), pl.program_id(2)

        @pl.when(jnp.logical_and(b == 0, jnp.logical_and(j == 0, i == 0)))
        def _():
            ds_ref[...] = jnp.zeros_like(ds_ref)

        def compute():
            if geo.sink:
                kall, vall = _rows2(kp_ref, kc_ref, _tile(0)), _rows2(vp_ref, vc_ref, _tile(0))
                tps = tiles_per_kv // A_BWD_SPLIT
                nb = 2 * tps
                tiles = [[kh * tiles_per_kv + s_ * tps + t for t in range(tps)]
                         for kh in range(A_KV_HEADS) for s_ in range(A_BWD_SPLIT)]
                kdup = [_dup_head(kall, kh) for kh in range(A_KV_HEADS)]
                vdup = [_dup_head(vall, kh) for kh in range(A_KV_HEADS)]
                cat = lambda f, ts: jnp.concatenate([f(t) for t in ts], axis=0)
                dq2s, dkks, dvvs, sks = grads(
                    [cat(lambda t: _stack_heads(q_ref[0, :, _tile(t)], SCALE), ts) for ts in tiles],
                    [kdup[n // A_BWD_SPLIT] for n in range(len(tiles))],
                    [vdup[n // A_BWD_SPLIT] for n in range(len(tiles))],
                    [cat(lambda t: _stack_heads(do_ref[0, :, _tile(t)]), ts) for ts in tiles], i,
                    [jnp.concatenate([stat_row(l_ref[0, :, _tile(t)]) for t in ts], axis=1) for ts in tiles],
                    [_sink_scalars(sink_ref, 2 * ts[0], nb) for ts in tiles], [None] * len(tiles))
                lane1 = _lane((1, PAIR_W))
                dsink = jnp.zeros((1, PAIR_W), F32)
                for ts, dq2, sk in zip(tiles, dq2s, sks):
                    for n, t in enumerate(ts):
                        dq_s[:, _tile(t)] = _unstack_heads(dq2[2 * BLOCK * n:2 * BLOCK * (n + 1)])
                    for bb in range(nb):
                        dsink = dsink + jnp.where(lane1 == 2 * ts[0] + bb, sk[bb], 0.0)
                per_kv = lambda parts, kh: functools.reduce(jnp.add, parts[kh * A_BWD_SPLIT:(kh + 1) * A_BWD_SPLIT])
                second = _lane((krows, PAIR_W)) >= HEAD_DIM
                dk_s[...] = jnp.where(second, _fold_heads(per_kv(dkks, 1)), _fold_heads(per_kv(dkks, 0)))
                dv_s[...] = jnp.where(second, _fold_heads(per_kv(dvvs, 1)), _fold_heads(per_kv(dvvs, 0)))
                ds_ref[0:1, :] += dsink
            else:
                dq2s, dkks, dvvs, _ = grads(
                    [_stack_heads(q_ref[0, :, _tile(t)], SCALE) for t in range(ntile)],
                    [_rows2(kp_ref, kc_ref, _tile(t), single) for t in range(ntile)],
                    [_rows2(vp_ref, vc_ref, _tile(t), single) for t in range(ntile)],
                    [_stack_heads(do_ref[0, :, _tile(t)]) for t in range(ntile)], i,
                    [stat_row(l_ref[0, :, _tile(t)]) for t in range(ntile)], [None] * ntile,
                    [stat_row(dl_ref[0, :, _tile(t)]) for t in range(ntile)])
                for t in range(ntile):
                    dq_s[:, _tile(t)] = _unstack_heads(dq2s[t])
                    dk_s[0:krows, _tile(t)] = dkks[t]
                    dv_s[0:krows, _tile(t)] = dvvs[t]

        def emit(dq, dk, dv):
            cos, sn = c_ref[0], s_ref[0]
            o_ref[0, :, 0:qw] = _unrope(dq, cos, sn).astype(BF16)
            o_ref[0, :, qw:qw + kw] = _unrope(dk, cos, sn).astype(BF16)
            o_ref[0, :, qw + kw:qw + 2 * kw] = dv.astype(BF16)
            if qw + 2 * kw < VAR_W:
                o_ref[0, :, qw + 2 * kw:VAR_W] = jnp.zeros((BLOCK, VAR_W - qw - 2 * kw), BF16)

        if single:
            compute()
            emit(dq_s[...], dk_s[0:BLOCK, :], dv_s[0:BLOCK, :])
            return

        @pl.when(i == 0)
        def _():
            car_q[...] = jnp.zeros_like(car_q)
            car_k[...] = jnp.zeros_like(car_k)
            car_v[...] = jnp.zeros_like(car_v)

        @pl.when(i == nblk)
        def _():
            dk_s[...] = jnp.zeros_like(dk_s)
            dv_s[...] = jnp.zeros_like(dv_s)

        pl.when(i < nblk)(compute)
        emit(car_q[...], car_k[...] + dk_s[0:BLOCK, :], car_v[...] + dv_s[0:BLOCK, :])
        car_q[...] = dq_s[...]
        car_k[...] = dk_s[BLOCK:2 * BLOCK, :]
        car_v[...] = dv_s[BLOCK:2 * BLOCK, :]

    cur = lambda i: jnp.minimum(i, nblk - 1)
    prv = lambda i: jnp.maximum(jnp.minimum(i, nblk - 1) - 1, 0)
    outb = lambda i: jnp.maximum(i - 1, 0)
    qrow = pl.BlockSpec((1, BLOCK, qw), lambda b, j, i: (b, cur(i), j))
    in_specs = [
        pl.BlockSpec((1, BLOCK, qw), lambda b, j, i: (b, cur(i), geo.qidx(j))),
        pl.BlockSpec((1, BLOCK, kw), lambda b, j, i: (b, prv(i), geo.kidx(j))),
        pl.BlockSpec((1, BLOCK, kw), lambda b, j, i: (b, cur(i), geo.kidx(j))),
        pl.BlockSpec((1, BLOCK, kw), lambda b, j, i: (b, prv(i), geo.vidx(j))),
        pl.BlockSpec((1, BLOCK, kw), lambda b, j, i: (b, cur(i), geo.vidx(j))),
        qrow, qrow,
    ]
    ins = [view(qkv, VAR_W)] * 5 + [view(do, qw), view(lse, qw)]
    if has_dlse:
        in_specs.append(qrow)
        ins.append(view(dlse, qw))
    in_specs += [
        pl.BlockSpec((1, BLOCK, PAIR_W), lambda b, j, i: (b, outb(i), j)),
        pl.BlockSpec((1, BLOCK, PAIR_W), lambda b, j, i: (b, outb(i), j)),
        pl.BlockSpec(memory_space=pltpu.SMEM),
    ]
    ins += [view(cosf, PAIR_W), view(sins, PAIR_W), sinks]
    scratch = [pltpu.VMEM((BLOCK, qw), F32), pltpu.VMEM((2 * BLOCK, kw), F32), pltpu.VMEM((2 * BLOCK, kw), F32),
               pltpu.VMEM((BLOCK, qw), F32), pltpu.VMEM((BLOCK, kw), F32), pltpu.VMEM((BLOCK, kw), F32)]
    dqkv, dsink = _pcall(
        body, name=name, grid=(NB, r, nsteps), in_specs=in_specs,
        out_specs=[pl.BlockSpec((1, BLOCK, VAR_W), lambda b, j, i: (b, outb(i), j)),
                   pl.BlockSpec((8, PAIR_W), lambda b, j, i: (0, 0))],
        out_shape=[jax.ShapeDtypeStruct((NB, tsub, r * VAR_W), BF16), jax.ShapeDtypeStruct((8, PAIR_W), F32)],
        scratch_shapes=scratch, compiler_params=_params(("arbitrary", "arbitrary", "arbitrary")),
    )(*ins)
    return dqkv, dsink


class _Rows:
    def __init__(self, N, T, tm):
        self.N, self.tm, self.tpe, self.grid = N, tm, T // tm, (N // tm,)

    def row(self, w, col=0):
        return pl.BlockSpec((self.tm, w), lambda i: (i, col))

    def ex(self, w):
        return pl.BlockSpec((1, 1, w), lambda i: (i // self.tpe, 0, 0))

    def const(self, shape):
        return pl.BlockSpec(shape, lambda i: tuple(0 for _ in shape))

    def view(self, w, r):
        return pl.BlockSpec((None, self.tm // r, r * w), lambda i: (i // self.tpe, i % self.tpe, 0))

    def first_of_example(self):
        return pl.program_id(0) % self.tpe == 0


def _acc(ref, first, val):
    @pl.when(first)
    def _():
        ref[0] = val

    @pl.when(jnp.logical_not(first))
    def _():
        ref[0] += val


def _colsum(v):
    return jnp.sum(v, axis=0, keepdims=True)


def _ln_stats(r):
    mu = jnp.mean(r, axis=-1, keepdims=True)
    xc = r - mu
    var = jnp.mean(xc * xc, axis=-1, keepdims=True)
    rstd = lax.rsqrt(var + LN_EPS)
    return xc * rstd, rstd


def _ln_bwd(dy, xhat, rstd, gain):
    dxh = dy * gain
    return rstd * (dxh - jnp.mean(dxh, axis=-1, keepdims=True) - xhat * jnp.mean(dxh * xhat, axis=-1, keepdims=True))


def _from_view(ref, scr, r):
    if r == 1:
        return ref[...]
    rows, w = ref.shape[0], ref.shape[1] // r
    for j in range(r):
        for c in range(w // LANES):
            scr.at[c][pl.ds(j, rows, stride=r), :] = ref[:, j * w + c * LANES:j * w + (c + 1) * LANES].astype(F32)
    return jnp.concatenate([scr[c] for c in range(w // LANES)], axis=1)


def _to_view(val, ref, scr, r):
    if r == 1:
        ref[...] = val.astype(ref.dtype)
        return
    rows, w = ref.shape[0], ref.shape[1] // r
    for c in range(w // LANES):
        scr[c] = val[:, c * LANES:(c + 1) * LANES]
    for j in range(r):
        for c in range(w // LANES):
            ref[:, j * w + c * LANES:j * w + (c + 1) * LANES] = scr.at[c][pl.ds(j, rows, stride=r), :].astype(ref.dtype)


def _silu_parts(v):
    s = jax.nn.sigmoid(v)
    return v * s, s * (1.0 + v * (1.0 - s))


def _local_step(x, mod, positions, w_in, rest_weights, sinks, ln1_g, ln1_b, ln2_g, ln2_b, target, hook=None):
    hook = hook or (lambda event, **data: None)
    NB, T, D = x.shape
    N = NB * T
    x2 = x.reshape(N, D)
    tgt2 = target.reshape(N, D)
    shift_m, scale_m, gate_m, shift_f, scale_f, gate_f = [mod[:, None, k * D:(k + 1) * D] for k in range(6)]
    cosf, sins = _rope_tables(positions)
    col = jnp.arange(QKV_P)
    vcol = col % VAR_W
    flags = jnp.where(col < VAR_W, vcol < QA_W + KA_W, vcol < 2 * GB_W).astype(F32)[None]
    R = _Rows(N, T, _pick(T, 256))
    sds = jax.ShapeDtypeStruct
    exsum = lambda w=D: sds((NB, 1, w), F32)
    ngrp = len(B_PATTERNS)

    *qkv, u = _inproj(x2, scale_m, shift_m, w_in, cosf, sins, flags, T=T, name="inproj_qkv")
    gates = _mm(u, w_in[QKV_P:], tb=True, out_dtype=BF16, name="inproj_gates")
    oa, la = _attn_fwd(qkv[0], sinks, None, NB=NB, T=T, name="attn_a_fwd")
    oa = oa.reshape(N, QA_W)
    (o1, l1), (o2, l2), (o3, l3) = _attn_fwd_b(qkv[1:], NB=NB, T=T, name="attn_b_fwd")
    w_a, w_b, w_o, w_gu, w_d = rest_weights()
    F = w_d.shape[0]
    dil = [r_ for _, r_ in B_PATTERNS]
    views = [R.view(GB_W, r_) for r_ in dil]
    tokbuf = pltpu.VMEM((GB_W // LANES, R.tm, LANES), F32)

    def merge_fwd(o1r, o2r, o3r, l1r, l2r, l3r, ob_ref, *bufs):
        os_ = [_from_view(ref, bufs[n], dil[n]) for n, ref in enumerate((o1r, o2r, o3r))]
        la, lb, lc = [_from_view(ref, bufs[3 + n], dil[n]) for n, ref in enumerate((l1r, l2r, l3r))]
        mx = jnp.maximum(jnp.maximum(la, lb), lc)
        ea, eb, ec = jnp.exp(la - mx), jnp.exp(lb - mx), jnp.exp(lc - mx)
        ob_ref[...] = ((ea * os_[0] + eb * os_[1] + ec * os_[2]) / (ea + eb + ec)).astype(BF16)

    ob = _pcall(merge_fwd, name="merge_fwd", grid=R.grid, in_specs=views + views, out_specs=R.row(GB_W),
                out_shape=sds((N, GB_W), BF16), scratch_shapes=[tokbuf] * 6,
                compiler_params=_params(("parallel",)))(o1, o2, o3, l1, l2, l3)

    f32 = lambda ref: ref[...].astype(F32)
    Rm = _Rows(N, T, _pick(T, 512))

    def mix_out(oa_r, ob_r, ga_r, gb_r, x_r, gm_r, g_r, b_r, sf_r, hf_r, wa_r, wb_r, wo_r,
                ya_ref, yb_ref, mg_ref, y_ref, r1_ref, u2_ref):
        ya = jnp.dot(oa_r[...], wa_r[...], preferred_element_type=F32).astype(BF16)
        yb = jnp.concatenate([jnp.dot(ob_r[...], wb_r[s_], preferred_element_type=F32)
                              for s_ in range(w_b.shape[0])], axis=1).astype(BF16)
        merged = (jax.nn.sigmoid(f32(ga_r)) * ya.astype(F32) + jax.nn.sigmoid(f32(gb_r)) * yb.astype(F32)).astype(BF16)
        y = jnp.dot(merged, wo_r[...], preferred_element_type=F32)
        r1 = ALPHA * x_r[...] + (1.0 + gm_r[0]) * y
        xhat, _ = _ln_stats(r1)
        x1 = xhat * g_r[...] + b_r[...]
        ya_ref[...], yb_ref[...], mg_ref[...], y_ref[...], r1_ref[...] = ya, yb, merged, y, r1
        u2_ref[...] = (x1 * (1.0 + sf_r[0]) + hf_r[0]).astype(BF16)

    ya, yb, merged, y, r1, u2 = _pcall(
        mix_out, name="mix_out", grid=Rm.grid,
        in_specs=[Rm.row(QA_W), Rm.row(GB_W), Rm.row(D, 0), Rm.row(D, 1), Rm.row(D), Rm.ex(D), Rm.const((1, D)),
                  Rm.const((1, D)), Rm.ex(D), Rm.ex(D), Rm.const(w_a.shape), Rm.const(w_b.shape), Rm.const(w_o.shape)],
        out_specs=[Rm.row(D)] * 6,
        out_shape=[sds((N, D), BF16)] * 3 + [sds((N, D), F32)] * 2 + [sds((N, D), BF16)],
        compiler_params=_params(("parallel",)))(oa, ob, gates, gates, x2, gate_m, ln1_g, ln1_b, scale_f, shift_f,
                                                w_a, w_b, w_o)

    w_gu = w_gu() if callable(w_gu) else w_gu
    tnf = w_gu.shape[2]
    nft = w_gu.shape[0] // 2
    tmf = _pick(N, 512)

    def ffn_up(u_r, wg_r, wu_r, hg_ref, hu_ref, a_ref):
        hg = jnp.dot(u_r[...], wg_r[...], preferred_element_type=F32)
        hu = jnp.dot(u_r[...], wu_r[...], preferred_element_type=F32)
        sl, _ = _silu_parts(hg)
        hg_ref[...] = hg.astype(BF16)
        hu_ref[...] = hu.astype(BF16)
        a_ref[...] = (sl * hu).astype(BF16)

    ftile = pl.BlockSpec((tmf, tnf), lambda j, i: (i, j))
    hg, hu, act = _pcall(
        ffn_up, name="ffn_up", grid=(nft, N // tmf),
        in_specs=[pl.BlockSpec((tmf, D), lambda j, i: (i, 0)), pl.BlockSpec((None, D, tnf), lambda j, i: (j, 0, 0)),
                  pl.BlockSpec((None, D, tnf), lambda j, i: (j + nft, 0, 0))],
        out_specs=[ftile] * 3, out_shape=[sds((N, F), BF16)] * 3,
        compiler_params=_params(("arbitrary", "parallel")))(u2, w_gu, w_gu)
    def ffn_down_norm2(act_r, wd_r, r1_r, g1_r, b1_r, t_r, gf_r, g_r, b_r,
                       dy2_ref, dx1_ref, dgf_ref, dg_ref, db_ref, loss_ref):
        first = R.first_of_example()
        y2v = jnp.dot(act_r[...], wd_r[...], preferred_element_type=F32)
        x1 = _ln_stats(r1_r[...])[0] * g1_r[...] + b1_r[...]
        r2 = ALPHA * x1 + (1.0 + gf_r[0]) * y2v
        xhat, rstd = _ln_stats(r2)
        err = xhat * g_r[...] + b_r[...] - t_r[...]
        dx2 = err * (1.0 / D)
        dr2 = _ln_bwd(dx2, xhat, rstd, g_r[...])
        dy2_ref[...] = ((1.0 + gf_r[0]) * dr2).astype(BF16)
        dx1_ref[...] = ALPHA * dr2
        _acc(dgf_ref, first, _colsum(dr2 * y2v))
        _acc(dg_ref, first, _colsum(dx2 * xhat))
        _acc(db_ref, first, _colsum(dx2))
        part = 0.5 * jnp.sum(jnp.mean(err * err, axis=-1, keepdims=True))
        _acc(loss_ref, first, jnp.broadcast_to(part, (1, 128)))

    dy2, dx1p, dgate_f, dg2, db2, loss_p = _pcall(
        ffn_down_norm2, name="ffn_down_norm2", grid=R.grid,
        in_specs=[R.row(F), R.const((F, D)), R.row(D), R.const((1, D)), R.const((1, D)), R.row(D), R.ex(D),
                  R.const((1, D)), R.const((1, D))],
        out_specs=[R.row(D), R.row(D), R.ex(D), R.ex(D), R.ex(D), R.ex(128)],
        out_shape=[sds((N, D), BF16), sds((N, D), F32), exsum(), exsum(), exsum(), exsum(128)],
        compiler_params=_params(("arbitrary",)))(act, w_d, r1, ln1_g, ln1_b, tgt2, gate_f, ln2_g, ln2_b)

    g_wd = _mm(act, dy2, ta=True, out_dtype=BF16, name="ffn_down_dw")

    tmd = _pick(N, 256)

    fchunk = _pick(F, 768)

    def ffn_down_dx(dy_r, wd_r, hg_r, hu_r, dh_ref):
        for t in range(F // fchunk):
            cs = slice(t * fchunk, (t + 1) * fchunk)
            da = lax.dot_general(dy_r[...], wd_r[cs, :], _NT, preferred_element_type=F32)
            sl, dsl = _silu_parts(hg_r[:, cs].astype(F32))
            dh_ref[:, cs] = (da * hu_r[:, cs].astype(F32) * dsl).astype(BF16)
            dh_ref[:, F + t * fchunk:F + (t + 1) * fchunk] = (da * sl).astype(BF16)

    rowd = lambda w_: pl.BlockSpec((tmd, w_), lambda i: (i, 0))
    dh = _pcall(
        ffn_down_dx, name="ffn_down_dx", grid=(N // tmd,),
        in_specs=[rowd(D), pl.BlockSpec((F, D), lambda i: (0, 0)), rowd(F), rowd(F)],
        out_specs=rowd(2 * F), out_shape=sds((N, 2 * F), BF16),
        compiler_params=_params(("parallel",)))(dy2, w_d, hg, hu)
    g_wgu = _mm(u2, dh, ta=True, out3=w_gu.shape[0], out_dtype=BF16, name="ffn_up_dw")

    def ffn_up_dx_norm1(dh_r, w_r, dx1p_r, r1_r, y_r, sf_r, gm_r, g_r, b_r,
                        dxp_ref, dy_ref, dsf_ref, dhf_ref, dgm_ref, dg_ref, db_ref):
        first = R.first_of_example()
        du2v = None
        for s_ in range(w_gu.shape[0]):
            part = lax.dot_general(dh_r[:, s_ * tnf:(s_ + 1) * tnf], w_r[s_], _NT, preferred_element_type=F32)
            du2v = part if du2v is None else du2v + part
        dx1 = dx1p_r[...] + du2v * (1.0 + sf_r[0])
        xhat, rstd = _ln_stats(r1_r[...])
        dr1 = _ln_bwd(dx1, xhat, rstd, g_r[...])
        dxp_ref[...] = ALPHA * dr1
        dy_ref[...] = ((1.0 + gm_r[0]) * dr1).astype(BF16)
        _acc(dsf_ref, first, _colsum(du2v * (xhat * g_r[...] + b_r[...])))
        _acc(dhf_ref, first, _colsum(du2v))
        _acc(dgm_ref, first, _colsum(dr1 * y_r[...]))
        _acc(dg_ref, first, _colsum(dx1 * xhat))
        _acc(db_ref, first, _colsum(dx1))

    dxp, dy, dscale_f, dshift_f, dgate_m, dg1, db1 = _pcall(
        ffn_up_dx_norm1, name="ffn_up_dx_norm1", grid=R.grid,
        in_specs=[R.row(2 * F), R.const(w_gu.shape)] + [R.row(D)] * 3 + [R.ex(D), R.ex(D), R.const((1, D)),
                                                                        R.const((1, D))],
        out_specs=[R.row(D), R.row(D)] + [R.ex(D)] * 5,
        out_shape=[sds((N, D), F32), sds((N, D), BF16)] + [exsum()] * 5,
        compiler_params=_params(("arbitrary",)))(dh, w_gu, dx1p, r1, y, scale_f, gate_m, ln1_g, ln1_b)

    g_wo = _mm(merged, dy, ta=True, out_dtype=BF16, name="out_proj_dw")

    def mix_out_bwd(dy_r, ya_r, yb_r, ga_r, gb_r, wo_r, wa_r, wb_r, dya_ref, dyb_ref, dg_ref, doa_ref, dob_ref):
        dm = lax.dot_general(dy_r[...], wo_r[...], _NT, preferred_element_type=F32).astype(BF16).astype(F32)
        sa, sb = jax.nn.sigmoid(f32(ga_r)), jax.nn.sigmoid(f32(gb_r))
        dya, dyb = (dm * sa).astype(BF16), (dm * sb).astype(BF16)
        dya_ref[...], dyb_ref[...] = dya, dyb
        dg_ref[:, :D] = (dm * f32(ya_r) * sa * (1.0 - sa)).astype(BF16)
        dg_ref[:, D:] = (dm * f32(yb_r) * sb * (1.0 - sb)).astype(BF16)
        doa_ref[...] = lax.dot_general(dya, wa_r[...], _NT, preferred_element_type=F32).astype(BF16)
        ds_ = D // w_b.shape[0]
        dob = None
        for s_ in range(w_b.shape[0]):
            part = lax.dot_general(dyb[:, s_ * ds_:(s_ + 1) * ds_], wb_r[s_], _NT, preferred_element_type=F32)
            dob = part if dob is None else dob + part
        dob_ref[...] = dob

    dya, dyb, dgates, doa, dob = _pcall(
        mix_out_bwd, name="mix_out_bwd", grid=Rm.grid,
        in_specs=[Rm.row(D)] * 3 + [Rm.row(D, 0), Rm.row(D, 1), Rm.const(w_o.shape), Rm.const(w_a.shape),
                                    Rm.const(w_b.shape)],
        out_specs=[Rm.row(D), Rm.row(D), Rm.row(2 * D), Rm.row(QA_W), Rm.row(GB_W)],
        out_shape=[sds((N, D), BF16), sds((N, D), BF16), sds((N, 2 * D), BF16), sds((N, QA_W), BF16), sds((N, GB_W), F32)],
        compiler_params=_params(("parallel",)))(dy, ya, yb, gates, gates, w_o, w_a, w_b)

    g_wa = _mm(oa, dya, ta=True, out_dtype=BF16, name="branch_a_dw")
    g_wb = _mm(ob, dyb, ta=True, out3=w_b.shape[0], out_dtype=BF16, name="branch_b_dw")
    hook("rest_grads", g_wa=g_wa, g_wb=g_wb, g_wo=g_wo, g_wgu=g_wgu, g_wd=g_wd)

    seg = (jnp.arange(GB_W)[:, None] // HEAD_DIM == jnp.arange(GB_W)[None, :] // HEAD_DIM).astype(BF16)

    def merge_bwd(dob_r, o1r, o2r, o3r, l1r, l2r, l3r, seg_r, d1, d2, d3, e1, e2, e3, *bufs):
        dob_v = dob_r[...]
        os_ = [_from_view(ref, bufs[n], dil[n]) for n, ref in enumerate((o1r, o2r, o3r))]
        la, lb, lc = [_from_view(ref, bufs[3 + n], dil[n]) for n, ref in enumerate((l1r, l2r, l3r))]
        mx = jnp.maximum(jnp.maximum(la, lb), lc)
        ea, eb, ec = jnp.exp(la - mx), jnp.exp(lb - mx), jnp.exp(lc - mx)
        inv = 1.0 / (ea + eb + ec)
        ws = [ea * inv, eb * inv, ec * inv]

        def headsum(v):
            hi = v.astype(BF16)
            r1_ = v - hi.astype(F32)
            mid = r1_.astype(BF16)
            lo = (r1_ - mid.astype(F32)).astype(BF16)
            sm = seg_r[...]
            return (jnp.dot(hi, sm, preferred_element_type=F32) + jnp.dot(mid, sm, preferred_element_type=F32)
                    + jnp.dot(lo, sm, preferred_element_type=F32))

        dws = [headsum(dob_v * o) for o in os_]
        mean = ws[0] * dws[0] + ws[1] * dws[1] + ws[2] * dws[2]
        for n, (w_, dw_, d_ref, e_ref) in enumerate(zip(ws, dws, (d1, d2, d3), (e1, e2, e3))):
            _to_view(w_ * dob_v, d_ref, bufs[6], dil[n])
            _to_view(w_ * (dw_ - mean), e_ref, bufs[7], dil[n])

    vshape = lambda r_, dt: sds((NB, T // r_, r_ * GB_W), dt)
    mb = _pcall(
        merge_bwd, name="merge_bwd", grid=R.grid, in_specs=[R.row(GB_W)] + views + views + [R.const((GB_W, GB_W))],
        out_specs=views + views, out_shape=[vshape(r_, BF16) for r_ in dil] + [vshape(r_, F32) for r_ in dil],
        scratch_shapes=[tokbuf] * 8, compiler_params=_params(("parallel",)))(dob, o1, o2, o3, l1, l2, l3, seg)
    do_b, dlse_b = mb[:3], mb[3:]
    hook("merge_bwd_done")

    dqkv_a, dsink = _attn_bwd(qkv[0], doa, la, None, cosf, sins, sinks, None, NB=NB, T=T, name="attn_a_bwd")
    hook("attn_a_bwd_done")
    dqkv = [dqkv_a]
    for g in range(ngrp):
        dqkv.append(_attn_bwd(qkv[1 + g], do_b[g], (l1, l2, l3)[g], dlse_b[g], cosf, sins, sinks, g, NB=NB, T=T,
                              name=f"attn_b{g}_bwd")[0])
        hook(f"attn_b{g}_bwd_done")

    g_win = [_mm(d3.reshape(N, VAR_W), u, ta=True, out_dtype=BF16, name=f"inproj_dw{v}") if VAR_DIL[v] == 1
             else _dw_view(d3, u, VAR_DIL[v], name=f"inproj_dw{v}") for v, d3 in enumerate(dqkv)]
    g_win.append(_mm(dgates, u, ta=True, out_dtype=BF16, name=f"inproj_dw{N_VAR}"))
    hook("win_grads", g_win=g_win)
    wvar = lambda v: (w_in, (VAR_W, D), (v, 0))
    dview = lambda v: (dqkv[v], VAR_DIL[v])
    du = _mm_multi([dview(0)], [wvar(0)], M=N, T=T, name="inproj_dx0")
    hook("inproj_dx0_done")
    def x_bwd(duv, ins, outs):
        (dxp_r, x_r, sm_r), (gx_ref, dsm_ref, dhm_ref) = ins, outs
        first = R.first_of_example()
        gx_ref[...] = dxp_r[...] + duv * (1.0 + sm_r[0])
        _acc(dsm_ref, first, _colsum(duv * x_r[...]))
        _acc(dhm_ref, first, _colsum(duv))

    gx, dscale_m, dshift_m = _mm_multi(
        [dview(v) for v in range(1, N_VAR)] + [dgates],
        [wvar(v) for v in range(1, N_VAR)] + [(w_in, (2 * D, D), (QKV_P // (2 * D), 0))],
        M=N, T=T, add=du, tm=R.tm, name="inproj_dx1",
        post=([dxp, x2, scale_m], [R.row(D), R.row(D), R.ex(D)], [R.row(D), R.ex(D), R.ex(D)],
              [sds((N, D), F32), exsum(), exsum()], x_bwd))
    hook("inproj_dx1_done")

    dmod =jnp.concatenate([dshift_m, dscale_m, dgate_m, dshift_f, dscale_f, dgate_f], axis=-1)[:, 0]
    ln_grads = jnp.concatenate([dg1, db1, dg2, db2], axis=1)
    return dict(loss=loss_p[:, 0, 0], grad_x=gx.reshape(NB, T, D), g_win=g_win, g_wa=g_wa, g_wb=g_wb, g_wo=g_wo,
                g_wgu=g_wgu, g_wd=g_wd, dmod=dmod, ln_grads=ln_grads, dsink=dsink[0, :A_Q_HEADS])


def _coords():
    return lax.axis_index("x"), lax.axis_index("y"), lax.axis_index("c")


def _allgather_small(blk, *, name):
    m_per, n = blk.shape

    def body(x_ref, out_ref, send_sems, recv_sems, local_sem):
        x, y, c = _coords()
        me, sibling = (x, y, c), (x, y, 1 - c)
        chips = [(1 - x, y), (x, 1 - y), (1 - x, 1 - y)]

        def rows(px, py, pc):
            return out_ref.at[pl.ds((4 * px + 2 * py + pc) * m_per, m_per), :]

        def copy(k, block, to, src=None):
            return pltpu.make_async_remote_copy(
                src_ref=rows(*block) if src is None else src, dst_ref=rows(*block),
                send_sem=send_sems.at[k], recv_sem=recv_sems.at[k], device_id=to, device_id_type=MESH)

        mine = pltpu.make_async_copy(x_ref, rows(*me), local_sem)
        mine.start()
        first = [copy(0, me, sibling, src=x_ref)]
        first += [copy(1 + j, me, (*chip, c), src=x_ref) for j, chip in enumerate(chips)]
        for cp in first:
            cp.start()
        passed = [copy(4 + j, (*chip, c), sibling) for j, chip in enumerate(chips)]
        for j, chip in enumerate(chips):
            copy(1 + j, (*chip, c), me).wait_recv()
            passed[j].start()
        copy(0, sibling, me).wait_recv()
        for j, chip in enumerate(chips):
            copy(4 + j, (*chip, 1 - c), me).wait_recv()
        for cp in first + passed:
            cp.wait_send()
        mine.wait()

    return _pcall(
        body, name=name, out_shape=jax.ShapeDtypeStruct((8 * m_per, n), blk.dtype),
        in_specs=[pl.BlockSpec(memory_space=pltpu.VMEM)], out_specs=pl.BlockSpec(memory_space=pltpu.VMEM),
        scratch_shapes=[pltpu.SemaphoreType.DMA((7,)), pltpu.SemaphoreType.DMA((7,)), pltpu.SemaphoreType.DMA],
        compiler_params=pltpu.CompilerParams(vmem_limit_bytes=VMEM_LIMIT_BYTES),
    )(blk)


def _exchange(srcs, dsts, plan, *, name, dst_inits=None):
    na = len(dsts)
    nrem = len(plan(0, 0, 0))

    def body(*refs):
        refs = list(refs)
        src_refs = [refs.pop(0) for _ in range(na)] if srcs is not None else None
        if dst_inits is not None:
            del refs[:na]
        dst_refs, (send_sems, recv_sems) = refs[:na], refs[na:]
        start, wait = _copies(dst_refs if src_refs is None else src_refs, dst_refs, send_sems, recv_sems, plan)
        start()
        wait()

    hbm = pl.BlockSpec(memory_space=pl.ANY)
    ins = (list(srcs) if srcs is not None else []) + (list(dst_inits) if dst_inits is not None else [])
    base = na if srcs is not None else 0
    aliases = {base + a: a for a in range(na)} if dst_inits is not None else {}
    return _pcall(
        body, name=name, out_shape=list(dsts), in_specs=[hbm] * len(ins), out_specs=[hbm] * na,
        input_output_aliases=aliases,
        scratch_shapes=[pltpu.SemaphoreType.DMA((na * nrem,)), pltpu.SemaphoreType.DMA((na * nrem,))],
    )(*ins)


def _other_chips(x, y):
    return [(1 - x, y), (x, 1 - y), (1 - x, 1 - y)]


def _round(ride, carrier, name):
    if carrier is not None:
        _RIDES.setdefault(carrier, []).append(ride)
        return
    srcs = ride.srcs() if callable(ride.srcs) else ride.srcs
    inits = ride.dst_inits() if callable(ride.dst_inits) else ride.dst_inits
    ride.out = list(_exchange(srcs, ride.dsts, ride.plan, name=name, dst_inits=inits))


class _Gather:
    def __init__(self, shards, chip, tag, carriers=(None, None)):
        def plan_ici(x, y, c):
            k = 2 * x + y
            return [((c,), (k, c), (2 * px + py, c), (px, py, c)) for px, py in _other_chips(x, y)]

        def plan_d2d(x, y, c):
            return [((2 * px + py, c), (2 * px + py, c), (2 * px + py, 1 - c), (x, y, 1 - c))
                    for px, py in _other_chips(x, y)]

        def plan_near(x, y, c):
            k = 2 * x + y
            return [((c,), (k, c), (2 * px + py, c), (px, py, c)) for px, py in ((1 - x, y), (x, 1 - y))]

        def plan_far(x, y, c):
            kx, ky, kd = 2 * (1 - x) + y, 2 * x + (1 - y), 2 * (1 - x) + (1 - y)
            hp = shards[0].shape[1] // 2
            top, bottom = pl.ds(0, hp), pl.ds(hp, hp)
            return [((kx, c, top), (kx, c, top), (kd, c, top), (x, 1 - y, c)),
                    ((ky, c, bottom), (ky, c, bottom), (kd, c, bottom), (1 - x, y, c))]

        self.shards, self.chip = shards, chip
        dsts = [jax.ShapeDtypeStruct((4,) + s.shape, s.dtype) for s in shards]
        if len(carriers) == 3:
            near = _Ride(shards, dsts, plan_near)
            ici = _Ride(None, dsts, plan_far, dst_inits=lambda: near.out)
            _round(near, carriers[0], f"gather_{tag}_near")
            _round(ici, carriers[1], f"gather_{tag}_far")
        else:
            ici = _Ride(shards, dsts, plan_ici)
            _round(ici, carriers[0], f"gather_{tag}_ici")
        self.d2d = _Ride(None, dsts, plan_d2d, dst_inits=lambda: ici.out)
        _round(self.d2d, carriers[-1], f"gather_{tag}_d2d")

    def result(self):
        full = [lax.dynamic_update_index_in_dim(f, s, self.chip, 0) for f, s in zip(self.d2d.out, self.shards)]
        return [f.reshape((4, 2 * f.shape[2], f.shape[3])) for f in full]


def _index_operand(i):
    return jnp.reshape(i, (1,)).astype(jnp.int32)


def _add_pairs(g, f, ci, *, name):
    s, _, hr, wd = g.shape
    tr = _pick(hr, 600, 16)

    def body(c_ref, a_ref, b_ref, o_ref):
        o_ref[...] = (a_ref[...].astype(F32) + b_ref[...].astype(F32)).astype(BF16)

    spec = pl.BlockSpec((1, tr, wd), lambda j, i, c: (j, i, 0))
    grid_spec = pltpu.PrefetchScalarGridSpec(
        num_scalar_prefetch=1, grid=(s, hr // tr),
        in_specs=[pl.BlockSpec((1, None, tr, wd), lambda j, i, c: (j, c[0], i, 0)), spec], out_specs=spec)
    return _pcall(body, name=name, grid_spec=grid_spec, out_shape=jax.ShapeDtypeStruct(f.shape, BF16),
                  compiler_params=_params(("parallel", "parallel")))(_index_operand(ci), g, f)


def _sum_chips(landed, pairs, chip, *, name):
    s, hr, wd = landed.shape
    tr = _pick(hr, 600, 16)

    def body(k_ref, l_ref, p_ref, o_ref):
        acc = None
        for k in range(s):
            part = jnp.where(k_ref[0] == k, p_ref[k], l_ref[k]).astype(F32)
            acc = part if acc is None else acc + part
        o_ref[...] = acc

    spec = pl.BlockSpec((s, tr, wd), lambda i, k: (0, i, 0))
    grid_spec = pltpu.PrefetchScalarGridSpec(
        num_scalar_prefetch=1, grid=(hr // tr,), in_specs=[spec, spec],
        out_specs=pl.BlockSpec((tr, wd), lambda i, k: (i, 0)))
    return _pcall(body, name=name, grid_spec=grid_spec, out_shape=jax.ShapeDtypeStruct((hr, wd), F32),
                  compiler_params=_params(("parallel",)))(_index_operand(chip), landed, pairs)


class _ReduceScatter:
    def __init__(self, gs, chip, ci, tag):
        self.gs, self.chip, self.ci, self.tag = gs, chip, ci, tag
        self.half_t = [jax.ShapeDtypeStruct((g.shape[0],) + g.shape[2:], BF16) for g in gs]

    def pair(self, carrier=None):
        plan = lambda x, y, c: [((slice(None), 1 - c), (), (), (x, y, 1 - c))]
        self.r1 = _Ride(self.gs, self.half_t, plan)
        _round(self.r1, carrier, f"reduce_{self.tag}_pair")

    def chips(self, carrier=None):
        def plan(x, y, c):
            k = 2 * x + y
            return [((2 * px + py,), (k,), (2 * px + py,), (px, py, c)) for px, py in _other_chips(x, y)]

        self.pairs = [_add_pairs(g, f, self.ci, name=f"reduce_{self.tag}_pair_add{n}")
                      for n, (g, f) in enumerate(zip(self.gs, self.r1.out))]
        self.r2 = _Ride(self.pairs, self.half_t, plan)
        _round(self.r2, carrier, f"reduce_{self.tag}_chips")

    def halves(self, carrier=None):
        plan = lambda x, y, c: [((), (c,), (1 - c,), (x, y, 1 - c))]
        self.mine = [_sum_chips(l, p, self.chip, name=f"reduce_{self.tag}_chip_sum{n}")
                     for n, (l, p) in enumerate(zip(self.r2.out, self.pairs))]
        self.r3 = _Ride(self.mine, [jax.ShapeDtypeStruct((2,) + m.shape, F32) for m in self.mine], plan)
        _round(self.r3, carrier, f"reduce_{self.tag}_halves")

    def result(self):
        return [lax.dynamic_update_index_in_dim(b, m, self.ci, 0).reshape(2 * m.shape[0], m.shape[1])
                for b, m in zip(self.r3.out, self.mine)]


def _ada_fwd(c_all, w_sh, b_sh, *, name):
    nb, d = c_all.shape
    wcols = w_sh.shape[1]
    tn = _pick(wcols, 512)

    def body(c_ref, w_ref, b_ref, o_ref, a_ref):
        cv = c_ref[...]
        act = cv * jax.nn.sigmoid(cv)
        a_ref[...] = act
        o_ref[...] = jnp.dot(act.astype(BF16), w_ref[...].astype(BF16), preferred_element_type=F32) + b_ref[...]

    return _pcall(
        body, name=name, grid=(wcols // tn,),
        in_specs=[pl.BlockSpec((nb, d), lambda j: (0, 0)), pl.BlockSpec((d, tn), lambda j: (0, j)),
                  pl.BlockSpec((1, tn), lambda j: (0, j))],
        out_specs=[pl.BlockSpec((nb, tn), lambda j: (0, j)), pl.BlockSpec((nb, d), lambda j: (0, 0))],
        out_shape=[jax.ShapeDtypeStruct((nb, wcols), F32), jax.ShapeDtypeStruct((nb, d), F32)],
        compiler_params=_params(("arbitrary",)))(c_all, w_sh, b_sh)


def _sum_devices(g, *, name):
    nd, m, w = g.shape

    def body(g_ref, o_ref):
        acc = g_ref[0]
        for k in range(1, nd):
            acc = acc + g_ref[k]
        o_ref[...] = acc

    return _pcall(body, name=name, out_shape=jax.ShapeDtypeStruct((m, w), F32),
                  compiler_params=pltpu.CompilerParams(vmem_limit_bytes=VMEM_LIMIT_BYTES))(g)


def _adamw(w, g, m, v, *, name):
    rows, cols = w.shape[-2:]
    tr = _pick(rows, max(8, (1 << 18) // cols), 8)
    c1 = 1.0 / (1.0 - ADAM_B1 ** ADAM_STEP)
    c2 = 1.0 / (1.0 - ADAM_B2 ** ADAM_STEP)

    def body(w_ref, g_ref, m_ref, v_ref, d_ref, nm_ref, nv_ref):
        gv = g_ref[...]
        nm = ADAM_B1 * m_ref[...] + (1.0 - ADAM_B1) * gv
        nv = ADAM_B2 * v_ref[...] + (1.0 - ADAM_B2) * (gv * gv)
        d_ref[...] = -ADAM_LR * ((nm * c1) / (jnp.sqrt(nv * c2) + ADAM_EPS) + ADAM_WD * w_ref[...])
        nm_ref[...] = nm
        nv_ref[...] = nv

    gspec = pl.BlockSpec((tr, cols), lambda i: (i, 0))
    spec = pl.BlockSpec((None, tr, cols), lambda i: (0, i, 0)) if w.ndim == 3 else gspec
    shp = jax.ShapeDtypeStruct(w.shape, F32)
    return _pcall(body, name=name, grid=(rows // tr,), in_specs=[spec, gspec, spec, spec], out_specs=[spec] * 3,
                  out_shape=[shp] * 3, compiler_params=_params(("parallel",)))(w, g, m, v)


def _permute_in_rows(wt):
    ngrp = len(B_PATTERNS)
    qb, kb, vb = (wt[A_W + n * QB_W:A_W + (n + 1) * QB_W] for n in range(3))
    parts = [wt[:A_W], jnp.zeros((VAR_W - A_W, wt.shape[1]), wt.dtype)]
    for g in range(ngrp):
        parts += [t[g * GB_W:(g + 1) * GB_W] for t in (qb, kb, vb)]
    return jnp.concatenate(parts + [wt[A_W + 3 * QB_W:]], axis=0)


def _unpermute_in_grads(pieces):
    ga, groups, gg = pieces[0], pieces[1:-1], pieces[-1]
    rows = [ga[:A_W]]
    for n in range(3):
        rows += [gp[n * GB_W:(n + 1) * GB_W] for gp in groups]
    return jnp.concatenate(rows + [gg], axis=0)


def kernel(x, c, positions, w_ada, b_ada, w_in, sinks, w_branch_a, w_branch_b, w_o, ln1_g, ln1_b, w_gate_up, w_down, ln2_g, ln2_b, loss_target, m_w_ada, m_b_ada, m_w_in, m_sinks, m_w_branch_a, m_w_branch_b, m_w_o, m_ln1_g, m_ln1_b, m_w_gate_up, m_w_down, m_ln2_g, m_ln2_b, v_w_ada, v_b_ada, v_w_in, v_sinks, v_w_branch_a, v_w_branch_b, v_w_o, v_ln1_g, v_ln1_b, v_w_gate_up, v_w_down, v_ln2_g, v_ln2_b):
    xi, yi, ci = _coords()
    chip = 2 * xi + yi
    dev = 4 * xi + 2 * yi + ci
    NB, T, D = x.shape
    nchip, ndev = 4, 8
    ada_cols = w_ada.shape[2]

    ra, ro, rd = w_branch_a.shape[1], w_o.shape[1], w_down.shape[1]
    rowsh = jnp.concatenate([w_branch_a[0], w_o[0], w_down[0]], axis=0)
    halves = lambda a: a.reshape(a.shape[:-2] + (2, a.shape[-2] // 2, a.shape[-1]))
    tr = lambda a: jnp.swapaxes(a, -1, -2)
    shards = [halves(w.astype(BF16)) for w in (tr(w_in[0]), rowsh, w_branch_b[0], w_gate_up[0])]
    gin = _Gather(shards[:1], chip, "w_in", carriers=("gather_c", "ada_fwd", "gather_mod"))

    c_blk = jnp.zeros((8, D), F32).at[:NB].set(c)
    c_all = _allgather_small(c_blk, name="gather_c").reshape(ndev, 8, D)[:, :NB].reshape(ndev * NB, D)
    b_sh = lax.dynamic_slice(b_ada, (0, chip * ada_cols), (1, ada_cols))
    mod_part, c_act = _ada_fwd(c_all, w_ada[0], b_sh, name="ada_fwd")
    mod_g = _allgather_small(mod_part, name="gather_mod").reshape(nchip, 2, ndev * NB, ada_cols)[:, 0]
    mod_all = jnp.transpose(mod_g, (1, 0, 2)).reshape(ndev * NB, nchip * ada_cols)
    mod = lax.dynamic_slice(mod_all, (NB * dev, 0), (NB, nchip * ada_cols))

    (g_in,) = gin.result()
    w_in_f = _permute_in_rows(g_in.reshape(nchip * g_in.shape[1], D))
    mix = _Gather(shards[1:3], chip, "w_mix", carriers=("inproj_qkv", "attn_a_fwd"))
    ffn = _Gather(shards[3:], chip, "w_ffn", carriers=("attn_a_fwd", "attn_b_fwd", "merge_fwd"))

    def rest_weights():
        g_rows, w_b_f = mix.result()
        return (g_rows[:, :ra].reshape(nchip * ra, D), w_b_f, g_rows[:, ra:ra + ro].reshape(nchip * ro, D),
                lambda: ffn.result()[0], g_rows[:, ra + ro:].reshape(nchip * rd, D))

    red = {}

    def hook(event, **g):
        if event == "rest_grads":
            gr_rows = jnp.concatenate([g["g_wa"].reshape(nchip, ra, D), g["g_wo"].reshape(nchip, ro, D),
                                       g["g_wd"].reshape(nchip, rd, D)], axis=1)
            red["ffn"] = _ReduceScatter([halves(g["g_wgu"])], chip, ci, "ffn")
            red["mix"] = _ReduceScatter([halves(gr_rows), halves(g["g_wb"])], chip, ci, "mix")
            red["ffn"].pair(carrier="merge_bwd")
            red["mix"].pair(carrier="merge_bwd")
        elif event == "merge_bwd_done":
            red["ffn"].chips(carrier="attn_a_bwd")
            red["mix"].chips(carrier="attn_b0_bwd")
        elif event == "attn_a_bwd_done":
            red["ffn"].halves(carrier="attn_b0_bwd")
        elif event == "attn_b0_bwd_done":
            red["mix"].halves(carrier="attn_b1_bwd")
        elif event == "win_grads":
            gr_in = _unpermute_in_grads(g["g_win"])
            red["w_in"] = _ReduceScatter([halves(gr_in.reshape(nchip, gr_in.shape[0] // nchip, D))], chip, ci, "w_in")
            red["w_in"].pair(carrier="inproj_dx0")
        elif event == "inproj_dx0_done":
            red["w_in"].chips(carrier="inproj_dx1")
        elif event == "inproj_dx1_done":
            red["w_in"].halves(carrier="gather_small")

    res = _local_step(x, mod, positions, w_in_f, rest_weights, sinks[0], ln1_g, ln1_b, ln2_g, ln2_b, loss_target, hook)
    (g_rows_red, g_w_b), (g_w_gu,) = red["mix"].result(), red["ffn"].result()
    g_w_a, g_w_o, g_w_d = g_rows_red[:ra], g_rows_red[ra:ra + ro], g_rows_red[ra + ro:]

    small_rows = 24
    misc = jnp.zeros((1, D), F32).at[0, :A_Q_HEADS].set(res["dsink"]).at[0, A_Q_HEADS].set(jnp.sum(res["loss"]))
    small = jnp.concatenate([res["dmod"].reshape(NB * 6, D), jnp.sum(res["ln_grads"], axis=0), misc,
                             jnp.zeros((small_rows - NB * 6 - 5, D), F32)], axis=0)
    small_all = _allgather_small(small, name="gather_small").reshape(ndev, small_rows, D)
    (g_w_in,) = red["w_in"].result()
    dmod_all = small_all[:, :NB * 6].reshape(ndev * NB, 6 * D)
    sums = _sum_devices(small_all, name="sum_small")
    g_b_ada = (sums[0:6] + sums[6:12]).reshape(1, 6 * D)
    g_ln1_g, g_ln1_b, g_ln2_g, g_ln2_b = (sums[12 + n][None] for n in range(4))
    g_sinks = sums[16, :A_Q_HEADS][None]
    loss = sums[16, A_Q_HEADS]
    dmod_sh = lax.dynamic_slice(dmod_all, (0, chip * ada_cols), (ndev * NB, ada_cols))
    g_w_ada = _mm(c_act, dmod_sh, ta=True, name="ada_dw")

    names = ["w_ada", "b_ada", "w_in", "sinks", "w_branch_a", "w_branch_b", "w_o", "ln1_g", "ln1_b",
             "w_gate_up", "w_down", "ln2_g", "ln2_b"]
    ws = [w_ada, b_ada, w_in, sinks, w_branch_a, w_branch_b, w_o, ln1_g, ln1_b, w_gate_up, w_down, ln2_g, ln2_b]
    ms = [m_w_ada, m_b_ada, m_w_in, m_sinks, m_w_branch_a, m_w_branch_b, m_w_o, m_ln1_g, m_ln1_b, m_w_gate_up,
          m_w_down, m_ln2_g, m_ln2_b]
    vs = [v_w_ada, v_b_ada, v_w_in, v_sinks, v_w_branch_a, v_w_branch_b, v_w_o, v_ln1_g, v_ln1_b, v_w_gate_up,
          v_w_down, v_ln2_g, v_ln2_b]
    gs = [g_w_ada, g_b_ada, g_w_in, g_sinks, g_w_a, g_w_b, g_w_o, g_ln1_g, g_ln1_b, g_w_gu, g_w_d, g_ln2_g, g_ln2_b]
    grads, deltas, new_ms, new_vs = [], [], [], []
    for name, w, g, m, v in zip(names, ws, gs, ms, vs):
        flip = tr if name == "w_in" else (lambda a: a)
        w, m, v = flip(w), flip(m), flip(v)
        g2 = g.reshape(w.shape[-2:])
        d, nm, nv = _adamw(w, g2, m, v, name="adamw_" + name)
        grads.append(flip(g2.reshape(w.shape)))
        deltas.append(flip(d))
        new_ms.append(flip(nm))
        new_vs.append(flip(nv))
    return (loss, res["grad_x"], *grads, *deltas, *new_ms, *new_vs)
```

```python
import functools

import jax
import jax.numpy as jnp
from jax import lax
from jax.experimental import pallas as pl
from jax.experimental.pallas import tpu as pltpu

F32 = jnp.float32
BF16 = jnp.bfloat16
MESH = pl.DeviceIdType.MESH

HEAD_DIM = 64
LANES = 128
PAIR_W = 2 * HEAD_DIM
BLOCK = 128
A_Q_HEADS = 16
A_KV_HEADS = 2
A_WINDOW = 128
B_PATTERNS = ((128, 1), (512, 4), (2048, 16))
B_GROUP_HEADS = 8
QA_W = A_Q_HEADS * HEAD_DIM
KA_W = A_KV_HEADS * HEAD_DIM
GB_W = B_GROUP_HEADS * HEAD_DIM
QB_W = GB_W * len(B_PATTERNS)
A_W = QA_W + 2 * KA_W
VAR_W = 3 * GB_W
N_VAR = 1 + len(B_PATTERNS)
VAR_DIL = (1,) + tuple(r for _, r in B_PATTERNS)
ROPE_COLS = (QA_W + KA_W,) + (2 * GB_W,) * len(B_PATTERNS)
A_BWD_SPLIT = 4
QKV_P = N_VAR * VAR_W
ROPE_THETA = 10000.0
LN_EPS = 1e-5
NEG_INF = -1e30
DEPTH = 1
ALPHA = (2 * DEPTH) ** 0.25
SCALE = HEAD_DIM ** -0.5

ADAM_LR, ADAM_B1, ADAM_B2, ADAM_EPS, ADAM_WD, ADAM_STEP = 0.001, 0.9, 0.999, 1e-08, 0.01, 10

VMEM_LIMIT_BYTES = 56 * 1024 * 1024
MM_TILE_BYTES = 36 * 1024 * 1024
MM_WHOLE_K = 4096


def _params(sem=None):
    return pltpu.CompilerParams(dimension_semantics=sem, vmem_limit_bytes=VMEM_LIMIT_BYTES)


_RIDES = {}


def _pcall(body, *, name, **kw):
    rides = _RIDES.pop(name, None)
    if rides is None:
        return pl.pallas_call(body, name=name, **kw)
    return _riding_call(body, rides, name=name, **kw)


def _copies(src_refs, dst_refs, send_sems, recv_sems, plan):
    x, y, c = lax.axis_index("x"), lax.axis_index("y"), lax.axis_index("c")
    remote = plan(x, y, c)
    nrem = len(remote)
    at = lambda ref, idx: ref.at[idx] if idx else ref

    def copy(a, n, landing):
        si, di, ri, peer = remote[n]
        return pltpu.make_async_remote_copy(
            src_ref=at(src_refs[a], si), dst_ref=at(dst_refs[a], ri if landing else di),
            send_sem=send_sems.at[a * nrem + n], recv_sem=recv_sems.at[a * nrem + n],
            device_id=peer, device_id_type=MESH)

    order = [(a, n) for a in range(len(dst_refs)) for n in range(nrem)]

    def start():
        for a, n in order:
            copy(a, n, False).start()

    def wait():
        for a, n in order:
            copy(a, n, True).wait_recv()
        for a, n in order:
            copy(a, n, False).wait_send()

    return start, wait


class _Ride:
    def __init__(self, srcs, dsts, plan, dst_inits=None):
        self.srcs, self.dsts, self.plan, self.dst_inits, self.out = srcs, dsts, plan, dst_inits, None


def _riding_call(body, rides, *, name, in_specs, out_specs, out_shape, grid=(), scratch_shapes=(), **kw):
    single = not isinstance(out_specs, (list, tuple))
    out_specs = [out_specs] if single else list(out_specs)
    out_shape = [out_shape] if single else list(out_shape)
    n_in, n_out, n_scr = len(in_specs), len(out_specs), len(scratch_shapes)
    xin, xdsts, sems, aliases, layout = [], [], [], {}, []
    for ride in rides:
        srcs = ride.srcs() if callable(ride.srcs) else ride.srcs
        inits = ride.dst_inits() if callable(ride.dst_inits) else ride.dst_inits
        na, nrem = len(ride.dsts), len(ride.plan(0, 0, 0))
        src_at = len(xin) if srcs is not None else None
        xin += list(srcs) if srcs is not None else []
        if inits is not None:
            aliases.update({n_in + len(xin) + a: n_out + len(xdsts) + a for a in range(na)})
            xin += list(inits)
        layout.append((src_at, len(xdsts), na))
        xdsts += list(ride.dsts)
        sems += [pltpu.SemaphoreType.DMA((na * nrem,)), pltpu.SemaphoreType.DMA((na * nrem,))]

    def wrapped(*refs):
        ins, xins = refs[:n_in], refs[n_in:n_in + len(xin)]
        outs = refs[n_in + len(xin):n_in + len(xin) + n_out]
        xouts = refs[n_in + len(xin) + n_out:n_in + len(xin) + n_out + len(xdsts)]
        scr = refs[n_in + len(xin) + n_out + len(xdsts):]
        rounds = []
        for k, (ride, (src_at, dst_at, na)) in enumerate(zip(rides, layout)):
            dsts = xouts[dst_at:dst_at + na]
            srcs = dsts if src_at is None else xins[src_at:src_at + na]
            rounds.append(_copies(srcs, dsts, scr[n_scr + 2 * k], scr[n_scr + 2 * k + 1], ride.plan))
        ids = [pl.program_id(a) for a in range(len(grid))]
        first = functools.reduce(jnp.logical_and, [i == 0 for i in ids], True)
        last = functools.reduce(jnp.logical_and, [i == g - 1 for i, g in zip(ids, grid)], True)

        def start_all():
            for start, _ in rounds:
                start()

        def wait_all():
            for _, wait in rounds:
                wait()

        start_all() if not grid else pl.when(first)(start_all)
        body(*ins, *outs, *scr[:n_scr])
        wait_all() if not grid else pl.when(last)(wait_all)

    hbm = pl.BlockSpec(memory_space=pl.ANY)
    gridkw = dict(grid=grid) if grid else {}

    def run(*args):
        res = pl.pallas_call(
            wrapped, name=name, in_specs=list(in_specs) + [hbm] * len(xin),
            out_specs=out_specs + [hbm] * len(xdsts), out_shape=out_shape + xdsts,
            scratch_shapes=list(scratch_shapes) + sems, input_output_aliases=aliases,
            compiler_params=_params(("arbitrary",) * len(grid) if grid else None), **gridkw,
        )(*args, *xin)
        for ride, (_, dst_at, na) in zip(rides, layout):
            ride.out = list(res[n_out + dst_at:n_out + dst_at + na])
        return res[0] if single else list(res[:n_out])

    return run


def _pick(n, target, quantum=128):
    t = (min(target, n) // quantum) * quantum
    while t >= quantum:
        if n % t == 0:
            return t
        t -= quantum
    return n


def _mm(a, b, *, name, ta=False, tb=False, b3=False, out3=0, out_dtype=F32, add=None, tm=1024, tn=1536, tk=1536):
    if ta:
        K, M = a.shape
    else:
        M, K = a.shape
    if b3 and tb:
        Nn, K2, tk = b.shape[1], b.shape[0] * b.shape[2], b.shape[2]
    elif b3:
        K2, Nn, tn = b.shape[1], b.shape[0] * b.shape[2], b.shape[2]
    elif tb:
        Nn, K2 = b.shape
    else:
        K2, Nn = b.shape
    assert K == K2, (a.shape, b.shape)
    if out3:
        tn = Nn // out3
    tm, tn, tk = _pick(M, tm), _pick(Nn, tn), _pick(K, tk)
    if not (b3 and tb) and K <= MM_WHOLE_K:
        tk = K
        fits = lambda: 4 * tk * (tm + tn) + 8 * tm * tn * (2 if add is not None else 1) <= MM_TILE_BYTES
        while not fits():
            if (tm >= tn or b3 or out3) and tm > 256:
                tm = _pick(M, tm - 128)
            elif not (b3 or out3) and tn > 256:
                tn = _pick(Nn, tn - 128)
            else:
                break
    nk = K // tk
    j_outer = K * Nn + (Nn // tn) * M * K < M * K + (M // tm) * K * Nn
    dn = (((0 if ta else 1,), (1 if tb else 0,)), ((), ()))

    def body(*refs):
        refs = list(refs)
        a_ref, b_ref = refs[:2]
        add_ref = refs[2] if add is not None else None
        o_ref = refs[3] if add is not None else refs[2]
        part = lax.dot_general(a_ref[...].astype(BF16), b_ref[...].astype(BF16), dn, preferred_element_type=F32)

        def finish(r):
            if add is not None:
                r = r + add_ref[...]
            o_ref[...] = r.astype(out_dtype)

        if nk == 1:
            finish(part)
            return
        acc = refs[-1]
        k = pl.program_id(2)

        @pl.when(k == 0)
        def _():
            acc[...] = part

        @pl.when(k > 0)
        def _():
            acc[...] += part

        @pl.when(k == nk - 1)
        def _():
            finish(acc[...])

    def spec(shape, index):
        return pl.BlockSpec(shape, (lambda j, i, k: index(i, j, k)) if j_outer else index)

    a_spec = spec((tk, tm), lambda i, j, k: (k, i)) if ta else spec((tm, tk), lambda i, j, k: (i, k))
    if b3 and tb:
        b_spec = spec((None, tn, tk), lambda i, j, k: (k, j, 0))
    elif b3:
        b_spec = spec((None, tk, tn), lambda i, j, k: (j, k, 0))
    elif tb:
        b_spec = spec((tn, tk), lambda i, j, k: (j, k))
    else:
        b_spec = spec((tk, tn), lambda i, j, k: (k, j))
    if out3:
        o_spec = spec((None, tm, tn), lambda i, j, k: (j, i, 0))
    else:
        o_spec = spec((tm, tn), lambda i, j, k: (i, j))
    ins, specs = [a, b], [a_spec, b_spec]
    if add is not None:
        ins.append(add)
        specs.append(o_spec)
    grid = (Nn // tn, M // tm, nk) if j_outer else (M // tm, Nn // tn, nk)
    return _pcall(
        body, name=name, grid=grid, in_specs=specs, out_specs=o_spec,
        out_shape=jax.ShapeDtypeStruct((out3, M, tn) if out3 else (M, Nn), out_dtype),
        scratch_shapes=[pltpu.VMEM((tm, tn), F32)] if nk > 1 else [],
        compiler_params=_params(("parallel", "parallel", "arbitrary")),
    )(*ins)


def _mm_multi(a_list, b_list, *, name, M, T=None, add=None, out_dtype=F32, tm=512, post=None):
    tm = _pick(T or M, tm)
    ns = len(a_list)
    dils = [a[1] if isinstance(a, tuple) else 0 for a in a_list]
    a_arrs = [a[0] if isinstance(a, tuple) else a for a in a_list]
    widths = [a.shape[-1] // max(r, 1) for a, r in zip(a_arrs, dils)]
    b_arrs, b_specs = [], []
    for b in b_list:
        arr, shp, idx = b if isinstance(b, tuple) else (b, b.shape, (0, 0))
        b_arrs.append(arr)
        b_specs.append(pl.BlockSpec(shp, lambda i, idx=idx: idx))
    Nn = b_specs[0].block_shape[1]
    dn = (((1,), (0,)), ((), ()))
    nmm = 2 * ns + (1 if add is not None else 0)
    p_arrs, p_in_specs, p_out_specs, p_out_shape, p_fn = post or ([], [], None, None, None)
    nin = nmm + len(p_arrs)
    nout = len(p_out_specs) if post else 1

    def body(*refs):
        a_refs, b_refs, scr = refs[:ns], refs[ns:2 * ns], list(refs[nin + nout:])
        acc = None
        for a_ref, b_ref, r in zip(a_refs, b_refs, dils):
            av = _from_view(a_ref, scr.pop(0), r) if r > 1 else a_ref[...]
            part = lax.dot_general(av.astype(BF16), b_ref[...], dn, preferred_element_type=F32)
            acc = part if acc is None else acc + part
        if add is not None:
            acc = acc + refs[2 * ns][...]
        if post:
            p_fn(acc, refs[nmm:nin], refs[nin:nin + nout])
        else:
            refs[nin][...] = acc.astype(out_dtype)

    tpe = (T or M) // tm
    a_specs = [pl.BlockSpec((None, tm // r, r * w), lambda i: (i // tpe, i % tpe, 0)) if r
               else pl.BlockSpec((tm, w), lambda i: (i, 0)) for r, w in zip(dils, widths)]
    o_spec = pl.BlockSpec((tm, Nn), lambda i: (i, 0))
    specs = a_specs + b_specs
    ins = a_arrs + b_arrs
    if add is not None:
        specs.append(o_spec)
        ins.append(add)
    scratch = [pltpu.VMEM((w // LANES, tm, LANES), F32) for r, w in zip(dils, widths) if r > 1]
    return _pcall(body, name=name, grid=(M // tm,), in_specs=specs + list(p_in_specs),
                  out_specs=list(p_out_specs) if post else o_spec, scratch_shapes=scratch,
                  out_shape=list(p_out_shape) if post else jax.ShapeDtypeStruct((M, Nn), out_dtype),
                  compiler_params=_params(("arbitrary",) if post else ("parallel",)))(*ins, *p_arrs)


def _dw_view(d3, u, r, *, name, tk=1024):
    NB, tsub, rw = d3.shape
    W, T, D = rw // r, tsub * r, u.shape[1]
    tk = _pick(T, tk)
    tpe, nk = T // tk, NB * T // tk

    def body(d_ref, u_ref, o_ref, acc, scr):
        k = pl.program_id(0)
        dv = _from_view(d_ref, scr, r).astype(BF16)
        part = lax.dot_general(dv, u_ref[...], _TN, preferred_element_type=F32)

        @pl.when(k == 0)
        def _():
            acc[...] = part

        @pl.when(k > 0)
        def _():
            acc[...] += part

        @pl.when(k == nk - 1)
        def _():
            o_ref[...] = acc[...].astype(o_ref.dtype)

    return _pcall(
        body, name=name, grid=(nk,),
        in_specs=[pl.BlockSpec((None, tk // r, rw), lambda k: (k // tpe, k % tpe, 0)), pl.BlockSpec((tk, D), lambda k: (k, 0))],
        out_specs=pl.BlockSpec((W, D), lambda k: (0, 0)), out_shape=jax.ShapeDtypeStruct((W, D), BF16),
        scratch_shapes=[pltpu.VMEM((W, D), F32), pltpu.VMEM((W // LANES, tk, LANES), F32)],
        compiler_params=_params(("arbitrary",)))(d3, u)


def _lane(shape):
    return lax.broadcasted_iota(jnp.int32, shape, len(shape) - 1)


def _rot_half(v):
    w = v.shape[-1]
    first = (_lane(v.shape) % HEAD_DIM) < (HEAD_DIM // 2)
    return jnp.where(first, pltpu.roll(v, w - HEAD_DIM // 2, v.ndim - 1), pltpu.roll(v, HEAD_DIM // 2, v.ndim - 1))


def _widen(t, w):
    return t if w == t.shape[-1] else jnp.concatenate([t] * (w // t.shape[-1]), axis=-1)


def _unrope(v, cos, sins):
    w = v.shape[-1]
    return v * _widen(cos, w) - _rot_half(v) * _widen(sins, w)


def _rope_tables(positions):
    half = HEAD_DIM // 2
    inv = ROPE_THETA ** (-jnp.arange(half, dtype=F32) / half)
    ang = positions.astype(F32)[..., None] * inv
    cos, sin = jnp.cos(ang), jnp.sin(ang)
    cosf = jnp.concatenate([cos, cos, cos, cos], axis=-1)
    sins = jnp.concatenate([-sin, sin, -sin, sin], axis=-1)
    n = positions.shape[0] * positions.shape[1]
    return cosf.reshape(n, PAIR_W), sins.reshape(n, PAIR_W)


def _inproj(x2, scale, shift, w, cosf, sins, *, T, name):
    N, D = x2.shape
    tm, tn = _pick(T, 512), VAR_W
    tpe = T // tm

    def body(x_ref, sc_ref, sh_ref, w_ref, c_ref, s_ref, *outs):
        o_refs, u_ref = outs[:N_VAR], outs[N_VAR]
        j = pl.program_id(1)

        @pl.when(j == 0)
        def _():
            u_ref[...] = (x_ref[...] * (1.0 + sc_ref[0]) + sh_ref[0]).astype(BF16)

        acc = lax.dot_general(u_ref[...], w_ref[...], (((1,), (1,)), ((), ())), preferred_element_type=F32)
        for v in range(N_VAR):
            @pl.when(j == v)
            def _(v=v):
                nrope = ROPE_COLS[v]
                head = acc[:, :nrope]
                head = head * _widen(c_ref[...], nrope) + _rot_half(head) * _widen(s_ref[...], nrope)
                _to_view(jnp.concatenate([head, acc[:, nrope:]], axis=1), o_refs[v], outs[N_VAR + 1], VAR_DIL[v])

    ex = pl.BlockSpec((1, 1, D), lambda i, j: (i // tpe, 0, 0))
    tab = pl.BlockSpec((tm, PAIR_W), lambda i, j: (i, 0))
    keep = lambda w_: pl.BlockSpec((tm, w_), lambda i, j: (i, 0))
    vspec = lambda r: pl.BlockSpec((None, tm // r, r * tn), lambda i, j: (i // tpe, i % tpe, 0))
    vshape = lambda r: jax.ShapeDtypeStruct((N // T, T // r, r * tn), BF16)
    return _pcall(
        body, name=name, grid=(N // tm, N_VAR),
        in_specs=[keep(D), ex, ex, pl.BlockSpec((tn, D), lambda i, j: (j, 0)), tab, tab],
        out_specs=[vspec(r) for r in VAR_DIL] + [keep(D)],
        out_shape=[vshape(r) for r in VAR_DIL] + [jax.ShapeDtypeStruct((N, D), BF16)],
        scratch_shapes=[pltpu.VMEM((tn // LANES, tm, LANES), F32)],
        compiler_params=_params(("parallel", "arbitrary")),
    )(x2, scale, shift, w, cosf, sins)


class _Geom:
    def __init__(self, g):
        if g is None:
            self.r, self.nq, self.n_back, self.sink = 1, A_Q_HEADS, A_WINDOW - 1, True
            self.qw, self.kw = QA_W, KA_W
            self.qidx = lambda j: 0
            self.kidx = lambda j: QA_W // KA_W
            self.vidx = lambda j: QA_W // KA_W + 1
        else:
            window, r = B_PATTERNS[g]
            self.r, self.nq, self.n_back, self.sink = r, B_GROUP_HEADS, window // r, False
            self.qw, self.kw = GB_W, GB_W
            self.qidx = lambda j: 3 * j
            self.kidx = lambda j: 3 * j + 1
            self.vidx = lambda j: 3 * j + 2
        self.ntile = self.qw // PAIR_W


def _stack_heads(t, scale=None):
    first = _lane(t.shape) < HEAD_DIM
    z = jnp.zeros_like(t)
    if scale is not None:
        t = t * jnp.asarray(scale, t.dtype)
    return jnp.concatenate([jnp.where(first, t, z), jnp.where(first, z, t)], axis=0)


def _lse_col(t):
    return jnp.concatenate([t[:, 0:1], t[:, HEAD_DIM:HEAD_DIM + 1]], axis=0)


def _lse_rows(t, width):
    first = _lane(t.shape) < HEAD_DIM
    other = pltpu.roll(t, HEAD_DIM, 1)
    full = jnp.concatenate([jnp.where(first, t, other), jnp.where(first, other, t)], axis=0)
    return _widen(full, width)


def _unstack_heads(v2):
    return jnp.where(_lane((BLOCK, PAIR_W)) < HEAD_DIM, v2[:BLOCK], v2[BLOCK:])


def _dup_head(t, kh):
    tf = t.astype(F32)
    keep = (_lane(t.shape) < HEAD_DIM) if kh == 0 else (_lane(t.shape) >= HEAD_DIM)
    return jnp.where(keep, tf, pltpu.roll(tf, HEAD_DIM, 1)).astype(t.dtype)


def _fold_heads(t):
    return t + pltpu.roll(t, HEAD_DIM, 1)


def _band_mask(rows, i, n_back, single):
    nkeys = BLOCK if single else 2 * BLOCK
    qi = jnp.bitwise_and(lax.broadcasted_iota(jnp.int32, (rows, nkeys), 0), BLOCK - 1)
    ki = lax.broadcasted_iota(jnp.int32, (rows, nkeys), 1)
    if single:
        return qi >= ki
    dist = qi + BLOCK - ki
    return jnp.logical_and(jnp.logical_and(dist >= 0, dist <= n_back), jnp.logical_or(ki >= BLOCK, i > 0))


def _per_block(col, scalars, fn):
    return jnp.concatenate([fn(col[b * BLOCK:(b + 1) * BLOCK], sc) for b, sc in enumerate(scalars)], axis=0)


def _sink_slot(rows):
    qi = jnp.bitwise_and(lax.broadcasted_iota(jnp.int32, (rows, 2 * BLOCK), 0), BLOCK - 1)
    return qi == lax.broadcasted_iota(jnp.int32, (rows, 2 * BLOCK), 1)


def _sink_scores(rows, sinks):
    blk = lax.broadcasted_iota(jnp.int32, (rows, 2 * BLOCK), 0) // BLOCK
    out = jnp.full((rows, 2 * BLOCK), sinks[-1], F32)
    for b in range(len(sinks) - 2, -1, -1):
        out = jnp.where(blk == b, sinks[b], out)
    return out


def _softmax_parts(s, valid, sinks):
    s = jnp.where(valid, s, NEG_INF)
    if sinks is not None:
        slot = _sink_slot(s.shape[0])
        s = jnp.where(slot, _sink_scores(s.shape[0], sinks), s)
    m = jnp.max(s, axis=1, keepdims=True)
    p = jnp.exp(s - m)
    den = jnp.sum(p, axis=1, keepdims=True)
    if sinks is not None:
        p = jnp.where(slot, 0.0, p)
    return p, m, den


_NT = (((1,), (1,)), ((), ()))
_TN = (((0,), (0,)), ((), ()))


def _rows2(prev_ref, cur_ref, cs, single=False):
    if single:
        return cur_ref[0, :, cs]
    return jnp.concatenate([prev_ref[0, :, cs], cur_ref[0, :, cs]], axis=0)


def _sink_scalars(sink_ref, first, nblocks):
    return [sink_ref[first + b] for b in range(nblocks)]


def _tile(t):
    return slice(t * PAIR_W, (t + 1) * PAIR_W)


def _attn_fwd(qkv, sinks, g, *, NB, T, name):
    geo = _Geom(g)
    r, qw, kw, ntile = geo.r, geo.qw, geo.kw, geo.ntile
    tsub = T // r
    nblk = tsub // BLOCK
    qkv3 = qkv.reshape(NB, tsub, r * VAR_W)
    out_dtype = BF16 if g is None else F32
    tiles_per_kv = ntile // A_KV_HEADS

    single = nblk == 1

    def body(q_ref, kp_ref, kc_ref, vp_ref, vc_ref, sink_ref, o_ref, l_ref):
        i = pl.program_id(2)
        if geo.sink:
            kall, vall = _rows2(kp_ref, kc_ref, _tile(0)), _rows2(vp_ref, vc_ref, _tile(0))
            kdup = [_dup_head(kall, kh) for kh in range(A_KV_HEADS)]
            vdup = [_dup_head(vall, kh) for kh in range(A_KV_HEADS)]
            tiles = [[t] for t in range(ntile)]
            q2s = [_stack_heads(q_ref[0, :, _tile(t)], SCALE) for t in range(ntile)]
            kks = [kdup[t // tiles_per_kv] for t in range(ntile)]
            vvs = [vdup[t // tiles_per_kv] for t in range(ntile)]
            sinkcols = [_sink_scalars(sink_ref, 2 * t, 2) for t in range(ntile)]
        else:
            tiles = [[t] for t in range(ntile)]
            q2s = [_stack_heads(q_ref[0, :, _tile(t)], SCALE) for t in range(ntile)]
            kks = [_rows2(kp_ref, kc_ref, _tile(t), single) for t in range(ntile)]
            vvs = [_rows2(vp_ref, vc_ref, _tile(t), single) for t in range(ntile)]
            sinkcols = [None] * ntile
        valid = _band_mask(q2s[0].shape[0], i, geo.n_back, single)
        ss = [lax.dot_general(q2, kk, _NT, preferred_element_type=F32) for q2, kk in zip(q2s, kks)]
        parts = [_softmax_parts(s, valid, sc) for s, sc in zip(ss, sinkcols)]
        o2s = [jnp.dot(p.astype(BF16), vv, preferred_element_type=F32) / den for (p, m, den), vv in zip(parts, vvs)]
        for ts, o2, (p, m, den) in zip(tiles, o2s, parts):
            lse2 = jnp.broadcast_to(m + jnp.log(den), (o2.shape[0], PAIR_W))
            for n, t in enumerate(ts):
                rows = slice(2 * BLOCK * n, 2 * BLOCK * (n + 1))
                o_ref[0, :, _tile(t)] = _unstack_heads(o2[rows]).astype(out_dtype)
                l_ref[0, :, _tile(t)] = _unstack_heads(lse2[rows])

    prev = lambda i: jnp.maximum(i - 1, 0)
    in_specs = [
        pl.BlockSpec((1, BLOCK, qw), lambda b, j, i: (b, i, geo.qidx(j))),
        pl.BlockSpec((1, BLOCK, kw), lambda b, j, i: (b, prev(i), geo.kidx(j))),
        pl.BlockSpec((1, BLOCK, kw), lambda b, j, i: (b, i, geo.kidx(j))),
        pl.BlockSpec((1, BLOCK, kw), lambda b, j, i: (b, prev(i), geo.vidx(j))),
        pl.BlockSpec((1, BLOCK, kw), lambda b, j, i: (b, i, geo.vidx(j))),
        pl.BlockSpec(memory_space=pltpu.SMEM),
    ]
    o_spec = pl.BlockSpec((1, BLOCK, qw), lambda b, j, i: (b, i, j))
    shape = (NB, tsub, r * qw)
    o, lse = _pcall(
        body, name=name, grid=(NB, r, nblk), in_specs=in_specs, out_specs=[o_spec, o_spec],
        out_shape=[jax.ShapeDtypeStruct(shape, out_dtype), jax.ShapeDtypeStruct(shape, F32)],
        compiler_params=_params(("parallel", "parallel", "arbitrary")),
    )(qkv3, qkv3, qkv3, qkv3, qkv3, sinks)
    return o, lse


def _attn_fwd_b(qkvs, *, NB, T, name):
    geos = [_Geom(g) for g in range(len(B_PATTERNS))]
    steps = T // BLOCK
    nt = GB_W // PAIR_W
    ng = len(geos)

    def where(geo, s):
        nblk = T // geo.r // BLOCK
        return s // steps, (s % steps) // nblk, (s % steps) % nblk

    def body(*refs):
        ins, outs = refs[:5 * ng], refs[5 * ng:]
        s = pl.program_id(0)
        q2s, kks, vvs, valids = [], [], [], []
        for n, geo in enumerate(geos):
            q_ref, kp_ref, kc_ref, vp_ref, vc_ref = ins[5 * n:5 * n + 5]
            single = T // geo.r // BLOCK == 1
            valid = _band_mask(2 * BLOCK, where(geo, s)[2], geo.n_back, single)
            for t in range(nt):
                q2s.append(_stack_heads(q_ref[0, :, _tile(t)], SCALE))
                kks.append(_rows2(kp_ref, kc_ref, _tile(t), single))
                vvs.append(_rows2(vp_ref, vc_ref, _tile(t), single))
                valids.append(valid)
        ss = [lax.dot_general(q2, kk, _NT, preferred_element_type=F32) for q2, kk in zip(q2s, kks)]
        parts = [_softmax_parts(sc, valid, None) for sc, valid in zip(ss, valids)]
        o2s = [jnp.dot(p.astype(BF16), vv, preferred_element_type=F32) / den for (p, m, den), vv in zip(parts, vvs)]
        for n in range(ng):
            o_ref, l_ref = outs[2 * n], outs[2 * n + 1]
            for t in range(nt):
                o2, (p, m, den) = o2s[n * nt + t], parts[n * nt + t]
                o_ref[0, :, _tile(t)] = _unstack_heads(o2)
                l_ref[0, :, _tile(t)] = _unstack_heads(jnp.broadcast_to(m + jnp.log(den), (2 * BLOCK, PAIR_W)))

    in_specs, ins, out_specs, out_shape = [], [], [], []
    for geo, qkv in zip(geos, qkvs):
        tsub = T // geo.r
        pos = lambda s, geo=geo: where(geo, s)
        prev = lambda i: jnp.maximum(i - 1, 0)
        blk = lambda col, back, pos=pos: pl.BlockSpec(
            (1, BLOCK, GB_W), lambda s: (pos(s)[0], prev(pos(s)[2]) if back else pos(s)[2], col(pos(s)[1])))
        in_specs += [blk(geo.qidx, False), blk(geo.kidx, True), blk(geo.kidx, False), blk(geo.vidx, True),
                     blk(geo.vidx, False)]
        ins += [qkv.reshape(NB, tsub, geo.r * VAR_W)] * 5
        out_specs += [blk(lambda j: j, False)] * 2
        out_shape += [jax.ShapeDtypeStruct((NB, tsub, geo.r * GB_W), F32)] * 2
    res = _pcall(body, name=name, grid=(NB * steps,), in_specs=in_specs, out_specs=out_specs, out_shape=out_shape,
                 compiler_params=_params(("arbitrary",)))(*ins)
    return [(res[2 * n], res[2 * n + 1]) for n in range(ng)]


def _attn_bwd(qkv, do, lse, dlse, cosf, sins, sinks, g, *, NB, T, name):
    geo = _Geom(g)
    r, qw, kw, ntile = geo.r, geo.qw, geo.kw, geo.ntile
    tsub = T // r
    nblk = tsub // BLOCK
    view = lambda a, w: a.reshape(NB, tsub, r * w)
    has_dlse = dlse is not None
    tiles_per_kv = ntile // A_KV_HEADS

    single = nblk == 1
    krows = BLOCK if single else 2 * BLOCK
    nsteps = 1 if single else nblk + 1

    def grads(q2s, kks, vvs, do2s, i, lserows, sinkcols, dlrows):
        nrow = q2s[0].shape[0]
        ki = lax.broadcasted_iota(jnp.int32, (krows, nrow), 0)
        qi = jnp.bitwise_and(lax.broadcasted_iota(jnp.int32, (krows, nrow), 1), BLOCK - 1)
        if single:
            valid = qi >= ki
        else:
            dist = qi + BLOCK - ki
            valid = jnp.logical_and(jnp.logical_and(dist >= 0, dist <= geo.n_back), jnp.logical_or(ki >= BLOCK, i > 0))
        sts = [lax.dot_general(kk, q2, _NT, preferred_element_type=F32) for q2, kk in zip(q2s, kks)]
        dpts = [lax.dot_general(vv, do2, _NT, preferred_element_type=F32) for do2, vv in zip(do2s, vvs)]
        pts, dsts, sks = [], [], []
        for st, dpt, ls, sc, dl in zip(sts, dpts, lserows, sinkcols, dlrows):
            sv = jnp.where(valid, st, NEG_INF)
            if sc is not None:
                slot = ki == qi
                blk = lax.broadcasted_iota(jnp.int32, (krows, nrow), 1) // BLOCK
                sink = jnp.full((krows, nrow), sc[-1], F32)
                for b in range(len(sc) - 2, -1, -1):
                    sink = jnp.where(blk == b, sc[b], sink)
                sv = jnp.where(slot, sink, sv)
                dpt = jnp.where(slot, 0.0, dpt)
            pt = jnp.exp(sv - ls)
            delta = jnp.sum(pt * dpt, axis=0, keepdims=True)
            if dl is not None:
                delta = delta - dl
            dst = pt * (dpt - delta)
            if sc is not None:
                cols = lambda a, b: a[:, b * BLOCK:(b + 1) * BLOCK]
                sks.append([jnp.sum(jnp.where(cols(slot, b), cols(dst, b), 0.0)) for b in range(len(sc))])
                dst, pt = jnp.where(slot, 0.0, dst), jnp.where(slot, 0.0, pt)
            else:
                sks.append(None)
            pts.append(pt.astype(BF16))
            dsts.append(dst.astype(BF16))
        dq2s = [lax.dot_general(dst, kk, _TN, preferred_element_type=F32) * SCALE for dst, kk in zip(dsts, kks)]
        dkks = [jnp.dot(dst, q2, preferred_element_type=F32) for dst, q2 in zip(dsts, q2s)]
        dvvs = [jnp.dot(pt, do2, preferred_element_type=F32) for pt, do2 in zip(pts, do2s)]
        return dq2s, dkks, dvvs, sks

    def stat_row(t):
        tt = t.T
        return jnp.concatenate([tt[0:1, :], tt[HEAD_DIM:HEAD_DIM + 1, :]], axis=1)

    def body(*refs):
        it = iter(refs)
        q_ref, kp_ref, kc_ref, vp_ref, vc_ref, do_ref, l_ref = (next(it) for _ in range(7))
        dl_ref = next(it) if has_dlse else None
        c_ref, s_ref, sink_ref, o_ref, ds_ref, dq_s, dk_s, dv_s, car_q, car_k, car_v = (next(it) for _ in range(11))
        b, j, i = pl.program_id(0), pl.program_id(1), pl.program_id(2)

        @pl.when(jnp.logical_and(b == 0, jnp.logical_and(j == 0, i == 0)))
        def _():
            ds_ref[...] = jnp.zeros_like(ds_ref)

        def compute():
            if geo.sink:
                kall, vall = _rows2(kp_ref, kc_ref, _tile(0)), _rows2(vp_ref, vc_ref, _tile(0))
                tps = tiles_per_kv // A_BWD_SPLIT
                nb = 2 * tps
                tiles = [[kh * tiles_per_kv + s_ * tps + t for t in range(tps)]
                         for kh in range(A_KV_HEADS) for s_ in range(A_BWD_SPLIT)]
                kdup = [_dup_head(kall, kh) for kh in range(A_KV_HEADS)]
                vdup = [_dup_head(vall, kh) for kh in range(A_KV_HEADS)]
                cat = lambda f, ts: jnp.concatenate([f(t) for t in ts], axis=0)
                dq2s, dkks, dvvs, sks = grads(
                    [cat(lambda t: _stack_heads(q_ref[0, :, _tile(t)], SCALE), ts) for ts in tiles],
                    [kdup[n // A_BWD_SPLIT] for n in range(len(tiles))],
                    [vdup[n // A_BWD_SPLIT] for n in range(len(tiles))],
                    [cat(lambda t: _stack_heads(do_ref[0, :, _tile(t)]), ts) for ts in tiles], i,
                    [jnp.concatenate([stat_row(l_ref[0, :, _tile(t)]) for t in ts], axis=1) for ts in tiles],
                    [_sink_scalars(sink_ref, 2 * ts[0], nb) for ts in tiles], [None] * len(tiles))
                lane1 = _lane((1, PAIR_W))
                dsink = jnp.zeros((1, PAIR_W), F32)
                for ts, dq2, sk in zip(tiles, dq2s, sks):
                    for n, t in enumerate(ts):
                        dq_s[:, _tile(t)] = _unstack_heads(dq2[2 * BLOCK * n:2 * BLOCK * (n + 1)])
                    for bb in range(nb):
                        dsink = dsink + jnp.where(lane1 == 2 * ts[0] + bb, sk[bb], 0.0)
                per_kv = lambda parts, kh: functools.reduce(jnp.add, parts[kh * A_BWD_SPLIT:(kh + 1) * A_BWD_SPLIT])
                second = _lane((krows, PAIR_W)) >= HEAD_DIM
                dk_s[...] = jnp.where(second, _fold_heads(per_kv(dkks, 1)), _fold_heads(per_kv(dkks, 0)))
                dv_s[...] = jnp.where(second, _fold_heads(per_kv(dvvs, 1)), _fold_heads(per_kv(dvvs, 0)))
                ds_ref[0:1, :] += dsink
            else:
                dq2s, dkks, dvvs, _ = grads(
                    [_stack_heads(q_ref[0, :, _tile(t)], SCALE) for t in range(ntile)],
                    [_rows2(kp_ref, kc_ref, _tile(t), single) for t in range(ntile)],
                    [_rows2(vp_ref, vc_ref, _tile(t), single) for t in range(ntile)],
                    [_stack_heads(do_ref[0, :, _tile(t)]) for t in range(ntile)], i,
                    [stat_row(l_ref[0, :, _tile(t)]) for t in range(ntile)], [None] * ntile,
                    [stat_row(dl_ref[0, :, _tile(t)]) for t in range(ntile)])
                for t in range(ntile):
                    dq_s[:, _tile(t)] = _unstack_heads(dq2s[t])
                    dk_s[0:krows, _tile(t)] = dkks[t]
                    dv_s[0:krows, _tile(t)] = dvvs[t]

        def emit(dq, dk, dv):
            cos, sn = c_ref[0], s_ref[0]
            o_ref[0, :, 0:qw] = _unrope(dq, cos, sn).astype(BF16)
            o_ref[0, :, qw:qw + kw] = _unrope(dk, cos, sn).astype(BF16)
            o_ref[0, :, qw + kw:qw + 2 * kw] = dv.astype(BF16)
            if qw + 2 * kw < VAR_W:
                o_ref[0, :, qw + 2 * kw:VAR_W] = jnp.zeros((BLOCK, VAR_W - qw - 2 * kw), BF16)

        if single:
            compute()
            emit(dq_s[...], dk_s[0:BLOCK, :], dv_s[0:BLOCK, :])
            return

        @pl.when(i == 0)
        def _():
            car_q[...] = jnp.zeros_like(car_q)
            car_k[...] = jnp.zeros_like(car_k)
            car_v[...] = jnp.zeros_like(car_v)

        @pl.when(i == nblk)
        def _():
            dk_s[...] = jnp.zeros_like(dk_s)
            dv_s[...] = jnp.zeros_like(dv_s)

        pl.when(i < nblk)(compute)
        emit(car_q[...], car_k[...] + dk_s[0:BLOCK, :], car_v[...] + dv_s[0:BLOCK, :])
        car_q[...] = dq_s[...]
        car_k[...] = dk_s[BLOCK:2 * BLOCK, :]
        car_v[...] = dv_s[BLOCK:2 * BLOCK, :]

    cur = lambda i: jnp.minimum(i, nblk - 1)
    prv = lambda i: jnp.maximum(jnp.minimum(i, nblk - 1) - 1, 0)
    outb = lambda i: jnp.maximum(i - 1, 0)
    qrow = pl.BlockSpec((1, BLOCK, qw), lambda b, j, i: (b, cur(i), j))
    in_specs = [
        pl.BlockSpec((1, BLOCK, qw), lambda b, j, i: (b, cur(i), geo.qidx(j))),
        pl.BlockSpec((1, BLOCK, kw), lambda b, j, i: (b, prv(i), geo.kidx(j))),
        pl.BlockSpec((1, BLOCK, kw), lambda b, j, i: (b, cur(i), geo.kidx(j))),
        pl.BlockSpec((1, BLOCK, kw), lambda b, j, i: (b, prv(i), geo.vidx(j))),
        pl.BlockSpec((1, BLOCK, kw), lambda b, j, i: (b, cur(i), geo.vidx(j))),
        qrow, qrow,
    ]
    ins = [view(qkv, VAR_W)] * 5 + [view(do, qw), view(lse, qw)]
    if has_dlse:
        in_specs.append(qrow)
        ins.append(view(dlse, qw))
    in_specs += [
        pl.BlockSpec((1, BLOCK, PAIR_W), lambda b, j, i: (b, outb(i), j)),
        pl.BlockSpec((1, BLOCK, PAIR_W), lambda b, j, i: (b, outb(i), j)),
        pl.BlockSpec(memory_space=pltpu.SMEM),
    ]
    ins += [view(cosf, PAIR_W), view(sins, PAIR_W), sinks]
    scratch = [pltpu.VMEM((BLOCK, qw), F32), pltpu.VMEM((2 * BLOCK, kw), F32), pltpu.VMEM((2 * BLOCK, kw), F32),
               pltpu.VMEM((BLOCK, qw), F32), pltpu.VMEM((BLOCK, kw), F32), pltpu.VMEM((BLOCK, kw), F32)]
    dqkv, dsink = _pcall(
        body, name=name, grid=(NB, r, nsteps), in_specs=in_specs,
        out_specs=[pl.BlockSpec((1, BLOCK, VAR_W), lambda b, j, i: (b, outb(i), j)),
                   pl.BlockSpec((8, PAIR_W), lambda b, j, i: (0, 0))],
        out_shape=[jax.ShapeDtypeStruct((NB, tsub, r * VAR_W), BF16), jax.ShapeDtypeStruct((8, PAIR_W), F32)],
        scratch_shapes=scratch, compiler_params=_params(("arbitrary", "arbitrary", "arbitrary")),
    )(*ins)
    return dqkv, dsink


class _Rows:
    def __init__(self, N, T, tm):
        self.N, self.tm, self.tpe, self.grid = N, tm, T // tm, (N // tm,)

    def row(self, w, col=0):
        return pl.BlockSpec((self.tm, w), lambda i: (i, col))

    def ex(self, w):
        return pl.BlockSpec((1, 1, w), lambda i: (i // self.tpe, 0, 0))

    def const(self, shape):
        return pl.BlockSpec(shape, lambda i: tuple(0 for _ in shape))

    def view(self, w, r):
        return pl.BlockSpec((None, self.tm // r, r * w), lambda i: (i // self.tpe, i % self.tpe, 0))

    def first_of_example(self):
        return pl.program_id(0) % self.tpe == 0


def _acc(ref, first, val):
    @pl.when(first)
    def _():
        ref[0] = val

    @pl.when(jnp.logical_not(first))
    def _():
        ref[0] += val


def _colsum(v):
    return jnp.sum(v, axis=0, keepdims=True)


def _ln_stats(r):
    mu = jnp.mean(r, axis=-1, keepdims=True)
    xc = r - mu
    var = jnp.mean(xc * xc, axis=-1, keepdims=True)
    rstd = lax.rsqrt(var + LN_EPS)
    return xc * rstd, rstd


def _ln_bwd(dy, xhat, rstd, gain):
    dxh = dy * gain
    return rstd * (dxh - jnp.mean(dxh, axis=-1, keepdims=True) - xhat * jnp.mean(dxh * xhat, axis=-1, keepdims=True))


def _from_view(ref, scr, r):
    if r == 1:
        return ref[...]
    rows, w = ref.shape[0], ref.shape[1] // r
    for j in range(r):
        for c in range(w // LANES):
            scr.at[c][pl.ds(j, rows, stride=r), :] = ref[:, j * w + c * LANES:j * w + (c + 1) * LANES].astype(F32)
    return jnp.concatenate([scr[c] for c in range(w // LANES)], axis=1)


def _to_view(val, ref, scr, r):
    if r == 1:
        ref[...] = val.astype(ref.dtype)
        return
    rows, w = ref.shape[0], ref.shape[1] // r
    for c in range(w // LANES):
        scr[c] = val[:, c * LANES:(c + 1) * LANES]
    for j in range(r):
        for c in range(w // LANES):
            ref[:, j * w + c * LANES:j * w + (c + 1) * LANES] = scr.at[c][pl.ds(j, rows, stride=r), :].astype(ref.dtype)


def _silu_parts(v):
    s = jax.nn.sigmoid(v)
    return v * s, s * (1.0 + v * (1.0 - s))


def _local_step(x, mod, positions, w_in, rest_weights, sinks, ln1_g, ln1_b, ln2_g, ln2_b, target, hook=None):
    hook = hook or (lambda event, **data: None)
    NB, T, D = x.shape
    N = NB * T
    x2 = x.reshape(N, D)
    tgt2 = target.reshape(N, D)
    shift_m, scale_m, gate_m, shift_f, scale_f, gate_f = [mod[:, None, k * D:(k + 1) * D] for k in range(6)]
    cosf, sins = _rope_tables(positions)
    R = _Rows(N, T, _pick(T, 256))
    sds = jax.ShapeDtypeStruct
    exsum = lambda w=D: sds((NB, 1, w), F32)
    ngrp = len(B_PATTERNS)

    *qkv, u = _inproj(x2, scale_m, shift_m, w_in, cosf, sins, T=T, name="inproj_qkv")
    gates = _mm(u, w_in[QKV_P:], tb=True, out_dtype=BF16, name="inproj_gates")
    oa, la = _attn_fwd(qkv[0], sinks, None, NB=NB, T=T, name="attn_a_fwd")
    oa = oa.reshape(N, QA_W)
    (o1, l1), (o2, l2), (o3, l3) = _attn_fwd_b(qkv[1:], NB=NB, T=T, name="attn_b_fwd")
    w_a, w_b, w_o, w_gu, w_d = rest_weights()
    F = w_d.shape[0]
    dil = [r_ for _, r_ in B_PATTERNS]
    views = [R.view(GB_W, r_) for r_ in dil]
    tokbuf = pltpu.VMEM((GB_W // LANES, R.tm, LANES), F32)

    def merge_fwd(o1r, o2r, o3r, l1r, l2r, l3r, ob_ref, *bufs):
        os_ = [_from_view(ref, bufs[n], dil[n]) for n, ref in enumerate((o1r, o2r, o3r))]
        la, lb, lc = [_from_view(ref, bufs[3 + n], dil[n]) for n, ref in enumerate((l1r, l2r, l3r))]
        mx = jnp.maximum(jnp.maximum(la, lb), lc)
        ea, eb, ec = jnp.exp(la - mx), jnp.exp(lb - mx), jnp.exp(lc - mx)
        ob_ref[...] = ((ea * os_[0] + eb * os_[1] + ec * os_[2]) / (ea + eb + ec)).astype(BF16)

    ob = _pcall(merge_fwd, name="merge_fwd", grid=R.grid, in_specs=views + views, out_specs=R.row(GB_W),
                out_shape=sds((N, GB_W), BF16), scratch_shapes=[tokbuf] * 6,
                compiler_params=_params(("parallel",)))(o1, o2, o3, l1, l2, l3)

    f32 = lambda ref: ref[...].astype(F32)
    Rm = _Rows(N, T, _pick(T, 512))

    def mix_out(oa_r, ob_r, ga_r, gb_r, x_r, gm_r, g_r, b_r, sf_r, hf_r, wa_r, wb_r, wo_r,
                ya_ref, yb_ref, mg_ref, y_ref, r1_ref, u2_ref):
        ya = jnp.dot(oa_r[...], wa_r[...], preferred_element_type=F32).astype(BF16)
        yb = jnp.concatenate([jnp.dot(ob_r[...], wb_r[s_], preferred_element_type=F32)
                              for s_ in range(w_b.shape[0])], axis=1).astype(BF16)
        merged = (jax.nn.sigmoid(f32(ga_r)) * ya.astype(F32) + jax.nn.sigmoid(f32(gb_r)) * yb.astype(F32)).astype(BF16)
        y = jnp.dot(merged, wo_r[...], preferred_element_type=F32)
        r1 = ALPHA * x_r[...] + (1.0 + gm_r[0]) * y
        xhat, _ = _ln_stats(r1)
        x1 = xhat * g_r[...] + b_r[...]
        ya_ref[...], yb_ref[...], mg_ref[...], y_ref[...], r1_ref[...] = ya, yb, merged, y, r1
        u2_ref[...] = (x1 * (1.0 + sf_r[0]) + hf_r[0]).astype(BF16)

    ya, yb, merged, y, r1, u2 = _pcall(
        mix_out, name="mix_out", grid=Rm.grid,
        in_specs=[Rm.row(QA_W), Rm.row(GB_W), Rm.row(D, 0), Rm.row(D, 1), Rm.row(D), Rm.ex(D), Rm.const((1, D)),
                  Rm.const((1, D)), Rm.ex(D), Rm.ex(D), Rm.const(w_a.shape), Rm.const(w_b.shape), Rm.const(w_o.shape)],
        out_specs=[Rm.row(D)] * 6,
        out_shape=[sds((N, D), BF16)] * 3 + [sds((N, D), F32)] * 2 + [sds((N, D), BF16)],
        compiler_params=_params(("parallel",)))(oa, ob, gates, gates, x2, gate_m, ln1_g, ln1_b, scale_f, shift_f,
                                                w_a, w_b, w_o)

    w_gu = w_gu() if callable(w_gu) else w_gu
    tnf = w_gu.shape[2]
    nft = w_gu.shape[0] // 2
    tmf = _pick(N, 512)

    def ffn_up(u_r, wg_r, wu_r, hg_ref, hu_ref, a_ref):
        hg = jnp.dot(u_r[...], wg_r[...], preferred_element_type=F32)
        hu = jnp.dot(u_r[...], wu_r[...], preferred_element_type=F32)
        sl, _ = _silu_parts(hg)
        hg_ref[...] = hg.astype(BF16)
        hu_ref[...] = hu.astype(BF16)
        a_ref[...] = (sl * hu).astype(BF16)

    ftile = pl.BlockSpec((tmf, tnf), lambda j, i: (i, j))
    hg, hu, act = _pcall(
        ffn_up, name="ffn_up", grid=(nft, N // tmf),
        in_specs=[pl.BlockSpec((tmf, D), lambda j, i: (i, 0)), pl.BlockSpec((None, D, tnf), lambda j, i: (j, 0, 0)),
                  pl.BlockSpec((None, D, tnf), lambda j, i: (j + nft, 0, 0))],
        out_specs=[ftile] * 3, out_shape=[sds((N, F), BF16)] * 3,
        compiler_params=_params(("arbitrary", "parallel")))(u2, w_gu, w_gu)
    def ffn_down_norm2(act_r, wd_r, r1_r, g1_r, b1_r, t_r, gf_r, g_r, b_r,
                       dy2_ref, dx1_ref, dgf_ref, dg_ref, db_ref, loss_ref):
        first = R.first_of_example()
        y2v = jnp.dot(act_r[...], wd_r[...], preferred_element_type=F32)
        x1 = _ln_stats(r1_r[...])[0] * g1_r[...] + b1_r[...]
        r2 = ALPHA * x1 + (1.0 + gf_r[0]) * y2v
        xhat, rstd = _ln_stats(r2)
        err = xhat * g_r[...] + b_r[...] - t_r[...]
        dx2 = err * (1.0 / D)
        dr2 = _ln_bwd(dx2, xhat, rstd, g_r[...])
        dy2_ref[...] = ((1.0 + gf_r[0]) * dr2).astype(BF16)
        dx1_ref[...] = ALPHA * dr2
        _acc(dgf_ref, first, _colsum(dr2 * y2v))
        _acc(dg_ref, first, _colsum(dx2 * xhat))
        _acc(db_ref, first, _colsum(dx2))
        part = 0.5 * jnp.sum(jnp.mean(err * err, axis=-1, keepdims=True))
        _acc(loss_ref, first, jnp.broadcast_to(part, (1, 128)))

    dy2, dx1p, dgate_f, dg2, db2, loss_p = _pcall(
        ffn_down_norm2, name="ffn_down_norm2", grid=R.grid,
        in_specs=[R.row(F), R.const((F, D)), R.row(D), R.const((1, D)), R.const((1, D)), R.row(D), R.ex(D),
                  R.const((1, D)), R.const((1, D))],
        out_specs=[R.row(D), R.row(D), R.ex(D), R.ex(D), R.ex(D), R.ex(128)],
        out_shape=[sds((N, D), BF16), sds((N, D), F32), exsum(), exsum(), exsum(), exsum(128)],
        compiler_params=_params(("arbitrary",)))(act, w_d, r1, ln1_g, ln1_b, tgt2, gate_f, ln2_g, ln2_b)

    g_wd = _mm(act, dy2, ta=True, out_dtype=BF16, name="ffn_down_dw")

    tmd = _pick(N, 256)

    fchunk = _pick(F, 768)

    def ffn_down_dx(dy_r, wd_r, hg_r, hu_r, dh_ref):
        for t in range(F // fchunk):
            cs = slice(t * fchunk, (t + 1) * fchunk)
            da = lax.dot_general(dy_r[...], wd_r[cs, :], _NT, preferred_element_type=F32)
            sl, dsl = _silu_parts(hg_r[:, cs].astype(F32))
            dh_ref[:, cs] = (da * hu_r[:, cs].astype(F32) * dsl).astype(BF16)
            dh_ref[:, F + t * fchunk:F + (t + 1) * fchunk] = (da * sl).astype(BF16)

    rowd = lambda w_: pl.BlockSpec((tmd, w_), lambda i: (i, 0))
    dh = _pcall(
        ffn_down_dx, name="ffn_down_dx", grid=(N // tmd,),
        in_specs=[rowd(D), pl.BlockSpec((F, D), lambda i: (0, 0)), rowd(F), rowd(F)],
        out_specs=rowd(2 * F), out_shape=sds((N, 2 * F), BF16),
        compiler_params=_params(("parallel",)))(dy2, w_d, hg, hu)
    g_wgu = _mm(u2, dh, ta=True, out3=w_gu.shape[0], out_dtype=BF16, name="ffn_up_dw")

    def ffn_up_dx_norm1(dh_r, w_r, dx1p_r, r1_r, y_r, sf_r, gm_r, g_r, b_r,
                        dxp_ref, dy_ref, dsf_ref, dhf_ref, dgm_ref, dg_ref, db_ref):
        first = R.first_of_example()
        du2v = None
        for s_ in range(w_gu.shape[0]):
            part = lax.dot_general(dh_r[:, s_ * tnf:(s_ + 1) * tnf], w_r[s_], _NT, preferred_element_type=F32)
            du2v = part if du2v is None else du2v + part
        dx1 = dx1p_r[...] + du2v * (1.0 + sf_r[0])
        xhat, rstd = _ln_stats(r1_r[...])
        dr1 = _ln_bwd(dx1, xhat, rstd, g_r[...])
        dxp_ref[...] = ALPHA * dr1
        dy_ref[...] = ((1.0 + gm_r[0]) * dr1).astype(BF16)
        _acc(dsf_ref, first, _colsum(du2v * (xhat * g_r[...] + b_r[...])))
        _acc(dhf_ref, first, _colsum(du2v))
        _acc(dgm_ref, first, _colsum(dr1 * y_r[...]))
        _acc(dg_ref, first, _colsum(dx1 * xhat))
        _acc(db_ref, first, _colsum(dx1))

    dxp, dy, dscale_f, dshift_f, dgate_m, dg1, db1 = _pcall(
        ffn_up_dx_norm1, name="ffn_up_dx_norm1", grid=R.grid,
        in_specs=[R.row(2 * F), R.const(w_gu.shape)] + [R.row(D)] * 3 + [R.ex(D), R.ex(D), R.const((1, D)),
                                                                        R.const((1, D))],
        out_specs=[R.row(D), R.row(D)] + [R.ex(D)] * 5,
        out_shape=[sds((N, D), F32), sds((N, D), BF16)] + [exsum()] * 5,
        compiler_params=_params(("arbitrary",)))(dh, w_gu, dx1p, r1, y, scale_f, gate_m, ln1_g, ln1_b)

    g_wo = _mm(merged, dy, ta=True, out_dtype=BF16, name="out_proj_dw")

    def mix_out_bwd(dy_r, ya_r, yb_r, ga_r, gb_r, wo_r, wa_r, wb_r, dya_ref, dyb_ref, dg_ref, doa_ref, dob_ref):
        dm = lax.dot_general(dy_r[...], wo_r[...], _NT, preferred_element_type=F32).astype(BF16).astype(F32)
        sa, sb = jax.nn.sigmoid(f32(ga_r)), jax.nn.sigmoid(f32(gb_r))
        dya, dyb = (dm * sa).astype(BF16), (dm * sb).astype(BF16)
        dya_ref[...], dyb_ref[...] = dya, dyb
        dg_ref[:, :D] = (dm * f32(ya_r) * sa * (1.0 - sa)).astype(BF16)
        dg_ref[:, D:] = (dm * f32(yb_r) * sb * (1.0 - sb)).astype(BF16)
        doa_ref[...] = lax.dot_general(dya, wa_r[...], _NT, preferred_element_type=F32).astype(BF16)
        ds_ = D // w_b.shape[0]
        dob = None
        for s_ in range(w_b.shape[0]):
            part = lax.dot_general(dyb[:, s_ * ds_:(s_ + 1) * ds_], wb_r[s_], _NT, preferred_element_type=F32)
            dob = part if dob is None else dob + part
        dob_ref[...] = dob

    dya, dyb, dgates, doa, dob = _pcall(
        mix_out_bwd, name="mix_out_bwd", grid=Rm.grid,
        in_specs=[Rm.row(D)] * 3 + [Rm.row(D, 0), Rm.row(D, 1), Rm.const(w_o.shape), Rm.const(w_a.shape),
                                    Rm.const(w_b.shape)],
        out_specs=[Rm.row(D), Rm.row(D), Rm.row(2 * D), Rm.row(QA_W), Rm.row(GB_W)],
        out_shape=[sds((N, D), BF16), sds((N, D), BF16), sds((N, 2 * D), BF16), sds((N, QA_W), BF16), sds((N, GB_W), F32)],
        compiler_params=_params(("parallel",)))(dy, ya, yb, gates, gates, w_o, w_a, w_b)

    g_wa = _mm(oa, dya, ta=True, out_dtype=BF16, name="branch_a_dw")
    g_wb = _mm(ob, dyb, ta=True, out3=w_b.shape[0], out_dtype=BF16, name="branch_b_dw")
    hook("rest_grads", g_wa=g_wa, g_wb=g_wb, g_wo=g_wo, g_wgu=g_wgu, g_wd=g_wd)

    seg = (jnp.arange(GB_W)[:, None] // HEAD_DIM == jnp.arange(GB_W)[None, :] // HEAD_DIM).astype(BF16)

    def merge_bwd(dob_r, o1r, o2r, o3r, l1r, l2r, l3r, seg_r, d1, d2, d3, e1, e2, e3, *bufs):
        dob_v = dob_r[...]
        os_ = [_from_view(ref, bufs[n], dil[n]) for n, ref in enumerate((o1r, o2r, o3r))]
        la, lb, lc = [_from_view(ref, bufs[3 + n], dil[n]) for n, ref in enumerate((l1r, l2r, l3r))]
        mx = jnp.maximum(jnp.maximum(la, lb), lc)
        ea, eb, ec = jnp.exp(la - mx), jnp.exp(lb - mx), jnp.exp(lc - mx)
        inv = 1.0 / (ea + eb + ec)
        ws = [ea * inv, eb * inv, ec * inv]

        def headsum(v):
            hi = v.astype(BF16)
            r1_ = v - hi.astype(F32)
            mid = r1_.astype(BF16)
            lo = (r1_ - mid.astype(F32)).astype(BF16)
            sm = seg_r[...]
            return (jnp.dot(hi, sm, preferred_element_type=F32) + jnp.dot(mid, sm, preferred_element_type=F32)
                    + jnp.dot(lo, sm, preferred_element_type=F32))

        dws = [headsum(dob_v * o) for o in os_]
        mean = ws[0] * dws[0] + ws[1] * dws[1] + ws[2] * dws[2]
        for n, (w_, dw_, d_ref, e_ref) in enumerate(zip(ws, dws, (d1, d2, d3), (e1, e2, e3))):
            _to_view(w_ * dob_v, d_ref, bufs[6], dil[n])
            _to_view(w_ * (dw_ - mean), e_ref, bufs[7], dil[n])

    vshape = lambda r_, dt: sds((NB, T // r_, r_ * GB_W), dt)
    mb = _pcall(
        merge_bwd, name="merge_bwd", grid=R.grid, in_specs=[R.row(GB_W)] + views + views + [R.const((GB_W, GB_W))],
        out_specs=views + views, out_shape=[vshape(r_, BF16) for r_ in dil] + [vshape(r_, F32) for r_ in dil],
        scratch_shapes=[tokbuf] * 8, compiler_params=_params(("parallel",)))(dob, o1, o2, o3, l1, l2, l3, seg)
    do_b, dlse_b = mb[:3], mb[3:]
    hook("merge_bwd_done")

    dqkv_a, dsink = _attn_bwd(qkv[0], doa, la, None, cosf, sins, sinks, None, NB=NB, T=T, name="attn_a_bwd")
    hook("attn_a_bwd_done")
    dqkv = [dqkv_a]
    for g in range(ngrp):
        dqkv.append(_attn_bwd(qkv[1 + g], do_b[g], (l1, l2, l3)[g], dlse_b[g], cosf, sins, sinks, g, NB=NB, T=T,
                              name=f"attn_b{g}_bwd")[0])
        hook(f"attn_b{g}_bwd_done")

    g_win = [_mm(d3.reshape(N, VAR_W), u, ta=True, out_dtype=BF16, name=f"inproj_dw{v}") if VAR_DIL[v] == 1
             else _dw_view(d3, u, VAR_DIL[v], name=f"inproj_dw{v}") for v, d3 in enumerate(dqkv)]
    g_win.append(_mm(dgates, u, ta=True, out_dtype=BF16, name=f"inproj_dw{N_VAR}"))
    hook("win_grads", g_win=g_win)
    wvar = lambda v: (w_in, (VAR_W, D), (v, 0))
    dview = lambda v: (dqkv[v], VAR_DIL[v])
    du = _mm_multi([dview(0)], [wvar(0)], M=N, T=T, name="inproj_dx0")
    hook("inproj_dx0_done")
    def x_bwd(duv, ins, outs):
        (dxp_r, x_r, sm_r), (gx_ref, dsm_ref, dhm_ref) = ins, outs
        first = R.first_of_example()
        gx_ref[...] = dxp_r[...] + duv * (1.0 + sm_r[0])
        _acc(dsm_ref, first, _colsum(duv * x_r[...]))
        _acc(dhm_ref, first, _colsum(duv))

    gx, dscale_m, dshift_m = _mm_multi(
        [dview(v) for v in range(1, N_VAR)] + [dgates],
        [wvar(v) for v in range(1, N_VAR)] + [(w_in, (2 * D, D), (QKV_P // (2 * D), 0))],
        M=N, T=T, add=du, tm=R.tm, name="inproj_dx1",
        post=([dxp, x2, scale_m], [R.row(D), R.row(D), R.ex(D)], [R.row(D), R.ex(D), R.ex(D)],
              [sds((N, D), F32), exsum(), exsum()], x_bwd))
    hook("inproj_dx1_done")

    dmod =jnp.concatenate([dshift_m, dscale_m, dgate_m, dshift_f, dscale_f, dgate_f], axis=-1)[:, 0]
    ln_grads = jnp.concatenate([dg1, db1, dg2, db2], axis=1)
    return dict(loss=loss_p[:, 0, 0], grad_x=gx.reshape(NB, T, D), g_win=g_win, g_wa=g_wa, g_wb=g_wb, g_wo=g_wo,
                g_wgu=g_wgu, g_wd=g_wd, dmod=dmod, ln_grads=ln_grads, dsink=dsink[0, :A_Q_HEADS])


def _coords():
    return lax.axis_index("x"), lax.axis_index("y"), lax.axis_index("c")


def _allgather_small(blk, *, name):
    m_per, n = blk.shape

    def body(x_ref, out_ref, send_sems, recv_sems, local_sem):
        x, y, c = _coords()
        me, sibling = (x, y, c), (x, y, 1 - c)
        chips = [(1 - x, y), (x, 1 - y), (1 - x, 1 - y)]

        def rows(px, py, pc):
            return out_ref.at[pl.ds((4 * px + 2 * py + pc) * m_per, m_per), :]

        def copy(k, block, to, src=None):
            return pltpu.make_async_remote_copy(
                src_ref=rows(*block) if src is None else src, dst_ref=rows(*block),
                send_sem=send_sems.at[k], recv_sem=recv_sems.at[k], device_id=to, device_id_type=MESH)

        mine = pltpu.make_async_copy(x_ref, rows(*me), local_sem)
        mine.start()
        first = [copy(0, me, sibling, src=x_ref)]
        first += [copy(1 + j, me, (*chip, c), src=x_ref) for j, chip in enumerate(chips)]
        for cp in first:
            cp.start()
        passed = [copy(4 + j, (*chip, c), sibling) for j, chip in enumerate(chips)]
        for j, chip in enumerate(chips):
            copy(1 + j, (*chip, c), me).wait_recv()
            passed[j].start()
        copy(0, sibling, me).wait_recv()
        for j, chip in enumerate(chips):
            copy(4 + j, (*chip, 1 - c), me).wait_recv()
        for cp in first + passed:
            cp.wait_send()
        mine.wait()

    return _pcall(
        body, name=name, out_shape=jax.ShapeDtypeStruct((8 * m_per, n), blk.dtype),
        in_specs=[pl.BlockSpec(memory_space=pltpu.VMEM)], out_specs=pl.BlockSpec(memory_space=pltpu.VMEM),
        scratch_shapes=[pltpu.SemaphoreType.DMA((7,)), pltpu.SemaphoreType.DMA((7,)), pltpu.SemaphoreType.DMA],
        compiler_params=pltpu.CompilerParams(vmem_limit_bytes=VMEM_LIMIT_BYTES),
    )(blk)


def _exchange(srcs, dsts, plan, *, name, dst_inits=None):
    na = len(dsts)
    nrem = len(plan(0, 0, 0))

    def body(*refs):
        refs = list(refs)
        src_refs = [refs.pop(0) for _ in range(na)] if srcs is not None else None
        if dst_inits is not None:
            del refs[:na]
        dst_refs, (send_sems, recv_sems) = refs[:na], refs[na:]
        start, wait = _copies(dst_refs if src_refs is None else src_refs, dst_refs, send_sems, recv_sems, plan)
        start()
        wait()

    hbm = pl.BlockSpec(memory_space=pl.ANY)
    ins = (list(srcs) if srcs is not None else []) + (list(dst_inits) if dst_inits is not None else [])
    base = na if srcs is not None else 0
    aliases = {base + a: a for a in range(na)} if dst_inits is not None else {}
    return _pcall(
        body, name=name, out_shape=list(dsts), in_specs=[hbm] * len(ins), out_specs=[hbm] * na,
        input_output_aliases=aliases,
        scratch_shapes=[pltpu.SemaphoreType.DMA((na * nrem,)), pltpu.SemaphoreType.DMA((na * nrem,))],
    )(*ins)


def _other_chips(x, y):
    return [(1 - x, y), (x, 1 - y), (1 - x, 1 - y)]


def _round(ride, carrier, name):
    if carrier is not None:
        _RIDES.setdefault(carrier, []).append(ride)
        return
    srcs = ride.srcs() if callable(ride.srcs) else ride.srcs
    inits = ride.dst_inits() if callable(ride.dst_inits) else ride.dst_inits
    ride.out = list(_exchange(srcs, ride.dsts, ride.plan, name=name, dst_inits=inits))


class _Gather:
    def __init__(self, shards, chip, tag, carriers=(None, None)):
        def plan_ici(x, y, c):
            k = 2 * x + y
            return [((c,), (k, c), (2 * px + py, c), (px, py, c)) for px, py in _other_chips(x, y)]

        def plan_d2d(x, y, c):
            return [((2 * px + py, c), (2 * px + py, c), (2 * px + py, 1 - c), (x, y, 1 - c))
                    for px, py in _other_chips(x, y)]

        def plan_near(x, y, c):
            k = 2 * x + y
            return [((c,), (k, c), (2 * px + py, c), (px, py, c)) for px, py in ((1 - x, y), (x, 1 - y))]

        def plan_far(x, y, c):
            kx, ky, kd = 2 * (1 - x) + y, 2 * x + (1 - y), 2 * (1 - x) + (1 - y)
            hp = shards[0].shape[1] // 2
            top, bottom = pl.ds(0, hp), pl.ds(hp, hp)
            return [((kx, c, top), (kx, c, top), (kd, c, top), (x, 1 - y, c)),
                    ((ky, c, bottom), (ky, c, bottom), (kd, c, bottom), (1 - x, y, c))]

        self.shards, self.chip = shards, chip
        dsts = [jax.ShapeDtypeStruct((4,) + s.shape, s.dtype) for s in shards]
        if len(carriers) == 3:
            near = _Ride(shards, dsts, plan_near)
            ici = _Ride(None, dsts, plan_far, dst_inits=lambda: near.out)
            _round(near, carriers[0], f"gather_{tag}_near")
            _round(ici, carriers[1], f"gather_{tag}_far")
        else:
            ici = _Ride(shards, dsts, plan_ici)
            _round(ici, carriers[0], f"gather_{tag}_ici")
        self.d2d = _Ride(None, dsts, plan_d2d, dst_inits=lambda: ici.out)
        _round(self.d2d, carriers[-1], f"gather_{tag}_d2d")

    def result(self):
        full = [lax.dynamic_update_index_in_dim(f, s, self.chip, 0) for f, s in zip(self.d2d.out, self.shards)]
        return [f.reshape((4, 2 * f.shape[2], f.shape[3])) for f in full]


def _index_operand(i):
    return jnp.reshape(i, (1,)).astype(jnp.int32)


def _add_pairs(g, f, ci, *, name):
    s, _, hr, wd = g.shape
    tr = _pick(hr, 600, 16)

    def body(c_ref, a_ref, b_ref, o_ref):
        o_ref[...] = (a_ref[...].astype(F32) + b_ref[...].astype(F32)).astype(BF16)

    spec = pl.BlockSpec((1, tr, wd), lambda j, i, c: (j, i, 0))
    grid_spec = pltpu.PrefetchScalarGridSpec(
        num_scalar_prefetch=1, grid=(s, hr // tr),
        in_specs=[pl.BlockSpec((1, None, tr, wd), lambda j, i, c: (j, c[0], i, 0)), spec], out_specs=spec)
    return _pcall(body, name=name, grid_spec=grid_spec, out_shape=jax.ShapeDtypeStruct(f.shape, BF16),
                  compiler_params=_params(("parallel", "parallel")))(_index_operand(ci), g, f)


def _sum_chips(landed, pairs, chip, *, name):
    s, hr, wd = landed.shape
    tr = _pick(hr, 600, 16)

    def body(k_ref, l_ref, p_ref, o_ref):
        acc = None
        for k in range(s):
            part = jnp.where(k_ref[0] == k, p_ref[k], l_ref[k]).astype(F32)
            acc = part if acc is None else acc + part
        o_ref[...] = acc

    spec = pl.BlockSpec((s, tr, wd), lambda i, k: (0, i, 0))
    grid_spec = pltpu.PrefetchScalarGridSpec(
        num_scalar_prefetch=1, grid=(hr // tr,), in_specs=[spec, spec],
        out_specs=pl.BlockSpec((tr, wd), lambda i, k: (i, 0)))
    return _pcall(body, name=name, grid_spec=grid_spec, out_shape=jax.ShapeDtypeStruct((hr, wd), F32),
                  compiler_params=_params(("parallel",)))(_index_operand(chip), landed, pairs)


class _ReduceScatter:
    def __init__(self, gs, chip, ci, tag):
        self.gs, self.chip, self.ci, self.tag = gs, chip, ci, tag
        self.half_t = [jax.ShapeDtypeStruct((g.shape[0],) + g.shape[2:], BF16) for g in gs]

    def pair(self, carrier=None):
        plan = lambda x, y, c: [((slice(None), 1 - c), (), (), (x, y, 1 - c))]
        self.r1 = _Ride(self.gs, self.half_t, plan)
        _round(self.r1, carrier, f"reduce_{self.tag}_pair")

    def chips(self, carrier=None):
        def plan(x, y, c):
            k = 2 * x + y
            return [((2 * px + py,), (k,), (2 * px + py,), (px, py, c)) for px, py in _other_chips(x, y)]

        self.pairs = [_add_pairs(g, f, self.ci, name=f"reduce_{self.tag}_pair_add{n}")
                      for n, (g, f) in enumerate(zip(self.gs, self.r1.out))]
        self.r2 = _Ride(self.pairs, self.half_t, plan)
        _round(self.r2, carrier, f"reduce_{self.tag}_chips")

    def halves(self, carrier=None):
        plan = lambda x, y, c: [((), (c,), (1 - c,), (x, y, 1 - c))]
        self.mine = [_sum_chips(l, p, self.chip, name=f"reduce_{self.tag}_chip_sum{n}")
                     for n, (l, p) in enumerate(zip(self.r2.out, self.pairs))]
        self.r3 = _Ride(self.mine, [jax.ShapeDtypeStruct((2,) + m.shape, F32) for m in self.mine], plan)
        _round(self.r3, carrier, f"reduce_{self.tag}_halves")

    def result(self):
        return [lax.dynamic_update_index_in_dim(b, m, self.ci, 0).reshape(2 * m.shape[0], m.shape[1])
                for b, m in zip(self.r3.out, self.mine)]


def _ada_fwd(c_all, w_sh, b_sh, *, name):
    nb, d = c_all.shape
    wcols = w_sh.shape[1]
    tn = _pick(wcols, 512)

    def body(c_ref, w_ref, b_ref, o_ref, a_ref):
        cv = c_ref[...]
        act = cv * jax.nn.sigmoid(cv)
        a_ref[...] = act
        o_ref[...] = jnp.dot(act.astype(BF16), w_ref[...].astype(BF16), preferred_element_type=F32) + b_ref[...]

    return _pcall(
        body, name=name, grid=(wcols // tn,),
        in_specs=[pl.BlockSpec((nb, d), lambda j: (0, 0)), pl.BlockSpec((d, tn), lambda j: (0, j)),
                  pl.BlockSpec((1, tn), lambda j: (0, j))],
        out_specs=[pl.BlockSpec((nb, tn), lambda j: (0, j)), pl.BlockSpec((nb, d), lambda j: (0, 0))],
        out_shape=[jax.ShapeDtypeStruct((nb, wcols), F32), jax.ShapeDtypeStruct((nb, d), F32)],
        compiler_params=_params(("arbitrary",)))(c_all, w_sh, b_sh)


def _sum_devices(g, *, name):
    nd, m, w = g.shape

    def body(g_ref, o_ref):
        acc = g_ref[0]
        for k in range(1, nd):
            acc = acc + g_ref[k]
        o_ref[...] = acc

    return _pcall(body, name=name, out_shape=jax.ShapeDtypeStruct((m, w), F32),
                  compiler_params=pltpu.CompilerParams(vmem_limit_bytes=VMEM_LIMIT_BYTES))(g)


def _adamw(w, g, m, v, *, name):
    rows, cols = w.shape[-2:]
    tr = _pick(rows, max(8, (1 << 18) // cols), 8)
    c1 = 1.0 / (1.0 - ADAM_B1 ** ADAM_STEP)
    c2 = 1.0 / (1.0 - ADAM_B2 ** ADAM_STEP)

    def body(w_ref, g_ref, m_ref, v_ref, d_ref, nm_ref, nv_ref):
        gv = g_ref[...]
        nm = ADAM_B1 * m_ref[...] + (1.0 - ADAM_B1) * gv
        nv = ADAM_B2 * v_ref[...] + (1.0 - ADAM_B2) * (gv * gv)
        d_ref[...] = -ADAM_LR * ((nm * c1) / (jnp.sqrt(nv * c2) + ADAM_EPS) + ADAM_WD * w_ref[...])
        nm_ref[...] = nm
        nv_ref[...] = nv

    gspec = pl.BlockSpec((tr, cols), lambda i: (i, 0))
    spec = pl.BlockSpec((None, tr, cols), lambda i: (0, i, 0)) if w.ndim == 3 else gspec
    shp = jax.ShapeDtypeStruct(w.shape, F32)
    return _pcall(body, name=name, grid=(rows // tr,), in_specs=[spec, gspec, spec, spec], out_specs=[spec] * 3,
                  out_shape=[shp] * 3, compiler_params=_params(("parallel",)))(w, g, m, v)


def _permute_in_rows(wt):
    ngrp = len(B_PATTERNS)
    qb, kb, vb = (wt[A_W + n * QB_W:A_W + (n + 1) * QB_W] for n in range(3))
    parts = [wt[:A_W], jnp.zeros((VAR_W - A_W, wt.shape[1]), wt.dtype)]
    for g in range(ngrp):
        parts += [t[g * GB_W:(g + 1) * GB_W] for t in (qb, kb, vb)]
    return jnp.concatenate(parts + [wt[A_W + 3 * QB_W:]], axis=0)


def _unpermute_in_grads(pieces):
    ga, groups, gg = pieces[0], pieces[1:-1], pieces[-1]
    rows = [ga[:A_W]]
    for n in range(3):
        rows += [gp[n * GB_W:(n + 1) * GB_W] for gp in groups]
    return jnp.concatenate(rows + [gg], axis=0)


def kernel(x, c, positions, w_ada, b_ada, w_in, sinks, w_branch_a, w_branch_b, w_o, ln1_g, ln1_b, w_gate_up, w_down, ln2_g, ln2_b, loss_target, m_w_ada, m_b_ada, m_w_in, m_sinks, m_w_branch_a, m_w_branch_b, m_w_o, m_ln1_g, m_ln1_b, m_w_gate_up, m_w_down, m_ln2_g, m_ln2_b, v_w_ada, v_b_ada, v_w_in, v_sinks, v_w_branch_a, v_w_branch_b, v_w_o, v_ln1_g, v_ln1_b, v_w_gate_up, v_w_down, v_ln2_g, v_ln2_b):
    xi, yi, ci = _coords()
    chip = 2 * xi + yi
    dev = 4 * xi + 2 * yi + ci
    NB, T, D = x.shape
    nchip, ndev = 4, 8
    ada_cols = w_ada.shape[2]

    ra, ro, rd = w_branch_a.shape[1], w_o.shape[1], w_down.shape[1]
    rowsh = jnp.concatenate([w_branch_a[0], w_o[0], w_down[0]], axis=0)
    halves = lambda a: a.reshape(a.shape[:-2] + (2, a.shape[-2] // 2, a.shape[-1]))
    tr = lambda a: jnp.swapaxes(a, -1, -2)
    shards = [halves(w.astype(BF16)) for w in (tr(w_in[0]), rowsh, w_branch_b[0], w_gate_up[0])]
    gin = _Gather(shards[:1], chip, "w_in", carriers=("gather_c", "ada_fwd", "gather_mod"))

    c_blk = jnp.zeros((8, D), F32).at[:NB].set(c)
    c_all = _allgather_small(c_blk, name="gather_c").reshape(ndev, 8, D)[:, :NB].reshape(ndev * NB, D)
    b_sh = lax.dynamic_slice(b_ada, (0, chip * ada_cols), (1, ada_cols))
    mod_part, c_act = _ada_fwd(c_all, w_ada[0], b_sh, name="ada_fwd")
    mod_g = _allgather_small(mod_part, name="gather_mod").reshape(nchip, 2, ndev * NB, ada_cols)[:, 0]
    mod_all = jnp.transpose(mod_g, (1, 0, 2)).reshape(ndev * NB, nchip * ada_cols)
    mod = lax.dynamic_slice(mod_all, (NB * dev, 0), (NB, nchip * ada_cols))

    (g_in,) = gin.result()
    w_in_f = _permute_in_rows(g_in.reshape(nchip * g_in.shape[1], D))
    mix = _Gather(shards[1:3], chip, "w_mix", carriers=("inproj_qkv", "attn_a_fwd"))
    ffn = _Gather(shards[3:], chip, "w_ffn", carriers=("attn_a_fwd", "attn_b_fwd", "merge_fwd"))

    def rest_weights():
        g_rows, w_b_f = mix.result()
        return (g_rows[:, :ra].reshape(nchip * ra, D), w_b_f, g_rows[:, ra:ra + ro].reshape(nchip * ro, D),
                lambda: ffn.result()[0], g_rows[:, ra + ro:].reshape(nchip * rd, D))

    red = {}

    def hook(event, **g):
        if event == "rest_grads":
            gr_rows = jnp.concatenate([g["g_wa"].reshape(nchip, ra, D), g["g_wo"].reshape(nchip, ro, D),
                                       g["g_wd"].reshape(nchip, rd, D)], axis=1)
            red["ffn"] = _ReduceScatter([halves(g["g_wgu"])], chip, ci, "ffn")
            red["mix"] = _ReduceScatter([halves(gr_rows), halves(g["g_wb"])], chip, ci, "mix")
            red["ffn"].pair(carrier="merge_bwd")
            red["mix"].pair(carrier="merge_bwd")
        elif event == "merge_bwd_done":
            red["ffn"].chips(carrier="attn_a_bwd")
            red["mix"].chips(carrier="attn_b0_bwd")
        elif event == "attn_a_bwd_done":
            red["ffn"].halves(carrier="attn_b0_bwd")
        elif event == "attn_b0_bwd_done":
            red["mix"].halves(carrier="attn_b1_bwd")
        elif event == "win_grads":
            gr_in = _unpermute_in_grads(g["g_win"])
            red["w_in"] = _ReduceScatter([halves(gr_in.reshape(nchip, gr_in.shape[0] // nchip, D))], chip, ci, "w_in")
            red["w_in"].pair(carrier="inproj_dx0")
        elif event == "inproj_dx0_done":
            red["w_in"].chips(carrier="inproj_dx1")
        elif event == "inproj_dx1_done":
            red["w_in"].halves(carrier="gather_small")

    res = _local_step(x, mod, positions, w_in_f, rest_weights, sinks[0], ln1_g, ln1_b, ln2_g, ln2_b, loss_target, hook)
    (g_rows_red, g_w_b), (g_w_gu,) = red["mix"].result(), red["ffn"].result()
    g_w_a, g_w_o, g_w_d = g_rows_red[:ra], g_rows_red[ra:ra + ro], g_rows_red[ra + ro:]

    small_rows = 24
    misc = jnp.zeros((1, D), F32).at[0, :A_Q_HEADS].set(res["dsink"]).at[0, A_Q_HEADS].set(jnp.sum(res["loss"]))
    small = jnp.concatenate([res["dmod"].reshape(NB * 6, D), jnp.sum(res["ln_grads"], axis=0), misc,
                             jnp.zeros((small_rows - NB * 6 - 5, D), F32)], axis=0)
    small_all = _allgather_small(small, name="gather_small").reshape(ndev, small_rows, D)
    (g_w_in,) = red["w_in"].result()
    dmod_all = small_all[:, :NB * 6].reshape(ndev * NB, 6 * D)
    sums = _sum_devices(small_all, name="sum_small")
    g_b_ada = (sums[0:6] + sums[6:12]).reshape(1, 6 * D)
    g_ln1_g, g_ln1_b, g_ln2_g, g_ln2_b = (sums[12 + n][None] for n in range(4))
    g_sinks = sums[16, :A_Q_HEADS][None]
    loss = sums[16, A_Q_HEADS]
    dmod_sh = lax.dynamic_slice(dmod_all, (0, chip * ada_cols), (ndev * NB, ada_cols))
    g_w_ada = _mm(c_act, dmod_sh, ta=True, name="ada_dw")

    names = ["w_ada", "b_ada", "w_in", "sinks", "w_branch_a", "w_branch_b", "w_o", "ln1_g", "ln1_b",
             "w_gate_up", "w_down", "ln2_g", "ln2_b"]
    ws = [w_ada, b_ada, w_in, sinks, w_branch_a, w_branch_b, w_o, ln1_g, ln1_b, w_gate_up, w_down, ln2_g, ln2_b]
    ms = [m_w_ada, m_b_ada, m_w_in, m_sinks, m_w_branch_a, m_w_branch_b, m_w_o, m_ln1_g, m_ln1_b, m_w_gate_up,
          m_w_down, m_ln2_g, m_ln2_b]
    vs = [v_w_ada, v_b_ada, v_w_in, v_sinks, v_w_branch_a, v_w_branch_b, v_w_o, v_ln1_g, v_ln1_b, v_w_gate_up,
          v_w_down, v_ln2_g, v_ln2_b]
    gs = [g_w_ada, g_b_ada, g_w_in, g_sinks, g_w_a, g_w_b, g_w_o, g_ln1_g, g_ln1_b, g_w_gu, g_w_d, g_ln2_g, g_ln2_b]
    grads, deltas, new_ms, new_vs = [], [], [], []
    for name, w, g, m, v in zip(names, ws, gs, ms, vs):
        flip = tr if name == "w_in" else (lambda a: a)
        w, m, v = flip(w), flip(m), flip(v)
        g2 = g.reshape(w.shape[-2:])
        d, nm, nv = _adamw(w, g2, m, v, name="adamw_" + name)
        grads.append(flip(g2.reshape(w.shape)))
        deltas.append(flip(d))
        new_ms.append(flip(nm))
        new_vs.append(flip(nv))
    return (loss, res["grad_x"], *grads, *deltas, *new_ms, *new_vs)
```

```python
import functools

import jax
import jax.numpy as jnp
from jax import lax
from jax.experimental import pallas as pl
from jax.experimental.pallas import tpu as pltpu

F32 = jnp.float32
BF16 = jnp.bfloat16
MESH = pl.DeviceIdType.MESH

HEAD_DIM = 64
LANES = 128
PAIR_W = 2 * HEAD_DIM
BLOCK = 128
A_Q_HEADS = 16
A_KV_HEADS = 2
A_WINDOW = 128
B_PATTERNS = ((128, 1), (512, 4), (2048, 16))
B_GROUP_HEADS = 8
QA_W = A_Q_HEADS * HEAD_DIM
KA_W = A_KV_HEADS * HEAD_DIM
GB_W = B_GROUP_HEADS * HEAD_DIM
QB_W = GB_W * len(B_PATTERNS)
A_W = QA_W + 2 * KA_W
VAR_W = 3 * GB_W
N_VAR = 1 + len(B_PATTERNS)
VAR_DIL = (1,) + tuple(r for _, r in B_PATTERNS)
A_BWD_SPLIT = 4
QKV_P = N_VAR * VAR_W
ROPE_THETA = 10000.0
LN_EPS = 1e-5
NEG_INF = -1e30
DEPTH = 1
ALPHA = (2 * DEPTH) ** 0.25
SCALE = HEAD_DIM ** -0.5

ADAM_LR, ADAM_B1, ADAM_B2, ADAM_EPS, ADAM_WD, ADAM_STEP = 0.001, 0.9, 0.999, 1e-08, 0.01, 10

VMEM_LIMIT_BYTES = 56 * 1024 * 1024
MM_TILE_BYTES = 36 * 1024 * 1024
MM_WHOLE_K = 4096


def _params(sem=None):
    return pltpu.CompilerParams(dimension_semantics=sem, vmem_limit_bytes=VMEM_LIMIT_BYTES)


_RIDES = {}


def _pcall(body, *, name, **kw):
    rides = _RIDES.pop(name, None)
    if rides is None:
        return pl.pallas_call(body, name=name, **kw)
    return _riding_call(body, rides, name=name, **kw)


def _copies(src_refs, dst_refs, send_sems, recv_sems, plan):
    x, y, c = lax.axis_index("x"), lax.axis_index("y"), lax.axis_index("c")
    remote = plan(x, y, c)
    nrem = len(remote)
    at = lambda ref, idx: ref.at[idx] if idx else ref

    def copy(a, n, landing):
        si, di, ri, peer = remote[n]
        return pltpu.make_async_remote_copy(
            src_ref=at(src_refs[a], si), dst_ref=at(dst_refs[a], ri if landing else di),
            send_sem=send_sems.at[a * nrem + n], recv_sem=recv_sems.at[a * nrem + n],
            device_id=peer, device_id_type=MESH)

    order = [(a, n) for a in range(len(dst_refs)) for n in range(nrem)]

    def start():
        for a, n in order:
            copy(a, n, False).start()

    def wait():
        for a, n in order:
            copy(a, n, True).wait_recv()
        for a, n in order:
            copy(a, n, False).wait_send()

    return start, wait


class _Ride:
    def __init__(self, srcs, dsts, plan, dst_inits=None):
        self.srcs, self.dsts, self.plan, self.dst_inits, self.out = srcs, dsts, plan, dst_inits, None


def _riding_call(body, rides, *, name, in_specs, out_specs, out_shape, grid=(), scratch_shapes=(), **kw):
    single = not isinstance(out_specs, (list, tuple))
    out_specs = [out_specs] if single else list(out_specs)
    out_shape = [out_shape] if single else list(out_shape)
    n_in, n_out, n_scr = len(in_specs), len(out_specs), len(scratch_shapes)
    xin, xdsts, sems, aliases, layout = [], [], [], {}, []
    for ride in rides:
        srcs = ride.srcs() if callable(ride.srcs) else ride.srcs
        inits = ride.dst_inits() if callable(ride.dst_inits) else ride.dst_inits
        na, nrem = len(ride.dsts), len(ride.plan(0, 0, 0))
        src_at = len(xin) if srcs is not None else None
        xin += list(srcs) if srcs is not None else []
        if inits is not None:
            aliases.update({n_in + len(xin) + a: n_out + len(xdsts) + a for a in range(na)})
            xin += list(inits)
        layout.append((src_at, len(xdsts), na))
        xdsts += list(ride.dsts)
        sems += [pltpu.SemaphoreType.DMA((na * nrem,)), pltpu.SemaphoreType.DMA((na * nrem,))]

    def wrapped(*refs):
        ins, xins = refs[:n_in], refs[n_in:n_in + len(xin)]
        outs = refs[n_in + len(xin):n_in + len(xin) + n_out]
        xouts = refs[n_in + len(xin) + n_out:n_in + len(xin) + n_out + len(xdsts)]
        scr = refs[n_in + len(xin) + n_out + len(xdsts):]
        rounds = []
        for k, (ride, (src_at, dst_at, na)) in enumerate(zip(rides, layout)):
            dsts = xouts[dst_at:dst_at + na]
            srcs = dsts if src_at is None else xins[src_at:src_at + na]
            rounds.append(_copies(srcs, dsts, scr[n_scr + 2 * k], scr[n_scr + 2 * k + 1], ride.plan))
        ids = [pl.program_id(a) for a in range(len(grid))]
        first = functools.reduce(jnp.logical_and, [i == 0 for i in ids], True)
        last = functools.reduce(jnp.logical_and, [i == g - 1 for i, g in zip(ids, grid)], True)

        def start_all():
            for start, _ in rounds:
                start()

        def wait_all():
            for _, wait in rounds:
                wait()

        start_all() if not grid else pl.when(first)(start_all)
        body(*ins, *outs, *scr[:n_scr])
        wait_all() if not grid else pl.when(last)(wait_all)

    hbm = pl.BlockSpec(memory_space=pl.ANY)
    gridkw = dict(grid=grid) if grid else {}

    def run(*args):
        res = pl.pallas_call(
            wrapped, name=name, in_specs=list(in_specs) + [hbm] * len(xin),
            out_specs=out_specs + [hbm] * len(xdsts), out_shape=out_shape + xdsts,
            scratch_shapes=list(scratch_shapes) + sems, input_output_aliases=aliases,
            compiler_params=_params(("arbitrary",) * len(grid) if grid else None), **gridkw,
        )(*args, *xin)
        for ride, (_, dst_at, na) in zip(rides, layout):
            ride.out = list(res[n_out + dst_at:n_out + dst_at + na])
        return res[0] if single else list(res[:n_out])

    return run


def _pick(n, target, quantum=128):
    t = (min(target, n) // quantum) * quantum
    while t >= quantum:
        if n % t == 0:
            return t
        t -= quantum
    return n


def _mm(a, b, *, name, ta=False, tb=False, b3=False, out3=0, out_dtype=F32, add=None, tm=1024, tn=1536, tk=1536):
    if ta:
        K, M = a.shape
    else:
        M, K = a.shape
    if b3 and tb:
        Nn, K2, tk = b.shape[1], b.shape[0] * b.shape[2], b.shape[2]
    elif b3:
        K2, Nn, tn = b.shape[1], b.shape[0] * b.shape[2], b.shape[2]
    elif tb:
        Nn, K2 = b.shape
    else:
        K2, Nn = b.shape
    assert K == K2, (a.shape, b.shape)
    if out3:
        tn = Nn // out3
    tm, tn, tk = _pick(M, tm), _pick(Nn, tn), _pick(K, tk)
    if not (b3 and tb) and K <= MM_WHOLE_K:
        tk = K
        fits = lambda: 4 * tk * (tm + tn) + 8 * tm * tn * (2 if add is not None else 1) <= MM_TILE_BYTES
        while not fits():
            if (tm >= tn or b3 or out3) and tm > 256:
                tm = _pick(M, tm - 128)
            elif not (b3 or out3) and tn > 256:
                tn = _pick(Nn, tn - 128)
            else:
                break
    nk = K // tk
    j_outer = K * Nn + (Nn // tn) * M * K < M * K + (M // tm) * K * Nn
    dn = (((0 if ta else 1,), (1 if tb else 0,)), ((), ()))

    def body(*refs):
        refs = list(refs)
        a_ref, b_ref = refs[:2]
        add_ref = refs[2] if add is not None else None
        o_ref = refs[3] if add is not None else refs[2]
        part = lax.dot_general(a_ref[...].astype(BF16), b_ref[...].astype(BF16), dn, preferred_element_type=F32)

        def finish(r):
            if add is not None:
                r = r + add_ref[...]
            o_ref[...] = r.astype(out_dtype)

        if nk == 1:
            finish(part)
            return
        acc = refs[-1]
        k = pl.program_id(2)

        @pl.when(k == 0)
        def _():
            acc[...] = part

        @pl.when(k > 0)
        def _():
            acc[...] += part

        @pl.when(k == nk - 1)
        def _():
            finish(acc[...])

    def spec(shape, index):
        return pl.BlockSpec(shape, (lambda j, i, k: index(i, j, k)) if j_outer else index)

    a_spec = spec((tk, tm), lambda i, j, k: (k, i)) if ta else spec((tm, tk), lambda i, j, k: (i, k))
    if b3 and tb:
        b_spec = spec((None, tn, tk), lambda i, j, k: (k, j, 0))
    elif b3:
        b_spec = spec((None, tk, tn), lambda i, j, k: (j, k, 0))
    elif tb:
        b_spec = spec((tn, tk), lambda i, j, k: (j, k))
    else:
        b_spec = spec((tk, tn), lambda i, j, k: (k, j))
    if out3:
        o_spec = spec((None, tm, tn), lambda i, j, k: (j, i, 0))
    else:
        o_spec = spec((tm, tn), lambda i, j, k: (i, j))
    ins, specs = [a, b], [a_spec, b_spec]
    if add is not None:
        ins.append(add)
        specs.append(o_spec)
    grid = (Nn // tn, M // tm, nk) if j_outer else (M // tm, Nn // tn, nk)
    return _pcall(
        body, name=name, grid=grid, in_specs=specs, out_specs=o_spec,
        out_shape=jax.ShapeDtypeStruct((out3, M, tn) if out3 else (M, Nn), out_dtype),
        scratch_shapes=[pltpu.VMEM((tm, tn), F32)] if nk > 1 else [],
        compiler_params=_params(("parallel", "parallel", "arbitrary")),
    )(*ins)


def _mm_multi(a_list, b_list, *, name, M, T=None, add=None, out_dtype=F32, tm=512, post=None):
    tm = _pick(T or M, tm)
    ns = len(a_list)
    dils = [a[1] if isinstance(a, tuple) else 0 for a in a_list]
    a_arrs = [a[0] if isinstance(a, tuple) else a for a in a_list]
    widths = [a.shape[-1] // max(r, 1) for a, r in zip(a_arrs, dils)]
    b_arrs, b_specs = [], []
    for b in b_list:
        arr, shp, idx = b if isinstance(b, tuple) else (b, b.shape, (0, 0))
        b_arrs.append(arr)
        b_specs.append(pl.BlockSpec(shp, lambda i, idx=idx: idx))
    Nn = b_specs[0].block_shape[1]
    dn = (((1,), (0,)), ((), ()))
    nmm = 2 * ns + (1 if add is not None else 0)
    p_arrs, p_in_specs, p_out_specs, p_out_shape, p_fn = post or ([], [], None, None, None)
    nin = nmm + len(p_arrs)
    nout = len(p_out_specs) if post else 1

    def body(*refs):
        a_refs, b_refs, scr = refs[:ns], refs[ns:2 * ns], list(refs[nin + nout:])
        acc = None
        for a_ref, b_ref, r in zip(a_refs, b_refs, dils):
            av = _from_view(a_ref, scr.pop(0), r) if r > 1 else a_ref[...]
            part = lax.dot_general(av.astype(BF16), b_ref[...], dn, preferred_element_type=F32)
            acc = part if acc is None else acc + part
        if add is not None:
            acc = acc + refs[2 * ns][...]
        if post:
            p_fn(acc, refs[nmm:nin], refs[nin:nin + nout])
        else:
            refs[nin][...] = acc.astype(out_dtype)

    tpe = (T or M) // tm
    a_specs = [pl.BlockSpec((None, tm // r, r * w), lambda i: (i // tpe, i % tpe, 0)) if r
               else pl.BlockSpec((tm, w), lambda i: (i, 0)) for r, w in zip(dils, widths)]
    o_spec = pl.BlockSpec((tm, Nn), lambda i: (i, 0))
    specs = a_specs + b_specs
    ins = a_arrs + b_arrs
    if add is not None:
        specs.append(o_spec)
        ins.append(add)
    scratch = [pltpu.VMEM((w // LANES, tm, LANES), F32) for r, w in zip(dils, widths) if r > 1]
    return _pcall(body, name=name, grid=(M // tm,), in_specs=specs + list(p_in_specs),
                  out_specs=list(p_out_specs) if post else o_spec, scratch_shapes=scratch,
                  out_shape=list(p_out_shape) if post else jax.ShapeDtypeStruct((M, Nn), out_dtype),
                  compiler_params=_params(("arbitrary",) if post else ("parallel",)))(*ins, *p_arrs)


def _dw_view(d3, u, r, *, name, tk=1024):
    NB, tsub, rw = d3.shape
    W, T, D = rw // r, tsub * r, u.shape[1]
    tk = _pick(T, tk)
    tpe, nk = T // tk, NB * T // tk

    def body(d_ref, u_ref, o_ref, acc, scr):
        k = pl.program_id(0)
        dv = _from_view(d_ref, scr, r).astype(BF16)
        part = lax.dot_general(dv, u_ref[...], _TN, preferred_element_type=F32)

        @pl.when(k == 0)
        def _():
            acc[...] = part

        @pl.when(k > 0)
        def _():
            acc[...] += part

        @pl.when(k == nk - 1)
        def _():
            o_ref[...] = acc[...].astype(o_ref.dtype)

    return _pcall(
        body, name=name, grid=(nk,),
        in_specs=[pl.BlockSpec((None, tk // r, rw), lambda k: (k // tpe, k % tpe, 0)), pl.BlockSpec((tk, D), lambda k: (k, 0))],
        out_specs=pl.BlockSpec((W, D), lambda k: (0, 0)), out_shape=jax.ShapeDtypeStruct((W, D), BF16),
        scratch_shapes=[pltpu.VMEM((W, D), F32), pltpu.VMEM((W // LANES, tk, LANES), F32)],
        compiler_params=_params(("arbitrary",)))(d3, u)


def _lane(shape):
    return lax.broadcasted_iota(jnp.int32, shape, len(shape) - 1)


def _rot_half(v):
    w = v.shape[-1]
    first = (_lane(v.shape) % HEAD_DIM) < (HEAD_DIM // 2)
    return jnp.where(first, pltpu.roll(v, w - HEAD_DIM // 2, v.ndim - 1), pltpu.roll(v, HEAD_DIM // 2, v.ndim - 1))


def _widen(t, w):
    return t if w == t.shape[-1] else jnp.concatenate([t] * (w // t.shape[-1]), axis=-1)


def _unrope(v, cos, sins):
    w = v.shape[-1]
    return v * _widen(cos, w) - _rot_half(v) * _widen(sins, w)


def _rope_tables(positions):
    half = HEAD_DIM // 2
    inv = ROPE_THETA ** (-jnp.arange(half, dtype=F32) / half)
    ang = positions.astype(F32)[..., None] * inv
    cos, sin = jnp.cos(ang), jnp.sin(ang)
    cosf = jnp.concatenate([cos, cos, cos, cos], axis=-1)
    sins = jnp.concatenate([-sin, sin, -sin, sin], axis=-1)
    n = positions.shape[0] * positions.shape[1]
    return cosf.reshape(n, PAIR_W), sins.reshape(n, PAIR_W)


def _inproj(x2, scale, shift, w, cosf, sins, flags, *, T, name):
    N, D = x2.shape
    tm, tn = _pick(T, 512), VAR_W
    tpe = T // tm

    def body(x_ref, sc_ref, sh_ref, w_ref, c_ref, s_ref, f_ref, *outs):
        o_refs, u_ref = outs[:N_VAR], outs[N_VAR]
        j = pl.program_id(1)

        @pl.when(j == 0)
        def _():
            u_ref[...] = (x_ref[...] * (1.0 + sc_ref[0]) + sh_ref[0]).astype(BF16)

        acc = lax.dot_general(u_ref[...], w_ref[...], (((1,), (1,)), ((), ())), preferred_element_type=F32)
        fl = f_ref[...]
        ce = 1.0 + (_widen(c_ref[...], tn) - 1.0) * fl
        se = _widen(s_ref[...], tn) * fl
        res = acc * ce + _rot_half(acc) * se
        for v in range(N_VAR):
            @pl.when(j == v)
            def _(v=v):
                _to_view(res, o_refs[v], outs[N_VAR + 1], VAR_DIL[v])

    ex = pl.BlockSpec((1, 1, D), lambda i, j: (i // tpe, 0, 0))
    tab = pl.BlockSpec((tm, PAIR_W), lambda i, j: (i, 0))
    keep = lambda w_: pl.BlockSpec((tm, w_), lambda i, j: (i, 0))
    vspec = lambda r: pl.BlockSpec((None, tm // r, r * tn), lambda i, j: (i // tpe, i % tpe, 0))
    vshape = lambda r: jax.ShapeDtypeStruct((N // T, T // r, r * tn), BF16)
    return _pcall(
        body, name=name, grid=(N // tm, N_VAR),
        in_specs=[keep(D), ex, ex, pl.BlockSpec((tn, D), lambda i, j: (j, 0)), tab, tab,
                  pl.BlockSpec((1, tn), lambda i, j: (0, j))],
        out_specs=[vspec(r) for r in VAR_DIL] + [keep(D)],
        out_shape=[vshape(r) for r in VAR_DIL] + [jax.ShapeDtypeStruct((N, D), BF16)],
        scratch_shapes=[pltpu.VMEM((tn // LANES, tm, LANES), F32)],
        compiler_params=_params(("parallel", "arbitrary")),
    )(x2, scale, shift, w, cosf, sins, flags)


class _Geom:
    def __init__(self, g):
        if g is None:
            self.r, self.nq, self.n_back, self.sink = 1, A_Q_HEADS, A_WINDOW - 1, True
            self.qw, self.kw = QA_W, KA_W
            self.qidx = lambda j: 0
            self.kidx = lambda j: QA_W // KA_W
            self.vidx = lambda j: QA_W // KA_W + 1
        else:
            window, r = B_PATTERNS[g]
            self.r, self.nq, self.n_back, self.sink = r, B_GROUP_HEADS, window // r, False
            self.qw, self.kw = GB_W, GB_W
            self.qidx = lambda j: 3 * j
            self.kidx = lambda j: 3 * j + 1
            self.vidx = lambda j: 3 * j + 2
        self.ntile = self.qw // PAIR_W


def _stack_heads(t, scale=None):
    first = _lane(t.shape) < HEAD_DIM
    z = jnp.zeros_like(t)
    if scale is not None:
        t = t * jnp.asarray(scale, t.dtype)
    return jnp.concatenate([jnp.where(first, t, z), jnp.where(first, z, t)], axis=0)


def _lse_col(t):
    return jnp.concatenate([t[:, 0:1], t[:, HEAD_DIM:HEAD_DIM + 1]], axis=0)


def _lse_rows(t, width):
    first = _lane(t.shape) < HEAD_DIM
    other = pltpu.roll(t, HEAD_DIM, 1)
    full = jnp.concatenate([jnp.where(first, t, other), jnp.where(first, other, t)], axis=0)
    return _widen(full, width)


def _unstack_heads(v2):
    return jnp.where(_lane((BLOCK, PAIR_W)) < HEAD_DIM, v2[:BLOCK], v2[BLOCK:])


def _dup_head(t, kh):
    tf = t.astype(F32)
    keep = (_lane(t.shape) < HEAD_DIM) if kh == 0 else (_lane(t.shape) >= HEAD_DIM)
    return jnp.where(keep, tf, pltpu.roll(tf, HEAD_DIM, 1)).astype(t.dtype)


def _fold_heads(t):
    return t + pltpu.roll(t, HEAD_DIM, 1)


def _band_mask(rows, i, n_back, single):
    nkeys = BLOCK if single else 2 * BLOCK
    qi = jnp.bitwise_and(lax.broadcasted_iota(jnp.int32, (rows, nkeys), 0), BLOCK - 1)
    ki = lax.broadcasted_iota(jnp.int32, (rows, nkeys), 1)
    if single:
        return qi >= ki
    dist = qi + BLOCK - ki
    return jnp.logical_and(jnp.logical_and(dist >= 0, dist <= n_back), jnp.logical_or(ki >= BLOCK, i > 0))


def _per_block(col, scalars, fn):
    return jnp.concatenate([fn(col[b * BLOCK:(b + 1) * BLOCK], sc) for b, sc in enumerate(scalars)], axis=0)


def _sink_slot(rows):
    qi = jnp.bitwise_and(lax.broadcasted_iota(jnp.int32, (rows, 2 * BLOCK), 0), BLOCK - 1)
    return qi == lax.broadcasted_iota(jnp.int32, (rows, 2 * BLOCK), 1)


def _sink_scores(rows, sinks):
    blk = lax.broadcasted_iota(jnp.int32, (rows, 2 * BLOCK), 0) // BLOCK
    out = jnp.full((rows, 2 * BLOCK), sinks[-1], F32)
    for b in range(len(sinks) - 2, -1, -1):
        out = jnp.where(blk == b, sinks[b], out)
    return out


def _softmax_parts(s, valid, sinks):
    s = jnp.where(valid, s, NEG_INF)
    if sinks is not None:
        slot = _sink_slot(s.shape[0])
        s = jnp.where(slot, _sink_scores(s.shape[0], sinks), s)
    m = jnp.max(s, axis=1, keepdims=True)
    p = jnp.exp(s - m)
    den = jnp.sum(p, axis=1, keepdims=True)
    if sinks is not None:
        p = jnp.where(slot, 0.0, p)
    return p, m, den


_NT = (((1,), (1,)), ((), ()))
_TN = (((0,), (0,)), ((), ()))


def _rows2(prev_ref, cur_ref, cs, single=False):
    if single:
        return cur_ref[0, :, cs]
    return jnp.concatenate([prev_ref[0, :, cs], cur_ref[0, :, cs]], axis=0)


def _sink_scalars(sink_ref, first, nblocks):
    return [sink_ref[first + b] for b in range(nblocks)]


def _tile(t):
    return slice(t * PAIR_W, (t + 1) * PAIR_W)


def _attn_fwd(qkv, sinks, g, *, NB, T, name):
    geo = _Geom(g)
    r, qw, kw, ntile = geo.r, geo.qw, geo.kw, geo.ntile
    tsub = T // r
    nblk = tsub // BLOCK
    qkv3 = qkv.reshape(NB, tsub, r * VAR_W)
    out_dtype = BF16 if g is None else F32
    tiles_per_kv = ntile // A_KV_HEADS

    single = nblk == 1

    def body(q_ref, kp_ref, kc_ref, vp_ref, vc_ref, sink_ref, o_ref, l_ref):
        i = pl.program_id(2)
        if geo.sink:
            kall, vall = _rows2(kp_ref, kc_ref, _tile(0)), _rows2(vp_ref, vc_ref, _tile(0))
            kdup = [_dup_head(kall, kh) for kh in range(A_KV_HEADS)]
            vdup = [_dup_head(vall, kh) for kh in range(A_KV_HEADS)]
            tiles = [[t] for t in range(ntile)]
            q2s = [_stack_heads(q_ref[0, :, _tile(t)], SCALE) for t in range(ntile)]
            kks = [kdup[t // tiles_per_kv] for t in range(ntile)]
            vvs = [vdup[t // tiles_per_kv] for t in range(ntile)]
            sinkcols = [_sink_scalars(sink_ref, 2 * t, 2) for t in range(ntile)]
        else:
            tiles = [[t] for t in range(ntile)]
            q2s = [_stack_heads(q_ref[0, :, _tile(t)], SCALE) for t in range(ntile)]
            kks = [_rows2(kp_ref, kc_ref, _tile(t), single) for t in range(ntile)]
            vvs = [_rows2(vp_ref, vc_ref, _tile(t), single) for t in range(ntile)]
            sinkcols = [None] * ntile
        valid = _band_mask(q2s[0].shape[0], i, geo.n_back, single)
        ss = [lax.dot_general(q2, kk, _NT, preferred_element_type=F32) for q2, kk in zip(q2s, kks)]
        parts = [_softmax_parts(s, valid, sc) for s, sc in zip(ss, sinkcols)]
        o2s = [jnp.dot(p.astype(BF16), vv, preferred_element_type=F32) / den for (p, m, den), vv in zip(parts, vvs)]
        for ts, o2, (p, m, den) in zip(tiles, o2s, parts):
            lse2 = jnp.broadcast_to(m + jnp.log(den), (o2.shape[0], PAIR_W))
            for n, t in enumerate(ts):
                rows = slice(2 * BLOCK * n, 2 * BLOCK * (n + 1))
                o_ref[0, :, _tile(t)] = _unstack_heads(o2[rows]).astype(out_dtype)
                l_ref[0, :, _tile(t)] = _unstack_heads(lse2[rows])

    prev = lambda i: jnp.maximum(i - 1, 0)
    in_specs = [
        pl.BlockSpec((1, BLOCK, qw), lambda b, j, i: (b, i, geo.qidx(j))),
        pl.BlockSpec((1, BLOCK, kw), lambda b, j, i: (b, prev(i), geo.kidx(j))),
        pl.BlockSpec((1, BLOCK, kw), lambda b, j, i: (b, i, geo.kidx(j))),
        pl.BlockSpec((1, BLOCK, kw), lambda b, j, i: (b, prev(i), geo.vidx(j))),
        pl.BlockSpec((1, BLOCK, kw), lambda b, j, i: (b, i, geo.vidx(j))),
        pl.BlockSpec(memory_space=pltpu.SMEM),
    ]
    o_spec = pl.BlockSpec((1, BLOCK, qw), lambda b, j, i: (b, i, j))
    shape = (NB, tsub, r * qw)
    o, lse = _pcall(
        body, name=name, grid=(NB, r, nblk), in_specs=in_specs, out_specs=[o_spec, o_spec],
        out_shape=[jax.ShapeDtypeStruct(shape, out_dtype), jax.ShapeDtypeStruct(shape, F32)],
        compiler_params=_params(("parallel", "parallel", "arbitrary")),
    )(qkv3, qkv3, qkv3, qkv3, qkv3, sinks)
    return o, lse


def _attn_fwd_b(qkvs, *, NB, T, name):
    geos = [_Geom(g) for g in range(len(B_PATTERNS))]
    steps = T // BLOCK
    nt = GB_W // PAIR_W
    ng = len(geos)

    def where(geo, s):
        nblk = T // geo.r // BLOCK
        return s // steps, (s % steps) // nblk, (s % steps) % nblk

    def body(*refs):
        ins, outs = refs[:5 * ng], refs[5 * ng:]
        s = pl.program_id(0)
        q2s, kks, vvs, valids = [], [], [], []
        for n, geo in enumerate(geos):
            q_ref, kp_ref, kc_ref, vp_ref, vc_ref = ins[5 * n:5 * n + 5]
            single = T // geo.r // BLOCK == 1
            valid = _band_mask(2 * BLOCK, where(geo, s)[2], geo.n_back, single)
            for t in range(nt):
                q2s.append(_stack_heads(q_ref[0, :, _tile(t)], SCALE))
                kks.append(_rows2(kp_ref, kc_ref, _tile(t), single))
                vvs.append(_rows2(vp_ref, vc_ref, _tile(t), single))
                valids.append(valid)
        ss = [lax.dot_general(q2, kk, _NT, preferred_element_type=F32) for q2, kk in zip(q2s, kks)]
        parts = [_softmax_parts(sc, valid, None) for sc, valid in zip(ss, valids)]
        o2s = [jnp.dot(p.astype(BF16), vv, preferred_element_type=F32) / den for (p, m, den), vv in zip(parts, vvs)]
        for n in range(ng):
            o_ref, l_ref = outs[2 * n], outs[2 * n + 1]
            for t in range(nt):
                o2, (p, m, den) = o2s[n * nt + t], parts[n * nt + t]
                o_ref[0, :, _tile(t)] = _unstack_heads(o2)
                l_ref[0, :, _tile(t)] = _unstack_heads(jnp.broadcast_to(m + jnp.log(den), (2 * BLOCK, PAIR_W)))

    in_specs, ins, out_specs, out_shape = [], [], [], []
    for geo, qkv in zip(geos, qkvs):
        tsub = T // geo.r
        pos = lambda s, geo=geo: where(geo, s)
        prev = lambda i: jnp.maximum(i - 1, 0)
        blk = lambda col, back, pos=pos: pl.BlockSpec(
            (1, BLOCK, GB_W), lambda s: (pos(s)[0], prev(pos(s)[2]) if back else pos(s)[2], col(pos(s)[1])))
        in_specs += [blk(geo.qidx, False), blk(geo.kidx, True), blk(geo.kidx, False), blk(geo.vidx, True),
                     blk(geo.vidx, False)]
        ins += [qkv.reshape(NB, tsub, geo.r * VAR_W)] * 5
        out_specs += [blk(lambda j: j, False)] * 2
        out_shape += [jax.ShapeDtypeStruct((NB, tsub, geo.r * GB_W), F32)] * 2
    res = _pcall(body, name=name, grid=(NB * steps,), in_specs=in_specs, out_specs=out_specs, out_shape=out_shape,
                 compiler_params=_params(("arbitrary",)))(*ins)
    return [(res[2 * n], res[2 * n + 1]) for n in range(ng)]


def _attn_bwd(qkv, do, lse, dlse, cosf, sins, sinks, g, *, NB, T, name):
    geo = _Geom(g)
    r, qw, kw, ntile = geo.r, geo.qw, geo.kw, geo.ntile
    tsub = T // r
    nblk = tsub // BLOCK
    view = lambda a, w: a.reshape(NB, tsub, r * w)
    has_dlse = dlse is not None
    tiles_per_kv = ntile // A_KV_HEADS

    single = nblk == 1
    krows = BLOCK if single else 2 * BLOCK
    nsteps = 1 if single else nblk + 1

    def grads(q2s, kks, vvs, do2s, i, lserows, sinkcols, dlrows):
        nrow = q2s[0].shape[0]
        ki = lax.broadcasted_iota(jnp.int32, (krows, nrow), 0)
        qi = jnp.bitwise_and(lax.broadcasted_iota(jnp.int32, (krows, nrow), 1), BLOCK - 1)
        if single:
            valid = qi >= ki
        else:
            dist = qi + BLOCK - ki
            valid = jnp.logical_and(jnp.logical_and(dist >= 0, dist <= geo.n_back), jnp.logical_or(ki >= BLOCK, i > 0))
        sts = [lax.dot_general(kk, q2, _NT, preferred_element_type=F32) for q2, kk in zip(q2s, kks)]
        dpts = [lax.dot_general(vv, do2, _NT, preferred_element_type=F32) for do2, vv in zip(do2s, vvs)]
        pts, dsts, sks = [], [], []
        for st, dpt, ls, sc, dl in zip(sts, dpts, lserows, sinkcols, dlrows):
            sv = jnp.where(valid, st, NEG_INF)
            if sc is not None:
                slot = ki == qi
                blk = lax.broadcasted_iota(jnp.int32, (krows, nrow), 1) // BLOCK
                sink = jnp.full((krows, nrow), sc[-1], F32)
                for b in range(len(sc) - 2, -1, -1):
                    sink = jnp.where(blk == b, sc[b], sink)
                sv = jnp.where(slot, sink, sv)
                dpt = jnp.where(slot, 0.0, dpt)
            pt = jnp.exp(sv - ls)
            delta = jnp.sum(pt * dpt, axis=0, keepdims=True)
            if dl is not None:
                delta = delta - dl
            dst = pt * (dpt - delta)
            if sc is not None:
                cols = lambda a, b: a[:, b * BLOCK:(b + 1) * BLOCK]
                sks.append([jnp.sum(jnp.where(cols(slot, b), cols(dst, b), 0.0)) for b in range(len(sc))])
                dst, pt = jnp.where(slot, 0.0, dst), jnp.where(slot, 0.0, pt)
            else:
                sks.append(None)
            pts.append(pt.astype(BF16))
            dsts.append(dst.astype(BF16))
        dq2s = [lax.dot_general(dst, kk, _TN, preferred_element_type=F32) * SCALE for dst, kk in zip(dsts, kks)]
        dkks = [jnp.dot(dst, q2, preferred_element_type=F32) for dst, q2 in zip(dsts, q2s)]
        dvvs = [jnp.dot(pt, do2, preferred_element_type=F32) for pt, do2 in zip(pts, do2s)]
        return dq2s, dkks, dvvs, sks

    def stat_row(t):
        tt = t.T
        return jnp.concatenate([tt[0:1, :], tt[HEAD_DIM:HEAD_DIM + 1, :]], axis=1)

    def body(*refs):
        it = iter(refs)
        q_ref, kp_ref, kc_ref, vp_ref, vc_ref, do_ref, l_ref = (next(it) for _ in range(7))
        dl_ref = next(it) if has_dlse else None
        c_ref, s_ref, sink_ref, o_ref, ds_ref, dq_s, dk_s, dv_s, car_q, car_k, car_v = (next(it) for _ in range(11))
        b, j, i = pl.program_id(0), pl.program_id(1), pl.program_id(2)

        @pl.when(jnp.logical_and(b == 0, jnp.logical_and(j == 0, i == 0)))
        def _():
            ds_ref[...] = jnp.zeros_like(ds_ref)

        def compute():
            if geo.sink:
                kall, vall = _rows2(kp_ref, kc_ref, _tile(0)), _rows2(vp_ref, vc_ref, _tile(0))
                tps = tiles_per_kv // A_BWD_SPLIT
                nb = 2 * tps
                tiles = [[kh * tiles_per_kv + s_ * tps + t for t in range(tps)]
                         for kh in range(A_KV_HEADS) for s_ in range(A_BWD_SPLIT)]
                kdup = [_dup_head(kall, kh) for kh in range(A_KV_HEADS)]
                vdup = [_dup_head(vall, kh) for kh in range(A_KV_HEADS)]
                cat = lambda f, ts: jnp.concatenate([f(t) for t in ts], axis=0)
                dq2s, dkks, dvvs, sks = grads(
                    [cat(lambda t: _stack_heads(q_ref[0, :, _tile(t)], SCALE), ts) for ts in tiles],
                    [kdup[n // A_BWD_SPLIT] for n in range(len(tiles))],
                    [vdup[n // A_BWD_SPLIT] for n in range(len(tiles))],
                    [cat(lambda t: _stack_heads(do_ref[0, :, _tile(t)]), ts) for ts in tiles], i,
                    [jnp.concatenate([stat_row(l_ref[0, :, _tile(t)]) for t in ts], axis=1) for ts in tiles],
                    [_sink_scalars(sink_ref, 2 * ts[0], nb) for ts in tiles], [None] * len(tiles))
                lane1 = _lane((1, PAIR_W))
                dsink = jnp.zeros((1, PAIR_W), F32)
                for ts, dq2, sk in zip(tiles, dq2s, sks):
                    for n, t in enumerate(ts):
                        dq_s[:, _tile(t)] = _unstack_heads(dq2[2 * BLOCK * n:2 * BLOCK * (n + 1)])
                    for bb in range(nb):
                        dsink = dsink + jnp.where(lane1 == 2 * ts[0] + bb, sk[bb], 0.0)
                per_kv = lambda parts, kh: functools.reduce(jnp.add, parts[kh * A_BWD_SPLIT:(kh + 1) * A_BWD_SPLIT])
                second = _lane((krows, PAIR_W)) >= HEAD_DIM
                dk_s[...] = jnp.where(second, _fold_heads(per_kv(dkks, 1)), _fold_heads(per_kv(dkks, 0)))
                dv_s[...] = jnp.where(second, _fold_heads(per_kv(dvvs, 1)), _fold_heads(per_kv(dvvs, 0)))
                ds_ref[0:1, :] += dsink
            else:
                dq2s, dkks, dvvs, _ = grads(
                    [_stack_heads(q_ref[0, :, _tile(t)], SCALE) for t in range(ntile)],
                    [_rows2(kp_ref, kc_ref, _tile(t), single) for t in range(ntile)],
                    [_rows2(vp_ref, vc_ref, _tile(t), single) for t in range(ntile)],
                    [_stack_heads(do_ref[0, :, _tile(t)]) for t in range(ntile)], i,
                    [stat_row(l_ref[0, :, _tile(t)]) for t in range(ntile)], [None] * ntile,
                    [stat_row(dl_ref[0, :, _tile(t)]) for t in range(ntile)])
                for t in range(ntile):
                    dq_s[:, _tile(t)] = _unstack_heads(dq2s[t])
                    dk_s[0:krows, _tile(t)] = dkks[t]
                    dv_s[0:krows, _tile(t)] = dvvs[t]

        def emit(dq, dk, dv):
            cos, sn = c_ref[0], s_ref[0]
            o_ref[0, :, 0:qw] = _unrope(dq, cos, sn).astype(BF16)
            o_ref[0, :, qw:qw + kw] = _unrope(dk, cos, sn).astype(BF16)
            o_ref[0, :, qw + kw:qw + 2 * kw] = dv.astype(BF16)
            if qw + 2 * kw < VAR_W:
                o_ref[0, :, qw + 2 * kw:VAR_W] = jnp.zeros((BLOCK, VAR_W - qw - 2 * kw), BF16)

        if single:
            compute()
            emit(dq_s[...], dk_s[0:BLOCK, :], dv_s[0:BLOCK, :])
            return

        @pl.when(i == 0)
        def _():
            car_q[...] = jnp.zeros_like(car_q)
            car_k[...] = jnp.zeros_like(car_k)
            car_v[...] = jnp.zeros_like(car_v)

        @pl.when(i == nblk)
        def _():
            dk_s[...] = jnp.zeros_like(dk_s)
            dv_s[...] = jnp.zeros_like(dv_s)

        pl.when(i < nblk)(compute)
        emit(car_q[...], car_k[...] + dk_s[0:BLOCK, :], car_v[...] + dv_s[0:BLOCK, :])
        car_q[...] = dq_s[...]
        car_k[...] = dk_s[BLOCK:2 * BLOCK, :]
        car_v[...] = dv_s[BLOCK:2 * BLOCK, :]

    cur = lambda i: jnp.minimum(i, nblk - 1)
    prv = lambda i: jnp.maximum(jnp.minimum(i, nblk - 1) - 1, 0)
    outb = lambda i: jnp.maximum(i - 1, 0)
    qrow = pl.BlockSpec((1, BLOCK, qw), lambda b, j, i: (b, cur(i), j))
    in_specs = [
        pl.BlockSpec((1, BLOCK, qw), lambda b, j, i: (b, cur(i), geo.qidx(j))),
        pl.BlockSpec((1, BLOCK, kw), lambda b, j, i: (b, prv(i), geo.kidx(j))),
        pl.BlockSpec((1, BLOCK, kw), lambda b, j, i: (b, cur(i), geo.kidx(j))),
        pl.BlockSpec((1, BLOCK, kw), lambda b, j, i: (b, prv(i), geo.vidx(j))),
        pl.BlockSpec((1, BLOCK, kw), lambda b, j, i: (b, cur(i), geo.vidx(j))),
        qrow, qrow,
    ]
    ins = [view(qkv, VAR_W)] * 5 + [view(do, qw), view(lse, qw)]
    if has_dlse:
        in_specs.append(qrow)
        ins.append(view(dlse, qw))
    in_specs += [
        pl.BlockSpec((1, BLOCK, PAIR_W), lambda b, j, i: (b, outb(i), j)),
        pl.BlockSpec((1, BLOCK, PAIR_W), lambda b, j, i: (b, outb(i), j)),
        pl.BlockSpec(memory_space=pltpu.SMEM),
    ]
    ins += [view(cosf, PAIR_W), view(sins, PAIR_W), sinks]
    scratch = [pltpu.VMEM((BLOCK, qw), F32), pltpu.VMEM((2 * BLOCK, kw), F32), pltpu.VMEM((2 * BLOCK, kw), F32),
               pltpu.VMEM((BLOCK, qw), F32), pltpu.VMEM((BLOCK, kw), F32), pltpu.VMEM((BLOCK, kw), F32)]
    dqkv, dsink = _pcall(
        body, name=name, grid=(NB, r, nsteps), in_specs=in_specs,
        out_specs=[pl.BlockSpec((1, BLOCK, VAR_W), lambda b, j, i: (b, outb(i), j)),
                   pl.BlockSpec((8, PAIR_W), lambda b, j, i: (0, 0))],
        out_shape=[jax.ShapeDtypeStruct((NB, tsub, r * VAR_W), BF16), jax.ShapeDtypeStruct((8, PAIR_W), F32)],
        scratch_shapes=scratch, compiler_params=_params(("arbitrary", "arbitrary", "arbitrary")),
    )(*ins)
    return dqkv, dsink


class _Rows:
    def __init__(self, N, T, tm):
        self.N, self.tm, self.tpe, self.grid = N, tm, T // tm, (N // tm,)

    def row(self, w, col=0):
        return pl.BlockSpec((self.tm, w), lambda i: (i, col))

    def ex(self, w):
        return pl.BlockSpec((1, 1, w), lambda i: (i // self.tpe, 0, 0))

    def const(self, shape):
        return pl.BlockSpec(shape, lambda i: tuple(0 for _ in shape))

    def view(self, w, r):
        return pl.BlockSpec((None, self.tm // r, r * w), lambda i: (i // self.tpe, i % self.tpe, 0))

    def first_of_example(self):
        return pl.program_id(0) % self.tpe == 0


def _acc(ref, first, val):
    @pl.when(first)
    def _():
        ref[0] = val

    @pl.when(jnp.logical_not(first))
    def _():
        ref[0] += val


def _colsum(v):
    return jnp.sum(v, axis=0, keepdims=True)


def _ln_stats(r):
    mu = jnp.mean(r, axis=-1, keepdims=True)
    xc = r - mu
    var = jnp.mean(xc * xc, axis=-1, keepdims=True)
    rstd = lax.rsqrt(var + LN_EPS)
    return xc * rstd, rstd


def _ln_bwd(dy, xhat, rstd, gain):
    dxh = dy * gain
    return rstd * (dxh - jnp.mean(dxh, axis=-1, keepdims=True) - xhat * jnp.mean(dxh * xhat, axis=-1, keepdims=True))


def _from_view(ref, scr, r):
    if r == 1:
        return ref[...]
    rows, w = ref.shape[0], ref.shape[1] // r
    for j in range(r):
        for c in range(w // LANES):
            scr.at[c][pl.ds(j, rows, stride=r), :] = ref[:, j * w + c * LANES:j * w + (c + 1) * LANES].astype(F32)
    return jnp.concatenate([scr[c] for c in range(w // LANES)], axis=1)


def _to_view(val, ref, scr, r):
    if r == 1:
        ref[...] = val.astype(ref.dtype)
        return
    rows, w = ref.shape[0], ref.shape[1] // r
    for c in range(w // LANES):
        scr[c] = val[:, c * LANES:(c + 1) * LANES]
    for j in range(r):
        for c in range(w // LANES):
            ref[:, j * w + c * LANES:j * w + (c + 1) * LANES] = scr.at[c][pl.ds(j, rows, stride=r), :].astype(ref.dtype)


def _silu_parts(v):
    s = jax.nn.sigmoid(v)
    return v * s, s * (1.0 + v * (1.0 - s))


def _local_step(x, mod, positions, w_in, rest_weights, sinks, ln1_g, ln1_b, ln2_g, ln2_b, target, hook=None):
    hook = hook or (lambda event, **data: None)
    NB, T, D = x.shape
    N = NB * T
    x2 = x.reshape(N, D)
    tgt2 = target.reshape(N, D)
    shift_m, scale_m, gate_m, shift_f, scale_f, gate_f = [mod[:, None, k * D:(k + 1) * D] for k in range(6)]
    cosf, sins = _rope_tables(positions)
    col = jnp.arange(QKV_P)
    vcol = col % VAR_W
    flags = jnp.where(col < VAR_W, vcol < QA_W + KA_W, vcol < 2 * GB_W).astype(F32)[None]
    R = _Rows(N, T, _pick(T, 256))
    sds = jax.ShapeDtypeStruct
    exsum = lambda w=D: sds((NB, 1, w), F32)
    ngrp = len(B_PATTERNS)

    *qkv, u = _inproj(x2, scale_m, shift_m, w_in, cosf, sins, flags, T=T, name="inproj_qkv")
    gates = _mm(u, w_in[QKV_P:], tb=True, out_dtype=BF16, name="inproj_gates")
    oa, la = _attn_fwd(qkv[0], sinks, None, NB=NB, T=T, name="attn_a_fwd")
    oa = oa.reshape(N, QA_W)
    (o1, l1), (o2, l2), (o3, l3) = _attn_fwd_b(qkv[1:], NB=NB, T=T, name="attn_b_fwd")
    w_a, w_b, w_o, w_gu, w_d = rest_weights()
    F = w_d.shape[0]
    dil = [r_ for _, r_ in B_PATTERNS]
    views = [R.view(GB_W, r_) for r_ in dil]
    tokbuf = pltpu.VMEM((GB_W // LANES, R.tm, LANES), F32)

    f32 = lambda ref: ref[...].astype(F32)
    Rm = _Rows(N, T, _pick(T, 512))

    def mix_out(o1r, o2r, o3r, l1r, l2r, l3r, oa_r, ga_r, gb_r, x_r, gm_r, g_r, b_r, sf_r, hf_r, wa_r, wb_r, wo_r,
                ob_ref, ya_ref, yb_ref, mg_ref, y_ref, r1_ref, u2_ref, *bufs):
        os_ = [_from_view(ref, bufs[n], dil[n]) for n, ref in enumerate((o1r, o2r, o3r))]
        la, lb, lc = [_from_view(ref, bufs[3 + n], dil[n]) for n, ref in enumerate((l1r, l2r, l3r))]
        mx = jnp.maximum(jnp.maximum(la, lb), lc)
        ea, eb, ec = jnp.exp(la - mx), jnp.exp(lb - mx), jnp.exp(lc - mx)
        ob = ((ea * os_[0] + eb * os_[1] + ec * os_[2]) / (ea + eb + ec)).astype(BF16)
        ob_ref[...] = ob
        ya = jnp.dot(oa_r[...], wa_r[...], preferred_element_type=F32).astype(BF16)
        yb = jnp.concatenate([jnp.dot(ob, wb_r[s_], preferred_element_type=F32)
                              for s_ in range(w_b.shape[0])], axis=1).astype(BF16)
        merged = (jax.nn.sigmoid(f32(ga_r)) * ya.astype(F32) + jax.nn.sigmoid(f32(gb_r)) * yb.astype(F32)).astype(BF16)
        y = jnp.dot(merged, wo_r[...], preferred_element_type=F32)
        r1 = ALPHA * x_r[...] + (1.0 + gm_r[0]) * y
        xhat, _ = _ln_stats(r1)
        x1 = xhat * g_r[...] + b_r[...]
        ya_ref[...], yb_ref[...], mg_ref[...], y_ref[...], r1_ref[...] = ya, yb, merged, y, r1
        u2_ref[...] = (x1 * (1.0 + sf_r[0]) + hf_r[0]).astype(BF16)

    mviews = [Rm.view(GB_W, r_) for r_ in dil]
    ob, ya, yb, merged, y, r1, u2 = _pcall(
        mix_out, name="mix_out", grid=Rm.grid,
        in_specs=mviews + mviews + [Rm.row(QA_W), Rm.row(D, 0), Rm.row(D, 1), Rm.row(D), Rm.ex(D), Rm.const((1, D)),
                                    Rm.const((1, D)), Rm.ex(D), Rm.ex(D), Rm.const(w_a.shape), Rm.const(w_b.shape),
                                    Rm.const(w_o.shape)],
        out_specs=[Rm.row(GB_W)] + [Rm.row(D)] * 6,
        out_shape=[sds((N, GB_W), BF16)] + [sds((N, D), BF16)] * 3 + [sds((N, D), F32)] * 2 + [sds((N, D), BF16)],
        scratch_shapes=[pltpu.VMEM((GB_W // LANES, Rm.tm, LANES), F32)] * 6,
        compiler_params=_params(("parallel",)))(o1, o2, o3, l1, l2, l3, oa, gates, gates, x2, gate_m, ln1_g, ln1_b,
                                                scale_f, shift_f, w_a, w_b, w_o)

    w_gu = w_gu() if callable(w_gu) else w_gu
    tnf = w_gu.shape[2]
    nft = w_gu.shape[0] // 2
    tmf = _pick(N, 512)

    def ffn_up(u_r, wg_r, wu_r, hg_ref, hu_ref, a_ref):
        hg = jnp.dot(u_r[...], wg_r[...], preferred_element_type=F32)
        hu = jnp.dot(u_r[...], wu_r[...], preferred_element_type=F32)
        sl, _ = _silu_parts(hg)
        hg_ref[...] = hg.astype(BF16)
        hu_ref[...] = hu.astype(BF16)
        a_ref[...] = (sl * hu).astype(BF16)

    ftile = pl.BlockSpec((tmf, tnf), lambda j, i: (i, j))
    hg, hu, act = _pcall(
        ffn_up, name="ffn_up", grid=(nft, N // tmf),
        in_specs=[pl.BlockSpec((tmf, D), lambda j, i: (i, 0)), pl.BlockSpec((None, D, tnf), lambda j, i: (j, 0, 0)),
                  pl.BlockSpec((None, D, tnf), lambda j, i: (j + nft, 0, 0))],
        out_specs=[ftile] * 3, out_shape=[sds((N, F), BF16)] * 3,
        compiler_params=_params(("arbitrary", "parallel")))(u2, w_gu, w_gu)
    fchunk = _pick(F, 768)

    def ffn_down_norm2(act_r, wd_r, r1_r, g1_r, b1_r, t_r, gf_r, g_r, b_r, hg_r, hu_r,
                       dy2_ref, dx1_ref, dgf_ref, dg_ref, db_ref, loss_ref, dh_ref):
        first = R.first_of_example()
        y2v = jnp.dot(act_r[...], wd_r[...], preferred_element_type=F32)
        x1 = _ln_stats(r1_r[...])[0] * g1_r[...] + b1_r[...]
        r2 = ALPHA * x1 + (1.0 + gf_r[0]) * y2v
        xhat, rstd = _ln_stats(r2)
        err = xhat * g_r[...] + b_r[...] - t_r[...]
        dx2 = err * (1.0 / D)
        dr2 = _ln_bwd(dx2, xhat, rstd, g_r[...])
        dy2 = ((1.0 + gf_r[0]) * dr2).astype(BF16)
        dy2_ref[...] = dy2
        dx1_ref[...] = ALPHA * dr2
        _acc(dgf_ref, first, _colsum(dr2 * y2v))
        _acc(dg_ref, first, _colsum(dx2 * xhat))
        _acc(db_ref, first, _colsum(dx2))
        part = 0.5 * jnp.sum(jnp.mean(err * err, axis=-1, keepdims=True))
        _acc(loss_ref, first, jnp.broadcast_to(part, (1, 128)))
        for t in range(F // fchunk):
            cs = slice(t * fchunk, (t + 1) * fchunk)
            da = lax.dot_general(dy2, wd_r[cs, :], _NT, preferred_element_type=F32)
            sl, dsl = _silu_parts(hg_r[:, cs].astype(F32))
            dh_ref[:, cs] = (da * hu_r[:, cs].astype(F32) * dsl).astype(BF16)
            dh_ref[:, F + t * fchunk:F + (t + 1) * fchunk] = (da * sl).astype(BF16)

    dy2, dx1p, dgate_f, dg2, db2, loss_p, dh = _pcall(
        ffn_down_norm2, name="ffn_down_norm2", grid=R.grid,
        in_specs=[R.row(F), R.const((F, D)), R.row(D), R.const((1, D)), R.const((1, D)), R.row(D), R.ex(D),
                  R.const((1, D)), R.const((1, D)), R.row(F), R.row(F)],
        out_specs=[R.row(D), R.row(D), R.ex(D), R.ex(D), R.ex(D), R.ex(128), R.row(2 * F)],
        out_shape=[sds((N, D), BF16), sds((N, D), F32), exsum(), exsum(), exsum(), exsum(128), sds((N, 2 * F), BF16)],
        compiler_params=_params(("arbitrary",)))(act, w_d, r1, ln1_g, ln1_b, tgt2, gate_f, ln2_g, ln2_b, hg, hu)

    g_wd = _mm(act, dy2, ta=True, out_dtype=BF16, name="ffn_down_dw")
    g_wgu = _mm(u2, dh, ta=True, out3=w_gu.shape[0], out_dtype=BF16, name="ffn_up_dw")

    def ffn_up_dx_norm1(dh_r, w_r, dx1p_r, r1_r, y_r, sf_r, gm_r, g_r, b_r,
                        dxp_ref, dy_ref, dsf_ref, dhf_ref, dgm_ref, dg_ref, db_ref):
        first = R.first_of_example()
        du2v = None
        for s_ in range(w_gu.shape[0]):
            part = lax.dot_general(dh_r[:, s_ * tnf:(s_ + 1) * tnf], w_r[s_], _NT, preferred_element_type=F32)
            du2v = part if du2v is None else du2v + part
        dx1 = dx1p_r[...] + du2v * (1.0 + sf_r[0])
        xhat, rstd = _ln_stats(r1_r[...])
        dr1 = _ln_bwd(dx1, xhat, rstd, g_r[...])
        dxp_ref[...] = ALPHA * dr1
        dy_ref[...] = ((1.0 + gm_r[0]) * dr1).astype(BF16)
        _acc(dsf_ref, first, _colsum(du2v * (xhat * g_r[...] + b_r[...])))
        _acc(dhf_ref, first, _colsum(du2v))
        _acc(dgm_ref, first, _colsum(dr1 * y_r[...]))
        _acc(dg_ref, first, _colsum(dx1 * xhat))
        _acc(db_ref, first, _colsum(dx1))

    dxp, dy, dscale_f, dshift_f, dgate_m, dg1, db1 = _pcall(
        ffn_up_dx_norm1, name="ffn_up_dx_norm1", grid=R.grid,
        in_specs=[R.row(2 * F), R.const(w_gu.shape)] + [R.row(D)] * 3 + [R.ex(D), R.ex(D), R.const((1, D)),
                                                                        R.const((1, D))],
        out_specs=[R.row(D), R.row(D)] + [R.ex(D)] * 5,
        out_shape=[sds((N, D), F32), sds((N, D), BF16)] + [exsum()] * 5,
        compiler_params=_params(("arbitrary",)))(dh, w_gu, dx1p, r1, y, scale_f, gate_m, ln1_g, ln1_b)

    g_wo = _mm(merged, dy, ta=True, out_dtype=BF16, name="out_proj_dw")

    def mix_out_bwd(dy_r, ya_r, yb_r, ga_r, gb_r, wo_r, wa_r, wb_r, dya_ref, dyb_ref, dg_ref, doa_ref, dob_ref):
        dm = lax.dot_general(dy_r[...], wo_r[...], _NT, preferred_element_type=F32).astype(BF16).astype(F32)
        sa, sb = jax.nn.sigmoid(f32(ga_r)), jax.nn.sigmoid(f32(gb_r))
        dya, dyb = (dm * sa).astype(BF16), (dm * sb).astype(BF16)
        dya_ref[...], dyb_ref[...] = dya, dyb
        dg_ref[:, :D] = (dm * f32(ya_r) * sa * (1.0 - sa)).astype(BF16)
        dg_ref[:, D:] = (dm * f32(yb_r) * sb * (1.0 - sb)).astype(BF16)
        doa_ref[...] = lax.dot_general(dya, wa_r[...], _NT, preferred_element_type=F32).astype(BF16)
        ds_ = D // w_b.shape[0]
        dob = None
        for s_ in range(w_b.shape[0]):
            part = lax.dot_general(dyb[:, s_ * ds_:(s_ + 1) * ds_], wb_r[s_], _NT, preferred_element_type=F32)
            dob = part if dob is None else dob + part
        dob_ref[...] = dob

    dya, dyb, dgates, doa, dob = _pcall(
        mix_out_bwd, name="mix_out_bwd", grid=Rm.grid,
        in_specs=[Rm.row(D)] * 3 + [Rm.row(D, 0), Rm.row(D, 1), Rm.const(w_o.shape), Rm.const(w_a.shape),
                                    Rm.const(w_b.shape)],
        out_specs=[Rm.row(D), Rm.row(D), Rm.row(2 * D), Rm.row(QA_W), Rm.row(GB_W)],
        out_shape=[sds((N, D), BF16), sds((N, D), BF16), sds((N, 2 * D), BF16), sds((N, QA_W), BF16), sds((N, GB_W), F32)],
        compiler_params=_params(("parallel",)))(dy, ya, yb, gates, gates, w_o, w_a, w_b)

    g_wa = _mm(oa, dya, ta=True, out_dtype=BF16, name="branch_a_dw")
    g_wb = _mm(ob, dyb, ta=True, out3=w_b.shape[0], out_dtype=BF16, name="branch_b_dw")
    hook("rest_grads", g_wa=g_wa, g_wb=g_wb, g_wo=g_wo, g_wgu=g_wgu, g_wd=g_wd)

    seg = (jnp.arange(GB_W)[:, None] // HEAD_DIM == jnp.arange(GB_W)[None, :] // HEAD_DIM).astype(BF16)

    def merge_bwd(dob_r, o1r, o2r, o3r, l1r, l2r, l3r, seg_r, d1, d2, d3, e1, e2, e3, *bufs):
        dob_v = dob_r[...]
        os_ = [_from_view(ref, bufs[n], dil[n]) for n, ref in enumerate((o1r, o2r, o3r))]
        la, lb, lc = [_from_view(ref, bufs[3 + n], dil[n]) for n, ref in enumerate((l1r, l2r, l3r))]
        mx = jnp.maximum(jnp.maximum(la, lb), lc)
        ea, eb, ec = jnp.exp(la - mx), jnp.exp(lb - mx), jnp.exp(lc - mx)
        inv = 1.0 / (ea + eb + ec)
        ws = [ea * inv, eb * inv, ec * inv]

        def headsum(v):
            hi = v.astype(BF16)
            lo = (v - hi.astype(F32)).astype(BF16)
            sm = seg_r[...]
            return jnp.dot(hi, sm, preferred_element_type=F32) + jnp.dot(lo, sm, preferred_element_type=F32)

        dws = [headsum(dob_v * o) for o in os_]
        mean = ws[0] * dws[0] + ws[1] * dws[1] + ws[2] * dws[2]
        for n, (w_, dw_, d_ref, e_ref) in enumerate(zip(ws, dws, (d1, d2, d3), (e1, e2, e3))):
            _to_view(w_ * dob_v, d_ref, bufs[6], dil[n])
            _to_view(w_ * (dw_ - mean), e_ref, bufs[7], dil[n])

    vshape = lambda r_, dt: sds((NB, T // r_, r_ * GB_W), dt)
    mb = _pcall(
        merge_bwd, name="merge_bwd", grid=R.grid, in_specs=[R.row(GB_W)] + views + views + [R.const((GB_W, GB_W))],
        out_specs=views + views, out_shape=[vshape(r_, BF16) for r_ in dil] + [vshape(r_, F32) for r_ in dil],
        scratch_shapes=[tokbuf] * 8, compiler_params=_params(("parallel",)))(dob, o1, o2, o3, l1, l2, l3, seg)
    do_b, dlse_b = mb[:3], mb[3:]
    hook("merge_bwd_done")

    dqkv_a, dsink = _attn_bwd(qkv[0], doa, la, None, cosf, sins, sinks, None, NB=NB, T=T, name="attn_a_bwd")
    hook("attn_a_bwd_done")
    dqkv = [dqkv_a]
    for g in range(ngrp):
        dqkv.append(_attn_bwd(qkv[1 + g], do_b[g], (l1, l2, l3)[g], dlse_b[g], cosf, sins, sinks, g, NB=NB, T=T,
                              name=f"attn_b{g}_bwd")[0])
        hook(f"attn_b{g}_bwd_done")

    g_win = [_mm(d3.reshape(N, VAR_W), u, ta=True, out_dtype=BF16, name=f"inproj_dw{v}") if VAR_DIL[v] == 1
             else _dw_view(d3, u, VAR_DIL[v], name=f"inproj_dw{v}") for v, d3 in enumerate(dqkv)]
    g_win.append(_mm(dgates, u, ta=True, out_dtype=BF16, name=f"inproj_dw{N_VAR}"))
    hook("win_grads", g_win=g_win)
    wvar = lambda v: (w_in, (VAR_W, D), (v, 0))
    dview = lambda v: (dqkv[v], VAR_DIL[v])
    du = _mm_multi([dview(0)], [wvar(0)], M=N, T=T, name="inproj_dx0")
    hook("inproj_dx0_done")
    def x_bwd(duv, ins, outs):
        (dxp_r, x_r, sm_r), (gx_ref, dsm_ref, dhm_ref) = ins, outs
        first = R.first_of_example()
        gx_ref[...] = dxp_r[...] + duv * (1.0 + sm_r[0])
        _acc(dsm_ref, first, _colsum(duv * x_r[...]))
        _acc(dhm_ref, first, _colsum(duv))

    gx, dscale_m, dshift_m = _mm_multi(
        [dview(v) for v in range(1, N_VAR)] + [dgates],
        [wvar(v) for v in range(1, N_VAR)] + [(w_in, (2 * D, D), (QKV_P // (2 * D), 0))],
        M=N, T=T, add=du, tm=R.tm, name="inproj_dx1",
        post=([dxp, x2, scale_m], [R.row(D), R.row(D), R.ex(D)], [R.row(D), R.ex(D), R.ex(D)],
              [sds((N, D), F32), exsum(), exsum()], x_bwd))
    hook("inproj_dx1_done")

    dmod =jnp.concatenate([dshift_m, dscale_m, dgate_m, dshift_f, dscale_f, dgate_f], axis=-1)[:, 0]
    ln_grads = jnp.concatenate([dg1, db1, dg2, db2], axis=1)
    return dict(loss=loss_p[:, 0, 0], grad_x=gx.reshape(NB, T, D), g_win=g_win, g_wa=g_wa, g_wb=g_wb, g_wo=g_wo,
                g_wgu=g_wgu, g_wd=g_wd, dmod=dmod, ln_grads=ln_grads, dsink=dsink[0, :A_Q_HEADS])


def _coords():
    return lax.axis_index("x"), lax.axis_index("y"), lax.axis_index("c")


def _allgather_small(blk, *, name):
    m_per, n = blk.shape

    def body(x_ref, out_ref, send_sems, recv_sems, local_sem):
        x, y, c = _coords()
        me, sibling = (x, y, c), (x, y, 1 - c)
        chips = [(1 - x, y), (x, 1 - y), (1 - x, 1 - y)]

        def rows(px, py, pc):
            return out_ref.at[pl.ds((4 * px + 2 * py + pc) * m_per, m_per), :]

        def copy(k, block, to, src=None):
            return pltpu.make_async_remote_copy(
                src_ref=rows(*block) if src is None else src, dst_ref=rows(*block),
                send_sem=send_sems.at[k], recv_sem=recv_sems.at[k], device_id=to, device_id_type=MESH)

        mine = pltpu.make_async_copy(x_ref, rows(*me), local_sem)
        mine.start()
        first = [copy(0, me, sibling, src=x_ref)]
        first += [copy(1 + j, me, (*chip, c), src=x_ref) for j, chip in enumerate(chips)]
        for cp in first:
            cp.start()
        passed = [copy(4 + j, (*chip, c), sibling) for j, chip in enumerate(chips)]
        for j, chip in enumerate(chips):
            copy(1 + j, (*chip, c), me).wait_recv()
            passed[j].start()
        copy(0, sibling, me).wait_recv()
        for j, chip in enumerate(chips):
            copy(4 + j, (*chip, 1 - c), me).wait_recv()
        for cp in first + passed:
            cp.wait_send()
        mine.wait()

    return _pcall(
        body, name=name, out_shape=jax.ShapeDtypeStruct((8 * m_per, n), blk.dtype),
        in_specs=[pl.BlockSpec(memory_space=pltpu.VMEM)], out_specs=pl.BlockSpec(memory_space=pltpu.VMEM),
        scratch_shapes=[pltpu.SemaphoreType.DMA((7,)), pltpu.SemaphoreType.DMA((7,)), pltpu.SemaphoreType.DMA],
        compiler_params=pltpu.CompilerParams(vmem_limit_bytes=VMEM_LIMIT_BYTES),
    )(blk)


def _exchange(srcs, dsts, plan, *, name, dst_inits=None):
    na = len(dsts)
    nrem = len(plan(0, 0, 0))

    def body(*refs):
        refs = list(refs)
        src_refs = [refs.pop(0) for _ in range(na)] if srcs is not None else None
        if dst_inits is not None:
            del refs[:na]
        dst_refs, (send_sems, recv_sems) = refs[:na], refs[na:]
        start, wait = _copies(dst_refs if src_refs is None else src_refs, dst_refs, send_sems, recv_sems, plan)
        start()
        wait()

    hbm = pl.BlockSpec(memory_space=pl.ANY)
    ins = (list(srcs) if srcs is not None else []) + (list(dst_inits) if dst_inits is not None else [])
    base = na if srcs is not None else 0
    aliases = {base + a: a for a in range(na)} if dst_inits is not None else {}
    return _pcall(
        body, name=name, out_shape=list(dsts), in_specs=[hbm] * len(ins), out_specs=[hbm] * na,
        input_output_aliases=aliases,
        scratch_shapes=[pltpu.SemaphoreType.DMA((na * nrem,)), pltpu.SemaphoreType.DMA((na * nrem,))],
    )(*ins)


def _other_chips(x, y):
    return [(1 - x, y), (x, 1 - y), (1 - x, 1 - y)]


def _round(ride, carrier, name):
    if carrier is not None:
        _RIDES.setdefault(carrier, []).append(ride)
        return
    srcs = ride.srcs() if callable(ride.srcs) else ride.srcs
    inits = ride.dst_inits() if callable(ride.dst_inits) else ride.dst_inits
    ride.out = list(_exchange(srcs, ride.dsts, ride.plan, name=name, dst_inits=inits))


class _Gather:
    def __init__(self, shards, chip, tag, carriers=(None, None)):
        def plan_ici(x, y, c):
            k = 2 * x + y
            return [((c,), (k, c), (2 * px + py, c), (px, py, c)) for px, py in _other_chips(x, y)]

        def plan_d2d(x, y, c):
            return [((2 * px + py, c), (2 * px + py, c), (2 * px + py, 1 - c), (x, y, 1 - c))
                    for px, py in _other_chips(x, y)]

        def plan_near(x, y, c):
            k = 2 * x + y
            return [((c,), (k, c), (2 * px + py, c), (px, py, c)) for px, py in ((1 - x, y), (x, 1 - y))]

        def plan_far(x, y, c):
            kx, ky, kd = 2 * (1 - x) + y, 2 * x + (1 - y), 2 * (1 - x) + (1 - y)
            hp = shards[0].shape[1] // 2
            top, bottom = pl.ds(0, hp), pl.ds(hp, hp)
            return [((kx, c, top), (kx, c, top), (kd, c, top), (x, 1 - y, c)),
                    ((ky, c, bottom), (ky, c, bottom), (kd, c, bottom), (1 - x, y, c))]

        self.shards, self.chip = shards, chip
        dsts = [jax.ShapeDtypeStruct((4,) + s.shape, s.dtype) for s in shards]
        if len(carriers) == 3:
            near = _Ride(shards, dsts, plan_near)
            ici = _Ride(None, dsts, plan_far, dst_inits=lambda: near.out)
            _round(near, carriers[0], f"gather_{tag}_near")
            _round(ici, carriers[1], f"gather_{tag}_far")
        else:
            ici = _Ride(shards, dsts, plan_ici)
            _round(ici, carriers[0], f"gather_{tag}_ici")
        self.d2d = _Ride(None, dsts, plan_d2d, dst_inits=lambda: ici.out)
        _round(self.d2d, carriers[-1], f"gather_{tag}_d2d")

    def result(self):
        full = [lax.dynamic_update_index_in_dim(f, s, self.chip, 0) for f, s in zip(self.d2d.out, self.shards)]
        return [f.reshape((4, 2 * f.shape[2], f.shape[3])) for f in full]


def _index_operand(i):
    return jnp.reshape(i, (1,)).astype(jnp.int32)


def _add_pairs(g, f, ci, *, name):
    s, _, hr, wd = g.shape
    tr = _pick(hr, 600, 16)

    def body(c_ref, a_ref, b_ref, o_ref):
        o_ref[...] = (a_ref[...].astype(F32) + b_ref[...].astype(F32)).astype(BF16)

    spec = pl.BlockSpec((1, tr, wd), lambda j, i, c: (j, i, 0))
    grid_spec = pltpu.PrefetchScalarGridSpec(
        num_scalar_prefetch=1, grid=(s, hr // tr),
        in_specs=[pl.BlockSpec((1, None, tr, wd), lambda j, i, c: (j, c[0], i, 0)), spec], out_specs=spec)
    return _pcall(body, name=name, grid_spec=grid_spec, out_shape=jax.ShapeDtypeStruct(f.shape, BF16),
                  compiler_params=_params(("parallel", "parallel")))(_index_operand(ci), g, f)


def _sum_chips(landed, pairs, chip, *, name):
    s, hr, wd = landed.shape
    tr = _pick(hr, 600, 16)

    def body(k_ref, l_ref, p_ref, o_ref):
        acc = None
        for k in range(s):
            part = jnp.where(k_ref[0] == k, p_ref[k], l_ref[k]).astype(F32)
            acc = part if acc is None else acc + part
        o_ref[...] = acc

    spec = pl.BlockSpec((s, tr, wd), lambda i, k: (0, i, 0))
    grid_spec = pltpu.PrefetchScalarGridSpec(
        num_scalar_prefetch=1, grid=(hr // tr,), in_specs=[spec, spec],
        out_specs=pl.BlockSpec((tr, wd), lambda i, k: (i, 0)))
    return _pcall(body, name=name, grid_spec=grid_spec, out_shape=jax.ShapeDtypeStruct((hr, wd), F32),
                  compiler_params=_params(("parallel",)))(_index_operand(chip), landed, pairs)


class _ReduceScatter:
    def __init__(self, gs, chip, ci, tag):
        self.gs, self.chip, self.ci, self.tag = gs, chip, ci, tag
        self.half_t = [jax.ShapeDtypeStruct((g.shape[0],) + g.shape[2:], BF16) for g in gs]

    def pair(self, carrier=None):
        plan = lambda x, y, c: [((slice(None), 1 - c), (), (), (x, y, 1 - c))]
        self.r1 = _Ride(self.gs, self.half_t, plan)
        _round(self.r1, carrier, f"reduce_{self.tag}_pair")

    def chips(self, carrier=None):
        def plan(x, y, c):
            k = 2 * x + y
            return [((2 * px + py,), (k,), (2 * px + py,), (px, py, c)) for px, py in _other_chips(x, y)]

        self.pairs = [_add_pairs(g, f, self.ci, name=f"reduce_{self.tag}_pair_add{n}")
                      for n, (g, f) in enumerate(zip(self.gs, self.r1.out))]
        self.r2 = _Ride(self.pairs, self.half_t, plan)
        _round(self.r2, carrier, f"reduce_{self.tag}_chips")

    def halves(self, carrier=None):
        plan = lambda x, y, c: [((), (c,), (1 - c,), (x, y, 1 - c))]
        self.mine = [_sum_chips(l, p, self.chip, name=f"reduce_{self.tag}_chip_sum{n}")
                     for n, (l, p) in enumerate(zip(self.r2.out, self.pairs))]
        self.r3 = _Ride(self.mine, [jax.ShapeDtypeStruct((2,) + m.shape, F32) for m in self.mine], plan)
        _round(self.r3, carrier, f"reduce_{self.tag}_halves")

    def result(self):
        return [lax.dynamic_update_index_in_dim(b, m, self.ci, 0).reshape(2 * m.shape[0], m.shape[1])
                for b, m in zip(self.r3.out, self.mine)]


def _ada_fwd(c_all, w_sh, b_sh, *, name):
    nb, d = c_all.shape
    wcols = w_sh.shape[1]
    tn = _pick(wcols, 512)

    def body(c_ref, w_ref, b_ref, o_ref, a_ref):
        cv = c_ref[...]
        act = cv * jax.nn.sigmoid(cv)
        a_ref[...] = act
        o_ref[...] = jnp.dot(act.astype(BF16), w_ref[...].astype(BF16), preferred_element_type=F32) + b_ref[...]

    return _pcall(
        body, name=name, grid=(wcols // tn,),
        in_specs=[pl.BlockSpec((nb, d), lambda j: (0, 0)), pl.BlockSpec((d, tn), lambda j: (0, j)),
                  pl.BlockSpec((1, tn), lambda j: (0, j))],
        out_specs=[pl.BlockSpec((nb, tn), lambda j: (0, j)), pl.BlockSpec((nb, d), lambda j: (0, 0))],
        out_shape=[jax.ShapeDtypeStruct((nb, wcols), F32), jax.ShapeDtypeStruct((nb, d), F32)],
        compiler_params=_params(("arbitrary",)))(c_all, w_sh, b_sh)


def _sum_devices(g, *, name):
    nd, m, w = g.shape

    def body(g_ref, o_ref):
        acc = g_ref[0]
        for k in range(1, nd):
            acc = acc + g_ref[k]
        o_ref[...] = acc

    return _pcall(body, name=name, out_shape=jax.ShapeDtypeStruct((m, w), F32),
                  compiler_params=pltpu.CompilerParams(vmem_limit_bytes=VMEM_LIMIT_BYTES))(g)


def _adamw(w, g, m, v, *, name):
    rows, cols = w.shape[-2:]
    tr = _pick(rows, max(8, (1 << 18) // cols), 8)
    c1 = 1.0 / (1.0 - ADAM_B1 ** ADAM_STEP)
    c2 = 1.0 / (1.0 - ADAM_B2 ** ADAM_STEP)

    def body(w_ref, g_ref, m_ref, v_ref, d_ref, nm_ref, nv_ref):
        gv = g_ref[...]
        nm = ADAM_B1 * m_ref[...] + (1.0 - ADAM_B1) * gv
        nv = ADAM_B2 * v_ref[...] + (1.0 - ADAM_B2) * (gv * gv)
        d_ref[...] = -ADAM_LR * ((nm * c1) / (jnp.sqrt(nv * c2) + ADAM_EPS) + ADAM_WD * w_ref[...])
        nm_ref[...] = nm
        nv_ref[...] = nv

    gspec = pl.BlockSpec((tr, cols), lambda i: (i, 0))
    spec = pl.BlockSpec((None, tr, cols), lambda i: (0, i, 0)) if w.ndim == 3 else gspec
    shp = jax.ShapeDtypeStruct(w.shape, F32)
    return _pcall(body, name=name, grid=(rows // tr,), in_specs=[spec, gspec, spec, spec], out_specs=[spec] * 3,
                  out_shape=[shp] * 3, compiler_params=_params(("parallel",)))(w, g, m, v)


def _permute_in_rows(wt):
    ngrp = len(B_PATTERNS)
    qb, kb, vb = (wt[A_W + n * QB_W:A_W + (n + 1) * QB_W] for n in range(3))
    parts = [wt[:A_W], jnp.zeros((VAR_W - A_W, wt.shape[1]), wt.dtype)]
    for g in range(ngrp):
        parts += [t[g * GB_W:(g + 1) * GB_W] for t in (qb, kb, vb)]
    return jnp.concatenate(parts + [wt[A_W + 3 * QB_W:]], axis=0)


def _unpermute_in_grads(pieces):
    ga, groups, gg = pieces[0], pieces[1:-1], pieces[-1]
    rows = [ga[:A_W]]
    for n in range(3):
        rows += [gp[n * GB_W:(n + 1) * GB_W] for gp in groups]
    return jnp.concatenate(rows + [gg], axis=0)


def kernel(x, c, positions, w_ada, b_ada, w_in, sinks, w_branch_a, w_branch_b, w_o, ln1_g, ln1_b, w_gate_up, w_down, ln2_g, ln2_b, loss_target, m_w_ada, m_b_ada, m_w_in, m_sinks, m_w_branch_a, m_w_branch_b, m_w_o, m_ln1_g, m_ln1_b, m_w_gate_up, m_w_down, m_ln2_g, m_ln2_b, v_w_ada, v_b_ada, v_w_in, v_sinks, v_w_branch_a, v_w_branch_b, v_w_o, v_ln1_g, v_ln1_b, v_w_gate_up, v_w_down, v_ln2_g, v_ln2_b):
    xi, yi, ci = _coords()
    chip = 2 * xi + yi
    dev = 4 * xi + 2 * yi + ci
    NB, T, D = x.shape
    nchip, ndev = 4, 8
    ada_cols = w_ada.shape[2]

    ra, ro, rd = w_branch_a.shape[1], w_o.shape[1], w_down.shape[1]
    rowsh = jnp.concatenate([w_branch_a[0], w_o[0], w_down[0]], axis=0)
    halves = lambda a: a.reshape(a.shape[:-2] + (2, a.shape[-2] // 2, a.shape[-1]))
    tr = lambda a: jnp.swapaxes(a, -1, -2)
    shards = [halves(w.astype(BF16)) for w in (tr(w_in[0]), rowsh, w_branch_b[0], w_gate_up[0])]
    gin = _Gather(shards[:1], chip, "w_in", carriers=("gather_c", "ada_fwd", "gather_mod"))

    c_blk = jnp.zeros((8, D), F32).at[:NB].set(c)
    c_all = _allgather_small(c_blk, name="gather_c").reshape(ndev, 8, D)[:, :NB].reshape(ndev * NB, D)
    b_sh = lax.dynamic_slice(b_ada, (0, chip * ada_cols), (1, ada_cols))
    mod_part, c_act = _ada_fwd(c_all, w_ada[0], b_sh, name="ada_fwd")
    mod_g = _allgather_small(mod_part, name="gather_mod").reshape(nchip, 2, ndev * NB, ada_cols)[:, 0]
    mod_all = jnp.transpose(mod_g, (1, 0, 2)).reshape(ndev * NB, nchip * ada_cols)
    mod = lax.dynamic_slice(mod_all, (NB * dev, 0), (NB, nchip * ada_cols))

    (g_in,) = gin.result()
    w_in_f = _permute_in_rows(g_in.reshape(nchip * g_in.shape[1], D))
    mix = _Gather(shards[1:3], chip, "w_mix", carriers=("inproj_qkv", "attn_a_fwd"))
    ffn = _Gather(shards[3:], chip, "w_ffn", carriers=("attn_a_fwd", "attn_b_fwd", "mix_out"))

    def rest_weights():
        g_rows, w_b_f = mix.result()
        return (g_rows[:, :ra].reshape(nchip * ra, D), w_b_f, g_rows[:, ra:ra + ro].reshape(nchip * ro, D),
                lambda: ffn.result()[0], g_rows[:, ra + ro:].reshape(nchip * rd, D))

    red = {}

    def hook(event, **g):
        if event == "rest_grads":
            gr_rows = jnp.concatenate([g["g_wa"].reshape(nchip, ra, D), g["g_wo"].reshape(nchip, ro, D),
                                       g["g_wd"].reshape(nchip, rd, D)], axis=1)
            red["ffn"] = _ReduceScatter([halves(g["g_wgu"])], chip, ci, "ffn")
            red["mix"] = _ReduceScatter([halves(gr_rows), halves(g["g_wb"])], chip, ci, "mix")
            red["ffn"].pair(carrier="merge_bwd")
            red["mix"].pair(carrier="merge_bwd")
        elif event == "merge_bwd_done":
            red["ffn"].chips(carrier="attn_a_bwd")
            red["mix"].chips(carrier="attn_b0_bwd")
        elif event == "attn_a_bwd_done":
            red["ffn"].halves(carrier="attn_b0_bwd")
        elif event == "attn_b0_bwd_done":
            red["mix"].halves(carrier="attn_b1_bwd")
        elif event == "win_grads":
            gr_in = _unpermute_in_grads(g["g_win"])
            red["w_in"] = _ReduceScatter([halves(gr_in.reshape(nchip, gr_in.shape[0] // nchip, D))], chip, ci, "w_in")
            red["w_in"].pair(carrier="inproj_dx0")
        elif event == "inproj_dx0_done":
            red["w_in"].chips(carrier="inproj_dx1")
        elif event == "inproj_dx1_done":
            red["w_in"].halves(carrier="gather_small")

    res = _local_step(x, mod, positions, w_in_f, rest_weights, sinks[0], ln1_g, ln1_b, ln2_g, ln2_b, loss_target, hook)
    (g_rows_red, g_w_b), (g_w_gu,) = red["mix"].result(), red["ffn"].result()
    g_w_a, g_w_o, g_w_d = g_rows_red[:ra], g_rows_red[ra:ra + ro], g_rows_red[ra + ro:]

    small_rows = 24
    misc = jnp.zeros((1, D), F32).at[0, :A_Q_HEADS].set(res["dsink"]).at[0, A_Q_HEADS].set(jnp.sum(res["loss"]))
    small = jnp.concatenate([res["dmod"].reshape(NB * 6, D), jnp.sum(res["ln_grads"], axis=0), misc,
                             jnp.zeros((small_rows - NB * 6 - 5, D), F32)], axis=0)
    small_all = _allgather_small(small, name="gather_small").reshape(ndev, small_rows, D)
    (g_w_in,) = red["w_in"].result()
    dmod_all = small_all[:, :NB * 6].reshape(ndev * NB, 6 * D)
    sums = _sum_devices(small_all, name="sum_small")
    g_b_ada = (sums[0:6] + sums[6:12]).reshape(1, 6 * D)
    g_ln1_g, g_ln1_b, g_ln2_g, g_ln2_b = (sums[12 + n][None] for n in range(4))
    g_sinks = sums[16, :A_Q_HEADS][None]
    loss = sums[16, A_Q_HEADS]
    dmod_sh = lax.dynamic_slice(dmod_all, (0, chip * ada_cols), (ndev * NB, ada_cols))
    g_w_ada = _mm(c_act, dmod_sh, ta=True, name="ada_dw")

    names = ["w_ada", "b_ada", "w_in", "sinks", "w_branch_a", "w_branch_b", "w_o", "ln1_g", "ln1_b",
             "w_gate_up", "w_down", "ln2_g", "ln2_b"]
    ws = [w_ada, b_ada, w_in, sinks, w_branch_a, w_branch_b, w_o, ln1_g, ln1_b, w_gate_up, w_down, ln2_g, ln2_b]
    ms = [m_w_ada, m_b_ada, m_w_in, m_sinks, m_w_branch_a, m_w_branch_b, m_w_o, m_ln1_g, m_ln1_b, m_w_gate_up,
          m_w_down, m_ln2_g, m_ln2_b]
    vs = [v_w_ada, v_b_ada, v_w_in, v_sinks, v_w_branch_a, v_w_branch_b, v_w_o, v_ln1_g, v_ln1_b, v_w_gate_up,
          v_w_down, v_ln2_g, v_ln2_b]
    gs = [g_w_ada, g_b_ada, g_w_in, g_sinks, g_w_a, g_w_b, g_w_o, g_ln1_g, g_ln1_b, g_w_gu, g_w_d, g_ln2_g, g_ln2_b]
    grads, deltas, new_ms, new_vs = [], [], [], []
    for name, w, g, m, v in zip(names, ws, gs, ms, vs):
        flip = tr if name == "w_in" else (lambda a: a)
        w, m, v = flip(w), flip(m), flip(v)
        g2 = g.reshape(w.shape[-2:])
        d, nm, nv = _adamw(w, g2, m, v, name="adamw_" + name)
        grads.append(flip(g2.reshape(w.shape)))
        deltas.append(flip(d))
        new_ms.append(flip(nm))
        new_vs.append(flip(nv))
    return (loss, res["grad_x"], *grads, *deltas, *new_ms, *new_vs)
```

```python
import functools

import jax
import jax.numpy as jnp
from jax import lax
from jax.experimental import pallas as pl
from jax.experimental.pallas import tpu as pltpu

F32 = jnp.float32
BF16 = jnp.bfloat16
MESH = pl.DeviceIdType.MESH

HEAD_DIM = 64
LANES = 128
PAIR_W = 2 * HEAD_DIM
BLOCK = 128
A_Q_HEADS = 16
A_KV_HEADS = 2
A_WINDOW = 128
B_PATTERNS = ((128, 1), (512, 4), (2048, 16))
B_GROUP_HEADS = 8
QA_W = A_Q_HEADS * HEAD_DIM
KA_W = A_KV_HEADS * HEAD_DIM
GB_W = B_GROUP_HEADS * HEAD_DIM
QB_W = GB_W * len(B_PATTERNS)
A_W = QA_W + 2 * KA_W
VAR_W = 3 * GB_W
N_VAR = 1 + len(B_PATTERNS)
VAR_DIL = (1,) + tuple(r for _, r in B_PATTERNS)
A_BWD_SPLIT = 4
QKV_P = N_VAR * VAR_W
ROPE_THETA = 10000.0
LN_EPS = 1e-5
NEG_INF = -1e30
DEPTH = 1
ALPHA = (2 * DEPTH) ** 0.25
SCALE = HEAD_DIM ** -0.5

ADAM_LR, ADAM_B1, ADAM_B2, ADAM_EPS, ADAM_WD, ADAM_STEP = 0.001, 0.9, 0.999, 1e-08, 0.01, 10

VMEM_LIMIT_BYTES = 56 * 1024 * 1024
MM_TILE_BYTES = 36 * 1024 * 1024
MM_WHOLE_K = 4096


def _params(sem=None):
    return pltpu.CompilerParams(dimension_semantics=sem, vmem_limit_bytes=VMEM_LIMIT_BYTES)


_RIDES = {}


def _pcall(body, *, name, **kw):
    rides = _RIDES.pop(name, None)
    if rides is None:
        return pl.pallas_call(body, name=name, **kw)
    return _riding_call(body, rides, name=name, **kw)


def _copies(src_refs, dst_refs, send_sems, recv_sems, plan):
    x, y, c = lax.axis_index("x"), lax.axis_index("y"), lax.axis_index("c")
    remote = plan(x, y, c)
    nrem = len(remote)
    at = lambda ref, idx: ref.at[idx] if idx else ref

    def copy(a, n, landing):
        si, di, ri, peer = remote[n]
        return pltpu.make_async_remote_copy(
            src_ref=at(src_refs[a], si), dst_ref=at(dst_refs[a], ri if landing else di),
            send_sem=send_sems.at[a * nrem + n], recv_sem=recv_sems.at[a * nrem + n],
            device_id=peer, device_id_type=MESH)

    order = [(a, n) for a in range(len(dst_refs)) for n in range(nrem)]

    def start():
        for a, n in order:
            copy(a, n, False).start()

    def wait():
        for a, n in order:
            copy(a, n, True).wait_recv()
        for a, n in order:
            copy(a, n, False).wait_send()

    return start, wait


class _Ride:
    def __init__(self, srcs, dsts, plan, dst_inits=None):
        self.srcs, self.dsts, self.plan, self.dst_inits, self.out = srcs, dsts, plan, dst_inits, None


def _riding_call(body, rides, *, name, in_specs, out_specs, out_shape, grid=(), scratch_shapes=(), **kw):
    single = not isinstance(out_specs, (list, tuple))
    out_specs = [out_specs] if single else list(out_specs)
    out_shape = [out_shape] if single else list(out_shape)
    n_in, n_out, n_scr = len(in_specs), len(out_specs), len(scratch_shapes)
    xin, xdsts, sems, aliases, layout = [], [], [], {}, []
    for ride in rides:
        srcs = ride.srcs() if callable(ride.srcs) else ride.srcs
        inits = ride.dst_inits() if callable(ride.dst_inits) else ride.dst_inits
        na, nrem = len(ride.dsts), len(ride.plan(0, 0, 0))
        src_at = len(xin) if srcs is not None else None
        xin += list(srcs) if srcs is not None else []
        if inits is not None:
            aliases.update({n_in + len(xin) + a: n_out + len(xdsts) + a for a in range(na)})
            xin += list(inits)
        layout.append((src_at, len(xdsts), na))
        xdsts += list(ride.dsts)
        sems += [pltpu.SemaphoreType.DMA((na * nrem,)), pltpu.SemaphoreType.DMA((na * nrem,))]

    def wrapped(*refs):
        ins, xins = refs[:n_in], refs[n_in:n_in + len(xin)]
        outs = refs[n_in + len(xin):n_in + len(xin) + n_out]
        xouts = refs[n_in + len(xin) + n_out:n_in + len(xin) + n_out + len(xdsts)]
        scr = refs[n_in + len(xin) + n_out + len(xdsts):]
        rounds = []
        for k, (ride, (src_at, dst_at, na)) in enumerate(zip(rides, layout)):
            dsts = xouts[dst_at:dst_at + na]
            srcs = dsts if src_at is None else xins[src_at:src_at + na]
            rounds.append(_copies(srcs, dsts, scr[n_scr + 2 * k], scr[n_scr + 2 * k + 1], ride.plan))
        ids = [pl.program_id(a) for a in range(len(grid))]
        first = functools.reduce(jnp.logical_and, [i == 0 for i in ids], True)
        last = functools.reduce(jnp.logical_and, [i == g - 1 for i, g in zip(ids, grid)], True)

        def start_all():
            for start, _ in rounds:
                start()

        def wait_all():
            for _, wait in rounds:
                wait()

        start_all() if not grid else pl.when(first)(start_all)
        body(*ins, *outs, *scr[:n_scr])
        wait_all() if not grid else pl.when(last)(wait_all)

    hbm = pl.BlockSpec(memory_space=pl.ANY)
    gridkw = dict(grid=grid) if grid else {}

    def run(*args):
        res = pl.pallas_call(
            wrapped, name=name, in_specs=list(in_specs) + [hbm] * len(xin),
            out_specs=out_specs + [hbm] * len(xdsts), out_shape=out_shape + xdsts,
            scratch_shapes=list(scratch_shapes) + sems, input_output_aliases=aliases,
            compiler_params=_params(("arbitrary",) * len(grid) if grid else None), **gridkw,
        )(*args, *xin)
        for ride, (_, dst_at, na) in zip(rides, layout):
            ride.out = list(res[n_out + dst_at:n_out + dst_at + na])
        return res[0] if single else list(res[:n_out])

    return run


def _pick(n, target, quantum=128):
    t = (min(target, n) // quantum) * quantum
    while t >= quantum:
        if n % t == 0:
            return t
        t -= quantum
    return n


def _mm(a, b, *, name, ta=False, tb=False, b3=False, out3=0, out_dtype=F32, add=None, tm=1024, tn=1536, tk=1536):
    if ta:
        K, M = a.shape
    else:
        M, K = a.shape
    if b3 and tb:
        Nn, K2, tk = b.shape[1], b.shape[0] * b.shape[2], b.shape[2]
    elif b3:
        K2, Nn, tn = b.shape[1], b.shape[0] * b.shape[2], b.shape[2]
    elif tb:
        Nn, K2 = b.shape
    else:
        K2, Nn = b.shape
    assert K == K2, (a.shape, b.shape)
    if out3:
        tn = Nn // out3
    tm, tn, tk = _pick(M, tm), _pick(Nn, tn), _pick(K, tk)
    if not (b3 and tb) and K <= MM_WHOLE_K:
        tk = K
        fits = lambda: 4 * tk * (tm + tn) + 8 * tm * tn * (2 if add is not None else 1) <= MM_TILE_BYTES
        while not fits():
            if (tm >= tn or b3 or out3) and tm > 256:
                tm = _pick(M, tm - 128)
            elif not (b3 or out3) and tn > 256:
                tn = _pick(Nn, tn - 128)
            else:
                break
    nk = K // tk
    j_outer = K * Nn + (Nn // tn) * M * K < M * K + (M // tm) * K * Nn
    dn = (((0 if ta else 1,), (1 if tb else 0,)), ((), ()))

    def body(*refs):
        refs = list(refs)
        a_ref, b_ref = refs[:2]
        add_ref = refs[2] if add is not None else None
        o_ref = refs[3] if add is not None else refs[2]
        part = lax.dot_general(a_ref[...].astype(BF16), b_ref[...].astype(BF16), dn, preferred_element_type=F32)

        def finish(r):
            if add is not None:
                r = r + add_ref[...]
            o_ref[...] = r.astype(out_dtype)

        if nk == 1:
            finish(part)
            return
        acc = refs[-1]
        k = pl.program_id(2)

        @pl.when(k == 0)
        def _():
            acc[...] = part

        @pl.when(k > 0)
        def _():
            acc[...] += part

        @pl.when(k == nk - 1)
        def _():
            finish(acc[...])

    def spec(shape, index):
        return pl.BlockSpec(shape, (lambda j, i, k: index(i, j, k)) if j_outer else index)

    a_spec = spec((tk, tm), lambda i, j, k: (k, i)) if ta else spec((tm, tk), lambda i, j, k: (i, k))
    if b3 and tb:
        b_spec = spec((None, tn, tk), lambda i, j, k: (k, j, 0))
    elif b3:
        b_spec = spec((None, tk, tn), lambda i, j, k: (j, k, 0))
    elif tb:
        b_spec = spec((tn, tk), lambda i, j, k: (j, k))
    else:
        b_spec = spec((tk, tn), lambda i, j, k: (k, j))
    if out3:
        o_spec = spec((None, tm, tn), lambda i, j, k: (j, i, 0))
    else:
        o_spec = spec((tm, tn), lambda i, j, k: (i, j))
    ins, specs = [a, b], [a_spec, b_spec]
    if add is not None:
        ins.append(add)
        specs.append(o_spec)
    grid = (Nn // tn, M // tm, nk) if j_outer else (M // tm, Nn // tn, nk)
    return _pcall(
        body, name=name, grid=grid, in_specs=specs, out_specs=o_spec,
        out_shape=jax.ShapeDtypeStruct((out3, M, tn) if out3 else (M, Nn), out_dtype),
        scratch_shapes=[pltpu.VMEM((tm, tn), F32)] if nk > 1 else [],
        compiler_params=_params(("parallel", "parallel", "arbitrary")),
    )(*ins)


def _mm_multi(a_list, b_list, *, name, M, T=None, add=None, out_dtype=F32, tm=512, post=None):
    tm = _pick(T or M, tm)
    ns = len(a_list)
    dils = [a[1] if isinstance(a, tuple) else 0 for a in a_list]
    a_arrs = [a[0] if isinstance(a, tuple) else a for a in a_list]
    widths = [a.shape[-1] // max(r, 1) for a, r in zip(a_arrs, dils)]
    b_arrs, b_specs = [], []
    for b in b_list:
        arr, shp, idx = b if isinstance(b, tuple) else (b, b.shape, (0, 0))
        b_arrs.append(arr)
        b_specs.append(pl.BlockSpec(shp, lambda i, idx=idx: idx))
    Nn = b_specs[0].block_shape[1]
    dn = (((1,), (0,)), ((), ()))
    nmm = 2 * ns + (1 if add is not None else 0)
    p_arrs, p_in_specs, p_out_specs, p_out_shape, p_fn = post or ([], [], None, None, None)
    nin = nmm + len(p_arrs)
    nout = len(p_out_specs) if post else 1

    def body(*refs):
        a_refs, b_refs, scr = refs[:ns], refs[ns:2 * ns], list(refs[nin + nout:])
        acc = None
        for a_ref, b_ref, r in zip(a_refs, b_refs, dils):
            av = _from_view(a_ref, scr.pop(0), r) if r > 1 else a_ref[...]
            part = lax.dot_general(av.astype(BF16), b_ref[...], dn, preferred_element_type=F32)
            acc = part if acc is None else acc + part
        if add is not None:
            acc = acc + refs[2 * ns][...]
        if post:
            p_fn(acc, refs[nmm:nin], refs[nin:nin + nout])
        else:
            refs[nin][...] = acc.astype(out_dtype)

    tpe = (T or M) // tm
    a_specs = [pl.BlockSpec((None, tm // r, r * w), lambda i: (i // tpe, i % tpe, 0)) if r
               else pl.BlockSpec((tm, w), lambda i: (i, 0)) for r, w in zip(dils, widths)]
    o_spec = pl.BlockSpec((tm, Nn), lambda i: (i, 0))
    specs = a_specs + b_specs
    ins = a_arrs + b_arrs
    if add is not None:
        specs.append(o_spec)
        ins.append(add)
    scratch = [pltpu.VMEM((w // LANES, tm, LANES), F32) for r, w in zip(dils, widths) if r > 1]
    return _pcall(body, name=name, grid=(M // tm,), in_specs=specs + list(p_in_specs),
                  out_specs=list(p_out_specs) if post else o_spec, scratch_shapes=scratch,
                  out_shape=list(p_out_shape) if post else jax.ShapeDtypeStruct((M, Nn), out_dtype),
                  compiler_params=_params(("arbitrary",) if post else ("parallel",)))(*ins, *p_arrs)


def _dw_view(d3, u, r, *, name, tk=1024):
    NB, tsub, rw = d3.shape
    W, T, D = rw // r, tsub * r, u.shape[1]
    tk = _pick(T, tk)
    tpe, nk = T // tk, NB * T // tk

    def body(d_ref, u_ref, o_ref, acc, scr):
        k = pl.program_id(0)
        dv = _from_view(d_ref, scr, r).astype(BF16)
        part = lax.dot_general(dv, u_ref[...], _TN, preferred_element_type=F32)

        @pl.when(k == 0)
        def _():
            acc[...] = part

        @pl.when(k > 0)
        def _():
            acc[...] += part

        @pl.when(k == nk - 1)
        def _():
            o_ref[...] = acc[...].astype(o_ref.dtype)

    return _pcall(
        body, name=name, grid=(nk,),
        in_specs=[pl.BlockSpec((None, tk // r, rw), lambda k: (k // tpe, k % tpe, 0)), pl.BlockSpec((tk, D), lambda k: (k, 0))],
        out_specs=pl.BlockSpec((W, D), lambda k: (0, 0)), out_shape=jax.ShapeDtypeStruct((W, D), BF16),
        scratch_shapes=[pltpu.VMEM((W, D), F32), pltpu.VMEM((W // LANES, tk, LANES), F32)],
        compiler_params=_params(("arbitrary",)))(d3, u)


def _lane(shape):
    return lax.broadcasted_iota(jnp.int32, shape, len(shape) - 1)


def _rot_half(v):
    w = v.shape[-1]
    first = (_lane(v.shape) % HEAD_DIM) < (HEAD_DIM // 2)
    return jnp.where(first, pltpu.roll(v, w - HEAD_DIM // 2, v.ndim - 1), pltpu.roll(v, HEAD_DIM // 2, v.ndim - 1))


def _widen(t, w):
    return t if w == t.shape[-1] else jnp.concatenate([t] * (w // t.shape[-1]), axis=-1)


def _unrope(v, cos, sins):
    w = v.shape[-1]
    return v * _widen(cos, w) - _rot_half(v) * _widen(sins, w)


def _rope_tables(positions):
    half = HEAD_DIM // 2
    inv = ROPE_THETA ** (-jnp.arange(half, dtype=F32) / half)
    ang = positions.astype(F32)[..., None] * inv
    cos, sin = jnp.cos(ang), jnp.sin(ang)
    cosf = jnp.concatenate([cos, cos, cos, cos], axis=-1)
    sins = jnp.concatenate([-sin, sin, -sin, sin], axis=-1)
    n = positions.shape[0] * positions.shape[1]
    return cosf.reshape(n, PAIR_W), sins.reshape(n, PAIR_W)


def _inproj(x2, scale, shift, w, cosf, sins, flags, *, T, name):
    N, D = x2.shape
    tm, tn = _pick(T, 512), VAR_W
    tpe = T // tm

    def body(x_ref, sc_ref, sh_ref, w_ref, c_ref, s_ref, f_ref, *outs):
        o_refs, u_ref = outs[:N_VAR], outs[N_VAR]
        j = pl.program_id(1)

        @pl.when(j == 0)
        def _():
            u_ref[...] = (x_ref[...] * (1.0 + sc_ref[0]) + sh_ref[0]).astype(BF16)

        acc = lax.dot_general(u_ref[...], w_ref[...], (((1,), (1,)), ((), ())), preferred_element_type=F32)
        fl = f_ref[...]
        ce = 1.0 + (_widen(c_ref[...], tn) - 1.0) * fl
        se = _widen(s_ref[...], tn) * fl
        res = acc * ce + _rot_half(acc) * se
        for v in range(N_VAR):
            @pl.when(j == v)
            def _(v=v):
                _to_view(res, o_refs[v], outs[N_VAR + 1], VAR_DIL[v])

    ex = pl.BlockSpec((1, 1, D), lambda i, j: (i // tpe, 0, 0))
    tab = pl.BlockSpec((tm, PAIR_W), lambda i, j: (i, 0))
    keep = lambda w_: pl.BlockSpec((tm, w_), lambda i, j: (i, 0))
    vspec = lambda r: pl.BlockSpec((None, tm // r, r * tn), lambda i, j: (i // tpe, i % tpe, 0))
    vshape = lambda r: jax.ShapeDtypeStruct((N // T, T // r, r * tn), BF16)
    return _pcall(
        body, name=name, grid=(N // tm, N_VAR),
        in_specs=[keep(D), ex, ex, pl.BlockSpec((tn, D), lambda i, j: (j, 0)), tab, tab,
                  pl.BlockSpec((1, tn), lambda i, j: (0, j))],
        out_specs=[vspec(r) for r in VAR_DIL] + [keep(D)],
        out_shape=[vshape(r) for r in VAR_DIL] + [jax.ShapeDtypeStruct((N, D), BF16)],
        scratch_shapes=[pltpu.VMEM((tn // LANES, tm, LANES), F32)],
        compiler_params=_params(("parallel", "arbitrary")),
    )(x2, scale, shift, w, cosf, sins, flags)


class _Geom:
    def __init__(self, g):
        if g is None:
            self.r, self.nq, self.n_back, self.sink = 1, A_Q_HEADS, A_WINDOW - 1, True
            self.qw, self.kw = QA_W, KA_W
            self.qidx = lambda j: 0
            self.kidx = lambda j: QA_W // KA_W
            self.vidx = lambda j: QA_W // KA_W + 1
        else:
            window, r = B_PATTERNS[g]
            self.r, self.nq, self.n_back, self.sink = r, B_GROUP_HEADS, window // r, False
            self.qw, self.kw = GB_W, GB_W
            self.qidx = lambda j: 3 * j
            self.kidx = lambda j: 3 * j + 1
            self.vidx = lambda j: 3 * j + 2
        self.ntile = self.qw // PAIR_W


def _stack_heads(t, scale=None):
    first = _lane(t.shape) < HEAD_DIM
    z = jnp.zeros_like(t)
    if scale is not None:
        t = t * jnp.asarray(scale, t.dtype)
    return jnp.concatenate([jnp.where(first, t, z), jnp.where(first, z, t)], axis=0)


def _unstack_heads(v2):
    return jnp.where(_lane((BLOCK, PAIR_W)) < HEAD_DIM, v2[:BLOCK], v2[BLOCK:])


def _dup_head(t, kh):
    tf = t.astype(F32)
    keep = (_lane(t.shape) < HEAD_DIM) if kh == 0 else (_lane(t.shape) >= HEAD_DIM)
    return jnp.where(keep, tf, pltpu.roll(tf, HEAD_DIM, 1)).astype(t.dtype)


def _fold_heads(t):
    return t + pltpu.roll(t, HEAD_DIM, 1)


def _band_mask(rows, i, n_back, single):
    nkeys = BLOCK if single else 2 * BLOCK
    qi = jnp.bitwise_and(lax.broadcasted_iota(jnp.int32, (rows, nkeys), 0), BLOCK - 1)
    ki = lax.broadcasted_iota(jnp.int32, (rows, nkeys), 1)
    if single:
        return qi >= ki
    dist = qi + BLOCK - ki
    return jnp.logical_and(jnp.logical_and(dist >= 0, dist <= n_back), jnp.logical_or(ki >= BLOCK, i > 0))


def _sink_slot(rows):
    qi = jnp.bitwise_and(lax.broadcasted_iota(jnp.int32, (rows, 2 * BLOCK), 0), BLOCK - 1)
    return qi == lax.broadcasted_iota(jnp.int32, (rows, 2 * BLOCK), 1)


def _sink_scores(rows, sinks):
    blk = lax.broadcasted_iota(jnp.int32, (rows, 2 * BLOCK), 0) // BLOCK
    out = jnp.full((rows, 2 * BLOCK), sinks[-1], F32)
    for b in range(len(sinks) - 2, -1, -1):
        out = jnp.where(blk == b, sinks[b], out)
    return out


def _softmax_parts(s, valid, sinks):
    s = jnp.where(valid, s, NEG_INF)
    if sinks is not None:
        slot = _sink_slot(s.shape[0])
        s = jnp.where(slot, _sink_scores(s.shape[0], sinks), s)
    m = jnp.max(s, axis=1, keepdims=True)
    p = jnp.exp(s - m)
    den = jnp.sum(p, axis=1, keepdims=True)
    if sinks is not None:
        p = jnp.where(slot, 0.0, p)
    return p, m, den


_NT = (((1,), (1,)), ((), ()))
_TN = (((0,), (0,)), ((), ()))


def _rows2(prev_ref, cur_ref, cs, single=False):
    if single:
        return cur_ref[0, :, cs]
    return jnp.concatenate([prev_ref[0, :, cs], cur_ref[0, :, cs]], axis=0)


def _sink_scalars(sink_ref, first, nblocks):
    return [sink_ref[first + b] for b in range(nblocks)]


def _tile(t):
    return slice(t * PAIR_W, (t + 1) * PAIR_W)


def _attn_fwd(qkv, sinks, g, *, NB, T, name):
    geo = _Geom(g)
    r, qw, kw, ntile = geo.r, geo.qw, geo.kw, geo.ntile
    tsub = T // r
    nblk = tsub // BLOCK
    qkv3 = qkv.reshape(NB, tsub, r * VAR_W)
    out_dtype = BF16 if g is None else F32
    tiles_per_kv = ntile // A_KV_HEADS

    single = nblk == 1

    def body(q_ref, kp_ref, kc_ref, vp_ref, vc_ref, sink_ref, o_ref, l_ref):
        i = pl.program_id(2)
        if geo.sink:
            kall, vall = _rows2(kp_ref, kc_ref, _tile(0)), _rows2(vp_ref, vc_ref, _tile(0))
            kdup = [_dup_head(kall, kh) for kh in range(A_KV_HEADS)]
            vdup = [_dup_head(vall, kh) for kh in range(A_KV_HEADS)]
            tiles = [[t] for t in range(ntile)]
            q2s = [_stack_heads(q_ref[0, :, _tile(t)], SCALE) for t in range(ntile)]
            kks = [kdup[t // tiles_per_kv] for t in range(ntile)]
            vvs = [vdup[t // tiles_per_kv] for t in range(ntile)]
            sinkcols = [_sink_scalars(sink_ref, 2 * t, 2) for t in range(ntile)]
        else:
            tiles = [[t] for t in range(ntile)]
            q2s = [_stack_heads(q_ref[0, :, _tile(t)], SCALE) for t in range(ntile)]
            kks = [_rows2(kp_ref, kc_ref, _tile(t), single) for t in range(ntile)]
            vvs = [_rows2(vp_ref, vc_ref, _tile(t), single) for t in range(ntile)]
            sinkcols = [None] * ntile
        valid = _band_mask(q2s[0].shape[0], i, geo.n_back, single)
        ss = [lax.dot_general(q2, kk, _NT, preferred_element_type=F32) for q2, kk in zip(q2s, kks)]
        parts = [_softmax_parts(s, valid, sc) for s, sc in zip(ss, sinkcols)]
        o2s = [jnp.dot(p.astype(BF16), vv, preferred_element_type=F32) / den for (p, m, den), vv in zip(parts, vvs)]
        for ts, o2, (p, m, den) in zip(tiles, o2s, parts):
            lse2 = jnp.broadcast_to(m + jnp.log(den), (o2.shape[0], PAIR_W))
            for n, t in enumerate(ts):
                rows = slice(2 * BLOCK * n, 2 * BLOCK * (n + 1))
                o_ref[0, :, _tile(t)] = _unstack_heads(o2[rows]).astype(out_dtype)
                l_ref[0, :, _tile(t)] = _unstack_heads(lse2[rows])

    prev = lambda i: jnp.maximum(i - 1, 0)
    in_specs = [
        pl.BlockSpec((1, BLOCK, qw), lambda b, j, i: (b, i, geo.qidx(j))),
        pl.BlockSpec((1, BLOCK, kw), lambda b, j, i: (b, prev(i), geo.kidx(j))),
        pl.BlockSpec((1, BLOCK, kw), lambda b, j, i: (b, i, geo.kidx(j))),
        pl.BlockSpec((1, BLOCK, kw), lambda b, j, i: (b, prev(i), geo.vidx(j))),
        pl.BlockSpec((1, BLOCK, kw), lambda b, j, i: (b, i, geo.vidx(j))),
        pl.BlockSpec(memory_space=pltpu.SMEM),
    ]
    o_spec = pl.BlockSpec((1, BLOCK, qw), lambda b, j, i: (b, i, j))
    shape = (NB, tsub, r * qw)
    o, lse = _pcall(
        body, name=name, grid=(NB, r, nblk), in_specs=in_specs, out_specs=[o_spec, o_spec],
        out_shape=[jax.ShapeDtypeStruct(shape, out_dtype), jax.ShapeDtypeStruct(shape, F32)],
        compiler_params=_params(("parallel", "parallel", "arbitrary")),
    )(qkv3, qkv3, qkv3, qkv3, qkv3, sinks)
    return o, lse


def _attn_fwd_b(qkvs, *, NB, T, name):
    geos = [_Geom(g) for g in range(len(B_PATTERNS))]
    steps = T // BLOCK
    nt = GB_W // PAIR_W
    ng = len(geos)

    def where(geo, s):
        nblk = T // geo.r // BLOCK
        return s // steps, (s % steps) // nblk, (s % steps) % nblk

    def body(*refs):
        ins, outs = refs[:5 * ng], refs[5 * ng:]
        s = pl.program_id(0)
        q2s, kks, vvs, valids = [], [], [], []
        for n, geo in enumerate(geos):
            q_ref, kp_ref, kc_ref, vp_ref, vc_ref = ins[5 * n:5 * n + 5]
            single = T // geo.r // BLOCK == 1
            valid = _band_mask(2 * BLOCK, where(geo, s)[2], geo.n_back, single)
            for t in range(nt):
                q2s.append(_stack_heads(q_ref[0, :, _tile(t)], SCALE))
                kks.append(_rows2(kp_ref, kc_ref, _tile(t), single))
                vvs.append(_rows2(vp_ref, vc_ref, _tile(t), single))
                valids.append(valid)
        ss = [lax.dot_general(q2, kk, _NT, preferred_element_type=F32) for q2, kk in zip(q2s, kks)]
        parts = [_softmax_parts(sc, valid, None) for sc, valid in zip(ss, valids)]
        o2s = [jnp.dot(p.astype(BF16), vv, preferred_element_type=F32) / den for (p, m, den), vv in zip(parts, vvs)]
        for n in range(ng):
            o_ref, l_ref = outs[2 * n], outs[2 * n + 1]
            for t in range(nt):
                o2, (p, m, den) = o2s[n * nt + t], parts[n * nt + t]
                o_ref[0, :, _tile(t)] = _unstack_heads(o2)
                l_ref[0, :, _tile(t)] = _unstack_heads(jnp.broadcast_to(m + jnp.log(den), (2 * BLOCK, PAIR_W)))

    in_specs, ins, out_specs, out_shape = [], [], [], []
    for geo, qkv in zip(geos, qkvs):
        tsub = T // geo.r
        pos = lambda s, geo=geo: where(geo, s)
        prev = lambda i: jnp.maximum(i - 1, 0)
        blk = lambda col, back, pos=pos: pl.BlockSpec(
            (1, BLOCK, GB_W), lambda s: (pos(s)[0], prev(pos(s)[2]) if back else pos(s)[2], col(pos(s)[1])))
        in_specs += [blk(geo.qidx, False), blk(geo.kidx, True), blk(geo.kidx, False), blk(geo.vidx, True),
                     blk(geo.vidx, False)]
        ins += [qkv.reshape(NB, tsub, geo.r * VAR_W)] * 5
        out_specs += [blk(lambda j: j, False)] * 2
        out_shape += [jax.ShapeDtypeStruct((NB, tsub, geo.r * GB_W), F32)] * 2
    res = _pcall(body, name=name, grid=(NB * steps,), in_specs=in_specs, out_specs=out_specs, out_shape=out_shape,
                 compiler_params=_params(("arbitrary",)))(*ins)
    return [(res[2 * n], res[2 * n + 1]) for n in range(ng)]


def _attn_bwd(qkv, do, lse, dlse, cosf, sins, sinks, g, *, NB, T, name):
    geo = _Geom(g)
    r, qw, kw, ntile = geo.r, geo.qw, geo.kw, geo.ntile
    tsub = T // r
    nblk = tsub // BLOCK
    view = lambda a, w: a.reshape(NB, tsub, r * w)
    has_dlse = dlse is not None
    tiles_per_kv = ntile // A_KV_HEADS

    single = nblk == 1
    krows = BLOCK if single else 2 * BLOCK
    nsteps = 1 if single else nblk + 1

    def grads(q2s, kks, vvs, do2s, i, lserows, sinkcols, dlrows):
        nrow = q2s[0].shape[0]
        ki = lax.broadcasted_iota(jnp.int32, (krows, nrow), 0)
        qi = jnp.bitwise_and(lax.broadcasted_iota(jnp.int32, (krows, nrow), 1), BLOCK - 1)
        if single:
            valid = qi >= ki
        else:
            dist = qi + BLOCK - ki
            valid = jnp.logical_and(jnp.logical_and(dist >= 0, dist <= geo.n_back), jnp.logical_or(ki >= BLOCK, i > 0))
        sts = [lax.dot_general(kk, q2, _NT, preferred_element_type=F32) for q2, kk in zip(q2s, kks)]
        dpts = [lax.dot_general(vv, do2, _NT, preferred_element_type=F32) for do2, vv in zip(do2s, vvs)]
        pts, dsts, sks = [], [], []
        for st, dpt, ls, sc, dl in zip(sts, dpts, lserows, sinkcols, dlrows):
            sv = jnp.where(valid, st, NEG_INF)
            if sc is not None:
                slot = ki == qi
                blk = lax.broadcasted_iota(jnp.int32, (krows, nrow), 1) // BLOCK
                sink = jnp.full((krows, nrow), sc[-1], F32)
                for b in range(len(sc) - 2, -1, -1):
                    sink = jnp.where(blk == b, sc[b], sink)
                sv = jnp.where(slot, sink, sv)
                dpt = jnp.where(slot, 0.0, dpt)
            pt = jnp.exp(sv - ls)
            delta = jnp.sum(pt * dpt, axis=0, keepdims=True)
            if dl is not None:
                delta = delta - dl
            dst = pt * (dpt - delta)
            if sc is not None:
                cols = lambda a, b: a[:, b * BLOCK:(b + 1) * BLOCK]
                sks.append([jnp.sum(jnp.where(cols(slot, b), cols(dst, b), 0.0)) for b in range(len(sc))])
                dst, pt = jnp.where(slot, 0.0, dst), jnp.where(slot, 0.0, pt)
            else:
                sks.append(None)
            pts.append(pt.astype(BF16))
            dsts.append(dst.astype(BF16))
        dq2s = [lax.dot_general(dst, kk, _TN, preferred_element_type=F32) * SCALE for dst, kk in zip(dsts, kks)]
        dkks = [jnp.dot(dst, q2, preferred_element_type=F32) for dst, q2 in zip(dsts, q2s)]
        dvvs = [jnp.dot(pt, do2, preferred_element_type=F32) for pt, do2 in zip(pts, do2s)]
        return dq2s, dkks, dvvs, sks

    def stat_row(t):
        tt = t.T
        return jnp.concatenate([tt[0:1, :], tt[HEAD_DIM:HEAD_DIM + 1, :]], axis=1)

    def body(*refs):
        it = iter(refs)
        q_ref, kp_ref, kc_ref, vp_ref, vc_ref, do_ref, l_ref = (next(it) for _ in range(7))
        dl_ref = next(it) if has_dlse else None
        c_ref, s_ref, sink_ref, o_ref, ds_ref, dq_s, dk_s, dv_s, car_q, car_k, car_v = (next(it) for _ in range(11))
        b, j, i = pl.program_id(0), pl.program_id(1), pl.program_id(2)

        @pl.when(jnp.logical_and(b == 0, jnp.logical_and(j == 0, i == 0)))
        def _():
            ds_ref[...] = jnp.zeros_like(ds_ref)

        def compute():
            if geo.sink:
                kall, vall = _rows2(kp_ref, kc_ref, _tile(0)), _rows2(vp_ref, vc_ref, _tile(0))
                tps = tiles_per_kv // A_BWD_SPLIT
                nb = 2 * tps
                tiles = [[kh * tiles_per_kv + s_ * tps + t for t in range(tps)]
                         for kh in range(A_KV_HEADS) for s_ in range(A_BWD_SPLIT)]
                kdup = [_dup_head(kall, kh) for kh in range(A_KV_HEADS)]
                vdup = [_dup_head(vall, kh) for kh in range(A_KV_HEADS)]
                cat = lambda f, ts: jnp.concatenate([f(t) for t in ts], axis=0)
                dq2s, dkks, dvvs, sks = grads(
                    [cat(lambda t: _stack_heads(q_ref[0, :, _tile(t)], SCALE), ts) for ts in tiles],
                    [kdup[n // A_BWD_SPLIT] for n in range(len(tiles))],
                    [vdup[n // A_BWD_SPLIT] for n in range(len(tiles))],
                    [cat(lambda t: _stack_heads(do_ref[0, :, _tile(t)]), ts) for ts in tiles], i,
                    [jnp.concatenate([stat_row(l_ref[0, :, _tile(t)]) for t in ts], axis=1) for ts in tiles],
                    [_sink_scalars(sink_ref, 2 * ts[0], nb) for ts in tiles], [None] * len(tiles))
                lane1 = _lane((1, PAIR_W))
                dsink = jnp.zeros((1, PAIR_W), F32)
                for ts, dq2, sk in zip(tiles, dq2s, sks):
                    for n, t in enumerate(ts):
                        dq_s[:, _tile(t)] = _unstack_heads(dq2[2 * BLOCK * n:2 * BLOCK * (n + 1)])
                    for bb in range(nb):
                        dsink = dsink + jnp.where(lane1 == 2 * ts[0] + bb, sk[bb], 0.0)
                per_kv = lambda parts, kh: functools.reduce(jnp.add, parts[kh * A_BWD_SPLIT:(kh + 1) * A_BWD_SPLIT])
                second = _lane((krows, PAIR_W)) >= HEAD_DIM
                dk_s[...] = jnp.where(second, _fold_heads(per_kv(dkks, 1)), _fold_heads(per_kv(dkks, 0)))
                dv_s[...] = jnp.where(second, _fold_heads(per_kv(dvvs, 1)), _fold_heads(per_kv(dvvs, 0)))
                ds_ref[0:1, :] += dsink
            else:
                dq2s, dkks, dvvs, _ = grads(
                    [_stack_heads(q_ref[0, :, _tile(t)], SCALE) for t in range(ntile)],
                    [_rows2(kp_ref, kc_ref, _tile(t), single) for t in range(ntile)],
                    [_rows2(vp_ref, vc_ref, _tile(t), single) for t in range(ntile)],
                    [_stack_heads(do_ref[0, :, _tile(t)]) for t in range(ntile)], i,
                    [stat_row(l_ref[0, :, _tile(t)]) for t in range(ntile)], [None] * ntile,
                    [stat_row(dl_ref[0, :, _tile(t)]) for t in range(ntile)])
                for t in range(ntile):
                    dq_s[:, _tile(t)] = _unstack_heads(dq2s[t])
                    dk_s[0:krows, _tile(t)] = dkks[t]
                    dv_s[0:krows, _tile(t)] = dvvs[t]

        def emit(dq, dk, dv):
            cos, sn = c_ref[0], s_ref[0]
            o_ref[0, :, 0:qw] = _unrope(dq, cos, sn).astype(BF16)
            o_ref[0, :, qw:qw + kw] = _unrope(dk, cos, sn).astype(BF16)
            o_ref[0, :, qw + kw:qw + 2 * kw] = dv.astype(BF16)
            if qw + 2 * kw < VAR_W:
                o_ref[0, :, qw + 2 * kw:VAR_W] = jnp.zeros((BLOCK, VAR_W - qw - 2 * kw), BF16)

        if single:
            compute()
            emit(dq_s[...], dk_s[0:BLOCK, :], dv_s[0:BLOCK, :])
            return

        @pl.when(i == 0)
        def _():
            car_q[...] = jnp.zeros_like(car_q)
            car_k[...] = jnp.zeros_like(car_k)
            car_v[...] = jnp.zeros_like(car_v)

        @pl.when(i == nblk)
        def _():
            dk_s[...] = jnp.zeros_like(dk_s)
            dv_s[...] = jnp.zeros_like(dv_s)

        pl.when(i < nblk)(compute)
        emit(car_q[...], car_k[...] + dk_s[0:BLOCK, :], car_v[...] + dv_s[0:BLOCK, :])
        car_q[...] = dq_s[...]
        car_k[...] = dk_s[BLOCK:2 * BLOCK, :]
        car_v[...] = dv_s[BLOCK:2 * BLOCK, :]

    cur = lambda i: jnp.minimum(i, nblk - 1)
    prv = lambda i: jnp.maximum(jnp.minimum(i, nblk - 1) - 1, 0)
    outb = lambda i: jnp.maximum(i - 1, 0)
    qrow = pl.BlockSpec((1, BLOCK, qw), lambda b, j, i: (b, cur(i), j))
    in_specs = [
        pl.BlockSpec((1, BLOCK, qw), lambda b, j, i: (b, cur(i), geo.qidx(j))),
        pl.BlockSpec((1, BLOCK, kw), lambda b, j, i: (b, prv(i), geo.kidx(j))),
        pl.BlockSpec((1, BLOCK, kw), lambda b, j, i: (b, cur(i), geo.kidx(j))),
        pl.BlockSpec((1, BLOCK, kw), lambda b, j, i: (b, prv(i), geo.vidx(j))),
        pl.BlockSpec((1, BLOCK, kw), lambda b, j, i: (b, cur(i), geo.vidx(j))),
        qrow, qrow,
    ]
    ins = [view(qkv, VAR_W)] * 5 + [view(do, qw), view(lse, qw)]
    if has_dlse:
        in_specs.append(qrow)
        ins.append(view(dlse, qw))
    in_specs += [
        pl.BlockSpec((1, BLOCK, PAIR_W), lambda b, j, i: (b, outb(i), j)),
        pl.BlockSpec((1, BLOCK, PAIR_W), lambda b, j, i: (b, outb(i), j)),
        pl.BlockSpec(memory_space=pltpu.SMEM),
    ]
    ins += [view(cosf, PAIR_W), view(sins, PAIR_W), sinks]
    scratch = [pltpu.VMEM((BLOCK, qw), F32), pltpu.VMEM((2 * BLOCK, kw), F32), pltpu.VMEM((2 * BLOCK, kw), F32),
               pltpu.VMEM((BLOCK, qw), F32), pltpu.VMEM((BLOCK, kw), F32), pltpu.VMEM((BLOCK, kw), F32)]
    dqkv, dsink = _pcall(
        body, name=name, grid=(NB, r, nsteps), in_specs=in_specs,
        out_specs=[pl.BlockSpec((1, BLOCK, VAR_W), lambda b, j, i: (b, outb(i), j)),
                   pl.BlockSpec((8, PAIR_W), lambda b, j, i: (0, 0))],
        out_shape=[jax.ShapeDtypeStruct((NB, tsub, r * VAR_W), BF16), jax.ShapeDtypeStruct((8, PAIR_W), F32)],
        scratch_shapes=scratch, compiler_params=_params(("arbitrary", "arbitrary", "arbitrary")),
    )(*ins)
    return dqkv, dsink


class _Rows:
    def __init__(self, N, T, tm):
        self.N, self.tm, self.tpe, self.grid = N, tm, T // tm, (N // tm,)

    def row(self, w, col=0):
        return pl.BlockSpec((self.tm, w), lambda i: (i, col))

    def ex(self, w):
        return pl.BlockSpec((1, 1, w), lambda i: (i // self.tpe, 0, 0))

    def const(self, shape):
        return pl.BlockSpec(shape, lambda i: tuple(0 for _ in shape))

    def view(self, w, r):
        return pl.BlockSpec((None, self.tm // r, r * w), lambda i: (i // self.tpe, i % self.tpe, 0))

    def first_of_example(self):
        return pl.program_id(0) % self.tpe == 0


def _acc(ref, first, val):
    @pl.when(first)
    def _():
        ref[0] = val

    @pl.when(jnp.logical_not(first))
    def _():
        ref[0] += val


def _colsum(v):
    return jnp.sum(v, axis=0, keepdims=True)


def _ln_stats(r):
    mu = jnp.mean(r, axis=-1, keepdims=True)
    xc = r - mu
    var = jnp.mean(xc * xc, axis=-1, keepdims=True)
    rstd = lax.rsqrt(var + LN_EPS)
    return xc * rstd, rstd


def _ln_bwd(dy, xhat, rstd, gain):
    dxh = dy * gain
    return rstd * (dxh - jnp.mean(dxh, axis=-1, keepdims=True) - xhat * jnp.mean(dxh * xhat, axis=-1, keepdims=True))


def _from_view(ref, scr, r):
    if r == 1:
        return ref[...]
    rows, w = ref.shape[0], ref.shape[1] // r
    for j in range(r):
        for c in range(w // LANES):
            scr.at[c][pl.ds(j, rows, stride=r), :] = ref[:, j * w + c * LANES:j * w + (c + 1) * LANES].astype(F32)
    return jnp.concatenate([scr[c] for c in range(w // LANES)], axis=1)


def _to_view(val, ref, scr, r):
    if r == 1:
        ref[...] = val.astype(ref.dtype)
        return
    rows, w = ref.shape[0], ref.shape[1] // r
    for c in range(w // LANES):
        scr[c] = val[:, c * LANES:(c + 1) * LANES]
    for j in range(r):
        for c in range(w // LANES):
            ref[:, j * w + c * LANES:j * w + (c + 1) * LANES] = scr.at[c][pl.ds(j, rows, stride=r), :].astype(ref.dtype)


def _silu_parts(v):
    s = jax.nn.sigmoid(v)
    return v * s, s * (1.0 + v * (1.0 - s))


def _local_step(x, mod, positions, w_in, rest_weights, sinks, ln1_g, ln1_b, ln2_g, ln2_b, target, hook=None):
    hook = hook or (lambda event, **data: None)
    NB, T, D = x.shape
    N = NB * T
    x2 = x.reshape(N, D)
    tgt2 = target.reshape(N, D)
    shift_m, scale_m, gate_m, shift_f, scale_f, gate_f = [mod[:, None, k * D:(k + 1) * D] for k in range(6)]
    cosf, sins = _rope_tables(positions)
    col = jnp.arange(QKV_P)
    vcol = col % VAR_W
    flags = jnp.where(col < VAR_W, vcol < QA_W + KA_W, vcol < 2 * GB_W).astype(F32)[None]
    R = _Rows(N, T, _pick(T, 256))
    sds = jax.ShapeDtypeStruct
    exsum = lambda w=D: sds((NB, 1, w), F32)
    ngrp = len(B_PATTERNS)

    *qkv, u = _inproj(x2, scale_m, shift_m, w_in, cosf, sins, flags, T=T, name="inproj_qkv")
    gates = _mm(u, w_in[QKV_P:], tb=True, out_dtype=BF16, name="inproj_gates")
    oa, la = _attn_fwd(qkv[0], sinks, None, NB=NB, T=T, name="attn_a_fwd")
    oa = oa.reshape(N, QA_W)
    (o1, l1), (o2, l2), (o3, l3) = _attn_fwd_b(qkv[1:], NB=NB, T=T, name="attn_b_fwd")
    w_a, w_b, w_o, w_gu, w_d = rest_weights()
    F = w_d.shape[0]
    dil = [r_ for _, r_ in B_PATTERNS]
    views = [R.view(GB_W, r_) for r_ in dil]
    tokbuf = pltpu.VMEM((GB_W // LANES, R.tm, LANES), F32)

    f32 = lambda ref: ref[...].astype(F32)
    Rm = _Rows(N, T, _pick(T, 512))

    def mix_out(o1r, o2r, o3r, l1r, l2r, l3r, oa_r, ga_r, gb_r, x_r, gm_r, g_r, b_r, sf_r, hf_r, wa_r, wb_r, wo_r,
                ob_ref, ya_ref, yb_ref, mg_ref, y_ref, r1_ref, u2_ref, *bufs):
        os_ = [_from_view(ref, bufs[n], dil[n]) for n, ref in enumerate((o1r, o2r, o3r))]
        la, lb, lc = [_from_view(ref, bufs[3 + n], dil[n]) for n, ref in enumerate((l1r, l2r, l3r))]
        mx = jnp.maximum(jnp.maximum(la, lb), lc)
        ea, eb, ec = jnp.exp(la - mx), jnp.exp(lb - mx), jnp.exp(lc - mx)
        ob = ((ea * os_[0] + eb * os_[1] + ec * os_[2]) / (ea + eb + ec)).astype(BF16)
        ob_ref[...] = ob
        ya = jnp.dot(oa_r[...], wa_r[...], preferred_element_type=F32).astype(BF16)
        yb = jnp.concatenate([jnp.dot(ob, wb_r[s_], preferred_element_type=F32)
                              for s_ in range(w_b.shape[0])], axis=1).astype(BF16)
        merged = (jax.nn.sigmoid(f32(ga_r)) * ya.astype(F32) + jax.nn.sigmoid(f32(gb_r)) * yb.astype(F32)).astype(BF16)
        y = jnp.dot(merged, wo_r[...], preferred_element_type=F32)
        r1 = ALPHA * x_r[...] + (1.0 + gm_r[0]) * y
        xhat, _ = _ln_stats(r1)
        x1 = xhat * g_r[...] + b_r[...]
        ya_ref[...], yb_ref[...], mg_ref[...], y_ref[...], r1_ref[...] = ya, yb, merged, y, r1
        u2_ref[...] = (x1 * (1.0 + sf_r[0]) + hf_r[0]).astype(BF16)

    mviews = [Rm.view(GB_W, r_) for r_ in dil]
    ob, ya, yb, merged, y, r1, u2 = _pcall(
        mix_out, name="mix_out", grid=Rm.grid,
        in_specs=mviews + mviews + [Rm.row(QA_W), Rm.row(D, 0), Rm.row(D, 1), Rm.row(D), Rm.ex(D), Rm.const((1, D)),
                                    Rm.const((1, D)), Rm.ex(D), Rm.ex(D), Rm.const(w_a.shape), Rm.const(w_b.shape),
                                    Rm.const(w_o.shape)],
        out_specs=[Rm.row(GB_W)] + [Rm.row(D)] * 6,
        out_shape=[sds((N, GB_W), BF16)] + [sds((N, D), BF16)] * 3 + [sds((N, D), F32)] * 2 + [sds((N, D), BF16)],
        scratch_shapes=[pltpu.VMEM((GB_W // LANES, Rm.tm, LANES), F32)] * 6,
        compiler_params=_params(("parallel",)))(o1, o2, o3, l1, l2, l3, oa, gates, gates, x2, gate_m, ln1_g, ln1_b,
                                                scale_f, shift_f, w_a, w_b, w_o)

    w_gu = w_gu() if callable(w_gu) else w_gu
    tnf = w_gu.shape[2]
    nft = w_gu.shape[0] // 2
    tmf = _pick(N, 512)

    def ffn_up(u_r, wg_r, wu_r, hg_ref, hu_ref, a_ref):
        hg = jnp.dot(u_r[...], wg_r[...], preferred_element_type=F32)
        hu = jnp.dot(u_r[...], wu_r[...], preferred_element_type=F32)
        sl, _ = _silu_parts(hg)
        hg_ref[...] = hg.astype(BF16)
        hu_ref[...] = hu.astype(BF16)
        a_ref[...] = (sl * hu).astype(BF16)

    ftile = pl.BlockSpec((tmf, tnf), lambda j, i: (i, j))
    hg, hu, act = _pcall(
        ffn_up, name="ffn_up", grid=(nft, N // tmf),
        in_specs=[pl.BlockSpec((tmf, D), lambda j, i: (i, 0)), pl.BlockSpec((None, D, tnf), lambda j, i: (j, 0, 0)),
                  pl.BlockSpec((None, D, tnf), lambda j, i: (j + nft, 0, 0))],
        out_specs=[ftile] * 3, out_shape=[sds((N, F), BF16)] * 3,
        compiler_params=_params(("arbitrary", "parallel")))(u2, w_gu, w_gu)
    fchunk = _pick(F, 768)

    def ffn_down_norm2(act_r, wd_r, r1_r, g1_r, b1_r, t_r, gf_r, g_r, b_r, hg_r, hu_r,
                       dy2_ref, dx1_ref, dgf_ref, dg_ref, db_ref, loss_ref, dh_ref):
        first = R.first_of_example()
        y2v = jnp.dot(act_r[...], wd_r[...], preferred_element_type=F32)
        x1 = _ln_stats(r1_r[...])[0] * g1_r[...] + b1_r[...]
        r2 = ALPHA * x1 + (1.0 + gf_r[0]) * y2v
        xhat, rstd = _ln_stats(r2)
        err = xhat * g_r[...] + b_r[...] - t_r[...]
        dx2 = err * (1.0 / D)
        dr2 = _ln_bwd(dx2, xhat, rstd, g_r[...])
        dy2 = ((1.0 + gf_r[0]) * dr2).astype(BF16)
        dy2_ref[...] = dy2
        dx1_ref[...] = ALPHA * dr2
        _acc(dgf_ref, first, _colsum(dr2 * y2v))
        _acc(dg_ref, first, _colsum(dx2 * xhat))
        _acc(db_ref, first, _colsum(dx2))
        part = 0.5 * jnp.sum(jnp.mean(err * err, axis=-1, keepdims=True))
        _acc(loss_ref, first, jnp.broadcast_to(part, (1, 128)))
        for t in range(F // fchunk):
            cs = slice(t * fchunk, (t + 1) * fchunk)
            da = lax.dot_general(dy2, wd_r[cs, :], _NT, preferred_element_type=F32)
            sl, dsl = _silu_parts(hg_r[:, cs].astype(F32))
            dh_ref[:, cs] = (da * hu_r[:, cs].astype(F32) * dsl).astype(BF16)
            dh_ref[:, F + t * fchunk:F + (t + 1) * fchunk] = (da * sl).astype(BF16)

    dy2, dx1p, dgate_f, dg2, db2, loss_p, dh = _pcall(
        ffn_down_norm2, name="ffn_down_norm2", grid=R.grid,
        in_specs=[R.row(F), R.const((F, D)), R.row(D), R.const((1, D)), R.const((1, D)), R.row(D), R.ex(D),
                  R.const((1, D)), R.const((1, D)), R.row(F), R.row(F)],
        out_specs=[R.row(D), R.row(D), R.ex(D), R.ex(D), R.ex(D), R.ex(128), R.row(2 * F)],
        out_shape=[sds((N, D), BF16), sds((N, D), F32), exsum(), exsum(), exsum(), exsum(128), sds((N, 2 * F), BF16)],
        compiler_params=_params(("arbitrary",)))(act, w_d, r1, ln1_g, ln1_b, tgt2, gate_f, ln2_g, ln2_b, hg, hu)

    g_wd = _mm(act, dy2, ta=True, out_dtype=BF16, name="ffn_down_dw")
    g_wgu = _mm(u2, dh, ta=True, out3=w_gu.shape[0], out_dtype=BF16, name="ffn_up_dw")
    hook("ffn_grads", g_wgu=g_wgu)

    def ffn_up_dx_norm1(dh_r, w_r, dx1p_r, r1_r, y_r, sf_r, gm_r, g_r, b_r,
                        dxp_ref, dy_ref, dsf_ref, dhf_ref, dgm_ref, dg_ref, db_ref):
        first = R.first_of_example()
        du2v = None
        for s_ in range(w_gu.shape[0]):
            part = lax.dot_general(dh_r[:, s_ * tnf:(s_ + 1) * tnf], w_r[s_], _NT, preferred_element_type=F32)
            du2v = part if du2v is None else du2v + part
        dx1 = dx1p_r[...] + du2v * (1.0 + sf_r[0])
        xhat, rstd = _ln_stats(r1_r[...])
        dr1 = _ln_bwd(dx1, xhat, rstd, g_r[...])
        dxp_ref[...] = ALPHA * dr1
        dy_ref[...] = ((1.0 + gm_r[0]) * dr1).astype(BF16)
        _acc(dsf_ref, first, _colsum(du2v * (xhat * g_r[...] + b_r[...])))
        _acc(dhf_ref, first, _colsum(du2v))
        _acc(dgm_ref, first, _colsum(dr1 * y_r[...]))
        _acc(dg_ref, first, _colsum(dx1 * xhat))
        _acc(db_ref, first, _colsum(dx1))

    dxp, dy, dscale_f, dshift_f, dgate_m, dg1, db1 = _pcall(
        ffn_up_dx_norm1, name="ffn_up_dx_norm1", grid=R.grid,
        in_specs=[R.row(2 * F), R.const(w_gu.shape)] + [R.row(D)] * 3 + [R.ex(D), R.ex(D), R.const((1, D)),
                                                                        R.const((1, D))],
        out_specs=[R.row(D), R.row(D)] + [R.ex(D)] * 5,
        out_shape=[sds((N, D), F32), sds((N, D), BF16)] + [exsum()] * 5,
        compiler_params=_params(("arbitrary",)))(dh, w_gu, dx1p, r1, y, scale_f, gate_m, ln1_g, ln1_b)

    g_wo = _mm(merged, dy, ta=True, out_dtype=BF16, name="out_proj_dw")

    seg = (jnp.arange(GB_W)[:, None] // HEAD_DIM == jnp.arange(GB_W)[None, :] // HEAD_DIM).astype(BF16)

    def mix_out_bwd(dy_r, ya_r, yb_r, ga_r, gb_r, wo_r, wa_r, wb_r, o1r, o2r, o3r, l1r, l2r, l3r, seg_r,
                    dya_ref, dyb_ref, dg_ref, doa_ref, d1, d2, d3, e1, e2, e3, *bufs):
        dm = lax.dot_general(dy_r[...], wo_r[...], _NT, preferred_element_type=F32).astype(BF16).astype(F32)
        sa, sb = jax.nn.sigmoid(f32(ga_r)), jax.nn.sigmoid(f32(gb_r))
        dya, dyb = (dm * sa).astype(BF16), (dm * sb).astype(BF16)
        dya_ref[...], dyb_ref[...] = dya, dyb
        dg_ref[:, :D] = (dm * f32(ya_r) * sa * (1.0 - sa)).astype(BF16)
        dg_ref[:, D:] = (dm * f32(yb_r) * sb * (1.0 - sb)).astype(BF16)
        doa_ref[...] = lax.dot_general(dya, wa_r[...], _NT, preferred_element_type=F32).astype(BF16)
        ds_ = D // w_b.shape[0]
        dob = None
        for s_ in range(w_b.shape[0]):
            part = lax.dot_general(dyb[:, s_ * ds_:(s_ + 1) * ds_], wb_r[s_], _NT, preferred_element_type=F32)
            dob = part if dob is None else dob + part
        dob_v = dob
        os_ = [_from_view(ref, bufs[n], dil[n]) for n, ref in enumerate((o1r, o2r, o3r))]
        la, lb, lc = [_from_view(ref, bufs[3 + n], dil[n]) for n, ref in enumerate((l1r, l2r, l3r))]
        mx = jnp.maximum(jnp.maximum(la, lb), lc)
        ea, eb, ec = jnp.exp(la - mx), jnp.exp(lb - mx), jnp.exp(lc - mx)
        inv = 1.0 / (ea + eb + ec)
        ws = [ea * inv, eb * inv, ec * inv]

        def headsum(v):
            hi = v.astype(BF16)
            lo = (v - hi.astype(F32)).astype(BF16)
            sm = seg_r[...]
            return jnp.dot(hi, sm, preferred_element_type=F32) + jnp.dot(lo, sm, preferred_element_type=F32)

        dws = [headsum(dob_v * o) for o in os_]
        mean = ws[0] * dws[0] + ws[1] * dws[1] + ws[2] * dws[2]
        for n, (w_, dw_, d_ref, e_ref) in enumerate(zip(ws, dws, (d1, d2, d3), (e1, e2, e3))):
            _to_view(w_ * dob_v, d_ref, bufs[6], dil[n])
            _to_view(w_ * (dw_ - mean), e_ref, bufs[7], dil[n])

    vshape = lambda r_, dt: sds((NB, T // r_, r_ * GB_W), dt)
    dya, dyb, dgates, doa, *mb = _pcall(
        mix_out_bwd, name="mix_out_bwd", grid=R.grid,
        in_specs=[R.row(D)] * 3 + [R.row(D, 0), R.row(D, 1), R.const(w_o.shape), R.const(w_a.shape), R.const(w_b.shape)]
        + views + views + [R.const((GB_W, GB_W))],
        out_specs=[R.row(D), R.row(D), R.row(2 * D), R.row(QA_W)] + views + views,
        out_shape=[sds((N, D), BF16), sds((N, D), BF16), sds((N, 2 * D), BF16), sds((N, QA_W), BF16)]
        + [vshape(r_, BF16) for r_ in dil] + [vshape(r_, F32) for r_ in dil],
        scratch_shapes=[tokbuf] * 8,
        compiler_params=_params(("parallel",)))(dy, ya, yb, gates, gates, w_o, w_a, w_b, o1, o2, o3, l1, l2, l3, seg)
    do_b, dlse_b = mb[:3], mb[3:]

    g_wa = _mm(oa, dya, ta=True, out_dtype=BF16, name="branch_a_dw")
    g_wb = _mm(ob, dyb, ta=True, out3=w_b.shape[0], out_dtype=BF16, name="branch_b_dw")
    hook("rest_grads", g_wa=g_wa, g_wb=g_wb, g_wo=g_wo, g_wd=g_wd)

    dqkv_a, dsink = _attn_bwd(qkv[0], doa, la, None, cosf, sins, sinks, None, NB=NB, T=T, name="attn_a_bwd")
    hook("attn_a_bwd_done")
    dqkv = [dqkv_a]
    for g in range(ngrp):
        dqkv.append(_attn_bwd(qkv[1 + g], do_b[g], (l1, l2, l3)[g], dlse_b[g], cosf, sins, sinks, g, NB=NB, T=T,
                              name=f"attn_b{g}_bwd")[0])
        hook(f"attn_b{g}_bwd_done")

    g_win = [_mm(d3.reshape(N, VAR_W), u, ta=True, out_dtype=BF16, name=f"inproj_dw{v}") if VAR_DIL[v] == 1
             else _dw_view(d3, u, VAR_DIL[v], name=f"inproj_dw{v}") for v, d3 in enumerate(dqkv)]
    g_win.append(_mm(dgates, u, ta=True, out_dtype=BF16, name=f"inproj_dw{N_VAR}"))
    hook("win_grads", g_win=g_win)
    wvar = lambda v: (w_in, (VAR_W, D), (v, 0))
    dview = lambda v: (dqkv[v], VAR_DIL[v])
    du = _mm_multi([dview(0)], [wvar(0)], M=N, T=T, name="inproj_dx0")
    hook("inproj_dx0_done")
    def x_bwd(duv, ins, outs):
        (dxp_r, x_r, sm_r), (gx_ref, dsm_ref, dhm_ref) = ins, outs
        first = R.first_of_example()
        gx_ref[...] = dxp_r[...] + duv * (1.0 + sm_r[0])
        _acc(dsm_ref, first, _colsum(duv * x_r[...]))
        _acc(dhm_ref, first, _colsum(duv))

    gx, dscale_m, dshift_m = _mm_multi(
        [dview(v) for v in range(1, N_VAR)] + [dgates],
        [wvar(v) for v in range(1, N_VAR)] + [(w_in, (2 * D, D), (QKV_P // (2 * D), 0))],
        M=N, T=T, add=du, tm=R.tm, name="inproj_dx1",
        post=([dxp, x2, scale_m], [R.row(D), R.row(D), R.ex(D)], [R.row(D), R.ex(D), R.ex(D)],
              [sds((N, D), F32), exsum(), exsum()], x_bwd))
    hook("inproj_dx1_done")

    dmod =jnp.concatenate([dshift_m, dscale_m, dgate_m, dshift_f, dscale_f, dgate_f], axis=-1)[:, 0]
    ln_grads = jnp.concatenate([dg1, db1, dg2, db2], axis=1)
    return dict(loss=loss_p[:, 0, 0], grad_x=gx.reshape(NB, T, D), g_win=g_win, g_wa=g_wa, g_wb=g_wb, g_wo=g_wo,
                g_wgu=g_wgu, g_wd=g_wd, dmod=dmod, ln_grads=ln_grads, dsink=dsink[0, :A_Q_HEADS])


def _coords():
    return lax.axis_index("x"), lax.axis_index("y"), lax.axis_index("c")


def _allgather_small(blk, *, name):
    m_per, n = blk.shape

    def body(x_ref, out_ref, send_sems, recv_sems, local_sem):
        x, y, c = _coords()
        me, sibling = (x, y, c), (x, y, 1 - c)
        chips = [(1 - x, y), (x, 1 - y), (1 - x, 1 - y)]

        def rows(px, py, pc):
            return out_ref.at[pl.ds((4 * px + 2 * py + pc) * m_per, m_per), :]

        def copy(k, block, to, src=None):
            return pltpu.make_async_remote_copy(
                src_ref=rows(*block) if src is None else src, dst_ref=rows(*block),
                send_sem=send_sems.at[k], recv_sem=recv_sems.at[k], device_id=to, device_id_type=MESH)

        mine = pltpu.make_async_copy(x_ref, rows(*me), local_sem)
        mine.start()
        first = [copy(0, me, sibling, src=x_ref)]
        first += [copy(1 + j, me, (*chip, c), src=x_ref) for j, chip in enumerate(chips)]
        for cp in first:
            cp.start()
        passed = [copy(4 + j, (*chip, c), sibling) for j, chip in enumerate(chips)]
        for j, chip in enumerate(chips):
            copy(1 + j, (*chip, c), me).wait_recv()
            passed[j].start()
        copy(0, sibling, me).wait_recv()
        for j, chip in enumerate(chips):
            copy(4 + j, (*chip, 1 - c), me).wait_recv()
        for cp in first + passed:
            cp.wait_send()
        mine.wait()

    return _pcall(
        body, name=name, out_shape=jax.ShapeDtypeStruct((8 * m_per, n), blk.dtype),
        in_specs=[pl.BlockSpec(memory_space=pltpu.VMEM)], out_specs=pl.BlockSpec(memory_space=pltpu.VMEM),
        scratch_shapes=[pltpu.SemaphoreType.DMA((7,)), pltpu.SemaphoreType.DMA((7,)), pltpu.SemaphoreType.DMA],
        compiler_params=pltpu.CompilerParams(vmem_limit_bytes=VMEM_LIMIT_BYTES),
    )(blk)


def _exchange(srcs, dsts, plan, *, name, dst_inits=None):
    na = len(dsts)
    nrem = len(plan(0, 0, 0))

    def body(*refs):
        refs = list(refs)
        src_refs = [refs.pop(0) for _ in range(na)] if srcs is not None else None
        if dst_inits is not None:
            del refs[:na]
        dst_refs, (send_sems, recv_sems) = refs[:na], refs[na:]
        start, wait = _copies(dst_refs if src_refs is None else src_refs, dst_refs, send_sems, recv_sems, plan)
        start()
        wait()

    hbm = pl.BlockSpec(memory_space=pl.ANY)
    ins = (list(srcs) if srcs is not None else []) + (list(dst_inits) if dst_inits is not None else [])
    base = na if srcs is not None else 0
    aliases = {base + a: a for a in range(na)} if dst_inits is not None else {}
    return _pcall(
        body, name=name, out_shape=list(dsts), in_specs=[hbm] * len(ins), out_specs=[hbm] * na,
        input_output_aliases=aliases,
        scratch_shapes=[pltpu.SemaphoreType.DMA((na * nrem,)), pltpu.SemaphoreType.DMA((na * nrem,))],
    )(*ins)


def _other_chips(x, y):
    return [(1 - x, y), (x, 1 - y), (1 - x, 1 - y)]


def _round(ride, carrier, name):
    if carrier is not None:
        _RIDES.setdefault(carrier, []).append(ride)
        return
    srcs = ride.srcs() if callable(ride.srcs) else ride.srcs
    inits = ride.dst_inits() if callable(ride.dst_inits) else ride.dst_inits
    ride.out = list(_exchange(srcs, ride.dsts, ride.plan, name=name, dst_inits=inits))


class _Gather:
    def __init__(self, shards, chip, tag, carriers=(None, None)):
        def plan_ici(x, y, c):
            k = 2 * x + y
            return [((c,), (k, c), (2 * px + py, c), (px, py, c)) for px, py in _other_chips(x, y)]

        def plan_d2d(x, y, c):
            return [((2 * px + py, c), (2 * px + py, c), (2 * px + py, 1 - c), (x, y, 1 - c))
                    for px, py in _other_chips(x, y)]

        def plan_near(x, y, c):
            k = 2 * x + y
            return [((c,), (k, c), (2 * px + py, c), (px, py, c)) for px, py in ((1 - x, y), (x, 1 - y))]

        def plan_far(x, y, c):
            kx, ky, kd = 2 * (1 - x) + y, 2 * x + (1 - y), 2 * (1 - x) + (1 - y)
            hp = shards[0].shape[1] // 2
            top, bottom = pl.ds(0, hp), pl.ds(hp, hp)
            return [((kx, c, top), (kx, c, top), (kd, c, top), (x, 1 - y, c)),
                    ((ky, c, bottom), (ky, c, bottom), (kd, c, bottom), (1 - x, y, c))]

        self.shards, self.chip = shards, chip
        dsts = [jax.ShapeDtypeStruct((4,) + s.shape, s.dtype) for s in shards]
        if len(carriers) == 3:
            near = _Ride(shards, dsts, plan_near)
            ici = _Ride(None, dsts, plan_far, dst_inits=lambda: near.out)
            _round(near, carriers[0], f"gather_{tag}_near")
            _round(ici, carriers[1], f"gather_{tag}_far")
        else:
            ici = _Ride(shards, dsts, plan_ici)
            _round(ici, carriers[0], f"gather_{tag}_ici")
        self.d2d = _Ride(None, dsts, plan_d2d, dst_inits=lambda: ici.out)
        _round(self.d2d, carriers[-1], f"gather_{tag}_d2d")

    def result(self):
        full = [lax.dynamic_update_index_in_dim(f, s, self.chip, 0) for f, s in zip(self.d2d.out, self.shards)]
        return [f.reshape((4, 2 * f.shape[2], f.shape[3])) for f in full]


def _index_operand(i):
    return jnp.reshape(i, (1,)).astype(jnp.int32)


def _add_pairs(g, f, ci, *, name):
    s, _, hr, wd = g.shape
    tr = _pick(hr, 600, 16)

    def body(c_ref, a_ref, b_ref, o_ref):
        o_ref[...] = (a_ref[...].astype(F32) + b_ref[...].astype(F32)).astype(BF16)

    spec = pl.BlockSpec((1, tr, wd), lambda j, i, c: (j, i, 0))
    grid_spec = pltpu.PrefetchScalarGridSpec(
        num_scalar_prefetch=1, grid=(s, hr // tr),
        in_specs=[pl.BlockSpec((1, None, tr, wd), lambda j, i, c: (j, c[0], i, 0)), spec], out_specs=spec)
    return _pcall(body, name=name, grid_spec=grid_spec, out_shape=jax.ShapeDtypeStruct(f.shape, BF16),
                  compiler_params=_params(("parallel", "parallel")))(_index_operand(ci), g, f)


def _sum_chips(landed, pairs, chip, *, name):
    s, hr, wd = landed.shape
    tr = _pick(hr, 600, 16)

    def body(k_ref, l_ref, p_ref, o_ref):
        acc = None
        for k in range(s):
            part = jnp.where(k_ref[0] == k, p_ref[k], l_ref[k]).astype(F32)
            acc = part if acc is None else acc + part
        o_ref[...] = acc

    spec = pl.BlockSpec((s, tr, wd), lambda i, k: (0, i, 0))
    grid_spec = pltpu.PrefetchScalarGridSpec(
        num_scalar_prefetch=1, grid=(hr // tr,), in_specs=[spec, spec],
        out_specs=pl.BlockSpec((tr, wd), lambda i, k: (i, 0)))
    return _pcall(body, name=name, grid_spec=grid_spec, out_shape=jax.ShapeDtypeStruct((hr, wd), F32),
                  compiler_params=_params(("parallel",)))(_index_operand(chip), landed, pairs)


class _ReduceScatter:
    def __init__(self, gs, chip, ci, tag):
        self.gs, self.chip, self.ci, self.tag = gs, chip, ci, tag
        self.half_t = [jax.ShapeDtypeStruct((g.shape[0],) + g.shape[2:], BF16) for g in gs]

    def pair(self, carrier=None):
        plan = lambda x, y, c: [((slice(None), 1 - c), (), (), (x, y, 1 - c))]
        self.r1 = _Ride(self.gs, self.half_t, plan)
        _round(self.r1, carrier, f"reduce_{self.tag}_pair")

    def chips(self, carrier=None):
        def plan(x, y, c):
            k = 2 * x + y
            return [((2 * px + py,), (k,), (2 * px + py,), (px, py, c)) for px, py in _other_chips(x, y)]

        self.pairs = [_add_pairs(g, f, self.ci, name=f"reduce_{self.tag}_pair_add{n}")
                      for n, (g, f) in enumerate(zip(self.gs, self.r1.out))]
        self.r2 = _Ride(self.pairs, self.half_t, plan)
        _round(self.r2, carrier, f"reduce_{self.tag}_chips")

    def halves(self, carrier=None):
        plan = lambda x, y, c: [((), (c,), (1 - c,), (x, y, 1 - c))]
        self.mine = [_sum_chips(l, p, self.chip, name=f"reduce_{self.tag}_chip_sum{n}")
                     for n, (l, p) in enumerate(zip(self.r2.out, self.pairs))]
        self.r3 = _Ride(self.mine, [jax.ShapeDtypeStruct((2,) + m.shape, F32) for m in self.mine], plan)
        _round(self.r3, carrier, f"reduce_{self.tag}_halves")

    def result(self):
        return [lax.dynamic_update_index_in_dim(b, m, self.ci, 0).reshape(2 * m.shape[0], m.shape[1])
                for b, m in zip(self.r3.out, self.mine)]


def _ada_fwd(c_all, w_sh, b_sh, *, name):
    nb, d = c_all.shape
    wcols = w_sh.shape[1]
    tn = _pick(wcols, 512)

    def body(c_ref, w_ref, b_ref, o_ref, a_ref):
        cv = c_ref[...]
        act = cv * jax.nn.sigmoid(cv)
        a_ref[...] = act
        o_ref[...] = jnp.dot(act.astype(BF16), w_ref[...].astype(BF16), preferred_element_type=F32) + b_ref[...]

    return _pcall(
        body, name=name, grid=(wcols // tn,),
        in_specs=[pl.BlockSpec((nb, d), lambda j: (0, 0)), pl.BlockSpec((d, tn), lambda j: (0, j)),
                  pl.BlockSpec((1, tn), lambda j: (0, j))],
        out_specs=[pl.BlockSpec((nb, tn), lambda j: (0, j)), pl.BlockSpec((nb, d), lambda j: (0, 0))],
        out_shape=[jax.ShapeDtypeStruct((nb, wcols), F32), jax.ShapeDtypeStruct((nb, d), F32)],
        compiler_params=_params(("arbitrary",)))(c_all, w_sh, b_sh)


def _sum_devices(g, *, name):
    nd, m, w = g.shape

    def body(g_ref, o_ref):
        acc = g_ref[0]
        for k in range(1, nd):
            acc = acc + g_ref[k]
        o_ref[...] = acc

    return _pcall(body, name=name, out_shape=jax.ShapeDtypeStruct((m, w), F32),
                  compiler_params=pltpu.CompilerParams(vmem_limit_bytes=VMEM_LIMIT_BYTES))(g)


def _adamw(w, g, m, v, *, name):
    rows, cols = w.shape[-2:]
    tr = _pick(rows, max(8, (1 << 18) // cols), 8)
    c1 = 1.0 / (1.0 - ADAM_B1 ** ADAM_STEP)
    c2 = 1.0 / (1.0 - ADAM_B2 ** ADAM_STEP)

    def body(w_ref, g_ref, m_ref, v_ref, d_ref, nm_ref, nv_ref):
        gv = g_ref[...]
        nm = ADAM_B1 * m_ref[...] + (1.0 - ADAM_B1) * gv
        nv = ADAM_B2 * v_ref[...] + (1.0 - ADAM_B2) * (gv * gv)
        d_ref[...] = -ADAM_LR * ((nm * c1) / (jnp.sqrt(nv * c2) + ADAM_EPS) + ADAM_WD * w_ref[...])
        nm_ref[...] = nm
        nv_ref[...] = nv

    gspec = pl.BlockSpec((tr, cols), lambda i: (i, 0))
    spec = pl.BlockSpec((None, tr, cols), lambda i: (0, i, 0)) if w.ndim == 3 else gspec
    shp = jax.ShapeDtypeStruct(w.shape, F32)
    return _pcall(body, name=name, grid=(rows // tr,), in_specs=[spec, gspec, spec, spec], out_specs=[spec] * 3,
                  out_shape=[shp] * 3, compiler_params=_params(("parallel",)))(w, g, m, v)


def _permute_in_rows(wt):
    ngrp = len(B_PATTERNS)
    qb, kb, vb = (wt[A_W + n * QB_W:A_W + (n + 1) * QB_W] for n in range(3))
    parts = [wt[:A_W], jnp.zeros((VAR_W - A_W, wt.shape[1]), wt.dtype)]
    for g in range(ngrp):
        parts += [t[g * GB_W:(g + 1) * GB_W] for t in (qb, kb, vb)]
    return jnp.concatenate(parts + [wt[A_W + 3 * QB_W:]], axis=0)


def _unpermute_in_grads(pieces):
    ga, groups, gg = pieces[0], pieces[1:-1], pieces[-1]
    rows = [ga[:A_W]]
    for n in range(3):
        rows += [gp[n * GB_W:(n + 1) * GB_W] for gp in groups]
    return jnp.concatenate(rows + [gg], axis=0)


def kernel(x, c, positions, w_ada, b_ada, w_in, sinks, w_branch_a, w_branch_b, w_o, ln1_g, ln1_b, w_gate_up, w_down, ln2_g, ln2_b, loss_target, m_w_ada, m_b_ada, m_w_in, m_sinks, m_w_branch_a, m_w_branch_b, m_w_o, m_ln1_g, m_ln1_b, m_w_gate_up, m_w_down, m_ln2_g, m_ln2_b, v_w_ada, v_b_ada, v_w_in, v_sinks, v_w_branch_a, v_w_branch_b, v_w_o, v_ln1_g, v_ln1_b, v_w_gate_up, v_w_down, v_ln2_g, v_ln2_b):
    xi, yi, ci = _coords()
    chip = 2 * xi + yi
    dev = 4 * xi + 2 * yi + ci
    NB, T, D = x.shape
    nchip, ndev = 4, 8
    ada_cols = w_ada.shape[2]

    ra, ro, rd = w_branch_a.shape[1], w_o.shape[1], w_down.shape[1]
    rowsh = jnp.concatenate([w_branch_a[0], w_o[0], w_down[0]], axis=0)
    halves = lambda a: a.reshape(a.shape[:-2] + (2, a.shape[-2] // 2, a.shape[-1]))
    tr = lambda a: jnp.swapaxes(a, -1, -2)
    shards = [halves(w.astype(BF16)) for w in (tr(w_in[0]), rowsh, w_branch_b[0], w_gate_up[0])]
    gin = _Gather(shards[:1], chip, "w_in", carriers=("gather_c", "ada_fwd", "gather_mod"))

    c_blk = jnp.zeros((8, D), F32).at[:NB].set(c)
    c_all = _allgather_small(c_blk, name="gather_c").reshape(ndev, 8, D)[:, :NB].reshape(ndev * NB, D)
    b_sh = lax.dynamic_slice(b_ada, (0, chip * ada_cols), (1, ada_cols))
    mod_part, c_act = _ada_fwd(c_all, w_ada[0], b_sh, name="ada_fwd")
    mod_g = _allgather_small(mod_part, name="gather_mod").reshape(nchip, 2, ndev * NB, ada_cols)[:, 0]
    mod_all = jnp.transpose(mod_g, (1, 0, 2)).reshape(ndev * NB, nchip * ada_cols)
    mod = lax.dynamic_slice(mod_all, (NB * dev, 0), (NB, nchip * ada_cols))

    (g_in,) = gin.result()
    w_in_f = _permute_in_rows(g_in.reshape(nchip * g_in.shape[1], D))
    mix = _Gather(shards[1:3], chip, "w_mix", carriers=("inproj_qkv", "attn_a_fwd"))
    ffn = _Gather(shards[3:], chip, "w_ffn", carriers=("attn_a_fwd", "attn_b_fwd", "mix_out"))

    def rest_weights():
        g_rows, w_b_f = mix.result()
        return (g_rows[:, :ra].reshape(nchip * ra, D), w_b_f, g_rows[:, ra:ra + ro].reshape(nchip * ro, D),
                lambda: ffn.result()[0], g_rows[:, ra + ro:].reshape(nchip * rd, D))

    red = {}

    def hook(event, **g):
        if event == "ffn_grads":
            red["ffn"] = _ReduceScatter([halves(g["g_wgu"])], chip, ci, "ffn")
            red["ffn"].pair(carrier="out_proj_dw")
        elif event == "rest_grads":
            gr_rows = jnp.concatenate([g["g_wa"].reshape(nchip, ra, D), g["g_wo"].reshape(nchip, ro, D),
                                       g["g_wd"].reshape(nchip, rd, D)], axis=1)
            red["mix"] = _ReduceScatter([halves(gr_rows), halves(g["g_wb"])], chip, ci, "mix")
            red["ffn"].chips(carrier="attn_a_bwd")
            red["mix"].pair(carrier="attn_a_bwd")
        elif event == "attn_a_bwd_done":
            red["ffn"].halves(carrier="attn_b0_bwd")
            red["mix"].chips(carrier="attn_b0_bwd")
        elif event == "attn_b0_bwd_done":
            red["mix"].halves(carrier="attn_b1_bwd")
        elif event == "win_grads":
            gr_in = _unpermute_in_grads(g["g_win"])
            red["w_in"] = _ReduceScatter([halves(gr_in.reshape(nchip, gr_in.shape[0] // nchip, D))], chip, ci, "w_in")
            red["w_in"].pair(carrier="inproj_dx0")
        elif event == "inproj_dx0_done":
            red["w_in"].chips(carrier="inproj_dx1")
        elif event == "inproj_dx1_done":
            red["w_in"].halves(carrier="gather_small")

    res = _local_step(x, mod, positions, w_in_f, rest_weights, sinks[0], ln1_g, ln1_b, ln2_g, ln2_b, loss_target, hook)
    (g_rows_red, g_w_b), (g_w_gu,) = red["mix"].result(), red["ffn"].result()
    g_w_a, g_w_o, g_w_d = g_rows_red[:ra], g_rows_red[ra:ra + ro], g_rows_red[ra + ro:]

    small_rows = 24
    misc = jnp.zeros((1, D), F32).at[0, :A_Q_HEADS].set(res["dsink"]).at[0, A_Q_HEADS].set(jnp.sum(res["loss"]))
    small = jnp.concatenate([res["dmod"].reshape(NB * 6, D), jnp.sum(res["ln_grads"], axis=0), misc,
                             jnp.zeros((small_rows - NB * 6 - 5, D), F32)], axis=0)
    small_all = _allgather_small(small, name="gather_small").reshape(ndev, small_rows, D)
    (g_w_in,) = red["w_in"].result()
    dmod_all = small_all[:, :NB * 6].reshape(ndev * NB, 6 * D)
    sums = _sum_devices(small_all, name="sum_small")
    g_b_ada = (sums[0:6] + sums[6:12]).reshape(1, 6 * D)
    g_ln1_g, g_ln1_b, g_ln2_g, g_ln2_b = (sums[12 + n][None] for n in range(4))
    g_sinks = sums[16, :A_Q_HEADS][None]
    loss = sums[16, A_Q_HEADS]
    dmod_sh = lax.dynamic_slice(dmod_all, (0, chip * ada_cols), (ndev * NB, ada_cols))
    g_w_ada = _mm(c_act, dmod_sh, ta=True, name="ada_dw")

    names = ["w_ada", "b_ada", "w_in", "sinks", "w_branch_a", "w_branch_b", "w_o", "ln1_g", "ln1_b",
             "w_gate_up", "w_down", "ln2_g", "ln2_b"]
    ws = [w_ada, b_ada, w_in, sinks, w_branch_a, w_branch_b, w_o, ln1_g, ln1_b, w_gate_up, w_down, ln2_g, ln2_b]
    ms = [m_w_ada, m_b_ada, m_w_in, m_sinks, m_w_branch_a, m_w_branch_b, m_w_o, m_ln1_g, m_ln1_b, m_w_gate_up,
          m_w_down, m_ln2_g, m_ln2_b]
    vs = [v_w_ada, v_b_ada, v_w_in, v_sinks, v_w_branch_a, v_w_branch_b, v_w_o, v_ln1_g, v_ln1_b, v_w_gate_up,
          v_w_down, v_ln2_g, v_ln2_b]
    gs = [g_w_ada, g_b_ada, g_w_in, g_sinks, g_w_a, g_w_b, g_w_o, g_ln1_g, g_ln1_b, g_w_gu, g_w_d, g_ln2_g, g_ln2_b]
    grads, deltas, new_ms, new_vs = [], [], [], []
    for name, w, g, m, v in zip(names, ws, gs, ms, vs):
        flip = tr if name == "w_in" else (lambda a: a)
        w, m, v = flip(w), flip(m), flip(v)
        g2 = g.reshape(w.shape[-2:])
        d, nm, nv = _adamw(w, g2, m, v, name="adamw_" + name)
        grads.append(flip(g2.reshape(w.shape)))
        deltas.append(flip(d))
        new_ms.append(flip(nm))
        new_vs.append(flip(nv))
    return (loss, res["grad_x"], *grads, *deltas, *new_ms, *new_vs)
```

```python
import functools

import jax
import jax.numpy as jnp
from jax import lax
from jax.experimental import pallas as pl
from jax.experimental.pallas import tpu as pltpu

F32 = jnp.float32
BF16 = jnp.bfloat16
MESH = pl.DeviceIdType.MESH

HEAD_DIM = 64
LANES = 128
PAIR_W = 2 * HEAD_DIM
BLOCK = 128
A_Q_HEADS = 16
A_KV_HEADS = 2
A_WINDOW = 128
B_PATTERNS = ((128, 1), (512, 4), (2048, 16))
B_GROUP_HEADS = 8
QA_W = A_Q_HEADS * HEAD_DIM
KA_W = A_KV_HEADS * HEAD_DIM
GB_W = B_GROUP_HEADS * HEAD_DIM
QB_W = GB_W * len(B_PATTERNS)
A_W = QA_W + 2 * KA_W
VAR_W = 3 * GB_W
N_VAR = 1 + len(B_PATTERNS)
VAR_DIL = (1,) + tuple(r for _, r in B_PATTERNS)
A_BWD_SPLIT = 4
QKV_P = N_VAR * VAR_W
ROPE_THETA = 10000.0
LN_EPS = 1e-5
NEG_INF = -1e30
DEPTH = 1
ALPHA = (2 * DEPTH) ** 0.25
SCALE = HEAD_DIM ** -0.5

ADAM_LR, ADAM_B1, ADAM_B2, ADAM_EPS, ADAM_WD, ADAM_STEP = 0.001, 0.9, 0.999, 1e-08, 0.01, 10

VMEM_LIMIT_BYTES = 56 * 1024 * 1024
MM_TILE_BYTES = 36 * 1024 * 1024
MM_WHOLE_K = 4096


def _params(sem=None):
    return pltpu.CompilerParams(dimension_semantics=sem, vmem_limit_bytes=VMEM_LIMIT_BYTES)


_RIDES = {}


def _pcall(body, *, name, **kw):
    rides = _RIDES.pop(name, None)
    if rides is None:
        return pl.pallas_call(body, name=name, **kw)
    return _riding_call(body, rides, name=name, **kw)


def _copies(src_refs, dst_refs, send_sems, recv_sems, plan):
    x, y, c = lax.axis_index("x"), lax.axis_index("y"), lax.axis_index("c")
    remote = plan(x, y, c)
    nrem = len(remote)
    at = lambda ref, idx: ref.at[idx] if idx else ref

    def copy(a, n, landing):
        si, di, ri, peer = remote[n]
        return pltpu.make_async_remote_copy(
            src_ref=at(src_refs[a], si), dst_ref=at(dst_refs[a], ri if landing else di),
            send_sem=send_sems.at[a * nrem + n], recv_sem=recv_sems.at[a * nrem + n],
            device_id=peer, device_id_type=MESH)

    order = [(a, n) for a in range(len(dst_refs)) for n in range(nrem)]

    def start():
        for a, n in order:
            copy(a, n, False).start()

    def wait():
        for a, n in order:
            copy(a, n, True).wait_recv()
        for a, n in order:
            copy(a, n, False).wait_send()

    return start, wait


class _Ride:
    def __init__(self, srcs, dsts, plan, dst_inits=None):
        self.srcs, self.dsts, self.plan, self.dst_inits, self.out = srcs, dsts, plan, dst_inits, None


def _riding_call(body, rides, *, name, in_specs, out_specs, out_shape, grid=(), scratch_shapes=(), **kw):
    single = not isinstance(out_specs, (list, tuple))
    out_specs = [out_specs] if single else list(out_specs)
    out_shape = [out_shape] if single else list(out_shape)
    n_in, n_out, n_scr = len(in_specs), len(out_specs), len(scratch_shapes)
    xin, xdsts, sems, aliases, layout = [], [], [], {}, []
    for ride in rides:
        srcs = ride.srcs() if callable(ride.srcs) else ride.srcs
        inits = ride.dst_inits() if callable(ride.dst_inits) else ride.dst_inits
        na, nrem = len(ride.dsts), len(ride.plan(0, 0, 0))
        src_at = len(xin) if srcs is not None else None
        xin += list(srcs) if srcs is not None else []
        if inits is not None:
            aliases.update({n_in + len(xin) + a: n_out + len(xdsts) + a for a in range(na)})
            xin += list(inits)
        layout.append((src_at, len(xdsts), na))
        xdsts += list(ride.dsts)
        sems += [pltpu.SemaphoreType.DMA((na * nrem,)), pltpu.SemaphoreType.DMA((na * nrem,))]

    def wrapped(*refs):
        ins, xins = refs[:n_in], refs[n_in:n_in + len(xin)]
        outs = refs[n_in + len(xin):n_in + len(xin) + n_out]
        xouts = refs[n_in + len(xin) + n_out:n_in + len(xin) + n_out + len(xdsts)]
        scr = refs[n_in + len(xin) + n_out + len(xdsts):]
        rounds = []
        for k, (ride, (src_at, dst_at, na)) in enumerate(zip(rides, layout)):
            dsts = xouts[dst_at:dst_at + na]
            srcs = dsts if src_at is None else xins[src_at:src_at + na]
            rounds.append(_copies(srcs, dsts, scr[n_scr + 2 * k], scr[n_scr + 2 * k + 1], ride.plan))
        ids = [pl.program_id(a) for a in range(len(grid))]
        first = functools.reduce(jnp.logical_and, [i == 0 for i in ids], True)
        last = functools.reduce(jnp.logical_and, [i == g - 1 for i, g in zip(ids, grid)], True)

        def start_all():
            for start, _ in rounds:
                start()

        def wait_all():
            for _, wait in rounds:
                wait()

        start_all() if not grid else pl.when(first)(start_all)
        body(*ins, *outs, *scr[:n_scr])
        wait_all() if not grid else pl.when(last)(wait_all)

    hbm = pl.BlockSpec(memory_space=pl.ANY)
    gridkw = dict(grid=grid) if grid else {}

    def run(*args):
        res = pl.pallas_call(
            wrapped, name=name, in_specs=list(in_specs) + [hbm] * len(xin),
            out_specs=out_specs + [hbm] * len(xdsts), out_shape=out_shape + xdsts,
            scratch_shapes=list(scratch_shapes) + sems, input_output_aliases=aliases,
            compiler_params=_params(("arbitrary",) * len(grid) if grid else None), **gridkw,
        )(*args, *xin)
        for ride, (_, dst_at, na) in zip(rides, layout):
            ride.out = list(res[n_out + dst_at:n_out + dst_at + na])
        return res[0] if single else list(res[:n_out])

    return run


def _pick(n, target, quantum=128):
    t = (min(target, n) // quantum) * quantum
    while t >= quantum:
        if n % t == 0:
            return t
        t -= quantum
    return n


def _mm(a, b, *, name, ta=False, tb=False, b3=False, out3=0, out_dtype=F32, add=None, tm=1024, tn=1536, tk=1536):
    if ta:
        K, M = a.shape
    else:
        M, K = a.shape
    if b3 and tb:
        Nn, K2, tk = b.shape[1], b.shape[0] * b.shape[2], b.shape[2]
    elif b3:
        K2, Nn, tn = b.shape[1], b.shape[0] * b.shape[2], b.shape[2]
    elif tb:
        Nn, K2 = b.shape
    else:
        K2, Nn = b.shape
    assert K == K2, (a.shape, b.shape)
    if out3:
        tn = Nn // out3
    tm, tn, tk = _pick(M, tm), _pick(Nn, tn), _pick(K, tk)
    if not (b3 and tb) and K <= MM_WHOLE_K:
        tk = K
        fits = lambda: 4 * tk * (tm + tn) + 8 * tm * tn * (2 if add is not None else 1) <= MM_TILE_BYTES
        while not fits():
            if (tm >= tn or b3 or out3) and tm > 256:
                tm = _pick(M, tm - 128)
            elif not (b3 or out3) and tn > 256:
                tn = _pick(Nn, tn - 128)
            else:
                break
    nk = K // tk
    j_outer = K * Nn + (Nn // tn) * M * K < M * K + (M // tm) * K * Nn
    dn = (((0 if ta else 1,), (1 if tb else 0,)), ((), ()))

    def body(*refs):
        refs = list(refs)
        a_ref, b_ref = refs[:2]
        add_ref = refs[2] if add is not None else None
        o_ref = refs[3] if add is not None else refs[2]
        part = lax.dot_general(a_ref[...].astype(BF16), b_ref[...].astype(BF16), dn, preferred_element_type=F32)

        def finish(r):
            if add is not None:
                r = r + add_ref[...]
            o_ref[...] = r.astype(out_dtype)

        if nk == 1:
            finish(part)
            return
        acc = refs[-1]
        k = pl.program_id(2)

        @pl.when(k == 0)
        def _():
            acc[...] = part

        @pl.when(k > 0)
        def _():
            acc[...] += part

        @pl.when(k == nk - 1)
        def _():
            finish(acc[...])

    def spec(shape, index):
        return pl.BlockSpec(shape, (lambda j, i, k: index(i, j, k)) if j_outer else index)

    a_spec = spec((tk, tm), lambda i, j, k: (k, i)) if ta else spec((tm, tk), lambda i, j, k: (i, k))
    if b3 and tb:
        b_spec = spec((None, tn, tk), lambda i, j, k: (k, j, 0))
    elif b3:
        b_spec = spec((None, tk, tn), lambda i, j, k: (j, k, 0))
    elif tb:
        b_spec = spec((tn, tk), lambda i, j, k: (j, k))
    else:
        b_spec = spec((tk, tn), lambda i, j, k: (k, j))
    if out3:
        o_spec = spec((None, tm, tn), lambda i, j, k: (j, i, 0))
    else:
        o_spec = spec((tm, tn), lambda i, j, k: (i, j))
    ins, specs = [a, b], [a_spec, b_spec]
    if add is not None:
        ins.append(add)
        specs.append(o_spec)
    grid = (Nn // tn, M // tm, nk) if j_outer else (M // tm, Nn // tn, nk)
    return _pcall(
        body, name=name, grid=grid, in_specs=specs, out_specs=o_spec,
        out_shape=jax.ShapeDtypeStruct((out3, M, tn) if out3 else (M, Nn), out_dtype),
        scratch_shapes=[pltpu.VMEM((tm, tn), F32)] if nk > 1 else [],
        compiler_params=_params(("parallel", "parallel", "arbitrary")),
    )(*ins)


def _mm_multi(a_list, b_list, *, name, M, T=None, add=None, out_dtype=F32, tm=512, post=None):
    tm = _pick(T or M, tm)
    ns = len(a_list)
    dils = [a[1] if isinstance(a, tuple) else 0 for a in a_list]
    a_arrs = [a[0] if isinstance(a, tuple) else a for a in a_list]
    widths = [a.shape[-1] // max(r, 1) for a, r in zip(a_arrs, dils)]
    b_arrs, b_specs = [], []
    for b in b_list:
        arr, shp, idx = b if isinstance(b, tuple) else (b, b.shape, (0, 0))
        b_arrs.append(arr)
        b_specs.append(pl.BlockSpec(shp, lambda i, idx=idx: idx))
    Nn = b_specs[0].block_shape[1]
    dn = (((1,), (0,)), ((), ()))
    nmm = 2 * ns + (1 if add is not None else 0)
    p_arrs, p_in_specs, p_out_specs, p_out_shape, p_fn = post or ([], [], None, None, None)
    nin = nmm + len(p_arrs)
    nout = len(p_out_specs) if post else 1

    def body(*refs):
        a_refs, b_refs, scr = refs[:ns], refs[ns:2 * ns], list(refs[nin + nout:])
        acc = None
        for a_ref, b_ref, r in zip(a_refs, b_refs, dils):
            av = _from_view(a_ref, scr.pop(0), r) if r > 1 else a_ref[...]
            part = lax.dot_general(av.astype(BF16), b_ref[...], dn, preferred_element_type=F32)
            acc = part if acc is None else acc + part
        if add is not None:
            acc = acc + refs[2 * ns][...]
        if post:
            p_fn(acc, refs[nmm:nin], refs[nin:nin + nout])
        else:
            refs[nin][...] = acc.astype(out_dtype)

    tpe = (T or M) // tm
    a_specs = [pl.BlockSpec((None, tm // r, r * w), lambda i: (i // tpe, i % tpe, 0)) if r
               else pl.BlockSpec((tm, w), lambda i: (i, 0)) for r, w in zip(dils, widths)]
    o_spec = pl.BlockSpec((tm, Nn), lambda i: (i, 0))
    specs = a_specs + b_specs
    ins = a_arrs + b_arrs
    if add is not None:
        specs.append(o_spec)
        ins.append(add)
    scratch = [pltpu.VMEM((w // LANES, tm, LANES), F32) for r, w in zip(dils, widths) if r > 1]
    return _pcall(body, name=name, grid=(M // tm,), in_specs=specs + list(p_in_specs),
                  out_specs=list(p_out_specs) if post else o_spec, scratch_shapes=scratch,
                  out_shape=list(p_out_shape) if post else jax.ShapeDtypeStruct((M, Nn), out_dtype),
                  compiler_params=_params(("arbitrary",) if post else ("parallel",)))(*ins, *p_arrs)


def _dw_view(d3, u, r, *, name, tk=1024):
    NB, tsub, rw = d3.shape
    W, T, D = rw // r, tsub * r, u.shape[1]
    tk = _pick(T, tk)
    tpe, nk = T // tk, NB * T // tk

    def body(d_ref, u_ref, o_ref, acc, scr):
        k = pl.program_id(0)
        dv = _from_view(d_ref, scr, r).astype(BF16)
        part = lax.dot_general(dv, u_ref[...], _TN, preferred_element_type=F32)

        @pl.when(k == 0)
        def _():
            acc[...] = part

        @pl.when(k > 0)
        def _():
            acc[...] += part

        @pl.when(k == nk - 1)
        def _():
            o_ref[...] = acc[...].astype(o_ref.dtype)

    return _pcall(
        body, name=name, grid=(nk,),
        in_specs=[pl.BlockSpec((None, tk // r, rw), lambda k: (k // tpe, k % tpe, 0)), pl.BlockSpec((tk, D), lambda k: (k, 0))],
        out_specs=pl.BlockSpec((W, D), lambda k: (0, 0)), out_shape=jax.ShapeDtypeStruct((W, D), BF16),
        scratch_shapes=[pltpu.VMEM((W, D), F32), pltpu.VMEM((W // LANES, tk, LANES), F32)],
        compiler_params=_params(("arbitrary",)))(d3, u)


def _lane(shape):
    return lax.broadcasted_iota(jnp.int32, shape, len(shape) - 1)


def _rot_half(v):
    w = v.shape[-1]
    first = (_lane(v.shape) % HEAD_DIM) < (HEAD_DIM // 2)
    return jnp.where(first, pltpu.roll(v, w - HEAD_DIM // 2, v.ndim - 1), pltpu.roll(v, HEAD_DIM // 2, v.ndim - 1))


def _widen(t, w):
    return t if w == t.shape[-1] else jnp.concatenate([t] * (w // t.shape[-1]), axis=-1)


def _unrope(v, cos, sins):
    w = v.shape[-1]
    return v * _widen(cos, w) - _rot_half(v) * _widen(sins, w)


def _rope_tables(positions):
    half = HEAD_DIM // 2
    inv = ROPE_THETA ** (-jnp.arange(half, dtype=F32) / half)
    ang = positions.astype(F32)[..., None] * inv
    cos, sin = jnp.cos(ang), jnp.sin(ang)
    cosf = jnp.concatenate([cos, cos, cos, cos], axis=-1)
    sins = jnp.concatenate([-sin, sin, -sin, sin], axis=-1)
    n = positions.shape[0] * positions.shape[1]
    return cosf.reshape(n, PAIR_W), sins.reshape(n, PAIR_W)


def _inproj(x2, scale, shift, w, cosf, sins, flags, *, T, name):
    N, D = x2.shape
    tm, tn = _pick(T, 512), VAR_W
    tpe = T // tm

    def body(x_ref, sc_ref, sh_ref, w_ref, c_ref, s_ref, f_ref, *outs):
        o_refs, u_ref = outs[:N_VAR], outs[N_VAR]
        j = pl.program_id(1)

        @pl.when(j == 0)
        def _():
            u_ref[...] = (x_ref[...] * (1.0 + sc_ref[0]) + sh_ref[0]).astype(BF16)

        acc = lax.dot_general(u_ref[...], w_ref[...], (((1,), (1,)), ((), ())), preferred_element_type=F32)
        fl = f_ref[...]
        ce = 1.0 + (_widen(c_ref[...], tn) - 1.0) * fl
        se = _widen(s_ref[...], tn) * fl
        res = acc * ce + _rot_half(acc) * se
        for v in range(N_VAR):
            @pl.when(j == v)
            def _(v=v):
                _to_view(res, o_refs[v], outs[N_VAR + 1], VAR_DIL[v])

    ex = pl.BlockSpec((1, 1, D), lambda i, j: (i // tpe, 0, 0))
    tab = pl.BlockSpec((tm, PAIR_W), lambda i, j: (i, 0))
    keep = lambda w_: pl.BlockSpec((tm, w_), lambda i, j: (i, 0))
    vspec = lambda r: pl.BlockSpec((None, tm // r, r * tn), lambda i, j: (i // tpe, i % tpe, 0))
    vshape = lambda r: jax.ShapeDtypeStruct((N // T, T // r, r * tn), BF16)
    return _pcall(
        body, name=name, grid=(N // tm, N_VAR),
        in_specs=[keep(D), ex, ex, pl.BlockSpec((tn, D), lambda i, j: (j, 0)), tab, tab,
                  pl.BlockSpec((1, tn), lambda i, j: (0, j))],
        out_specs=[vspec(r) for r in VAR_DIL] + [keep(D)],
        out_shape=[vshape(r) for r in VAR_DIL] + [jax.ShapeDtypeStruct((N, D), BF16)],
        scratch_shapes=[pltpu.VMEM((tn // LANES, tm, LANES), F32)],
        compiler_params=_params(("parallel", "arbitrary")),
    )(x2, scale, shift, w, cosf, sins, flags)


class _Geom:
    def __init__(self, g):
        if g is None:
            self.r, self.nq, self.n_back, self.sink = 1, A_Q_HEADS, A_WINDOW - 1, True
            self.qw, self.kw = QA_W, KA_W
            self.qidx = lambda j: 0
            self.kidx = lambda j: QA_W // KA_W
            self.vidx = lambda j: QA_W // KA_W + 1
        else:
            window, r = B_PATTERNS[g]
            self.r, self.nq, self.n_back, self.sink = r, B_GROUP_HEADS, window // r, False
            self.qw, self.kw = GB_W, GB_W
            self.qidx = lambda j: 3 * j
            self.kidx = lambda j: 3 * j + 1
            self.vidx = lambda j: 3 * j + 2
        self.ntile = self.qw // PAIR_W


def _stack_heads(t, scale=None):
    first = _lane(t.shape) < HEAD_DIM
    z = jnp.zeros_like(t)
    if scale is not None:
        t = t * jnp.asarray(scale, t.dtype)
    return jnp.concatenate([jnp.where(first, t, z), jnp.where(first, z, t)], axis=0)


def _unstack_heads(v2):
    return jnp.where(_lane((BLOCK, PAIR_W)) < HEAD_DIM, v2[:BLOCK], v2[BLOCK:])


def _dup_head(t, kh):
    tf = t.astype(F32)
    keep = (_lane(t.shape) < HEAD_DIM) if kh == 0 else (_lane(t.shape) >= HEAD_DIM)
    return jnp.where(keep, tf, pltpu.roll(tf, HEAD_DIM, 1)).astype(t.dtype)


def _fold_heads(t):
    return t + pltpu.roll(t, HEAD_DIM, 1)


def _band_mask(rows, i, n_back, single):
    nkeys = BLOCK if single else 2 * BLOCK
    qi = jnp.bitwise_and(lax.broadcasted_iota(jnp.int32, (rows, nkeys), 0), BLOCK - 1)
    ki = lax.broadcasted_iota(jnp.int32, (rows, nkeys), 1)
    if single:
        return qi >= ki
    dist = qi + BLOCK - ki
    return jnp.logical_and(jnp.logical_and(dist >= 0, dist <= n_back), jnp.logical_or(ki >= BLOCK, i > 0))


def _sink_slot(rows):
    qi = jnp.bitwise_and(lax.broadcasted_iota(jnp.int32, (rows, 2 * BLOCK), 0), BLOCK - 1)
    return qi == lax.broadcasted_iota(jnp.int32, (rows, 2 * BLOCK), 1)


def _sink_scores(rows, sinks):
    blk = lax.broadcasted_iota(jnp.int32, (rows, 2 * BLOCK), 0) // BLOCK
    out = jnp.full((rows, 2 * BLOCK), sinks[-1], F32)
    for b in range(len(sinks) - 2, -1, -1):
        out = jnp.where(blk == b, sinks[b], out)
    return out


def _softmax_parts(s, valid, sinks):
    s = jnp.where(valid, s, NEG_INF)
    if sinks is not None:
        slot = _sink_slot(s.shape[0])
        s = jnp.where(slot, _sink_scores(s.shape[0], sinks), s)
    m = jnp.max(s, axis=1, keepdims=True)
    p = jnp.exp(s - m)
    den = jnp.sum(p, axis=1, keepdims=True)
    if sinks is not None:
        p = jnp.where(slot, 0.0, p)
    return p, m, den


_NT = (((1,), (1,)), ((), ()))
_TN = (((0,), (0,)), ((), ()))


def _rows2(prev_ref, cur_ref, cs, single=False):
    if single:
        return cur_ref[0, :, cs]
    return jnp.concatenate([prev_ref[0, :, cs], cur_ref[0, :, cs]], axis=0)


def _sink_scalars(sink_ref, first, nblocks):
    return [sink_ref[first + b] for b in range(nblocks)]


def _tile(t):
    return slice(t * PAIR_W, (t + 1) * PAIR_W)


def _attn_fwd(qkv, sinks, g, *, NB, T, name):
    geo = _Geom(g)
    r, qw, kw, ntile = geo.r, geo.qw, geo.kw, geo.ntile
    tsub = T // r
    nblk = tsub // BLOCK
    qkv3 = qkv.reshape(NB, tsub, r * VAR_W)
    out_dtype = BF16 if g is None else F32
    tiles_per_kv = ntile // A_KV_HEADS

    single = nblk == 1

    def body(q_ref, kp_ref, kc_ref, vp_ref, vc_ref, sink_ref, o_ref, l_ref):
        i = pl.program_id(2)
        if geo.sink:
            kall, vall = _rows2(kp_ref, kc_ref, _tile(0)), _rows2(vp_ref, vc_ref, _tile(0))
            kdup = [_dup_head(kall, kh) for kh in range(A_KV_HEADS)]
            vdup = [_dup_head(vall, kh) for kh in range(A_KV_HEADS)]
            tiles = [[t] for t in range(ntile)]
            q2s = [_stack_heads(q_ref[0, :, _tile(t)], SCALE) for t in range(ntile)]
            kks = [kdup[t // tiles_per_kv] for t in range(ntile)]
            vvs = [vdup[t // tiles_per_kv] for t in range(ntile)]
            sinkcols = [_sink_scalars(sink_ref, 2 * t, 2) for t in range(ntile)]
        else:
            tiles = [[t] for t in range(ntile)]
            q2s = [_stack_heads(q_ref[0, :, _tile(t)], SCALE) for t in range(ntile)]
            kks = [_rows2(kp_ref, kc_ref, _tile(t), single) for t in range(ntile)]
            vvs = [_rows2(vp_ref, vc_ref, _tile(t), single) for t in range(ntile)]
            sinkcols = [None] * ntile
        valid = _band_mask(q2s[0].shape[0], i, geo.n_back, single)
        ss = [lax.dot_general(q2, kk, _NT, preferred_element_type=F32) for q2, kk in zip(q2s, kks)]
        parts = [_softmax_parts(s, valid, sc) for s, sc in zip(ss, sinkcols)]
        o2s = [jnp.dot(p.astype(BF16), vv, preferred_element_type=F32) / den for (p, m, den), vv in zip(parts, vvs)]
        for ts, o2, (p, m, den) in zip(tiles, o2s, parts):
            lse2 = jnp.broadcast_to(m + jnp.log(den), (o2.shape[0], PAIR_W))
            for n, t in enumerate(ts):
                rows = slice(2 * BLOCK * n, 2 * BLOCK * (n + 1))
                o_ref[0, :, _tile(t)] = _unstack_heads(o2[rows]).astype(out_dtype)
                l_ref[0, :, _tile(t)] = _unstack_heads(lse2[rows])

    prev = lambda i: jnp.maximum(i - 1, 0)
    in_specs = [
        pl.BlockSpec((1, BLOCK, qw), lambda b, j, i: (b, i, geo.qidx(j))),
        pl.BlockSpec((1, BLOCK, kw), lambda b, j, i: (b, prev(i), geo.kidx(j))),
        pl.BlockSpec((1, BLOCK, kw), lambda b, j, i: (b, i, geo.kidx(j))),
        pl.BlockSpec((1, BLOCK, kw), lambda b, j, i: (b, prev(i), geo.vidx(j))),
        pl.BlockSpec((1, BLOCK, kw), lambda b, j, i: (b, i, geo.vidx(j))),
        pl.BlockSpec(memory_space=pltpu.SMEM),
    ]
    o_spec = pl.BlockSpec((1, BLOCK, qw), lambda b, j, i: (b, i, j))
    shape = (NB, tsub, r * qw)
    o, lse = _pcall(
        body, name=name, grid=(NB, r, nblk), in_specs=in_specs, out_specs=[o_spec, o_spec],
        out_shape=[jax.ShapeDtypeStruct(shape, out_dtype), jax.ShapeDtypeStruct(shape, F32)],
        compiler_params=_params(("parallel", "parallel", "arbitrary")),
    )(qkv3, qkv3, qkv3, qkv3, qkv3, sinks)
    return o, lse


def _attn_fwd_b(qkvs, *, NB, T, name):
    geos = [_Geom(g) for g in range(len(B_PATTERNS))]
    steps = T // BLOCK
    nt = GB_W // PAIR_W
    ng = len(geos)

    def where(geo, s):
        nblk = T // geo.r // BLOCK
        return s // steps, (s % steps) // nblk, (s % steps) % nblk

    def body(*refs):
        ins, outs = refs[:5 * ng], refs[5 * ng:]
        s = pl.program_id(0)
        q2s, kks, vvs, valids = [], [], [], []
        for n, geo in enumerate(geos):
            q_ref, kp_ref, kc_ref, vp_ref, vc_ref = ins[5 * n:5 * n + 5]
            single = T // geo.r // BLOCK == 1
            valid = _band_mask(2 * BLOCK, where(geo, s)[2], geo.n_back, single)
            for t in range(nt):
                q2s.append(_stack_heads(q_ref[0, :, _tile(t)], SCALE))
                kks.append(_rows2(kp_ref, kc_ref, _tile(t), single))
                vvs.append(_rows2(vp_ref, vc_ref, _tile(t), single))
                valids.append(valid)
        ss = [lax.dot_general(q2, kk, _NT, preferred_element_type=F32) for q2, kk in zip(q2s, kks)]
        parts = [_softmax_parts(sc, valid, None) for sc, valid in zip(ss, valids)]
        o2s = [jnp.dot(p.astype(BF16), vv, preferred_element_type=F32) / den for (p, m, den), vv in zip(parts, vvs)]
        for n in range(ng):
            o_ref, l_ref = outs[2 * n], outs[2 * n + 1]
            for t in range(nt):
                o2, (p, m, den) = o2s[n * nt + t], parts[n * nt + t]
                o_ref[0, :, _tile(t)] = _unstack_heads(o2)
                l_ref[0, :, _tile(t)] = _unstack_heads(jnp.broadcast_to(m + jnp.log(den), (2 * BLOCK, PAIR_W)))

    in_specs, ins, out_specs, out_shape = [], [], [], []
    for geo, qkv in zip(geos, qkvs):
        tsub = T // geo.r
        pos = lambda s, geo=geo: where(geo, s)
        prev = lambda i: jnp.maximum(i - 1, 0)
        blk = lambda col, back, pos=pos: pl.BlockSpec(
            (1, BLOCK, GB_W), lambda s: (pos(s)[0], prev(pos(s)[2]) if back else pos(s)[2], col(pos(s)[1])))
        in_specs += [blk(geo.qidx, False), blk(geo.kidx, True), blk(geo.kidx, False), blk(geo.vidx, True),
                     blk(geo.vidx, False)]
        ins += [qkv.reshape(NB, tsub, geo.r * VAR_W)] * 5
        out_specs += [blk(lambda j: j, False)] * 2
        out_shape += [jax.ShapeDtypeStruct((NB, tsub, geo.r * GB_W), F32)] * 2
    res = _pcall(body, name=name, grid=(NB * steps,), in_specs=in_specs, out_specs=out_specs, out_shape=out_shape,
                 compiler_params=_params(("arbitrary",)))(*ins)
    return [(res[2 * n], res[2 * n + 1]) for n in range(ng)]


def _attn_bwd(qkv, do, lse, dlse, cosf, sins, sinks, g, *, NB, T, name):
    geo = _Geom(g)
    r, qw, kw, ntile = geo.r, geo.qw, geo.kw, geo.ntile
    tsub = T // r
    nblk = tsub // BLOCK
    view = lambda a, w: a.reshape(NB, tsub, r * w)
    has_dlse = dlse is not None
    tiles_per_kv = ntile // A_KV_HEADS

    single = nblk == 1
    krows = BLOCK if single else 2 * BLOCK
    nsteps = 1 if single else nblk + 1

    def grads(q2s, kks, vvs, do2s, i, lserows, sinkcols, dlrows):
        nrow = q2s[0].shape[0]
        ki = lax.broadcasted_iota(jnp.int32, (krows, nrow), 0)
        qi = jnp.bitwise_and(lax.broadcasted_iota(jnp.int32, (krows, nrow), 1), BLOCK - 1)
        if single:
            valid = qi >= ki
        else:
            dist = qi + BLOCK - ki
            valid = jnp.logical_and(jnp.logical_and(dist >= 0, dist <= geo.n_back), jnp.logical_or(ki >= BLOCK, i > 0))
        sts = [lax.dot_general(kk, q2, _NT, preferred_element_type=F32) for q2, kk in zip(q2s, kks)]
        dpts = [lax.dot_general(vv, do2, _NT, preferred_element_type=F32) for do2, vv in zip(do2s, vvs)]
        pts, dsts, sks = [], [], []
        for st, dpt, ls, sc, dl in zip(sts, dpts, lserows, sinkcols, dlrows):
            sv = jnp.where(valid, st, NEG_INF)
            if sc is not None:
                slot = ki == qi
                blk = lax.broadcasted_iota(jnp.int32, (krows, nrow), 1) // BLOCK
                sink = jnp.full((krows, nrow), sc[-1], F32)
                for b in range(len(sc) - 2, -1, -1):
                    sink = jnp.where(blk == b, sc[b], sink)
                sv = jnp.where(slot, sink, sv)
                dpt = jnp.where(slot, 0.0, dpt)
            pt = jnp.exp(sv - ls)
            delta = jnp.sum(pt * dpt, axis=0, keepdims=True)
            if dl is not None:
                delta = delta - dl
            dst = pt * (dpt - delta)
            if sc is not None:
                cols = lambda a, b: a[:, b * BLOCK:(b + 1) * BLOCK]
                sks.append([jnp.sum(jnp.where(cols(slot, b), cols(dst, b), 0.0)) for b in range(len(sc))])
                dst, pt = jnp.where(slot, 0.0, dst), jnp.where(slot, 0.0, pt)
            else:
                sks.append(None)
            pts.append(pt.astype(BF16))
            dsts.append(dst.astype(BF16))
        dq2s = [lax.dot_general(dst, kk, _TN, preferred_element_type=F32) * SCALE for dst, kk in zip(dsts, kks)]
        dkks = [jnp.dot(dst, q2, preferred_element_type=F32) for dst, q2 in zip(dsts, q2s)]
        dvvs = [jnp.dot(pt, do2, preferred_element_type=F32) for pt, do2 in zip(pts, do2s)]
        return dq2s, dkks, dvvs, sks

    def stat_row(t):
        tt = t.T
        return jnp.concatenate([tt[0:1, :], tt[HEAD_DIM:HEAD_DIM + 1, :]], axis=1)

    def body(*refs):
        it = iter(refs)
        q_ref, kp_ref, kc_ref, vp_ref, vc_ref, do_ref, l_ref = (next(it) for _ in range(7))
        dl_ref = next(it) if has_dlse else None
        c_ref, s_ref, sink_ref, o_ref, ds_ref, dq_s, dk_s, dv_s, car_q, car_k, car_v = (next(it) for _ in range(11))
        b, j, i = pl.program_id(0), pl.program_id(1), pl.program_id(2)

        @pl.when(jnp.logical_and(b == 0, jnp.logical_and(j == 0, i == 0)))
        def _():
            ds_ref[...] = jnp.zeros_like(ds_ref)

        def compute():
            if geo.sink:
                kall, vall = _rows2(kp_ref, kc_ref, _tile(0)), _rows2(vp_ref, vc_ref, _tile(0))
                tps = tiles_per_kv // A_BWD_SPLIT
                nb = 2 * tps
                tiles = [[kh * tiles_per_kv + s_ * tps + t for t in range(tps)]
                         for kh in range(A_KV_HEADS) for s_ in range(A_BWD_SPLIT)]
                kdup = [_dup_head(kall, kh) for kh in range(A_KV_HEADS)]
                vdup = [_dup_head(vall, kh) for kh in range(A_KV_HEADS)]
                cat = lambda f, ts: jnp.concatenate([f(t) for t in ts], axis=0)
                dq2s, dkks, dvvs, sks = grads(
                    [cat(lambda t: _stack_heads(q_ref[0, :, _tile(t)], SCALE), ts) for ts in tiles],
                    [kdup[n // A_BWD_SPLIT] for n in range(len(tiles))],
                    [vdup[n // A_BWD_SPLIT] for n in range(len(tiles))],
                    [cat(lambda t: _stack_heads(do_ref[0, :, _tile(t)]), ts) for ts in tiles], i,
                    [jnp.concatenate([stat_row(l_ref[0, :, _tile(t)]) for t in ts], axis=1) for ts in tiles],
                    [_sink_scalars(sink_ref, 2 * ts[0], nb) for ts in tiles], [None] * len(tiles))
                lane1 = _lane((1, PAIR_W))
                dsink = jnp.zeros((1, PAIR_W), F32)
                for ts, dq2, sk in zip(tiles, dq2s, sks):
                    for n, t in enumerate(ts):
                        dq_s[:, _tile(t)] = _unstack_heads(dq2[2 * BLOCK * n:2 * BLOCK * (n + 1)])
                    for bb in range(nb):
                        dsink = dsink + jnp.where(lane1 == 2 * ts[0] + bb, sk[bb], 0.0)
                per_kv = lambda parts, kh: functools.reduce(jnp.add, parts[kh * A_BWD_SPLIT:(kh + 1) * A_BWD_SPLIT])
                second = _lane((krows, PAIR_W)) >= HEAD_DIM
                dk_s[...] = jnp.where(second, _fold_heads(per_kv(dkks, 1)), _fold_heads(per_kv(dkks, 0)))
                dv_s[...] = jnp.where(second, _fold_heads(per_kv(dvvs, 1)), _fold_heads(per_kv(dvvs, 0)))
                ds_ref[0:1, :] += dsink
            else:
                dq2s, dkks, dvvs, _ = grads(
                    [_stack_heads(q_ref[0, :, _tile(t)], SCALE) for t in range(ntile)],
                    [_rows2(kp_ref, kc_ref, _tile(t), single) for t in range(ntile)],
                    [_rows2(vp_ref, vc_ref, _tile(t), single) for t in range(ntile)],
                    [_stack_heads(do_ref[0, :, _tile(t)]) for t in range(ntile)], i,
                    [stat_row(l_ref[0, :, _tile(t)]) for t in range(ntile)], [None] * ntile,
                    [stat_row(dl_ref[0, :, _tile(t)]) for t in range(ntile)])
                for t in range(ntile):
                    dq_s[:, _tile(t)] = _unstack_heads(dq2s[t])
                    dk_s[0:krows, _tile(t)] = dkks[t]
                    dv_s[0:krows, _tile(t)] = dvvs[t]

        def emit(dq, dk, dv):
            cos, sn = c_ref[0], s_ref[0]
            o_ref[0, :, 0:qw] = _unrope(dq, cos, sn).astype(BF16)
            o_ref[0, :, qw:qw + kw] = _unrope(dk, cos, sn).astype(BF16)
            o_ref[0, :, qw + kw:qw + 2 * kw] = dv.astype(BF16)
            if qw + 2 * kw < VAR_W:
                o_ref[0, :, qw + 2 * kw:VAR_W] = jnp.zeros((BLOCK, VAR_W - qw - 2 * kw), BF16)

        if single:
            compute()
            emit(dq_s[...], dk_s[0:BLOCK, :], dv_s[0:BLOCK, :])
            return

        @pl.when(i == 0)
        def _():
            car_q[...] = jnp.zeros_like(car_q)
            car_k[...] = jnp.zeros_like(car_k)
            car_v[...] = jnp.zeros_like(car_v)

        @pl.when(i == nblk)
        def _():
            dk_s[...] = jnp.zeros_like(dk_s)
            dv_s[...] = jnp.zeros_like(dv_s)

        pl.when(i < nblk)(compute)
        emit(car_q[...], car_k[...] + dk_s[0:BLOCK, :], car_v[...] + dv_s[0:BLOCK, :])
        car_q[...] = dq_s[...]
        car_k[...] = dk_s[BLOCK:2 * BLOCK, :]
        car_v[...] = dv_s[BLOCK:2 * BLOCK, :]

    cur = lambda i: jnp.minimum(i, nblk - 1)
    prv = lambda i: jnp.maximum(jnp.minimum(i, nblk - 1) - 1, 0)
    outb = lambda i: jnp.maximum(i - 1, 0)
    qrow = pl.BlockSpec((1, BLOCK, qw), lambda b, j, i: (b, cur(i), j))
    in_specs = [
        pl.BlockSpec((1, BLOCK, qw), lambda b, j, i: (b, cur(i), geo.qidx(j))),
        pl.BlockSpec((1, BLOCK, kw), lambda b, j, i: (b, prv(i), geo.kidx(j))),
        pl.BlockSpec((1, BLOCK, kw), lambda b, j, i: (b, cur(i), geo.kidx(j))),
        pl.BlockSpec((1, BLOCK, kw), lambda b, j, i: (b, prv(i), geo.vidx(j))),
        pl.BlockSpec((1, BLOCK, kw), lambda b, j, i: (b, cur(i), geo.vidx(j))),
        qrow, qrow,
    ]
    ins = [view(qkv, VAR_W)] * 5 + [view(do, qw), view(lse, qw)]
    if has_dlse:
        in_specs.append(qrow)
        ins.append(view(dlse, qw))
    in_specs += [
        pl.BlockSpec((1, BLOCK, PAIR_W), lambda b, j, i: (b, outb(i), j)),
        pl.BlockSpec((1, BLOCK, PAIR_W), lambda b, j, i: (b, outb(i), j)),
        pl.BlockSpec(memory_space=pltpu.SMEM),
    ]
    ins += [view(cosf, PAIR_W), view(sins, PAIR_W), sinks]
    scratch = [pltpu.VMEM((BLOCK, qw), F32), pltpu.VMEM((2 * BLOCK, kw), F32), pltpu.VMEM((2 * BLOCK, kw), F32),
               pltpu.VMEM((BLOCK, qw), F32), pltpu.VMEM((BLOCK, kw), F32), pltpu.VMEM((BLOCK, kw), F32)]
    dqkv, dsink = _pcall(
        body, name=name, grid=(NB, r, nsteps), in_specs=in_specs,
        out_specs=[pl.BlockSpec((1, BLOCK, VAR_W), lambda b, j, i: (b, outb(i), j)),
                   pl.BlockSpec((8, PAIR_W), lambda b, j, i: (0, 0))],
        out_shape=[jax.ShapeDtypeStruct((NB, tsub, r * VAR_W), BF16), jax.ShapeDtypeStruct((8, PAIR_W), F32)],
        scratch_shapes=scratch, compiler_params=_params(("arbitrary", "arbitrary", "arbitrary")),
    )(*ins)
    return dqkv, dsink


class _Rows:
    def __init__(self, N, T, tm):
        self.N, self.tm, self.tpe, self.grid = N, tm, T // tm, (N // tm,)

    def row(self, w, col=0):
        return pl.BlockSpec((self.tm, w), lambda i: (i, col))

    def ex(self, w):
        return pl.BlockSpec((1, 1, w), lambda i: (i // self.tpe, 0, 0))

    def const(self, shape):
        return pl.BlockSpec(shape, lambda i: tuple(0 for _ in shape))

    def view(self, w, r):
        return pl.BlockSpec((None, self.tm // r, r * w), lambda i: (i // self.tpe, i % self.tpe, 0))

    def first_of_example(self):
        return pl.program_id(0) % self.tpe == 0


def _acc(ref, first, val):
    @pl.when(first)
    def _():
        ref[0] = val

    @pl.when(jnp.logical_not(first))
    def _():
        ref[0] += val


def _colsum(v):
    return jnp.sum(v, axis=0, keepdims=True)


def _ln_stats(r):
    mu = jnp.mean(r, axis=-1, keepdims=True)
    xc = r - mu
    var = jnp.mean(xc * xc, axis=-1, keepdims=True)
    rstd = lax.rsqrt(var + LN_EPS)
    return xc * rstd, rstd


def _ln_bwd(dy, xhat, rstd, gain):
    dxh = dy * gain
    return rstd * (dxh - jnp.mean(dxh, axis=-1, keepdims=True) - xhat * jnp.mean(dxh * xhat, axis=-1, keepdims=True))


def _from_view(ref, scr, r):
    if r == 1:
        return ref[...]
    rows, w = ref.shape[0], ref.shape[1] // r
    for j in range(r):
        for c in range(w // LANES):
            scr.at[c][pl.ds(j, rows, stride=r), :] = ref[:, j * w + c * LANES:j * w + (c + 1) * LANES].astype(F32)
    return jnp.concatenate([scr[c] for c in range(w // LANES)], axis=1)


def _to_view(val, ref, scr, r):
    if r == 1:
        ref[...] = val.astype(ref.dtype)
        return
    rows, w = ref.shape[0], ref.shape[1] // r
    for c in range(w // LANES):
        scr[c] = val[:, c * LANES:(c + 1) * LANES]
    for j in range(r):
        for c in range(w // LANES):
            ref[:, j * w + c * LANES:j * w + (c + 1) * LANES] = scr.at[c][pl.ds(j, rows, stride=r), :].astype(ref.dtype)


def _silu_parts(v):
    s = jax.nn.sigmoid(v)
    return v * s, s * (1.0 + v * (1.0 - s))


def _local_step(x, mod, positions, w_in, rest_weights, sinks, ln1_g, ln1_b, ln2_g, ln2_b, target, hook=None):
    hook = hook or (lambda event, **data: None)
    NB, T, D = x.shape
    N = NB * T
    x2 = x.reshape(N, D)
    tgt2 = target.reshape(N, D)
    shift_m, scale_m, gate_m, shift_f, scale_f, gate_f = [mod[:, None, k * D:(k + 1) * D] for k in range(6)]
    cosf, sins = _rope_tables(positions)
    col = jnp.arange(QKV_P)
    vcol = col % VAR_W
    flags = jnp.where(col < VAR_W, vcol < QA_W + KA_W, vcol < 2 * GB_W).astype(F32)[None]
    R = _Rows(N, T, _pick(T, 256))
    sds = jax.ShapeDtypeStruct
    exsum = lambda w=D: sds((NB, 1, w), F32)
    ngrp = len(B_PATTERNS)

    *qkv, u = _inproj(x2, scale_m, shift_m, w_in, cosf, sins, flags, T=T, name="inproj_qkv")
    gates = _mm(u, w_in[QKV_P:], tb=True, out_dtype=BF16, name="inproj_gates")
    oa, la = _attn_fwd(qkv[0], sinks, None, NB=NB, T=T, name="attn_a_fwd")
    oa = oa.reshape(N, QA_W)
    (o1, l1), (o2, l2), (o3, l3) = _attn_fwd_b(qkv[1:], NB=NB, T=T, name="attn_b_fwd")
    w_a, w_b, w_o, w_gu, w_d = rest_weights()
    F = w_d.shape[0]
    dil = [r_ for _, r_ in B_PATTERNS]
    views = [R.view(GB_W, r_) for r_ in dil]
    tokbuf = pltpu.VMEM((GB_W // LANES, R.tm, LANES), F32)

    f32 = lambda ref: ref[...].astype(F32)
    Rm = _Rows(N, T, _pick(T, 512))

    def mix_out(o1r, o2r, o3r, l1r, l2r, l3r, oa_r, ga_r, gb_r, x_r, gm_r, g_r, b_r, sf_r, hf_r, wa_r, wb_r, wo_r,
                ob_ref, ya_ref, yb_ref, mg_ref, y_ref, r1_ref, u2_ref, *bufs):
        os_ = [_from_view(ref, bufs[n], dil[n]) for n, ref in enumerate((o1r, o2r, o3r))]
        la, lb, lc = [_from_view(ref, bufs[3 + n], dil[n]) for n, ref in enumerate((l1r, l2r, l3r))]
        mx = jnp.maximum(jnp.maximum(la, lb), lc)
        ea, eb, ec = jnp.exp(la - mx), jnp.exp(lb - mx), jnp.exp(lc - mx)
        ob = ((ea * os_[0] + eb * os_[1] + ec * os_[2]) / (ea + eb + ec)).astype(BF16)
        ob_ref[...] = ob
        ya = jnp.dot(oa_r[...], wa_r[...], preferred_element_type=F32).astype(BF16)
        yb = jnp.concatenate([jnp.dot(ob, wb_r[s_], preferred_element_type=F32)
                              for s_ in range(w_b.shape[0])], axis=1).astype(BF16)
        merged = (jax.nn.sigmoid(f32(ga_r)) * ya.astype(F32) + jax.nn.sigmoid(f32(gb_r)) * yb.astype(F32)).astype(BF16)
        y = jnp.dot(merged, wo_r[...], preferred_element_type=F32)
        r1 = ALPHA * x_r[...] + (1.0 + gm_r[0]) * y
        xhat, _ = _ln_stats(r1)
        x1 = xhat * g_r[...] + b_r[...]
        ya_ref[...], yb_ref[...], mg_ref[...], y_ref[...], r1_ref[...] = ya, yb, merged, y, r1
        u2_ref[...] = (x1 * (1.0 + sf_r[0]) + hf_r[0]).astype(BF16)

    mviews = [Rm.view(GB_W, r_) for r_ in dil]
    ob, ya, yb, merged, y, r1, u2 = _pcall(
        mix_out, name="mix_out", grid=Rm.grid,
        in_specs=mviews + mviews + [Rm.row(QA_W), Rm.row(D, 0), Rm.row(D, 1), Rm.row(D), Rm.ex(D), Rm.const((1, D)),
                                    Rm.const((1, D)), Rm.ex(D), Rm.ex(D), Rm.const(w_a.shape), Rm.const(w_b.shape),
                                    Rm.const(w_o.shape)],
        out_specs=[Rm.row(GB_W)] + [Rm.row(D)] * 6,
        out_shape=[sds((N, GB_W), BF16)] + [sds((N, D), BF16)] * 3 + [sds((N, D), F32)] * 2 + [sds((N, D), BF16)],
        scratch_shapes=[pltpu.VMEM((GB_W // LANES, Rm.tm, LANES), F32)] * 6,
        compiler_params=_params(("parallel",)))(o1, o2, o3, l1, l2, l3, oa, gates, gates, x2, gate_m, ln1_g, ln1_b,
                                                scale_f, shift_f, w_a, w_b, w_o)

    w_gu = w_gu() if callable(w_gu) else w_gu
    tnf = w_gu.shape[2]
    nft = w_gu.shape[0] // 2
    tmf = _pick(N, 512)

    def ffn_up(u_r, wg_r, wu_r, hg_ref, hu_ref, a_ref):
        hg = jnp.dot(u_r[...], wg_r[...], preferred_element_type=F32)
        hu = jnp.dot(u_r[...], wu_r[...], preferred_element_type=F32)
        sl, _ = _silu_parts(hg)
        hg_ref[...] = hg.astype(BF16)
        hu_ref[...] = hu.astype(BF16)
        a_ref[...] = (sl * hu).astype(BF16)

    ftile = pl.BlockSpec((tmf, tnf), lambda j, i: (i, j))
    hg, hu, act = _pcall(
        ffn_up, name="ffn_up", grid=(nft, N // tmf),
        in_specs=[pl.BlockSpec((tmf, D), lambda j, i: (i, 0)), pl.BlockSpec((None, D, tnf), lambda j, i: (j, 0, 0)),
                  pl.BlockSpec((None, D, tnf), lambda j, i: (j + nft, 0, 0))],
        out_specs=[ftile] * 3, out_shape=[sds((N, F), BF16)] * 3,
        compiler_params=_params(("arbitrary", "parallel")))(u2, w_gu, w_gu)
    fchunk = _pick(F, 768)

    def ffn_down_norm2(act_r, wd_r, r1_r, g1_r, b1_r, t_r, gf_r, g_r, b_r, hg_r, hu_r,
                       dy2_ref, dx1_ref, dgf_ref, dg_ref, db_ref, loss_ref, dh_ref):
        first = R.first_of_example()
        y2v = jnp.dot(act_r[...], wd_r[...], preferred_element_type=F32)
        x1 = _ln_stats(r1_r[...])[0] * g1_r[...] + b1_r[...]
        r2 = ALPHA * x1 + (1.0 + gf_r[0]) * y2v
        xhat, rstd = _ln_stats(r2)
        err = xhat * g_r[...] + b_r[...] - t_r[...]
        dx2 = err * (1.0 / D)
        dr2 = _ln_bwd(dx2, xhat, rstd, g_r[...])
        dy2 = ((1.0 + gf_r[0]) * dr2).astype(BF16)
        dy2_ref[...] = dy2
        dx1_ref[...] = ALPHA * dr2
        _acc(dgf_ref, first, _colsum(dr2 * y2v))
        _acc(dg_ref, first, _colsum(dx2 * xhat))
        _acc(db_ref, first, _colsum(dx2))
        part = 0.5 * jnp.sum(jnp.mean(err * err, axis=-1, keepdims=True))
        _acc(loss_ref, first, jnp.broadcast_to(part, (1, 128)))
        for t in range(F // fchunk):
            cs = slice(t * fchunk, (t + 1) * fchunk)
            da = lax.dot_general(dy2, wd_r[cs, :], _NT, preferred_element_type=F32)
            sl, dsl = _silu_parts(hg_r[:, cs].astype(F32))
            dh_ref[:, cs] = (da * hu_r[:, cs].astype(F32) * dsl).astype(BF16)
            dh_ref[:, F + t * fchunk:F + (t + 1) * fchunk] = (da * sl).astype(BF16)

    dy2, dx1p, dgate_f, dg2, db2, loss_p, dh = _pcall(
        ffn_down_norm2, name="ffn_down_norm2", grid=R.grid,
        in_specs=[R.row(F), R.const((F, D)), R.row(D), R.const((1, D)), R.const((1, D)), R.row(D), R.ex(D),
                  R.const((1, D)), R.const((1, D)), R.row(F), R.row(F)],
        out_specs=[R.row(D), R.row(D), R.ex(D), R.ex(D), R.ex(D), R.ex(128), R.row(2 * F)],
        out_shape=[sds((N, D), BF16), sds((N, D), F32), exsum(), exsum(), exsum(), exsum(128), sds((N, 2 * F), BF16)],
        compiler_params=_params(("arbitrary",)))(act, w_d, r1, ln1_g, ln1_b, tgt2, gate_f, ln2_g, ln2_b, hg, hu)

    g_wd = _mm(act, dy2, ta=True, out_dtype=BF16, name="ffn_down_dw")
    g_wgu = _mm(u2, dh, ta=True, out3=w_gu.shape[0], out_dtype=BF16, name="ffn_up_dw")
    hook("ffn_grads", g_wgu=g_wgu)

    def ffn_up_dx_norm1(dh_r, w_r, dx1p_r, r1_r, y_r, sf_r, gm_r, g_r, b_r,
                        dxp_ref, dy_ref, dsf_ref, dhf_ref, dgm_ref, dg_ref, db_ref):
        first = R.first_of_example()
        du2v = None
        for s_ in range(w_gu.shape[0]):
            part = lax.dot_general(dh_r[:, s_ * tnf:(s_ + 1) * tnf], w_r[s_], _NT, preferred_element_type=F32)
            du2v = part if du2v is None else du2v + part
        dx1 = dx1p_r[...] + du2v * (1.0 + sf_r[0])
        xhat, rstd = _ln_stats(r1_r[...])
        dr1 = _ln_bwd(dx1, xhat, rstd, g_r[...])
        dxp_ref[...] = ALPHA * dr1
        dy_ref[...] = ((1.0 + gm_r[0]) * dr1).astype(BF16)
        _acc(dsf_ref, first, _colsum(du2v * (xhat * g_r[...] + b_r[...])))
        _acc(dhf_ref, first, _colsum(du2v))
        _acc(dgm_ref, first, _colsum(dr1 * y_r[...]))
        _acc(dg_ref, first, _colsum(dx1 * xhat))
        _acc(db_ref, first, _colsum(dx1))

    dxp, dy, dscale_f, dshift_f, dgate_m, dg1, db1 = _pcall(
        ffn_up_dx_norm1, name="ffn_up_dx_norm1", grid=R.grid,
        in_specs=[R.row(2 * F), R.const(w_gu.shape)] + [R.row(D)] * 3 + [R.ex(D), R.ex(D), R.const((1, D)),
                                                                        R.const((1, D))],
        out_specs=[R.row(D), R.row(D)] + [R.ex(D)] * 5,
        out_shape=[sds((N, D), F32), sds((N, D), BF16)] + [exsum()] * 5,
        compiler_params=_params(("arbitrary",)))(dh, w_gu, dx1p, r1, y, scale_f, gate_m, ln1_g, ln1_b)

    g_wo = _mm(merged, dy, ta=True, out_dtype=BF16, name="out_proj_dw")

    seg = (jnp.arange(GB_W)[:, None] // HEAD_DIM == jnp.arange(GB_W)[None, :] // HEAD_DIM).astype(BF16)

    def mix_out_bwd(dy_r, ya_r, yb_r, ga_r, gb_r, wo_r, wa_r, wb_r, o1r, o2r, o3r, l1r, l2r, l3r, seg_r,
                    dya_ref, dyb_ref, dg_ref, doa_ref, d1, d2, d3, e1, e2, e3, *bufs):
        dm = lax.dot_general(dy_r[...], wo_r[...], _NT, preferred_element_type=F32).astype(BF16).astype(F32)
        sa, sb = jax.nn.sigmoid(f32(ga_r)), jax.nn.sigmoid(f32(gb_r))
        dya, dyb = (dm * sa).astype(BF16), (dm * sb).astype(BF16)
        dya_ref[...], dyb_ref[...] = dya, dyb
        dg_ref[:, :D] = (dm * f32(ya_r) * sa * (1.0 - sa)).astype(BF16)
        dg_ref[:, D:] = (dm * f32(yb_r) * sb * (1.0 - sb)).astype(BF16)
        doa_ref[...] = lax.dot_general(dya, wa_r[...], _NT, preferred_element_type=F32).astype(BF16)
        ds_ = D // w_b.shape[0]
        dob = None
        for s_ in range(w_b.shape[0]):
            part = lax.dot_general(dyb[:, s_ * ds_:(s_ + 1) * ds_], wb_r[s_], _NT, preferred_element_type=F32)
            dob = part if dob is None else dob + part
        dob_v = dob
        os_ = [_from_view(ref, bufs[n], dil[n]) for n, ref in enumerate((o1r, o2r, o3r))]
        la, lb, lc = [_from_view(ref, bufs[3 + n], dil[n]) for n, ref in enumerate((l1r, l2r, l3r))]
        mx = jnp.maximum(jnp.maximum(la, lb), lc)
        ea, eb, ec = jnp.exp(la - mx), jnp.exp(lb - mx), jnp.exp(lc - mx)
        inv = 1.0 / (ea + eb + ec)
        ws = [ea * inv, eb * inv, ec * inv]

        def headsum(v):
            hi = v.astype(BF16)
            lo = (v - hi.astype(F32)).astype(BF16)
            sm = seg_r[...]
            return jnp.dot(hi, sm, preferred_element_type=F32) + jnp.dot(lo, sm, preferred_element_type=F32)

        dws = [headsum(dob_v * o) for o in os_]
        mean = ws[0] * dws[0] + ws[1] * dws[1] + ws[2] * dws[2]
        for n, (w_, dw_, d_ref, e_ref) in enumerate(zip(ws, dws, (d1, d2, d3), (e1, e2, e3))):
            _to_view(w_ * dob_v, d_ref, bufs[6], dil[n])
            _to_view(w_ * (dw_ - mean), e_ref, bufs[7], dil[n])

    vshape = lambda r_, dt: sds((NB, T // r_, r_ * GB_W), dt)
    dya, dyb, dgates, doa, *mb = _pcall(
        mix_out_bwd, name="mix_out_bwd", grid=R.grid,
        in_specs=[R.row(D)] * 3 + [R.row(D, 0), R.row(D, 1), R.const(w_o.shape), R.const(w_a.shape), R.const(w_b.shape)]
        + views + views + [R.const((GB_W, GB_W))],
        out_specs=[R.row(D), R.row(D), R.row(2 * D), R.row(QA_W)] + views + views,
        out_shape=[sds((N, D), BF16), sds((N, D), BF16), sds((N, 2 * D), BF16), sds((N, QA_W), BF16)]
        + [vshape(r_, BF16) for r_ in dil] + [vshape(r_, F32) for r_ in dil],
        scratch_shapes=[tokbuf] * 8,
        compiler_params=_params(("parallel",)))(dy, ya, yb, gates, gates, w_o, w_a, w_b, o1, o2, o3, l1, l2, l3, seg)
    do_b, dlse_b = mb[:3], mb[3:]

    g_wa = _mm(oa, dya, ta=True, out_dtype=BF16, name="branch_a_dw")
    g_wb = _mm(ob, dyb, ta=True, out3=w_b.shape[0], out_dtype=BF16, name="branch_b_dw")
    hook("rest_grads", g_wa=g_wa, g_wb=g_wb, g_wo=g_wo, g_wd=g_wd)

    dqkv_a, dsink = _attn_bwd(qkv[0], doa, la, None, cosf, sins, sinks, None, NB=NB, T=T, name="attn_a_bwd")
    hook("attn_a_bwd_done")
    dqkv = [dqkv_a]
    for g in range(ngrp):
        dqkv.append(_attn_bwd(qkv[1 + g], do_b[g], (l1, l2, l3)[g], dlse_b[g], cosf, sins, sinks, g, NB=NB, T=T,
                              name=f"attn_b{g}_bwd")[0])
        hook(f"attn_b{g}_bwd_done")

    g_win = [_mm(d3.reshape(N, VAR_W), u, ta=True, out_dtype=BF16, name=f"inproj_dw{v}") if VAR_DIL[v] == 1
             else _dw_view(d3, u, VAR_DIL[v], name=f"inproj_dw{v}") for v, d3 in enumerate(dqkv)]
    g_win.append(_mm(dgates, u, ta=True, out_dtype=BF16, name=f"inproj_dw{N_VAR}"))
    hook("win_grads", g_win=g_win)
    wvar = lambda v: (w_in, (VAR_W, D), (v, 0))
    dview = lambda v: (dqkv[v], VAR_DIL[v])
    du = _mm_multi([dview(0)], [wvar(0)], M=N, T=T, name="inproj_dx0")
    hook("inproj_dx0_done")
    def x_bwd(duv, ins, outs):
        (dxp_r, x_r, sm_r), (gx_ref, dsm_ref, dhm_ref) = ins, outs
        first = R.first_of_example()
        gx_ref[...] = dxp_r[...] + duv * (1.0 + sm_r[0])
        _acc(dsm_ref, first, _colsum(duv * x_r[...]))
        _acc(dhm_ref, first, _colsum(duv))

    gx, dscale_m, dshift_m = _mm_multi(
        [dview(v) for v in range(1, N_VAR)] + [dgates],
        [wvar(v) for v in range(1, N_VAR)] + [(w_in, (2 * D, D), (QKV_P // (2 * D), 0))],
        M=N, T=T, add=du, tm=R.tm, name="inproj_dx1",
        post=([dxp, x2, scale_m], [R.row(D), R.row(D), R.ex(D)], [R.row(D), R.ex(D), R.ex(D)],
              [sds((N, D), F32), exsum(), exsum()], x_bwd))
    hook("inproj_dx1_done")

    dmod =jnp.concatenate([dshift_m, dscale_m, dgate_m, dshift_f, dscale_f, dgate_f], axis=-1)[:, 0]
    ln_grads = jnp.concatenate([dg1, db1, dg2, db2], axis=1)
    return dict(loss=loss_p[:, 0, 0], grad_x=gx.reshape(NB, T, D), g_win=g_win, g_wa=g_wa, g_wb=g_wb, g_wo=g_wo,
                g_wgu=g_wgu, g_wd=g_wd, dmod=dmod, ln_grads=ln_grads, dsink=dsink[0, :A_Q_HEADS])


def _coords():
    return lax.axis_index("x"), lax.axis_index("y"), lax.axis_index("c")


def _allgather_small(blk, *, name):
    m_per, n = blk.shape

    def body(x_ref, out_ref, send_sems, recv_sems, local_sem):
        x, y, c = _coords()
        me, sibling = (x, y, c), (x, y, 1 - c)
        chips = [(1 - x, y), (x, 1 - y), (1 - x, 1 - y)]

        def rows(px, py, pc):
            return out_ref.at[pl.ds((4 * px + 2 * py + pc) * m_per, m_per), :]

        def copy(k, block, to, src=None):
            return pltpu.make_async_remote_copy(
                src_ref=rows(*block) if src is None else src, dst_ref=rows(*block),
                send_sem=send_sems.at[k], recv_sem=recv_sems.at[k], device_id=to, device_id_type=MESH)

        mine = pltpu.make_async_copy(x_ref, rows(*me), local_sem)
        mine.start()
        first = [copy(0, me, sibling, src=x_ref)]
        first += [copy(1 + j, me, (*chip, c), src=x_ref) for j, chip in enumerate(chips)]
        for cp in first:
            cp.start()
        passed = [copy(4 + j, (*chip, c), sibling) for j, chip in enumerate(chips)]
        for j, chip in enumerate(chips):
            copy(1 + j, (*chip, c), me).wait_recv()
            passed[j].start()
        copy(0, sibling, me).wait_recv()
        for j, chip in enumerate(chips):
            copy(4 + j, (*chip, 1 - c), me).wait_recv()
        for cp in first + passed:
            cp.wait_send()
        mine.wait()

    return _pcall(
        body, name=name, out_shape=jax.ShapeDtypeStruct((8 * m_per, n), blk.dtype),
        in_specs=[pl.BlockSpec(memory_space=pltpu.VMEM)], out_specs=pl.BlockSpec(memory_space=pltpu.VMEM),
        scratch_shapes=[pltpu.SemaphoreType.DMA((7,)), pltpu.SemaphoreType.DMA((7,)), pltpu.SemaphoreType.DMA],
        compiler_params=pltpu.CompilerParams(vmem_limit_bytes=VMEM_LIMIT_BYTES),
    )(blk)


def _exchange(srcs, dsts, plan, *, name, dst_inits=None):
    na = len(dsts)
    nrem = len(plan(0, 0, 0))

    def body(*refs):
        refs = list(refs)
        src_refs = [refs.pop(0) for _ in range(na)] if srcs is not None else None
        if dst_inits is not None:
            del refs[:na]
        dst_refs, (send_sems, recv_sems) = refs[:na], refs[na:]
        start, wait = _copies(dst_refs if src_refs is None else src_refs, dst_refs, send_sems, recv_sems, plan)
        start()
        wait()

    hbm = pl.BlockSpec(memory_space=pl.ANY)
    ins = (list(srcs) if srcs is not None else []) + (list(dst_inits) if dst_inits is not None else [])
    base = na if srcs is not None else 0
    aliases = {base + a: a for a in range(na)} if dst_inits is not None else {}
    return _pcall(
        body, name=name, out_shape=list(dsts), in_specs=[hbm] * len(ins), out_specs=[hbm] * na,
        input_output_aliases=aliases,
        scratch_shapes=[pltpu.SemaphoreType.DMA((na * nrem,)), pltpu.SemaphoreType.DMA((na * nrem,))],
    )(*ins)


def _other_chips(x, y):
    return [(1 - x, y), (x, 1 - y), (1 - x, 1 - y)]


def _round(ride, carrier, name):
    if carrier is not None:
        _RIDES.setdefault(carrier, []).append(ride)
        return
    srcs = ride.srcs() if callable(ride.srcs) else ride.srcs
    inits = ride.dst_inits() if callable(ride.dst_inits) else ride.dst_inits
    ride.out = list(_exchange(srcs, ride.dsts, ride.plan, name=name, dst_inits=inits))


class _Gather:
    def __init__(self, shards, chip, tag, carriers=(None, None)):
        def plan_ici(x, y, c):
            k = 2 * x + y
            return [((c,), (k, c), (2 * px + py, c), (px, py, c)) for px, py in _other_chips(x, y)]

        def plan_d2d(x, y, c):
            return [((2 * px + py, c), (2 * px + py, c), (2 * px + py, 1 - c), (x, y, 1 - c))
                    for px, py in _other_chips(x, y)]

        def plan_near(x, y, c):
            k = 2 * x + y
            return [((c,), (k, c), (2 * px + py, c), (px, py, c)) for px, py in ((1 - x, y), (x, 1 - y))]

        def plan_far(x, y, c):
            kx, ky, kd = 2 * (1 - x) + y, 2 * x + (1 - y), 2 * (1 - x) + (1 - y)
            hp = shards[0].shape[1] // 2
            top, bottom = pl.ds(0, hp), pl.ds(hp, hp)
            return [((kx, c, top), (kx, c, top), (kd, c, top), (x, 1 - y, c)),
                    ((ky, c, bottom), (ky, c, bottom), (kd, c, bottom), (1 - x, y, c))]

        self.shards, self.chip = shards, chip
        dsts = [jax.ShapeDtypeStruct((4,) + s.shape, s.dtype) for s in shards]
        if len(carriers) == 3:
            near = _Ride(shards, dsts, plan_near)
            ici = _Ride(None, dsts, plan_far, dst_inits=lambda: near.out)
            _round(near, carriers[0], f"gather_{tag}_near")
            _round(ici, carriers[1], f"gather_{tag}_far")
        else:
            ici = _Ride(shards, dsts, plan_ici)
            _round(ici, carriers[0], f"gather_{tag}_ici")
        self.d2d = _Ride(None, dsts, plan_d2d, dst_inits=lambda: ici.out)
        _round(self.d2d, carriers[-1], f"gather_{tag}_d2d")

    def result(self):
        full = [lax.dynamic_update_index_in_dim(f, s, self.chip, 0) for f, s in zip(self.d2d.out, self.shards)]
        return [f.reshape((4, 2 * f.shape[2], f.shape[3])) for f in full]


def _index_operand(i):
    return jnp.reshape(i, (1,)).astype(jnp.int32)


def _add_pairs(g, f, ci, *, name):
    s, _, hr, wd = g.shape
    tr = _pick(hr, 600, 16)

    def body(c_ref, a_ref, b_ref, o_ref):
        o_ref[...] = (a_ref[...].astype(F32) + b_ref[...].astype(F32)).astype(BF16)

    spec = pl.BlockSpec((1, tr, wd), lambda j, i, c: (j, i, 0))
    grid_spec = pltpu.PrefetchScalarGridSpec(
        num_scalar_prefetch=1, grid=(s, hr // tr),
        in_specs=[pl.BlockSpec((1, None, tr, wd), lambda j, i, c: (j, c[0], i, 0)), spec], out_specs=spec)
    return _pcall(body, name=name, grid_spec=grid_spec, out_shape=jax.ShapeDtypeStruct(f.shape, BF16),
                  compiler_params=_params(("parallel", "parallel")))(_index_operand(ci), g, f)


def _sum_chips(landed, pairs, chip, ci, *, name):
    s, hr, wd = landed.shape
    tr = _pick(hr, 600, 16)

    def body(k_ref, l_ref, p_ref, o_ref):
        acc = None
        for k in range(s):
            part = jnp.where(k_ref[0] == k, p_ref[k], l_ref[k]).astype(F32)
            acc = part if acc is None else acc + part
        o_ref[...] = acc

    spec = pl.BlockSpec((s, tr, wd), lambda i, k: (0, i, 0))
    grid_spec = pltpu.PrefetchScalarGridSpec(
        num_scalar_prefetch=1, grid=(hr // tr,), in_specs=[spec, spec],
        out_specs=pl.BlockSpec((None, tr, wd), lambda i, k: (k[1], i, 0)))
    where = jnp.stack([chip, ci]).astype(jnp.int32)
    return _pcall(body, name=name, grid_spec=grid_spec, out_shape=jax.ShapeDtypeStruct((2, hr, wd), F32),
                  compiler_params=_params(("parallel",)))(where, landed, pairs)


class _ReduceScatter:
    def __init__(self, gs, chip, ci, tag):
        self.gs, self.chip, self.ci, self.tag = gs, chip, ci, tag
        self.half_t = [jax.ShapeDtypeStruct((g.shape[0],) + g.shape[2:], BF16) for g in gs]

    def pair(self, carrier=None):
        plan = lambda x, y, c: [((slice(None), 1 - c), (), (), (x, y, 1 - c))]
        self.r1 = _Ride(self.gs, self.half_t, plan)
        _round(self.r1, carrier, f"reduce_{self.tag}_pair")

    def chips(self, carrier=None):
        def plan(x, y, c):
            k = 2 * x + y
            return [((2 * px + py,), (k,), (2 * px + py,), (px, py, c)) for px, py in _other_chips(x, y)]

        self.pairs = [_add_pairs(g, f, self.ci, name=f"reduce_{self.tag}_pair_add{n}")
                      for n, (g, f) in enumerate(zip(self.gs, self.r1.out))]
        self.r2 = _Ride(self.pairs, self.half_t, plan)
        _round(self.r2, carrier, f"reduce_{self.tag}_chips")

    def halves(self, carrier=None):
        plan = lambda x, y, c: [((c,), (c,), (1 - c,), (x, y, 1 - c))]
        mine = [_sum_chips(l, p, self.chip, self.ci, name=f"reduce_{self.tag}_chip_sum{n}")
                for n, (l, p) in enumerate(zip(self.r2.out, self.pairs))]
        self.r3 = _Ride(None, [jax.ShapeDtypeStruct(m.shape, F32) for m in mine], plan, dst_inits=mine)
        _round(self.r3, carrier, f"reduce_{self.tag}_halves")

    def result(self):
        return [b.reshape(2 * b.shape[1], b.shape[2]) for b in self.r3.out]


def _ada_fwd(c_all, w_sh, b_sh, *, name):
    nb, d = c_all.shape
    wcols = w_sh.shape[1]
    tn = _pick(wcols, 512)

    def body(c_ref, w_ref, b_ref, o_ref, a_ref):
        cv = c_ref[...]
        act = cv * jax.nn.sigmoid(cv)
        a_ref[...] = act
        o_ref[...] = jnp.dot(act.astype(BF16), w_ref[...].astype(BF16), preferred_element_type=F32) + b_ref[...]

    return _pcall(
        body, name=name, grid=(wcols // tn,),
        in_specs=[pl.BlockSpec((nb, d), lambda j: (0, 0)), pl.BlockSpec((d, tn), lambda j: (0, j)),
                  pl.BlockSpec((1, tn), lambda j: (0, j))],
        out_specs=[pl.BlockSpec((nb, tn), lambda j: (0, j)), pl.BlockSpec((nb, d), lambda j: (0, 0))],
        out_shape=[jax.ShapeDtypeStruct((nb, wcols), F32), jax.ShapeDtypeStruct((nb, d), F32)],
        compiler_params=_params(("arbitrary",)))(c_all, w_sh, b_sh)


def _sum_devices(g, *, name):
    nd, m, w = g.shape

    def body(g_ref, o_ref):
        acc = g_ref[0]
        for k in range(1, nd):
            acc = acc + g_ref[k]
        o_ref[...] = acc

    return _pcall(body, name=name, out_shape=jax.ShapeDtypeStruct((m, w), F32),
                  compiler_params=pltpu.CompilerParams(vmem_limit_bytes=VMEM_LIMIT_BYTES))(g)


def _adamw(w, g, m, v, *, name):
    rows, cols = w.shape[-2:]
    tr = _pick(rows, max(8, (1 << 18) // cols), 8)
    c1 = 1.0 / (1.0 - ADAM_B1 ** ADAM_STEP)
    c2 = 1.0 / (1.0 - ADAM_B2 ** ADAM_STEP)

    def body(w_ref, g_ref, m_ref, v_ref, d_ref, nm_ref, nv_ref):
        gv = g_ref[...]
        nm = ADAM_B1 * m_ref[...] + (1.0 - ADAM_B1) * gv
        nv = ADAM_B2 * v_ref[...] + (1.0 - ADAM_B2) * (gv * gv)
        d_ref[...] = -ADAM_LR * ((nm * c1) / (jnp.sqrt(nv * c2) + ADAM_EPS) + ADAM_WD * w_ref[...])
        nm_ref[...] = nm
        nv_ref[...] = nv

    gspec = pl.BlockSpec((tr, cols), lambda i: (i, 0))
    spec = pl.BlockSpec((None, tr, cols), lambda i: (0, i, 0)) if w.ndim == 3 else gspec
    shp = jax.ShapeDtypeStruct(w.shape, F32)
    return _pcall(body, name=name, grid=(rows // tr,), in_specs=[spec, gspec, spec, spec], out_specs=[spec] * 3,
                  out_shape=[shp] * 3, compiler_params=_params(("parallel",)))(w, g, m, v)


def _permute_in_rows(wt):
    ngrp = len(B_PATTERNS)
    qb, kb, vb = (wt[A_W + n * QB_W:A_W + (n + 1) * QB_W] for n in range(3))
    parts = [wt[:A_W], jnp.zeros((VAR_W - A_W, wt.shape[1]), wt.dtype)]
    for g in range(ngrp):
        parts += [t[g * GB_W:(g + 1) * GB_W] for t in (qb, kb, vb)]
    return jnp.concatenate(parts + [wt[A_W + 3 * QB_W:]], axis=0)


def _unpermute_in_grads(pieces):
    ga, groups, gg = pieces[0], pieces[1:-1], pieces[-1]
    rows = [ga[:A_W]]
    for n in range(3):
        rows += [gp[n * GB_W:(n + 1) * GB_W] for gp in groups]
    return jnp.concatenate(rows + [gg], axis=0)


def kernel(x, c, positions, w_ada, b_ada, w_in, sinks, w_branch_a, w_branch_b, w_o, ln1_g, ln1_b, w_gate_up, w_down, ln2_g, ln2_b, loss_target, m_w_ada, m_b_ada, m_w_in, m_sinks, m_w_branch_a, m_w_branch_b, m_w_o, m_ln1_g, m_ln1_b, m_w_gate_up, m_w_down, m_ln2_g, m_ln2_b, v_w_ada, v_b_ada, v_w_in, v_sinks, v_w_branch_a, v_w_branch_b, v_w_o, v_ln1_g, v_ln1_b, v_w_gate_up, v_w_down, v_ln2_g, v_ln2_b):
    xi, yi, ci = _coords()
    chip = 2 * xi + yi
    dev = 4 * xi + 2 * yi + ci
    NB, T, D = x.shape
    nchip, ndev = 4, 8
    ada_cols = w_ada.shape[2]

    ra, ro, rd = w_branch_a.shape[1], w_o.shape[1], w_down.shape[1]
    rowsh = jnp.concatenate([w_branch_a[0], w_o[0], w_down[0]], axis=0)
    halves = lambda a: a.reshape(a.shape[:-2] + (2, a.shape[-2] // 2, a.shape[-1]))
    tr = lambda a: jnp.swapaxes(a, -1, -2)
    shards = [halves(w.astype(BF16)) for w in (tr(w_in[0]), rowsh, w_branch_b[0], w_gate_up[0])]
    gin = _Gather(shards[:1], chip, "w_in", carriers=("gather_c", "ada_fwd", "gather_mod"))

    c_blk = jnp.zeros((8, D), F32).at[:NB].set(c)
    c_all = _allgather_small(c_blk, name="gather_c").reshape(ndev, 8, D)[:, :NB].reshape(ndev * NB, D)
    b_sh = lax.dynamic_slice(b_ada, (0, chip * ada_cols), (1, ada_cols))
    mod_part, c_act = _ada_fwd(c_all, w_ada[0], b_sh, name="ada_fwd")
    mod_g = _allgather_small(mod_part, name="gather_mod").reshape(nchip, 2, ndev * NB, ada_cols)[:, 0]
    mod_all = jnp.transpose(mod_g, (1, 0, 2)).reshape(ndev * NB, nchip * ada_cols)
    mod = lax.dynamic_slice(mod_all, (NB * dev, 0), (NB, nchip * ada_cols))

    (g_in,) = gin.result()
    w_in_f = _permute_in_rows(g_in.reshape(nchip * g_in.shape[1], D))
    mix = _Gather(shards[1:3], chip, "w_mix", carriers=("inproj_qkv", "attn_a_fwd"))
    ffn = _Gather(shards[3:], chip, "w_ffn", carriers=("attn_a_fwd", "attn_b_fwd", "mix_out"))

    def rest_weights():
        g_rows, w_b_f = mix.result()
        return (g_rows[:, :ra].reshape(nchip * ra, D), w_b_f, g_rows[:, ra:ra + ro].reshape(nchip * ro, D),
                lambda: ffn.result()[0], g_rows[:, ra + ro:].reshape(nchip * rd, D))

    red = {}

    def hook(event, **g):
        if event == "ffn_grads":
            red["ffn"] = _ReduceScatter([halves(g["g_wgu"])], chip, ci, "ffn")
            red["ffn"].pair(carrier="out_proj_dw")
        elif event == "rest_grads":
            gr_rows = jnp.concatenate([g["g_wa"].reshape(nchip, ra, D), g["g_wo"].reshape(nchip, ro, D),
                                       g["g_wd"].reshape(nchip, rd, D)], axis=1)
            red["mix"] = _ReduceScatter([halves(gr_rows), halves(g["g_wb"])], chip, ci, "mix")
            red["ffn"].chips(carrier="attn_a_bwd")
            red["mix"].pair(carrier="attn_a_bwd")
        elif event == "attn_a_bwd_done":
            red["ffn"].halves(carrier="attn_b0_bwd")
            red["mix"].chips(carrier="attn_b0_bwd")
        elif event == "attn_b0_bwd_done":
            red["mix"].halves(carrier="attn_b1_bwd")
        elif event == "win_grads":
            gr_in = _unpermute_in_grads(g["g_win"])
            red["w_in"] = _ReduceScatter([halves(gr_in.reshape(nchip, gr_in.shape[0] // nchip, D))], chip, ci, "w_in")
            red["w_in"].pair(carrier="inproj_dx0")
        elif event == "inproj_dx0_done":
            red["w_in"].chips(carrier="inproj_dx1")
        elif event == "inproj_dx1_done":
            red["w_in"].halves(carrier="gather_small")

    res = _local_step(x, mod, positions, w_in_f, rest_weights, sinks[0], ln1_g, ln1_b, ln2_g, ln2_b, loss_target, hook)
    (g_rows_red, g_w_b), (g_w_gu,) = red["mix"].result(), red["ffn"].result()
    g_w_a, g_w_o, g_w_d = g_rows_red[:ra], g_rows_red[ra:ra + ro], g_rows_red[ra + ro:]

    small_rows = 24
    misc = jnp.zeros((1, D), F32).at[0, :A_Q_HEADS].set(res["dsink"]).at[0, A_Q_HEADS].set(jnp.sum(res["loss"]))
    small = jnp.concatenate([res["dmod"].reshape(NB * 6, D), jnp.sum(res["ln_grads"], axis=0), misc,
                             jnp.zeros((small_rows - NB * 6 - 5, D), F32)], axis=0)
    small_all = _allgather_small(small, name="gather_small").reshape(ndev, small_rows, D)
    (g_w_in,) = red["w_in"].result()
    dmod_all = small_all[:, :NB * 6].reshape(ndev * NB, 6 * D)
    sums = _sum_devices(small_all, name="sum_small")
    g_b_ada = (sums[0:6] + sums[6:12]).reshape(1, 6 * D)
    g_ln1_g, g_ln1_b, g_ln2_g, g_ln2_b = (sums[12 + n][None] for n in range(4))
    g_sinks = sums[16, :A_Q_HEADS][None]
    loss = sums[16, A_Q_HEADS]
    dmod_sh = lax.dynamic_slice(dmod_all, (0, chip * ada_cols), (ndev * NB, ada_cols))
    g_w_ada = _mm(c_act, dmod_sh, ta=True, name="ada_dw")

    names = ["w_ada", "b_ada", "w_in", "sinks", "w_branch_a", "w_branch_b", "w_o", "ln1_g", "ln1_b",
             "w_gate_up", "w_down", "ln2_g", "ln2_b"]
    ws = [w_ada, b_ada, w_in, sinks, w_branch_a, w_branch_b, w_o, ln1_g, ln1_b, w_gate_up, w_down, ln2_g, ln2_b]
    ms = [m_w_ada, m_b_ada, m_w_in, m_sinks, m_w_branch_a, m_w_branch_b, m_w_o, m_ln1_g, m_ln1_b, m_w_gate_up,
          m_w_down, m_ln2_g, m_ln2_b]
    vs = [v_w_ada, v_b_ada, v_w_in, v_sinks, v_w_branch_a, v_w_branch_b, v_w_o, v_ln1_g, v_ln1_b, v_w_gate_up,
          v_w_down, v_ln2_g, v_ln2_b]
    gs = [g_w_ada, g_b_ada, g_w_in, g_sinks, g_w_a, g_w_b, g_w_o, g_ln1_g, g_ln1_b, g_w_gu, g_w_d, g_ln2_g, g_ln2_b]
    grads, deltas, new_ms, new_vs = [], [], [], []
    for name, w, g, m, v in zip(names, ws, gs, ms, vs):
        flip = tr if name == "w_in" else (lambda a: a)
        w, m, v = flip(w), flip(m), flip(v)
        g2 = g.reshape(w.shape[-2:])
        d, nm, nv = _adamw(w, g2, m, v, name="adamw_" + name)
        grads.append(flip(g2.reshape(w.shape)))
        deltas.append(flip(d))
        new_ms.append(flip(nm))
        new_vs.append(flip(nv))
    return (loss, res["grad_x"], *grads, *deltas, *new_ms, *new_vs)
```

```python
import functools

import jax
import jax.numpy as jnp
from jax import lax
from jax.experimental import pallas as pl
from jax.experimental.pallas import tpu as pltpu

F32 = jnp.float32
BF16 = jnp.bfloat16
MESH = pl.DeviceIdType.MESH

HEAD_DIM = 64
LANES = 128
PAIR_W = 2 * HEAD_DIM
BLOCK = 128
A_Q_HEADS = 16
A_KV_HEADS = 2
A_WINDOW = 128
B_PATTERNS = ((128, 1), (512, 4), (2048, 16))
B_GROUP_HEADS = 8
QA_W = A_Q_HEADS * HEAD_DIM
KA_W = A_KV_HEADS * HEAD_DIM
GB_W = B_GROUP_HEADS * HEAD_DIM
QB_W = GB_W * len(B_PATTERNS)
A_W = QA_W + 2 * KA_W
VAR_W = 3 * GB_W
N_VAR = 1 + len(B_PATTERNS)
VAR_DIL = (1,) + tuple(r for _, r in B_PATTERNS)
A_BWD_SPLIT = 4
QKV_P = N_VAR * VAR_W
ROPE_THETA = 10000.0
LN_EPS = 1e-5
NEG_INF = -1e30
DEPTH = 1
ALPHA = (2 * DEPTH) ** 0.25
SCALE = HEAD_DIM ** -0.5

ADAM_LR, ADAM_B1, ADAM_B2, ADAM_EPS, ADAM_WD, ADAM_STEP = 0.001, 0.9, 0.999, 1e-08, 0.01, 10

VMEM_LIMIT_BYTES = 56 * 1024 * 1024
MM_TILE_BYTES = 36 * 1024 * 1024
MM_WHOLE_K = 4096


def _params(sem=None):
    return pltpu.CompilerParams(dimension_semantics=sem, vmem_limit_bytes=VMEM_LIMIT_BYTES)


_RIDES = {}


def _pcall(body, *, name, **kw):
    rides = _RIDES.pop(name, None)
    if rides is None:
        return pl.pallas_call(body, name=name, **kw)
    return _riding_call(body, rides, name=name, **kw)


def _copies(src_refs, dst_refs, send_sems, recv_sems, plan):
    x, y, c = lax.axis_index("x"), lax.axis_index("y"), lax.axis_index("c")
    remote = plan(x, y, c)
    nrem = len(remote)
    at = lambda ref, idx: ref.at[idx] if idx else ref

    def copy(a, n, landing):
        si, di, ri, peer = remote[n]
        return pltpu.make_async_remote_copy(
            src_ref=at(src_refs[a], si), dst_ref=at(dst_refs[a], ri if landing else di),
            send_sem=send_sems.at[a * nrem + n], recv_sem=recv_sems.at[a * nrem + n],
            device_id=peer, device_id_type=MESH)

    order = [(a, n) for a in range(len(dst_refs)) for n in range(nrem)]

    def start():
        for a, n in order:
            copy(a, n, False).start()

    def wait():
        for a, n in order:
            copy(a, n, True).wait_recv()
        for a, n in order:
            copy(a, n, False).wait_send()

    return start, wait


class _Ride:
    def __init__(self, srcs, dsts, plan, dst_inits=None):
        self.srcs, self.dsts, self.plan, self.dst_inits, self.out = srcs, dsts, plan, dst_inits, None


def _riding_call(body, rides, *, name, in_specs, out_specs, out_shape, grid=(), scratch_shapes=(), **kw):
    single = not isinstance(out_specs, (list, tuple))
    out_specs = [out_specs] if single else list(out_specs)
    out_shape = [out_shape] if single else list(out_shape)
    n_in, n_out, n_scr = len(in_specs), len(out_specs), len(scratch_shapes)
    xin, xdsts, sems, aliases, layout = [], [], [], {}, []
    for ride in rides:
        srcs = ride.srcs() if callable(ride.srcs) else ride.srcs
        inits = ride.dst_inits() if callable(ride.dst_inits) else ride.dst_inits
        na, nrem = len(ride.dsts), len(ride.plan(0, 0, 0))
        src_at = len(xin) if srcs is not None else None
        xin += list(srcs) if srcs is not None else []
        if inits is not None:
            aliases.update({n_in + len(xin) + a: n_out + len(xdsts) + a for a in range(na)})
            xin += list(inits)
        layout.append((src_at, len(xdsts), na))
        xdsts += list(ride.dsts)
        sems += [pltpu.SemaphoreType.DMA((na * nrem,)), pltpu.SemaphoreType.DMA((na * nrem,))]

    def wrapped(*refs):
        ins, xins = refs[:n_in], refs[n_in:n_in + len(xin)]
        outs = refs[n_in + len(xin):n_in + len(xin) + n_out]
        xouts = refs[n_in + len(xin) + n_out:n_in + len(xin) + n_out + len(xdsts)]
        scr = refs[n_in + len(xin) + n_out + len(xdsts):]
        rounds = []
        for k, (ride, (src_at, dst_at, na)) in enumerate(zip(rides, layout)):
            dsts = xouts[dst_at:dst_at + na]
            srcs = dsts if src_at is None else xins[src_at:src_at + na]
            rounds.append(_copies(srcs, dsts, scr[n_scr + 2 * k], scr[n_scr + 2 * k + 1], ride.plan))
        ids = [pl.program_id(a) for a in range(len(grid))]
        first = functools.reduce(jnp.logical_and, [i == 0 for i in ids], True)
        last = functools.reduce(jnp.logical_and, [i == g - 1 for i, g in zip(ids, grid)], True)

        def start_all():
            for start, _ in rounds:
                start()

        def wait_all():
            for _, wait in rounds:
                wait()

        start_all() if not grid else pl.when(first)(start_all)
        body(*ins, *outs, *scr[:n_scr])
        wait_all() if not grid else pl.when(last)(wait_all)

    hbm = pl.BlockSpec(memory_space=pl.ANY)
    gridkw = dict(grid=grid) if grid else {}

    def run(*args):
        res = pl.pallas_call(
            wrapped, name=name, in_specs=list(in_specs) + [hbm] * len(xin),
            out_specs=out_specs + [hbm] * len(xdsts), out_shape=out_shape + xdsts,
            scratch_shapes=list(scratch_shapes) + sems, input_output_aliases=aliases,
            compiler_params=_params(("arbitrary",) * len(grid) if grid else None), **gridkw,
        )(*args, *xin)
        for ride, (_, dst_at, na) in zip(rides, layout):
            ride.out = list(res[n_out + dst_at:n_out + dst_at + na])
        return res[0] if single else list(res[:n_out])

    return run


def _pick(n, target, quantum=128):
    t = (min(target, n) // quantum) * quantum
    while t >= quantum:
        if n % t == 0:
            return t
        t -= quantum
    return n


def _mm(a, b, *, name, ta=False, tb=False, b3=False, out3=0, out_dtype=F32, add=None, tm=1024, tn=1536, tk=1536):
    if ta:
        K, M = a.shape
    else:
        M, K = a.shape
    if b3 and tb:
        Nn, K2, tk = b.shape[1], b.shape[0] * b.shape[2], b.shape[2]
    elif b3:
        K2, Nn, tn = b.shape[1], b.shape[0] * b.shape[2], b.shape[2]
    elif tb:
        Nn, K2 = b.shape
    else:
        K2, Nn = b.shape
    assert K == K2, (a.shape, b.shape)
    if out3:
        tn = Nn // out3
    tm, tn, tk = _pick(M, tm), _pick(Nn, tn), _pick(K, tk)
    if not (b3 and tb) and K <= MM_WHOLE_K:
        tk = K
        fits = lambda: 4 * tk * (tm + tn) + 8 * tm * tn * (2 if add is not None else 1) <= MM_TILE_BYTES
        while not fits():
            if (tm >= tn or b3 or out3) and tm > 256:
                tm = _pick(M, tm - 128)
            elif not (b3 or out3) and tn > 256:
                tn = _pick(Nn, tn - 128)
            else:
                break
    nk = K // tk
    j_outer = K * Nn + (Nn // tn) * M * K < M * K + (M // tm) * K * Nn
    dn = (((0 if ta else 1,), (1 if tb else 0,)), ((), ()))

    def body(*refs):
        refs = list(refs)
        a_ref, b_ref = refs[:2]
        add_ref = refs[2] if add is not None else None
        o_ref = refs[3] if add is not None else refs[2]
        part = lax.dot_general(a_ref[...].astype(BF16), b_ref[...].astype(BF16), dn, preferred_element_type=F32)

        def finish(r):
            if add is not None:
                r = r + add_ref[...]
            o_ref[...] = r.astype(out_dtype)

        if nk == 1:
            finish(part)
            return
        acc = refs[-1]
        k = pl.program_id(2)

        @pl.when(k == 0)
        def _():
            acc[...] = part

        @pl.when(k > 0)
        def _():
            acc[...] += part

        @pl.when(k == nk - 1)
        def _():
            finish(acc[...])

    def spec(shape, index):
        return pl.BlockSpec(shape, (lambda j, i, k: index(i, j, k)) if j_outer else index)

    a_spec = spec((tk, tm), lambda i, j, k: (k, i)) if ta else spec((tm, tk), lambda i, j, k: (i, k))
    if b3 and tb:
        b_spec = spec((None, tn, tk), lambda i, j, k: (k, j, 0))
    elif b3:
        b_spec = spec((None, tk, tn), lambda i, j, k: (j, k, 0))
    elif tb:
        b_spec = spec((tn, tk), lambda i, j, k: (j, k))
    else:
        b_spec = spec((tk, tn), lambda i, j, k: (k, j))
    if out3:
        o_spec = spec((None, tm, tn), lambda i, j, k: (j, i, 0))
    else:
        o_spec = spec((tm, tn), lambda i, j, k: (i, j))
    ins, specs = [a, b], [a_spec, b_spec]
    if add is not None:
        ins.append(add)
        specs.append(o_spec)
    grid = (Nn // tn, M // tm, nk) if j_outer else (M // tm, Nn // tn, nk)
    return _pcall(
        body, name=name, grid=grid, in_specs=specs, out_specs=o_spec,
        out_shape=jax.ShapeDtypeStruct((out3, M, tn) if out3 else (M, Nn), out_dtype),
        scratch_shapes=[pltpu.VMEM((tm, tn), F32)] if nk > 1 else [],
        compiler_params=_params(("parallel", "parallel", "arbitrary")),
    )(*ins)


def _mm_multi(a_list, b_list, *, name, M, T=None, add=None, out_dtype=F32, tm=512, post=None):
    tm = _pick(T or M, tm)
    ns = len(a_list)
    dils = [a[1] if isinstance(a, tuple) else 0 for a in a_list]
    a_arrs = [a[0] if isinstance(a, tuple) else a for a in a_list]
    widths = [a.shape[-1] // max(r, 1) for a, r in zip(a_arrs, dils)]
    b_arrs, b_specs = [], []
    for b in b_list:
        arr, shp, idx = b if isinstance(b, tuple) else (b, b.shape, (0, 0))
        b_arrs.append(arr)
        b_specs.append(pl.BlockSpec(shp, lambda i, idx=idx: idx))
    Nn = b_specs[0].block_shape[1]
    dn = (((1,), (0,)), ((), ()))
    nmm = 2 * ns + (1 if add is not None else 0)
    p_arrs, p_in_specs, p_out_specs, p_out_shape, p_fn = post or ([], [], None, None, None)
    nin = nmm + len(p_arrs)
    nout = len(p_out_specs) if post else 1

    def body(*refs):
        a_refs, b_refs, scr = refs[:ns], refs[ns:2 * ns], list(refs[nin + nout:])
        acc = None
        for a_ref, b_ref, r in zip(a_refs, b_refs, dils):
            av = _from_view(a_ref, scr.pop(0), r) if r > 1 else a_ref[...]
            part = lax.dot_general(av.astype(BF16), b_ref[...], dn, preferred_element_type=F32)
            acc = part if acc is None else acc + part
        if add is not None:
            acc = acc + refs[2 * ns][...]
        if post:
            p_fn(acc, refs[nmm:nin], refs[nin:nin + nout])
        else:
            refs[nin][...] = acc.astype(out_dtype)

    tpe = (T or M) // tm
    a_specs = [pl.BlockSpec((None, tm // r, r * w), lambda i: (i // tpe, i % tpe, 0)) if r
               else pl.BlockSpec((tm, w), lambda i: (i, 0)) for r, w in zip(dils, widths)]
    o_spec = pl.BlockSpec((tm, Nn), lambda i: (i, 0))
    specs = a_specs + b_specs
    ins = a_arrs + b_arrs
    if add is not None:
        specs.append(o_spec)
        ins.append(add)
    scratch = [pltpu.VMEM((w // LANES, tm, LANES), F32) for r, w in zip(dils, widths) if r > 1]
    return _pcall(body, name=name, grid=(M // tm,), in_specs=specs + list(p_in_specs),
                  out_specs=list(p_out_specs) if post else o_spec, scratch_shapes=scratch,
                  out_shape=list(p_out_shape) if post else jax.ShapeDtypeStruct((M, Nn), out_dtype),
                  compiler_params=_params(("arbitrary",) if post else ("parallel",)))(*ins, *p_arrs)


def _dw_view(d3, u, r, *, name, tk=1024):
    NB, tsub, rw = d3.shape
    W, T, D = rw // r, tsub * r, u.shape[1]
    tk = _pick(T, tk)
    tpe, nk = T // tk, NB * T // tk

    def body(d_ref, u_ref, o_ref, acc, scr):
        k = pl.program_id(0)
        dv = _from_view(d_ref, scr, r).astype(BF16)
        part = lax.dot_general(dv, u_ref[...], _TN, preferred_element_type=F32)

        @pl.when(k == 0)
        def _():
            acc[...] = part

        @pl.when(k > 0)
        def _():
            acc[...] += part

        @pl.when(k == nk - 1)
        def _():
            o_ref[...] = acc[...].astype(o_ref.dtype)

    return _pcall(
        body, name=name, grid=(nk,),
        in_specs=[pl.BlockSpec((None, tk // r, rw), lambda k: (k // tpe, k % tpe, 0)), pl.BlockSpec((tk, D), lambda k: (k, 0))],
        out_specs=pl.BlockSpec((W, D), lambda k: (0, 0)), out_shape=jax.ShapeDtypeStruct((W, D), BF16),
        scratch_shapes=[pltpu.VMEM((W, D), F32), pltpu.VMEM((W // LANES, tk, LANES), F32)],
        compiler_params=_params(("arbitrary",)))(d3, u)


def _lane(shape):
    return lax.broadcasted_iota(jnp.int32, shape, len(shape) - 1)


def _rot_half(v):
    w = v.shape[-1]
    first = (_lane(v.shape) % HEAD_DIM) < (HEAD_DIM // 2)
    return jnp.where(first, pltpu.roll(v, w - HEAD_DIM // 2, v.ndim - 1), pltpu.roll(v, HEAD_DIM // 2, v.ndim - 1))


def _widen(t, w):
    return t if w == t.shape[-1] else jnp.concatenate([t] * (w // t.shape[-1]), axis=-1)


def _unrope(v, cos, sins):
    w = v.shape[-1]
    return v * _widen(cos, w) - _rot_half(v) * _widen(sins, w)


def _rope_tables(positions):
    half = HEAD_DIM // 2
    inv = ROPE_THETA ** (-jnp.arange(half, dtype=F32) / half)
    ang = positions.astype(F32)[..., None] * inv
    cos, sin = jnp.cos(ang), jnp.sin(ang)
    cosf = jnp.concatenate([cos, cos, cos, cos], axis=-1)
    sins = jnp.concatenate([-sin, sin, -sin, sin], axis=-1)
    n = positions.shape[0] * positions.shape[1]
    return cosf.reshape(n, PAIR_W), sins.reshape(n, PAIR_W)


def _inproj(x2, scale, shift, w, cosf, sins, flags, *, T, name):
    N, D = x2.shape
    tm, tn = _pick(T, 512), VAR_W
    tpe = T // tm

    def body(x_ref, sc_ref, sh_ref, w_ref, c_ref, s_ref, f_ref, *outs):
        o_refs, u_ref = outs[:N_VAR], outs[N_VAR]
        j = pl.program_id(1)

        @pl.when(j == 0)
        def _():
            u_ref[...] = (x_ref[...] * (1.0 + sc_ref[0]) + sh_ref[0]).astype(BF16)

        acc = lax.dot_general(u_ref[...], w_ref[...], (((1,), (1,)), ((), ())), preferred_element_type=F32)
        fl = f_ref[...]
        ce = 1.0 + (_widen(c_ref[...], tn) - 1.0) * fl
        se = _widen(s_ref[...], tn) * fl
        res = acc * ce + _rot_half(acc) * se
        for v in range(N_VAR):
            @pl.when(j == v)
            def _(v=v):
                _to_view(res, o_refs[v], outs[N_VAR + 1], VAR_DIL[v])

    ex = pl.BlockSpec((1, 1, D), lambda i, j: (i // tpe, 0, 0))
    tab = pl.BlockSpec((tm, PAIR_W), lambda i, j: (i, 0))
    keep = lambda w_: pl.BlockSpec((tm, w_), lambda i, j: (i, 0))
    vspec = lambda r: pl.BlockSpec((None, tm // r, r * tn), lambda i, j: (i // tpe, i % tpe, 0))
    vshape = lambda r: jax.ShapeDtypeStruct((N // T, T // r, r * tn), BF16)
    return _pcall(
        body, name=name, grid=(N // tm, N_VAR),
        in_specs=[keep(D), ex, ex, pl.BlockSpec((tn, D), lambda i, j: (j, 0)), tab, tab,
                  pl.BlockSpec((1, tn), lambda i, j: (0, j))],
        out_specs=[vspec(r) for r in VAR_DIL] + [keep(D)],
        out_shape=[vshape(r) for r in VAR_DIL] + [jax.ShapeDtypeStruct((N, D), BF16)],
        scratch_shapes=[pltpu.VMEM((tn // LANES, tm, LANES), F32)],
        compiler_params=_params(("parallel", "arbitrary")),
    )(x2, scale, shift, w, cosf, sins, flags)


class _Geom:
    def __init__(self, g):
        if g is None:
            self.r, self.nq, self.n_back, self.sink = 1, A_Q_HEADS, A_WINDOW - 1, True
            self.qw, self.kw = QA_W, KA_W
            self.qidx = lambda j: 0
            self.kidx = lambda j: QA_W // KA_W
            self.vidx = lambda j: QA_W // KA_W + 1
        else:
            window, r = B_PATTERNS[g]
            self.r, self.nq, self.n_back, self.sink = r, B_GROUP_HEADS, window // r, False
            self.qw, self.kw = GB_W, GB_W
            self.qidx = lambda j: 3 * j
            self.kidx = lambda j: 3 * j + 1
            self.vidx = lambda j: 3 * j + 2
        self.ntile = self.qw // PAIR_W


def _stack_heads(t, scale=None):
    first = _lane(t.shape) < HEAD_DIM
    z = jnp.zeros_like(t)
    if scale is not None:
        t = t * jnp.asarray(scale, t.dtype)
    return jnp.concatenate([jnp.where(first, t, z), jnp.where(first, z, t)], axis=0)


def _unstack_heads(v2):
    return jnp.where(_lane((BLOCK, PAIR_W)) < HEAD_DIM, v2[:BLOCK], v2[BLOCK:])


def _dup_head(t, kh):
    tf = t.astype(F32)
    keep = (_lane(t.shape) < HEAD_DIM) if kh == 0 else (_lane(t.shape) >= HEAD_DIM)
    return jnp.where(keep, tf, pltpu.roll(tf, HEAD_DIM, 1)).astype(t.dtype)


def _fold_heads(t):
    return t + pltpu.roll(t, HEAD_DIM, 1)


def _band_mask(rows, i, n_back, single):
    nkeys = BLOCK if single else 2 * BLOCK
    qi = jnp.bitwise_and(lax.broadcasted_iota(jnp.int32, (rows, nkeys), 0), BLOCK - 1)
    ki = lax.broadcasted_iota(jnp.int32, (rows, nkeys), 1)
    if single:
        return qi >= ki
    dist = qi + BLOCK - ki
    return jnp.logical_and(jnp.logical_and(dist >= 0, dist <= n_back), jnp.logical_or(ki >= BLOCK, i > 0))


def _sink_slot(rows):
    qi = jnp.bitwise_and(lax.broadcasted_iota(jnp.int32, (rows, 2 * BLOCK), 0), BLOCK - 1)
    return qi == lax.broadcasted_iota(jnp.int32, (rows, 2 * BLOCK), 1)


def _sink_scores(rows, sinks):
    blk = lax.broadcasted_iota(jnp.int32, (rows, 2 * BLOCK), 0) // BLOCK
    out = jnp.full((rows, 2 * BLOCK), sinks[-1], F32)
    for b in range(len(sinks) - 2, -1, -1):
        out = jnp.where(blk == b, sinks[b], out)
    return out


def _softmax_parts(s, valid, sinks):
    s = jnp.where(valid, s, NEG_INF)
    if sinks is not None:
        slot = _sink_slot(s.shape[0])
        s = jnp.where(slot, _sink_scores(s.shape[0], sinks), s)
    m = jnp.max(s, axis=1, keepdims=True)
    p = jnp.exp(s - m)
    den = jnp.sum(p, axis=1, keepdims=True)
    if sinks is not None:
        p = jnp.where(slot, 0.0, p)
    return p, m, den


_NT = (((1,), (1,)), ((), ()))
_TN = (((0,), (0,)), ((), ()))


def _rows2(prev_ref, cur_ref, cs, single=False):
    if single:
        return cur_ref[0, :, cs]
    return jnp.concatenate([prev_ref[0, :, cs], cur_ref[0, :, cs]], axis=0)


def _sink_scalars(sink_ref, first, nblocks):
    return [sink_ref[first + b] for b in range(nblocks)]


def _tile(t):
    return slice(t * PAIR_W, (t + 1) * PAIR_W)


def _attn_fwd(qkv, sinks, g, *, NB, T, name):
    geo = _Geom(g)
    r, qw, kw, ntile = geo.r, geo.qw, geo.kw, geo.ntile
    tsub = T // r
    nblk = tsub // BLOCK
    qkv3 = qkv.reshape(NB, tsub, r * VAR_W)
    out_dtype = BF16 if g is None else F32
    tiles_per_kv = ntile // A_KV_HEADS

    single = nblk == 1

    def body(q_ref, kp_ref, kc_ref, vp_ref, vc_ref, sink_ref, o_ref, l_ref):
        i = pl.program_id(2)
        if geo.sink:
            kall, vall = _rows2(kp_ref, kc_ref, _tile(0)), _rows2(vp_ref, vc_ref, _tile(0))
            kdup = [_dup_head(kall, kh) for kh in range(A_KV_HEADS)]
            vdup = [_dup_head(vall, kh) for kh in range(A_KV_HEADS)]
            tiles = [[t] for t in range(ntile)]
            q2s = [_stack_heads(q_ref[0, :, _tile(t)], SCALE) for t in range(ntile)]
            kks = [kdup[t // tiles_per_kv] for t in range(ntile)]
            vvs = [vdup[t // tiles_per_kv] for t in range(ntile)]
            sinkcols = [_sink_scalars(sink_ref, 2 * t, 2) for t in range(ntile)]
        else:
            tiles = [[t] for t in range(ntile)]
            q2s = [_stack_heads(q_ref[0, :, _tile(t)], SCALE) for t in range(ntile)]
            kks = [_rows2(kp_ref, kc_ref, _tile(t), single) for t in range(ntile)]
            vvs = [_rows2(vp_ref, vc_ref, _tile(t), single) for t in range(ntile)]
            sinkcols = [None] * ntile
        valid = _band_mask(q2s[0].shape[0], i, geo.n_back, single)
        ss = [lax.dot_general(q2, kk, _NT, preferred_element_type=F32) for q2, kk in zip(q2s, kks)]
        parts = [_softmax_parts(s, valid, sc) for s, sc in zip(ss, sinkcols)]
        o2s = [jnp.dot(p.astype(BF16), vv, preferred_element_type=F32) / den for (p, m, den), vv in zip(parts, vvs)]
        for ts, o2, (p, m, den) in zip(tiles, o2s, parts):
            lse2 = jnp.broadcast_to(m + jnp.log(den), (o2.shape[0], PAIR_W))
            for n, t in enumerate(ts):
                rows = slice(2 * BLOCK * n, 2 * BLOCK * (n + 1))
                o_ref[0, :, _tile(t)] = _unstack_heads(o2[rows]).astype(out_dtype)
                l_ref[0, :, _tile(t)] = _unstack_heads(lse2[rows])

    prev = lambda i: jnp.maximum(i - 1, 0)
    in_specs = [
        pl.BlockSpec((1, BLOCK, qw), lambda b, j, i: (b, i, geo.qidx(j))),
        pl.BlockSpec((1, BLOCK, kw), lambda b, j, i: (b, prev(i), geo.kidx(j))),
        pl.BlockSpec((1, BLOCK, kw), lambda b, j, i: (b, i, geo.kidx(j))),
        pl.BlockSpec((1, BLOCK, kw), lambda b, j, i: (b, prev(i), geo.vidx(j))),
        pl.BlockSpec((1, BLOCK, kw), lambda b, j, i: (b, i, geo.vidx(j))),
        pl.BlockSpec(memory_space=pltpu.SMEM),
    ]
    o_spec = pl.BlockSpec((1, BLOCK, qw), lambda b, j, i: (b, i, j))
    shape = (NB, tsub, r * qw)
    o, lse = _pcall(
        body, name=name, grid=(NB, r, nblk), in_specs=in_specs, out_specs=[o_spec, o_spec],
        out_shape=[jax.ShapeDtypeStruct(shape, out_dtype), jax.ShapeDtypeStruct(shape, F32)],
        compiler_params=_params(("parallel", "parallel", "arbitrary")),
    )(qkv3, qkv3, qkv3, qkv3, qkv3, sinks)
    return o, lse


def _attn_fwd_b(qkvs, *, NB, T, name):
    geos = [_Geom(g) for g in range(len(B_PATTERNS))]
    steps = T // BLOCK
    nt = GB_W // PAIR_W
    ng = len(geos)

    def where(geo, s):
        nblk = T // geo.r // BLOCK
        return s // steps, (s % steps) // nblk, (s % steps) % nblk

    def body(*refs):
        ins, outs = refs[:5 * ng], refs[5 * ng:]
        s = pl.program_id(0)
        q2s, kks, vvs, valids = [], [], [], []
        for n, geo in enumerate(geos):
            q_ref, kp_ref, kc_ref, vp_ref, vc_ref = ins[5 * n:5 * n + 5]
            single = T // geo.r // BLOCK == 1
            valid = _band_mask(2 * BLOCK, where(geo, s)[2], geo.n_back, single)
            for t in range(nt):
                q2s.append(_stack_heads(q_ref[0, :, _tile(t)], SCALE))
                kks.append(_rows2(kp_ref, kc_ref, _tile(t), single))
                vvs.append(_rows2(vp_ref, vc_ref, _tile(t), single))
                valids.append(valid)
        ss = [lax.dot_general(q2, kk, _NT, preferred_element_type=F32) for q2, kk in zip(q2s, kks)]
        parts = [_softmax_parts(sc, valid, None) for sc, valid in zip(ss, valids)]
        o2s = [jnp.dot(p.astype(BF16), vv, preferred_element_type=F32) / den for (p, m, den), vv in zip(parts, vvs)]
        for n in range(ng):
            o_ref, l_ref = outs[2 * n], outs[2 * n + 1]
            for t in range(nt):
                o2, (p, m, den) = o2s[n * nt + t], parts[n * nt + t]
                o_ref[0, :, _tile(t)] = _unstack_heads(o2)
                l_ref[0, :, _tile(t)] = _unstack_heads(jnp.broadcast_to(m + jnp.log(den), (2 * BLOCK, PAIR_W)))

    in_specs, ins, out_specs, out_shape = [], [], [], []
    for geo, qkv in zip(geos, qkvs):
        tsub = T // geo.r
        pos = lambda s, geo=geo: where(geo, s)
        prev = lambda i: jnp.maximum(i - 1, 0)
        blk = lambda col, back, pos=pos: pl.BlockSpec(
            (1, BLOCK, GB_W), lambda s: (pos(s)[0], prev(pos(s)[2]) if back else pos(s)[2], col(pos(s)[1])))
        in_specs += [blk(geo.qidx, False), blk(geo.kidx, True), blk(geo.kidx, False), blk(geo.vidx, True),
                     blk(geo.vidx, False)]
        ins += [qkv.reshape(NB, tsub, geo.r * VAR_W)] * 5
        out_specs += [blk(lambda j: j, False)] * 2
        out_shape += [jax.ShapeDtypeStruct((NB, tsub, geo.r * GB_W), F32)] * 2
    res = _pcall(body, name=name, grid=(NB * steps,), in_specs=in_specs, out_specs=out_specs, out_shape=out_shape,
                 compiler_params=_params(("arbitrary",)))(*ins)
    return [(res[2 * n], res[2 * n + 1]) for n in range(ng)]


def _attn_bwd(qkv, do, lse, dlse, cosf, sins, sinks, g, *, NB, T, name):
    geo = _Geom(g)
    r, qw, kw, ntile = geo.r, geo.qw, geo.kw, geo.ntile
    tsub = T // r
    nblk = tsub // BLOCK
    view = lambda a, w: a.reshape(NB, tsub, r * w)
    has_dlse = dlse is not None
    tiles_per_kv = ntile // A_KV_HEADS

    single = nblk == 1
    krows = BLOCK if single else 2 * BLOCK
    nsteps = 1 if single else nblk + 1

    def grads(q2s, kks, vvs, do2s, i, lserows, sinkcols, dlrows):
        nrow = q2s[0].shape[0]
        ki = lax.broadcasted_iota(jnp.int32, (krows, nrow), 0)
        qi = jnp.bitwise_and(lax.broadcasted_iota(jnp.int32, (krows, nrow), 1), BLOCK - 1)
        if single:
            valid = qi >= ki
        else:
            dist = qi + BLOCK - ki
            valid = jnp.logical_and(jnp.logical_and(dist >= 0, dist <= geo.n_back), jnp.logical_or(ki >= BLOCK, i > 0))
        sts = [lax.dot_general(kk, q2, _NT, preferred_element_type=F32) for q2, kk in zip(q2s, kks)]
        dpts = [lax.dot_general(vv, do2, _NT, preferred_element_type=F32) for do2, vv in zip(do2s, vvs)]
        pts, dsts, sks = [], [], []
        for st, dpt, ls, sc, dl in zip(sts, dpts, lserows, sinkcols, dlrows):
            sv = jnp.where(valid, st, NEG_INF)
            if sc is not None:
                slot = ki == qi
                blk = lax.broadcasted_iota(jnp.int32, (krows, nrow), 1) // BLOCK
                sink = jnp.full((krows, nrow), sc[-1], F32)
                for b in range(len(sc) - 2, -1, -1):
                    sink = jnp.where(blk == b, sc[b], sink)
                sv = jnp.where(slot, sink, sv)
                dpt = jnp.where(slot, 0.0, dpt)
            pt = jnp.exp(sv - ls)
            delta = jnp.sum(pt * dpt, axis=0, keepdims=True)
            if dl is not None:
                delta = delta - dl
            dst = pt * (dpt - delta)
            if sc is not None:
                cols = lambda a, b: a[:, b * BLOCK:(b + 1) * BLOCK]
                sks.append([jnp.sum(jnp.where(cols(slot, b), cols(dst, b), 0.0)) for b in range(len(sc))])
                dst, pt = jnp.where(slot, 0.0, dst), jnp.where(slot, 0.0, pt)
            else:
                sks.append(None)
            pts.append(pt.astype(BF16))
            dsts.append(dst.astype(BF16))
        dq2s = [lax.dot_general(dst, kk, _TN, preferred_element_type=F32) * SCALE for dst, kk in zip(dsts, kks)]
        dkks = [jnp.dot(dst, q2, preferred_element_type=F32) for dst, q2 in zip(dsts, q2s)]
        dvvs = [jnp.dot(pt, do2, preferred_element_type=F32) for pt, do2 in zip(pts, do2s)]
        return dq2s, dkks, dvvs, sks

    def stat_row(t):
        tt = t.T
        return jnp.concatenate([tt[0:1, :], tt[HEAD_DIM:HEAD_DIM + 1, :]], axis=1)

    def body(*refs):
        it = iter(refs)
        q_ref, kp_ref, kc_ref, vp_ref, vc_ref, do_ref, l_ref = (next(it) for _ in range(7))
        dl_ref = next(it) if has_dlse else None
        c_ref, s_ref, sink_ref, o_ref, ds_ref, dq_s, dk_s, dv_s, car_q, car_k, car_v = (next(it) for _ in range(11))
        b, j, i = pl.program_id(0), pl.program_id(1), pl.program_id(2)

        @pl.when(jnp.logical_and(b == 0, jnp.logical_and(j == 0, i == 0)))
        def _():
            ds_ref[...] = jnp.zeros_like(ds_ref)

        def compute():
            if geo.sink:
                kall, vall = _rows2(kp_ref, kc_ref, _tile(0)), _rows2(vp_ref, vc_ref, _tile(0))
                tps = tiles_per_kv // A_BWD_SPLIT
                nb = 2 * tps
                tiles = [[kh * tiles_per_kv + s_ * tps + t for t in range(tps)]
                         for kh in range(A_KV_HEADS) for s_ in range(A_BWD_SPLIT)]
                kdup = [_dup_head(kall, kh) for kh in range(A_KV_HEADS)]
                vdup = [_dup_head(vall, kh) for kh in range(A_KV_HEADS)]
                cat = lambda f, ts: jnp.concatenate([f(t) for t in ts], axis=0)
                dq2s, dkks, dvvs, sks = grads(
                    [cat(lambda t: _stack_heads(q_ref[0, :, _tile(t)], SCALE), ts) for ts in tiles],
                    [kdup[n // A_BWD_SPLIT] for n in range(len(tiles))],
                    [vdup[n // A_BWD_SPLIT] for n in range(len(tiles))],
                    [cat(lambda t: _stack_heads(do_ref[0, :, _tile(t)]), ts) for ts in tiles], i,
                    [jnp.concatenate([stat_row(l_ref[0, :, _tile(t)]) for t in ts], axis=1) for ts in tiles],
                    [_sink_scalars(sink_ref, 2 * ts[0], nb) for ts in tiles], [None] * len(tiles))
                lane1 = _lane((1, PAIR_W))
                dsink = jnp.zeros((1, PAIR_W), F32)
                for ts, dq2, sk in zip(tiles, dq2s, sks):
                    for n, t in enumerate(ts):
                        dq_s[:, _tile(t)] = _unstack_heads(dq2[2 * BLOCK * n:2 * BLOCK * (n + 1)])
                    for bb in range(nb):
                        dsink = dsink + jnp.where(lane1 == 2 * ts[0] + bb, sk[bb], 0.0)
                per_kv = lambda parts, kh: functools.reduce(jnp.add, parts[kh * A_BWD_SPLIT:(kh + 1) * A_BWD_SPLIT])
                second = _lane((krows, PAIR_W)) >= HEAD_DIM
                dk_s[...] = jnp.where(second, _fold_heads(per_kv(dkks, 1)), _fold_heads(per_kv(dkks, 0)))
                dv_s[...] = jnp.where(second, _fold_heads(per_kv(dvvs, 1)), _fold_heads(per_kv(dvvs, 0)))
                ds_ref[0:1, :] += dsink
            else:
                dq2s, dkks, dvvs, _ = grads(
                    [_stack_heads(q_ref[0, :, _tile(t)], SCALE) for t in range(ntile)],
                    [_rows2(kp_ref, kc_ref, _tile(t), single) for t in range(ntile)],
                    [_rows2(vp_ref, vc_ref, _tile(t), single) for t in range(ntile)],
                    [_stack_heads(do_ref[0, :, _tile(t)]) for t in range(ntile)], i,
                    [stat_row(l_ref[0, :, _tile(t)]) for t in range(ntile)], [None] * ntile,
                    [stat_row(dl_ref[0, :, _tile(t)]) for t in range(ntile)])
                for t in range(ntile):
                    dq_s[:, _tile(t)] = _unstack_heads(dq2s[t])
                    dk_s[0:krows, _tile(t)] = dkks[t]
                    dv_s[0:krows, _tile(t)] = dvvs[t]

        def emit(dq, dk, dv):
            cos, sn = c_ref[0], s_ref[0]
            o_ref[0, :, 0:qw] = _unrope(dq, cos, sn).astype(BF16)
            o_ref[0, :, qw:qw + kw] = _unrope(dk, cos, sn).astype(BF16)
            o_ref[0, :, qw + kw:qw + 2 * kw] = dv.astype(BF16)
            if qw + 2 * kw < VAR_W:
                o_ref[0, :, qw + 2 * kw:VAR_W] = jnp.zeros((BLOCK, VAR_W - qw - 2 * kw), BF16)

        if single:
            compute()
            emit(dq_s[...], dk_s[0:BLOCK, :], dv_s[0:BLOCK, :])
            return

        @pl.when(i == 0)
        def _():
            car_q[...] = jnp.zeros_like(car_q)
            car_k[...] = jnp.zeros_like(car_k)
            car_v[...] = jnp.zeros_like(car_v)

        @pl.when(i == nblk)
        def _():
            dk_s[...] = jnp.zeros_like(dk_s)
            dv_s[...] = jnp.zeros_like(dv_s)

        pl.when(i < nblk)(compute)
        emit(car_q[...], car_k[...] + dk_s[0:BLOCK, :], car_v[...] + dv_s[0:BLOCK, :])
        car_q[...] = dq_s[...]
        car_k[...] = dk_s[BLOCK:2 * BLOCK, :]
        car_v[...] = dv_s[BLOCK:2 * BLOCK, :]

    cur = lambda i: jnp.minimum(i, nblk - 1)
    prv = lambda i: jnp.maximum(jnp.minimum(i, nblk - 1) - 1, 0)
    outb = lambda i: jnp.maximum(i - 1, 0)
    qrow = pl.BlockSpec((1, BLOCK, qw), lambda b, j, i: (b, cur(i), j))
    in_specs = [
        pl.BlockSpec((1, BLOCK, qw), lambda b, j, i: (b, cur(i), geo.qidx(j))),
        pl.BlockSpec((1, BLOCK, kw), lambda b, j, i: (b, prv(i), geo.kidx(j))),
        pl.BlockSpec((1, BLOCK, kw), lambda b, j, i: (b, cur(i), geo.kidx(j))),
        pl.BlockSpec((1, BLOCK, kw), lambda b, j, i: (b, prv(i), geo.vidx(j))),
        pl.BlockSpec((1, BLOCK, kw), lambda b, j, i: (b, cur(i), geo.vidx(j))),
        qrow, qrow,
    ]
    ins = [view(qkv, VAR_W)] * 5 + [view(do, qw), view(lse, qw)]
    if has_dlse:
        in_specs.append(qrow)
        ins.append(view(dlse, qw))
    in_specs += [
        pl.BlockSpec((1, BLOCK, PAIR_W), lambda b, j, i: (b, outb(i), j)),
        pl.BlockSpec((1, BLOCK, PAIR_W), lambda b, j, i: (b, outb(i), j)),
        pl.BlockSpec(memory_space=pltpu.SMEM),
    ]
    ins += [view(cosf, PAIR_W), view(sins, PAIR_W), sinks]
    scratch = [pltpu.VMEM((BLOCK, qw), F32), pltpu.VMEM((2 * BLOCK, kw), F32), pltpu.VMEM((2 * BLOCK, kw), F32),
               pltpu.VMEM((BLOCK, qw), F32), pltpu.VMEM((BLOCK, kw), F32), pltpu.VMEM((BLOCK, kw), F32)]
    dqkv, dsink = _pcall(
        body, name=name, grid=(NB, r, nsteps), in_specs=in_specs,
        out_specs=[pl.BlockSpec((1, BLOCK, VAR_W), lambda b, j, i: (b, outb(i), j)),
                   pl.BlockSpec((8, PAIR_W), lambda b, j, i: (0, 0))],
        out_shape=[jax.ShapeDtypeStruct((NB, tsub, r * VAR_W), BF16), jax.ShapeDtypeStruct((8, PAIR_W), F32)],
        scratch_shapes=scratch, compiler_params=_params(("arbitrary", "arbitrary", "arbitrary")),
    )(*ins)
    return dqkv, dsink


class _Rows:
    def __init__(self, N, T, tm):
        self.N, self.tm, self.tpe, self.grid = N, tm, T // tm, (N // tm,)

    def row(self, w, col=0):
        return pl.BlockSpec((self.tm, w), lambda i: (i, col))

    def ex(self, w):
        return pl.BlockSpec((1, 1, w), lambda i: (i // self.tpe, 0, 0))

    def const(self, shape):
        return pl.BlockSpec(shape, lambda i: tuple(0 for _ in shape))

    def view(self, w, r):
        return pl.BlockSpec((None, self.tm // r, r * w), lambda i: (i // self.tpe, i % self.tpe, 0))

    def first_of_example(self):
        return pl.program_id(0) % self.tpe == 0


def _acc(ref, first, val):
    @pl.when(first)
    def _():
        ref[0] = val

    @pl.when(jnp.logical_not(first))
    def _():
        ref[0] += val


def _colsum(v):
    return jnp.sum(v, axis=0, keepdims=True)


def _ln_stats(r):
    mu = jnp.mean(r, axis=-1, keepdims=True)
    xc = r - mu
    var = jnp.mean(xc * xc, axis=-1, keepdims=True)
    rstd = lax.rsqrt(var + LN_EPS)
    return xc * rstd, rstd


def _ln_bwd(dy, xhat, rstd, gain):
    dxh = dy * gain
    return rstd * (dxh - jnp.mean(dxh, axis=-1, keepdims=True) - xhat * jnp.mean(dxh * xhat, axis=-1, keepdims=True))


def _from_view(ref, scr, r):
    if r == 1:
        return ref[...]
    rows, w = ref.shape[0], ref.shape[1] // r
    for j in range(r):
        for c in range(w // LANES):
            scr.at[c][pl.ds(j, rows, stride=r), :] = ref[:, j * w + c * LANES:j * w + (c + 1) * LANES].astype(F32)
    return jnp.concatenate([scr[c] for c in range(w // LANES)], axis=1)


def _to_view(val, ref, scr, r):
    if r == 1:
        ref[...] = val.astype(ref.dtype)
        return
    rows, w = ref.shape[0], ref.shape[1] // r
    for c in range(w // LANES):
        scr[c] = val[:, c * LANES:(c + 1) * LANES]
    for j in range(r):
        for c in range(w // LANES):
            ref[:, j * w + c * LANES:j * w + (c + 1) * LANES] = scr.at[c][pl.ds(j, rows, stride=r), :].astype(ref.dtype)


def _silu_parts(v):
    s = jax.nn.sigmoid(v)
    return v * s, s * (1.0 + v * (1.0 - s))


def _local_step(x, mod, positions, w_in, rest_weights, sinks, ln1_g, ln1_b, ln2_g, ln2_b, target, hook=None):
    hook = hook or (lambda event, **data: None)
    NB, T, D = x.shape
    N = NB * T
    x2 = x.reshape(N, D)
    tgt2 = target.reshape(N, D)
    shift_m, scale_m, gate_m, shift_f, scale_f, gate_f = [mod[:, None, k * D:(k + 1) * D] for k in range(6)]
    cosf, sins = _rope_tables(positions)
    col = jnp.arange(QKV_P)
    vcol = col % VAR_W
    flags = jnp.where(col < VAR_W, vcol < QA_W + KA_W, vcol < 2 * GB_W).astype(F32)[None]
    R = _Rows(N, T, _pick(T, 256))
    sds = jax.ShapeDtypeStruct
    exsum = lambda w=D: sds((NB, 1, w), F32)
    ngrp = len(B_PATTERNS)

    *qkv, u = _inproj(x2, scale_m, shift_m, w_in, cosf, sins, flags, T=T, name="inproj_qkv")
    gates = _mm(u, w_in[QKV_P:], tb=True, out_dtype=BF16, name="inproj_gates")
    oa, la = _attn_fwd(qkv[0], sinks, None, NB=NB, T=T, name="attn_a_fwd")
    oa = oa.reshape(N, QA_W)
    (o1, l1), (o2, l2), (o3, l3) = _attn_fwd_b(qkv[1:], NB=NB, T=T, name="attn_b_fwd")
    w_a, w_b, w_o, w_gu, w_d = rest_weights()
    F = w_d.shape[0]
    dil = [r_ for _, r_ in B_PATTERNS]
    views = [R.view(GB_W, r_) for r_ in dil]
    tokbuf = pltpu.VMEM((GB_W // LANES, R.tm, LANES), F32)

    f32 = lambda ref: ref[...].astype(F32)
    Rm = _Rows(N, T, _pick(T, 512))

    def mix_out(o1r, o2r, o3r, l1r, l2r, l3r, oa_r, ga_r, gb_r, x_r, gm_r, g_r, b_r, sf_r, hf_r, wa_r, wb_r, wo_r,
                ob_ref, ya_ref, yb_ref, mg_ref, y_ref, r1_ref, u2_ref, *bufs):
        os_ = [_from_view(ref, bufs[n], dil[n]) for n, ref in enumerate((o1r, o2r, o3r))]
        la, lb, lc = [_from_view(ref, bufs[3 + n], dil[n]) for n, ref in enumerate((l1r, l2r, l3r))]
        mx = jnp.maximum(jnp.maximum(la, lb), lc)
        ea, eb, ec = jnp.exp(la - mx), jnp.exp(lb - mx), jnp.exp(lc - mx)
        ob = ((ea * os_[0] + eb * os_[1] + ec * os_[2]) / (ea + eb + ec)).astype(BF16)
        ob_ref[...] = ob
        ya = jnp.dot(oa_r[...], wa_r[...], preferred_element_type=F32).astype(BF16)
        yb = jnp.concatenate([jnp.dot(ob, wb_r[s_], preferred_element_type=F32)
                              for s_ in range(w_b.shape[0])], axis=1).astype(BF16)
        merged = (jax.nn.sigmoid(f32(ga_r)) * ya.astype(F32) + jax.nn.sigmoid(f32(gb_r)) * yb.astype(F32)).astype(BF16)
        y = jnp.dot(merged, wo_r[...], preferred_element_type=F32)
        r1 = ALPHA * x_r[...] + (1.0 + gm_r[0]) * y
        xhat, _ = _ln_stats(r1)
        x1 = xhat * g_r[...] + b_r[...]
        ya_ref[...], yb_ref[...], mg_ref[...], y_ref[...], r1_ref[...] = ya, yb, merged, y, r1
        u2_ref[...] = (x1 * (1.0 + sf_r[0]) + hf_r[0]).astype(BF16)

    mviews = [Rm.view(GB_W, r_) for r_ in dil]
    ob, ya, yb, merged, y, r1, u2 = _pcall(
        mix_out, name="mix_out", grid=Rm.grid,
        in_specs=mviews + mviews + [Rm.row(QA_W), Rm.row(D, 0), Rm.row(D, 1), Rm.row(D), Rm.ex(D), Rm.const((1, D)),
                                    Rm.const((1, D)), Rm.ex(D), Rm.ex(D), Rm.const(w_a.shape), Rm.const(w_b.shape),
                                    Rm.const(w_o.shape)],
        out_specs=[Rm.row(GB_W)] + [Rm.row(D)] * 6,
        out_shape=[sds((N, GB_W), BF16)] + [sds((N, D), BF16)] * 3 + [sds((N, D), F32)] * 2 + [sds((N, D), BF16)],
        scratch_shapes=[pltpu.VMEM((GB_W // LANES, Rm.tm, LANES), F32)] * 6,
        compiler_params=_params(("parallel",)))(o1, o2, o3, l1, l2, l3, oa, gates, gates, x2, gate_m, ln1_g, ln1_b,
                                                scale_f, shift_f, w_a, w_b, w_o)

    w_gu = w_gu() if callable(w_gu) else w_gu
    tnf = w_gu.shape[2]
    nft = w_gu.shape[0] // 2
    tmf = _pick(N, 512)

    def ffn_up(u_r, wg_r, wu_r, hg_ref, hu_ref, a_ref):
        hg = jnp.dot(u_r[...], wg_r[...], preferred_element_type=F32)
        hu = jnp.dot(u_r[...], wu_r[...], preferred_element_type=F32)
        sl, _ = _silu_parts(hg)
        hg_ref[...] = hg.astype(BF16)
        hu_ref[...] = hu.astype(BF16)
        a_ref[...] = (sl * hu).astype(BF16)

    ftile = pl.BlockSpec((tmf, tnf), lambda j, i: (i, j))
    hg, hu, act = _pcall(
        ffn_up, name="ffn_up", grid=(nft, N // tmf),
        in_specs=[pl.BlockSpec((tmf, D), lambda j, i: (i, 0)), pl.BlockSpec((None, D, tnf), lambda j, i: (j, 0, 0)),
                  pl.BlockSpec((None, D, tnf), lambda j, i: (j + nft, 0, 0))],
        out_specs=[ftile] * 3, out_shape=[sds((N, F), BF16)] * 3,
        compiler_params=_params(("arbitrary", "parallel")))(u2, w_gu, w_gu)
    fchunk = _pick(F, 768)

    def ffn_down_norm2(act_r, wd_r, r1_r, g1_r, b1_r, t_r, gf_r, g_r, b_r, hg_r, hu_r,
                       dy2_ref, dx1_ref, dgf_ref, dg_ref, db_ref, loss_ref, dh_ref):
        first = R.first_of_example()
        y2v = jnp.dot(act_r[...], wd_r[...], preferred_element_type=F32)
        x1 = _ln_stats(r1_r[...])[0] * g1_r[...] + b1_r[...]
        r2 = ALPHA * x1 + (1.0 + gf_r[0]) * y2v
        xhat, rstd = _ln_stats(r2)
        err = xhat * g_r[...] + b_r[...] - t_r[...]
        dx2 = err * (1.0 / D)
        dr2 = _ln_bwd(dx2, xhat, rstd, g_r[...])
        dy2 = ((1.0 + gf_r[0]) * dr2).astype(BF16)
        dy2_ref[...] = dy2
        dx1_ref[...] = ALPHA * dr2
        _acc(dgf_ref, first, _colsum(dr2 * y2v))
        _acc(dg_ref, first, _colsum(dx2 * xhat))
        _acc(db_ref, first, _colsum(dx2))
        part = 0.5 * jnp.sum(jnp.mean(err * err, axis=-1, keepdims=True))
        _acc(loss_ref, first, jnp.broadcast_to(part, (1, 128)))
        for t in range(F // fchunk):
            cs = slice(t * fchunk, (t + 1) * fchunk)
            da = lax.dot_general(dy2, wd_r[cs, :], _NT, preferred_element_type=F32)
            sl, dsl = _silu_parts(hg_r[:, cs].astype(F32))
            dh_ref[:, cs] = (da * hu_r[:, cs].astype(F32) * dsl).astype(BF16)
            dh_ref[:, F + t * fchunk:F + (t + 1) * fchunk] = (da * sl).astype(BF16)

    dy2, dx1p, dgate_f, dg2, db2, loss_p, dh = _pcall(
        ffn_down_norm2, name="ffn_down_norm2", grid=R.grid,
        in_specs=[R.row(F), R.const((F, D)), R.row(D), R.const((1, D)), R.const((1, D)), R.row(D), R.ex(D),
                  R.const((1, D)), R.const((1, D)), R.row(F), R.row(F)],
        out_specs=[R.row(D), R.row(D), R.ex(D), R.ex(D), R.ex(D), R.ex(128), R.row(2 * F)],
        out_shape=[sds((N, D), BF16), sds((N, D), F32), exsum(), exsum(), exsum(), exsum(128), sds((N, 2 * F), BF16)],
        compiler_params=_params(("arbitrary",)))(act, w_d, r1, ln1_g, ln1_b, tgt2, gate_f, ln2_g, ln2_b, hg, hu)

    g_wd = _mm(act, dy2, ta=True, out_dtype=BF16, name="ffn_down_dw")
    g_wgu = _mm(u2, dh, ta=True, out3=w_gu.shape[0], out_dtype=BF16, name="ffn_up_dw")
    hook("ffn_grads", g_wgu=g_wgu)

    def ffn_up_dx_norm1(dh_r, w_r, dx1p_r, r1_r, y_r, sf_r, gm_r, g_r, b_r,
                        dxp_ref, dy_ref, dsf_ref, dhf_ref, dgm_ref, dg_ref, db_ref):
        first = R.first_of_example()
        du2v = None
        for s_ in range(w_gu.shape[0]):
            part = lax.dot_general(dh_r[:, s_ * tnf:(s_ + 1) * tnf], w_r[s_], _NT, preferred_element_type=F32)
            du2v = part if du2v is None else du2v + part
        dx1 = dx1p_r[...] + du2v * (1.0 + sf_r[0])
        xhat, rstd = _ln_stats(r1_r[...])
        dr1 = _ln_bwd(dx1, xhat, rstd, g_r[...])
        dxp_ref[...] = ALPHA * dr1
        dy_ref[...] = ((1.0 + gm_r[0]) * dr1).astype(BF16)
        _acc(dsf_ref, first, _colsum(du2v * (xhat * g_r[...] + b_r[...])))
        _acc(dhf_ref, first, _colsum(du2v))
        _acc(dgm_ref, first, _colsum(dr1 * y_r[...]))
        _acc(dg_ref, first, _colsum(dx1 * xhat))
        _acc(db_ref, first, _colsum(dx1))

    dxp, dy, dscale_f, dshift_f, dgate_m, dg1, db1 = _pcall(
        ffn_up_dx_norm1, name="ffn_up_dx_norm1", grid=R.grid,
        in_specs=[R.row(2 * F), R.const(w_gu.shape)] + [R.row(D)] * 3 + [R.ex(D), R.ex(D), R.const((1, D)),
                                                                        R.const((1, D))],
        out_specs=[R.row(D), R.row(D)] + [R.ex(D)] * 5,
        out_shape=[sds((N, D), F32), sds((N, D), BF16)] + [exsum()] * 5,
        compiler_params=_params(("arbitrary",)))(dh, w_gu, dx1p, r1, y, scale_f, gate_m, ln1_g, ln1_b)

    g_wo = _mm(merged, dy, ta=True, out_dtype=BF16, name="out_proj_dw")

    seg = (jnp.arange(GB_W)[:, None] // HEAD_DIM == jnp.arange(GB_W)[None, :] // HEAD_DIM).astype(BF16)

    def mix_out_bwd(dy_r, ya_r, yb_r, ga_r, gb_r, wo_r, wa_r, wb_r, o1r, o2r, o3r, l1r, l2r, l3r, seg_r,
                    dya_ref, dyb_ref, dg_ref, doa_ref, d1, d2, d3, e1, e2, e3, *bufs):
        dm = lax.dot_general(dy_r[...], wo_r[...], _NT, preferred_element_type=F32).astype(BF16).astype(F32)
        sa, sb = jax.nn.sigmoid(f32(ga_r)), jax.nn.sigmoid(f32(gb_r))
        dya, dyb = (dm * sa).astype(BF16), (dm * sb).astype(BF16)
        dya_ref[...], dyb_ref[...] = dya, dyb
        dg_ref[:, :D] = (dm * f32(ya_r) * sa * (1.0 - sa)).astype(BF16)
        dg_ref[:, D:] = (dm * f32(yb_r) * sb * (1.0 - sb)).astype(BF16)
        doa_ref[...] = lax.dot_general(dya, wa_r[...], _NT, preferred_element_type=F32).astype(BF16)
        ds_ = D // w_b.shape[0]
        dob = None
        for s_ in range(w_b.shape[0]):
            part = lax.dot_general(dyb[:, s_ * ds_:(s_ + 1) * ds_], wb_r[s_], _NT, preferred_element_type=F32)
            dob = part if dob is None else dob + part
        dob_v = dob
        os_ = [_from_view(ref, bufs[n], dil[n]) for n, ref in enumerate((o1r, o2r, o3r))]
        la, lb, lc = [_from_view(ref, bufs[3 + n], dil[n]) for n, ref in enumerate((l1r, l2r, l3r))]
        mx = jnp.maximum(jnp.maximum(la, lb), lc)
        ea, eb, ec = jnp.exp(la - mx), jnp.exp(lb - mx), jnp.exp(lc - mx)
        inv = 1.0 / (ea + eb + ec)
        ws = [ea * inv, eb * inv, ec * inv]

        def headsum(v):
            hi = v.astype(BF16)
            lo = (v - hi.astype(F32)).astype(BF16)
            sm = seg_r[...]
            return jnp.dot(hi, sm, preferred_element_type=F32) + jnp.dot(lo, sm, preferred_element_type=F32)

        dws = [headsum(dob_v * o) for o in os_]
        mean = ws[0] * dws[0] + ws[1] * dws[1] + ws[2] * dws[2]
        for n, (w_, dw_, d_ref, e_ref) in enumerate(zip(ws, dws, (d1, d2, d3), (e1, e2, e3))):
            _to_view(w_ * dob_v, d_ref, bufs[6], dil[n])
            _to_view(w_ * (dw_ - mean), e_ref, bufs[7], dil[n])

    vshape = lambda r_, dt: sds((NB, T // r_, r_ * GB_W), dt)
    dya, dyb, dgates, doa, *mb = _pcall(
        mix_out_bwd, name="mix_out_bwd", grid=R.grid,
        in_specs=[R.row(D)] * 3 + [R.row(D, 0), R.row(D, 1), R.const(w_o.shape), R.const(w_a.shape), R.const(w_b.shape)]
        + views + views + [R.const((GB_W, GB_W))],
        out_specs=[R.row(D), R.row(D), R.row(2 * D), R.row(QA_W)] + views + views,
        out_shape=[sds((N, D), BF16), sds((N, D), BF16), sds((N, 2 * D), BF16), sds((N, QA_W), BF16)]
        + [vshape(r_, BF16) for r_ in dil] + [vshape(r_, F32) for r_ in dil],
        scratch_shapes=[tokbuf] * 8,
        compiler_params=_params(("parallel",)))(dy, ya, yb, gates, gates, w_o, w_a, w_b, o1, o2, o3, l1, l2, l3, seg)
    do_b, dlse_b = mb[:3], mb[3:]

    g_wa = _mm(oa, dya, ta=True, out_dtype=BF16, name="branch_a_dw")
    g_wb = _mm(ob, dyb, ta=True, out3=w_b.shape[0], out_dtype=BF16, name="branch_b_dw")
    hook("rest_grads", g_wa=g_wa, g_wb=g_wb, g_wo=g_wo, g_wd=g_wd)

    dqkv_a, dsink = _attn_bwd(qkv[0], doa, la, None, cosf, sins, sinks, None, NB=NB, T=T, name="attn_a_bwd")
    hook("attn_a_bwd_done")
    dqkv = [dqkv_a]
    for g in range(ngrp):
        dqkv.append(_attn_bwd(qkv[1 + g], do_b[g], (l1, l2, l3)[g], dlse_b[g], cosf, sins, sinks, g, NB=NB, T=T,
                              name=f"attn_b{g}_bwd")[0])
        hook(f"attn_b{g}_bwd_done")

    g_win = [_mm(d3.reshape(N, VAR_W), u, ta=True, out_dtype=BF16, name=f"inproj_dw{v}") if VAR_DIL[v] == 1
             else _dw_view(d3, u, VAR_DIL[v], name=f"inproj_dw{v}") for v, d3 in enumerate(dqkv)]
    g_win.append(_mm(dgates, u, ta=True, out_dtype=BF16, name=f"inproj_dw{N_VAR}"))
    hook("win_grads", g_win=g_win)
    wvar = lambda v: (w_in, (VAR_W, D), (v, 0))
    dview = lambda v: (dqkv[v], VAR_DIL[v])
    du = _mm_multi([dview(0)], [wvar(0)], M=N, T=T, name="inproj_dx0")
    hook("inproj_dx0_done")
    def x_bwd(duv, ins, outs):
        (dxp_r, x_r, sm_r), (gx_ref, dsm_ref, dhm_ref) = ins, outs
        first = R.first_of_example()
        gx_ref[...] = dxp_r[...] + duv * (1.0 + sm_r[0])
        _acc(dsm_ref, first, _colsum(duv * x_r[...]))
        _acc(dhm_ref, first, _colsum(duv))

    gx, dscale_m, dshift_m = _mm_multi(
        [dview(v) for v in range(1, N_VAR)] + [dgates],
        [wvar(v) for v in range(1, N_VAR)] + [(w_in, (2 * D, D), (QKV_P // (2 * D), 0))],
        M=N, T=T, add=du, tm=R.tm, name="inproj_dx1",
        post=([dxp, x2, scale_m], [R.row(D), R.row(D), R.ex(D)], [R.row(D), R.ex(D), R.ex(D)],
              [sds((N, D), F32), exsum(), exsum()], x_bwd))
    hook("inproj_dx1_done")

    dmod =jnp.concatenate([dshift_m, dscale_m, dgate_m, dshift_f, dscale_f, dgate_f], axis=-1)[:, 0]
    ln_grads = jnp.concatenate([dg1, db1, dg2, db2], axis=1)
    return dict(loss=loss_p[:, 0, 0], grad_x=gx.reshape(NB, T, D), g_win=g_win, g_wa=g_wa, g_wb=g_wb, g_wo=g_wo,
                g_wgu=g_wgu, g_wd=g_wd, dmod=dmod, ln_grads=ln_grads, dsink=dsink[0, :A_Q_HEADS])


def _coords():
    return lax.axis_index("x"), lax.axis_index("y"), lax.axis_index("c")


def _allgather_small(blk, *, name):
    m_per, n = blk.shape

    def body(x_ref, out_ref, send_sems, recv_sems, local_sem):
        x, y, c = _coords()
        me, sibling = (x, y, c), (x, y, 1 - c)
        chips = [(1 - x, y), (x, 1 - y), (1 - x, 1 - y)]

        def rows(px, py, pc):
            return out_ref.at[pl.ds((4 * px + 2 * py + pc) * m_per, m_per), :]

        def copy(k, block, to, src=None):
            return pltpu.make_async_remote_copy(
                src_ref=rows(*block) if src is None else src, dst_ref=rows(*block),
                send_sem=send_sems.at[k], recv_sem=recv_sems.at[k], device_id=to, device_id_type=MESH)

        mine = pltpu.make_async_copy(x_ref, rows(*me), local_sem)
        mine.start()
        first = [copy(0, me, sibling, src=x_ref)]
        first += [copy(1 + j, me, (*chip, c), src=x_ref) for j, chip in enumerate(chips)]
        for cp in first:
            cp.start()
        passed = [copy(4 + j, (*chip, c), sibling) for j, chip in enumerate(chips)]
        for j, chip in enumerate(chips):
            copy(1 + j, (*chip, c), me).wait_recv()
            passed[j].start()
        copy(0, sibling, me).wait_recv()
        for j, chip in enumerate(chips):
            copy(4 + j, (*chip, 1 - c), me).wait_recv()
        for cp in first + passed:
            cp.wait_send()
        mine.wait()

    return _pcall(
        body, name=name, out_shape=jax.ShapeDtypeStruct((8 * m_per, n), blk.dtype),
        in_specs=[pl.BlockSpec(memory_space=pltpu.VMEM)], out_specs=pl.BlockSpec(memory_space=pltpu.VMEM),
        scratch_shapes=[pltpu.SemaphoreType.DMA((7,)), pltpu.SemaphoreType.DMA((7,)), pltpu.SemaphoreType.DMA],
        compiler_params=pltpu.CompilerParams(vmem_limit_bytes=VMEM_LIMIT_BYTES),
    )(blk)


def _exchange(srcs, dsts, plan, *, name, dst_inits=None):
    na = len(dsts)
    nrem = len(plan(0, 0, 0))

    def body(*refs):
        refs = list(refs)
        src_refs = [refs.pop(0) for _ in range(na)] if srcs is not None else None
        if dst_inits is not None:
            del refs[:na]
        dst_refs, (send_sems, recv_sems) = refs[:na], refs[na:]
        start, wait = _copies(dst_refs if src_refs is None else src_refs, dst_refs, send_sems, recv_sems, plan)
        start()
        wait()

    hbm = pl.BlockSpec(memory_space=pl.ANY)
    ins = (list(srcs) if srcs is not None else []) + (list(dst_inits) if dst_inits is not None else [])
    base = na if srcs is not None else 0
    aliases = {base + a: a for a in range(na)} if dst_inits is not None else {}
    return _pcall(
        body, name=name, out_shape=list(dsts), in_specs=[hbm] * len(ins), out_specs=[hbm] * na,
        input_output_aliases=aliases,
        scratch_shapes=[pltpu.SemaphoreType.DMA((na * nrem,)), pltpu.SemaphoreType.DMA((na * nrem,))],
    )(*ins)


def _other_chips(x, y):
    return [(1 - x, y), (x, 1 - y), (1 - x, 1 - y)]


def _round(ride, carrier, name):
    if carrier is not None:
        _RIDES.setdefault(carrier, []).append(ride)
        return
    srcs = ride.srcs() if callable(ride.srcs) else ride.srcs
    inits = ride.dst_inits() if callable(ride.dst_inits) else ride.dst_inits
    ride.out = list(_exchange(srcs, ride.dsts, ride.plan, name=name, dst_inits=inits))


def _index_operand(i):
    return jnp.reshape(i, (1,)).astype(jnp.int32)


def _cast_own_shard(w, chip, *, name):
    rows, cols = w.shape
    hr = rows // 2
    tr = _pick(hr, 600, 16)

    def body(k_ref, w_ref, o_ref):
        o_ref[...] = w_ref[...].astype(BF16)

    grid_spec = pltpu.PrefetchScalarGridSpec(
        num_scalar_prefetch=1, grid=(2, hr // tr),
        in_specs=[pl.BlockSpec((tr, cols), lambda h, i, k: (h * (hr // tr) + i, 0))],
        out_specs=pl.BlockSpec((None, None, tr, cols), lambda h, i, k: (k[0], h, i, 0)))
    return _pcall(body, name=name, grid_spec=grid_spec, out_shape=jax.ShapeDtypeStruct((4, 2, hr, cols), BF16),
                  compiler_params=_params(("parallel", "parallel")))(_index_operand(chip), w)


class _Gather:
    def __init__(self, shards, chip, tag, carriers=(None, None), in_place=False):
        own = (lambda k, c: (k, c)) if in_place else (lambda k, c: (c,))

        def plan_ici(x, y, c):
            k = 2 * x + y
            return [(own(k, c), (k, c), (2 * px + py, c), (px, py, c)) for px, py in _other_chips(x, y)]

        def plan_d2d(x, y, c):
            return [((2 * px + py, c), (2 * px + py, c), (2 * px + py, 1 - c), (x, y, 1 - c))
                    for px, py in _other_chips(x, y)]

        def plan_near(x, y, c):
            k = 2 * x + y
            return [(own(k, c), (k, c), (2 * px + py, c), (px, py, c)) for px, py in ((1 - x, y), (x, 1 - y))]

        def plan_far(x, y, c):
            kx, ky, kd = 2 * (1 - x) + y, 2 * x + (1 - y), 2 * (1 - x) + (1 - y)
            hp = dsts[0].shape[2] // 2
            top, bottom = pl.ds(0, hp), pl.ds(hp, hp)
            return [((kx, c, top), (kx, c, top), (kd, c, top), (x, 1 - y, c)),
                    ((ky, c, bottom), (ky, c, bottom), (kd, c, bottom), (1 - x, y, c))]

        self.shards, self.chip, self.in_place = shards, chip, in_place
        dsts = [jax.ShapeDtypeStruct(s.shape if in_place else (4,) + s.shape, s.dtype) for s in shards]
        first = lambda plan: _Ride(None, dsts, plan, dst_inits=shards) if in_place else _Ride(shards, dsts, plan)
        if len(carriers) == 3:
            near = first(plan_near)
            ici = _Ride(None, dsts, plan_far, dst_inits=lambda: near.out)
            _round(near, carriers[0], f"gather_{tag}_near")
            _round(ici, carriers[1], f"gather_{tag}_far")
        else:
            ici = first(plan_ici)
            _round(ici, carriers[0], f"gather_{tag}_ici")
        self.d2d = _Ride(None, dsts, plan_d2d, dst_inits=lambda: ici.out)
        _round(self.d2d, carriers[-1], f"gather_{tag}_d2d")

    def result(self):
        full = self.d2d.out if self.in_place else [
            lax.dynamic_update_index_in_dim(f, s, self.chip, 0) for f, s in zip(self.d2d.out, self.shards)]
        return [f.reshape((4, 2 * f.shape[2], f.shape[3])) for f in full]


def _add_pairs(g, f, ci, *, name):
    s, _, hr, wd = g.shape
    tr = _pick(hr, 600, 16)

    def body(c_ref, a_ref, b_ref, o_ref):
        o_ref[...] = (a_ref[...].astype(F32) + b_ref[...].astype(F32)).astype(BF16)

    spec = pl.BlockSpec((1, tr, wd), lambda j, i, c: (j, i, 0))
    grid_spec = pltpu.PrefetchScalarGridSpec(
        num_scalar_prefetch=1, grid=(s, hr // tr),
        in_specs=[pl.BlockSpec((1, None, tr, wd), lambda j, i, c: (j, c[0], i, 0)), spec], out_specs=spec)
    return _pcall(body, name=name, grid_spec=grid_spec, out_shape=jax.ShapeDtypeStruct(f.shape, BF16),
                  compiler_params=_params(("parallel", "parallel")))(_index_operand(ci), g, f)


def _sum_chips(landed, pairs, chip, ci, *, name):
    s, hr, wd = landed.shape
    tr = _pick(hr, 600, 16)

    def body(k_ref, l_ref, p_ref, o_ref):
        acc = None
        for k in range(s):
            part = jnp.where(k_ref[0] == k, p_ref[k], l_ref[k]).astype(F32)
            acc = part if acc is None else acc + part
        o_ref[...] = acc

    spec = pl.BlockSpec((s, tr, wd), lambda i, k: (0, i, 0))
    grid_spec = pltpu.PrefetchScalarGridSpec(
        num_scalar_prefetch=1, grid=(hr // tr,), in_specs=[spec, spec],
        out_specs=pl.BlockSpec((None, tr, wd), lambda i, k: (k[1], i, 0)))
    where = jnp.stack([chip, ci]).astype(jnp.int32)
    return _pcall(body, name=name, grid_spec=grid_spec, out_shape=jax.ShapeDtypeStruct((2, hr, wd), F32),
                  compiler_params=_params(("parallel",)))(where, landed, pairs)


class _ReduceScatter:
    def __init__(self, gs, chip, ci, tag):
        self.gs, self.chip, self.ci, self.tag = gs, chip, ci, tag
        self.half_t = [jax.ShapeDtypeStruct((g.shape[0],) + g.shape[2:], BF16) for g in gs]

    def pair(self, carrier=None):
        plan = lambda x, y, c: [((slice(None), 1 - c), (), (), (x, y, 1 - c))]
        self.r1 = _Ride(self.gs, self.half_t, plan)
        _round(self.r1, carrier, f"reduce_{self.tag}_pair")

    def chips(self, carrier=None):
        def plan(x, y, c):
            k = 2 * x + y
            return [((2 * px + py,), (k,), (2 * px + py,), (px, py, c)) for px, py in _other_chips(x, y)]

        self.pairs = [_add_pairs(g, f, self.ci, name=f"reduce_{self.tag}_pair_add{n}")
                      for n, (g, f) in enumerate(zip(self.gs, self.r1.out))]
        self.r2 = _Ride(self.pairs, self.half_t, plan)
        _round(self.r2, carrier, f"reduce_{self.tag}_chips")

    def halves(self, carrier=None):
        plan = lambda x, y, c: [((c,), (c,), (1 - c,), (x, y, 1 - c))]
        mine = [_sum_chips(l, p, self.chip, self.ci, name=f"reduce_{self.tag}_chip_sum{n}")
                for n, (l, p) in enumerate(zip(self.r2.out, self.pairs))]
        self.r3 = _Ride(None, [jax.ShapeDtypeStruct(m.shape, F32) for m in mine], plan, dst_inits=mine)
        _round(self.r3, carrier, f"reduce_{self.tag}_halves")

    def result(self):
        return [b.reshape(2 * b.shape[1], b.shape[2]) for b in self.r3.out]


def _ada_fwd(c_all, w_sh, b_sh, *, name):
    nb, d = c_all.shape
    wcols = w_sh.shape[1]
    tn = _pick(wcols, 512)

    def body(c_ref, w_ref, b_ref, o_ref, a_ref):
        cv = c_ref[...]
        act = cv * jax.nn.sigmoid(cv)
        a_ref[...] = act
        o_ref[...] = jnp.dot(act.astype(BF16), w_ref[...].astype(BF16), preferred_element_type=F32) + b_ref[...]

    return _pcall(
        body, name=name, grid=(wcols // tn,),
        in_specs=[pl.BlockSpec((nb, d), lambda j: (0, 0)), pl.BlockSpec((d, tn), lambda j: (0, j)),
                  pl.BlockSpec((1, tn), lambda j: (0, j))],
        out_specs=[pl.BlockSpec((nb, tn), lambda j: (0, j)), pl.BlockSpec((nb, d), lambda j: (0, 0))],
        out_shape=[jax.ShapeDtypeStruct((nb, wcols), F32), jax.ShapeDtypeStruct((nb, d), F32)],
        compiler_params=_params(("arbitrary",)))(c_all, w_sh, b_sh)


def _sum_devices(g, *, name):
    nd, m, w = g.shape

    def body(g_ref, o_ref):
        acc = g_ref[0]
        for k in range(1, nd):
            acc = acc + g_ref[k]
        o_ref[...] = acc

    return _pcall(body, name=name, out_shape=jax.ShapeDtypeStruct((m, w), F32),
                  compiler_params=pltpu.CompilerParams(vmem_limit_bytes=VMEM_LIMIT_BYTES))(g)


def _adamw(w, g, m, v, *, name):
    rows, cols = w.shape[-2:]
    tr = _pick(rows, max(8, (1 << 18) // cols), 8)
    c1 = 1.0 / (1.0 - ADAM_B1 ** ADAM_STEP)
    c2 = 1.0 / (1.0 - ADAM_B2 ** ADAM_STEP)

    def body(w_ref, g_ref, m_ref, v_ref, d_ref, nm_ref, nv_ref):
        gv = g_ref[...]
        nm = ADAM_B1 * m_ref[...] + (1.0 - ADAM_B1) * gv
        nv = ADAM_B2 * v_ref[...] + (1.0 - ADAM_B2) * (gv * gv)
        d_ref[...] = -ADAM_LR * ((nm * c1) / (jnp.sqrt(nv * c2) + ADAM_EPS) + ADAM_WD * w_ref[...])
        nm_ref[...] = nm
        nv_ref[...] = nv

    gspec = pl.BlockSpec((tr, cols), lambda i: (i, 0))
    spec = pl.BlockSpec((None, tr, cols), lambda i: (0, i, 0)) if w.ndim == 3 else gspec
    shp = jax.ShapeDtypeStruct(w.shape, F32)
    return _pcall(body, name=name, grid=(rows // tr,), in_specs=[spec, gspec, spec, spec], out_specs=[spec] * 3,
                  out_shape=[shp] * 3, compiler_params=_params(("parallel",)))(w, g, m, v)


def _permute_in_rows(wt):
    ngrp = len(B_PATTERNS)
    qb, kb, vb = (wt[A_W + n * QB_W:A_W + (n + 1) * QB_W] for n in range(3))
    parts = [wt[:A_W], jnp.zeros((VAR_W - A_W, wt.shape[1]), wt.dtype)]
    for g in range(ngrp):
        parts += [t[g * GB_W:(g + 1) * GB_W] for t in (qb, kb, vb)]
    return jnp.concatenate(parts + [wt[A_W + 3 * QB_W:]], axis=0)


def _unpermute_in_grads(pieces):
    ga, groups, gg = pieces[0], pieces[1:-1], pieces[-1]
    rows = [ga[:A_W]]
    for n in range(3):
        rows += [gp[n * GB_W:(n + 1) * GB_W] for gp in groups]
    return jnp.concatenate(rows + [gg], axis=0)


def kernel(x, c, positions, w_ada, b_ada, w_in, sinks, w_branch_a, w_branch_b, w_o, ln1_g, ln1_b, w_gate_up, w_down, ln2_g, ln2_b, loss_target, m_w_ada, m_b_ada, m_w_in, m_sinks, m_w_branch_a, m_w_branch_b, m_w_o, m_ln1_g, m_ln1_b, m_w_gate_up, m_w_down, m_ln2_g, m_ln2_b, v_w_ada, v_b_ada, v_w_in, v_sinks, v_w_branch_a, v_w_branch_b, v_w_o, v_ln1_g, v_ln1_b, v_w_gate_up, v_w_down, v_ln2_g, v_ln2_b):
    xi, yi, ci = _coords()
    chip = 2 * xi + yi
    dev = 4 * xi + 2 * yi + ci
    NB, T, D = x.shape
    nchip, ndev = 4, 8
    ada_cols = w_ada.shape[2]

    ra, ro, rd = w_branch_a.shape[1], w_o.shape[1], w_down.shape[1]
    rowsh = jnp.concatenate([w_branch_a[0], w_o[0], w_down[0]], axis=0)
    halves = lambda a: a.reshape(a.shape[:-2] + (2, a.shape[-2] // 2, a.shape[-1]))
    tr = lambda a: jnp.swapaxes(a, -1, -2)
    shards = [halves(w.astype(BF16)) for w in (rowsh, w_branch_b[0])]
    gin = _Gather([_cast_own_shard(tr(w_in[0]), chip, name="cast_w_in")], chip, "w_in",
                  carriers=("gather_c", "ada_fwd", "gather_mod"), in_place=True)

    c_blk = jnp.zeros((8, D), F32).at[:NB].set(c)
    c_all = _allgather_small(c_blk, name="gather_c").reshape(ndev, 8, D)[:, :NB].reshape(ndev * NB, D)
    b_sh = lax.dynamic_slice(b_ada, (0, chip * ada_cols), (1, ada_cols))
    mod_part, c_act = _ada_fwd(c_all, w_ada[0], b_sh, name="ada_fwd")
    mod_g = _allgather_small(mod_part, name="gather_mod").reshape(nchip, 2, ndev * NB, ada_cols)[:, 0]
    mod_all = jnp.transpose(mod_g, (1, 0, 2)).reshape(ndev * NB, nchip * ada_cols)
    mod = lax.dynamic_slice(mod_all, (NB * dev, 0), (NB, nchip * ada_cols))

    (g_in,) = gin.result()
    w_in_f = _permute_in_rows(g_in.reshape(nchip * g_in.shape[1], D))
    mix = _Gather(shards, chip, "w_mix", carriers=("inproj_qkv", "attn_a_fwd"))
    ffn = _Gather([_cast_own_shard(w_gate_up[0], chip, name="cast_w_gate_up")], chip, "w_ffn",
                  carriers=("attn_a_fwd", "attn_b_fwd", "mix_out"), in_place=True)

    def rest_weights():
        g_rows, w_b_f = mix.result()
        return (g_rows[:, :ra].reshape(nchip * ra, D), w_b_f, g_rows[:, ra:ra + ro].reshape(nchip * ro, D),
                lambda: ffn.result()[0], g_rows[:, ra + ro:].reshape(nchip * rd, D))

    red = {}

    def hook(event, **g):
        if event == "ffn_grads":
            red["ffn"] = _ReduceScatter([halves(g["g_wgu"])], chip, ci, "ffn")
            red["ffn"].pair(carrier="out_proj_dw")
        elif event == "rest_grads":
            gr_rows = jnp.concatenate([g["g_wa"].reshape(nchip, ra, D), g["g_wo"].reshape(nchip, ro, D),
                                       g["g_wd"].reshape(nchip, rd, D)], axis=1)
            red["mix"] = _ReduceScatter([halves(gr_rows), halves(g["g_wb"])], chip, ci, "mix")
            red["ffn"].chips(carrier="attn_a_bwd")
            red["mix"].pair(carrier="attn_a_bwd")
        elif event == "attn_a_bwd_done":
            red["ffn"].halves(carrier="attn_b0_bwd")
            red["mix"].chips(carrier="attn_b0_bwd")
        elif event == "attn_b0_bwd_done":
            red["mix"].halves(carrier="attn_b1_bwd")
        elif event == "win_grads":
            gr_in = _unpermute_in_grads(g["g_win"])
            red["w_in"] = _ReduceScatter([halves(gr_in.reshape(nchip, gr_in.shape[0] // nchip, D))], chip, ci, "w_in")
            red["w_in"].pair(carrier="inproj_dx0")
        elif event == "inproj_dx0_done":
            red["w_in"].chips(carrier="inproj_dx1")
        elif event == "inproj_dx1_done":
            red["w_in"].halves(carrier="gather_small")

    res = _local_step(x, mod, positions, w_in_f, rest_weights, sinks[0], ln1_g, ln1_b, ln2_g, ln2_b, loss_target, hook)
    (g_rows_red, g_w_b), (g_w_gu,) = red["mix"].result(), red["ffn"].result()
    g_w_a, g_w_o, g_w_d = g_rows_red[:ra], g_rows_red[ra:ra + ro], g_rows_red[ra + ro:]

    small_rows = 24
    misc = jnp.zeros((1, D), F32).at[0, :A_Q_HEADS].set(res["dsink"]).at[0, A_Q_HEADS].set(jnp.sum(res["loss"]))
    small = jnp.concatenate([res["dmod"].reshape(NB * 6, D), jnp.sum(res["ln_grads"], axis=0), misc,
                             jnp.zeros((small_rows - NB * 6 - 5, D), F32)], axis=0)
    small_all = _allgather_small(small, name="gather_small").reshape(ndev, small_rows, D)
    (g_w_in,) = red["w_in"].result()
    dmod_all = small_all[:, :NB * 6].reshape(ndev * NB, 6 * D)
    sums = _sum_devices(small_all, name="sum_small")
    g_b_ada = (sums[0:6] + sums[6:12]).reshape(1, 6 * D)
    g_ln1_g, g_ln1_b, g_ln2_g, g_ln2_b = (sums[12 + n][None] for n in range(4))
    g_sinks = sums[16, :A_Q_HEADS][None]
    loss = sums[16, A_Q_HEADS]
    dmod_sh = lax.dynamic_slice(dmod_all, (0, chip * ada_cols), (ndev * NB, ada_cols))
    g_w_ada = _mm(c_act, dmod_sh, ta=True, name="ada_dw")

    names = ["w_ada", "b_ada", "w_in", "sinks", "w_branch_a", "w_branch_b", "w_o", "ln1_g", "ln1_b",
             "w_gate_up", "w_down", "ln2_g", "ln2_b"]
    ws = [w_ada, b_ada, w_in, sinks, w_branch_a, w_branch_b, w_o, ln1_g, ln1_b, w_gate_up, w_down, ln2_g, ln2_b]
    ms = [m_w_ada, m_b_ada, m_w_in, m_sinks, m_w_branch_a, m_w_branch_b, m_w_o, m_ln1_g, m_ln1_b, m_w_gate_up,
          m_w_down, m_ln2_g, m_ln2_b]
    vs = [v_w_ada, v_b_ada, v_w_in, v_sinks, v_w_branch_a, v_w_branch_b, v_w_o, v_ln1_g, v_ln1_b, v_w_gate_up,
          v_w_down, v_ln2_g, v_ln2_b]
    gs = [g_w_ada, g_b_ada, g_w_in, g_sinks, g_w_a, g_w_b, g_w_o, g_ln1_g, g_ln1_b, g_w_gu, g_w_d, g_ln2_g, g_ln2_b]
    grads, deltas, new_ms, new_vs = [], [], [], []
    for name, w, g, m, v in zip(names, ws, gs, ms, vs):
        flip = tr if name == "w_in" else (lambda a: a)
        w, m, v = flip(w), flip(m), flip(v)
        g2 = g.reshape(w.shape[-2:])
        d, nm, nv = _adamw(w, g2, m, v, name="adamw_" + name)
        grads.append(flip(g2.reshape(w.shape)))
        deltas.append(flip(d))
        new_ms.append(flip(nm))
        new_vs.append(flip(nv))
    return (loss, res["grad_x"], *grads, *deltas, *new_ms, *new_vs)
```

```python
import functools

import jax
import jax.numpy as jnp
from jax import lax
from jax.experimental import pallas as pl
from jax.experimental.pallas import tpu as pltpu

F32 = jnp.float32
BF16 = jnp.bfloat16
MESH = pl.DeviceIdType.MESH

HEAD_DIM = 64
LANES = 128
PAIR_W = 2 * HEAD_DIM
BLOCK = 128
A_Q_HEADS = 16
A_KV_HEADS = 2
A_WINDOW = 128
B_PATTERNS = ((128, 1), (512, 4), (2048, 16))
B_GROUP_HEADS = 8
QA_W = A_Q_HEADS * HEAD_DIM
KA_W = A_KV_HEADS * HEAD_DIM
GB_W = B_GROUP_HEADS * HEAD_DIM
QB_W = GB_W * len(B_PATTERNS)
A_W = QA_W + 2 * KA_W
VAR_W = 3 * GB_W
N_VAR = 1 + len(B_PATTERNS)
VAR_DIL = (1,) + tuple(r for _, r in B_PATTERNS)
A_BWD_SPLIT = 4
QKV_P = N_VAR * VAR_W
ROPE_THETA = 10000.0
LN_EPS = 1e-5
NEG_INF = -1e30
DEPTH = 1
ALPHA = (2 * DEPTH) ** 0.25
SCALE = HEAD_DIM ** -0.5

ADAM_LR, ADAM_B1, ADAM_B2, ADAM_EPS, ADAM_WD, ADAM_STEP = 0.001, 0.9, 0.999, 1e-08, 0.01, 10

VMEM_LIMIT_BYTES = 56 * 1024 * 1024
MM_TILE_BYTES = 36 * 1024 * 1024
MM_WHOLE_K = 4096


def _params(sem=None):
    return pltpu.CompilerParams(dimension_semantics=sem, vmem_limit_bytes=VMEM_LIMIT_BYTES)


_RIDES = {}


def _pcall(body, *, name, **kw):
    rides = _RIDES.pop(name, None)
    if rides is None:
        return pl.pallas_call(body, name=name, **kw)
    return _riding_call(body, rides, name=name, **kw)


def _copies(src_refs, dst_refs, send_sems, recv_sems, plan):
    x, y, c = lax.axis_index("x"), lax.axis_index("y"), lax.axis_index("c")
    remote = plan(x, y, c)
    nrem = len(remote)
    at = lambda ref, idx: ref.at[idx] if idx else ref

    def copy(a, n, landing):
        si, di, ri, peer = remote[n]
        return pltpu.make_async_remote_copy(
            src_ref=at(src_refs[a], si), dst_ref=at(dst_refs[a], ri if landing else di),
            send_sem=send_sems.at[a * nrem + n], recv_sem=recv_sems.at[a * nrem + n],
            device_id=peer, device_id_type=MESH)

    order = [(a, n) for a in range(len(dst_refs)) for n in range(nrem)]

    def start():
        for a, n in order:
            copy(a, n, False).start()

    def wait():
        for a, n in order:
            copy(a, n, True).wait_recv()
        for a, n in order:
            copy(a, n, False).wait_send()

    return start, wait


class _Ride:
    def __init__(self, srcs, dsts, plan, dst_inits=None):
        self.srcs, self.dsts, self.plan, self.dst_inits, self.out = srcs, dsts, plan, dst_inits, None


def _riding_call(body, rides, *, name, in_specs, out_specs, out_shape, grid=(), scratch_shapes=(), **kw):
    single = not isinstance(out_specs, (list, tuple))
    out_specs = [out_specs] if single else list(out_specs)
    out_shape = [out_shape] if single else list(out_shape)
    n_in, n_out, n_scr = len(in_specs), len(out_specs), len(scratch_shapes)
    xin, xdsts, sems, aliases, layout = [], [], [], {}, []
    for ride in rides:
        srcs = ride.srcs() if callable(ride.srcs) else ride.srcs
        inits = ride.dst_inits() if callable(ride.dst_inits) else ride.dst_inits
        na, nrem = len(ride.dsts), len(ride.plan(0, 0, 0))
        src_at = len(xin) if srcs is not None else None
        xin += list(srcs) if srcs is not None else []
        if inits is not None:
            aliases.update({n_in + len(xin) + a: n_out + len(xdsts) + a for a in range(na)})
            xin += list(inits)
        layout.append((src_at, len(xdsts), na))
        xdsts += list(ride.dsts)
        sems += [pltpu.SemaphoreType.DMA((na * nrem,)), pltpu.SemaphoreType.DMA((na * nrem,))]

    def wrapped(*refs):
        ins, xins = refs[:n_in], refs[n_in:n_in + len(xin)]
        outs = refs[n_in + len(xin):n_in + len(xin) + n_out]
        xouts = refs[n_in + len(xin) + n_out:n_in + len(xin) + n_out + len(xdsts)]
        scr = refs[n_in + len(xin) + n_out + len(xdsts):]
        rounds = []
        for k, (ride, (src_at, dst_at, na)) in enumerate(zip(rides, layout)):
            dsts = xouts[dst_at:dst_at + na]
            srcs = dsts if src_at is None else xins[src_at:src_at + na]
            rounds.append(_copies(srcs, dsts, scr[n_scr + 2 * k], scr[n_scr + 2 * k + 1], ride.plan))
        ids = [pl.program_id(a) for a in range(len(grid))]
        first = functools.reduce(jnp.logical_and, [i == 0 for i in ids], True)
        last = functools.reduce(jnp.logical_and, [i == g - 1 for i, g in zip(ids, grid)], True)

        def start_all():
            for start, _ in rounds:
                start()

        def wait_all():
            for _, wait in rounds:
                wait()

        start_all() if not grid else pl.when(first)(start_all)
        body(*ins, *outs, *scr[:n_scr])
        wait_all() if not grid else pl.when(last)(wait_all)

    hbm = pl.BlockSpec(memory_space=pl.ANY)
    gridkw = dict(grid=grid) if grid else {}

    def run(*args):
        res = pl.pallas_call(
            wrapped, name=name, in_specs=list(in_specs) + [hbm] * len(xin),
            out_specs=out_specs + [hbm] * len(xdsts), out_shape=out_shape + xdsts,
            scratch_shapes=list(scratch_shapes) + sems, input_output_aliases=aliases,
            compiler_params=_params(("arbitrary",) * len(grid) if grid else None), **gridkw,
        )(*args, *xin)
        for ride, (_, dst_at, na) in zip(rides, layout):
            ride.out = list(res[n_out + dst_at:n_out + dst_at + na])
        return res[0] if single else list(res[:n_out])

    return run


def _pick(n, target, quantum=128):
    t = (min(target, n) // quantum) * quantum
    while t >= quantum:
        if n % t == 0:
            return t
        t -= quantum
    return n


def _mm(a, b, *, name, ta=False, tb=False, b3=False, out3=0, out_dtype=F32, add=None, tm=1024, tn=1536, tk=1536):
    if ta:
        K, M = a.shape
    else:
        M, K = a.shape
    if b3 and tb:
        Nn, K2, tk = b.shape[1], b.shape[0] * b.shape[2], b.shape[2]
    elif b3:
        K2, Nn, tn = b.shape[1], b.shape[0] * b.shape[2], b.shape[2]
    elif tb:
        Nn, K2 = b.shape
    else:
        K2, Nn = b.shape
    assert K == K2, (a.shape, b.shape)
    if out3:
        tn = Nn // out3
    tm, tn, tk = _pick(M, tm), _pick(Nn, tn), _pick(K, tk)
    if not (b3 and tb) and K <= MM_WHOLE_K:
        tk = K
        fits = lambda: 4 * tk * (tm + tn) + 8 * tm * tn * (2 if add is not None else 1) <= MM_TILE_BYTES
        while not fits():
            if (tm >= tn or b3 or out3) and tm > 256:
                tm = _pick(M, tm - 128)
            elif not (b3 or out3) and tn > 256:
                tn = _pick(Nn, tn - 128)
            else:
                break
    nk = K // tk
    j_outer = K * Nn + (Nn // tn) * M * K < M * K + (M // tm) * K * Nn
    dn = (((0 if ta else 1,), (1 if tb else 0,)), ((), ()))

    def body(*refs):
        refs = list(refs)
        a_ref, b_ref = refs[:2]
        add_ref = refs[2] if add is not None else None
        o_ref = refs[3] if add is not None else refs[2]
        part = lax.dot_general(a_ref[...].astype(BF16), b_ref[...].astype(BF16), dn, preferred_element_type=F32)

        def finish(r):
            if add is not None:
                r = r + add_ref[...]
            o_ref[...] = r.astype(out_dtype)

        if nk == 1:
            finish(part)
            return
        acc = refs[-1]
        k = pl.program_id(2)

        @pl.when(k == 0)
        def _():
            acc[...] = part

        @pl.when(k > 0)
        def _():
            acc[...] += part

        @pl.when(k == nk - 1)
        def _():
            finish(acc[...])

    def spec(shape, index):
        return pl.BlockSpec(shape, (lambda j, i, k: index(i, j, k)) if j_outer else index)

    a_spec = spec((tk, tm), lambda i, j, k: (k, i)) if ta else spec((tm, tk), lambda i, j, k: (i, k))
    if b3 and tb:
        b_spec = spec((None, tn, tk), lambda i, j, k: (k, j, 0))
    elif b3:
        b_spec = spec((None, tk, tn), lambda i, j, k: (j, k, 0))
    elif tb:
        b_spec = spec((tn, tk), lambda i, j, k: (j, k))
    else:
        b_spec = spec((tk, tn), lambda i, j, k: (k, j))
    if out3:
        o_spec = spec((None, tm, tn), lambda i, j, k: (j, i, 0))
    else:
        o_spec = spec((tm, tn), lambda i, j, k: (i, j))
    ins, specs = [a, b], [a_spec, b_spec]
    if add is not None:
        ins.append(add)
        specs.append(o_spec)
    grid = (Nn // tn, M // tm, nk) if j_outer else (M // tm, Nn // tn, nk)
    return _pcall(
        body, name=name, grid=grid, in_specs=specs, out_specs=o_spec,
        out_shape=jax.ShapeDtypeStruct((out3, M, tn) if out3 else (M, Nn), out_dtype),
        scratch_shapes=[pltpu.VMEM((tm, tn), F32)] if nk > 1 else [],
        compiler_params=_params(("parallel", "parallel", "arbitrary")),
    )(*ins)


def _mm_multi(a_list, b_list, *, name, M, T=None, add=None, out_dtype=F32, tm=512, post=None):
    tm = _pick(T or M, tm)
    ns = len(a_list)
    dils = [a[1] if isinstance(a, tuple) else 0 for a in a_list]
    a_arrs = [a[0] if isinstance(a, tuple) else a for a in a_list]
    widths = [a.shape[-1] // max(r, 1) for a, r in zip(a_arrs, dils)]
    b_arrs, b_specs = [], []
    for b in b_list:
        arr, shp, idx = b if isinstance(b, tuple) else (b, b.shape, (0, 0))
        b_arrs.append(arr)
        b_specs.append(pl.BlockSpec(shp, lambda i, idx=idx: idx))
    Nn = b_specs[0].block_shape[1]
    dn = (((1,), (0,)), ((), ()))
    nmm = 2 * ns + (1 if add is not None else 0)
    p_arrs, p_in_specs, p_out_specs, p_out_shape, p_fn = post or ([], [], None, None, None)
    nin = nmm + len(p_arrs)
    nout = len(p_out_specs) if post else 1

    def body(*refs):
        a_refs, b_refs, scr = refs[:ns], refs[ns:2 * ns], list(refs[nin + nout:])
        acc = None
        for a_ref, b_ref, r in zip(a_refs, b_refs, dils):
            av = _from_view(a_ref, scr.pop(0), r) if r > 1 else a_ref[...]
            part = lax.dot_general(av.astype(BF16), b_ref[...], dn, preferred_element_type=F32)
            acc = part if acc is None else acc + part
        if add is not None:
            acc = acc + refs[2 * ns][...]
        if post:
            p_fn(acc, refs[nmm:nin], refs[nin:nin + nout])
        else:
            refs[nin][...] = acc.astype(out_dtype)

    tpe = (T or M) // tm
    a_specs = [pl.BlockSpec((None, tm // r, r * w), lambda i: (i // tpe, i % tpe, 0)) if r
               else pl.BlockSpec((tm, w), lambda i: (i, 0)) for r, w in zip(dils, widths)]
    o_spec = pl.BlockSpec((tm, Nn), lambda i: (i, 0))
    specs = a_specs + b_specs
    ins = a_arrs + b_arrs
    if add is not None:
        specs.append(o_spec)
        ins.append(add)
    scratch = [pltpu.VMEM((w // LANES, tm, LANES), F32) for r, w in zip(dils, widths) if r > 1]
    return _pcall(body, name=name, grid=(M // tm,), in_specs=specs + list(p_in_specs),
                  out_specs=list(p_out_specs) if post else o_spec, scratch_shapes=scratch,
                  out_shape=list(p_out_shape) if post else jax.ShapeDtypeStruct((M, Nn), out_dtype),
                  compiler_params=_params(("arbitrary",) if post else ("parallel",)))(*ins, *p_arrs)


def _dw_view(d3, u, r, *, name, tk=1024):
    NB, tsub, rw = d3.shape
    W, T, D = rw // r, tsub * r, u.shape[1]
    tk = _pick(T, tk)
    tpe, nk = T // tk, NB * T // tk

    def body(d_ref, u_ref, o_ref, acc, scr):
        k = pl.program_id(0)
        dv = _from_view(d_ref, scr, r).astype(BF16)
        part = lax.dot_general(dv, u_ref[...], _TN, preferred_element_type=F32)

        @pl.when(k == 0)
        def _():
            acc[...] = part

        @pl.when(k > 0)
        def _():
            acc[...] += part

        @pl.when(k == nk - 1)
        def _():
            o_ref[...] = acc[...].astype(o_ref.dtype)

    return _pcall(
        body, name=name, grid=(nk,),
        in_specs=[pl.BlockSpec((None, tk // r, rw), lambda k: (k // tpe, k % tpe, 0)), pl.BlockSpec((tk, D), lambda k: (k, 0))],
        out_specs=pl.BlockSpec((W, D), lambda k: (0, 0)), out_shape=jax.ShapeDtypeStruct((W, D), BF16),
        scratch_shapes=[pltpu.VMEM((W, D), F32), pltpu.VMEM((W // LANES, tk, LANES), F32)],
        compiler_params=_params(("arbitrary",)))(d3, u)


def _lane(shape):
    return lax.broadcasted_iota(jnp.int32, shape, len(shape) - 1)


def _rot_half(v):
    w = v.shape[-1]
    first = (_lane(v.shape) % HEAD_DIM) < (HEAD_DIM // 2)
    return jnp.where(first, pltpu.roll(v, w - HEAD_DIM // 2, v.ndim - 1), pltpu.roll(v, HEAD_DIM // 2, v.ndim - 1))


def _widen(t, w):
    return t if w == t.shape[-1] else jnp.concatenate([t] * (w // t.shape[-1]), axis=-1)


def _unrope(v, cos, sins):
    w = v.shape[-1]
    return v * _widen(cos, w) - _rot_half(v) * _widen(sins, w)


def _rope_tables(positions):
    half = HEAD_DIM // 2
    inv = ROPE_THETA ** (-jnp.arange(half, dtype=F32) / half)
    ang = positions.astype(F32)[..., None] * inv
    cos, sin = jnp.cos(ang), jnp.sin(ang)
    cosf = jnp.concatenate([cos, cos, cos, cos], axis=-1)
    sins = jnp.concatenate([-sin, sin, -sin, sin], axis=-1)
    n = positions.shape[0] * positions.shape[1]
    return cosf.reshape(n, PAIR_W), sins.reshape(n, PAIR_W)


def _inproj(x2, scale, shift, w, cosf, sins, flags, *, T, name):
    N, D = x2.shape
    tm, tn = _pick(T, 512), VAR_W
    tpe = T // tm

    def body(x_ref, sc_ref, sh_ref, w_ref, c_ref, s_ref, f_ref, *outs):
        o_refs, u_ref = outs[:N_VAR], outs[N_VAR]
        j = pl.program_id(1)

        @pl.when(j == 0)
        def _():
            u_ref[...] = (x_ref[...] * (1.0 + sc_ref[0]) + sh_ref[0]).astype(BF16)

        acc = lax.dot_general(u_ref[...], w_ref[...], (((1,), (1,)), ((), ())), preferred_element_type=F32)
        fl = f_ref[...]
        ce = 1.0 + (_widen(c_ref[...], tn) - 1.0) * fl
        se = _widen(s_ref[...], tn) * fl
        res = acc * ce + _rot_half(acc) * se
        for v in range(N_VAR):
            @pl.when(j == v)
            def _(v=v):
                _to_view(res, o_refs[v], outs[N_VAR + 1], VAR_DIL[v])

    ex = pl.BlockSpec((1, 1, D), lambda i, j: (i // tpe, 0, 0))
    tab = pl.BlockSpec((tm, PAIR_W), lambda i, j: (i, 0))
    keep = lambda w_: pl.BlockSpec((tm, w_), lambda i, j: (i, 0))
    vspec = lambda r: pl.BlockSpec((None, tm // r, r * tn), lambda i, j: (i // tpe, i % tpe, 0))
    vshape = lambda r: jax.ShapeDtypeStruct((N // T, T // r, r * tn), BF16)
    return _pcall(
        body, name=name, grid=(N // tm, N_VAR),
        in_specs=[keep(D), ex, ex, pl.BlockSpec((tn, D), lambda i, j: (j, 0)), tab, tab,
                  pl.BlockSpec((1, tn), lambda i, j: (0, j))],
        out_specs=[vspec(r) for r in VAR_DIL] + [keep(D)],
        out_shape=[vshape(r) for r in VAR_DIL] + [jax.ShapeDtypeStruct((N, D), BF16)],
        scratch_shapes=[pltpu.VMEM((tn // LANES, tm, LANES), F32)],
        compiler_params=_params(("parallel", "arbitrary")),
    )(x2, scale, shift, w, cosf, sins, flags)


class _Geom:
    def __init__(self, g):
        if g is None:
            self.r, self.nq, self.n_back, self.sink = 1, A_Q_HEADS, A_WINDOW - 1, True
            self.qw, self.kw = QA_W, KA_W
            self.qidx = lambda j: 0
            self.kidx = lambda j: QA_W // KA_W
            self.vidx = lambda j: QA_W // KA_W + 1
        else:
            window, r = B_PATTERNS[g]
            self.r, self.nq, self.n_back, self.sink = r, B_GROUP_HEADS, window // r, False
            self.qw, self.kw = GB_W, GB_W
            self.qidx = lambda j: 3 * j
            self.kidx = lambda j: 3 * j + 1
            self.vidx = lambda j: 3 * j + 2
        self.ntile = self.qw // PAIR_W


def _stack_heads(t, scale=None):
    first = _lane(t.shape) < HEAD_DIM
    z = jnp.zeros_like(t)
    if scale is not None:
        t = t * jnp.asarray(scale, t.dtype)
    return jnp.concatenate([jnp.where(first, t, z), jnp.where(first, z, t)], axis=0)


def _unstack_heads(v2):
    return jnp.where(_lane((BLOCK, PAIR_W)) < HEAD_DIM, v2[:BLOCK], v2[BLOCK:])


def _dup_head(t, kh):
    tf = t.astype(F32)
    keep = (_lane(t.shape) < HEAD_DIM) if kh == 0 else (_lane(t.shape) >= HEAD_DIM)
    return jnp.where(keep, tf, pltpu.roll(tf, HEAD_DIM, 1)).astype(t.dtype)


def _fold_heads(t):
    return t + pltpu.roll(t, HEAD_DIM, 1)


def _band_mask(rows, i, n_back, single):
    nkeys = BLOCK if single else 2 * BLOCK
    qi = jnp.bitwise_and(lax.broadcasted_iota(jnp.int32, (rows, nkeys), 0), BLOCK - 1)
    ki = lax.broadcasted_iota(jnp.int32, (rows, nkeys), 1)
    if single:
        return qi >= ki
    dist = qi + BLOCK - ki
    return jnp.logical_and(jnp.logical_and(dist >= 0, dist <= n_back), jnp.logical_or(ki >= BLOCK, i > 0))


def _sink_slot(rows):
    qi = jnp.bitwise_and(lax.broadcasted_iota(jnp.int32, (rows, 2 * BLOCK), 0), BLOCK - 1)
    return qi == lax.broadcasted_iota(jnp.int32, (rows, 2 * BLOCK), 1)


def _sink_scores(rows, sinks):
    blk = lax.broadcasted_iota(jnp.int32, (rows, 2 * BLOCK), 0) // BLOCK
    out = jnp.full((rows, 2 * BLOCK), sinks[-1], F32)
    for b in range(len(sinks) - 2, -1, -1):
        out = jnp.where(blk == b, sinks[b], out)
    return out


def _softmax_parts(s, valid, sinks):
    s = jnp.where(valid, s, NEG_INF)
    if sinks is not None:
        slot = _sink_slot(s.shape[0])
        s = jnp.where(slot, _sink_scores(s.shape[0], sinks), s)
    m = jnp.max(s, axis=1, keepdims=True)
    p = jnp.exp(s - m)
    den = jnp.sum(p, axis=1, keepdims=True)
    if sinks is not None:
        p = jnp.where(slot, 0.0, p)
    return p, m, den


_NT = (((1,), (1,)), ((), ()))
_TN = (((0,), (0,)), ((), ()))


def _rows2(prev_ref, cur_ref, cs, single=False):
    if single:
        return cur_ref[0, :, cs]
    return jnp.concatenate([prev_ref[0, :, cs], cur_ref[0, :, cs]], axis=0)


def _sink_scalars(sink_ref, first, nblocks):
    return [sink_ref[first + b] for b in range(nblocks)]


def _tile(t):
    return slice(t * PAIR_W, (t + 1) * PAIR_W)


def _attn_fwd(qkv, sinks, g, *, NB, T, name):
    geo = _Geom(g)
    r, qw, kw, ntile = geo.r, geo.qw, geo.kw, geo.ntile
    tsub = T // r
    nblk = tsub // BLOCK
    qkv3 = qkv.reshape(NB, tsub, r * VAR_W)
    out_dtype = BF16 if g is None else F32
    tiles_per_kv = ntile // A_KV_HEADS

    single = nblk == 1

    def body(q_ref, kp_ref, kc_ref, vp_ref, vc_ref, sink_ref, o_ref, l_ref):
        i = pl.program_id(2)
        if geo.sink:
            kall, vall = _rows2(kp_ref, kc_ref, _tile(0)), _rows2(vp_ref, vc_ref, _tile(0))
            kdup = [_dup_head(kall, kh) for kh in range(A_KV_HEADS)]
            vdup = [_dup_head(vall, kh) for kh in range(A_KV_HEADS)]
            tiles = [[t] for t in range(ntile)]
            q2s = [_stack_heads(q_ref[0, :, _tile(t)], SCALE) for t in range(ntile)]
            kks = [kdup[t // tiles_per_kv] for t in range(ntile)]
            vvs = [vdup[t // tiles_per_kv] for t in range(ntile)]
            sinkcols = [_sink_scalars(sink_ref, 2 * t, 2) for t in range(ntile)]
        else:
            tiles = [[t] for t in range(ntile)]
            q2s = [_stack_heads(q_ref[0, :, _tile(t)], SCALE) for t in range(ntile)]
            kks = [_rows2(kp_ref, kc_ref, _tile(t), single) for t in range(ntile)]
            vvs = [_rows2(vp_ref, vc_ref, _tile(t), single) for t in range(ntile)]
            sinkcols = [None] * ntile
        valid = _band_mask(q2s[0].shape[0], i, geo.n_back, single)
        ss = [lax.dot_general(q2, kk, _NT, preferred_element_type=F32) for q2, kk in zip(q2s, kks)]
        parts = [_softmax_parts(s, valid, sc) for s, sc in zip(ss, sinkcols)]
        o2s = [jnp.dot(p.astype(BF16), vv, preferred_element_type=F32) / den for (p, m, den), vv in zip(parts, vvs)]
        for ts, o2, (p, m, den) in zip(tiles, o2s, parts):
            lse2 = jnp.broadcast_to(m + jnp.log(den), (o2.shape[0], PAIR_W))
            for n, t in enumerate(ts):
                rows = slice(2 * BLOCK * n, 2 * BLOCK * (n + 1))
                o_ref[0, :, _tile(t)] = _unstack_heads(o2[rows]).astype(out_dtype)
                l_ref[0, :, _tile(t)] = _unstack_heads(lse2[rows])

    prev = lambda i: jnp.maximum(i - 1, 0)
    in_specs = [
        pl.BlockSpec((1, BLOCK, qw), lambda b, j, i: (b, i, geo.qidx(j))),
        pl.BlockSpec((1, BLOCK, kw), lambda b, j, i: (b, prev(i), geo.kidx(j))),
        pl.BlockSpec((1, BLOCK, kw), lambda b, j, i: (b, i, geo.kidx(j))),
        pl.BlockSpec((1, BLOCK, kw), lambda b, j, i: (b, prev(i), geo.vidx(j))),
        pl.BlockSpec((1, BLOCK, kw), lambda b, j, i: (b, i, geo.vidx(j))),
        pl.BlockSpec(memory_space=pltpu.SMEM),
    ]
    o_spec = pl.BlockSpec((1, BLOCK, qw), lambda b, j, i: (b, i, j))
    shape = (NB, tsub, r * qw)
    o, lse = _pcall(
        body, name=name, grid=(NB, r, nblk), in_specs=in_specs, out_specs=[o_spec, o_spec],
        out_shape=[jax.ShapeDtypeStruct(shape, out_dtype), jax.ShapeDtypeStruct(shape, F32)],
        compiler_params=_params(("parallel", "parallel", "arbitrary")),
    )(qkv3, qkv3, qkv3, qkv3, qkv3, sinks)
    return o, lse


def _attn_fwd_b(qkvs, *, NB, T, name):
    geos = [_Geom(g) for g in range(len(B_PATTERNS))]
    steps = T // BLOCK
    nt = GB_W // PAIR_W
    ng = len(geos)

    def where(geo, s):
        nblk = T // geo.r // BLOCK
        return s // steps, (s % steps) // nblk, (s % steps) % nblk

    def body(*refs):
        ins, outs = refs[:5 * ng], refs[5 * ng:]
        s = pl.program_id(0)
        q2s, kks, vvs, valids = [], [], [], []
        for n, geo in enumerate(geos):
            q_ref, kp_ref, kc_ref, vp_ref, vc_ref = ins[5 * n:5 * n + 5]
            single = T // geo.r // BLOCK == 1
            valid = _band_mask(2 * BLOCK, where(geo, s)[2], geo.n_back, single)
            for t in range(nt):
                q2s.append(_stack_heads(q_ref[0, :, _tile(t)], SCALE))
                kks.append(_rows2(kp_ref, kc_ref, _tile(t), single))
                vvs.append(_rows2(vp_ref, vc_ref, _tile(t), single))
                valids.append(valid)
        ss = [lax.dot_general(q2, kk, _NT, preferred_element_type=F32) for q2, kk in zip(q2s, kks)]
        parts = [_softmax_parts(sc, valid, None) for sc, valid in zip(ss, valids)]
        o2s = [jnp.dot(p.astype(BF16), vv, preferred_element_type=F32) / den for (p, m, den), vv in zip(parts, vvs)]
        for n in range(ng):
            o_ref, l_ref = outs[2 * n], outs[2 * n + 1]
            for t in range(nt):
                o2, (p, m, den) = o2s[n * nt + t], parts[n * nt + t]
                o_ref[0, :, _tile(t)] = _unstack_heads(o2)
                l_ref[0, :, _tile(t)] = _unstack_heads(jnp.broadcast_to(m + jnp.log(den), (2 * BLOCK, PAIR_W)))

    in_specs, ins, out_specs, out_shape = [], [], [], []
    for geo, qkv in zip(geos, qkvs):
        tsub = T // geo.r
        pos = lambda s, geo=geo: where(geo, s)
        prev = lambda i: jnp.maximum(i - 1, 0)
        blk = lambda col, back, pos=pos: pl.BlockSpec(
            (1, BLOCK, GB_W), lambda s: (pos(s)[0], prev(pos(s)[2]) if back else pos(s)[2], col(pos(s)[1])))
        in_specs += [blk(geo.qidx, False), blk(geo.kidx, True), blk(geo.kidx, False), blk(geo.vidx, True),
                     blk(geo.vidx, False)]
        ins += [qkv.reshape(NB, tsub, geo.r * VAR_W)] * 5
        out_specs += [blk(lambda j: j, False)] * 2
        out_shape += [jax.ShapeDtypeStruct((NB, tsub, geo.r * GB_W), F32)] * 2
    res = _pcall(body, name=name, grid=(NB * steps,), in_specs=in_specs, out_specs=out_specs, out_shape=out_shape,
                 compiler_params=_params(("arbitrary",)))(*ins)
    return [(res[2 * n], res[2 * n + 1]) for n in range(ng)]


def _attn_bwd(qkv, do, lse, dlse, cosf, sins, sinks, g, *, NB, T, name):
    geo = _Geom(g)
    r, qw, kw, ntile = geo.r, geo.qw, geo.kw, geo.ntile
    tsub = T // r
    nblk = tsub // BLOCK
    view = lambda a, w: a.reshape(NB, tsub, r * w)
    has_dlse = dlse is not None
    tiles_per_kv = ntile // A_KV_HEADS

    single = nblk == 1
    krows = BLOCK if single else 2 * BLOCK
    nsteps = 1 if single else nblk + 1

    def grads(q2s, kks, vvs, do2s, i, lserows, sinkcols, dlrows):
        nrow = q2s[0].shape[0]
        ki = lax.broadcasted_iota(jnp.int32, (krows, nrow), 0)
        qi = jnp.bitwise_and(lax.broadcasted_iota(jnp.int32, (krows, nrow), 1), BLOCK - 1)
        if single:
            valid = qi >= ki
        else:
            dist = qi + BLOCK - ki
            valid = jnp.logical_and(jnp.logical_and(dist >= 0, dist <= geo.n_back), jnp.logical_or(ki >= BLOCK, i > 0))
        sts = [lax.dot_general(kk, q2, _NT, preferred_element_type=F32) for q2, kk in zip(q2s, kks)]
        dpts = [lax.dot_general(vv, do2, _NT, preferred_element_type=F32) for do2, vv in zip(do2s, vvs)]
        pts, dsts, sks = [], [], []
        for st, dpt, ls, sc, dl in zip(sts, dpts, lserows, sinkcols, dlrows):
            sv = jnp.where(valid, st, NEG_INF)
            if sc is not None:
                slot = ki == qi
                blk = lax.broadcasted_iota(jnp.int32, (krows, nrow), 1) // BLOCK
                sink = jnp.full((krows, nrow), sc[-1], F32)
                for b in range(len(sc) - 2, -1, -1):
                    sink = jnp.where(blk == b, sc[b], sink)
                sv = jnp.where(slot, sink, sv)
                dpt = jnp.where(slot, 0.0, dpt)
            pt = jnp.exp(sv - ls)
            delta = jnp.sum(pt * dpt, axis=0, keepdims=True)
            if dl is not None:
                delta = delta - dl
            dst = pt * (dpt - delta)
            if sc is not None:
                cols = lambda a, b: a[:, b * BLOCK:(b + 1) * BLOCK]
                sks.append([jnp.sum(jnp.where(cols(slot, b), cols(dst, b), 0.0)) for b in range(len(sc))])
                dst, pt = jnp.where(slot, 0.0, dst), jnp.where(slot, 0.0, pt)
            else:
                sks.append(None)
            pts.append(pt.astype(BF16))
            dsts.append(dst.astype(BF16))
        dq2s = [lax.dot_general(dst, kk, _TN, preferred_element_type=F32) * SCALE for dst, kk in zip(dsts, kks)]
        dkks = [jnp.dot(dst, q2, preferred_element_type=F32) for dst, q2 in zip(dsts, q2s)]
        dvvs = [jnp.dot(pt, do2, preferred_element_type=F32) for pt, do2 in zip(pts, do2s)]
        return dq2s, dkks, dvvs, sks

    def stat_row(t):
        tt = t.T
        return jnp.concatenate([tt[0:1, :], tt[HEAD_DIM:HEAD_DIM + 1, :]], axis=1)

    def body(*refs):
        it = iter(refs)
        q_ref, kp_ref, kc_ref, vp_ref, vc_ref, do_ref, l_ref = (next(it) for _ in range(7))
        dl_ref = next(it) if has_dlse else None
        c_ref, s_ref, sink_ref, o_ref, ds_ref, dq_s, dk_s, dv_s, car_q, car_k, car_v = (next(it) for _ in range(11))
        b, j, i = pl.program_id(0), pl.program_id(1), pl.program_id(2)

        @pl.when(jnp.logical_and(b == 0, jnp.logical_and(j == 0, i == 0)))
        def _():
            ds_ref[...] = jnp.zeros_like(ds_ref)

        def compute():
            if geo.sink:
                kall, vall = _rows2(kp_ref, kc_ref, _tile(0)), _rows2(vp_ref, vc_ref, _tile(0))
                tps = tiles_per_kv // A_BWD_SPLIT
                nb = 2 * tps
                tiles = [[kh * tiles_per_kv + s_ * tps + t for t in range(tps)]
                         for kh in range(A_KV_HEADS) for s_ in range(A_BWD_SPLIT)]
                kdup = [_dup_head(kall, kh) for kh in range(A_KV_HEADS)]
                vdup = [_dup_head(vall, kh) for kh in range(A_KV_HEADS)]
                cat = lambda f, ts: jnp.concatenate([f(t) for t in ts], axis=0)
                dq2s, dkks, dvvs, sks = grads(
                    [cat(lambda t: _stack_heads(q_ref[0, :, _tile(t)], SCALE), ts) for ts in tiles],
                    [kdup[n // A_BWD_SPLIT] for n in range(len(tiles))],
                    [vdup[n // A_BWD_SPLIT] for n in range(len(tiles))],
                    [cat(lambda t: _stack_heads(do_ref[0, :, _tile(t)]), ts) for ts in tiles], i,
                    [jnp.concatenate([stat_row(l_ref[0, :, _tile(t)]) for t in ts], axis=1) for ts in tiles],
                    [_sink_scalars(sink_ref, 2 * ts[0], nb) for ts in tiles], [None] * len(tiles))
                lane1 = _lane((1, PAIR_W))
                dsink = jnp.zeros((1, PAIR_W), F32)
                for ts, dq2, sk in zip(tiles, dq2s, sks):
                    for n, t in enumerate(ts):
                        dq_s[:, _tile(t)] = _unstack_heads(dq2[2 * BLOCK * n:2 * BLOCK * (n + 1)])
                    for bb in range(nb):
                        dsink = dsink + jnp.where(lane1 == 2 * ts[0] + bb, sk[bb], 0.0)
                per_kv = lambda parts, kh: functools.reduce(jnp.add, parts[kh * A_BWD_SPLIT:(kh + 1) * A_BWD_SPLIT])
                second = _lane((krows, PAIR_W)) >= HEAD_DIM
                dk_s[...] = jnp.where(second, _fold_heads(per_kv(dkks, 1)), _fold_heads(per_kv(dkks, 0)))
                dv_s[...] = jnp.where(second, _fold_heads(per_kv(dvvs, 1)), _fold_heads(per_kv(dvvs, 0)))
                ds_ref[0:1, :] += dsink
            else:
                dq2s, dkks, dvvs, _ = grads(
                    [_stack_heads(q_ref[0, :, _tile(t)], SCALE) for t in range(ntile)],
                    [_rows2(kp_ref, kc_ref, _tile(t), single) for t in range(ntile)],
                    [_rows2(vp_ref, vc_ref, _tile(t), single) for t in range(ntile)],
                    [_stack_heads(do_ref[0, :, _tile(t)]) for t in range(ntile)], i,
                    [stat_row(l_ref[0, :, _tile(t)]) for t in range(ntile)], [None] * ntile,
                    [stat_row(dl_ref[0, :, _tile(t)]) for t in range(ntile)])
                for t in range(ntile):
                    dq_s[:, _tile(t)] = _unstack_heads(dq2s[t])
                    dk_s[0:krows, _tile(t)] = dkks[t]
                    dv_s[0:krows, _tile(t)] = dvvs[t]

        def emit(dq, dk, dv):
            cos, sn = c_ref[0], s_ref[0]
            o_ref[0, :, 0:qw] = _unrope(dq, cos, sn).astype(BF16)
            o_ref[0, :, qw:qw + kw] = _unrope(dk, cos, sn).astype(BF16)
            o_ref[0, :, qw + kw:qw + 2 * kw] = dv.astype(BF16)
            if qw + 2 * kw < VAR_W:
                o_ref[0, :, qw + 2 * kw:VAR_W] = jnp.zeros((BLOCK, VAR_W - qw - 2 * kw), BF16)

        if single:
            compute()
            emit(dq_s[...], dk_s[0:BLOCK, :], dv_s[0:BLOCK, :])
            return

        @pl.when(i == 0)
        def _():
            car_q[...] = jnp.zeros_like(car_q)
            car_k[...] = jnp.zeros_like(car_k)
            car_v[...] = jnp.zeros_like(car_v)

        @pl.when(i == nblk)
        def _():
            dk_s[...] = jnp.zeros_like(dk_s)
            dv_s[...] = jnp.zeros_like(dv_s)

        pl.when(i < nblk)(compute)
        emit(car_q[...], car_k[...] + dk_s[0:BLOCK, :], car_v[...] + dv_s[0:BLOCK, :])
        car_q[...] = dq_s[...]
        car_k[...] = dk_s[BLOCK:2 * BLOCK, :]
        car_v[...] = dv_s[BLOCK:2 * BLOCK, :]

    cur = lambda i: jnp.minimum(i, nblk - 1)
    prv = lambda i: jnp.maximum(jnp.minimum(i, nblk - 1) - 1, 0)
    outb = lambda i: jnp.maximum(i - 1, 0)
    qrow = pl.BlockSpec((1, BLOCK, qw), lambda b, j, i: (b, cur(i), j))
    in_specs = [
        pl.BlockSpec((1, BLOCK, qw), lambda b, j, i: (b, cur(i), geo.qidx(j))),
        pl.BlockSpec((1, BLOCK, kw), lambda b, j, i: (b, prv(i), geo.kidx(j))),
        pl.BlockSpec((1, BLOCK, kw), lambda b, j, i: (b, cur(i), geo.kidx(j))),
        pl.BlockSpec((1, BLOCK, kw), lambda b, j, i: (b, prv(i), geo.vidx(j))),
        pl.BlockSpec((1, BLOCK, kw), lambda b, j, i: (b, cur(i), geo.vidx(j))),
        qrow, qrow,
    ]
    ins = [view(qkv, VAR_W)] * 5 + [view(do, qw), view(lse, qw)]
    if has_dlse:
        in_specs.append(qrow)
        ins.append(view(dlse, qw))
    in_specs += [
        pl.BlockSpec((1, BLOCK, PAIR_W), lambda b, j, i: (b, outb(i), j)),
        pl.BlockSpec((1, BLOCK, PAIR_W), lambda b, j, i: (b, outb(i), j)),
        pl.BlockSpec(memory_space=pltpu.SMEM),
    ]
    ins += [view(cosf, PAIR_W), view(sins, PAIR_W), sinks]
    scratch = [pltpu.VMEM((BLOCK, qw), F32), pltpu.VMEM((2 * BLOCK, kw), F32), pltpu.VMEM((2 * BLOCK, kw), F32),
               pltpu.VMEM((BLOCK, qw), F32), pltpu.VMEM((BLOCK, kw), F32), pltpu.VMEM((BLOCK, kw), F32)]
    dqkv, dsink = _pcall(
        body, name=name, grid=(NB, r, nsteps), in_specs=in_specs,
        out_specs=[pl.BlockSpec((1, BLOCK, VAR_W), lambda b, j, i: (b, outb(i), j)),
                   pl.BlockSpec((8, PAIR_W), lambda b, j, i: (0, 0))],
        out_shape=[jax.ShapeDtypeStruct((NB, tsub, r * VAR_W), BF16), jax.ShapeDtypeStruct((8, PAIR_W), F32)],
        scratch_shapes=scratch, compiler_params=_params(("arbitrary", "arbitrary", "arbitrary")),
    )(*ins)
    return dqkv, dsink


class _Rows:
    def __init__(self, N, T, tm):
        self.N, self.tm, self.tpe, self.grid = N, tm, T // tm, (N // tm,)

    def row(self, w, col=0):
        return pl.BlockSpec((self.tm, w), lambda i: (i, col))

    def ex(self, w):
        return pl.BlockSpec((1, 1, w), lambda i: (i // self.tpe, 0, 0))

    def const(self, shape):
        return pl.BlockSpec(shape, lambda i: tuple(0 for _ in shape))

    def view(self, w, r):
        return pl.BlockSpec((None, self.tm // r, r * w), lambda i: (i // self.tpe, i % self.tpe, 0))

    def first_of_example(self):
        return pl.program_id(0) % self.tpe == 0


def _acc(ref, first, val):
    @pl.when(first)
    def _():
        ref[0] = val

    @pl.when(jnp.logical_not(first))
    def _():
        ref[0] += val


def _colsum(v):
    return jnp.sum(v, axis=0, keepdims=True)


def _ln_stats(r):
    mu = jnp.mean(r, axis=-1, keepdims=True)
    xc = r - mu
    var = jnp.mean(xc * xc, axis=-1, keepdims=True)
    rstd = lax.rsqrt(var + LN_EPS)
    return xc * rstd, rstd


def _ln_bwd(dy, xhat, rstd, gain):
    dxh = dy * gain
    return rstd * (dxh - jnp.mean(dxh, axis=-1, keepdims=True) - xhat * jnp.mean(dxh * xhat, axis=-1, keepdims=True))


def _from_view(ref, scr, r):
    if r == 1:
        return ref[...]
    rows, w = ref.shape[0], ref.shape[1] // r
    for j in range(r):
        for c in range(w // LANES):
            scr.at[c][pl.ds(j, rows, stride=r), :] = ref[:, j * w + c * LANES:j * w + (c + 1) * LANES].astype(F32)
    return jnp.concatenate([scr[c] for c in range(w // LANES)], axis=1)


def _to_view(val, ref, scr, r):
    if r == 1:
        ref[...] = val.astype(ref.dtype)
        return
    rows, w = ref.shape[0], ref.shape[1] // r
    for c in range(w // LANES):
        scr[c] = val[:, c * LANES:(c + 1) * LANES]
    for j in range(r):
        for c in range(w // LANES):
            ref[:, j * w + c * LANES:j * w + (c + 1) * LANES] = scr.at[c][pl.ds(j, rows, stride=r), :].astype(ref.dtype)


def _silu_parts(v):
    s = jax.nn.sigmoid(v)
    return v * s, s * (1.0 + v * (1.0 - s))


def _local_step(x, mod, positions, w_in, rest_weights, sinks, ln1_g, ln1_b, ln2_g, ln2_b, target, hook=None):
    hook = hook or (lambda event, **data: None)
    NB, T, D = x.shape
    N = NB * T
    x2 = x.reshape(N, D)
    tgt2 = target.reshape(N, D)
    shift_m, scale_m, gate_m, shift_f, scale_f, gate_f = [mod[:, None, k * D:(k + 1) * D] for k in range(6)]
    cosf, sins = _rope_tables(positions)
    col = jnp.arange(QKV_P)
    vcol = col % VAR_W
    flags = jnp.where(col < VAR_W, vcol < QA_W + KA_W, vcol < 2 * GB_W).astype(F32)[None]
    R = _Rows(N, T, _pick(T, 256))
    sds = jax.ShapeDtypeStruct
    exsum = lambda w=D: sds((NB, 1, w), F32)
    ngrp = len(B_PATTERNS)

    *qkv, u = _inproj(x2, scale_m, shift_m, w_in, cosf, sins, flags, T=T, name="inproj_qkv")
    gates = _mm(u, w_in[QKV_P:], tb=True, out_dtype=BF16, name="inproj_gates")
    oa, la = _attn_fwd(qkv[0], sinks, None, NB=NB, T=T, name="attn_a_fwd")
    oa = oa.reshape(N, QA_W)
    (o1, l1), (o2, l2), (o3, l3) = _attn_fwd_b(qkv[1:], NB=NB, T=T, name="attn_b_fwd")
    w_a, w_b, w_o, w_gu, w_d = rest_weights()
    F = w_d.shape[0]
    dil = [r_ for _, r_ in B_PATTERNS]
    views = [R.view(GB_W, r_) for r_ in dil]
    tokbuf = pltpu.VMEM((GB_W // LANES, R.tm, LANES), F32)

    f32 = lambda ref: ref[...].astype(F32)
    Rm = _Rows(N, T, _pick(T, 512))

    def mix_out(o1r, o2r, o3r, l1r, l2r, l3r, oa_r, ga_r, gb_r, x_r, gm_r, g_r, b_r, sf_r, hf_r, wa_r, wb_r, wo_r,
                ob_ref, ya_ref, yb_ref, mg_ref, y_ref, r1_ref, u2_ref, *bufs):
        os_ = [_from_view(ref, bufs[n], dil[n]) for n, ref in enumerate((o1r, o2r, o3r))]
        la, lb, lc = [_from_view(ref, bufs[3 + n], dil[n]) for n, ref in enumerate((l1r, l2r, l3r))]
        mx = jnp.maximum(jnp.maximum(la, lb), lc)
        ea, eb, ec = jnp.exp(la - mx), jnp.exp(lb - mx), jnp.exp(lc - mx)
        ob = ((ea * os_[0] + eb * os_[1] + ec * os_[2]) / (ea + eb + ec)).astype(BF16)
        ob_ref[...] = ob
        ya = jnp.dot(oa_r[...], wa_r[...], preferred_element_type=F32).astype(BF16)
        yb = jnp.concatenate([jnp.dot(ob, wb_r[s_], preferred_element_type=F32)
                              for s_ in range(w_b.shape[0])], axis=1).astype(BF16)
        merged = (jax.nn.sigmoid(f32(ga_r)) * ya.astype(F32) + jax.nn.sigmoid(f32(gb_r)) * yb.astype(F32)).astype(BF16)
        y = jnp.dot(merged, wo_r[...], preferred_element_type=F32)
        r1 = ALPHA * x_r[...] + (1.0 + gm_r[0]) * y
        xhat, _ = _ln_stats(r1)
        x1 = xhat * g_r[...] + b_r[...]
        ya_ref[...], yb_ref[...], mg_ref[...], y_ref[...], r1_ref[...] = ya, yb, merged, y, r1
        u2_ref[...] = (x1 * (1.0 + sf_r[0]) + hf_r[0]).astype(BF16)

    mviews = [Rm.view(GB_W, r_) for r_ in dil]
    ob, ya, yb, merged, y, r1, u2 = _pcall(
        mix_out, name="mix_out", grid=Rm.grid,
        in_specs=mviews + mviews + [Rm.row(QA_W), Rm.row(D, 0), Rm.row(D, 1), Rm.row(D), Rm.ex(D), Rm.const((1, D)),
                                    Rm.const((1, D)), Rm.ex(D), Rm.ex(D), Rm.const(w_a.shape), Rm.const(w_b.shape),
                                    Rm.const(w_o.shape)],
        out_specs=[Rm.row(GB_W)] + [Rm.row(D)] * 6,
        out_shape=[sds((N, GB_W), BF16)] + [sds((N, D), BF16)] * 3 + [sds((N, D), F32)] * 2 + [sds((N, D), BF16)],
        scratch_shapes=[pltpu.VMEM((GB_W // LANES, Rm.tm, LANES), F32)] * 6,
        compiler_params=_params(("parallel",)))(o1, o2, o3, l1, l2, l3, oa, gates, gates, x2, gate_m, ln1_g, ln1_b,
                                                scale_f, shift_f, w_a, w_b, w_o)

    w_gu = w_gu() if callable(w_gu) else w_gu
    tnf = w_gu.shape[2]
    nft = w_gu.shape[0] // 2
    tmf = _pick(N, 512)

    def ffn_up(u_r, wg_r, wu_r, hg_ref, hu_ref, a_ref):
        hg = jnp.dot(u_r[...], wg_r[...], preferred_element_type=F32)
        hu = jnp.dot(u_r[...], wu_r[...], preferred_element_type=F32)
        sl, _ = _silu_parts(hg)
        hg_ref[...] = hg.astype(BF16)
        hu_ref[...] = hu.astype(BF16)
        a_ref[...] = (sl * hu).astype(BF16)

    ftile = pl.BlockSpec((tmf, tnf), lambda j, i: (i, j))
    hg, hu, act = _pcall(
        ffn_up, name="ffn_up", grid=(nft, N // tmf),
        in_specs=[pl.BlockSpec((tmf, D), lambda j, i: (i, 0)), pl.BlockSpec((None, D, tnf), lambda j, i: (j, 0, 0)),
                  pl.BlockSpec((None, D, tnf), lambda j, i: (j + nft, 0, 0))],
        out_specs=[ftile] * 3, out_shape=[sds((N, F), BF16)] * 3,
        compiler_params=_params(("arbitrary", "parallel")))(u2, w_gu, w_gu)
    fchunk = _pick(F, 768)

    def ffn_down_norm2(act_r, wd_r, r1_r, g1_r, b1_r, t_r, gf_r, g_r, b_r, hg_r, hu_r,
                       dy2_ref, dx1_ref, dgf_ref, dg_ref, db_ref, loss_ref, dh_ref):
        first = R.first_of_example()
        y2v = jnp.dot(act_r[...], wd_r[...], preferred_element_type=F32)
        x1 = _ln_stats(r1_r[...])[0] * g1_r[...] + b1_r[...]
        r2 = ALPHA * x1 + (1.0 + gf_r[0]) * y2v
        xhat, rstd = _ln_stats(r2)
        err = xhat * g_r[...] + b_r[...] - t_r[...]
        dx2 = err * (1.0 / D)
        dr2 = _ln_bwd(dx2, xhat, rstd, g_r[...])
        dy2 = ((1.0 + gf_r[0]) * dr2).astype(BF16)
        dy2_ref[...] = dy2
        dx1_ref[...] = ALPHA * dr2
        _acc(dgf_ref, first, _colsum(dr2 * y2v))
        _acc(dg_ref, first, _colsum(dx2 * xhat))
        _acc(db_ref, first, _colsum(dx2))
        part = 0.5 * jnp.sum(jnp.mean(err * err, axis=-1, keepdims=True))
        _acc(loss_ref, first, jnp.broadcast_to(part, (1, 128)))
        for t in range(F // fchunk):
            cs = slice(t * fchunk, (t + 1) * fchunk)
            da = lax.dot_general(dy2, wd_r[cs, :], _NT, preferred_element_type=F32)
            sl, dsl = _silu_parts(hg_r[:, cs].astype(F32))
            dh_ref[:, cs] = (da * hu_r[:, cs].astype(F32) * dsl).astype(BF16)
            dh_ref[:, F + t * fchunk:F + (t + 1) * fchunk] = (da * sl).astype(BF16)

    dy2, dx1p, dgate_f, dg2, db2, loss_p, dh = _pcall(
        ffn_down_norm2, name="ffn_down_norm2", grid=R.grid,
        in_specs=[R.row(F), R.const((F, D)), R.row(D), R.const((1, D)), R.const((1, D)), R.row(D), R.ex(D),
                  R.const((1, D)), R.const((1, D)), R.row(F), R.row(F)],
        out_specs=[R.row(D), R.row(D), R.ex(D), R.ex(D), R.ex(D), R.ex(128), R.row(2 * F)],
        out_shape=[sds((N, D), BF16), sds((N, D), F32), exsum(), exsum(), exsum(), exsum(128), sds((N, 2 * F), BF16)],
        compiler_params=_params(("arbitrary",)))(act, w_d, r1, ln1_g, ln1_b, tgt2, gate_f, ln2_g, ln2_b, hg, hu)

    g_wd = _mm(act, dy2, ta=True, out_dtype=BF16, name="ffn_down_dw")
    g_wgu = _mm(u2, dh, ta=True, out3=w_gu.shape[0], out_dtype=BF16, name="ffn_up_dw")
    hook("ffn_grads", g_wgu=g_wgu)

    def ffn_up_dx_norm1(dh_r, w_r, dx1p_r, r1_r, y_r, sf_r, gm_r, g_r, b_r,
                        dxp_ref, dy_ref, dsf_ref, dhf_ref, dgm_ref, dg_ref, db_ref):
        first = R.first_of_example()
        du2v = None
        for s_ in range(w_gu.shape[0]):
            part = lax.dot_general(dh_r[:, s_ * tnf:(s_ + 1) * tnf], w_r[s_], _NT, preferred_element_type=F32)
            du2v = part if du2v is None else du2v + part
        dx1 = dx1p_r[...] + du2v * (1.0 + sf_r[0])
        xhat, rstd = _ln_stats(r1_r[...])
        dr1 = _ln_bwd(dx1, xhat, rstd, g_r[...])
        dxp_ref[...] = ALPHA * dr1
        dy_ref[...] = ((1.0 + gm_r[0]) * dr1).astype(BF16)
        _acc(dsf_ref, first, _colsum(du2v * (xhat * g_r[...] + b_r[...])))
        _acc(dhf_ref, first, _colsum(du2v))
        _acc(dgm_ref, first, _colsum(dr1 * y_r[...]))
        _acc(dg_ref, first, _colsum(dx1 * xhat))
        _acc(db_ref, first, _colsum(dx1))

    dxp, dy, dscale_f, dshift_f, dgate_m, dg1, db1 = _pcall(
        ffn_up_dx_norm1, name="ffn_up_dx_norm1", grid=R.grid,
        in_specs=[R.row(2 * F), R.const(w_gu.shape)] + [R.row(D)] * 3 + [R.ex(D), R.ex(D), R.const((1, D)),
                                                                        R.const((1, D))],
        out_specs=[R.row(D), R.row(D)] + [R.ex(D)] * 5,
        out_shape=[sds((N, D), F32), sds((N, D), BF16)] + [exsum()] * 5,
        compiler_params=_params(("arbitrary",)))(dh, w_gu, dx1p, r1, y, scale_f, gate_m, ln1_g, ln1_b)

    g_wo = _mm(merged, dy, ta=True, out_dtype=BF16, name="out_proj_dw")

    seg = (jnp.arange(GB_W)[:, None] // HEAD_DIM == jnp.arange(GB_W)[None, :] // HEAD_DIM).astype(BF16)

    def mix_out_bwd(dy_r, ya_r, yb_r, ga_r, gb_r, wo_r, wa_r, wb_r, o1r, o2r, o3r, l1r, l2r, l3r, seg_r,
                    dya_ref, dyb_ref, dg_ref, doa_ref, d1, d2, d3, e1, e2, e3, *bufs):
        dm = lax.dot_general(dy_r[...], wo_r[...], _NT, preferred_element_type=F32).astype(BF16).astype(F32)
        sa, sb = jax.nn.sigmoid(f32(ga_r)), jax.nn.sigmoid(f32(gb_r))
        dya, dyb = (dm * sa).astype(BF16), (dm * sb).astype(BF16)
        dya_ref[...], dyb_ref[...] = dya, dyb
        dg_ref[:, :D] = (dm * f32(ya_r) * sa * (1.0 - sa)).astype(BF16)
        dg_ref[:, D:] = (dm * f32(yb_r) * sb * (1.0 - sb)).astype(BF16)
        doa_ref[...] = lax.dot_general(dya, wa_r[...], _NT, preferred_element_type=F32).astype(BF16)
        ds_ = D // w_b.shape[0]
        dob = None
        for s_ in range(w_b.shape[0]):
            part = lax.dot_general(dyb[:, s_ * ds_:(s_ + 1) * ds_], wb_r[s_], _NT, preferred_element_type=F32)
            dob = part if dob is None else dob + part
        dob_v = dob
        os_ = [_from_view(ref, bufs[n], dil[n]) for n, ref in enumerate((o1r, o2r, o3r))]
        la, lb, lc = [_from_view(ref, bufs[3 + n], dil[n]) for n, ref in enumerate((l1r, l2r, l3r))]
        mx = jnp.maximum(jnp.maximum(la, lb), lc)
        ea, eb, ec = jnp.exp(la - mx), jnp.exp(lb - mx), jnp.exp(lc - mx)
        inv = 1.0 / (ea + eb + ec)
        ws = [ea * inv, eb * inv, ec * inv]

        def headsum(v):
            hi = v.astype(BF16)
            lo = (v - hi.astype(F32)).astype(BF16)
            sm = seg_r[...]
            return jnp.dot(hi, sm, preferred_element_type=F32) + jnp.dot(lo, sm, preferred_element_type=F32)

        dws = [headsum(dob_v * o) for o in os_]
        mean = ws[0] * dws[0] + ws[1] * dws[1] + ws[2] * dws[2]
        for n, (w_, dw_, d_ref, e_ref) in enumerate(zip(ws, dws, (d1, d2, d3), (e1, e2, e3))):
            _to_view(w_ * dob_v, d_ref, bufs[6], dil[n])
            _to_view(w_ * (dw_ - mean), e_ref, bufs[7], dil[n])

    vshape = lambda r_, dt: sds((NB, T // r_, r_ * GB_W), dt)
    dya, dyb, dgates, doa, *mb = _pcall(
        mix_out_bwd, name="mix_out_bwd", grid=R.grid,
        in_specs=[R.row(D)] * 3 + [R.row(D, 0), R.row(D, 1), R.const(w_o.shape), R.const(w_a.shape), R.const(w_b.shape)]
        + views + views + [R.const((GB_W, GB_W))],
        out_specs=[R.row(D), R.row(D), R.row(2 * D), R.row(QA_W)] + views + views,
        out_shape=[sds((N, D), BF16), sds((N, D), BF16), sds((N, 2 * D), BF16), sds((N, QA_W), BF16)]
        + [vshape(r_, BF16) for r_ in dil] + [vshape(r_, F32) for r_ in dil],
        scratch_shapes=[tokbuf] * 8,
        compiler_params=_params(("parallel",)))(dy, ya, yb, gates, gates, w_o, w_a, w_b, o1, o2, o3, l1, l2, l3, seg)
    do_b, dlse_b = mb[:3], mb[3:]

    g_wa = _mm(oa, dya, ta=True, out_dtype=BF16, name="branch_a_dw")
    g_wb = _mm(ob, dyb, ta=True, out3=w_b.shape[0], out_dtype=BF16, name="branch_b_dw")
    hook("rest_grads", g_wa=g_wa, g_wb=g_wb, g_wo=g_wo, g_wd=g_wd)

    dqkv_a, dsink = _attn_bwd(qkv[0], doa, la, None, cosf, sins, sinks, None, NB=NB, T=T, name="attn_a_bwd")
    hook("attn_a_bwd_done")
    dqkv = [dqkv_a]
    for g in range(ngrp):
        dqkv.append(_attn_bwd(qkv[1 + g], do_b[g], (l1, l2, l3)[g], dlse_b[g], cosf, sins, sinks, g, NB=NB, T=T,
                              name=f"attn_b{g}_bwd")[0])
        hook(f"attn_b{g}_bwd_done")

    g_win = [_mm(d3.reshape(N, VAR_W), u, ta=True, out_dtype=BF16, name=f"inproj_dw{v}") if VAR_DIL[v] == 1
             else _dw_view(d3, u, VAR_DIL[v], name=f"inproj_dw{v}") for v, d3 in enumerate(dqkv)]
    g_win.append(_mm(dgates, u, ta=True, out_dtype=BF16, name=f"inproj_dw{N_VAR}"))
    hook("win_grads", g_win=g_win)
    wvar = lambda v: (w_in, (VAR_W, D), (v, 0))
    dview = lambda v: (dqkv[v], VAR_DIL[v])
    du = _mm_multi([dview(0)], [wvar(0)], M=N, T=T, name="inproj_dx0")
    hook("inproj_dx0_done")
    def x_bwd(duv, ins, outs):
        (dxp_r, x_r, sm_r), (gx_ref, dsm_ref, dhm_ref) = ins, outs
        first = R.first_of_example()
        gx_ref[...] = dxp_r[...] + duv * (1.0 + sm_r[0])
        _acc(dsm_ref, first, _colsum(duv * x_r[...]))
        _acc(dhm_ref, first, _colsum(duv))

    gx, dscale_m, dshift_m = _mm_multi(
        [dview(v) for v in range(1, N_VAR)] + [dgates],
        [wvar(v) for v in range(1, N_VAR)] + [(w_in, (2 * D, D), (QKV_P // (2 * D), 0))],
        M=N, T=T, add=du, tm=R.tm, name="inproj_dx1",
        post=([dxp, x2, scale_m], [R.row(D), R.row(D), R.ex(D)], [R.row(D), R.ex(D), R.ex(D)],
              [sds((N, D), F32), exsum(), exsum()], x_bwd))
    hook("inproj_dx1_done")

    dmod =jnp.concatenate([dshift_m, dscale_m, dgate_m, dshift_f, dscale_f, dgate_f], axis=-1)[:, 0]
    ln_grads = jnp.concatenate([dg1, db1, dg2, db2], axis=1)
    return dict(loss=loss_p[:, 0, 0], grad_x=gx.reshape(NB, T, D), g_win=g_win, g_wa=g_wa, g_wb=g_wb, g_wo=g_wo,
                g_wgu=g_wgu, g_wd=g_wd, dmod=dmod, ln_grads=ln_grads, dsink=dsink[0, :A_Q_HEADS])


def _coords():
    return lax.axis_index("x"), lax.axis_index("y"), lax.axis_index("c")


def _allgather_small(blk, *, name):
    m_per, n = blk.shape

    def body(x_ref, out_ref, send_sems, recv_sems, local_sem):
        x, y, c = _coords()
        me, sibling = (x, y, c), (x, y, 1 - c)
        chips = [(1 - x, y), (x, 1 - y), (1 - x, 1 - y)]

        def rows(px, py, pc):
            return out_ref.at[pl.ds((4 * px + 2 * py + pc) * m_per, m_per), :]

        def copy(k, block, to, src=None):
            return pltpu.make_async_remote_copy(
                src_ref=rows(*block) if src is None else src, dst_ref=rows(*block),
                send_sem=send_sems.at[k], recv_sem=recv_sems.at[k], device_id=to, device_id_type=MESH)

        mine = pltpu.make_async_copy(x_ref, rows(*me), local_sem)
        mine.start()
        first = [copy(0, me, sibling, src=x_ref)]
        first += [copy(1 + j, me, (*chip, c), src=x_ref) for j, chip in enumerate(chips)]
        for cp in first:
            cp.start()
        passed = [copy(4 + j, (*chip, c), sibling) for j, chip in enumerate(chips)]
        for j, chip in enumerate(chips):
            copy(1 + j, (*chip, c), me).wait_recv()
            passed[j].start()
        copy(0, sibling, me).wait_recv()
        for j, chip in enumerate(chips):
            copy(4 + j, (*chip, 1 - c), me).wait_recv()
        for cp in first + passed:
            cp.wait_send()
        mine.wait()

    return _pcall(
        body, name=name, out_shape=jax.ShapeDtypeStruct((8 * m_per, n), blk.dtype),
        in_specs=[pl.BlockSpec(memory_space=pltpu.VMEM)], out_specs=pl.BlockSpec(memory_space=pltpu.VMEM),
        scratch_shapes=[pltpu.SemaphoreType.DMA((7,)), pltpu.SemaphoreType.DMA((7,)), pltpu.SemaphoreType.DMA],
        compiler_params=pltpu.CompilerParams(vmem_limit_bytes=VMEM_LIMIT_BYTES),
    )(blk)


def _exchange(srcs, dsts, plan, *, name, dst_inits=None):
    na = len(dsts)
    nrem = len(plan(0, 0, 0))

    def body(*refs):
        refs = list(refs)
        src_refs = [refs.pop(0) for _ in range(na)] if srcs is not None else None
        if dst_inits is not None:
            del refs[:na]
        dst_refs, (send_sems, recv_sems) = refs[:na], refs[na:]
        start, wait = _copies(dst_refs if src_refs is None else src_refs, dst_refs, send_sems, recv_sems, plan)
        start()
        wait()

    hbm = pl.BlockSpec(memory_space=pl.ANY)
    ins = (list(srcs) if srcs is not None else []) + (list(dst_inits) if dst_inits is not None else [])
    base = na if srcs is not None else 0
    aliases = {base + a: a for a in range(na)} if dst_inits is not None else {}
    return _pcall(
        body, name=name, out_shape=list(dsts), in_specs=[hbm] * len(ins), out_specs=[hbm] * na,
        input_output_aliases=aliases,
        scratch_shapes=[pltpu.SemaphoreType.DMA((na * nrem,)), pltpu.SemaphoreType.DMA((na * nrem,))],
    )(*ins)


def _other_chips(x, y):
    return [(1 - x, y), (x, 1 - y), (1 - x, 1 - y)]


def _round(ride, carrier, name):
    if carrier is not None:
        _RIDES.setdefault(carrier, []).append(ride)
        return
    srcs = ride.srcs() if callable(ride.srcs) else ride.srcs
    inits = ride.dst_inits() if callable(ride.dst_inits) else ride.dst_inits
    ride.out = list(_exchange(srcs, ride.dsts, ride.plan, name=name, dst_inits=inits))


class _Gather:
    def __init__(self, shards, chip, tag, carriers=(None, None)):
        def plan_ici(x, y, c):
            k = 2 * x + y
            return [((c,), (k, c), (2 * px + py, c), (px, py, c)) for px, py in _other_chips(x, y)]

        def plan_d2d(x, y, c):
            return [((2 * px + py, c), (2 * px + py, c), (2 * px + py, 1 - c), (x, y, 1 - c))
                    for px, py in _other_chips(x, y)]

        def plan_near(x, y, c):
            k = 2 * x + y
            return [((c,), (k, c), (2 * px + py, c), (px, py, c)) for px, py in ((1 - x, y), (x, 1 - y))]

        def plan_far(x, y, c):
            kx, ky, kd = 2 * (1 - x) + y, 2 * x + (1 - y), 2 * (1 - x) + (1 - y)
            hp = shards[0].shape[1] // 2
            top, bottom = pl.ds(0, hp), pl.ds(hp, hp)
            return [((kx, c, top), (kx, c, top), (kd, c, top), (x, 1 - y, c)),
                    ((ky, c, bottom), (ky, c, bottom), (kd, c, bottom), (1 - x, y, c))]

        self.shards, self.chip = shards, chip
        dsts = [jax.ShapeDtypeStruct((4,) + s.shape, s.dtype) for s in shards]
        if len(carriers) == 3:
            near = _Ride(shards, dsts, plan_near)
            ici = _Ride(None, dsts, plan_far, dst_inits=lambda: near.out)
            _round(near, carriers[0], f"gather_{tag}_near")
            _round(ici, carriers[1], f"gather_{tag}_far")
        else:
            ici = _Ride(shards, dsts, plan_ici)
            _round(ici, carriers[0], f"gather_{tag}_ici")
        self.d2d = _Ride(None, dsts, plan_d2d, dst_inits=lambda: ici.out)
        _round(self.d2d, carriers[-1], f"gather_{tag}_d2d")

    def result(self):
        full = [lax.dynamic_update_index_in_dim(f, s, self.chip, 0) for f, s in zip(self.d2d.out, self.shards)]
        return [f.reshape((4, 2 * f.shape[2], f.shape[3])) for f in full]


def _index_operand(i):
    return jnp.reshape(i, (1,)).astype(jnp.int32)


def _add_pairs(g, f, ci, *, name):
    s, _, hr, wd = g.shape
    tr = _pick(hr, 600, 16)

    def body(c_ref, a_ref, b_ref, o_ref):
        o_ref[...] = (a_ref[...].astype(F32) + b_ref[...].astype(F32)).astype(BF16)

    spec = pl.BlockSpec((1, tr, wd), lambda j, i, c: (j, i, 0))
    grid_spec = pltpu.PrefetchScalarGridSpec(
        num_scalar_prefetch=1, grid=(s, hr // tr),
        in_specs=[pl.BlockSpec((1, None, tr, wd), lambda j, i, c: (j, c[0], i, 0)), spec], out_specs=spec)
    return _pcall(body, name=name, grid_spec=grid_spec, out_shape=jax.ShapeDtypeStruct(f.shape, BF16),
                  compiler_params=_params(("parallel", "parallel")))(_index_operand(ci), g, f)


def _sum_chips(landed, pairs, chip, ci, *, name):
    s, hr, wd = landed.shape
    tr = _pick(hr, 600, 16)

    def body(k_ref, l_ref, p_ref, o_ref):
        acc = None
        for k in range(s):
            part = jnp.where(k_ref[0] == k, p_ref[k], l_ref[k]).astype(F32)
            acc = part if acc is None else acc + part
        o_ref[...] = acc

    spec = pl.BlockSpec((s, tr, wd), lambda i, k: (0, i, 0))
    grid_spec = pltpu.PrefetchScalarGridSpec(
        num_scalar_prefetch=1, grid=(hr // tr,), in_specs=[spec, spec],
        out_specs=pl.BlockSpec((None, tr, wd), lambda i, k: (k[1], i, 0)))
    where = jnp.stack([chip, ci]).astype(jnp.int32)
    return _pcall(body, name=name, grid_spec=grid_spec, out_shape=jax.ShapeDtypeStruct((2, hr, wd), F32),
                  compiler_params=_params(("parallel",)))(where, landed, pairs)


class _ReduceScatter:
    def __init__(self, gs, chip, ci, tag):
        self.gs, self.chip, self.ci, self.tag = gs, chip, ci, tag
        self.half_t = [jax.ShapeDtypeStruct((g.shape[0],) + g.shape[2:], BF16) for g in gs]

    def pair(self, carrier=None):
        plan = lambda x, y, c: [((slice(None), 1 - c), (), (), (x, y, 1 - c))]
        self.r1 = _Ride(self.gs, self.half_t, plan)
        _round(self.r1, carrier, f"reduce_{self.tag}_pair")

    def chips(self, carrier=None):
        def plan(x, y, c):
            k = 2 * x + y
            return [((2 * px + py,), (k,), (2 * px + py,), (px, py, c)) for px, py in _other_chips(x, y)]

        self.pairs = [_add_pairs(g, f, self.ci, name=f"reduce_{self.tag}_pair_add{n}")
                      for n, (g, f) in enumerate(zip(self.gs, self.r1.out))]
        self.r2 = _Ride(self.pairs, self.half_t, plan)
        _round(self.r2, carrier, f"reduce_{self.tag}_chips")

    def halves(self, carrier=None):
        plan = lambda x, y, c: [((c,), (c,), (1 - c,), (x, y, 1 - c))]
        mine = [_sum_chips(l, p, self.chip, self.ci, name=f"reduce_{self.tag}_chip_sum{n}")
                for n, (l, p) in enumerate(zip(self.r2.out, self.pairs))]
        self.r3 = _Ride(None, [jax.ShapeDtypeStruct(m.shape, F32) for m in mine], plan, dst_inits=mine)
        _round(self.r3, carrier, f"reduce_{self.tag}_halves")

    def result(self):
        return [b.reshape(2 * b.shape[1], b.shape[2]) for b in self.r3.out]


def _ada_fwd(c_all, w_sh, b_sh, *, name):
    nb, d = c_all.shape
    wcols = w_sh.shape[1]
    tn = _pick(wcols, 512)

    def body(c_ref, w_ref, b_ref, o_ref, a_ref):
        cv = c_ref[...]
        act = cv * jax.nn.sigmoid(cv)
        a_ref[...] = act
        o_ref[...] = jnp.dot(act.astype(BF16), w_ref[...].astype(BF16), preferred_element_type=F32) + b_ref[...]

    return _pcall(
        body, name=name, grid=(wcols // tn,),
        in_specs=[pl.BlockSpec((nb, d), lambda j: (0, 0)), pl.BlockSpec((d, tn), lambda j: (0, j)),
                  pl.BlockSpec((1, tn), lambda j: (0, j))],
        out_specs=[pl.BlockSpec((nb, tn), lambda j: (0, j)), pl.BlockSpec((nb, d), lambda j: (0, 0))],
        out_shape=[jax.ShapeDtypeStruct((nb, wcols), F32), jax.ShapeDtypeStruct((nb, d), F32)],
        compiler_params=_params(("arbitrary",)))(c_all, w_sh, b_sh)


def _sum_devices(g, *, name):
    nd, m, w = g.shape

    def body(g_ref, o_ref):
        acc = g_ref[0]
        for k in range(1, nd):
            acc = acc + g_ref[k]
        o_ref[...] = acc

    return _pcall(body, name=name, out_shape=jax.ShapeDtypeStruct((m, w), F32),
                  compiler_params=pltpu.CompilerParams(vmem_limit_bytes=VMEM_LIMIT_BYTES))(g)


def _adamw(w, g, m, v, *, name):
    rows, cols = w.shape[-2:]
    tr = _pick(rows, max(8, (1 << 18) // cols), 8)
    c1 = 1.0 / (1.0 - ADAM_B1 ** ADAM_STEP)
    c2 = 1.0 / (1.0 - ADAM_B2 ** ADAM_STEP)

    def body(w_ref, g_ref, m_ref, v_ref, d_ref, nm_ref, nv_ref):
        gv = g_ref[...]
        nm = ADAM_B1 * m_ref[...] + (1.0 - ADAM_B1) * gv
        nv = ADAM_B2 * v_ref[...] + (1.0 - ADAM_B2) * (gv * gv)
        d_ref[...] = -ADAM_LR * ((nm * c1) / (jnp.sqrt(nv * c2) + ADAM_EPS) + ADAM_WD * w_ref[...])
        nm_ref[...] = nm
        nv_ref[...] = nv

    gspec = pl.BlockSpec((tr, cols), lambda i: (i, 0))
    spec = pl.BlockSpec((None, tr, cols), lambda i: (0, i, 0)) if w.ndim == 3 else gspec
    shp = jax.ShapeDtypeStruct(w.shape, F32)
    return _pcall(body, name=name, grid=(rows // tr,), in_specs=[spec, gspec, spec, spec], out_specs=[spec] * 3,
                  out_shape=[shp] * 3, compiler_params=_params(("parallel",)))(w, g, m, v)


def _permute_in_rows(wt):
    ngrp = len(B_PATTERNS)
    qb, kb, vb = (wt[A_W + n * QB_W:A_W + (n + 1) * QB_W] for n in range(3))
    parts = [wt[:A_W], jnp.zeros((VAR_W - A_W, wt.shape[1]), wt.dtype)]
    for g in range(ngrp):
        parts += [t[g * GB_W:(g + 1) * GB_W] for t in (qb, kb, vb)]
    return jnp.concatenate(parts + [wt[A_W + 3 * QB_W:]], axis=0)


def _unpermute_in_grads(pieces):
    ga, groups, gg = pieces[0], pieces[1:-1], pieces[-1]
    rows = [ga[:A_W]]
    for n in range(3):
        rows += [gp[n * GB_W:(n + 1) * GB_W] for gp in groups]
    return jnp.concatenate(rows + [gg], axis=0)


def kernel(x, c, positions, w_ada, b_ada, w_in, sinks, w_branch_a, w_branch_b, w_o, ln1_g, ln1_b, w_gate_up, w_down, ln2_g, ln2_b, loss_target, m_w_ada, m_b_ada, m_w_in, m_sinks, m_w_branch_a, m_w_branch_b, m_w_o, m_ln1_g, m_ln1_b, m_w_gate_up, m_w_down, m_ln2_g, m_ln2_b, v_w_ada, v_b_ada, v_w_in, v_sinks, v_w_branch_a, v_w_branch_b, v_w_o, v_ln1_g, v_ln1_b, v_w_gate_up, v_w_down, v_ln2_g, v_ln2_b):
    xi, yi, ci = _coords()
    chip = 2 * xi + yi
    dev = 4 * xi + 2 * yi + ci
    NB, T, D = x.shape
    nchip, ndev = 4, 8
    ada_cols = w_ada.shape[2]

    ra, ro, rd = w_branch_a.shape[1], w_o.shape[1], w_down.shape[1]
    halves = lambda a: a.reshape(a.shape[:-2] + (2, a.shape[-2] // 2, a.shape[-1]))
    tr = lambda a: jnp.swapaxes(a, -1, -2)
    shards = [halves(w.astype(BF16))
              for w in (tr(w_in[0]), w_branch_a[0], w_o[0], w_down[0], w_branch_b[0], w_gate_up[0])]
    gin = _Gather(shards[:1], chip, "w_in", carriers=("gather_c", "ada_fwd", "gather_mod"))

    c_blk = jnp.zeros((8, D), F32).at[:NB].set(c)
    c_all = _allgather_small(c_blk, name="gather_c").reshape(ndev, 8, D)[:, :NB].reshape(ndev * NB, D)
    b_sh = lax.dynamic_slice(b_ada, (0, chip * ada_cols), (1, ada_cols))
    mod_part, c_act = _ada_fwd(c_all, w_ada[0], b_sh, name="ada_fwd")
    mod_g = _allgather_small(mod_part, name="gather_mod").reshape(nchip, 2, ndev * NB, ada_cols)[:, 0]
    mod_all = jnp.transpose(mod_g, (1, 0, 2)).reshape(ndev * NB, nchip * ada_cols)
    mod = lax.dynamic_slice(mod_all, (NB * dev, 0), (NB, nchip * ada_cols))

    (g_in,) = gin.result()
    w_in_f = _permute_in_rows(g_in.reshape(nchip * g_in.shape[1], D))
    mix = _Gather(shards[1:5], chip, "w_mix", carriers=("inproj_qkv", "attn_a_fwd"))
    ffn = _Gather(shards[5:], chip, "w_ffn", carriers=("attn_a_fwd", "attn_b_fwd", "mix_out"))

    def rest_weights():
        w_a_f, w_o_f, w_d_f, w_b_f = mix.result()
        return (w_a_f.reshape(nchip * ra, D), w_b_f, w_o_f.reshape(nchip * ro, D),
                lambda: ffn.result()[0], w_d_f.reshape(nchip * rd, D))

    red = {}

    def hook(event, **g):
        if event == "ffn_grads":
            red["ffn"] = _ReduceScatter([halves(g["g_wgu"])], chip, ci, "ffn")
            red["ffn"].pair(carrier="out_proj_dw")
        elif event == "rest_grads":
            by_chip = lambda a: halves(a.reshape(nchip, a.shape[0] // nchip, D))
            red["mix"] = _ReduceScatter([by_chip(g["g_wa"]), by_chip(g["g_wo"]), by_chip(g["g_wd"]), halves(g["g_wb"])],
                                        chip, ci, "mix")
            red["ffn"].chips(carrier="attn_a_bwd")
            red["mix"].pair(carrier="attn_a_bwd")
        elif event == "attn_a_bwd_done":
            red["ffn"].halves(carrier="attn_b0_bwd")
            red["mix"].chips(carrier="attn_b0_bwd")
        elif event == "attn_b0_bwd_done":
            red["mix"].halves(carrier="attn_b1_bwd")
        elif event == "win_grads":
            gr_in = _unpermute_in_grads(g["g_win"])
            red["w_in"] = _ReduceScatter([halves(gr_in.reshape(nchip, gr_in.shape[0] // nchip, D))], chip, ci, "w_in")
            red["w_in"].pair(carrier="inproj_dx0")
        elif event == "inproj_dx0_done":
            red["w_in"].chips(carrier="inproj_dx1")
        elif event == "inproj_dx1_done":
            red["w_in"].halves(carrier="gather_small")

    res = _local_step(x, mod, positions, w_in_f, rest_weights, sinks[0], ln1_g, ln1_b, ln2_g, ln2_b, loss_target, hook)
    (g_w_a, g_w_o, g_w_d, g_w_b), (g_w_gu,) = red["mix"].result(), red["ffn"].result()

    small_rows = 24
    misc = jnp.zeros((1, D), F32).at[0, :A_Q_HEADS].set(res["dsink"]).at[0, A_Q_HEADS].set(jnp.sum(res["loss"]))
    small = jnp.concatenate([res["dmod"].reshape(NB * 6, D), jnp.sum(res["ln_grads"], axis=0), misc,
                             jnp.zeros((small_rows - NB * 6 - 5, D), F32)], axis=0)
    small_all = _allgather_small(small, name="gather_small").reshape(ndev, small_rows, D)
    (g_w_in,) = red["w_in"].result()
    dmod_all = small_all[:, :NB * 6].reshape(ndev * NB, 6 * D)
    sums = _sum_devices(small_all, name="sum_small")
    g_b_ada = (sums[0:6] + sums[6:12]).reshape(1, 6 * D)
    g_ln1_g, g_ln1_b, g_ln2_g, g_ln2_b = (sums[12 + n][None] for n in range(4))
    g_sinks = sums[16, :A_Q_HEADS][None]
    loss = sums[16, A_Q_HEADS]
    dmod_sh = lax.dynamic_slice(dmod_all, (0, chip * ada_cols), (ndev * NB, ada_cols))
    g_w_ada = _mm(c_act, dmod_sh, ta=True, name="ada_dw")

    names = ["w_ada", "b_ada", "w_in", "sinks", "w_branch_a", "w_branch_b", "w_o", "ln1_g", "ln1_b",
             "w_gate_up", "w_down", "ln2_g", "ln2_b"]
    ws = [w_ada, b_ada, w_in, sinks, w_branch_a, w_branch_b, w_o, ln1_g, ln1_b, w_gate_up, w_down, ln2_g, ln2_b]
    ms = [m_w_ada, m_b_ada, m_w_in, m_sinks, m_w_branch_a, m_w_branch_b, m_w_o, m_ln1_g, m_ln1_b, m_w_gate_up,
          m_w_down, m_ln2_g, m_ln2_b]
    vs = [v_w_ada, v_b_ada, v_w_in, v_sinks, v_w_branch_a, v_w_branch_b, v_w_o, v_ln1_g, v_ln1_b, v_w_gate_up,
          v_w_down, v_ln2_g, v_ln2_b]
    gs = [g_w_ada, g_b_ada, g_w_in, g_sinks, g_w_a, g_w_b, g_w_o, g_ln1_g, g_ln1_b, g_w_gu, g_w_d, g_ln2_g, g_ln2_b]
    grads, deltas, new_ms, new_vs = [], [], [], []
    for name, w, g, m, v in zip(names, ws, gs, ms, vs):
        flip = tr if name == "w_in" else (lambda a: a)
        w, m, v = flip(w), flip(m), flip(v)
        g2 = g.reshape(w.shape[-2:])
        d, nm, nv = _adamw(w, g2, m, v, name="adamw_" + name)
        grads.append(flip(g2.reshape(w.shape)))
        deltas.append(flip(d))
        new_ms.append(flip(nm))
        new_vs.append(flip(nv))
    return (loss, res["grad_x"], *grads, *deltas, *new_ms, *new_vs)
```

```python
import functools

import jax
import jax.numpy as jnp
from jax import lax
from jax.experimental import pallas as pl
from jax.experimental.pallas import tpu as pltpu

F32 = jnp.float32
BF16 = jnp.bfloat16
MESH = pl.DeviceIdType.MESH

HEAD_DIM = 64
LANES = 128
PAIR_W = 2 * HEAD_DIM
BLOCK = 128
A_Q_HEADS = 16
A_KV_HEADS = 2
A_WINDOW = 128
B_PATTERNS = ((128, 1), (512, 4), (2048, 16))
B_GROUP_HEADS = 8
QA_W = A_Q_HEADS * HEAD_DIM
KA_W = A_KV_HEADS * HEAD_DIM
GB_W = B_GROUP_HEADS * HEAD_DIM
QB_W = GB_W * len(B_PATTERNS)
A_W = QA_W + 2 * KA_W
VAR_W = 3 * GB_W
N_VAR = 1 + len(B_PATTERNS)
VAR_DIL = (1,) + tuple(r for _, r in B_PATTERNS)
A_BWD_SPLIT = 4
QKV_P = N_VAR * VAR_W
ROPE_THETA = 10000.0
LN_EPS = 1e-5
NEG_INF = -1e30
DEPTH = 1
ALPHA = (2 * DEPTH) ** 0.25
SCALE = HEAD_DIM ** -0.5

ADAM_LR, ADAM_B1, ADAM_B2, ADAM_EPS, ADAM_WD, ADAM_STEP = 0.001, 0.9, 0.999, 1e-08, 0.01, 10

VMEM_LIMIT_BYTES = 56 * 1024 * 1024
MM_TILE_BYTES = 36 * 1024 * 1024
MM_WHOLE_K = 4096


def _params(sem=None):
    return pltpu.CompilerParams(dimension_semantics=sem, vmem_limit_bytes=VMEM_LIMIT_BYTES)


_RIDES = {}


def _pcall(body, *, name, **kw):
    rides = _RIDES.pop(name, None)
    if rides is None:
        return pl.pallas_call(body, name=name, **kw)
    return _riding_call(body, rides, name=name, **kw)


def _copies(src_refs, dst_refs, send_sems, recv_sems, plan):
    x, y, c = lax.axis_index("x"), lax.axis_index("y"), lax.axis_index("c")
    remote = plan(x, y, c)
    nrem = len(remote)
    at = lambda ref, idx: ref.at[idx] if idx else ref

    def copy(a, n, landing):
        si, di, ri, peer = remote[n]
        return pltpu.make_async_remote_copy(
            src_ref=at(src_refs[a], si), dst_ref=at(dst_refs[a], ri if landing else di),
            send_sem=send_sems.at[a * nrem + n], recv_sem=recv_sems.at[a * nrem + n],
            device_id=peer, device_id_type=MESH)

    order = [(a, n) for a in range(len(dst_refs)) for n in range(nrem)]

    def start():
        for a, n in order:
            copy(a, n, False).start()

    def wait():
        for a, n in order:
            copy(a, n, True).wait_recv()
        for a, n in order:
            copy(a, n, False).wait_send()

    return start, wait


class _Ride:
    def __init__(self, srcs, dsts, plan, dst_inits=None):
        self.srcs, self.dsts, self.plan, self.dst_inits, self.out = srcs, dsts, plan, dst_inits, None


def _riding_call(body, rides, *, name, in_specs, out_specs, out_shape, grid=(), scratch_shapes=(), **kw):
    single = not isinstance(out_specs, (list, tuple))
    out_specs = [out_specs] if single else list(out_specs)
    out_shape = [out_shape] if single else list(out_shape)
    n_in, n_out, n_scr = len(in_specs), len(out_specs), len(scratch_shapes)
    xin, xdsts, sems, aliases, layout = [], [], [], {}, []
    for ride in rides:
        srcs = ride.srcs() if callable(ride.srcs) else ride.srcs
        inits = ride.dst_inits() if callable(ride.dst_inits) else ride.dst_inits
        na, nrem = len(ride.dsts), len(ride.plan(0, 0, 0))
        src_at = len(xin) if srcs is not None else None
        xin += list(srcs) if srcs is not None else []
        if inits is not None:
            aliases.update({n_in + len(xin) + a: n_out + len(xdsts) + a for a in range(na)})
            xin += list(inits)
        layout.append((src_at, len(xdsts), na))
        xdsts += list(ride.dsts)
        sems += [pltpu.SemaphoreType.DMA((na * nrem,)), pltpu.SemaphoreType.DMA((na * nrem,))]

    def wrapped(*refs):
        ins, xins = refs[:n_in], refs[n_in:n_in + len(xin)]
        outs = refs[n_in + len(xin):n_in + len(xin) + n_out]
        xouts = refs[n_in + len(xin) + n_out:n_in + len(xin) + n_out + len(xdsts)]
        scr = refs[n_in + len(xin) + n_out + len(xdsts):]
        rounds = []
        for k, (ride, (src_at, dst_at, na)) in enumerate(zip(rides, layout)):
            dsts = xouts[dst_at:dst_at + na]
            srcs = dsts if src_at is None else xins[src_at:src_at + na]
            rounds.append(_copies(srcs, dsts, scr[n_scr + 2 * k], scr[n_scr + 2 * k + 1], ride.plan))
        ids = [pl.program_id(a) for a in range(len(grid))]
        first = functools.reduce(jnp.logical_and, [i == 0 for i in ids], True)
        last = functools.reduce(jnp.logical_and, [i == g - 1 for i, g in zip(ids, grid)], True)

        def start_all():
            for start, _ in rounds:
                start()

        def wait_all():
            for _, wait in rounds:
                wait()

        start_all() if not grid else pl.when(first)(start_all)
        body(*ins, *outs, *scr[:n_scr])
        wait_all() if not grid else pl.when(last)(wait_all)

    hbm = pl.BlockSpec(memory_space=pl.ANY)
    gridkw = dict(grid=grid) if grid else {}

    def run(*args):
        res = pl.pallas_call(
            wrapped, name=name, in_specs=list(in_specs) + [hbm] * len(xin),
            out_specs=out_specs + [hbm] * len(xdsts), out_shape=out_shape + xdsts,
            scratch_shapes=list(scratch_shapes) + sems, input_output_aliases=aliases,
            compiler_params=_params(("arbitrary",) * len(grid) if grid else None), **gridkw,
        )(*args, *xin)
        for ride, (_, dst_at, na) in zip(rides, layout):
            ride.out = list(res[n_out + dst_at:n_out + dst_at + na])
        return res[0] if single else list(res[:n_out])

    return run


def _pick(n, target, quantum=128):
    t = (min(target, n) // quantum) * quantum
    while t >= quantum:
        if n % t == 0:
            return t
        t -= quantum
    return n


def _mm(a, b, *, name, ta=False, tb=False, b3=False, out3=0, out_dtype=F32, add=None, tm=1024, tn=1536, tk=1536,
        b_rows=None):
    if ta:
        K, M = a.shape
    else:
        M, K = a.shape
    if b3 and tb:
        Nn, K2, tk = b.shape[1], b.shape[0] * b.shape[2], b.shape[2]
    elif b3:
        K2, Nn, tn = b.shape[1], b.shape[0] * b.shape[2], b.shape[2]
    elif tb:
        Nn, K2 = b.shape
        if b_rows is not None:
            Nn = b_rows[1]
    else:
        K2, Nn = b.shape
    assert K == K2, (a.shape, b.shape)
    assert b_rows is None or (tb and not b3), "b_rows needs a plain transposed b"
    if out3:
        tn = Nn // out3
    tm, tn, tk = _pick(M, tm), _pick(Nn, tn), _pick(K, tk)
    if not (b3 and tb) and K <= MM_WHOLE_K:
        tk = K
        fits = lambda: 4 * tk * (tm + tn) + 8 * tm * tn * (2 if add is not None else 1) <= MM_TILE_BYTES
        while not fits():
            if (tm >= tn or b3 or out3) and tm > 256:
                tm = _pick(M, tm - 128)
            elif not (b3 or out3) and tn > 256:
                tn = _pick(Nn, tn - 128)
            else:
                break
    nk = K // tk
    j_outer = K * Nn + (Nn // tn) * M * K < M * K + (M // tm) * K * Nn
    dn = (((0 if ta else 1,), (1 if tb else 0,)), ((), ()))

    def body(*refs):
        refs = list(refs)
        a_ref, b_ref = refs[:2]
        add_ref = refs[2] if add is not None else None
        o_ref = refs[3] if add is not None else refs[2]
        part = lax.dot_general(a_ref[...].astype(BF16), b_ref[...].astype(BF16), dn, preferred_element_type=F32)

        def finish(r):
            if add is not None:
                r = r + add_ref[...]
            o_ref[...] = r.astype(out_dtype)

        if nk == 1:
            finish(part)
            return
        acc = refs[-1]
        k = pl.program_id(2)

        @pl.when(k == 0)
        def _():
            acc[...] = part

        @pl.when(k > 0)
        def _():
            acc[...] += part

        @pl.when(k == nk - 1)
        def _():
            finish(acc[...])

    def spec(shape, index):
        return pl.BlockSpec(shape, (lambda j, i, k: index(i, j, k)) if j_outer else index)

    a_spec = spec((tk, tm), lambda i, j, k: (k, i)) if ta else spec((tm, tk), lambda i, j, k: (i, k))
    if b3 and tb:
        b_spec = spec((None, tn, tk), lambda i, j, k: (k, j, 0))
    elif b3:
        b_spec = spec((None, tk, tn), lambda i, j, k: (j, k, 0))
    elif tb:
        first = 0 if b_rows is None else b_rows[0]
        assert first % tn == 0, (first, tn)
        b_spec = spec((tn, tk), lambda i, j, k: (first // tn + j, k))
    else:
        b_spec = spec((tk, tn), lambda i, j, k: (k, j))
    if out3:
        o_spec = spec((None, tm, tn), lambda i, j, k: (j, i, 0))
    else:
        o_spec = spec((tm, tn), lambda i, j, k: (i, j))
    ins, specs = [a, b], [a_spec, b_spec]
    if add is not None:
        ins.append(add)
        specs.append(o_spec)
    grid = (Nn // tn, M // tm, nk) if j_outer else (M // tm, Nn // tn, nk)
    return _pcall(
        body, name=name, grid=grid, in_specs=specs, out_specs=o_spec,
        out_shape=jax.ShapeDtypeStruct((out3, M, tn) if out3 else (M, Nn), out_dtype),
        scratch_shapes=[pltpu.VMEM((tm, tn), F32)] if nk > 1 else [],
        compiler_params=_params(("parallel", "parallel", "arbitrary")),
    )(*ins)


def _mm_multi(a_list, b_list, *, name, M, T=None, add=None, out_dtype=F32, tm=512, post=None):
    tm = _pick(T or M, tm)
    ns = len(a_list)
    dils = [a[1] if isinstance(a, tuple) else 0 for a in a_list]
    a_arrs = [a[0] if isinstance(a, tuple) else a for a in a_list]
    widths = [a.shape[-1] // max(r, 1) for a, r in zip(a_arrs, dils)]
    b_arrs, b_specs = [], []
    for b in b_list:
        arr, shp, idx = b if isinstance(b, tuple) else (b, b.shape, (0, 0))
        b_arrs.append(arr)
        b_specs.append(pl.BlockSpec(shp, lambda i, idx=idx: idx))
    Nn = b_specs[0].block_shape[1]
    dn = (((1,), (0,)), ((), ()))
    nmm = 2 * ns + (1 if add is not None else 0)
    p_arrs, p_in_specs, p_out_specs, p_out_shape, p_fn = post or ([], [], None, None, None)
    nin = nmm + len(p_arrs)
    nout = len(p_out_specs) if post else 1

    def body(*refs):
        a_refs, b_refs, scr = refs[:ns], refs[ns:2 * ns], list(refs[nin + nout:])
        acc = None
        for a_ref, b_ref, r in zip(a_refs, b_refs, dils):
            av = _from_view(a_ref, scr.pop(0), r) if r > 1 else a_ref[...]
            part = lax.dot_general(av.astype(BF16), b_ref[...], dn, preferred_element_type=F32)
            acc = part if acc is None else acc + part
        if add is not None:
            acc = acc + refs[2 * ns][...]
        if post:
            p_fn(acc, refs[nmm:nin], refs[nin:nin + nout])
        else:
            refs[nin][...] = acc.astype(out_dtype)

    tpe = (T or M) // tm
    a_specs = [pl.BlockSpec((None, tm // r, r * w), lambda i: (i // tpe, i % tpe, 0)) if r
               else pl.BlockSpec((tm, w), lambda i: (i, 0)) for r, w in zip(dils, widths)]
    o_spec = pl.BlockSpec((tm, Nn), lambda i: (i, 0))
    specs = a_specs + b_specs
    ins = a_arrs + b_arrs
    if add is not None:
        specs.append(o_spec)
        ins.append(add)
    scratch = [pltpu.VMEM((w // LANES, tm, LANES), F32) for r, w in zip(dils, widths) if r > 1]
    return _pcall(body, name=name, grid=(M // tm,), in_specs=specs + list(p_in_specs),
                  out_specs=list(p_out_specs) if post else o_spec, scratch_shapes=scratch,
                  out_shape=list(p_out_shape) if post else jax.ShapeDtypeStruct((M, Nn), out_dtype),
                  compiler_params=_params(("arbitrary",) if post else ("parallel",)))(*ins, *p_arrs)


def _dw_view(d3, u, r, *, name, tk=1024):
    NB, tsub, rw = d3.shape
    W, T, D = rw // r, tsub * r, u.shape[1]
    tk = _pick(T, tk)
    tpe, nk = T // tk, NB * T // tk

    def body(d_ref, u_ref, o_ref, acc, scr):
        k = pl.program_id(0)
        dv = _from_view(d_ref, scr, r).astype(BF16)
        part = lax.dot_general(dv, u_ref[...], _TN, preferred_element_type=F32)

        @pl.when(k == 0)
        def _():
            acc[...] = part

        @pl.when(k > 0)
        def _():
            acc[...] += part

        @pl.when(k == nk - 1)
        def _():
            o_ref[...] = acc[...].astype(o_ref.dtype)

    return _pcall(
        body, name=name, grid=(nk,),
        in_specs=[pl.BlockSpec((None, tk // r, rw), lambda k: (k // tpe, k % tpe, 0)), pl.BlockSpec((tk, D), lambda k: (k, 0))],
        out_specs=pl.BlockSpec((W, D), lambda k: (0, 0)), out_shape=jax.ShapeDtypeStruct((W, D), BF16),
        scratch_shapes=[pltpu.VMEM((W, D), F32), pltpu.VMEM((W // LANES, tk, LANES), F32)],
        compiler_params=_params(("arbitrary",)))(d3, u)


def _lane(shape):
    return lax.broadcasted_iota(jnp.int32, shape, len(shape) - 1)


def _rot_half(v):
    w = v.shape[-1]
    first = (_lane(v.shape) % HEAD_DIM) < (HEAD_DIM // 2)
    return jnp.where(first, pltpu.roll(v, w - HEAD_DIM // 2, v.ndim - 1), pltpu.roll(v, HEAD_DIM // 2, v.ndim - 1))


def _widen(t, w):
    return t if w == t.shape[-1] else jnp.concatenate([t] * (w // t.shape[-1]), axis=-1)


def _unrope(v, cos, sins):
    w = v.shape[-1]
    return v * _widen(cos, w) - _rot_half(v) * _widen(sins, w)


def _rope_tables(positions):
    half = HEAD_DIM // 2
    inv = ROPE_THETA ** (-jnp.arange(half, dtype=F32) / half)
    ang = positions.astype(F32)[..., None] * inv
    cos, sin = jnp.cos(ang), jnp.sin(ang)
    cosf = jnp.concatenate([cos, cos, cos, cos], axis=-1)
    sins = jnp.concatenate([-sin, sin, -sin, sin], axis=-1)
    n = positions.shape[0] * positions.shape[1]
    return cosf.reshape(n, PAIR_W), sins.reshape(n, PAIR_W)


def _inproj(x2, scale, shift, w, cosf, sins, flags, *, T, name):
    N, D = x2.shape
    tm, tn = _pick(T, 512), VAR_W
    tpe = T // tm

    def body(x_ref, sc_ref, sh_ref, w_ref, c_ref, s_ref, f_ref, *outs):
        o_refs, u_ref = outs[:N_VAR], outs[N_VAR]
        j = pl.program_id(1)

        @pl.when(j == 0)
        def _():
            u_ref[...] = (x_ref[...] * (1.0 + sc_ref[0]) + sh_ref[0]).astype(BF16)

        acc = lax.dot_general(u_ref[...], w_ref[...], (((1,), (1,)), ((), ())), preferred_element_type=F32)
        fl = f_ref[...]
        ce = 1.0 + (_widen(c_ref[...], tn) - 1.0) * fl
        se = _widen(s_ref[...], tn) * fl
        res = acc * ce + _rot_half(acc) * se
        for v in range(N_VAR):
            @pl.when(j == v)
            def _(v=v):
                _to_view(res, o_refs[v], outs[N_VAR + 1], VAR_DIL[v])

    ex = pl.BlockSpec((1, 1, D), lambda i, j: (i // tpe, 0, 0))
    tab = pl.BlockSpec((tm, PAIR_W), lambda i, j: (i, 0))
    keep = lambda w_: pl.BlockSpec((tm, w_), lambda i, j: (i, 0))
    vspec = lambda r: pl.BlockSpec((None, tm // r, r * tn), lambda i, j: (i // tpe, i % tpe, 0))
    vshape = lambda r: jax.ShapeDtypeStruct((N // T, T // r, r * tn), BF16)
    return _pcall(
        body, name=name, grid=(N // tm, N_VAR),
        in_specs=[keep(D), ex, ex, pl.BlockSpec((tn, D), lambda i, j: (j, 0)), tab, tab,
                  pl.BlockSpec((1, tn), lambda i, j: (0, j))],
        out_specs=[vspec(r) for r in VAR_DIL] + [keep(D)],
        out_shape=[vshape(r) for r in VAR_DIL] + [jax.ShapeDtypeStruct((N, D), BF16)],
        scratch_shapes=[pltpu.VMEM((tn // LANES, tm, LANES), F32)],
        compiler_params=_params(("parallel", "arbitrary")),
    )(x2, scale, shift, w, cosf, sins, flags)


class _Geom:
    def __init__(self, g):
        if g is None:
            self.r, self.nq, self.n_back, self.sink = 1, A_Q_HEADS, A_WINDOW - 1, True
            self.qw, self.kw = QA_W, KA_W
            self.qidx = lambda j: 0
            self.kidx = lambda j: QA_W // KA_W
            self.vidx = lambda j: QA_W // KA_W + 1
        else:
            window, r = B_PATTERNS[g]
            self.r, self.nq, self.n_back, self.sink = r, B_GROUP_HEADS, window // r, False
            self.qw, self.kw = GB_W, GB_W
            self.qidx = lambda j: 3 * j
            self.kidx = lambda j: 3 * j + 1
            self.vidx = lambda j: 3 * j + 2
        self.ntile = self.qw // PAIR_W


def _stack_heads(t, scale=None):
    first = _lane(t.shape) < HEAD_DIM
    z = jnp.zeros_like(t)
    if scale is not None:
        t = t * jnp.asarray(scale, t.dtype)
    return jnp.concatenate([jnp.where(first, t, z), jnp.where(first, z, t)], axis=0)


def _unstack_heads(v2):
    return jnp.where(_lane((BLOCK, PAIR_W)) < HEAD_DIM, v2[:BLOCK], v2[BLOCK:])


def _dup_head(t, kh):
    tf = t.astype(F32)
    keep = (_lane(t.shape) < HEAD_DIM) if kh == 0 else (_lane(t.shape) >= HEAD_DIM)
    return jnp.where(keep, tf, pltpu.roll(tf, HEAD_DIM, 1)).astype(t.dtype)


def _fold_heads(t):
    return t + pltpu.roll(t, HEAD_DIM, 1)


def _band_mask(rows, i, n_back, single):
    nkeys = BLOCK if single else 2 * BLOCK
    qi = jnp.bitwise_and(lax.broadcasted_iota(jnp.int32, (rows, nkeys), 0), BLOCK - 1)
    ki = lax.broadcasted_iota(jnp.int32, (rows, nkeys), 1)
    if single:
        return qi >= ki
    dist = qi + BLOCK - ki
    return jnp.logical_and(jnp.logical_and(dist >= 0, dist <= n_back), jnp.logical_or(ki >= BLOCK, i > 0))


def _sink_slot(rows):
    qi = jnp.bitwise_and(lax.broadcasted_iota(jnp.int32, (rows, 2 * BLOCK), 0), BLOCK - 1)
    return qi == lax.broadcasted_iota(jnp.int32, (rows, 2 * BLOCK), 1)


def _sink_scores(rows, sinks):
    blk = lax.broadcasted_iota(jnp.int32, (rows, 2 * BLOCK), 0) // BLOCK
    out = jnp.full((rows, 2 * BLOCK), sinks[-1], F32)
    for b in range(len(sinks) - 2, -1, -1):
        out = jnp.where(blk == b, sinks[b], out)
    return out


def _softmax_parts(s, valid, sinks):
    s = jnp.where(valid, s, NEG_INF)
    if sinks is not None:
        slot = _sink_slot(s.shape[0])
        s = jnp.where(slot, _sink_scores(s.shape[0], sinks), s)
    m = jnp.max(s, axis=1, keepdims=True)
    p = jnp.exp(s - m)
    den = jnp.sum(p, axis=1, keepdims=True)
    if sinks is not None:
        p = jnp.where(slot, 0.0, p)
    return p, m, den


_NT = (((1,), (1,)), ((), ()))
_TN = (((0,), (0,)), ((), ()))


def _rows2(prev_ref, cur_ref, cs, single=False):
    if single:
        return cur_ref[0, :, cs]
    return jnp.concatenate([prev_ref[0, :, cs], cur_ref[0, :, cs]], axis=0)


def _sink_scalars(sink_ref, first, nblocks):
    return [sink_ref[first + b] for b in range(nblocks)]


def _tile(t):
    return slice(t * PAIR_W, (t + 1) * PAIR_W)


def _attn_fwd(qkv, sinks, g, *, NB, T, name):
    geo = _Geom(g)
    r, qw, kw, ntile = geo.r, geo.qw, geo.kw, geo.ntile
    tsub = T // r
    nblk = tsub // BLOCK
    qkv3 = qkv.reshape(NB, tsub, r * VAR_W)
    out_dtype = BF16 if g is None else F32
    tiles_per_kv = ntile // A_KV_HEADS

    single = nblk == 1

    def body(q_ref, kp_ref, kc_ref, vp_ref, vc_ref, sink_ref, o_ref, l_ref):
        i = pl.program_id(2)
        if geo.sink:
            kall, vall = _rows2(kp_ref, kc_ref, _tile(0)), _rows2(vp_ref, vc_ref, _tile(0))
            kdup = [_dup_head(kall, kh) for kh in range(A_KV_HEADS)]
            vdup = [_dup_head(vall, kh) for kh in range(A_KV_HEADS)]
            tiles = [[t] for t in range(ntile)]
            q2s = [_stack_heads(q_ref[0, :, _tile(t)], SCALE) for t in range(ntile)]
            kks = [kdup[t // tiles_per_kv] for t in range(ntile)]
            vvs = [vdup[t // tiles_per_kv] for t in range(ntile)]
            sinkcols = [_sink_scalars(sink_ref, 2 * t, 2) for t in range(ntile)]
        else:
            tiles = [[t] for t in range(ntile)]
            q2s = [_stack_heads(q_ref[0, :, _tile(t)], SCALE) for t in range(ntile)]
            kks = [_rows2(kp_ref, kc_ref, _tile(t), single) for t in range(ntile)]
            vvs = [_rows2(vp_ref, vc_ref, _tile(t), single) for t in range(ntile)]
            sinkcols = [None] * ntile
        valid = _band_mask(q2s[0].shape[0], i, geo.n_back, single)
        ss = [lax.dot_general(q2, kk, _NT, preferred_element_type=F32) for q2, kk in zip(q2s, kks)]
        parts = [_softmax_parts(s, valid, sc) for s, sc in zip(ss, sinkcols)]
        o2s = [jnp.dot(p.astype(BF16), vv, preferred_element_type=F32) / den for (p, m, den), vv in zip(parts, vvs)]
        for ts, o2, (p, m, den) in zip(tiles, o2s, parts):
            lse2 = jnp.broadcast_to(m + jnp.log(den), (o2.shape[0], PAIR_W))
            for n, t in enumerate(ts):
                rows = slice(2 * BLOCK * n, 2 * BLOCK * (n + 1))
                o_ref[0, :, _tile(t)] = _unstack_heads(o2[rows]).astype(out_dtype)
                l_ref[0, :, _tile(t)] = _unstack_heads(lse2[rows])

    prev = lambda i: jnp.maximum(i - 1, 0)
    in_specs = [
        pl.BlockSpec((1, BLOCK, qw), lambda b, j, i: (b, i, geo.qidx(j))),
        pl.BlockSpec((1, BLOCK, kw), lambda b, j, i: (b, prev(i), geo.kidx(j))),
        pl.BlockSpec((1, BLOCK, kw), lambda b, j, i: (b, i, geo.kidx(j))),
        pl.BlockSpec((1, BLOCK, kw), lambda b, j, i: (b, prev(i), geo.vidx(j))),
        pl.BlockSpec((1, BLOCK, kw), lambda b, j, i: (b, i, geo.vidx(j))),
        pl.BlockSpec(memory_space=pltpu.SMEM),
    ]
    o_spec = pl.BlockSpec((1, BLOCK, qw), lambda b, j, i: (b, i, j))
    shape = (NB, tsub, r * qw)
    o, lse = _pcall(
        body, name=name, grid=(NB, r, nblk), in_specs=in_specs, out_specs=[o_spec, o_spec],
        out_shape=[jax.ShapeDtypeStruct(shape, out_dtype), jax.ShapeDtypeStruct(shape, F32)],
        compiler_params=_params(("parallel", "parallel", "arbitrary")),
    )(qkv3, qkv3, qkv3, qkv3, qkv3, sinks)
    return o, lse


def _attn_fwd_b(qkvs, *, NB, T, name):
    geos = [_Geom(g) for g in range(len(B_PATTERNS))]
    steps = T // BLOCK
    nt = GB_W // PAIR_W
    ng = len(geos)

    def where(geo, s):
        nblk = T // geo.r // BLOCK
        return s // steps, (s % steps) // nblk, (s % steps) % nblk

    def body(*refs):
        ins, outs = refs[:5 * ng], refs[5 * ng:]
        s = pl.program_id(0)
        q2s, kks, vvs, valids = [], [], [], []
        for n, geo in enumerate(geos):
            q_ref, kp_ref, kc_ref, vp_ref, vc_ref = ins[5 * n:5 * n + 5]
            single = T // geo.r // BLOCK == 1
            valid = _band_mask(2 * BLOCK, where(geo, s)[2], geo.n_back, single)
            for t in range(nt):
                q2s.append(_stack_heads(q_ref[0, :, _tile(t)], SCALE))
                kks.append(_rows2(kp_ref, kc_ref, _tile(t), single))
                vvs.append(_rows2(vp_ref, vc_ref, _tile(t), single))
                valids.append(valid)
        ss = [lax.dot_general(q2, kk, _NT, preferred_element_type=F32) for q2, kk in zip(q2s, kks)]
        parts = [_softmax_parts(sc, valid, None) for sc, valid in zip(ss, valids)]
        o2s = [jnp.dot(p.astype(BF16), vv, preferred_element_type=F32) / den for (p, m, den), vv in zip(parts, vvs)]
        for n in range(ng):
            o_ref, l_ref = outs[2 * n], outs[2 * n + 1]
            for t in range(nt):
                o2, (p, m, den) = o2s[n * nt + t], parts[n * nt + t]
                o_ref[0, :, _tile(t)] = _unstack_heads(o2)
                l_ref[0, :, _tile(t)] = _unstack_heads(jnp.broadcast_to(m + jnp.log(den), (2 * BLOCK, PAIR_W)))

    in_specs, ins, out_specs, out_shape = [], [], [], []
    for geo, qkv in zip(geos, qkvs):
        tsub = T // geo.r
        pos = lambda s, geo=geo: where(geo, s)
        prev = lambda i: jnp.maximum(i - 1, 0)
        blk = lambda col, back, pos=pos: pl.BlockSpec(
            (1, BLOCK, GB_W), lambda s: (pos(s)[0], prev(pos(s)[2]) if back else pos(s)[2], col(pos(s)[1])))
        in_specs += [blk(geo.qidx, False), blk(geo.kidx, True), blk(geo.kidx, False), blk(geo.vidx, True),
                     blk(geo.vidx, False)]
        ins += [qkv.reshape(NB, tsub, geo.r * VAR_W)] * 5
        out_specs += [blk(lambda j: j, False)] * 2
        out_shape += [jax.ShapeDtypeStruct((NB, tsub, geo.r * GB_W), F32)] * 2
    res = _pcall(body, name=name, grid=(NB * steps,), in_specs=in_specs, out_specs=out_specs, out_shape=out_shape,
                 compiler_params=_params(("arbitrary",)))(*ins)
    return [(res[2 * n], res[2 * n + 1]) for n in range(ng)]


def _attn_bwd(qkv, do, lse, dlse, cosf, sins, sinks, g, *, NB, T, name):
    geo = _Geom(g)
    r, qw, kw, ntile = geo.r, geo.qw, geo.kw, geo.ntile
    tsub = T // r
    nblk = tsub // BLOCK
    view = lambda a, w: a.reshape(NB, tsub, r * w)
    has_dlse = dlse is not None
    tiles_per_kv = ntile // A_KV_HEADS

    single = nblk == 1
    krows = BLOCK if single else 2 * BLOCK
    nsteps = 1 if single else nblk + 1

    def grads(q2s, kks, vvs, do2s, i, lserows, sinkcols, dlrows):
        nrow = q2s[0].shape[0]
        ki = lax.broadcasted_iota(jnp.int32, (krows, nrow), 0)
        qi = jnp.bitwise_and(lax.broadcasted_iota(jnp.int32, (krows, nrow), 1), BLOCK - 1)
        if single:
            valid = qi >= ki
        else:
            dist = qi + BLOCK - ki
            valid = jnp.logical_and(jnp.logical_and(dist >= 0, dist <= geo.n_back), jnp.logical_or(ki >= BLOCK, i > 0))
        sts = [lax.dot_general(kk, q2, _NT, preferred_element_type=F32) for q2, kk in zip(q2s, kks)]
        dpts = [lax.dot_general(vv, do2, _NT, preferred_element_type=F32) for do2, vv in zip(do2s, vvs)]
        pts, dsts, sks = [], [], []
        for st, dpt, ls, sc, dl in zip(sts, dpts, lserows, sinkcols, dlrows):
            sv = jnp.where(valid, st, NEG_INF)
            if sc is not None:
                slot = ki == qi
                blk = lax.broadcasted_iota(jnp.int32, (krows, nrow), 1) // BLOCK
                sink = jnp.full((krows, nrow), sc[-1], F32)
                for b in range(len(sc) - 2, -1, -1):
                    sink = jnp.where(blk == b, sc[b], sink)
                sv = jnp.where(slot, sink, sv)
                dpt = jnp.where(slot, 0.0, dpt)
            pt = jnp.exp(sv - ls)
            delta = jnp.sum(pt * dpt, axis=0, keepdims=True)
            if dl is not None:
                delta = delta - dl
            dst = pt * (dpt - delta)
            if sc is not None:
                cols = lambda a, b: a[:, b * BLOCK:(b + 1) * BLOCK]
                sks.append([jnp.sum(jnp.where(cols(slot, b), cols(dst, b), 0.0)) for b in range(len(sc))])
                dst, pt = jnp.where(slot, 0.0, dst), jnp.where(slot, 0.0, pt)
            else:
                sks.append(None)
            pts.append(pt.astype(BF16))
            dsts.append(dst.astype(BF16))
        dq2s = [lax.dot_general(dst, kk, _TN, preferred_element_type=F32) * SCALE for dst, kk in zip(dsts, kks)]
        dkks = [jnp.dot(dst, q2, preferred_element_type=F32) for dst, q2 in zip(dsts, q2s)]
        dvvs = [jnp.dot(pt, do2, preferred_element_type=F32) for pt, do2 in zip(pts, do2s)]
        return dq2s, dkks, dvvs, sks

    def stat_row(t):
        tt = t.T
        return jnp.concatenate([tt[0:1, :], tt[HEAD_DIM:HEAD_DIM + 1, :]], axis=1)

    def body(*refs):
        it = iter(refs)
        q_ref, kp_ref, kc_ref, vp_ref, vc_ref, do_ref, l_ref = (next(it) for _ in range(7))
        dl_ref = next(it) if has_dlse else None
        c_ref, s_ref, sink_ref, o_ref, ds_ref, dq_s, dk_s, dv_s, car_q, car_k, car_v = (next(it) for _ in range(11))
        b, j, i = pl.program_id(0), pl.program_id(1), pl.program_id(2)

        @pl.when(jnp.logical_and(b == 0, jnp.logical_and(j == 0, i == 0)))
        def _():
            ds_ref[...] = jnp.zeros_like(ds_ref)

        def compute():
            if geo.sink:
                kall, vall = _rows2(kp_ref, kc_ref, _tile(0)), _rows2(vp_ref, vc_ref, _tile(0))
                tps = tiles_per_kv // A_BWD_SPLIT
                nb = 2 * tps
                tiles = [[kh * tiles_per_kv + s_ * tps + t for t in range(tps)]
                         for kh in range(A_KV_HEADS) for s_ in range(A_BWD_SPLIT)]
                kdup = [_dup_head(kall, kh) for kh in range(A_KV_HEADS)]
                vdup = [_dup_head(vall, kh) for kh in range(A_KV_HEADS)]
                cat = lambda f, ts: jnp.concatenate([f(t) for t in ts], axis=0)
                dq2s, dkks, dvvs, sks = grads(
                    [cat(lambda t: _stack_heads(q_ref[0, :, _tile(t)], SCALE), ts) for ts in tiles],
                    [kdup[n // A_BWD_SPLIT] for n in range(len(tiles))],
                    [vdup[n // A_BWD_SPLIT] for n in range(len(tiles))],
                    [cat(lambda t: _stack_heads(do_ref[0, :, _tile(t)]), ts) for ts in tiles], i,
                    [jnp.concatenate([stat_row(l_ref[0, :, _tile(t)]) for t in ts], axis=1) for ts in tiles],
                    [_sink_scalars(sink_ref, 2 * ts[0], nb) for ts in tiles], [None] * len(tiles))
                lane1 = _lane((1, PAIR_W))
                dsink = jnp.zeros((1, PAIR_W), F32)
                for ts, dq2, sk in zip(tiles, dq2s, sks):
                    for n, t in enumerate(ts):
                        dq_s[:, _tile(t)] = _unstack_heads(dq2[2 * BLOCK * n:2 * BLOCK * (n + 1)])
                    for bb in range(nb):
                        dsink = dsink + jnp.where(lane1 == 2 * ts[0] + bb, sk[bb], 0.0)
                per_kv = lambda parts, kh: functools.reduce(jnp.add, parts[kh * A_BWD_SPLIT:(kh + 1) * A_BWD_SPLIT])
                second = _lane((krows, PAIR_W)) >= HEAD_DIM
                dk_s[...] = jnp.where(second, _fold_heads(per_kv(dkks, 1)), _fold_heads(per_kv(dkks, 0)))
                dv_s[...] = jnp.where(second, _fold_heads(per_kv(dvvs, 1)), _fold_heads(per_kv(dvvs, 0)))
                ds_ref[0:1, :] += dsink
            else:
                dq2s, dkks, dvvs, _ = grads(
                    [_stack_heads(q_ref[0, :, _tile(t)], SCALE) for t in range(ntile)],
                    [_rows2(kp_ref, kc_ref, _tile(t), single) for t in range(ntile)],
                    [_rows2(vp_ref, vc_ref, _tile(t), single) for t in range(ntile)],
                    [_stack_heads(do_ref[0, :, _tile(t)]) for t in range(ntile)], i,
                    [stat_row(l_ref[0, :, _tile(t)]) for t in range(ntile)], [None] * ntile,
                    [stat_row(dl_ref[0, :, _tile(t)]) for t in range(ntile)])
                for t in range(ntile):
                    dq_s[:, _tile(t)] = _unstack_heads(dq2s[t])
                    dk_s[0:krows, _tile(t)] = dkks[t]
                    dv_s[0:krows, _tile(t)] = dvvs[t]

        def emit(dq, dk, dv):
            cos, sn = c_ref[0], s_ref[0]
            o_ref[0, :, 0:qw] = _unrope(dq, cos, sn).astype(BF16)
            o_ref[0, :, qw:qw + kw] = _unrope(dk, cos, sn).astype(BF16)
            o_ref[0, :, qw + kw:qw + 2 * kw] = dv.astype(BF16)
            if qw + 2 * kw < VAR_W:
                o_ref[0, :, qw + 2 * kw:VAR_W] = jnp.zeros((BLOCK, VAR_W - qw - 2 * kw), BF16)

        if single:
            compute()
            emit(dq_s[...], dk_s[0:BLOCK, :], dv_s[0:BLOCK, :])
            return

        @pl.when(i == 0)
        def _():
            car_q[...] = jnp.zeros_like(car_q)
            car_k[...] = jnp.zeros_like(car_k)
            car_v[...] = jnp.zeros_like(car_v)

        @pl.when(i == nblk)
        def _():
            dk_s[...] = jnp.zeros_like(dk_s)
            dv_s[...] = jnp.zeros_like(dv_s)

        pl.when(i < nblk)(compute)
        emit(car_q[...], car_k[...] + dk_s[0:BLOCK, :], car_v[...] + dv_s[0:BLOCK, :])
        car_q[...] = dq_s[...]
        car_k[...] = dk_s[BLOCK:2 * BLOCK, :]
        car_v[...] = dv_s[BLOCK:2 * BLOCK, :]

    cur = lambda i: jnp.minimum(i, nblk - 1)
    prv = lambda i: jnp.maximum(jnp.minimum(i, nblk - 1) - 1, 0)
    outb = lambda i: jnp.maximum(i - 1, 0)
    qrow = pl.BlockSpec((1, BLOCK, qw), lambda b, j, i: (b, cur(i), j))
    in_specs = [
        pl.BlockSpec((1, BLOCK, qw), lambda b, j, i: (b, cur(i), geo.qidx(j))),
        pl.BlockSpec((1, BLOCK, kw), lambda b, j, i: (b, prv(i), geo.kidx(j))),
        pl.BlockSpec((1, BLOCK, kw), lambda b, j, i: (b, cur(i), geo.kidx(j))),
        pl.BlockSpec((1, BLOCK, kw), lambda b, j, i: (b, prv(i), geo.vidx(j))),
        pl.BlockSpec((1, BLOCK, kw), lambda b, j, i: (b, cur(i), geo.vidx(j))),
        qrow, qrow,
    ]
    ins = [view(qkv, VAR_W)] * 5 + [view(do, qw), view(lse, qw)]
    if has_dlse:
        in_specs.append(qrow)
        ins.append(view(dlse, qw))
    in_specs += [
        pl.BlockSpec((1, BLOCK, PAIR_W), lambda b, j, i: (b, outb(i), j)),
        pl.BlockSpec((1, BLOCK, PAIR_W), lambda b, j, i: (b, outb(i), j)),
        pl.BlockSpec(memory_space=pltpu.SMEM),
    ]
    ins += [view(cosf, PAIR_W), view(sins, PAIR_W), sinks]
    scratch = [pltpu.VMEM((BLOCK, qw), F32), pltpu.VMEM((2 * BLOCK, kw), F32), pltpu.VMEM((2 * BLOCK, kw), F32),
               pltpu.VMEM((BLOCK, qw), F32), pltpu.VMEM((BLOCK, kw), F32), pltpu.VMEM((BLOCK, kw), F32)]
    dqkv, dsink = _pcall(
        body, name=name, grid=(NB, r, nsteps), in_specs=in_specs,
        out_specs=[pl.BlockSpec((1, BLOCK, VAR_W), lambda b, j, i: (b, outb(i), j)),
                   pl.BlockSpec((8, PAIR_W), lambda b, j, i: (0, 0))],
        out_shape=[jax.ShapeDtypeStruct((NB, tsub, r * VAR_W), BF16), jax.ShapeDtypeStruct((8, PAIR_W), F32)],
        scratch_shapes=scratch, compiler_params=_params(("arbitrary", "arbitrary", "arbitrary")),
    )(*ins)
    return dqkv, dsink


class _Rows:
    def __init__(self, N, T, tm):
        self.N, self.tm, self.tpe, self.grid = N, tm, T // tm, (N // tm,)

    def row(self, w, col=0):
        return pl.BlockSpec((self.tm, w), lambda i: (i, col))

    def ex(self, w):
        return pl.BlockSpec((1, 1, w), lambda i: (i // self.tpe, 0, 0))

    def const(self, shape):
        return pl.BlockSpec(shape, lambda i: tuple(0 for _ in shape))

    def view(self, w, r):
        return pl.BlockSpec((None, self.tm // r, r * w), lambda i: (i // self.tpe, i % self.tpe, 0))

    def first_of_example(self):
        return pl.program_id(0) % self.tpe == 0


def _acc(ref, first, val):
    @pl.when(first)
    def _():
        ref[0] = val

    @pl.when(jnp.logical_not(first))
    def _():
        ref[0] += val


def _colsum(v):
    return jnp.sum(v, axis=0, keepdims=True)


def _ln_stats(r):
    mu = jnp.mean(r, axis=-1, keepdims=True)
    xc = r - mu
    var = jnp.mean(xc * xc, axis=-1, keepdims=True)
    rstd = lax.rsqrt(var + LN_EPS)
    return xc * rstd, rstd


def _ln_bwd(dy, xhat, rstd, gain):
    dxh = dy * gain
    return rstd * (dxh - jnp.mean(dxh, axis=-1, keepdims=True) - xhat * jnp.mean(dxh * xhat, axis=-1, keepdims=True))


def _from_view(ref, scr, r):
    if r == 1:
        return ref[...]
    rows, w = ref.shape[0], ref.shape[1] // r
    for j in range(r):
        for c in range(w // LANES):
            scr.at[c][pl.ds(j, rows, stride=r), :] = ref[:, j * w + c * LANES:j * w + (c + 1) * LANES].astype(F32)
    return jnp.concatenate([scr[c] for c in range(w // LANES)], axis=1)


def _to_view(val, ref, scr, r):
    if r == 1:
        ref[...] = val.astype(ref.dtype)
        return
    rows, w = ref.shape[0], ref.shape[1] // r
    for c in range(w // LANES):
        scr[c] = val[:, c * LANES:(c + 1) * LANES]
    for j in range(r):
        for c in range(w // LANES):
            ref[:, j * w + c * LANES:j * w + (c + 1) * LANES] = scr.at[c][pl.ds(j, rows, stride=r), :].astype(ref.dtype)


def _silu_parts(v):
    s = jax.nn.sigmoid(v)
    return v * s, s * (1.0 + v * (1.0 - s))


def _local_step(x, mod, positions, w_in, rest_weights, sinks, ln1_g, ln1_b, ln2_g, ln2_b, target, hook=None):
    hook = hook or (lambda event, **data: None)
    NB, T, D = x.shape
    N = NB * T
    x2 = x.reshape(N, D)
    tgt2 = target.reshape(N, D)
    shift_m, scale_m, gate_m, shift_f, scale_f, gate_f = [mod[:, None, k * D:(k + 1) * D] for k in range(6)]
    cosf, sins = _rope_tables(positions)
    col = jnp.arange(QKV_P)
    vcol = col % VAR_W
    flags = jnp.where(col < VAR_W, vcol < QA_W + KA_W, vcol < 2 * GB_W).astype(F32)[None]
    R = _Rows(N, T, _pick(T, 256))
    sds = jax.ShapeDtypeStruct
    exsum = lambda w=D: sds((NB, 1, w), F32)
    ngrp = len(B_PATTERNS)

    *qkv, u = _inproj(x2, scale_m, shift_m, w_in, cosf, sins, flags, T=T, name="inproj_qkv")
    gates = _mm(u, w_in, tb=True, b_rows=(QKV_P, w_in.shape[0] - QKV_P), out_dtype=BF16, name="inproj_gates")
    oa, la = _attn_fwd(qkv[0], sinks, None, NB=NB, T=T, name="attn_a_fwd")
    oa = oa.reshape(N, QA_W)
    (o1, l1), (o2, l2), (o3, l3) = _attn_fwd_b(qkv[1:], NB=NB, T=T, name="attn_b_fwd")
    w_a, w_b, w_o, w_gu, w_d = rest_weights()
    F = w_d.shape[0]
    dil = [r_ for _, r_ in B_PATTERNS]
    views = [R.view(GB_W, r_) for r_ in dil]
    tokbuf = pltpu.VMEM((GB_W // LANES, R.tm, LANES), F32)

    f32 = lambda ref: ref[...].astype(F32)
    Rm = _Rows(N, T, _pick(T, 512))

    def mix_out(o1r, o2r, o3r, l1r, l2r, l3r, oa_r, ga_r, gb_r, x_r, gm_r, g_r, b_r, sf_r, hf_r, wa_r, wb_r, wo_r,
                ob_ref, ya_ref, yb_ref, mg_ref, y_ref, r1_ref, u2_ref, *bufs):
        os_ = [_from_view(ref, bufs[n], dil[n]) for n, ref in enumerate((o1r, o2r, o3r))]
        la, lb, lc = [_from_view(ref, bufs[3 + n], dil[n]) for n, ref in enumerate((l1r, l2r, l3r))]
        mx = jnp.maximum(jnp.maximum(la, lb), lc)
        ea, eb, ec = jnp.exp(la - mx), jnp.exp(lb - mx), jnp.exp(lc - mx)
        ob = ((ea * os_[0] + eb * os_[1] + ec * os_[2]) / (ea + eb + ec)).astype(BF16)
        ob_ref[...] = ob
        ya = jnp.dot(oa_r[...], wa_r[...], preferred_element_type=F32).astype(BF16)
        yb = jnp.concatenate([jnp.dot(ob, wb_r[s_], preferred_element_type=F32)
                              for s_ in range(w_b.shape[0])], axis=1).astype(BF16)
        merged = (jax.nn.sigmoid(f32(ga_r)) * ya.astype(F32) + jax.nn.sigmoid(f32(gb_r)) * yb.astype(F32)).astype(BF16)
        y = jnp.dot(merged, wo_r[...], preferred_element_type=F32)
        r1 = ALPHA * x_r[...] + (1.0 + gm_r[0]) * y
        xhat, _ = _ln_stats(r1)
        x1 = xhat * g_r[...] + b_r[...]
        ya_ref[...], yb_ref[...], mg_ref[...], y_ref[...], r1_ref[...] = ya, yb, merged, y, r1
        u2_ref[...] = (x1 * (1.0 + sf_r[0]) + hf_r[0]).astype(BF16)

    mviews = [Rm.view(GB_W, r_) for r_ in dil]
    ob, ya, yb, merged, y, r1, u2 = _pcall(
        mix_out, name="mix_out", grid=Rm.grid,
        in_specs=mviews + mviews + [Rm.row(QA_W), Rm.row(D, 0), Rm.row(D, 1), Rm.row(D), Rm.ex(D), Rm.const((1, D)),
                                    Rm.const((1, D)), Rm.ex(D), Rm.ex(D), Rm.const(w_a.shape), Rm.const(w_b.shape),
                                    Rm.const(w_o.shape)],
        out_specs=[Rm.row(GB_W)] + [Rm.row(D)] * 6,
        out_shape=[sds((N, GB_W), BF16)] + [sds((N, D), BF16)] * 3 + [sds((N, D), F32)] * 2 + [sds((N, D), BF16)],
        scratch_shapes=[pltpu.VMEM((GB_W // LANES, Rm.tm, LANES), F32)] * 6,
        compiler_params=_params(("parallel",)))(o1, o2, o3, l1, l2, l3, oa, gates, gates, x2, gate_m, ln1_g, ln1_b,
                                                scale_f, shift_f, w_a, w_b, w_o)

    w_gu = w_gu() if callable(w_gu) else w_gu
    tnf = w_gu.shape[2]
    nft = w_gu.shape[0] // 2
    tmf = _pick(N, 512)

    def ffn_up(u_r, wg_r, wu_r, hg_ref, hu_ref, a_ref):
        hg = jnp.dot(u_r[...], wg_r[...], preferred_element_type=F32)
        hu = jnp.dot(u_r[...], wu_r[...], preferred_element_type=F32)
        sl, _ = _silu_parts(hg)
        hg_ref[...] = hg.astype(BF16)
        hu_ref[...] = hu.astype(BF16)
        a_ref[...] = (sl * hu).astype(BF16)

    ftile = pl.BlockSpec((tmf, tnf), lambda j, i: (i, j))
    hg, hu, act = _pcall(
        ffn_up, name="ffn_up", grid=(nft, N // tmf),
        in_specs=[pl.BlockSpec((tmf, D), lambda j, i: (i, 0)), pl.BlockSpec((None, D, tnf), lambda j, i: (j, 0, 0)),
                  pl.BlockSpec((None, D, tnf), lambda j, i: (j + nft, 0, 0))],
        out_specs=[ftile] * 3, out_shape=[sds((N, F), BF16)] * 3,
        compiler_params=_params(("arbitrary", "parallel")))(u2, w_gu, w_gu)
    fchunk = _pick(F, 768)

    def ffn_down_norm2(act_r, wd_r, r1_r, g1_r, b1_r, t_r, gf_r, g_r, b_r, hg_r, hu_r,
                       dy2_ref, dx1_ref, dgf_ref, dg_ref, db_ref, loss_ref, dh_ref):
        first = R.first_of_example()
        y2v = jnp.dot(act_r[...], wd_r[...], preferred_element_type=F32)
        x1 = _ln_stats(r1_r[...])[0] * g1_r[...] + b1_r[...]
        r2 = ALPHA * x1 + (1.0 + gf_r[0]) * y2v
        xhat, rstd = _ln_stats(r2)
        err = xhat * g_r[...] + b_r[...] - t_r[...]
        dx2 = err * (1.0 / D)
        dr2 = _ln_bwd(dx2, xhat, rstd, g_r[...])
        dy2 = ((1.0 + gf_r[0]) * dr2).astype(BF16)
        dy2_ref[...] = dy2
        dx1_ref[...] = ALPHA * dr2
        _acc(dgf_ref, first, _colsum(dr2 * y2v))
        _acc(dg_ref, first, _colsum(dx2 * xhat))
        _acc(db_ref, first, _colsum(dx2))
        part = 0.5 * jnp.sum(jnp.mean(err * err, axis=-1, keepdims=True))
        _acc(loss_ref, first, jnp.broadcast_to(part, (1, 128)))
        for t in range(F // fchunk):
            cs = slice(t * fchunk, (t + 1) * fchunk)
            da = lax.dot_general(dy2, wd_r[cs, :], _NT, preferred_element_type=F32)
            sl, dsl = _silu_parts(hg_r[:, cs].astype(F32))
            dh_ref[:, cs] = (da * hu_r[:, cs].astype(F32) * dsl).astype(BF16)
            dh_ref[:, F + t * fchunk:F + (t + 1) * fchunk] = (da * sl).astype(BF16)

    dy2, dx1p, dgate_f, dg2, db2, loss_p, dh = _pcall(
        ffn_down_norm2, name="ffn_down_norm2", grid=R.grid,
        in_specs=[R.row(F), R.const((F, D)), R.row(D), R.const((1, D)), R.const((1, D)), R.row(D), R.ex(D),
                  R.const((1, D)), R.const((1, D)), R.row(F), R.row(F)],
        out_specs=[R.row(D), R.row(D), R.ex(D), R.ex(D), R.ex(D), R.ex(128), R.row(2 * F)],
        out_shape=[sds((N, D), BF16), sds((N, D), F32), exsum(), exsum(), exsum(), exsum(128), sds((N, 2 * F), BF16)],
        compiler_params=_params(("arbitrary",)))(act, w_d, r1, ln1_g, ln1_b, tgt2, gate_f, ln2_g, ln2_b, hg, hu)

    g_wd = _mm(act, dy2, ta=True, out_dtype=BF16, name="ffn_down_dw")
    g_wgu = _mm(u2, dh, ta=True, out3=w_gu.shape[0], out_dtype=BF16, name="ffn_up_dw")
    hook("ffn_grads", g_wgu=g_wgu)

    def ffn_up_dx_norm1(dh_r, w_r, dx1p_r, r1_r, y_r, sf_r, gm_r, g_r, b_r,
                        dxp_ref, dy_ref, dsf_ref, dhf_ref, dgm_ref, dg_ref, db_ref):
        first = R.first_of_example()
        du2v = None
        for s_ in range(w_gu.shape[0]):
            part = lax.dot_general(dh_r[:, s_ * tnf:(s_ + 1) * tnf], w_r[s_], _NT, preferred_element_type=F32)
            du2v = part if du2v is None else du2v + part
        dx1 = dx1p_r[...] + du2v * (1.0 + sf_r[0])
        xhat, rstd = _ln_stats(r1_r[...])
        dr1 = _ln_bwd(dx1, xhat, rstd, g_r[...])
        dxp_ref[...] = ALPHA * dr1
        dy_ref[...] = ((1.0 + gm_r[0]) * dr1).astype(BF16)
        _acc(dsf_ref, first, _colsum(du2v * (xhat * g_r[...] + b_r[...])))
        _acc(dhf_ref, first, _colsum(du2v))
        _acc(dgm_ref, first, _colsum(dr1 * y_r[...]))
        _acc(dg_ref, first, _colsum(dx1 * xhat))
        _acc(db_ref, first, _colsum(dx1))

    dxp, dy, dscale_f, dshift_f, dgate_m, dg1, db1 = _pcall(
        ffn_up_dx_norm1, name="ffn_up_dx_norm1", grid=R.grid,
        in_specs=[R.row(2 * F), R.const(w_gu.shape)] + [R.row(D)] * 3 + [R.ex(D), R.ex(D), R.const((1, D)),
                                                                        R.const((1, D))],
        out_specs=[R.row(D), R.row(D)] + [R.ex(D)] * 5,
        out_shape=[sds((N, D), F32), sds((N, D), BF16)] + [exsum()] * 5,
        compiler_params=_params(("arbitrary",)))(dh, w_gu, dx1p, r1, y, scale_f, gate_m, ln1_g, ln1_b)

    g_wo = _mm(merged, dy, ta=True, out_dtype=BF16, name="out_proj_dw")

    seg = (jnp.arange(GB_W)[:, None] // HEAD_DIM == jnp.arange(GB_W)[None, :] // HEAD_DIM).astype(BF16)

    def mix_out_bwd(dy_r, ya_r, yb_r, ga_r, gb_r, wo_r, wa_r, wb_r, o1r, o2r, o3r, l1r, l2r, l3r, seg_r,
                    dya_ref, dyb_ref, dg_ref, doa_ref, d1, d2, d3, e1, e2, e3, *bufs):
        dm = lax.dot_general(dy_r[...], wo_r[...], _NT, preferred_element_type=F32).astype(BF16).astype(F32)
        sa, sb = jax.nn.sigmoid(f32(ga_r)), jax.nn.sigmoid(f32(gb_r))
        dya, dyb = (dm * sa).astype(BF16), (dm * sb).astype(BF16)
        dya_ref[...], dyb_ref[...] = dya, dyb
        dg_ref[:, :D] = (dm * f32(ya_r) * sa * (1.0 - sa)).astype(BF16)
        dg_ref[:, D:] = (dm * f32(yb_r) * sb * (1.0 - sb)).astype(BF16)
        doa_ref[...] = lax.dot_general(dya, wa_r[...], _NT, preferred_element_type=F32).astype(BF16)
        ds_ = D // w_b.shape[0]
        dob = None
        for s_ in range(w_b.shape[0]):
            part = lax.dot_general(dyb[:, s_ * ds_:(s_ + 1) * ds_], wb_r[s_], _NT, preferred_element_type=F32)
            dob = part if dob is None else dob + part
        dob_v = dob
        os_ = [_from_view(ref, bufs[n], dil[n]) for n, ref in enumerate((o1r, o2r, o3r))]
        la, lb, lc = [_from_view(ref, bufs[3 + n], dil[n]) for n, ref in enumerate((l1r, l2r, l3r))]
        mx = jnp.maximum(jnp.maximum(la, lb), lc)
        ea, eb, ec = jnp.exp(la - mx), jnp.exp(lb - mx), jnp.exp(lc - mx)
        inv = 1.0 / (ea + eb + ec)
        ws = [ea * inv, eb * inv, ec * inv]

        def headsum(v):
            hi = v.astype(BF16)
            lo = (v - hi.astype(F32)).astype(BF16)
            sm = seg_r[...]
            return jnp.dot(hi, sm, preferred_element_type=F32) + jnp.dot(lo, sm, preferred_element_type=F32)

        dws = [headsum(dob_v * o) for o in os_]
        mean = ws[0] * dws[0] + ws[1] * dws[1] + ws[2] * dws[2]
        for n, (w_, dw_, d_ref, e_ref) in enumerate(zip(ws, dws, (d1, d2, d3), (e1, e2, e3))):
            _to_view(w_ * dob_v, d_ref, bufs[6], dil[n])
            _to_view(w_ * (dw_ - mean), e_ref, bufs[7], dil[n])

    vshape = lambda r_, dt: sds((NB, T // r_, r_ * GB_W), dt)
    dya, dyb, dgates, doa, *mb = _pcall(
        mix_out_bwd, name="mix_out_bwd", grid=R.grid,
        in_specs=[R.row(D)] * 3 + [R.row(D, 0), R.row(D, 1), R.const(w_o.shape), R.const(w_a.shape), R.const(w_b.shape)]
        + views + views + [R.const((GB_W, GB_W))],
        out_specs=[R.row(D), R.row(D), R.row(2 * D), R.row(QA_W)] + views + views,
        out_shape=[sds((N, D), BF16), sds((N, D), BF16), sds((N, 2 * D), BF16), sds((N, QA_W), BF16)]
        + [vshape(r_, BF16) for r_ in dil] + [vshape(r_, F32) for r_ in dil],
        scratch_shapes=[tokbuf] * 8,
        compiler_params=_params(("parallel",)))(dy, ya, yb, gates, gates, w_o, w_a, w_b, o1, o2, o3, l1, l2, l3, seg)
    do_b, dlse_b = mb[:3], mb[3:]

    g_wa = _mm(oa, dya, ta=True, out_dtype=BF16, name="branch_a_dw")
    g_wb = _mm(ob, dyb, ta=True, out3=w_b.shape[0], out_dtype=BF16, name="branch_b_dw")
    hook("rest_grads", g_wa=g_wa, g_wb=g_wb, g_wo=g_wo, g_wd=g_wd)

    dqkv_a, dsink = _attn_bwd(qkv[0], doa, la, None, cosf, sins, sinks, None, NB=NB, T=T, name="attn_a_bwd")
    hook("attn_a_bwd_done")
    dqkv = [dqkv_a]
    for g in range(ngrp):
        dqkv.append(_attn_bwd(qkv[1 + g], do_b[g], (l1, l2, l3)[g], dlse_b[g], cosf, sins, sinks, g, NB=NB, T=T,
                              name=f"attn_b{g}_bwd")[0])
        hook(f"attn_b{g}_bwd_done")

    g_win = [_mm(d3.reshape(N, VAR_W), u, ta=True, out_dtype=BF16, name=f"inproj_dw{v}") if VAR_DIL[v] == 1
             else _dw_view(d3, u, VAR_DIL[v], name=f"inproj_dw{v}") for v, d3 in enumerate(dqkv)]
    g_win.append(_mm(dgates, u, ta=True, out_dtype=BF16, name=f"inproj_dw{N_VAR}"))
    hook("win_grads", g_win=g_win)
    wvar = lambda v: (w_in, (VAR_W, D), (v, 0))
    dview = lambda v: (dqkv[v], VAR_DIL[v])
    du = _mm_multi([dview(0)], [wvar(0)], M=N, T=T, name="inproj_dx0")
    hook("inproj_dx0_done")
    def x_bwd(duv, ins, outs):
        (dxp_r, x_r, sm_r), (gx_ref, dsm_ref, dhm_ref) = ins, outs
        first = R.first_of_example()
        gx_ref[...] = dxp_r[...] + duv * (1.0 + sm_r[0])
        _acc(dsm_ref, first, _colsum(duv * x_r[...]))
        _acc(dhm_ref, first, _colsum(duv))

    gx, dscale_m, dshift_m = _mm_multi(
        [dview(v) for v in range(1, N_VAR)] + [dgates],
        [wvar(v) for v in range(1, N_VAR)] + [(w_in, (2 * D, D), (QKV_P // (2 * D), 0))],
        M=N, T=T, add=du, tm=R.tm, name="inproj_dx1",
        post=([dxp, x2, scale_m], [R.row(D), R.row(D), R.ex(D)], [R.row(D), R.ex(D), R.ex(D)],
              [sds((N, D), F32), exsum(), exsum()], x_bwd))
    hook("inproj_dx1_done")

    dmod =jnp.concatenate([dshift_m, dscale_m, dgate_m, dshift_f, dscale_f, dgate_f], axis=-1)[:, 0]
    ln_grads = jnp.concatenate([dg1, db1, dg2, db2], axis=1)
    return dict(loss=loss_p[:, 0, 0], grad_x=gx.reshape(NB, T, D), g_win=g_win, g_wa=g_wa, g_wb=g_wb, g_wo=g_wo,
                g_wgu=g_wgu, g_wd=g_wd, dmod=dmod, ln_grads=ln_grads, dsink=dsink[0, :A_Q_HEADS])


def _coords():
    return lax.axis_index("x"), lax.axis_index("y"), lax.axis_index("c")


def _allgather_small(blk, *, name):
    m_per, n = blk.shape

    def body(x_ref, out_ref, send_sems, recv_sems, local_sem):
        x, y, c = _coords()
        me, sibling = (x, y, c), (x, y, 1 - c)
        chips = [(1 - x, y), (x, 1 - y), (1 - x, 1 - y)]

        def rows(px, py, pc):
            return out_ref.at[pl.ds((4 * px + 2 * py + pc) * m_per, m_per), :]

        def copy(k, block, to, src=None):
            return pltpu.make_async_remote_copy(
                src_ref=rows(*block) if src is None else src, dst_ref=rows(*block),
                send_sem=send_sems.at[k], recv_sem=recv_sems.at[k], device_id=to, device_id_type=MESH)

        mine = pltpu.make_async_copy(x_ref, rows(*me), local_sem)
        mine.start()
        first = [copy(0, me, sibling, src=x_ref)]
        first += [copy(1 + j, me, (*chip, c), src=x_ref) for j, chip in enumerate(chips)]
        for cp in first:
            cp.start()
        passed = [copy(4 + j, (*chip, c), sibling) for j, chip in enumerate(chips)]
        for j, chip in enumerate(chips):
            copy(1 + j, (*chip, c), me).wait_recv()
            passed[j].start()
        copy(0, sibling, me).wait_recv()
        for j, chip in enumerate(chips):
            copy(4 + j, (*chip, 1 - c), me).wait_recv()
        for cp in first + passed:
            cp.wait_send()
        mine.wait()

    return _pcall(
        body, name=name, out_shape=jax.ShapeDtypeStruct((8 * m_per, n), blk.dtype),
        in_specs=[pl.BlockSpec(memory_space=pltpu.VMEM)], out_specs=pl.BlockSpec(memory_space=pltpu.VMEM),
        scratch_shapes=[pltpu.SemaphoreType.DMA((7,)), pltpu.SemaphoreType.DMA((7,)), pltpu.SemaphoreType.DMA],
        compiler_params=pltpu.CompilerParams(vmem_limit_bytes=VMEM_LIMIT_BYTES),
    )(blk)


def _exchange(srcs, dsts, plan, *, name, dst_inits=None):
    na = len(dsts)
    nrem = len(plan(0, 0, 0))

    def body(*refs):
        refs = list(refs)
        src_refs = [refs.pop(0) for _ in range(na)] if srcs is not None else None
        if dst_inits is not None:
            del refs[:na]
        dst_refs, (send_sems, recv_sems) = refs[:na], refs[na:]
        start, wait = _copies(dst_refs if src_refs is None else src_refs, dst_refs, send_sems, recv_sems, plan)
        start()
        wait()

    hbm = pl.BlockSpec(memory_space=pl.ANY)
    ins = (list(srcs) if srcs is not None else []) + (list(dst_inits) if dst_inits is not None else [])
    base = na if srcs is not None else 0
    aliases = {base + a: a for a in range(na)} if dst_inits is not None else {}
    return _pcall(
        body, name=name, out_shape=list(dsts), in_specs=[hbm] * len(ins), out_specs=[hbm] * na,
        input_output_aliases=aliases,
        scratch_shapes=[pltpu.SemaphoreType.DMA((na * nrem,)), pltpu.SemaphoreType.DMA((na * nrem,))],
    )(*ins)


def _other_chips(x, y):
    return [(1 - x, y), (x, 1 - y), (1 - x, 1 - y)]


def _round(ride, carrier, name):
    if carrier is not None:
        _RIDES.setdefault(carrier, []).append(ride)
        return
    srcs = ride.srcs() if callable(ride.srcs) else ride.srcs
    inits = ride.dst_inits() if callable(ride.dst_inits) else ride.dst_inits
    ride.out = list(_exchange(srcs, ride.dsts, ride.plan, name=name, dst_inits=inits))


class _Gather:
    def __init__(self, shards, chip, tag, carriers=(None, None)):
        def plan_ici(x, y, c):
            k = 2 * x + y
            return [((c,), (k, c), (2 * px + py, c), (px, py, c)) for px, py in _other_chips(x, y)]

        def plan_d2d(x, y, c):
            return [((2 * px + py, c), (2 * px + py, c), (2 * px + py, 1 - c), (x, y, 1 - c))
                    for px, py in _other_chips(x, y)]

        def plan_near(x, y, c):
            k = 2 * x + y
            return [((c,), (k, c), (2 * px + py, c), (px, py, c)) for px, py in ((1 - x, y), (x, 1 - y))]

        def plan_far(x, y, c):
            kx, ky, kd = 2 * (1 - x) + y, 2 * x + (1 - y), 2 * (1 - x) + (1 - y)
            hp = shards[0].shape[1] // 2
            top, bottom = pl.ds(0, hp), pl.ds(hp, hp)
            return [((kx, c, top), (kx, c, top), (kd, c, top), (x, 1 - y, c)),
                    ((ky, c, bottom), (ky, c, bottom), (kd, c, bottom), (1 - x, y, c))]

        self.shards, self.chip = shards, chip
        dsts = [jax.ShapeDtypeStruct((4,) + s.shape, s.dtype) for s in shards]
        if len(carriers) == 3:
            near = _Ride(shards, dsts, plan_near)
            ici = _Ride(None, dsts, plan_far, dst_inits=lambda: near.out)
            _round(near, carriers[0], f"gather_{tag}_near")
            _round(ici, carriers[1], f"gather_{tag}_far")
        else:
            ici = _Ride(shards, dsts, plan_ici)
            _round(ici, carriers[0], f"gather_{tag}_ici")
        self.d2d = _Ride(None, dsts, plan_d2d, dst_inits=lambda: ici.out)
        _round(self.d2d, carriers[-1], f"gather_{tag}_d2d")

    def result(self):
        full = [lax.dynamic_update_index_in_dim(f, s, self.chip, 0) for f, s in zip(self.d2d.out, self.shards)]
        return [f.reshape((4, 2 * f.shape[2], f.shape[3])) for f in full]


def _index_operand(i):
    return jnp.reshape(i, (1,)).astype(jnp.int32)


def _add_pairs(g, f, ci, *, name):
    s, _, hr, wd = g.shape
    tr = _pick(hr, 600, 16)

    def body(c_ref, a_ref, b_ref, o_ref):
        o_ref[...] = (a_ref[...].astype(F32) + b_ref[...].astype(F32)).astype(BF16)

    spec = pl.BlockSpec((1, tr, wd), lambda j, i, c: (j, i, 0))
    grid_spec = pltpu.PrefetchScalarGridSpec(
        num_scalar_prefetch=1, grid=(s, hr // tr),
        in_specs=[pl.BlockSpec((1, None, tr, wd), lambda j, i, c: (j, c[0], i, 0)), spec], out_specs=spec)
    return _pcall(body, name=name, grid_spec=grid_spec, out_shape=jax.ShapeDtypeStruct(f.shape, BF16),
                  compiler_params=_params(("parallel", "parallel")))(_index_operand(ci), g, f)


def _sum_chips(landed, pairs, chip, ci, *, name):
    s, hr, wd = landed.shape
    tr = _pick(hr, 600, 16)

    def body(k_ref, l_ref, p_ref, o_ref):
        acc = None
        for k in range(s):
            part = jnp.where(k_ref[0] == k, p_ref[k], l_ref[k]).astype(F32)
            acc = part if acc is None else acc + part
        o_ref[...] = acc

    spec = pl.BlockSpec((s, tr, wd), lambda i, k: (0, i, 0))
    grid_spec = pltpu.PrefetchScalarGridSpec(
        num_scalar_prefetch=1, grid=(hr // tr,), in_specs=[spec, spec],
        out_specs=pl.BlockSpec((None, tr, wd), lambda i, k: (k[1], i, 0)))
    where = jnp.stack([chip, ci]).astype(jnp.int32)
    return _pcall(body, name=name, grid_spec=grid_spec, out_shape=jax.ShapeDtypeStruct((2, hr, wd), F32),
                  compiler_params=_params(("parallel",)))(where, landed, pairs)


class _ReduceScatter:
    def __init__(self, gs, chip, ci, tag):
        self.gs, self.chip, self.ci, self.tag = gs, chip, ci, tag
        self.half_t = [jax.ShapeDtypeStruct((g.shape[0],) + g.shape[2:], BF16) for g in gs]

    def pair(self, carrier=None):
        plan = lambda x, y, c: [((slice(None), 1 - c), (), (), (x, y, 1 - c))]
        self.r1 = _Ride(self.gs, self.half_t, plan)
        _round(self.r1, carrier, f"reduce_{self.tag}_pair")

    def chips(self, carrier=None):
        def plan(x, y, c):
            k = 2 * x + y
            return [((2 * px + py,), (k,), (2 * px + py,), (px, py, c)) for px, py in _other_chips(x, y)]

        self.pairs = [_add_pairs(g, f, self.ci, name=f"reduce_{self.tag}_pair_add{n}")
                      for n, (g, f) in enumerate(zip(self.gs, self.r1.out))]
        self.r2 = _Ride(self.pairs, self.half_t, plan)
        _round(self.r2, carrier, f"reduce_{self.tag}_chips")

    def halves(self, carrier=None):
        plan = lambda x, y, c: [((c,), (c,), (1 - c,), (x, y, 1 - c))]
        mine = [_sum_chips(l, p, self.chip, self.ci, name=f"reduce_{self.tag}_chip_sum{n}")
                for n, (l, p) in enumerate(zip(self.r2.out, self.pairs))]
        self.r3 = _Ride(None, [jax.ShapeDtypeStruct(m.shape, F32) for m in mine], plan, dst_inits=mine)
        _round(self.r3, carrier, f"reduce_{self.tag}_halves")

    def result(self):
        return [b.reshape(2 * b.shape[1], b.shape[2]) for b in self.r3.out]


def _ada_fwd(c_all, w_sh, b_sh, *, name):
    nb, d = c_all.shape
    wcols = w_sh.shape[1]
    tn = _pick(wcols, 512)

    def body(c_ref, w_ref, b_ref, o_ref, a_ref):
        cv = c_ref[...]
        act = cv * jax.nn.sigmoid(cv)
        a_ref[...] = act
        o_ref[...] = jnp.dot(act.astype(BF16), w_ref[...].astype(BF16), preferred_element_type=F32) + b_ref[...]

    return _pcall(
        body, name=name, grid=(wcols // tn,),
        in_specs=[pl.BlockSpec((nb, d), lambda j: (0, 0)), pl.BlockSpec((d, tn), lambda j: (0, j)),
                  pl.BlockSpec((1, tn), lambda j: (0, j))],
        out_specs=[pl.BlockSpec((nb, tn), lambda j: (0, j)), pl.BlockSpec((nb, d), lambda j: (0, 0))],
        out_shape=[jax.ShapeDtypeStruct((nb, wcols), F32), jax.ShapeDtypeStruct((nb, d), F32)],
        compiler_params=_params(("arbitrary",)))(c_all, w_sh, b_sh)


def _sum_devices(g, *, name):
    nd, m, w = g.shape

    def body(g_ref, o_ref):
        acc = g_ref[0]
        for k in range(1, nd):
            acc = acc + g_ref[k]
        o_ref[...] = acc

    return _pcall(body, name=name, out_shape=jax.ShapeDtypeStruct((m, w), F32),
                  compiler_params=pltpu.CompilerParams(vmem_limit_bytes=VMEM_LIMIT_BYTES))(g)


def _adamw(w, g, m, v, *, name):
    rows, cols = w.shape[-2:]
    tr = _pick(rows, max(8, (1 << 18) // cols), 8)
    c1 = 1.0 / (1.0 - ADAM_B1 ** ADAM_STEP)
    c2 = 1.0 / (1.0 - ADAM_B2 ** ADAM_STEP)

    def body(w_ref, g_ref, m_ref, v_ref, d_ref, nm_ref, nv_ref):
        gv = g_ref[...]
        nm = ADAM_B1 * m_ref[...] + (1.0 - ADAM_B1) * gv
        nv = ADAM_B2 * v_ref[...] + (1.0 - ADAM_B2) * (gv * gv)
        d_ref[...] = -ADAM_LR * ((nm * c1) / (jnp.sqrt(nv * c2) + ADAM_EPS) + ADAM_WD * w_ref[...])
        nm_ref[...] = nm
        nv_ref[...] = nv

    gspec = pl.BlockSpec((tr, cols), lambda i: (i, 0))
    spec = pl.BlockSpec((None, tr, cols), lambda i: (0, i, 0)) if w.ndim == 3 else gspec
    shp = jax.ShapeDtypeStruct(w.shape, F32)
    return _pcall(body, name=name, grid=(rows // tr,), in_specs=[spec, gspec, spec, spec], out_specs=[spec] * 3,
                  out_shape=[shp] * 3, compiler_params=_params(("parallel",)))(w, g, m, v)


def _permute_in_rows(wt):
    ngrp = len(B_PATTERNS)
    qb, kb, vb = (wt[A_W + n * QB_W:A_W + (n + 1) * QB_W] for n in range(3))
    parts = [wt[:A_W], jnp.zeros((VAR_W - A_W, wt.shape[1]), wt.dtype)]
    for g in range(ngrp):
        parts += [t[g * GB_W:(g + 1) * GB_W] for t in (qb, kb, vb)]
    return jnp.concatenate(parts + [wt[A_W + 3 * QB_W:]], axis=0)


def _unpermute_in_grads(pieces):
    ga, groups, gg = pieces[0], pieces[1:-1], pieces[-1]
    rows = [ga[:A_W]]
    for n in range(3):
        rows += [gp[n * GB_W:(n + 1) * GB_W] for gp in groups]
    return jnp.concatenate(rows + [gg], axis=0)


def kernel(x, c, positions, w_ada, b_ada, w_in, sinks, w_branch_a, w_branch_b, w_o, ln1_g, ln1_b, w_gate_up, w_down, ln2_g, ln2_b, loss_target, m_w_ada, m_b_ada, m_w_in, m_sinks, m_w_branch_a, m_w_branch_b, m_w_o, m_ln1_g, m_ln1_b, m_w_gate_up, m_w_down, m_ln2_g, m_ln2_b, v_w_ada, v_b_ada, v_w_in, v_sinks, v_w_branch_a, v_w_branch_b, v_w_o, v_ln1_g, v_ln1_b, v_w_gate_up, v_w_down, v_ln2_g, v_ln2_b):
    xi, yi, ci = _coords()
    chip = 2 * xi + yi
    dev = 4 * xi + 2 * yi + ci
    NB, T, D = x.shape
    nchip, ndev = 4, 8
    ada_cols = w_ada.shape[2]

    ra, ro, rd = w_branch_a.shape[1], w_o.shape[1], w_down.shape[1]
    halves = lambda a: a.reshape(a.shape[:-2] + (2, a.shape[-2] // 2, a.shape[-1]))
    tr = lambda a: jnp.swapaxes(a, -1, -2)
    shards = [halves(w.astype(BF16))
              for w in (tr(w_in[0]), w_branch_a[0], w_o[0], w_down[0], w_branch_b[0], w_gate_up[0])]
    gin = _Gather(shards[:1], chip, "w_in", carriers=("gather_c", "ada_fwd", "gather_mod"))

    c_blk = jnp.zeros((8, D), F32).at[:NB].set(c)
    c_all = _allgather_small(c_blk, name="gather_c").reshape(ndev, 8, D)[:, :NB].reshape(ndev * NB, D)
    b_sh = lax.dynamic_slice(b_ada, (0, chip * ada_cols), (1, ada_cols))
    mod_part, c_act = _ada_fwd(c_all, w_ada[0], b_sh, name="ada_fwd")
    mod_g = _allgather_small(mod_part, name="gather_mod").reshape(nchip, 2, ndev * NB, ada_cols)[:, 0]
    mod_all = jnp.transpose(mod_g, (1, 0, 2)).reshape(ndev * NB, nchip * ada_cols)
    mod = lax.dynamic_slice(mod_all, (NB * dev, 0), (NB, nchip * ada_cols))

    (g_in,) = gin.result()
    w_in_f = _permute_in_rows(g_in.reshape(nchip * g_in.shape[1], D))
    mix = _Gather(shards[1:5], chip, "w_mix", carriers=("inproj_qkv", "attn_a_fwd"))
    ffn = _Gather(shards[5:], chip, "w_ffn", carriers=("attn_a_fwd", "attn_b_fwd", "mix_out"))

    def rest_weights():
        w_a_f, w_o_f, w_d_f, w_b_f = mix.result()
        return (w_a_f.reshape(nchip * ra, D), w_b_f, w_o_f.reshape(nchip * ro, D),
                lambda: ffn.result()[0], w_d_f.reshape(nchip * rd, D))

    red = {}

    def hook(event, **g):
        if event == "ffn_grads":
            red["ffn"] = _ReduceScatter([halves(g["g_wgu"])], chip, ci, "ffn")
            red["ffn"].pair(carrier="out_proj_dw")
        elif event == "rest_grads":
            by_chip = lambda a: halves(a.reshape(nchip, a.shape[0] // nchip, D))
            red["mix"] = _ReduceScatter([by_chip(g["g_wa"]), by_chip(g["g_wo"]), by_chip(g["g_wd"]), halves(g["g_wb"])],
                                        chip, ci, "mix")
            red["ffn"].chips(carrier="attn_a_bwd")
            red["mix"].pair(carrier="attn_a_bwd")
        elif event == "attn_a_bwd_done":
            red["ffn"].halves(carrier="attn_b0_bwd")
            red["mix"].chips(carrier="attn_b0_bwd")
        elif event == "attn_b0_bwd_done":
            red["mix"].halves(carrier="attn_b1_bwd")
        elif event == "win_grads":
            gr_in = _unpermute_in_grads(g["g_win"])
            red["w_in"] = _ReduceScatter([halves(gr_in.reshape(nchip, gr_in.shape[0] // nchip, D))], chip, ci, "w_in")
            red["w_in"].pair(carrier="inproj_dx0")
        elif event == "inproj_dx0_done":
            red["w_in"].chips(carrier="inproj_dx1")
        elif event == "inproj_dx1_done":
            red["w_in"].halves(carrier="gather_small")

    res = _local_step(x, mod, positions, w_in_f, rest_weights, sinks[0], ln1_g, ln1_b, ln2_g, ln2_b, loss_target, hook)
    (g_w_a, g_w_o, g_w_d, g_w_b), (g_w_gu,) = red["mix"].result(), red["ffn"].result()

    small_rows = 24
    misc = jnp.zeros((1, D), F32).at[0, :A_Q_HEADS].set(res["dsink"]).at[0, A_Q_HEADS].set(jnp.sum(res["loss"]))
    small = jnp.concatenate([res["dmod"].reshape(NB * 6, D), jnp.sum(res["ln_grads"], axis=0), misc,
                             jnp.zeros((small_rows - NB * 6 - 5, D), F32)], axis=0)
    small_all = _allgather_small(small, name="gather_small").reshape(ndev, small_rows, D)
    (g_w_in,) = red["w_in"].result()
    dmod_all = small_all[:, :NB * 6].reshape(ndev * NB, 6 * D)
    sums = _sum_devices(small_all, name="sum_small")
    g_b_ada = (sums[0:6] + sums[6:12]).reshape(1, 6 * D)
    g_ln1_g, g_ln1_b, g_ln2_g, g_ln2_b = (sums[12 + n][None] for n in range(4))
    g_sinks = sums[16, :A_Q_HEADS][None]
    loss = sums[16, A_Q_HEADS]
    dmod_sh = lax.dynamic_slice(dmod_all, (0, chip * ada_cols), (ndev * NB, ada_cols))
    g_w_ada = _mm(c_act, dmod_sh, ta=True, name="ada_dw")

    names = ["w_ada", "b_ada", "w_in", "sinks", "w_branch_a", "w_branch_b", "w_o", "ln1_g", "ln1_b",
             "w_gate_up", "w_down", "ln2_g", "ln2_b"]
    ws = [w_ada, b_ada, w_in, sinks, w_branch_a, w_branch_b, w_o, ln1_g, ln1_b, w_gate_up, w_down, ln2_g, ln2_b]
    ms = [m_w_ada, m_b_ada, m_w_in, m_sinks, m_w_branch_a, m_w_branch_b, m_w_o, m_ln1_g, m_ln1_b, m_w_gate_up,
          m_w_down, m_ln2_g, m_ln2_b]
    vs = [v_w_ada, v_b_ada, v_w_in, v_sinks, v_w_branch_a, v_w_branch_b, v_w_o, v_ln1_g, v_ln1_b, v_w_gate_up,
          v_w_down, v_ln2_g, v_ln2_b]
    gs = [g_w_ada, g_b_ada, g_w_in, g_sinks, g_w_a, g_w_b, g_w_o, g_ln1_g, g_ln1_b, g_w_gu, g_w_d, g_ln2_g, g_ln2_b]
    grads, deltas, new_ms, new_vs = [], [], [], []
    for name, w, g, m, v in zip(names, ws, gs, ms, vs):
        flip = tr if name == "w_in" else (lambda a: a)
        w, m, v = flip(w), flip(m), flip(v)
        g2 = g.reshape(w.shape[-2:])
        d, nm, nv = _adamw(w, g2, m, v, name="adamw_" + name)
        grads.append(flip(g2.reshape(w.shape)))
        deltas.append(flip(d))
        new_ms.append(flip(nm))
        new_vs.append(flip(nv))
    return (loss, res["grad_x"], *grads, *deltas, *new_ms, *new_vs)
```

```python
import functools

import jax
import jax.numpy as jnp
from jax import lax
from jax.experimental import pallas as pl
from jax.experimental.pallas import tpu as pltpu

F32 = jnp.float32
BF16 = jnp.bfloat16
MESH = pl.DeviceIdType.MESH

HEAD_DIM = 64
LANES = 128
PAIR_W = 2 * HEAD_DIM
BLOCK = 128
A_Q_HEADS = 16
A_KV_HEADS = 2
A_WINDOW = 128
B_PATTERNS = ((128, 1), (512, 4), (2048, 16))
B_GROUP_HEADS = 8
QA_W = A_Q_HEADS * HEAD_DIM
KA_W = A_KV_HEADS * HEAD_DIM
GB_W = B_GROUP_HEADS * HEAD_DIM
QB_W = GB_W * len(B_PATTERNS)
A_W = QA_W + 2 * KA_W
VAR_W = 3 * GB_W
N_VAR = 1 + len(B_PATTERNS)
VAR_DIL = (1,) + tuple(r for _, r in B_PATTERNS)
A_BWD_SPLIT = 4
QKV_P = N_VAR * VAR_W
ROPE_THETA = 10000.0
LN_EPS = 1e-5
NEG_INF = -1e30
DEPTH = 1
ALPHA = (2 * DEPTH) ** 0.25
SCALE = HEAD_DIM ** -0.5

ADAM_LR, ADAM_B1, ADAM_B2, ADAM_EPS, ADAM_WD, ADAM_STEP = 0.001, 0.9, 0.999, 1e-08, 0.01, 10

VMEM_LIMIT_BYTES = 56 * 1024 * 1024
MM_TILE_BYTES = 36 * 1024 * 1024
MM_WHOLE_K = 4096


def _params(sem=None):
    return pltpu.CompilerParams(dimension_semantics=sem, vmem_limit_bytes=VMEM_LIMIT_BYTES)


_RIDES = {}


def _pcall(body, *, name, **kw):
    rides = _RIDES.pop(name, None)
    if rides is None:
        return pl.pallas_call(body, name=name, **kw)
    return _riding_call(body, rides, name=name, **kw)


def _copies(src_refs, dst_refs, send_sems, recv_sems, plan):
    x, y, c = lax.axis_index("x"), lax.axis_index("y"), lax.axis_index("c")
    remote = plan(x, y, c)
    nrem = len(remote)
    at = lambda ref, idx: ref.at[idx] if idx else ref

    def copy(a, n, landing):
        si, di, ri, peer = remote[n]
        return pltpu.make_async_remote_copy(
            src_ref=at(src_refs[a], si), dst_ref=at(dst_refs[a], ri if landing else di),
            send_sem=send_sems.at[a * nrem + n], recv_sem=recv_sems.at[a * nrem + n],
            device_id=peer, device_id_type=MESH)

    order = [(a, n) for a in range(len(dst_refs)) for n in range(nrem)]

    def start():
        for a, n in order:
            copy(a, n, False).start()

    def wait():
        for a, n in order:
            copy(a, n, True).wait_recv()
        for a, n in order:
            copy(a, n, False).wait_send()

    return start, wait


class _Ride:
    def __init__(self, srcs, dsts, plan, dst_inits=None):
        self.srcs, self.dsts, self.plan, self.dst_inits, self.out = srcs, dsts, plan, dst_inits, None


def _riding_call(body, rides, *, name, in_specs, out_specs, out_shape, grid=(), scratch_shapes=(), **kw):
    single = not isinstance(out_specs, (list, tuple))
    out_specs = [out_specs] if single else list(out_specs)
    out_shape = [out_shape] if single else list(out_shape)
    n_in, n_out, n_scr = len(in_specs), len(out_specs), len(scratch_shapes)
    xin, xdsts, sems, aliases, layout = [], [], [], {}, []
    for ride in rides:
        srcs = ride.srcs() if callable(ride.srcs) else ride.srcs
        inits = ride.dst_inits() if callable(ride.dst_inits) else ride.dst_inits
        na, nrem = len(ride.dsts), len(ride.plan(0, 0, 0))
        src_at = len(xin) if srcs is not None else None
        xin += list(srcs) if srcs is not None else []
        if inits is not None:
            aliases.update({n_in + len(xin) + a: n_out + len(xdsts) + a for a in range(na)})
            xin += list(inits)
        layout.append((src_at, len(xdsts), na))
        xdsts += list(ride.dsts)
        sems += [pltpu.SemaphoreType.DMA((na * nrem,)), pltpu.SemaphoreType.DMA((na * nrem,))]

    def wrapped(*refs):
        ins, xins = refs[:n_in], refs[n_in:n_in + len(xin)]
        outs = refs[n_in + len(xin):n_in + len(xin) + n_out]
        xouts = refs[n_in + len(xin) + n_out:n_in + len(xin) + n_out + len(xdsts)]
        scr = refs[n_in + len(xin) + n_out + len(xdsts):]
        rounds = []
        for k, (ride, (src_at, dst_at, na)) in enumerate(zip(rides, layout)):
            dsts = xouts[dst_at:dst_at + na]
            srcs = dsts if src_at is None else xins[src_at:src_at + na]
            rounds.append(_copies(srcs, dsts, scr[n_scr + 2 * k], scr[n_scr + 2 * k + 1], ride.plan))
        ids = [pl.program_id(a) for a in range(len(grid))]
        first = functools.reduce(jnp.logical_and, [i == 0 for i in ids], True)
        last = functools.reduce(jnp.logical_and, [i == g - 1 for i, g in zip(ids, grid)], True)

        def start_all():
            for start, _ in rounds:
                start()

        def wait_all():
            for _, wait in rounds:
                wait()

        start_all() if not grid else pl.when(first)(start_all)
        body(*ins, *outs, *scr[:n_scr])
        wait_all() if not grid else pl.when(last)(wait_all)

    hbm = pl.BlockSpec(memory_space=pl.ANY)
    gridkw = dict(grid=grid) if grid else {}

    def run(*args):
        res = pl.pallas_call(
            wrapped, name=name, in_specs=list(in_specs) + [hbm] * len(xin),
            out_specs=out_specs + [hbm] * len(xdsts), out_shape=out_shape + xdsts,
            scratch_shapes=list(scratch_shapes) + sems, input_output_aliases=aliases,
            compiler_params=_params(("arbitrary",) * len(grid) if grid else None), **gridkw,
        )(*args, *xin)
        for ride, (_, dst_at, na) in zip(rides, layout):
            ride.out = list(res[n_out + dst_at:n_out + dst_at + na])
        return res[0] if single else list(res[:n_out])

    return run


def _pick(n, target, quantum=128):
    t = (min(target, n) // quantum) * quantum
    while t >= quantum:
        if n % t == 0:
            return t
        t -= quantum
    return n


def _mm(a, b, *, name, ta=False, tb=False, b3=False, out3=0, out_dtype=F32, add=None, tm=1024, tn=1536, tk=1536,
        b_rows=None, m_rows=None, out_rows=None, out_init=None):
    if ta:
        K, M = a.shape
    else:
        M, K = a.shape
    M = m_rows or M
    if b3 and tb:
        Nn, K2, tk = b.shape[1], b.shape[0] * b.shape[2], b.shape[2]
    elif b3:
        K2, Nn, tn = b.shape[1], b.shape[0] * b.shape[2], b.shape[2]
    elif tb:
        Nn, K2 = b.shape
        if b_rows is not None:
            Nn = b_rows[1]
    else:
        K2, Nn = b.shape
    assert K == K2, (a.shape, b.shape)
    assert b_rows is None or (tb and not b3), "b_rows needs a plain transposed b"
    if out3:
        tn = Nn // out3
    tm, tn, tk = _pick(M, tm), _pick(Nn, tn), _pick(K, tk)
    if not (b3 and tb) and K <= MM_WHOLE_K:
        tk = K
        fits = lambda: 4 * tk * (tm + tn) + 8 * tm * tn * (2 if add is not None else 1) <= MM_TILE_BYTES
        while not fits():
            if (tm >= tn or b3 or out3) and tm > 256:
                tm = _pick(M, tm - 128)
            elif not (b3 or out3) and tn > 256:
                tn = _pick(Nn, tn - 128)
            else:
                break
    nk = K // tk
    j_outer = K * Nn + (Nn // tn) * M * K < M * K + (M // tm) * K * Nn
    dn = (((0 if ta else 1,), (1 if tb else 0,)), ((), ()))

    def body(*refs):
        refs = list(refs)
        a_ref, b_ref = refs[:2]
        add_ref = refs[2] if add is not None else None
        o_ref = refs[2 + (add is not None) + (out_init is not None)]
        part = lax.dot_general(a_ref[...].astype(BF16), b_ref[...].astype(BF16), dn, preferred_element_type=F32)

        def finish(r):
            if add is not None:
                r = r + add_ref[...]
            o_ref[...] = r.astype(out_dtype)

        if nk == 1:
            finish(part)
            return
        acc = refs[-1]
        k = pl.program_id(2)

        @pl.when(k == 0)
        def _():
            acc[...] = part

        @pl.when(k > 0)
        def _():
            acc[...] += part

        @pl.when(k == nk - 1)
        def _():
            finish(acc[...])

    def spec(shape, index):
        return pl.BlockSpec(shape, (lambda j, i, k: index(i, j, k)) if j_outer else index)

    a_spec = spec((tk, tm), lambda i, j, k: (k, i)) if ta else spec((tm, tk), lambda i, j, k: (i, k))
    if b3 and tb:
        b_spec = spec((None, tn, tk), lambda i, j, k: (k, j, 0))
    elif b3:
        b_spec = spec((None, tk, tn), lambda i, j, k: (j, k, 0))
    elif tb:
        first = 0 if b_rows is None else b_rows[0]
        assert first % tn == 0, (first, tn)
        b_spec = spec((tn, tk), lambda i, j, k: (first // tn + j, k))
    else:
        b_spec = spec((tk, tn), lambda i, j, k: (k, j))
    if out3:
        o_spec = spec((None, tm, tn), lambda i, j, k: (j, i, 0))
    elif out_rows:
        o_spec = spec((tm, tn), lambda i, j, k: (out_rows[1](i), j))
    else:
        o_spec = spec((tm, tn), lambda i, j, k: (i, j))
    ins, specs, aliases = [a, b], [a_spec, b_spec], {}
    if add is not None:
        ins.append(add)
        specs.append(o_spec)
    if out_init is not None:
        aliases = {len(ins): 0}
        ins.append(out_init)
        specs.append(pl.BlockSpec(memory_space=pl.ANY))
    grid = (Nn // tn, M // tm, nk) if j_outer else (M // tm, Nn // tn, nk)
    out_shape = (out3, M, tn) if out3 else (out_rows[0], Nn) if out_rows else (M, Nn)
    return _pcall(
        body, name=name, grid=grid, in_specs=specs, out_specs=o_spec,
        out_shape=jax.ShapeDtypeStruct(out_shape, out_dtype),
        scratch_shapes=[pltpu.VMEM((tm, tn), F32)] if nk > 1 else [],
        compiler_params=_params(("parallel", "parallel", "arbitrary")),
        **({"input_output_aliases": aliases} if aliases else {}),
    )(*ins)


def _mm_multi(a_list, b_list, *, name, M, T=None, add=None, out_dtype=F32, tm=512, post=None):
    tm = _pick(T or M, tm)
    ns = len(a_list)
    dils = [a[1] if isinstance(a, tuple) else 0 for a in a_list]
    a_arrs = [a[0] if isinstance(a, tuple) else a for a in a_list]
    widths = [a.shape[-1] // max(r, 1) for a, r in zip(a_arrs, dils)]
    b_arrs, b_specs = [], []
    for b in b_list:
        arr, shp, idx = b if isinstance(b, tuple) else (b, b.shape, (0, 0))
        b_arrs.append(arr)
        b_specs.append(pl.BlockSpec(shp, lambda i, idx=idx: idx))
    Nn = b_specs[0].block_shape[1]
    dn = (((1,), (0,)), ((), ()))
    nmm = 2 * ns + (1 if add is not None else 0)
    p_arrs, p_in_specs, p_out_specs, p_out_shape, p_fn = post or ([], [], None, None, None)
    nin = nmm + len(p_arrs)
    nout = len(p_out_specs) if post else 1

    def body(*refs):
        a_refs, b_refs, scr = refs[:ns], refs[ns:2 * ns], list(refs[nin + nout:])
        acc = None
        for a_ref, b_ref, r in zip(a_refs, b_refs, dils):
            av = _from_view(a_ref, scr.pop(0), r) if r > 1 else a_ref[...]
            part = lax.dot_general(av.astype(BF16), b_ref[...], dn, preferred_element_type=F32)
            acc = part if acc is None else acc + part
        if add is not None:
            acc = acc + refs[2 * ns][...]
        if post:
            p_fn(acc, refs[nmm:nin], refs[nin:nin + nout])
        else:
            refs[nin][...] = acc.astype(out_dtype)

    tpe = (T or M) // tm
    a_specs = [pl.BlockSpec((None, tm // r, r * w), lambda i: (i // tpe, i % tpe, 0)) if r
               else pl.BlockSpec((tm, w), lambda i: (i, 0)) for r, w in zip(dils, widths)]
    o_spec = pl.BlockSpec((tm, Nn), lambda i: (i, 0))
    specs = a_specs + b_specs
    ins = a_arrs + b_arrs
    if add is not None:
        specs.append(o_spec)
        ins.append(add)
    scratch = [pltpu.VMEM((w // LANES, tm, LANES), F32) for r, w in zip(dils, widths) if r > 1]
    return _pcall(body, name=name, grid=(M // tm,), in_specs=specs + list(p_in_specs),
                  out_specs=list(p_out_specs) if post else o_spec, scratch_shapes=scratch,
                  out_shape=list(p_out_shape) if post else jax.ShapeDtypeStruct((M, Nn), out_dtype),
                  compiler_params=_params(("arbitrary",) if post else ("parallel",)))(*ins, *p_arrs)


def _dw_view(d3, u, r, *, name, into, segs, tk=1024):
    NB, tsub, rw = d3.shape
    W, T, D = rw // r, tsub * r, u.shape[1]
    tk = _pick(T, tk)
    tpe, nk = T // tk, NB * T // tk

    def body(d_ref, u_ref, into_ref, o_ref, acc, scr, stage, sems):
        k = pl.program_id(0)
        dv = _from_view(d_ref, scr, r).astype(BF16)
        part = lax.dot_general(dv, u_ref[...], _TN, preferred_element_type=F32)

        @pl.when(k == 0)
        def _():
            acc[...] = part

        @pl.when(k > 0)
        def _():
            acc[...] += part

        @pl.when(k == nk - 1)
        def _():
            stage[...] = acc[...].astype(BF16)
            copies = [pltpu.make_async_copy(stage.at[pl.ds(src, n)], o_ref.at[pl.ds(dst, n)], sems.at[q])
                      for q, (src, dst, n) in enumerate(segs)]
            for cp in copies:
                cp.start()
            for cp in copies:
                cp.wait()

    hbm = pl.BlockSpec(memory_space=pl.ANY)
    return _pcall(
        body, name=name, grid=(nk,),
        in_specs=[pl.BlockSpec((None, tk // r, rw), lambda k: (k // tpe, k % tpe, 0)), pl.BlockSpec((tk, D), lambda k: (k, 0)),
                  hbm],
        out_specs=hbm, out_shape=jax.ShapeDtypeStruct(into.shape, BF16), input_output_aliases={2: 0},
        scratch_shapes=[pltpu.VMEM((W, D), F32), pltpu.VMEM((W // LANES, tk, LANES), F32), pltpu.VMEM((W, D), BF16),
                        pltpu.SemaphoreType.DMA((len(segs),))],
        compiler_params=_params(("arbitrary",)))(d3, u, into)


def _lane(shape):
    return lax.broadcasted_iota(jnp.int32, shape, len(shape) - 1)


def _rot_half(v):
    w = v.shape[-1]
    first = (_lane(v.shape) % HEAD_DIM) < (HEAD_DIM // 2)
    return jnp.where(first, pltpu.roll(v, w - HEAD_DIM // 2, v.ndim - 1), pltpu.roll(v, HEAD_DIM // 2, v.ndim - 1))


def _widen(t, w):
    return t if w == t.shape[-1] else jnp.concatenate([t] * (w // t.shape[-1]), axis=-1)


def _unrope(v, cos, sins):
    w = v.shape[-1]
    return v * _widen(cos, w) - _rot_half(v) * _widen(sins, w)


def _rope_tables(positions):
    half = HEAD_DIM // 2
    inv = ROPE_THETA ** (-jnp.arange(half, dtype=F32) / half)
    ang = positions.astype(F32)[..., None] * inv
    cos, sin = jnp.cos(ang), jnp.sin(ang)
    cosf = jnp.concatenate([cos, cos, cos, cos], axis=-1)
    sins = jnp.concatenate([-sin, sin, -sin, sin], axis=-1)
    n = positions.shape[0] * positions.shape[1]
    return cosf.reshape(n, PAIR_W), sins.reshape(n, PAIR_W)


def _inproj(x2, scale, shift, w, cosf, sins, flags, *, T, name):
    N, D = x2.shape
    tm, tn = _pick(T, 512), VAR_W
    tpe = T // tm

    def body(x_ref, sc_ref, sh_ref, w_ref, c_ref, s_ref, f_ref, *outs):
        o_refs, u_ref = outs[:N_VAR], outs[N_VAR]
        j = pl.program_id(1)

        @pl.when(j == 0)
        def _():
            u_ref[...] = (x_ref[...] * (1.0 + sc_ref[0]) + sh_ref[0]).astype(BF16)

        acc = lax.dot_general(u_ref[...], w_ref[...], (((1,), (1,)), ((), ())), preferred_element_type=F32)
        fl = f_ref[...]
        ce = 1.0 + (_widen(c_ref[...], tn) - 1.0) * fl
        se = _widen(s_ref[...], tn) * fl
        res = acc * ce + _rot_half(acc) * se
        for v in range(N_VAR):
            @pl.when(j == v)
            def _(v=v):
                _to_view(res, o_refs[v], outs[N_VAR + 1], VAR_DIL[v])

    ex = pl.BlockSpec((1, 1, D), lambda i, j: (i // tpe, 0, 0))
    tab = pl.BlockSpec((tm, PAIR_W), lambda i, j: (i, 0))
    keep = lambda w_: pl.BlockSpec((tm, w_), lambda i, j: (i, 0))
    vspec = lambda r: pl.BlockSpec((None, tm // r, r * tn), lambda i, j: (i // tpe, i % tpe, 0))
    vshape = lambda r: jax.ShapeDtypeStruct((N // T, T // r, r * tn), BF16)
    return _pcall(
        body, name=name, grid=(N // tm, N_VAR),
        in_specs=[keep(D), ex, ex, pl.BlockSpec((tn, D), lambda i, j: (j, 0)), tab, tab,
                  pl.BlockSpec((1, tn), lambda i, j: (0, j))],
        out_specs=[vspec(r) for r in VAR_DIL] + [keep(D)],
        out_shape=[vshape(r) for r in VAR_DIL] + [jax.ShapeDtypeStruct((N, D), BF16)],
        scratch_shapes=[pltpu.VMEM((tn // LANES, tm, LANES), F32)],
        compiler_params=_params(("parallel", "arbitrary")),
    )(x2, scale, shift, w, cosf, sins, flags)


class _Geom:
    def __init__(self, g):
        if g is None:
            self.r, self.nq, self.n_back, self.sink = 1, A_Q_HEADS, A_WINDOW - 1, True
            self.qw, self.kw = QA_W, KA_W
            self.qidx = lambda j: 0
            self.kidx = lambda j: QA_W // KA_W
            self.vidx = lambda j: QA_W // KA_W + 1
        else:
            window, r = B_PATTERNS[g]
            self.r, self.nq, self.n_back, self.sink = r, B_GROUP_HEADS, window // r, False
            self.qw, self.kw = GB_W, GB_W
            self.qidx = lambda j: 3 * j
            self.kidx = lambda j: 3 * j + 1
            self.vidx = lambda j: 3 * j + 2
        self.ntile = self.qw // PAIR_W


def _stack_heads(t, scale=None):
    first = _lane(t.shape) < HEAD_DIM
    z = jnp.zeros_like(t)
    if scale is not None:
        t = t * jnp.asarray(scale, t.dtype)
    return jnp.concatenate([jnp.where(first, t, z), jnp.where(first, z, t)], axis=0)


def _unstack_heads(v2):
    return jnp.where(_lane((BLOCK, PAIR_W)) < HEAD_DIM, v2[:BLOCK], v2[BLOCK:])


def _dup_head(t, kh):
    tf = t.astype(F32)
    keep = (_lane(t.shape) < HEAD_DIM) if kh == 0 else (_lane(t.shape) >= HEAD_DIM)
    return jnp.where(keep, tf, pltpu.roll(tf, HEAD_DIM, 1)).astype(t.dtype)


def _fold_heads(t):
    return t + pltpu.roll(t, HEAD_DIM, 1)


def _band_mask(rows, i, n_back, single):
    nkeys = BLOCK if single else 2 * BLOCK
    qi = jnp.bitwise_and(lax.broadcasted_iota(jnp.int32, (rows, nkeys), 0), BLOCK - 1)
    ki = lax.broadcasted_iota(jnp.int32, (rows, nkeys), 1)
    if single:
        return qi >= ki
    dist = qi + BLOCK - ki
    return jnp.logical_and(jnp.logical_and(dist >= 0, dist <= n_back), jnp.logical_or(ki >= BLOCK, i > 0))


def _sink_slot(rows):
    qi = jnp.bitwise_and(lax.broadcasted_iota(jnp.int32, (rows, 2 * BLOCK), 0), BLOCK - 1)
    return qi == lax.broadcasted_iota(jnp.int32, (rows, 2 * BLOCK), 1)


def _sink_scores(rows, sinks):
    blk = lax.broadcasted_iota(jnp.int32, (rows, 2 * BLOCK), 0) // BLOCK
    out = jnp.full((rows, 2 * BLOCK), sinks[-1], F32)
    for b in range(len(sinks) - 2, -1, -1):
        out = jnp.where(blk == b, sinks[b], out)
    return out


def _softmax_parts(s, valid, sinks):
    s = jnp.where(valid, s, NEG_INF)
    if sinks is not None:
        slot = _sink_slot(s.shape[0])
        s = jnp.where(slot, _sink_scores(s.shape[0], sinks), s)
    m = jnp.max(s, axis=1, keepdims=True)
    p = jnp.exp(s - m)
    den = jnp.sum(p, axis=1, keepdims=True)
    if sinks is not None:
        p = jnp.where(slot, 0.0, p)
    return p, m, den


_NT = (((1,), (1,)), ((), ()))
_TN = (((0,), (0,)), ((), ()))


def _rows2(prev_ref, cur_ref, cs, single=False):
    if single:
        return cur_ref[0, :, cs]
    return jnp.concatenate([prev_ref[0, :, cs], cur_ref[0, :, cs]], axis=0)


def _sink_scalars(sink_ref, first, nblocks):
    return [sink_ref[first + b] for b in range(nblocks)]


def _tile(t):
    return slice(t * PAIR_W, (t + 1) * PAIR_W)


def _attn_fwd(qkv, sinks, g, *, NB, T, name):
    geo = _Geom(g)
    r, qw, kw, ntile = geo.r, geo.qw, geo.kw, geo.ntile
    tsub = T // r
    nblk = tsub // BLOCK
    qkv3 = qkv.reshape(NB, tsub, r * VAR_W)
    out_dtype = BF16 if g is None else F32
    tiles_per_kv = ntile // A_KV_HEADS

    single = nblk == 1

    def body(q_ref, kp_ref, kc_ref, vp_ref, vc_ref, sink_ref, o_ref, l_ref):
        i = pl.program_id(2)
        if geo.sink:
            kall, vall = _rows2(kp_ref, kc_ref, _tile(0)), _rows2(vp_ref, vc_ref, _tile(0))
            kdup = [_dup_head(kall, kh) for kh in range(A_KV_HEADS)]
            vdup = [_dup_head(vall, kh) for kh in range(A_KV_HEADS)]
            tiles = [[t] for t in range(ntile)]
            q2s = [_stack_heads(q_ref[0, :, _tile(t)], SCALE) for t in range(ntile)]
            kks = [kdup[t // tiles_per_kv] for t in range(ntile)]
            vvs = [vdup[t // tiles_per_kv] for t in range(ntile)]
            sinkcols = [_sink_scalars(sink_ref, 2 * t, 2) for t in range(ntile)]
        else:
            tiles = [[t] for t in range(ntile)]
            q2s = [_stack_heads(q_ref[0, :, _tile(t)], SCALE) for t in range(ntile)]
            kks = [_rows2(kp_ref, kc_ref, _tile(t), single) for t in range(ntile)]
            vvs = [_rows2(vp_ref, vc_ref, _tile(t), single) for t in range(ntile)]
            sinkcols = [None] * ntile
        valid = _band_mask(q2s[0].shape[0], i, geo.n_back, single)
        ss = [lax.dot_general(q2, kk, _NT, preferred_element_type=F32) for q2, kk in zip(q2s, kks)]
        parts = [_softmax_parts(s, valid, sc) for s, sc in zip(ss, sinkcols)]
        o2s = [jnp.dot(p.astype(BF16), vv, preferred_element_type=F32) / den for (p, m, den), vv in zip(parts, vvs)]
        for ts, o2, (p, m, den) in zip(tiles, o2s, parts):
            lse2 = jnp.broadcast_to(m + jnp.log(den), (o2.shape[0], PAIR_W))
            for n, t in enumerate(ts):
                rows = slice(2 * BLOCK * n, 2 * BLOCK * (n + 1))
                o_ref[0, :, _tile(t)] = _unstack_heads(o2[rows]).astype(out_dtype)
                l_ref[0, :, _tile(t)] = _unstack_heads(lse2[rows])

    prev = lambda i: jnp.maximum(i - 1, 0)
    in_specs = [
        pl.BlockSpec((1, BLOCK, qw), lambda b, j, i: (b, i, geo.qidx(j))),
        pl.BlockSpec((1, BLOCK, kw), lambda b, j, i: (b, prev(i), geo.kidx(j))),
        pl.BlockSpec((1, BLOCK, kw), lambda b, j, i: (b, i, geo.kidx(j))),
        pl.BlockSpec((1, BLOCK, kw), lambda b, j, i: (b, prev(i), geo.vidx(j))),
        pl.BlockSpec((1, BLOCK, kw), lambda b, j, i: (b, i, geo.vidx(j))),
        pl.BlockSpec(memory_space=pltpu.SMEM),
    ]
    o_spec = pl.BlockSpec((1, BLOCK, qw), lambda b, j, i: (b, i, j))
    shape = (NB, tsub, r * qw)
    o, lse = _pcall(
        body, name=name, grid=(NB, r, nblk), in_specs=in_specs, out_specs=[o_spec, o_spec],
        out_shape=[jax.ShapeDtypeStruct(shape, out_dtype), jax.ShapeDtypeStruct(shape, F32)],
        compiler_params=_params(("parallel", "parallel", "arbitrary")),
    )(qkv3, qkv3, qkv3, qkv3, qkv3, sinks)
    return o, lse


def _attn_fwd_b(qkvs, *, NB, T, name):
    geos = [_Geom(g) for g in range(len(B_PATTERNS))]
    steps = T // BLOCK
    nt = GB_W // PAIR_W
    ng = len(geos)

    def where(geo, s):
        nblk = T // geo.r // BLOCK
        return s // steps, (s % steps) // nblk, (s % steps) % nblk

    def body(*refs):
        ins, outs = refs[:5 * ng], refs[5 * ng:]
        s = pl.program_id(0)
        q2s, kks, vvs, valids = [], [], [], []
        for n, geo in enumerate(geos):
            q_ref, kp_ref, kc_ref, vp_ref, vc_ref = ins[5 * n:5 * n + 5]
            single = T // geo.r // BLOCK == 1
            valid = _band_mask(2 * BLOCK, where(geo, s)[2], geo.n_back, single)
            for t in range(nt):
                q2s.append(_stack_heads(q_ref[0, :, _tile(t)], SCALE))
                kks.append(_rows2(kp_ref, kc_ref, _tile(t), single))
                vvs.append(_rows2(vp_ref, vc_ref, _tile(t), single))
                valids.append(valid)
        ss = [lax.dot_general(q2, kk, _NT, preferred_element_type=F32) for q2, kk in zip(q2s, kks)]
        parts = [_softmax_parts(sc, valid, None) for sc, valid in zip(ss, valids)]
        o2s = [jnp.dot(p.astype(BF16), vv, preferred_element_type=F32) / den for (p, m, den), vv in zip(parts, vvs)]
        for n in range(ng):
            o_ref, l_ref = outs[2 * n], outs[2 * n + 1]
            for t in range(nt):
                o2, (p, m, den) = o2s[n * nt + t], parts[n * nt + t]
                o_ref[0, :, _tile(t)] = _unstack_heads(o2)
                l_ref[0, :, _tile(t)] = _unstack_heads(jnp.broadcast_to(m + jnp.log(den), (2 * BLOCK, PAIR_W)))

    in_specs, ins, out_specs, out_shape = [], [], [], []
    for geo, qkv in zip(geos, qkvs):
        tsub = T // geo.r
        pos = lambda s, geo=geo: where(geo, s)
        prev = lambda i: jnp.maximum(i - 1, 0)
        blk = lambda col, back, pos=pos: pl.BlockSpec(
            (1, BLOCK, GB_W), lambda s: (pos(s)[0], prev(pos(s)[2]) if back else pos(s)[2], col(pos(s)[1])))
        in_specs += [blk(geo.qidx, False), blk(geo.kidx, True), blk(geo.kidx, False), blk(geo.vidx, True),
                     blk(geo.vidx, False)]
        ins += [qkv.reshape(NB, tsub, geo.r * VAR_W)] * 5
        out_specs += [blk(lambda j: j, False)] * 2
        out_shape += [jax.ShapeDtypeStruct((NB, tsub, geo.r * GB_W), F32)] * 2
    res = _pcall(body, name=name, grid=(NB * steps,), in_specs=in_specs, out_specs=out_specs, out_shape=out_shape,
                 compiler_params=_params(("arbitrary",)))(*ins)
    return [(res[2 * n], res[2 * n + 1]) for n in range(ng)]


def _attn_bwd(qkv, do, lse, dlse, cosf, sins, sinks, g, *, NB, T, name):
    geo = _Geom(g)
    r, qw, kw, ntile = geo.r, geo.qw, geo.kw, geo.ntile
    tsub = T // r
    nblk = tsub // BLOCK
    view = lambda a, w: a.reshape(NB, tsub, r * w)
    has_dlse = dlse is not None
    tiles_per_kv = ntile // A_KV_HEADS

    single = nblk == 1
    krows = BLOCK if single else 2 * BLOCK
    nsteps = 1 if single else nblk + 1

    def grads(q2s, kks, vvs, do2s, i, lserows, sinkcols, dlrows):
        nrow = q2s[0].shape[0]
        ki = lax.broadcasted_iota(jnp.int32, (krows, nrow), 0)
        qi = jnp.bitwise_and(lax.broadcasted_iota(jnp.int32, (krows, nrow), 1), BLOCK - 1)
        if single:
            valid = qi >= ki
        else:
            dist = qi + BLOCK - ki
            valid = jnp.logical_and(jnp.logical_and(dist >= 0, dist <= geo.n_back), jnp.logical_or(ki >= BLOCK, i > 0))
        sts = [lax.dot_general(kk, q2, _NT, preferred_element_type=F32) for q2, kk in zip(q2s, kks)]
        dpts = [lax.dot_general(vv, do2, _NT, preferred_element_type=F32) for do2, vv in zip(do2s, vvs)]
        pts, dsts, sks = [], [], []
        for st, dpt, ls, sc, dl in zip(sts, dpts, lserows, sinkcols, dlrows):
            sv = jnp.where(valid, st, NEG_INF)
            if sc is not None:
                slot = ki == qi
                blk = lax.broadcasted_iota(jnp.int32, (krows, nrow), 1) // BLOCK
                sink = jnp.full((krows, nrow), sc[-1], F32)
                for b in range(len(sc) - 2, -1, -1):
                    sink = jnp.where(blk == b, sc[b], sink)
                sv = jnp.where(slot, sink, sv)
                dpt = jnp.where(slot, 0.0, dpt)
            pt = jnp.exp(sv - ls)
            delta = jnp.sum(pt * dpt, axis=0, keepdims=True)
            if dl is not None:
                delta = delta - dl
            dst = pt * (dpt - delta)
            if sc is not None:
                cols = lambda a, b: a[:, b * BLOCK:(b + 1) * BLOCK]
                sks.append([jnp.sum(jnp.where(cols(slot, b), cols(dst, b), 0.0)) for b in range(len(sc))])
                dst, pt = jnp.where(slot, 0.0, dst), jnp.where(slot, 0.0, pt)
            else:
                sks.append(None)
            pts.append(pt.astype(BF16))
            dsts.append(dst.astype(BF16))
        dq2s = [lax.dot_general(dst, kk, _TN, preferred_element_type=F32) * SCALE for dst, kk in zip(dsts, kks)]
        dkks = [jnp.dot(dst, q2, preferred_element_type=F32) for dst, q2 in zip(dsts, q2s)]
        dvvs = [jnp.dot(pt, do2, preferred_element_type=F32) for pt, do2 in zip(pts, do2s)]
        return dq2s, dkks, dvvs, sks

    def stat_row(t):
        tt = t.T
        return jnp.concatenate([tt[0:1, :], tt[HEAD_DIM:HEAD_DIM + 1, :]], axis=1)

    def body(*refs):
        it = iter(refs)
        q_ref, kp_ref, kc_ref, vp_ref, vc_ref, do_ref, l_ref = (next(it) for _ in range(7))
        dl_ref = next(it) if has_dlse else None
        c_ref, s_ref, sink_ref, o_ref, ds_ref, dq_s, dk_s, dv_s, car_q, car_k, car_v = (next(it) for _ in range(11))
        b, j, i = pl.program_id(0), pl.program_id(1), pl.program_id(2)

        @pl.when(jnp.logical_and(b == 0, jnp.logical_and(j == 0, i == 0)))
        def _():
            ds_ref[...] = jnp.zeros_like(ds_ref)

        def compute():
            if geo.sink:
                kall, vall = _rows2(kp_ref, kc_ref, _tile(0)), _rows2(vp_ref, vc_ref, _tile(0))
                tps = tiles_per_kv // A_BWD_SPLIT
                nb = 2 * tps
                tiles = [[kh * tiles_per_kv + s_ * tps + t for t in range(tps)]
                         for kh in range(A_KV_HEADS) for s_ in range(A_BWD_SPLIT)]
                kdup = [_dup_head(kall, kh) for kh in range(A_KV_HEADS)]
                vdup = [_dup_head(vall, kh) for kh in range(A_KV_HEADS)]
                cat = lambda f, ts: jnp.concatenate([f(t) for t in ts], axis=0)
                dq2s, dkks, dvvs, sks = grads(
                    [cat(lambda t: _stack_heads(q_ref[0, :, _tile(t)], SCALE), ts) for ts in tiles],
                    [kdup[n // A_BWD_SPLIT] for n in range(len(tiles))],
                    [vdup[n // A_BWD_SPLIT] for n in range(len(tiles))],
                    [cat(lambda t: _stack_heads(do_ref[0, :, _tile(t)]), ts) for ts in tiles], i,
                    [jnp.concatenate([stat_row(l_ref[0, :, _tile(t)]) for t in ts], axis=1) for ts in tiles],
                    [_sink_scalars(sink_ref, 2 * ts[0], nb) for ts in tiles], [None] * len(tiles))
                lane1 = _lane((1, PAIR_W))
                dsink = jnp.zeros((1, PAIR_W), F32)
                for ts, dq2, sk in zip(tiles, dq2s, sks):
                    for n, t in enumerate(ts):
                        dq_s[:, _tile(t)] = _unstack_heads(dq2[2 * BLOCK * n:2 * BLOCK * (n + 1)])
                    for bb in range(nb):
                        dsink = dsink + jnp.where(lane1 == 2 * ts[0] + bb, sk[bb], 0.0)
                per_kv = lambda parts, kh: functools.reduce(jnp.add, parts[kh * A_BWD_SPLIT:(kh + 1) * A_BWD_SPLIT])
                second = _lane((krows, PAIR_W)) >= HEAD_DIM
                dk_s[...] = jnp.where(second, _fold_heads(per_kv(dkks, 1)), _fold_heads(per_kv(dkks, 0)))
                dv_s[...] = jnp.where(second, _fold_heads(per_kv(dvvs, 1)), _fold_heads(per_kv(dvvs, 0)))
                ds_ref[0:1, :] += dsink
            else:
                dq2s, dkks, dvvs, _ = grads(
                    [_stack_heads(q_ref[0, :, _tile(t)], SCALE) for t in range(ntile)],
                    [_rows2(kp_ref, kc_ref, _tile(t), single) for t in range(ntile)],
                    [_rows2(vp_ref, vc_ref, _tile(t), single) for t in range(ntile)],
                    [_stack_heads(do_ref[0, :, _tile(t)]) for t in range(ntile)], i,
                    [stat_row(l_ref[0, :, _tile(t)]) for t in range(ntile)], [None] * ntile,
                    [stat_row(dl_ref[0, :, _tile(t)]) for t in range(ntile)])
                for t in range(ntile):
                    dq_s[:, _tile(t)] = _unstack_heads(dq2s[t])
                    dk_s[0:krows, _tile(t)] = dkks[t]
                    dv_s[0:krows, _tile(t)] = dvvs[t]

        def emit(dq, dk, dv):
            cos, sn = c_ref[0], s_ref[0]
            o_ref[0, :, 0:qw] = _unrope(dq, cos, sn).astype(BF16)
            o_ref[0, :, qw:qw + kw] = _unrope(dk, cos, sn).astype(BF16)
            o_ref[0, :, qw + kw:qw + 2 * kw] = dv.astype(BF16)
            if qw + 2 * kw < VAR_W:
                o_ref[0, :, qw + 2 * kw:VAR_W] = jnp.zeros((BLOCK, VAR_W - qw - 2 * kw), BF16)

        if single:
            compute()
            emit(dq_s[...], dk_s[0:BLOCK, :], dv_s[0:BLOCK, :])
            return

        @pl.when(i == 0)
        def _():
            car_q[...] = jnp.zeros_like(car_q)
            car_k[...] = jnp.zeros_like(car_k)
            car_v[...] = jnp.zeros_like(car_v)

        @pl.when(i == nblk)
        def _():
            dk_s[...] = jnp.zeros_like(dk_s)
            dv_s[...] = jnp.zeros_like(dv_s)

        pl.when(i < nblk)(compute)
        emit(car_q[...], car_k[...] + dk_s[0:BLOCK, :], car_v[...] + dv_s[0:BLOCK, :])
        car_q[...] = dq_s[...]
        car_k[...] = dk_s[BLOCK:2 * BLOCK, :]
        car_v[...] = dv_s[BLOCK:2 * BLOCK, :]

    cur = lambda i: jnp.minimum(i, nblk - 1)
    prv = lambda i: jnp.maximum(jnp.minimum(i, nblk - 1) - 1, 0)
    outb = lambda i: jnp.maximum(i - 1, 0)
    qrow = pl.BlockSpec((1, BLOCK, qw), lambda b, j, i: (b, cur(i), j))
    in_specs = [
        pl.BlockSpec((1, BLOCK, qw), lambda b, j, i: (b, cur(i), geo.qidx(j))),
        pl.BlockSpec((1, BLOCK, kw), lambda b, j, i: (b, prv(i), geo.kidx(j))),
        pl.BlockSpec((1, BLOCK, kw), lambda b, j, i: (b, cur(i), geo.kidx(j))),
        pl.BlockSpec((1, BLOCK, kw), lambda b, j, i: (b, prv(i), geo.vidx(j))),
        pl.BlockSpec((1, BLOCK, kw), lambda b, j, i: (b, cur(i), geo.vidx(j))),
        qrow, qrow,
    ]
    ins = [view(qkv, VAR_W)] * 5 + [view(do, qw), view(lse, qw)]
    if has_dlse:
        in_specs.append(qrow)
        ins.append(view(dlse, qw))
    in_specs += [
        pl.BlockSpec((1, BLOCK, PAIR_W), lambda b, j, i: (b, outb(i), j)),
        pl.BlockSpec((1, BLOCK, PAIR_W), lambda b, j, i: (b, outb(i), j)),
        pl.BlockSpec(memory_space=pltpu.SMEM),
    ]
    ins += [view(cosf, PAIR_W), view(sins, PAIR_W), sinks]
    scratch = [pltpu.VMEM((BLOCK, qw), F32), pltpu.VMEM((2 * BLOCK, kw), F32), pltpu.VMEM((2 * BLOCK, kw), F32),
               pltpu.VMEM((BLOCK, qw), F32), pltpu.VMEM((BLOCK, kw), F32), pltpu.VMEM((BLOCK, kw), F32)]
    dqkv, dsink = _pcall(
        body, name=name, grid=(NB, r, nsteps), in_specs=in_specs,
        out_specs=[pl.BlockSpec((1, BLOCK, VAR_W), lambda b, j, i: (b, outb(i), j)),
                   pl.BlockSpec((8, PAIR_W), lambda b, j, i: (0, 0))],
        out_shape=[jax.ShapeDtypeStruct((NB, tsub, r * VAR_W), BF16), jax.ShapeDtypeStruct((8, PAIR_W), F32)],
        scratch_shapes=scratch, compiler_params=_params(("arbitrary", "arbitrary", "arbitrary")),
    )(*ins)
    return dqkv, dsink


class _Rows:
    def __init__(self, N, T, tm):
        self.N, self.tm, self.tpe, self.grid = N, tm, T // tm, (N // tm,)

    def row(self, w, col=0):
        return pl.BlockSpec((self.tm, w), lambda i: (i, col))

    def ex(self, w):
        return pl.BlockSpec((1, 1, w), lambda i: (i // self.tpe, 0, 0))

    def const(self, shape):
        return pl.BlockSpec(shape, lambda i: tuple(0 for _ in shape))

    def view(self, w, r):
        return pl.BlockSpec((None, self.tm // r, r * w), lambda i: (i // self.tpe, i % self.tpe, 0))

    def first_of_example(self):
        return pl.program_id(0) % self.tpe == 0


def _acc(ref, first, val):
    @pl.when(first)
    def _():
        ref[0] = val

    @pl.when(jnp.logical_not(first))
    def _():
        ref[0] += val


def _colsum(v):
    return jnp.sum(v, axis=0, keepdims=True)


def _ln_stats(r):
    mu = jnp.mean(r, axis=-1, keepdims=True)
    xc = r - mu
    var = jnp.mean(xc * xc, axis=-1, keepdims=True)
    rstd = lax.rsqrt(var + LN_EPS)
    return xc * rstd, rstd


def _ln_bwd(dy, xhat, rstd, gain):
    dxh = dy * gain
    return rstd * (dxh - jnp.mean(dxh, axis=-1, keepdims=True) - xhat * jnp.mean(dxh * xhat, axis=-1, keepdims=True))


def _from_view(ref, scr, r):
    if r == 1:
        return ref[...]
    rows, w = ref.shape[0], ref.shape[1] // r
    for j in range(r):
        for c in range(w // LANES):
            scr.at[c][pl.ds(j, rows, stride=r), :] = ref[:, j * w + c * LANES:j * w + (c + 1) * LANES].astype(F32)
    return jnp.concatenate([scr[c] for c in range(w // LANES)], axis=1)


def _to_view(val, ref, scr, r):
    if r == 1:
        ref[...] = val.astype(ref.dtype)
        return
    rows, w = ref.shape[0], ref.shape[1] // r
    for c in range(w // LANES):
        scr[c] = val[:, c * LANES:(c + 1) * LANES]
    for j in range(r):
        for c in range(w // LANES):
            ref[:, j * w + c * LANES:j * w + (c + 1) * LANES] = scr.at[c][pl.ds(j, rows, stride=r), :].astype(ref.dtype)


def _silu_parts(v):
    s = jax.nn.sigmoid(v)
    return v * s, s * (1.0 + v * (1.0 - s))


def _local_step(x, mod, positions, w_in, rest_weights, sinks, ln1_g, ln1_b, ln2_g, ln2_b, target, hook=None):
    hook = hook or (lambda event, **data: None)
    NB, T, D = x.shape
    N = NB * T
    x2 = x.reshape(N, D)
    tgt2 = target.reshape(N, D)
    shift_m, scale_m, gate_m, shift_f, scale_f, gate_f = [mod[:, None, k * D:(k + 1) * D] for k in range(6)]
    cosf, sins = _rope_tables(positions)
    col = jnp.arange(QKV_P)
    vcol = col % VAR_W
    flags = jnp.where(col < VAR_W, vcol < QA_W + KA_W, vcol < 2 * GB_W).astype(F32)[None]
    R = _Rows(N, T, _pick(T, 256))
    sds = jax.ShapeDtypeStruct
    exsum = lambda w=D: sds((NB, 1, w), F32)
    ngrp = len(B_PATTERNS)

    *qkv, u = _inproj(x2, scale_m, shift_m, w_in, cosf, sins, flags, T=T, name="inproj_qkv")
    gates = _mm(u, w_in, tb=True, b_rows=(QKV_P, w_in.shape[0] - QKV_P), out_dtype=BF16, name="inproj_gates")
    oa, la = _attn_fwd(qkv[0], sinks, None, NB=NB, T=T, name="attn_a_fwd")
    oa = oa.reshape(N, QA_W)
    (o1, l1), (o2, l2), (o3, l3) = _attn_fwd_b(qkv[1:], NB=NB, T=T, name="attn_b_fwd")
    w_a, w_b, w_o, w_gu, w_d = rest_weights()
    F = w_d.shape[0]
    dil = [r_ for _, r_ in B_PATTERNS]
    views = [R.view(GB_W, r_) for r_ in dil]
    tokbuf = pltpu.VMEM((GB_W // LANES, R.tm, LANES), F32)

    f32 = lambda ref: ref[...].astype(F32)
    Rm = _Rows(N, T, _pick(T, 512))

    def mix_out(o1r, o2r, o3r, l1r, l2r, l3r, oa_r, ga_r, gb_r, x_r, gm_r, g_r, b_r, sf_r, hf_r, wa_r, wb_r, wo_r,
                ob_ref, ya_ref, yb_ref, mg_ref, y_ref, r1_ref, u2_ref, *bufs):
        os_ = [_from_view(ref, bufs[n], dil[n]) for n, ref in enumerate((o1r, o2r, o3r))]
        la, lb, lc = [_from_view(ref, bufs[3 + n], dil[n]) for n, ref in enumerate((l1r, l2r, l3r))]
        mx = jnp.maximum(jnp.maximum(la, lb), lc)
        ea, eb, ec = jnp.exp(la - mx), jnp.exp(lb - mx), jnp.exp(lc - mx)
        ob = ((ea * os_[0] + eb * os_[1] + ec * os_[2]) / (ea + eb + ec)).astype(BF16)
        ob_ref[...] = ob
        ya = jnp.dot(oa_r[...], wa_r[...], preferred_element_type=F32).astype(BF16)
        yb = jnp.concatenate([jnp.dot(ob, wb_r[s_], preferred_element_type=F32)
                              for s_ in range(w_b.shape[0])], axis=1).astype(BF16)
        merged = (jax.nn.sigmoid(f32(ga_r)) * ya.astype(F32) + jax.nn.sigmoid(f32(gb_r)) * yb.astype(F32)).astype(BF16)
        y = jnp.dot(merged, wo_r[...], preferred_element_type=F32)
        r1 = ALPHA * x_r[...] + (1.0 + gm_r[0]) * y
        xhat, _ = _ln_stats(r1)
        x1 = xhat * g_r[...] + b_r[...]
        ya_ref[...], yb_ref[...], mg_ref[...], y_ref[...], r1_ref[...] = ya, yb, merged, y, r1
        u2_ref[...] = (x1 * (1.0 + sf_r[0]) + hf_r[0]).astype(BF16)

    mviews = [Rm.view(GB_W, r_) for r_ in dil]
    ob, ya, yb, merged, y, r1, u2 = _pcall(
        mix_out, name="mix_out", grid=Rm.grid,
        in_specs=mviews + mviews + [Rm.row(QA_W), Rm.row(D, 0), Rm.row(D, 1), Rm.row(D), Rm.ex(D), Rm.const((1, D)),
                                    Rm.const((1, D)), Rm.ex(D), Rm.ex(D), Rm.const(w_a.shape), Rm.const(w_b.shape),
                                    Rm.const(w_o.shape)],
        out_specs=[Rm.row(GB_W)] + [Rm.row(D)] * 6,
        out_shape=[sds((N, GB_W), BF16)] + [sds((N, D), BF16)] * 3 + [sds((N, D), F32)] * 2 + [sds((N, D), BF16)],
        scratch_shapes=[pltpu.VMEM((GB_W // LANES, Rm.tm, LANES), F32)] * 6,
        compiler_params=_params(("parallel",)))(o1, o2, o3, l1, l2, l3, oa, gates, gates, x2, gate_m, ln1_g, ln1_b,
                                                scale_f, shift_f, w_a, w_b, w_o)

    w_gu = w_gu() if callable(w_gu) else w_gu
    tnf = w_gu.shape[2]
    nft = w_gu.shape[0] // 2
    tmf = _pick(N, 512)

    def ffn_up(u_r, wg_r, wu_r, hg_ref, hu_ref, a_ref):
        hg = jnp.dot(u_r[...], wg_r[...], preferred_element_type=F32)
        hu = jnp.dot(u_r[...], wu_r[...], preferred_element_type=F32)
        sl, _ = _silu_parts(hg)
        hg_ref[...] = hg.astype(BF16)
        hu_ref[...] = hu.astype(BF16)
        a_ref[...] = (sl * hu).astype(BF16)

    ftile = pl.BlockSpec((tmf, tnf), lambda j, i: (i, j))
    hg, hu, act = _pcall(
        ffn_up, name="ffn_up", grid=(nft, N // tmf),
        in_specs=[pl.BlockSpec((tmf, D), lambda j, i: (i, 0)), pl.BlockSpec((None, D, tnf), lambda j, i: (j, 0, 0)),
                  pl.BlockSpec((None, D, tnf), lambda j, i: (j + nft, 0, 0))],
        out_specs=[ftile] * 3, out_shape=[sds((N, F), BF16)] * 3,
        compiler_params=_params(("arbitrary", "parallel")))(u2, w_gu, w_gu)
    fchunk = _pick(F, 768)

    def ffn_down_norm2(act_r, wd_r, r1_r, g1_r, b1_r, t_r, gf_r, g_r, b_r, hg_r, hu_r,
                       dy2_ref, dx1_ref, dgf_ref, dg_ref, db_ref, loss_ref, dh_ref):
        first = R.first_of_example()
        y2v = jnp.dot(act_r[...], wd_r[...], preferred_element_type=F32)
        x1 = _ln_stats(r1_r[...])[0] * g1_r[...] + b1_r[...]
        r2 = ALPHA * x1 + (1.0 + gf_r[0]) * y2v
        xhat, rstd = _ln_stats(r2)
        err = xhat * g_r[...] + b_r[...] - t_r[...]
        dx2 = err * (1.0 / D)
        dr2 = _ln_bwd(dx2, xhat, rstd, g_r[...])
        dy2 = ((1.0 + gf_r[0]) * dr2).astype(BF16)
        dy2_ref[...] = dy2
        dx1_ref[...] = ALPHA * dr2
        _acc(dgf_ref, first, _colsum(dr2 * y2v))
        _acc(dg_ref, first, _colsum(dx2 * xhat))
        _acc(db_ref, first, _colsum(dx2))
        part = 0.5 * jnp.sum(jnp.mean(err * err, axis=-1, keepdims=True))
        _acc(loss_ref, first, jnp.broadcast_to(part, (1, 128)))
        for t in range(F // fchunk):
            cs = slice(t * fchunk, (t + 1) * fchunk)
            da = lax.dot_general(dy2, wd_r[cs, :], _NT, preferred_element_type=F32)
            sl, dsl = _silu_parts(hg_r[:, cs].astype(F32))
            dh_ref[:, cs] = (da * hu_r[:, cs].astype(F32) * dsl).astype(BF16)
            dh_ref[:, F + t * fchunk:F + (t + 1) * fchunk] = (da * sl).astype(BF16)

    dy2, dx1p, dgate_f, dg2, db2, loss_p, dh = _pcall(
        ffn_down_norm2, name="ffn_down_norm2", grid=R.grid,
        in_specs=[R.row(F), R.const((F, D)), R.row(D), R.const((1, D)), R.const((1, D)), R.row(D), R.ex(D),
                  R.const((1, D)), R.const((1, D)), R.row(F), R.row(F)],
        out_specs=[R.row(D), R.row(D), R.ex(D), R.ex(D), R.ex(D), R.ex(128), R.row(2 * F)],
        out_shape=[sds((N, D), BF16), sds((N, D), F32), exsum(), exsum(), exsum(), exsum(128), sds((N, 2 * F), BF16)],
        compiler_params=_params(("arbitrary",)))(act, w_d, r1, ln1_g, ln1_b, tgt2, gate_f, ln2_g, ln2_b, hg, hu)

    g_wd = _mm(act, dy2, ta=True, out_dtype=BF16, name="ffn_down_dw")
    g_wgu = _mm(u2, dh, ta=True, out3=w_gu.shape[0], out_dtype=BF16, name="ffn_up_dw")
    hook("ffn_grads", g_wgu=g_wgu)

    def ffn_up_dx_norm1(dh_r, w_r, dx1p_r, r1_r, y_r, sf_r, gm_r, g_r, b_r,
                        dxp_ref, dy_ref, dsf_ref, dhf_ref, dgm_ref, dg_ref, db_ref):
        first = R.first_of_example()
        du2v = None
        for s_ in range(w_gu.shape[0]):
            part = lax.dot_general(dh_r[:, s_ * tnf:(s_ + 1) * tnf], w_r[s_], _NT, preferred_element_type=F32)
            du2v = part if du2v is None else du2v + part
        dx1 = dx1p_r[...] + du2v * (1.0 + sf_r[0])
        xhat, rstd = _ln_stats(r1_r[...])
        dr1 = _ln_bwd(dx1, xhat, rstd, g_r[...])
        dxp_ref[...] = ALPHA * dr1
        dy_ref[...] = ((1.0 + gm_r[0]) * dr1).astype(BF16)
        _acc(dsf_ref, first, _colsum(du2v * (xhat * g_r[...] + b_r[...])))
        _acc(dhf_ref, first, _colsum(du2v))
        _acc(dgm_ref, first, _colsum(dr1 * y_r[...]))
        _acc(dg_ref, first, _colsum(dx1 * xhat))
        _acc(db_ref, first, _colsum(dx1))

    dxp, dy, dscale_f, dshift_f, dgate_m, dg1, db1 = _pcall(
        ffn_up_dx_norm1, name="ffn_up_dx_norm1", grid=R.grid,
        in_specs=[R.row(2 * F), R.const(w_gu.shape)] + [R.row(D)] * 3 + [R.ex(D), R.ex(D), R.const((1, D)),
                                                                        R.const((1, D))],
        out_specs=[R.row(D), R.row(D)] + [R.ex(D)] * 5,
        out_shape=[sds((N, D), F32), sds((N, D), BF16)] + [exsum()] * 5,
        compiler_params=_params(("arbitrary",)))(dh, w_gu, dx1p, r1, y, scale_f, gate_m, ln1_g, ln1_b)

    g_wo = _mm(merged, dy, ta=True, out_dtype=BF16, name="out_proj_dw")

    seg = (jnp.arange(GB_W)[:, None] // HEAD_DIM == jnp.arange(GB_W)[None, :] // HEAD_DIM).astype(BF16)

    def mix_out_bwd(dy_r, ya_r, yb_r, ga_r, gb_r, wo_r, wa_r, wb_r, o1r, o2r, o3r, l1r, l2r, l3r, seg_r,
                    dya_ref, dyb_ref, dg_ref, doa_ref, d1, d2, d3, e1, e2, e3, *bufs):
        dm = lax.dot_general(dy_r[...], wo_r[...], _NT, preferred_element_type=F32).astype(BF16).astype(F32)
        sa, sb = jax.nn.sigmoid(f32(ga_r)), jax.nn.sigmoid(f32(gb_r))
        dya, dyb = (dm * sa).astype(BF16), (dm * sb).astype(BF16)
        dya_ref[...], dyb_ref[...] = dya, dyb
        dg_ref[:, :D] = (dm * f32(ya_r) * sa * (1.0 - sa)).astype(BF16)
        dg_ref[:, D:] = (dm * f32(yb_r) * sb * (1.0 - sb)).astype(BF16)
        doa_ref[...] = lax.dot_general(dya, wa_r[...], _NT, preferred_element_type=F32).astype(BF16)
        ds_ = D // w_b.shape[0]
        dob = None
        for s_ in range(w_b.shape[0]):
            part = lax.dot_general(dyb[:, s_ * ds_:(s_ + 1) * ds_], wb_r[s_], _NT, preferred_element_type=F32)
            dob = part if dob is None else dob + part
        dob_v = dob
        os_ = [_from_view(ref, bufs[n], dil[n]) for n, ref in enumerate((o1r, o2r, o3r))]
        la, lb, lc = [_from_view(ref, bufs[3 + n], dil[n]) for n, ref in enumerate((l1r, l2r, l3r))]
        mx = jnp.maximum(jnp.maximum(la, lb), lc)
        ea, eb, ec = jnp.exp(la - mx), jnp.exp(lb - mx), jnp.exp(lc - mx)
        inv = 1.0 / (ea + eb + ec)
        ws = [ea * inv, eb * inv, ec * inv]

        def headsum(v):
            hi = v.astype(BF16)
            lo = (v - hi.astype(F32)).astype(BF16)
            sm = seg_r[...]
            return jnp.dot(hi, sm, preferred_element_type=F32) + jnp.dot(lo, sm, preferred_element_type=F32)

        dws = [headsum(dob_v * o) for o in os_]
        mean = ws[0] * dws[0] + ws[1] * dws[1] + ws[2] * dws[2]
        for n, (w_, dw_, d_ref, e_ref) in enumerate(zip(ws, dws, (d1, d2, d3), (e1, e2, e3))):
            _to_view(w_ * dob_v, d_ref, bufs[6], dil[n])
            _to_view(w_ * (dw_ - mean), e_ref, bufs[7], dil[n])

    vshape = lambda r_, dt: sds((NB, T // r_, r_ * GB_W), dt)
    dya, dyb, dgates, doa, *mb = _pcall(
        mix_out_bwd, name="mix_out_bwd", grid=R.grid,
        in_specs=[R.row(D)] * 3 + [R.row(D, 0), R.row(D, 1), R.const(w_o.shape), R.const(w_a.shape), R.const(w_b.shape)]
        + views + views + [R.const((GB_W, GB_W))],
        out_specs=[R.row(D), R.row(D), R.row(2 * D), R.row(QA_W)] + views + views,
        out_shape=[sds((N, D), BF16), sds((N, D), BF16), sds((N, 2 * D), BF16), sds((N, QA_W), BF16)]
        + [vshape(r_, BF16) for r_ in dil] + [vshape(r_, F32) for r_ in dil],
        scratch_shapes=[tokbuf] * 8,
        compiler_params=_params(("parallel",)))(dy, ya, yb, gates, gates, w_o, w_a, w_b, o1, o2, o3, l1, l2, l3, seg)
    do_b, dlse_b = mb[:3], mb[3:]

    g_wa = _mm(oa, dya, ta=True, out_dtype=BF16, name="branch_a_dw")
    g_wb = _mm(ob, dyb, ta=True, out3=w_b.shape[0], out_dtype=BF16, name="branch_b_dw")
    hook("rest_grads", g_wa=g_wa, g_wb=g_wb, g_wo=g_wo, g_wd=g_wd)

    dqkv_a, dsink = _attn_bwd(qkv[0], doa, la, None, cosf, sins, sinks, None, NB=NB, T=T, name="attn_a_bwd")
    hook("attn_a_bwd_done")
    dqkv = [dqkv_a]
    for g in range(ngrp):
        dqkv.append(_attn_bwd(qkv[1 + g], do_b[g], (l1, l2, l3)[g], dlse_b[g], cosf, sins, sinks, g, NB=NB, T=T,
                              name=f"attn_b{g}_bwd")[0])
        hook(f"attn_b{g}_bwd_done")

    blk = 256
    gate0 = A_W + 3 * QB_W
    win_rows = gate0 + dgates.shape[1]
    dw = functools.partial(_mm, ta=True, out_dtype=BF16, tm=blk)
    g_win = dw(dqkv[0].reshape(N, VAR_W), u, m_rows=A_W, out_rows=(win_rows, lambda i: i), name="inproj_dw0")
    for v in range(1, N_VAR):
        g = v - 1
        if VAR_DIL[v] == 1:
            block_of = lambda i, g=g: (A_W + g * GB_W) // blk + (i // (GB_W // blk)) * (QB_W // blk) + i % (GB_W // blk)
            g_win = dw(dqkv[v].reshape(N, VAR_W), u, out_rows=(win_rows, block_of), out_init=g_win, name=f"inproj_dw{v}")
        else:
            segs = [(s * GB_W, A_W + s * QB_W + g * GB_W, GB_W) for s in range(3)]
            g_win = _dw_view(dqkv[v], u, VAR_DIL[v], into=g_win, segs=segs, name=f"inproj_dw{v}")
    g_win = dw(dgates, u, out_rows=(win_rows, lambda i: gate0 // blk + i), out_init=g_win, name=f"inproj_dw{N_VAR}")
    hook("win_grads", g_win=g_win)
    wvar = lambda v: (w_in, (VAR_W, D), (v, 0))
    dview = lambda v: (dqkv[v], VAR_DIL[v])
    du = _mm_multi([dview(0)], [wvar(0)], M=N, T=T, name="inproj_dx0")
    hook("inproj_dx0_done")
    def x_bwd(duv, ins, outs):
        (dxp_r, x_r, sm_r), (gx_ref, dsm_ref, dhm_ref) = ins, outs
        first = R.first_of_example()
        gx_ref[...] = dxp_r[...] + duv * (1.0 + sm_r[0])
        _acc(dsm_ref, first, _colsum(duv * x_r[...]))
        _acc(dhm_ref, first, _colsum(duv))

    gx, dscale_m, dshift_m = _mm_multi(
        [dview(v) for v in range(1, N_VAR)] + [dgates],
        [wvar(v) for v in range(1, N_VAR)] + [(w_in, (2 * D, D), (QKV_P // (2 * D), 0))],
        M=N, T=T, add=du, tm=R.tm, name="inproj_dx1",
        post=([dxp, x2, scale_m], [R.row(D), R.row(D), R.ex(D)], [R.row(D), R.ex(D), R.ex(D)],
              [sds((N, D), F32), exsum(), exsum()], x_bwd))
    hook("inproj_dx1_done")

    dmod =jnp.concatenate([dshift_m, dscale_m, dgate_m, dshift_f, dscale_f, dgate_f], axis=-1)[:, 0]
    ln_grads = jnp.concatenate([dg1, db1, dg2, db2], axis=1)
    return dict(loss=loss_p[:, 0, 0], grad_x=gx.reshape(NB, T, D), g_win=g_win, g_wa=g_wa, g_wb=g_wb, g_wo=g_wo,
                g_wgu=g_wgu, g_wd=g_wd, dmod=dmod, ln_grads=ln_grads, dsink=dsink[0, :A_Q_HEADS])


def _coords():
    return lax.axis_index("x"), lax.axis_index("y"), lax.axis_index("c")


def _allgather_small(blk, *, name):
    m_per, n = blk.shape

    def body(x_ref, out_ref, send_sems, recv_sems, local_sem):
        x, y, c = _coords()
        me, sibling = (x, y, c), (x, y, 1 - c)
        chips = [(1 - x, y), (x, 1 - y), (1 - x, 1 - y)]

        def rows(px, py, pc):
            return out_ref.at[pl.ds((4 * px + 2 * py + pc) * m_per, m_per), :]

        def copy(k, block, to, src=None):
            return pltpu.make_async_remote_copy(
                src_ref=rows(*block) if src is None else src, dst_ref=rows(*block),
                send_sem=send_sems.at[k], recv_sem=recv_sems.at[k], device_id=to, device_id_type=MESH)

        mine = pltpu.make_async_copy(x_ref, rows(*me), local_sem)
        mine.start()
        first = [copy(0, me, sibling, src=x_ref)]
        first += [copy(1 + j, me, (*chip, c), src=x_ref) for j, chip in enumerate(chips)]
        for cp in first:
            cp.start()
        passed = [copy(4 + j, (*chip, c), sibling) for j, chip in enumerate(chips)]
        for j, chip in enumerate(chips):
            copy(1 + j, (*chip, c), me).wait_recv()
            passed[j].start()
        copy(0, sibling, me).wait_recv()
        for j, chip in enumerate(chips):
            copy(4 + j, (*chip, 1 - c), me).wait_recv()
        for cp in first + passed:
            cp.wait_send()
        mine.wait()

    return _pcall(
        body, name=name, out_shape=jax.ShapeDtypeStruct((8 * m_per, n), blk.dtype),
        in_specs=[pl.BlockSpec(memory_space=pltpu.VMEM)], out_specs=pl.BlockSpec(memory_space=pltpu.VMEM),
        scratch_shapes=[pltpu.SemaphoreType.DMA((7,)), pltpu.SemaphoreType.DMA((7,)), pltpu.SemaphoreType.DMA],
        compiler_params=pltpu.CompilerParams(vmem_limit_bytes=VMEM_LIMIT_BYTES),
    )(blk)


def _exchange(srcs, dsts, plan, *, name, dst_inits=None):
    na = len(dsts)
    nrem = len(plan(0, 0, 0))

    def body(*refs):
        refs = list(refs)
        src_refs = [refs.pop(0) for _ in range(na)] if srcs is not None else None
        if dst_inits is not None:
            del refs[:na]
        dst_refs, (send_sems, recv_sems) = refs[:na], refs[na:]
        start, wait = _copies(dst_refs if src_refs is None else src_refs, dst_refs, send_sems, recv_sems, plan)
        start()
        wait()

    hbm = pl.BlockSpec(memory_space=pl.ANY)
    ins = (list(srcs) if srcs is not None else []) + (list(dst_inits) if dst_inits is not None else [])
    base = na if srcs is not None else 0
    aliases = {base + a: a for a in range(na)} if dst_inits is not None else {}
    return _pcall(
        body, name=name, out_shape=list(dsts), in_specs=[hbm] * len(ins), out_specs=[hbm] * na,
        input_output_aliases=aliases,
        scratch_shapes=[pltpu.SemaphoreType.DMA((na * nrem,)), pltpu.SemaphoreType.DMA((na * nrem,))],
    )(*ins)


def _other_chips(x, y):
    return [(1 - x, y), (x, 1 - y), (1 - x, 1 - y)]


def _round(ride, carrier, name):
    if carrier is not None:
        _RIDES.setdefault(carrier, []).append(ride)
        return
    srcs = ride.srcs() if callable(ride.srcs) else ride.srcs
    inits = ride.dst_inits() if callable(ride.dst_inits) else ride.dst_inits
    ride.out = list(_exchange(srcs, ride.dsts, ride.plan, name=name, dst_inits=inits))


class _Gather:
    def __init__(self, shards, chip, tag, carriers=(None, None)):
        def plan_ici(x, y, c):
            k = 2 * x + y
            return [((c,), (k, c), (2 * px + py, c), (px, py, c)) for px, py in _other_chips(x, y)]

        def plan_d2d(x, y, c):
            return [((2 * px + py, c), (2 * px + py, c), (2 * px + py, 1 - c), (x, y, 1 - c))
                    for px, py in _other_chips(x, y)]

        def plan_near(x, y, c):
            k = 2 * x + y
            return [((c,), (k, c), (2 * px + py, c), (px, py, c)) for px, py in ((1 - x, y), (x, 1 - y))]

        def plan_far(x, y, c):
            kx, ky, kd = 2 * (1 - x) + y, 2 * x + (1 - y), 2 * (1 - x) + (1 - y)
            hp = shards[0].shape[1] // 2
            top, bottom = pl.ds(0, hp), pl.ds(hp, hp)
            return [((kx, c, top), (kx, c, top), (kd, c, top), (x, 1 - y, c)),
                    ((ky, c, bottom), (ky, c, bottom), (kd, c, bottom), (1 - x, y, c))]

        self.shards, self.chip = shards, chip
        dsts = [jax.ShapeDtypeStruct((4,) + s.shape, s.dtype) for s in shards]
        if len(carriers) == 3:
            near = _Ride(shards, dsts, plan_near)
            ici = _Ride(None, dsts, plan_far, dst_inits=lambda: near.out)
            _round(near, carriers[0], f"gather_{tag}_near")
            _round(ici, carriers[1], f"gather_{tag}_far")
        else:
            ici = _Ride(shards, dsts, plan_ici)
            _round(ici, carriers[0], f"gather_{tag}_ici")
        self.d2d = _Ride(None, dsts, plan_d2d, dst_inits=lambda: ici.out)
        _round(self.d2d, carriers[-1], f"gather_{tag}_d2d")

    def result(self):
        full = [lax.dynamic_update_index_in_dim(f, s, self.chip, 0) for f, s in zip(self.d2d.out, self.shards)]
        return [f.reshape((4, 2 * f.shape[2], f.shape[3])) for f in full]


def _index_operand(i):
    return jnp.reshape(i, (1,)).astype(jnp.int32)


def _add_pairs(g, f, ci, *, name):
    s, _, hr, wd = g.shape
    tr = _pick(hr, 600, 16)

    def body(c_ref, a_ref, b_ref, o_ref):
        o_ref[...] = (a_ref[...].astype(F32) + b_ref[...].astype(F32)).astype(BF16)

    spec = pl.BlockSpec((1, tr, wd), lambda j, i, c: (j, i, 0))
    grid_spec = pltpu.PrefetchScalarGridSpec(
        num_scalar_prefetch=1, grid=(s, hr // tr),
        in_specs=[pl.BlockSpec((1, None, tr, wd), lambda j, i, c: (j, c[0], i, 0)), spec], out_specs=spec)
    return _pcall(body, name=name, grid_spec=grid_spec, out_shape=jax.ShapeDtypeStruct(f.shape, BF16),
                  compiler_params=_params(("parallel", "parallel")))(_index_operand(ci), g, f)


def _sum_chips(landed, pairs, chip, ci, *, name):
    s, hr, wd = landed.shape
    tr = _pick(hr, 600, 16)

    def body(k_ref, l_ref, p_ref, o_ref):
        acc = None
        for k in range(s):
            part = jnp.where(k_ref[0] == k, p_ref[k], l_ref[k]).astype(F32)
            acc = part if acc is None else acc + part
        o_ref[...] = acc

    spec = pl.BlockSpec((s, tr, wd), lambda i, k: (0, i, 0))
    grid_spec = pltpu.PrefetchScalarGridSpec(
        num_scalar_prefetch=1, grid=(hr // tr,), in_specs=[spec, spec],
        out_specs=pl.BlockSpec((None, tr, wd), lambda i, k: (k[1], i, 0)))
    where = jnp.stack([chip, ci]).astype(jnp.int32)
    return _pcall(body, name=name, grid_spec=grid_spec, out_shape=jax.ShapeDtypeStruct((2, hr, wd), F32),
                  compiler_params=_params(("parallel",)))(where, landed, pairs)


class _ReduceScatter:
    def __init__(self, gs, chip, ci, tag):
        self.gs, self.chip, self.ci, self.tag = gs, chip, ci, tag
        self.half_t = [jax.ShapeDtypeStruct((g.shape[0],) + g.shape[2:], BF16) for g in gs]

    def pair(self, carrier=None):
        plan = lambda x, y, c: [((slice(None), 1 - c), (), (), (x, y, 1 - c))]
        self.r1 = _Ride(self.gs, self.half_t, plan)
        _round(self.r1, carrier, f"reduce_{self.tag}_pair")

    def chips(self, carrier=None):
        def plan(x, y, c):
            k = 2 * x + y
            return [((2 * px + py,), (k,), (2 * px + py,), (px, py, c)) for px, py in _other_chips(x, y)]

        self.pairs = [_add_pairs(g, f, self.ci, name=f"reduce_{self.tag}_pair_add{n}")
                      for n, (g, f) in enumerate(zip(self.gs, self.r1.out))]
        self.r2 = _Ride(self.pairs, self.half_t, plan)
        _round(self.r2, carrier, f"reduce_{self.tag}_chips")

    def halves(self, carrier=None):
        plan = lambda x, y, c: [((c,), (c,), (1 - c,), (x, y, 1 - c))]
        mine = [_sum_chips(l, p, self.chip, self.ci, name=f"reduce_{self.tag}_chip_sum{n}")
                for n, (l, p) in enumerate(zip(self.r2.out, self.pairs))]
        self.r3 = _Ride(None, [jax.ShapeDtypeStruct(m.shape, F32) for m in mine], plan, dst_inits=mine)
        _round(self.r3, carrier, f"reduce_{self.tag}_halves")

    def result(self):
        return [b.reshape(2 * b.shape[1], b.shape[2]) for b in self.r3.out]


def _ada_fwd(c_all, w_sh, b_sh, *, name):
    nb, d = c_all.shape
    wcols = w_sh.shape[1]
    tn = _pick(wcols, 512)

    def body(c_ref, w_ref, b_ref, o_ref, a_ref):
        cv = c_ref[...]
        act = cv * jax.nn.sigmoid(cv)
        a_ref[...] = act
        o_ref[...] = jnp.dot(act.astype(BF16), w_ref[...].astype(BF16), preferred_element_type=F32) + b_ref[...]

    return _pcall(
        body, name=name, grid=(wcols // tn,),
        in_specs=[pl.BlockSpec((nb, d), lambda j: (0, 0)), pl.BlockSpec((d, tn), lambda j: (0, j)),
                  pl.BlockSpec((1, tn), lambda j: (0, j))],
        out_specs=[pl.BlockSpec((nb, tn), lambda j: (0, j)), pl.BlockSpec((nb, d), lambda j: (0, 0))],
        out_shape=[jax.ShapeDtypeStruct((nb, wcols), F32), jax.ShapeDtypeStruct((nb, d), F32)],
        compiler_params=_params(("arbitrary",)))(c_all, w_sh, b_sh)


def _sum_devices(g, *, name):
    nd, m, w = g.shape

    def body(g_ref, o_ref):
        acc = g_ref[0]
        for k in range(1, nd):
            acc = acc + g_ref[k]
        o_ref[...] = acc

    return _pcall(body, name=name, out_shape=jax.ShapeDtypeStruct((m, w), F32),
                  compiler_params=pltpu.CompilerParams(vmem_limit_bytes=VMEM_LIMIT_BYTES))(g)


def _adamw(w, g, m, v, *, name):
    rows, cols = w.shape[-2:]
    tr = _pick(rows, max(8, (1 << 18) // cols), 8)
    c1 = 1.0 / (1.0 - ADAM_B1 ** ADAM_STEP)
    c2 = 1.0 / (1.0 - ADAM_B2 ** ADAM_STEP)

    def body(w_ref, g_ref, m_ref, v_ref, d_ref, nm_ref, nv_ref):
        gv = g_ref[...]
        nm = ADAM_B1 * m_ref[...] + (1.0 - ADAM_B1) * gv
        nv = ADAM_B2 * v_ref[...] + (1.0 - ADAM_B2) * (gv * gv)
        d_ref[...] = -ADAM_LR * ((nm * c1) / (jnp.sqrt(nv * c2) + ADAM_EPS) + ADAM_WD * w_ref[...])
        nm_ref[...] = nm
        nv_ref[...] = nv

    gspec = pl.BlockSpec((tr, cols), lambda i: (i, 0))
    spec = pl.BlockSpec((None, tr, cols), lambda i: (0, i, 0)) if w.ndim == 3 else gspec
    shp = jax.ShapeDtypeStruct(w.shape, F32)
    return _pcall(body, name=name, grid=(rows // tr,), in_specs=[spec, gspec, spec, spec], out_specs=[spec] * 3,
                  out_shape=[shp] * 3, compiler_params=_params(("parallel",)))(w, g, m, v)


def _permute_in_rows(wt):
    ngrp = len(B_PATTERNS)
    qb, kb, vb = (wt[A_W + n * QB_W:A_W + (n + 1) * QB_W] for n in range(3))
    parts = [wt[:A_W], jnp.zeros((VAR_W - A_W, wt.shape[1]), wt.dtype)]
    for g in range(ngrp):
        parts += [t[g * GB_W:(g + 1) * GB_W] for t in (qb, kb, vb)]
    return jnp.concatenate(parts + [wt[A_W + 3 * QB_W:]], axis=0)


def kernel(x, c, positions, w_ada, b_ada, w_in, sinks, w_branch_a, w_branch_b, w_o, ln1_g, ln1_b, w_gate_up, w_down, ln2_g, ln2_b, loss_target, m_w_ada, m_b_ada, m_w_in, m_sinks, m_w_branch_a, m_w_branch_b, m_w_o, m_ln1_g, m_ln1_b, m_w_gate_up, m_w_down, m_ln2_g, m_ln2_b, v_w_ada, v_b_ada, v_w_in, v_sinks, v_w_branch_a, v_w_branch_b, v_w_o, v_ln1_g, v_ln1_b, v_w_gate_up, v_w_down, v_ln2_g, v_ln2_b):
    xi, yi, ci = _coords()
    chip = 2 * xi + yi
    dev = 4 * xi + 2 * yi + ci
    NB, T, D = x.shape
    nchip, ndev = 4, 8
    ada_cols = w_ada.shape[2]

    ra, ro, rd = w_branch_a.shape[1], w_o.shape[1], w_down.shape[1]
    halves = lambda a: a.reshape(a.shape[:-2] + (2, a.shape[-2] // 2, a.shape[-1]))
    tr = lambda a: jnp.swapaxes(a, -1, -2)
    shards = [halves(w.astype(BF16))
              for w in (tr(w_in[0]), w_branch_a[0], w_o[0], w_down[0], w_branch_b[0], w_gate_up[0])]
    gin = _Gather(shards[:1], chip, "w_in", carriers=("gather_c", "ada_fwd", "gather_mod"))

    c_blk = jnp.zeros((8, D), F32).at[:NB].set(c)
    c_all = _allgather_small(c_blk, name="gather_c").reshape(ndev, 8, D)[:, :NB].reshape(ndev * NB, D)
    b_sh = lax.dynamic_slice(b_ada, (0, chip * ada_cols), (1, ada_cols))
    mod_part, c_act = _ada_fwd(c_all, w_ada[0], b_sh, name="ada_fwd")
    mod_g = _allgather_small(mod_part, name="gather_mod").reshape(nchip, 2, ndev * NB, ada_cols)[:, 0]
    mod_all = jnp.transpose(mod_g, (1, 0, 2)).reshape(ndev * NB, nchip * ada_cols)
    mod = lax.dynamic_slice(mod_all, (NB * dev, 0), (NB, nchip * ada_cols))

    (g_in,) = gin.result()
    w_in_f = _permute_in_rows(g_in.reshape(nchip * g_in.shape[1], D))
    mix = _Gather(shards[1:5], chip, "w_mix", carriers=("inproj_qkv", "attn_a_fwd"))
    ffn = _Gather(shards[5:], chip, "w_ffn", carriers=("attn_a_fwd", "attn_b_fwd", "mix_out"))

    def rest_weights():
        w_a_f, w_o_f, w_d_f, w_b_f = mix.result()
        return (w_a_f.reshape(nchip * ra, D), w_b_f, w_o_f.reshape(nchip * ro, D),
                lambda: ffn.result()[0], w_d_f.reshape(nchip * rd, D))

    red = {}

    def hook(event, **g):
        if event == "ffn_grads":
            red["ffn"] = _ReduceScatter([halves(g["g_wgu"])], chip, ci, "ffn")
            red["ffn"].pair(carrier="out_proj_dw")
        elif event == "rest_grads":
            by_chip = lambda a: halves(a.reshape(nchip, a.shape[0] // nchip, D))
            red["mix"] = _ReduceScatter([by_chip(g["g_wa"]), by_chip(g["g_wo"]), by_chip(g["g_wd"]), halves(g["g_wb"])],
                                        chip, ci, "mix")
            red["ffn"].chips(carrier="attn_a_bwd")
            red["mix"].pair(carrier="attn_a_bwd")
        elif event == "attn_a_bwd_done":
            red["ffn"].halves(carrier="attn_b0_bwd")
            red["mix"].chips(carrier="attn_b0_bwd")
        elif event == "attn_b0_bwd_done":
            red["mix"].halves(carrier="attn_b1_bwd")
        elif event == "win_grads":
            gr_in = g["g_win"]
            red["w_in"] = _ReduceScatter([halves(gr_in.reshape(nchip, gr_in.shape[0] // nchip, D))], chip, ci, "w_in")
            red["w_in"].pair(carrier="inproj_dx0")
        elif event == "inproj_dx0_done":
            red["w_in"].chips(carrier="inproj_dx1")
        elif event == "inproj_dx1_done":
            red["w_in"].halves(carrier="gather_small")

    res = _local_step(x, mod, positions, w_in_f, rest_weights, sinks[0], ln1_g, ln1_b, ln2_g, ln2_b, loss_target, hook)
    (g_w_a, g_w_o, g_w_d, g_w_b), (g_w_gu,) = red["mix"].result(), red["ffn"].result()

    small_rows = 24
    misc = jnp.zeros((1, D), F32).at[0, :A_Q_HEADS].set(res["dsink"]).at[0, A_Q_HEADS].set(jnp.sum(res["loss"]))
    small = jnp.concatenate([res["dmod"].reshape(NB * 6, D), jnp.sum(res["ln_grads"], axis=0), misc,
                             jnp.zeros((small_rows - NB * 6 - 5, D), F32)], axis=0)
    small_all = _allgather_small(small, name="gather_small").reshape(ndev, small_rows, D)
    (g_w_in,) = red["w_in"].result()
    dmod_all = small_all[:, :NB * 6].reshape(ndev * NB, 6 * D)
    sums = _sum_devices(small_all, name="sum_small")
    g_b_ada = (sums[0:6] + sums[6:12]).reshape(1, 6 * D)
    g_ln1_g, g_ln1_b, g_ln2_g, g_ln2_b = (sums[12 + n][None] for n in range(4))
    g_sinks = sums[16, :A_Q_HEADS][None]
    loss = sums[16, A_Q_HEADS]
    dmod_sh = lax.dynamic_slice(dmod_all, (0, chip * ada_cols), (ndev * NB, ada_cols))
    g_w_ada = _mm(c_act, dmod_sh, ta=True, name="ada_dw")

    names = ["w_ada", "b_ada", "w_in", "sinks", "w_branch_a", "w_branch_b", "w_o", "ln1_g", "ln1_b",
             "w_gate_up", "w_down", "ln2_g", "ln2_b"]
    ws = [w_ada, b_ada, w_in, sinks, w_branch_a, w_branch_b, w_o, ln1_g, ln1_b, w_gate_up, w_down, ln2_g, ln2_b]
    ms = [m_w_ada, m_b_ada, m_w_in, m_sinks, m_w_branch_a, m_w_branch_b, m_w_o, m_ln1_g, m_ln1_b, m_w_gate_up,
          m_w_down, m_ln2_g, m_ln2_b]
    vs = [v_w_ada, v_b_ada, v_w_in, v_sinks, v_w_branch_a, v_w_branch_b, v_w_o, v_ln1_g, v_ln1_b, v_w_gate_up,
          v_w_down, v_ln2_g, v_ln2_b]
    gs = [g_w_ada, g_b_ada, g_w_in, g_sinks, g_w_a, g_w_b, g_w_o, g_ln1_g, g_ln1_b, g_w_gu, g_w_d, g_ln2_g, g_ln2_b]
    grads, deltas, new_ms, new_vs = [], [], [], []
    for name, w, g, m, v in zip(names, ws, gs, ms, vs):
        flip = tr if name == "w_in" else (lambda a: a)
        w, m, v = flip(w), flip(m), flip(v)
        g2 = g.reshape(w.shape[-2:])
        d, nm, nv = _adamw(w, g2, m, v, name="adamw_" + name)
        grads.append(flip(g2.reshape(w.shape)))
        deltas.append(flip(d))
        new_ms.append(flip(nm))
        new_vs.append(flip(nv))
    return (loss, res["grad_x"], *grads, *deltas, *new_ms, *new_vs)
```

```python
import functools

import jax
import jax.numpy as jnp
from jax import lax
from jax.experimental import pallas as pl
from jax.experimental.pallas import tpu as pltpu

F32 = jnp.float32
BF16 = jnp.bfloat16
MESH = pl.DeviceIdType.MESH

HEAD_DIM = 64
LANES = 128
PAIR_W = 2 * HEAD_DIM
BLOCK = 128
A_Q_HEADS = 16
A_KV_HEADS = 2
A_WINDOW = 128
B_PATTERNS = ((128, 1), (512, 4), (2048, 16))
B_GROUP_HEADS = 8
QA_W = A_Q_HEADS * HEAD_DIM
KA_W = A_KV_HEADS * HEAD_DIM
GB_W = B_GROUP_HEADS * HEAD_DIM
QB_W = GB_W * len(B_PATTERNS)
A_W = QA_W + 2 * KA_W
VAR_W = 3 * GB_W
N_VAR = 1 + len(B_PATTERNS)
VAR_DIL = (1,) + tuple(r for _, r in B_PATTERNS)
A_BWD_SPLIT = 4
QKV_P = N_VAR * VAR_W
ROPE_THETA = 10000.0
LN_EPS = 1e-5
NEG_INF = -1e30
DEPTH = 1
ALPHA = (2 * DEPTH) ** 0.25
SCALE = HEAD_DIM ** -0.5

ADAM_LR, ADAM_B1, ADAM_B2, ADAM_EPS, ADAM_WD, ADAM_STEP = 0.001, 0.9, 0.999, 1e-08, 0.01, 10

VMEM_LIMIT_BYTES = 56 * 1024 * 1024
MM_TILE_BYTES = 36 * 1024 * 1024
MM_WHOLE_K = 4096


def _params(sem=None):
    return pltpu.CompilerParams(dimension_semantics=sem, vmem_limit_bytes=VMEM_LIMIT_BYTES)


_RIDES = {}


def _pcall(body, *, name, **kw):
    rides = _RIDES.pop(name, None)
    if rides is None:
        return pl.pallas_call(body, name=name, **kw)
    return _riding_call(body, rides, name=name, **kw)


def _copies(src_refs, dst_refs, send_sems, recv_sems, plan):
    x, y, c = lax.axis_index("x"), lax.axis_index("y"), lax.axis_index("c")
    remote = plan(x, y, c)
    nrem = len(remote)
    at = lambda ref, idx: ref.at[idx] if idx else ref

    def copy(a, n, landing):
        si, di, ri, peer = remote[n]
        return pltpu.make_async_remote_copy(
            src_ref=at(src_refs[a], si), dst_ref=at(dst_refs[a], ri if landing else di),
            send_sem=send_sems.at[a * nrem + n], recv_sem=recv_sems.at[a * nrem + n],
            device_id=peer, device_id_type=MESH)

    order = [(a, n) for a in range(len(dst_refs)) for n in range(nrem)]

    def start():
        for a, n in order:
            copy(a, n, False).start()

    def wait():
        for a, n in order:
            copy(a, n, True).wait_recv()
        for a, n in order:
            copy(a, n, False).wait_send()

    return start, wait


class _Ride:
    def __init__(self, srcs, dsts, plan, dst_inits=None):
        self.srcs, self.dsts, self.plan, self.dst_inits, self.out = srcs, dsts, plan, dst_inits, None


def _riding_call(body, rides, *, name, in_specs, out_specs, out_shape, grid=(), scratch_shapes=(), **kw):
    single = not isinstance(out_specs, (list, tuple))
    out_specs = [out_specs] if single else list(out_specs)
    out_shape = [out_shape] if single else list(out_shape)
    n_in, n_out, n_scr = len(in_specs), len(out_specs), len(scratch_shapes)
    xin, xdsts, sems, aliases, layout = [], [], [], {}, []
    for ride in rides:
        srcs = ride.srcs() if callable(ride.srcs) else ride.srcs
        inits = ride.dst_inits() if callable(ride.dst_inits) else ride.dst_inits
        na, nrem = len(ride.dsts), len(ride.plan(0, 0, 0))
        src_at = len(xin) if srcs is not None else None
        xin += list(srcs) if srcs is not None else []
        if inits is not None:
            aliases.update({n_in + len(xin) + a: n_out + len(xdsts) + a for a in range(na)})
            xin += list(inits)
        layout.append((src_at, len(xdsts), na))
        xdsts += list(ride.dsts)
        sems += [pltpu.SemaphoreType.DMA((na * nrem,)), pltpu.SemaphoreType.DMA((na * nrem,))]

    def wrapped(*refs):
        ins, xins = refs[:n_in], refs[n_in:n_in + len(xin)]
        outs = refs[n_in + len(xin):n_in + len(xin) + n_out]
        xouts = refs[n_in + len(xin) + n_out:n_in + len(xin) + n_out + len(xdsts)]
        scr = refs[n_in + len(xin) + n_out + len(xdsts):]
        rounds = []
        for k, (ride, (src_at, dst_at, na)) in enumerate(zip(rides, layout)):
            dsts = xouts[dst_at:dst_at + na]
            srcs = dsts if src_at is None else xins[src_at:src_at + na]
            rounds.append(_copies(srcs, dsts, scr[n_scr + 2 * k], scr[n_scr + 2 * k + 1], ride.plan))
        ids = [pl.program_id(a) for a in range(len(grid))]
        first = functools.reduce(jnp.logical_and, [i == 0 for i in ids], True)
        last = functools.reduce(jnp.logical_and, [i == g - 1 for i, g in zip(ids, grid)], True)

        def start_all():
            for start, _ in rounds:
                start()

        def wait_all():
            for _, wait in rounds:
                wait()

        start_all() if not grid else pl.when(first)(start_all)
        body(*ins, *outs, *scr[:n_scr])
        wait_all() if not grid else pl.when(last)(wait_all)

    hbm = pl.BlockSpec(memory_space=pl.ANY)
    gridkw = dict(grid=grid) if grid else {}

    def run(*args):
        res = pl.pallas_call(
            wrapped, name=name, in_specs=list(in_specs) + [hbm] * len(xin),
            out_specs=out_specs + [hbm] * len(xdsts), out_shape=out_shape + xdsts,
            scratch_shapes=list(scratch_shapes) + sems, input_output_aliases=aliases,
            compiler_params=_params(("arbitrary",) * len(grid) if grid else None), **gridkw,
        )(*args, *xin)
        for ride, (_, dst_at, na) in zip(rides, layout):
            ride.out = list(res[n_out + dst_at:n_out + dst_at + na])
        return res[0] if single else list(res[:n_out])

    return run


def _pick(n, target, quantum=128):
    t = (min(target, n) // quantum) * quantum
    while t >= quantum:
        if n % t == 0:
            return t
        t -= quantum
    return n


def _mm(a, b, *, name, ta=False, tb=False, b3=False, out3=0, out_dtype=F32, add=None, tm=1024, tn=1536, tk=1536,
        b_rows=None, m_rows=None, out_rows=None, out_init=None):
    if ta:
        K, M = a.shape
    else:
        M, K = a.shape
    M = m_rows or M
    if b3 and tb:
        Nn, K2, tk = b.shape[1], b.shape[0] * b.shape[2], b.shape[2]
    elif b3:
        K2, Nn, tn = b.shape[1], b.shape[0] * b.shape[2], b.shape[2]
    elif tb:
        Nn, K2 = b.shape
        if b_rows is not None:
            Nn = b_rows[1]
    else:
        K2, Nn = b.shape
    assert K == K2, (a.shape, b.shape)
    assert b_rows is None or (tb and not b3), "b_rows needs a plain transposed b"
    if out3:
        tn = Nn // out3
    tm, tn, tk = _pick(M, tm), _pick(Nn, tn), _pick(K, tk)
    if not (b3 and tb) and K <= MM_WHOLE_K:
        tk = K
        fits = lambda: 4 * tk * (tm + tn) + 8 * tm * tn * (2 if add is not None else 1) <= MM_TILE_BYTES
        while not fits():
            if (tm >= tn or b3 or out3) and tm > 256:
                tm = _pick(M, tm - 128)
            elif not (b3 or out3) and tn > 256:
                tn = _pick(Nn, tn - 128)
            else:
                break
    nk = K // tk
    j_outer = K * Nn + (Nn // tn) * M * K < M * K + (M // tm) * K * Nn
    dn = (((0 if ta else 1,), (1 if tb else 0,)), ((), ()))

    def body(*refs):
        refs = list(refs)
        a_ref, b_ref = refs[:2]
        add_ref = refs[2] if add is not None else None
        o_ref = refs[2 + (add is not None) + (out_init is not None)]
        part = lax.dot_general(a_ref[...].astype(BF16), b_ref[...].astype(BF16), dn, preferred_element_type=F32)

        def finish(r):
            if add is not None:
                r = r + add_ref[...]
            o_ref[...] = r.astype(out_dtype)

        if nk == 1:
            finish(part)
            return
        acc = refs[-1]
        k = pl.program_id(2)

        @pl.when(k == 0)
        def _():
            acc[...] = part

        @pl.when(k > 0)
        def _():
            acc[...] += part

        @pl.when(k == nk - 1)
        def _():
            finish(acc[...])

    def spec(shape, index):
        return pl.BlockSpec(shape, (lambda j, i, k: index(i, j, k)) if j_outer else index)

    a_spec = spec((tk, tm), lambda i, j, k: (k, i)) if ta else spec((tm, tk), lambda i, j, k: (i, k))
    if b3 and tb:
        b_spec = spec((None, tn, tk), lambda i, j, k: (k, j, 0))
    elif b3:
        b_spec = spec((None, tk, tn), lambda i, j, k: (j, k, 0))
    elif tb:
        first = 0 if b_rows is None else b_rows[0]
        assert first % tn == 0, (first, tn)
        b_spec = spec((tn, tk), lambda i, j, k: (first // tn + j, k))
    else:
        b_spec = spec((tk, tn), lambda i, j, k: (k, j))
    if out3:
        o_spec = spec((None, tm, tn), lambda i, j, k: (j, i, 0))
    elif out_rows:
        o_spec = spec((tm, tn), lambda i, j, k: (out_rows[1](i), j))
    else:
        o_spec = spec((tm, tn), lambda i, j, k: (i, j))
    ins, specs, aliases = [a, b], [a_spec, b_spec], {}
    if add is not None:
        ins.append(add)
        specs.append(o_spec)
    if out_init is not None:
        aliases = {len(ins): 0}
        ins.append(out_init)
        specs.append(pl.BlockSpec(memory_space=pl.ANY))
    grid = (Nn // tn, M // tm, nk) if j_outer else (M // tm, Nn // tn, nk)
    out_shape = (out3, M, tn) if out3 else (out_rows[0], Nn) if out_rows else (M, Nn)
    return _pcall(
        body, name=name, grid=grid, in_specs=specs, out_specs=o_spec,
        out_shape=jax.ShapeDtypeStruct(out_shape, out_dtype),
        scratch_shapes=[pltpu.VMEM((tm, tn), F32)] if nk > 1 else [],
        compiler_params=_params(("parallel", "parallel", "arbitrary")),
        **({"input_output_aliases": aliases} if aliases else {}),
    )(*ins)


def _mm_multi(a_list, b_list, *, name, M, T=None, add=None, out_dtype=F32, tm=512, post=None):
    tm = _pick(T or M, tm)
    ns = len(a_list)
    dils = [a[1] if isinstance(a, tuple) else 0 for a in a_list]
    a_arrs = [a[0] if isinstance(a, tuple) else a for a in a_list]
    widths = [a.shape[-1] // max(r, 1) for a, r in zip(a_arrs, dils)]
    b_arrs, b_specs = [], []
    for b in b_list:
        arr, shp, idx = b if isinstance(b, tuple) else (b, b.shape, (0, 0))
        b_arrs.append(arr)
        b_specs.append(pl.BlockSpec(shp, lambda i, idx=idx: idx))
    Nn = b_specs[0].block_shape[1]
    dn = (((1,), (0,)), ((), ()))
    nmm = 2 * ns + (1 if add is not None else 0)
    p_arrs, p_in_specs, p_out_specs, p_out_shape, p_fn = post or ([], [], None, None, None)
    nin = nmm + len(p_arrs)
    nout = len(p_out_specs) if post else 1

    def body(*refs):
        a_refs, b_refs, scr = refs[:ns], refs[ns:2 * ns], list(refs[nin + nout:])
        acc = None
        for a_ref, b_ref, r in zip(a_refs, b_refs, dils):
            av = _from_view(a_ref, scr.pop(0), r) if r > 1 else a_ref[...]
            part = lax.dot_general(av.astype(BF16), b_ref[...], dn, preferred_element_type=F32)
            acc = part if acc is None else acc + part
        if add is not None:
            acc = acc + refs[2 * ns][...]
        if post:
            p_fn(acc, refs[nmm:nin], refs[nin:nin + nout])
        else:
            refs[nin][...] = acc.astype(out_dtype)

    tpe = (T or M) // tm
    a_specs = [pl.BlockSpec((None, tm // r, r * w), lambda i: (i // tpe, i % tpe, 0)) if r
               else pl.BlockSpec((tm, w), lambda i: (i, 0)) for r, w in zip(dils, widths)]
    o_spec = pl.BlockSpec((tm, Nn), lambda i: (i, 0))
    specs = a_specs + b_specs
    ins = a_arrs + b_arrs
    if add is not None:
        specs.append(o_spec)
        ins.append(add)
    scratch = [pltpu.VMEM((w // LANES, tm, LANES), F32) for r, w in zip(dils, widths) if r > 1]
    return _pcall(body, name=name, grid=(M // tm,), in_specs=specs + list(p_in_specs),
                  out_specs=list(p_out_specs) if post else o_spec, scratch_shapes=scratch,
                  out_shape=list(p_out_shape) if post else jax.ShapeDtypeStruct((M, Nn), out_dtype),
                  compiler_params=_params(("arbitrary",) if post else ("parallel",)))(*ins, *p_arrs)


def _dw_view(d3, u, r, *, name, into, segs, tk=1024):
    NB, tsub, rw = d3.shape
    W, T, D = rw // r, tsub * r, u.shape[1]
    tk = _pick(T, tk)
    tpe, nk = T // tk, NB * T // tk

    def body(d_ref, u_ref, into_ref, o_ref, acc, scr, stage, sems):
        k = pl.program_id(0)
        dv = _from_view(d_ref, scr, r).astype(BF16)
        part = lax.dot_general(dv, u_ref[...], _TN, preferred_element_type=F32)

        @pl.when(k == 0)
        def _():
            acc[...] = part

        @pl.when(k > 0)
        def _():
            acc[...] += part

        @pl.when(k == nk - 1)
        def _():
            stage[...] = acc[...].astype(BF16)
            copies = [pltpu.make_async_copy(stage.at[pl.ds(src, n)], o_ref.at[pl.ds(dst, n)], sems.at[q])
                      for q, (src, dst, n) in enumerate(segs)]
            for cp in copies:
                cp.start()
            for cp in copies:
                cp.wait()

    hbm = pl.BlockSpec(memory_space=pl.ANY)
    return _pcall(
        body, name=name, grid=(nk,),
        in_specs=[pl.BlockSpec((None, tk // r, rw), lambda k: (k // tpe, k % tpe, 0)), pl.BlockSpec((tk, D), lambda k: (k, 0)),
                  hbm],
        out_specs=hbm, out_shape=jax.ShapeDtypeStruct(into.shape, BF16), input_output_aliases={2: 0},
        scratch_shapes=[pltpu.VMEM((W, D), F32), pltpu.VMEM((W // LANES, tk, LANES), F32), pltpu.VMEM((W, D), BF16),
                        pltpu.SemaphoreType.DMA((len(segs),))],
        compiler_params=_params(("arbitrary",)))(d3, u, into)


def _lane(shape):
    return lax.broadcasted_iota(jnp.int32, shape, len(shape) - 1)


def _rot_half(v):
    w = v.shape[-1]
    first = (_lane(v.shape) % HEAD_DIM) < (HEAD_DIM // 2)
    return jnp.where(first, pltpu.roll(v, w - HEAD_DIM // 2, v.ndim - 1), pltpu.roll(v, HEAD_DIM // 2, v.ndim - 1))


def _widen(t, w):
    return t if w == t.shape[-1] else jnp.concatenate([t] * (w // t.shape[-1]), axis=-1)


def _unrope(v, cos, sins):
    w = v.shape[-1]
    return v * _widen(cos, w) - _rot_half(v) * _widen(sins, w)


def _rope_tables(positions):
    half = HEAD_DIM // 2
    ngrp = PAIR_W // half
    inv = ROPE_THETA ** (-jnp.arange(half, dtype=F32) / half)
    inv_row = jnp.concatenate([inv] * ngrp)[None]
    n = positions.shape[0] * positions.shape[1]
    rows = _pick(n // ngrp, 256, 8)

    def body(p_ref, inv_ref, c_ref, s_ref):
        grp = _lane((rows, PAIR_W)) // half
        pick = lambda vals: functools.reduce(lambda acc, q: jnp.where(grp == q, vals[q], acc), range(ngrp - 1), vals[-1])
        ang = pick([p_ref[q * rows:(q + 1) * rows, :] for q in range(ngrp)]) * inv_ref[...]
        cos, sin = jnp.cos(ang), jnp.sin(ang)
        for q in range(ngrp):
            spread = lambda t: pick([pltpu.roll(t, ((g - q) % ngrp) * half, 1) if g != q else t for g in range(ngrp)])
            sq = spread(sin)
            c_ref[q * rows:(q + 1) * rows, :] = spread(cos)
            s_ref[q * rows:(q + 1) * rows, :] = jnp.where(grp % 2 == 0, -sq, sq)

    out = jax.ShapeDtypeStruct((n, PAIR_W), F32)
    blk = pl.BlockSpec((ngrp * rows, PAIR_W), lambda i: (i, 0))
    return _pcall(
        body, name="rope_tables", grid=(n // (ngrp * rows),),
        in_specs=[pl.BlockSpec((ngrp * rows, 1), lambda i: (i, 0)), pl.BlockSpec((1, PAIR_W), lambda i: (0, 0))],
        out_specs=[blk, blk], out_shape=[out, out], compiler_params=_params(("parallel",)),
    )(positions.astype(F32).reshape(n, 1), inv_row)


def _inproj(x2, scale, shift, w, cosf, sins, flags, *, T, name):
    N, D = x2.shape
    tm, tn = _pick(T, 512), VAR_W
    tpe = T // tm

    def body(x_ref, sc_ref, sh_ref, w_ref, c_ref, s_ref, f_ref, *outs):
        o_refs, u_ref = outs[:N_VAR], outs[N_VAR]
        j = pl.program_id(1)

        @pl.when(j == 0)
        def _():
            u_ref[...] = (x_ref[...] * (1.0 + sc_ref[0]) + sh_ref[0]).astype(BF16)

        acc = lax.dot_general(u_ref[...], w_ref[...], (((1,), (1,)), ((), ())), preferred_element_type=F32)
        fl = f_ref[...]
        ce = 1.0 + (_widen(c_ref[...], tn) - 1.0) * fl
        se = _widen(s_ref[...], tn) * fl
        res = acc * ce + _rot_half(acc) * se
        for v in range(N_VAR):
            @pl.when(j == v)
            def _(v=v):
                _to_view(res, o_refs[v], outs[N_VAR + 1], VAR_DIL[v])

    ex = pl.BlockSpec((1, 1, D), lambda i, j: (i // tpe, 0, 0))
    tab = pl.BlockSpec((tm, PAIR_W), lambda i, j: (i, 0))
    keep = lambda w_: pl.BlockSpec((tm, w_), lambda i, j: (i, 0))
    vspec = lambda r: pl.BlockSpec((None, tm // r, r * tn), lambda i, j: (i // tpe, i % tpe, 0))
    vshape = lambda r: jax.ShapeDtypeStruct((N // T, T // r, r * tn), BF16)
    return _pcall(
        body, name=name, grid=(N // tm, N_VAR),
        in_specs=[keep(D), ex, ex, pl.BlockSpec((tn, D), lambda i, j: (j, 0)), tab, tab,
                  pl.BlockSpec((1, tn), lambda i, j: (0, j))],
        out_specs=[vspec(r) for r in VAR_DIL] + [keep(D)],
        out_shape=[vshape(r) for r in VAR_DIL] + [jax.ShapeDtypeStruct((N, D), BF16)],
        scratch_shapes=[pltpu.VMEM((tn // LANES, tm, LANES), F32)],
        compiler_params=_params(("parallel", "arbitrary")),
    )(x2, scale, shift, w, cosf, sins, flags)


class _Geom:
    def __init__(self, g):
        if g is None:
            self.r, self.nq, self.n_back, self.sink = 1, A_Q_HEADS, A_WINDOW - 1, True
            self.qw, self.kw = QA_W, KA_W
            self.qidx = lambda j: 0
            self.kidx = lambda j: QA_W // KA_W
            self.vidx = lambda j: QA_W // KA_W + 1
        else:
            window, r = B_PATTERNS[g]
            self.r, self.nq, self.n_back, self.sink = r, B_GROUP_HEADS, window // r, False
            self.qw, self.kw = GB_W, GB_W
            self.qidx = lambda j: 3 * j
            self.kidx = lambda j: 3 * j + 1
            self.vidx = lambda j: 3 * j + 2
        self.ntile = self.qw // PAIR_W


def _stack_heads(t, scale=None):
    first = _lane(t.shape) < HEAD_DIM
    z = jnp.zeros_like(t)
    if scale is not None:
        t = t * jnp.asarray(scale, t.dtype)
    return jnp.concatenate([jnp.where(first, t, z), jnp.where(first, z, t)], axis=0)


def _unstack_heads(v2):
    return jnp.where(_lane((BLOCK, PAIR_W)) < HEAD_DIM, v2[:BLOCK], v2[BLOCK:])


def _dup_head(t, kh):
    tf = t.astype(F32)
    keep = (_lane(t.shape) < HEAD_DIM) if kh == 0 else (_lane(t.shape) >= HEAD_DIM)
    return jnp.where(keep, tf, pltpu.roll(tf, HEAD_DIM, 1)).astype(t.dtype)


def _fold_heads(t):
    return t + pltpu.roll(t, HEAD_DIM, 1)


def _band_mask(rows, i, n_back, single):
    nkeys = BLOCK if single else 2 * BLOCK
    qi = jnp.bitwise_and(lax.broadcasted_iota(jnp.int32, (rows, nkeys), 0), BLOCK - 1)
    ki = lax.broadcasted_iota(jnp.int32, (rows, nkeys), 1)
    if single:
        return qi >= ki
    dist = qi + BLOCK - ki
    return jnp.logical_and(jnp.logical_and(dist >= 0, dist <= n_back), jnp.logical_or(ki >= BLOCK, i > 0))


def _sink_slot(rows):
    qi = jnp.bitwise_and(lax.broadcasted_iota(jnp.int32, (rows, 2 * BLOCK), 0), BLOCK - 1)
    return qi == lax.broadcasted_iota(jnp.int32, (rows, 2 * BLOCK), 1)


def _sink_scores(rows, sinks):
    blk = lax.broadcasted_iota(jnp.int32, (rows, 2 * BLOCK), 0) // BLOCK
    out = jnp.full((rows, 2 * BLOCK), sinks[-1], F32)
    for b in range(len(sinks) - 2, -1, -1):
        out = jnp.where(blk == b, sinks[b], out)
    return out


def _softmax_parts(s, valid, sinks):
    s = jnp.where(valid, s, NEG_INF)
    if sinks is not None:
        slot = _sink_slot(s.shape[0])
        s = jnp.where(slot, _sink_scores(s.shape[0], sinks), s)
    m = jnp.max(s, axis=1, keepdims=True)
    p = jnp.exp(s - m)
    den = jnp.sum(p, axis=1, keepdims=True)
    if sinks is not None:
        p = jnp.where(slot, 0.0, p)
    return p, m, den


_NT = (((1,), (1,)), ((), ()))
_TN = (((0,), (0,)), ((), ()))


def _rows2(prev_ref, cur_ref, cs, single=False):
    if single:
        return cur_ref[0, :, cs]
    return jnp.concatenate([prev_ref[0, :, cs], cur_ref[0, :, cs]], axis=0)


def _sink_scalars(sink_ref, first, nblocks):
    return [sink_ref[first + b] for b in range(nblocks)]


def _tile(t):
    return slice(t * PAIR_W, (t + 1) * PAIR_W)


def _attn_fwd(qkv, sinks, g, *, NB, T, name):
    geo = _Geom(g)
    r, qw, kw, ntile = geo.r, geo.qw, geo.kw, geo.ntile
    tsub = T // r
    nblk = tsub // BLOCK
    qkv3 = qkv.reshape(NB, tsub, r * VAR_W)
    out_dtype = BF16 if g is None else F32
    tiles_per_kv = ntile // A_KV_HEADS

    single = nblk == 1

    def body(q_ref, kp_ref, kc_ref, vp_ref, vc_ref, sink_ref, o_ref, l_ref):
        i = pl.program_id(2)
        if geo.sink:
            kall, vall = _rows2(kp_ref, kc_ref, _tile(0)), _rows2(vp_ref, vc_ref, _tile(0))
            kdup = [_dup_head(kall, kh) for kh in range(A_KV_HEADS)]
            vdup = [_dup_head(vall, kh) for kh in range(A_KV_HEADS)]
            tiles = [[t] for t in range(ntile)]
            q2s = [_stack_heads(q_ref[0, :, _tile(t)], SCALE) for t in range(ntile)]
            kks = [kdup[t // tiles_per_kv] for t in range(ntile)]
            vvs = [vdup[t // tiles_per_kv] for t in range(ntile)]
            sinkcols = [_sink_scalars(sink_ref, 2 * t, 2) for t in range(ntile)]
        else:
            tiles = [[t] for t in range(ntile)]
            q2s = [_stack_heads(q_ref[0, :, _tile(t)], SCALE) for t in range(ntile)]
            kks = [_rows2(kp_ref, kc_ref, _tile(t), single) for t in range(ntile)]
            vvs = [_rows2(vp_ref, vc_ref, _tile(t), single) for t in range(ntile)]
            sinkcols = [None] * ntile
        valid = _band_mask(q2s[0].shape[0], i, geo.n_back, single)
        ss = [lax.dot_general(q2, kk, _NT, preferred_element_type=F32) for q2, kk in zip(q2s, kks)]
        parts = [_softmax_parts(s, valid, sc) for s, sc in zip(ss, sinkcols)]
        o2s = [jnp.dot(p.astype(BF16), vv, preferred_element_type=F32) / den for (p, m, den), vv in zip(parts, vvs)]
        for ts, o2, (p, m, den) in zip(tiles, o2s, parts):
            lse2 = jnp.broadcast_to(m + jnp.log(den), (o2.shape[0], PAIR_W))
            for n, t in enumerate(ts):
                rows = slice(2 * BLOCK * n, 2 * BLOCK * (n + 1))
                o_ref[0, :, _tile(t)] = _unstack_heads(o2[rows]).astype(out_dtype)
                l_ref[0, :, _tile(t)] = _unstack_heads(lse2[rows])

    prev = lambda i: jnp.maximum(i - 1, 0)
    in_specs = [
        pl.BlockSpec((1, BLOCK, qw), lambda b, j, i: (b, i, geo.qidx(j))),
        pl.BlockSpec((1, BLOCK, kw), lambda b, j, i: (b, prev(i), geo.kidx(j))),
        pl.BlockSpec((1, BLOCK, kw), lambda b, j, i: (b, i, geo.kidx(j))),
        pl.BlockSpec((1, BLOCK, kw), lambda b, j, i: (b, prev(i), geo.vidx(j))),
        pl.BlockSpec((1, BLOCK, kw), lambda b, j, i: (b, i, geo.vidx(j))),
        pl.BlockSpec(memory_space=pltpu.SMEM),
    ]
    o_spec = pl.BlockSpec((1, BLOCK, qw), lambda b, j, i: (b, i, j))
    shape = (NB, tsub, r * qw)
    o, lse = _pcall(
        body, name=name, grid=(NB, r, nblk), in_specs=in_specs, out_specs=[o_spec, o_spec],
        out_shape=[jax.ShapeDtypeStruct(shape, out_dtype), jax.ShapeDtypeStruct(shape, F32)],
        compiler_params=_params(("parallel", "parallel", "arbitrary")),
    )(qkv3, qkv3, qkv3, qkv3, qkv3, sinks)
    return o, lse


def _attn_fwd_b(qkvs, *, NB, T, name):
    geos = [_Geom(g) for g in range(len(B_PATTERNS))]
    steps = T // BLOCK
    nt = GB_W // PAIR_W
    ng = len(geos)

    def where(geo, s):
        nblk = T // geo.r // BLOCK
        return s // steps, (s % steps) // nblk, (s % steps) % nblk

    def body(*refs):
        ins, outs = refs[:5 * ng], refs[5 * ng:]
        s = pl.program_id(0)
        q2s, kks, vvs, valids = [], [], [], []
        for n, geo in enumerate(geos):
            q_ref, kp_ref, kc_ref, vp_ref, vc_ref = ins[5 * n:5 * n + 5]
            single = T // geo.r // BLOCK == 1
            valid = _band_mask(2 * BLOCK, where(geo, s)[2], geo.n_back, single)
            for t in range(nt):
                q2s.append(_stack_heads(q_ref[0, :, _tile(t)], SCALE))
                kks.append(_rows2(kp_ref, kc_ref, _tile(t), single))
                vvs.append(_rows2(vp_ref, vc_ref, _tile(t), single))
                valids.append(valid)
        ss = [lax.dot_general(q2, kk, _NT, preferred_element_type=F32) for q2, kk in zip(q2s, kks)]
        parts = [_softmax_parts(sc, valid, None) for sc, valid in zip(ss, valids)]
        o2s = [jnp.dot(p.astype(BF16), vv, preferred_element_type=F32) / den for (p, m, den), vv in zip(parts, vvs)]
        for n in range(ng):
            o_ref, l_ref = outs[2 * n], outs[2 * n + 1]
            for t in range(nt):
                o2, (p, m, den) = o2s[n * nt + t], parts[n * nt + t]
                o_ref[0, :, _tile(t)] = _unstack_heads(o2)
                l_ref[0, :, _tile(t)] = _unstack_heads(jnp.broadcast_to(m + jnp.log(den), (2 * BLOCK, PAIR_W)))

    in_specs, ins, out_specs, out_shape = [], [], [], []
    for geo, qkv in zip(geos, qkvs):
        tsub = T // geo.r
        pos = lambda s, geo=geo: where(geo, s)
        prev = lambda i: jnp.maximum(i - 1, 0)
        blk = lambda col, back, pos=pos: pl.BlockSpec(
            (1, BLOCK, GB_W), lambda s: (pos(s)[0], prev(pos(s)[2]) if back else pos(s)[2], col(pos(s)[1])))
        in_specs += [blk(geo.qidx, False), blk(geo.kidx, True), blk(geo.kidx, False), blk(geo.vidx, True),
                     blk(geo.vidx, False)]
        ins += [qkv.reshape(NB, tsub, geo.r * VAR_W)] * 5
        out_specs += [blk(lambda j: j, False)] * 2
        out_shape += [jax.ShapeDtypeStruct((NB, tsub, geo.r * GB_W), F32)] * 2
    res = _pcall(body, name=name, grid=(NB * steps,), in_specs=in_specs, out_specs=out_specs, out_shape=out_shape,
                 compiler_params=_params(("arbitrary",)))(*ins)
    return [(res[2 * n], res[2 * n + 1]) for n in range(ng)]


def _attn_bwd(qkv, do, lse, dlse, cosf, sins, sinks, g, *, NB, T, name):
    geo = _Geom(g)
    r, qw, kw, ntile = geo.r, geo.qw, geo.kw, geo.ntile
    tsub = T // r
    nblk = tsub // BLOCK
    view = lambda a, w: a.reshape(NB, tsub, r * w)
    has_dlse = dlse is not None
    tiles_per_kv = ntile // A_KV_HEADS

    single = nblk == 1
    krows = BLOCK if single else 2 * BLOCK
    nsteps = 1 if single else nblk + 1

    def grads(q2s, kks, vvs, do2s, i, lserows, sinkcols, dlrows):
        nrow = q2s[0].shape[0]
        ki = lax.broadcasted_iota(jnp.int32, (krows, nrow), 0)
        qi = jnp.bitwise_and(lax.broadcasted_iota(jnp.int32, (krows, nrow), 1), BLOCK - 1)
        if single:
            valid = qi >= ki
        else:
            dist = qi + BLOCK - ki
            valid = jnp.logical_and(jnp.logical_and(dist >= 0, dist <= geo.n_back), jnp.logical_or(ki >= BLOCK, i > 0))
        sts = [lax.dot_general(kk, q2, _NT, preferred_element_type=F32) for q2, kk in zip(q2s, kks)]
        dpts = [lax.dot_general(vv, do2, _NT, preferred_element_type=F32) for do2, vv in zip(do2s, vvs)]
        pts, dsts, sks = [], [], []
        for st, dpt, ls, sc, dl in zip(sts, dpts, lserows, sinkcols, dlrows):
            sv = jnp.where(valid, st, NEG_INF)
            if sc is not None:
                slot = ki == qi
                blk = lax.broadcasted_iota(jnp.int32, (krows, nrow), 1) // BLOCK
                sink = jnp.full((krows, nrow), sc[-1], F32)
                for b in range(len(sc) - 2, -1, -1):
                    sink = jnp.where(blk == b, sc[b], sink)
                sv = jnp.where(slot, sink, sv)
                dpt = jnp.where(slot, 0.0, dpt)
            pt = jnp.exp(sv - ls)
            delta = jnp.sum(pt * dpt, axis=0, keepdims=True)
            if dl is not None:
                delta = delta - dl
            dst = pt * (dpt - delta)
            if sc is not None:
                cols = lambda a, b: a[:, b * BLOCK:(b + 1) * BLOCK]
                sks.append([jnp.sum(jnp.where(cols(slot, b), cols(dst, b), 0.0)) for b in range(len(sc))])
                dst, pt = jnp.where(slot, 0.0, dst), jnp.where(slot, 0.0, pt)
            else:
                sks.append(None)
            pts.append(pt.astype(BF16))
            dsts.append(dst.astype(BF16))
        dq2s = [lax.dot_general(dst, kk, _TN, preferred_element_type=F32) * SCALE for dst, kk in zip(dsts, kks)]
        dkks = [jnp.dot(dst, q2, preferred_element_type=F32) for dst, q2 in zip(dsts, q2s)]
        dvvs = [jnp.dot(pt, do2, preferred_element_type=F32) for pt, do2 in zip(pts, do2s)]
        return dq2s, dkks, dvvs, sks

    def stat_row(t):
        tt = t.T
        return jnp.concatenate([tt[0:1, :], tt[HEAD_DIM:HEAD_DIM + 1, :]], axis=1)

    def body(*refs):
        it = iter(refs)
        q_ref, kp_ref, kc_ref, vp_ref, vc_ref, do_ref, l_ref = (next(it) for _ in range(7))
        dl_ref = next(it) if has_dlse else None
        c_ref, s_ref, sink_ref, o_ref, ds_ref, dq_s, dk_s, dv_s, car_q, car_k, car_v = (next(it) for _ in range(11))
        b, j, i = pl.program_id(0), pl.program_id(1), pl.program_id(2)

        @pl.when(jnp.logical_and(b == 0, jnp.logical_and(j == 0, i == 0)))
        def _():
            ds_ref[...] = jnp.zeros_like(ds_ref)

        def compute():
            if geo.sink:
                kall, vall = _rows2(kp_ref, kc_ref, _tile(0)), _rows2(vp_ref, vc_ref, _tile(0))
                tps = tiles_per_kv // A_BWD_SPLIT
                nb = 2 * tps
                tiles = [[kh * tiles_per_kv + s_ * tps + t for t in range(tps)]
                         for kh in range(A_KV_HEADS) for s_ in range(A_BWD_SPLIT)]
                kdup = [_dup_head(kall, kh) for kh in range(A_KV_HEADS)]
                vdup = [_dup_head(vall, kh) for kh in range(A_KV_HEADS)]
                cat = lambda f, ts: jnp.concatenate([f(t) for t in ts], axis=0)
                dq2s, dkks, dvvs, sks = grads(
                    [cat(lambda t: _stack_heads(q_ref[0, :, _tile(t)], SCALE), ts) for ts in tiles],
                    [kdup[n // A_BWD_SPLIT] for n in range(len(tiles))],
                    [vdup[n // A_BWD_SPLIT] for n in range(len(tiles))],
                    [cat(lambda t: _stack_heads(do_ref[0, :, _tile(t)]), ts) for ts in tiles], i,
                    [jnp.concatenate([stat_row(l_ref[0, :, _tile(t)]) for t in ts], axis=1) for ts in tiles],
                    [_sink_scalars(sink_ref, 2 * ts[0], nb) for ts in tiles], [None] * len(tiles))
                lane1 = _lane((1, PAIR_W))
                dsink = jnp.zeros((1, PAIR_W), F32)
                for ts, dq2, sk in zip(tiles, dq2s, sks):
                    for n, t in enumerate(ts):
                        dq_s[:, _tile(t)] = _unstack_heads(dq2[2 * BLOCK * n:2 * BLOCK * (n + 1)])
                    for bb in range(nb):
                        dsink = dsink + jnp.where(lane1 == 2 * ts[0] + bb, sk[bb], 0.0)
                per_kv = lambda parts, kh: functools.reduce(jnp.add, parts[kh * A_BWD_SPLIT:(kh + 1) * A_BWD_SPLIT])
                second = _lane((krows, PAIR_W)) >= HEAD_DIM
                dk_s[...] = jnp.where(second, _fold_heads(per_kv(dkks, 1)), _fold_heads(per_kv(dkks, 0)))
                dv_s[...] = jnp.where(second, _fold_heads(per_kv(dvvs, 1)), _fold_heads(per_kv(dvvs, 0)))
                ds_ref[0:1, :] += dsink
            else:
                dq2s, dkks, dvvs, _ = grads(
                    [_stack_heads(q_ref[0, :, _tile(t)], SCALE) for t in range(ntile)],
                    [_rows2(kp_ref, kc_ref, _tile(t), single) for t in range(ntile)],
                    [_rows2(vp_ref, vc_ref, _tile(t), single) for t in range(ntile)],
                    [_stack_heads(do_ref[0, :, _tile(t)]) for t in range(ntile)], i,
                    [stat_row(l_ref[0, :, _tile(t)]) for t in range(ntile)], [None] * ntile,
                    [stat_row(dl_ref[0, :, _tile(t)]) for t in range(ntile)])
                for t in range(ntile):
                    dq_s[:, _tile(t)] = _unstack_heads(dq2s[t])
                    dk_s[0:krows, _tile(t)] = dkks[t]
                    dv_s[0:krows, _tile(t)] = dvvs[t]

        def emit(dq, dk, dv):
            cos, sn = c_ref[0], s_ref[0]
            o_ref[0, :, 0:qw] = _unrope(dq, cos, sn).astype(BF16)
            o_ref[0, :, qw:qw + kw] = _unrope(dk, cos, sn).astype(BF16)
            o_ref[0, :, qw + kw:qw + 2 * kw] = dv.astype(BF16)
            if qw + 2 * kw < VAR_W:
                o_ref[0, :, qw + 2 * kw:VAR_W] = jnp.zeros((BLOCK, VAR_W - qw - 2 * kw), BF16)

        if single:
            compute()
            emit(dq_s[...], dk_s[0:BLOCK, :], dv_s[0:BLOCK, :])
            return

        @pl.when(i == 0)
        def _():
            car_q[...] = jnp.zeros_like(car_q)
            car_k[...] = jnp.zeros_like(car_k)
            car_v[...] = jnp.zeros_like(car_v)

        @pl.when(i == nblk)
        def _():
            dk_s[...] = jnp.zeros_like(dk_s)
            dv_s[...] = jnp.zeros_like(dv_s)

        pl.when(i < nblk)(compute)
        emit(car_q[...], car_k[...] + dk_s[0:BLOCK, :], car_v[...] + dv_s[0:BLOCK, :])
        car_q[...] = dq_s[...]
        car_k[...] = dk_s[BLOCK:2 * BLOCK, :]
        car_v[...] = dv_s[BLOCK:2 * BLOCK, :]

    cur = lambda i: jnp.minimum(i, nblk - 1)
    prv = lambda i: jnp.maximum(jnp.minimum(i, nblk - 1) - 1, 0)
    outb = lambda i: jnp.maximum(i - 1, 0)
    qrow = pl.BlockSpec((1, BLOCK, qw), lambda b, j, i: (b, cur(i), j))
    in_specs = [
        pl.BlockSpec((1, BLOCK, qw), lambda b, j, i: (b, cur(i), geo.qidx(j))),
        pl.BlockSpec((1, BLOCK, kw), lambda b, j, i: (b, prv(i), geo.kidx(j))),
        pl.BlockSpec((1, BLOCK, kw), lambda b, j, i: (b, cur(i), geo.kidx(j))),
        pl.BlockSpec((1, BLOCK, kw), lambda b, j, i: (b, prv(i), geo.vidx(j))),
        pl.BlockSpec((1, BLOCK, kw), lambda b, j, i: (b, cur(i), geo.vidx(j))),
        qrow, qrow,
    ]
    ins = [view(qkv, VAR_W)] * 5 + [view(do, qw), view(lse, qw)]
    if has_dlse:
        in_specs.append(qrow)
        ins.append(view(dlse, qw))
    in_specs += [
        pl.BlockSpec((1, BLOCK, PAIR_W), lambda b, j, i: (b, outb(i), j)),
        pl.BlockSpec((1, BLOCK, PAIR_W), lambda b, j, i: (b, outb(i), j)),
        pl.BlockSpec(memory_space=pltpu.SMEM),
    ]
    ins += [view(cosf, PAIR_W), view(sins, PAIR_W), sinks]
    scratch = [pltpu.VMEM((BLOCK, qw), F32), pltpu.VMEM((2 * BLOCK, kw), F32), pltpu.VMEM((2 * BLOCK, kw), F32),
               pltpu.VMEM((BLOCK, qw), F32), pltpu.VMEM((BLOCK, kw), F32), pltpu.VMEM((BLOCK, kw), F32)]
    dqkv, dsink = _pcall(
        body, name=name, grid=(NB, r, nsteps), in_specs=in_specs,
        out_specs=[pl.BlockSpec((1, BLOCK, VAR_W), lambda b, j, i: (b, outb(i), j)),
                   pl.BlockSpec((8, PAIR_W), lambda b, j, i: (0, 0))],
        out_shape=[jax.ShapeDtypeStruct((NB, tsub, r * VAR_W), BF16), jax.ShapeDtypeStruct((8, PAIR_W), F32)],
        scratch_shapes=scratch, compiler_params=_params(("arbitrary", "arbitrary", "arbitrary")),
    )(*ins)
    return dqkv, dsink


class _Rows:
    def __init__(self, N, T, tm):
        self.N, self.tm, self.tpe, self.grid = N, tm, T // tm, (N // tm,)

    def row(self, w, col=0):
        return pl.BlockSpec((self.tm, w), lambda i: (i, col))

    def ex(self, w):
        return pl.BlockSpec((1, 1, w), lambda i: (i // self.tpe, 0, 0))

    def const(self, shape):
        return pl.BlockSpec(shape, lambda i: tuple(0 for _ in shape))

    def view(self, w, r):
        return pl.BlockSpec((None, self.tm // r, r * w), lambda i: (i // self.tpe, i % self.tpe, 0))

    def first_of_example(self):
        return pl.program_id(0) % self.tpe == 0


def _acc(ref, first, val):
    @pl.when(first)
    def _():
        ref[0] = val

    @pl.when(jnp.logical_not(first))
    def _():
        ref[0] += val


def _colsum(v):
    return jnp.sum(v, axis=0, keepdims=True)


def _ln_stats(r):
    mu = jnp.mean(r, axis=-1, keepdims=True)
    xc = r - mu
    var = jnp.mean(xc * xc, axis=-1, keepdims=True)
    rstd = lax.rsqrt(var + LN_EPS)
    return xc * rstd, rstd


def _ln_bwd(dy, xhat, rstd, gain):
    dxh = dy * gain
    return rstd * (dxh - jnp.mean(dxh, axis=-1, keepdims=True) - xhat * jnp.mean(dxh * xhat, axis=-1, keepdims=True))


def _from_view(ref, scr, r):
    if r == 1:
        return ref[...]
    rows, w = ref.shape[0], ref.shape[1] // r
    for j in range(r):
        for c in range(w // LANES):
            scr.at[c][pl.ds(j, rows, stride=r), :] = ref[:, j * w + c * LANES:j * w + (c + 1) * LANES].astype(F32)
    return jnp.concatenate([scr[c] for c in range(w // LANES)], axis=1)


def _to_view(val, ref, scr, r):
    if r == 1:
        ref[...] = val.astype(ref.dtype)
        return
    rows, w = ref.shape[0], ref.shape[1] // r
    for c in range(w // LANES):
        scr[c] = val[:, c * LANES:(c + 1) * LANES]
    for j in range(r):
        for c in range(w // LANES):
            ref[:, j * w + c * LANES:j * w + (c + 1) * LANES] = scr.at[c][pl.ds(j, rows, stride=r), :].astype(ref.dtype)


def _silu_parts(v):
    s = jax.nn.sigmoid(v)
    return v * s, s * (1.0 + v * (1.0 - s))


def _local_step(x, mod, positions, w_in, rest_weights, sinks, ln1_g, ln1_b, ln2_g, ln2_b, target, hook=None):
    hook = hook or (lambda event, **data: None)
    NB, T, D = x.shape
    N = NB * T
    x2 = x.reshape(N, D)
    tgt2 = target.reshape(N, D)
    shift_m, scale_m, gate_m, shift_f, scale_f, gate_f = [mod[:, None, k * D:(k + 1) * D] for k in range(6)]
    cosf, sins = _rope_tables(positions)
    col = jnp.arange(QKV_P)
    vcol = col % VAR_W
    flags = jnp.where(col < VAR_W, vcol < QA_W + KA_W, vcol < 2 * GB_W).astype(F32)[None]
    R = _Rows(N, T, _pick(T, 256))
    sds = jax.ShapeDtypeStruct
    exsum = lambda w=D: sds((NB, 1, w), F32)
    ngrp = len(B_PATTERNS)

    *qkv, u = _inproj(x2, scale_m, shift_m, w_in, cosf, sins, flags, T=T, name="inproj_qkv")
    gates = _mm(u, w_in, tb=True, b_rows=(QKV_P, w_in.shape[0] - QKV_P), out_dtype=BF16, name="inproj_gates")
    oa, la = _attn_fwd(qkv[0], sinks, None, NB=NB, T=T, name="attn_a_fwd")
    oa = oa.reshape(N, QA_W)
    (o1, l1), (o2, l2), (o3, l3) = _attn_fwd_b(qkv[1:], NB=NB, T=T, name="attn_b_fwd")
    w_a, w_b, w_o, w_gu, w_d = rest_weights()
    F = w_d.shape[0]
    dil = [r_ for _, r_ in B_PATTERNS]
    views = [R.view(GB_W, r_) for r_ in dil]
    tokbuf = pltpu.VMEM((GB_W // LANES, R.tm, LANES), F32)

    f32 = lambda ref: ref[...].astype(F32)
    Rm = _Rows(N, T, _pick(T, 512))

    def mix_out(o1r, o2r, o3r, l1r, l2r, l3r, oa_r, ga_r, gb_r, x_r, gm_r, g_r, b_r, sf_r, hf_r, wa_r, wb_r, wo_r,
                ob_ref, ya_ref, yb_ref, mg_ref, y_ref, r1_ref, u2_ref, *bufs):
        os_ = [_from_view(ref, bufs[n], dil[n]) for n, ref in enumerate((o1r, o2r, o3r))]
        la, lb, lc = [_from_view(ref, bufs[3 + n], dil[n]) for n, ref in enumerate((l1r, l2r, l3r))]
        mx = jnp.maximum(jnp.maximum(la, lb), lc)
        ea, eb, ec = jnp.exp(la - mx), jnp.exp(lb - mx), jnp.exp(lc - mx)
        ob = ((ea * os_[0] + eb * os_[1] + ec * os_[2]) / (ea + eb + ec)).astype(BF16)
        ob_ref[...] = ob
        ya = jnp.dot(oa_r[...], wa_r[...], preferred_element_type=F32).astype(BF16)
        yb = jnp.concatenate([jnp.dot(ob, wb_r[s_], preferred_element_type=F32)
                              for s_ in range(w_b.shape[0])], axis=1).astype(BF16)
        merged = (jax.nn.sigmoid(f32(ga_r)) * ya.astype(F32) + jax.nn.sigmoid(f32(gb_r)) * yb.astype(F32)).astype(BF16)
        y = jnp.dot(merged, wo_r[...], preferred_element_type=F32)
        r1 = ALPHA * x_r[...] + (1.0 + gm_r[0]) * y
        xhat, _ = _ln_stats(r1)
        x1 = xhat * g_r[...] + b_r[...]
        ya_ref[...], yb_ref[...], mg_ref[...], y_ref[...], r1_ref[...] = ya, yb, merged, y, r1
        u2_ref[...] = (x1 * (1.0 + sf_r[0]) + hf_r[0]).astype(BF16)

    mviews = [Rm.view(GB_W, r_) for r_ in dil]
    ob, ya, yb, merged, y, r1, u2 = _pcall(
        mix_out, name="mix_out", grid=Rm.grid,
        in_specs=mviews + mviews + [Rm.row(QA_W), Rm.row(D, 0), Rm.row(D, 1), Rm.row(D), Rm.ex(D), Rm.const((1, D)),
                                    Rm.const((1, D)), Rm.ex(D), Rm.ex(D), Rm.const(w_a.shape), Rm.const(w_b.shape),
                                    Rm.const(w_o.shape)],
        out_specs=[Rm.row(GB_W)] + [Rm.row(D)] * 6,
        out_shape=[sds((N, GB_W), BF16)] + [sds((N, D), BF16)] * 3 + [sds((N, D), F32)] * 2 + [sds((N, D), BF16)],
        scratch_shapes=[pltpu.VMEM((GB_W // LANES, Rm.tm, LANES), F32)] * 6,
        compiler_params=_params(("parallel",)))(o1, o2, o3, l1, l2, l3, oa, gates, gates, x2, gate_m, ln1_g, ln1_b,
                                                scale_f, shift_f, w_a, w_b, w_o)

    w_gu = w_gu() if callable(w_gu) else w_gu
    tnf = w_gu.shape[2]
    nft = w_gu.shape[0] // 2
    tmf = _pick(N, 512)

    def ffn_up(u_r, wg_r, wu_r, hg_ref, hu_ref, a_ref):
        hg = jnp.dot(u_r[...], wg_r[...], preferred_element_type=F32)
        hu = jnp.dot(u_r[...], wu_r[...], preferred_element_type=F32)
        sl, _ = _silu_parts(hg)
        hg_ref[...] = hg.astype(BF16)
        hu_ref[...] = hu.astype(BF16)
        a_ref[...] = (sl * hu).astype(BF16)

    ftile = pl.BlockSpec((tmf, tnf), lambda j, i: (i, j))
    hg, hu, act = _pcall(
        ffn_up, name="ffn_up", grid=(nft, N // tmf),
        in_specs=[pl.BlockSpec((tmf, D), lambda j, i: (i, 0)), pl.BlockSpec((None, D, tnf), lambda j, i: (j, 0, 0)),
                  pl.BlockSpec((None, D, tnf), lambda j, i: (j + nft, 0, 0))],
        out_specs=[ftile] * 3, out_shape=[sds((N, F), BF16)] * 3,
        compiler_params=_params(("arbitrary", "parallel")))(u2, w_gu, w_gu)
    fchunk = _pick(F, 768)

    def ffn_down_norm2(act_r, wd_r, r1_r, g1_r, b1_r, t_r, gf_r, g_r, b_r, hg_r, hu_r,
                       dy2_ref, dx1_ref, dgf_ref, dg_ref, db_ref, loss_ref, dh_ref):
        first = R.first_of_example()
        y2v = jnp.dot(act_r[...], wd_r[...], preferred_element_type=F32)
        x1 = _ln_stats(r1_r[...])[0] * g1_r[...] + b1_r[...]
        r2 = ALPHA * x1 + (1.0 + gf_r[0]) * y2v
        xhat, rstd = _ln_stats(r2)
        err = xhat * g_r[...] + b_r[...] - t_r[...]
        dx2 = err * (1.0 / D)
        dr2 = _ln_bwd(dx2, xhat, rstd, g_r[...])
        dy2 = ((1.0 + gf_r[0]) * dr2).astype(BF16)
        dy2_ref[...] = dy2
        dx1_ref[...] = ALPHA * dr2
        _acc(dgf_ref, first, _colsum(dr2 * y2v))
        _acc(dg_ref, first, _colsum(dx2 * xhat))
        _acc(db_ref, first, _colsum(dx2))
        part = 0.5 * jnp.sum(jnp.mean(err * err, axis=-1, keepdims=True))
        _acc(loss_ref, first, jnp.broadcast_to(part, (1, 128)))
        for t in range(F // fchunk):
            cs = slice(t * fchunk, (t + 1) * fchunk)
            da = lax.dot_general(dy2, wd_r[cs, :], _NT, preferred_element_type=F32)
            sl, dsl = _silu_parts(hg_r[:, cs].astype(F32))
            dh_ref[:, cs] = (da * hu_r[:, cs].astype(F32) * dsl).astype(BF16)
            dh_ref[:, F + t * fchunk:F + (t + 1) * fchunk] = (da * sl).astype(BF16)

    dy2, dx1p, dgate_f, dg2, db2, loss_p, dh = _pcall(
        ffn_down_norm2, name="ffn_down_norm2", grid=R.grid,
        in_specs=[R.row(F), R.const((F, D)), R.row(D), R.const((1, D)), R.const((1, D)), R.row(D), R.ex(D),
                  R.const((1, D)), R.const((1, D)), R.row(F), R.row(F)],
        out_specs=[R.row(D), R.row(D), R.ex(D), R.ex(D), R.ex(D), R.ex(128), R.row(2 * F)],
        out_shape=[sds((N, D), BF16), sds((N, D), F32), exsum(), exsum(), exsum(), exsum(128), sds((N, 2 * F), BF16)],
        compiler_params=_params(("arbitrary",)))(act, w_d, r1, ln1_g, ln1_b, tgt2, gate_f, ln2_g, ln2_b, hg, hu)

    g_wd = _mm(act, dy2, ta=True, out_dtype=BF16, name="ffn_down_dw")
    g_wgu = _mm(u2, dh, ta=True, out3=w_gu.shape[0], out_dtype=BF16, name="ffn_up_dw")
    hook("ffn_grads", g_wgu=g_wgu)

    def ffn_up_dx_norm1(dh_r, w_r, dx1p_r, r1_r, y_r, sf_r, gm_r, g_r, b_r,
                        dxp_ref, dy_ref, dsf_ref, dhf_ref, dgm_ref, dg_ref, db_ref):
        first = R.first_of_example()
        du2v = None
        for s_ in range(w_gu.shape[0]):
            part = lax.dot_general(dh_r[:, s_ * tnf:(s_ + 1) * tnf], w_r[s_], _NT, preferred_element_type=F32)
            du2v = part if du2v is None else du2v + part
        dx1 = dx1p_r[...] + du2v * (1.0 + sf_r[0])
        xhat, rstd = _ln_stats(r1_r[...])
        dr1 = _ln_bwd(dx1, xhat, rstd, g_r[...])
        dxp_ref[...] = ALPHA * dr1
        dy_ref[...] = ((1.0 + gm_r[0]) * dr1).astype(BF16)
        _acc(dsf_ref, first, _colsum(du2v * (xhat * g_r[...] + b_r[...])))
        _acc(dhf_ref, first, _colsum(du2v))
        _acc(dgm_ref, first, _colsum(dr1 * y_r[...]))
        _acc(dg_ref, first, _colsum(dx1 * xhat))
        _acc(db_ref, first, _colsum(dx1))

    dxp, dy, dscale_f, dshift_f, dgate_m, dg1, db1 = _pcall(
        ffn_up_dx_norm1, name="ffn_up_dx_norm1", grid=R.grid,
        in_specs=[R.row(2 * F), R.const(w_gu.shape)] + [R.row(D)] * 3 + [R.ex(D), R.ex(D), R.const((1, D)),
                                                                        R.const((1, D))],
        out_specs=[R.row(D), R.row(D)] + [R.ex(D)] * 5,
        out_shape=[sds((N, D), F32), sds((N, D), BF16)] + [exsum()] * 5,
        compiler_params=_params(("arbitrary",)))(dh, w_gu, dx1p, r1, y, scale_f, gate_m, ln1_g, ln1_b)

    g_wo = _mm(merged, dy, ta=True, out_dtype=BF16, name="out_proj_dw")

    seg = (jnp.arange(GB_W)[:, None] // HEAD_DIM == jnp.arange(GB_W)[None, :] // HEAD_DIM).astype(BF16)

    def mix_out_bwd(dy_r, ya_r, yb_r, ga_r, gb_r, wo_r, wa_r, wb_r, o1r, o2r, o3r, l1r, l2r, l3r, seg_r,
                    dya_ref, dyb_ref, dg_ref, doa_ref, d1, d2, d3, e1, e2, e3, *bufs):
        dm = lax.dot_general(dy_r[...], wo_r[...], _NT, preferred_element_type=F32).astype(BF16).astype(F32)
        sa, sb = jax.nn.sigmoid(f32(ga_r)), jax.nn.sigmoid(f32(gb_r))
        dya, dyb = (dm * sa).astype(BF16), (dm * sb).astype(BF16)
        dya_ref[...], dyb_ref[...] = dya, dyb
        dg_ref[:, :D] = (dm * f32(ya_r) * sa * (1.0 - sa)).astype(BF16)
        dg_ref[:, D:] = (dm * f32(yb_r) * sb * (1.0 - sb)).astype(BF16)
        doa_ref[...] = lax.dot_general(dya, wa_r[...], _NT, preferred_element_type=F32).astype(BF16)
        ds_ = D // w_b.shape[0]
        dob = None
        for s_ in range(w_b.shape[0]):
            part = lax.dot_general(dyb[:, s_ * ds_:(s_ + 1) * ds_], wb_r[s_], _NT, preferred_element_type=F32)
            dob = part if dob is None else dob + part
        dob_v = dob
        os_ = [_from_view(ref, bufs[n], dil[n]) for n, ref in enumerate((o1r, o2r, o3r))]
        la, lb, lc = [_from_view(ref, bufs[3 + n], dil[n]) for n, ref in enumerate((l1r, l2r, l3r))]
        mx = jnp.maximum(jnp.maximum(la, lb), lc)
        ea, eb, ec = jnp.exp(la - mx), jnp.exp(lb - mx), jnp.exp(lc - mx)
        inv = 1.0 / (ea + eb + ec)
        ws = [ea * inv, eb * inv, ec * inv]

        def headsum(v):
            hi = v.astype(BF16)
            lo = (v - hi.astype(F32)).astype(BF16)
            sm = seg_r[...]
            return jnp.dot(hi, sm, preferred_element_type=F32) + jnp.dot(lo, sm, preferred_element_type=F32)

        dws = [headsum(dob_v * o) for o in os_]
        mean = ws[0] * dws[0] + ws[1] * dws[1] + ws[2] * dws[2]
        for n, (w_, dw_, d_ref, e_ref) in enumerate(zip(ws, dws, (d1, d2, d3), (e1, e2, e3))):
            _to_view(w_ * dob_v, d_ref, bufs[6], dil[n])
            _to_view(w_ * (dw_ - mean), e_ref, bufs[7], dil[n])

    vshape = lambda r_, dt: sds((NB, T // r_, r_ * GB_W), dt)
    dya, dyb, dgates, doa, *mb = _pcall(
        mix_out_bwd, name="mix_out_bwd", grid=R.grid,
        in_specs=[R.row(D)] * 3 + [R.row(D, 0), R.row(D, 1), R.const(w_o.shape), R.const(w_a.shape), R.const(w_b.shape)]
        + views + views + [R.const((GB_W, GB_W))],
        out_specs=[R.row(D), R.row(D), R.row(2 * D), R.row(QA_W)] + views + views,
        out_shape=[sds((N, D), BF16), sds((N, D), BF16), sds((N, 2 * D), BF16), sds((N, QA_W), BF16)]
        + [vshape(r_, BF16) for r_ in dil] + [vshape(r_, F32) for r_ in dil],
        scratch_shapes=[tokbuf] * 8,
        compiler_params=_params(("parallel",)))(dy, ya, yb, gates, gates, w_o, w_a, w_b, o1, o2, o3, l1, l2, l3, seg)
    do_b, dlse_b = mb[:3], mb[3:]

    g_wa = _mm(oa, dya, ta=True, out_dtype=BF16, name="branch_a_dw")
    g_wb = _mm(ob, dyb, ta=True, out3=w_b.shape[0], out_dtype=BF16, name="branch_b_dw")
    hook("rest_grads", g_wa=g_wa, g_wb=g_wb, g_wo=g_wo, g_wd=g_wd)

    dqkv_a, dsink = _attn_bwd(qkv[0], doa, la, None, cosf, sins, sinks, None, NB=NB, T=T, name="attn_a_bwd")
    hook("attn_a_bwd_done")
    dqkv = [dqkv_a]
    for g in range(ngrp):
        dqkv.append(_attn_bwd(qkv[1 + g], do_b[g], (l1, l2, l3)[g], dlse_b[g], cosf, sins, sinks, g, NB=NB, T=T,
                              name=f"attn_b{g}_bwd")[0])
        hook(f"attn_b{g}_bwd_done")

    blk = 256
    gate0 = A_W + 3 * QB_W
    win_rows = gate0 + dgates.shape[1]
    dw = functools.partial(_mm, ta=True, out_dtype=BF16, tm=blk)
    g_win = dw(dqkv[0].reshape(N, VAR_W), u, m_rows=A_W, out_rows=(win_rows, lambda i: i), name="inproj_dw0")
    for v in range(1, N_VAR):
        g = v - 1
        if VAR_DIL[v] == 1:
            block_of = lambda i, g=g: (A_W + g * GB_W) // blk + (i // (GB_W // blk)) * (QB_W // blk) + i % (GB_W // blk)
            g_win = dw(dqkv[v].reshape(N, VAR_W), u, out_rows=(win_rows, block_of), out_init=g_win, name=f"inproj_dw{v}")
        else:
            segs = [(s * GB_W, A_W + s * QB_W + g * GB_W, GB_W) for s in range(3)]
            g_win = _dw_view(dqkv[v], u, VAR_DIL[v], into=g_win, segs=segs, name=f"inproj_dw{v}")
    g_win = dw(dgates, u, out_rows=(win_rows, lambda i: gate0 // blk + i), out_init=g_win, name=f"inproj_dw{N_VAR}")
    hook("win_grads", g_win=g_win)
    wvar = lambda v: (w_in, (VAR_W, D), (v, 0))
    dview = lambda v: (dqkv[v], VAR_DIL[v])
    du = _mm_multi([dview(0)], [wvar(0)], M=N, T=T, name="inproj_dx0")
    hook("inproj_dx0_done")
    def x_bwd(duv, ins, outs):
        (dxp_r, x_r, sm_r), (gx_ref, dsm_ref, dhm_ref) = ins, outs
        first = R.first_of_example()
        gx_ref[...] = dxp_r[...] + duv * (1.0 + sm_r[0])
        _acc(dsm_ref, first, _colsum(duv * x_r[...]))
        _acc(dhm_ref, first, _colsum(duv))

    gx, dscale_m, dshift_m = _mm_multi(
        [dview(v) for v in range(1, N_VAR)] + [dgates],
        [wvar(v) for v in range(1, N_VAR)] + [(w_in, (2 * D, D), (QKV_P // (2 * D), 0))],
        M=N, T=T, add=du, tm=R.tm, name="inproj_dx1",
        post=([dxp, x2, scale_m], [R.row(D), R.row(D), R.ex(D)], [R.row(D), R.ex(D), R.ex(D)],
              [sds((N, D), F32), exsum(), exsum()], x_bwd))
    hook("inproj_dx1_done")

    dmod =jnp.concatenate([dshift_m, dscale_m, dgate_m, dshift_f, dscale_f, dgate_f], axis=-1)[:, 0]
    ln_grads = jnp.concatenate([dg1, db1, dg2, db2], axis=1)
    return dict(loss=loss_p[:, 0, 0], grad_x=gx.reshape(NB, T, D), g_win=g_win, g_wa=g_wa, g_wb=g_wb, g_wo=g_wo,
                g_wgu=g_wgu, g_wd=g_wd, dmod=dmod, ln_grads=ln_grads, dsink=dsink[0, :A_Q_HEADS])


def _coords():
    return lax.axis_index("x"), lax.axis_index("y"), lax.axis_index("c")


def _allgather_small(blk, *, name):
    m_per, n = blk.shape

    def body(x_ref, out_ref, send_sems, recv_sems, local_sem):
        x, y, c = _coords()
        me, sibling = (x, y, c), (x, y, 1 - c)
        chips = [(1 - x, y), (x, 1 - y), (1 - x, 1 - y)]

        def rows(px, py, pc):
            return out_ref.at[pl.ds((4 * px + 2 * py + pc) * m_per, m_per), :]

        def copy(k, block, to, src=None):
            return pltpu.make_async_remote_copy(
                src_ref=rows(*block) if src is None else src, dst_ref=rows(*block),
                send_sem=send_sems.at[k], recv_sem=recv_sems.at[k], device_id=to, device_id_type=MESH)

        mine = pltpu.make_async_copy(x_ref, rows(*me), local_sem)
        mine.start()
        first = [copy(0, me, sibling, src=x_ref)]
        first += [copy(1 + j, me, (*chip, c), src=x_ref) for j, chip in enumerate(chips)]
        for cp in first:
            cp.start()
        passed = [copy(4 + j, (*chip, c), sibling) for j, chip in enumerate(chips)]
        for j, chip in enumerate(chips):
            copy(1 + j, (*chip, c), me).wait_recv()
            passed[j].start()
        copy(0, sibling, me).wait_recv()
        for j, chip in enumerate(chips):
            copy(4 + j, (*chip, 1 - c), me).wait_recv()
        for cp in first + passed:
            cp.wait_send()
        mine.wait()

    return _pcall(
        body, name=name, out_shape=jax.ShapeDtypeStruct((8 * m_per, n), blk.dtype),
        in_specs=[pl.BlockSpec(memory_space=pltpu.VMEM)], out_specs=pl.BlockSpec(memory_space=pltpu.VMEM),
        scratch_shapes=[pltpu.SemaphoreType.DMA((7,)), pltpu.SemaphoreType.DMA((7,)), pltpu.SemaphoreType.DMA],
        compiler_params=pltpu.CompilerParams(vmem_limit_bytes=VMEM_LIMIT_BYTES),
    )(blk)


def _exchange(srcs, dsts, plan, *, name, dst_inits=None):
    na = len(dsts)
    nrem = len(plan(0, 0, 0))

    def body(*refs):
        refs = list(refs)
        src_refs = [refs.pop(0) for _ in range(na)] if srcs is not None else None
        if dst_inits is not None:
            del refs[:na]
        dst_refs, (send_sems, recv_sems) = refs[:na], refs[na:]
        start, wait = _copies(dst_refs if src_refs is None else src_refs, dst_refs, send_sems, recv_sems, plan)
        start()
        wait()

    hbm = pl.BlockSpec(memory_space=pl.ANY)
    ins = (list(srcs) if srcs is not None else []) + (list(dst_inits) if dst_inits is not None else [])
    base = na if srcs is not None else 0
    aliases = {base + a: a for a in range(na)} if dst_inits is not None else {}
    return _pcall(
        body, name=name, out_shape=list(dsts), in_specs=[hbm] * len(ins), out_specs=[hbm] * na,
        input_output_aliases=aliases,
        scratch_shapes=[pltpu.SemaphoreType.DMA((na * nrem,)), pltpu.SemaphoreType.DMA((na * nrem,))],
    )(*ins)


def _other_chips(x, y):
    return [(1 - x, y), (x, 1 - y), (1 - x, 1 - y)]


def _round(ride, carrier, name):
    if carrier is not None:
        _RIDES.setdefault(carrier, []).append(ride)
        return
    srcs = ride.srcs() if callable(ride.srcs) else ride.srcs
    inits = ride.dst_inits() if callable(ride.dst_inits) else ride.dst_inits
    ride.out = list(_exchange(srcs, ride.dsts, ride.plan, name=name, dst_inits=inits))


class _Gather:
    def __init__(self, shards, chip, tag, carriers=(None, None)):
        def plan_ici(x, y, c):
            k = 2 * x + y
            return [((c,), (k, c), (2 * px + py, c), (px, py, c)) for px, py in _other_chips(x, y)]

        def plan_d2d(x, y, c):
            return [((2 * px + py, c), (2 * px + py, c), (2 * px + py, 1 - c), (x, y, 1 - c))
                    for px, py in _other_chips(x, y)]

        def plan_near(x, y, c):
            k = 2 * x + y
            return [((c,), (k, c), (2 * px + py, c), (px, py, c)) for px, py in ((1 - x, y), (x, 1 - y))]

        def plan_far(x, y, c):
            kx, ky, kd = 2 * (1 - x) + y, 2 * x + (1 - y), 2 * (1 - x) + (1 - y)
            hp = shards[0].shape[1] // 2
            top, bottom = pl.ds(0, hp), pl.ds(hp, hp)
            return [((kx, c, top), (kx, c, top), (kd, c, top), (x, 1 - y, c)),
                    ((ky, c, bottom), (ky, c, bottom), (kd, c, bottom), (1 - x, y, c))]

        self.shards, self.chip = shards, chip
        dsts = [jax.ShapeDtypeStruct((4,) + s.shape, s.dtype) for s in shards]
        if len(carriers) == 3:
            near = _Ride(shards, dsts, plan_near)
            ici = _Ride(None, dsts, plan_far, dst_inits=lambda: near.out)
            _round(near, carriers[0], f"gather_{tag}_near")
            _round(ici, carriers[1], f"gather_{tag}_far")
        else:
            ici = _Ride(shards, dsts, plan_ici)
            _round(ici, carriers[0], f"gather_{tag}_ici")
        self.d2d = _Ride(None, dsts, plan_d2d, dst_inits=lambda: ici.out)
        _round(self.d2d, carriers[-1], f"gather_{tag}_d2d")

    def result(self):
        full = [lax.dynamic_update_index_in_dim(f, s, self.chip, 0) for f, s in zip(self.d2d.out, self.shards)]
        return [f.reshape((4, 2 * f.shape[2], f.shape[3])) for f in full]


def _index_operand(i):
    return jnp.reshape(i, (1,)).astype(jnp.int32)


def _add_pairs(g, f, ci, *, name):
    s, _, hr, wd = g.shape
    tr = _pick(hr, 600, 16)

    def body(c_ref, a_ref, b_ref, o_ref):
        o_ref[...] = (a_ref[...].astype(F32) + b_ref[...].astype(F32)).astype(BF16)

    spec = pl.BlockSpec((1, tr, wd), lambda j, i, c: (j, i, 0))
    grid_spec = pltpu.PrefetchScalarGridSpec(
        num_scalar_prefetch=1, grid=(s, hr // tr),
        in_specs=[pl.BlockSpec((1, None, tr, wd), lambda j, i, c: (j, c[0], i, 0)), spec], out_specs=spec)
    return _pcall(body, name=name, grid_spec=grid_spec, out_shape=jax.ShapeDtypeStruct(f.shape, BF16),
                  compiler_params=_params(("parallel", "parallel")))(_index_operand(ci), g, f)


def _sum_chips(landed, pairs, chip, ci, *, name):
    s, hr, wd = landed.shape
    tr = _pick(hr, 600, 16)

    def body(k_ref, l_ref, p_ref, o_ref):
        acc = None
        for k in range(s):
            part = jnp.where(k_ref[0] == k, p_ref[k], l_ref[k]).astype(F32)
            acc = part if acc is None else acc + part
        o_ref[...] = acc

    spec = pl.BlockSpec((s, tr, wd), lambda i, k: (0, i, 0))
    grid_spec = pltpu.PrefetchScalarGridSpec(
        num_scalar_prefetch=1, grid=(hr // tr,), in_specs=[spec, spec],
        out_specs=pl.BlockSpec((None, tr, wd), lambda i, k: (k[1], i, 0)))
    where = jnp.stack([chip, ci]).astype(jnp.int32)
    return _pcall(body, name=name, grid_spec=grid_spec, out_shape=jax.ShapeDtypeStruct((2, hr, wd), F32),
                  compiler_params=_params(("parallel",)))(where, landed, pairs)


class _ReduceScatter:
    def __init__(self, gs, chip, ci, tag):
        self.gs, self.chip, self.ci, self.tag = gs, chip, ci, tag
        self.half_t = [jax.ShapeDtypeStruct((g.shape[0],) + g.shape[2:], BF16) for g in gs]

    def pair(self, carrier=None):
        plan = lambda x, y, c: [((slice(None), 1 - c), (), (), (x, y, 1 - c))]
        self.r1 = _Ride(self.gs, self.half_t, plan)
        _round(self.r1, carrier, f"reduce_{self.tag}_pair")

    def chips(self, carrier=None):
        def plan(x, y, c):
            k = 2 * x + y
            return [((2 * px + py,), (k,), (2 * px + py,), (px, py, c)) for px, py in _other_chips(x, y)]

        self.pairs = [_add_pairs(g, f, self.ci, name=f"reduce_{self.tag}_pair_add{n}")
                      for n, (g, f) in enumerate(zip(self.gs, self.r1.out))]
        self.r2 = _Ride(self.pairs, self.half_t, plan)
        _round(self.r2, carrier, f"reduce_{self.tag}_chips")

    def halves(self, carrier=None):
        plan = lambda x, y, c: [((c,), (c,), (1 - c,), (x, y, 1 - c))]
        mine = [_sum_chips(l, p, self.chip, self.ci, name=f"reduce_{self.tag}_chip_sum{n}")
                for n, (l, p) in enumerate(zip(self.r2.out, self.pairs))]
        self.r3 = _Ride(None, [jax.ShapeDtypeStruct(m.shape, F32) for m in mine], plan, dst_inits=mine)
        _round(self.r3, carrier, f"reduce_{self.tag}_halves")

    def result(self):
        return [b.reshape(2 * b.shape[1], b.shape[2]) for b in self.r3.out]


def _ada_fwd(c_all, w_sh, b_sh, *, name):
    nb, d = c_all.shape
    wcols = w_sh.shape[1]
    tn = _pick(wcols, 512)

    def body(c_ref, w_ref, b_ref, o_ref, a_ref):
        cv = c_ref[...]
        act = cv * jax.nn.sigmoid(cv)
        a_ref[...] = act
        o_ref[...] = jnp.dot(act.astype(BF16), w_ref[...].astype(BF16), preferred_element_type=F32) + b_ref[...]

    return _pcall(
        body, name=name, grid=(wcols // tn,),
        in_specs=[pl.BlockSpec((nb, d), lambda j: (0, 0)), pl.BlockSpec((d, tn), lambda j: (0, j)),
                  pl.BlockSpec((1, tn), lambda j: (0, j))],
        out_specs=[pl.BlockSpec((nb, tn), lambda j: (0, j)), pl.BlockSpec((nb, d), lambda j: (0, 0))],
        out_shape=[jax.ShapeDtypeStruct((nb, wcols), F32), jax.ShapeDtypeStruct((nb, d), F32)],
        compiler_params=_params(("arbitrary",)))(c_all, w_sh, b_sh)


def _sum_devices(g, *, name):
    nd, m, w = g.shape

    def body(g_ref, o_ref):
        acc = g_ref[0]
        for k in range(1, nd):
            acc = acc + g_ref[k]
        o_ref[...] = acc

    return _pcall(body, name=name, out_shape=jax.ShapeDtypeStruct((m, w), F32),
                  compiler_params=pltpu.CompilerParams(vmem_limit_bytes=VMEM_LIMIT_BYTES))(g)


def _adamw(w, g, m, v, *, name):
    rows, cols = w.shape[-2:]
    tr = _pick(rows, max(8, (1 << 18) // cols), 8)
    c1 = 1.0 / (1.0 - ADAM_B1 ** ADAM_STEP)
    c2 = 1.0 / (1.0 - ADAM_B2 ** ADAM_STEP)

    def body(w_ref, g_ref, m_ref, v_ref, d_ref, nm_ref, nv_ref):
        gv = g_ref[...]
        nm = ADAM_B1 * m_ref[...] + (1.0 - ADAM_B1) * gv
        nv = ADAM_B2 * v_ref[...] + (1.0 - ADAM_B2) * (gv * gv)
        d_ref[...] = -ADAM_LR * ((nm * c1) / (jnp.sqrt(nv * c2) + ADAM_EPS) + ADAM_WD * w_ref[...])
        nm_ref[...] = nm
        nv_ref[...] = nv

    gspec = pl.BlockSpec((tr, cols), lambda i: (i, 0))
    spec = pl.BlockSpec((None, tr, cols), lambda i: (0, i, 0)) if w.ndim == 3 else gspec
    shp = jax.ShapeDtypeStruct(w.shape, F32)
    return _pcall(body, name=name, grid=(rows // tr,), in_specs=[spec, gspec, spec, spec], out_specs=[spec] * 3,
                  out_shape=[shp] * 3, compiler_params=_params(("parallel",)))(w, g, m, v)


def _permute_in_rows(wt):
    ngrp = len(B_PATTERNS)
    qb, kb, vb = (wt[A_W + n * QB_W:A_W + (n + 1) * QB_W] for n in range(3))
    parts = [wt[:A_W], jnp.zeros((VAR_W - A_W, wt.shape[1]), wt.dtype)]
    for g in range(ngrp):
        parts += [t[g * GB_W:(g + 1) * GB_W] for t in (qb, kb, vb)]
    return jnp.concatenate(parts + [wt[A_W + 3 * QB_W:]], axis=0)


def kernel(x, c, positions, w_ada, b_ada, w_in, sinks, w_branch_a, w_branch_b, w_o, ln1_g, ln1_b, w_gate_up, w_down, ln2_g, ln2_b, loss_target, m_w_ada, m_b_ada, m_w_in, m_sinks, m_w_branch_a, m_w_branch_b, m_w_o, m_ln1_g, m_ln1_b, m_w_gate_up, m_w_down, m_ln2_g, m_ln2_b, v_w_ada, v_b_ada, v_w_in, v_sinks, v_w_branch_a, v_w_branch_b, v_w_o, v_ln1_g, v_ln1_b, v_w_gate_up, v_w_down, v_ln2_g, v_ln2_b):
    xi, yi, ci = _coords()
    chip = 2 * xi + yi
    dev = 4 * xi + 2 * yi + ci
    NB, T, D = x.shape
    nchip, ndev = 4, 8
    ada_cols = w_ada.shape[2]

    ra, ro, rd = w_branch_a.shape[1], w_o.shape[1], w_down.shape[1]
    halves = lambda a: a.reshape(a.shape[:-2] + (2, a.shape[-2] // 2, a.shape[-1]))
    tr = lambda a: jnp.swapaxes(a, -1, -2)
    shards = [halves(w.astype(BF16))
              for w in (tr(w_in[0]), w_branch_a[0], w_o[0], w_down[0], w_branch_b[0], w_gate_up[0])]
    gin = _Gather(shards[:1], chip, "w_in", carriers=("gather_c", "ada_fwd", "gather_mod"))

    c_blk = jnp.zeros((8, D), F32).at[:NB].set(c)
    c_all = _allgather_small(c_blk, name="gather_c").reshape(ndev, 8, D)[:, :NB].reshape(ndev * NB, D)
    b_sh = lax.dynamic_slice(b_ada, (0, chip * ada_cols), (1, ada_cols))
    mod_part, c_act = _ada_fwd(c_all, w_ada[0], b_sh, name="ada_fwd")
    mod_g = _allgather_small(mod_part, name="gather_mod").reshape(nchip, 2, ndev * NB, ada_cols)[:, 0]
    mod_all = jnp.transpose(mod_g, (1, 0, 2)).reshape(ndev * NB, nchip * ada_cols)
    mod = lax.dynamic_slice(mod_all, (NB * dev, 0), (NB, nchip * ada_cols))

    (g_in,) = gin.result()
    w_in_f = _permute_in_rows(g_in.reshape(nchip * g_in.shape[1], D))
    mix = _Gather(shards[1:5], chip, "w_mix", carriers=("inproj_qkv", "attn_a_fwd"))
    ffn = _Gather(shards[5:], chip, "w_ffn", carriers=("attn_a_fwd", "attn_b_fwd", "mix_out"))

    def rest_weights():
        w_a_f, w_o_f, w_d_f, w_b_f = mix.result()
        return (w_a_f.reshape(nchip * ra, D), w_b_f, w_o_f.reshape(nchip * ro, D),
                lambda: ffn.result()[0], w_d_f.reshape(nchip * rd, D))

    red = {}

    def hook(event, **g):
        if event == "ffn_grads":
            red["ffn"] = _ReduceScatter([halves(g["g_wgu"])], chip, ci, "ffn")
            red["ffn"].pair(carrier="out_proj_dw")
        elif event == "rest_grads":
            by_chip = lambda a: halves(a.reshape(nchip, a.shape[0] // nchip, D))
            red["mix"] = _ReduceScatter([by_chip(g["g_wa"]), by_chip(g["g_wo"]), by_chip(g["g_wd"]), halves(g["g_wb"])],
                                        chip, ci, "mix")
            red["ffn"].chips(carrier="attn_a_bwd")
            red["mix"].pair(carrier="attn_a_bwd")
        elif event == "attn_a_bwd_done":
            red["ffn"].halves(carrier="attn_b0_bwd")
            red["mix"].chips(carrier="attn_b0_bwd")
        elif event == "attn_b0_bwd_done":
            red["mix"].halves(carrier="attn_b1_bwd")
        elif event == "win_grads":
            gr_in = g["g_win"]
            red["w_in"] = _ReduceScatter([halves(gr_in.reshape(nchip, gr_in.shape[0] // nchip, D))], chip, ci, "w_in")
            red["w_in"].pair(carrier="inproj_dx0")
        elif event == "inproj_dx0_done":
            red["w_in"].chips(carrier="inproj_dx1")
        elif event == "inproj_dx1_done":
            red["w_in"].halves(carrier="gather_small")

    res = _local_step(x, mod, positions, w_in_f, rest_weights, sinks[0], ln1_g, ln1_b, ln2_g, ln2_b, loss_target, hook)
    (g_w_a, g_w_o, g_w_d, g_w_b), (g_w_gu,) = red["mix"].result(), red["ffn"].result()

    small_rows = 24
    misc = jnp.zeros((1, D), F32).at[0, :A_Q_HEADS].set(res["dsink"]).at[0, A_Q_HEADS].set(jnp.sum(res["loss"]))
    small = jnp.concatenate([res["dmod"].reshape(NB * 6, D), jnp.sum(res["ln_grads"], axis=0), misc,
                             jnp.zeros((small_rows - NB * 6 - 5, D), F32)], axis=0)
    small_all = _allgather_small(small, name="gather_small").reshape(ndev, small_rows, D)
    (g_w_in,) = red["w_in"].result()
    dmod_all = small_all[:, :NB * 6].reshape(ndev * NB, 6 * D)
    sums = _sum_devices(small_all, name="sum_small")
    g_b_ada = (sums[0:6] + sums[6:12]).reshape(1, 6 * D)
    g_ln1_g, g_ln1_b, g_ln2_g, g_ln2_b = (sums[12 + n][None] for n in range(4))
    g_sinks = sums[16, :A_Q_HEADS][None]
    loss = sums[16, A_Q_HEADS]
    dmod_sh = lax.dynamic_slice(dmod_all, (0, chip * ada_cols), (ndev * NB, ada_cols))
    g_w_ada = _mm(c_act, dmod_sh, ta=True, name="ada_dw")

    names = ["w_ada", "b_ada", "w_in", "sinks", "w_branch_a", "w_branch_b", "w_o", "ln1_g", "ln1_b",
             "w_gate_up", "w_down", "ln2_g", "ln2_b"]
    ws = [w_ada, b_ada, w_in, sinks, w_branch_a, w_branch_b, w_o, ln1_g, ln1_b, w_gate_up, w_down, ln2_g, ln2_b]
    ms = [m_w_ada, m_b_ada, m_w_in, m_sinks, m_w_branch_a, m_w_branch_b, m_w_o, m_ln1_g, m_ln1_b, m_w_gate_up,
          m_w_down, m_ln2_g, m_ln2_b]
    vs = [v_w_ada, v_b_ada, v_w_in, v_sinks, v_w_branch_a, v_w_branch_b, v_w_o, v_ln1_g, v_ln1_b, v_w_gate_up,
          v_w_down, v_ln2_g, v_ln2_b]
    gs = [g_w_ada, g_b_ada, g_w_in, g_sinks, g_w_a, g_w_b, g_w_o, g_ln1_g, g_ln1_b, g_w_gu, g_w_d, g_ln2_g, g_ln2_b]
    grads, deltas, new_ms, new_vs = [], [], [], []
    for name, w, g, m, v in zip(names, ws, gs, ms, vs):
        flip = tr if name == "w_in" else (lambda a: a)
        w, m, v = flip(w), flip(m), flip(v)
        g2 = g.reshape(w.shape[-2:])
        d, nm, nv = _adamw(w, g2, m, v, name="adamw_" + name)
        grads.append(flip(g2.reshape(w.shape)))
        deltas.append(flip(d))
        new_ms.append(flip(nm))
        new_vs.append(flip(nv))
    return (loss, res["grad_x"], *grads, *deltas, *new_ms, *new_vs)
```
